```python
import jax, jax.numpy as jnp
from jax import lax
import numpy as np

D_MODEL = 1024
BATCH = 8
SEQ = 2048
DEPTH = 1

CHUNK = 64
EPS = 1e-6
POOL_WIDTH = 512
POOL_WINDOWS = (2, 4, 8, 16)
POOL_GROUPS = len(POOL_WINDOWS)
POOL_GROUP_DIM = POOL_WIDTH // POOL_GROUPS
POOL_MAX_WIN = max(POOL_WINDOWS)
GLA_HEADS = 4
GLA_DK = D_MODEL // 2
GLA_DV = D_MODEL
GLA_HK = GLA_DK // GLA_HEADS
GLA_HV = GLA_DV // GLA_HEADS
GLA_GATE_RANK = 16
GLA_GATE_NORM = 16.0
N_BRANCH = 2
D_FF = 2816
CONV_W = 3

IN_WIDTHS = (POOL_WIDTH, GLA_DK, GLA_DK, GLA_DV, GLA_DV, GLA_GATE_RANK, N_BRANCH * D_MODEL)
IN_TOTAL = sum(IN_WIDTHS)
IN_OFFSETS = tuple(int(o) for o in np.cumsum(IN_WIDTHS)[:-1])

kernel_name = "hybrid_pool_gla_convffn"


def rmsnorm(x, g):
    xf = x.astype(jnp.float32)
    y = xf * lax.rsqrt(jnp.mean(xf * xf, axis=-1, keepdims=True) + EPS)
    return (y * g.astype(jnp.float32)).astype(x.dtype)


def pool_mixer(u, w_grp, scale):
    B, S, _ = u.shape
    uf = u.astype(jnp.float32)
    cs = jnp.pad(jnp.cumsum(uf, axis=1), ((0, 0), (POOL_MAX_WIN, 0), (0, 0)))
    pos = jnp.arange(S, dtype=jnp.float32)
    outs = []
    for g, w in enumerate(POOL_WINDOWS):
        sl = slice(g * POOL_GROUP_DIM, (g + 1) * POOL_GROUP_DIM)
        win_sum = cs[:, POOL_MAX_WIN:, sl] - cs[:, POOL_MAX_WIN - w:POOL_MAX_WIN - w + S, sl]
        count = jnp.minimum(pos + 1.0, float(w))[None, :, None]
        outs.append(win_sum / count - uf[..., sl])
    p = jnp.stack(outs, axis=2).astype(u.dtype)
    p = jnp.einsum('bsgc,gcd->bsgd', p, w_grp).reshape(B, S, POOL_WIDTH)
    return p * scale


def gla_chunk_causal(q, k, v, log_a):
    B, S, H, K = q.shape
    V = v.shape[-1]
    N = S // CHUNK
    f32 = jnp.float32
    qf = (q.astype(f32) * (K ** -0.5)).reshape(B, N, CHUNK, H, K)
    kf = k.astype(f32).reshape(B, N, CHUNK, H, K)
    vf = v.astype(f32).reshape(B, N, CHUNK, H, V)
    bc = jnp.cumsum(log_a.astype(f32).reshape(B, N, CHUNK, H, K), axis=2)
    e_pos, e_neg = jnp.exp(bc), jnp.exp(-bc)
    q_fw, k_fw = qf * e_pos, kf * e_neg
    q_bw, k_bw = qf * e_neg, kf * e_pos
    s_fw = jnp.einsum('bnthk,bnshk->bnhts', q_fw, k_fw)
    s_bw = jnp.einsum('bnthk,bnshk->bnhts', q_bw, k_bw)
    lower = jnp.tril(jnp.ones((CHUNK, CHUNK), dtype=bool))
    scores = jnp.where(lower, s_fw, s_bw)
    intra = jnp.einsum('bnhts,bnshv->bnthv', scores, vf)
    b_last = bc[:, :, -1]
    k_dec = kf * jnp.exp(b_last[:, :, None] - bc)
    kv = jnp.einsum('bnshk,bnshv->bnhkv', k_dec, vf)

    def step(state, inp):
        decay, kv_c = inp
        return decay[..., None] * state + kv_c, state

    xs = (jnp.moveaxis(jnp.exp(b_last), 1, 0), jnp.moveaxis(kv, 1, 0))
    _, s_prev = lax.scan(step, jnp.zeros((B, H, K, V), f32), xs)
    inter = jnp.einsum('bnthk,nbhkv->bnthv', q_fw, s_prev)
    return (intra + inter).reshape(B, S, H, V)


def conv_ffn(h, w_up, w_conv, b_conv, w_down):
    S = h.shape[1]
    u = h @ w_up
    up = jnp.pad(u, ((0, 0), (CONV_W - 1, 0), (0, 0)))
    c = b_conv + sum(up[:, j:j + S] * w_conv[j] for j in range(CONV_W))
    gate, val = jnp.split(c, 2, axis=-1)
    return (jax.nn.silu(gate) * val) @ w_down


def _fwd_setup_inputs(seed: int = 0) -> dict:
    key = jax.random.key(seed)
    ks = jax.random.split(key, 20)
    f32 = jnp.float32
    L = DEPTH

    def nrm(k, shape, scale):
        return jax.random.normal(k, shape, f32) * scale

    return {
        "x": nrm(ks[0], (BATCH, SEQ, D_MODEL), 1.0),
        "g_mix": 1.0 + nrm(ks[1], (L, D_MODEL), 0.02),
        "w_in": nrm(ks[2], (L, D_MODEL, IN_TOTAL), D_MODEL ** -0.5),
        "b_gate": nrm(ks[3], (L, N_BRANCH * D_MODEL), 0.01),
        "w_gk_up": nrm(ks[4], (L, GLA_GATE_RANK, GLA_DK), GLA_GATE_RANK ** -0.5),
        "b_gk": nrm(ks[5], (L, GLA_DK), 0.1),
        "w_pool_grp": nrm(ks[6], (L, POOL_GROUPS, POOL_GROUP_DIM, POOL_GROUP_DIM), POOL_GROUP_DIM ** -0.5),
        "pool_scale": 1.0 + nrm(ks[7], (L, POOL_WIDTH), 0.02),
        "g_gla_head": 1.0 + nrm(ks[8], (L, GLA_HV), 0.02),
        "w_pool_proj": nrm(ks[9], (L, POOL_WIDTH, D_MODEL), POOL_WIDTH ** -0.5),
        "w_gla_proj": nrm(ks[10], (L, GLA_DV, D_MODEL), GLA_DV ** -0.5),
        "w_out": nrm(ks[11], (L, D_MODEL, D_MODEL), D_MODEL ** -0.5),
        "g_ffn": 1.0 + nrm(ks[12], (L, D_MODEL), 0.02),
        "w_up": nrm(ks[13], (L, D_MODEL, 2 * D_FF), D_MODEL ** -0.5),
        "w_conv": nrm(ks[14], (L, CONV_W, 2 * D_FF), CONV_W ** -0.5),
        "b_conv": nrm(ks[15], (L, 2 * D_FF), 0.01),
        "w_down": nrm(ks[16], (L, D_FF, D_MODEL), D_FF ** -0.5),
        "g_final": 1.0 + nrm(ks[17], (D_MODEL,), 0.02),
    }


def _fwd_reference(x, g_mix, w_in, b_gate, w_gk_up, b_gk, w_pool_grp, pool_scale, g_gla_head,
              w_pool_proj, w_gla_proj, w_out, g_ffn, w_up, w_conv, b_conv, w_down, g_final):
    B, S, _ = x.shape
    for l in range(DEPTH):
        h = rmsnorm(x, g_mix[l])
        z = h @ w_in[l]
        z_pool, z_q, z_k, z_v, z_og, z_gk, z_gate = jnp.split(z, IN_OFFSETS, axis=-1)
        y_pool = pool_mixer(z_pool, w_pool_grp[l], pool_scale[l]) @ w_pool_proj[l]
        log_a = jax.nn.log_sigmoid((z_gk @ w_gk_up[l] + b_gk[l]).astype(jnp.float32)) / GLA_GATE_NORM
        o = gla_chunk_causal(
            z_q.reshape(B, S, GLA_HEADS, GLA_HK),
            z_k.reshape(B, S, GLA_HEADS, GLA_HK),
            z_v.reshape(B, S, GLA_HEADS, GLA_HV),
            log_a.reshape(B, S, GLA_HEADS, GLA_HK))
        o = rmsnorm(o, g_gla_head[l]).reshape(B, S, GLA_DV).astype(x.dtype) * jax.nn.silu(z_og)
        y_gla = o @ w_gla_proj[l]
        gates = jax.nn.sigmoid(z_gate + b_gate[l]).reshape(B, S, N_BRANCH, D_MODEL)
        mixed = gates[:, :, 0] * y_pool + gates[:, :, 1] * y_gla
        x = x + mixed @ w_out[l]
        h2 = rmsnorm(x, g_ffn[l])
        x = x + conv_ffn(h2, w_up[l], w_conv[l], b_conv[l], w_down[l])
    return rmsnorm(x, g_final)


import jax as _jax
import jax.numpy as _jnp

TWIN_FORMAT = 'train_step'
FWD_PARAMS = ['x', 'g_mix', 'w_in', 'b_gate', 'w_gk_up', 'b_gk', 'w_pool_grp', 'pool_scale', 'g_gla_head', 'w_pool_proj', 'w_gla_proj', 'w_out', 'g_ffn', 'w_up', 'w_conv', 'b_conv', 'w_down', 'g_final']
TWIN_WEIGHTS = ['g_mix', 'w_in', 'b_gate', 'w_gk_up', 'b_gk', 'w_pool_grp', 'pool_scale', 'g_gla_head', 'w_pool_proj', 'w_gla_proj', 'w_out', 'g_ffn', 'w_up', 'w_conv', 'b_conv', 'w_down', 'g_final']
TWIN_DIFF_INPUT = 'x'
TWIN_INPUTS = ['x', 'g_mix', 'w_in', 'b_gate', 'w_gk_up', 'b_gk', 'w_pool_grp', 'pool_scale', 'g_gla_head', 'w_pool_proj', 'w_gla_proj', 'w_out', 'g_ffn', 'w_up', 'w_conv', 'b_conv', 'w_down', 'g_final', 'loss_target', 'm_g_mix', 'm_w_in', 'm_b_gate', 'm_w_gk_up', 'm_b_gk', 'm_w_pool_grp', 'm_pool_scale', 'm_g_gla_head', 'm_w_pool_proj', 'm_w_gla_proj', 'm_w_out', 'm_g_ffn', 'm_w_up', 'm_w_conv', 'm_b_conv', 'm_w_down', 'm_g_final', 'v_g_mix', 'v_w_in', 'v_b_gate', 'v_w_gk_up', 'v_b_gk', 'v_w_pool_grp', 'v_pool_scale', 'v_g_gla_head', 'v_w_pool_proj', 'v_w_gla_proj', 'v_w_out', 'v_g_ffn', 'v_w_up', 'v_w_conv', 'v_b_conv', 'v_w_down', 'v_g_final']
TWIN_OUTPUTS = ['loss', 'grad_x', 'grad_g_mix', 'grad_w_in', 'grad_b_gate', 'grad_w_gk_up', 'grad_b_gk', 'grad_w_pool_grp', 'grad_pool_scale', 'grad_g_gla_head', 'grad_w_pool_proj', 'grad_w_gla_proj', 'grad_w_out', 'grad_g_ffn', 'grad_w_up', 'grad_w_conv', 'grad_b_conv', 'grad_w_down', 'grad_g_final', 'delta_g_mix', 'delta_w_in', 'delta_b_gate', 'delta_w_gk_up', 'delta_b_gk', 'delta_w_pool_grp', 'delta_pool_scale', 'delta_g_gla_head', 'delta_w_pool_proj', 'delta_w_gla_proj', 'delta_w_out', 'delta_g_ffn', 'delta_w_up', 'delta_w_conv', 'delta_b_conv', 'delta_w_down', 'delta_g_final', 'new_m_g_mix', 'new_m_w_in', 'new_m_b_gate', 'new_m_w_gk_up', 'new_m_b_gk', 'new_m_w_pool_grp', 'new_m_pool_scale', 'new_m_g_gla_head', 'new_m_w_pool_proj', 'new_m_w_gla_proj', 'new_m_w_out', 'new_m_g_ffn', 'new_m_w_up', 'new_m_w_conv', 'new_m_b_conv', 'new_m_w_down', 'new_m_g_final', 'new_v_g_mix', 'new_v_w_in', 'new_v_b_gate', 'new_v_w_gk_up', 'new_v_b_gk', 'new_v_w_pool_grp', 'new_v_pool_scale', 'new_v_g_gla_head', 'new_v_w_pool_proj', 'new_v_w_gla_proj', 'new_v_w_out', 'new_v_g_ffn', 'new_v_w_up', 'new_v_w_conv', 'new_v_b_conv', 'new_v_w_down', 'new_v_g_final']
TWIN_LEAF_KINDS = {'loss': 'loss', 'grad_x': 'grad_x', 'grad_g_mix': 'grad_w', 'grad_w_in': 'grad_w', 'grad_b_gate': 'grad_w', 'grad_w_gk_up': 'grad_w', 'grad_b_gk': 'grad_w', 'grad_w_pool_grp': 'grad_w', 'grad_pool_scale': 'grad_w', 'grad_g_gla_head': 'grad_w', 'grad_w_pool_proj': 'grad_w', 'grad_w_gla_proj': 'grad_w', 'grad_w_out': 'grad_w', 'grad_g_ffn': 'grad_w', 'grad_w_up': 'grad_w', 'grad_w_conv': 'grad_w', 'grad_b_conv': 'grad_w', 'grad_w_down': 'grad_w', 'grad_g_final': 'grad_w', 'delta_g_mix': 'delta_w', 'delta_w_in': 'delta_w', 'delta_b_gate': 'delta_w', 'delta_w_gk_up': 'delta_w', 'delta_b_gk': 'delta_w', 'delta_w_pool_grp': 'delta_w', 'delta_pool_scale': 'delta_w', 'delta_g_gla_head': 'delta_w', 'delta_w_pool_proj': 'delta_w', 'delta_w_gla_proj': 'delta_w', 'delta_w_out': 'delta_w', 'delta_g_ffn': 'delta_w', 'delta_w_up': 'delta_w', 'delta_w_conv': 'delta_w', 'delta_b_conv': 'delta_w', 'delta_w_down': 'delta_w', 'delta_g_final': 'delta_w', 'new_m_g_mix': 'new_m', 'new_m_w_in': 'new_m', 'new_m_b_gate': 'new_m', 'new_m_w_gk_up': 'new_m', 'new_m_b_gk': 'new_m', 'new_m_w_pool_grp': 'new_m', 'new_m_pool_scale': 'new_m', 'new_m_g_gla_head': 'new_m', 'new_m_w_pool_proj': 'new_m', 'new_m_w_gla_proj': 'new_m', 'new_m_w_out': 'new_m', 'new_m_g_ffn': 'new_m', 'new_m_w_up': 'new_m', 'new_m_w_conv': 'new_m', 'new_m_b_conv': 'new_m', 'new_m_w_down': 'new_m', 'new_m_g_final': 'new_m', 'new_v_g_mix': 'new_v', 'new_v_w_in': 'new_v', 'new_v_b_gate': 'new_v', 'new_v_w_gk_up': 'new_v', 'new_v_b_gk': 'new_v', 'new_v_w_pool_grp': 'new_v', 'new_v_pool_scale': 'new_v', 'new_v_g_gla_head': 'new_v', 'new_v_w_pool_proj': 'new_v', 'new_v_w_gla_proj': 'new_v', 'new_v_w_out': 'new_v', 'new_v_g_ffn': 'new_v', 'new_v_w_up': 'new_v', 'new_v_w_conv': 'new_v', 'new_v_b_conv': 'new_v', 'new_v_w_down': 'new_v', 'new_v_g_final': 'new_v'}


def _forward(args):
    return _fwd_reference(*[args[k] for k in FWD_PARAMS])


def _output_shape():
    out = _jax.eval_shape(lambda: _forward(_fwd_setup_inputs(0)))
    return out.shape, out.dtype

N_MICROBATCH = 1
ADAM_LR = 0.001
ADAM_B1 = 0.9
ADAM_B2 = 0.999
ADAM_EPS = 1e-08
ADAM_WD = 0.01
ADAM_STEP = 10
PER_EXAMPLE_BATCH_AXIS = {'x': 0, 'loss_target': 0}
SHARED_INPUTS = []
_WEIGHT_DTYPES = {'g_mix': _jnp.float32, 'w_in': _jnp.float32, 'b_gate': _jnp.float32, 'w_gk_up': _jnp.float32, 'b_gk': _jnp.float32, 'w_pool_grp': _jnp.float32, 'pool_scale': _jnp.float32, 'g_gla_head': _jnp.float32, 'w_pool_proj': _jnp.float32, 'w_gla_proj': _jnp.float32, 'w_out': _jnp.float32, 'g_ffn': _jnp.float32, 'w_up': _jnp.float32, 'w_conv': _jnp.float32, 'b_conv': _jnp.float32, 'w_down': _jnp.float32, 'g_final': _jnp.float32}
MOMENT_SCALE = {'g_mix': 1.011553e-01, 'w_in': 4.329200e-02, 'b_gate': 1.935881e-02, 'w_gk_up': 5.920371e-03, 'b_gk': 2.302789e-02, 'w_pool_grp': 8.122815e-02, 'pool_scale': 8.054112e-02, 'g_gla_head': 7.977454e-02, 'w_pool_proj': 5.685033e-02, 'w_gla_proj': 3.891040e-02, 'w_out': 6.921361e-02, 'g_ffn': 8.719730e-02, 'w_up': 3.567811e-02, 'w_conv': 3.677439e-02, 'b_conv': 3.652119e-02, 'w_down': 5.829762e-02, 'g_final': 1.601219e+01}


def _to_microbatches(a, axis):
    t = _jnp.moveaxis(a, axis, 0)
    t = t.reshape((N_MICROBATCH, t.shape[0] // N_MICROBATCH) + t.shape[1:])
    return _jnp.moveaxis(t, 1, axis + 1)


def setup_inputs(seed: int = 0) -> dict:
    inp = _fwd_setup_inputs(seed)
    key = _jax.random.fold_in(_jax.random.key(seed), 7919)
    shape, _ = _output_shape()
    out = dict(inp)
    out["loss_target"] = _jax.random.normal(_jax.random.fold_in(key, 0), shape, _jnp.float32)
    for i, name in enumerate(TWIN_WEIGHTS):
        w = inp[name].astype(_jnp.float32)
        if MOMENT_SCALE is None:
            s = _jnp.sqrt(_jnp.mean(_jnp.square(w)) + 1e-30)
        else:
            s = MOMENT_SCALE[name]
        km, kv = _jax.random.split(_jax.random.fold_in(key, i + 1))
        out[name] = w
        out["m_" + name] = s * _jax.random.normal(km, w.shape, _jnp.float32)
        out["v_" + name] = (s * s) * _jax.random.uniform(kv, w.shape, _jnp.float32, 0.5, 1.5)
    if N_MICROBATCH > 1:
        for name, axis in PER_EXAMPLE_BATCH_AXIS.items():
            out[name] = _to_microbatches(out[name], axis)
    return {'x': out['x'], 'g_mix': out['g_mix'], 'w_in': out['w_in'], 'b_gate': out['b_gate'], 'w_gk_up': out['w_gk_up'], 'b_gk': out['b_gk'], 'w_pool_grp': out['w_pool_grp'], 'pool_scale': out['pool_scale'], 'g_gla_head': out['g_gla_head'], 'w_pool_proj': out['w_pool_proj'], 'w_gla_proj': out['w_gla_proj'], 'w_out': out['w_out'], 'g_ffn': out['g_ffn'], 'w_up': out['w_up'], 'w_conv': out['w_conv'], 'b_conv': out['b_conv'], 'w_down': out['w_down'], 'g_final': out['g_final'], 'loss_target': out['loss_target'], 'm_g_mix': out['m_g_mix'], 'm_w_in': out['m_w_in'], 'm_b_gate': out['m_b_gate'], 'm_w_gk_up': out['m_w_gk_up'], 'm_b_gk': out['m_b_gk'], 'm_w_pool_grp': out['m_w_pool_grp'], 'm_pool_scale': out['m_pool_scale'], 'm_g_gla_head': out['m_g_gla_head'], 'm_w_pool_proj': out['m_w_pool_proj'], 'm_w_gla_proj': out['m_w_gla_proj'], 'm_w_out': out['m_w_out'], 'm_g_ffn': out['m_g_ffn'], 'm_w_up': out['m_w_up'], 'm_w_conv': out['m_w_conv'], 'm_b_conv': out['m_b_conv'], 'm_w_down': out['m_w_down'], 'm_g_final': out['m_g_final'], 'v_g_mix': out['v_g_mix'], 'v_w_in': out['v_w_in'], 'v_b_gate': out['v_b_gate'], 'v_w_gk_up': out['v_w_gk_up'], 'v_b_gk': out['v_b_gk'], 'v_w_pool_grp': out['v_w_pool_grp'], 'v_pool_scale': out['v_pool_scale'], 'v_g_gla_head': out['v_g_gla_head'], 'v_w_pool_proj': out['v_w_pool_proj'], 'v_w_gla_proj': out['v_w_gla_proj'], 'v_w_out': out['v_w_out'], 'v_g_ffn': out['v_g_ffn'], 'v_w_up': out['v_w_up'], 'v_w_conv': out['v_w_conv'], 'v_b_conv': out['v_b_conv'], 'v_w_down': out['v_w_down'], 'v_g_final': out['v_g_final']}


def _loss(weights, diff, rest, loss_target):
    with _jax.named_scope("forward"):
        args = {**rest, TWIN_DIFF_INPUT: diff, **{k: w.astype(_WEIGHT_DTYPES[k]) for k, w in weights.items()}}
        y = _forward(args)
    with _jax.named_scope("loss_head"):
        err = _jnp.square(y.astype(_jnp.float32) - loss_target)
        return 0.5 * _jnp.sum(_jnp.mean(err, axis=-1)) if err.ndim else 0.5 * err


def _adamw(w, g, m, v):
    m = ADAM_B1 * m + (1.0 - ADAM_B1) * g
    v = ADAM_B2 * v + (1.0 - ADAM_B2) * _jnp.square(g)
    m_hat = m / (1.0 - ADAM_B1 ** ADAM_STEP)
    v_hat = v / (1.0 - ADAM_B2 ** ADAM_STEP)
    delta = -ADAM_LR * (m_hat / (_jnp.sqrt(v_hat) + ADAM_EPS) + ADAM_WD * w)
    return delta, m, v


def reference(x, g_mix, w_in, b_gate, w_gk_up, b_gk, w_pool_grp, pool_scale, g_gla_head, w_pool_proj, w_gla_proj, w_out, g_ffn, w_up, w_conv, b_conv, w_down, g_final, loss_target, m_g_mix, m_w_in, m_b_gate, m_w_gk_up, m_b_gk, m_w_pool_grp, m_pool_scale, m_g_gla_head, m_w_pool_proj, m_w_gla_proj, m_w_out, m_g_ffn, m_w_up, m_w_conv, m_b_conv, m_w_down, m_g_final, v_g_mix, v_w_in, v_b_gate, v_w_gk_up, v_b_gk, v_w_pool_grp, v_pool_scale, v_g_gla_head, v_w_pool_proj, v_w_gla_proj, v_w_out, v_g_ffn, v_w_up, v_w_conv, v_b_conv, v_w_down, v_g_final):
    given = dict(x=x, g_mix=g_mix, w_in=w_in, b_gate=b_gate, w_gk_up=w_gk_up, b_gk=b_gk, w_pool_grp=w_pool_grp, pool_scale=pool_scale, g_gla_head=g_gla_head, w_pool_proj=w_pool_proj, w_gla_proj=w_gla_proj, w_out=w_out, g_ffn=g_ffn, w_up=w_up, w_conv=w_conv, b_conv=b_conv, w_down=w_down, g_final=g_final, loss_target=loss_target, m_g_mix=m_g_mix, m_w_in=m_w_in, m_b_gate=m_b_gate, m_w_gk_up=m_w_gk_up, m_b_gk=m_b_gk, m_w_pool_grp=m_w_pool_grp, m_pool_scale=m_pool_scale, m_g_gla_head=m_g_gla_head, m_w_pool_proj=m_w_pool_proj, m_w_gla_proj=m_w_gla_proj, m_w_out=m_w_out, m_g_ffn=m_g_ffn, m_w_up=m_w_up, m_w_conv=m_w_conv, m_b_conv=m_b_conv, m_w_down=m_w_down, m_g_final=m_g_final, v_g_mix=v_g_mix, v_w_in=v_w_in, v_b_gate=v_b_gate, v_w_gk_up=v_w_gk_up, v_b_gk=v_b_gk, v_w_pool_grp=v_w_pool_grp, v_pool_scale=v_pool_scale, v_g_gla_head=v_g_gla_head, v_w_pool_proj=v_w_pool_proj, v_w_gla_proj=v_w_gla_proj, v_w_out=v_w_out, v_g_ffn=v_g_ffn, v_w_up=v_w_up, v_w_conv=v_w_conv, v_b_conv=v_b_conv, v_w_down=v_w_down, v_g_final=v_g_final)
    weights = {n: given[n] for n in TWIN_WEIGHTS}
    shared = {n: given[n] for n in SHARED_INPUTS}
    per_example = {n: given[n] for n in ['x']}
    grad_fn = _jax.value_and_grad(_loss, argnums=(0, 1))

    def one_microbatch(ex, loss_target):
        ex = dict(ex)
        diff = ex.pop(TWIN_DIFF_INPUT)
        return grad_fn(weights, diff, {**shared, **ex}, loss_target)

    if N_MICROBATCH == 1:
        loss, (grad_w, grad_x) = one_microbatch(per_example, given["loss_target"])
    else:
        def body(carry, xs):
            loss_sum, grad_sum = carry
            l_k, (gw_k, gx_k) = one_microbatch(xs[0], xs[1])
            with _jax.named_scope("update"):
                return (loss_sum + l_k, _jax.tree.map(_jnp.add, grad_sum, gw_k)), gx_k

        init = (_jnp.zeros((), _jnp.float32), _jax.tree.map(_jnp.zeros_like, weights))
        (loss, grad_w), grad_x = _jax.lax.scan(body, init, (per_example, given["loss_target"]))
    with _jax.named_scope("update"):
        delta_w, new_m, new_v = {}, {}, {}
        for n in TWIN_WEIGHTS:
            delta_w[n], new_m[n], new_v[n] = _adamw(weights[n], grad_w[n], given["m_" + n], given["v_" + n])
    return (loss, grad_x, *[grad_w[n] for n in TWIN_WEIGHTS], *[delta_w[n] for n in TWIN_WEIGHTS],
            *[new_m[n] for n in TWIN_WEIGHTS], *[new_v[n] for n in TWIN_WEIGHTS])
```

```python
import functools

import jax
import jax.numpy as jnp
from jax import lax
from jax.experimental import pallas as pl
from jax.experimental.pallas import tpu as pltpu

F32 = jnp.float32
BF16 = jnp.bfloat16
MESH = pl.DeviceIdType.MESH

D = 1024
EPS = 1e-6
CHUNK = 64
POOL_W = 512
POOL_WINDOWS = (2, 4, 8, 16)
HEADS = 4
HK = 128
HV = 256
GATE_RANK = 16
D_FF = 2816
N_UP = 2 * D_FF
N_IN = 5648
QSCALE = HK ** -0.5
N_INR = 5760
OFF_GATE, OFF_V, OFF_OG, OFF_POOL, OFF_Q, OFF_K, OFF_GK = 0, 2048, 3072, 4096, 4608, 5120, 5632

ADAM_LR, ADAM_B1, ADAM_B2, ADAM_EPS, ADAM_WD, ADAM_STEP = 0.001, 0.9, 0.999, 1e-08, 0.01, 10

VMEM_LIMIT = 56 * 1024 * 1024


def _cp(*sem):
    return pltpu.CompilerParams(dimension_semantics=sem if sem else None, vmem_limit_bytes=VMEM_LIMIT)


def _dot(a, b):
    return jnp.dot(a, b, preferred_element_type=F32)


def _dot_nt(a, b):
    return lax.dot_general(a, b, (((1,), (1,)), ((), ())), preferred_element_type=F32)


def _dot_tn(a, b):
    return lax.dot_general(a, b, (((0,), (0,)), ((), ())), preferred_element_type=F32)


def _sigmoid(v):
    return 1.0 / (1.0 + jnp.exp(-v))


def _rows(shape):
    return lax.broadcasted_iota(jnp.int32, shape, 0)


def _pick_row(v, r):
    return jnp.sum(jnp.where(_rows(v.shape) == r, v, 0.0), axis=0, keepdims=True)


def _norm_matmul(x, g, w, name, ts, tn):
    s = x.shape[0]
    if w.ndim == 3:
        nj, tn = w.shape[0], w.shape[2]
        w_spec = pl.BlockSpec((None, D, tn), lambda i, j: (j, 0, 0))
    else:
        nj = w.shape[1] // tn
        w_spec = pl.BlockSpec((D, tn), lambda i, j: (0, j))

    def body(x_ref, g_ref, w_ref, z_ref, h_ref):
        @pl.when(pl.program_id(1) == 0)
        def _():
            xv = x_ref[...]
            r = lax.rsqrt(jnp.mean(xv * xv, axis=-1, keepdims=True) + EPS)
            h_ref[...] = (xv * r * g_ref[...]).astype(BF16)

        z_ref[...] = _dot(h_ref[...], w_ref[...]).astype(BF16)

    return pl.pallas_call(
        body, name=name, grid=(s // ts, nj),
        in_specs=[pl.BlockSpec((ts, D), lambda i, j: (i, 0)), pl.BlockSpec((1, D), lambda i, j: (0, 0)), w_spec],
        out_specs=[pl.BlockSpec((ts, tn), lambda i, j: (i, j)), pl.BlockSpec((ts, D), lambda i, j: (i, 0))],
        out_shape=[jax.ShapeDtypeStruct((s, nj * tn), BF16), jax.ShapeDtypeStruct((s, D), BF16)],
        compiler_params=_cp("arbitrary", "arbitrary"),
    )(x, g, w)


def _matmul_nt_normbwd(dz, w, x, g, resid, name, ts, tk):
    s = x.shape[0]
    if w.ndim == 3:
        nk, tk = w.shape[0], w.shape[2]
        w_spec = pl.BlockSpec((None, D, tk), lambda i, k: (k, 0, 0))
    else:
        nk = w.shape[1] // tk
        w_spec = pl.BlockSpec((D, tk), lambda i, k: (0, k))

    def body(dz_ref, w_ref, x_ref, g_ref, r_ref, o_ref, ob_ref, dg_ref, acc_ref):
        i, k = pl.program_id(0), pl.program_id(1)

        @pl.when(k == 0)
        def _():
            acc_ref[...] = jnp.zeros_like(acc_ref)

        @pl.when((i == 0) & (k == 0))
        def _():
            dg_ref[...] = jnp.zeros_like(dg_ref)

        acc_ref[...] += _dot_nt(dz_ref[...], w_ref[...])

        @pl.when(k == nk - 1)
        def _():
            dh = acc_ref[...]
            xv = x_ref[...]
            r = lax.rsqrt(jnp.mean(xv * xv, axis=-1, keepdims=True) + EPS)
            xh = xv * r
            dg_ref[...] += jnp.sum(dh * xh, axis=0, keepdims=True)
            dxh = dh * g_ref[...]
            out = r_ref[...] + r * (dxh - xh * jnp.mean(dxh * xh, axis=-1, keepdims=True))
            o_ref[...] = out
            ob_ref[...] = out.astype(BF16)

    row = lambda i, k: (i, 0)
    return pl.pallas_call(
        body, name=name, grid=(s // ts, nk),
        in_specs=[pl.BlockSpec((ts, tk), lambda i, k: (i, k)), w_spec, pl.BlockSpec((ts, D), row),
                  pl.BlockSpec((1, D), lambda i, k: (0, 0)), pl.BlockSpec((ts, D), row)],
        out_specs=[pl.BlockSpec((ts, D), row), pl.BlockSpec((ts, D), row), pl.BlockSpec((1, D), lambda i, k: (0, 0))],
        out_shape=[jax.ShapeDtypeStruct((s, D), F32), jax.ShapeDtypeStruct((s, D), BF16),
                   jax.ShapeDtypeStruct((1, D), F32)],
        scratch_shapes=[pltpu.VMEM((ts, D), F32)],
        compiler_params=_cp("arbitrary", "arbitrary"),
    )(dz, w, x, g, resid)


def _matmul_tn(a, b, name, tn, tk, shard_major=False):
    s, m = a.shape
    n = b.shape[1]
    nj, nk = n // tn, s // tk

    def body(a_ref, b_ref, o_ref):
        @pl.when(pl.program_id(1) == 0)
        def _():
            o_ref[...] = jnp.zeros_like(o_ref)

        o_ref[...] += _dot_tn(a_ref[...], b_ref[...])

    if shard_major:
        out_spec = pl.BlockSpec((None, m, tn), lambda j, k: (j, 0, 0))
        out_shape = jax.ShapeDtypeStruct((nj, m, tn), F32)
    else:
        out_spec = pl.BlockSpec((m, tn), lambda j, k: (0, j))
        out_shape = jax.ShapeDtypeStruct((m, n), F32)
    return pl.pallas_call(
        body, name=name, grid=(nj, nk),
        in_specs=[pl.BlockSpec((tk, m), lambda j, k: (k, 0)), pl.BlockSpec((tk, tn), lambda j, k: (k, j))],
        out_specs=out_spec, out_shape=out_shape,
        compiler_params=_cp("arbitrary", "arbitrary"),
    )(a, b)


def _pool_fwd(zr, wgrp, scale):
    s = zr.shape[0]

    def body(u_ref, w_ref, sc_ref, p_ref, pp_ref):
        row = _rows((s, 128))
        for gi, win in enumerate(POOL_WINDOWS):
            cs = slice(gi * 128, (gi + 1) * 128)
            u = u_ref[:, cs].astype(F32)
            acc, k = u, 1
            while k < win:
                acc = acc + jnp.where(row >= k, pltpu.roll(acc, k, 0), 0.0)
                k *= 2
            cnt = jnp.minimum(row + 1, win).astype(F32)
            p = (acc / cnt - u).astype(BF16)
            p_ref[:, cs] = p
            pp_ref[:, cs] = (_dot(p, w_ref[gi].astype(BF16)) * sc_ref[:, cs]).astype(BF16)

    return pl.pallas_call(
        body, name="pool_fwd", grid=(1,),
        in_specs=[pl.BlockSpec((s, POOL_W), lambda i: (0, OFF_POOL // POOL_W)),
                  pl.BlockSpec((4, 128, 128), lambda i: (0, 0, 0)), pl.BlockSpec((1, POOL_W), lambda i: (0, 0))],
        out_specs=[pl.BlockSpec((s, POOL_W), lambda i: (0, 0))] * 2,
        out_shape=[jax.ShapeDtypeStruct((s, POOL_W), BF16)] * 2,
        compiler_params=_cp("arbitrary"),
    )(zr, wgrp, scale)


def _pool_bwd(p, dpp, wgrp, scale):
    s = p.shape[0]

    def body(p_ref, dpp_ref, w_ref, sc_ref, dz_ref, dw_ref, dsc_ref):
        row = _rows((s, 128))
        for gi, win in enumerate(POOL_WINDOWS):
            cs = slice(gi * 128, (gi + 1) * 128)
            pv = p_ref[:, cs]
            wb = w_ref[gi].astype(BF16)
            dpp_v = dpp_ref[:, cs].astype(F32)
            dsc_ref[:, cs] = jnp.sum(dpp_v * _dot(pv, wb), axis=0, keepdims=True)
            dpm = (dpp_v * sc_ref[:, cs]).astype(BF16)
            dw_ref[gi] = _dot_tn(pv, dpm)
            dp = _dot_nt(dpm, wb)
            cnt = jnp.minimum(row + 1, win).astype(F32)
            acc, k = dp / cnt, 1
            while k < win:
                acc = acc + jnp.where(row < s - k, pltpu.roll(acc, s - k, 0), 0.0)
                k *= 2
            dz_ref[:, cs] = (acc - dp).astype(BF16)

    full = lambda i: (0, 0)
    return pl.pallas_call(
        body, name="pool_bwd", grid=(1,),
        in_specs=[pl.BlockSpec((s, POOL_W), full), pl.BlockSpec((s, POOL_W), full),
                  pl.BlockSpec((4, 128, 128), lambda i: (0, 0, 0)), pl.BlockSpec((1, POOL_W), full)],
        out_specs=[pl.BlockSpec((s, POOL_W), full), pl.BlockSpec((4, 128, 128), lambda i: (0, 0, 0)),
                   pl.BlockSpec((1, POOL_W), full)],
        out_shape=[jax.ShapeDtypeStruct((s, POOL_W), BF16), jax.ShapeDtypeStruct((4, 128, 128), F32),
                   jax.ShapeDtypeStruct((1, POOL_W), F32)],
        compiler_params=_cp("arbitrary"),
    )(p, dpp, wgrp, scale)


def _gla_decay(zgk_ref, wgk_ref, bgk_ref, rb):
    g = _dot(zgk_ref[...], wgk_ref[...].astype(BF16)) + bgk_ref[...]
    la = (jnp.minimum(g, 0.0) - jnp.log(1.0 + jnp.exp(-jnp.abs(g)))) * (1.0 / 16.0)
    rowm = _rows((rb, HK)) & (CHUNK - 1)
    bc, k = la, 1
    while k < CHUNK:
        bc = bc + jnp.where(rowm >= k, pltpu.roll(bc, k, 0), 0.0)
        k *= 2
    return g, jnp.exp(bc), jnp.exp(-bc)


def _gla_specs(rb, rmap):
    return [pl.BlockSpec((rb, HK), lambda h, r: (rmap(h, r), OFF_Q // HK + h)),
            pl.BlockSpec((rb, HK), lambda h, r: (rmap(h, r), OFF_K // HK + h)),
            pl.BlockSpec((rb, HV), lambda h, r: (rmap(h, r), OFF_V // HV + h)),
            pl.BlockSpec((rb, 128), lambda h, r: (rmap(h, r), OFF_GK // 128))]


def _gla_fwd(zr, wgk, bgk, ghead, rb):
    s = zr.shape[0]
    nc = rb // CHUNK

    def body(q_ref, k_ref, v_ref, zgk_ref, zog_ref, wgk_ref, bgk_ref, gh_ref, o_ref, og_ref, sp_ref, st_ref):
        @pl.when(pl.program_id(1) == 0)
        def _():
            st_ref[...] = jnp.zeros_like(st_ref)

        _, e_pos, e_neg = _gla_decay(zgk_ref, wgk_ref, bgk_ref, rb)
        lower = _rows((CHUNK, CHUNK)) >= lax.broadcasted_iota(jnp.int32, (CHUNK, CHUNK), 1)
        for c in range(nc):
            sl = slice(c * CHUNK, (c + 1) * CHUNK)
            q = q_ref[sl, :].astype(F32) * QSCALE
            k = k_ref[sl, :].astype(F32)
            v = v_ref[sl, :]
            ec, fc = e_pos[sl], e_neg[sl]
            qfw = (q * ec).astype(BF16)
            kfw_f = k * fc
            s_fw = _dot_nt(qfw, kfw_f.astype(BF16))
            s_bw = _dot_nt((q * fc).astype(BF16), (k * ec).astype(BF16))
            pm = jnp.where(lower, s_fw, s_bw).astype(BF16)
            st = st_ref[...]
            stb = st.astype(BF16)
            sp_ref[c] = stb
            o = _dot(pm, v) + _dot_nt(qfw, stb)
            e_last = _pick_row(ec, CHUNK - 1)
            kdec = (kfw_f * e_last).astype(BF16)
            st_ref[...] = st * e_last + _dot_tn(v, kdec)
            r = lax.rsqrt(jnp.mean(o * o, axis=-1, keepdims=True) + EPS)
            zo = zog_ref[sl, :].astype(F32)
            o_ref[sl, :] = o.astype(BF16)
            og_ref[sl, :] = (o * r * gh_ref[...] * zo * _sigmoid(zo)).astype(BF16)

    rmap = lambda h, r: r
    return pl.pallas_call(
        body, name="gla_fwd", grid=(HEADS, s // rb),
        in_specs=_gla_specs(rb, rmap) + [
            pl.BlockSpec((rb, HV), lambda h, r: (r, OFF_OG // HV + h)),
            pl.BlockSpec((128, HK), lambda h, r: (0, h)), pl.BlockSpec((1, HK), lambda h, r: (0, h)),
            pl.BlockSpec((1, HV), lambda h, r: (0, 0))],
        out_specs=[pl.BlockSpec((rb, HV), lambda h, r: (r, h)), pl.BlockSpec((rb, HV), lambda h, r: (r, h)),
                   pl.BlockSpec((nc, None, HV, HK), lambda h, r: (r, h, 0, 0))],
        out_shape=[jax.ShapeDtypeStruct((s, D), BF16), jax.ShapeDtypeStruct((s, D), BF16),
                   jax.ShapeDtypeStruct((s // CHUNK, HEADS, HV, HK), BF16)],
        scratch_shapes=[pltpu.VMEM((HV, HK), F32)],
        compiler_params=_cp("arbitrary", "arbitrary"),
    )(zr, zr, zr, zr, zr, wgk, bgk, ghead)


def _gla_bwd(zr, do, sp, wgk, bgk, rb):
    s = zr.shape[0]
    nc = rb // CHUNK
    nr = s // rb

    def body(q_ref, k_ref, v_ref, zgk_ref, do_ref, sp_ref, wgk_ref, bgk_ref, dq_ref, dk_ref, dv_ref, dg_ref,
             gt_ref, dbc_ref):
        @pl.when(pl.program_id(1) == 0)
        def _():
            gt_ref[...] = jnp.zeros_like(gt_ref)

        g, e_pos, e_neg = _gla_decay(zgk_ref, wgk_ref, bgk_ref, rb)
        lower = _rows((CHUNK, CHUNK)) >= lax.broadcasted_iota(jnp.int32, (CHUNK, CHUNK), 1)
        is_last = _rows((CHUNK, HK)) == CHUNK - 1
        for c in reversed(range(nc)):
            sl = slice(c * CHUNK, (c + 1) * CHUNK)
            q = q_ref[sl, :].astype(F32) * QSCALE
            k = k_ref[sl, :].astype(F32)
            v = v_ref[sl, :]
            dov = do_ref[sl, :]
            ec, fc = e_pos[sl], e_neg[sl]
            qfw_f, kfw_f, qbw_f, kbw_f = q * ec, k * fc, q * fc, k * ec
            qfw, kfw, qbw, kbw = qfw_f.astype(BF16), kfw_f.astype(BF16), qbw_f.astype(BF16), kbw_f.astype(BF16)
            pm = jnp.where(lower, _dot_nt(qfw, kfw), _dot_nt(qbw, kbw)).astype(BF16)
            e_last = _pick_row(ec, CHUNK - 1)
            kdec = (kfw_f * e_last).astype(BF16)
            gt = gt_ref[...]
            gtb = gt.astype(BF16)
            spv = sp_ref[c]
            dp = _dot_nt(dov, v)
            dv_ref[sl, :] = (_dot_tn(pm, dov) + _dot_nt(kdec, gtb)).astype(BF16)
            ds_fw = jnp.where(lower, dp, 0.0).astype(BF16)
            ds_bw = jnp.where(lower, 0.0, dp).astype(BF16)
            dqfw = _dot(ds_fw, kfw) + _dot(dov, spv)
            dkfw = _dot_tn(ds_fw, qfw)
            dqbw = _dot(ds_bw, kbw)
            dkbw = _dot_tn(ds_bw, qbw)
            dkdec = _dot(v, gtb)
            de_last = (jnp.sum(gt * spv.astype(F32), axis=0, keepdims=True)
                       + jnp.sum(dkdec * kfw_f, axis=0, keepdims=True))
            dkfw = dkfw + dkdec * e_last
            dq_ref[sl, :] = ((dqfw * ec + dqbw * fc) * QSCALE).astype(BF16)
            dk_ref[sl, :] = (dkfw * fc + dkbw * ec).astype(BF16)
            dbc = dqfw * qfw_f - dqbw * qbw_f + dkbw * kbw_f - dkfw * kfw_f
            dbc_ref[sl, :] = dbc + jnp.where(is_last, de_last * e_last, 0.0)
            gt_ref[...] = _dot_tn(dov, qfw) + gt * e_last
        rowm = _rows((rb, HK)) & (CHUNK - 1)
        dla, kk = dbc_ref[...], 1
        while kk < CHUNK:
            dla = dla + jnp.where(rowm < CHUNK - kk, pltpu.roll(dla, rb - kk, 0), 0.0)
            kk *= 2
        dg_ref[...] = dla * (1.0 / 16.0) * _sigmoid(-g)

    rmap = lambda h, r: nr - 1 - r
    return pl.pallas_call(
        body, name="gla_bwd", grid=(HEADS, nr),
        in_specs=_gla_specs(rb, rmap) + [
            pl.BlockSpec((rb, HV), lambda h, r: (nr - 1 - r, h)),
            pl.BlockSpec((nc, None, HV, HK), lambda h, r: (nr - 1 - r, h, 0, 0)),
            pl.BlockSpec((128, HK), lambda h, r: (0, h)), pl.BlockSpec((1, HK), lambda h, r: (0, h))],
        out_specs=[pl.BlockSpec((rb, HK), lambda h, r: (nr - 1 - r, h)), pl.BlockSpec((rb, HK), lambda h, r: (nr - 1 - r, h)),
                   pl.BlockSpec((rb, HV), lambda h, r: (nr - 1 - r, h)), pl.BlockSpec((rb, HK), lambda h, r: (nr - 1 - r, h))],
        out_shape=[jax.ShapeDtypeStruct((s, HEADS * HK), BF16), jax.ShapeDtypeStruct((s, HEADS * HK), BF16),
                   jax.ShapeDtypeStruct((s, D), BF16), jax.ShapeDtypeStruct((s, HEADS * HK), F32)],
        scratch_shapes=[pltpu.VMEM((HV, HK), F32), pltpu.VMEM((rb, HK), F32)],
        compiler_params=_cp("arbitrary", "arbitrary"),
    )(zr, zr, zr, zr, do, sp, wgk, bgk)


def _gk_bwd(dgpre, zr, wgk, ts):
    s = zr.shape[0]

    def body(dg_ref, zgk_ref, w_ref, dz_ref, dw_ref, db_ref):
        @pl.when(pl.program_id(0) == 0)
        def _():
            dw_ref[...] = jnp.zeros_like(dw_ref)
            db_ref[...] = jnp.zeros_like(db_ref)

        dg = dg_ref[...]
        dgb = dg.astype(BF16)
        dz_ref[...] = _dot_nt(dgb, w_ref[...].astype(BF16)).astype(BF16)
        dw_ref[...] += _dot_tn(zgk_ref[...], dgb)
        db_ref[...] += jnp.sum(dg, axis=0, keepdims=True)

    return pl.pallas_call(
        body, name="gk_bwd", grid=(s // ts,),
        in_specs=[pl.BlockSpec((ts, 512), lambda i: (i, 0)), pl.BlockSpec((ts, 128), lambda i: (i, OFF_GK // 128)),
                  pl.BlockSpec((128, 512), lambda i: (0, 0))],
        out_specs=[pl.BlockSpec((ts, 128), lambda i: (i, 0)), pl.BlockSpec((128, 512), lambda i: (0, 0)),
                   pl.BlockSpec((1, 512), lambda i: (0, 0))],
        out_shape=[jax.ShapeDtypeStruct((s, 128), BF16), jax.ShapeDtypeStruct((128, 512), F32),
                   jax.ShapeDtypeStruct((1, 512), F32)],
        compiler_params=_cp("arbitrary"),
    )(dgpre, zr, wgk)


def _merge_fwd(x, zr, pp, og, bgate, wpp, wgla, wout, ts):
    s = x.shape[0]

    def body(x_ref, z0_ref, z1_ref, pp_ref, og_ref, bg_ref, wpp_ref, wgla_ref, wout_ref,
             x1_ref, mix_ref, yp_ref, yg_ref):
        ppv = pp_ref[...]
        yp = jnp.concatenate([_dot(ppv, wpp_ref[j]) for j in range(4)], axis=1)
        yg = _dot(og_ref[...], wgla_ref[...])
        g0 = _sigmoid(z0_ref[...].astype(F32) + bg_ref[:, :D])
        g1 = _sigmoid(z1_ref[...].astype(F32) + bg_ref[:, D:])
        mixed = (g0 * yp + g1 * yg).astype(BF16)
        x1_ref[...] = x_ref[...] + _dot(mixed, wout_ref[...])
        mix_ref[...] = mixed
        yp_ref[...] = yp.astype(BF16)
        yg_ref[...] = yg.astype(BF16)

    row = lambda i: (i, 0)
    const2 = lambda i: (0, 0)
    return pl.pallas_call(
        body, name="merge_fwd", grid=(s // ts,),
        in_specs=[pl.BlockSpec((ts, D), row), pl.BlockSpec((ts, D), lambda i: (i, 0)), pl.BlockSpec((ts, D), lambda i: (i, 1)),
                  pl.BlockSpec((ts, POOL_W), row), pl.BlockSpec((ts, D), row), pl.BlockSpec((1, 2 * D), const2),
                  pl.BlockSpec((4, POOL_W, 256), lambda i: (0, 0, 0)), pl.BlockSpec((D, D), const2),
                  pl.BlockSpec((D, D), const2)],
        out_specs=[pl.BlockSpec((ts, D), row)] * 4,
        out_shape=[jax.ShapeDtypeStruct((s, D), F32)] + [jax.ShapeDtypeStruct((s, D), BF16)] * 3,
        compiler_params=_cp("arbitrary"),
    )(x, zr, zr, pp, og, bgate, wpp, wgla, wout)


def _merge_bwd(dx1b, zr, yp, yg, o, bgate, ghead, wpp, wgla, wout, ts):
    s = dx1b.shape[0]

    def body(dx_ref, z0_ref, z1_ref, zog_ref, yp_ref, yg_ref, o_ref, bg_ref, gh_ref, wpp_ref, wgla_ref, wout_ref,
             dzg_ref, dyp_ref, dyg_ref, dpp_ref, do_ref, dzog_ref, dbg_ref, dgh_ref):
        @pl.when(pl.program_id(0) == 0)
        def _():
            dbg_ref[...] = jnp.zeros_like(dbg_ref)
            dgh_ref[...] = jnp.zeros_like(dgh_ref)

        dmix = _dot_nt(dx_ref[...], wout_ref[...])
        g0 = _sigmoid(z0_ref[...].astype(F32) + bg_ref[:, :D])
        g1 = _sigmoid(z1_ref[...].astype(F32) + bg_ref[:, D:])
        dypb = (dmix * g0).astype(BF16)
        dygb = (dmix * g1).astype(BF16)
        dz0 = dmix * yp_ref[...].astype(F32) * g0 * (1.0 - g0)
        dz1 = dmix * yg_ref[...].astype(F32) * g1 * (1.0 - g1)
        dzg_ref[:, :D] = dz0.astype(BF16)
        dzg_ref[:, D:] = dz1.astype(BF16)
        dbg_ref[:, :D] += jnp.sum(dz0, axis=0, keepdims=True)
        dbg_ref[:, D:] += jnp.sum(dz1, axis=0, keepdims=True)
        dyp_ref[...] = dypb
        dyg_ref[...] = dygb
        dpp = _dot_nt(dypb[:, 0:256], wpp_ref[0])
        for j in range(1, 4):
            dpp = dpp + _dot_nt(dypb[:, j * 256:(j + 1) * 256], wpp_ref[j])
        dpp_ref[...] = dpp.astype(BF16)
        dog = _dot_nt(dygb, wgla_ref[...])
        gh = gh_ref[...]
        dgh = jnp.zeros((1, HV), F32)
        for h in range(HEADS):
            cs = slice(h * HV, (h + 1) * HV)
            ov = o_ref[:, cs].astype(F32)
            r = lax.rsqrt(jnp.mean(ov * ov, axis=-1, keepdims=True) + EPS)
            oh = ov * r
            zo = zog_ref[:, cs].astype(F32)
            sg = _sigmoid(zo)
            dog_h = dog[:, cs]
            don = dog_h * zo * sg
            dzog_ref[:, cs] = (dog_h * oh * gh * sg * (1.0 + zo * (1.0 - sg))).astype(BF16)
            dgh = dgh + jnp.sum(don * oh, axis=0, keepdims=True)
            doh = don * gh
            do_ref[:, cs] = (r * (doh - oh * jnp.mean(doh * oh, axis=-1, keepdims=True))).astype(BF16)
        dgh_ref[...] += dgh

    row = lambda i: (i, 0)
    const2 = lambda i: (0, 0)
    return pl.pallas_call(
        body, name="merge_bwd", grid=(s // ts,),
        in_specs=[pl.BlockSpec((ts, D), row), pl.BlockSpec((ts, D), lambda i: (i, 0)), pl.BlockSpec((ts, D), lambda i: (i, 1)),
                  pl.BlockSpec((ts, D), lambda i: (i, OFF_OG // D)), pl.BlockSpec((ts, D), row), pl.BlockSpec((ts, D), row),
                  pl.BlockSpec((ts, D), row), pl.BlockSpec((1, 2 * D), const2), pl.BlockSpec((1, HV), const2),
                  pl.BlockSpec((4, POOL_W, 256), lambda i: (0, 0, 0)), pl.BlockSpec((D, D), const2),
                  pl.BlockSpec((D, D), const2)],
        out_specs=[pl.BlockSpec((ts, 2 * D), row), pl.BlockSpec((ts, D), row), pl.BlockSpec((ts, D), row),
                   pl.BlockSpec((ts, POOL_W), row), pl.BlockSpec((ts, D), row), pl.BlockSpec((ts, D), row),
                   pl.BlockSpec((1, 2 * D), const2), pl.BlockSpec((1, HV), const2)],
        out_shape=[jax.ShapeDtypeStruct((s, 2 * D), BF16), jax.ShapeDtypeStruct((s, D), BF16),
                   jax.ShapeDtypeStruct((s, D), BF16), jax.ShapeDtypeStruct((s, POOL_W), BF16),
                   jax.ShapeDtypeStruct((s, D), BF16), jax.ShapeDtypeStruct((s, D), BF16),
                   jax.ShapeDtypeStruct((1, 2 * D), F32), jax.ShapeDtypeStruct((1, HV), F32)],
        compiler_params=_cp("arbitrary"),
    )(dx1b, zr, zr, zr, yp, yg, o, bgate, ghead, wpp, wgla, wout)


HALO = 16
CCH = 1408


def _conv_taps(u_ref, halo_ref, cs, first, ts):
    u = u_ref[:, cs].astype(F32)
    hal = halo_ref[:, cs].astype(F32)
    h1 = jnp.where(first, 0.0, _pick_row(hal, HALO - 1))
    h2 = jnp.where(first, 0.0, _pick_row(hal, HALO - 2))
    row = _rows(u.shape)
    r1 = jnp.where(row == 0, h1, pltpu.roll(u, 1, 0))
    r2 = jnp.where(row == 0, h2, jnp.where(row == 1, h1, pltpu.roll(u, 2, 0)))
    return u, r1, r2


def _ffn_down_loss(u, x1, tgt, wconv, bconv, wdown, gfin, ts):
    s = x1.shape[0]

    def body(u_ref, halo_ref, x1_ref, t_ref, wc_ref, bc_ref, wd_ref, gf_ref, a_ref, dx_ref, dxb_ref, ls_ref, dgf_ref):
        i = pl.program_id(0)

        @pl.when(i == 0)
        def _():
            ls_ref[...] = jnp.zeros_like(ls_ref)
            dgf_ref[...] = jnp.zeros_like(dgf_ref)

        first = i == 0
        acc = x1_ref[...]
        for hf in range(2):
            cg = slice(hf * CCH, (hf + 1) * CCH)
            cv = slice(D_FF + hf * CCH, D_FF + (hf + 1) * CCH)
            vals = []
            for cs in (cg, cv):
                u0, u1, u2 = _conv_taps(u_ref, halo_ref, cs, first, ts)
                vals.append(bc_ref[:, cs] + wc_ref[0:1, cs] * u2 + wc_ref[1:2, cs] * u1 + wc_ref[2:3, cs] * u0)
            a = (vals[0] * _sigmoid(vals[0]) * vals[1]).astype(BF16)
            a_ref[:, cg] = a
            acc = acc + _dot(a, wd_ref[cg, :])
        r = lax.rsqrt(jnp.mean(acc * acc, axis=-1, keepdims=True) + EPS)
        xh = acc * r
        gf = gf_ref[...]
        err = xh * gf - t_ref[...]
        ls_ref[...] += (0.5 / D) * jnp.sum(jnp.sum(err * err, axis=-1, keepdims=True), axis=0, keepdims=True)
        dy = err * (1.0 / D)
        dgf_ref[...] += jnp.sum(dy * xh, axis=0, keepdims=True)
        dxh = dy * gf
        dx = r * (dxh - xh * jnp.mean(dxh * xh, axis=-1, keepdims=True))
        dx_ref[...] = dx
        dxb_ref[...] = dx.astype(BF16)

    row = lambda i: (i, 0)
    const2 = lambda i: (0, 0)
    return pl.pallas_call(
        body, name="ffn_down_loss", grid=(s // ts,),
        in_specs=[pl.BlockSpec((ts, N_UP), row),
                  pl.BlockSpec((HALO, N_UP), lambda i: (jnp.maximum(i * (ts // HALO) - 1, 0), 0)),
                  pl.BlockSpec((ts, D), row), pl.BlockSpec((ts, D), row), pl.BlockSpec((3, N_UP), const2),
                  pl.BlockSpec((1, N_UP), const2), pl.BlockSpec((D_FF, D), const2), pl.BlockSpec((1, D), const2)],
        out_specs=[pl.BlockSpec((ts, D_FF), row), pl.BlockSpec((ts, D), row), pl.BlockSpec((ts, D), row),
                   pl.BlockSpec((1, 128), const2), pl.BlockSpec((1, D), const2)],
        out_shape=[jax.ShapeDtypeStruct((s, D_FF), BF16), jax.ShapeDtypeStruct((s, D), F32),
                   jax.ShapeDtypeStruct((s, D), BF16), jax.ShapeDtypeStruct((1, 128), F32),
                   jax.ShapeDtypeStruct((1, D), F32)],
        compiler_params=_cp("arbitrary"),
    )(u, u, x1, tgt, wconv, bconv, wdown, gfin)


def _ffn_bwd(dx2b, u, wconv, bconv, wdown, ts):
    s = dx2b.shape[0]
    nt = s // ts

    def body(dx_ref, u_ref, halo_ref, wc_ref, bc_ref, wd_ref, du_ref, db_ref, dw_ref, nxt_ref):
        i = pl.program_id(0)

        @pl.when(i == 0)
        def _():
            db_ref[...] = jnp.zeros_like(db_ref)
            dw_ref[...] = jnp.zeros_like(dw_ref)
            nxt_ref[...] = jnp.zeros_like(nxt_ref)

        first = i == nt - 1
        dxv = dx_ref[...]
        row = _rows((ts, CCH))
        for hf in range(2):
            cg = slice(hf * CCH, (hf + 1) * CCH)
            cv = slice(D_FF + hf * CCH, D_FF + (hf + 1) * CCH)
            da = _dot_nt(dxv, wd_ref[cg, :])
            taps, vals = [], []
            for cs in (cg, cv):
                t3 = _conv_taps(u_ref, halo_ref, cs, first, ts)
                taps.append(t3)
                vals.append(bc_ref[:, cs] + wc_ref[0:1, cs] * t3[2] + wc_ref[1:2, cs] * t3[1] + wc_ref[2:3, cs] * t3[0])
            sg = _sigmoid(vals[0])
            dcs = (da * vals[1] * sg * (1.0 + vals[0] * (1.0 - sg)), da * vals[0] * sg)
            for cs, (u0, u1, u2), dc in zip((cg, cv), taps, dcs):
                db_ref[:, cs] += jnp.sum(dc, axis=0, keepdims=True)
                dw_ref[0:1, cs] += jnp.sum(dc * u2, axis=0, keepdims=True)
                dw_ref[1:2, cs] += jnp.sum(dc * u1, axis=0, keepdims=True)
                dw_ref[2:3, cs] += jnp.sum(dc * u0, axis=0, keepdims=True)
                n1 = nxt_ref[0:1, cs]
                n2 = nxt_ref[1:2, cs]
                f1 = jnp.where(row == ts - 1, n1, pltpu.roll(dc, ts - 1, 0))
                f2 = jnp.where(row == ts - 1, n2, jnp.where(row == ts - 2, n1, pltpu.roll(dc, ts - 2, 0)))
                du_ref[:, cs] = (wc_ref[2:3, cs] * dc + wc_ref[1:2, cs] * f1 + wc_ref[0:1, cs] * f2).astype(BF16)
                nxt_ref[:, cs] = dc[0:8, :]

    rev = lambda i: (nt - 1 - i, 0)
    const2 = lambda i: (0, 0)
    return pl.pallas_call(
        body, name="ffn_bwd", grid=(nt,),
        in_specs=[pl.BlockSpec((ts, D), rev), pl.BlockSpec((ts, N_UP), rev),
                  pl.BlockSpec((HALO, N_UP), lambda i: (jnp.maximum((nt - 1 - i) * (ts // HALO) - 1, 0), 0)),
                  pl.BlockSpec((3, N_UP), const2), pl.BlockSpec((1, N_UP), const2), pl.BlockSpec((D_FF, D), const2)],
        out_specs=[pl.BlockSpec((ts, N_UP), rev), pl.BlockSpec((1, N_UP), const2), pl.BlockSpec((3, N_UP), const2)],
        out_shape=[jax.ShapeDtypeStruct((s, N_UP), BF16), jax.ShapeDtypeStruct((1, N_UP), F32),
                   jax.ShapeDtypeStruct((3, N_UP), F32)],
        scratch_shapes=[pltpu.VMEM((8, N_UP), F32)],
        compiler_params=_cp("arbitrary"),
    )(dx2b, u, u, wconv, bconv, wdown)


ANY = pl.BlockSpec(memory_space=pl.ANY)


def _place():
    x, y, c = lax.axis_index("x"), lax.axis_index("y"), lax.axis_index("c")
    chips = [(1 - x, y), (x, 1 - y), (1 - x, 1 - y)]
    return x, y, c, chips


def _half_rows(c, half):
    return pl.ds(pl.multiple_of(c * half, 8), half)


def _remote(src, dst, send_sems, recv_sems, k, to):
    return pltpu.make_async_remote_copy(src_ref=src, dst_ref=dst, send_sem=send_sems.at[k], recv_sem=recv_sems.at[k],
                                        device_id=to, device_id_type=MESH)


def _all_gather_weights(big, small):
    nb, ns = len(big), len(small)
    n = nb + ns
    n_sem = 6 * nb + 3 * ns

    def body(*refs):
        ins, outs = refs[:n], refs[n:2 * n]
        send_sems, recv_sems, loc_sems = refs[2 * n:]
        x, y, c, chips = _place()
        me = 2 * x + y
        sib = (x, y, 1 - c)
        local = [pltpu.make_async_copy(ins[a], outs[a].at[me], loc_sems.at[a]) for a in range(n)]
        for cp in local:
            cp.start()
        started = []
        for a in range(nb):
            rows = _half_rows(c, big[a].shape[0] // 2)
            for k, ch in enumerate(chips):
                cp = _remote(ins[a].at[rows], outs[a].at[me, rows], send_sems, recv_sems, 6 * a + k, (ch[0], ch[1], c))
                cp.start()
                started.append(cp)
        for a in range(ns):
            for k, ch in enumerate(chips):
                cp = _remote(ins[nb + a], outs[nb + a].at[me], send_sems, recv_sems, 6 * nb + 3 * a + k,
                             (ch[0], ch[1], c))
                cp.start()
                started.append(cp)
        for a in range(nb):
            rows = _half_rows(c, big[a].shape[0] // 2)
            for k, ch in enumerate(chips):
                landed = outs[a].at[2 * ch[0] + ch[1], rows]
                _remote(landed, landed, send_sems, recv_sems, 6 * a + k, sib).wait_recv()
                cp = _remote(landed, landed, send_sems, recv_sems, 6 * a + 3 + k, sib)
                cp.start()
                started.append(cp)
        for a in range(nb):
            rows = _half_rows(1 - c, big[a].shape[0] // 2)
            for k, ch in enumerate(chips):
                landed = outs[a].at[2 * ch[0] + ch[1], rows]
                _remote(landed, landed, send_sems, recv_sems, 6 * a + 3 + k, sib).wait_recv()
        for a in range(ns):
            for k, ch in enumerate(chips):
                landed = outs[nb + a].at[2 * ch[0] + ch[1]]
                _remote(landed, landed, send_sems, recv_sems, 6 * nb + 3 * a + k, sib).wait_recv()
        for cp in started:
            cp.wait_send()
        for cp in local:
            cp.wait()

    arrs = list(big) + list(small)
    return pl.pallas_call(
        body, name="all_gather_weights",
        in_specs=[ANY] * n, out_specs=[ANY] * n,
        out_shape=[jax.ShapeDtypeStruct((4,) + a.shape, a.dtype) for a in arrs],
        scratch_shapes=[pltpu.SemaphoreType.DMA((n_sem,)), pltpu.SemaphoreType.DMA((n_sem,)),
                        pltpu.SemaphoreType.DMA((n,))],
        compiler_params=pltpu.CompilerParams(has_side_effects=True),
    )(*arrs)


def _sibling_exchange(grads, small):
    nb = len(grads)
    n = nb + 1

    def body(*refs):
        ins, outs = refs[:n], refs[n:2 * n]
        send_sems, recv_sems = refs[2 * n:]
        x, y, c, _ = _place()
        sib = (x, y, 1 - c)
        cps = []
        for a in range(nb):
            rows = _half_rows(1 - c, grads[a].shape[1] // 2)
            cps.append(_remote(ins[a].at[:, rows, :], outs[a], send_sems, recv_sems, a, sib))
        cps.append(_remote(ins[nb], outs[nb], send_sems, recv_sems, nb, sib))
        for cp in cps:
            cp.start()
        for cp in cps:
            cp.wait()

    out_shape = [jax.ShapeDtypeStruct((4, g.shape[1] // 2, g.shape[2]), F32) for g in grads]
    out_shape.append(jax.ShapeDtypeStruct(small.shape, F32))
    return pl.pallas_call(
        body, name="sibling_exchange", in_specs=[ANY] * n, out_specs=[ANY] * n, out_shape=out_shape,
        scratch_shapes=[pltpu.SemaphoreType.DMA((n,)), pltpu.SemaphoreType.DMA((n,))],
        compiler_params=pltpu.CompilerParams(has_side_effects=True),
    )(*grads, small)


def _chip_exchange(partials, small):
    nb = len(partials)
    n = nb + 1

    def body(*refs):
        ins, outs = refs[:n], refs[n:2 * n]
        send_sems, recv_sems, loc_sem = refs[2 * n:]
        x, y, c, chips = _place()
        me = 2 * x + y
        mine = pltpu.make_async_copy(ins[nb], outs[nb].at[me], loc_sem)
        mine.start()
        cps = []
        for a in range(nb):
            for k, ch in enumerate(chips):
                cps.append(_remote(ins[a].at[2 * ch[0] + ch[1]], outs[a].at[k], send_sems, recv_sems, 3 * a + k,
                                   (ch[0], ch[1], c)))
        for k, ch in enumerate(chips):
            cps.append(_remote(ins[nb], outs[nb].at[me], send_sems, recv_sems, 3 * nb + k, (ch[0], ch[1], c)))
        for cp in cps:
            cp.start()
        for a in range(nb):
            for k in range(3):
                cps[3 * a + k].wait_recv()
        for k, ch in enumerate(chips):
            landed = outs[nb].at[2 * ch[0] + ch[1]]
            _remote(landed, landed, send_sems, recv_sems, 3 * nb + k, (ch[0], ch[1], c)).wait_recv()
        for cp in cps:
            cp.wait_send()
        mine.wait()

    out_shape = [jax.ShapeDtypeStruct((3,) + p.shape[1:], BF16) for p in partials]
    out_shape.append(jax.ShapeDtypeStruct((4,) + small.shape, F32))
    return pl.pallas_call(
        body, name="chip_exchange", in_specs=[ANY] * n, out_specs=[ANY] * n, out_shape=out_shape,
        scratch_shapes=[pltpu.SemaphoreType.DMA((3 * n,)), pltpu.SemaphoreType.DMA((3 * n,)), pltpu.SemaphoreType.DMA],
        compiler_params=pltpu.CompilerParams(has_side_effects=True),
    )(*partials, small)


def _sibling_share(halves):
    n = len(halves)

    def body(*refs):
        ins, outs = refs[:n], refs[n:2 * n]
        send_sems, recv_sems, loc_sems = refs[2 * n:]
        x, y, c, _ = _place()
        sib = (x, y, 1 - c)
        cps, local = [], []
        for a in range(n):
            rows = _half_rows(c, halves[a].shape[0])
            local.append(pltpu.make_async_copy(ins[a], outs[a].at[rows], loc_sems.at[a]))
            cps.append(_remote(ins[a], outs[a].at[rows], send_sems, recv_sems, a, sib))
        for cp in local + cps:
            cp.start()
        for a in range(n):
            other = outs[a].at[_half_rows(1 - c, halves[a].shape[0])]
            _remote(other, other, send_sems, recv_sems, a, sib).wait_recv()
        for cp in cps:
            cp.wait_send()
        for cp in local:
            cp.wait()

    return pl.pallas_call(
        body, name="sibling_share", in_specs=[ANY] * n, out_specs=[ANY] * n,
        out_shape=[jax.ShapeDtypeStruct((2 * h.shape[0], h.shape[1]), F32) for h in halves],
        scratch_shapes=[pltpu.SemaphoreType.DMA((n,)), pltpu.SemaphoreType.DMA((n,)), pltpu.SemaphoreType.DMA((n,))],
        compiler_params=pltpu.CompilerParams(has_side_effects=True),
    )(*halves)


def _row_tile(rows, cols, mult):
    best = mult
    for t in range(mult, rows + 1, mult):
        if rows % t == 0 and t * cols * 4 <= (1 << 20):
            best = t
    return best if rows % best == 0 else rows


def _chip_partial(place, g, t, name):
    _, r, cdim = g.shape
    half = r // 2
    tr = _row_tile(half, cdim, 16)
    nt = half // tr

    def body(pl_ref, g_ref, t_ref, pf_ref, pb_ref):
        v = g_ref[...] + t_ref[...]
        pf_ref[...] = v
        pb_ref[...] = v.astype(BF16)

    blk = (None, tr, cdim)
    return pl.pallas_call(
        body, name=name,
        grid_spec=pltpu.PrefetchScalarGridSpec(
            num_scalar_prefetch=1, grid=(4, nt),
            in_specs=[pl.BlockSpec(blk, lambda j, i, p: (j, p[1] * nt + i, 0)), pl.BlockSpec(blk, lambda j, i, p: (j, i, 0))],
            out_specs=[pl.BlockSpec(blk, lambda j, i, p: (j, i, 0))] * 2),
        out_shape=[jax.ShapeDtypeStruct((4, half, cdim), F32), jax.ShapeDtypeStruct((4, half, cdim), BF16)],
        compiler_params=_cp("arbitrary", "arbitrary"),
    )(place, g, t)


def _finish_half(place, pf, rb, name):
    _, half, cdim = pf.shape
    tr = _row_tile(half, cdim, 16)

    def body(pl_ref, pf_ref, rb_ref, o_ref):
        o_ref[...] = ((pf_ref[...] + rb_ref[0].astype(F32)) + rb_ref[1].astype(F32)) + rb_ref[2].astype(F32)

    return pl.pallas_call(
        body, name=name,
        grid_spec=pltpu.PrefetchScalarGridSpec(
            num_scalar_prefetch=1, grid=(half // tr,),
            in_specs=[pl.BlockSpec((None, tr, cdim), lambda i, p: (p[0], i, 0)), pl.BlockSpec((3, tr, cdim), lambda i, p: (0, i, 0))],
            out_specs=pl.BlockSpec((tr, cdim), lambda i, p: (i, 0))),
        out_shape=jax.ShapeDtypeStruct((half, cdim), F32),
        compiler_params=_cp("arbitrary"),
    )(place, pf, rb)


def _add2(a, b, name):
    def body(a_ref, b_ref, o_ref):
        o_ref[...] = a_ref[...] + b_ref[...]

    return pl.pallas_call(body, name=name, out_shape=jax.ShapeDtypeStruct(a.shape, F32))(a, b)


def _adam_math(w, g, m, v):
    m = ADAM_B1 * m + (1.0 - ADAM_B1) * g
    v = ADAM_B2 * v + (1.0 - ADAM_B2) * (g * g)
    m_hat = m / (1.0 - ADAM_B1 ** ADAM_STEP)
    v_hat = v / (1.0 - ADAM_B2 ** ADAM_STEP)
    return -ADAM_LR * (m_hat / (jnp.sqrt(v_hat) + ADAM_EPS) + ADAM_WD * w), m, v


def _adam(w, g, m, v, name):
    r, cdim = w.shape
    tr = _row_tile(r, cdim, 8)

    def body(w_ref, g_ref, m_ref, v_ref, d_ref, mo_ref, vo_ref):
        d, mn, vn = _adam_math(w_ref[...], g_ref[...], m_ref[...], v_ref[...])
        d_ref[...] = d
        mo_ref[...] = mn
        vo_ref[...] = vn

    spec = pl.BlockSpec((tr, cdim), lambda i: (i, 0))
    return pl.pallas_call(
        body, name=name, grid=(r // tr,), in_specs=[spec] * 4, out_specs=[spec] * 3,
        out_shape=[jax.ShapeDtypeStruct((r, cdim), F32)] * 3, compiler_params=_cp("arbitrary"),
    )(w, g, m, v)


def _adam_small(chip_sums, w, m, v):
    def body(s_ref, w_ref, m_ref, v_ref, g_ref, d_ref, mo_ref, vo_ref):
        g = ((s_ref[0] + s_ref[1]) + s_ref[2]) + s_ref[3]
        d, mn, vn = _adam_math(w_ref[...], g, m_ref[...], v_ref[...])
        g_ref[...] = g
        d_ref[...] = d
        mo_ref[...] = mn
        vo_ref[...] = vn

    return pl.pallas_call(body, name="adam_small", out_shape=[jax.ShapeDtypeStruct(w.shape, F32)] * 4)(chip_sums, w, m, v)


SMALL = (("g_mix", (1, 1024)), ("b_gate", (1, 2048)), ("w_gk_up", (1, 16, 512)), ("b_gk", (1, 512)),
         ("w_pool_grp", (1, 4, 128, 128)), ("pool_scale", (1, 512)), ("g_gla_head", (1, 256)), ("g_ffn", (1, 1024)),
         ("w_conv", (1, 3, 5632)), ("b_conv", (1, 5632)), ("g_final", (1024,)), ("loss", (768,)))
SMALL_ROWS = 808


def _pack_small(parts):
    flat = jnp.concatenate([parts[n].astype(F32).reshape(-1) for n, _ in SMALL])
    return flat.reshape(SMALL_ROWS, 128)


def _unpack_small(buf):
    flat = buf.reshape(-1)
    out, off = {}, 0
    for n, shp in SMALL:
        size = 1
        for d_ in shp:
            size *= d_
        out[n] = flat[off:off + size].reshape(shp)
        off += size
    return out


def kernel(x, g_mix, w_in, b_gate, w_gk_up, b_gk, w_pool_grp, pool_scale, g_gla_head, w_pool_proj, w_gla_proj, w_out, g_ffn, w_up, w_conv, b_conv, w_down, g_final, loss_target, m_g_mix, m_w_in, m_b_gate, m_w_gk_up, m_b_gk, m_w_pool_grp, m_pool_scale, m_g_gla_head, m_w_pool_proj, m_w_gla_proj, m_w_out, m_g_ffn, m_w_up, m_w_conv, m_b_conv, m_w_down, m_g_final, v_g_mix, v_w_in, v_b_gate, v_w_gk_up, v_b_gk, v_w_pool_grp, v_pool_scale, v_g_gla_head, v_w_pool_proj, v_w_gla_proj, v_w_out, v_g_ffn, v_w_up, v_w_conv, v_b_conv, v_w_down, v_g_final):
    s = x.shape[1]
    ts = min(s, 512)
    tm = min(s, 256)
    cx, cy, cc = lax.axis_index("x"), lax.axis_index("y"), lax.axis_index("c")
    chip = 2 * cx + cy
    place = jnp.stack([chip, cc]).astype(jnp.int32)

    big_names = ("w_in", "w_pool_proj", "w_gla_proj", "w_out", "w_up", "w_down")
    shards = dict(w_in=w_in[0], w_pool_proj=w_pool_proj[0], w_gla_proj=w_gla_proj[0], w_out=w_out[0], w_up=w_up[0],
                  w_down=w_down[0])
    gathered = _all_gather_weights([shards[n].astype(BF16) for n in big_names], [w_gk_up[0], w_conv[0]])
    wg = dict(zip(big_names, gathered[:6]))
    wgk_full = jnp.transpose(gathered[6], (1, 0, 2)).reshape(GATE_RANK, 512)
    wconv_full = jnp.transpose(gathered[7], (1, 0, 2)).reshape(3, N_UP)
    wgk_pad = jnp.concatenate([wgk_full, jnp.zeros((128 - GATE_RANK, 512), F32)], axis=0)
    w_in_full = jnp.transpose(wg["w_in"], (1, 0, 2)).reshape(D, N_IN)
    w_r = jnp.concatenate([w_in_full[:, 3600:], w_in_full[:, 1536:3584], w_in_full[:, 0:1536], w_in_full[:, 3584:3600],
                           jnp.zeros((D, 128 - GATE_RANK), BF16)], axis=1)
    wpp, wup = wg["w_pool_proj"], wg["w_up"]
    wgla = wg["w_gla_proj"].reshape(D, D)
    wout = wg["w_out"].reshape(D, D)
    wdown = wg["w_down"].reshape(D_FF, D)

    xs, tgt = x[0], loss_target[0]
    wgrp = w_pool_grp[0]

    zr, h = _norm_matmul(xs, g_mix, w_r, "in_proj", ts, 1152)
    p, pp = _pool_fwd(zr, wgrp, pool_scale)
    o, og, sp = _gla_fwd(zr, wgk_pad, b_gk, g_gla_head, tm)
    x1, mixed, yp, yg = _merge_fwd(xs, zr, pp, og, b_gate, wpp, wgla, wout, tm)
    u, h2 = _norm_matmul(x1, g_ffn, wup, "ffn_up", ts, None)
    a, dx2, dx2b, loss_part, dgfin = _ffn_down_loss(u, x1, tgt, wconv_full, b_conv, wdown, g_final.reshape(1, D), tm)

    du, dbconv, dwconv = _ffn_bwd(dx2b, u, wconv_full, b_conv, wdown, tm)
    dw_down = _matmul_tn(a, dx2b, "dw_down", D, ts)
    dw_up = _matmul_tn(h2, du, "dw_up", 1408, ts, shard_major=True)
    dx1, dx1b, dgffn = _matmul_nt_normbwd(du, wup, x1, g_ffn, dx2, "ffn_up_bwd", tm, None)
    dzg, dyp, dyg, dpp, do, dzog, dbgate, dghead = _merge_bwd(dx1b, zr, yp, yg, o, b_gate, g_gla_head, wpp, wgla, wout, tm)
    dw_out = _matmul_tn(mixed, dx1b, "dw_out", D, ts)
    dw_gla = _matmul_tn(og, dyg, "dw_gla", D, ts)
    dw_pp = _matmul_tn(pp, dyp, "dw_pp", 256, ts, shard_major=True)
    dzp, dwgrp, dscale = _pool_bwd(p, dpp, wgrp, pool_scale)
    dq, dk, dv, dgpre = _gla_bwd(zr, do, sp, wgk_pad, b_gk, tm)
    dzgk, dwgk, dbgk = _gk_bwd(dgpre, zr, wgk_pad, ts)
    dzr = jnp.concatenate([dzg, dv, dzog, dzp, dq, dk, dzgk], axis=1)
    grad_x, _, dgmix = _matmul_nt_normbwd(dzr, w_r, xs, g_mix, dx1, "in_proj_bwd", tm, 1152)
    dw_r = _matmul_tn(h, dzr, "dw_in", 640, ts)
    dw_in = jnp.concatenate([dw_r[:, OFF_POOL:OFF_GK], dw_r[:, OFF_V:OFF_POOL], dw_r[:, OFF_GK:OFF_GK + GATE_RANK],
                             dw_r[:, OFF_GATE:OFF_V]], axis=1)
    dw_in = jnp.transpose(dw_in.reshape(D, 4, N_IN // 4), (1, 0, 2))

    grads = [dw_in, dw_pp, dw_gla.reshape(4, 256, D), dw_out.reshape(4, 256, D), dw_up, dw_down.reshape(4, 704, D)]
    small_mine = _pack_small(dict(
        g_mix=dgmix, b_gate=dbgate, w_gk_up=dwgk[:GATE_RANK], b_gk=dbgk, w_pool_grp=dwgrp, pool_scale=dscale,
        g_gla_head=dghead, g_ffn=dgffn, w_conv=dwconv, b_conv=dbconv, g_final=dgfin,
        loss=jnp.concatenate([loss_part.reshape(128), jnp.zeros((640,), F32)])))
    from_sib = _sibling_exchange(grads, small_mine)
    part_f, part_b = [], []
    for n, g, t in zip(big_names, grads, from_sib[:6]):
        pf, pb = _chip_partial(place, g, t, "chip_partial_" + n)
        part_f.append(pf)
        part_b.append(pb)
    small_chip = _add2(small_mine, from_sib[6], "chip_partial_small")
    landed = _chip_exchange(part_b, small_chip)
    halves = [_finish_half(place, pf, rb, "finish_" + n) for n, pf, rb in zip(big_names, part_f, landed[:6])]
    gfull = dict(zip(big_names, _sibling_share(halves)))

    ms = dict(w_in=m_w_in, w_pool_proj=m_w_pool_proj, w_gla_proj=m_w_gla_proj, w_out=m_w_out, w_up=m_w_up, w_down=m_w_down)
    vs = dict(w_in=v_w_in, w_pool_proj=v_w_pool_proj, w_gla_proj=v_w_gla_proj, w_out=v_w_out, w_up=v_w_up, w_down=v_w_down)
    grad, delta, new_m, new_v = {}, {}, {}, {}
    for n in big_names:
        d_, m_, v_ = _adam(shards[n], gfull[n], ms[n][0], vs[n][0], "adam_" + n)
        grad[n], delta[n], new_m[n], new_v[n] = gfull[n][None], d_[None], m_[None], v_[None]

    def widen(a, width):
        z = jnp.zeros(a.shape[:-1] + (4, width), F32)
        return lax.dynamic_update_slice(z, a[..., None, :], (0,) * (a.ndim - 1) + (chip, 0)).reshape(a.shape[:-1] + (4 * width,))

    def small_of(g_mix, b_gate, w_gk_up, b_gk, w_pool_grp, pool_scale, g_gla_head, g_ffn, w_conv, b_conv, g_final):
        return _pack_small(dict(g_mix=g_mix, b_gate=b_gate, w_gk_up=widen(w_gk_up, 128), b_gk=b_gk, w_pool_grp=w_pool_grp,
                                pool_scale=pool_scale, g_gla_head=g_gla_head, g_ffn=g_ffn, w_conv=widen(w_conv, 1408),
                                b_conv=b_conv, g_final=g_final, loss=jnp.zeros((768,), F32)))

    sw = small_of(g_mix, b_gate, w_gk_up, b_gk, w_pool_grp, pool_scale, g_gla_head, g_ffn, w_conv, b_conv, g_final)
    sm = small_of(m_g_mix, m_b_gate, m_w_gk_up, m_b_gk, m_w_pool_grp, m_pool_scale, m_g_gla_head, m_g_ffn, m_w_conv,
                  m_b_conv, m_g_final)
    sv = small_of(v_g_mix, v_b_gate, v_w_gk_up, v_b_gk, v_w_pool_grp, v_pool_scale, v_g_gla_head, v_g_ffn, v_w_conv,
                  v_b_conv, v_g_final)
    sg, sd, smo, svo = _adam_small(landed[6], sw, sm, sv)

    def narrow(a, width):
        return lax.dynamic_slice_in_dim(a.reshape(a.shape[:-1] + (4, width)), chip, 1, axis=a.ndim - 1).reshape(
            a.shape[:-1] + (width,))

    loss = None
    for dst, buf in ((grad, sg), (delta, sd), (new_m, smo), (new_v, svo)):
        parts = _unpack_small(buf)
        if dst is grad:
            loss = parts["loss"][0]
        for n, _ in SMALL[:-1]:
            val = parts[n]
            if n == "w_gk_up":
                val = narrow(val, 128)
            elif n == "w_conv":
                val = narrow(val, 1408)
            dst[n] = val

    order = ("g_mix", "w_in", "b_gate", "w_gk_up", "b_gk", "w_pool_grp", "pool_scale", "g_gla_head", "w_pool_proj",
             "w_gla_proj", "w_out", "g_ffn", "w_up", "w_conv", "b_conv", "w_down", "g_final")
    return (loss, grad_x[None], *[grad[n] for n in order], *[delta[n] for n in order], *[new_m[n] for n in order],
            *[new_v[n] for n in order])
```

```python
import functools

import jax
import jax.numpy as jnp
from jax import lax
from jax.experimental import pallas as pl
from jax.experimental.pallas import tpu as pltpu

F32 = jnp.float32
BF16 = jnp.bfloat16
MESH = pl.DeviceIdType.MESH

D = 1024
EPS = 1e-6
CHUNK = 64
POOL_W = 512
POOL_WINDOWS = (2, 4, 8, 16)
HEADS = 4
HK = 128
HV = 256
GATE_RANK = 16
D_FF = 2816
N_UP = 2 * D_FF
N_IN = 5648
QSCALE = HK ** -0.5
N_INR = 5760
OFF_GATE, OFF_V, OFF_OG, OFF_POOL, OFF_Q, OFF_K, OFF_GK = 0, 2048, 3072, 4096, 4608, 5120, 5632

ADAM_LR, ADAM_B1, ADAM_B2, ADAM_EPS, ADAM_WD, ADAM_STEP = 0.001, 0.9, 0.999, 1e-08, 0.01, 10

VMEM_LIMIT = 56 * 1024 * 1024


def _cp(*sem):
    return pltpu.CompilerParams(dimension_semantics=sem if sem else None, vmem_limit_bytes=VMEM_LIMIT)


def _dot(a, b):
    return jnp.dot(a, b, preferred_element_type=F32)


def _dot_nt(a, b):
    return lax.dot_general(a, b, (((1,), (1,)), ((), ())), preferred_element_type=F32)


def _dot_tn(a, b):
    return lax.dot_general(a, b, (((0,), (0,)), ((), ())), preferred_element_type=F32)


def _sigmoid(v):
    return 1.0 / (1.0 + jnp.exp(-v))


def _rows(shape):
    return lax.broadcasted_iota(jnp.int32, shape, 0)


def _pick_row(v, r):
    return jnp.sum(jnp.where(_rows(v.shape) == r, v, 0.0), axis=0, keepdims=True)


def _norm_matmul(x, g, w, name, ts, tn, transposed=False):
    s = x.shape[0]
    if transposed:
        nj = w.shape[0] // tn
        w_spec = pl.BlockSpec((tn, D), lambda i, j: (j, 0))
    elif w.ndim == 3:
        nj, tn = w.shape[0], w.shape[2]
        w_spec = pl.BlockSpec((None, D, tn), lambda i, j: (j, 0, 0))
    else:
        nj = w.shape[1] // tn
        w_spec = pl.BlockSpec((D, tn), lambda i, j: (0, j))
    mm = _dot_nt if transposed else _dot

    def body(x_ref, g_ref, w_ref, z_ref, h_ref):
        @pl.when(pl.program_id(1) == 0)
        def _():
            xv = x_ref[...]
            r = lax.rsqrt(jnp.mean(xv * xv, axis=-1, keepdims=True) + EPS)
            h_ref[...] = (xv * r * g_ref[...]).astype(BF16)

        z_ref[...] = mm(h_ref[...], w_ref[...]).astype(BF16)

    return pl.pallas_call(
        body, name=name, grid=(s // ts, nj),
        in_specs=[pl.BlockSpec((ts, D), lambda i, j: (i, 0)), pl.BlockSpec((1, D), lambda i, j: (0, 0)), w_spec],
        out_specs=[pl.BlockSpec((ts, tn), lambda i, j: (i, j)), pl.BlockSpec((ts, D), lambda i, j: (i, 0))],
        out_shape=[jax.ShapeDtypeStruct((s, nj * tn), BF16), jax.ShapeDtypeStruct((s, D), BF16)],
        compiler_params=_cp("arbitrary", "arbitrary"),
    )(x, g, w)


def _matmul_nt_normbwd(dz, w, x, g, resid, name, ts, tk, transposed=False):
    s = x.shape[0]
    if transposed:
        nk = w.shape[0] // tk
        w_spec = pl.BlockSpec((tk, D), lambda i, k: (k, 0))
    elif w.ndim == 3:
        nk, tk = w.shape[0], w.shape[2]
        w_spec = pl.BlockSpec((None, D, tk), lambda i, k: (k, 0, 0))
    else:
        nk = w.shape[1] // tk
        w_spec = pl.BlockSpec((D, tk), lambda i, k: (0, k))
    mm = _dot if transposed else _dot_nt

    def body(dz_ref, w_ref, x_ref, g_ref, r_ref, o_ref, ob_ref, dg_ref, acc_ref):
        i, k = pl.program_id(0), pl.program_id(1)

        @pl.when(k == 0)
        def _():
            acc_ref[...] = jnp.zeros_like(acc_ref)

        @pl.when((i == 0) & (k == 0))
        def _():
            dg_ref[...] = jnp.zeros_like(dg_ref)

        acc_ref[...] += mm(dz_ref[...], w_ref[...])

        @pl.when(k == nk - 1)
        def _():
            dh = acc_ref[...]
            xv = x_ref[...]
            r = lax.rsqrt(jnp.mean(xv * xv, axis=-1, keepdims=True) + EPS)
            xh = xv * r
            dg_ref[...] += jnp.sum(dh * xh, axis=0, keepdims=True)
            dxh = dh * g_ref[...]
            out = r_ref[...] + r * (dxh - xh * jnp.mean(dxh * xh, axis=-1, keepdims=True))
            o_ref[...] = out
            ob_ref[...] = out.astype(BF16)

    row = lambda i, k: (i, 0)
    return pl.pallas_call(
        body, name=name, grid=(s // ts, nk),
        in_specs=[pl.BlockSpec((ts, tk), lambda i, k: (i, k)), w_spec, pl.BlockSpec((ts, D), row),
                  pl.BlockSpec((1, D), lambda i, k: (0, 0)), pl.BlockSpec((ts, D), row)],
        out_specs=[pl.BlockSpec((ts, D), row), pl.BlockSpec((ts, D), row), pl.BlockSpec((1, D), lambda i, k: (0, 0))],
        out_shape=[jax.ShapeDtypeStruct((s, D), F32), jax.ShapeDtypeStruct((s, D), BF16),
                   jax.ShapeDtypeStruct((1, D), F32)],
        scratch_shapes=[pltpu.VMEM((ts, D), F32)],
        compiler_params=_cp("arbitrary", "arbitrary"),
    )(dz, w, x, g, resid)


def _matmul_tn(a, b, name, tn, tk, shard_major=False, tm=None):
    s, m = a.shape
    n = b.shape[1]
    tm = m if tm is None else tm
    ni, nj, nk = m // tm, n // tn, s // tk

    def body(a_ref, b_ref, o_ref):
        @pl.when(pl.program_id(2) == 0)
        def _():
            o_ref[...] = jnp.zeros_like(o_ref)

        o_ref[...] += _dot_tn(a_ref[...], b_ref[...])

    if shard_major:
        out_spec = pl.BlockSpec((None, tm, tn), lambda i, j, k: (j, i, 0))
        out_shape = jax.ShapeDtypeStruct((nj, m, tn), F32)
    else:
        out_spec = pl.BlockSpec((tm, tn), lambda i, j, k: (i, j))
        out_shape = jax.ShapeDtypeStruct((m, n), F32)
    return pl.pallas_call(
        body, name=name, grid=(ni, nj, nk),
        in_specs=[pl.BlockSpec((tk, tm), lambda i, j, k: (k, i)), pl.BlockSpec((tk, tn), lambda i, j, k: (k, j))],
        out_specs=out_spec, out_shape=out_shape,
        compiler_params=_cp("arbitrary", "arbitrary", "arbitrary"),
    )(a, b)


def _pool_fwd(zr, wgrp, scale):
    s = zr.shape[0]

    def body(u_ref, w_ref, sc_ref, p_ref, pp_ref):
        row = _rows((s, 128))
        for gi, win in enumerate(POOL_WINDOWS):
            cs = slice(gi * 128, (gi + 1) * 128)
            u = u_ref[:, cs].astype(F32)
            acc, k = u, 1
            while k < win:
                acc = acc + jnp.where(row >= k, pltpu.roll(acc, k, 0), 0.0)
                k *= 2
            cnt = jnp.minimum(row + 1, win).astype(F32)
            p = (acc / cnt - u).astype(BF16)
            p_ref[:, cs] = p
            pp_ref[:, cs] = (_dot(p, w_ref[gi].astype(BF16)) * sc_ref[:, cs]).astype(BF16)

    return pl.pallas_call(
        body, name="pool_fwd", grid=(1,),
        in_specs=[pl.BlockSpec((s, POOL_W), lambda i: (0, OFF_POOL // POOL_W)),
                  pl.BlockSpec((4, 128, 128), lambda i: (0, 0, 0)), pl.BlockSpec((1, POOL_W), lambda i: (0, 0))],
        out_specs=[pl.BlockSpec((s, POOL_W), lambda i: (0, 0))] * 2,
        out_shape=[jax.ShapeDtypeStruct((s, POOL_W), BF16)] * 2,
        compiler_params=_cp("arbitrary"),
    )(zr, wgrp, scale)


def _pool_bwd(p, dpp, wgrp, scale):
    s = p.shape[0]

    def body(p_ref, dpp_ref, w_ref, sc_ref, dz_ref, dw_ref, dsc_ref):
        row = _rows((s, 128))
        for gi, win in enumerate(POOL_WINDOWS):
            cs = slice(gi * 128, (gi + 1) * 128)
            pv = p_ref[:, cs]
            wb = w_ref[gi].astype(BF16)
            dpp_v = dpp_ref[:, cs].astype(F32)
            dsc_ref[:, cs] = jnp.sum(dpp_v * _dot(pv, wb), axis=0, keepdims=True)
            dpm = (dpp_v * sc_ref[:, cs]).astype(BF16)
            dw_ref[gi] = _dot_tn(pv, dpm)
            dp = _dot_nt(dpm, wb)
            cnt = jnp.minimum(row + 1, win).astype(F32)
            acc, k = dp / cnt, 1
            while k < win:
                acc = acc + jnp.where(row < s - k, pltpu.roll(acc, s - k, 0), 0.0)
                k *= 2
            dz_ref[:, cs] = (acc - dp).astype(BF16)

    full = lambda i: (0, 0)
    return pl.pallas_call(
        body, name="pool_bwd", grid=(1,),
        in_specs=[pl.BlockSpec((s, POOL_W), full), pl.BlockSpec((s, POOL_W), full),
                  pl.BlockSpec((4, 128, 128), lambda i: (0, 0, 0)), pl.BlockSpec((1, POOL_W), full)],
        out_specs=[pl.BlockSpec((s, POOL_W), full), pl.BlockSpec((4, 128, 128), lambda i: (0, 0, 0)),
                   pl.BlockSpec((1, POOL_W), full)],
        out_shape=[jax.ShapeDtypeStruct((s, POOL_W), BF16), jax.ShapeDtypeStruct((4, 128, 128), F32),
                   jax.ShapeDtypeStruct((1, POOL_W), F32)],
        compiler_params=_cp("arbitrary"),
    )(p, dpp, wgrp, scale)


def _gla_decay(zgk_ref, wgk_ref, bgk_ref, rb):
    g = _dot(zgk_ref[...], wgk_ref[...].astype(BF16)) + bgk_ref[...]
    la = (jnp.minimum(g, 0.0) - jnp.log(1.0 + jnp.exp(-jnp.abs(g)))) * (1.0 / 16.0)
    rowm = _rows((rb, HK)) & (CHUNK - 1)
    bc, k = la, 1
    while k < CHUNK:
        bc = bc + jnp.where(rowm >= k, pltpu.roll(bc, k, 0), 0.0)
        k *= 2
    return g, jnp.exp(bc), jnp.exp(-bc)


def _gla_specs(rb, rmap):
    return [pl.BlockSpec((rb, HK), lambda h, r: (rmap(h, r), OFF_Q // HK + h)),
            pl.BlockSpec((rb, HK), lambda h, r: (rmap(h, r), OFF_K // HK + h)),
            pl.BlockSpec((rb, HV), lambda h, r: (rmap(h, r), OFF_V // HV + h)),
            pl.BlockSpec((rb, 128), lambda h, r: (rmap(h, r), OFF_GK // 128))]


def _gla_fwd(zr, wgk, bgk, ghead, rb):
    s = zr.shape[0]
    nc = rb // CHUNK

    def body(q_ref, k_ref, v_ref, zgk_ref, zog_ref, wgk_ref, bgk_ref, gh_ref, o_ref, og_ref, sp_ref, st_ref):
        @pl.when(pl.program_id(1) == 0)
        def _():
            st_ref[...] = jnp.zeros_like(st_ref)

        _, e_pos, e_neg = _gla_decay(zgk_ref, wgk_ref, bgk_ref, rb)
        lower = _rows((CHUNK, CHUNK)) >= lax.broadcasted_iota(jnp.int32, (CHUNK, CHUNK), 1)
        for c in range(nc):
            sl = slice(c * CHUNK, (c + 1) * CHUNK)
            q = q_ref[sl, :].astype(F32) * QSCALE
            k = k_ref[sl, :].astype(F32)
            v = v_ref[sl, :]
            ec, fc = e_pos[sl], e_neg[sl]
            qfw = (q * ec).astype(BF16)
            kfw_f = k * fc
            s_fw = _dot_nt(qfw, kfw_f.astype(BF16))
            s_bw = _dot_nt((q * fc).astype(BF16), (k * ec).astype(BF16))
            pm = jnp.where(lower, s_fw, s_bw).astype(BF16)
            st = st_ref[...]
            stb = st.astype(BF16)
            sp_ref[c] = stb
            o = _dot(pm, v) + _dot_nt(qfw, stb)
            e_last = _pick_row(ec, CHUNK - 1)
            kdec = (kfw_f * e_last).astype(BF16)
            st_ref[...] = st * e_last + _dot_tn(v, kdec)
            r = lax.rsqrt(jnp.mean(o * o, axis=-1, keepdims=True) + EPS)
            zo = zog_ref[sl, :].astype(F32)
            o_ref[sl, :] = o.astype(BF16)
            og_ref[sl, :] = (o * r * gh_ref[...] * zo * _sigmoid(zo)).astype(BF16)

    rmap = lambda h, r: r
    return pl.pallas_call(
        body, name="gla_fwd", grid=(HEADS, s // rb),
        in_specs=_gla_specs(rb, rmap) + [
            pl.BlockSpec((rb, HV), lambda h, r: (r, OFF_OG // HV + h)),
            pl.BlockSpec((128, HK), lambda h, r: (0, h)), pl.BlockSpec((1, HK), lambda h, r: (0, h)),
            pl.BlockSpec((1, HV), lambda h, r: (0, 0))],
        out_specs=[pl.BlockSpec((rb, HV), lambda h, r: (r, h)), pl.BlockSpec((rb, HV), lambda h, r: (r, h)),
                   pl.BlockSpec((nc, None, HV, HK), lambda h, r: (r, h, 0, 0))],
        out_shape=[jax.ShapeDtypeStruct((s, D), BF16), jax.ShapeDtypeStruct((s, D), BF16),
                   jax.ShapeDtypeStruct((s // CHUNK, HEADS, HV, HK), BF16)],
        scratch_shapes=[pltpu.VMEM((HV, HK), F32)],
        compiler_params=_cp("arbitrary", "arbitrary"),
    )(zr, zr, zr, zr, zr, wgk, bgk, ghead)


def _gla_bwd(zr, do, sp, wgk, bgk, rb):
    s = zr.shape[0]
    nc = rb // CHUNK
    nr = s // rb

    def body(q_ref, k_ref, v_ref, zgk_ref, do_ref, sp_ref, wgk_ref, bgk_ref, dq_ref, dk_ref, dv_ref, dg_ref,
             gt_ref, dbc_ref):
        @pl.when(pl.program_id(1) == 0)
        def _():
            gt_ref[...] = jnp.zeros_like(gt_ref)

        g, e_pos, e_neg = _gla_decay(zgk_ref, wgk_ref, bgk_ref, rb)
        lower = _rows((CHUNK, CHUNK)) >= lax.broadcasted_iota(jnp.int32, (CHUNK, CHUNK), 1)
        is_last = _rows((CHUNK, HK)) == CHUNK - 1
        for c in reversed(range(nc)):
            sl = slice(c * CHUNK, (c + 1) * CHUNK)
            q = q_ref[sl, :].astype(F32) * QSCALE
            k = k_ref[sl, :].astype(F32)
            v = v_ref[sl, :]
            dov = do_ref[sl, :]
            ec, fc = e_pos[sl], e_neg[sl]
            qfw_f, kfw_f, qbw_f, kbw_f = q * ec, k * fc, q * fc, k * ec
            qfw, kfw, qbw, kbw = qfw_f.astype(BF16), kfw_f.astype(BF16), qbw_f.astype(BF16), kbw_f.astype(BF16)
            pm = jnp.where(lower, _dot_nt(qfw, kfw), _dot_nt(qbw, kbw)).astype(BF16)
            e_last = _pick_row(ec, CHUNK - 1)
            kdec = (kfw_f * e_last).astype(BF16)
            gt = gt_ref[...]
            gtb = gt.astype(BF16)
            spv = sp_ref[c]
            dp = _dot_nt(dov, v)
            dv_ref[sl, :] = (_dot_tn(pm, dov) + _dot_nt(kdec, gtb)).astype(BF16)
            ds_fw = jnp.where(lower, dp, 0.0).astype(BF16)
            ds_bw = jnp.where(lower, 0.0, dp).astype(BF16)
            dqfw = _dot(ds_fw, kfw) + _dot(dov, spv)
            dkfw = _dot_tn(ds_fw, qfw)
            dqbw = _dot(ds_bw, kbw)
            dkbw = _dot_tn(ds_bw, qbw)
            dkdec = _dot(v, gtb)
            de_last = (jnp.sum(gt * spv.astype(F32), axis=0, keepdims=True)
                       + jnp.sum(dkdec * kfw_f, axis=0, keepdims=True))
            dkfw = dkfw + dkdec * e_last
            dq_ref[sl, :] = ((dqfw * ec + dqbw * fc) * QSCALE).astype(BF16)
            dk_ref[sl, :] = (dkfw * fc + dkbw * ec).astype(BF16)
            dbc = dqfw * qfw_f - dqbw * qbw_f + dkbw * kbw_f - dkfw * kfw_f
            dbc_ref[sl, :] = dbc + jnp.where(is_last, de_last * e_last, 0.0)
            gt_ref[...] = _dot_tn(dov, qfw) + gt * e_last
        rowm = _rows((rb, HK)) & (CHUNK - 1)
        dla, kk = dbc_ref[...], 1
        while kk < CHUNK:
            dla = dla + jnp.where(rowm < CHUNK - kk, pltpu.roll(dla, rb - kk, 0), 0.0)
            kk *= 2
        dg_ref[...] = dla * (1.0 / 16.0) * _sigmoid(-g)

    rmap = lambda h, r: nr - 1 - r
    return pl.pallas_call(
        body, name="gla_bwd", grid=(HEADS, nr),
        in_specs=_gla_specs(rb, rmap) + [
            pl.BlockSpec((rb, HV), lambda h, r: (nr - 1 - r, h)),
            pl.BlockSpec((nc, None, HV, HK), lambda h, r: (nr - 1 - r, h, 0, 0)),
            pl.BlockSpec((128, HK), lambda h, r: (0, h)), pl.BlockSpec((1, HK), lambda h, r: (0, h))],
        out_specs=[pl.BlockSpec((rb, HK), lambda h, r: (nr - 1 - r, h)), pl.BlockSpec((rb, HK), lambda h, r: (nr - 1 - r, h)),
                   pl.BlockSpec((rb, HV), lambda h, r: (nr - 1 - r, h)), pl.BlockSpec((rb, HK), lambda h, r: (nr - 1 - r, h))],
        out_shape=[jax.ShapeDtypeStruct((s, HEADS * HK), BF16), jax.ShapeDtypeStruct((s, HEADS * HK), BF16),
                   jax.ShapeDtypeStruct((s, D), BF16), jax.ShapeDtypeStruct((s, HEADS * HK), F32)],
        scratch_shapes=[pltpu.VMEM((HV, HK), F32), pltpu.VMEM((rb, HK), F32)],
        compiler_params=_cp("arbitrary", "arbitrary"),
    )(zr, zr, zr, zr, do, sp, wgk, bgk)


def _gk_bwd(dgpre, zr, wgk, ts):
    s = zr.shape[0]

    def body(dg_ref, zgk_ref, w_ref, dz_ref, dw_ref, db_ref):
        @pl.when(pl.program_id(0) == 0)
        def _():
            dw_ref[...] = jnp.zeros_like(dw_ref)
            db_ref[...] = jnp.zeros_like(db_ref)

        dg = dg_ref[...]
        dgb = dg.astype(BF16)
        dz_ref[...] = _dot_nt(dgb, w_ref[...].astype(BF16)).astype(BF16)
        dw_ref[...] += _dot_tn(zgk_ref[...], dgb)
        db_ref[...] += jnp.sum(dg, axis=0, keepdims=True)

    return pl.pallas_call(
        body, name="gk_bwd", grid=(s // ts,),
        in_specs=[pl.BlockSpec((ts, 512), lambda i: (i, 0)), pl.BlockSpec((ts, 128), lambda i: (i, OFF_GK // 128)),
                  pl.BlockSpec((128, 512), lambda i: (0, 0))],
        out_specs=[pl.BlockSpec((ts, 128), lambda i: (i, 0)), pl.BlockSpec((128, 512), lambda i: (0, 0)),
                   pl.BlockSpec((1, 512), lambda i: (0, 0))],
        out_shape=[jax.ShapeDtypeStruct((s, 128), BF16), jax.ShapeDtypeStruct((128, 512), F32),
                   jax.ShapeDtypeStruct((1, 512), F32)],
        compiler_params=_cp("arbitrary"),
    )(dgpre, zr, wgk)


def _merge_fwd(x, zr, pp, og, bgate, wpp, wgla, wout, ts):
    s = x.shape[0]

    def body(x_ref, z0_ref, z1_ref, pp_ref, og_ref, bg_ref, wpp_ref, wgla_ref, wout_ref,
             x1_ref, mix_ref, yp_ref, yg_ref):
        ppv = pp_ref[...]
        yp = jnp.concatenate([_dot(ppv, wpp_ref[j]) for j in range(4)], axis=1)
        yg = _dot(og_ref[...], wgla_ref[...])
        g0 = _sigmoid(z0_ref[...].astype(F32) + bg_ref[:, :D])
        g1 = _sigmoid(z1_ref[...].astype(F32) + bg_ref[:, D:])
        mixed = (g0 * yp + g1 * yg).astype(BF16)
        x1_ref[...] = x_ref[...] + _dot(mixed, wout_ref[...])
        mix_ref[...] = mixed
        yp_ref[...] = yp.astype(BF16)
        yg_ref[...] = yg.astype(BF16)

    row = lambda i: (i, 0)
    const2 = lambda i: (0, 0)
    return pl.pallas_call(
        body, name="merge_fwd", grid=(s // ts,),
        in_specs=[pl.BlockSpec((ts, D), row), pl.BlockSpec((ts, D), lambda i: (i, 0)), pl.BlockSpec((ts, D), lambda i: (i, 1)),
                  pl.BlockSpec((ts, POOL_W), row), pl.BlockSpec((ts, D), row), pl.BlockSpec((1, 2 * D), const2),
                  pl.BlockSpec((4, POOL_W, 256), lambda i: (0, 0, 0)), pl.BlockSpec((D, D), const2),
                  pl.BlockSpec((D, D), const2)],
        out_specs=[pl.BlockSpec((ts, D), row)] * 4,
        out_shape=[jax.ShapeDtypeStruct((s, D), F32)] + [jax.ShapeDtypeStruct((s, D), BF16)] * 3,
        compiler_params=_cp("arbitrary"),
    )(x, zr, zr, pp, og, bgate, wpp, wgla, wout)


def _merge_bwd(dx1b, zr, yp, yg, o, bgate, ghead, wpp, wgla, wout, ts):
    s = dx1b.shape[0]

    def body(dx_ref, z0_ref, z1_ref, zog_ref, yp_ref, yg_ref, o_ref, bg_ref, gh_ref, wpp_ref, wgla_ref, wout_ref,
             dzg_ref, dyp_ref, dyg_ref, dpp_ref, do_ref, dzog_ref, dbg_ref, dgh_ref):
        @pl.when(pl.program_id(0) == 0)
        def _():
            dbg_ref[...] = jnp.zeros_like(dbg_ref)
            dgh_ref[...] = jnp.zeros_like(dgh_ref)

        dmix = _dot_nt(dx_ref[...], wout_ref[...])
        g0 = _sigmoid(z0_ref[...].astype(F32) + bg_ref[:, :D])
        g1 = _sigmoid(z1_ref[...].astype(F32) + bg_ref[:, D:])
        dypb = (dmix * g0).astype(BF16)
        dygb = (dmix * g1).astype(BF16)
        dz0 = dmix * yp_ref[...].astype(F32) * g0 * (1.0 - g0)
        dz1 = dmix * yg_ref[...].astype(F32) * g1 * (1.0 - g1)
        dzg_ref[:, :D] = dz0.astype(BF16)
        dzg_ref[:, D:] = dz1.astype(BF16)
        dbg_ref[:, :D] += jnp.sum(dz0, axis=0, keepdims=True)
        dbg_ref[:, D:] += jnp.sum(dz1, axis=0, keepdims=True)
        dyp_ref[...] = dypb
        dyg_ref[...] = dygb
        dpp = _dot_nt(dypb[:, 0:256], wpp_ref[0])
        for j in range(1, 4):
            dpp = dpp + _dot_nt(dypb[:, j * 256:(j + 1) * 256], wpp_ref[j])
        dpp_ref[...] = dpp.astype(BF16)
        dog = _dot_nt(dygb, wgla_ref[...])
        gh = gh_ref[...]
        dgh = jnp.zeros((1, HV), F32)
        for h in range(HEADS):
            cs = slice(h * HV, (h + 1) * HV)
            ov = o_ref[:, cs].astype(F32)
            r = lax.rsqrt(jnp.mean(ov * ov, axis=-1, keepdims=True) + EPS)
            oh = ov * r
            zo = zog_ref[:, cs].astype(F32)
            sg = _sigmoid(zo)
            dog_h = dog[:, cs]
            don = dog_h * zo * sg
            dzog_ref[:, cs] = (dog_h * oh * gh * sg * (1.0 + zo * (1.0 - sg))).astype(BF16)
            dgh = dgh + jnp.sum(don * oh, axis=0, keepdims=True)
            doh = don * gh
            do_ref[:, cs] = (r * (doh - oh * jnp.mean(doh * oh, axis=-1, keepdims=True))).astype(BF16)
        dgh_ref[...] += dgh

    row = lambda i: (i, 0)
    const2 = lambda i: (0, 0)
    return pl.pallas_call(
        body, name="merge_bwd", grid=(s // ts,),
        in_specs=[pl.BlockSpec((ts, D), row), pl.BlockSpec((ts, D), lambda i: (i, 0)), pl.BlockSpec((ts, D), lambda i: (i, 1)),
                  pl.BlockSpec((ts, D), lambda i: (i, OFF_OG // D)), pl.BlockSpec((ts, D), row), pl.BlockSpec((ts, D), row),
                  pl.BlockSpec((ts, D), row), pl.BlockSpec((1, 2 * D), const2), pl.BlockSpec((1, HV), const2),
                  pl.BlockSpec((4, POOL_W, 256), lambda i: (0, 0, 0)), pl.BlockSpec((D, D), const2),
                  pl.BlockSpec((D, D), const2)],
        out_specs=[pl.BlockSpec((ts, 2 * D), row), pl.BlockSpec((ts, D), row), pl.BlockSpec((ts, D), row),
                   pl.BlockSpec((ts, POOL_W), row), pl.BlockSpec((ts, D), row), pl.BlockSpec((ts, D), row),
                   pl.BlockSpec((1, 2 * D), const2), pl.BlockSpec((1, HV), const2)],
        out_shape=[jax.ShapeDtypeStruct((s, 2 * D), BF16), jax.ShapeDtypeStruct((s, D), BF16),
                   jax.ShapeDtypeStruct((s, D), BF16), jax.ShapeDtypeStruct((s, POOL_W), BF16),
                   jax.ShapeDtypeStruct((s, D), BF16), jax.ShapeDtypeStruct((s, D), BF16),
                   jax.ShapeDtypeStruct((1, 2 * D), F32), jax.ShapeDtypeStruct((1, HV), F32)],
        compiler_params=_cp("arbitrary"),
    )(dx1b, zr, zr, zr, yp, yg, o, bgate, ghead, wpp, wgla, wout)


HALO = 16
CCH = 1408


def _conv_taps(u_ref, halo_ref, cs, first, ts):
    u = u_ref[:, cs].astype(F32)
    hal = halo_ref[:, cs].astype(F32)
    h1 = jnp.where(first, 0.0, _pick_row(hal, HALO - 1))
    h2 = jnp.where(first, 0.0, _pick_row(hal, HALO - 2))
    row = _rows(u.shape)
    r1 = jnp.where(row == 0, h1, pltpu.roll(u, 1, 0))
    r2 = jnp.where(row == 0, h2, jnp.where(row == 1, h1, pltpu.roll(u, 2, 0)))
    return u, r1, r2


def _ffn_down_loss(u, x1, tgt, wconv, bconv, wdown, gfin, ts):
    s = x1.shape[0]

    def body(u_ref, halo_ref, x1_ref, t_ref, wc_ref, bc_ref, wd_ref, gf_ref, a_ref, dx_ref, dxb_ref, ls_ref, dgf_ref):
        i = pl.program_id(0)

        @pl.when(i == 0)
        def _():
            ls_ref[...] = jnp.zeros_like(ls_ref)
            dgf_ref[...] = jnp.zeros_like(dgf_ref)

        first = i == 0
        acc = x1_ref[...]
        for hf in range(2):
            cg = slice(hf * CCH, (hf + 1) * CCH)
            cv = slice(D_FF + hf * CCH, D_FF + (hf + 1) * CCH)
            vals = []
            for cs in (cg, cv):
                u0, u1, u2 = _conv_taps(u_ref, halo_ref, cs, first, ts)
                vals.append(bc_ref[:, cs] + wc_ref[0:1, cs] * u2 + wc_ref[1:2, cs] * u1 + wc_ref[2:3, cs] * u0)
            a = (vals[0] * _sigmoid(vals[0]) * vals[1]).astype(BF16)
            a_ref[:, cg] = a
            acc = acc + _dot(a, wd_ref[cg, :])
        r = lax.rsqrt(jnp.mean(acc * acc, axis=-1, keepdims=True) + EPS)
        xh = acc * r
        gf = gf_ref[...]
        err = xh * gf - t_ref[...]
        ls_ref[...] += (0.5 / D) * jnp.sum(jnp.sum(err * err, axis=-1, keepdims=True), axis=0, keepdims=True)
        dy = err * (1.0 / D)
        dgf_ref[...] += jnp.sum(dy * xh, axis=0, keepdims=True)
        dxh = dy * gf
        dx = r * (dxh - xh * jnp.mean(dxh * xh, axis=-1, keepdims=True))
        dx_ref[...] = dx
        dxb_ref[...] = dx.astype(BF16)

    row = lambda i: (i, 0)
    const2 = lambda i: (0, 0)
    return pl.pallas_call(
        body, name="ffn_down_loss", grid=(s // ts,),
        in_specs=[pl.BlockSpec((ts, N_UP), row),
                  pl.BlockSpec((HALO, N_UP), lambda i: (jnp.maximum(i * (ts // HALO) - 1, 0), 0)),
                  pl.BlockSpec((ts, D), row), pl.BlockSpec((ts, D), row), pl.BlockSpec((3, N_UP), const2),
                  pl.BlockSpec((1, N_UP), const2), pl.BlockSpec((D_FF, D), const2), pl.BlockSpec((1, D), const2)],
        out_specs=[pl.BlockSpec((ts, D_FF), row), pl.BlockSpec((ts, D), row), pl.BlockSpec((ts, D), row),
                   pl.BlockSpec((1, 128), const2), pl.BlockSpec((1, D), const2)],
        out_shape=[jax.ShapeDtypeStruct((s, D_FF), BF16), jax.ShapeDtypeStruct((s, D), F32),
                   jax.ShapeDtypeStruct((s, D), BF16), jax.ShapeDtypeStruct((1, 128), F32),
                   jax.ShapeDtypeStruct((1, D), F32)],
        compiler_params=_cp("arbitrary"),
    )(u, u, x1, tgt, wconv, bconv, wdown, gfin)


def _ffn_bwd(dx2b, u, wconv, bconv, wdown, ts):
    s = dx2b.shape[0]
    nt = s // ts

    def body(dx_ref, u_ref, halo_ref, wc_ref, bc_ref, wd_ref, du_ref, db_ref, dw_ref, nxt_ref):
        i = pl.program_id(0)

        @pl.when(i == 0)
        def _():
            db_ref[...] = jnp.zeros_like(db_ref)
            dw_ref[...] = jnp.zeros_like(dw_ref)
            nxt_ref[...] = jnp.zeros_like(nxt_ref)

        first = i == nt - 1
        dxv = dx_ref[...]
        row = _rows((ts, CCH))
        for hf in range(2):
            cg = slice(hf * CCH, (hf + 1) * CCH)
            cv = slice(D_FF + hf * CCH, D_FF + (hf + 1) * CCH)
            da = _dot_nt(dxv, wd_ref[cg, :])
            taps, vals = [], []
            for cs in (cg, cv):
                t3 = _conv_taps(u_ref, halo_ref, cs, first, ts)
                taps.append(t3)
                vals.append(bc_ref[:, cs] + wc_ref[0:1, cs] * t3[2] + wc_ref[1:2, cs] * t3[1] + wc_ref[2:3, cs] * t3[0])
            sg = _sigmoid(vals[0])
            dcs = (da * vals[1] * sg * (1.0 + vals[0] * (1.0 - sg)), da * vals[0] * sg)
            for cs, (u0, u1, u2), dc in zip((cg, cv), taps, dcs):
                db_ref[:, cs] += jnp.sum(dc, axis=0, keepdims=True)
                dw_ref[0:1, cs] += jnp.sum(dc * u2, axis=0, keepdims=True)
                dw_ref[1:2, cs] += jnp.sum(dc * u1, axis=0, keepdims=True)
                dw_ref[2:3, cs] += jnp.sum(dc * u0, axis=0, keepdims=True)
                n1 = nxt_ref[0:1, cs]
                n2 = nxt_ref[1:2, cs]
                f1 = jnp.where(row == ts - 1, n1, pltpu.roll(dc, ts - 1, 0))
                f2 = jnp.where(row == ts - 1, n2, jnp.where(row == ts - 2, n1, pltpu.roll(dc, ts - 2, 0)))
                du_ref[:, cs] = (wc_ref[2:3, cs] * dc + wc_ref[1:2, cs] * f1 + wc_ref[0:1, cs] * f2).astype(BF16)
                nxt_ref[:, cs] = dc[0:8, :]

    rev = lambda i: (nt - 1 - i, 0)
    const2 = lambda i: (0, 0)
    return pl.pallas_call(
        body, name="ffn_bwd", grid=(nt,),
        in_specs=[pl.BlockSpec((ts, D), rev), pl.BlockSpec((ts, N_UP), rev),
                  pl.BlockSpec((HALO, N_UP), lambda i: (jnp.maximum((nt - 1 - i) * (ts // HALO) - 1, 0), 0)),
                  pl.BlockSpec((3, N_UP), const2), pl.BlockSpec((1, N_UP), const2), pl.BlockSpec((D_FF, D), const2)],
        out_specs=[pl.BlockSpec((ts, N_UP), rev), pl.BlockSpec((1, N_UP), const2), pl.BlockSpec((3, N_UP), const2)],
        out_shape=[jax.ShapeDtypeStruct((s, N_UP), BF16), jax.ShapeDtypeStruct((1, N_UP), F32),
                   jax.ShapeDtypeStruct((3, N_UP), F32)],
        scratch_shapes=[pltpu.VMEM((8, N_UP), F32)],
        compiler_params=_cp("arbitrary"),
    )(dx2b, u, u, wconv, bconv, wdown)


ANY = pl.BlockSpec(memory_space=pl.ANY)


def _place():
    x, y, c = lax.axis_index("x"), lax.axis_index("y"), lax.axis_index("c")
    chips = [(1 - x, y), (x, 1 - y), (1 - x, 1 - y)]
    return x, y, c, chips


def _half(shape, c, axis):
    size = shape[axis] // 2
    cut = pl.ds(pl.multiple_of(c * size, 8 if axis == 0 else 128), size)
    return (cut, slice(None)) if axis == 0 else (slice(None), cut)


def _half_shape(shape, axis):
    return (shape[0] // 2, shape[1]) if axis == 0 else (shape[0], shape[1] // 2)


def _remote(src, dst, send_sems, recv_sems, k, to):
    return pltpu.make_async_remote_copy(src_ref=src, dst_ref=dst, send_sem=send_sems.at[k], recv_sem=recv_sems.at[k],
                                        device_id=to, device_id_type=MESH)


def _all_gather_weights(big, axes, small):
    nb, ns = len(big), len(small)
    n = nb + ns
    n_sem = 6 * nb + 3 * ns

    def body(*refs):
        ins, outs = refs[:n], refs[n:2 * n]
        send_sems, recv_sems = refs[2 * n:]
        x, y, c, chips = _place()
        me = 2 * x + y
        sib = (x, y, 1 - c)
        started = []
        for a in range(nb):
            mine = _half(big[a].shape, c, axes[a])
            for k, ch in enumerate(chips):
                cp = _remote(ins[a].at[mine], outs[a].at[(me,) + mine], send_sems, recv_sems, 6 * a + k,
                             (ch[0], ch[1], c))
                cp.start()
                started.append(cp)
        for a in range(ns):
            for k, ch in enumerate(chips):
                cp = _remote(ins[nb + a], outs[nb + a].at[me], send_sems, recv_sems, 6 * nb + 3 * a + k,
                             (ch[0], ch[1], c))
                cp.start()
                started.append(cp)
        for a in range(nb):
            mine = _half(big[a].shape, c, axes[a])
            for k, ch in enumerate(chips):
                landed = outs[a].at[(2 * ch[0] + ch[1],) + mine]
                _remote(landed, landed, send_sems, recv_sems, 6 * a + k, sib).wait_recv()
                cp = _remote(landed, landed, send_sems, recv_sems, 6 * a + 3 + k, sib)
                cp.start()
                started.append(cp)
        for a in range(nb):
            other = _half(big[a].shape, 1 - c, axes[a])
            for k, ch in enumerate(chips):
                landed = outs[a].at[(2 * ch[0] + ch[1],) + other]
                _remote(landed, landed, send_sems, recv_sems, 6 * a + 3 + k, sib).wait_recv()
        for a in range(ns):
            for k, ch in enumerate(chips):
                landed = outs[nb + a].at[2 * ch[0] + ch[1]]
                _remote(landed, landed, send_sems, recv_sems, 6 * nb + 3 * a + k, sib).wait_recv()
        for cp in started:
            cp.wait_send()

    arrs = list(big) + list(small)
    return pl.pallas_call(
        body, name="all_gather_weights",
        in_specs=[ANY] * n, out_specs=[ANY] * n,
        out_shape=[jax.ShapeDtypeStruct((4,) + a.shape, a.dtype) for a in arrs],
        scratch_shapes=[pltpu.SemaphoreType.DMA((n_sem,)), pltpu.SemaphoreType.DMA((n_sem,))],
        compiler_params=pltpu.CompilerParams(has_side_effects=True),
    )(*arrs)


def _sibling_exchange(grads, axes, small):
    nb = len(grads)
    n = nb + 1

    def body(*refs):
        ins, outs = refs[:n], refs[n:2 * n]
        send_sems, recv_sems = refs[2 * n:]
        x, y, c, _ = _place()
        sib = (x, y, 1 - c)
        cps = []
        for a in range(nb):
            theirs = _half(grads[a].shape[1:], 1 - c, axes[a])
            cps.append(_remote(ins[a].at[(slice(None),) + theirs], outs[a], send_sems, recv_sems, a, sib))
        cps.append(_remote(ins[nb], outs[nb], send_sems, recv_sems, nb, sib))
        for cp in cps:
            cp.start()
        for cp in cps:
            cp.wait()

    out_shape = [jax.ShapeDtypeStruct((4,) + _half_shape(g.shape[1:], ax), F32) for g, ax in zip(grads, axes)]
    out_shape.append(jax.ShapeDtypeStruct(small.shape, F32))
    return pl.pallas_call(
        body, name="sibling_exchange", in_specs=[ANY] * n, out_specs=[ANY] * n, out_shape=out_shape,
        scratch_shapes=[pltpu.SemaphoreType.DMA((n,)), pltpu.SemaphoreType.DMA((n,))],
        compiler_params=pltpu.CompilerParams(has_side_effects=True),
    )(*grads, small)


def _chip_exchange(partials, small):
    nb = len(partials)
    n = nb + 1

    def body(*refs):
        ins, outs = refs[:n], refs[n:2 * n]
        send_sems, recv_sems = refs[2 * n:]
        x, y, c, chips = _place()
        me = 2 * x + y
        cps = []
        for a in range(nb):
            for k, ch in enumerate(chips):
                cps.append(_remote(ins[a].at[2 * ch[0] + ch[1]], outs[a].at[k], send_sems, recv_sems, 3 * a + k,
                                   (ch[0], ch[1], c)))
        for k, ch in enumerate(chips):
            cps.append(_remote(ins[nb], outs[nb].at[me], send_sems, recv_sems, 3 * nb + k, (ch[0], ch[1], c)))
        for cp in cps:
            cp.start()
        for a in range(nb):
            for k in range(3):
                cps[3 * a + k].wait_recv()
        for k, ch in enumerate(chips):
            landed = outs[nb].at[2 * ch[0] + ch[1]]
            _remote(landed, landed, send_sems, recv_sems, 3 * nb + k, (ch[0], ch[1], c)).wait_recv()
        for cp in cps:
            cp.wait_send()

    out_shape = [jax.ShapeDtypeStruct((3,) + p.shape[1:], BF16) for p in partials]
    out_shape.append(jax.ShapeDtypeStruct((4,) + small.shape, F32))
    return pl.pallas_call(
        body, name="chip_exchange", in_specs=[ANY] * n, out_specs=[ANY] * n, out_shape=out_shape,
        scratch_shapes=[pltpu.SemaphoreType.DMA((3 * n,)), pltpu.SemaphoreType.DMA((3 * n,))],
        compiler_params=pltpu.CompilerParams(has_side_effects=True),
    )(*partials, small)


def _sibling_share(halves):
    n = len(halves)

    def body(*refs):
        ins, outs = refs[:n], refs[n:2 * n]
        send_sems, recv_sems = refs[2 * n:]
        x, y, c, _ = _place()
        cps = [_remote(ins[a], outs[a], send_sems, recv_sems, a, (x, y, 1 - c)) for a in range(n)]
        for cp in cps:
            cp.start()
        for cp in cps:
            cp.wait()

    return pl.pallas_call(
        body, name="sibling_share", in_specs=[ANY] * n, out_specs=[ANY] * n,
        out_shape=[jax.ShapeDtypeStruct(h.shape, F32) for h in halves],
        scratch_shapes=[pltpu.SemaphoreType.DMA((n,)), pltpu.SemaphoreType.DMA((n,))],
        compiler_params=pltpu.CompilerParams(has_side_effects=True),
    )(*halves)


def _row_tile(rows, cols, mult):
    best = mult
    for t in range(mult, rows + 1, mult):
        if rows % t == 0 and t * cols * 4 <= (1 << 20):
            best = t
    return best if rows % best == 0 else rows


COL_TILE = 128


def _half_tiling(hshape, axis, mult):
    hr, hc = hshape
    if axis == 0:
        tr = _row_tile(hr, hc, mult)
        return tr, hc, hr // tr
    return hr, COL_TILE, hc // COL_TILE


def _tile_idx(axis, t):
    return (t, 0) if axis == 0 else (0, t)


def _chip_partial(place, g, t, axis, name):
    hshape = t.shape[1:]
    br, bc, nt = _half_tiling(hshape, axis, 16)

    def body(pl_ref, g_ref, t_ref, pf_ref, pb_ref):
        v = g_ref[...] + t_ref[...]
        pf_ref[...] = v
        pb_ref[...] = v.astype(BF16)

    blk = (None, br, bc)
    return pl.pallas_call(
        body, name=name,
        grid_spec=pltpu.PrefetchScalarGridSpec(
            num_scalar_prefetch=1, grid=(4, nt),
            in_specs=[pl.BlockSpec(blk, lambda j, i, p: (j,) + _tile_idx(axis, p[1] * nt + i)),
                      pl.BlockSpec(blk, lambda j, i, p: (j,) + _tile_idx(axis, i))],
            out_specs=[pl.BlockSpec(blk, lambda j, i, p: (j,) + _tile_idx(axis, i))] * 2),
        out_shape=[jax.ShapeDtypeStruct((4,) + hshape, F32), jax.ShapeDtypeStruct((4,) + hshape, BF16)],
        compiler_params=_cp("arbitrary", "arbitrary"),
    )(place, g, t)


def _finish_half(place, pf, rb, axis, name):
    hshape = pf.shape[1:]
    br, bc, nt = _half_tiling(hshape, axis, 16)

    def body(pl_ref, pf_ref, rb_ref, o_ref):
        o_ref[...] = ((pf_ref[...] + rb_ref[0].astype(F32)) + rb_ref[1].astype(F32)) + rb_ref[2].astype(F32)

    return pl.pallas_call(
        body, name=name,
        grid_spec=pltpu.PrefetchScalarGridSpec(
            num_scalar_prefetch=1, grid=(nt,),
            in_specs=[pl.BlockSpec((None, br, bc), lambda i, p: (p[0],) + _tile_idx(axis, i)),
                      pl.BlockSpec((3, br, bc), lambda i, p: (0,) + _tile_idx(axis, i))],
            out_specs=pl.BlockSpec((br, bc), lambda i, p: _tile_idx(axis, i))),
        out_shape=jax.ShapeDtypeStruct(hshape, F32),
        compiler_params=_cp("arbitrary"),
    )(place, pf, rb)


def _add2(a, b, name):
    def body(a_ref, b_ref, o_ref):
        o_ref[...] = a_ref[...] + b_ref[...]

    return pl.pallas_call(body, name=name, out_shape=jax.ShapeDtypeStruct(a.shape, F32))(a, b)


def _adam_math(w, g, m, v):
    m = ADAM_B1 * m + (1.0 - ADAM_B1) * g
    v = ADAM_B2 * v + (1.0 - ADAM_B2) * (g * g)
    m_hat = m / (1.0 - ADAM_B1 ** ADAM_STEP)
    v_hat = v / (1.0 - ADAM_B2 ** ADAM_STEP)
    return -ADAM_LR * (m_hat / (jnp.sqrt(v_hat) + ADAM_EPS) + ADAM_WD * w), m, v


def _adam_halves(place, w, mine, theirs, m, v, axis, name):
    br, bc, nt = _half_tiling(mine.shape, axis, 8)

    def body(pl_ref, w_ref, a_ref, b_ref, m_ref, v_ref, g_ref, d_ref, mo_ref, vo_ref):
        is_mine = pl.program_id(0) // nt == pl_ref[1]
        g = jnp.where(is_mine, a_ref[...], b_ref[...])
        d, mn, vn = _adam_math(w_ref[...], g, m_ref[...], v_ref[...])
        g_ref[...] = g
        d_ref[...] = d
        mo_ref[...] = mn
        vo_ref[...] = vn

    full = pl.BlockSpec((br, bc), lambda i, p: _tile_idx(axis, i))
    half = pl.BlockSpec((br, bc), lambda i, p: _tile_idx(axis, i % nt))
    return pl.pallas_call(
        body, name=name,
        grid_spec=pltpu.PrefetchScalarGridSpec(
            num_scalar_prefetch=1, grid=(2 * nt,), in_specs=[full, half, half, full, full], out_specs=[full] * 4),
        out_shape=[jax.ShapeDtypeStruct(w.shape, F32)] * 4, compiler_params=_cp("arbitrary"),
    )(place, w, mine, theirs, m, v)


def _adam_small(chip_sums, w, m, v):
    def body(s_ref, w_ref, m_ref, v_ref, g_ref, d_ref, mo_ref, vo_ref):
        g = ((s_ref[0] + s_ref[1]) + s_ref[2]) + s_ref[3]
        d, mn, vn = _adam_math(w_ref[...], g, m_ref[...], v_ref[...])
        g_ref[...] = g
        d_ref[...] = d
        mo_ref[...] = mn
        vo_ref[...] = vn

    return pl.pallas_call(body, name="adam_small", out_shape=[jax.ShapeDtypeStruct(w.shape, F32)] * 4)(chip_sums, w, m, v)


SMALL = (("g_mix", (1, 1024)), ("b_gate", (1, 2048)), ("w_gk_up", (1, 16, 512)), ("b_gk", (1, 512)),
         ("w_pool_grp", (1, 4, 128, 128)), ("pool_scale", (1, 512)), ("g_gla_head", (1, 256)), ("g_ffn", (1, 1024)),
         ("w_conv", (1, 3, 5632)), ("b_conv", (1, 5632)), ("g_final", (1024,)), ("loss", (768,)))
SMALL_ROWS = 808


def _pack_small(parts):
    flat = jnp.concatenate([parts[n].astype(F32).reshape(-1) for n, _ in SMALL])
    return flat.reshape(SMALL_ROWS, 128)


def _unpack_small(buf):
    flat = buf.reshape(-1)
    out, off = {}, 0
    for n, shp in SMALL:
        size = 1
        for d_ in shp:
            size *= d_
        out[n] = flat[off:off + size].reshape(shp)
        off += size
    return out


def kernel(x, g_mix, w_in, b_gate, w_gk_up, b_gk, w_pool_grp, pool_scale, g_gla_head, w_pool_proj, w_gla_proj, w_out, g_ffn, w_up, w_conv, b_conv, w_down, g_final, loss_target, m_g_mix, m_w_in, m_b_gate, m_w_gk_up, m_b_gk, m_w_pool_grp, m_pool_scale, m_g_gla_head, m_w_pool_proj, m_w_gla_proj, m_w_out, m_g_ffn, m_w_up, m_w_conv, m_b_conv, m_w_down, m_g_final, v_g_mix, v_w_in, v_b_gate, v_w_gk_up, v_b_gk, v_w_pool_grp, v_pool_scale, v_g_gla_head, v_w_pool_proj, v_w_gla_proj, v_w_out, v_g_ffn, v_w_up, v_w_conv, v_b_conv, v_w_down, v_g_final):
    s = x.shape[1]
    ts = min(s, 512)
    tm = min(s, 256)
    cx, cy, cc = lax.axis_index("x"), lax.axis_index("y"), lax.axis_index("c")
    chip = 2 * cx + cy
    place = jnp.stack([chip, cc]).astype(jnp.int32)

    big_names = ("w_in", "w_pool_proj", "w_gla_proj", "w_out", "w_up", "w_down")
    axes = (1, 0, 0, 0, 0, 0)
    shards = dict(w_in=jnp.transpose(w_in[0]), w_pool_proj=w_pool_proj[0], w_gla_proj=w_gla_proj[0], w_out=w_out[0],
                  w_up=w_up[0], w_down=w_down[0])
    own = [shards[n].astype(BF16) for n in big_names] + [w_gk_up[0], w_conv[0]]
    gathered = _all_gather_weights(own[:6], axes, own[6:])
    gathered = [lax.dynamic_update_slice(g, o_[None], (chip, 0, 0)) for g, o_ in zip(gathered, own)]
    wg = dict(zip(big_names, gathered[:6]))
    wgk_full = jnp.transpose(gathered[6], (1, 0, 2)).reshape(GATE_RANK, 512)
    wconv_full = jnp.transpose(gathered[7], (1, 0, 2)).reshape(3, N_UP)
    wgk_pad = jnp.concatenate([wgk_full, jnp.zeros((128 - GATE_RANK, 512), F32)], axis=0)
    w_in_t = wg["w_in"].reshape(N_IN, D)
    w_rt = jnp.concatenate([w_in_t[3600:], w_in_t[1536:3584], w_in_t[0:1536], w_in_t[3584:3600],
                            jnp.zeros((128 - GATE_RANK, D), BF16)], axis=0)
    wpp, wup = wg["w_pool_proj"], wg["w_up"]
    wgla = wg["w_gla_proj"].reshape(D, D)
    wout = wg["w_out"].reshape(D, D)
    wdown = wg["w_down"].reshape(D_FF, D)

    xs, tgt = x[0], loss_target[0]
    wgrp = w_pool_grp[0]

    zr, h = _norm_matmul(xs, g_mix, w_rt, "in_proj", ts, 1152, transposed=True)
    p, pp = _pool_fwd(zr, wgrp, pool_scale)
    o, og, sp = _gla_fwd(zr, wgk_pad, b_gk, g_gla_head, tm)
    x1, mixed, yp, yg = _merge_fwd(xs, zr, pp, og, b_gate, wpp, wgla, wout, tm)
    u, h2 = _norm_matmul(x1, g_ffn, wup, "ffn_up", ts, None)
    a, dx2, dx2b, loss_part, dgfin = _ffn_down_loss(u, x1, tgt, wconv_full, b_conv, wdown, g_final.reshape(1, D), tm)

    du, dbconv, dwconv = _ffn_bwd(dx2b, u, wconv_full, b_conv, wdown, tm)
    dw_down = _matmul_tn(a, dx2b, "dw_down", D, ts)
    dw_up = _matmul_tn(h2, du, "dw_up", 1408, ts, shard_major=True)
    dx1, dx1b, dgffn = _matmul_nt_normbwd(du, wup, x1, g_ffn, dx2, "ffn_up_bwd", tm, None)
    dzg, dyp, dyg, dpp, do, dzog, dbgate, dghead = _merge_bwd(dx1b, zr, yp, yg, o, b_gate, g_gla_head, wpp, wgla, wout, tm)
    dw_out = _matmul_tn(mixed, dx1b, "dw_out", D, ts)
    dw_gla = _matmul_tn(og, dyg, "dw_gla", D, ts)
    dw_pp = _matmul_tn(pp, dyp, "dw_pp", 256, ts, shard_major=True)
    dzp, dwgrp, dscale = _pool_bwd(p, dpp, wgrp, pool_scale)
    dq, dk, dv, dgpre = _gla_bwd(zr, do, sp, wgk_pad, b_gk, tm)
    dzgk, dwgk, dbgk = _gk_bwd(dgpre, zr, wgk_pad, ts)
    dzr = jnp.concatenate([dzg, dv, dzog, dzp, dq, dk, dzgk], axis=1)
    grad_x, _, dgmix = _matmul_nt_normbwd(dzr, w_rt, xs, g_mix, dx1, "in_proj_bwd", tm, 1152, transposed=True)
    dw_rt = _matmul_tn(dzr, h, "dw_in", D, ts, tm=1152)
    dw_in_t = jnp.concatenate([dw_rt[OFF_POOL:OFF_GK], dw_rt[OFF_V:OFF_POOL], dw_rt[OFF_GK:OFF_GK + GATE_RANK],
                               dw_rt[OFF_GATE:OFF_V]], axis=0).reshape(4, N_IN // 4, D)

    grads = [dw_in_t, dw_pp, dw_gla.reshape(4, 256, D), dw_out.reshape(4, 256, D), dw_up, dw_down.reshape(4, 704, D)]
    small_mine = _pack_small(dict(
        g_mix=dgmix, b_gate=dbgate, w_gk_up=dwgk[:GATE_RANK], b_gk=dbgk, w_pool_grp=dwgrp, pool_scale=dscale,
        g_gla_head=dghead, g_ffn=dgffn, w_conv=dwconv, b_conv=dbconv, g_final=dgfin,
        loss=jnp.concatenate([loss_part.reshape(128), jnp.zeros((640,), F32)])))
    from_sib = _sibling_exchange(grads, axes, small_mine)
    part_f, part_b = [], []
    for n, ax, g, t in zip(big_names, axes, grads, from_sib[:6]):
        pf, pb = _chip_partial(place, g, t, ax, "chip_partial_" + n)
        part_f.append(pf)
        part_b.append(pb)
    small_chip = _add2(small_mine, from_sib[6], "chip_partial_small")
    landed = _chip_exchange(part_b, small_chip)
    small_sums = lax.dynamic_update_slice(landed[6], small_chip[None], (chip, 0, 0))
    halves = [_finish_half(place, pf, rb, ax, "finish_" + n) for n, ax, pf, rb in zip(big_names, axes, part_f, landed[:6])]
    sib_halves = _sibling_share(halves)

    ms = dict(w_in=jnp.transpose(m_w_in[0]), w_pool_proj=m_w_pool_proj[0], w_gla_proj=m_w_gla_proj[0], w_out=m_w_out[0],
              w_up=m_w_up[0], w_down=m_w_down[0])
    vs = dict(w_in=jnp.transpose(v_w_in[0]), w_pool_proj=v_w_pool_proj[0], w_gla_proj=v_w_gla_proj[0], w_out=v_w_out[0],
              w_up=v_w_up[0], w_down=v_w_down[0])
    grad, delta, new_m, new_v = {}, {}, {}, {}
    for n, ax, mine, theirs in zip(big_names, axes, halves, sib_halves):
        res = _adam_halves(place, shards[n], mine, theirs, ms[n], vs[n], ax, "adam_" + n)
        if n == "w_in":
            res = [jnp.transpose(r_) for r_ in res]
        grad[n], delta[n], new_m[n], new_v[n] = [r_[None] for r_ in res]

    def widen(a, width):
        z = jnp.zeros(a.shape[:-1] + (4, width), F32)
        return lax.dynamic_update_slice(z, a[..., None, :], (0,) * (a.ndim - 1) + (chip, 0)).reshape(a.shape[:-1] + (4 * width,))

    def small_of(g_mix, b_gate, w_gk_up, b_gk, w_pool_grp, pool_scale, g_gla_head, g_ffn, w_conv, b_conv, g_final):
        return _pack_small(dict(g_mix=g_mix, b_gate=b_gate, w_gk_up=widen(w_gk_up, 128), b_gk=b_gk, w_pool_grp=w_pool_grp,
                                pool_scale=pool_scale, g_gla_head=g_gla_head, g_ffn=g_ffn, w_conv=widen(w_conv, 1408),
                                b_conv=b_conv, g_final=g_final, loss=jnp.zeros((768,), F32)))

    sw = small_of(g_mix, b_gate, w_gk_up, b_gk, w_pool_grp, pool_scale, g_gla_head, g_ffn, w_conv, b_conv, g_final)
    sm = small_of(m_g_mix, m_b_gate, m_w_gk_up, m_b_gk, m_w_pool_grp, m_pool_scale, m_g_gla_head, m_g_ffn, m_w_conv,
                  m_b_conv, m_g_final)
    sv = small_of(v_g_mix, v_b_gate, v_w_gk_up, v_b_gk, v_w_pool_grp, v_pool_scale, v_g_gla_head, v_g_ffn, v_w_conv,
                  v_b_conv, v_g_final)
    sg, sd, smo, svo = _adam_small(small_sums, sw, sm, sv)

    def narrow(a, width):
        return lax.dynamic_slice_in_dim(a.reshape(a.shape[:-1] + (4, width)), chip, 1, axis=a.ndim - 1).reshape(
            a.shape[:-1] + (width,))

    loss = None
    for dst, buf in ((grad, sg), (delta, sd), (new_m, smo), (new_v, svo)):
        parts = _unpack_small(buf)
        if dst is grad:
            loss = parts["loss"][0]
        for n, _ in SMALL[:-1]:
            val = parts[n]
            if n == "w_gk_up":
                val = narrow(val, 128)
            elif n == "w_conv":
                val = narrow(val, 1408)
            dst[n] = val

    order = ("g_mix", "w_in", "b_gate", "w_gk_up", "b_gk", "w_pool_grp", "pool_scale", "g_gla_head", "w_pool_proj",
             "w_gla_proj", "w_out", "g_ffn", "w_up", "w_conv", "b_conv", "w_down", "g_final")
    return (loss, grad_x[None], *[grad[n] for n in order], *[delta[n] for n in order], *[new_m[n] for n in order],
            *[new_v[n] for n in order])
```

```python
import functools

import jax
import jax.numpy as jnp
from jax import lax
from jax.experimental import pallas as pl
from jax.experimental.pallas import tpu as pltpu

F32 = jnp.float32
BF16 = jnp.bfloat16
MESH = pl.DeviceIdType.MESH

D = 1024
EPS = 1e-6
CHUNK = 64
POOL_W = 512
POOL_WINDOWS = (2, 4, 8, 16)
HEADS = 4
HK = 128
HV = 256
GATE_RANK = 16
D_FF = 2816
N_UP = 2 * D_FF
N_IN = 5648
QSCALE = HK ** -0.5
N_INR = 5760
OFF_GATE, OFF_V, OFF_OG, OFF_POOL, OFF_Q, OFF_K, OFF_GK = 0, 2048, 3072, 4096, 4608, 5120, 5632

ADAM_LR, ADAM_B1, ADAM_B2, ADAM_EPS, ADAM_WD, ADAM_STEP = 0.001, 0.9, 0.999, 1e-08, 0.01, 10

VMEM_LIMIT = 56 * 1024 * 1024


def _cp(*sem):
    return pltpu.CompilerParams(dimension_semantics=sem if sem else None, vmem_limit_bytes=VMEM_LIMIT)


def _dot(a, b):
    return jnp.dot(a, b, preferred_element_type=F32)


def _dot_nt(a, b):
    return lax.dot_general(a, b, (((1,), (1,)), ((), ())), preferred_element_type=F32)


def _dot_tn(a, b):
    return lax.dot_general(a, b, (((0,), (0,)), ((), ())), preferred_element_type=F32)


def _sigmoid(v):
    return 1.0 / (1.0 + jnp.exp(-v))


def _rows(shape):
    return lax.broadcasted_iota(jnp.int32, shape, 0)


def _pick_row(v, r):
    return jnp.sum(jnp.where(_rows(v.shape) == r, v, 0.0), axis=0, keepdims=True)


def _norm_matmul(x, g, w, name, ts, tn, transposed=False):
    s = x.shape[0]
    if transposed:
        nj = w.shape[0] // tn
        w_spec = pl.BlockSpec((tn, D), lambda i, j: (j, 0))
    elif w.ndim == 3:
        nj, tn = w.shape[0], w.shape[2]
        w_spec = pl.BlockSpec((None, D, tn), lambda i, j: (j, 0, 0))
    else:
        nj = w.shape[1] // tn
        w_spec = pl.BlockSpec((D, tn), lambda i, j: (0, j))
    mm = _dot_nt if transposed else _dot

    def body(x_ref, g_ref, w_ref, z_ref, h_ref):
        @pl.when(pl.program_id(1) == 0)
        def _():
            xv = x_ref[...]
            r = lax.rsqrt(jnp.mean(xv * xv, axis=-1, keepdims=True) + EPS)
            h_ref[...] = (xv * r * g_ref[...]).astype(BF16)

        z_ref[...] = mm(h_ref[...], w_ref[...]).astype(BF16)

    return pl.pallas_call(
        body, name=name, grid=(s // ts, nj),
        in_specs=[pl.BlockSpec((ts, D), lambda i, j: (i, 0)), pl.BlockSpec((1, D), lambda i, j: (0, 0)), w_spec],
        out_specs=[pl.BlockSpec((ts, tn), lambda i, j: (i, j)), pl.BlockSpec((ts, D), lambda i, j: (i, 0))],
        out_shape=[jax.ShapeDtypeStruct((s, nj * tn), BF16), jax.ShapeDtypeStruct((s, D), BF16)],
        compiler_params=_cp("arbitrary", "arbitrary"),
    )(x, g, w)


def _matmul_nt_normbwd(dz, w, x, g, resid, name, ts, tk, transposed=False):
    s = x.shape[0]
    if transposed:
        nk = w.shape[0] // tk
        w_spec = pl.BlockSpec((tk, D), lambda i, k: (k, 0))
    elif w.ndim == 3:
        nk, tk = w.shape[0], w.shape[2]
        w_spec = pl.BlockSpec((None, D, tk), lambda i, k: (k, 0, 0))
    else:
        nk = w.shape[1] // tk
        w_spec = pl.BlockSpec((D, tk), lambda i, k: (0, k))
    mm = _dot if transposed else _dot_nt

    def body(dz_ref, w_ref, x_ref, g_ref, r_ref, o_ref, ob_ref, dg_ref, acc_ref):
        i, k = pl.program_id(0), pl.program_id(1)

        @pl.when(k == 0)
        def _():
            acc_ref[...] = jnp.zeros_like(acc_ref)

        @pl.when((i == 0) & (k == 0))
        def _():
            dg_ref[...] = jnp.zeros_like(dg_ref)

        acc_ref[...] += mm(dz_ref[...], w_ref[...])

        @pl.when(k == nk - 1)
        def _():
            dh = acc_ref[...]
            xv = x_ref[...]
            r = lax.rsqrt(jnp.mean(xv * xv, axis=-1, keepdims=True) + EPS)
            xh = xv * r
            dg_ref[...] += jnp.sum(dh * xh, axis=0, keepdims=True)
            dxh = dh * g_ref[...]
            out = r_ref[...] + r * (dxh - xh * jnp.mean(dxh * xh, axis=-1, keepdims=True))
            o_ref[...] = out
            ob_ref[...] = out.astype(BF16)

    row = lambda i, k: (i, 0)
    return pl.pallas_call(
        body, name=name, grid=(s // ts, nk),
        in_specs=[pl.BlockSpec((ts, tk), lambda i, k: (i, k)), w_spec, pl.BlockSpec((ts, D), row),
                  pl.BlockSpec((1, D), lambda i, k: (0, 0)), pl.BlockSpec((ts, D), row)],
        out_specs=[pl.BlockSpec((ts, D), row), pl.BlockSpec((ts, D), row), pl.BlockSpec((1, D), lambda i, k: (0, 0))],
        out_shape=[jax.ShapeDtypeStruct((s, D), F32), jax.ShapeDtypeStruct((s, D), BF16),
                   jax.ShapeDtypeStruct((1, D), F32)],
        scratch_shapes=[pltpu.VMEM((ts, D), F32)],
        compiler_params=_cp("arbitrary", "arbitrary"),
    )(dz, w, x, g, resid)


def _matmul_tn(a, b, name, tn, tk, shard_major=False, tm=None):
    s, m = a.shape
    n = b.shape[1]
    tm = m if tm is None else tm
    ni, nj, nk = m // tm, n // tn, s // tk

    def body(a_ref, b_ref, o_ref):
        @pl.when(pl.program_id(2) == 0)
        def _():
            o_ref[...] = jnp.zeros_like(o_ref)

        o_ref[...] += _dot_tn(a_ref[...], b_ref[...])

    if shard_major:
        out_spec = pl.BlockSpec((None, tm, tn), lambda i, j, k: (j, i, 0))
        out_shape = jax.ShapeDtypeStruct((nj, m, tn), F32)
    else:
        out_spec = pl.BlockSpec((tm, tn), lambda i, j, k: (i, j))
        out_shape = jax.ShapeDtypeStruct((m, n), F32)
    return pl.pallas_call(
        body, name=name, grid=(ni, nj, nk),
        in_specs=[pl.BlockSpec((tk, tm), lambda i, j, k: (k, i)), pl.BlockSpec((tk, tn), lambda i, j, k: (k, j))],
        out_specs=out_spec, out_shape=out_shape,
        compiler_params=_cp("arbitrary", "arbitrary", "arbitrary"),
    )(a, b)


def _pool_fwd(zr, wgrp, scale):
    s = zr.shape[0]

    def body(u_ref, w_ref, sc_ref, p_ref, pp_ref):
        row = _rows((s, 128))
        for gi, win in enumerate(POOL_WINDOWS):
            cs = slice(gi * 128, (gi + 1) * 128)
            u = u_ref[:, cs].astype(F32)
            acc, k = u, 1
            while k < win:
                acc = acc + jnp.where(row >= k, pltpu.roll(acc, k, 0), 0.0)
                k *= 2
            cnt = jnp.minimum(row + 1, win).astype(F32)
            p = (acc / cnt - u).astype(BF16)
            p_ref[:, cs] = p
            pp_ref[:, cs] = (_dot(p, w_ref[gi].astype(BF16)) * sc_ref[:, cs]).astype(BF16)

    return pl.pallas_call(
        body, name="pool_fwd", grid=(1,),
        in_specs=[pl.BlockSpec((s, POOL_W), lambda i: (0, OFF_POOL // POOL_W)),
                  pl.BlockSpec((4, 128, 128), lambda i: (0, 0, 0)), pl.BlockSpec((1, POOL_W), lambda i: (0, 0))],
        out_specs=[pl.BlockSpec((s, POOL_W), lambda i: (0, 0))] * 2,
        out_shape=[jax.ShapeDtypeStruct((s, POOL_W), BF16)] * 2,
        compiler_params=_cp("arbitrary"),
    )(zr, wgrp, scale)


def _pool_bwd(p, dpp, wgrp, scale):
    s = p.shape[0]

    def body(p_ref, dpp_ref, w_ref, sc_ref, dz_ref, dw_ref, dsc_ref):
        row = _rows((s, 128))
        for gi, win in enumerate(POOL_WINDOWS):
            cs = slice(gi * 128, (gi + 1) * 128)
            pv = p_ref[:, cs]
            wb = w_ref[gi].astype(BF16)
            dpp_v = dpp_ref[:, cs].astype(F32)
            dsc_ref[:, cs] = jnp.sum(dpp_v * _dot(pv, wb), axis=0, keepdims=True)
            dpm = (dpp_v * sc_ref[:, cs]).astype(BF16)
            dw_ref[gi] = _dot_tn(pv, dpm)
            dp = _dot_nt(dpm, wb)
            cnt = jnp.minimum(row + 1, win).astype(F32)
            acc, k = dp / cnt, 1
            while k < win:
                acc = acc + jnp.where(row < s - k, pltpu.roll(acc, s - k, 0), 0.0)
                k *= 2
            dz_ref[:, cs] = (acc - dp).astype(BF16)

    full = lambda i: (0, 0)
    return pl.pallas_call(
        body, name="pool_bwd", grid=(1,),
        in_specs=[pl.BlockSpec((s, POOL_W), full), pl.BlockSpec((s, POOL_W), full),
                  pl.BlockSpec((4, 128, 128), lambda i: (0, 0, 0)), pl.BlockSpec((1, POOL_W), full)],
        out_specs=[pl.BlockSpec((s, POOL_W), full), pl.BlockSpec((4, 128, 128), lambda i: (0, 0, 0)),
                   pl.BlockSpec((1, POOL_W), full)],
        out_shape=[jax.ShapeDtypeStruct((s, POOL_W), BF16), jax.ShapeDtypeStruct((4, 128, 128), F32),
                   jax.ShapeDtypeStruct((1, POOL_W), F32)],
        compiler_params=_cp("arbitrary"),
    )(p, dpp, wgrp, scale)


def _gla_decay(zgk_ref, wgk_ref, bgk_ref, rb):
    g = _dot(zgk_ref[...], wgk_ref[...].astype(BF16)) + bgk_ref[...]
    la = (jnp.minimum(g, 0.0) - jnp.log(1.0 + jnp.exp(-jnp.abs(g)))) * (1.0 / 16.0)
    rowm = _rows((rb, HK)) & (CHUNK - 1)
    bc, k = la, 1
    while k < CHUNK:
        bc = bc + jnp.where(rowm >= k, pltpu.roll(bc, k, 0), 0.0)
        k *= 2
    return g, jnp.exp(bc), jnp.exp(-bc)


def _gla_specs(rb, rmap):
    return [pl.BlockSpec((rb, HK), lambda h, r: (rmap(h, r), OFF_Q // HK + h)),
            pl.BlockSpec((rb, HK), lambda h, r: (rmap(h, r), OFF_K // HK + h)),
            pl.BlockSpec((rb, HV), lambda h, r: (rmap(h, r), OFF_V // HV + h)),
            pl.BlockSpec((rb, 128), lambda h, r: (rmap(h, r), OFF_GK // 128))]


def _gla_fwd(zr, wgk, bgk, ghead, rb):
    s = zr.shape[0]
    nc = rb // CHUNK

    def body(q_ref, k_ref, v_ref, zgk_ref, zog_ref, wgk_ref, bgk_ref, gh_ref, o_ref, og_ref, sp_ref, st_ref):
        @pl.when(pl.program_id(1) == 0)
        def _():
            st_ref[...] = jnp.zeros_like(st_ref)

        _, e_pos, e_neg = _gla_decay(zgk_ref, wgk_ref, bgk_ref, rb)
        lower = _rows((CHUNK, CHUNK)) >= lax.broadcasted_iota(jnp.int32, (CHUNK, CHUNK), 1)
        for c in range(nc):
            sl = slice(c * CHUNK, (c + 1) * CHUNK)
            q = q_ref[sl, :].astype(F32) * QSCALE
            k = k_ref[sl, :].astype(F32)
            v = v_ref[sl, :]
            ec, fc = e_pos[sl], e_neg[sl]
            qfw = (q * ec).astype(BF16)
            kfw_f = k * fc
            s_fw = _dot_nt(qfw, kfw_f.astype(BF16))
            s_bw = _dot_nt((q * fc).astype(BF16), (k * ec).astype(BF16))
            pm = jnp.where(lower, s_fw, s_bw).astype(BF16)
            st = st_ref[...]
            stb = st.astype(BF16)
            sp_ref[c] = stb
            o = _dot(pm, v) + _dot_nt(qfw, stb)
            e_last = _pick_row(ec, CHUNK - 1)
            kdec = (kfw_f * e_last).astype(BF16)
            st_ref[...] = st * e_last + _dot_tn(v, kdec)
            r = lax.rsqrt(jnp.mean(o * o, axis=-1, keepdims=True) + EPS)
            zo = zog_ref[sl, :].astype(F32)
            o_ref[sl, :] = o.astype(BF16)
            og_ref[sl, :] = (o * r * gh_ref[...] * zo * _sigmoid(zo)).astype(BF16)

    rmap = lambda h, r: r
    return pl.pallas_call(
        body, name="gla_fwd", grid=(HEADS, s // rb),
        in_specs=_gla_specs(rb, rmap) + [
            pl.BlockSpec((rb, HV), lambda h, r: (r, OFF_OG // HV + h)),
            pl.BlockSpec((128, HK), lambda h, r: (0, h)), pl.BlockSpec((1, HK), lambda h, r: (0, h)),
            pl.BlockSpec((1, HV), lambda h, r: (0, 0))],
        out_specs=[pl.BlockSpec((rb, HV), lambda h, r: (r, h)), pl.BlockSpec((rb, HV), lambda h, r: (r, h)),
                   pl.BlockSpec((nc, None, HV, HK), lambda h, r: (r, h, 0, 0))],
        out_shape=[jax.ShapeDtypeStruct((s, D), BF16), jax.ShapeDtypeStruct((s, D), BF16),
                   jax.ShapeDtypeStruct((s // CHUNK, HEADS, HV, HK), BF16)],
        scratch_shapes=[pltpu.VMEM((HV, HK), F32)],
        compiler_params=_cp("arbitrary", "arbitrary"),
    )(zr, zr, zr, zr, zr, wgk, bgk, ghead)


def _gla_bwd(zr, do, sp, wgk, bgk, rb):
    s = zr.shape[0]
    nc = rb // CHUNK
    nr = s // rb

    def body(q_ref, k_ref, v_ref, zgk_ref, do_ref, sp_ref, wgk_ref, bgk_ref, dq_ref, dk_ref, dv_ref, dg_ref,
             gt_ref, dbc_ref):
        @pl.when(pl.program_id(1) == 0)
        def _():
            gt_ref[...] = jnp.zeros_like(gt_ref)

        g, e_pos, e_neg = _gla_decay(zgk_ref, wgk_ref, bgk_ref, rb)
        lower = _rows((CHUNK, CHUNK)) >= lax.broadcasted_iota(jnp.int32, (CHUNK, CHUNK), 1)
        is_last = _rows((CHUNK, HK)) == CHUNK - 1
        for c in reversed(range(nc)):
            sl = slice(c * CHUNK, (c + 1) * CHUNK)
            q = q_ref[sl, :].astype(F32) * QSCALE
            k = k_ref[sl, :].astype(F32)
            v = v_ref[sl, :]
            dov = do_ref[sl, :]
            ec, fc = e_pos[sl], e_neg[sl]
            qfw_f, kfw_f, qbw_f, kbw_f = q * ec, k * fc, q * fc, k * ec
            qfw, kfw, qbw, kbw = qfw_f.astype(BF16), kfw_f.astype(BF16), qbw_f.astype(BF16), kbw_f.astype(BF16)
            pm = jnp.where(lower, _dot_nt(qfw, kfw), _dot_nt(qbw, kbw)).astype(BF16)
            e_last = _pick_row(ec, CHUNK - 1)
            kdec = (kfw_f * e_last).astype(BF16)
            gt = gt_ref[...]
            gtb = gt.astype(BF16)
            spv = sp_ref[c]
            dp = _dot_nt(dov, v)
            dv_ref[sl, :] = (_dot_tn(pm, dov) + _dot_nt(kdec, gtb)).astype(BF16)
            ds_fw = jnp.where(lower, dp, 0.0).astype(BF16)
            ds_bw = jnp.where(lower, 0.0, dp).astype(BF16)
            dqfw = _dot(ds_fw, kfw) + _dot(dov, spv)
            dkfw = _dot_tn(ds_fw, qfw)
            dqbw = _dot(ds_bw, kbw)
            dkbw = _dot_tn(ds_bw, qbw)
            dkdec = _dot(v, gtb)
            de_last = (jnp.sum(gt * spv.astype(F32), axis=0, keepdims=True)
                       + jnp.sum(dkdec * kfw_f, axis=0, keepdims=True))
            dkfw = dkfw + dkdec * e_last
            dq_ref[sl, :] = ((dqfw * ec + dqbw * fc) * QSCALE).astype(BF16)
            dk_ref[sl, :] = (dkfw * fc + dkbw * ec).astype(BF16)
            dbc = dqfw * qfw_f - dqbw * qbw_f + dkbw * kbw_f - dkfw * kfw_f
            dbc_ref[sl, :] = dbc + jnp.where(is_last, de_last * e_last, 0.0)
            gt_ref[...] = _dot_tn(dov, qfw) + gt * e_last
        rowm = _rows((rb, HK)) & (CHUNK - 1)
        dla, kk = dbc_ref[...], 1
        while kk < CHUNK:
            dla = dla + jnp.where(rowm < CHUNK - kk, pltpu.roll(dla, rb - kk, 0), 0.0)
            kk *= 2
        dg_ref[...] = dla * (1.0 / 16.0) * _sigmoid(-g)

    rmap = lambda h, r: nr - 1 - r
    return pl.pallas_call(
        body, name="gla_bwd", grid=(HEADS, nr),
        in_specs=_gla_specs(rb, rmap) + [
            pl.BlockSpec((rb, HV), lambda h, r: (nr - 1 - r, h)),
            pl.BlockSpec((nc, None, HV, HK), lambda h, r: (nr - 1 - r, h, 0, 0)),
            pl.BlockSpec((128, HK), lambda h, r: (0, h)), pl.BlockSpec((1, HK), lambda h, r: (0, h))],
        out_specs=[pl.BlockSpec((rb, HK), lambda h, r: (nr - 1 - r, h)), pl.BlockSpec((rb, HK), lambda h, r: (nr - 1 - r, h)),
                   pl.BlockSpec((rb, HV), lambda h, r: (nr - 1 - r, h)), pl.BlockSpec((rb, HK), lambda h, r: (nr - 1 - r, h))],
        out_shape=[jax.ShapeDtypeStruct((s, HEADS * HK), BF16), jax.ShapeDtypeStruct((s, HEADS * HK), BF16),
                   jax.ShapeDtypeStruct((s, D), BF16), jax.ShapeDtypeStruct((s, HEADS * HK), F32)],
        scratch_shapes=[pltpu.VMEM((HV, HK), F32), pltpu.VMEM((rb, HK), F32)],
        compiler_params=_cp("arbitrary", "arbitrary"),
    )(zr, zr, zr, zr, do, sp, wgk, bgk)


def _gk_bwd(dgpre, zr, wgk, ts):
    s = zr.shape[0]

    def body(dg_ref, zgk_ref, w_ref, dz_ref, dw_ref, db_ref):
        @pl.when(pl.program_id(0) == 0)
        def _():
            dw_ref[...] = jnp.zeros_like(dw_ref)
            db_ref[...] = jnp.zeros_like(db_ref)

        dg = dg_ref[...]
        dgb = dg.astype(BF16)
        dz_ref[...] = _dot_nt(dgb, w_ref[...].astype(BF16)).astype(BF16)
        dw_ref[...] += _dot_tn(zgk_ref[...], dgb)
        db_ref[...] += jnp.sum(dg, axis=0, keepdims=True)

    return pl.pallas_call(
        body, name="gk_bwd", grid=(s // ts,),
        in_specs=[pl.BlockSpec((ts, 512), lambda i: (i, 0)), pl.BlockSpec((ts, 128), lambda i: (i, OFF_GK // 128)),
                  pl.BlockSpec((128, 512), lambda i: (0, 0))],
        out_specs=[pl.BlockSpec((ts, 128), lambda i: (i, 0)), pl.BlockSpec((128, 512), lambda i: (0, 0)),
                   pl.BlockSpec((1, 512), lambda i: (0, 0))],
        out_shape=[jax.ShapeDtypeStruct((s, 128), BF16), jax.ShapeDtypeStruct((128, 512), F32),
                   jax.ShapeDtypeStruct((1, 512), F32)],
        compiler_params=_cp("arbitrary"),
    )(dgpre, zr, wgk)


def _merge_fwd(x, zr, pp, og, bgate, wpp, wgla, wout, ts):
    s = x.shape[0]

    def body(x_ref, z0_ref, z1_ref, pp_ref, og_ref, bg_ref, wpp_ref, wgla_ref, wout_ref,
             x1_ref, mix_ref, yp_ref, yg_ref):
        ppv = pp_ref[...]
        yp = jnp.concatenate([_dot(ppv, wpp_ref[j]) for j in range(4)], axis=1)
        yg = _dot(og_ref[...], wgla_ref[...])
        g0 = _sigmoid(z0_ref[...].astype(F32) + bg_ref[:, :D])
        g1 = _sigmoid(z1_ref[...].astype(F32) + bg_ref[:, D:])
        mixed = (g0 * yp + g1 * yg).astype(BF16)
        x1_ref[...] = x_ref[...] + _dot(mixed, wout_ref[...])
        mix_ref[...] = mixed
        yp_ref[...] = yp.astype(BF16)
        yg_ref[...] = yg.astype(BF16)

    row = lambda i: (i, 0)
    const2 = lambda i: (0, 0)
    return pl.pallas_call(
        body, name="merge_fwd", grid=(s // ts,),
        in_specs=[pl.BlockSpec((ts, D), row), pl.BlockSpec((ts, D), lambda i: (i, 0)), pl.BlockSpec((ts, D), lambda i: (i, 1)),
                  pl.BlockSpec((ts, POOL_W), row), pl.BlockSpec((ts, D), row), pl.BlockSpec((1, 2 * D), const2),
                  pl.BlockSpec((4, POOL_W, 256), lambda i: (0, 0, 0)), pl.BlockSpec((D, D), const2),
                  pl.BlockSpec((D, D), const2)],
        out_specs=[pl.BlockSpec((ts, D), row)] * 4,
        out_shape=[jax.ShapeDtypeStruct((s, D), F32)] + [jax.ShapeDtypeStruct((s, D), BF16)] * 3,
        compiler_params=_cp("arbitrary"),
    )(x, zr, zr, pp, og, bgate, wpp, wgla, wout)


def _merge_bwd(dx1b, zr, yp, yg, o, bgate, ghead, wpp, wgla, wout, ts):
    s = dx1b.shape[0]

    def body(dx_ref, z0_ref, z1_ref, zog_ref, yp_ref, yg_ref, o_ref, bg_ref, gh_ref, wpp_ref, wgla_ref, wout_ref,
             dzg_ref, dyp_ref, dyg_ref, dpp_ref, do_ref, dzog_ref, dbg_ref, dgh_ref):
        @pl.when(pl.program_id(0) == 0)
        def _():
            dbg_ref[...] = jnp.zeros_like(dbg_ref)
            dgh_ref[...] = jnp.zeros_like(dgh_ref)

        dmix = _dot_nt(dx_ref[...], wout_ref[...])
        g0 = _sigmoid(z0_ref[...].astype(F32) + bg_ref[:, :D])
        g1 = _sigmoid(z1_ref[...].astype(F32) + bg_ref[:, D:])
        dypb = (dmix * g0).astype(BF16)
        dygb = (dmix * g1).astype(BF16)
        dz0 = dmix * yp_ref[...].astype(F32) * g0 * (1.0 - g0)
        dz1 = dmix * yg_ref[...].astype(F32) * g1 * (1.0 - g1)
        dzg_ref[:, :D] = dz0.astype(BF16)
        dzg_ref[:, D:] = dz1.astype(BF16)
        dbg_ref[:, :D] += jnp.sum(dz0, axis=0, keepdims=True)
        dbg_ref[:, D:] += jnp.sum(dz1, axis=0, keepdims=True)
        dyp_ref[...] = dypb
        dyg_ref[...] = dygb
        dpp = _dot_nt(dypb[:, 0:256], wpp_ref[0])
        for j in range(1, 4):
            dpp = dpp + _dot_nt(dypb[:, j * 256:(j + 1) * 256], wpp_ref[j])
        dpp_ref[...] = dpp.astype(BF16)
        dog = _dot_nt(dygb, wgla_ref[...])
        gh = gh_ref[...]
        dgh = jnp.zeros((1, HV), F32)
        for h in range(HEADS):
            cs = slice(h * HV, (h + 1) * HV)
            ov = o_ref[:, cs].astype(F32)
            r = lax.rsqrt(jnp.mean(ov * ov, axis=-1, keepdims=True) + EPS)
            oh = ov * r
            zo = zog_ref[:, cs].astype(F32)
            sg = _sigmoid(zo)
            dog_h = dog[:, cs]
            don = dog_h * zo * sg
            dzog_ref[:, cs] = (dog_h * oh * gh * sg * (1.0 + zo * (1.0 - sg))).astype(BF16)
            dgh = dgh + jnp.sum(don * oh, axis=0, keepdims=True)
            doh = don * gh
            do_ref[:, cs] = (r * (doh - oh * jnp.mean(doh * oh, axis=-1, keepdims=True))).astype(BF16)
        dgh_ref[...] += dgh

    row = lambda i: (i, 0)
    const2 = lambda i: (0, 0)
    return pl.pallas_call(
        body, name="merge_bwd", grid=(s // ts,),
        in_specs=[pl.BlockSpec((ts, D), row), pl.BlockSpec((ts, D), lambda i: (i, 0)), pl.BlockSpec((ts, D), lambda i: (i, 1)),
                  pl.BlockSpec((ts, D), lambda i: (i, OFF_OG // D)), pl.BlockSpec((ts, D), row), pl.BlockSpec((ts, D), row),
                  pl.BlockSpec((ts, D), row), pl.BlockSpec((1, 2 * D), const2), pl.BlockSpec((1, HV), const2),
                  pl.BlockSpec((4, POOL_W, 256), lambda i: (0, 0, 0)), pl.BlockSpec((D, D), const2),
                  pl.BlockSpec((D, D), const2)],
        out_specs=[pl.BlockSpec((ts, 2 * D), row), pl.BlockSpec((ts, D), row), pl.BlockSpec((ts, D), row),
                   pl.BlockSpec((ts, POOL_W), row), pl.BlockSpec((ts, D), row), pl.BlockSpec((ts, D), row),
                   pl.BlockSpec((1, 2 * D), const2), pl.BlockSpec((1, HV), const2)],
        out_shape=[jax.ShapeDtypeStruct((s, 2 * D), BF16), jax.ShapeDtypeStruct((s, D), BF16),
                   jax.ShapeDtypeStruct((s, D), BF16), jax.ShapeDtypeStruct((s, POOL_W), BF16),
                   jax.ShapeDtypeStruct((s, D), BF16), jax.ShapeDtypeStruct((s, D), BF16),
                   jax.ShapeDtypeStruct((1, 2 * D), F32), jax.ShapeDtypeStruct((1, HV), F32)],
        compiler_params=_cp("arbitrary"),
    )(dx1b, zr, zr, zr, yp, yg, o, bgate, ghead, wpp, wgla, wout)


HALO = 16
CCH = 1408


def _conv_taps(u_ref, halo_ref, cs, first, ts):
    u = u_ref[:, cs].astype(F32)
    hal = halo_ref[:, cs].astype(F32)
    h1 = jnp.where(first, 0.0, _pick_row(hal, HALO - 1))
    h2 = jnp.where(first, 0.0, _pick_row(hal, HALO - 2))
    row = _rows(u.shape)
    r1 = jnp.where(row == 0, h1, pltpu.roll(u, 1, 0))
    r2 = jnp.where(row == 0, h2, jnp.where(row == 1, h1, pltpu.roll(u, 2, 0)))
    return u, r1, r2


def _ffn_down_loss(u, x1, tgt, wconv, bconv, wdown, gfin, ts):
    s = x1.shape[0]

    def body(u_ref, halo_ref, x1_ref, t_ref, wc_ref, bc_ref, wd_ref, gf_ref, a_ref, dx_ref, dxb_ref, ls_ref, dgf_ref):
        i = pl.program_id(0)

        @pl.when(i == 0)
        def _():
            ls_ref[...] = jnp.zeros_like(ls_ref)
            dgf_ref[...] = jnp.zeros_like(dgf_ref)

        first = i == 0
        acc = x1_ref[...]
        for hf in range(2):
            cg = slice(hf * CCH, (hf + 1) * CCH)
            cv = slice(D_FF + hf * CCH, D_FF + (hf + 1) * CCH)
            vals = []
            for cs in (cg, cv):
                u0, u1, u2 = _conv_taps(u_ref, halo_ref, cs, first, ts)
                vals.append(bc_ref[:, cs] + wc_ref[0:1, cs] * u2 + wc_ref[1:2, cs] * u1 + wc_ref[2:3, cs] * u0)
            a = (vals[0] * _sigmoid(vals[0]) * vals[1]).astype(BF16)
            a_ref[:, cg] = a
            acc = acc + _dot(a, wd_ref[cg, :])
        r = lax.rsqrt(jnp.mean(acc * acc, axis=-1, keepdims=True) + EPS)
        xh = acc * r
        gf = gf_ref[...]
        err = xh * gf - t_ref[...]
        ls_ref[...] += (0.5 / D) * jnp.sum(jnp.sum(err * err, axis=-1, keepdims=True), axis=0, keepdims=True)
        dy = err * (1.0 / D)
        dgf_ref[...] += jnp.sum(dy * xh, axis=0, keepdims=True)
        dxh = dy * gf
        dx = r * (dxh - xh * jnp.mean(dxh * xh, axis=-1, keepdims=True))
        dx_ref[...] = dx
        dxb_ref[...] = dx.astype(BF16)

    row = lambda i: (i, 0)
    const2 = lambda i: (0, 0)
    return pl.pallas_call(
        body, name="ffn_down_loss", grid=(s // ts,),
        in_specs=[pl.BlockSpec((ts, N_UP), row),
                  pl.BlockSpec((HALO, N_UP), lambda i: (jnp.maximum(i * (ts // HALO) - 1, 0), 0)),
                  pl.BlockSpec((ts, D), row), pl.BlockSpec((ts, D), row), pl.BlockSpec((3, N_UP), const2),
                  pl.BlockSpec((1, N_UP), const2), pl.BlockSpec((D_FF, D), const2), pl.BlockSpec((1, D), const2)],
        out_specs=[pl.BlockSpec((ts, D_FF), row), pl.BlockSpec((ts, D), row), pl.BlockSpec((ts, D), row),
                   pl.BlockSpec((1, 128), const2), pl.BlockSpec((1, D), const2)],
        out_shape=[jax.ShapeDtypeStruct((s, D_FF), BF16), jax.ShapeDtypeStruct((s, D), F32),
                   jax.ShapeDtypeStruct((s, D), BF16), jax.ShapeDtypeStruct((1, 128), F32),
                   jax.ShapeDtypeStruct((1, D), F32)],
        compiler_params=_cp("arbitrary"),
    )(u, u, x1, tgt, wconv, bconv, wdown, gfin)


def _ffn_bwd(dx2b, u, wconv, bconv, wdown, ts):
    s = dx2b.shape[0]
    nt = s // ts

    def body(dx_ref, u_ref, halo_ref, wc_ref, bc_ref, wd_ref, du_ref, db_ref, dw_ref, nxt_ref):
        i = pl.program_id(0)

        @pl.when(i == 0)
        def _():
            db_ref[...] = jnp.zeros_like(db_ref)
            dw_ref[...] = jnp.zeros_like(dw_ref)
            nxt_ref[...] = jnp.zeros_like(nxt_ref)

        first = i == nt - 1
        dxv = dx_ref[...]
        row = _rows((ts, CCH))
        for hf in range(2):
            cg = slice(hf * CCH, (hf + 1) * CCH)
            cv = slice(D_FF + hf * CCH, D_FF + (hf + 1) * CCH)
            da = _dot_nt(dxv, wd_ref[cg, :])
            taps, vals = [], []
            for cs in (cg, cv):
                t3 = _conv_taps(u_ref, halo_ref, cs, first, ts)
                taps.append(t3)
                vals.append(bc_ref[:, cs] + wc_ref[0:1, cs] * t3[2] + wc_ref[1:2, cs] * t3[1] + wc_ref[2:3, cs] * t3[0])
            sg = _sigmoid(vals[0])
            dcs = (da * vals[1] * sg * (1.0 + vals[0] * (1.0 - sg)), da * vals[0] * sg)
            for cs, (u0, u1, u2), dc in zip((cg, cv), taps, dcs):
                db_ref[:, cs] += jnp.sum(dc, axis=0, keepdims=True)
                dw_ref[0:1, cs] += jnp.sum(dc * u2, axis=0, keepdims=True)
                dw_ref[1:2, cs] += jnp.sum(dc * u1, axis=0, keepdims=True)
                dw_ref[2:3, cs] += jnp.sum(dc * u0, axis=0, keepdims=True)
                n1 = nxt_ref[0:1, cs]
                n2 = nxt_ref[1:2, cs]
                f1 = jnp.where(row == ts - 1, n1, pltpu.roll(dc, ts - 1, 0))
                f2 = jnp.where(row == ts - 1, n2, jnp.where(row == ts - 2, n1, pltpu.roll(dc, ts - 2, 0)))
                du_ref[:, cs] = (wc_ref[2:3, cs] * dc + wc_ref[1:2, cs] * f1 + wc_ref[0:1, cs] * f2).astype(BF16)
                nxt_ref[:, cs] = dc[0:8, :]

    rev = lambda i: (nt - 1 - i, 0)
    const2 = lambda i: (0, 0)
    return pl.pallas_call(
        body, name="ffn_bwd", grid=(nt,),
        in_specs=[pl.BlockSpec((ts, D), rev), pl.BlockSpec((ts, N_UP), rev),
                  pl.BlockSpec((HALO, N_UP), lambda i: (jnp.maximum((nt - 1 - i) * (ts // HALO) - 1, 0), 0)),
                  pl.BlockSpec((3, N_UP), const2), pl.BlockSpec((1, N_UP), const2), pl.BlockSpec((D_FF, D), const2)],
        out_specs=[pl.BlockSpec((ts, N_UP), rev), pl.BlockSpec((1, N_UP), const2), pl.BlockSpec((3, N_UP), const2)],
        out_shape=[jax.ShapeDtypeStruct((s, N_UP), BF16), jax.ShapeDtypeStruct((1, N_UP), F32),
                   jax.ShapeDtypeStruct((3, N_UP), F32)],
        scratch_shapes=[pltpu.VMEM((8, N_UP), F32)],
        compiler_params=_cp("arbitrary"),
    )(dx2b, u, u, wconv, bconv, wdown)


ANY = pl.BlockSpec(memory_space=pl.ANY)


def _place():
    x, y, c = lax.axis_index("x"), lax.axis_index("y"), lax.axis_index("c")
    chips = [(1 - x, y), (x, 1 - y), (1 - x, 1 - y)]
    return x, y, c, chips


def _half(shape, c, axis):
    size = shape[axis] // 2
    cut = pl.ds(pl.multiple_of(c * size, 8 if axis == 0 else 128), size)
    return (cut, slice(None)) if axis == 0 else (slice(None), cut)


def _half_shape(shape, axis):
    return (shape[0] // 2, shape[1]) if axis == 0 else (shape[0], shape[1] // 2)


def _remote(src, dst, send_sems, recv_sems, k, to):
    return pltpu.make_async_remote_copy(src_ref=src, dst_ref=dst, send_sem=send_sems.at[k], recv_sem=recv_sems.at[k],
                                        device_id=to, device_id_type=MESH)


def _all_gather_weights(big, axes, small):
    nb, ns = len(big), len(small)
    n = nb + ns
    n_sem = 6 * nb + 3 * ns

    def body(*refs):
        ins, outs = refs[:n], refs[n:2 * n]
        send_sems, recv_sems = refs[2 * n:]
        x, y, c, chips = _place()
        me = 2 * x + y
        sib = (x, y, 1 - c)
        started = []
        for a in range(nb):
            mine = _half(big[a].shape, c, axes[a])
            for k, ch in enumerate(chips):
                cp = _remote(ins[a].at[mine], outs[a].at[(me,) + mine], send_sems, recv_sems, 6 * a + k,
                             (ch[0], ch[1], c))
                cp.start()
                started.append(cp)
        for a in range(ns):
            for k, ch in enumerate(chips):
                cp = _remote(ins[nb + a], outs[nb + a].at[me], send_sems, recv_sems, 6 * nb + 3 * a + k,
                             (ch[0], ch[1], c))
                cp.start()
                started.append(cp)
        for a in range(nb):
            mine = _half(big[a].shape, c, axes[a])
            for k, ch in enumerate(chips):
                landed = outs[a].at[(2 * ch[0] + ch[1],) + mine]
                _remote(landed, landed, send_sems, recv_sems, 6 * a + k, sib).wait_recv()
                cp = _remote(landed, landed, send_sems, recv_sems, 6 * a + 3 + k, sib)
                cp.start()
                started.append(cp)
        for a in range(nb):
            other = _half(big[a].shape, 1 - c, axes[a])
            for k, ch in enumerate(chips):
                landed = outs[a].at[(2 * ch[0] + ch[1],) + other]
                _remote(landed, landed, send_sems, recv_sems, 6 * a + 3 + k, sib).wait_recv()
        for a in range(ns):
            for k, ch in enumerate(chips):
                landed = outs[nb + a].at[2 * ch[0] + ch[1]]
                _remote(landed, landed, send_sems, recv_sems, 6 * nb + 3 * a + k, sib).wait_recv()
        for cp in started:
            cp.wait_send()

    arrs = list(big) + list(small)
    return pl.pallas_call(
        body, name="all_gather_weights",
        in_specs=[ANY] * n, out_specs=[ANY] * n,
        out_shape=[jax.ShapeDtypeStruct((4,) + a.shape, a.dtype) for a in arrs],
        scratch_shapes=[pltpu.SemaphoreType.DMA((n_sem,)), pltpu.SemaphoreType.DMA((n_sem,))],
        compiler_params=pltpu.CompilerParams(has_side_effects=True),
    )(*arrs)


def _sibling_exchange(grads, axes, small, name):
    nb = len(grads)
    n = nb + (small is not None)

    def body(*refs):
        ins, outs = refs[:n], refs[n:2 * n]
        send_sems, recv_sems = refs[2 * n:]
        x, y, c, _ = _place()
        sib = (x, y, 1 - c)
        cps = []
        for a in range(nb):
            theirs = _half(grads[a].shape[1:], 1 - c, axes[a])
            cps.append(_remote(ins[a].at[(slice(None),) + theirs], outs[a], send_sems, recv_sems, a, sib))
        if small is not None:
            cps.append(_remote(ins[nb], outs[nb], send_sems, recv_sems, nb, sib))
        for cp in cps:
            cp.start()
        for cp in cps:
            cp.wait()

    out_shape = [jax.ShapeDtypeStruct((4,) + _half_shape(g.shape[1:], ax), F32) for g, ax in zip(grads, axes)]
    if small is not None:
        out_shape.append(jax.ShapeDtypeStruct(small.shape, F32))
    return pl.pallas_call(
        body, name=name, in_specs=[ANY] * n, out_specs=[ANY] * n, out_shape=out_shape,
        scratch_shapes=[pltpu.SemaphoreType.DMA((n,)), pltpu.SemaphoreType.DMA((n,))],
        compiler_params=pltpu.CompilerParams(has_side_effects=True),
    )(*grads, *([] if small is None else [small]))


def _gather_share(lands, axes, name):
    n = len(lands)

    def body(*refs):
        outs = refs[n:2 * n]
        send_sems, recv_sems = refs[2 * n:]
        x, y, c, chips = _place()
        sib = (x, y, 1 - c)
        cps = []
        for a in range(n):
            mine = _half(lands[a].shape[1:], c, axes[a])
            for k, ch in enumerate(chips):
                landed = outs[a].at[(2 * ch[0] + ch[1],) + mine]
                cps.append(_remote(landed, landed, send_sems, recv_sems, 3 * a + k, sib))
        for cp in cps:
            cp.start()
        for a in range(n):
            other = _half(lands[a].shape[1:], 1 - c, axes[a])
            for k, ch in enumerate(chips):
                landed = outs[a].at[(2 * ch[0] + ch[1],) + other]
                _remote(landed, landed, send_sems, recv_sems, 3 * a + k, sib).wait_recv()
        for cp in cps:
            cp.wait_send()

    return pl.pallas_call(
        body, name=name, in_specs=[ANY] * n, out_specs=[ANY] * n,
        out_shape=[jax.ShapeDtypeStruct(a.shape, a.dtype) for a in lands],
        input_output_aliases={a: a for a in range(n)},
        scratch_shapes=[pltpu.SemaphoreType.DMA((3 * n,)), pltpu.SemaphoreType.DMA((3 * n,))],
        compiler_params=pltpu.CompilerParams(has_side_effects=True),
    )(*lands)


def _sibling_share(halves, name):
    n = len(halves)

    def body(*refs):
        ins, outs = refs[:n], refs[n:2 * n]
        send_sems, recv_sems = refs[2 * n:]
        x, y, c, _ = _place()
        cps = [_remote(ins[a], outs[a], send_sems, recv_sems, a, (x, y, 1 - c)) for a in range(n)]
        for cp in cps:
            cp.start()
        for cp in cps:
            cp.wait()

    return pl.pallas_call(
        body, name=name, in_specs=[ANY] * n, out_specs=[ANY] * n,
        out_shape=[jax.ShapeDtypeStruct(h.shape, F32) for h in halves],
        scratch_shapes=[pltpu.SemaphoreType.DMA((n,)), pltpu.SemaphoreType.DMA((n,))],
        compiler_params=pltpu.CompilerParams(has_side_effects=True),
    )(*halves)


HBM = pl.BlockSpec(memory_space=pltpu.HBM)
SEM = pl.BlockSpec(memory_space=pltpu.SEMAPHORE)
DATAFLOW = pltpu.SideEffectType.DATAFLOW_SIDE_EFFECTING


def _split_start(name, srcs, land_shapes, plan, n_copies, after):
    lands = [lax.empty(shp, dt) for shp, dt in land_shapes]
    bufs = list(srcs) + lands
    nb, ns = len(bufs), len(srcs)

    def body(*refs):
        send_sems, recv_sems, token = refs[nb + 1], refs[nb + 2], refs[-1]
        for k, (src, dst, to) in enumerate(plan(refs[:ns], refs[ns:nb])):
            _remote(src, dst, send_sems, recv_sems, k, to).start()
        token[...] = jnp.zeros_like(token)

    res = pl.pallas_call(
        body, name=name,
        out_shape=(pltpu.SemaphoreType.DMA((n_copies,)), pltpu.SemaphoreType.DMA((n_copies,)),
                   *[pltpu.HBM(b.shape, b.dtype) for b in bufs], jax.ShapeDtypeStruct((8, 128), F32)),
        in_specs=[HBM] * nb + [ANY],
        out_specs=(SEM, SEM, *[HBM] * nb, pl.BlockSpec(memory_space=pltpu.VMEM)),
        input_output_aliases={i: 2 + i for i in range(nb)},
        compiler_params=pltpu.CompilerParams(has_side_effects=DATAFLOW),
    )(*[pltpu.with_memory_space_constraint(b, pltpu.HBM) for b in bufs], after)
    return (res[0], res[1], list(res[2:2 + nb])), res[-1]


def _split_wait(name, handle, n_srcs, plan, after):
    send_sems, recv_sems, bufs = handle
    nb = len(bufs)

    def body(*refs):
        sends, recvs = refs[nb], refs[nb + 1]
        for k, (src, dst, to) in enumerate(plan(refs[:n_srcs], refs[n_srcs:nb])):
            cp = _remote(src, dst, sends, recvs, k, to)
            cp.wait_send()
            cp.wait_recv()

    res = pl.pallas_call(
        body, name=name, out_shape=[pltpu.HBM(b.shape, b.dtype) for b in bufs],
        in_specs=[HBM] * nb + [SEM, SEM, ANY], out_specs=[HBM] * nb,
        input_output_aliases={i: i for i in range(nb)},
        compiler_params=pltpu.CompilerParams(has_side_effects=DATAFLOW),
    )(*bufs, send_sems, recv_sems, after)
    return list(res[:n_srcs]), list(res[n_srcs:])


def _gather_plan(shapes, axes):
    def plan(srcs, lands):
        x, y, c, chips = _place()
        out = []
        for a, (shape, axis) in enumerate(zip(shapes, axes)):
            mine = _half(shape, c, axis)
            for ch in chips:
                out.append((srcs[a].at[mine], lands[a].at[(2 * x + y,) + mine], (ch[0], ch[1], c)))
        return out
    return plan


def _reduce_plan(n_big, with_small):
    def plan(srcs, lands):
        x, y, c, chips = _place()
        out = []
        for a in range(n_big):
            for k, ch in enumerate(chips):
                out.append((srcs[a].at[2 * ch[0] + ch[1]], lands[a].at[k], (ch[0], ch[1], c)))
        if with_small:
            for ch in chips:
                out.append((srcs[n_big], lands[n_big].at[2 * x + y], (ch[0], ch[1], c)))
        return out
    return plan


def _row_tile(rows, cols, mult):
    best = mult
    for t in range(mult, rows + 1, mult):
        if rows % t == 0 and t * cols * 4 <= (1 << 20):
            best = t
    return best if rows % best == 0 else rows


COL_TILE = 128


def _half_tiling(hshape, axis, mult):
    hr, hc = hshape
    if axis == 0:
        tr = _row_tile(hr, hc, mult)
        return tr, hc, hr // tr
    return hr, COL_TILE, hc // COL_TILE


def _tile_idx(axis, t):
    return (t, 0) if axis == 0 else (0, t)


def _chip_partial(place, g, t, axis, name):
    hshape = t.shape[1:]
    br, bc, nt = _half_tiling(hshape, axis, 16)

    def body(pl_ref, g_ref, t_ref, pf_ref, pb_ref):
        v = g_ref[...] + t_ref[...]
        pf_ref[...] = v
        pb_ref[...] = v.astype(BF16)

    blk = (None, br, bc)
    return pl.pallas_call(
        body, name=name,
        grid_spec=pltpu.PrefetchScalarGridSpec(
            num_scalar_prefetch=1, grid=(4, nt),
            in_specs=[pl.BlockSpec(blk, lambda j, i, p: (j,) + _tile_idx(axis, p[1] * nt + i)),
                      pl.BlockSpec(blk, lambda j, i, p: (j,) + _tile_idx(axis, i))],
            out_specs=[pl.BlockSpec(blk, lambda j, i, p: (j,) + _tile_idx(axis, i))] * 2),
        out_shape=[jax.ShapeDtypeStruct((4,) + hshape, F32), jax.ShapeDtypeStruct((4,) + hshape, BF16)],
        compiler_params=_cp("arbitrary", "arbitrary"),
    )(place, g, t)


def _finish_half(place, pf, rb, axis, name):
    hshape = pf.shape[1:]
    br, bc, nt = _half_tiling(hshape, axis, 16)

    def body(pl_ref, pf_ref, rb_ref, o_ref):
        o_ref[...] = ((pf_ref[...] + rb_ref[0].astype(F32)) + rb_ref[1].astype(F32)) + rb_ref[2].astype(F32)

    return pl.pallas_call(
        body, name=name,
        grid_spec=pltpu.PrefetchScalarGridSpec(
            num_scalar_prefetch=1, grid=(nt,),
            in_specs=[pl.BlockSpec((None, br, bc), lambda i, p: (p[0],) + _tile_idx(axis, i)),
                      pl.BlockSpec((3, br, bc), lambda i, p: (0,) + _tile_idx(axis, i))],
            out_specs=pl.BlockSpec((br, bc), lambda i, p: _tile_idx(axis, i))),
        out_shape=jax.ShapeDtypeStruct(hshape, F32),
        compiler_params=_cp("arbitrary"),
    )(place, pf, rb)


def _add2(a, b, name):
    def body(a_ref, b_ref, o_ref):
        o_ref[...] = a_ref[...] + b_ref[...]

    return pl.pallas_call(body, name=name, out_shape=jax.ShapeDtypeStruct(a.shape, F32))(a, b)


def _adam_math(w, g, m, v):
    m = ADAM_B1 * m + (1.0 - ADAM_B1) * g
    v = ADAM_B2 * v + (1.0 - ADAM_B2) * (g * g)
    m_hat = m / (1.0 - ADAM_B1 ** ADAM_STEP)
    v_hat = v / (1.0 - ADAM_B2 ** ADAM_STEP)
    return -ADAM_LR * (m_hat / (jnp.sqrt(v_hat) + ADAM_EPS) + ADAM_WD * w), m, v


def _adam_halves(place, w, mine, theirs, m, v, axis, name):
    br, bc, nt = _half_tiling(mine.shape, axis, 8)

    def body(pl_ref, w_ref, a_ref, b_ref, m_ref, v_ref, g_ref, d_ref, mo_ref, vo_ref):
        is_mine = pl.program_id(0) // nt == pl_ref[1]
        g = jnp.where(is_mine, a_ref[...], b_ref[...])
        d, mn, vn = _adam_math(w_ref[...], g, m_ref[...], v_ref[...])
        g_ref[...] = g
        d_ref[...] = d
        mo_ref[...] = mn
        vo_ref[...] = vn

    full = pl.BlockSpec((br, bc), lambda i, p: _tile_idx(axis, i))
    half = pl.BlockSpec((br, bc), lambda i, p: _tile_idx(axis, i % nt))
    return pl.pallas_call(
        body, name=name,
        grid_spec=pltpu.PrefetchScalarGridSpec(
            num_scalar_prefetch=1, grid=(2 * nt,), in_specs=[full, half, half, full, full], out_specs=[full] * 4),
        out_shape=[jax.ShapeDtypeStruct(w.shape, F32)] * 4, compiler_params=_cp("arbitrary"),
    )(place, w, mine, theirs, m, v)


def _adam_small(chip_sums, w, m, v):
    def body(s_ref, w_ref, m_ref, v_ref, g_ref, d_ref, mo_ref, vo_ref):
        g = ((s_ref[0] + s_ref[1]) + s_ref[2]) + s_ref[3]
        d, mn, vn = _adam_math(w_ref[...], g, m_ref[...], v_ref[...])
        g_ref[...] = g
        d_ref[...] = d
        mo_ref[...] = mn
        vo_ref[...] = vn

    return pl.pallas_call(body, name="adam_small", out_shape=[jax.ShapeDtypeStruct(w.shape, F32)] * 4)(chip_sums, w, m, v)


SMALL = (("g_mix", (1, 1024)), ("b_gate", (1, 2048)), ("w_gk_up", (1, 16, 512)), ("b_gk", (1, 512)),
         ("w_pool_grp", (1, 4, 128, 128)), ("pool_scale", (1, 512)), ("g_gla_head", (1, 256)), ("g_ffn", (1, 1024)),
         ("w_conv", (1, 3, 5632)), ("b_conv", (1, 5632)), ("g_final", (1024,)), ("loss", (768,)))
SMALL_ROWS = 808


def _pack_small(parts):
    flat = jnp.concatenate([parts[n].astype(F32).reshape(-1) for n, _ in SMALL])
    return flat.reshape(SMALL_ROWS, 128)


def _unpack_small(buf):
    flat = buf.reshape(-1)
    out, off = {}, 0
    for n, shp in SMALL:
        size = 1
        for d_ in shp:
            size *= d_
        out[n] = flat[off:off + size].reshape(shp)
        off += size
    return out


def kernel(x, g_mix, w_in, b_gate, w_gk_up, b_gk, w_pool_grp, pool_scale, g_gla_head, w_pool_proj, w_gla_proj, w_out, g_ffn, w_up, w_conv, b_conv, w_down, g_final, loss_target, m_g_mix, m_w_in, m_b_gate, m_w_gk_up, m_b_gk, m_w_pool_grp, m_pool_scale, m_g_gla_head, m_w_pool_proj, m_w_gla_proj, m_w_out, m_g_ffn, m_w_up, m_w_conv, m_b_conv, m_w_down, m_g_final, v_g_mix, v_w_in, v_b_gate, v_w_gk_up, v_b_gk, v_w_pool_grp, v_pool_scale, v_g_gla_head, v_w_pool_proj, v_w_gla_proj, v_w_out, v_g_ffn, v_w_up, v_w_conv, v_b_conv, v_w_down, v_g_final):
    s = x.shape[1]
    ts = min(s, 512)
    tm = min(s, 256)
    cx, cy, cc = lax.axis_index("x"), lax.axis_index("y"), lax.axis_index("c")
    chip = 2 * cx + cy
    place = jnp.stack([chip, cc]).astype(jnp.int32)

    big_names = ("w_in", "w_pool_proj", "w_gla_proj", "w_out", "w_up", "w_down")
    axes = (1, 0, 0, 0, 0, 0)
    shards = dict(w_in=jnp.transpose(w_in[0]), w_pool_proj=w_pool_proj[0], w_gla_proj=w_gla_proj[0], w_out=w_out[0],
                  w_up=w_up[0], w_down=w_down[0])
    own = [shards[n].astype(BF16) for n in big_names] + [w_gk_up[0], w_conv[0]]

    def fill_own(lands, mine):
        return [lax.dynamic_update_slice(g, o_[None], (chip, 0, 0)) for g, o_ in zip(lands, mine)]

    first = fill_own(_all_gather_weights(own[:1], axes[:1], own[6:]), own[:1] + own[6:])
    mix_plan_w = _gather_plan([o_.shape for o_ in own[1:4]], axes[1:4])
    mix_w, tok = _split_start("gather_mix_start", own[1:4], [((4,) + o_.shape, BF16) for o_ in own[1:4]], mix_plan_w, 9,
                              first[0])
    ffn_plan_w = _gather_plan([o_.shape for o_ in own[4:6]], axes[4:6])
    ffn_w, tok = _split_start("gather_ffn_start", own[4:6], [((4,) + o_.shape, BF16) for o_ in own[4:6]], ffn_plan_w, 6, tok)

    wgk_full = jnp.transpose(first[1], (1, 0, 2)).reshape(GATE_RANK, 512)
    wconv_full = jnp.transpose(first[2], (1, 0, 2)).reshape(3, N_UP)
    wgk_pad = jnp.concatenate([wgk_full, jnp.zeros((128 - GATE_RANK, 512), F32)], axis=0)
    w_in_t = first[0].reshape(N_IN, D)
    w_rt = jnp.concatenate([w_in_t[3600:], w_in_t[1536:3584], w_in_t[0:1536], w_in_t[3584:3600],
                            jnp.zeros((128 - GATE_RANK, D), BF16)], axis=0)

    xs, tgt = x[0], loss_target[0]
    wgrp = w_pool_grp[0]

    zr, h = _norm_matmul(xs, g_mix + tok[0:1, 0:1], w_rt, "in_proj", ts, 1152, transposed=True)
    p, pp = _pool_fwd(zr, wgrp, pool_scale)
    o, og, sp = _gla_fwd(zr, wgk_pad, b_gk, g_gla_head, tm)
    mine, lands = _split_wait("gather_mix_wait", mix_w, 3, mix_plan_w, og)
    wpp, wgla, wout = fill_own(_gather_share(lands, axes[1:4], "gather_mix_share"), mine)
    wgla, wout = wgla.reshape(D, D), wout.reshape(D, D)
    x1, mixed, yp, yg = _merge_fwd(xs, zr, pp, og, b_gate, wpp, wgla, wout, tm)
    mine, lands = _split_wait("gather_ffn_wait", ffn_w, 2, ffn_plan_w, x1)
    wup, wdown = fill_own(_gather_share(lands, axes[4:6], "gather_ffn_share"), mine)
    wdown = wdown.reshape(D_FF, D)
    u, h2 = _norm_matmul(x1, g_ffn, wup, "ffn_up", ts, None)
    a, dx2, dx2b, loss_part, dgfin = _ffn_down_loss(u, x1, tgt, wconv_full, b_conv, wdown, g_final.reshape(1, D), tm)

    du, dbconv, dwconv = _ffn_bwd(dx2b, u, wconv_full, b_conv, wdown, tm)
    dw_down = _matmul_tn(a, dx2b, "dw_down", D, ts)
    dw_up = _matmul_tn(h2, du, "dw_up", 1408, ts, shard_major=True)

    ffn_names, ffn_axes = ("w_up", "w_down"), (0, 0)
    ffn_grads = [dw_up, dw_down.reshape(4, 704, D)]
    ffn_sib = _sibling_exchange(ffn_grads, ffn_axes, None, "sibling_exchange_ffn")
    ffn_pf, ffn_pb = zip(*[_chip_partial(place, g, t, ax, "chip_partial_" + n)
                           for n, ax, g, t in zip(ffn_names, ffn_axes, ffn_grads, ffn_sib)])
    ffn_plan = _reduce_plan(2, False)
    ffn_handle, ffn_token = _split_start("reduce_ffn_start", ffn_pb, [((3,) + p.shape[1:], BF16) for p in ffn_pb],
                                         ffn_plan, 6, ffn_pf[0])

    dx1, dx1b, dgffn = _matmul_nt_normbwd(du, wup, x1, g_ffn + ffn_token[0:1, 0:1], dx2, "ffn_up_bwd", tm, None)
    dzg, dyp, dyg, dpp, do, dzog, dbgate, dghead = _merge_bwd(dx1b, zr, yp, yg, o, b_gate, g_gla_head, wpp, wgla, wout, tm)
    dw_out = _matmul_tn(mixed, dx1b, "dw_out", D, ts)
    dw_gla = _matmul_tn(og, dyg, "dw_gla", D, ts)
    dw_pp = _matmul_tn(pp, dyp, "dw_pp", 256, ts, shard_major=True)
    dzp, dwgrp, dscale = _pool_bwd(p, dpp, wgrp, pool_scale)
    dq, dk, dv, dgpre = _gla_bwd(zr, do, sp, wgk_pad, b_gk, tm)
    dzgk, dwgk, dbgk = _gk_bwd(dgpre, zr, wgk_pad, ts)
    dzr = jnp.concatenate([dzg, dv, dzog, dzp, dq, dk, dzgk], axis=1)
    grad_x, _, dgmix = _matmul_nt_normbwd(dzr, w_rt, xs, g_mix, dx1, "in_proj_bwd", tm, 1152, transposed=True)
    dw_rt = _matmul_tn(dzr, h, "dw_in", D, ts, tm=1152)
    dw_in_t = jnp.concatenate([dw_rt[OFF_POOL:OFF_GK], dw_rt[OFF_V:OFF_POOL], dw_rt[OFF_GK:OFF_GK + GATE_RANK],
                               dw_rt[OFF_GATE:OFF_V]], axis=0).reshape(4, N_IN // 4, D)

    mix_names, mix_axes = ("w_in", "w_pool_proj", "w_gla_proj", "w_out"), (1, 0, 0, 0)
    mix_grads = [dw_in_t, dw_pp, dw_gla.reshape(4, 256, D), dw_out.reshape(4, 256, D)]
    small_mine = _pack_small(dict(
        g_mix=dgmix, b_gate=dbgate, w_gk_up=dwgk[:GATE_RANK], b_gk=dbgk, w_pool_grp=dwgrp, pool_scale=dscale,
        g_gla_head=dghead, g_ffn=dgffn, w_conv=dwconv, b_conv=dbconv, g_final=dgfin,
        loss=jnp.concatenate([loss_part.reshape(128), jnp.zeros((640,), F32)])))
    mix_sib = _sibling_exchange(mix_grads, mix_axes, small_mine, "sibling_exchange_mix")
    mix_pf, mix_pb = zip(*[_chip_partial(place, g, t, ax, "chip_partial_" + n)
                           for n, ax, g, t in zip(mix_names, mix_axes, mix_grads, mix_sib[:4])])
    small_chip = _add2(small_mine, mix_sib[4], "chip_partial_small")
    mix_plan = _reduce_plan(4, True)
    mix_handle, mix_token = _split_start(
        "reduce_mix_start", list(mix_pb) + [small_chip],
        [((3,) + p.shape[1:], BF16) for p in mix_pb] + [((4,) + small_chip.shape, F32)], mix_plan, 15, mix_pf[0])

    ms = dict(w_in=jnp.transpose(m_w_in[0]), w_pool_proj=m_w_pool_proj[0], w_gla_proj=m_w_gla_proj[0], w_out=m_w_out[0],
              w_up=m_w_up[0], w_down=m_w_down[0])
    vs = dict(w_in=jnp.transpose(v_w_in[0]), w_pool_proj=v_w_pool_proj[0], w_gla_proj=v_w_gla_proj[0], w_out=v_w_out[0],
              w_up=v_w_up[0], w_down=v_w_down[0])
    grad, delta, new_m, new_v = {}, {}, {}, {}

    def finish_and_update(names, group_axes, part_f, landed, tag):
        halves = [_finish_half(place, pf, rb, ax, "finish_" + n) for n, ax, pf, rb in zip(names, group_axes, part_f, landed)]
        sib_halves = _sibling_share(halves, "sibling_share_" + tag)
        for n, ax, mine, theirs in zip(names, group_axes, halves, sib_halves):
            res = _adam_halves(place, shards[n], mine, theirs, ms[n], vs[n], ax, "adam_" + n)
            if n == "w_in":
                res = [jnp.transpose(r_) for r_ in res]
            grad[n], delta[n], new_m[n], new_v[n] = [r_[None] for r_ in res]

    _, ffn_landed = _split_wait("reduce_ffn_wait", ffn_handle, 2, ffn_plan, mix_token)
    finish_and_update(ffn_names, ffn_axes, ffn_pf, ffn_landed, "ffn")
    mix_sent, mix_landed = _split_wait("reduce_mix_wait", mix_handle, 5, mix_plan, delta["w_down"])
    small_sums = lax.dynamic_update_slice(mix_landed[4], mix_sent[4][None], (chip, 0, 0))
    finish_and_update(mix_names, mix_axes, mix_pf, mix_landed[:4], "mix")

    def widen(a, width):
        z = jnp.zeros(a.shape[:-1] + (4, width), F32)
        return lax.dynamic_update_slice(z, a[..., None, :], (0,) * (a.ndim - 1) + (chip, 0)).reshape(a.shape[:-1] + (4 * width,))

    def small_of(g_mix, b_gate, w_gk_up, b_gk, w_pool_grp, pool_scale, g_gla_head, g_ffn, w_conv, b_conv, g_final):
        return _pack_small(dict(g_mix=g_mix, b_gate=b_gate, w_gk_up=widen(w_gk_up, 128), b_gk=b_gk, w_pool_grp=w_pool_grp,
                                pool_scale=pool_scale, g_gla_head=g_gla_head, g_ffn=g_ffn, w_conv=widen(w_conv, 1408),
                                b_conv=b_conv, g_final=g_final, loss=jnp.zeros((768,), F32)))

    sw = small_of(g_mix, b_gate, w_gk_up, b_gk, w_pool_grp, pool_scale, g_gla_head, g_ffn, w_conv, b_conv, g_final)
    sm = small_of(m_g_mix, m_b_gate, m_w_gk_up, m_b_gk, m_w_pool_grp, m_pool_scale, m_g_gla_head, m_g_ffn, m_w_conv,
                  m_b_conv, m_g_final)
    sv = small_of(v_g_mix, v_b_gate, v_w_gk_up, v_b_gk, v_w_pool_grp, v_pool_scale, v_g_gla_head, v_g_ffn, v_w_conv,
                  v_b_conv, v_g_final)
    sg, sd, smo, svo = _adam_small(small_sums, sw, sm, sv)

    def narrow(a, width):
        return lax.dynamic_slice_in_dim(a.reshape(a.shape[:-1] + (4, width)), chip, 1, axis=a.ndim - 1).reshape(
            a.shape[:-1] + (width,))

    loss = None
    for dst, buf in ((grad, sg), (delta, sd), (new_m, smo), (new_v, svo)):
        parts = _unpack_small(buf)
        if dst is grad:
            loss = parts["loss"][0]
        for n, _ in SMALL[:-1]:
            val = parts[n]
            if n == "w_gk_up":
                val = narrow(val, 128)
            elif n == "w_conv":
                val = narrow(val, 1408)
            dst[n] = val

    order = ("g_mix", "w_in", "b_gate", "w_gk_up", "b_gk", "w_pool_grp", "pool_scale", "g_gla_head", "w_pool_proj",
             "w_gla_proj", "w_out", "g_ffn", "w_up", "w_conv", "b_conv", "w_down", "g_final")
    return (loss, grad_x[None], *[grad[n] for n in order], *[delta[n] for n in order], *[new_m[n] for n in order],
            *[new_v[n] for n in order])
```

```python
import functools

import jax
import jax.numpy as jnp
from jax import lax
from jax.experimental import pallas as pl
from jax.experimental.pallas import tpu as pltpu

F32 = jnp.float32
BF16 = jnp.bfloat16
MESH = pl.DeviceIdType.MESH

D = 1024
EPS = 1e-6
CHUNK = 64
POOL_W = 512
POOL_WINDOWS = (2, 4, 8, 16)
HEADS = 4
HK = 128
HV = 256
GATE_RANK = 16
D_FF = 2816
N_UP = 2 * D_FF
N_IN = 5648
QSCALE = HK ** -0.5
N_INR = 5760
OFF_GATE, OFF_V, OFF_OG, OFF_POOL, OFF_Q, OFF_K, OFF_GK = 0, 2048, 3072, 4096, 4608, 5120, 5632

ADAM_LR, ADAM_B1, ADAM_B2, ADAM_EPS, ADAM_WD, ADAM_STEP = 0.001, 0.9, 0.999, 1e-08, 0.01, 10

VMEM_LIMIT = 56 * 1024 * 1024


def _cp(*sem):
    return pltpu.CompilerParams(dimension_semantics=sem if sem else None, vmem_limit_bytes=VMEM_LIMIT)


def _dot(a, b):
    return jnp.dot(a, b, preferred_element_type=F32)


def _dot_nt(a, b):
    return lax.dot_general(a, b, (((1,), (1,)), ((), ())), preferred_element_type=F32)


def _dot_tn(a, b):
    return lax.dot_general(a, b, (((0,), (0,)), ((), ())), preferred_element_type=F32)


def _sigmoid(v):
    return 1.0 / (1.0 + jnp.exp(-v))


def _rows(shape):
    return lax.broadcasted_iota(jnp.int32, shape, 0)


def _pick_row(v, r):
    return jnp.sum(jnp.where(_rows(v.shape) == r, v, 0.0), axis=0, keepdims=True)


def _rmsnorm(x, g, name, ts):
    s = x.shape[0]

    def body(x_ref, g_ref, h_ref):
        xv = x_ref[...]
        r = lax.rsqrt(jnp.mean(xv * xv, axis=-1, keepdims=True) + EPS)
        h_ref[...] = (xv * r * g_ref[...]).astype(BF16)

    return pl.pallas_call(
        body, name=name, grid=(s // ts,),
        in_specs=[pl.BlockSpec((ts, D), lambda i: (i, 0)), pl.BlockSpec((1, D), lambda i: (0, 0))],
        out_specs=pl.BlockSpec((ts, D), lambda i: (i, 0)), out_shape=jax.ShapeDtypeStruct((s, D), BF16),
        compiler_params=_cp("arbitrary"),
    )(x, g)


MM_ROWS = 512


def _matmul_resident(h, w, name, tn, transposed=False):
    s = h.shape[0]
    if transposed:
        nj = w.shape[0] // tn
        w_spec = pl.BlockSpec((tn, D), lambda j: (j, 0))
    elif w.ndim == 3:
        nj, tn = w.shape[0], w.shape[2]
        w_spec = pl.BlockSpec((None, D, tn), lambda j: (j, 0, 0))
    else:
        nj = w.shape[1] // tn
        w_spec = pl.BlockSpec((D, tn), lambda j: (0, j))
    mm = _dot_nt if transposed else _dot
    rc = min(s, MM_ROWS)

    def body(h_ref, w_ref, z_ref):
        for r0 in range(0, s, rc):
            z_ref[r0:r0 + rc, :] = mm(h_ref[r0:r0 + rc, :], w_ref[...]).astype(BF16)

    return pl.pallas_call(
        body, name=name, grid=(nj,),
        in_specs=[pl.BlockSpec((s, D), lambda j: (0, 0)), w_spec],
        out_specs=pl.BlockSpec((s, tn), lambda j: (0, j)), out_shape=jax.ShapeDtypeStruct((s, nj * tn), BF16),
        compiler_params=_cp("arbitrary"),
    )(h, w)


def _matmul_nt_normbwd(dz, w, x, g, resid, name, ts, transposed=False):
    s = x.shape[0]

    def body(dz_ref, w_hbm, x_ref, g_ref, r_ref, o_ref, ob_ref, dg_ref, w_ref, sem):
        @pl.when(pl.program_id(0) == 0)
        def _():
            cp = pltpu.make_async_copy(w_hbm, w_ref, sem)
            cp.start()
            cp.wait()
            dg_ref[...] = jnp.zeros_like(dg_ref)

        if transposed:
            dh = _dot(dz_ref[...], w_ref[...])
        else:
            kc = w.shape[2]
            dh = _dot_nt(dz_ref[:, 0:kc], w_ref[0])
            for j in range(1, w.shape[0]):
                dh = dh + _dot_nt(dz_ref[:, j * kc:(j + 1) * kc], w_ref[j])
        xv = x_ref[...]
        r = lax.rsqrt(jnp.mean(xv * xv, axis=-1, keepdims=True) + EPS)
        xh = xv * r
        dg_ref[...] += jnp.sum(dh * xh, axis=0, keepdims=True)
        dxh = dh * g_ref[...]
        out = r_ref[...] + r * (dxh - xh * jnp.mean(dxh * xh, axis=-1, keepdims=True))
        o_ref[...] = out
        ob_ref[...] = out.astype(BF16)

    row = lambda i: (i, 0)
    kdim = dz.shape[1]
    return pl.pallas_call(
        body, name=name, grid=(s // ts,),
        in_specs=[pl.BlockSpec((ts, kdim), row), ANY, pl.BlockSpec((ts, D), row),
                  pl.BlockSpec((1, D), lambda i: (0, 0)), pl.BlockSpec((ts, D), row)],
        out_specs=[pl.BlockSpec((ts, D), row), pl.BlockSpec((ts, D), row), pl.BlockSpec((1, D), lambda i: (0, 0))],
        out_shape=[jax.ShapeDtypeStruct((s, D), F32), jax.ShapeDtypeStruct((s, D), BF16),
                   jax.ShapeDtypeStruct((1, D), F32)],
        scratch_shapes=[pltpu.VMEM(w.shape, BF16), pltpu.SemaphoreType.DMA],
        compiler_params=_cp("arbitrary"),
    )(dz, w, x, g, resid)


def _matmul_tn(a, b, name, tn, tk, shard_major=False, tm=None):
    s, m = a.shape
    n = b.shape[1]
    tm = m if tm is None else tm
    ni, nj, nk = m // tm, n // tn, s // tk

    def body(a_ref, b_ref, o_ref):
        if nk == 1:
            o_ref[...] = _dot_tn(a_ref[...], b_ref[...])
            return

        @pl.when(pl.program_id(2) == 0)
        def _():
            o_ref[...] = jnp.zeros_like(o_ref)

        o_ref[...] += _dot_tn(a_ref[...], b_ref[...])

    if shard_major:
        out_spec = pl.BlockSpec((None, tm, tn), lambda i, j, k: (j, i, 0))
        out_shape = jax.ShapeDtypeStruct((nj, m, tn), F32)
    else:
        out_spec = pl.BlockSpec((tm, tn), lambda i, j, k: (i, j))
        out_shape = jax.ShapeDtypeStruct((m, n), F32)
    return pl.pallas_call(
        body, name=name, grid=(ni, nj, nk),
        in_specs=[pl.BlockSpec((tk, tm), lambda i, j, k: (k, i)), pl.BlockSpec((tk, tn), lambda i, j, k: (k, j))],
        out_specs=out_spec, out_shape=out_shape,
        compiler_params=_cp("arbitrary", "arbitrary", "arbitrary"),
    )(a, b)


def _pool_fwd(zr, wgrp, scale):
    s = zr.shape[0]

    def body(u_ref, w_ref, sc_ref, p_ref, pp_ref):
        row = _rows((s, 128))
        for gi, win in enumerate(POOL_WINDOWS):
            cs = slice(gi * 128, (gi + 1) * 128)
            u = u_ref[:, cs].astype(F32)
            acc, k = u, 1
            while k < win:
                acc = acc + jnp.where(row >= k, pltpu.roll(acc, k, 0), 0.0)
                k *= 2
            cnt = jnp.minimum(row + 1, win).astype(F32)
            p = (acc / cnt - u).astype(BF16)
            p_ref[:, cs] = p
            pp_ref[:, cs] = (_dot(p, w_ref[gi].astype(BF16)) * sc_ref[:, cs]).astype(BF16)

    return pl.pallas_call(
        body, name="pool_fwd", grid=(1,),
        in_specs=[pl.BlockSpec((s, POOL_W), lambda i: (0, OFF_POOL // POOL_W)),
                  pl.BlockSpec((4, 128, 128), lambda i: (0, 0, 0)), pl.BlockSpec((1, POOL_W), lambda i: (0, 0))],
        out_specs=[pl.BlockSpec((s, POOL_W), lambda i: (0, 0))] * 2,
        out_shape=[jax.ShapeDtypeStruct((s, POOL_W), BF16)] * 2,
        compiler_params=_cp("arbitrary"),
    )(zr, wgrp, scale)


def _pool_bwd(p, dpp, wgrp, scale):
    s = p.shape[0]

    def body(p_ref, dpp_ref, w_ref, sc_ref, dz_ref, dw_ref, dsc_ref):
        row = _rows((s, 128))
        for gi, win in enumerate(POOL_WINDOWS):
            cs = slice(gi * 128, (gi + 1) * 128)
            pv = p_ref[:, cs]
            wb = w_ref[gi].astype(BF16)
            dpp_v = dpp_ref[:, cs].astype(F32)
            dsc_ref[:, cs] = jnp.sum(dpp_v * _dot(pv, wb), axis=0, keepdims=True)
            dpm = (dpp_v * sc_ref[:, cs]).astype(BF16)
            dw_ref[gi] = _dot_tn(pv, dpm)
            dp = _dot_nt(dpm, wb)
            cnt = jnp.minimum(row + 1, win).astype(F32)
            acc, k = dp / cnt, 1
            while k < win:
                acc = acc + jnp.where(row < s - k, pltpu.roll(acc, s - k, 0), 0.0)
                k *= 2
            dz_ref[:, cs] = (acc - dp).astype(BF16)

    full = lambda i: (0, 0)
    return pl.pallas_call(
        body, name="pool_bwd", grid=(1,),
        in_specs=[pl.BlockSpec((s, POOL_W), full), pl.BlockSpec((s, POOL_W), full),
                  pl.BlockSpec((4, 128, 128), lambda i: (0, 0, 0)), pl.BlockSpec((1, POOL_W), full)],
        out_specs=[pl.BlockSpec((s, POOL_W), full), pl.BlockSpec((4, 128, 128), lambda i: (0, 0, 0)),
                   pl.BlockSpec((1, POOL_W), full)],
        out_shape=[jax.ShapeDtypeStruct((s, POOL_W), BF16), jax.ShapeDtypeStruct((4, 128, 128), F32),
                   jax.ShapeDtypeStruct((1, POOL_W), F32)],
        compiler_params=_cp("arbitrary"),
    )(p, dpp, wgrp, scale)


def _gla_decay(zgk_ref, wgk_ref, bgk_ref, rb):
    g = _dot(zgk_ref[...], wgk_ref[...].astype(BF16)) + bgk_ref[...]
    la = (jnp.minimum(g, 0.0) - jnp.log(1.0 + jnp.exp(-jnp.abs(g)))) * (1.0 / 16.0)
    rowm = _rows((rb, HK)) & (CHUNK - 1)
    bc, k = la, 1
    while k < CHUNK:
        bc = bc + jnp.where(rowm >= k, pltpu.roll(bc, k, 0), 0.0)
        k *= 2
    return g, jnp.exp(bc), jnp.exp(-bc)


def _gla_specs(rb, rmap):
    return [pl.BlockSpec((rb, HK), lambda h, r: (rmap(h, r), OFF_Q // HK + h)),
            pl.BlockSpec((rb, HK), lambda h, r: (rmap(h, r), OFF_K // HK + h)),
            pl.BlockSpec((rb, HV), lambda h, r: (rmap(h, r), OFF_V // HV + h)),
            pl.BlockSpec((rb, 128), lambda h, r: (rmap(h, r), OFF_GK // 128))]


def _gla_fwd(zr, wgk, bgk, ghead, rb):
    s = zr.shape[0]
    nc = rb // CHUNK

    def body(q_ref, k_ref, v_ref, zgk_ref, zog_ref, wgk_ref, bgk_ref, gh_ref, o_ref, og_ref, sp_ref, st_ref):
        @pl.when(pl.program_id(1) == 0)
        def _():
            st_ref[...] = jnp.zeros_like(st_ref)

        _, e_pos, e_neg = _gla_decay(zgk_ref, wgk_ref, bgk_ref, rb)
        lower = _rows((CHUNK, CHUNK)) >= lax.broadcasted_iota(jnp.int32, (CHUNK, CHUNK), 1)
        for c in range(nc):
            sl = slice(c * CHUNK, (c + 1) * CHUNK)
            q = q_ref[sl, :].astype(F32) * QSCALE
            k = k_ref[sl, :].astype(F32)
            v = v_ref[sl, :]
            ec, fc = e_pos[sl], e_neg[sl]
            qfw = (q * ec).astype(BF16)
            kfw_f = k * fc
            s_fw = _dot_nt(qfw, kfw_f.astype(BF16))
            s_bw = _dot_nt((q * fc).astype(BF16), (k * ec).astype(BF16))
            pm = jnp.where(lower, s_fw, s_bw).astype(BF16)
            st = st_ref[...]
            stb = st.astype(BF16)
            sp_ref[c] = stb
            o = _dot(pm, v) + _dot_nt(qfw, stb)
            e_last = _pick_row(ec, CHUNK - 1)
            kdec = (kfw_f * e_last).astype(BF16)
            st_ref[...] = st * e_last + _dot_tn(v, kdec)
            r = lax.rsqrt(jnp.mean(o * o, axis=-1, keepdims=True) + EPS)
            zo = zog_ref[sl, :].astype(F32)
            o_ref[sl, :] = o.astype(BF16)
            og_ref[sl, :] = (o * r * gh_ref[...] * zo * _sigmoid(zo)).astype(BF16)

    rmap = lambda h, r: r
    return pl.pallas_call(
        body, name="gla_fwd", grid=(HEADS, s // rb),
        in_specs=_gla_specs(rb, rmap) + [
            pl.BlockSpec((rb, HV), lambda h, r: (r, OFF_OG // HV + h)),
            pl.BlockSpec((128, HK), lambda h, r: (0, h)), pl.BlockSpec((1, HK), lambda h, r: (0, h)),
            pl.BlockSpec((1, HV), lambda h, r: (0, 0))],
        out_specs=[pl.BlockSpec((rb, HV), lambda h, r: (r, h)), pl.BlockSpec((rb, HV), lambda h, r: (r, h)),
                   pl.BlockSpec((nc, None, HV, HK), lambda h, r: (r, h, 0, 0))],
        out_shape=[jax.ShapeDtypeStruct((s, D), BF16), jax.ShapeDtypeStruct((s, D), BF16),
                   jax.ShapeDtypeStruct((s // CHUNK, HEADS, HV, HK), BF16)],
        scratch_shapes=[pltpu.VMEM((HV, HK), F32)],
        compiler_params=_cp("arbitrary", "arbitrary"),
    )(zr, zr, zr, zr, zr, wgk, bgk, ghead)


def _gla_bwd(zr, do, sp, wgk, bgk, rb):
    s = zr.shape[0]
    nc = rb // CHUNK
    nr = s // rb

    def body(q_ref, k_ref, v_ref, zgk_ref, do_ref, sp_ref, wgk_ref, bgk_ref, dq_ref, dk_ref, dv_ref, dg_ref,
             gt_ref, dbc_ref):
        @pl.when(pl.program_id(1) == 0)
        def _():
            gt_ref[...] = jnp.zeros_like(gt_ref)

        g, e_pos, e_neg = _gla_decay(zgk_ref, wgk_ref, bgk_ref, rb)
        lower = _rows((CHUNK, CHUNK)) >= lax.broadcasted_iota(jnp.int32, (CHUNK, CHUNK), 1)
        is_last = _rows((CHUNK, HK)) == CHUNK - 1
        for c in reversed(range(nc)):
            sl = slice(c * CHUNK, (c + 1) * CHUNK)
            q = q_ref[sl, :].astype(F32) * QSCALE
            k = k_ref[sl, :].astype(F32)
            v = v_ref[sl, :]
            dov = do_ref[sl, :]
            ec, fc = e_pos[sl], e_neg[sl]
            qfw_f, kfw_f, qbw_f, kbw_f = q * ec, k * fc, q * fc, k * ec
            qfw, kfw, qbw, kbw = qfw_f.astype(BF16), kfw_f.astype(BF16), qbw_f.astype(BF16), kbw_f.astype(BF16)
            pm = jnp.where(lower, _dot_nt(qfw, kfw), _dot_nt(qbw, kbw)).astype(BF16)
            e_last = _pick_row(ec, CHUNK - 1)
            kdec = (kfw_f * e_last).astype(BF16)
            gt = gt_ref[...]
            gtb = gt.astype(BF16)
            spv = sp_ref[c]
            dp = _dot_nt(dov, v)
            dv_ref[sl, :] = (_dot_tn(pm, dov) + _dot_nt(kdec, gtb)).astype(BF16)
            ds_fw = jnp.where(lower, dp, 0.0).astype(BF16)
            ds_bw = jnp.where(lower, 0.0, dp).astype(BF16)
            dqfw = _dot(ds_fw, kfw) + _dot(dov, spv)
            dkfw = _dot_tn(ds_fw, qfw)
            dqbw = _dot(ds_bw, kbw)
            dkbw = _dot_tn(ds_bw, qbw)
            dkdec = _dot(v, gtb)
            de_last = (jnp.sum(gt * spv.astype(F32), axis=0, keepdims=True)
                       + jnp.sum(dkdec * kfw_f, axis=0, keepdims=True))
            dkfw = dkfw + dkdec * e_last
            dq_ref[sl, :] = ((dqfw * ec + dqbw * fc) * QSCALE).astype(BF16)
            dk_ref[sl, :] = (dkfw * fc + dkbw * ec).astype(BF16)
            dbc = dqfw * qfw_f - dqbw * qbw_f + dkbw * kbw_f - dkfw * kfw_f
            dbc_ref[sl, :] = dbc + jnp.where(is_last, de_last * e_last, 0.0)
            gt_ref[...] = _dot_tn(dov, qfw) + gt * e_last
        rowm = _rows((rb, HK)) & (CHUNK - 1)
        dla, kk = dbc_ref[...], 1
        while kk < CHUNK:
            dla = dla + jnp.where(rowm < CHUNK - kk, pltpu.roll(dla, rb - kk, 0), 0.0)
            kk *= 2
        dg_ref[...] = dla * (1.0 / 16.0) * _sigmoid(-g)

    rmap = lambda h, r: nr - 1 - r
    return pl.pallas_call(
        body, name="gla_bwd", grid=(HEADS, nr),
        in_specs=_gla_specs(rb, rmap) + [
            pl.BlockSpec((rb, HV), lambda h, r: (nr - 1 - r, h)),
            pl.BlockSpec((nc, None, HV, HK), lambda h, r: (nr - 1 - r, h, 0, 0)),
            pl.BlockSpec((128, HK), lambda h, r: (0, h)), pl.BlockSpec((1, HK), lambda h, r: (0, h))],
        out_specs=[pl.BlockSpec((rb, HK), lambda h, r: (nr - 1 - r, h)), pl.BlockSpec((rb, HK), lambda h, r: (nr - 1 - r, h)),
                   pl.BlockSpec((rb, HV), lambda h, r: (nr - 1 - r, h)), pl.BlockSpec((rb, HK), lambda h, r: (nr - 1 - r, h))],
        out_shape=[jax.ShapeDtypeStruct((s, HEADS * HK), BF16), jax.ShapeDtypeStruct((s, HEADS * HK), BF16),
                   jax.ShapeDtypeStruct((s, D), BF16), jax.ShapeDtypeStruct((s, HEADS * HK), F32)],
        scratch_shapes=[pltpu.VMEM((HV, HK), F32), pltpu.VMEM((rb, HK), F32)],
        compiler_params=_cp("arbitrary", "arbitrary"),
    )(zr, zr, zr, zr, do, sp, wgk, bgk)


def _gk_bwd(dgpre, zr, wgk, ts):
    s = zr.shape[0]

    def body(dg_ref, zgk_ref, w_ref, dz_ref, dw_ref, db_ref):
        @pl.when(pl.program_id(0) == 0)
        def _():
            dw_ref[...] = jnp.zeros_like(dw_ref)
            db_ref[...] = jnp.zeros_like(db_ref)

        dg = dg_ref[...]
        dgb = dg.astype(BF16)
        dz_ref[...] = _dot_nt(dgb, w_ref[...].astype(BF16)).astype(BF16)
        dw_ref[...] += _dot_tn(zgk_ref[...], dgb)
        db_ref[...] += jnp.sum(dg, axis=0, keepdims=True)

    return pl.pallas_call(
        body, name="gk_bwd", grid=(s // ts,),
        in_specs=[pl.BlockSpec((ts, 512), lambda i: (i, 0)), pl.BlockSpec((ts, 128), lambda i: (i, OFF_GK // 128)),
                  pl.BlockSpec((128, 512), lambda i: (0, 0))],
        out_specs=[pl.BlockSpec((ts, 128), lambda i: (i, 0)), pl.BlockSpec((128, 512), lambda i: (0, 0)),
                   pl.BlockSpec((1, 512), lambda i: (0, 0))],
        out_shape=[jax.ShapeDtypeStruct((s, 128), BF16), jax.ShapeDtypeStruct((128, 512), F32),
                   jax.ShapeDtypeStruct((1, 512), F32)],
        compiler_params=_cp("arbitrary"),
    )(dgpre, zr, wgk)


def _merge_fwd(x, zr, pp, og, bgate, wpp, wgla, wout, ts):
    s = x.shape[0]

    def body(x_ref, z0_ref, z1_ref, pp_ref, og_ref, bg_ref, wpp_ref, wgla_ref, wout_ref,
             x1_ref, mix_ref, yp_ref, yg_ref):
        ppv = pp_ref[...]
        yp = jnp.concatenate([_dot(ppv, wpp_ref[j]) for j in range(4)], axis=1)
        yg = _dot(og_ref[...], wgla_ref[...])
        g0 = _sigmoid(z0_ref[...].astype(F32) + bg_ref[:, :D])
        g1 = _sigmoid(z1_ref[...].astype(F32) + bg_ref[:, D:])
        mixed = (g0 * yp + g1 * yg).astype(BF16)
        x1_ref[...] = x_ref[...] + _dot(mixed, wout_ref[...])
        mix_ref[...] = mixed
        yp_ref[...] = yp.astype(BF16)
        yg_ref[...] = yg.astype(BF16)

    row = lambda i: (i, 0)
    const2 = lambda i: (0, 0)
    return pl.pallas_call(
        body, name="merge_fwd", grid=(s // ts,),
        in_specs=[pl.BlockSpec((ts, D), row), pl.BlockSpec((ts, D), lambda i: (i, 0)), pl.BlockSpec((ts, D), lambda i: (i, 1)),
                  pl.BlockSpec((ts, POOL_W), row), pl.BlockSpec((ts, D), row), pl.BlockSpec((1, 2 * D), const2),
                  pl.BlockSpec((4, POOL_W, 256), lambda i: (0, 0, 0)), pl.BlockSpec((D, D), const2),
                  pl.BlockSpec((D, D), const2)],
        out_specs=[pl.BlockSpec((ts, D), row)] * 4,
        out_shape=[jax.ShapeDtypeStruct((s, D), F32)] + [jax.ShapeDtypeStruct((s, D), BF16)] * 3,
        compiler_params=_cp("arbitrary"),
    )(x, zr, zr, pp, og, bgate, wpp, wgla, wout)


def _merge_bwd(dx1b, zr, yp, yg, o, bgate, ghead, wpp, wgla, wout, ts):
    s = dx1b.shape[0]

    def body(dx_ref, z0_ref, z1_ref, zog_ref, yp_ref, yg_ref, o_ref, bg_ref, gh_ref, wpp_ref, wgla_ref, wout_ref,
             dzg_ref, dyp_ref, dyg_ref, dpp_ref, do_ref, dzog_ref, dbg_ref, dgh_ref):
        @pl.when(pl.program_id(0) == 0)
        def _():
            dbg_ref[...] = jnp.zeros_like(dbg_ref)
            dgh_ref[...] = jnp.zeros_like(dgh_ref)

        dmix = _dot_nt(dx_ref[...], wout_ref[...])
        g0 = _sigmoid(z0_ref[...].astype(F32) + bg_ref[:, :D])
        g1 = _sigmoid(z1_ref[...].astype(F32) + bg_ref[:, D:])
        dypb = (dmix * g0).astype(BF16)
        dygb = (dmix * g1).astype(BF16)
        dz0 = dmix * yp_ref[...].astype(F32) * g0 * (1.0 - g0)
        dz1 = dmix * yg_ref[...].astype(F32) * g1 * (1.0 - g1)
        dzg_ref[:, :D] = dz0.astype(BF16)
        dzg_ref[:, D:] = dz1.astype(BF16)
        dbg_ref[:, :D] += jnp.sum(dz0, axis=0, keepdims=True)
        dbg_ref[:, D:] += jnp.sum(dz1, axis=0, keepdims=True)
        dyp_ref[...] = dypb
        dyg_ref[...] = dygb
        dpp = _dot_nt(dypb[:, 0:256], wpp_ref[0])
        for j in range(1, 4):
            dpp = dpp + _dot_nt(dypb[:, j * 256:(j + 1) * 256], wpp_ref[j])
        dpp_ref[...] = dpp.astype(BF16)
        dog = _dot_nt(dygb, wgla_ref[...])
        gh = gh_ref[...]
        dgh = jnp.zeros((1, HV), F32)
        for h in range(HEADS):
            cs = slice(h * HV, (h + 1) * HV)
            ov = o_ref[:, cs].astype(F32)
            r = lax.rsqrt(jnp.mean(ov * ov, axis=-1, keepdims=True) + EPS)
            oh = ov * r
            zo = zog_ref[:, cs].astype(F32)
            sg = _sigmoid(zo)
            dog_h = dog[:, cs]
            don = dog_h * zo * sg
            dzog_ref[:, cs] = (dog_h * oh * gh * sg * (1.0 + zo * (1.0 - sg))).astype(BF16)
            dgh = dgh + jnp.sum(don * oh, axis=0, keepdims=True)
            doh = don * gh
            do_ref[:, cs] = (r * (doh - oh * jnp.mean(doh * oh, axis=-1, keepdims=True))).astype(BF16)
        dgh_ref[...] += dgh

    row = lambda i: (i, 0)
    const2 = lambda i: (0, 0)
    return pl.pallas_call(
        body, name="merge_bwd", grid=(s // ts,),
        in_specs=[pl.BlockSpec((ts, D), row), pl.BlockSpec((ts, D), lambda i: (i, 0)), pl.BlockSpec((ts, D), lambda i: (i, 1)),
                  pl.BlockSpec((ts, D), lambda i: (i, OFF_OG // D)), pl.BlockSpec((ts, D), row), pl.BlockSpec((ts, D), row),
                  pl.BlockSpec((ts, D), row), pl.BlockSpec((1, 2 * D), const2), pl.BlockSpec((1, HV), const2),
                  pl.BlockSpec((4, POOL_W, 256), lambda i: (0, 0, 0)), pl.BlockSpec((D, D), const2),
                  pl.BlockSpec((D, D), const2)],
        out_specs=[pl.BlockSpec((ts, 2 * D), row), pl.BlockSpec((ts, D), row), pl.BlockSpec((ts, D), row),
                   pl.BlockSpec((ts, POOL_W), row), pl.BlockSpec((ts, D), row), pl.BlockSpec((ts, D), row),
                   pl.BlockSpec((1, 2 * D), const2), pl.BlockSpec((1, HV), const2)],
        out_shape=[jax.ShapeDtypeStruct((s, 2 * D), BF16), jax.ShapeDtypeStruct((s, D), BF16),
                   jax.ShapeDtypeStruct((s, D), BF16), jax.ShapeDtypeStruct((s, POOL_W), BF16),
                   jax.ShapeDtypeStruct((s, D), BF16), jax.ShapeDtypeStruct((s, D), BF16),
                   jax.ShapeDtypeStruct((1, 2 * D), F32), jax.ShapeDtypeStruct((1, HV), F32)],
        compiler_params=_cp("arbitrary"),
    )(dx1b, zr, zr, zr, yp, yg, o, bgate, ghead, wpp, wgla, wout)


HALO = 16
CCH = 1408


def _conv_taps(u_ref, halo_ref, cs, first, ts):
    u = u_ref[:, cs].astype(F32)
    hal = halo_ref[:, cs].astype(F32)
    h1 = jnp.where(first, 0.0, _pick_row(hal, HALO - 1))
    h2 = jnp.where(first, 0.0, _pick_row(hal, HALO - 2))
    row = _rows(u.shape)
    r1 = jnp.where(row == 0, h1, pltpu.roll(u, 1, 0))
    r2 = jnp.where(row == 0, h2, jnp.where(row == 1, h1, pltpu.roll(u, 2, 0)))
    return u, r1, r2


def _ffn_down_loss(u, x1, tgt, wconv, bconv, wdown, gfin, ts):
    s = x1.shape[0]

    def body(u_ref, halo_ref, x1_ref, t_ref, wc_ref, bc_ref, wd_ref, gf_ref, a_ref, dx_ref, dxb_ref, ls_ref, dgf_ref):
        i = pl.program_id(0)

        @pl.when(i == 0)
        def _():
            ls_ref[...] = jnp.zeros_like(ls_ref)
            dgf_ref[...] = jnp.zeros_like(dgf_ref)

        first = i == 0
        acc = x1_ref[...]
        for hf in range(2):
            cg = slice(hf * CCH, (hf + 1) * CCH)
            cv = slice(D_FF + hf * CCH, D_FF + (hf + 1) * CCH)
            vals = []
            for cs in (cg, cv):
                u0, u1, u2 = _conv_taps(u_ref, halo_ref, cs, first, ts)
                vals.append(bc_ref[:, cs] + wc_ref[0:1, cs] * u2 + wc_ref[1:2, cs] * u1 + wc_ref[2:3, cs] * u0)
            a = (vals[0] * _sigmoid(vals[0]) * vals[1]).astype(BF16)
            a_ref[:, cg] = a
            acc = acc + _dot(a, wd_ref[cg, :])
        r = lax.rsqrt(jnp.mean(acc * acc, axis=-1, keepdims=True) + EPS)
        xh = acc * r
        gf = gf_ref[...]
        err = xh * gf - t_ref[...]
        ls_ref[...] += (0.5 / D) * jnp.sum(jnp.sum(err * err, axis=-1, keepdims=True), axis=0, keepdims=True)
        dy = err * (1.0 / D)
        dgf_ref[...] += jnp.sum(dy * xh, axis=0, keepdims=True)
        dxh = dy * gf
        dx = r * (dxh - xh * jnp.mean(dxh * xh, axis=-1, keepdims=True))
        dx_ref[...] = dx
        dxb_ref[...] = dx.astype(BF16)

    row = lambda i: (i, 0)
    const2 = lambda i: (0, 0)
    return pl.pallas_call(
        body, name="ffn_down_loss", grid=(s // ts,),
        in_specs=[pl.BlockSpec((ts, N_UP), row),
                  pl.BlockSpec((HALO, N_UP), lambda i: (jnp.maximum(i * (ts // HALO) - 1, 0), 0)),
                  pl.BlockSpec((ts, D), row), pl.BlockSpec((ts, D), row), pl.BlockSpec((3, N_UP), const2),
                  pl.BlockSpec((1, N_UP), const2), pl.BlockSpec((D_FF, D), const2), pl.BlockSpec((1, D), const2)],
        out_specs=[pl.BlockSpec((ts, D_FF), row), pl.BlockSpec((ts, D), row), pl.BlockSpec((ts, D), row),
                   pl.BlockSpec((1, 128), const2), pl.BlockSpec((1, D), const2)],
        out_shape=[jax.ShapeDtypeStruct((s, D_FF), BF16), jax.ShapeDtypeStruct((s, D), F32),
                   jax.ShapeDtypeStruct((s, D), BF16), jax.ShapeDtypeStruct((1, 128), F32),
                   jax.ShapeDtypeStruct((1, D), F32)],
        compiler_params=_cp("arbitrary"),
    )(u, u, x1, tgt, wconv, bconv, wdown, gfin)


def _ffn_bwd(dx2b, u, wconv, bconv, wdown, ts):
    s = dx2b.shape[0]
    nt = s // ts

    def body(dx_ref, u_ref, halo_ref, wc_ref, bc_ref, wd_ref, du_ref, db_ref, dw_ref, nxt_ref):
        i = pl.program_id(0)

        @pl.when(i == 0)
        def _():
            db_ref[...] = jnp.zeros_like(db_ref)
            dw_ref[...] = jnp.zeros_like(dw_ref)
            nxt_ref[...] = jnp.zeros_like(nxt_ref)

        first = i == nt - 1
        dxv = dx_ref[...]
        row = _rows((ts, CCH))
        for hf in range(2):
            cg = slice(hf * CCH, (hf + 1) * CCH)
            cv = slice(D_FF + hf * CCH, D_FF + (hf + 1) * CCH)
            da = _dot_nt(dxv, wd_ref[cg, :])
            taps, vals = [], []
            for cs in (cg, cv):
                t3 = _conv_taps(u_ref, halo_ref, cs, first, ts)
                taps.append(t3)
                vals.append(bc_ref[:, cs] + wc_ref[0:1, cs] * t3[2] + wc_ref[1:2, cs] * t3[1] + wc_ref[2:3, cs] * t3[0])
            sg = _sigmoid(vals[0])
            dcs = (da * vals[1] * sg * (1.0 + vals[0] * (1.0 - sg)), da * vals[0] * sg)
            for cs, (u0, u1, u2), dc in zip((cg, cv), taps, dcs):
                db_ref[:, cs] += jnp.sum(dc, axis=0, keepdims=True)
                dw_ref[0:1, cs] += jnp.sum(dc * u2, axis=0, keepdims=True)
                dw_ref[1:2, cs] += jnp.sum(dc * u1, axis=0, keepdims=True)
                dw_ref[2:3, cs] += jnp.sum(dc * u0, axis=0, keepdims=True)
                n1 = nxt_ref[0:1, cs]
                n2 = nxt_ref[1:2, cs]
                f1 = jnp.where(row == ts - 1, n1, pltpu.roll(dc, ts - 1, 0))
                f2 = jnp.where(row == ts - 1, n2, jnp.where(row == ts - 2, n1, pltpu.roll(dc, ts - 2, 0)))
                du_ref[:, cs] = (wc_ref[2:3, cs] * dc + wc_ref[1:2, cs] * f1 + wc_ref[0:1, cs] * f2).astype(BF16)
                nxt_ref[:, cs] = dc[0:8, :]

    rev = lambda i: (nt - 1 - i, 0)
    const2 = lambda i: (0, 0)
    return pl.pallas_call(
        body, name="ffn_bwd", grid=(nt,),
        in_specs=[pl.BlockSpec((ts, D), rev), pl.BlockSpec((ts, N_UP), rev),
                  pl.BlockSpec((HALO, N_UP), lambda i: (jnp.maximum((nt - 1 - i) * (ts // HALO) - 1, 0), 0)),
                  pl.BlockSpec((3, N_UP), const2), pl.BlockSpec((1, N_UP), const2), pl.BlockSpec((D_FF, D), const2)],
        out_specs=[pl.BlockSpec((ts, N_UP), rev), pl.BlockSpec((1, N_UP), const2), pl.BlockSpec((3, N_UP), const2)],
        out_shape=[jax.ShapeDtypeStruct((s, N_UP), BF16), jax.ShapeDtypeStruct((1, N_UP), F32),
                   jax.ShapeDtypeStruct((3, N_UP), F32)],
        scratch_shapes=[pltpu.VMEM((8, N_UP), F32)],
        compiler_params=_cp("arbitrary"),
    )(dx2b, u, u, wconv, bconv, wdown)


ANY = pl.BlockSpec(memory_space=pl.ANY)


def _place():
    x, y, c = lax.axis_index("x"), lax.axis_index("y"), lax.axis_index("c")
    chips = [(1 - x, y), (x, 1 - y), (1 - x, 1 - y)]
    return x, y, c, chips


def _half(shape, c, axis):
    size = shape[axis] // 2
    cut = pl.ds(pl.multiple_of(c * size, 8 if axis == 0 else 128), size)
    return (cut, slice(None)) if axis == 0 else (slice(None), cut)


def _half_shape(shape, axis):
    return (shape[0] // 2, shape[1]) if axis == 0 else (shape[0], shape[1] // 2)


def _remote(src, dst, send_sems, recv_sems, k, to):
    return pltpu.make_async_remote_copy(src_ref=src, dst_ref=dst, send_sem=send_sems.at[k], recv_sem=recv_sems.at[k],
                                        device_id=to, device_id_type=MESH)


def _all_gather_weights(big, axes, small):
    nb, ns = len(big), len(small)
    n = nb + ns
    n_sem = 6 * nb + 3 * ns

    def body(*refs):
        ins, outs = refs[:n], refs[n:2 * n]
        send_sems, recv_sems = refs[2 * n:]
        x, y, c, chips = _place()
        me = 2 * x + y
        sib = (x, y, 1 - c)
        started = []
        for a in range(nb):
            mine = _half(big[a].shape, c, axes[a])
            for k, ch in enumerate(chips):
                cp = _remote(ins[a].at[mine], outs[a].at[(me,) + mine], send_sems, recv_sems, 6 * a + k,
                             (ch[0], ch[1], c))
                cp.start()
                started.append(cp)
        for a in range(ns):
            for k, ch in enumerate(chips):
                cp = _remote(ins[nb + a], outs[nb + a].at[me], send_sems, recv_sems, 6 * nb + 3 * a + k,
                             (ch[0], ch[1], c))
                cp.start()
                started.append(cp)
        for a in range(nb):
            mine = _half(big[a].shape, c, axes[a])
            for k, ch in enumerate(chips):
                landed = outs[a].at[(2 * ch[0] + ch[1],) + mine]
                _remote(landed, landed, send_sems, recv_sems, 6 * a + k, sib).wait_recv()
                cp = _remote(landed, landed, send_sems, recv_sems, 6 * a + 3 + k, sib)
                cp.start()
                started.append(cp)
        for a in range(nb):
            other = _half(big[a].shape, 1 - c, axes[a])
            for k, ch in enumerate(chips):
                landed = outs[a].at[(2 * ch[0] + ch[1],) + other]
                _remote(landed, landed, send_sems, recv_sems, 6 * a + 3 + k, sib).wait_recv()
        for a in range(ns):
            for k, ch in enumerate(chips):
                landed = outs[nb + a].at[2 * ch[0] + ch[1]]
                _remote(landed, landed, send_sems, recv_sems, 6 * nb + 3 * a + k, sib).wait_recv()
        for cp in started:
            cp.wait_send()

    arrs = list(big) + list(small)
    return pl.pallas_call(
        body, name="all_gather_weights",
        in_specs=[ANY] * n, out_specs=[ANY] * n,
        out_shape=[jax.ShapeDtypeStruct((4,) + a.shape, a.dtype) for a in arrs],
        scratch_shapes=[pltpu.SemaphoreType.DMA((n_sem,)), pltpu.SemaphoreType.DMA((n_sem,))],
        compiler_params=pltpu.CompilerParams(has_side_effects=True),
    )(*arrs)


def _sibling_exchange(grads, axes, small, name):
    nb = len(grads)
    n = nb + (small is not None)

    def body(*refs):
        ins, outs = refs[:n], refs[n:2 * n]
        send_sems, recv_sems = refs[2 * n:]
        x, y, c, _ = _place()
        sib = (x, y, 1 - c)
        cps = []
        for a in range(nb):
            theirs = _half(grads[a].shape[1:], 1 - c, axes[a])
            cps.append(_remote(ins[a].at[(slice(None),) + theirs], outs[a], send_sems, recv_sems, a, sib))
        if small is not None:
            cps.append(_remote(ins[nb], outs[nb], send_sems, recv_sems, nb, sib))
        for cp in cps:
            cp.start()
        for cp in cps:
            cp.wait()

    out_shape = [jax.ShapeDtypeStruct((4,) + _half_shape(g.shape[1:], ax), F32) for g, ax in zip(grads, axes)]
    if small is not None:
        out_shape.append(jax.ShapeDtypeStruct(small.shape, F32))
    return pl.pallas_call(
        body, name=name, in_specs=[ANY] * n, out_specs=[ANY] * n, out_shape=out_shape,
        scratch_shapes=[pltpu.SemaphoreType.DMA((n,)), pltpu.SemaphoreType.DMA((n,))],
        compiler_params=pltpu.CompilerParams(has_side_effects=True),
    )(*grads, *([] if small is None else [small]))


def _gather_share(lands, axes, name):
    n = len(lands)

    def body(*refs):
        outs = refs[n:2 * n]
        send_sems, recv_sems = refs[2 * n:]
        x, y, c, chips = _place()
        sib = (x, y, 1 - c)
        cps = []
        for a in range(n):
            mine = _half(lands[a].shape[1:], c, axes[a])
            for k, ch in enumerate(chips):
                landed = outs[a].at[(2 * ch[0] + ch[1],) + mine]
                cps.append(_remote(landed, landed, send_sems, recv_sems, 3 * a + k, sib))
        for cp in cps:
            cp.start()
        for a in range(n):
            other = _half(lands[a].shape[1:], 1 - c, axes[a])
            for k, ch in enumerate(chips):
                landed = outs[a].at[(2 * ch[0] + ch[1],) + other]
                _remote(landed, landed, send_sems, recv_sems, 3 * a + k, sib).wait_recv()
        for cp in cps:
            cp.wait_send()

    return pl.pallas_call(
        body, name=name, in_specs=[ANY] * n, out_specs=[ANY] * n,
        out_shape=[jax.ShapeDtypeStruct(a.shape, a.dtype) for a in lands],
        input_output_aliases={a: a for a in range(n)},
        scratch_shapes=[pltpu.SemaphoreType.DMA((3 * n,)), pltpu.SemaphoreType.DMA((3 * n,))],
        compiler_params=pltpu.CompilerParams(has_side_effects=True),
    )(*lands)


def _sibling_share(halves, name):
    n = len(halves)

    def body(*refs):
        ins, outs = refs[:n], refs[n:2 * n]
        send_sems, recv_sems = refs[2 * n:]
        x, y, c, _ = _place()
        cps = [_remote(ins[a], outs[a], send_sems, recv_sems, a, (x, y, 1 - c)) for a in range(n)]
        for cp in cps:
            cp.start()
        for cp in cps:
            cp.wait()

    return pl.pallas_call(
        body, name=name, in_specs=[ANY] * n, out_specs=[ANY] * n,
        out_shape=[jax.ShapeDtypeStruct(h.shape, F32) for h in halves],
        scratch_shapes=[pltpu.SemaphoreType.DMA((n,)), pltpu.SemaphoreType.DMA((n,))],
        compiler_params=pltpu.CompilerParams(has_side_effects=True),
    )(*halves)


HBM = pl.BlockSpec(memory_space=pltpu.HBM)
SEM = pl.BlockSpec(memory_space=pltpu.SEMAPHORE)
DATAFLOW = pltpu.SideEffectType.DATAFLOW_SIDE_EFFECTING


def _split_start(name, srcs, land_shapes, plan, n_copies, after):
    lands = [lax.empty(shp, dt) for shp, dt in land_shapes]
    bufs = list(srcs) + lands
    nb, ns = len(bufs), len(srcs)

    def body(*refs):
        send_sems, recv_sems, token = refs[nb + 1], refs[nb + 2], refs[-1]
        for k, (src, dst, to) in enumerate(plan(refs[:ns], refs[ns:nb])):
            _remote(src, dst, send_sems, recv_sems, k, to).start()
        token[...] = jnp.zeros_like(token)

    res = pl.pallas_call(
        body, name=name,
        out_shape=(pltpu.SemaphoreType.DMA((n_copies,)), pltpu.SemaphoreType.DMA((n_copies,)),
                   *[pltpu.HBM(b.shape, b.dtype) for b in bufs], jax.ShapeDtypeStruct((8, 128), F32)),
        in_specs=[HBM] * nb + [ANY],
        out_specs=(SEM, SEM, *[HBM] * nb, pl.BlockSpec(memory_space=pltpu.VMEM)),
        input_output_aliases={i: 2 + i for i in range(nb)},
        compiler_params=pltpu.CompilerParams(has_side_effects=DATAFLOW),
    )(*[pltpu.with_memory_space_constraint(b, pltpu.HBM) for b in bufs], after)
    return (res[0], res[1], list(res[2:2 + nb])), res[-1]


def _split_wait(name, handle, n_srcs, plan, after):
    send_sems, recv_sems, bufs = handle
    nb = len(bufs)

    def body(*refs):
        sends, recvs = refs[nb], refs[nb + 1]
        for k, (src, dst, to) in enumerate(plan(refs[:n_srcs], refs[n_srcs:nb])):
            cp = _remote(src, dst, sends, recvs, k, to)
            cp.wait_send()
            cp.wait_recv()

    res = pl.pallas_call(
        body, name=name, out_shape=[pltpu.HBM(b.shape, b.dtype) for b in bufs],
        in_specs=[HBM] * nb + [SEM, SEM, ANY], out_specs=[HBM] * nb,
        input_output_aliases={i: i for i in range(nb)},
        compiler_params=pltpu.CompilerParams(has_side_effects=DATAFLOW),
    )(*bufs, send_sems, recv_sems, after)
    return list(res[:n_srcs]), list(res[n_srcs:])


def _gather_plan(shapes, axes):
    def plan(srcs, lands):
        x, y, c, chips = _place()
        out = []
        for a, (shape, axis) in enumerate(zip(shapes, axes)):
            mine = _half(shape, c, axis)
            for ch in chips:
                out.append((srcs[a].at[mine], lands[a].at[(2 * x + y,) + mine], (ch[0], ch[1], c)))
        return out
    return plan


def _sibling_plan(shapes, axes):
    def plan(srcs, lands):
        x, y, c, _ = _place()
        return [(srcs[a].at[(slice(None),) + _half(shape, 1 - c, axis)], lands[a], (x, y, 1 - c))
                for a, (shape, axis) in enumerate(zip(shapes, axes))]
    return plan


def _reduce_plan(n_big, with_small):
    def plan(srcs, lands):
        x, y, c, chips = _place()
        out = []
        for a in range(n_big):
            for k, ch in enumerate(chips):
                out.append((srcs[a].at[2 * ch[0] + ch[1]], lands[a].at[k], (ch[0], ch[1], c)))
        if with_small:
            for ch in chips:
                out.append((srcs[n_big], lands[n_big].at[2 * x + y], (ch[0], ch[1], c)))
        return out
    return plan


def _row_tile(rows, cols, mult):
    best = mult
    for t in range(mult, rows + 1, mult):
        if rows % t == 0 and t * cols * 4 <= (1 << 20):
            best = t
    return best if rows % best == 0 else rows


COL_TILE = 128


def _half_tiling(hshape, axis, mult):
    hr, hc = hshape
    if axis == 0:
        tr = _row_tile(hr, hc, mult)
        return tr, hc, hr // tr
    return hr, COL_TILE, hc // COL_TILE


def _tile_idx(axis, t):
    return (t, 0) if axis == 0 else (0, t)


def _chip_partial(place, g, t, axis, name):
    hshape = t.shape[1:]
    br, bc, nt = _half_tiling(hshape, axis, 16)

    def body(pl_ref, g_ref, t_ref, pf_ref, pb_ref):
        v = g_ref[...] + t_ref[...]
        pf_ref[...] = v
        pb_ref[...] = v.astype(BF16)

    blk = (None, br, bc)
    return pl.pallas_call(
        body, name=name,
        grid_spec=pltpu.PrefetchScalarGridSpec(
            num_scalar_prefetch=1, grid=(4, nt),
            in_specs=[pl.BlockSpec(blk, lambda j, i, p: (j,) + _tile_idx(axis, p[1] * nt + i)),
                      pl.BlockSpec(blk, lambda j, i, p: (j,) + _tile_idx(axis, i))],
            out_specs=[pl.BlockSpec(blk, lambda j, i, p: (j,) + _tile_idx(axis, i))] * 2),
        out_shape=[jax.ShapeDtypeStruct((4,) + hshape, F32), jax.ShapeDtypeStruct((4,) + hshape, BF16)],
        compiler_params=_cp("arbitrary", "arbitrary"),
    )(place, g, t)


def _finish_half(place, pf, rb, axis, name):
    hshape = pf.shape[1:]
    br, bc, nt = _half_tiling(hshape, axis, 16)

    def body(pl_ref, pf_ref, rb_ref, o_ref):
        o_ref[...] = ((pf_ref[...] + rb_ref[0].astype(F32)) + rb_ref[1].astype(F32)) + rb_ref[2].astype(F32)

    return pl.pallas_call(
        body, name=name,
        grid_spec=pltpu.PrefetchScalarGridSpec(
            num_scalar_prefetch=1, grid=(nt,),
            in_specs=[pl.BlockSpec((None, br, bc), lambda i, p: (p[0],) + _tile_idx(axis, i)),
                      pl.BlockSpec((3, br, bc), lambda i, p: (0,) + _tile_idx(axis, i))],
            out_specs=pl.BlockSpec((br, bc), lambda i, p: _tile_idx(axis, i))),
        out_shape=jax.ShapeDtypeStruct(hshape, F32),
        compiler_params=_cp("arbitrary"),
    )(place, pf, rb)


def _add2(a, b, name):
    def body(a_ref, b_ref, o_ref):
        o_ref[...] = a_ref[...] + b_ref[...]

    return pl.pallas_call(body, name=name, out_shape=jax.ShapeDtypeStruct(a.shape, F32))(a, b)


def _adam_math(w, g, m, v):
    m = ADAM_B1 * m + (1.0 - ADAM_B1) * g
    v = ADAM_B2 * v + (1.0 - ADAM_B2) * (g * g)
    m_hat = m / (1.0 - ADAM_B1 ** ADAM_STEP)
    v_hat = v / (1.0 - ADAM_B2 ** ADAM_STEP)
    return -ADAM_LR * (m_hat / (jnp.sqrt(v_hat) + ADAM_EPS) + ADAM_WD * w), m, v


def _adam_halves(place, w, mine, theirs, m, v, axis, name):
    br, bc, nt = _half_tiling(mine.shape, axis, 8)

    def body(pl_ref, w_ref, a_ref, b_ref, m_ref, v_ref, g_ref, d_ref, mo_ref, vo_ref):
        is_mine = pl.program_id(0) // nt == pl_ref[1]
        g = jnp.where(is_mine, a_ref[...], b_ref[...])
        d, mn, vn = _adam_math(w_ref[...], g, m_ref[...], v_ref[...])
        g_ref[...] = g
        d_ref[...] = d
        mo_ref[...] = mn
        vo_ref[...] = vn

    full = pl.BlockSpec((br, bc), lambda i, p: _tile_idx(axis, i))
    half = pl.BlockSpec((br, bc), lambda i, p: _tile_idx(axis, i % nt))
    return pl.pallas_call(
        body, name=name,
        grid_spec=pltpu.PrefetchScalarGridSpec(
            num_scalar_prefetch=1, grid=(2 * nt,), in_specs=[full, half, half, full, full], out_specs=[full] * 4),
        out_shape=[jax.ShapeDtypeStruct(w.shape, F32)] * 4, compiler_params=_cp("arbitrary"),
    )(place, w, mine, theirs, m, v)


def _adam_small(chip_sums, w, m, v):
    def body(s_ref, w_ref, m_ref, v_ref, g_ref, d_ref, mo_ref, vo_ref):
        g = ((s_ref[0] + s_ref[1]) + s_ref[2]) + s_ref[3]
        d, mn, vn = _adam_math(w_ref[...], g, m_ref[...], v_ref[...])
        g_ref[...] = g
        d_ref[...] = d
        mo_ref[...] = mn
        vo_ref[...] = vn

    return pl.pallas_call(body, name="adam_small", out_shape=[jax.ShapeDtypeStruct(w.shape, F32)] * 4)(chip_sums, w, m, v)


SMALL = (("g_mix", (1, 1024)), ("b_gate", (1, 2048)), ("w_gk_up", (1, 16, 512)), ("b_gk", (1, 512)),
         ("w_pool_grp", (1, 4, 128, 128)), ("pool_scale", (1, 512)), ("g_gla_head", (1, 256)), ("g_ffn", (1, 1024)),
         ("w_conv", (1, 3, 5632)), ("b_conv", (1, 5632)), ("g_final", (1024,)), ("loss", (768,)))
SMALL_ROWS = 808


def _pack_small(parts):
    flat = jnp.concatenate([parts[n].astype(F32).reshape(-1) for n, _ in SMALL])
    return flat.reshape(SMALL_ROWS, 128)


def _unpack_small(buf):
    flat = buf.reshape(-1)
    out, off = {}, 0
    for n, shp in SMALL:
        size = 1
        for d_ in shp:
            size *= d_
        out[n] = flat[off:off + size].reshape(shp)
        off += size
    return out


def kernel(x, g_mix, w_in, b_gate, w_gk_up, b_gk, w_pool_grp, pool_scale, g_gla_head, w_pool_proj, w_gla_proj, w_out, g_ffn, w_up, w_conv, b_conv, w_down, g_final, loss_target, m_g_mix, m_w_in, m_b_gate, m_w_gk_up, m_b_gk, m_w_pool_grp, m_pool_scale, m_g_gla_head, m_w_pool_proj, m_w_gla_proj, m_w_out, m_g_ffn, m_w_up, m_w_conv, m_b_conv, m_w_down, m_g_final, v_g_mix, v_w_in, v_b_gate, v_w_gk_up, v_b_gk, v_w_pool_grp, v_pool_scale, v_g_gla_head, v_w_pool_proj, v_w_gla_proj, v_w_out, v_g_ffn, v_w_up, v_w_conv, v_b_conv, v_w_down, v_g_final):
    s = x.shape[1]
    ts = min(s, 512)
    tm = min(s, 256)
    cx, cy, cc = lax.axis_index("x"), lax.axis_index("y"), lax.axis_index("c")
    chip = 2 * cx + cy
    place = jnp.stack([chip, cc]).astype(jnp.int32)

    big_names = ("w_in", "w_pool_proj", "w_gla_proj", "w_out", "w_up", "w_down")
    axes = (1, 0, 0, 0, 0, 0)
    shards = dict(w_in=jnp.transpose(w_in[0]), w_pool_proj=w_pool_proj[0], w_gla_proj=w_gla_proj[0], w_out=w_out[0],
                  w_up=w_up[0], w_down=w_down[0])
    own = [shards[n].astype(BF16) for n in big_names] + [w_gk_up[0], w_conv[0]]

    def fill_own(lands, mine):
        return [lax.dynamic_update_slice(g, o_[None], (chip, 0, 0)) for g, o_ in zip(lands, mine)]

    first = fill_own(_all_gather_weights(own[:1], axes[:1], own[6:]), own[:1] + own[6:])
    mix_plan_w = _gather_plan([o_.shape for o_ in own[1:4]], axes[1:4])
    mix_w, tok = _split_start("gather_mix_start", own[1:4], [((4,) + o_.shape, BF16) for o_ in own[1:4]], mix_plan_w, 9,
                              first[0])
    ffn_plan_w = _gather_plan([o_.shape for o_ in own[4:6]], axes[4:6])
    ffn_w, tok = _split_start("gather_ffn_start", own[4:6], [((4,) + o_.shape, BF16) for o_ in own[4:6]], ffn_plan_w, 6, tok)

    wgk_full = jnp.transpose(first[1], (1, 0, 2)).reshape(GATE_RANK, 512)
    wconv_full = jnp.transpose(first[2], (1, 0, 2)).reshape(3, N_UP)
    wgk_pad = jnp.concatenate([wgk_full, jnp.zeros((128 - GATE_RANK, 512), F32)], axis=0)
    nsh = N_IN // 4

    w_in_t = first[0].reshape(N_IN, D)
    w_rt = jnp.concatenate([w_in_t[3600:], w_in_t[1536:3584], w_in_t[0:1536], w_in_t[3584:3600],
                            jnp.zeros((128 - GATE_RANK, D), BF16)], axis=0)

    xs, tgt = x[0], loss_target[0]
    wgrp = w_pool_grp[0]

    h = _rmsnorm(xs, g_mix + tok[0:1, 0:1], "norm_mix", ts)
    zr = _matmul_resident(h, w_rt, "in_proj", 1152, transposed=True)
    p, pp = _pool_fwd(zr, wgrp, pool_scale)
    o, og, sp = _gla_fwd(zr, wgk_pad, b_gk, g_gla_head, tm)
    mine, lands = _split_wait("gather_mix_wait", mix_w, 3, mix_plan_w, og)
    wpp, wgla, wout = fill_own(_gather_share(lands, axes[1:4], "gather_mix_share"), mine)
    wgla, wout = wgla.reshape(D, D), wout.reshape(D, D)
    x1, mixed, yp, yg = _merge_fwd(xs, zr, pp, og, b_gate, wpp, wgla, wout, tm)
    mine, lands = _split_wait("gather_ffn_wait", ffn_w, 2, ffn_plan_w, x1)
    wup, wdown = fill_own(_gather_share(lands, axes[4:6], "gather_ffn_share"), mine)
    wdown = wdown.reshape(D_FF, D)
    h2 = _rmsnorm(x1, g_ffn, "norm_ffn", ts)
    u = _matmul_resident(h2, wup, "ffn_up", None)
    a, dx2, dx2b, loss_part, dgfin = _ffn_down_loss(u, x1, tgt, wconv_full, b_conv, wdown, g_final.reshape(1, D), tm)

    du, dbconv, dwconv = _ffn_bwd(dx2b, u, wconv_full, b_conv, wdown, tm)
    dw_down = _matmul_tn(a, dx2b, "dw_down", D, s, tm=1408)
    dw_up = _matmul_tn(h2, du, "dw_up", 1408, s, shard_major=True)

    def exchange_start(tag, grads, group_axes, after):
        plan = _sibling_plan([g.shape[1:] for g in grads], group_axes)
        lands = [((4,) + _half_shape(g.shape[1:], ax), F32) for g, ax in zip(grads, group_axes)]
        handle, token = _split_start("sibling_" + tag + "_start", grads, lands, plan, len(grads), after)
        return (handle, plan, len(grads)), token

    def partials(tag, names, group_axes, exchange, after):
        handle, plan, n = exchange
        mine, theirs = _split_wait("sibling_" + tag + "_wait", handle, n, plan, after)
        return zip(*[_chip_partial(place, g, t, ax, "chip_partial_" + nm)
                     for nm, ax, g, t in zip(names, group_axes, mine, theirs)])

    ffn_names, ffn_axes = ("w_up", "w_down"), (0, 0)
    ffn_x, token = exchange_start("ffn", [dw_up, dw_down.reshape(4, 704, D)], ffn_axes, du)
    dx1, dx1b, dgffn = _matmul_nt_normbwd(du, wup, x1, g_ffn + token[0:1, 0:1], dx2, "ffn_up_bwd", ts)
    ffn_pf, ffn_pb = partials("ffn", ffn_names, ffn_axes, ffn_x, dx1b)
    ffn_plan = _reduce_plan(2, False)
    ffn_handle, token = _split_start("reduce_ffn_start", ffn_pb, [((3,) + p.shape[1:], BF16) for p in ffn_pb],
                                     ffn_plan, 6, ffn_pf[0])

    dzg, dyp, dyg, dpp, do, dzog, dbgate, dghead = _merge_bwd(dx1b, zr, yp, yg, o, b_gate + token[0:1, 0:1], g_gla_head,
                                                             wpp, wgla, wout, tm)
    dw_out = _matmul_tn(mixed, dx1b, "dw_out", D, s)
    dw_gla = _matmul_tn(og, dyg, "dw_gla", D, s)
    dw_pp = _matmul_tn(pp, dyp, "dw_pp", 256, s, shard_major=True)
    dzp, dwgrp, dscale = _pool_bwd(p, dpp, wgrp, pool_scale)
    dq, dk, dv, dgpre = _gla_bwd(zr, do, sp, wgk_pad, b_gk, tm)
    dzgk, dwgk, dbgk = _gk_bwd(dgpre, zr, wgk_pad, ts)
    dzr = jnp.concatenate([dzg, dv, dzog, dzp, dq, dk, dzgk], axis=1)
    dw_rt = _matmul_tn(dzr, h, "dw_in", D, s, tm=1152)

    def grad_rows(lo, hi):
        out = []
        for seg_lo, seg_hi, at in ((0, 1536, OFF_POOL), (1536, 3584, OFF_V), (3584, 3600, OFF_GK), (3600, N_IN, OFF_GATE)):
            a_, b_ = max(lo, seg_lo), min(hi, seg_hi)
            if a_ < b_:
                out.append(dw_rt[at + a_ - seg_lo:at + b_ - seg_lo])
        return jnp.concatenate(out, axis=0)

    dw_in_t = jnp.stack([grad_rows(j * nsh, (j + 1) * nsh) for j in range(4)])

    mix_names, mix_axes = ("w_in", "w_pool_proj", "w_gla_proj", "w_out"), (1, 0, 0, 0)
    mix_x, token = exchange_start("mix", [dw_in_t, dw_pp, dw_gla.reshape(4, 256, D), dw_out.reshape(4, 256, D)], mix_axes,
                                  dzr)
    grad_x, _, dgmix = _matmul_nt_normbwd(dzr, w_rt, xs, g_mix + token[0:1, 0:1], dx1, "in_proj_bwd", ts, transposed=True)
    small_mine = _pack_small(dict(
        g_mix=dgmix, b_gate=dbgate, w_gk_up=dwgk[:GATE_RANK], b_gk=dbgk, w_pool_grp=dwgrp, pool_scale=dscale,
        g_gla_head=dghead, g_ffn=dgffn, w_conv=dwconv, b_conv=dbconv, g_final=dgfin,
        loss=jnp.concatenate([loss_part.reshape(128), jnp.zeros((640,), F32)])))
    mix_pf, mix_pb = partials("mix", mix_names, mix_axes, mix_x, grad_x)
    small_sib = _sibling_exchange([], (), small_mine, "sibling_exchange_small")[0]
    small_chip = _add2(small_mine, small_sib, "chip_partial_small")
    mix_plan = _reduce_plan(4, True)
    mix_handle, mix_token = _split_start(
        "reduce_mix_start", list(mix_pb) + [small_chip],
        [((3,) + p.shape[1:], BF16) for p in mix_pb] + [((4,) + small_chip.shape, F32)], mix_plan, 15, mix_pf[0])

    ms = dict(w_in=jnp.transpose(m_w_in[0]), w_pool_proj=m_w_pool_proj[0], w_gla_proj=m_w_gla_proj[0], w_out=m_w_out[0],
              w_up=m_w_up[0], w_down=m_w_down[0])
    vs = dict(w_in=jnp.transpose(v_w_in[0]), w_pool_proj=v_w_pool_proj[0], w_gla_proj=v_w_gla_proj[0], w_out=v_w_out[0],
              w_up=v_w_up[0], w_down=v_w_down[0])
    grad, delta, new_m, new_v = {}, {}, {}, {}

    def finish_and_update(names, group_axes, part_f, landed, tag):
        halves = [_finish_half(place, pf, rb, ax, "finish_" + n) for n, ax, pf, rb in zip(names, group_axes, part_f, landed)]
        sib_halves = _sibling_share(halves, "sibling_share_" + tag)
        for n, ax, mine, theirs in zip(names, group_axes, halves, sib_halves):
            res = _adam_halves(place, shards[n], mine, theirs, ms[n], vs[n], ax, "adam_" + n)
            if n == "w_in":
                res = [jnp.transpose(r_) for r_ in res]
            grad[n], delta[n], new_m[n], new_v[n] = [r_[None] for r_ in res]

    _, ffn_landed = _split_wait("reduce_ffn_wait", ffn_handle, 2, ffn_plan, mix_token)
    finish_and_update(ffn_names, ffn_axes, ffn_pf, ffn_landed, "ffn")
    mix_sent, mix_landed = _split_wait("reduce_mix_wait", mix_handle, 5, mix_plan, delta["w_down"])
    small_sums = lax.dynamic_update_slice(mix_landed[4], mix_sent[4][None], (chip, 0, 0))
    finish_and_update(mix_names, mix_axes, mix_pf, mix_landed[:4], "mix")

    def widen(a, width):
        z = jnp.zeros(a.shape[:-1] + (4, width), F32)
        return lax.dynamic_update_slice(z, a[..., None, :], (0,) * (a.ndim - 1) + (chip, 0)).reshape(a.shape[:-1] + (4 * width,))

    def small_of(g_mix, b_gate, w_gk_up, b_gk, w_pool_grp, pool_scale, g_gla_head, g_ffn, w_conv, b_conv, g_final):
        return _pack_small(dict(g_mix=g_mix, b_gate=b_gate, w_gk_up=widen(w_gk_up, 128), b_gk=b_gk, w_pool_grp=w_pool_grp,
                                pool_scale=pool_scale, g_gla_head=g_gla_head, g_ffn=g_ffn, w_conv=widen(w_conv, 1408),
                                b_conv=b_conv, g_final=g_final, loss=jnp.zeros((768,), F32)))

    sw = small_of(g_mix, b_gate, w_gk_up, b_gk, w_pool_grp, pool_scale, g_gla_head, g_ffn, w_conv, b_conv, g_final)
    sm = small_of(m_g_mix, m_b_gate, m_w_gk_up, m_b_gk, m_w_pool_grp, m_pool_scale, m_g_gla_head, m_g_ffn, m_w_conv,
                  m_b_conv, m_g_final)
    sv = small_of(v_g_mix, v_b_gate, v_w_gk_up, v_b_gk, v_w_pool_grp, v_pool_scale, v_g_gla_head, v_g_ffn, v_w_conv,
                  v_b_conv, v_g_final)
    sg, sd, smo, svo = _adam_small(small_sums, sw, sm, sv)

    def narrow(a, width):
        return lax.dynamic_slice_in_dim(a.reshape(a.shape[:-1] + (4, width)), chip, 1, axis=a.ndim - 1).reshape(
            a.shape[:-1] + (width,))

    loss = None
    for dst, buf in ((grad, sg), (delta, sd), (new_m, smo), (new_v, svo)):
        parts = _unpack_small(buf)
        if dst is grad:
            loss = parts["loss"][0]
        for n, _ in SMALL[:-1]:
            val = parts[n]
            if n == "w_gk_up":
                val = narrow(val, 128)
            elif n == "w_conv":
                val = narrow(val, 1408)
            dst[n] = val

    order = ("g_mix", "w_in", "b_gate", "w_gk_up", "b_gk", "w_pool_grp", "pool_scale", "g_gla_head", "w_pool_proj",
             "w_gla_proj", "w_out", "g_ffn", "w_up", "w_conv", "b_conv", "w_down", "g_final")
    return (loss, grad_x[None], *[grad[n] for n in order], *[delta[n] for n in order], *[new_m[n] for n in order],
            *[new_v[n] for n in order])
```

```python
import functools

import jax
import jax.numpy as jnp
from jax import lax
from jax.experimental import pallas as pl
from jax.experimental.pallas import tpu as pltpu

F32 = jnp.float32
BF16 = jnp.bfloat16
MESH = pl.DeviceIdType.MESH

D = 1024
EPS = 1e-6
CHUNK = 64
POOL_W = 512
POOL_WINDOWS = (2, 4, 8, 16)
HEADS = 4
HK = 128
HV = 256
GATE_RANK = 16
D_FF = 2816
N_UP = 2 * D_FF
N_IN = 5648
QSCALE = HK ** -0.5
N_INR = 5760
OFF_GATE, OFF_V, OFF_OG, OFF_POOL, OFF_Q, OFF_K, OFF_GK = 0, 2048, 3072, 4096, 4608, 5120, 5632

ADAM_LR, ADAM_B1, ADAM_B2, ADAM_EPS, ADAM_WD, ADAM_STEP = 0.001, 0.9, 0.999, 1e-08, 0.01, 10

VMEM_LIMIT = 56 * 1024 * 1024


def _cp(*sem):
    return pltpu.CompilerParams(dimension_semantics=sem if sem else None, vmem_limit_bytes=VMEM_LIMIT)


def _dot(a, b):
    return jnp.dot(a, b, preferred_element_type=F32)


def _dot_nt(a, b):
    return lax.dot_general(a, b, (((1,), (1,)), ((), ())), preferred_element_type=F32)


def _dot_tn(a, b):
    return lax.dot_general(a, b, (((0,), (0,)), ((), ())), preferred_element_type=F32)


def _sigmoid(v):
    return 1.0 / (1.0 + jnp.exp(-v))


def _rows(shape):
    return lax.broadcasted_iota(jnp.int32, shape, 0)


def _pick_row(v, r):
    return jnp.sum(jnp.where(_rows(v.shape) == r, v, 0.0), axis=0, keepdims=True)


def _rmsnorm(x, g, name, ts):
    s = x.shape[0]

    def body(x_ref, g_ref, h_ref):
        xv = x_ref[...]
        r = lax.rsqrt(jnp.mean(xv * xv, axis=-1, keepdims=True) + EPS)
        h_ref[...] = (xv * r * g_ref[...]).astype(BF16)

    return pl.pallas_call(
        body, name=name, grid=(s // ts,),
        in_specs=[pl.BlockSpec((ts, D), lambda i: (i, 0)), pl.BlockSpec((1, D), lambda i: (0, 0))],
        out_specs=pl.BlockSpec((ts, D), lambda i: (i, 0)), out_shape=jax.ShapeDtypeStruct((s, D), BF16),
        compiler_params=_cp("arbitrary"),
    )(x, g)


MM_ROWS = 512


def _matmul_resident(h, w, name, tn, transposed=False):
    s = h.shape[0]
    if transposed:
        nj = w.shape[0] // tn
        w_spec = pl.BlockSpec((tn, D), lambda j: (j, 0))
    elif w.ndim == 3:
        nj, tn = w.shape[0], w.shape[2]
        w_spec = pl.BlockSpec((None, D, tn), lambda j: (j, 0, 0))
    else:
        nj = w.shape[1] // tn
        w_spec = pl.BlockSpec((D, tn), lambda j: (0, j))
    mm = _dot_nt if transposed else _dot
    rc = min(s, MM_ROWS)

    def body(h_ref, w_ref, z_ref):
        for r0 in range(0, s, rc):
            z_ref[r0:r0 + rc, :] = mm(h_ref[r0:r0 + rc, :], w_ref[...]).astype(BF16)

    return pl.pallas_call(
        body, name=name, grid=(nj,),
        in_specs=[pl.BlockSpec((s, D), lambda j: (0, 0)), w_spec],
        out_specs=pl.BlockSpec((s, tn), lambda j: (0, j)), out_shape=jax.ShapeDtypeStruct((s, nj * tn), BF16),
        compiler_params=_cp("arbitrary"),
    )(h, w)


def _matmul_nt_normbwd(dz, w, x, g, resid, name, ts, transposed=False):
    s = x.shape[0]

    def body(dz_ref, w_hbm, x_ref, g_ref, r_ref, o_ref, ob_ref, dg_ref, w_ref, sem):
        @pl.when(pl.program_id(0) == 0)
        def _():
            cp = pltpu.make_async_copy(w_hbm, w_ref, sem)
            cp.start()
            cp.wait()
            dg_ref[...] = jnp.zeros_like(dg_ref)

        if transposed:
            dh = _dot(dz_ref[...], w_ref[...])
        else:
            kc = w.shape[2]
            dh = _dot_nt(dz_ref[:, 0:kc], w_ref[0])
            for j in range(1, w.shape[0]):
                dh = dh + _dot_nt(dz_ref[:, j * kc:(j + 1) * kc], w_ref[j])
        xv = x_ref[...]
        r = lax.rsqrt(jnp.mean(xv * xv, axis=-1, keepdims=True) + EPS)
        xh = xv * r
        dg_ref[...] += jnp.sum(dh * xh, axis=0, keepdims=True)
        dxh = dh * g_ref[...]
        out = r_ref[...] + r * (dxh - xh * jnp.mean(dxh * xh, axis=-1, keepdims=True))
        o_ref[...] = out
        ob_ref[...] = out.astype(BF16)

    row = lambda i: (i, 0)
    kdim = dz.shape[1]
    return pl.pallas_call(
        body, name=name, grid=(s // ts,),
        in_specs=[pl.BlockSpec((ts, kdim), row), ANY, pl.BlockSpec((ts, D), row),
                  pl.BlockSpec((1, D), lambda i: (0, 0)), pl.BlockSpec((ts, D), row)],
        out_specs=[pl.BlockSpec((ts, D), row), pl.BlockSpec((ts, D), row), pl.BlockSpec((1, D), lambda i: (0, 0))],
        out_shape=[jax.ShapeDtypeStruct((s, D), F32), jax.ShapeDtypeStruct((s, D), BF16),
                   jax.ShapeDtypeStruct((1, D), F32)],
        scratch_shapes=[pltpu.VMEM(w.shape, BF16), pltpu.SemaphoreType.DMA],
        compiler_params=_cp("arbitrary"),
    )(dz, w, x, g, resid)


def _matmul_tn(a, b, name, tn, tk, shard_major=False, tm=None):
    s, m = a.shape
    n = b.shape[1]
    tm = m if tm is None else tm
    ni, nj, nk = m // tm, n // tn, s // tk

    def body(a_ref, b_ref, o_ref):
        if nk == 1:
            o_ref[...] = _dot_tn(a_ref[...], b_ref[...])
            return

        @pl.when(pl.program_id(2) == 0)
        def _():
            o_ref[...] = jnp.zeros_like(o_ref)

        o_ref[...] += _dot_tn(a_ref[...], b_ref[...])

    if shard_major:
        out_spec = pl.BlockSpec((None, tm, tn), lambda i, j, k: (j, i, 0))
        out_shape = jax.ShapeDtypeStruct((nj, m, tn), F32)
    else:
        out_spec = pl.BlockSpec((tm, tn), lambda i, j, k: (i, j))
        out_shape = jax.ShapeDtypeStruct((m, n), F32)
    return pl.pallas_call(
        body, name=name, grid=(ni, nj, nk),
        in_specs=[pl.BlockSpec((tk, tm), lambda i, j, k: (k, i)), pl.BlockSpec((tk, tn), lambda i, j, k: (k, j))],
        out_specs=out_spec, out_shape=out_shape,
        compiler_params=_cp("arbitrary", "arbitrary", "arbitrary"),
    )(a, b)


def _pool_fwd(zr, wgrp, scale):
    s = zr.shape[0]

    def body(u_ref, w_ref, sc_ref, p_ref, pp_ref):
        row = _rows((s, 128))
        for gi, win in enumerate(POOL_WINDOWS):
            cs = slice(gi * 128, (gi + 1) * 128)
            u = u_ref[:, cs].astype(F32)
            acc, k = u, 1
            while k < win:
                acc = acc + jnp.where(row >= k, pltpu.roll(acc, k, 0), 0.0)
                k *= 2
            cnt = jnp.minimum(row + 1, win).astype(F32)
            p = (acc / cnt - u).astype(BF16)
            p_ref[:, cs] = p
            pp_ref[:, cs] = (_dot(p, w_ref[gi].astype(BF16)) * sc_ref[:, cs]).astype(BF16)

    return pl.pallas_call(
        body, name="pool_fwd", grid=(1,),
        in_specs=[pl.BlockSpec((s, POOL_W), lambda i: (0, OFF_POOL // POOL_W)),
                  pl.BlockSpec((4, 128, 128), lambda i: (0, 0, 0)), pl.BlockSpec((1, POOL_W), lambda i: (0, 0))],
        out_specs=[pl.BlockSpec((s, POOL_W), lambda i: (0, 0))] * 2,
        out_shape=[jax.ShapeDtypeStruct((s, POOL_W), BF16)] * 2,
        compiler_params=_cp("arbitrary"),
    )(zr, wgrp, scale)


def _pool_bwd(p, dpp, wgrp, scale):
    s = p.shape[0]

    def body(p_ref, dpp_ref, w_ref, sc_ref, dz_ref, dw_ref, dsc_ref):
        row = _rows((s, 128))
        for gi, win in enumerate(POOL_WINDOWS):
            cs = slice(gi * 128, (gi + 1) * 128)
            pv = p_ref[:, cs]
            wb = w_ref[gi].astype(BF16)
            dpp_v = dpp_ref[:, cs].astype(F32)
            dsc_ref[:, cs] = jnp.sum(dpp_v * _dot(pv, wb), axis=0, keepdims=True)
            dpm = (dpp_v * sc_ref[:, cs]).astype(BF16)
            dw_ref[gi] = _dot_tn(pv, dpm)
            dp = _dot_nt(dpm, wb)
            cnt = jnp.minimum(row + 1, win).astype(F32)
            acc, k = dp / cnt, 1
            while k < win:
                acc = acc + jnp.where(row < s - k, pltpu.roll(acc, s - k, 0), 0.0)
                k *= 2
            dz_ref[:, cs] = (acc - dp).astype(BF16)

    full = lambda i: (0, 0)
    return pl.pallas_call(
        body, name="pool_bwd", grid=(1,),
        in_specs=[pl.BlockSpec((s, POOL_W), full), pl.BlockSpec((s, POOL_W), full),
                  pl.BlockSpec((4, 128, 128), lambda i: (0, 0, 0)), pl.BlockSpec((1, POOL_W), full)],
        out_specs=[pl.BlockSpec((s, POOL_W), full), pl.BlockSpec((4, 128, 128), lambda i: (0, 0, 0)),
                   pl.BlockSpec((1, POOL_W), full)],
        out_shape=[jax.ShapeDtypeStruct((s, POOL_W), BF16), jax.ShapeDtypeStruct((4, 128, 128), F32),
                   jax.ShapeDtypeStruct((1, POOL_W), F32)],
        compiler_params=_cp("arbitrary"),
    )(p, dpp, wgrp, scale)


def _gla_decay(zgk_ref, wgk_ref, bgk_ref, rb):
    g = _dot(zgk_ref[...], wgk_ref[...].astype(BF16)) + bgk_ref[...]
    la = (jnp.minimum(g, 0.0) - jnp.log(1.0 + jnp.exp(-jnp.abs(g)))) * (1.0 / 16.0)
    rowm = _rows((rb, HK)) & (CHUNK - 1)
    bc, k = la, 1
    while k < CHUNK:
        bc = bc + jnp.where(rowm >= k, pltpu.roll(bc, k, 0), 0.0)
        k *= 2
    return g, jnp.exp(bc), jnp.exp(-bc)


def _gla_specs(rb, rmap):
    return [pl.BlockSpec((rb, HK), lambda h, r: (rmap(h, r), OFF_Q // HK + h)),
            pl.BlockSpec((rb, HK), lambda h, r: (rmap(h, r), OFF_K // HK + h)),
            pl.BlockSpec((rb, HV), lambda h, r: (rmap(h, r), OFF_V // HV + h)),
            pl.BlockSpec((rb, 128), lambda h, r: (rmap(h, r), OFF_GK // 128))]


def _gla_fwd(zr, wgk, bgk, ghead, rb):
    s = zr.shape[0]
    nc = rb // CHUNK

    def body(q_ref, k_ref, v_ref, zgk_ref, zog_ref, wgk_ref, bgk_ref, gh_ref, o_ref, og_ref, sp_ref, st_ref):
        @pl.when(pl.program_id(1) == 0)
        def _():
            st_ref[...] = jnp.zeros_like(st_ref)

        _, e_pos, e_neg = _gla_decay(zgk_ref, wgk_ref, bgk_ref, rb)
        lower = _rows((CHUNK, CHUNK)) >= lax.broadcasted_iota(jnp.int32, (CHUNK, CHUNK), 1)
        for c in range(nc):
            sl = slice(c * CHUNK, (c + 1) * CHUNK)
            q = q_ref[sl, :].astype(F32) * QSCALE
            k = k_ref[sl, :].astype(F32)
            v = v_ref[sl, :]
            ec, fc = e_pos[sl], e_neg[sl]
            qfw = (q * ec).astype(BF16)
            kfw_f = k * fc
            s_fw = _dot_nt(qfw, kfw_f.astype(BF16))
            s_bw = _dot_nt((q * fc).astype(BF16), (k * ec).astype(BF16))
            pm = jnp.where(lower, s_fw, s_bw).astype(BF16)
            st = st_ref[...]
            stb = st.astype(BF16)
            sp_ref[c] = stb
            o = _dot(pm, v) + _dot_nt(qfw, stb)
            e_last = _pick_row(ec, CHUNK - 1)
            kdec = (kfw_f * e_last).astype(BF16)
            st_ref[...] = st * e_last + _dot_tn(v, kdec)
            r = lax.rsqrt(jnp.mean(o * o, axis=-1, keepdims=True) + EPS)
            zo = zog_ref[sl, :].astype(F32)
            o_ref[sl, :] = o.astype(BF16)
            og_ref[sl, :] = (o * r * gh_ref[...] * zo * _sigmoid(zo)).astype(BF16)

    rmap = lambda h, r: r
    return pl.pallas_call(
        body, name="gla_fwd", grid=(HEADS, s // rb),
        in_specs=_gla_specs(rb, rmap) + [
            pl.BlockSpec((rb, HV), lambda h, r: (r, OFF_OG // HV + h)),
            pl.BlockSpec((128, HK), lambda h, r: (0, h)), pl.BlockSpec((1, HK), lambda h, r: (0, h)),
            pl.BlockSpec((1, HV), lambda h, r: (0, 0))],
        out_specs=[pl.BlockSpec((rb, HV), lambda h, r: (r, h)), pl.BlockSpec((rb, HV), lambda h, r: (r, h)),
                   pl.BlockSpec((nc, None, HV, HK), lambda h, r: (r, h, 0, 0))],
        out_shape=[jax.ShapeDtypeStruct((s, D), BF16), jax.ShapeDtypeStruct((s, D), BF16),
                   jax.ShapeDtypeStruct((s // CHUNK, HEADS, HV, HK), BF16)],
        scratch_shapes=[pltpu.VMEM((HV, HK), F32)],
        compiler_params=_cp("arbitrary", "arbitrary"),
    )(zr, zr, zr, zr, zr, wgk, bgk, ghead)


def _gla_bwd(zr, do, sp, wgk, bgk, rb):
    s = zr.shape[0]
    nc = rb // CHUNK
    nr = s // rb

    def body(q_ref, k_ref, v_ref, zgk_ref, do_ref, sp_ref, wgk_ref, bgk_ref, dq_ref, dk_ref, dv_ref, dg_ref,
             gt_ref, dbc_ref):
        @pl.when(pl.program_id(1) == 0)
        def _():
            gt_ref[...] = jnp.zeros_like(gt_ref)

        g, e_pos, e_neg = _gla_decay(zgk_ref, wgk_ref, bgk_ref, rb)
        lower = _rows((CHUNK, CHUNK)) >= lax.broadcasted_iota(jnp.int32, (CHUNK, CHUNK), 1)
        is_last = _rows((CHUNK, HK)) == CHUNK - 1
        for c in reversed(range(nc)):
            sl = slice(c * CHUNK, (c + 1) * CHUNK)
            q = q_ref[sl, :].astype(F32) * QSCALE
            k = k_ref[sl, :].astype(F32)
            v = v_ref[sl, :]
            dov = do_ref[sl, :]
            ec, fc = e_pos[sl], e_neg[sl]
            qfw_f, kfw_f, qbw_f, kbw_f = q * ec, k * fc, q * fc, k * ec
            qfw, kfw, qbw, kbw = qfw_f.astype(BF16), kfw_f.astype(BF16), qbw_f.astype(BF16), kbw_f.astype(BF16)
            pm = jnp.where(lower, _dot_nt(qfw, kfw), _dot_nt(qbw, kbw)).astype(BF16)
            e_last = _pick_row(ec, CHUNK - 1)
            kdec = (kfw_f * e_last).astype(BF16)
            gt = gt_ref[...]
            gtb = gt.astype(BF16)
            spv = sp_ref[c]
            dp = _dot_nt(dov, v)
            dv_ref[sl, :] = (_dot_tn(pm, dov) + _dot_nt(kdec, gtb)).astype(BF16)
            ds_fw = jnp.where(lower, dp, 0.0).astype(BF16)
            ds_bw = jnp.where(lower, 0.0, dp).astype(BF16)
            dqfw = _dot(ds_fw, kfw) + _dot(dov, spv)
            dkfw = _dot_tn(ds_fw, qfw)
            dqbw = _dot(ds_bw, kbw)
            dkbw = _dot_tn(ds_bw, qbw)
            dkdec = _dot(v, gtb)
            de_last = (jnp.sum(gt * spv.astype(F32), axis=0, keepdims=True)
                       + jnp.sum(dkdec * kfw_f, axis=0, keepdims=True))
            dkfw = dkfw + dkdec * e_last
            dq_ref[sl, :] = ((dqfw * ec + dqbw * fc) * QSCALE).astype(BF16)
            dk_ref[sl, :] = (dkfw * fc + dkbw * ec).astype(BF16)
            dbc = dqfw * qfw_f - dqbw * qbw_f + dkbw * kbw_f - dkfw * kfw_f
            dbc_ref[sl, :] = dbc + jnp.where(is_last, de_last * e_last, 0.0)
            gt_ref[...] = _dot_tn(dov, qfw) + gt * e_last
        rowm = _rows((rb, HK)) & (CHUNK - 1)
        dla, kk = dbc_ref[...], 1
        while kk < CHUNK:
            dla = dla + jnp.where(rowm < CHUNK - kk, pltpu.roll(dla, rb - kk, 0), 0.0)
            kk *= 2
        dg_ref[...] = dla * (1.0 / 16.0) * _sigmoid(-g)

    rmap = lambda h, r: nr - 1 - r
    return pl.pallas_call(
        body, name="gla_bwd", grid=(HEADS, nr),
        in_specs=_gla_specs(rb, rmap) + [
            pl.BlockSpec((rb, HV), lambda h, r: (nr - 1 - r, h)),
            pl.BlockSpec((nc, None, HV, HK), lambda h, r: (nr - 1 - r, h, 0, 0)),
            pl.BlockSpec((128, HK), lambda h, r: (0, h)), pl.BlockSpec((1, HK), lambda h, r: (0, h))],
        out_specs=[pl.BlockSpec((rb, HK), lambda h, r: (nr - 1 - r, h)), pl.BlockSpec((rb, HK), lambda h, r: (nr - 1 - r, h)),
                   pl.BlockSpec((rb, HV), lambda h, r: (nr - 1 - r, h)), pl.BlockSpec((rb, HK), lambda h, r: (nr - 1 - r, h))],
        out_shape=[jax.ShapeDtypeStruct((s, HEADS * HK), BF16), jax.ShapeDtypeStruct((s, HEADS * HK), BF16),
                   jax.ShapeDtypeStruct((s, D), BF16), jax.ShapeDtypeStruct((s, HEADS * HK), F32)],
        scratch_shapes=[pltpu.VMEM((HV, HK), F32), pltpu.VMEM((rb, HK), F32)],
        compiler_params=_cp("arbitrary", "arbitrary"),
    )(zr, zr, zr, zr, do, sp, wgk, bgk)


def _gk_bwd(dgpre, zr, wgk, ts):
    s = zr.shape[0]

    def body(dg_ref, zgk_ref, w_ref, dz_ref, dw_ref, db_ref):
        @pl.when(pl.program_id(0) == 0)
        def _():
            dw_ref[...] = jnp.zeros_like(dw_ref)
            db_ref[...] = jnp.zeros_like(db_ref)

        dg = dg_ref[...]
        dgb = dg.astype(BF16)
        dz_ref[...] = _dot_nt(dgb, w_ref[...].astype(BF16)).astype(BF16)
        dw_ref[...] += _dot_tn(zgk_ref[...], dgb)
        db_ref[...] += jnp.sum(dg, axis=0, keepdims=True)

    return pl.pallas_call(
        body, name="gk_bwd", grid=(s // ts,),
        in_specs=[pl.BlockSpec((ts, 512), lambda i: (i, 0)), pl.BlockSpec((ts, 128), lambda i: (i, OFF_GK // 128)),
                  pl.BlockSpec((128, 512), lambda i: (0, 0))],
        out_specs=[pl.BlockSpec((ts, 128), lambda i: (i, 0)), pl.BlockSpec((128, 512), lambda i: (0, 0)),
                   pl.BlockSpec((1, 512), lambda i: (0, 0))],
        out_shape=[jax.ShapeDtypeStruct((s, 128), BF16), jax.ShapeDtypeStruct((128, 512), F32),
                   jax.ShapeDtypeStruct((1, 512), F32)],
        compiler_params=_cp("arbitrary"),
    )(dgpre, zr, wgk)


def _merge_fwd(x, zr, pp, og, bgate, wpp, wgla, wout, ts):
    s = x.shape[0]

    def body(x_ref, z0_ref, z1_ref, pp_ref, og_ref, bg_ref, wpp_ref, wgla_ref, wout_ref,
             x1_ref, mix_ref, yp_ref, yg_ref):
        ppv = pp_ref[...]
        yp = jnp.concatenate([_dot(ppv, wpp_ref[j]) for j in range(4)], axis=1)
        yg = _dot(og_ref[...], wgla_ref[...])
        g0 = _sigmoid(z0_ref[...].astype(F32) + bg_ref[:, :D])
        g1 = _sigmoid(z1_ref[...].astype(F32) + bg_ref[:, D:])
        mixed = (g0 * yp + g1 * yg).astype(BF16)
        x1_ref[...] = x_ref[...] + _dot(mixed, wout_ref[...])
        mix_ref[...] = mixed
        yp_ref[...] = yp.astype(BF16)
        yg_ref[...] = yg.astype(BF16)

    row = lambda i: (i, 0)
    const2 = lambda i: (0, 0)
    return pl.pallas_call(
        body, name="merge_fwd", grid=(s // ts,),
        in_specs=[pl.BlockSpec((ts, D), row), pl.BlockSpec((ts, D), lambda i: (i, 0)), pl.BlockSpec((ts, D), lambda i: (i, 1)),
                  pl.BlockSpec((ts, POOL_W), row), pl.BlockSpec((ts, D), row), pl.BlockSpec((1, 2 * D), const2),
                  pl.BlockSpec((4, POOL_W, 256), lambda i: (0, 0, 0)), pl.BlockSpec((D, D), const2),
                  pl.BlockSpec((D, D), const2)],
        out_specs=[pl.BlockSpec((ts, D), row)] * 4,
        out_shape=[jax.ShapeDtypeStruct((s, D), F32)] + [jax.ShapeDtypeStruct((s, D), BF16)] * 3,
        compiler_params=_cp("arbitrary"),
    )(x, zr, zr, pp, og, bgate, wpp, wgla, wout)


def _merge_bwd(dx1b, zr, yp, yg, o, bgate, ghead, wpp, wgla, wout, ts):
    s = dx1b.shape[0]

    def body(dx_ref, z0_ref, z1_ref, zog_ref, yp_ref, yg_ref, o_ref, bg_ref, gh_ref, wpp_ref, wgla_ref, wout_ref,
             dzg_ref, dyp_ref, dyg_ref, dpp_ref, do_ref, dzog_ref, dbg_ref, dgh_ref):
        @pl.when(pl.program_id(0) == 0)
        def _():
            dbg_ref[...] = jnp.zeros_like(dbg_ref)
            dgh_ref[...] = jnp.zeros_like(dgh_ref)

        dmix = _dot_nt(dx_ref[...], wout_ref[...])
        g0 = _sigmoid(z0_ref[...].astype(F32) + bg_ref[:, :D])
        g1 = _sigmoid(z1_ref[...].astype(F32) + bg_ref[:, D:])
        dypb = (dmix * g0).astype(BF16)
        dygb = (dmix * g1).astype(BF16)
        dz0 = dmix * yp_ref[...].astype(F32) * g0 * (1.0 - g0)
        dz1 = dmix * yg_ref[...].astype(F32) * g1 * (1.0 - g1)
        dzg_ref[:, :D] = dz0.astype(BF16)
        dzg_ref[:, D:] = dz1.astype(BF16)
        dbg_ref[:, :D] += jnp.sum(dz0, axis=0, keepdims=True)
        dbg_ref[:, D:] += jnp.sum(dz1, axis=0, keepdims=True)
        dyp_ref[...] = dypb
        dyg_ref[...] = dygb
        dpp = _dot_nt(dypb[:, 0:256], wpp_ref[0])
        for j in range(1, 4):
            dpp = dpp + _dot_nt(dypb[:, j * 256:(j + 1) * 256], wpp_ref[j])
        dpp_ref[...] = dpp.astype(BF16)
        dog = _dot_nt(dygb, wgla_ref[...])
        gh = gh_ref[...]
        dgh = jnp.zeros((1, HV), F32)
        for h in range(HEADS):
            cs = slice(h * HV, (h + 1) * HV)
            ov = o_ref[:, cs].astype(F32)
            r = lax.rsqrt(jnp.mean(ov * ov, axis=-1, keepdims=True) + EPS)
            oh = ov * r
            zo = zog_ref[:, cs].astype(F32)
            sg = _sigmoid(zo)
            dog_h = dog[:, cs]
            don = dog_h * zo * sg
            dzog_ref[:, cs] = (dog_h * oh * gh * sg * (1.0 + zo * (1.0 - sg))).astype(BF16)
            dgh = dgh + jnp.sum(don * oh, axis=0, keepdims=True)
            doh = don * gh
            do_ref[:, cs] = (r * (doh - oh * jnp.mean(doh * oh, axis=-1, keepdims=True))).astype(BF16)
        dgh_ref[...] += dgh

    row = lambda i: (i, 0)
    const2 = lambda i: (0, 0)
    return pl.pallas_call(
        body, name="merge_bwd", grid=(s // ts,),
        in_specs=[pl.BlockSpec((ts, D), row), pl.BlockSpec((ts, D), lambda i: (i, 0)), pl.BlockSpec((ts, D), lambda i: (i, 1)),
                  pl.BlockSpec((ts, D), lambda i: (i, OFF_OG // D)), pl.BlockSpec((ts, D), row), pl.BlockSpec((ts, D), row),
                  pl.BlockSpec((ts, D), row), pl.BlockSpec((1, 2 * D), const2), pl.BlockSpec((1, HV), const2),
                  pl.BlockSpec((4, POOL_W, 256), lambda i: (0, 0, 0)), pl.BlockSpec((D, D), const2),
                  pl.BlockSpec((D, D), const2)],
        out_specs=[pl.BlockSpec((ts, 2 * D), row), pl.BlockSpec((ts, D), row), pl.BlockSpec((ts, D), row),
                   pl.BlockSpec((ts, POOL_W), row), pl.BlockSpec((ts, D), row), pl.BlockSpec((ts, D), row),
                   pl.BlockSpec((1, 2 * D), const2), pl.BlockSpec((1, HV), const2)],
        out_shape=[jax.ShapeDtypeStruct((s, 2 * D), BF16), jax.ShapeDtypeStruct((s, D), BF16),
                   jax.ShapeDtypeStruct((s, D), BF16), jax.ShapeDtypeStruct((s, POOL_W), BF16),
                   jax.ShapeDtypeStruct((s, D), BF16), jax.ShapeDtypeStruct((s, D), BF16),
                   jax.ShapeDtypeStruct((1, 2 * D), F32), jax.ShapeDtypeStruct((1, HV), F32)],
        compiler_params=_cp("arbitrary"),
    )(dx1b, zr, zr, zr, yp, yg, o, bgate, ghead, wpp, wgla, wout)


HALO = 16
CCH = 1408


def _conv_taps(u_ref, halo_ref, cs, first, ts):
    u = u_ref[:, cs].astype(F32)
    hal = halo_ref[:, cs].astype(F32)
    h1 = jnp.where(first, 0.0, _pick_row(hal, HALO - 1))
    h2 = jnp.where(first, 0.0, _pick_row(hal, HALO - 2))
    row = _rows(u.shape)
    r1 = jnp.where(row == 0, h1, pltpu.roll(u, 1, 0))
    r2 = jnp.where(row == 0, h2, jnp.where(row == 1, h1, pltpu.roll(u, 2, 0)))
    return u, r1, r2


def _ffn_down_loss(u, x1, tgt, wconv, bconv, wdown, gfin, ts):
    s = x1.shape[0]

    def body(u_ref, halo_ref, x1_ref, t_ref, wc_ref, bc_ref, wd_ref, gf_ref, a_ref, dx_ref, dxb_ref, ls_ref, dgf_ref):
        i = pl.program_id(0)

        @pl.when(i == 0)
        def _():
            ls_ref[...] = jnp.zeros_like(ls_ref)
            dgf_ref[...] = jnp.zeros_like(dgf_ref)

        first = i == 0
        acc = x1_ref[...]
        for hf in range(2):
            cg = slice(hf * CCH, (hf + 1) * CCH)
            cv = slice(D_FF + hf * CCH, D_FF + (hf + 1) * CCH)
            vals = []
            for cs in (cg, cv):
                u0, u1, u2 = _conv_taps(u_ref, halo_ref, cs, first, ts)
                vals.append(bc_ref[:, cs] + wc_ref[0:1, cs] * u2 + wc_ref[1:2, cs] * u1 + wc_ref[2:3, cs] * u0)
            a = (vals[0] * _sigmoid(vals[0]) * vals[1]).astype(BF16)
            a_ref[:, cg] = a
            acc = acc + _dot(a, wd_ref[cg, :])
        r = lax.rsqrt(jnp.mean(acc * acc, axis=-1, keepdims=True) + EPS)
        xh = acc * r
        gf = gf_ref[...]
        err = xh * gf - t_ref[...]
        ls_ref[...] += (0.5 / D) * jnp.sum(jnp.sum(err * err, axis=-1, keepdims=True), axis=0, keepdims=True)
        dy = err * (1.0 / D)
        dgf_ref[...] += jnp.sum(dy * xh, axis=0, keepdims=True)
        dxh = dy * gf
        dx = r * (dxh - xh * jnp.mean(dxh * xh, axis=-1, keepdims=True))
        dx_ref[...] = dx
        dxb_ref[...] = dx.astype(BF16)

    row = lambda i: (i, 0)
    const2 = lambda i: (0, 0)
    return pl.pallas_call(
        body, name="ffn_down_loss", grid=(s // ts,),
        in_specs=[pl.BlockSpec((ts, N_UP), row),
                  pl.BlockSpec((HALO, N_UP), lambda i: (jnp.maximum(i * (ts // HALO) - 1, 0), 0)),
                  pl.BlockSpec((ts, D), row), pl.BlockSpec((ts, D), row), pl.BlockSpec((3, N_UP), const2),
                  pl.BlockSpec((1, N_UP), const2), pl.BlockSpec((D_FF, D), const2), pl.BlockSpec((1, D), const2)],
        out_specs=[pl.BlockSpec((ts, D_FF), row), pl.BlockSpec((ts, D), row), pl.BlockSpec((ts, D), row),
                   pl.BlockSpec((1, 128), const2), pl.BlockSpec((1, D), const2)],
        out_shape=[jax.ShapeDtypeStruct((s, D_FF), BF16), jax.ShapeDtypeStruct((s, D), F32),
                   jax.ShapeDtypeStruct((s, D), BF16), jax.ShapeDtypeStruct((1, 128), F32),
                   jax.ShapeDtypeStruct((1, D), F32)],
        compiler_params=_cp("arbitrary"),
    )(u, u, x1, tgt, wconv, bconv, wdown, gfin)


def _ffn_bwd(dx2b, u, wconv, bconv, wdown, ts):
    s = dx2b.shape[0]
    nt = s // ts

    def body(dx_ref, u_ref, halo_ref, wc_ref, bc_ref, wd_ref, du_ref, db_ref, dw_ref, nxt_ref):
        i = pl.program_id(0)

        @pl.when(i == 0)
        def _():
            db_ref[...] = jnp.zeros_like(db_ref)
            dw_ref[...] = jnp.zeros_like(dw_ref)
            nxt_ref[...] = jnp.zeros_like(nxt_ref)

        first = i == nt - 1
        dxv = dx_ref[...]
        row = _rows((ts, CCH))
        for hf in range(2):
            cg = slice(hf * CCH, (hf + 1) * CCH)
            cv = slice(D_FF + hf * CCH, D_FF + (hf + 1) * CCH)
            da = _dot_nt(dxv, wd_ref[cg, :])
            taps, vals = [], []
            for cs in (cg, cv):
                t3 = _conv_taps(u_ref, halo_ref, cs, first, ts)
                taps.append(t3)
                vals.append(bc_ref[:, cs] + wc_ref[0:1, cs] * t3[2] + wc_ref[1:2, cs] * t3[1] + wc_ref[2:3, cs] * t3[0])
            sg = _sigmoid(vals[0])
            dcs = (da * vals[1] * sg * (1.0 + vals[0] * (1.0 - sg)), da * vals[0] * sg)
            for cs, (u0, u1, u2), dc in zip((cg, cv), taps, dcs):
                db_ref[:, cs] += jnp.sum(dc, axis=0, keepdims=True)
                dw_ref[0:1, cs] += jnp.sum(dc * u2, axis=0, keepdims=True)
                dw_ref[1:2, cs] += jnp.sum(dc * u1, axis=0, keepdims=True)
                dw_ref[2:3, cs] += jnp.sum(dc * u0, axis=0, keepdims=True)
                n1 = nxt_ref[0:1, cs]
                n2 = nxt_ref[1:2, cs]
                f1 = jnp.where(row == ts - 1, n1, pltpu.roll(dc, ts - 1, 0))
                f2 = jnp.where(row == ts - 1, n2, jnp.where(row == ts - 2, n1, pltpu.roll(dc, ts - 2, 0)))
                du_ref[:, cs] = (wc_ref[2:3, cs] * dc + wc_ref[1:2, cs] * f1 + wc_ref[0:1, cs] * f2).astype(BF16)
                nxt_ref[:, cs] = dc[0:8, :]

    rev = lambda i: (nt - 1 - i, 0)
    const2 = lambda i: (0, 0)
    return pl.pallas_call(
        body, name="ffn_bwd", grid=(nt,),
        in_specs=[pl.BlockSpec((ts, D), rev), pl.BlockSpec((ts, N_UP), rev),
                  pl.BlockSpec((HALO, N_UP), lambda i: (jnp.maximum((nt - 1 - i) * (ts // HALO) - 1, 0), 0)),
                  pl.BlockSpec((3, N_UP), const2), pl.BlockSpec((1, N_UP), const2), pl.BlockSpec((D_FF, D), const2)],
        out_specs=[pl.BlockSpec((ts, N_UP), rev), pl.BlockSpec((1, N_UP), const2), pl.BlockSpec((3, N_UP), const2)],
        out_shape=[jax.ShapeDtypeStruct((s, N_UP), BF16), jax.ShapeDtypeStruct((1, N_UP), F32),
                   jax.ShapeDtypeStruct((3, N_UP), F32)],
        scratch_shapes=[pltpu.VMEM((8, N_UP), F32)],
        compiler_params=_cp("arbitrary"),
    )(dx2b, u, u, wconv, bconv, wdown)


ANY = pl.BlockSpec(memory_space=pl.ANY)


def _place():
    x, y, c = lax.axis_index("x"), lax.axis_index("y"), lax.axis_index("c")
    chips = [(1 - x, y), (x, 1 - y), (1 - x, 1 - y)]
    return x, y, c, chips


def _half(shape, c, axis):
    size = shape[axis] // 2
    cut = pl.ds(pl.multiple_of(c * size, 8 if axis == 0 else 128), size)
    return (cut, slice(None)) if axis == 0 else (slice(None), cut)


def _half_shape(shape, axis):
    return (shape[0] // 2, shape[1]) if axis == 0 else (shape[0], shape[1] // 2)


def _remote(src, dst, send_sems, recv_sems, k, to):
    return pltpu.make_async_remote_copy(src_ref=src, dst_ref=dst, send_sem=send_sems.at[k], recv_sem=recv_sems.at[k],
                                        device_id=to, device_id_type=MESH)


def _all_gather_weights(big, axes, small):
    nb, ns = len(big), len(small)
    n = nb + ns
    n_sem = 6 * nb + 3 * ns

    def body(*refs):
        ins, outs = refs[:n], refs[n:2 * n]
        send_sems, recv_sems = refs[2 * n:]
        x, y, c, chips = _place()
        me = 2 * x + y
        sib = (x, y, 1 - c)
        started = []
        for a in range(nb):
            mine = _half(big[a].shape, c, axes[a])
            for k, ch in enumerate(chips):
                cp = _remote(ins[a].at[mine], outs[a].at[(me,) + mine], send_sems, recv_sems, 6 * a + k,
                             (ch[0], ch[1], c))
                cp.start()
                started.append(cp)
        for a in range(ns):
            for k, ch in enumerate(chips):
                cp = _remote(ins[nb + a], outs[nb + a].at[me], send_sems, recv_sems, 6 * nb + 3 * a + k,
                             (ch[0], ch[1], c))
                cp.start()
                started.append(cp)
        for a in range(nb):
            mine = _half(big[a].shape, c, axes[a])
            for k, ch in enumerate(chips):
                landed = outs[a].at[(2 * ch[0] + ch[1],) + mine]
                _remote(landed, landed, send_sems, recv_sems, 6 * a + k, sib).wait_recv()
                cp = _remote(landed, landed, send_sems, recv_sems, 6 * a + 3 + k, sib)
                cp.start()
                started.append(cp)
        for a in range(nb):
            other = _half(big[a].shape, 1 - c, axes[a])
            for k, ch in enumerate(chips):
                landed = outs[a].at[(2 * ch[0] + ch[1],) + other]
                _remote(landed, landed, send_sems, recv_sems, 6 * a + 3 + k, sib).wait_recv()
        for a in range(ns):
            for k, ch in enumerate(chips):
                landed = outs[nb + a].at[2 * ch[0] + ch[1]]
                _remote(landed, landed, send_sems, recv_sems, 6 * nb + 3 * a + k, sib).wait_recv()
        for cp in started:
            cp.wait_send()

    arrs = list(big) + list(small)
    return pl.pallas_call(
        body, name="all_gather_weights",
        in_specs=[ANY] * n, out_specs=[ANY] * n,
        out_shape=[jax.ShapeDtypeStruct((4,) + a.shape, a.dtype) for a in arrs],
        scratch_shapes=[pltpu.SemaphoreType.DMA((n_sem,)), pltpu.SemaphoreType.DMA((n_sem,))],
        compiler_params=pltpu.CompilerParams(has_side_effects=True),
    )(*arrs)


def _sibling_exchange(grads, axes, small, name):
    nb = len(grads)
    n = nb + (small is not None)

    def body(*refs):
        ins, outs = refs[:n], refs[n:2 * n]
        send_sems, recv_sems = refs[2 * n:]
        x, y, c, _ = _place()
        sib = (x, y, 1 - c)
        cps = []
        for a in range(nb):
            theirs = _half(grads[a].shape[1:], 1 - c, axes[a])
            cps.append(_remote(ins[a].at[(slice(None),) + theirs], outs[a], send_sems, recv_sems, a, sib))
        if small is not None:
            cps.append(_remote(ins[nb], outs[nb], send_sems, recv_sems, nb, sib))
        for cp in cps:
            cp.start()
        for cp in cps:
            cp.wait()

    out_shape = [jax.ShapeDtypeStruct((4,) + _half_shape(g.shape[1:], ax), F32) for g, ax in zip(grads, axes)]
    if small is not None:
        out_shape.append(jax.ShapeDtypeStruct(small.shape, F32))
    return pl.pallas_call(
        body, name=name, in_specs=[ANY] * n, out_specs=[ANY] * n, out_shape=out_shape,
        scratch_shapes=[pltpu.SemaphoreType.DMA((n,)), pltpu.SemaphoreType.DMA((n,))],
        compiler_params=pltpu.CompilerParams(has_side_effects=True),
    )(*grads, *([] if small is None else [small]))


def _gather_share(lands, axes, name):
    n = len(lands)

    def body(*refs):
        outs = refs[n:2 * n]
        send_sems, recv_sems = refs[2 * n:]
        x, y, c, chips = _place()
        sib = (x, y, 1 - c)
        cps = []
        for a in range(n):
            mine = _half(lands[a].shape[1:], c, axes[a])
            for k, ch in enumerate(chips):
                landed = outs[a].at[(2 * ch[0] + ch[1],) + mine]
                cps.append(_remote(landed, landed, send_sems, recv_sems, 3 * a + k, sib))
        for cp in cps:
            cp.start()
        for a in range(n):
            other = _half(lands[a].shape[1:], 1 - c, axes[a])
            for k, ch in enumerate(chips):
                landed = outs[a].at[(2 * ch[0] + ch[1],) + other]
                _remote(landed, landed, send_sems, recv_sems, 3 * a + k, sib).wait_recv()
        for cp in cps:
            cp.wait_send()

    return pl.pallas_call(
        body, name=name, in_specs=[ANY] * n, out_specs=[ANY] * n,
        out_shape=[jax.ShapeDtypeStruct(a.shape, a.dtype) for a in lands],
        input_output_aliases={a: a for a in range(n)},
        scratch_shapes=[pltpu.SemaphoreType.DMA((3 * n,)), pltpu.SemaphoreType.DMA((3 * n,))],
        compiler_params=pltpu.CompilerParams(has_side_effects=True),
    )(*lands)


def _sibling_share(halves, name):
    n = len(halves)

    def body(*refs):
        ins, outs = refs[:n], refs[n:2 * n]
        send_sems, recv_sems = refs[2 * n:]
        x, y, c, _ = _place()
        cps = [_remote(ins[a], outs[a], send_sems, recv_sems, a, (x, y, 1 - c)) for a in range(n)]
        for cp in cps:
            cp.start()
        for cp in cps:
            cp.wait()

    return pl.pallas_call(
        body, name=name, in_specs=[ANY] * n, out_specs=[ANY] * n,
        out_shape=[jax.ShapeDtypeStruct(h.shape, F32) for h in halves],
        scratch_shapes=[pltpu.SemaphoreType.DMA((n,)), pltpu.SemaphoreType.DMA((n,))],
        compiler_params=pltpu.CompilerParams(has_side_effects=True),
    )(*halves)


HBM = pl.BlockSpec(memory_space=pltpu.HBM)
SEM = pl.BlockSpec(memory_space=pltpu.SEMAPHORE)
DATAFLOW = pltpu.SideEffectType.DATAFLOW_SIDE_EFFECTING


def _split_start(name, srcs, land_shapes, plan, n_copies, after):
    lands = [lax.empty(shp, dt) for shp, dt in land_shapes]
    bufs = list(srcs) + lands
    nb, ns = len(bufs), len(srcs)

    def body(*refs):
        send_sems, recv_sems, token = refs[nb + 1], refs[nb + 2], refs[-1]
        for k, (src, dst, to) in enumerate(plan(refs[:ns], refs[ns:nb])):
            _remote(src, dst, send_sems, recv_sems, k, to).start()
        token[...] = jnp.zeros_like(token)

    res = pl.pallas_call(
        body, name=name,
        out_shape=(pltpu.SemaphoreType.DMA((n_copies,)), pltpu.SemaphoreType.DMA((n_copies,)),
                   *[pltpu.HBM(b.shape, b.dtype) for b in bufs], jax.ShapeDtypeStruct((8, 128), F32)),
        in_specs=[HBM] * nb + [ANY],
        out_specs=(SEM, SEM, *[HBM] * nb, pl.BlockSpec(memory_space=pltpu.VMEM)),
        input_output_aliases={i: 2 + i for i in range(nb)},
        compiler_params=pltpu.CompilerParams(has_side_effects=DATAFLOW),
    )(*[pltpu.with_memory_space_constraint(b, pltpu.HBM) for b in bufs], after)
    return (res[0], res[1], list(res[2:2 + nb])), res[-1]


def _split_wait(name, handle, n_srcs, plan, after):
    send_sems, recv_sems, bufs = handle
    nb = len(bufs)

    def body(*refs):
        sends, recvs = refs[nb], refs[nb + 1]
        for k, (src, dst, to) in enumerate(plan(refs[:n_srcs], refs[n_srcs:nb])):
            cp = _remote(src, dst, sends, recvs, k, to)
            cp.wait_send()
            cp.wait_recv()

    res = pl.pallas_call(
        body, name=name, out_shape=[pltpu.HBM(b.shape, b.dtype) for b in bufs],
        in_specs=[HBM] * nb + [SEM, SEM, ANY], out_specs=[HBM] * nb,
        input_output_aliases={i: i for i in range(nb)},
        compiler_params=pltpu.CompilerParams(has_side_effects=DATAFLOW),
    )(*bufs, send_sems, recv_sems, after)
    return list(res[:n_srcs]), list(res[n_srcs:])


def _gather_plan(shapes, axes, n_whole=0):
    def plan(srcs, lands):
        x, y, c, chips = _place()
        out = []
        for a, (shape, axis) in enumerate(zip(shapes, axes)):
            mine = _half(shape, c, axis)
            for ch in chips:
                out.append((srcs[a].at[mine], lands[a].at[(2 * x + y,) + mine], (ch[0], ch[1], c)))
        for a in range(len(shapes), len(shapes) + n_whole):
            for ch in chips:
                out.append((srcs[a], lands[a].at[2 * x + y], (ch[0], ch[1], c)))
        return out
    return plan


def _sibling_plan(shapes, axes):
    def plan(srcs, lands):
        x, y, c, _ = _place()
        return [(srcs[a].at[(slice(None),) + _half(shape, 1 - c, axis)], lands[a], (x, y, 1 - c))
                for a, (shape, axis) in enumerate(zip(shapes, axes))]
    return plan


def _reduce_plan(n_big, with_small):
    def plan(srcs, lands):
        x, y, c, chips = _place()
        out = []
        for a in range(n_big):
            for k, ch in enumerate(chips):
                out.append((srcs[a].at[2 * ch[0] + ch[1]], lands[a].at[k], (ch[0], ch[1], c)))
        if with_small:
            for ch in chips:
                out.append((srcs[n_big], lands[n_big].at[2 * x + y], (ch[0], ch[1], c)))
        return out
    return plan


def _row_tile(rows, cols, mult):
    best = mult
    for t in range(mult, rows + 1, mult):
        if rows % t == 0 and t * cols * 4 <= (1 << 20):
            best = t
    return best if rows % best == 0 else rows


COL_TILE = 128


def _half_tiling(hshape, axis, mult):
    hr, hc = hshape
    if axis == 0:
        tr = _row_tile(hr, hc, mult)
        return tr, hc, hr // tr
    return hr, COL_TILE, hc // COL_TILE


def _tile_idx(axis, t):
    return (t, 0) if axis == 0 else (0, t)


def _chip_partial(place, g, t, axis, name):
    hshape = t.shape[1:]
    br, bc, nt = _half_tiling(hshape, axis, 16)

    def body(pl_ref, g_ref, t_ref, pf_ref, pb_ref):
        v = g_ref[...] + t_ref[...]
        pb_ref[...] = v.astype(BF16)

        @pl.when(pl.program_id(1) == pl_ref[0])
        def _():
            pf_ref[...] = v

    blk = (None, br, bc)
    return pl.pallas_call(
        body, name=name,
        grid_spec=pltpu.PrefetchScalarGridSpec(
            num_scalar_prefetch=1, grid=(nt, 4),
            in_specs=[pl.BlockSpec(blk, lambda i, j, p: (j,) + _tile_idx(axis, p[1] * nt + i)),
                      pl.BlockSpec(blk, lambda i, j, p: (j,) + _tile_idx(axis, i))],
            out_specs=[pl.BlockSpec((br, bc), lambda i, j, p: _tile_idx(axis, i)),
                       pl.BlockSpec(blk, lambda i, j, p: (j,) + _tile_idx(axis, i))]),
        out_shape=[jax.ShapeDtypeStruct(hshape, F32), jax.ShapeDtypeStruct((4,) + hshape, BF16)],
        compiler_params=_cp("arbitrary", "arbitrary"),
    )(place, g, t)


def _finish_half(pf, rb, axis, name):
    hshape = pf.shape
    br, bc, nt = _half_tiling(hshape, axis, 16)

    def body(pf_ref, rb_ref, o_ref):
        o_ref[...] = ((pf_ref[...] + rb_ref[0].astype(F32)) + rb_ref[1].astype(F32)) + rb_ref[2].astype(F32)

    return pl.pallas_call(
        body, name=name, grid=(nt,),
        in_specs=[pl.BlockSpec((br, bc), lambda i: _tile_idx(axis, i)),
                  pl.BlockSpec((3, br, bc), lambda i: (0,) + _tile_idx(axis, i))],
        out_specs=pl.BlockSpec((br, bc), lambda i: _tile_idx(axis, i)),
        out_shape=jax.ShapeDtypeStruct(hshape, F32),
        compiler_params=_cp("arbitrary"),
    )(pf, rb)


def _add2(a, b, name):
    def body(a_ref, b_ref, o_ref):
        o_ref[...] = a_ref[...] + b_ref[...]

    return pl.pallas_call(body, name=name, out_shape=jax.ShapeDtypeStruct(a.shape, F32))(a, b)


def _adam_math(w, g, m, v):
    m = ADAM_B1 * m + (1.0 - ADAM_B1) * g
    v = ADAM_B2 * v + (1.0 - ADAM_B2) * (g * g)
    m_hat = m / (1.0 - ADAM_B1 ** ADAM_STEP)
    v_hat = v / (1.0 - ADAM_B2 ** ADAM_STEP)
    return -ADAM_LR * (m_hat / (jnp.sqrt(v_hat) + ADAM_EPS) + ADAM_WD * w), m, v


def _adam_halves(place, w, mine, theirs, m, v, axis, name):
    br, bc, nt = _half_tiling(mine.shape, axis, 8)

    def body(pl_ref, w_ref, a_ref, b_ref, m_ref, v_ref, g_ref, d_ref, mo_ref, vo_ref):
        is_mine = pl.program_id(0) // nt == pl_ref[1]
        g = jnp.where(is_mine, a_ref[...], b_ref[...])
        d, mn, vn = _adam_math(w_ref[...], g, m_ref[...], v_ref[...])
        g_ref[...] = g
        d_ref[...] = d
        mo_ref[...] = mn
        vo_ref[...] = vn

    full = pl.BlockSpec((br, bc), lambda i, p: _tile_idx(axis, i))
    half = pl.BlockSpec((br, bc), lambda i, p: _tile_idx(axis, i % nt))
    return pl.pallas_call(
        body, name=name,
        grid_spec=pltpu.PrefetchScalarGridSpec(
            num_scalar_prefetch=1, grid=(2 * nt,), in_specs=[full, half, half, full, full], out_specs=[full] * 4),
        out_shape=[jax.ShapeDtypeStruct(w.shape, F32)] * 4, compiler_params=_cp("arbitrary"),
    )(place, w, mine, theirs, m, v)


def _adam_small(chip_sums, w, m, v):
    def body(s_ref, w_ref, m_ref, v_ref, g_ref, d_ref, mo_ref, vo_ref):
        g = ((s_ref[0] + s_ref[1]) + s_ref[2]) + s_ref[3]
        d, mn, vn = _adam_math(w_ref[...], g, m_ref[...], v_ref[...])
        g_ref[...] = g
        d_ref[...] = d
        mo_ref[...] = mn
        vo_ref[...] = vn

    return pl.pallas_call(body, name="adam_small", out_shape=[jax.ShapeDtypeStruct(w.shape, F32)] * 4)(chip_sums, w, m, v)


SMALL = (("g_mix", (1, 1024)), ("b_gate", (1, 2048)), ("w_gk_up", (1, 16, 512)), ("b_gk", (1, 512)),
         ("w_pool_grp", (1, 4, 128, 128)), ("pool_scale", (1, 512)), ("g_gla_head", (1, 256)), ("g_ffn", (1, 1024)),
         ("w_conv", (1, 3, 5632)), ("b_conv", (1, 5632)), ("g_final", (1024,)), ("loss", (768,)))
SMALL_ROWS = 808


def _pack_small(parts):
    flat = jnp.concatenate([parts[n].astype(F32).reshape(-1) for n, _ in SMALL])
    return flat.reshape(SMALL_ROWS, 128)


def _unpack_small(buf):
    flat = buf.reshape(-1)
    out, off = {}, 0
    for n, shp in SMALL:
        size = 1
        for d_ in shp:
            size *= d_
        out[n] = flat[off:off + size].reshape(shp)
        off += size
    return out


def kernel(x, g_mix, w_in, b_gate, w_gk_up, b_gk, w_pool_grp, pool_scale, g_gla_head, w_pool_proj, w_gla_proj, w_out, g_ffn, w_up, w_conv, b_conv, w_down, g_final, loss_target, m_g_mix, m_w_in, m_b_gate, m_w_gk_up, m_b_gk, m_w_pool_grp, m_pool_scale, m_g_gla_head, m_w_pool_proj, m_w_gla_proj, m_w_out, m_g_ffn, m_w_up, m_w_conv, m_b_conv, m_w_down, m_g_final, v_g_mix, v_w_in, v_b_gate, v_w_gk_up, v_b_gk, v_w_pool_grp, v_pool_scale, v_g_gla_head, v_w_pool_proj, v_w_gla_proj, v_w_out, v_g_ffn, v_w_up, v_w_conv, v_b_conv, v_w_down, v_g_final):
    s = x.shape[1]
    ts = min(s, 512)
    tm = min(s, 256)
    cx, cy, cc = lax.axis_index("x"), lax.axis_index("y"), lax.axis_index("c")
    chip = 2 * cx + cy
    place = jnp.stack([chip, cc]).astype(jnp.int32)

    big_names = ("w_in", "w_pool_proj", "w_gla_proj", "w_out", "w_up", "w_down")
    axes = (1, 0, 0, 0, 0, 0)
    shards = dict(w_in=jnp.transpose(w_in[0]), w_pool_proj=w_pool_proj[0], w_gla_proj=w_gla_proj[0], w_out=w_out[0],
                  w_up=w_up[0], w_down=w_down[0])
    def fill_own(lands, mine):
        return [lax.dynamic_update_slice(g, o_[None], (chip, 0, 0)) for g, o_ in zip(lands, mine)]

    def gather_start(tag, halves, group_axes, whole, after):
        plan = _gather_plan([o_.shape for o_ in halves], group_axes, len(whole))
        srcs = list(halves) + list(whole)
        handle, token = _split_start("gather_" + tag + "_start", srcs, [((4,) + o_.shape, o_.dtype) for o_ in srcs], plan,
                                     3 * len(srcs), after)
        return (handle, plan, len(halves), len(srcs), group_axes), token

    def gather_finish(tag, started, after):
        handle, plan, n_halves, n, group_axes = started
        mine, lands = _split_wait("gather_" + tag + "_wait", handle, n, plan, after)
        lands[:n_halves] = _gather_share(lands[:n_halves], group_axes, "gather_" + tag + "_share")
        return fill_own(lands, mine)

    def widen(a, width):
        z = jnp.zeros(a.shape[:-1] + (4, width), F32)
        return lax.dynamic_update_slice(z, a[..., None, :], (0,) * (a.ndim - 1) + (chip, 0)).reshape(a.shape[:-1] + (4 * width,))

    def small_of(g_mix, b_gate, w_gk_up, b_gk, w_pool_grp, pool_scale, g_gla_head, g_ffn, w_conv, b_conv, g_final):
        return _pack_small(dict(g_mix=g_mix, b_gate=b_gate, w_gk_up=widen(w_gk_up, 128), b_gk=b_gk, w_pool_grp=w_pool_grp,
                                pool_scale=pool_scale, g_gla_head=g_gla_head, g_ffn=g_ffn, w_conv=widen(w_conv, 1408),
                                b_conv=b_conv, g_final=g_final, loss=jnp.zeros((768,), F32)))

    in_w, tok = gather_start("in", [shards["w_in"].astype(BF16)], axes[:1], [], g_mix)
    zero = tok[0, 0]
    own = [(shards[n] + zero).astype(BF16) for n in big_names[1:]]
    mix_w, tok = gather_start("mix", own[0:3], axes[1:4], [w_gk_up[0] + zero, w_conv[0] + zero], tok)
    ffn_w, tok = gather_start("ffn", own[3:5], axes[4:6], [], tok)
    zero = tok[0, 0]
    sw = small_of(g_mix + zero, b_gate, w_gk_up, b_gk, w_pool_grp, pool_scale, g_gla_head, g_ffn, w_conv, b_conv, g_final)
    sm = small_of(m_g_mix + zero, m_b_gate, m_w_gk_up, m_b_gk, m_w_pool_grp, m_pool_scale, m_g_gla_head, m_g_ffn,
                  m_w_conv, m_b_conv, m_g_final)
    sv = small_of(v_g_mix + zero, v_b_gate, v_w_gk_up, v_b_gk, v_w_pool_grp, v_pool_scale, v_g_gla_head, v_g_ffn,
                  v_w_conv, v_b_conv, v_g_final)
    w_in_t = gather_finish("in", in_w, sv)[0].reshape(N_IN, D)
    w_rt = jnp.concatenate([w_in_t[3600:], w_in_t[1536:3584], w_in_t[0:1536], w_in_t[3584:3600],
                            jnp.zeros((128 - GATE_RANK, D), BF16)], axis=0)
    nsh = N_IN // 4

    xs, tgt = x[0], loss_target[0]
    wgrp = w_pool_grp[0]

    h = _rmsnorm(xs, g_mix, "norm_mix", ts)
    zr = _matmul_resident(h, w_rt, "in_proj", 1152, transposed=True)
    p, pp = _pool_fwd(zr, wgrp, pool_scale)
    wpp, wgla, wout, wgk4, wconv4 = gather_finish("mix", mix_w, pp)
    wgla, wout = wgla.reshape(D, D), wout.reshape(D, D)
    wgk_full = jnp.transpose(wgk4, (1, 0, 2)).reshape(GATE_RANK, 512)
    wconv_full = jnp.transpose(wconv4, (1, 0, 2)).reshape(3, N_UP)
    wgk_pad = jnp.concatenate([wgk_full, jnp.zeros((128 - GATE_RANK, 512), F32)], axis=0)
    o, og, sp = _gla_fwd(zr, wgk_pad, b_gk, g_gla_head, tm)
    x1, mixed, yp, yg = _merge_fwd(xs, zr, pp, og, b_gate, wpp, wgla, wout, tm)
    wup, wdown = gather_finish("ffn", ffn_w, x1)
    wdown = wdown.reshape(D_FF, D)
    h2 = _rmsnorm(x1, g_ffn, "norm_ffn", ts)
    u = _matmul_resident(h2, wup, "ffn_up", None)
    a, dx2, dx2b, loss_part, dgfin = _ffn_down_loss(u, x1, tgt, wconv_full, b_conv, wdown, g_final.reshape(1, D), tm)

    du, dbconv, dwconv = _ffn_bwd(dx2b, u, wconv_full, b_conv, wdown, tm)
    dw_down = _matmul_tn(a, dx2b, "dw_down", D, s, tm=1408)
    dw_up = _matmul_tn(h2, du, "dw_up", 1408, s, shard_major=True)

    def exchange_start(tag, grads, group_axes, after):
        plan = _sibling_plan([g.shape[1:] for g in grads], group_axes)
        lands = [((4,) + _half_shape(g.shape[1:], ax), F32) for g, ax in zip(grads, group_axes)]
        handle, token = _split_start("sibling_" + tag + "_start", grads, lands, plan, len(grads), after)
        return (handle, plan, len(grads)), token

    def partials(tag, names, group_axes, exchange, after):
        handle, plan, n = exchange
        mine, theirs = _split_wait("sibling_" + tag + "_wait", handle, n, plan, after)
        return zip(*[_chip_partial(place, g, t, ax, "chip_partial_" + nm)
                     for nm, ax, g, t in zip(names, group_axes, mine, theirs)])

    ffn_names, ffn_axes = ("w_up", "w_down"), (0, 0)
    ffn_x, token = exchange_start("ffn", [dw_up, dw_down.reshape(4, 704, D)], ffn_axes, du)
    dx1, dx1b, dgffn = _matmul_nt_normbwd(du, wup, x1, g_ffn + token[0:1, 0:1], dx2, "ffn_up_bwd", ts)
    ffn_pf, ffn_pb = partials("ffn", ffn_names, ffn_axes, ffn_x, dx1b)
    ffn_plan = _reduce_plan(2, False)
    ffn_handle, token = _split_start("reduce_ffn_start", ffn_pb, [((3,) + p.shape[1:], BF16) for p in ffn_pb],
                                     ffn_plan, 6, ffn_pf[0])

    dzg, dyp, dyg, dpp, do, dzog, dbgate, dghead = _merge_bwd(dx1b, zr, yp, yg, o, b_gate + token[0:1, 0:1], g_gla_head,
                                                             wpp, wgla, wout, tm)
    dw_out = _matmul_tn(mixed, dx1b, "dw_out", D, s)
    dw_gla = _matmul_tn(og, dyg, "dw_gla", D, s)
    dw_pp = _matmul_tn(pp, dyp, "dw_pp", 256, s, shard_major=True)

    out_names, out_axes = ("w_pool_proj", "w_gla_proj", "w_out"), (0, 0, 0)
    out_x, token = exchange_start("out", [dw_pp, dw_gla.reshape(4, 256, D), dw_out.reshape(4, 256, D)], out_axes, dpp)
    dzp, dwgrp, dscale = _pool_bwd(p, dpp, wgrp, pool_scale + token[0:1, 0:1])
    out_pf, out_pb = partials("out", out_names, out_axes, out_x, dzp)
    out_plan = _reduce_plan(3, False)
    out_handle, token = _split_start("reduce_out_start", out_pb, [((3,) + p_.shape[1:], BF16) for p_ in out_pb],
                                     out_plan, 9, out_pf[0])
    dq, dk, dv, dgpre = _gla_bwd(zr, do, sp, wgk_pad, b_gk + token[0:1, 0:1], tm)
    dzgk, dwgk, dbgk = _gk_bwd(dgpre, zr, wgk_pad, ts)
    dzr = jnp.concatenate([dzg, dv, dzog, dzp, dq, dk, dzgk], axis=1)
    dw_rt = _matmul_tn(dzr, h, "dw_in", D, s, tm=1152)

    def grad_rows(lo, hi):
        out = []
        for seg_lo, seg_hi, at in ((0, 1536, OFF_POOL), (1536, 3584, OFF_V), (3584, 3600, OFF_GK), (3600, N_IN, OFF_GATE)):
            a_, b_ = max(lo, seg_lo), min(hi, seg_hi)
            if a_ < b_:
                out.append(dw_rt[at + a_ - seg_lo:at + b_ - seg_lo])
        return jnp.concatenate(out, axis=0)

    dw_in_t = jnp.stack([grad_rows(j * nsh, (j + 1) * nsh) for j in range(4)])

    in_x, token = exchange_start("in", [dw_in_t], (1,), dzr)
    grad_x, _, dgmix = _matmul_nt_normbwd(dzr, w_rt, xs, g_mix + token[0:1, 0:1], dx1, "in_proj_bwd", ts, transposed=True)
    small_mine = _pack_small(dict(
        g_mix=dgmix, b_gate=dbgate, w_gk_up=dwgk[:GATE_RANK], b_gk=dbgk, w_pool_grp=dwgrp, pool_scale=dscale,
        g_gla_head=dghead, g_ffn=dgffn, w_conv=dwconv, b_conv=dbconv, g_final=dgfin,
        loss=jnp.concatenate([loss_part.reshape(128), jnp.zeros((640,), F32)])))
    in_pf, in_pb = partials("in", ("w_in",), (1,), in_x, grad_x)
    small_sib = _sibling_exchange([], (), small_mine, "sibling_exchange_small")[0]
    small_chip = _add2(small_mine, small_sib, "chip_partial_small")
    in_plan = _reduce_plan(1, True)
    in_handle, token = _split_start(
        "reduce_in_start", [in_pb[0], small_chip],
        [((3,) + in_pb[0].shape[1:], BF16), ((4,) + small_chip.shape, F32)], in_plan, 6, in_pf[0])

    ms = dict(w_in=jnp.transpose(m_w_in[0]), w_pool_proj=m_w_pool_proj[0], w_gla_proj=m_w_gla_proj[0], w_out=m_w_out[0],
              w_up=m_w_up[0], w_down=m_w_down[0])
    vs = dict(w_in=jnp.transpose(v_w_in[0]), w_pool_proj=v_w_pool_proj[0], w_gla_proj=v_w_gla_proj[0], w_out=v_w_out[0],
              w_up=v_w_up[0], w_down=v_w_down[0])
    grad, delta, new_m, new_v = {}, {}, {}, {}

    def finish_and_update(names, group_axes, part_f, landed, tag):
        halves = [_finish_half(pf, rb, ax, "finish_" + n) for n, ax, pf, rb in zip(names, group_axes, part_f, landed)]
        sib_halves = _sibling_share(halves, "sibling_share_" + tag)
        for n, ax, mine, theirs in zip(names, group_axes, halves, sib_halves):
            res = _adam_halves(place, shards[n], mine, theirs, ms[n], vs[n], ax, "adam_" + n)
            if n == "w_in":
                res = [jnp.transpose(r_) for r_ in res]
            grad[n], delta[n], new_m[n], new_v[n] = [r_[None] for r_ in res]

    _, ffn_landed = _split_wait("reduce_ffn_wait", ffn_handle, 2, ffn_plan, token)
    _, out_landed = _split_wait("reduce_out_wait", out_handle, 3, out_plan, ffn_landed[0])
    finish_and_update(ffn_names + out_names, ffn_axes + out_axes, ffn_pf + out_pf, ffn_landed + out_landed, "rest")
    in_sent, in_landed = _split_wait("reduce_in_wait", in_handle, 2, in_plan, delta["w_out"])
    small_sums = lax.dynamic_update_slice(in_landed[1], in_sent[1][None], (chip, 0, 0))
    finish_and_update(("w_in",), (1,), in_pf, in_landed[:1], "in")
    sg, sd, smo, svo = _adam_small(small_sums, sw, sm, sv)

    def narrow(a, width):
        return lax.dynamic_slice_in_dim(a.reshape(a.shape[:-1] + (4, width)), chip, 1, axis=a.ndim - 1).reshape(
            a.shape[:-1] + (width,))

    loss = None
    for dst, buf in ((grad, sg), (delta, sd), (new_m, smo), (new_v, svo)):
        parts = _unpack_small(buf)
        if dst is grad:
            loss = parts["loss"][0]
        for n, _ in SMALL[:-1]:
            val = parts[n]
            if n == "w_gk_up":
                val = narrow(val, 128)
            elif n == "w_conv":
                val = narrow(val, 1408)
            dst[n] = val

    order = ("g_mix", "w_in", "b_gate", "w_gk_up", "b_gk", "w_pool_grp", "pool_scale", "g_gla_head", "w_pool_proj",
             "w_gla_proj", "w_out", "g_ffn", "w_up", "w_conv", "b_conv", "w_down", "g_final")
    return (loss, grad_x[None], *[grad[n] for n in order], *[delta[n] for n in order], *[new_m[n] for n in order],
            *[new_v[n] for n in order])
```

```python
import functools

import jax
import jax.numpy as jnp
from jax import lax
from jax.experimental import pallas as pl
from jax.experimental.pallas import tpu as pltpu

F32 = jnp.float32
BF16 = jnp.bfloat16
MESH = pl.DeviceIdType.MESH

D = 1024
EPS = 1e-6
CHUNK = 64
POOL_W = 512
POOL_WINDOWS = (2, 4, 8, 16)
HEADS = 4
HK = 128
HV = 256
GATE_RANK = 16
D_FF = 2816
N_UP = 2 * D_FF
N_IN = 5648
QSCALE = HK ** -0.5
N_INR = 5760
OFF_GATE, OFF_V, OFF_OG, OFF_POOL, OFF_Q, OFF_K, OFF_GK = 0, 2048, 3072, 4096, 4608, 5120, 5632

ADAM_LR, ADAM_B1, ADAM_B2, ADAM_EPS, ADAM_WD, ADAM_STEP = 0.001, 0.9, 0.999, 1e-08, 0.01, 10

VMEM_LIMIT = 56 * 1024 * 1024


def _cp(*sem):
    return pltpu.CompilerParams(dimension_semantics=sem if sem else None, vmem_limit_bytes=VMEM_LIMIT)


def _dot(a, b):
    return jnp.dot(a, b, preferred_element_type=F32)


def _dot_nt(a, b):
    return lax.dot_general(a, b, (((1,), (1,)), ((), ())), preferred_element_type=F32)


def _dot_tn(a, b):
    return lax.dot_general(a, b, (((0,), (0,)), ((), ())), preferred_element_type=F32)


def _sigmoid(v):
    return 1.0 / (1.0 + jnp.exp(-v))


def _rows(shape):
    return lax.broadcasted_iota(jnp.int32, shape, 0)


def _pick_row(v, r):
    return jnp.sum(jnp.where(_rows(v.shape) == r, v, 0.0), axis=0, keepdims=True)


def _rmsnorm(x, g, name, ts):
    s = x.shape[0]

    def body(x_ref, g_ref, h_ref):
        xv = x_ref[...]
        r = lax.rsqrt(jnp.mean(xv * xv, axis=-1, keepdims=True) + EPS)
        h_ref[...] = (xv * r * g_ref[...]).astype(BF16)

    return pl.pallas_call(
        body, name=name, grid=(s // ts,),
        in_specs=[pl.BlockSpec((ts, D), lambda i: (i, 0)), pl.BlockSpec((1, D), lambda i: (0, 0))],
        out_specs=pl.BlockSpec((ts, D), lambda i: (i, 0)), out_shape=jax.ShapeDtypeStruct((s, D), BF16),
        compiler_params=_cp("arbitrary"),
    )(x, g)


MM_ROWS = 512


def _matmul_resident(h, w, name, tn, transposed=False):
    s = h.shape[0]
    if transposed:
        nj = w.shape[0] // tn
        w_spec = pl.BlockSpec((tn, D), lambda j: (j, 0))
    elif w.ndim == 3:
        nj, tn = w.shape[0], w.shape[2]
        w_spec = pl.BlockSpec((None, D, tn), lambda j: (j, 0, 0))
    else:
        nj = w.shape[1] // tn
        w_spec = pl.BlockSpec((D, tn), lambda j: (0, j))
    mm = _dot_nt if transposed else _dot
    rc = min(s, MM_ROWS)

    def body(h_ref, w_ref, z_ref):
        for r0 in range(0, s, rc):
            z_ref[r0:r0 + rc, :] = mm(h_ref[r0:r0 + rc, :], w_ref[...]).astype(BF16)

    return pl.pallas_call(
        body, name=name, grid=(nj,),
        in_specs=[pl.BlockSpec((s, D), lambda j: (0, 0)), w_spec],
        out_specs=pl.BlockSpec((s, tn), lambda j: (0, j)), out_shape=jax.ShapeDtypeStruct((s, nj * tn), BF16),
        compiler_params=_cp("arbitrary"),
    )(h, w)


def _matmul_nt_normbwd(dz, w, x, g, resid, name, ts, transposed=False):
    s = x.shape[0]

    def body(dz_ref, w_hbm, x_ref, g_ref, r_ref, o_ref, ob_ref, dg_ref, w_ref, sem):
        @pl.when(pl.program_id(0) == 0)
        def _():
            cp = pltpu.make_async_copy(w_hbm, w_ref, sem)
            cp.start()
            cp.wait()
            dg_ref[...] = jnp.zeros_like(dg_ref)

        if transposed:
            dh = _dot(dz_ref[...], w_ref[...])
        else:
            kc = w.shape[2]
            dh = _dot_nt(dz_ref[:, 0:kc], w_ref[0])
            for j in range(1, w.shape[0]):
                dh = dh + _dot_nt(dz_ref[:, j * kc:(j + 1) * kc], w_ref[j])
        xv = x_ref[...]
        r = lax.rsqrt(jnp.mean(xv * xv, axis=-1, keepdims=True) + EPS)
        xh = xv * r
        dg_ref[...] += jnp.sum(dh * xh, axis=0, keepdims=True)
        dxh = dh * g_ref[...]
        out = r_ref[...] + r * (dxh - xh * jnp.mean(dxh * xh, axis=-1, keepdims=True))
        o_ref[...] = out
        ob_ref[...] = out.astype(BF16)

    row = lambda i: (i, 0)
    kdim = dz.shape[1]
    return pl.pallas_call(
        body, name=name, grid=(s // ts,),
        in_specs=[pl.BlockSpec((ts, kdim), row), ANY, pl.BlockSpec((ts, D), row),
                  pl.BlockSpec((1, D), lambda i: (0, 0)), pl.BlockSpec((ts, D), row)],
        out_specs=[pl.BlockSpec((ts, D), row), pl.BlockSpec((ts, D), row), pl.BlockSpec((1, D), lambda i: (0, 0))],
        out_shape=[jax.ShapeDtypeStruct((s, D), F32), jax.ShapeDtypeStruct((s, D), BF16),
                   jax.ShapeDtypeStruct((1, D), F32)],
        scratch_shapes=[pltpu.VMEM(w.shape, BF16), pltpu.SemaphoreType.DMA],
        compiler_params=_cp("arbitrary"),
    )(dz, w, x, g, resid)


def _matmul_tn(a, b, name, tn, tk, shard_major=False, tm=None):
    s, m = a.shape
    n = b.shape[1]
    tm = m if tm is None else tm
    ni, nj, nk = m // tm, n // tn, s // tk

    def body(a_ref, b_ref, o_ref):
        if nk == 1:
            o_ref[...] = _dot_tn(a_ref[...], b_ref[...])
            return

        @pl.when(pl.program_id(2) == 0)
        def _():
            o_ref[...] = jnp.zeros_like(o_ref)

        o_ref[...] += _dot_tn(a_ref[...], b_ref[...])

    if shard_major:
        out_spec = pl.BlockSpec((None, tm, tn), lambda i, j, k: (j, i, 0))
        out_shape = jax.ShapeDtypeStruct((nj, m, tn), F32)
    else:
        out_spec = pl.BlockSpec((tm, tn), lambda i, j, k: (i, j))
        out_shape = jax.ShapeDtypeStruct((m, n), F32)
    return pl.pallas_call(
        body, name=name, grid=(ni, nj, nk),
        in_specs=[pl.BlockSpec((tk, tm), lambda i, j, k: (k, i)), pl.BlockSpec((tk, tn), lambda i, j, k: (k, j))],
        out_specs=out_spec, out_shape=out_shape,
        compiler_params=_cp("arbitrary", "arbitrary", "arbitrary"),
    )(a, b)


def _pool_fwd(zr, wgrp, scale):
    s = zr.shape[0]

    def body(u_ref, w_ref, sc_ref, p_ref, pp_ref):
        row = _rows((s, 128))
        for gi, win in enumerate(POOL_WINDOWS):
            cs = slice(gi * 128, (gi + 1) * 128)
            u = u_ref[:, cs].astype(F32)
            acc, k = u, 1
            while k < win:
                acc = acc + jnp.where(row >= k, pltpu.roll(acc, k, 0), 0.0)
                k *= 2
            cnt = jnp.minimum(row + 1, win).astype(F32)
            p = (acc / cnt - u).astype(BF16)
            p_ref[:, cs] = p
            pp_ref[:, cs] = (_dot(p, w_ref[gi].astype(BF16)) * sc_ref[:, cs]).astype(BF16)

    return pl.pallas_call(
        body, name="pool_fwd", grid=(1,),
        in_specs=[pl.BlockSpec((s, POOL_W), lambda i: (0, OFF_POOL // POOL_W)),
                  pl.BlockSpec((4, 128, 128), lambda i: (0, 0, 0)), pl.BlockSpec((1, POOL_W), lambda i: (0, 0))],
        out_specs=[pl.BlockSpec((s, POOL_W), lambda i: (0, 0))] * 2,
        out_shape=[jax.ShapeDtypeStruct((s, POOL_W), BF16)] * 2,
        compiler_params=_cp("arbitrary"),
    )(zr, wgrp, scale)


def _pool_bwd(p, dpp, wgrp, scale):
    s = p.shape[0]

    def body(p_ref, dpp_ref, w_ref, sc_ref, dz_ref, dw_ref, dsc_ref):
        row = _rows((s, 128))
        for gi, win in enumerate(POOL_WINDOWS):
            cs = slice(gi * 128, (gi + 1) * 128)
            pv = p_ref[:, cs]
            wb = w_ref[gi].astype(BF16)
            dpp_v = dpp_ref[:, cs].astype(F32)
            dsc_ref[:, cs] = jnp.sum(dpp_v * _dot(pv, wb), axis=0, keepdims=True)
            dpm = (dpp_v * sc_ref[:, cs]).astype(BF16)
            dw_ref[gi] = _dot_tn(pv, dpm)
            dp = _dot_nt(dpm, wb)
            cnt = jnp.minimum(row + 1, win).astype(F32)
            acc, k = dp / cnt, 1
            while k < win:
                acc = acc + jnp.where(row < s - k, pltpu.roll(acc, s - k, 0), 0.0)
                k *= 2
            dz_ref[:, cs] = (acc - dp).astype(BF16)

    full = lambda i: (0, 0)
    return pl.pallas_call(
        body, name="pool_bwd", grid=(1,),
        in_specs=[pl.BlockSpec((s, POOL_W), full), pl.BlockSpec((s, POOL_W), full),
                  pl.BlockSpec((4, 128, 128), lambda i: (0, 0, 0)), pl.BlockSpec((1, POOL_W), full)],
        out_specs=[pl.BlockSpec((s, POOL_W), full), pl.BlockSpec((4, 128, 128), lambda i: (0, 0, 0)),
                   pl.BlockSpec((1, POOL_W), full)],
        out_shape=[jax.ShapeDtypeStruct((s, POOL_W), BF16), jax.ShapeDtypeStruct((4, 128, 128), F32),
                   jax.ShapeDtypeStruct((1, POOL_W), F32)],
        compiler_params=_cp("arbitrary"),
    )(p, dpp, wgrp, scale)


def _gla_decay(zgk_ref, wgk_ref, bgk_ref, rb):
    g = _dot(zgk_ref[...], wgk_ref[...].astype(BF16)) + bgk_ref[...]
    la = (jnp.minimum(g, 0.0) - jnp.log(1.0 + jnp.exp(-jnp.abs(g)))) * (1.0 / 16.0)
    rowm = _rows((rb, HK)) & (CHUNK - 1)
    bc, k = la, 1
    while k < CHUNK:
        bc = bc + jnp.where(rowm >= k, pltpu.roll(bc, k, 0), 0.0)
        k *= 2
    return g, jnp.exp(bc), jnp.exp(-bc)


def _gla_specs(rb, rmap):
    return [pl.BlockSpec((rb, HK), lambda h, r: (rmap(h, r), OFF_Q // HK + h)),
            pl.BlockSpec((rb, HK), lambda h, r: (rmap(h, r), OFF_K // HK + h)),
            pl.BlockSpec((rb, HV), lambda h, r: (rmap(h, r), OFF_V // HV + h)),
            pl.BlockSpec((rb, 128), lambda h, r: (rmap(h, r), OFF_GK // 128))]


def _gla_fwd(zr, wgk, bgk, ghead, rb):
    s = zr.shape[0]
    nc = rb // CHUNK

    def body(q_ref, k_ref, v_ref, zgk_ref, zog_ref, wgk_ref, bgk_ref, gh_ref, o_ref, og_ref, sp_ref, st_ref):
        @pl.when(pl.program_id(1) == 0)
        def _():
            st_ref[...] = jnp.zeros_like(st_ref)

        _, e_pos, e_neg = _gla_decay(zgk_ref, wgk_ref, bgk_ref, rb)
        lower = _rows((CHUNK, CHUNK)) >= lax.broadcasted_iota(jnp.int32, (CHUNK, CHUNK), 1)
        for c in range(nc):
            sl = slice(c * CHUNK, (c + 1) * CHUNK)
            q = q_ref[sl, :].astype(F32) * QSCALE
            k = k_ref[sl, :].astype(F32)
            v = v_ref[sl, :]
            ec, fc = e_pos[sl], e_neg[sl]
            qfw = (q * ec).astype(BF16)
            kfw_f = k * fc
            s_fw = _dot_nt(qfw, kfw_f.astype(BF16))
            s_bw = _dot_nt((q * fc).astype(BF16), (k * ec).astype(BF16))
            pm = jnp.where(lower, s_fw, s_bw).astype(BF16)
            st = st_ref[...]
            stb = st.astype(BF16)
            sp_ref[c] = stb
            o = _dot(pm, v) + _dot_nt(qfw, stb)
            e_last = _pick_row(ec, CHUNK - 1)
            kdec = (kfw_f * e_last).astype(BF16)
            st_ref[...] = st * e_last + _dot_tn(v, kdec)
            r = lax.rsqrt(jnp.mean(o * o, axis=-1, keepdims=True) + EPS)
            zo = zog_ref[sl, :].astype(F32)
            o_ref[sl, :] = o.astype(BF16)
            og_ref[sl, :] = (o * r * gh_ref[...] * zo * _sigmoid(zo)).astype(BF16)

    rmap = lambda h, r: r
    return pl.pallas_call(
        body, name="gla_fwd", grid=(HEADS, s // rb),
        in_specs=_gla_specs(rb, rmap) + [
            pl.BlockSpec((rb, HV), lambda h, r: (r, OFF_OG // HV + h)),
            pl.BlockSpec((128, HK), lambda h, r: (0, h)), pl.BlockSpec((1, HK), lambda h, r: (0, h)),
            pl.BlockSpec((1, HV), lambda h, r: (0, 0))],
        out_specs=[pl.BlockSpec((rb, HV), lambda h, r: (r, h)), pl.BlockSpec((rb, HV), lambda h, r: (r, h)),
                   pl.BlockSpec((nc, None, HV, HK), lambda h, r: (r, h, 0, 0))],
        out_shape=[jax.ShapeDtypeStruct((s, D), BF16), jax.ShapeDtypeStruct((s, D), BF16),
                   jax.ShapeDtypeStruct((s // CHUNK, HEADS, HV, HK), BF16)],
        scratch_shapes=[pltpu.VMEM((HV, HK), F32)],
        compiler_params=_cp("arbitrary", "arbitrary"),
    )(zr, zr, zr, zr, zr, wgk, bgk, ghead)


def _gla_bwd(zr, do, sp, wgk, bgk, rb):
    s = zr.shape[0]
    nc = rb // CHUNK
    nr = s // rb

    def body(q_ref, k_ref, v_ref, zgk_ref, do_ref, sp_ref, wgk_ref, bgk_ref, dq_ref, dk_ref, dv_ref, dg_ref,
             gt_ref, dbc_ref):
        @pl.when(pl.program_id(1) == 0)
        def _():
            gt_ref[...] = jnp.zeros_like(gt_ref)

        g, e_pos, e_neg = _gla_decay(zgk_ref, wgk_ref, bgk_ref, rb)
        lower = _rows((CHUNK, CHUNK)) >= lax.broadcasted_iota(jnp.int32, (CHUNK, CHUNK), 1)
        is_last = _rows((CHUNK, HK)) == CHUNK - 1
        for c in reversed(range(nc)):
            sl = slice(c * CHUNK, (c + 1) * CHUNK)
            q = q_ref[sl, :].astype(F32) * QSCALE
            k = k_ref[sl, :].astype(F32)
            v = v_ref[sl, :]
            dov = do_ref[sl, :]
            ec, fc = e_pos[sl], e_neg[sl]
            qfw_f, kfw_f, qbw_f, kbw_f = q * ec, k * fc, q * fc, k * ec
            qfw, kfw, qbw, kbw = qfw_f.astype(BF16), kfw_f.astype(BF16), qbw_f.astype(BF16), kbw_f.astype(BF16)
            pm = jnp.where(lower, _dot_nt(qfw, kfw), _dot_nt(qbw, kbw)).astype(BF16)
            e_last = _pick_row(ec, CHUNK - 1)
            kdec = (kfw_f * e_last).astype(BF16)
            gt = gt_ref[...]
            gtb = gt.astype(BF16)
            spv = sp_ref[c]
            dp = _dot_nt(dov, v)
            dv_ref[sl, :] = (_dot_tn(pm, dov) + _dot_nt(kdec, gtb)).astype(BF16)
            ds_fw = jnp.where(lower, dp, 0.0).astype(BF16)
            ds_bw = jnp.where(lower, 0.0, dp).astype(BF16)
            dqfw = _dot(ds_fw, kfw) + _dot(dov, spv)
            dkfw = _dot_tn(ds_fw, qfw)
            dqbw = _dot(ds_bw, kbw)
            dkbw = _dot_tn(ds_bw, qbw)
            dkdec = _dot(v, gtb)
            de_last = (jnp.sum(gt * spv.astype(F32), axis=0, keepdims=True)
                       + jnp.sum(dkdec * kfw_f, axis=0, keepdims=True))
            dkfw = dkfw + dkdec * e_last
            dq_ref[sl, :] = ((dqfw * ec + dqbw * fc) * QSCALE).astype(BF16)
            dk_ref[sl, :] = (dkfw * fc + dkbw * ec).astype(BF16)
            dbc = dqfw * qfw_f - dqbw * qbw_f + dkbw * kbw_f - dkfw * kfw_f
            dbc_ref[sl, :] = dbc + jnp.where(is_last, de_last * e_last, 0.0)
            gt_ref[...] = _dot_tn(dov, qfw) + gt * e_last
        rowm = _rows((rb, HK)) & (CHUNK - 1)
        dla, kk = dbc_ref[...], 1
        while kk < CHUNK:
            dla = dla + jnp.where(rowm < CHUNK - kk, pltpu.roll(dla, rb - kk, 0), 0.0)
            kk *= 2
        dg_ref[...] = dla * (1.0 / 16.0) * _sigmoid(-g)

    rmap = lambda h, r: nr - 1 - r
    return pl.pallas_call(
        body, name="gla_bwd", grid=(HEADS, nr),
        in_specs=_gla_specs(rb, rmap) + [
            pl.BlockSpec((rb, HV), lambda h, r: (nr - 1 - r, h)),
            pl.BlockSpec((nc, None, HV, HK), lambda h, r: (nr - 1 - r, h, 0, 0)),
            pl.BlockSpec((128, HK), lambda h, r: (0, h)), pl.BlockSpec((1, HK), lambda h, r: (0, h))],
        out_specs=[pl.BlockSpec((rb, HK), lambda h, r: (nr - 1 - r, h)), pl.BlockSpec((rb, HK), lambda h, r: (nr - 1 - r, h)),
                   pl.BlockSpec((rb, HV), lambda h, r: (nr - 1 - r, h)), pl.BlockSpec((rb, HK), lambda h, r: (nr - 1 - r, h))],
        out_shape=[jax.ShapeDtypeStruct((s, HEADS * HK), BF16), jax.ShapeDtypeStruct((s, HEADS * HK), BF16),
                   jax.ShapeDtypeStruct((s, D), BF16), jax.ShapeDtypeStruct((s, HEADS * HK), F32)],
        scratch_shapes=[pltpu.VMEM((HV, HK), F32), pltpu.VMEM((rb, HK), F32)],
        compiler_params=_cp("arbitrary", "arbitrary"),
    )(zr, zr, zr, zr, do, sp, wgk, bgk)


def _gk_bwd(dgpre, zr, wgk, ts):
    s = zr.shape[0]

    def body(dg_ref, zgk_ref, w_ref, dz_ref, dw_ref, db_ref):
        @pl.when(pl.program_id(0) == 0)
        def _():
            dw_ref[...] = jnp.zeros_like(dw_ref)
            db_ref[...] = jnp.zeros_like(db_ref)

        dg = dg_ref[...]
        dgb = dg.astype(BF16)
        dz_ref[...] = _dot_nt(dgb, w_ref[...].astype(BF16)).astype(BF16)
        dw_ref[...] += _dot_tn(zgk_ref[...], dgb)
        db_ref[...] += jnp.sum(dg, axis=0, keepdims=True)

    return pl.pallas_call(
        body, name="gk_bwd", grid=(s // ts,),
        in_specs=[pl.BlockSpec((ts, 512), lambda i: (i, 0)), pl.BlockSpec((ts, 128), lambda i: (i, OFF_GK // 128)),
                  pl.BlockSpec((128, 512), lambda i: (0, 0))],
        out_specs=[pl.BlockSpec((ts, 128), lambda i: (i, 0)), pl.BlockSpec((128, 512), lambda i: (0, 0)),
                   pl.BlockSpec((1, 512), lambda i: (0, 0))],
        out_shape=[jax.ShapeDtypeStruct((s, 128), BF16), jax.ShapeDtypeStruct((128, 512), F32),
                   jax.ShapeDtypeStruct((1, 512), F32)],
        compiler_params=_cp("arbitrary"),
    )(dgpre, zr, wgk)


def _merge_fwd(x, zr, pp, og, bgate, wpp, wgla, wout, ts):
    s = x.shape[0]

    def body(x_ref, z0_ref, z1_ref, pp_ref, og_ref, bg_ref, wpp_ref, wgla_ref, wout_ref,
             x1_ref, mix_ref, yp_ref, yg_ref):
        ppv = pp_ref[...]
        yp = jnp.concatenate([_dot(ppv, wpp_ref[j]) for j in range(4)], axis=1)
        yg = _dot(og_ref[...], wgla_ref[...])
        g0 = _sigmoid(z0_ref[...].astype(F32) + bg_ref[:, :D])
        g1 = _sigmoid(z1_ref[...].astype(F32) + bg_ref[:, D:])
        mixed = (g0 * yp + g1 * yg).astype(BF16)
        x1_ref[...] = x_ref[...] + _dot(mixed, wout_ref[...])
        mix_ref[...] = mixed
        yp_ref[...] = yp.astype(BF16)
        yg_ref[...] = yg.astype(BF16)

    row = lambda i: (i, 0)
    const2 = lambda i: (0, 0)
    return pl.pallas_call(
        body, name="merge_fwd", grid=(s // ts,),
        in_specs=[pl.BlockSpec((ts, D), row), pl.BlockSpec((ts, D), lambda i: (i, 0)), pl.BlockSpec((ts, D), lambda i: (i, 1)),
                  pl.BlockSpec((ts, POOL_W), row), pl.BlockSpec((ts, D), row), pl.BlockSpec((1, 2 * D), const2),
                  pl.BlockSpec((4, POOL_W, 256), lambda i: (0, 0, 0)), pl.BlockSpec((D, D), const2),
                  pl.BlockSpec((D, D), const2)],
        out_specs=[pl.BlockSpec((ts, D), row)] * 4,
        out_shape=[jax.ShapeDtypeStruct((s, D), F32)] + [jax.ShapeDtypeStruct((s, D), BF16)] * 3,
        compiler_params=_cp("arbitrary"),
    )(x, zr, zr, pp, og, bgate, wpp, wgla, wout)


def _merge_bwd(dx1b, zr, yp, yg, o, bgate, ghead, wpp, wgla, wout, ts):
    s = dx1b.shape[0]

    def body(dx_ref, z0_ref, z1_ref, zog_ref, yp_ref, yg_ref, o_ref, bg_ref, gh_ref, wpp_ref, wgla_ref, wout_ref,
             dzg_ref, dyp_ref, dyg_ref, dpp_ref, do_ref, dzog_ref, dbg_ref, dgh_ref):
        @pl.when(pl.program_id(0) == 0)
        def _():
            dbg_ref[...] = jnp.zeros_like(dbg_ref)
            dgh_ref[...] = jnp.zeros_like(dgh_ref)

        dmix = _dot_nt(dx_ref[...], wout_ref[...])
        g0 = _sigmoid(z0_ref[...].astype(F32) + bg_ref[:, :D])
        g1 = _sigmoid(z1_ref[...].astype(F32) + bg_ref[:, D:])
        dypb = (dmix * g0).astype(BF16)
        dygb = (dmix * g1).astype(BF16)
        dz0 = dmix * yp_ref[...].astype(F32) * g0 * (1.0 - g0)
        dz1 = dmix * yg_ref[...].astype(F32) * g1 * (1.0 - g1)
        dzg_ref[:, :D] = dz0.astype(BF16)
        dzg_ref[:, D:] = dz1.astype(BF16)
        dbg_ref[:, :D] += jnp.sum(dz0, axis=0, keepdims=True)
        dbg_ref[:, D:] += jnp.sum(dz1, axis=0, keepdims=True)
        dyp_ref[...] = dypb
        dyg_ref[...] = dygb
        dpp = _dot_nt(dypb[:, 0:256], wpp_ref[0])
        for j in range(1, 4):
            dpp = dpp + _dot_nt(dypb[:, j * 256:(j + 1) * 256], wpp_ref[j])
        dpp_ref[...] = dpp.astype(BF16)
        dog = _dot_nt(dygb, wgla_ref[...])
        gh = gh_ref[...]
        dgh = jnp.zeros((1, HV), F32)
        for h in range(HEADS):
            cs = slice(h * HV, (h + 1) * HV)
            ov = o_ref[:, cs].astype(F32)
            r = lax.rsqrt(jnp.mean(ov * ov, axis=-1, keepdims=True) + EPS)
            oh = ov * r
            zo = zog_ref[:, cs].astype(F32)
            sg = _sigmoid(zo)
            dog_h = dog[:, cs]
            don = dog_h * zo * sg
            dzog_ref[:, cs] = (dog_h * oh * gh * sg * (1.0 + zo * (1.0 - sg))).astype(BF16)
            dgh = dgh + jnp.sum(don * oh, axis=0, keepdims=True)
            doh = don * gh
            do_ref[:, cs] = (r * (doh - oh * jnp.mean(doh * oh, axis=-1, keepdims=True))).astype(BF16)
        dgh_ref[...] += dgh

    row = lambda i: (i, 0)
    const2 = lambda i: (0, 0)
    return pl.pallas_call(
        body, name="merge_bwd", grid=(s // ts,),
        in_specs=[pl.BlockSpec((ts, D), row), pl.BlockSpec((ts, D), lambda i: (i, 0)), pl.BlockSpec((ts, D), lambda i: (i, 1)),
                  pl.BlockSpec((ts, D), lambda i: (i, OFF_OG // D)), pl.BlockSpec((ts, D), row), pl.BlockSpec((ts, D), row),
                  pl.BlockSpec((ts, D), row), pl.BlockSpec((1, 2 * D), const2), pl.BlockSpec((1, HV), const2),
                  pl.BlockSpec((4, POOL_W, 256), lambda i: (0, 0, 0)), pl.BlockSpec((D, D), const2),
                  pl.BlockSpec((D, D), const2)],
        out_specs=[pl.BlockSpec((ts, 2 * D), row), pl.BlockSpec((ts, D), row), pl.BlockSpec((ts, D), row),
                   pl.BlockSpec((ts, POOL_W), row), pl.BlockSpec((ts, D), row), pl.BlockSpec((ts, D), row),
                   pl.BlockSpec((1, 2 * D), const2), pl.BlockSpec((1, HV), const2)],
        out_shape=[jax.ShapeDtypeStruct((s, 2 * D), BF16), jax.ShapeDtypeStruct((s, D), BF16),
                   jax.ShapeDtypeStruct((s, D), BF16), jax.ShapeDtypeStruct((s, POOL_W), BF16),
                   jax.ShapeDtypeStruct((s, D), BF16), jax.ShapeDtypeStruct((s, D), BF16),
                   jax.ShapeDtypeStruct((1, 2 * D), F32), jax.ShapeDtypeStruct((1, HV), F32)],
        compiler_params=_cp("arbitrary"),
    )(dx1b, zr, zr, zr, yp, yg, o, bgate, ghead, wpp, wgla, wout)


HALO = 16
CCH = 1408


def _conv_taps(u_ref, halo_ref, cs, first, ts):
    u = u_ref[:, cs].astype(F32)
    hal = halo_ref[:, cs].astype(F32)
    h1 = jnp.where(first, 0.0, _pick_row(hal, HALO - 1))
    h2 = jnp.where(first, 0.0, _pick_row(hal, HALO - 2))
    row = _rows(u.shape)
    r1 = jnp.where(row == 0, h1, pltpu.roll(u, 1, 0))
    r2 = jnp.where(row == 0, h2, jnp.where(row == 1, h1, pltpu.roll(u, 2, 0)))
    return u, r1, r2


def _ffn_down_loss(u, x1, tgt, wconv, bconv, wdown, gfin, ts):
    s = x1.shape[0]

    def body(u_ref, halo_ref, x1_ref, t_ref, wc_ref, bc_ref, wd_ref, gf_ref, a_ref, c_ref, dx_ref, dxb_ref, ls_ref,
             dgf_ref):
        i = pl.program_id(0)

        @pl.when(i == 0)
        def _():
            ls_ref[...] = jnp.zeros_like(ls_ref)
            dgf_ref[...] = jnp.zeros_like(dgf_ref)

        first = i == 0
        acc = x1_ref[...]
        for hf in range(2):
            cg = slice(hf * CCH, (hf + 1) * CCH)
            cv = slice(D_FF + hf * CCH, D_FF + (hf + 1) * CCH)
            vals = []
            for cs in (cg, cv):
                u0, u1, u2 = _conv_taps(u_ref, halo_ref, cs, first, ts)
                vals.append(bc_ref[:, cs] + wc_ref[0:1, cs] * u2 + wc_ref[1:2, cs] * u1 + wc_ref[2:3, cs] * u0)
                c_ref[:, cs] = vals[-1].astype(BF16)
            a = (vals[0] * _sigmoid(vals[0]) * vals[1]).astype(BF16)
            a_ref[:, cg] = a
            acc = acc + _dot(a, wd_ref[cg, :])
        r = lax.rsqrt(jnp.mean(acc * acc, axis=-1, keepdims=True) + EPS)
        xh = acc * r
        gf = gf_ref[...]
        err = xh * gf - t_ref[...]
        ls_ref[...] += (0.5 / D) * jnp.sum(jnp.sum(err * err, axis=-1, keepdims=True), axis=0, keepdims=True)
        dy = err * (1.0 / D)
        dgf_ref[...] += jnp.sum(dy * xh, axis=0, keepdims=True)
        dxh = dy * gf
        dx = r * (dxh - xh * jnp.mean(dxh * xh, axis=-1, keepdims=True))
        dx_ref[...] = dx
        dxb_ref[...] = dx.astype(BF16)

    row = lambda i: (i, 0)
    const2 = lambda i: (0, 0)
    return pl.pallas_call(
        body, name="ffn_down_loss", grid=(s // ts,),
        in_specs=[pl.BlockSpec((ts, N_UP), row),
                  pl.BlockSpec((HALO, N_UP), lambda i: (jnp.maximum(i * (ts // HALO) - 1, 0), 0)),
                  pl.BlockSpec((ts, D), row), pl.BlockSpec((ts, D), row), pl.BlockSpec((3, N_UP), const2),
                  pl.BlockSpec((1, N_UP), const2), pl.BlockSpec((D_FF, D), const2), pl.BlockSpec((1, D), const2)],
        out_specs=[pl.BlockSpec((ts, D_FF), row), pl.BlockSpec((ts, N_UP), row), pl.BlockSpec((ts, D), row),
                   pl.BlockSpec((ts, D), row), pl.BlockSpec((1, 128), const2), pl.BlockSpec((1, D), const2)],
        out_shape=[jax.ShapeDtypeStruct((s, D_FF), BF16), jax.ShapeDtypeStruct((s, N_UP), BF16),
                   jax.ShapeDtypeStruct((s, D), F32), jax.ShapeDtypeStruct((s, D), BF16),
                   jax.ShapeDtypeStruct((1, 128), F32), jax.ShapeDtypeStruct((1, D), F32)],
        compiler_params=_cp("arbitrary"),
    )(u, u, x1, tgt, wconv, bconv, wdown, gfin)


def _ffn_bwd(dx2b, u, c, wconv, wdown, ts):
    s = dx2b.shape[0]
    nt = s // ts

    def body(dx_ref, u_ref, c_ref, wc_ref, wd_ref, du_ref, db_ref, dw_ref, nxt_ref):
        @pl.when(pl.program_id(0) == 0)
        def _():
            db_ref[...] = jnp.zeros_like(db_ref)
            dw_ref[...] = jnp.zeros_like(dw_ref)
            nxt_ref[...] = jnp.zeros_like(nxt_ref)

        dxv = dx_ref[...]
        row = _rows((ts, CCH))
        for hf in range(2):
            cg = slice(hf * CCH, (hf + 1) * CCH)
            cv = slice(D_FF + hf * CCH, D_FF + (hf + 1) * CCH)
            da = _dot_nt(dxv, wd_ref[cg, :])
            gate = c_ref[:, cg].astype(F32)
            val = c_ref[:, cv].astype(F32)
            sg = _sigmoid(gate)
            dcs = (da * val * sg * (1.0 + gate * (1.0 - sg)), da * gate * sg)
            for cs, dc in zip((cg, cv), dcs):
                n1 = nxt_ref[0:1, cs]
                n2 = nxt_ref[1:2, cs]
                f1 = jnp.where(row == ts - 1, n1, pltpu.roll(dc, ts - 1, 0))
                f2 = jnp.where(row == ts - 1, n2, jnp.where(row == ts - 2, n1, pltpu.roll(dc, ts - 2, 0)))
                uv = u_ref[:, cs].astype(F32)
                db_ref[:, cs] += jnp.sum(dc, axis=0, keepdims=True)
                dw_ref[0:1, cs] += jnp.sum(f2 * uv, axis=0, keepdims=True)
                dw_ref[1:2, cs] += jnp.sum(f1 * uv, axis=0, keepdims=True)
                dw_ref[2:3, cs] += jnp.sum(dc * uv, axis=0, keepdims=True)
                du_ref[:, cs] = (wc_ref[2:3, cs] * dc + wc_ref[1:2, cs] * f1 + wc_ref[0:1, cs] * f2).astype(BF16)
                nxt_ref[:, cs] = dc[0:8, :]

    rev = lambda i: (nt - 1 - i, 0)
    const2 = lambda i: (0, 0)
    return pl.pallas_call(
        body, name="ffn_bwd", grid=(nt,),
        in_specs=[pl.BlockSpec((ts, D), rev), pl.BlockSpec((ts, N_UP), rev), pl.BlockSpec((ts, N_UP), rev),
                  pl.BlockSpec((3, N_UP), const2), pl.BlockSpec((D_FF, D), const2)],
        out_specs=[pl.BlockSpec((ts, N_UP), rev), pl.BlockSpec((1, N_UP), const2), pl.BlockSpec((3, N_UP), const2)],
        out_shape=[jax.ShapeDtypeStruct((s, N_UP), BF16), jax.ShapeDtypeStruct((1, N_UP), F32),
                   jax.ShapeDtypeStruct((3, N_UP), F32)],
        scratch_shapes=[pltpu.VMEM((8, N_UP), F32)],
        compiler_params=_cp("arbitrary"),
    )(dx2b, u, c, wconv, wdown)


ANY = pl.BlockSpec(memory_space=pl.ANY)


def _place():
    x, y, c = lax.axis_index("x"), lax.axis_index("y"), lax.axis_index("c")
    chips = [(1 - x, y), (x, 1 - y), (1 - x, 1 - y)]
    return x, y, c, chips


def _half(shape, c, axis):
    size = shape[axis] // 2
    cut = pl.ds(pl.multiple_of(c * size, 8 if axis == 0 else 128), size)
    return (cut, slice(None)) if axis == 0 else (slice(None), cut)


def _half_shape(shape, axis):
    return (shape[0] // 2, shape[1]) if axis == 0 else (shape[0], shape[1] // 2)


def _remote(src, dst, send_sems, recv_sems, k, to):
    return pltpu.make_async_remote_copy(src_ref=src, dst_ref=dst, send_sem=send_sems.at[k], recv_sem=recv_sems.at[k],
                                        device_id=to, device_id_type=MESH)


def _all_gather_weights(big, axes, small):
    nb, ns = len(big), len(small)
    n = nb + ns
    n_sem = 6 * nb + 3 * ns

    def body(*refs):
        ins, outs = refs[:n], refs[n:2 * n]
        send_sems, recv_sems = refs[2 * n:]
        x, y, c, chips = _place()
        me = 2 * x + y
        sib = (x, y, 1 - c)
        started = []
        for a in range(nb):
            mine = _half(big[a].shape, c, axes[a])
            for k, ch in enumerate(chips):
                cp = _remote(ins[a].at[mine], outs[a].at[(me,) + mine], send_sems, recv_sems, 6 * a + k,
                             (ch[0], ch[1], c))
                cp.start()
                started.append(cp)
        for a in range(ns):
            for k, ch in enumerate(chips):
                cp = _remote(ins[nb + a], outs[nb + a].at[me], send_sems, recv_sems, 6 * nb + 3 * a + k,
                             (ch[0], ch[1], c))
                cp.start()
                started.append(cp)
        for a in range(nb):
            mine = _half(big[a].shape, c, axes[a])
            for k, ch in enumerate(chips):
                landed = outs[a].at[(2 * ch[0] + ch[1],) + mine]
                _remote(landed, landed, send_sems, recv_sems, 6 * a + k, sib).wait_recv()
                cp = _remote(landed, landed, send_sems, recv_sems, 6 * a + 3 + k, sib)
                cp.start()
                started.append(cp)
        for a in range(nb):
            other = _half(big[a].shape, 1 - c, axes[a])
            for k, ch in enumerate(chips):
                landed = outs[a].at[(2 * ch[0] + ch[1],) + other]
                _remote(landed, landed, send_sems, recv_sems, 6 * a + 3 + k, sib).wait_recv()
        for a in range(ns):
            for k, ch in enumerate(chips):
                landed = outs[nb + a].at[2 * ch[0] + ch[1]]
                _remote(landed, landed, send_sems, recv_sems, 6 * nb + 3 * a + k, sib).wait_recv()
        for cp in started:
            cp.wait_send()

    arrs = list(big) + list(small)
    return pl.pallas_call(
        body, name="all_gather_weights",
        in_specs=[ANY] * n, out_specs=[ANY] * n,
        out_shape=[jax.ShapeDtypeStruct((4,) + a.shape, a.dtype) for a in arrs],
        scratch_shapes=[pltpu.SemaphoreType.DMA((n_sem,)), pltpu.SemaphoreType.DMA((n_sem,))],
        compiler_params=pltpu.CompilerParams(has_side_effects=True),
    )(*arrs)


def _sibling_exchange(grads, axes, small, name):
    nb = len(grads)
    n = nb + (small is not None)

    def body(*refs):
        ins, outs = refs[:n], refs[n:2 * n]
        send_sems, recv_sems = refs[2 * n:]
        x, y, c, _ = _place()
        sib = (x, y, 1 - c)
        cps = []
        for a in range(nb):
            theirs = _half(grads[a].shape[1:], 1 - c, axes[a])
            cps.append(_remote(ins[a].at[(slice(None),) + theirs], outs[a], send_sems, recv_sems, a, sib))
        if small is not None:
            cps.append(_remote(ins[nb], outs[nb], send_sems, recv_sems, nb, sib))
        for cp in cps:
            cp.start()
        for cp in cps:
            cp.wait()

    out_shape = [jax.ShapeDtypeStruct((4,) + _half_shape(g.shape[1:], ax), F32) for g, ax in zip(grads, axes)]
    if small is not None:
        out_shape.append(jax.ShapeDtypeStruct(small.shape, F32))
    return pl.pallas_call(
        body, name=name, in_specs=[ANY] * n, out_specs=[ANY] * n, out_shape=out_shape,
        scratch_shapes=[pltpu.SemaphoreType.DMA((n,)), pltpu.SemaphoreType.DMA((n,))],
        compiler_params=pltpu.CompilerParams(has_side_effects=True),
    )(*grads, *([] if small is None else [small]))


def _gather_share(lands, axes, name):
    n = len(lands)

    def body(*refs):
        outs = refs[n:2 * n]
        send_sems, recv_sems = refs[2 * n:]
        x, y, c, chips = _place()
        sib = (x, y, 1 - c)
        cps = []
        for a in range(n):
            mine = _half(lands[a].shape[1:], c, axes[a])
            for k, ch in enumerate(chips):
                landed = outs[a].at[(2 * ch[0] + ch[1],) + mine]
                cps.append(_remote(landed, landed, send_sems, recv_sems, 3 * a + k, sib))
        for cp in cps:
            cp.start()
        for a in range(n):
            other = _half(lands[a].shape[1:], 1 - c, axes[a])
            for k, ch in enumerate(chips):
                landed = outs[a].at[(2 * ch[0] + ch[1],) + other]
                _remote(landed, landed, send_sems, recv_sems, 3 * a + k, sib).wait_recv()
        for cp in cps:
            cp.wait_send()

    return pl.pallas_call(
        body, name=name, in_specs=[ANY] * n, out_specs=[ANY] * n,
        out_shape=[jax.ShapeDtypeStruct(a.shape, a.dtype) for a in lands],
        input_output_aliases={a: a for a in range(n)},
        scratch_shapes=[pltpu.SemaphoreType.DMA((3 * n,)), pltpu.SemaphoreType.DMA((3 * n,))],
        compiler_params=pltpu.CompilerParams(has_side_effects=True),
    )(*lands)


def _sibling_share(halves, name):
    n = len(halves)

    def body(*refs):
        ins, outs = refs[:n], refs[n:2 * n]
        send_sems, recv_sems = refs[2 * n:]
        x, y, c, _ = _place()
        cps = [_remote(ins[a], outs[a], send_sems, recv_sems, a, (x, y, 1 - c)) for a in range(n)]
        for cp in cps:
            cp.start()
        for cp in cps:
            cp.wait()

    return pl.pallas_call(
        body, name=name, in_specs=[ANY] * n, out_specs=[ANY] * n,
        out_shape=[jax.ShapeDtypeStruct(h.shape, F32) for h in halves],
        scratch_shapes=[pltpu.SemaphoreType.DMA((n,)), pltpu.SemaphoreType.DMA((n,))],
        compiler_params=pltpu.CompilerParams(has_side_effects=True),
    )(*halves)


HBM = pl.BlockSpec(memory_space=pltpu.HBM)
SEM = pl.BlockSpec(memory_space=pltpu.SEMAPHORE)
DATAFLOW = pltpu.SideEffectType.DATAFLOW_SIDE_EFFECTING


def _split_start(name, srcs, land_shapes, plan, n_copies, after):
    lands = [lax.empty(shp, dt) for shp, dt in land_shapes]
    bufs = list(srcs) + lands
    nb, ns = len(bufs), len(srcs)

    def body(*refs):
        send_sems, recv_sems, token = refs[nb + 1], refs[nb + 2], refs[-1]
        for k, (src, dst, to) in enumerate(plan(refs[:ns], refs[ns:nb])):
            _remote(src, dst, send_sems, recv_sems, k, to).start()
        token[...] = jnp.zeros_like(token)

    res = pl.pallas_call(
        body, name=name,
        out_shape=(pltpu.SemaphoreType.DMA((n_copies,)), pltpu.SemaphoreType.DMA((n_copies,)),
                   *[pltpu.HBM(b.shape, b.dtype) for b in bufs], jax.ShapeDtypeStruct((8, 128), F32)),
        in_specs=[HBM] * nb + [ANY],
        out_specs=(SEM, SEM, *[HBM] * nb, pl.BlockSpec(memory_space=pltpu.VMEM)),
        input_output_aliases={i: 2 + i for i in range(nb)},
        compiler_params=pltpu.CompilerParams(has_side_effects=DATAFLOW),
    )(*[pltpu.with_memory_space_constraint(b, pltpu.HBM) for b in bufs], after)
    return (res[0], res[1], list(res[2:2 + nb])), res[-1]


def _split_wait(name, handle, n_srcs, plan, after):
    send_sems, recv_sems, bufs = handle
    nb = len(bufs)

    def body(*refs):
        sends, recvs = refs[nb], refs[nb + 1]
        for k, (src, dst, to) in enumerate(plan(refs[:n_srcs], refs[n_srcs:nb])):
            cp = _remote(src, dst, sends, recvs, k, to)
            cp.wait_send()
            cp.wait_recv()

    res = pl.pallas_call(
        body, name=name, out_shape=[pltpu.HBM(b.shape, b.dtype) for b in bufs],
        in_specs=[HBM] * nb + [SEM, SEM, ANY], out_specs=[HBM] * nb,
        input_output_aliases={i: i for i in range(nb)},
        compiler_params=pltpu.CompilerParams(has_side_effects=DATAFLOW),
    )(*bufs, send_sems, recv_sems, after)
    return list(res[:n_srcs]), list(res[n_srcs:])


def _gather_plan(shapes, axes, n_whole=0):
    def plan(srcs, lands):
        x, y, c, chips = _place()
        out = []
        for a, (shape, axis) in enumerate(zip(shapes, axes)):
            mine = _half(shape, c, axis)
            for ch in chips:
                out.append((srcs[a].at[mine], lands[a].at[(2 * x + y,) + mine], (ch[0], ch[1], c)))
        for a in range(len(shapes), len(shapes) + n_whole):
            for ch in chips:
                out.append((srcs[a], lands[a].at[2 * x + y], (ch[0], ch[1], c)))
        return out
    return plan


def _sibling_plan(shapes, axes):
    def plan(srcs, lands):
        x, y, c, _ = _place()
        return [(srcs[a].at[(slice(None),) + _half(shape, 1 - c, axis)], lands[a], (x, y, 1 - c))
                for a, (shape, axis) in enumerate(zip(shapes, axes))]
    return plan


def _reduce_plan(n_big, with_small):
    def plan(srcs, lands):
        x, y, c, chips = _place()
        out = []
        for a in range(n_big):
            for k, ch in enumerate(chips):
                out.append((srcs[a].at[2 * ch[0] + ch[1]], lands[a].at[k], (ch[0], ch[1], c)))
        if with_small:
            for ch in chips:
                out.append((srcs[n_big], lands[n_big].at[2 * x + y], (ch[0], ch[1], c)))
        return out
    return plan


def _row_tile(rows, cols, mult):
    best = mult
    for t in range(mult, rows + 1, mult):
        if rows % t == 0 and t * cols * 4 <= (1 << 20):
            best = t
    return best if rows % best == 0 else rows


COL_TILE = 256


def _half_tiling(hshape, axis, mult):
    hr, hc = hshape
    if axis == 0:
        tr = _row_tile(hr, hc, mult)
        return tr, hc, hr // tr
    return hr, COL_TILE, hc // COL_TILE


def _tile_idx(axis, t):
    return (t, 0) if axis == 0 else (0, t)


def _chip_partial(place, g, t, axis, name):
    hshape = t.shape[1:]
    br, bc, nt = _half_tiling(hshape, axis, 16)

    def body(pl_ref, g_ref, t_ref, pf_ref, pb_ref):
        v = g_ref[...] + t_ref[...]
        pb_ref[...] = v.astype(BF16)

        @pl.when(pl.program_id(1) == pl_ref[0])
        def _():
            pf_ref[...] = v

    blk = (None, br, bc)
    return pl.pallas_call(
        body, name=name,
        grid_spec=pltpu.PrefetchScalarGridSpec(
            num_scalar_prefetch=1, grid=(nt, 4),
            in_specs=[pl.BlockSpec(blk, lambda i, j, p: (j,) + _tile_idx(axis, p[1] * nt + i)),
                      pl.BlockSpec(blk, lambda i, j, p: (j,) + _tile_idx(axis, i))],
            out_specs=[pl.BlockSpec((br, bc), lambda i, j, p: _tile_idx(axis, i)),
                       pl.BlockSpec(blk, lambda i, j, p: (j,) + _tile_idx(axis, i))]),
        out_shape=[jax.ShapeDtypeStruct(hshape, F32), jax.ShapeDtypeStruct((4,) + hshape, BF16)],
        compiler_params=_cp("arbitrary", "arbitrary"),
    )(place, g, t)


def _finish_half(pf, rb, axis, name):
    hshape = pf.shape
    br, bc, nt = _half_tiling(hshape, axis, 16)

    def body(pf_ref, rb_ref, o_ref):
        o_ref[...] = ((pf_ref[...] + rb_ref[0].astype(F32)) + rb_ref[1].astype(F32)) + rb_ref[2].astype(F32)

    return pl.pallas_call(
        body, name=name, grid=(nt,),
        in_specs=[pl.BlockSpec((br, bc), lambda i: _tile_idx(axis, i)),
                  pl.BlockSpec((3, br, bc), lambda i: (0,) + _tile_idx(axis, i))],
        out_specs=pl.BlockSpec((br, bc), lambda i: _tile_idx(axis, i)),
        out_shape=jax.ShapeDtypeStruct(hshape, F32),
        compiler_params=_cp("arbitrary"),
    )(pf, rb)


def _add2(a, b, name):
    def body(a_ref, b_ref, o_ref):
        o_ref[...] = a_ref[...] + b_ref[...]

    return pl.pallas_call(body, name=name, out_shape=jax.ShapeDtypeStruct(a.shape, F32))(a, b)


def _adam_math(w, g, m, v):
    m = ADAM_B1 * m + (1.0 - ADAM_B1) * g
    v = ADAM_B2 * v + (1.0 - ADAM_B2) * (g * g)
    m_hat = m / (1.0 - ADAM_B1 ** ADAM_STEP)
    v_hat = v / (1.0 - ADAM_B2 ** ADAM_STEP)
    return -ADAM_LR * (m_hat / (jnp.sqrt(v_hat) + ADAM_EPS) + ADAM_WD * w), m, v


def _adam_halves(place, w, mine, theirs, m, v, axis, name):
    br, bc, nt = _half_tiling(mine.shape, axis, 8)

    def body(pl_ref, w_ref, a_ref, b_ref, m_ref, v_ref, g_ref, d_ref, mo_ref, vo_ref):
        is_mine = pl.program_id(0) // nt == pl_ref[1]
        g = jnp.where(is_mine, a_ref[...], b_ref[...])
        d, mn, vn = _adam_math(w_ref[...], g, m_ref[...], v_ref[...])
        g_ref[...] = g
        d_ref[...] = d
        mo_ref[...] = mn
        vo_ref[...] = vn

    full = pl.BlockSpec((br, bc), lambda i, p: _tile_idx(axis, i))
    half = pl.BlockSpec((br, bc), lambda i, p: _tile_idx(axis, i % nt))
    return pl.pallas_call(
        body, name=name,
        grid_spec=pltpu.PrefetchScalarGridSpec(
            num_scalar_prefetch=1, grid=(2 * nt,), in_specs=[full, half, half, full, full], out_specs=[full] * 4),
        out_shape=[jax.ShapeDtypeStruct(w.shape, F32)] * 4, compiler_params=_cp("arbitrary"),
    )(place, w, mine, theirs, m, v)


def _sum_chips(chip_sums):
    def body(s_ref, g_ref):
        g_ref[...] = ((s_ref[0] + s_ref[1]) + s_ref[2]) + s_ref[3]

    return pl.pallas_call(body, name="sum_small", out_shape=jax.ShapeDtypeStruct(chip_sums.shape[1:], F32))(chip_sums)


def _adam_small(ws, gs, ms, vs):
    n = len(ws)

    def body(*refs):
        for i in range(n):
            d, mn, vn = _adam_math(refs[i][...], refs[n + i][...], refs[2 * n + i][...], refs[3 * n + i][...])
            refs[4 * n + i][...] = d
            refs[5 * n + i][...] = mn
            refs[6 * n + i][...] = vn

    out = pl.pallas_call(body, name="adam_small",
                         out_shape=[jax.ShapeDtypeStruct(w.shape, F32) for w in ws] * 3)(*ws, *gs, *ms, *vs)
    return out[:n], out[n:2 * n], out[2 * n:]


SMALL = (("g_mix", (1, 1024)), ("b_gate", (1, 2048)), ("w_gk_up", (1, 16, 512)), ("b_gk", (1, 512)),
         ("w_pool_grp", (1, 4, 128, 128)), ("pool_scale", (1, 512)), ("g_gla_head", (1, 256)), ("g_ffn", (1, 1024)),
         ("w_conv", (1, 3, 5632)), ("b_conv", (1, 5632)), ("g_final", (1024,)), ("loss", (768,)))
SMALL_ROWS = 808


def _pack_small(parts):
    flat = jnp.concatenate([parts[n].astype(F32).reshape(-1) for n, _ in SMALL])
    return flat.reshape(SMALL_ROWS, 128)


def _unpack_small(buf):
    flat = buf.reshape(-1)
    out, off = {}, 0
    for n, shp in SMALL:
        size = 1
        for d_ in shp:
            size *= d_
        out[n] = flat[off:off + size].reshape(shp)
        off += size
    return out


def kernel(x, g_mix, w_in, b_gate, w_gk_up, b_gk, w_pool_grp, pool_scale, g_gla_head, w_pool_proj, w_gla_proj, w_out, g_ffn, w_up, w_conv, b_conv, w_down, g_final, loss_target, m_g_mix, m_w_in, m_b_gate, m_w_gk_up, m_b_gk, m_w_pool_grp, m_pool_scale, m_g_gla_head, m_w_pool_proj, m_w_gla_proj, m_w_out, m_g_ffn, m_w_up, m_w_conv, m_b_conv, m_w_down, m_g_final, v_g_mix, v_w_in, v_b_gate, v_w_gk_up, v_b_gk, v_w_pool_grp, v_pool_scale, v_g_gla_head, v_w_pool_proj, v_w_gla_proj, v_w_out, v_g_ffn, v_w_up, v_w_conv, v_b_conv, v_w_down, v_g_final):
    s = x.shape[1]
    ts = min(s, 512)
    tm = min(s, 256)
    cx, cy, cc = lax.axis_index("x"), lax.axis_index("y"), lax.axis_index("c")
    chip = 2 * cx + cy
    place = jnp.stack([chip, cc]).astype(jnp.int32)

    big_names = ("w_in", "w_pool_proj", "w_gla_proj", "w_out", "w_up", "w_down")
    axes = (1, 0, 0, 0, 0, 0)
    shards = dict(w_in=jnp.transpose(w_in[0]), w_pool_proj=w_pool_proj[0], w_gla_proj=w_gla_proj[0], w_out=w_out[0],
                  w_up=w_up[0], w_down=w_down[0])
    def fill_own(lands, mine):
        return [lax.dynamic_update_slice(g, o_[None], (chip, 0, 0)) for g, o_ in zip(lands, mine)]

    def gather_start(tag, halves, group_axes, whole, after):
        plan = _gather_plan([o_.shape for o_ in halves], group_axes, len(whole))
        srcs = list(halves) + list(whole)
        handle, token = _split_start("gather_" + tag + "_start", srcs, [((4,) + o_.shape, o_.dtype) for o_ in srcs], plan,
                                     3 * len(srcs), after)
        return (handle, plan, len(halves), len(srcs), group_axes), token

    def gather_finish(tag, started, after):
        handle, plan, n_halves, n, group_axes = started
        mine, lands = _split_wait("gather_" + tag + "_wait", handle, n, plan, after)
        lands[:n_halves] = _gather_share(lands[:n_halves], group_axes, "gather_" + tag + "_share")
        return fill_own(lands, mine)

    in_w, tok = gather_start("in", [shards["w_in"].astype(BF16)], axes[:1], [], g_mix)
    zero = tok[0, 0]
    own = [(shards[n] + zero).astype(BF16) for n in big_names[1:]]
    mix_w, tok = gather_start("mix", own[0:3], axes[1:4], [w_gk_up[0] + zero, w_conv[0] + zero], tok)
    ffn_w, tok = gather_start("ffn", own[3:5], axes[4:6], [], tok)
    w_in_t = gather_finish("in", in_w, tok)[0].reshape(N_IN, D)
    w_rt = jnp.concatenate([w_in_t[3600:], w_in_t[1536:3584], w_in_t[0:1536], w_in_t[3584:3600],
                            jnp.zeros((128 - GATE_RANK, D), BF16)], axis=0)
    nsh = N_IN // 4

    xs, tgt = x[0], loss_target[0]
    wgrp = w_pool_grp[0]

    h = _rmsnorm(xs, g_mix, "norm_mix", ts)
    zr = _matmul_resident(h, w_rt, "in_proj", 1152, transposed=True)
    p, pp = _pool_fwd(zr, wgrp, pool_scale)
    wpp, wgla, wout, wgk4, wconv4 = gather_finish("mix", mix_w, pp)
    wgla, wout = wgla.reshape(D, D), wout.reshape(D, D)
    wgk_full = jnp.transpose(wgk4, (1, 0, 2)).reshape(GATE_RANK, 512)
    wconv_full = jnp.transpose(wconv4, (1, 0, 2)).reshape(3, N_UP)
    wgk_pad = jnp.concatenate([wgk_full, jnp.zeros((128 - GATE_RANK, 512), F32)], axis=0)
    o, og, sp = _gla_fwd(zr, wgk_pad, b_gk, g_gla_head, tm)
    x1, mixed, yp, yg = _merge_fwd(xs, zr, pp, og, b_gate, wpp, wgla, wout, tm)
    wup, wdown = gather_finish("ffn", ffn_w, x1)
    wdown = wdown.reshape(D_FF, D)
    h2 = _rmsnorm(x1, g_ffn, "norm_ffn", ts)
    u = _matmul_resident(h2, wup, "ffn_up", None)
    a, conv_out, dx2, dx2b, loss_part, dgfin = _ffn_down_loss(u, x1, tgt, wconv_full, b_conv, wdown,
                                                              g_final.reshape(1, D), tm)

    du, dbconv, dwconv = _ffn_bwd(dx2b, u, conv_out, wconv_full, wdown, tm)
    dw_down = _matmul_tn(a, dx2b, "dw_down", D, s, tm=1408)
    dw_up = _matmul_tn(h2, du, "dw_up", 1408, s, shard_major=True)

    def exchange_start(tag, grads, group_axes, after):
        plan = _sibling_plan([g.shape[1:] for g in grads], group_axes)
        lands = [((4,) + _half_shape(g.shape[1:], ax), F32) for g, ax in zip(grads, group_axes)]
        handle, token = _split_start("sibling_" + tag + "_start", grads, lands, plan, len(grads), after)
        return (handle, plan, len(grads)), token

    def partials(tag, names, group_axes, exchange, after):
        handle, plan, n = exchange
        mine, theirs = _split_wait("sibling_" + tag + "_wait", handle, n, plan, after)
        return zip(*[_chip_partial(place, g, t, ax, "chip_partial_" + nm)
                     for nm, ax, g, t in zip(names, group_axes, mine, theirs)])

    ffn_names, ffn_axes = ("w_up", "w_down"), (0, 0)
    ffn_x, token = exchange_start("ffn", [dw_up, dw_down.reshape(4, 704, D)], ffn_axes, du)
    dx1, dx1b, dgffn = _matmul_nt_normbwd(du, wup, x1, g_ffn + token[0:1, 0:1], dx2, "ffn_up_bwd", ts)
    ffn_pf, ffn_pb = partials("ffn", ffn_names, ffn_axes, ffn_x, dx1b)
    ffn_plan = _reduce_plan(2, False)
    ffn_handle, token = _split_start("reduce_ffn_start", ffn_pb, [((3,) + p.shape[1:], BF16) for p in ffn_pb],
                                     ffn_plan, 6, ffn_pf[0])

    dzg, dyp, dyg, dpp, do, dzog, dbgate, dghead = _merge_bwd(dx1b, zr, yp, yg, o, b_gate + token[0:1, 0:1], g_gla_head,
                                                             wpp, wgla, wout, tm)
    dw_out = _matmul_tn(mixed, dx1b, "dw_out", D, s)
    dw_gla = _matmul_tn(og, dyg, "dw_gla", D, s)
    dw_pp = _matmul_tn(pp, dyp, "dw_pp", 256, s, shard_major=True)

    out_names, out_axes = ("w_pool_proj", "w_gla_proj", "w_out"), (0, 0, 0)
    out_x, token = exchange_start("out", [dw_pp, dw_gla.reshape(4, 256, D), dw_out.reshape(4, 256, D)], out_axes, dpp)
    dzp, dwgrp, dscale = _pool_bwd(p, dpp, wgrp, pool_scale + token[0:1, 0:1])
    out_pf, out_pb = partials("out", out_names, out_axes, out_x, dzp)
    out_plan = _reduce_plan(3, False)
    out_handle, token = _split_start("reduce_out_start", out_pb, [((3,) + p_.shape[1:], BF16) for p_ in out_pb],
                                     out_plan, 9, out_pf[0])
    dq, dk, dv, dgpre = _gla_bwd(zr, do, sp, wgk_pad, b_gk + token[0:1, 0:1], tm)
    dzgk, dwgk, dbgk = _gk_bwd(dgpre, zr, wgk_pad, ts)
    dzr = jnp.concatenate([dzg, dv, dzog, dzp, dq, dk, dzgk], axis=1)
    dw_rt = _matmul_tn(dzr, h, "dw_in", D, s, tm=1152)

    def grad_rows(lo, hi):
        out = []
        for seg_lo, seg_hi, at in ((0, 1536, OFF_POOL), (1536, 3584, OFF_V), (3584, 3600, OFF_GK), (3600, N_IN, OFF_GATE)):
            a_, b_ = max(lo, seg_lo), min(hi, seg_hi)
            if a_ < b_:
                out.append(dw_rt[at + a_ - seg_lo:at + b_ - seg_lo])
        return jnp.concatenate(out, axis=0)

    dw_in_t = jnp.stack([grad_rows(j * nsh, (j + 1) * nsh) for j in range(4)])

    in_x, token = exchange_start("in", [dw_in_t], (1,), dzr)
    grad_x, _, dgmix = _matmul_nt_normbwd(dzr, w_rt, xs, g_mix + token[0:1, 0:1], dx1, "in_proj_bwd", ts, transposed=True)
    small_mine = _pack_small(dict(
        g_mix=dgmix, b_gate=dbgate, w_gk_up=dwgk[:GATE_RANK], b_gk=dbgk, w_pool_grp=dwgrp, pool_scale=dscale,
        g_gla_head=dghead, g_ffn=dgffn, w_conv=dwconv, b_conv=dbconv, g_final=dgfin,
        loss=jnp.concatenate([loss_part.reshape(128), jnp.zeros((640,), F32)])))
    in_pf, in_pb = partials("in", ("w_in",), (1,), in_x, grad_x)
    small_sib = _sibling_exchange([], (), small_mine, "sibling_exchange_small")[0]
    small_chip = _add2(small_mine, small_sib, "chip_partial_small")
    in_plan = _reduce_plan(1, True)
    in_handle, token = _split_start(
        "reduce_in_start", [in_pb[0], small_chip],
        [((3,) + in_pb[0].shape[1:], BF16), ((4,) + small_chip.shape, F32)], in_plan, 6, in_pf[0])

    ms = dict(w_in=jnp.transpose(m_w_in[0]), w_pool_proj=m_w_pool_proj[0], w_gla_proj=m_w_gla_proj[0], w_out=m_w_out[0],
              w_up=m_w_up[0], w_down=m_w_down[0])
    vs = dict(w_in=jnp.transpose(v_w_in[0]), w_pool_proj=v_w_pool_proj[0], w_gla_proj=v_w_gla_proj[0], w_out=v_w_out[0],
              w_up=v_w_up[0], w_down=v_w_down[0])
    grad, delta, new_m, new_v = {}, {}, {}, {}

    def finish_and_update(names, group_axes, part_f, landed, tag):
        halves = [_finish_half(pf, rb, ax, "finish_" + n) for n, ax, pf, rb in zip(names, group_axes, part_f, landed)]
        sib_halves = _sibling_share(halves, "sibling_share_" + tag)
        for n, ax, mine, theirs in zip(names, group_axes, halves, sib_halves):
            res = _adam_halves(place, shards[n], mine, theirs, ms[n], vs[n], ax, "adam_" + n)
            if n == "w_in":
                res = [jnp.transpose(r_) for r_ in res]
            grad[n], delta[n], new_m[n], new_v[n] = [r_[None] for r_ in res]

    _, ffn_landed = _split_wait("reduce_ffn_wait", ffn_handle, 2, ffn_plan, token)
    _, out_landed = _split_wait("reduce_out_wait", out_handle, 3, out_plan, ffn_landed[0])
    finish_and_update(ffn_names + out_names, ffn_axes + out_axes, ffn_pf + out_pf, ffn_landed + out_landed, "rest")
    in_sent, in_landed = _split_wait("reduce_in_wait", in_handle, 2, in_plan, delta["w_out"])
    small_sums = lax.dynamic_update_slice(in_landed[1], in_sent[1][None], (chip, 0, 0))
    finish_and_update(("w_in",), (1,), in_pf, in_landed[:1], "in")

    def narrow(a, width):
        return lax.dynamic_slice_in_dim(a.reshape(a.shape[:-1] + (4, width)), chip, 1, axis=a.ndim - 1).reshape(
            a.shape[:-1] + (width,))

    parts = _unpack_small(_sum_chips(small_sums))
    loss = parts["loss"][0]
    parts["w_gk_up"] = narrow(parts["w_gk_up"], 128)
    parts["w_conv"] = narrow(parts["w_conv"], 1408)
    small_names = [n for n, _ in SMALL[:-1]]
    given = dict(g_mix=(g_mix, m_g_mix, v_g_mix), b_gate=(b_gate, m_b_gate, v_b_gate), w_gk_up=(w_gk_up, m_w_gk_up, v_w_gk_up),
                 b_gk=(b_gk, m_b_gk, v_b_gk), w_pool_grp=(w_pool_grp, m_w_pool_grp, v_w_pool_grp),
                 pool_scale=(pool_scale, m_pool_scale, v_pool_scale), g_gla_head=(g_gla_head, m_g_gla_head, v_g_gla_head),
                 g_ffn=(g_ffn, m_g_ffn, v_g_ffn), w_conv=(w_conv, m_w_conv, v_w_conv), b_conv=(b_conv, m_b_conv, v_b_conv),
                 g_final=(g_final.reshape(1, D), m_g_final.reshape(1, D), v_g_final.reshape(1, D)))
    parts["g_final"] = parts["g_final"].reshape(1, D)
    ds, mo, vo = _adam_small([given[n][0] for n in small_names], [parts[n] for n in small_names],
                             [given[n][1] for n in small_names], [given[n][2] for n in small_names])
    for i, n in enumerate(small_names):
        shp = (D,) if n == "g_final" else parts[n].shape
        grad[n], delta[n], new_m[n], new_v[n] = [r_.reshape(shp) for r_ in (parts[n], ds[i], mo[i], vo[i])]

    order = ("g_mix", "w_in", "b_gate", "w_gk_up", "b_gk", "w_pool_grp", "pool_scale", "g_gla_head", "w_pool_proj",
             "w_gla_proj", "w_out", "g_ffn", "w_up", "w_conv", "b_conv", "w_down", "g_final")
    return (loss, grad_x[None], *[grad[n] for n in order], *[delta[n] for n in order], *[new_m[n] for n in order],
            *[new_v[n] for n in order])
```

```python
import functools

import jax
import jax.numpy as jnp
from jax import lax
from jax.experimental import pallas as pl
from jax.experimental.pallas import tpu as pltpu

F32 = jnp.float32
BF16 = jnp.bfloat16
MESH = pl.DeviceIdType.MESH

D = 1024
EPS = 1e-6
CHUNK = 64
POOL_W = 512
POOL_WINDOWS = (2, 4, 8, 16)
HEADS = 4
HK = 128
HV = 256
GATE_RANK = 16
D_FF = 2816
N_UP = 2 * D_FF
N_IN = 5648
QSCALE = HK ** -0.5
N_INR = 5760
OFF_GATE, OFF_V, OFF_OG, OFF_POOL, OFF_Q, OFF_K, OFF_GK = 0, 2048, 3072, 4096, 4608, 5120, 5632

ADAM_LR, ADAM_B1, ADAM_B2, ADAM_EPS, ADAM_WD, ADAM_STEP = 0.001, 0.9, 0.999, 1e-08, 0.01, 10

VMEM_LIMIT = 56 * 1024 * 1024


def _cp(*sem):
    return pltpu.CompilerParams(dimension_semantics=sem if sem else None, vmem_limit_bytes=VMEM_LIMIT)


def _dot(a, b):
    return jnp.dot(a, b, preferred_element_type=F32)


def _dot_nt(a, b):
    return lax.dot_general(a, b, (((1,), (1,)), ((), ())), preferred_element_type=F32)


def _dot_tn(a, b):
    return lax.dot_general(a, b, (((0,), (0,)), ((), ())), preferred_element_type=F32)


def _sigmoid(v):
    return 1.0 / (1.0 + jnp.exp(-v))


def _rows(shape):
    return lax.broadcasted_iota(jnp.int32, shape, 0)


def _pick_row(v, r):
    return jnp.sum(jnp.where(_rows(v.shape) == r, v, 0.0), axis=0, keepdims=True)


def _rmsnorm(x, g, name, ts):
    s = x.shape[0]

    def body(x_ref, g_ref, h_ref):
        xv = x_ref[...]
        r = lax.rsqrt(jnp.mean(xv * xv, axis=-1, keepdims=True) + EPS)
        h_ref[...] = (xv * r * g_ref[...]).astype(BF16)

    return pl.pallas_call(
        body, name=name, grid=(s // ts,),
        in_specs=[pl.BlockSpec((ts, D), lambda i: (i, 0)), pl.BlockSpec((1, D), lambda i: (0, 0))],
        out_specs=pl.BlockSpec((ts, D), lambda i: (i, 0)), out_shape=jax.ShapeDtypeStruct((s, D), BF16),
        compiler_params=_cp("arbitrary"),
    )(x, g)


MM_ROWS = 512


def _matmul_resident(h, w, name, tn, transposed=False):
    s = h.shape[0]
    if transposed:
        nj = w.shape[0] // tn
        w_spec = pl.BlockSpec((tn, D), lambda j: (j, 0))
    elif w.ndim == 3:
        nj, tn = w.shape[0], w.shape[2]
        w_spec = pl.BlockSpec((None, D, tn), lambda j: (j, 0, 0))
    else:
        nj = w.shape[1] // tn
        w_spec = pl.BlockSpec((D, tn), lambda j: (0, j))
    mm = _dot_nt if transposed else _dot
    rc = min(s, MM_ROWS)

    def body(h_ref, w_ref, z_ref):
        for r0 in range(0, s, rc):
            z_ref[r0:r0 + rc, :] = mm(h_ref[r0:r0 + rc, :], w_ref[...]).astype(BF16)

    return pl.pallas_call(
        body, name=name, grid=(nj,),
        in_specs=[pl.BlockSpec((s, D), lambda j: (0, 0)), w_spec],
        out_specs=pl.BlockSpec((s, tn), lambda j: (0, j)), out_shape=jax.ShapeDtypeStruct((s, nj * tn), BF16),
        compiler_params=_cp("arbitrary"),
    )(h, w)


def _matmul_nt_normbwd(dz, w, x, g, resid, name, ts, transposed=False):
    s = x.shape[0]

    def body(dz_ref, w_hbm, x_ref, g_ref, r_ref, o_ref, ob_ref, dg_ref, w_ref, sem):
        @pl.when(pl.program_id(0) == 0)
        def _():
            cp = pltpu.make_async_copy(w_hbm, w_ref, sem)
            cp.start()
            cp.wait()
            dg_ref[...] = jnp.zeros_like(dg_ref)

        if transposed:
            dh = _dot(dz_ref[...], w_ref[...])
        else:
            kc = w.shape[2]
            dh = _dot_nt(dz_ref[:, 0:kc], w_ref[0])
            for j in range(1, w.shape[0]):
                dh = dh + _dot_nt(dz_ref[:, j * kc:(j + 1) * kc], w_ref[j])
        xv = x_ref[...]
        r = lax.rsqrt(jnp.mean(xv * xv, axis=-1, keepdims=True) + EPS)
        xh = xv * r
        dg_ref[...] += jnp.sum(dh * xh, axis=0, keepdims=True)
        dxh = dh * g_ref[...]
        out = r_ref[...] + r * (dxh - xh * jnp.mean(dxh * xh, axis=-1, keepdims=True))
        o_ref[...] = out
        ob_ref[...] = out.astype(BF16)

    row = lambda i: (i, 0)
    kdim = dz.shape[1]
    return pl.pallas_call(
        body, name=name, grid=(s // ts,),
        in_specs=[pl.BlockSpec((ts, kdim), row), ANY, pl.BlockSpec((ts, D), row),
                  pl.BlockSpec((1, D), lambda i: (0, 0)), pl.BlockSpec((ts, D), row)],
        out_specs=[pl.BlockSpec((ts, D), row), pl.BlockSpec((ts, D), row), pl.BlockSpec((1, D), lambda i: (0, 0))],
        out_shape=[jax.ShapeDtypeStruct((s, D), F32), jax.ShapeDtypeStruct((s, D), BF16),
                   jax.ShapeDtypeStruct((1, D), F32)],
        scratch_shapes=[pltpu.VMEM(w.shape, BF16), pltpu.SemaphoreType.DMA],
        compiler_params=_cp("arbitrary"),
    )(dz, w, x, g, resid)


def _matmul_tn(a, b, name, tn, tk, shard_major=False, tm=None):
    s, m = a.shape
    n = b.shape[1]
    tm = m if tm is None else tm
    ni, nj, nk = m // tm, n // tn, s // tk

    def body(a_ref, b_ref, o_ref):
        if nk == 1:
            o_ref[...] = _dot_tn(a_ref[...], b_ref[...])
            return

        @pl.when(pl.program_id(2) == 0)
        def _():
            o_ref[...] = jnp.zeros_like(o_ref)

        o_ref[...] += _dot_tn(a_ref[...], b_ref[...])

    if shard_major:
        out_spec = pl.BlockSpec((None, tm, tn), lambda i, j, k: (j, i, 0))
        out_shape = jax.ShapeDtypeStruct((nj, m, tn), F32)
    else:
        out_spec = pl.BlockSpec((tm, tn), lambda i, j, k: (i, j))
        out_shape = jax.ShapeDtypeStruct((m, n), F32)
    return pl.pallas_call(
        body, name=name, grid=(ni, nj, nk),
        in_specs=[pl.BlockSpec((tk, tm), lambda i, j, k: (k, i)), pl.BlockSpec((tk, tn), lambda i, j, k: (k, j))],
        out_specs=out_spec, out_shape=out_shape,
        compiler_params=_cp("arbitrary", "arbitrary", "arbitrary"),
    )(a, b)


def _pool_fwd(zr, wgrp, scale):
    s = zr.shape[0]

    def body(u_ref, w_ref, sc_ref, p_ref, pp_ref):
        row = _rows((s, 128))
        for gi, win in enumerate(POOL_WINDOWS):
            cs = slice(gi * 128, (gi + 1) * 128)
            u = u_ref[:, cs].astype(F32)
            acc, k = u, 1
            while k < win:
                acc = acc + jnp.where(row >= k, pltpu.roll(acc, k, 0), 0.0)
                k *= 2
            cnt = jnp.minimum(row + 1, win).astype(F32)
            p = (acc / cnt - u).astype(BF16)
            p_ref[:, cs] = p
            pp_ref[:, cs] = (_dot(p, w_ref[gi].astype(BF16)) * sc_ref[:, cs]).astype(BF16)

    return pl.pallas_call(
        body, name="pool_fwd", grid=(1,),
        in_specs=[pl.BlockSpec((s, POOL_W), lambda i: (0, OFF_POOL // POOL_W)),
                  pl.BlockSpec((4, 128, 128), lambda i: (0, 0, 0)), pl.BlockSpec((1, POOL_W), lambda i: (0, 0))],
        out_specs=[pl.BlockSpec((s, POOL_W), lambda i: (0, 0))] * 2,
        out_shape=[jax.ShapeDtypeStruct((s, POOL_W), BF16)] * 2,
        compiler_params=_cp("arbitrary"),
    )(zr, wgrp, scale)


def _pool_bwd(p, dpp, wgrp, scale):
    s = p.shape[0]

    def body(p_ref, dpp_ref, w_ref, sc_ref, dz_ref, dw_ref, dsc_ref):
        row = _rows((s, 128))
        for gi, win in enumerate(POOL_WINDOWS):
            cs = slice(gi * 128, (gi + 1) * 128)
            pv = p_ref[:, cs]
            wb = w_ref[gi].astype(BF16)
            dpp_v = dpp_ref[:, cs].astype(F32)
            dsc_ref[:, cs] = jnp.sum(dpp_v * _dot(pv, wb), axis=0, keepdims=True)
            dpm = (dpp_v * sc_ref[:, cs]).astype(BF16)
            dw_ref[gi] = _dot_tn(pv, dpm)
            dp = _dot_nt(dpm, wb)
            cnt = jnp.minimum(row + 1, win).astype(F32)
            acc, k = dp / cnt, 1
            while k < win:
                acc = acc + jnp.where(row < s - k, pltpu.roll(acc, s - k, 0), 0.0)
                k *= 2
            dz_ref[:, cs] = (acc - dp).astype(BF16)

    full = lambda i: (0, 0)
    return pl.pallas_call(
        body, name="pool_bwd", grid=(1,),
        in_specs=[pl.BlockSpec((s, POOL_W), full), pl.BlockSpec((s, POOL_W), full),
                  pl.BlockSpec((4, 128, 128), lambda i: (0, 0, 0)), pl.BlockSpec((1, POOL_W), full)],
        out_specs=[pl.BlockSpec((s, POOL_W), full), pl.BlockSpec((4, 128, 128), lambda i: (0, 0, 0)),
                   pl.BlockSpec((1, POOL_W), full)],
        out_shape=[jax.ShapeDtypeStruct((s, POOL_W), BF16), jax.ShapeDtypeStruct((4, 128, 128), F32),
                   jax.ShapeDtypeStruct((1, POOL_W), F32)],
        compiler_params=_cp("arbitrary"),
    )(p, dpp, wgrp, scale)


def _gla_decay(zgk_ref, wgk_ref, bgk_ref, rb):
    g = _dot(zgk_ref[...], wgk_ref[...].astype(BF16)) + bgk_ref[...]
    la = (jnp.minimum(g, 0.0) - jnp.log(1.0 + jnp.exp(-jnp.abs(g)))) * (1.0 / 16.0)
    rowm = _rows(la.shape) & (CHUNK - 1)
    bc, k = la, 1
    while k < CHUNK:
        bc = bc + jnp.where(rowm >= k, pltpu.roll(bc, k, 0), 0.0)
        k *= 2
    return g, jnp.exp(bc), jnp.exp(-bc)


GLA_HB = 4


def _gla_specs(rb, rmap):
    wk, wv = GLA_HB * HK, GLA_HB * HV
    return [pl.BlockSpec((rb, wk), lambda h, r: (rmap(h, r), OFF_Q // wk + h)),
            pl.BlockSpec((rb, wk), lambda h, r: (rmap(h, r), OFF_K // wk + h)),
            pl.BlockSpec((rb, wv), lambda h, r: (rmap(h, r), OFF_V // wv + h)),
            pl.BlockSpec((rb, 128), lambda h, r: (rmap(h, r), OFF_GK // 128))]


def _gla_fwd(zr, wgk, bgk, ghead, rb):
    s = zr.shape[0]
    nc = rb // CHUNK
    wk, wv = GLA_HB * HK, GLA_HB * HV

    def body(q_ref, k_ref, v_ref, zgk_ref, zog_ref, wgk_ref, bgk_ref, gh_ref, o_ref, og_ref, sp_ref, st_ref):
        @pl.when(pl.program_id(1) == 0)
        def _():
            st_ref[...] = jnp.zeros_like(st_ref)

        _, e_pos, e_neg = _gla_decay(zgk_ref, wgk_ref, bgk_ref, rb)
        lower = _rows((CHUNK, CHUNK)) >= lax.broadcasted_iota(jnp.int32, (CHUNK, CHUNK), 1)
        for c in range(nc):
            sl = slice(c * CHUNK, (c + 1) * CHUNK)
            for hh in range(GLA_HB):
                ck, cv = slice(hh * HK, (hh + 1) * HK), slice(hh * HV, (hh + 1) * HV)
                q = q_ref[sl, ck].astype(F32) * QSCALE
                k = k_ref[sl, ck].astype(F32)
                v = v_ref[sl, cv]
                ec, fc = e_pos[sl, ck], e_neg[sl, ck]
                qfw = (q * ec).astype(BF16)
                kfw_f = k * fc
                s_fw = _dot_nt(qfw, kfw_f.astype(BF16))
                s_bw = _dot_nt((q * fc).astype(BF16), (k * ec).astype(BF16))
                pm = jnp.where(lower, s_fw, s_bw).astype(BF16)
                st = st_ref[hh]
                stb = st.astype(BF16)
                sp_ref[c, hh] = stb
                o = _dot(pm, v) + _dot_nt(qfw, stb)
                e_last = _pick_row(ec, CHUNK - 1)
                kdec = (kfw_f * e_last).astype(BF16)
                st_ref[hh] = st * e_last + _dot_tn(v, kdec)
                r = lax.rsqrt(jnp.mean(o * o, axis=-1, keepdims=True) + EPS)
                zo = zog_ref[sl, cv].astype(F32)
                o_ref[sl, cv] = o.astype(BF16)
                og_ref[sl, cv] = (o * r * gh_ref[...] * zo * _sigmoid(zo)).astype(BF16)

    rmap = lambda h, r: r
    return pl.pallas_call(
        body, name="gla_fwd", grid=(HEADS // GLA_HB, s // rb),
        in_specs=_gla_specs(rb, rmap) + [
            pl.BlockSpec((rb, wv), lambda h, r: (r, OFF_OG // wv + h)),
            pl.BlockSpec((128, wk), lambda h, r: (0, h)), pl.BlockSpec((1, wk), lambda h, r: (0, h)),
            pl.BlockSpec((1, HV), lambda h, r: (0, 0))],
        out_specs=[pl.BlockSpec((rb, wv), lambda h, r: (r, h)), pl.BlockSpec((rb, wv), lambda h, r: (r, h)),
                   pl.BlockSpec((nc, GLA_HB, HV, HK), lambda h, r: (r, h, 0, 0))],
        out_shape=[jax.ShapeDtypeStruct((s, D), BF16), jax.ShapeDtypeStruct((s, D), BF16),
                   jax.ShapeDtypeStruct((s // CHUNK, HEADS, HV, HK), BF16)],
        scratch_shapes=[pltpu.VMEM((GLA_HB, HV, HK), F32)],
        compiler_params=_cp("arbitrary", "arbitrary"),
    )(zr, zr, zr, zr, zr, wgk, bgk, ghead)


def _gla_bwd(zr, do, sp, wgk, bgk, rb):
    s = zr.shape[0]
    nc = rb // CHUNK
    nr = s // rb
    wk, wv = GLA_HB * HK, GLA_HB * HV

    def body(q_ref, k_ref, v_ref, zgk_ref, do_ref, sp_ref, wgk_ref, bgk_ref, dq_ref, dk_ref, dv_ref, dg_ref,
             gt_ref, dbc_ref):
        @pl.when(pl.program_id(1) == 0)
        def _():
            gt_ref[...] = jnp.zeros_like(gt_ref)

        g, e_pos, e_neg = _gla_decay(zgk_ref, wgk_ref, bgk_ref, rb)
        lower = _rows((CHUNK, CHUNK)) >= lax.broadcasted_iota(jnp.int32, (CHUNK, CHUNK), 1)
        is_last = _rows((CHUNK, HK)) == CHUNK - 1
        for c in reversed(range(nc)):
            sl = slice(c * CHUNK, (c + 1) * CHUNK)
            for hh in range(GLA_HB):
                ck, cv = slice(hh * HK, (hh + 1) * HK), slice(hh * HV, (hh + 1) * HV)
                q = q_ref[sl, ck].astype(F32) * QSCALE
                k = k_ref[sl, ck].astype(F32)
                v = v_ref[sl, cv]
                dov = do_ref[sl, cv]
                ec, fc = e_pos[sl, ck], e_neg[sl, ck]
                qfw_f, kfw_f, qbw_f, kbw_f = q * ec, k * fc, q * fc, k * ec
                qfw, kfw, qbw, kbw = qfw_f.astype(BF16), kfw_f.astype(BF16), qbw_f.astype(BF16), kbw_f.astype(BF16)
                pm = jnp.where(lower, _dot_nt(qfw, kfw), _dot_nt(qbw, kbw)).astype(BF16)
                e_last = _pick_row(ec, CHUNK - 1)
                kdec = (kfw_f * e_last).astype(BF16)
                gt = gt_ref[hh]
                gtb = gt.astype(BF16)
                spv = sp_ref[c, hh]
                dp = _dot_nt(dov, v)
                dv_ref[sl, cv] = (_dot_tn(pm, dov) + _dot_nt(kdec, gtb)).astype(BF16)
                ds_fw = jnp.where(lower, dp, 0.0).astype(BF16)
                ds_bw = jnp.where(lower, 0.0, dp).astype(BF16)
                dqfw = _dot(ds_fw, kfw) + _dot(dov, spv)
                dkfw = _dot_tn(ds_fw, qfw)
                dqbw = _dot(ds_bw, kbw)
                dkbw = _dot_tn(ds_bw, qbw)
                dkdec = _dot(v, gtb)
                de_last = (jnp.sum(gt * spv.astype(F32), axis=0, keepdims=True)
                           + jnp.sum(dkdec * kfw_f, axis=0, keepdims=True))
                dkfw = dkfw + dkdec * e_last
                dq_ref[sl, ck] = ((dqfw * ec + dqbw * fc) * QSCALE).astype(BF16)
                dk_ref[sl, ck] = (dkfw * fc + dkbw * ec).astype(BF16)
                dbc = dqfw * qfw_f - dqbw * qbw_f + dkbw * kbw_f - dkfw * kfw_f
                dbc_ref[sl, ck] = dbc + jnp.where(is_last, de_last * e_last, 0.0)
                gt_ref[hh] = _dot_tn(dov, qfw) + gt * e_last
        rowm = _rows((rb, wk)) & (CHUNK - 1)
        dla, kk = dbc_ref[...], 1
        while kk < CHUNK:
            dla = dla + jnp.where(rowm < CHUNK - kk, pltpu.roll(dla, rb - kk, 0), 0.0)
            kk *= 2
        dg_ref[...] = dla * (1.0 / 16.0) * _sigmoid(-g)

    rmap = lambda h, r: nr - 1 - r
    rev = lambda h, r: (nr - 1 - r, h)
    return pl.pallas_call(
        body, name="gla_bwd", grid=(HEADS // GLA_HB, nr),
        in_specs=_gla_specs(rb, rmap) + [
            pl.BlockSpec((rb, wv), rev),
            pl.BlockSpec((nc, GLA_HB, HV, HK), lambda h, r: (nr - 1 - r, h, 0, 0)),
            pl.BlockSpec((128, wk), lambda h, r: (0, h)), pl.BlockSpec((1, wk), lambda h, r: (0, h))],
        out_specs=[pl.BlockSpec((rb, wk), rev), pl.BlockSpec((rb, wk), rev), pl.BlockSpec((rb, wv), rev),
                   pl.BlockSpec((rb, wk), rev)],
        out_shape=[jax.ShapeDtypeStruct((s, HEADS * HK), BF16), jax.ShapeDtypeStruct((s, HEADS * HK), BF16),
                   jax.ShapeDtypeStruct((s, D), BF16), jax.ShapeDtypeStruct((s, HEADS * HK), F32)],
        scratch_shapes=[pltpu.VMEM((GLA_HB, HV, HK), F32), pltpu.VMEM((rb, wk), F32)],
        compiler_params=_cp("arbitrary", "arbitrary"),
    )(zr, zr, zr, zr, do, sp, wgk, bgk)


def _gk_bwd(dgpre, zr, wgk, ts):
    s = zr.shape[0]

    def body(dg_ref, zgk_ref, w_ref, dz_ref, dw_ref, db_ref):
        @pl.when(pl.program_id(0) == 0)
        def _():
            dw_ref[...] = jnp.zeros_like(dw_ref)
            db_ref[...] = jnp.zeros_like(db_ref)

        dg = dg_ref[...]
        dgb = dg.astype(BF16)
        dz_ref[...] = _dot_nt(dgb, w_ref[...].astype(BF16)).astype(BF16)
        dw_ref[...] += _dot_tn(zgk_ref[...], dgb)
        db_ref[...] += jnp.sum(dg, axis=0, keepdims=True)

    return pl.pallas_call(
        body, name="gk_bwd", grid=(s // ts,),
        in_specs=[pl.BlockSpec((ts, 512), lambda i: (i, 0)), pl.BlockSpec((ts, 128), lambda i: (i, OFF_GK // 128)),
                  pl.BlockSpec((128, 512), lambda i: (0, 0))],
        out_specs=[pl.BlockSpec((ts, 128), lambda i: (i, 0)), pl.BlockSpec((128, 512), lambda i: (0, 0)),
                   pl.BlockSpec((1, 512), lambda i: (0, 0))],
        out_shape=[jax.ShapeDtypeStruct((s, 128), BF16), jax.ShapeDtypeStruct((128, 512), F32),
                   jax.ShapeDtypeStruct((1, 512), F32)],
        compiler_params=_cp("arbitrary"),
    )(dgpre, zr, wgk)


def _merge_fwd(x, zr, pp, og, bgate, wpp, wgla, wout, ts):
    s = x.shape[0]

    def body(x_ref, z0_ref, z1_ref, pp_ref, og_ref, bg_ref, wpp_ref, wgla_ref, wout_ref,
             x1_ref, mix_ref, yp_ref, yg_ref):
        ppv = pp_ref[...]
        yp = jnp.concatenate([_dot(ppv, wpp_ref[j]) for j in range(4)], axis=1)
        yg = _dot(og_ref[...], wgla_ref[...])
        g0 = _sigmoid(z0_ref[...].astype(F32) + bg_ref[:, :D])
        g1 = _sigmoid(z1_ref[...].astype(F32) + bg_ref[:, D:])
        mixed = (g0 * yp + g1 * yg).astype(BF16)
        x1_ref[...] = x_ref[...] + _dot(mixed, wout_ref[...])
        mix_ref[...] = mixed
        yp_ref[...] = yp.astype(BF16)
        yg_ref[...] = yg.astype(BF16)

    row = lambda i: (i, 0)
    const2 = lambda i: (0, 0)
    return pl.pallas_call(
        body, name="merge_fwd", grid=(s // ts,),
        in_specs=[pl.BlockSpec((ts, D), row), pl.BlockSpec((ts, D), lambda i: (i, 0)), pl.BlockSpec((ts, D), lambda i: (i, 1)),
                  pl.BlockSpec((ts, POOL_W), row), pl.BlockSpec((ts, D), row), pl.BlockSpec((1, 2 * D), const2),
                  pl.BlockSpec((4, POOL_W, 256), lambda i: (0, 0, 0)), pl.BlockSpec((D, D), const2),
                  pl.BlockSpec((D, D), const2)],
        out_specs=[pl.BlockSpec((ts, D), row)] * 4,
        out_shape=[jax.ShapeDtypeStruct((s, D), F32)] + [jax.ShapeDtypeStruct((s, D), BF16)] * 3,
        compiler_params=_cp("arbitrary"),
    )(x, zr, zr, pp, og, bgate, wpp, wgla, wout)


def _merge_bwd(dx1b, zr, yp, yg, o, bgate, ghead, wpp, wgla, wout, ts):
    s = dx1b.shape[0]

    def body(dx_ref, z0_ref, z1_ref, zog_ref, yp_ref, yg_ref, o_ref, bg_ref, gh_ref, wpp_ref, wgla_ref, wout_ref,
             dzg_ref, dyp_ref, dyg_ref, dpp_ref, do_ref, dzog_ref, dbg_ref, dgh_ref):
        @pl.when(pl.program_id(0) == 0)
        def _():
            dbg_ref[...] = jnp.zeros_like(dbg_ref)
            dgh_ref[...] = jnp.zeros_like(dgh_ref)

        dmix = _dot_nt(dx_ref[...], wout_ref[...])
        g0 = _sigmoid(z0_ref[...].astype(F32) + bg_ref[:, :D])
        g1 = _sigmoid(z1_ref[...].astype(F32) + bg_ref[:, D:])
        dypb = (dmix * g0).astype(BF16)
        dygb = (dmix * g1).astype(BF16)
        dz0 = dmix * yp_ref[...].astype(F32) * g0 * (1.0 - g0)
        dz1 = dmix * yg_ref[...].astype(F32) * g1 * (1.0 - g1)
        dzg_ref[:, :D] = dz0.astype(BF16)
        dzg_ref[:, D:] = dz1.astype(BF16)
        dbg_ref[:, :D] += jnp.sum(dz0, axis=0, keepdims=True)
        dbg_ref[:, D:] += jnp.sum(dz1, axis=0, keepdims=True)
        dyp_ref[...] = dypb
        dyg_ref[...] = dygb
        dpp = _dot_nt(dypb[:, 0:256], wpp_ref[0])
        for j in range(1, 4):
            dpp = dpp + _dot_nt(dypb[:, j * 256:(j + 1) * 256], wpp_ref[j])
        dpp_ref[...] = dpp.astype(BF16)
        dog = _dot_nt(dygb, wgla_ref[...])
        gh = gh_ref[...]
        dgh = jnp.zeros((1, HV), F32)
        for h in range(HEADS):
            cs = slice(h * HV, (h + 1) * HV)
            ov = o_ref[:, cs].astype(F32)
            r = lax.rsqrt(jnp.mean(ov * ov, axis=-1, keepdims=True) + EPS)
            oh = ov * r
            zo = zog_ref[:, cs].astype(F32)
            sg = _sigmoid(zo)
            dog_h = dog[:, cs]
            don = dog_h * zo * sg
            dzog_ref[:, cs] = (dog_h * oh * gh * sg * (1.0 + zo * (1.0 - sg))).astype(BF16)
            dgh = dgh + jnp.sum(don * oh, axis=0, keepdims=True)
            doh = don * gh
            do_ref[:, cs] = (r * (doh - oh * jnp.mean(doh * oh, axis=-1, keepdims=True))).astype(BF16)
        dgh_ref[...] += dgh

    row = lambda i: (i, 0)
    const2 = lambda i: (0, 0)
    return pl.pallas_call(
        body, name="merge_bwd", grid=(s // ts,),
        in_specs=[pl.BlockSpec((ts, D), row), pl.BlockSpec((ts, D), lambda i: (i, 0)), pl.BlockSpec((ts, D), lambda i: (i, 1)),
                  pl.BlockSpec((ts, D), lambda i: (i, OFF_OG // D)), pl.BlockSpec((ts, D), row), pl.BlockSpec((ts, D), row),
                  pl.BlockSpec((ts, D), row), pl.BlockSpec((1, 2 * D), const2), pl.BlockSpec((1, HV), const2),
                  pl.BlockSpec((4, POOL_W, 256), lambda i: (0, 0, 0)), pl.BlockSpec((D, D), const2),
                  pl.BlockSpec((D, D), const2)],
        out_specs=[pl.BlockSpec((ts, 2 * D), row), pl.BlockSpec((ts, D), row), pl.BlockSpec((ts, D), row),
                   pl.BlockSpec((ts, POOL_W), row), pl.BlockSpec((ts, D), row), pl.BlockSpec((ts, D), row),
                   pl.BlockSpec((1, 2 * D), const2), pl.BlockSpec((1, HV), const2)],
        out_shape=[jax.ShapeDtypeStruct((s, 2 * D), BF16), jax.ShapeDtypeStruct((s, D), BF16),
                   jax.ShapeDtypeStruct((s, D), BF16), jax.ShapeDtypeStruct((s, POOL_W), BF16),
                   jax.ShapeDtypeStruct((s, D), BF16), jax.ShapeDtypeStruct((s, D), BF16),
                   jax.ShapeDtypeStruct((1, 2 * D), F32), jax.ShapeDtypeStruct((1, HV), F32)],
        compiler_params=_cp("arbitrary"),
    )(dx1b, zr, zr, zr, yp, yg, o, bgate, ghead, wpp, wgla, wout)


HALO = 16
CCH = 1408


def _conv_taps(u_ref, halo_ref, cs, first, ts):
    u = u_ref[:, cs].astype(F32)
    hal = halo_ref[:, cs].astype(F32)
    h1 = jnp.where(first, 0.0, _pick_row(hal, HALO - 1))
    h2 = jnp.where(first, 0.0, _pick_row(hal, HALO - 2))
    row = _rows(u.shape)
    r1 = jnp.where(row == 0, h1, pltpu.roll(u, 1, 0))
    r2 = jnp.where(row == 0, h2, jnp.where(row == 1, h1, pltpu.roll(u, 2, 0)))
    return u, r1, r2


def _ffn_down_loss(u, x1, tgt, wconv, bconv, wdown, gfin, ts):
    s = x1.shape[0]

    def body(u_ref, halo_ref, x1_ref, t_ref, wc_ref, bc_ref, wd_ref, gf_ref, a_ref, c_ref, dx_ref, dxb_ref, ls_ref,
             dgf_ref):
        i = pl.program_id(0)

        @pl.when(i == 0)
        def _():
            ls_ref[...] = jnp.zeros_like(ls_ref)
            dgf_ref[...] = jnp.zeros_like(dgf_ref)

        first = i == 0
        acc = x1_ref[...]
        for hf in range(2):
            cg = slice(hf * CCH, (hf + 1) * CCH)
            cv = slice(D_FF + hf * CCH, D_FF + (hf + 1) * CCH)
            vals = []
            for cs in (cg, cv):
                u0, u1, u2 = _conv_taps(u_ref, halo_ref, cs, first, ts)
                vals.append(bc_ref[:, cs] + wc_ref[0:1, cs] * u2 + wc_ref[1:2, cs] * u1 + wc_ref[2:3, cs] * u0)
                c_ref[:, cs] = vals[-1].astype(BF16)
            a = (vals[0] * _sigmoid(vals[0]) * vals[1]).astype(BF16)
            a_ref[:, cg] = a
            acc = acc + _dot(a, wd_ref[cg, :])
        r = lax.rsqrt(jnp.mean(acc * acc, axis=-1, keepdims=True) + EPS)
        xh = acc * r
        gf = gf_ref[...]
        err = xh * gf - t_ref[...]
        ls_ref[...] += (0.5 / D) * jnp.sum(jnp.sum(err * err, axis=-1, keepdims=True), axis=0, keepdims=True)
        dy = err * (1.0 / D)
        dgf_ref[...] += jnp.sum(dy * xh, axis=0, keepdims=True)
        dxh = dy * gf
        dx = r * (dxh - xh * jnp.mean(dxh * xh, axis=-1, keepdims=True))
        dx_ref[...] = dx
        dxb_ref[...] = dx.astype(BF16)

    row = lambda i: (i, 0)
    const2 = lambda i: (0, 0)
    return pl.pallas_call(
        body, name="ffn_down_loss", grid=(s // ts,),
        in_specs=[pl.BlockSpec((ts, N_UP), row),
                  pl.BlockSpec((HALO, N_UP), lambda i: (jnp.maximum(i * (ts // HALO) - 1, 0), 0)),
                  pl.BlockSpec((ts, D), row), pl.BlockSpec((ts, D), row), pl.BlockSpec((3, N_UP), const2),
                  pl.BlockSpec((1, N_UP), const2), pl.BlockSpec((D_FF, D), const2), pl.BlockSpec((1, D), const2)],
        out_specs=[pl.BlockSpec((ts, D_FF), row), pl.BlockSpec((ts, N_UP), row), pl.BlockSpec((ts, D), row),
                   pl.BlockSpec((ts, D), row), pl.BlockSpec((1, 128), const2), pl.BlockSpec((1, D), const2)],
        out_shape=[jax.ShapeDtypeStruct((s, D_FF), BF16), jax.ShapeDtypeStruct((s, N_UP), BF16),
                   jax.ShapeDtypeStruct((s, D), F32), jax.ShapeDtypeStruct((s, D), BF16),
                   jax.ShapeDtypeStruct((1, 128), F32), jax.ShapeDtypeStruct((1, D), F32)],
        compiler_params=_cp("arbitrary"),
    )(u, u, x1, tgt, wconv, bconv, wdown, gfin)


def _ffn_bwd(dx2b, u, c, wconv, wdown, ts):
    s = dx2b.shape[0]
    nt = s // ts

    def body(dx_ref, u_ref, c_ref, wc_ref, wd_ref, du_ref, db_ref, dw_ref, nxt_ref):
        @pl.when(pl.program_id(0) == 0)
        def _():
            db_ref[...] = jnp.zeros_like(db_ref)
            dw_ref[...] = jnp.zeros_like(dw_ref)
            nxt_ref[...] = jnp.zeros_like(nxt_ref)

        dxv = dx_ref[...]
        row = _rows((ts, CCH))
        for hf in range(2):
            cg = slice(hf * CCH, (hf + 1) * CCH)
            cv = slice(D_FF + hf * CCH, D_FF + (hf + 1) * CCH)
            da = _dot_nt(dxv, wd_ref[cg, :])
            gate = c_ref[:, cg].astype(F32)
            val = c_ref[:, cv].astype(F32)
            sg = _sigmoid(gate)
            dcs = (da * val * sg * (1.0 + gate * (1.0 - sg)), da * gate * sg)
            for cs, dc in zip((cg, cv), dcs):
                n1 = nxt_ref[0:1, cs]
                n2 = nxt_ref[1:2, cs]
                f1 = jnp.where(row == ts - 1, n1, pltpu.roll(dc, ts - 1, 0))
                f2 = jnp.where(row == ts - 1, n2, jnp.where(row == ts - 2, n1, pltpu.roll(dc, ts - 2, 0)))
                uv = u_ref[:, cs].astype(F32)
                db_ref[:, cs] += jnp.sum(dc, axis=0, keepdims=True)
                dw_ref[0:1, cs] += jnp.sum(f2 * uv, axis=0, keepdims=True)
                dw_ref[1:2, cs] += jnp.sum(f1 * uv, axis=0, keepdims=True)
                dw_ref[2:3, cs] += jnp.sum(dc * uv, axis=0, keepdims=True)
                du_ref[:, cs] = (wc_ref[2:3, cs] * dc + wc_ref[1:2, cs] * f1 + wc_ref[0:1, cs] * f2).astype(BF16)
                nxt_ref[:, cs] = dc[0:8, :]

    rev = lambda i: (nt - 1 - i, 0)
    const2 = lambda i: (0, 0)
    return pl.pallas_call(
        body, name="ffn_bwd", grid=(nt,),
        in_specs=[pl.BlockSpec((ts, D), rev), pl.BlockSpec((ts, N_UP), rev), pl.BlockSpec((ts, N_UP), rev),
                  pl.BlockSpec((3, N_UP), const2), pl.BlockSpec((D_FF, D), const2)],
        out_specs=[pl.BlockSpec((ts, N_UP), rev), pl.BlockSpec((1, N_UP), const2), pl.BlockSpec((3, N_UP), const2)],
        out_shape=[jax.ShapeDtypeStruct((s, N_UP), BF16), jax.ShapeDtypeStruct((1, N_UP), F32),
                   jax.ShapeDtypeStruct((3, N_UP), F32)],
        scratch_shapes=[pltpu.VMEM((8, N_UP), F32)],
        compiler_params=_cp("arbitrary"),
    )(dx2b, u, c, wconv, wdown)


ANY = pl.BlockSpec(memory_space=pl.ANY)


def _place():
    x, y, c = lax.axis_index("x"), lax.axis_index("y"), lax.axis_index("c")
    chips = [(1 - x, y), (x, 1 - y), (1 - x, 1 - y)]
    return x, y, c, chips


def _half(shape, c, axis):
    size = shape[axis] // 2
    cut = pl.ds(pl.multiple_of(c * size, 8 if axis == 0 else 128), size)
    return (cut, slice(None)) if axis == 0 else (slice(None), cut)


def _half_shape(shape, axis):
    return (shape[0] // 2, shape[1]) if axis == 0 else (shape[0], shape[1] // 2)


def _remote(src, dst, send_sems, recv_sems, k, to):
    return pltpu.make_async_remote_copy(src_ref=src, dst_ref=dst, send_sem=send_sems.at[k], recv_sem=recv_sems.at[k],
                                        device_id=to, device_id_type=MESH)


def _all_gather_weights(big, axes, small):
    nb, ns = len(big), len(small)
    n = nb + ns
    n_sem = 6 * nb + 3 * ns

    def body(*refs):
        ins, outs = refs[:n], refs[n:2 * n]
        send_sems, recv_sems = refs[2 * n:]
        x, y, c, chips = _place()
        me = 2 * x + y
        sib = (x, y, 1 - c)
        started = []
        for a in range(nb):
            mine = _half(big[a].shape, c, axes[a])
            for k, ch in enumerate(chips):
                cp = _remote(ins[a].at[mine], outs[a].at[(me,) + mine], send_sems, recv_sems, 6 * a + k,
                             (ch[0], ch[1], c))
                cp.start()
                started.append(cp)
        for a in range(ns):
            for k, ch in enumerate(chips):
                cp = _remote(ins[nb + a], outs[nb + a].at[me], send_sems, recv_sems, 6 * nb + 3 * a + k,
                             (ch[0], ch[1], c))
                cp.start()
                started.append(cp)
        for a in range(nb):
            mine = _half(big[a].shape, c, axes[a])
            for k, ch in enumerate(chips):
                landed = outs[a].at[(2 * ch[0] + ch[1],) + mine]
                _remote(landed, landed, send_sems, recv_sems, 6 * a + k, sib).wait_recv()
                cp = _remote(landed, landed, send_sems, recv_sems, 6 * a + 3 + k, sib)
                cp.start()
                started.append(cp)
        for a in range(nb):
            other = _half(big[a].shape, 1 - c, axes[a])
            for k, ch in enumerate(chips):
                landed = outs[a].at[(2 * ch[0] + ch[1],) + other]
                _remote(landed, landed, send_sems, recv_sems, 6 * a + 3 + k, sib).wait_recv()
        for a in range(ns):
            for k, ch in enumerate(chips):
                landed = outs[nb + a].at[2 * ch[0] + ch[1]]
                _remote(landed, landed, send_sems, recv_sems, 6 * nb + 3 * a + k, sib).wait_recv()
        for cp in started:
            cp.wait_send()

    arrs = list(big) + list(small)
    return pl.pallas_call(
        body, name="all_gather_weights",
        in_specs=[ANY] * n, out_specs=[ANY] * n,
        out_shape=[jax.ShapeDtypeStruct((4,) + a.shape, a.dtype) for a in arrs],
        scratch_shapes=[pltpu.SemaphoreType.DMA((n_sem,)), pltpu.SemaphoreType.DMA((n_sem,))],
        compiler_params=pltpu.CompilerParams(has_side_effects=True),
    )(*arrs)


def _sibling_exchange(grads, axes, small, name):
    nb = len(grads)
    n = nb + (small is not None)

    def body(*refs):
        ins, outs = refs[:n], refs[n:2 * n]
        send_sems, recv_sems = refs[2 * n:]
        x, y, c, _ = _place()
        sib = (x, y, 1 - c)
        cps = []
        for a in range(nb):
            theirs = _half(grads[a].shape[1:], 1 - c, axes[a])
            cps.append(_remote(ins[a].at[(slice(None),) + theirs], outs[a], send_sems, recv_sems, a, sib))
        if small is not None:
            cps.append(_remote(ins[nb], outs[nb], send_sems, recv_sems, nb, sib))
        for cp in cps:
            cp.start()
        for cp in cps:
            cp.wait()

    out_shape = [jax.ShapeDtypeStruct((4,) + _half_shape(g.shape[1:], ax), F32) for g, ax in zip(grads, axes)]
    if small is not None:
        out_shape.append(jax.ShapeDtypeStruct(small.shape, F32))
    return pl.pallas_call(
        body, name=name, in_specs=[ANY] * n, out_specs=[ANY] * n, out_shape=out_shape,
        scratch_shapes=[pltpu.SemaphoreType.DMA((n,)), pltpu.SemaphoreType.DMA((n,))],
        compiler_params=pltpu.CompilerParams(has_side_effects=True),
    )(*grads, *([] if small is None else [small]))


def _gather_share(lands, axes, name):
    n = len(lands)

    def body(*refs):
        outs = refs[n:2 * n]
        send_sems, recv_sems = refs[2 * n:]
        x, y, c, chips = _place()
        sib = (x, y, 1 - c)
        cps = []
        for a in range(n):
            mine = _half(lands[a].shape[1:], c, axes[a])
            for k, ch in enumerate(chips):
                landed = outs[a].at[(2 * ch[0] + ch[1],) + mine]
                cps.append(_remote(landed, landed, send_sems, recv_sems, 3 * a + k, sib))
        for cp in cps:
            cp.start()
        for a in range(n):
            other = _half(lands[a].shape[1:], 1 - c, axes[a])
            for k, ch in enumerate(chips):
                landed = outs[a].at[(2 * ch[0] + ch[1],) + other]
                _remote(landed, landed, send_sems, recv_sems, 3 * a + k, sib).wait_recv()
        for cp in cps:
            cp.wait_send()

    return pl.pallas_call(
        body, name=name, in_specs=[ANY] * n, out_specs=[ANY] * n,
        out_shape=[jax.ShapeDtypeStruct(a.shape, a.dtype) for a in lands],
        input_output_aliases={a: a for a in range(n)},
        scratch_shapes=[pltpu.SemaphoreType.DMA((3 * n,)), pltpu.SemaphoreType.DMA((3 * n,))],
        compiler_params=pltpu.CompilerParams(has_side_effects=True),
    )(*lands)


def _sibling_share(halves, name):
    n = len(halves)

    def body(*refs):
        ins, outs = refs[:n], refs[n:2 * n]
        send_sems, recv_sems = refs[2 * n:]
        x, y, c, _ = _place()
        cps = [_remote(ins[a], outs[a], send_sems, recv_sems, a, (x, y, 1 - c)) for a in range(n)]
        for cp in cps:
            cp.start()
        for cp in cps:
            cp.wait()

    return pl.pallas_call(
        body, name=name, in_specs=[ANY] * n, out_specs=[ANY] * n,
        out_shape=[jax.ShapeDtypeStruct(h.shape, F32) for h in halves],
        scratch_shapes=[pltpu.SemaphoreType.DMA((n,)), pltpu.SemaphoreType.DMA((n,))],
        compiler_params=pltpu.CompilerParams(has_side_effects=True),
    )(*halves)


HBM = pl.BlockSpec(memory_space=pltpu.HBM)
SEM = pl.BlockSpec(memory_space=pltpu.SEMAPHORE)
DATAFLOW = pltpu.SideEffectType.DATAFLOW_SIDE_EFFECTING


def _split_start(name, srcs, land_shapes, plan, n_copies, after):
    lands = [lax.empty(shp, dt) for shp, dt in land_shapes]
    bufs = list(srcs) + lands
    nb, ns = len(bufs), len(srcs)

    def body(*refs):
        send_sems, recv_sems, token = refs[nb + 1], refs[nb + 2], refs[-1]
        for k, (src, dst, to) in enumerate(plan(refs[:ns], refs[ns:nb])):
            _remote(src, dst, send_sems, recv_sems, k, to).start()
        token[...] = jnp.zeros_like(token)

    res = pl.pallas_call(
        body, name=name,
        out_shape=(pltpu.SemaphoreType.DMA((n_copies,)), pltpu.SemaphoreType.DMA((n_copies,)),
                   *[pltpu.HBM(b.shape, b.dtype) for b in bufs], jax.ShapeDtypeStruct((8, 128), F32)),
        in_specs=[HBM] * nb + [ANY],
        out_specs=(SEM, SEM, *[HBM] * nb, pl.BlockSpec(memory_space=pltpu.VMEM)),
        input_output_aliases={i: 2 + i for i in range(nb)},
        compiler_params=pltpu.CompilerParams(has_side_effects=DATAFLOW),
    )(*[pltpu.with_memory_space_constraint(b, pltpu.HBM) for b in bufs], after)
    return (res[0], res[1], list(res[2:2 + nb])), res[-1]


def _split_wait(name, handle, n_srcs, plan, after):
    send_sems, recv_sems, bufs = handle
    nb = len(bufs)

    def body(*refs):
        sends, recvs = refs[nb], refs[nb + 1]
        for k, (src, dst, to) in enumerate(plan(refs[:n_srcs], refs[n_srcs:nb])):
            cp = _remote(src, dst, sends, recvs, k, to)
            cp.wait_send()
            cp.wait_recv()

    res = pl.pallas_call(
        body, name=name, out_shape=[pltpu.HBM(b.shape, b.dtype) for b in bufs],
        in_specs=[HBM] * nb + [SEM, SEM, ANY], out_specs=[HBM] * nb,
        input_output_aliases={i: i for i in range(nb)},
        compiler_params=pltpu.CompilerParams(has_side_effects=DATAFLOW),
    )(*bufs, send_sems, recv_sems, after)
    return list(res[:n_srcs]), list(res[n_srcs:])


def _gather_plan(shapes, axes, n_whole=0):
    def plan(srcs, lands):
        x, y, c, chips = _place()
        out = []
        for a, (shape, axis) in enumerate(zip(shapes, axes)):
            mine = _half(shape, c, axis)
            for ch in chips:
                out.append((srcs[a].at[mine], lands[a].at[(2 * x + y,) + mine], (ch[0], ch[1], c)))
        for a in range(len(shapes), len(shapes) + n_whole):
            for ch in chips:
                out.append((srcs[a], lands[a].at[2 * x + y], (ch[0], ch[1], c)))
        return out
    return plan


def _sibling_plan(shapes, axes):
    def plan(srcs, lands):
        x, y, c, _ = _place()
        return [(srcs[a].at[(slice(None),) + _half(shape, 1 - c, axis)], lands[a], (x, y, 1 - c))
                for a, (shape, axis) in enumerate(zip(shapes, axes))]
    return plan


def _reduce_plan(n_big, with_small):
    def plan(srcs, lands):
        x, y, c, chips = _place()
        out = []
        for a in range(n_big):
            for k, ch in enumerate(chips):
                out.append((srcs[a].at[2 * ch[0] + ch[1]], lands[a].at[k], (ch[0], ch[1], c)))
        if with_small:
            for ch in chips:
                out.append((srcs[n_big], lands[n_big].at[2 * x + y], (ch[0], ch[1], c)))
        return out
    return plan


def _row_tile(rows, cols, mult):
    best = mult
    for t in range(mult, rows + 1, mult):
        if rows % t == 0 and t * cols * 4 <= (1 << 20):
            best = t
    return best if rows % best == 0 else rows


COL_TILE = 256


def _half_tiling(hshape, axis, mult):
    hr, hc = hshape
    if axis == 0:
        tr = _row_tile(hr, hc, mult)
        return tr, hc, hr // tr
    return hr, COL_TILE, hc // COL_TILE


def _tile_idx(axis, t):
    return (t, 0) if axis == 0 else (0, t)


def _chip_partial(place, g, t, axis, name):
    hshape = t.shape[1:]
    br, bc, nt = _half_tiling(hshape, axis, 16)

    def body(pl_ref, g_ref, t_ref, pf_ref, pb_ref):
        v = g_ref[...] + t_ref[...]
        pb_ref[...] = v.astype(BF16)

        @pl.when(pl.program_id(1) == pl_ref[0])
        def _():
            pf_ref[...] = v

    blk = (None, br, bc)
    return pl.pallas_call(
        body, name=name,
        grid_spec=pltpu.PrefetchScalarGridSpec(
            num_scalar_prefetch=1, grid=(nt, 4),
            in_specs=[pl.BlockSpec(blk, lambda i, j, p: (j,) + _tile_idx(axis, p[1] * nt + i)),
                      pl.BlockSpec(blk, lambda i, j, p: (j,) + _tile_idx(axis, i))],
            out_specs=[pl.BlockSpec((br, bc), lambda i, j, p: _tile_idx(axis, i)),
                       pl.BlockSpec(blk, lambda i, j, p: (j,) + _tile_idx(axis, i))]),
        out_shape=[jax.ShapeDtypeStruct(hshape, F32), jax.ShapeDtypeStruct((4,) + hshape, BF16)],
        compiler_params=_cp("arbitrary", "arbitrary"),
    )(place, g, t)


def _finish_half(pf, rb, axis, name):
    hshape = pf.shape
    br, bc, nt = _half_tiling(hshape, axis, 16)

    def body(pf_ref, rb_ref, o_ref):
        o_ref[...] = ((pf_ref[...] + rb_ref[0].astype(F32)) + rb_ref[1].astype(F32)) + rb_ref[2].astype(F32)

    return pl.pallas_call(
        body, name=name, grid=(nt,),
        in_specs=[pl.BlockSpec((br, bc), lambda i: _tile_idx(axis, i)),
                  pl.BlockSpec((3, br, bc), lambda i: (0,) + _tile_idx(axis, i))],
        out_specs=pl.BlockSpec((br, bc), lambda i: _tile_idx(axis, i)),
        out_shape=jax.ShapeDtypeStruct(hshape, F32),
        compiler_params=_cp("arbitrary"),
    )(pf, rb)


def _add2(a, b, name):
    def body(a_ref, b_ref, o_ref):
        o_ref[...] = a_ref[...] + b_ref[...]

    return pl.pallas_call(body, name=name, out_shape=jax.ShapeDtypeStruct(a.shape, F32))(a, b)


def _adam_math(w, g, m, v):
    m = ADAM_B1 * m + (1.0 - ADAM_B1) * g
    v = ADAM_B2 * v + (1.0 - ADAM_B2) * (g * g)
    m_hat = m / (1.0 - ADAM_B1 ** ADAM_STEP)
    v_hat = v / (1.0 - ADAM_B2 ** ADAM_STEP)
    return -ADAM_LR * (m_hat / (jnp.sqrt(v_hat) + ADAM_EPS) + ADAM_WD * w), m, v


def _adam_halves(place, w, mine, theirs, m, v, axis, name):
    br, bc, nt = _half_tiling(mine.shape, axis, 8)

    def body(pl_ref, w_ref, a_ref, b_ref, m_ref, v_ref, g_ref, d_ref, mo_ref, vo_ref):
        is_mine = pl.program_id(0) // nt == pl_ref[1]
        g = jnp.where(is_mine, a_ref[...], b_ref[...])
        d, mn, vn = _adam_math(w_ref[...], g, m_ref[...], v_ref[...])
        g_ref[...] = g
        d_ref[...] = d
        mo_ref[...] = mn
        vo_ref[...] = vn

    full = pl.BlockSpec((br, bc), lambda i, p: _tile_idx(axis, i))
    half = pl.BlockSpec((br, bc), lambda i, p: _tile_idx(axis, i % nt))
    return pl.pallas_call(
        body, name=name,
        grid_spec=pltpu.PrefetchScalarGridSpec(
            num_scalar_prefetch=1, grid=(2 * nt,), in_specs=[full, half, half, full, full], out_specs=[full] * 4),
        out_shape=[jax.ShapeDtypeStruct(w.shape, F32)] * 4, compiler_params=_cp("arbitrary"),
    )(place, w, mine, theirs, m, v)


def _sum_chips(chip_sums):
    def body(s_ref, g_ref):
        g_ref[...] = ((s_ref[0] + s_ref[1]) + s_ref[2]) + s_ref[3]

    return pl.pallas_call(body, name="sum_small", out_shape=jax.ShapeDtypeStruct(chip_sums.shape[1:], F32))(chip_sums)


def _adam_small(ws, gs, ms, vs):
    n = len(ws)

    def body(*refs):
        for i in range(n):
            d, mn, vn = _adam_math(refs[i][...], refs[n + i][...], refs[2 * n + i][...], refs[3 * n + i][...])
            refs[4 * n + i][...] = d
            refs[5 * n + i][...] = mn
            refs[6 * n + i][...] = vn

    out = pl.pallas_call(body, name="adam_small",
                         out_shape=[jax.ShapeDtypeStruct(w.shape, F32) for w in ws] * 3)(*ws, *gs, *ms, *vs)
    return out[:n], out[n:2 * n], out[2 * n:]


SMALL = (("g_mix", (1, 1024)), ("b_gate", (1, 2048)), ("w_gk_up", (1, 16, 512)), ("b_gk", (1, 512)),
         ("w_pool_grp", (1, 4, 128, 128)), ("pool_scale", (1, 512)), ("g_gla_head", (1, 256)), ("g_ffn", (1, 1024)),
         ("w_conv", (1, 3, 5632)), ("b_conv", (1, 5632)), ("g_final", (1024,)), ("loss", (768,)))
SMALL_ROWS = 808


def _pack_small(parts):
    flat = jnp.concatenate([parts[n].astype(F32).reshape(-1) for n, _ in SMALL])
    return flat.reshape(SMALL_ROWS, 128)


def _unpack_small(buf):
    flat = buf.reshape(-1)
    out, off = {}, 0
    for n, shp in SMALL:
        size = 1
        for d_ in shp:
            size *= d_
        out[n] = flat[off:off + size].reshape(shp)
        off += size
    return out


def kernel(x, g_mix, w_in, b_gate, w_gk_up, b_gk, w_pool_grp, pool_scale, g_gla_head, w_pool_proj, w_gla_proj, w_out, g_ffn, w_up, w_conv, b_conv, w_down, g_final, loss_target, m_g_mix, m_w_in, m_b_gate, m_w_gk_up, m_b_gk, m_w_pool_grp, m_pool_scale, m_g_gla_head, m_w_pool_proj, m_w_gla_proj, m_w_out, m_g_ffn, m_w_up, m_w_conv, m_b_conv, m_w_down, m_g_final, v_g_mix, v_w_in, v_b_gate, v_w_gk_up, v_b_gk, v_w_pool_grp, v_pool_scale, v_g_gla_head, v_w_pool_proj, v_w_gla_proj, v_w_out, v_g_ffn, v_w_up, v_w_conv, v_b_conv, v_w_down, v_g_final):
    s = x.shape[1]
    ts = min(s, 512)
    tm = min(s, 256)
    cx, cy, cc = lax.axis_index("x"), lax.axis_index("y"), lax.axis_index("c")
    chip = 2 * cx + cy
    place = jnp.stack([chip, cc]).astype(jnp.int32)

    big_names = ("w_in", "w_pool_proj", "w_gla_proj", "w_out", "w_up", "w_down")
    axes = (1, 0, 0, 0, 0, 0)
    shards = dict(w_in=jnp.transpose(w_in[0]), w_pool_proj=w_pool_proj[0], w_gla_proj=w_gla_proj[0], w_out=w_out[0],
                  w_up=w_up[0], w_down=w_down[0])
    def fill_own(lands, mine):
        return [lax.dynamic_update_slice(g, o_[None], (chip, 0, 0)) for g, o_ in zip(lands, mine)]

    def gather_start(tag, halves, group_axes, whole, after):
        plan = _gather_plan([o_.shape for o_ in halves], group_axes, len(whole))
        srcs = list(halves) + list(whole)
        handle, token = _split_start("gather_" + tag + "_start", srcs, [((4,) + o_.shape, o_.dtype) for o_ in srcs], plan,
                                     3 * len(srcs), after)
        return (handle, plan, len(halves), len(srcs), group_axes), token

    def gather_finish(tag, started, after):
        handle, plan, n_halves, n, group_axes = started
        mine, lands = _split_wait("gather_" + tag + "_wait", handle, n, plan, after)
        lands[:n_halves] = _gather_share(lands[:n_halves], group_axes, "gather_" + tag + "_share")
        return fill_own(lands, mine)

    in_w, tok = gather_start("in", [shards["w_in"].astype(BF16)], axes[:1], [], g_mix)
    zero = tok[0, 0]
    own = [(shards[n] + zero).astype(BF16) for n in big_names[1:]]
    mix_w, tok = gather_start("mix", own[0:3], axes[1:4], [w_gk_up[0] + zero, w_conv[0] + zero], tok)
    ffn_w, tok = gather_start("ffn", own[3:5], axes[4:6], [], tok)
    xs, tgt = x[0], loss_target[0]
    wgrp = w_pool_grp[0]
    h = _rmsnorm(xs, g_mix + tok[0:1, 0:1], "norm_mix", ts)
    w_in_t = gather_finish("in", in_w, h)[0].reshape(N_IN, D)
    w_rt = jnp.concatenate([w_in_t[3600:], w_in_t[1536:3584], w_in_t[0:1536], w_in_t[3584:3600],
                            jnp.zeros((128 - GATE_RANK, D), BF16)], axis=0)
    nsh = N_IN // 4

    zr = _matmul_resident(h, w_rt, "in_proj", 1152, transposed=True)
    p, pp = _pool_fwd(zr, wgrp, pool_scale)
    wpp, wgla, wout, wgk4, wconv4 = gather_finish("mix", mix_w, pp)
    wgla, wout = wgla.reshape(D, D), wout.reshape(D, D)
    wgk_full = jnp.transpose(wgk4, (1, 0, 2)).reshape(GATE_RANK, 512)
    wconv_full = jnp.transpose(wconv4, (1, 0, 2)).reshape(3, N_UP)
    wgk_pad = jnp.concatenate([wgk_full, jnp.zeros((128 - GATE_RANK, 512), F32)], axis=0)
    o, og, sp = _gla_fwd(zr, wgk_pad, b_gk, g_gla_head, ts)
    x1, mixed, yp, yg = _merge_fwd(xs, zr, pp, og, b_gate, wpp, wgla, wout, tm)
    wup, wdown = gather_finish("ffn", ffn_w, x1)
    wdown = wdown.reshape(D_FF, D)
    h2 = _rmsnorm(x1, g_ffn, "norm_ffn", ts)
    u = _matmul_resident(h2, wup, "ffn_up", None)
    a, conv_out, dx2, dx2b, loss_part, dgfin = _ffn_down_loss(u, x1, tgt, wconv_full, b_conv, wdown,
                                                              g_final.reshape(1, D), tm)

    du, dbconv, dwconv = _ffn_bwd(dx2b, u, conv_out, wconv_full, wdown, tm)
    dw_down = _matmul_tn(a, dx2b, "dw_down", D, s, tm=1408)
    dw_up = _matmul_tn(h2, du, "dw_up", 1408, s, shard_major=True)

    def exchange_start(tag, grads, group_axes, after):
        plan = _sibling_plan([g.shape[1:] for g in grads], group_axes)
        lands = [((4,) + _half_shape(g.shape[1:], ax), F32) for g, ax in zip(grads, group_axes)]
        handle, token = _split_start("sibling_" + tag + "_start", grads, lands, plan, len(grads), after)
        return (handle, plan, len(grads)), token

    def partials(tag, names, group_axes, exchange, after):
        handle, plan, n = exchange
        mine, theirs = _split_wait("sibling_" + tag + "_wait", handle, n, plan, after)
        return zip(*[_chip_partial(place, g, t, ax, "chip_partial_" + nm)
                     for nm, ax, g, t in zip(names, group_axes, mine, theirs)])

    ffn_names, ffn_axes = ("w_up", "w_down"), (0, 0)
    ffn_x, token = exchange_start("ffn", [dw_up, dw_down.reshape(4, 704, D)], ffn_axes, du)
    dx1, dx1b, dgffn = _matmul_nt_normbwd(du, wup, x1, g_ffn + token[0:1, 0:1], dx2, "ffn_up_bwd", ts)
    ffn_pf, ffn_pb = partials("ffn", ffn_names, ffn_axes, ffn_x, dx1b)
    ffn_plan = _reduce_plan(2, False)
    ffn_handle, token = _split_start("reduce_ffn_start", ffn_pb, [((3,) + p.shape[1:], BF16) for p in ffn_pb],
                                     ffn_plan, 6, ffn_pf[0])

    dzg, dyp, dyg, dpp, do, dzog, dbgate, dghead = _merge_bwd(dx1b, zr, yp, yg, o, b_gate + token[0:1, 0:1], g_gla_head,
                                                             wpp, wgla, wout, tm)
    dw_out = _matmul_tn(mixed, dx1b, "dw_out", D, s)
    dw_gla = _matmul_tn(og, dyg, "dw_gla", D, s)
    dw_pp = _matmul_tn(pp, dyp, "dw_pp", 256, s, shard_major=True)

    out_names, out_axes = ("w_pool_proj", "w_gla_proj", "w_out"), (0, 0, 0)
    out_x, token = exchange_start("out", [dw_pp, dw_gla.reshape(4, 256, D), dw_out.reshape(4, 256, D)], out_axes, dpp)
    dzp, dwgrp, dscale = _pool_bwd(p, dpp, wgrp, pool_scale + token[0:1, 0:1])
    out_pf, out_pb = partials("out", out_names, out_axes, out_x, dzp)
    out_plan = _reduce_plan(3, False)
    out_handle, token = _split_start("reduce_out_start", out_pb, [((3,) + p_.shape[1:], BF16) for p_ in out_pb],
                                     out_plan, 9, out_pf[0])
    dq, dk, dv, dgpre = _gla_bwd(zr, do, sp, wgk_pad, b_gk + token[0:1, 0:1], ts)
    dzgk, dwgk, dbgk = _gk_bwd(dgpre, zr, wgk_pad, ts)
    dzr = jnp.concatenate([dzg, dv, dzog, dzp, dq, dk, dzgk], axis=1)
    dw_rt = _matmul_tn(dzr, h, "dw_in", D, s, tm=1152)

    def grad_rows(lo, hi):
        out = []
        for seg_lo, seg_hi, at in ((0, 1536, OFF_POOL), (1536, 3584, OFF_V), (3584, 3600, OFF_GK), (3600, N_IN, OFF_GATE)):
            a_, b_ = max(lo, seg_lo), min(hi, seg_hi)
            if a_ < b_:
                out.append(dw_rt[at + a_ - seg_lo:at + b_ - seg_lo])
        return jnp.concatenate(out, axis=0)

    dw_in_t = jnp.stack([grad_rows(j * nsh, (j + 1) * nsh) for j in range(4)])

    in_sib = _sibling_exchange([dw_in_t], (1,), None, "sibling_exchange_in")
    in_pf, in_pb = _chip_partial(place, dw_in_t, in_sib[0], 1, "chip_partial_w_in")
    in_plan = _reduce_plan(1, False)
    in_handle, token = _split_start("reduce_in_start", [in_pb], [((3,) + in_pb.shape[1:], BF16)], in_plan, 3, in_pf)
    grad_x, _, dgmix = _matmul_nt_normbwd(dzr, w_rt, xs, g_mix + token[0:1, 0:1], dx1, "in_proj_bwd", ts, transposed=True)
    small_mine = _pack_small(dict(
        g_mix=dgmix, b_gate=dbgate, w_gk_up=dwgk[:GATE_RANK], b_gk=dbgk, w_pool_grp=dwgrp, pool_scale=dscale,
        g_gla_head=dghead, g_ffn=dgffn, w_conv=dwconv, b_conv=dbconv, g_final=dgfin,
        loss=jnp.concatenate([loss_part.reshape(128), jnp.zeros((640,), F32)])))
    small_sib = _sibling_exchange([], (), small_mine, "sibling_exchange_small")[0]
    small_chip = _add2(small_mine, small_sib, "chip_partial_small")
    small_plan = _reduce_plan(0, True)
    small_handle, token = _split_start("reduce_small_start", [small_chip], [((4,) + small_chip.shape, F32)], small_plan, 3,
                                       small_mine)

    ms = dict(w_in=jnp.transpose(m_w_in[0]), w_pool_proj=m_w_pool_proj[0], w_gla_proj=m_w_gla_proj[0], w_out=m_w_out[0],
              w_up=m_w_up[0], w_down=m_w_down[0])
    vs = dict(w_in=jnp.transpose(v_w_in[0]), w_pool_proj=v_w_pool_proj[0], w_gla_proj=v_w_gla_proj[0], w_out=v_w_out[0],
              w_up=v_w_up[0], w_down=v_w_down[0])
    grad, delta, new_m, new_v = {}, {}, {}, {}

    def finish_and_update(names, group_axes, part_f, landed, tag):
        halves = [_finish_half(pf, rb, ax, "finish_" + n) for n, ax, pf, rb in zip(names, group_axes, part_f, landed)]
        sib_halves = _sibling_share(halves, "sibling_share_" + tag)
        for n, ax, mine, theirs in zip(names, group_axes, halves, sib_halves):
            res = _adam_halves(place, shards[n], mine, theirs, ms[n], vs[n], ax, "adam_" + n)
            if n == "w_in":
                res = [jnp.transpose(r_) for r_ in res]
            grad[n], delta[n], new_m[n], new_v[n] = [r_[None] for r_ in res]

    _, ffn_landed = _split_wait("reduce_ffn_wait", ffn_handle, 2, ffn_plan, token)
    _, out_landed = _split_wait("reduce_out_wait", out_handle, 3, out_plan, ffn_landed[0])
    finish_and_update(ffn_names + out_names, ffn_axes + out_axes, ffn_pf + out_pf, ffn_landed + out_landed, "rest")
    _, in_landed = _split_wait("reduce_in_wait", in_handle, 1, in_plan, delta["w_out"])
    finish_and_update(("w_in",), (1,), (in_pf,), in_landed, "in")
    small_sent, small_landed = _split_wait("reduce_small_wait", small_handle, 1, small_plan, delta["w_in"])
    small_sums = lax.dynamic_update_slice(small_landed[0], small_sent[0][None], (chip, 0, 0))

    def narrow(a, width):
        return lax.dynamic_slice_in_dim(a.reshape(a.shape[:-1] + (4, width)), chip, 1, axis=a.ndim - 1).reshape(
            a.shape[:-1] + (width,))

    parts = _unpack_small(_sum_chips(small_sums))
    loss = parts["loss"][0]
    parts["w_gk_up"] = narrow(parts["w_gk_up"], 128)
    parts["w_conv"] = narrow(parts["w_conv"], 1408)
    small_names = [n for n, _ in SMALL[:-1]]
    given = dict(g_mix=(g_mix, m_g_mix, v_g_mix), b_gate=(b_gate, m_b_gate, v_b_gate), w_gk_up=(w_gk_up, m_w_gk_up, v_w_gk_up),
                 b_gk=(b_gk, m_b_gk, v_b_gk), w_pool_grp=(w_pool_grp, m_w_pool_grp, v_w_pool_grp),
                 pool_scale=(pool_scale, m_pool_scale, v_pool_scale), g_gla_head=(g_gla_head, m_g_gla_head, v_g_gla_head),
                 g_ffn=(g_ffn, m_g_ffn, v_g_ffn), w_conv=(w_conv, m_w_conv, v_w_conv), b_conv=(b_conv, m_b_conv, v_b_conv),
                 g_final=(g_final.reshape(1, D), m_g_final.reshape(1, D), v_g_final.reshape(1, D)))
    parts["g_final"] = parts["g_final"].reshape(1, D)
    ds, mo, vo = _adam_small([given[n][0] for n in small_names], [parts[n] for n in small_names],
                             [given[n][1] for n in small_names], [given[n][2] for n in small_names])
    for i, n in enumerate(small_names):
        shp = (D,) if n == "g_final" else parts[n].shape
        grad[n], delta[n], new_m[n], new_v[n] = [r_.reshape(shp) for r_ in (parts[n], ds[i], mo[i], vo[i])]

    order = ("g_mix", "w_in", "b_gate", "w_gk_up", "b_gk", "w_pool_grp", "pool_scale", "g_gla_head", "w_pool_proj",
             "w_gla_proj", "w_out", "g_ffn", "w_up", "w_conv", "b_conv", "w_down", "g_final")
    return (loss, grad_x[None], *[grad[n] for n in order], *[delta[n] for n in order], *[new_m[n] for n in order],
            *[new_v[n] for n in order])
```

```python
import functools

import jax
import jax.numpy as jnp
from jax import lax
from jax.experimental import pallas as pl
from jax.experimental.pallas import tpu as pltpu

F32 = jnp.float32
BF16 = jnp.bfloat16
MESH = pl.DeviceIdType.MESH

D = 1024
EPS = 1e-6
CHUNK = 64
POOL_W = 512
POOL_WINDOWS = (2, 4, 8, 16)
HEADS = 4
HK = 128
HV = 256
GATE_RANK = 16
D_FF = 2816
N_UP = 2 * D_FF
N_IN = 5648
QSCALE = HK ** -0.5
N_INR = 5760
OFF_GATE, OFF_V, OFF_OG, OFF_POOL, OFF_Q, OFF_K, OFF_GK = 0, 2048, 3072, 4096, 4608, 5120, 5632

ADAM_LR, ADAM_B1, ADAM_B2, ADAM_EPS, ADAM_WD, ADAM_STEP = 0.001, 0.9, 0.999, 1e-08, 0.01, 10

VMEM_LIMIT = 56 * 1024 * 1024


def _cp(*sem):
    return pltpu.CompilerParams(dimension_semantics=sem if sem else None, vmem_limit_bytes=VMEM_LIMIT)


def _dot(a, b):
    return jnp.dot(a, b, preferred_element_type=F32)


def _dot_nt(a, b):
    return lax.dot_general(a, b, (((1,), (1,)), ((), ())), preferred_element_type=F32)


def _dot_tn(a, b):
    return lax.dot_general(a, b, (((0,), (0,)), ((), ())), preferred_element_type=F32)


def _sigmoid(v):
    return 1.0 / (1.0 + jnp.exp(-v))


def _rows(shape):
    return lax.broadcasted_iota(jnp.int32, shape, 0)


def _pick_row(v, r):
    return jnp.sum(jnp.where(_rows(v.shape) == r, v, 0.0), axis=0, keepdims=True)


def _rmsnorm(x, g, name, ts):
    s = x.shape[0]

    def body(x_ref, g_ref, h_ref):
        xv = x_ref[...]
        r = lax.rsqrt(jnp.mean(xv * xv, axis=-1, keepdims=True) + EPS)
        h_ref[...] = (xv * r * g_ref[...]).astype(BF16)

    return pl.pallas_call(
        body, name=name, grid=(s // ts,),
        in_specs=[pl.BlockSpec((ts, D), lambda i: (i, 0)), pl.BlockSpec((1, D), lambda i: (0, 0))],
        out_specs=pl.BlockSpec((ts, D), lambda i: (i, 0)), out_shape=jax.ShapeDtypeStruct((s, D), BF16),
        compiler_params=_cp("arbitrary"),
    )(x, g)


MM_ROWS = 512


def _matmul_resident(h, w, name, tn, transposed=False):
    s = h.shape[0]
    if transposed:
        nj = w.shape[0] // tn
        w_spec = pl.BlockSpec((tn, D), lambda j: (j, 0))
    elif w.ndim == 3:
        nj, tn = w.shape[0], w.shape[2]
        w_spec = pl.BlockSpec((None, D, tn), lambda j: (j, 0, 0))
    else:
        nj = w.shape[1] // tn
        w_spec = pl.BlockSpec((D, tn), lambda j: (0, j))
    mm = _dot_nt if transposed else _dot
    rc = min(s, MM_ROWS)

    def body(h_ref, w_ref, z_ref):
        for r0 in range(0, s, rc):
            z_ref[r0:r0 + rc, :] = mm(h_ref[r0:r0 + rc, :], w_ref[...]).astype(BF16)

    return pl.pallas_call(
        body, name=name, grid=(nj,),
        in_specs=[pl.BlockSpec((s, D), lambda j: (0, 0)), w_spec],
        out_specs=pl.BlockSpec((s, tn), lambda j: (0, j)), out_shape=jax.ShapeDtypeStruct((s, nj * tn), BF16),
        compiler_params=_cp("arbitrary"),
    )(h, w)


def _matmul_nt_normbwd(dz, w, x, g, resid, name, ts, transposed=False):
    s = x.shape[0]

    def body(dz_ref, w_hbm, x_ref, g_ref, r_ref, o_ref, ob_ref, dg_ref, w_ref, sem):
        @pl.when(pl.program_id(0) == 0)
        def _():
            cp = pltpu.make_async_copy(w_hbm, w_ref, sem)
            cp.start()
            cp.wait()
            dg_ref[...] = jnp.zeros_like(dg_ref)

        if transposed:
            dh = _dot(dz_ref[...], w_ref[...])
        else:
            kc = w.shape[2]
            dh = _dot_nt(dz_ref[:, 0:kc], w_ref[0])
            for j in range(1, w.shape[0]):
                dh = dh + _dot_nt(dz_ref[:, j * kc:(j + 1) * kc], w_ref[j])
        xv = x_ref[...]
        r = lax.rsqrt(jnp.mean(xv * xv, axis=-1, keepdims=True) + EPS)
        xh = xv * r
        dg_ref[...] += jnp.sum(dh * xh, axis=0, keepdims=True)
        dxh = dh * g_ref[...]
        out = r_ref[...] + r * (dxh - xh * jnp.mean(dxh * xh, axis=-1, keepdims=True))
        o_ref[...] = out
        ob_ref[...] = out.astype(BF16)

    row = lambda i: (i, 0)
    kdim = dz.shape[1]
    return pl.pallas_call(
        body, name=name, grid=(s // ts,),
        in_specs=[pl.BlockSpec((ts, kdim), row), ANY, pl.BlockSpec((ts, D), row),
                  pl.BlockSpec((1, D), lambda i: (0, 0)), pl.BlockSpec((ts, D), row)],
        out_specs=[pl.BlockSpec((ts, D), row), pl.BlockSpec((ts, D), row), pl.BlockSpec((1, D), lambda i: (0, 0))],
        out_shape=[jax.ShapeDtypeStruct((s, D), F32), jax.ShapeDtypeStruct((s, D), BF16),
                   jax.ShapeDtypeStruct((1, D), F32)],
        scratch_shapes=[pltpu.VMEM(w.shape, BF16), pltpu.SemaphoreType.DMA],
        compiler_params=_cp("arbitrary"),
    )(dz, w, x, g, resid)


def _matmul_tn(a, b, name, tn, tk, shard_major=False, tm=None):
    s, m = a.shape
    n = b.shape[1]
    tm = m if tm is None else tm
    ni, nj, nk = m // tm, n // tn, s // tk

    def body(a_ref, b_ref, o_ref):
        if nk == 1:
            o_ref[...] = _dot_tn(a_ref[...], b_ref[...])
            return

        @pl.when(pl.program_id(2) == 0)
        def _():
            o_ref[...] = jnp.zeros_like(o_ref)

        o_ref[...] += _dot_tn(a_ref[...], b_ref[...])

    if shard_major:
        out_spec = pl.BlockSpec((None, tm, tn), lambda i, j, k: (j, i, 0))
        out_shape = jax.ShapeDtypeStruct((nj, m, tn), F32)
    else:
        out_spec = pl.BlockSpec((tm, tn), lambda i, j, k: (i, j))
        out_shape = jax.ShapeDtypeStruct((m, n), F32)
    return pl.pallas_call(
        body, name=name, grid=(ni, nj, nk),
        in_specs=[pl.BlockSpec((tk, tm), lambda i, j, k: (k, i)), pl.BlockSpec((tk, tn), lambda i, j, k: (k, j))],
        out_specs=out_spec, out_shape=out_shape,
        compiler_params=_cp("arbitrary", "arbitrary", "arbitrary"),
    )(a, b)


def _pool_fwd(zr, wgrp, scale):
    s = zr.shape[0]

    def body(u_ref, w_ref, sc_ref, p_ref, pp_ref):
        row = _rows((s, 128))
        for gi, win in enumerate(POOL_WINDOWS):
            cs = slice(gi * 128, (gi + 1) * 128)
            u = u_ref[:, cs].astype(F32)
            acc, k = u, 1
            while k < win:
                acc = acc + jnp.where(row >= k, pltpu.roll(acc, k, 0), 0.0)
                k *= 2
            cnt = jnp.minimum(row + 1, win).astype(F32)
            p = (acc / cnt - u).astype(BF16)
            p_ref[:, cs] = p
            pp_ref[:, cs] = (_dot(p, w_ref[gi].astype(BF16)) * sc_ref[:, cs]).astype(BF16)

    return pl.pallas_call(
        body, name="pool_fwd", grid=(1,),
        in_specs=[pl.BlockSpec((s, POOL_W), lambda i: (0, OFF_POOL // POOL_W)),
                  pl.BlockSpec((4, 128, 128), lambda i: (0, 0, 0)), pl.BlockSpec((1, POOL_W), lambda i: (0, 0))],
        out_specs=[pl.BlockSpec((s, POOL_W), lambda i: (0, 0))] * 2,
        out_shape=[jax.ShapeDtypeStruct((s, POOL_W), BF16)] * 2,
        compiler_params=_cp("arbitrary"),
    )(zr, wgrp, scale)


def _pool_bwd(p, dpp, wgrp, scale):
    s = p.shape[0]

    def body(p_ref, dpp_ref, w_ref, sc_ref, dz_ref, dw_ref, dsc_ref):
        row = _rows((s, 128))
        for gi, win in enumerate(POOL_WINDOWS):
            cs = slice(gi * 128, (gi + 1) * 128)
            pv = p_ref[:, cs]
            wb = w_ref[gi].astype(BF16)
            dpp_v = dpp_ref[:, cs].astype(F32)
            dsc_ref[:, cs] = jnp.sum(dpp_v * _dot(pv, wb), axis=0, keepdims=True)
            dpm = (dpp_v * sc_ref[:, cs]).astype(BF16)
            dw_ref[gi] = _dot_tn(pv, dpm)
            dp = _dot_nt(dpm, wb)
            cnt = jnp.minimum(row + 1, win).astype(F32)
            acc, k = dp / cnt, 1
            while k < win:
                acc = acc + jnp.where(row < s - k, pltpu.roll(acc, s - k, 0), 0.0)
                k *= 2
            dz_ref[:, cs] = (acc - dp).astype(BF16)

    full = lambda i: (0, 0)
    return pl.pallas_call(
        body, name="pool_bwd", grid=(1,),
        in_specs=[pl.BlockSpec((s, POOL_W), full), pl.BlockSpec((s, POOL_W), full),
                  pl.BlockSpec((4, 128, 128), lambda i: (0, 0, 0)), pl.BlockSpec((1, POOL_W), full)],
        out_specs=[pl.BlockSpec((s, POOL_W), full), pl.BlockSpec((4, 128, 128), lambda i: (0, 0, 0)),
                   pl.BlockSpec((1, POOL_W), full)],
        out_shape=[jax.ShapeDtypeStruct((s, POOL_W), BF16), jax.ShapeDtypeStruct((4, 128, 128), F32),
                   jax.ShapeDtypeStruct((1, POOL_W), F32)],
        compiler_params=_cp("arbitrary"),
    )(p, dpp, wgrp, scale)


def _gla_decay(zgk_ref, wgk_ref, bgk_ref, rb):
    g = _dot(zgk_ref[...], wgk_ref[...].astype(BF16)) + bgk_ref[...]
    la = (jnp.minimum(g, 0.0) - jnp.log(1.0 + jnp.exp(-jnp.abs(g)))) * (1.0 / 16.0)
    rowm = _rows(la.shape) & (CHUNK - 1)
    bc, k = la, 1
    while k < CHUNK:
        bc = bc + jnp.where(rowm >= k, pltpu.roll(bc, k, 0), 0.0)
        k *= 2
    return g, jnp.exp(bc), jnp.exp(-bc)


GLA_HB = 4


def _gla_specs(rb, rmap):
    wk, wv = GLA_HB * HK, GLA_HB * HV
    return [pl.BlockSpec((rb, wk), lambda h, r: (rmap(h, r), OFF_Q // wk + h)),
            pl.BlockSpec((rb, wk), lambda h, r: (rmap(h, r), OFF_K // wk + h)),
            pl.BlockSpec((rb, wv), lambda h, r: (rmap(h, r), OFF_V // wv + h)),
            pl.BlockSpec((rb, 128), lambda h, r: (rmap(h, r), OFF_GK // 128))]


def _gla_fwd(zr, wgk, bgk, ghead, rb):
    s = zr.shape[0]
    nc = rb // CHUNK
    wk, wv = GLA_HB * HK, GLA_HB * HV

    def body(q_ref, k_ref, v_ref, zgk_ref, zog_ref, wgk_ref, bgk_ref, gh_ref, o_ref, og_ref, sp_ref, st_ref):
        @pl.when(pl.program_id(1) == 0)
        def _():
            st_ref[...] = jnp.zeros_like(st_ref)

        _, e_pos, e_neg = _gla_decay(zgk_ref, wgk_ref, bgk_ref, rb)
        lower = _rows((CHUNK, CHUNK)) >= lax.broadcasted_iota(jnp.int32, (CHUNK, CHUNK), 1)
        for c in range(nc):
            sl = slice(c * CHUNK, (c + 1) * CHUNK)
            for hh in range(GLA_HB):
                ck, cv = slice(hh * HK, (hh + 1) * HK), slice(hh * HV, (hh + 1) * HV)
                q = q_ref[sl, ck].astype(F32) * QSCALE
                k = k_ref[sl, ck].astype(F32)
                v = v_ref[sl, cv]
                ec, fc = e_pos[sl, ck], e_neg[sl, ck]
                qfw = (q * ec).astype(BF16)
                kfw_f = k * fc
                s_fw = _dot_nt(qfw, kfw_f.astype(BF16))
                s_bw = _dot_nt((q * fc).astype(BF16), (k * ec).astype(BF16))
                pm = jnp.where(lower, s_fw, s_bw).astype(BF16)
                st = st_ref[hh]
                stb = st.astype(BF16)
                sp_ref[c, hh] = stb
                o = _dot(pm, v) + _dot_nt(qfw, stb)
                e_last = _pick_row(ec, CHUNK - 1)
                kdec = (kfw_f * e_last).astype(BF16)
                st_ref[hh] = st * e_last + _dot_tn(v, kdec)
                r = lax.rsqrt(jnp.mean(o * o, axis=-1, keepdims=True) + EPS)
                zo = zog_ref[sl, cv].astype(F32)
                o_ref[sl, cv] = o.astype(BF16)
                og_ref[sl, cv] = (o * r * gh_ref[...] * zo * _sigmoid(zo)).astype(BF16)

    rmap = lambda h, r: r
    return pl.pallas_call(
        body, name="gla_fwd", grid=(HEADS // GLA_HB, s // rb),
        in_specs=_gla_specs(rb, rmap) + [
            pl.BlockSpec((rb, wv), lambda h, r: (r, OFF_OG // wv + h)),
            pl.BlockSpec((128, wk), lambda h, r: (0, h)), pl.BlockSpec((1, wk), lambda h, r: (0, h)),
            pl.BlockSpec((1, HV), lambda h, r: (0, 0))],
        out_specs=[pl.BlockSpec((rb, wv), lambda h, r: (r, h)), pl.BlockSpec((rb, wv), lambda h, r: (r, h)),
                   pl.BlockSpec((nc, GLA_HB, HV, HK), lambda h, r: (r, h, 0, 0))],
        out_shape=[jax.ShapeDtypeStruct((s, D), BF16), jax.ShapeDtypeStruct((s, D), BF16),
                   jax.ShapeDtypeStruct((s // CHUNK, HEADS, HV, HK), BF16)],
        scratch_shapes=[pltpu.VMEM((GLA_HB, HV, HK), F32)],
        compiler_params=_cp("arbitrary", "arbitrary"),
    )(zr, zr, zr, zr, zr, wgk, bgk, ghead)


def _gla_bwd(zr, do, sp, wgk, bgk, rb):
    s = zr.shape[0]
    nc = rb // CHUNK
    nr = s // rb
    wk, wv = GLA_HB * HK, GLA_HB * HV

    def body(q_ref, k_ref, v_ref, zgk_ref, do_ref, sp_ref, wgk_ref, bgk_ref, dq_ref, dk_ref, dv_ref, dg_ref,
             gt_ref, dbc_ref):
        @pl.when(pl.program_id(1) == 0)
        def _():
            gt_ref[...] = jnp.zeros_like(gt_ref)

        g, e_pos, e_neg = _gla_decay(zgk_ref, wgk_ref, bgk_ref, rb)
        lower = _rows((CHUNK, CHUNK)) >= lax.broadcasted_iota(jnp.int32, (CHUNK, CHUNK), 1)
        is_last = _rows((CHUNK, HK)) == CHUNK - 1
        for c in reversed(range(nc)):
            sl = slice(c * CHUNK, (c + 1) * CHUNK)
            for hh in range(GLA_HB):
                ck, cv = slice(hh * HK, (hh + 1) * HK), slice(hh * HV, (hh + 1) * HV)
                q = q_ref[sl, ck].astype(F32) * QSCALE
                k = k_ref[sl, ck].astype(F32)
                v = v_ref[sl, cv]
                dov = do_ref[sl, cv]
                ec, fc = e_pos[sl, ck], e_neg[sl, ck]
                qfw_f, kfw_f, qbw_f, kbw_f = q * ec, k * fc, q * fc, k * ec
                qfw, kfw, qbw, kbw = qfw_f.astype(BF16), kfw_f.astype(BF16), qbw_f.astype(BF16), kbw_f.astype(BF16)
                pm = jnp.where(lower, _dot_nt(qfw, kfw), _dot_nt(qbw, kbw)).astype(BF16)
                e_last = _pick_row(ec, CHUNK - 1)
                kdec = (kfw_f * e_last).astype(BF16)
                gt = gt_ref[hh]
                gtb = gt.astype(BF16)
                spv = sp_ref[c, hh]
                dp = _dot_nt(dov, v)
                dv_ref[sl, cv] = (_dot_tn(pm, dov) + _dot_nt(kdec, gtb)).astype(BF16)
                ds_fw = jnp.where(lower, dp, 0.0).astype(BF16)
                ds_bw = jnp.where(lower, 0.0, dp).astype(BF16)
                dqfw = _dot(ds_fw, kfw) + _dot(dov, spv)
                dkfw = _dot_tn(ds_fw, qfw)
                dqbw = _dot(ds_bw, kbw)
                dkbw = _dot_tn(ds_bw, qbw)
                dkdec = _dot(v, gtb)
                de_last = (jnp.sum(gt * spv.astype(F32), axis=0, keepdims=True)
                           + jnp.sum(dkdec * kfw_f, axis=0, keepdims=True))
                dkfw = dkfw + dkdec * e_last
                dq_ref[sl, ck] = ((dqfw * ec + dqbw * fc) * QSCALE).astype(BF16)
                dk_ref[sl, ck] = (dkfw * fc + dkbw * ec).astype(BF16)
                dbc = dqfw * qfw_f - dqbw * qbw_f + dkbw * kbw_f - dkfw * kfw_f
                dbc_ref[sl, ck] = dbc + jnp.where(is_last, de_last * e_last, 0.0)
                gt_ref[hh] = _dot_tn(dov, qfw) + gt * e_last
        rowm = _rows((rb, wk)) & (CHUNK - 1)
        dla, kk = dbc_ref[...], 1
        while kk < CHUNK:
            dla = dla + jnp.where(rowm < CHUNK - kk, pltpu.roll(dla, rb - kk, 0), 0.0)
            kk *= 2
        dg_ref[...] = dla * (1.0 / 16.0) * _sigmoid(-g)

    rmap = lambda h, r: nr - 1 - r
    rev = lambda h, r: (nr - 1 - r, h)
    return pl.pallas_call(
        body, name="gla_bwd", grid=(HEADS // GLA_HB, nr),
        in_specs=_gla_specs(rb, rmap) + [
            pl.BlockSpec((rb, wv), rev),
            pl.BlockSpec((nc, GLA_HB, HV, HK), lambda h, r: (nr - 1 - r, h, 0, 0)),
            pl.BlockSpec((128, wk), lambda h, r: (0, h)), pl.BlockSpec((1, wk), lambda h, r: (0, h))],
        out_specs=[pl.BlockSpec((rb, wk), rev), pl.BlockSpec((rb, wk), rev), pl.BlockSpec((rb, wv), rev),
                   pl.BlockSpec((rb, wk), rev)],
        out_shape=[jax.ShapeDtypeStruct((s, HEADS * HK), BF16), jax.ShapeDtypeStruct((s, HEADS * HK), BF16),
                   jax.ShapeDtypeStruct((s, D), BF16), jax.ShapeDtypeStruct((s, HEADS * HK), F32)],
        scratch_shapes=[pltpu.VMEM((GLA_HB, HV, HK), F32), pltpu.VMEM((rb, wk), F32)],
        compiler_params=_cp("arbitrary", "arbitrary"),
    )(zr, zr, zr, zr, do, sp, wgk, bgk)


def _gk_bwd(dgpre, zr, wgk, ts):
    s = zr.shape[0]

    def body(dg_ref, zgk_ref, w_ref, dz_ref, dw_ref, db_ref):
        @pl.when(pl.program_id(0) == 0)
        def _():
            dw_ref[...] = jnp.zeros_like(dw_ref)
            db_ref[...] = jnp.zeros_like(db_ref)

        dg = dg_ref[...]
        dgb = dg.astype(BF16)
        dz_ref[...] = _dot_nt(dgb, w_ref[...].astype(BF16)).astype(BF16)
        dw_ref[...] += _dot_tn(zgk_ref[...], dgb)
        db_ref[...] += jnp.sum(dg, axis=0, keepdims=True)

    return pl.pallas_call(
        body, name="gk_bwd", grid=(s // ts,),
        in_specs=[pl.BlockSpec((ts, 512), lambda i: (i, 0)), pl.BlockSpec((ts, 128), lambda i: (i, OFF_GK // 128)),
                  pl.BlockSpec((128, 512), lambda i: (0, 0))],
        out_specs=[pl.BlockSpec((ts, 128), lambda i: (i, 0)), pl.BlockSpec((128, 512), lambda i: (0, 0)),
                   pl.BlockSpec((1, 512), lambda i: (0, 0))],
        out_shape=[jax.ShapeDtypeStruct((s, 128), BF16), jax.ShapeDtypeStruct((128, 512), F32),
                   jax.ShapeDtypeStruct((1, 512), F32)],
        compiler_params=_cp("arbitrary"),
    )(dgpre, zr, wgk)


def _merge_fwd(x, zr, pp, og, bgate, wpp, wgla, wout, ts):
    s = x.shape[0]

    def body(x_ref, z0_ref, z1_ref, pp_ref, og_ref, bg_ref, wpp_ref, wgla_ref, wout_ref,
             x1_ref, mix_ref, yp_ref, yg_ref):
        ppv = pp_ref[...]
        yp = jnp.concatenate([_dot(ppv, wpp_ref[j]) for j in range(4)], axis=1)
        yg = _dot(og_ref[...], wgla_ref[...])
        g0 = _sigmoid(z0_ref[...].astype(F32) + bg_ref[:, :D])
        g1 = _sigmoid(z1_ref[...].astype(F32) + bg_ref[:, D:])
        mixed = (g0 * yp + g1 * yg).astype(BF16)
        x1_ref[...] = x_ref[...] + _dot(mixed, wout_ref[...])
        mix_ref[...] = mixed
        yp_ref[...] = yp.astype(BF16)
        yg_ref[...] = yg.astype(BF16)

    row = lambda i: (i, 0)
    const2 = lambda i: (0, 0)
    return pl.pallas_call(
        body, name="merge_fwd", grid=(s // ts,),
        in_specs=[pl.BlockSpec((ts, D), row), pl.BlockSpec((ts, D), lambda i: (i, 0)), pl.BlockSpec((ts, D), lambda i: (i, 1)),
                  pl.BlockSpec((ts, POOL_W), row), pl.BlockSpec((ts, D), row), pl.BlockSpec((1, 2 * D), const2),
                  pl.BlockSpec((4, POOL_W, 256), lambda i: (0, 0, 0)), pl.BlockSpec((D, D), const2),
                  pl.BlockSpec((D, D), const2)],
        out_specs=[pl.BlockSpec((ts, D), row)] * 4,
        out_shape=[jax.ShapeDtypeStruct((s, D), F32)] + [jax.ShapeDtypeStruct((s, D), BF16)] * 3,
        compiler_params=_cp("arbitrary"),
    )(x, zr, zr, pp, og, bgate, wpp, wgla, wout)


def _merge_bwd(dx1b, zr, yp, yg, o, bgate, ghead, wpp, wgla, wout, ts):
    s = dx1b.shape[0]

    def body(dx_ref, z0_ref, z1_ref, zog_ref, yp_ref, yg_ref, o_ref, bg_ref, gh_ref, wpp_ref, wgla_ref, wout_ref,
             dzg_ref, dyp_ref, dyg_ref, dpp_ref, do_ref, dzog_ref, dbg_ref, dgh_ref):
        @pl.when(pl.program_id(0) == 0)
        def _():
            dbg_ref[...] = jnp.zeros_like(dbg_ref)
            dgh_ref[...] = jnp.zeros_like(dgh_ref)

        dmix = _dot_nt(dx_ref[...], wout_ref[...])
        g0 = _sigmoid(z0_ref[...].astype(F32) + bg_ref[:, :D])
        g1 = _sigmoid(z1_ref[...].astype(F32) + bg_ref[:, D:])
        dypb = (dmix * g0).astype(BF16)
        dygb = (dmix * g1).astype(BF16)
        dz0 = dmix * yp_ref[...].astype(F32) * g0 * (1.0 - g0)
        dz1 = dmix * yg_ref[...].astype(F32) * g1 * (1.0 - g1)
        dzg_ref[:, :D] = dz0.astype(BF16)
        dzg_ref[:, D:] = dz1.astype(BF16)
        dbg_ref[:, :D] += jnp.sum(dz0, axis=0, keepdims=True)
        dbg_ref[:, D:] += jnp.sum(dz1, axis=0, keepdims=True)
        dyp_ref[...] = dypb
        dyg_ref[...] = dygb
        dpp = _dot_nt(dypb[:, 0:256], wpp_ref[0])
        for j in range(1, 4):
            dpp = dpp + _dot_nt(dypb[:, j * 256:(j + 1) * 256], wpp_ref[j])
        dpp_ref[...] = dpp.astype(BF16)
        dog = _dot_nt(dygb, wgla_ref[...])
        gh = gh_ref[...]
        dgh = jnp.zeros((1, HV), F32)
        for h in range(HEADS):
            cs = slice(h * HV, (h + 1) * HV)
            ov = o_ref[:, cs].astype(F32)
            r = lax.rsqrt(jnp.mean(ov * ov, axis=-1, keepdims=True) + EPS)
            oh = ov * r
            zo = zog_ref[:, cs].astype(F32)
            sg = _sigmoid(zo)
            dog_h = dog[:, cs]
            don = dog_h * zo * sg
            dzog_ref[:, cs] = (dog_h * oh * gh * sg * (1.0 + zo * (1.0 - sg))).astype(BF16)
            dgh = dgh + jnp.sum(don * oh, axis=0, keepdims=True)
            doh = don * gh
            do_ref[:, cs] = (r * (doh - oh * jnp.mean(doh * oh, axis=-1, keepdims=True))).astype(BF16)
        dgh_ref[...] += dgh

    row = lambda i: (i, 0)
    const2 = lambda i: (0, 0)
    return pl.pallas_call(
        body, name="merge_bwd", grid=(s // ts,),
        in_specs=[pl.BlockSpec((ts, D), row), pl.BlockSpec((ts, D), lambda i: (i, 0)), pl.BlockSpec((ts, D), lambda i: (i, 1)),
                  pl.BlockSpec((ts, D), lambda i: (i, OFF_OG // D)), pl.BlockSpec((ts, D), row), pl.BlockSpec((ts, D), row),
                  pl.BlockSpec((ts, D), row), pl.BlockSpec((1, 2 * D), const2), pl.BlockSpec((1, HV), const2),
                  pl.BlockSpec((4, POOL_W, 256), lambda i: (0, 0, 0)), pl.BlockSpec((D, D), const2),
                  pl.BlockSpec((D, D), const2)],
        out_specs=[pl.BlockSpec((ts, 2 * D), row), pl.BlockSpec((ts, D), row), pl.BlockSpec((ts, D), row),
                   pl.BlockSpec((ts, POOL_W), row), pl.BlockSpec((ts, D), row), pl.BlockSpec((ts, D), row),
                   pl.BlockSpec((1, 2 * D), const2), pl.BlockSpec((1, HV), const2)],
        out_shape=[jax.ShapeDtypeStruct((s, 2 * D), BF16), jax.ShapeDtypeStruct((s, D), BF16),
                   jax.ShapeDtypeStruct((s, D), BF16), jax.ShapeDtypeStruct((s, POOL_W), BF16),
                   jax.ShapeDtypeStruct((s, D), BF16), jax.ShapeDtypeStruct((s, D), BF16),
                   jax.ShapeDtypeStruct((1, 2 * D), F32), jax.ShapeDtypeStruct((1, HV), F32)],
        compiler_params=_cp("arbitrary"),
    )(dx1b, zr, zr, zr, yp, yg, o, bgate, ghead, wpp, wgla, wout)


HALO = 16
CCH = 1408


def _conv_taps(u_ref, halo_ref, cs, first, ts):
    u = u_ref[:, cs].astype(F32)
    hal = halo_ref[:, cs].astype(F32)
    h1 = jnp.where(first, 0.0, _pick_row(hal, HALO - 1))
    h2 = jnp.where(first, 0.0, _pick_row(hal, HALO - 2))
    row = _rows(u.shape)
    r1 = jnp.where(row == 0, h1, pltpu.roll(u, 1, 0))
    r2 = jnp.where(row == 0, h2, jnp.where(row == 1, h1, pltpu.roll(u, 2, 0)))
    return u, r1, r2


def _ffn_down_loss(u, x1, tgt, wconv, bconv, wdown, gfin, ts):
    s = x1.shape[0]

    def body(u_ref, halo_ref, x1_ref, t_ref, wc_ref, bc_ref, wd_ref, gf_ref, a_ref, c_ref, dx_ref, dxb_ref, ls_ref,
             dgf_ref):
        i = pl.program_id(0)

        @pl.when(i == 0)
        def _():
            ls_ref[...] = jnp.zeros_like(ls_ref)
            dgf_ref[...] = jnp.zeros_like(dgf_ref)

        first = i == 0
        acc = x1_ref[...]
        for hf in range(2):
            cg = slice(hf * CCH, (hf + 1) * CCH)
            cv = slice(D_FF + hf * CCH, D_FF + (hf + 1) * CCH)
            vals = []
            for cs in (cg, cv):
                u0, u1, u2 = _conv_taps(u_ref, halo_ref, cs, first, ts)
                vals.append(bc_ref[:, cs] + wc_ref[0:1, cs] * u2 + wc_ref[1:2, cs] * u1 + wc_ref[2:3, cs] * u0)
                c_ref[:, cs] = vals[-1].astype(BF16)
            a = (vals[0] * _sigmoid(vals[0]) * vals[1]).astype(BF16)
            a_ref[:, cg] = a
            acc = acc + _dot(a, wd_ref[cg, :])
        r = lax.rsqrt(jnp.mean(acc * acc, axis=-1, keepdims=True) + EPS)
        xh = acc * r
        gf = gf_ref[...]
        err = xh * gf - t_ref[...]
        ls_ref[...] += (0.5 / D) * jnp.sum(jnp.sum(err * err, axis=-1, keepdims=True), axis=0, keepdims=True)
        dy = err * (1.0 / D)
        dgf_ref[...] += jnp.sum(dy * xh, axis=0, keepdims=True)
        dxh = dy * gf
        dx = r * (dxh - xh * jnp.mean(dxh * xh, axis=-1, keepdims=True))
        dx_ref[...] = dx
        dxb_ref[...] = dx.astype(BF16)

    row = lambda i: (i, 0)
    const2 = lambda i: (0, 0)
    return pl.pallas_call(
        body, name="ffn_down_loss", grid=(s // ts,),
        in_specs=[pl.BlockSpec((ts, N_UP), row),
                  pl.BlockSpec((HALO, N_UP), lambda i: (jnp.maximum(i * (ts // HALO) - 1, 0), 0)),
                  pl.BlockSpec((ts, D), row), pl.BlockSpec((ts, D), row), pl.BlockSpec((3, N_UP), const2),
                  pl.BlockSpec((1, N_UP), const2), pl.BlockSpec((D_FF, D), const2), pl.BlockSpec((1, D), const2)],
        out_specs=[pl.BlockSpec((ts, D_FF), row), pl.BlockSpec((ts, N_UP), row), pl.BlockSpec((ts, D), row),
                   pl.BlockSpec((ts, D), row), pl.BlockSpec((1, 128), const2), pl.BlockSpec((1, D), const2)],
        out_shape=[jax.ShapeDtypeStruct((s, D_FF), BF16), jax.ShapeDtypeStruct((s, N_UP), BF16),
                   jax.ShapeDtypeStruct((s, D), F32), jax.ShapeDtypeStruct((s, D), BF16),
                   jax.ShapeDtypeStruct((1, 128), F32), jax.ShapeDtypeStruct((1, D), F32)],
        compiler_params=_cp("arbitrary"),
    )(u, u, x1, tgt, wconv, bconv, wdown, gfin)


def _ffn_bwd(dx2b, u, c, wconv, wdown, ts):
    s = dx2b.shape[0]
    nt = s // ts

    def body(dx_ref, u_ref, c_ref, wc_ref, wd_ref, du_ref, db_ref, dw_ref, nxt_ref):
        @pl.when(pl.program_id(0) == 0)
        def _():
            db_ref[...] = jnp.zeros_like(db_ref)
            dw_ref[...] = jnp.zeros_like(dw_ref)
            nxt_ref[...] = jnp.zeros_like(nxt_ref)

        dxv = dx_ref[...]
        row = _rows((ts, CCH))
        for hf in range(2):
            cg = slice(hf * CCH, (hf + 1) * CCH)
            cv = slice(D_FF + hf * CCH, D_FF + (hf + 1) * CCH)
            da = _dot_nt(dxv, wd_ref[cg, :])
            gate = c_ref[:, cg].astype(F32)
            val = c_ref[:, cv].astype(F32)
            sg = _sigmoid(gate)
            dcs = (da * val * sg * (1.0 + gate * (1.0 - sg)), da * gate * sg)
            for cs, dc in zip((cg, cv), dcs):
                n1 = nxt_ref[0:1, cs]
                n2 = nxt_ref[1:2, cs]
                f1 = jnp.where(row == ts - 1, n1, pltpu.roll(dc, ts - 1, 0))
                f2 = jnp.where(row == ts - 1, n2, jnp.where(row == ts - 2, n1, pltpu.roll(dc, ts - 2, 0)))
                uv = u_ref[:, cs].astype(F32)
                db_ref[:, cs] += jnp.sum(dc, axis=0, keepdims=True)
                dw_ref[0:1, cs] += jnp.sum(f2 * uv, axis=0, keepdims=True)
                dw_ref[1:2, cs] += jnp.sum(f1 * uv, axis=0, keepdims=True)
                dw_ref[2:3, cs] += jnp.sum(dc * uv, axis=0, keepdims=True)
                du_ref[:, cs] = (wc_ref[2:3, cs] * dc + wc_ref[1:2, cs] * f1 + wc_ref[0:1, cs] * f2).astype(BF16)
                nxt_ref[:, cs] = dc[0:8, :]

    rev = lambda i: (nt - 1 - i, 0)
    const2 = lambda i: (0, 0)
    return pl.pallas_call(
        body, name="ffn_bwd", grid=(nt,),
        in_specs=[pl.BlockSpec((ts, D), rev), pl.BlockSpec((ts, N_UP), rev), pl.BlockSpec((ts, N_UP), rev),
                  pl.BlockSpec((3, N_UP), const2), pl.BlockSpec((D_FF, D), const2)],
        out_specs=[pl.BlockSpec((ts, N_UP), rev), pl.BlockSpec((1, N_UP), const2), pl.BlockSpec((3, N_UP), const2)],
        out_shape=[jax.ShapeDtypeStruct((s, N_UP), BF16), jax.ShapeDtypeStruct((1, N_UP), F32),
                   jax.ShapeDtypeStruct((3, N_UP), F32)],
        scratch_shapes=[pltpu.VMEM((8, N_UP), F32)],
        compiler_params=_cp("arbitrary"),
    )(dx2b, u, c, wconv, wdown)


ANY = pl.BlockSpec(memory_space=pl.ANY)


def _place():
    x, y, c = lax.axis_index("x"), lax.axis_index("y"), lax.axis_index("c")
    chips = [(1 - x, y), (x, 1 - y), (1 - x, 1 - y)]
    return x, y, c, chips


def _half(shape, c, axis):
    size = shape[axis] // 2
    cut = pl.ds(pl.multiple_of(c * size, 8 if axis == 0 else 128), size)
    return (cut, slice(None)) if axis == 0 else (slice(None), cut)


def _half_shape(shape, axis):
    return (shape[0] // 2, shape[1]) if axis == 0 else (shape[0], shape[1] // 2)


def _remote(src, dst, send_sems, recv_sems, k, to):
    return pltpu.make_async_remote_copy(src_ref=src, dst_ref=dst, send_sem=send_sems.at[k], recv_sem=recv_sems.at[k],
                                        device_id=to, device_id_type=MESH)


def _all_gather_weights(big, axes, small):
    nb, ns = len(big), len(small)
    n = nb + ns
    n_sem = 6 * nb + 3 * ns

    def body(*refs):
        ins, outs = refs[:n], refs[n:2 * n]
        send_sems, recv_sems = refs[2 * n:]
        x, y, c, chips = _place()
        me = 2 * x + y
        sib = (x, y, 1 - c)
        started = []
        for a in range(nb):
            mine = _half(big[a].shape, c, axes[a])
            for k, ch in enumerate(chips):
                cp = _remote(ins[a].at[mine], outs[a].at[(me,) + mine], send_sems, recv_sems, 6 * a + k,
                             (ch[0], ch[1], c))
                cp.start()
                started.append(cp)
        for a in range(ns):
            for k, ch in enumerate(chips):
                cp = _remote(ins[nb + a], outs[nb + a].at[me], send_sems, recv_sems, 6 * nb + 3 * a + k,
                             (ch[0], ch[1], c))
                cp.start()
                started.append(cp)
        for a in range(nb):
            mine = _half(big[a].shape, c, axes[a])
            for k, ch in enumerate(chips):
                landed = outs[a].at[(2 * ch[0] + ch[1],) + mine]
                _remote(landed, landed, send_sems, recv_sems, 6 * a + k, sib).wait_recv()
                cp = _remote(landed, landed, send_sems, recv_sems, 6 * a + 3 + k, sib)
                cp.start()
                started.append(cp)
        for a in range(nb):
            other = _half(big[a].shape, 1 - c, axes[a])
            for k, ch in enumerate(chips):
                landed = outs[a].at[(2 * ch[0] + ch[1],) + other]
                _remote(landed, landed, send_sems, recv_sems, 6 * a + 3 + k, sib).wait_recv()
        for a in range(ns):
            for k, ch in enumerate(chips):
                landed = outs[nb + a].at[2 * ch[0] + ch[1]]
                _remote(landed, landed, send_sems, recv_sems, 6 * nb + 3 * a + k, sib).wait_recv()
        for cp in started:
            cp.wait_send()

    arrs = list(big) + list(small)
    return pl.pallas_call(
        body, name="all_gather_weights",
        in_specs=[ANY] * n, out_specs=[ANY] * n,
        out_shape=[jax.ShapeDtypeStruct((4,) + a.shape, a.dtype) for a in arrs],
        scratch_shapes=[pltpu.SemaphoreType.DMA((n_sem,)), pltpu.SemaphoreType.DMA((n_sem,))],
        compiler_params=pltpu.CompilerParams(has_side_effects=True),
    )(*arrs)


def _sibling_exchange(grads, axes, smalls, name):
    nb = len(grads)
    n = nb + len(smalls)

    def body(*refs):
        ins, outs = refs[:n], refs[n:2 * n]
        send_sems, recv_sems = refs[2 * n:]
        x, y, c, _ = _place()
        sib = (x, y, 1 - c)
        cps = []
        for a in range(nb):
            theirs = _half(grads[a].shape[1:], 1 - c, axes[a])
            cps.append(_remote(ins[a].at[(slice(None),) + theirs], outs[a], send_sems, recv_sems, a, sib))
        for a in range(nb, n):
            cps.append(_remote(ins[a], outs[a], send_sems, recv_sems, a, sib))
        for cp in cps:
            cp.start()
        for cp in cps:
            cp.wait()

    out_shape = [jax.ShapeDtypeStruct((4,) + _half_shape(g.shape[1:], ax), F32) for g, ax in zip(grads, axes)]
    out_shape += [jax.ShapeDtypeStruct(a.shape, F32) for a in smalls]
    return pl.pallas_call(
        body, name=name, in_specs=[ANY] * n, out_specs=[ANY] * n, out_shape=out_shape,
        scratch_shapes=[pltpu.SemaphoreType.DMA((n,)), pltpu.SemaphoreType.DMA((n,))],
        compiler_params=pltpu.CompilerParams(has_side_effects=True),
    )(*grads, *smalls)


def _gather_share(lands, axes, name):
    n = len(lands)

    def body(*refs):
        outs = refs[n:2 * n]
        send_sems, recv_sems = refs[2 * n:]
        x, y, c, chips = _place()
        sib = (x, y, 1 - c)
        cps = []
        for a in range(n):
            mine = _half(lands[a].shape[1:], c, axes[a])
            for k, ch in enumerate(chips):
                landed = outs[a].at[(2 * ch[0] + ch[1],) + mine]
                cps.append(_remote(landed, landed, send_sems, recv_sems, 3 * a + k, sib))
        for cp in cps:
            cp.start()
        for a in range(n):
            other = _half(lands[a].shape[1:], 1 - c, axes[a])
            for k, ch in enumerate(chips):
                landed = outs[a].at[(2 * ch[0] + ch[1],) + other]
                _remote(landed, landed, send_sems, recv_sems, 3 * a + k, sib).wait_recv()
        for cp in cps:
            cp.wait_send()

    return pl.pallas_call(
        body, name=name, in_specs=[ANY] * n, out_specs=[ANY] * n,
        out_shape=[jax.ShapeDtypeStruct(a.shape, a.dtype) for a in lands],
        input_output_aliases={a: a for a in range(n)},
        scratch_shapes=[pltpu.SemaphoreType.DMA((3 * n,)), pltpu.SemaphoreType.DMA((3 * n,))],
        compiler_params=pltpu.CompilerParams(has_side_effects=True),
    )(*lands)


def _sibling_share(halves, name):
    n = len(halves)

    def body(*refs):
        ins, outs = refs[:n], refs[n:2 * n]
        send_sems, recv_sems = refs[2 * n:]
        x, y, c, _ = _place()
        cps = [_remote(ins[a], outs[a], send_sems, recv_sems, a, (x, y, 1 - c)) for a in range(n)]
        for cp in cps:
            cp.start()
        for cp in cps:
            cp.wait()

    return pl.pallas_call(
        body, name=name, in_specs=[ANY] * n, out_specs=[ANY] * n,
        out_shape=[jax.ShapeDtypeStruct(h.shape, F32) for h in halves],
        scratch_shapes=[pltpu.SemaphoreType.DMA((n,)), pltpu.SemaphoreType.DMA((n,))],
        compiler_params=pltpu.CompilerParams(has_side_effects=True),
    )(*halves)


HBM = pl.BlockSpec(memory_space=pltpu.HBM)
SEM = pl.BlockSpec(memory_space=pltpu.SEMAPHORE)
DATAFLOW = pltpu.SideEffectType.DATAFLOW_SIDE_EFFECTING


def _split_start(name, srcs, land_shapes, plan, n_copies, after):
    lands = [lax.empty(shp, dt) for shp, dt in land_shapes]
    bufs = list(srcs) + lands
    nb, ns = len(bufs), len(srcs)

    def body(*refs):
        send_sems, recv_sems, token = refs[nb + 1], refs[nb + 2], refs[-1]
        for k, (src, dst, to) in enumerate(plan(refs[:ns], refs[ns:nb])):
            _remote(src, dst, send_sems, recv_sems, k, to).start()
        token[...] = jnp.zeros_like(token)

    res = pl.pallas_call(
        body, name=name,
        out_shape=(pltpu.SemaphoreType.DMA((n_copies,)), pltpu.SemaphoreType.DMA((n_copies,)),
                   *[pltpu.HBM(b.shape, b.dtype) for b in bufs], jax.ShapeDtypeStruct((8, 128), F32)),
        in_specs=[HBM] * nb + [ANY],
        out_specs=(SEM, SEM, *[HBM] * nb, pl.BlockSpec(memory_space=pltpu.VMEM)),
        input_output_aliases={i: 2 + i for i in range(nb)},
        compiler_params=pltpu.CompilerParams(has_side_effects=DATAFLOW),
    )(*[pltpu.with_memory_space_constraint(b, pltpu.HBM) for b in bufs], after)
    return (res[0], res[1], list(res[2:2 + nb])), res[-1]


def _split_wait(name, handle, n_srcs, plan, after):
    send_sems, recv_sems, bufs = handle
    nb = len(bufs)

    def body(*refs):
        sends, recvs = refs[nb], refs[nb + 1]
        for k, (src, dst, to) in enumerate(plan(refs[:n_srcs], refs[n_srcs:nb])):
            cp = _remote(src, dst, sends, recvs, k, to)
            cp.wait_send()
            cp.wait_recv()

    res = pl.pallas_call(
        body, name=name, out_shape=[pltpu.HBM(b.shape, b.dtype) for b in bufs],
        in_specs=[HBM] * nb + [SEM, SEM, ANY], out_specs=[HBM] * nb,
        input_output_aliases={i: i for i in range(nb)},
        compiler_params=pltpu.CompilerParams(has_side_effects=DATAFLOW),
    )(*bufs, send_sems, recv_sems, after)
    return list(res[:n_srcs]), list(res[n_srcs:])


def _gather_plan(shapes, axes, n_whole=0):
    def plan(srcs, lands):
        x, y, c, chips = _place()
        out = []
        for a, (shape, axis) in enumerate(zip(shapes, axes)):
            mine = _half(shape, c, axis)
            for ch in chips:
                out.append((srcs[a].at[mine], lands[a].at[(2 * x + y,) + mine], (ch[0], ch[1], c)))
        for a in range(len(shapes), len(shapes) + n_whole):
            for ch in chips:
                out.append((srcs[a], lands[a].at[2 * x + y], (ch[0], ch[1], c)))
        return out
    return plan


def _sibling_plan(shapes, axes):
    def plan(srcs, lands):
        x, y, c, _ = _place()
        return [(srcs[a].at[(slice(None),) + _half(shape, 1 - c, axis)], lands[a], (x, y, 1 - c))
                for a, (shape, axis) in enumerate(zip(shapes, axes))]
    return plan


def _reduce_plan(n_big, n_small):
    def plan(srcs, lands):
        x, y, c, chips = _place()
        out = []
        for a in range(n_big):
            for k, ch in enumerate(chips):
                out.append((srcs[a].at[2 * ch[0] + ch[1]], lands[a].at[k], (ch[0], ch[1], c)))
        for a in range(n_big, n_big + n_small):
            for ch in chips:
                out.append((srcs[a], lands[a].at[2 * x + y], (ch[0], ch[1], c)))
        return out
    return plan


def _row_tile(rows, cols, mult):
    best = mult
    for t in range(mult, rows + 1, mult):
        if rows % t == 0 and t * cols * 4 <= (2 << 20):
            best = t
    return best if rows % best == 0 else rows


COL_TILE = 256


def _half_tiling(hshape, axis, mult):
    hr, hc = hshape
    if axis == 0:
        tr = _row_tile(hr, hc, mult)
        return tr, hc, hr // tr
    return hr, COL_TILE, hc // COL_TILE


def _tile_idx(axis, t):
    return (t, 0) if axis == 0 else (0, t)


def _chip_partial(place, g, t, axis, name):
    hshape = t.shape[1:]
    br, bc, nt = _half_tiling(hshape, axis, 16)

    def body(pl_ref, g_ref, t_ref, pf_ref, pb_ref):
        v = g_ref[...] + t_ref[...]
        pb_ref[...] = v.astype(BF16)

        @pl.when(pl.program_id(1) == pl_ref[0])
        def _():
            pf_ref[...] = v

    blk = (None, br, bc)
    return pl.pallas_call(
        body, name=name,
        grid_spec=pltpu.PrefetchScalarGridSpec(
            num_scalar_prefetch=1, grid=(nt, 4),
            in_specs=[pl.BlockSpec(blk, lambda i, j, p: (j,) + _tile_idx(axis, p[1] * nt + i)),
                      pl.BlockSpec(blk, lambda i, j, p: (j,) + _tile_idx(axis, i))],
            out_specs=[pl.BlockSpec((br, bc), lambda i, j, p: _tile_idx(axis, i)),
                       pl.BlockSpec(blk, lambda i, j, p: (j,) + _tile_idx(axis, i))]),
        out_shape=[jax.ShapeDtypeStruct(hshape, F32), jax.ShapeDtypeStruct((4,) + hshape, BF16)],
        compiler_params=_cp("arbitrary", "arbitrary"),
    )(place, g, t)


def _finish_half(pf, rb, axis, name):
    hshape = pf.shape
    br, bc, nt = _half_tiling(hshape, axis, 16)

    def body(pf_ref, rb_ref, o_ref):
        o_ref[...] = ((pf_ref[...] + rb_ref[0].astype(F32)) + rb_ref[1].astype(F32)) + rb_ref[2].astype(F32)

    return pl.pallas_call(
        body, name=name, grid=(nt,),
        in_specs=[pl.BlockSpec((br, bc), lambda i: _tile_idx(axis, i)),
                  pl.BlockSpec((3, br, bc), lambda i: (0,) + _tile_idx(axis, i))],
        out_specs=pl.BlockSpec((br, bc), lambda i: _tile_idx(axis, i)),
        out_shape=jax.ShapeDtypeStruct(hshape, F32),
        compiler_params=_cp("arbitrary"),
    )(pf, rb)


def _add2(a, b, name):
    def body(a_ref, b_ref, o_ref):
        o_ref[...] = a_ref[...] + b_ref[...]

    return pl.pallas_call(body, name=name, out_shape=jax.ShapeDtypeStruct(a.shape, F32))(a, b)


def _adam_math(w, g, m, v):
    m = ADAM_B1 * m + (1.0 - ADAM_B1) * g
    v = ADAM_B2 * v + (1.0 - ADAM_B2) * (g * g)
    m_hat = m / (1.0 - ADAM_B1 ** ADAM_STEP)
    v_hat = v / (1.0 - ADAM_B2 ** ADAM_STEP)
    return -ADAM_LR * (m_hat / (jnp.sqrt(v_hat) + ADAM_EPS) + ADAM_WD * w), m, v


def _adam_halves(place, w, mine, theirs, m, v, axis, name):
    br, bc, nt = _half_tiling(mine.shape, axis, 8)

    def body(pl_ref, w_ref, a_ref, b_ref, m_ref, v_ref, g_ref, d_ref, mo_ref, vo_ref):
        is_mine = pl.program_id(0) // nt == pl_ref[1]
        g = jnp.where(is_mine, a_ref[...], b_ref[...])
        d, mn, vn = _adam_math(w_ref[...], g, m_ref[...], v_ref[...])
        g_ref[...] = g
        d_ref[...] = d
        mo_ref[...] = mn
        vo_ref[...] = vn

    full = pl.BlockSpec((br, bc), lambda i, p: _tile_idx(axis, i))
    half = pl.BlockSpec((br, bc), lambda i, p: _tile_idx(axis, i % nt))
    return pl.pallas_call(
        body, name=name,
        grid_spec=pltpu.PrefetchScalarGridSpec(
            num_scalar_prefetch=1, grid=(2 * nt,), in_specs=[full, half, half, full, full], out_specs=[full] * 4),
        out_shape=[jax.ShapeDtypeStruct(w.shape, F32)] * 4, compiler_params=_cp("arbitrary"),
    )(place, w, mine, theirs, m, v)


def _add_many(xs, ys, name):
    n = len(xs)

    def body(*refs):
        for i in range(n):
            refs[2 * n + i][...] = refs[i][...] + refs[n + i][...]

    return pl.pallas_call(body, name=name, out_shape=[jax.ShapeDtypeStruct(a.shape, F32) for a in xs])(*xs, *ys)


def _adam_small(place, owns, landed, ws, ms, vs, widths):
    n, nw = len(owns), len(ws)

    def body(pl_ref, *refs):
        own_r, land_r = refs[:n], refs[n:2 * n]
        w_r, m_r, v_r = (refs[2 * n + k * nw:2 * n + (k + 1) * nw] for k in range(3))
        outs = refs[2 * n + 3 * nw:]
        g_o, d_o, m_o, v_o = outs[:n], outs[n:n + nw], outs[n + nw:n + 2 * nw], outs[n + 2 * nw:]
        for me in range(4):
            @pl.when(pl_ref[0] == me)
            def _(me=me):
                for i in range(n):
                    p = [own_r[i][...] if k == me else land_r[i][k] for k in range(4)]
                    g = ((p[0] + p[1]) + p[2]) + p[3]
                    if i < nw and widths[i]:
                        g = g[:, me * widths[i]:(me + 1) * widths[i]]
                    g_o[i][...] = g
                    if i < nw:
                        d, mn, vn = _adam_math(w_r[i][...], g, m_r[i][...], v_r[i][...])
                        d_o[i][...] = d
                        m_o[i][...] = mn
                        v_o[i][...] = vn

    g_shapes = [jax.ShapeDtypeStruct(ws[i].shape if i < nw else owns[i].shape, F32) for i in range(n)]
    w_shapes = [jax.ShapeDtypeStruct(w.shape, F32) for w in ws]
    whole = lambda a: pl.BlockSpec(a.shape, lambda i, p, nd=len(a.shape): (0,) * nd)
    ins = list(owns) + list(landed) + list(ws) + list(ms) + list(vs)
    out_shape = g_shapes + w_shapes * 3
    out = pl.pallas_call(
        body, name="adam_small",
        grid_spec=pltpu.PrefetchScalarGridSpec(num_scalar_prefetch=1, grid=(1,), in_specs=[whole(a) for a in ins],
                                               out_specs=[whole(a) for a in out_shape]),
        out_shape=out_shape, compiler_params=_cp("arbitrary"),
    )(place, *ins)
    return out[:n], out[n:n + nw], out[n + nw:n + 2 * nw], out[n + 2 * nw:]


def kernel(x, g_mix, w_in, b_gate, w_gk_up, b_gk, w_pool_grp, pool_scale, g_gla_head, w_pool_proj, w_gla_proj, w_out, g_ffn, w_up, w_conv, b_conv, w_down, g_final, loss_target, m_g_mix, m_w_in, m_b_gate, m_w_gk_up, m_b_gk, m_w_pool_grp, m_pool_scale, m_g_gla_head, m_w_pool_proj, m_w_gla_proj, m_w_out, m_g_ffn, m_w_up, m_w_conv, m_b_conv, m_w_down, m_g_final, v_g_mix, v_w_in, v_b_gate, v_w_gk_up, v_b_gk, v_w_pool_grp, v_pool_scale, v_g_gla_head, v_w_pool_proj, v_w_gla_proj, v_w_out, v_g_ffn, v_w_up, v_w_conv, v_b_conv, v_w_down, v_g_final):
    s = x.shape[1]
    ts = min(s, 512)
    tm = min(s, 256)
    cx, cy, cc = lax.axis_index("x"), lax.axis_index("y"), lax.axis_index("c")
    chip = 2 * cx + cy
    place = jnp.stack([chip, cc]).astype(jnp.int32)

    big_names = ("w_in", "w_pool_proj", "w_gla_proj", "w_out", "w_up", "w_down")
    axes = (1, 0, 0, 0, 0, 0)
    shards = dict(w_in=jnp.transpose(w_in[0]), w_pool_proj=w_pool_proj[0], w_gla_proj=w_gla_proj[0], w_out=w_out[0],
                  w_up=w_up[0], w_down=w_down[0])
    def fill_own(lands, mine):
        return [lax.dynamic_update_slice(g, o_[None], (chip, 0, 0)) for g, o_ in zip(lands, mine)]

    def gather_start(tag, halves, group_axes, whole, after):
        plan = _gather_plan([o_.shape for o_ in halves], group_axes, len(whole))
        srcs = list(halves) + list(whole)
        handle, token = _split_start("gather_" + tag + "_start", srcs, [((4,) + o_.shape, o_.dtype) for o_ in srcs], plan,
                                     3 * len(srcs), after)
        return (handle, plan, len(halves), len(srcs), group_axes), token

    def gather_finish(tag, started, after):
        handle, plan, n_halves, n, group_axes = started
        mine, lands = _split_wait("gather_" + tag + "_wait", handle, n, plan, after)
        lands[:n_halves] = _gather_share(lands[:n_halves], group_axes, "gather_" + tag + "_share")
        return fill_own(lands, mine)

    in_w, tok = gather_start("in", [shards["w_in"].astype(BF16)], axes[:1], [], g_mix)
    zero = tok[0, 0]
    own = [(shards[n] + zero).astype(BF16) for n in big_names[1:]]
    mix_w, tok = gather_start("mix", own[0:3], axes[1:4], [w_gk_up[0] + zero, w_conv[0] + zero], tok)
    ffn_w, tok = gather_start("ffn", own[3:5], axes[4:6], [], tok)
    xs, tgt = x[0], loss_target[0]
    wgrp = w_pool_grp[0]
    h = _rmsnorm(xs, g_mix, "norm_mix", ts)
    w_in_t = gather_finish("in", in_w, h)[0].reshape(N_IN, D)
    w_rt = jnp.concatenate([w_in_t[3600:], w_in_t[1536:3584], w_in_t[0:1536], w_in_t[3584:3600],
                            jnp.zeros((128 - GATE_RANK, D), BF16)], axis=0)
    nsh = N_IN // 4

    zr = _matmul_resident(h, w_rt, "in_proj", 1152, transposed=True)
    p, pp = _pool_fwd(zr, wgrp, pool_scale)
    wpp, wgla, wout, wgk4, wconv4 = gather_finish("mix", mix_w, pp)
    wgla, wout = wgla.reshape(D, D), wout.reshape(D, D)
    wgk_full = jnp.transpose(wgk4, (1, 0, 2)).reshape(GATE_RANK, 512)
    wconv_full = jnp.transpose(wconv4, (1, 0, 2)).reshape(3, N_UP)
    wgk_pad = jnp.concatenate([wgk_full, jnp.zeros((128 - GATE_RANK, 512), F32)], axis=0)
    o, og, sp = _gla_fwd(zr, wgk_pad, b_gk, g_gla_head, ts)
    x1, mixed, yp, yg = _merge_fwd(xs, zr, pp, og, b_gate, wpp, wgla, wout, ts)
    wup, wdown = gather_finish("ffn", ffn_w, x1)
    wdown = wdown.reshape(D_FF, D)
    h2 = _rmsnorm(x1, g_ffn, "norm_ffn", ts)
    u = _matmul_resident(h2, wup, "ffn_up", None)
    a, conv_out, dx2, dx2b, loss_part, dgfin = _ffn_down_loss(u, x1, tgt, wconv_full, b_conv, wdown,
                                                              g_final.reshape(1, D), tm)

    du, dbconv, dwconv = _ffn_bwd(dx2b, u, conv_out, wconv_full, wdown, tm)
    dw_down = _matmul_tn(a, dx2b, "dw_down", D, s, tm=1408)
    dw_up = _matmul_tn(h2, du, "dw_up", 1408, s, shard_major=True)

    def exchange_start(tag, grads, group_axes, after):
        plan = _sibling_plan([g.shape[1:] for g in grads], group_axes)
        lands = [((4,) + _half_shape(g.shape[1:], ax), F32) for g, ax in zip(grads, group_axes)]
        handle, token = _split_start("sibling_" + tag + "_start", grads, lands, plan, len(grads), after)
        return (handle, plan, len(grads)), token

    def partials(tag, names, group_axes, exchange, after):
        handle, plan, n = exchange
        mine, theirs = _split_wait("sibling_" + tag + "_wait", handle, n, plan, after)
        return zip(*[_chip_partial(place, g, t, ax, "chip_partial_" + nm)
                     for nm, ax, g, t in zip(names, group_axes, mine, theirs)])

    ffn_names, ffn_axes = ("w_up", "w_down"), (0, 0)
    ffn_x, token = exchange_start("ffn", [dw_up, dw_down.reshape(4, 704, D)], ffn_axes, du)
    dx1, dx1b, dgffn = _matmul_nt_normbwd(du, wup, x1, g_ffn + token[0:1, 0:1], dx2, "ffn_up_bwd", ts)
    ffn_pf, ffn_pb = partials("ffn", ffn_names, ffn_axes, ffn_x, dx1b)
    ffn_plan = _reduce_plan(2, False)
    ffn_handle, token = _split_start("reduce_ffn_start", ffn_pb, [((3,) + p.shape[1:], BF16) for p in ffn_pb],
                                     ffn_plan, 6, ffn_pf[0])

    dzg, dyp, dyg, dpp, do, dzog, dbgate, dghead = _merge_bwd(dx1b, zr, yp, yg, o, b_gate + token[0:1, 0:1], g_gla_head,
                                                             wpp, wgla, wout, ts)
    dw_out = _matmul_tn(mixed, dx1b, "dw_out", D, s)
    dw_gla = _matmul_tn(og, dyg, "dw_gla", D, s)
    dw_pp = _matmul_tn(pp, dyp, "dw_pp", 256, s, shard_major=True)

    out_names, out_axes = ("w_pool_proj", "w_gla_proj", "w_out"), (0, 0, 0)
    out_x, token = exchange_start("out", [dw_pp, dw_gla.reshape(4, 256, D), dw_out.reshape(4, 256, D)], out_axes, dpp)
    dzp, dwgrp, dscale = _pool_bwd(p, dpp, wgrp, pool_scale + token[0:1, 0:1])
    out_pf, out_pb = partials("out", out_names, out_axes, out_x, dzp)
    out_plan = _reduce_plan(3, False)
    out_handle, token = _split_start("reduce_out_start", out_pb, [((3,) + p_.shape[1:], BF16) for p_ in out_pb],
                                     out_plan, 9, out_pf[0])
    dq, dk, dv, dgpre = _gla_bwd(zr, do, sp, wgk_pad, b_gk + token[0:1, 0:1], ts)
    dzgk, dwgk, dbgk = _gk_bwd(dgpre, zr, wgk_pad, ts)
    dzr = jnp.concatenate([dzg, dv, dzog, dzp, dq, dk, dzgk], axis=1)
    dw_rt = _matmul_tn(dzr, h, "dw_in", D, s, tm=1152)

    def grad_rows(lo, hi):
        out = []
        for seg_lo, seg_hi, at in ((0, 1536, OFF_POOL), (1536, 3584, OFF_V), (3584, 3600, OFF_GK), (3600, N_IN, OFF_GATE)):
            a_, b_ = max(lo, seg_lo), min(hi, seg_hi)
            if a_ < b_:
                out.append(dw_rt[at + a_ - seg_lo:at + b_ - seg_lo])
        return jnp.concatenate(out, axis=0)

    dw_in_t = jnp.stack([grad_rows(j * nsh, (j + 1) * nsh) for j in range(4)])

    in_sib = _sibling_exchange([dw_in_t], (1,), [], "sibling_exchange_in")
    in_pf, in_pb = _chip_partial(place, dw_in_t, in_sib[0], 1, "chip_partial_w_in")
    in_plan = _reduce_plan(1, 0)
    in_handle, token = _split_start("reduce_in_start", [in_pb], [((3,) + in_pb.shape[1:], BF16)], in_plan, 3, in_pf)
    grad_x, _, dgmix = _matmul_nt_normbwd(dzr, w_rt, xs, g_mix + token[0:1, 0:1], dx1, "in_proj_bwd", ts, transposed=True)
    small_names = ("g_mix", "b_gate", "w_gk_up", "b_gk", "w_pool_grp", "pool_scale", "g_gla_head", "g_ffn", "w_conv",
                   "b_conv", "g_final")
    small_mine = [dgmix, dbgate, dwgk[:GATE_RANK], dbgk, dwgrp.reshape(4 * 128, 128), dscale, dghead, dgffn, dwconv, dbconv,
                  dgfin, loss_part]
    small_sib = _sibling_exchange([], (), small_mine, "sibling_exchange_small")
    small_chip = _add_many(small_mine, small_sib, "chip_partial_small")
    small_plan = _reduce_plan(0, len(small_chip))
    small_handle, token = _split_start("reduce_small_start", small_chip, [((4,) + a_.shape, F32) for a_ in small_chip],
                                       small_plan, 3 * len(small_chip), small_mine[0])

    ms = dict(w_in=jnp.transpose(m_w_in[0]), w_pool_proj=m_w_pool_proj[0], w_gla_proj=m_w_gla_proj[0], w_out=m_w_out[0],
              w_up=m_w_up[0], w_down=m_w_down[0])
    vs = dict(w_in=jnp.transpose(v_w_in[0]), w_pool_proj=v_w_pool_proj[0], w_gla_proj=v_w_gla_proj[0], w_out=v_w_out[0],
              w_up=v_w_up[0], w_down=v_w_down[0])
    grad, delta, new_m, new_v = {}, {}, {}, {}

    def finish_and_update(names, group_axes, part_f, landed, tag):
        halves = [_finish_half(pf, rb, ax, "finish_" + n) for n, ax, pf, rb in zip(names, group_axes, part_f, landed)]
        sib_halves = _sibling_share(halves, "sibling_share_" + tag)
        for n, ax, mine, theirs in zip(names, group_axes, halves, sib_halves):
            res = _adam_halves(place, shards[n], mine, theirs, ms[n], vs[n], ax, "adam_" + n)
            if n == "w_in":
                res = [jnp.transpose(r_) for r_ in res]
            grad[n], delta[n], new_m[n], new_v[n] = [r_[None] for r_ in res]

    _, ffn_landed = _split_wait("reduce_ffn_wait", ffn_handle, 2, ffn_plan, token)
    _, out_landed = _split_wait("reduce_out_wait", out_handle, 3, out_plan, ffn_landed[0])
    finish_and_update(ffn_names + out_names, ffn_axes + out_axes, ffn_pf + out_pf, ffn_landed + out_landed, "rest")
    _, in_landed = _split_wait("reduce_in_wait", in_handle, 1, in_plan, delta["w_out"])
    finish_and_update(("w_in",), (1,), (in_pf,), in_landed, "in")
    small_sent, small_landed = _split_wait("reduce_small_wait", small_handle, len(small_chip), small_plan, delta["w_in"])
    given = dict(g_mix=(g_mix, m_g_mix, v_g_mix), b_gate=(b_gate, m_b_gate, v_b_gate), w_gk_up=(w_gk_up, m_w_gk_up, v_w_gk_up),
                 b_gk=(b_gk, m_b_gk, v_b_gk), w_pool_grp=(w_pool_grp, m_w_pool_grp, v_w_pool_grp),
                 pool_scale=(pool_scale, m_pool_scale, v_pool_scale), g_gla_head=(g_gla_head, m_g_gla_head, v_g_gla_head),
                 g_ffn=(g_ffn, m_g_ffn, v_g_ffn), w_conv=(w_conv, m_w_conv, v_w_conv), b_conv=(b_conv, m_b_conv, v_b_conv),
                 g_final=(g_final, m_g_final, v_g_final))
    flat2 = lambda a: a.reshape(-1, a.shape[-1])
    widths = [dict(w_gk_up=128, w_conv=1408).get(n) for n in small_names]
    totals, ds, mo, vo = _adam_small(place, small_sent, small_landed, *[[flat2(given[n][k]) for n in small_names] for k in range(3)],
                                     widths)
    loss = totals[-1][0, 0]
    for i, n in enumerate(small_names):
        shp = given[n][0].shape
        grad[n], delta[n], new_m[n], new_v[n] = [r_.reshape(shp) for r_ in (totals[i], ds[i], mo[i], vo[i])]

    order = ("g_mix", "w_in", "b_gate", "w_gk_up", "b_gk", "w_pool_grp", "pool_scale", "g_gla_head", "w_pool_proj",
             "w_gla_proj", "w_out", "g_ffn", "w_up", "w_conv", "b_conv", "w_down", "g_final")
    return (loss, grad_x[None], *[grad[n] for n in order], *[delta[n] for n in order], *[new_m[n] for n in order],
            *[new_v[n] for n in order])
```

```python
import functools

import jax
import jax.numpy as jnp
from jax import lax
from jax.experimental import pallas as pl
from jax.experimental.pallas import tpu as pltpu

F32 = jnp.float32
BF16 = jnp.bfloat16
MESH = pl.DeviceIdType.MESH

D = 1024
EPS = 1e-6
CHUNK = 64
POOL_W = 512
POOL_WINDOWS = (2, 4, 8, 16)
HEADS = 4
HK = 128
HV = 256
GATE_RANK = 16
D_FF = 2816
N_UP = 2 * D_FF
N_IN = 5648
QSCALE = HK ** -0.5
N_INR = 5760
OFF_GATE, OFF_V, OFF_OG, OFF_POOL, OFF_Q, OFF_K, OFF_GK = 0, 2048, 3072, 4096, 4608, 5120, 5632

ADAM_LR, ADAM_B1, ADAM_B2, ADAM_EPS, ADAM_WD, ADAM_STEP = 0.001, 0.9, 0.999, 1e-08, 0.01, 10

VMEM_LIMIT = 56 * 1024 * 1024


def _cp(*sem):
    return pltpu.CompilerParams(dimension_semantics=sem if sem else None, vmem_limit_bytes=VMEM_LIMIT)


def _dot(a, b):
    return jnp.dot(a, b, preferred_element_type=F32)


def _dot_nt(a, b):
    return lax.dot_general(a, b, (((1,), (1,)), ((), ())), preferred_element_type=F32)


def _dot_tn(a, b):
    return lax.dot_general(a, b, (((0,), (0,)), ((), ())), preferred_element_type=F32)


def _sigmoid(v):
    return 1.0 / (1.0 + jnp.exp(-v))


def _rows(shape):
    return lax.broadcasted_iota(jnp.int32, shape, 0)


def _pick_row(v, r):
    return jnp.sum(jnp.where(_rows(v.shape) == r, v, 0.0), axis=0, keepdims=True)


def _rmsnorm(x, g, name, ts):
    s = x.shape[0]

    def body(x_ref, g_ref, h_ref):
        xv = x_ref[...]
        r = lax.rsqrt(jnp.mean(xv * xv, axis=-1, keepdims=True) + EPS)
        h_ref[...] = (xv * r * g_ref[...]).astype(BF16)

    return pl.pallas_call(
        body, name=name, grid=(s // ts,),
        in_specs=[pl.BlockSpec((ts, D), lambda i: (i, 0)), pl.BlockSpec((1, D), lambda i: (0, 0))],
        out_specs=pl.BlockSpec((ts, D), lambda i: (i, 0)), out_shape=jax.ShapeDtypeStruct((s, D), BF16),
        compiler_params=_cp("arbitrary"),
    )(x, g)


MM_ROWS = 512


def _matmul_resident(h, w, name, tn, transposed=False):
    s = h.shape[0]
    if transposed:
        nj = w.shape[0] // tn
        w_spec = pl.BlockSpec((tn, D), lambda j: (j, 0))
    elif w.ndim == 3:
        nj, tn = w.shape[0], w.shape[2]
        w_spec = pl.BlockSpec((None, D, tn), lambda j: (j, 0, 0))
    else:
        nj = w.shape[1] // tn
        w_spec = pl.BlockSpec((D, tn), lambda j: (0, j))
    mm = _dot_nt if transposed else _dot
    rc = min(s, MM_ROWS)

    def body(h_ref, w_ref, z_ref):
        for r0 in range(0, s, rc):
            z_ref[r0:r0 + rc, :] = mm(h_ref[r0:r0 + rc, :], w_ref[...]).astype(BF16)

    return pl.pallas_call(
        body, name=name, grid=(nj,),
        in_specs=[pl.BlockSpec((s, D), lambda j: (0, 0)), w_spec],
        out_specs=pl.BlockSpec((s, tn), lambda j: (0, j)), out_shape=jax.ShapeDtypeStruct((s, nj * tn), BF16),
        compiler_params=_cp("arbitrary"),
    )(h, w)


def _matmul_nt_normbwd(dz, w, x, g, resid, name, ts, transposed=False):
    s = x.shape[0]

    def body(dz_ref, w_hbm, x_ref, g_ref, r_ref, o_ref, ob_ref, dg_ref, w_ref, sem):
        @pl.when(pl.program_id(0) == 0)
        def _():
            cp = pltpu.make_async_copy(w_hbm, w_ref, sem)
            cp.start()
            cp.wait()
            dg_ref[...] = jnp.zeros_like(dg_ref)

        if transposed:
            dh = _dot(dz_ref[...], w_ref[...])
        else:
            kc = w.shape[2]
            dh = _dot_nt(dz_ref[:, 0:kc], w_ref[0])
            for j in range(1, w.shape[0]):
                dh = dh + _dot_nt(dz_ref[:, j * kc:(j + 1) * kc], w_ref[j])
        xv = x_ref[...]
        r = lax.rsqrt(jnp.mean(xv * xv, axis=-1, keepdims=True) + EPS)
        xh = xv * r
        dg_ref[...] += jnp.sum(dh * xh, axis=0, keepdims=True)
        dxh = dh * g_ref[...]
        out = r_ref[...] + r * (dxh - xh * jnp.mean(dxh * xh, axis=-1, keepdims=True))
        o_ref[...] = out
        ob_ref[...] = out.astype(BF16)

    row = lambda i: (i, 0)
    kdim = dz.shape[1]
    return pl.pallas_call(
        body, name=name, grid=(s // ts,),
        in_specs=[pl.BlockSpec((ts, kdim), row), ANY, pl.BlockSpec((ts, D), row),
                  pl.BlockSpec((1, D), lambda i: (0, 0)), pl.BlockSpec((ts, D), row)],
        out_specs=[pl.BlockSpec((ts, D), row), pl.BlockSpec((ts, D), row), pl.BlockSpec((1, D), lambda i: (0, 0))],
        out_shape=[jax.ShapeDtypeStruct((s, D), F32), jax.ShapeDtypeStruct((s, D), BF16),
                   jax.ShapeDtypeStruct((1, D), F32)],
        scratch_shapes=[pltpu.VMEM(w.shape, BF16), pltpu.SemaphoreType.DMA],
        compiler_params=_cp("arbitrary"),
    )(dz, w, x, g, resid)


def _matmul_tn(a, b, name, tn, tk, shard_major=False, tm=None):
    s, m = a.shape
    n = b.shape[1]
    tm = m if tm is None else tm
    ni, nj, nk = m // tm, n // tn, s // tk

    def body(a_ref, b_ref, o_ref):
        if nk == 1:
            o_ref[...] = _dot_tn(a_ref[...], b_ref[...])
            return

        @pl.when(pl.program_id(2) == 0)
        def _():
            o_ref[...] = jnp.zeros_like(o_ref)

        o_ref[...] += _dot_tn(a_ref[...], b_ref[...])

    if shard_major:
        out_spec = pl.BlockSpec((None, tm, tn), lambda i, j, k: (j, i, 0))
        out_shape = jax.ShapeDtypeStruct((nj, m, tn), F32)
    else:
        out_spec = pl.BlockSpec((tm, tn), lambda i, j, k: (i, j))
        out_shape = jax.ShapeDtypeStruct((m, n), F32)
    return pl.pallas_call(
        body, name=name, grid=(ni, nj, nk),
        in_specs=[pl.BlockSpec((tk, tm), lambda i, j, k: (k, i)), pl.BlockSpec((tk, tn), lambda i, j, k: (k, j))],
        out_specs=out_spec, out_shape=out_shape,
        compiler_params=_cp("arbitrary", "arbitrary", "arbitrary"),
    )(a, b)


def _pool_fwd(zr, wgrp, scale):
    s = zr.shape[0]

    def body(u_ref, w_ref, sc_ref, p_ref, pp_ref):
        row = _rows((s, 128))
        for gi, win in enumerate(POOL_WINDOWS):
            cs = slice(gi * 128, (gi + 1) * 128)
            u = u_ref[:, cs].astype(F32)
            acc, k = u, 1
            while k < win:
                acc = acc + jnp.where(row >= k, pltpu.roll(acc, k, 0), 0.0)
                k *= 2
            cnt = jnp.minimum(row + 1, win).astype(F32)
            p = (acc / cnt - u).astype(BF16)
            p_ref[:, cs] = p
            pp_ref[:, cs] = (_dot(p, w_ref[gi].astype(BF16)) * sc_ref[:, cs]).astype(BF16)

    return pl.pallas_call(
        body, name="pool_fwd", grid=(1,),
        in_specs=[pl.BlockSpec((s, POOL_W), lambda i: (0, OFF_POOL // POOL_W)),
                  pl.BlockSpec((4, 128, 128), lambda i: (0, 0, 0)), pl.BlockSpec((1, POOL_W), lambda i: (0, 0))],
        out_specs=[pl.BlockSpec((s, POOL_W), lambda i: (0, 0))] * 2,
        out_shape=[jax.ShapeDtypeStruct((s, POOL_W), BF16)] * 2,
        compiler_params=_cp("arbitrary"),
    )(zr, wgrp, scale)


def _pool_bwd(p, dpp, wgrp, scale):
    s = p.shape[0]

    def body(p_ref, dpp_ref, w_ref, sc_ref, dz_ref, dw_ref, dsc_ref):
        row = _rows((s, 128))
        for gi, win in enumerate(POOL_WINDOWS):
            cs = slice(gi * 128, (gi + 1) * 128)
            pv = p_ref[:, cs]
            wb = w_ref[gi].astype(BF16)
            dpp_v = dpp_ref[:, cs].astype(F32)
            dsc_ref[:, cs] = jnp.sum(dpp_v * _dot(pv, wb), axis=0, keepdims=True)
            dpm = (dpp_v * sc_ref[:, cs]).astype(BF16)
            dw_ref[gi] = _dot_tn(pv, dpm)
            dp = _dot_nt(dpm, wb)
            cnt = jnp.minimum(row + 1, win).astype(F32)
            acc, k = dp / cnt, 1
            while k < win:
                acc = acc + jnp.where(row < s - k, pltpu.roll(acc, s - k, 0), 0.0)
                k *= 2
            dz_ref[:, cs] = (acc - dp).astype(BF16)

    full = lambda i: (0, 0)
    return pl.pallas_call(
        body, name="pool_bwd", grid=(1,),
        in_specs=[pl.BlockSpec((s, POOL_W), full), pl.BlockSpec((s, POOL_W), full),
                  pl.BlockSpec((4, 128, 128), lambda i: (0, 0, 0)), pl.BlockSpec((1, POOL_W), full)],
        out_specs=[pl.BlockSpec((s, POOL_W), full), pl.BlockSpec((4, 128, 128), lambda i: (0, 0, 0)),
                   pl.BlockSpec((1, POOL_W), full)],
        out_shape=[jax.ShapeDtypeStruct((s, POOL_W), BF16), jax.ShapeDtypeStruct((4, 128, 128), F32),
                   jax.ShapeDtypeStruct((1, POOL_W), F32)],
        compiler_params=_cp("arbitrary"),
    )(p, dpp, wgrp, scale)


def _gla_decay(zgk_ref, wgk_ref, bgk_ref, rb):
    g = _dot(zgk_ref[...], wgk_ref[...].astype(BF16)) + bgk_ref[...]
    la = (jnp.minimum(g, 0.0) - jnp.log(1.0 + jnp.exp(-jnp.abs(g)))) * (1.0 / 16.0)
    rowm = _rows(la.shape) & (CHUNK - 1)
    bc, k = la, 1
    while k < CHUNK:
        bc = bc + jnp.where(rowm >= k, pltpu.roll(bc, k, 0), 0.0)
        k *= 2
    return g, jnp.exp(bc), jnp.exp(-bc)


GLA_HB = 4


def _gla_specs(rb, rmap):
    wk, wv = GLA_HB * HK, GLA_HB * HV
    return [pl.BlockSpec((rb, wk), lambda h, r: (rmap(h, r), OFF_Q // wk + h)),
            pl.BlockSpec((rb, wk), lambda h, r: (rmap(h, r), OFF_K // wk + h)),
            pl.BlockSpec((rb, wv), lambda h, r: (rmap(h, r), OFF_V // wv + h)),
            pl.BlockSpec((rb, 128), lambda h, r: (rmap(h, r), OFF_GK // 128))]


def _gla_fwd(zr, wgk, bgk, ghead, rb):
    s = zr.shape[0]
    nc = rb // CHUNK
    wk, wv = GLA_HB * HK, GLA_HB * HV

    def body(q_ref, k_ref, v_ref, zgk_ref, zog_ref, wgk_ref, bgk_ref, gh_ref, o_ref, og_ref, sp_ref, st_ref):
        @pl.when(pl.program_id(1) == 0)
        def _():
            st_ref[...] = jnp.zeros_like(st_ref)

        _, e_pos, e_neg = _gla_decay(zgk_ref, wgk_ref, bgk_ref, rb)
        lower = _rows((CHUNK, CHUNK)) >= lax.broadcasted_iota(jnp.int32, (CHUNK, CHUNK), 1)
        for c in range(nc):
            sl = slice(c * CHUNK, (c + 1) * CHUNK)
            for hh in range(GLA_HB):
                ck, cv = slice(hh * HK, (hh + 1) * HK), slice(hh * HV, (hh + 1) * HV)
                q = q_ref[sl, ck].astype(F32) * QSCALE
                k = k_ref[sl, ck].astype(F32)
                v = v_ref[sl, cv]
                ec, fc = e_pos[sl, ck], e_neg[sl, ck]
                qfw = (q * ec).astype(BF16)
                kfw_f = k * fc
                s_fw = _dot_nt(qfw, kfw_f.astype(BF16))
                s_bw = _dot_nt((q * fc).astype(BF16), (k * ec).astype(BF16))
                pm = jnp.where(lower, s_fw, s_bw).astype(BF16)
                st = st_ref[hh]
                stb = st.astype(BF16)
                sp_ref[c, hh] = stb
                o = _dot(pm, v) + _dot_nt(qfw, stb)
                e_last = _pick_row(ec, CHUNK - 1)
                kdec = (kfw_f * e_last).astype(BF16)
                st_ref[hh] = st * e_last + _dot_tn(v, kdec)
                r = lax.rsqrt(jnp.mean(o * o, axis=-1, keepdims=True) + EPS)
                zo = zog_ref[sl, cv].astype(F32)
                o_ref[sl, cv] = o.astype(BF16)
                og_ref[sl, cv] = (o * r * gh_ref[...] * zo * _sigmoid(zo)).astype(BF16)

    rmap = lambda h, r: r
    return pl.pallas_call(
        body, name="gla_fwd", grid=(HEADS // GLA_HB, s // rb),
        in_specs=_gla_specs(rb, rmap) + [
            pl.BlockSpec((rb, wv), lambda h, r: (r, OFF_OG // wv + h)),
            pl.BlockSpec((128, wk), lambda h, r: (0, h)), pl.BlockSpec((1, wk), lambda h, r: (0, h)),
            pl.BlockSpec((1, HV), lambda h, r: (0, 0))],
        out_specs=[pl.BlockSpec((rb, wv), lambda h, r: (r, h)), pl.BlockSpec((rb, wv), lambda h, r: (r, h)),
                   pl.BlockSpec((nc, GLA_HB, HV, HK), lambda h, r: (r, h, 0, 0))],
        out_shape=[jax.ShapeDtypeStruct((s, D), BF16), jax.ShapeDtypeStruct((s, D), BF16),
                   jax.ShapeDtypeStruct((s // CHUNK, HEADS, HV, HK), BF16)],
        scratch_shapes=[pltpu.VMEM((GLA_HB, HV, HK), F32)],
        compiler_params=_cp("arbitrary", "arbitrary"),
    )(zr, zr, zr, zr, zr, wgk, bgk, ghead)


def _gla_bwd(zr, do, sp, wgk, bgk, rb):
    s = zr.shape[0]
    nc = rb // CHUNK
    nr = s // rb
    wk, wv = GLA_HB * HK, GLA_HB * HV

    def body(q_ref, k_ref, v_ref, zgk_ref, do_ref, sp_ref, wgk_ref, bgk_ref, dq_ref, dk_ref, dv_ref, dg_ref,
             gt_ref, dbc_ref):
        @pl.when(pl.program_id(1) == 0)
        def _():
            gt_ref[...] = jnp.zeros_like(gt_ref)

        g, e_pos, e_neg = _gla_decay(zgk_ref, wgk_ref, bgk_ref, rb)
        lower = _rows((CHUNK, CHUNK)) >= lax.broadcasted_iota(jnp.int32, (CHUNK, CHUNK), 1)
        is_last = _rows((CHUNK, HK)) == CHUNK - 1
        for c in reversed(range(nc)):
            sl = slice(c * CHUNK, (c + 1) * CHUNK)
            for hh in range(GLA_HB):
                ck, cv = slice(hh * HK, (hh + 1) * HK), slice(hh * HV, (hh + 1) * HV)
                q = q_ref[sl, ck].astype(F32) * QSCALE
                k = k_ref[sl, ck].astype(F32)
                v = v_ref[sl, cv]
                dov = do_ref[sl, cv]
                ec, fc = e_pos[sl, ck], e_neg[sl, ck]
                qfw_f, kfw_f, qbw_f, kbw_f = q * ec, k * fc, q * fc, k * ec
                qfw, kfw, qbw, kbw = qfw_f.astype(BF16), kfw_f.astype(BF16), qbw_f.astype(BF16), kbw_f.astype(BF16)
                pm = jnp.where(lower, _dot_nt(qfw, kfw), _dot_nt(qbw, kbw)).astype(BF16)
                e_last = _pick_row(ec, CHUNK - 1)
                kdec = (kfw_f * e_last).astype(BF16)
                gt = gt_ref[hh]
                gtb = gt.astype(BF16)
                spv = sp_ref[c, hh]
                dp = _dot_nt(dov, v)
                dv_ref[sl, cv] = (_dot_tn(pm, dov) + _dot_nt(kdec, gtb)).astype(BF16)
                ds_fw = jnp.where(lower, dp, 0.0).astype(BF16)
                ds_bw = jnp.where(lower, 0.0, dp).astype(BF16)
                dqfw = _dot(ds_fw, kfw) + _dot(dov, spv)
                dkfw = _dot_tn(ds_fw, qfw)
                dqbw = _dot(ds_bw, kbw)
                dkbw = _dot_tn(ds_bw, qbw)
                dkdec = _dot(v, gtb)
                de_last = (jnp.sum(gt * spv.astype(F32), axis=0, keepdims=True)
                           + jnp.sum(dkdec * kfw_f, axis=0, keepdims=True))
                dkfw = dkfw + dkdec * e_last
                dq_ref[sl, ck] = ((dqfw * ec + dqbw * fc) * QSCALE).astype(BF16)
                dk_ref[sl, ck] = (dkfw * fc + dkbw * ec).astype(BF16)
                dbc = dqfw * qfw_f - dqbw * qbw_f + dkbw * kbw_f - dkfw * kfw_f
                dbc_ref[sl, ck] = dbc + jnp.where(is_last, de_last * e_last, 0.0)
                gt_ref[hh] = _dot_tn(dov, qfw) + gt * e_last
        rowm = _rows((rb, wk)) & (CHUNK - 1)
        dla, kk = dbc_ref[...], 1
        while kk < CHUNK:
            dla = dla + jnp.where(rowm < CHUNK - kk, pltpu.roll(dla, rb - kk, 0), 0.0)
            kk *= 2
        dg_ref[...] = dla * (1.0 / 16.0) * _sigmoid(-g)

    rmap = lambda h, r: nr - 1 - r
    rev = lambda h, r: (nr - 1 - r, h)
    return pl.pallas_call(
        body, name="gla_bwd", grid=(HEADS // GLA_HB, nr),
        in_specs=_gla_specs(rb, rmap) + [
            pl.BlockSpec((rb, wv), rev),
            pl.BlockSpec((nc, GLA_HB, HV, HK), lambda h, r: (nr - 1 - r, h, 0, 0)),
            pl.BlockSpec((128, wk), lambda h, r: (0, h)), pl.BlockSpec((1, wk), lambda h, r: (0, h))],
        out_specs=[pl.BlockSpec((rb, wk), rev), pl.BlockSpec((rb, wk), rev), pl.BlockSpec((rb, wv), rev),
                   pl.BlockSpec((rb, wk), rev)],
        out_shape=[jax.ShapeDtypeStruct((s, HEADS * HK), BF16), jax.ShapeDtypeStruct((s, HEADS * HK), BF16),
                   jax.ShapeDtypeStruct((s, D), BF16), jax.ShapeDtypeStruct((s, HEADS * HK), F32)],
        scratch_shapes=[pltpu.VMEM((GLA_HB, HV, HK), F32), pltpu.VMEM((rb, wk), F32)],
        compiler_params=_cp("arbitrary", "arbitrary"),
    )(zr, zr, zr, zr, do, sp, wgk, bgk)


def _gk_bwd(dgpre, zr, wgk, ts):
    s = zr.shape[0]

    def body(dg_ref, zgk_ref, w_ref, dz_ref, dw_ref, db_ref):
        @pl.when(pl.program_id(0) == 0)
        def _():
            dw_ref[...] = jnp.zeros_like(dw_ref)
            db_ref[...] = jnp.zeros_like(db_ref)

        dg = dg_ref[...]
        dgb = dg.astype(BF16)
        dz_ref[...] = _dot_nt(dgb, w_ref[...].astype(BF16)).astype(BF16)
        dw_ref[...] += _dot_tn(zgk_ref[...], dgb)
        db_ref[...] += jnp.sum(dg, axis=0, keepdims=True)

    return pl.pallas_call(
        body, name="gk_bwd", grid=(s // ts,),
        in_specs=[pl.BlockSpec((ts, 512), lambda i: (i, 0)), pl.BlockSpec((ts, 128), lambda i: (i, OFF_GK // 128)),
                  pl.BlockSpec((128, 512), lambda i: (0, 0))],
        out_specs=[pl.BlockSpec((ts, 128), lambda i: (i, 0)), pl.BlockSpec((128, 512), lambda i: (0, 0)),
                   pl.BlockSpec((1, 512), lambda i: (0, 0))],
        out_shape=[jax.ShapeDtypeStruct((s, 128), BF16), jax.ShapeDtypeStruct((128, 512), F32),
                   jax.ShapeDtypeStruct((1, 512), F32)],
        compiler_params=_cp("arbitrary"),
    )(dgpre, zr, wgk)


def _merge_fwd(x, zr, pp, og, bgate, wpp, wgla, wout, ts):
    s = x.shape[0]

    def body(x_ref, z0_ref, z1_ref, pp_ref, og_ref, bg_ref, wpp_ref, wgla_ref, wout_ref,
             x1_ref, mix_ref, yp_ref, yg_ref):
        ppv = pp_ref[...]
        yp = jnp.concatenate([_dot(ppv, wpp_ref[j]) for j in range(4)], axis=1)
        yg = _dot(og_ref[...], wgla_ref[...])
        g0 = _sigmoid(z0_ref[...].astype(F32) + bg_ref[:, :D])
        g1 = _sigmoid(z1_ref[...].astype(F32) + bg_ref[:, D:])
        mixed = (g0 * yp + g1 * yg).astype(BF16)
        x1_ref[...] = x_ref[...] + _dot(mixed, wout_ref[...])
        mix_ref[...] = mixed
        yp_ref[...] = yp.astype(BF16)
        yg_ref[...] = yg.astype(BF16)

    row = lambda i: (i, 0)
    const2 = lambda i: (0, 0)
    return pl.pallas_call(
        body, name="merge_fwd", grid=(s // ts,),
        in_specs=[pl.BlockSpec((ts, D), row), pl.BlockSpec((ts, D), lambda i: (i, 0)), pl.BlockSpec((ts, D), lambda i: (i, 1)),
                  pl.BlockSpec((ts, POOL_W), row), pl.BlockSpec((ts, D), row), pl.BlockSpec((1, 2 * D), const2),
                  pl.BlockSpec((4, POOL_W, 256), lambda i: (0, 0, 0)), pl.BlockSpec((D, D), const2),
                  pl.BlockSpec((D, D), const2)],
        out_specs=[pl.BlockSpec((ts, D), row)] * 4,
        out_shape=[jax.ShapeDtypeStruct((s, D), F32)] + [jax.ShapeDtypeStruct((s, D), BF16)] * 3,
        compiler_params=_cp("arbitrary"),
    )(x, zr, zr, pp, og, bgate, wpp, wgla, wout)


def _merge_bwd(dx1b, zr, yp, yg, o, bgate, ghead, wpp, wgla, wout, ts):
    s = dx1b.shape[0]

    def body(dx_ref, z0_ref, z1_ref, zog_ref, yp_ref, yg_ref, o_ref, bg_ref, gh_ref, wpp_ref, wgla_ref, wout_ref,
             dzg_ref, dyp_ref, dyg_ref, dpp_ref, do_ref, dzog_ref, dbg_ref, dgh_ref):
        @pl.when(pl.program_id(0) == 0)
        def _():
            dbg_ref[...] = jnp.zeros_like(dbg_ref)
            dgh_ref[...] = jnp.zeros_like(dgh_ref)

        dmix = _dot_nt(dx_ref[...], wout_ref[...])
        g0 = _sigmoid(z0_ref[...].astype(F32) + bg_ref[:, :D])
        g1 = _sigmoid(z1_ref[...].astype(F32) + bg_ref[:, D:])
        dypb = (dmix * g0).astype(BF16)
        dygb = (dmix * g1).astype(BF16)
        dz0 = dmix * yp_ref[...].astype(F32) * g0 * (1.0 - g0)
        dz1 = dmix * yg_ref[...].astype(F32) * g1 * (1.0 - g1)
        dzg_ref[:, :D] = dz0.astype(BF16)
        dzg_ref[:, D:] = dz1.astype(BF16)
        dbg_ref[:, :D] += jnp.sum(dz0, axis=0, keepdims=True)
        dbg_ref[:, D:] += jnp.sum(dz1, axis=0, keepdims=True)
        dyp_ref[...] = dypb
        dyg_ref[...] = dygb
        dpp = _dot_nt(dypb[:, 0:256], wpp_ref[0])
        for j in range(1, 4):
            dpp = dpp + _dot_nt(dypb[:, j * 256:(j + 1) * 256], wpp_ref[j])
        dpp_ref[...] = dpp.astype(BF16)
        dog = _dot_nt(dygb, wgla_ref[...])
        gh = gh_ref[...]
        dgh = jnp.zeros((1, HV), F32)
        for h in range(HEADS):
            cs = slice(h * HV, (h + 1) * HV)
            ov = o_ref[:, cs].astype(F32)
            r = lax.rsqrt(jnp.mean(ov * ov, axis=-1, keepdims=True) + EPS)
            oh = ov * r
            zo = zog_ref[:, cs].astype(F32)
            sg = _sigmoid(zo)
            dog_h = dog[:, cs]
            don = dog_h * zo * sg
            dzog_ref[:, cs] = (dog_h * oh * gh * sg * (1.0 + zo * (1.0 - sg))).astype(BF16)
            dgh = dgh + jnp.sum(don * oh, axis=0, keepdims=True)
            doh = don * gh
            do_ref[:, cs] = (r * (doh - oh * jnp.mean(doh * oh, axis=-1, keepdims=True))).astype(BF16)
        dgh_ref[...] += dgh

    row = lambda i: (i, 0)
    const2 = lambda i: (0, 0)
    return pl.pallas_call(
        body, name="merge_bwd", grid=(s // ts,),
        in_specs=[pl.BlockSpec((ts, D), row), pl.BlockSpec((ts, D), lambda i: (i, 0)), pl.BlockSpec((ts, D), lambda i: (i, 1)),
                  pl.BlockSpec((ts, D), lambda i: (i, OFF_OG // D)), pl.BlockSpec((ts, D), row), pl.BlockSpec((ts, D), row),
                  pl.BlockSpec((ts, D), row), pl.BlockSpec((1, 2 * D), const2), pl.BlockSpec((1, HV), const2),
                  pl.BlockSpec((4, POOL_W, 256), lambda i: (0, 0, 0)), pl.BlockSpec((D, D), const2),
                  pl.BlockSpec((D, D), const2)],
        out_specs=[pl.BlockSpec((ts, 2 * D), row), pl.BlockSpec((ts, D), row), pl.BlockSpec((ts, D), row),
                   pl.BlockSpec((ts, POOL_W), row), pl.BlockSpec((ts, D), row), pl.BlockSpec((ts, D), row),
                   pl.BlockSpec((1, 2 * D), const2), pl.BlockSpec((1, HV), const2)],
        out_shape=[jax.ShapeDtypeStruct((s, 2 * D), BF16), jax.ShapeDtypeStruct((s, D), BF16),
                   jax.ShapeDtypeStruct((s, D), BF16), jax.ShapeDtypeStruct((s, POOL_W), BF16),
                   jax.ShapeDtypeStruct((s, D), BF16), jax.ShapeDtypeStruct((s, D), BF16),
                   jax.ShapeDtypeStruct((1, 2 * D), F32), jax.ShapeDtypeStruct((1, HV), F32)],
        compiler_params=_cp("arbitrary"),
    )(dx1b, zr, zr, zr, yp, yg, o, bgate, ghead, wpp, wgla, wout)


HALO = 16
CCH = 1408


def _conv_taps(u_ref, halo_ref, cs, first, ts):
    u = u_ref[:, cs].astype(F32)
    hal = halo_ref[:, cs].astype(F32)
    h1 = jnp.where(first, 0.0, _pick_row(hal, HALO - 1))
    h2 = jnp.where(first, 0.0, _pick_row(hal, HALO - 2))
    row = _rows(u.shape)
    r1 = jnp.where(row == 0, h1, pltpu.roll(u, 1, 0))
    r2 = jnp.where(row == 0, h2, jnp.where(row == 1, h1, pltpu.roll(u, 2, 0)))
    return u, r1, r2


def _ffn_down_loss(u, x1, tgt, wconv, bconv, wdown, gfin, ts):
    s = x1.shape[0]

    def body(u_ref, halo_ref, x1_ref, t_ref, wc_ref, bc_ref, wd_ref, gf_ref, a_ref, c_ref, dx_ref, dxb_ref, ls_ref,
             dgf_ref):
        i = pl.program_id(0)

        @pl.when(i == 0)
        def _():
            ls_ref[...] = jnp.zeros_like(ls_ref)
            dgf_ref[...] = jnp.zeros_like(dgf_ref)

        first = i == 0
        acc = x1_ref[...]
        for hf in range(2):
            cg = slice(hf * CCH, (hf + 1) * CCH)
            cv = slice(D_FF + hf * CCH, D_FF + (hf + 1) * CCH)
            vals = []
            for cs in (cg, cv):
                u0, u1, u2 = _conv_taps(u_ref, halo_ref, cs, first, ts)
                vals.append(bc_ref[:, cs] + wc_ref[0:1, cs] * u2 + wc_ref[1:2, cs] * u1 + wc_ref[2:3, cs] * u0)
                c_ref[:, cs] = vals[-1].astype(BF16)
            a = (vals[0] * _sigmoid(vals[0]) * vals[1]).astype(BF16)
            a_ref[:, cg] = a
            acc = acc + _dot(a, wd_ref[cg, :])
        r = lax.rsqrt(jnp.mean(acc * acc, axis=-1, keepdims=True) + EPS)
        xh = acc * r
        gf = gf_ref[...]
        err = xh * gf - t_ref[...]
        ls_ref[...] += (0.5 / D) * jnp.sum(jnp.sum(err * err, axis=-1, keepdims=True), axis=0, keepdims=True)
        dy = err * (1.0 / D)
        dgf_ref[...] += jnp.sum(dy * xh, axis=0, keepdims=True)
        dxh = dy * gf
        dx = r * (dxh - xh * jnp.mean(dxh * xh, axis=-1, keepdims=True))
        dx_ref[...] = dx
        dxb_ref[...] = dx.astype(BF16)

    row = lambda i: (i, 0)
    const2 = lambda i: (0, 0)
    return pl.pallas_call(
        body, name="ffn_down_loss", grid=(s // ts,),
        in_specs=[pl.BlockSpec((ts, N_UP), row),
                  pl.BlockSpec((HALO, N_UP), lambda i: (jnp.maximum(i * (ts // HALO) - 1, 0), 0)),
                  pl.BlockSpec((ts, D), row), pl.BlockSpec((ts, D), row), pl.BlockSpec((3, N_UP), const2),
                  pl.BlockSpec((1, N_UP), const2), pl.BlockSpec((D_FF, D), const2), pl.BlockSpec((1, D), const2)],
        out_specs=[pl.BlockSpec((ts, D_FF), row), pl.BlockSpec((ts, N_UP), row), pl.BlockSpec((ts, D), row),
                   pl.BlockSpec((ts, D), row), pl.BlockSpec((1, 128), const2), pl.BlockSpec((1, D), const2)],
        out_shape=[jax.ShapeDtypeStruct((s, D_FF), BF16), jax.ShapeDtypeStruct((s, N_UP), BF16),
                   jax.ShapeDtypeStruct((s, D), F32), jax.ShapeDtypeStruct((s, D), BF16),
                   jax.ShapeDtypeStruct((1, 128), F32), jax.ShapeDtypeStruct((1, D), F32)],
        compiler_params=_cp("arbitrary"),
    )(u, u, x1, tgt, wconv, bconv, wdown, gfin)


def _ffn_bwd(dx2b, u, c, wconv, wdown, ts):
    s = dx2b.shape[0]
    nt = s // ts

    def body(dx_ref, u_ref, c_ref, wc_ref, wd_ref, du_ref, db_ref, dw_ref, nxt_ref):
        @pl.when(pl.program_id(0) == 0)
        def _():
            db_ref[...] = jnp.zeros_like(db_ref)
            dw_ref[...] = jnp.zeros_like(dw_ref)
            nxt_ref[...] = jnp.zeros_like(nxt_ref)

        dxv = dx_ref[...]
        row = _rows((ts, CCH))
        for hf in range(2):
            cg = slice(hf * CCH, (hf + 1) * CCH)
            cv = slice(D_FF + hf * CCH, D_FF + (hf + 1) * CCH)
            da = _dot_nt(dxv, wd_ref[cg, :])
            gate = c_ref[:, cg].astype(F32)
            val = c_ref[:, cv].astype(F32)
            sg = _sigmoid(gate)
            dcs = (da * val * sg * (1.0 + gate * (1.0 - sg)), da * gate * sg)
            for cs, dc in zip((cg, cv), dcs):
                n1 = nxt_ref[0:1, cs]
                n2 = nxt_ref[1:2, cs]
                f1 = jnp.where(row == ts - 1, n1, pltpu.roll(dc, ts - 1, 0))
                f2 = jnp.where(row == ts - 1, n2, jnp.where(row == ts - 2, n1, pltpu.roll(dc, ts - 2, 0)))
                uv = u_ref[:, cs].astype(F32)
                db_ref[:, cs] += jnp.sum(dc, axis=0, keepdims=True)
                dw_ref[0:1, cs] += jnp.sum(f2 * uv, axis=0, keepdims=True)
                dw_ref[1:2, cs] += jnp.sum(f1 * uv, axis=0, keepdims=True)
                dw_ref[2:3, cs] += jnp.sum(dc * uv, axis=0, keepdims=True)
                du_ref[:, cs] = (wc_ref[2:3, cs] * dc + wc_ref[1:2, cs] * f1 + wc_ref[0:1, cs] * f2).astype(BF16)
                nxt_ref[:, cs] = dc[0:8, :]

    rev = lambda i: (nt - 1 - i, 0)
    const2 = lambda i: (0, 0)
    return pl.pallas_call(
        body, name="ffn_bwd", grid=(nt,),
        in_specs=[pl.BlockSpec((ts, D), rev), pl.BlockSpec((ts, N_UP), rev), pl.BlockSpec((ts, N_UP), rev),
                  pl.BlockSpec((3, N_UP), const2), pl.BlockSpec((D_FF, D), const2)],
        out_specs=[pl.BlockSpec((ts, N_UP), rev), pl.BlockSpec((1, N_UP), const2), pl.BlockSpec((3, N_UP), const2)],
        out_shape=[jax.ShapeDtypeStruct((s, N_UP), BF16), jax.ShapeDtypeStruct((1, N_UP), F32),
                   jax.ShapeDtypeStruct((3, N_UP), F32)],
        scratch_shapes=[pltpu.VMEM((8, N_UP), F32)],
        compiler_params=_cp("arbitrary"),
    )(dx2b, u, c, wconv, wdown)


ANY = pl.BlockSpec(memory_space=pl.ANY)


def _place():
    x, y, c = lax.axis_index("x"), lax.axis_index("y"), lax.axis_index("c")
    chips = [(1 - x, y), (x, 1 - y), (1 - x, 1 - y)]
    return x, y, c, chips


def _half(shape, c, axis):
    size = shape[axis] // 2
    cut = pl.ds(pl.multiple_of(c * size, 8 if axis == 0 else 128), size)
    return (cut, slice(None)) if axis == 0 else (slice(None), cut)


def _half_shape(shape, axis):
    return (shape[0] // 2, shape[1]) if axis == 0 else (shape[0], shape[1] // 2)


def _remote(src, dst, send_sems, recv_sems, k, to):
    return pltpu.make_async_remote_copy(src_ref=src, dst_ref=dst, send_sem=send_sems.at[k], recv_sem=recv_sems.at[k],
                                        device_id=to, device_id_type=MESH)


def _all_gather_weights(big, axes, small):
    nb, ns = len(big), len(small)
    n = nb + ns
    n_sem = 6 * nb + 3 * ns

    def body(*refs):
        ins, outs = refs[:n], refs[n:2 * n]
        send_sems, recv_sems = refs[2 * n:]
        x, y, c, chips = _place()
        me = 2 * x + y
        sib = (x, y, 1 - c)
        started = []
        for a in range(nb):
            mine = _half(big[a].shape, c, axes[a])
            for k, ch in enumerate(chips):
                cp = _remote(ins[a].at[mine], outs[a].at[(me,) + mine], send_sems, recv_sems, 6 * a + k,
                             (ch[0], ch[1], c))
                cp.start()
                started.append(cp)
        for a in range(ns):
            for k, ch in enumerate(chips):
                cp = _remote(ins[nb + a], outs[nb + a].at[me], send_sems, recv_sems, 6 * nb + 3 * a + k,
                             (ch[0], ch[1], c))
                cp.start()
                started.append(cp)
        for a in range(nb):
            mine = _half(big[a].shape, c, axes[a])
            for k, ch in enumerate(chips):
                landed = outs[a].at[(2 * ch[0] + ch[1],) + mine]
                _remote(landed, landed, send_sems, recv_sems, 6 * a + k, sib).wait_recv()
                cp = _remote(landed, landed, send_sems, recv_sems, 6 * a + 3 + k, sib)
                cp.start()
                started.append(cp)
        for a in range(nb):
            other = _half(big[a].shape, 1 - c, axes[a])
            for k, ch in enumerate(chips):
                landed = outs[a].at[(2 * ch[0] + ch[1],) + other]
                _remote(landed, landed, send_sems, recv_sems, 6 * a + 3 + k, sib).wait_recv()
        for a in range(ns):
            for k, ch in enumerate(chips):
                landed = outs[nb + a].at[2 * ch[0] + ch[1]]
                _remote(landed, landed, send_sems, recv_sems, 6 * nb + 3 * a + k, sib).wait_recv()
        for cp in started:
            cp.wait_send()

    arrs = list(big) + list(small)
    return pl.pallas_call(
        body, name="all_gather_weights",
        in_specs=[ANY] * n, out_specs=[ANY] * n,
        out_shape=[jax.ShapeDtypeStruct((4,) + a.shape, a.dtype) for a in arrs],
        scratch_shapes=[pltpu.SemaphoreType.DMA((n_sem,)), pltpu.SemaphoreType.DMA((n_sem,))],
        compiler_params=pltpu.CompilerParams(has_side_effects=True),
    )(*arrs)


def _sibling_exchange(grads, axes, smalls, name):
    nb = len(grads)
    n = nb + len(smalls)

    def body(*refs):
        ins, outs = refs[:n], refs[n:2 * n]
        send_sems, recv_sems = refs[2 * n:]
        x, y, c, _ = _place()
        sib = (x, y, 1 - c)
        cps = []
        for a in range(nb):
            theirs = _half(grads[a].shape[1:], 1 - c, axes[a])
            cps.append(_remote(ins[a].at[(slice(None),) + theirs], outs[a], send_sems, recv_sems, a, sib))
        for a in range(nb, n):
            cps.append(_remote(ins[a], outs[a], send_sems, recv_sems, a, sib))
        for cp in cps:
            cp.start()
        for cp in cps:
            cp.wait()

    out_shape = [jax.ShapeDtypeStruct((4,) + _half_shape(g.shape[1:], ax), F32) for g, ax in zip(grads, axes)]
    out_shape += [jax.ShapeDtypeStruct(a.shape, F32) for a in smalls]
    return pl.pallas_call(
        body, name=name, in_specs=[ANY] * n, out_specs=[ANY] * n, out_shape=out_shape,
        scratch_shapes=[pltpu.SemaphoreType.DMA((n,)), pltpu.SemaphoreType.DMA((n,))],
        compiler_params=pltpu.CompilerParams(has_side_effects=True),
    )(*grads, *smalls)


def _gather_share(lands, axes, name):
    n = len(lands)

    def body(*refs):
        outs = refs[n:2 * n]
        send_sems, recv_sems = refs[2 * n:]
        x, y, c, chips = _place()
        sib = (x, y, 1 - c)
        cps = []
        for a in range(n):
            mine = _half(lands[a].shape[1:], c, axes[a])
            for k, ch in enumerate(chips):
                landed = outs[a].at[(2 * ch[0] + ch[1],) + mine]
                cps.append(_remote(landed, landed, send_sems, recv_sems, 3 * a + k, sib))
        for cp in cps:
            cp.start()
        for a in range(n):
            other = _half(lands[a].shape[1:], 1 - c, axes[a])
            for k, ch in enumerate(chips):
                landed = outs[a].at[(2 * ch[0] + ch[1],) + other]
                _remote(landed, landed, send_sems, recv_sems, 3 * a + k, sib).wait_recv()
        for cp in cps:
            cp.wait_send()

    return pl.pallas_call(
        body, name=name, in_specs=[ANY] * n, out_specs=[ANY] * n,
        out_shape=[jax.ShapeDtypeStruct(a.shape, a.dtype) for a in lands],
        input_output_aliases={a: a for a in range(n)},
        scratch_shapes=[pltpu.SemaphoreType.DMA((3 * n,)), pltpu.SemaphoreType.DMA((3 * n,))],
        compiler_params=pltpu.CompilerParams(has_side_effects=True),
    )(*lands)


def _sibling_share(halves, name):
    n = len(halves)

    def body(*refs):
        ins, outs = refs[:n], refs[n:2 * n]
        send_sems, recv_sems = refs[2 * n:]
        x, y, c, _ = _place()
        cps = [_remote(ins[a], outs[a], send_sems, recv_sems, a, (x, y, 1 - c)) for a in range(n)]
        for cp in cps:
            cp.start()
        for cp in cps:
            cp.wait()

    return pl.pallas_call(
        body, name=name, in_specs=[ANY] * n, out_specs=[ANY] * n,
        out_shape=[jax.ShapeDtypeStruct(h.shape, F32) for h in halves],
        scratch_shapes=[pltpu.SemaphoreType.DMA((n,)), pltpu.SemaphoreType.DMA((n,))],
        compiler_params=pltpu.CompilerParams(has_side_effects=True),
    )(*halves)


HBM = pl.BlockSpec(memory_space=pltpu.HBM)
SEM = pl.BlockSpec(memory_space=pltpu.SEMAPHORE)
DATAFLOW = pltpu.SideEffectType.DATAFLOW_SIDE_EFFECTING


def _split_start(name, srcs, land_shapes, plan, n_copies, after):
    lands = [lax.empty(shp, dt) for shp, dt in land_shapes]
    bufs = list(srcs) + lands
    nb, ns = len(bufs), len(srcs)

    def body(*refs):
        send_sems, recv_sems, token = refs[nb + 1], refs[nb + 2], refs[-1]
        for k, (src, dst, to) in enumerate(plan(refs[:ns], refs[ns:nb])):
            _remote(src, dst, send_sems, recv_sems, k, to).start()
        token[...] = jnp.zeros_like(token)

    res = pl.pallas_call(
        body, name=name,
        out_shape=(pltpu.SemaphoreType.DMA((n_copies,)), pltpu.SemaphoreType.DMA((n_copies,)),
                   *[pltpu.HBM(b.shape, b.dtype) for b in bufs], jax.ShapeDtypeStruct((8, 128), F32)),
        in_specs=[HBM] * nb + [ANY],
        out_specs=(SEM, SEM, *[HBM] * nb, pl.BlockSpec(memory_space=pltpu.VMEM)),
        input_output_aliases={i: 2 + i for i in range(nb)},
        compiler_params=pltpu.CompilerParams(has_side_effects=DATAFLOW),
    )(*[pltpu.with_memory_space_constraint(b, pltpu.HBM) for b in bufs], after)
    return (res[0], res[1], list(res[2:2 + nb])), res[-1]


def _split_wait(name, handle, n_srcs, plan, after):
    send_sems, recv_sems, bufs = handle
    nb = len(bufs)

    def body(*refs):
        sends, recvs = refs[nb], refs[nb + 1]
        for k, (src, dst, to) in enumerate(plan(refs[:n_srcs], refs[n_srcs:nb])):
            cp = _remote(src, dst, sends, recvs, k, to)
            cp.wait_send()
            cp.wait_recv()

    res = pl.pallas_call(
        body, name=name, out_shape=[pltpu.HBM(b.shape, b.dtype) for b in bufs],
        in_specs=[HBM] * nb + [SEM, SEM, ANY], out_specs=[HBM] * nb,
        input_output_aliases={i: i for i in range(nb)},
        compiler_params=pltpu.CompilerParams(has_side_effects=DATAFLOW),
    )(*bufs, send_sems, recv_sems, after)
    return list(res[:n_srcs]), list(res[n_srcs:])


def _gather_plan(shapes, axes, n_whole=0):
    def plan(srcs, lands):
        x, y, c, chips = _place()
        out = []
        for a, (shape, axis) in enumerate(zip(shapes, axes)):
            mine = _half(shape, c, axis)
            for ch in chips:
                out.append((srcs[a].at[mine], lands[a].at[(2 * x + y,) + mine], (ch[0], ch[1], c)))
        for a in range(len(shapes), len(shapes) + n_whole):
            for ch in chips:
                out.append((srcs[a], lands[a].at[2 * x + y], (ch[0], ch[1], c)))
        return out
    return plan


def _sibling_plan(shapes, axes):
    def plan(srcs, lands):
        x, y, c, _ = _place()
        return [(srcs[a].at[(slice(None),) + _half(shape, 1 - c, axis)], lands[a], (x, y, 1 - c))
                for a, (shape, axis) in enumerate(zip(shapes, axes))]
    return plan


def _reduce_plan(n_big, n_small):
    def plan(srcs, lands):
        x, y, c, chips = _place()
        out = []
        for a in range(n_big):
            for k, ch in enumerate(chips):
                out.append((srcs[a].at[2 * ch[0] + ch[1]], lands[a].at[k], (ch[0], ch[1], c)))
        for a in range(n_big, n_big + n_small):
            for ch in chips:
                out.append((srcs[a], lands[a].at[2 * x + y], (ch[0], ch[1], c)))
        return out
    return plan


def _row_tile(rows, cols, mult):
    best = mult
    for t in range(mult, rows + 1, mult):
        if rows % t == 0 and t * cols * 4 <= (2 << 20):
            best = t
    return best if rows % best == 0 else rows


COL_TILE = 256


def _half_tiling(hshape, axis, mult):
    hr, hc = hshape
    if axis == 0:
        tr = _row_tile(hr, hc, mult)
        return tr, hc, hr // tr
    return hr, COL_TILE, hc // COL_TILE


def _tile_idx(axis, t):
    return (t, 0) if axis == 0 else (0, t)


def _chip_partial(place, g, t, axis, name):
    hshape = t.shape[1:]
    br, bc, nt = _half_tiling(hshape, axis, 16)

    def body(pl_ref, g_ref, t_ref, pf_ref, pb_ref):
        v = g_ref[...] + t_ref[...]
        pb_ref[...] = v.astype(BF16)

        @pl.when(pl.program_id(1) == pl_ref[0])
        def _():
            pf_ref[...] = v

    blk = (None, br, bc)
    return pl.pallas_call(
        body, name=name,
        grid_spec=pltpu.PrefetchScalarGridSpec(
            num_scalar_prefetch=1, grid=(nt, 4),
            in_specs=[pl.BlockSpec(blk, lambda i, j, p: (j,) + _tile_idx(axis, p[1] * nt + i)),
                      pl.BlockSpec(blk, lambda i, j, p: (j,) + _tile_idx(axis, i))],
            out_specs=[pl.BlockSpec((br, bc), lambda i, j, p: _tile_idx(axis, i)),
                       pl.BlockSpec(blk, lambda i, j, p: (j,) + _tile_idx(axis, i))]),
        out_shape=[jax.ShapeDtypeStruct(hshape, F32), jax.ShapeDtypeStruct((4,) + hshape, BF16)],
        compiler_params=_cp("arbitrary", "arbitrary"),
    )(place, g, t)


def _finish_half(pf, rb, axis, name):
    hshape = pf.shape
    br, bc, nt = _half_tiling(hshape, axis, 16)

    def body(pf_ref, rb_ref, o_ref):
        o_ref[...] = ((pf_ref[...] + rb_ref[0].astype(F32)) + rb_ref[1].astype(F32)) + rb_ref[2].astype(F32)

    return pl.pallas_call(
        body, name=name, grid=(nt,),
        in_specs=[pl.BlockSpec((br, bc), lambda i: _tile_idx(axis, i)),
                  pl.BlockSpec((3, br, bc), lambda i: (0,) + _tile_idx(axis, i))],
        out_specs=pl.BlockSpec((br, bc), lambda i: _tile_idx(axis, i)),
        out_shape=jax.ShapeDtypeStruct(hshape, F32),
        compiler_params=_cp("arbitrary"),
    )(pf, rb)


def _add2(a, b, name):
    def body(a_ref, b_ref, o_ref):
        o_ref[...] = a_ref[...] + b_ref[...]

    return pl.pallas_call(body, name=name, out_shape=jax.ShapeDtypeStruct(a.shape, F32))(a, b)


def _adam_math(w, g, m, v):
    m = ADAM_B1 * m + (1.0 - ADAM_B1) * g
    v = ADAM_B2 * v + (1.0 - ADAM_B2) * (g * g)
    m_hat = m / (1.0 - ADAM_B1 ** ADAM_STEP)
    v_hat = v / (1.0 - ADAM_B2 ** ADAM_STEP)
    return -ADAM_LR * (m_hat / (jnp.sqrt(v_hat) + ADAM_EPS) + ADAM_WD * w), m, v


def _adam_halves(place, w, mine, theirs, m, v, axis, name):
    br, bc, nt = _half_tiling(mine.shape, axis, 8)

    def body(pl_ref, w_ref, a_ref, b_ref, m_ref, v_ref, g_ref, d_ref, mo_ref, vo_ref):
        is_mine = pl.program_id(0) // nt == pl_ref[1]
        g = jnp.where(is_mine, a_ref[...], b_ref[...])
        d, mn, vn = _adam_math(w_ref[...], g, m_ref[...], v_ref[...])
        g_ref[...] = g
        d_ref[...] = d
        mo_ref[...] = mn
        vo_ref[...] = vn

    full = pl.BlockSpec((br, bc), lambda i, p: _tile_idx(axis, i))
    half = pl.BlockSpec((br, bc), lambda i, p: _tile_idx(axis, i % nt))
    return pl.pallas_call(
        body, name=name,
        grid_spec=pltpu.PrefetchScalarGridSpec(
            num_scalar_prefetch=1, grid=(2 * nt,), in_specs=[full, half, half, full, full], out_specs=[full] * 4),
        out_shape=[jax.ShapeDtypeStruct(w.shape, F32)] * 4, compiler_params=_cp("arbitrary"),
    )(place, w, mine, theirs, m, v)


def _add_many(xs, ys, name):
    n = len(xs)

    def body(*refs):
        for i in range(n):
            refs[2 * n + i][...] = refs[i][...] + refs[n + i][...]

    return pl.pallas_call(body, name=name, out_shape=[jax.ShapeDtypeStruct(a.shape, F32) for a in xs])(*xs, *ys)


def _adam_small(place, owns, landed, ws, ms, vs, widths):
    n, nw = len(owns), len(ws)

    def body(pl_ref, *refs):
        own_r, land_r = refs[:n], refs[n:2 * n]
        w_r, m_r, v_r = (refs[2 * n + k * nw:2 * n + (k + 1) * nw] for k in range(3))
        outs = refs[2 * n + 3 * nw:]
        g_o, d_o, m_o, v_o = outs[:n], outs[n:n + nw], outs[n + nw:n + 2 * nw], outs[n + 2 * nw:]
        for me in range(4):
            @pl.when(pl_ref[0] == me)
            def _(me=me):
                for i in range(n):
                    p = [own_r[i][...] if k == me else land_r[i][k] for k in range(4)]
                    g = ((p[0] + p[1]) + p[2]) + p[3]
                    if i < nw and widths[i]:
                        g = g[:, me * widths[i]:(me + 1) * widths[i]]
                    g_o[i][...] = g
                    if i < nw:
                        d, mn, vn = _adam_math(w_r[i][...], g, m_r[i][...], v_r[i][...])
                        d_o[i][...] = d
                        m_o[i][...] = mn
                        v_o[i][...] = vn

    g_shapes = [jax.ShapeDtypeStruct(ws[i].shape if i < nw else owns[i].shape, F32) for i in range(n)]
    w_shapes = [jax.ShapeDtypeStruct(w.shape, F32) for w in ws]
    whole = lambda a: pl.BlockSpec(a.shape, lambda i, p, nd=len(a.shape): (0,) * nd)
    ins = list(owns) + list(landed) + list(ws) + list(ms) + list(vs)
    out_shape = g_shapes + w_shapes * 3
    out = pl.pallas_call(
        body, name="adam_small",
        grid_spec=pltpu.PrefetchScalarGridSpec(num_scalar_prefetch=1, grid=(1,), in_specs=[whole(a) for a in ins],
                                               out_specs=[whole(a) for a in out_shape]),
        out_shape=out_shape, compiler_params=_cp("arbitrary"),
    )(place, *ins)
    return out[:n], out[n:n + nw], out[n + nw:n + 2 * nw], out[n + 2 * nw:]


def kernel(x, g_mix, w_in, b_gate, w_gk_up, b_gk, w_pool_grp, pool_scale, g_gla_head, w_pool_proj, w_gla_proj, w_out, g_ffn, w_up, w_conv, b_conv, w_down, g_final, loss_target, m_g_mix, m_w_in, m_b_gate, m_w_gk_up, m_b_gk, m_w_pool_grp, m_pool_scale, m_g_gla_head, m_w_pool_proj, m_w_gla_proj, m_w_out, m_g_ffn, m_w_up, m_w_conv, m_b_conv, m_w_down, m_g_final, v_g_mix, v_w_in, v_b_gate, v_w_gk_up, v_b_gk, v_w_pool_grp, v_pool_scale, v_g_gla_head, v_w_pool_proj, v_w_gla_proj, v_w_out, v_g_ffn, v_w_up, v_w_conv, v_b_conv, v_w_down, v_g_final):
    s = x.shape[1]
    ts = min(s, 512)
    tm = min(s, 256)
    cx, cy, cc = lax.axis_index("x"), lax.axis_index("y"), lax.axis_index("c")
    chip = 2 * cx + cy
    place = jnp.stack([chip, cc]).astype(jnp.int32)

    big_names = ("w_in", "w_pool_proj", "w_gla_proj", "w_out", "w_up", "w_down")
    axes = (1, 0, 0, 0, 0, 0)
    shards = dict(w_in=jnp.transpose(w_in[0]), w_pool_proj=w_pool_proj[0], w_gla_proj=w_gla_proj[0], w_out=w_out[0],
                  w_up=w_up[0], w_down=w_down[0])
    def fill_own(lands, mine):
        return [lax.dynamic_update_slice(g, o_[None], (chip, 0, 0)) for g, o_ in zip(lands, mine)]

    def gather_start(tag, halves, group_axes, whole, after):
        plan = _gather_plan([o_.shape for o_ in halves], group_axes, len(whole))
        srcs = list(halves) + list(whole)
        handle, token = _split_start("gather_" + tag + "_start", srcs, [((4,) + o_.shape, o_.dtype) for o_ in srcs], plan,
                                     3 * len(srcs), after)
        return (handle, plan, len(halves), len(srcs), group_axes), token

    def gather_finish(tag, started, after):
        handle, plan, n_halves, n, group_axes = started
        mine, lands = _split_wait("gather_" + tag + "_wait", handle, n, plan, after)
        lands[:n_halves] = _gather_share(lands[:n_halves], group_axes, "gather_" + tag + "_share")
        return fill_own(lands, mine)

    in_w, tok = gather_start("in", [shards["w_in"].astype(BF16)], axes[:1], [], g_mix)
    zero = tok[0, 0]
    own = [(shards[n] + zero).astype(BF16) for n in big_names[1:]]
    mix_w, tok = gather_start("mix", own[0:3], axes[1:4], [w_gk_up[0] + zero, w_conv[0] + zero], tok)
    ffn_w, tok = gather_start("ffn", own[3:5], axes[4:6], [], tok)
    xs, tgt = x[0], loss_target[0]
    wgrp = w_pool_grp[0]
    h = _rmsnorm(xs, g_mix + tok[0:1, 0:1], "norm_mix", ts)
    w_in_t = gather_finish("in", in_w, h)[0].reshape(N_IN, D)
    w_rt = jnp.concatenate([w_in_t[3600:], w_in_t[1536:3584], w_in_t[0:1536], w_in_t[3584:3600],
                            jnp.zeros((128 - GATE_RANK, D), BF16)], axis=0)
    nsh = N_IN // 4

    zr = _matmul_resident(h, w_rt, "in_proj", 1152, transposed=True)
    p, pp = _pool_fwd(zr, wgrp, pool_scale)
    wpp, wgla, wout, wgk4, wconv4 = gather_finish("mix", mix_w, pp)
    wgla, wout = wgla.reshape(D, D), wout.reshape(D, D)
    wgk_full = jnp.transpose(wgk4, (1, 0, 2)).reshape(GATE_RANK, 512)
    wconv_full = jnp.transpose(wconv4, (1, 0, 2)).reshape(3, N_UP)
    wgk_pad = jnp.concatenate([wgk_full, jnp.zeros((128 - GATE_RANK, 512), F32)], axis=0)
    o, og, sp = _gla_fwd(zr, wgk_pad, b_gk, g_gla_head, ts)
    x1, mixed, yp, yg = _merge_fwd(xs, zr, pp, og, b_gate, wpp, wgla, wout, ts)
    wup, wdown = gather_finish("ffn", ffn_w, x1)
    wdown = wdown.reshape(D_FF, D)
    h2 = _rmsnorm(x1, g_ffn, "norm_ffn", ts)
    u = _matmul_resident(h2, wup, "ffn_up", None)
    a, conv_out, dx2, dx2b, loss_part, dgfin = _ffn_down_loss(u, x1, tgt, wconv_full, b_conv, wdown,
                                                              g_final.reshape(1, D), tm)

    du, dbconv, dwconv = _ffn_bwd(dx2b, u, conv_out, wconv_full, wdown, tm)
    dw_down = _matmul_tn(a, dx2b, "dw_down", D, s, tm=1408)
    dw_up = _matmul_tn(h2, du, "dw_up", 1408, s, shard_major=True)

    def exchange_start(tag, grads, group_axes, after):
        plan = _sibling_plan([g.shape[1:] for g in grads], group_axes)
        lands = [((4,) + _half_shape(g.shape[1:], ax), F32) for g, ax in zip(grads, group_axes)]
        handle, token = _split_start("sibling_" + tag + "_start", grads, lands, plan, len(grads), after)
        return (handle, plan, len(grads)), token

    def partials(tag, names, group_axes, exchange, after):
        handle, plan, n = exchange
        mine, theirs = _split_wait("sibling_" + tag + "_wait", handle, n, plan, after)
        return zip(*[_chip_partial(place, g, t, ax, "chip_partial_" + nm)
                     for nm, ax, g, t in zip(names, group_axes, mine, theirs)])

    ffn_names, ffn_axes = ("w_up", "w_down"), (0, 0)
    ffn_x, token = exchange_start("ffn", [dw_up, dw_down.reshape(4, 704, D)], ffn_axes, du)
    dx1, dx1b, dgffn = _matmul_nt_normbwd(du, wup, x1, g_ffn + token[0:1, 0:1], dx2, "ffn_up_bwd", ts)
    ffn_pf, ffn_pb = partials("ffn", ffn_names, ffn_axes, ffn_x, dx1b)
    ffn_plan = _reduce_plan(2, False)
    ffn_handle, token = _split_start("reduce_ffn_start", ffn_pb, [((3,) + p.shape[1:], BF16) for p in ffn_pb],
                                     ffn_plan, 6, ffn_pf[0])

    dzg, dyp, dyg, dpp, do, dzog, dbgate, dghead = _merge_bwd(dx1b, zr, yp, yg, o, b_gate + token[0:1, 0:1], g_gla_head,
                                                             wpp, wgla, wout, ts)
    dw_out = _matmul_tn(mixed, dx1b, "dw_out", D, s)
    dw_gla = _matmul_tn(og, dyg, "dw_gla", D, s)
    dw_pp = _matmul_tn(pp, dyp, "dw_pp", 256, s, shard_major=True)

    out_names, out_axes = ("w_pool_proj", "w_gla_proj", "w_out"), (0, 0, 0)
    out_x, token = exchange_start("out", [dw_pp, dw_gla.reshape(4, 256, D), dw_out.reshape(4, 256, D)], out_axes, dpp)
    dzp, dwgrp, dscale = _pool_bwd(p, dpp, wgrp, pool_scale + token[0:1, 0:1])
    out_pf, out_pb = partials("out", out_names, out_axes, out_x, dzp)
    out_plan = _reduce_plan(3, False)
    out_handle, token = _split_start("reduce_out_start", out_pb, [((3,) + p_.shape[1:], BF16) for p_ in out_pb],
                                     out_plan, 9, out_pf[0])
    dq, dk, dv, dgpre = _gla_bwd(zr, do, sp, wgk_pad, b_gk + token[0:1, 0:1], ts)
    dzgk, dwgk, dbgk = _gk_bwd(dgpre, zr, wgk_pad, ts)
    dzr = jnp.concatenate([dzg, dv, dzog, dzp, dq, dk, dzgk], axis=1)
    dw_rt = _matmul_tn(dzr, h, "dw_in", D, s, tm=1152)

    def grad_rows(lo, hi):
        out = []
        for seg_lo, seg_hi, at in ((0, 1536, OFF_POOL), (1536, 3584, OFF_V), (3584, 3600, OFF_GK), (3600, N_IN, OFF_GATE)):
            a_, b_ = max(lo, seg_lo), min(hi, seg_hi)
            if a_ < b_:
                out.append(dw_rt[at + a_ - seg_lo:at + b_ - seg_lo])
        return jnp.concatenate(out, axis=0)

    dw_in_t = jnp.stack([grad_rows(j * nsh, (j + 1) * nsh) for j in range(4)])

    in_sib = _sibling_exchange([dw_in_t], (1,), [], "sibling_exchange_in")
    in_pf, in_pb = _chip_partial(place, dw_in_t, in_sib[0], 1, "chip_partial_w_in")
    in_plan = _reduce_plan(1, 0)
    in_handle, token = _split_start("reduce_in_start", [in_pb], [((3,) + in_pb.shape[1:], BF16)], in_plan, 3, in_pf)
    grad_x, _, dgmix = _matmul_nt_normbwd(dzr, w_rt, xs, g_mix + token[0:1, 0:1], dx1, "in_proj_bwd", ts, transposed=True)
    small_names = ("g_mix", "b_gate", "w_gk_up", "b_gk", "w_pool_grp", "pool_scale", "g_gla_head", "g_ffn", "w_conv",
                   "b_conv", "g_final")
    small_mine = [dgmix, dbgate, dwgk[:GATE_RANK], dbgk, dwgrp.reshape(4 * 128, 128), dscale, dghead, dgffn, dwconv, dbconv,
                  dgfin, loss_part]
    small_sib = _sibling_exchange([], (), small_mine, "sibling_exchange_small")
    small_chip = _add_many(small_mine, small_sib, "chip_partial_small")
    small_plan = _reduce_plan(0, len(small_chip))
    small_handle, token = _split_start("reduce_small_start", small_chip, [((4,) + a_.shape, F32) for a_ in small_chip],
                                       small_plan, 3 * len(small_chip), small_mine[0])

    ms = dict(w_in=jnp.transpose(m_w_in[0]), w_pool_proj=m_w_pool_proj[0], w_gla_proj=m_w_gla_proj[0], w_out=m_w_out[0],
              w_up=m_w_up[0], w_down=m_w_down[0])
    vs = dict(w_in=jnp.transpose(v_w_in[0]), w_pool_proj=v_w_pool_proj[0], w_gla_proj=v_w_gla_proj[0], w_out=v_w_out[0],
              w_up=v_w_up[0], w_down=v_w_down[0])
    grad, delta, new_m, new_v = {}, {}, {}, {}

    def finish_and_update(names, group_axes, part_f, landed, tag):
        halves = [_finish_half(pf, rb, ax, "finish_" + n) for n, ax, pf, rb in zip(names, group_axes, part_f, landed)]
        sib_halves = _sibling_share(halves, "sibling_share_" + tag)
        for n, ax, mine, theirs in zip(names, group_axes, halves, sib_halves):
            res = _adam_halves(place, shards[n], mine, theirs, ms[n], vs[n], ax, "adam_" + n)
            if n == "w_in":
                res = [jnp.transpose(r_) for r_ in res]
            grad[n], delta[n], new_m[n], new_v[n] = [r_[None] for r_ in res]

    _, ffn_landed = _split_wait("reduce_ffn_wait", ffn_handle, 2, ffn_plan, token)
    _, out_landed = _split_wait("reduce_out_wait", out_handle, 3, out_plan, ffn_landed[0])
    finish_and_update(ffn_names + out_names, ffn_axes + out_axes, ffn_pf + out_pf, ffn_landed + out_landed, "rest")
    _, in_landed = _split_wait("reduce_in_wait", in_handle, 1, in_plan, delta["w_out"])
    finish_and_update(("w_in",), (1,), (in_pf,), in_landed, "in")
    small_sent, small_landed = _split_wait("reduce_small_wait", small_handle, len(small_chip), small_plan, delta["w_in"])
    given = dict(g_mix=(g_mix, m_g_mix, v_g_mix), b_gate=(b_gate, m_b_gate, v_b_gate), w_gk_up=(w_gk_up, m_w_gk_up, v_w_gk_up),
                 b_gk=(b_gk, m_b_gk, v_b_gk), w_pool_grp=(w_pool_grp, m_w_pool_grp, v_w_pool_grp),
                 pool_scale=(pool_scale, m_pool_scale, v_pool_scale), g_gla_head=(g_gla_head, m_g_gla_head, v_g_gla_head),
                 g_ffn=(g_ffn, m_g_ffn, v_g_ffn), w_conv=(w_conv, m_w_conv, v_w_conv), b_conv=(b_conv, m_b_conv, v_b_conv),
                 g_final=(g_final, m_g_final, v_g_final))
    flat2 = lambda a: a.reshape(-1, a.shape[-1])
    widths = [dict(w_gk_up=128, w_conv=1408).get(n) for n in small_names]
    totals, ds, mo, vo = _adam_small(place, small_sent, small_landed, *[[flat2(given[n][k]) for n in small_names] for k in range(3)],
                                     widths)
    loss = totals[-1][0, 0]
    for i, n in enumerate(small_names):
        shp = given[n][0].shape
        grad[n], delta[n], new_m[n], new_v[n] = [r_.reshape(shp) for r_ in (totals[i], ds[i], mo[i], vo[i])]

    order = ("g_mix", "w_in", "b_gate", "w_gk_up", "b_gk", "w_pool_grp", "pool_scale", "g_gla_head", "w_pool_proj",
             "w_gla_proj", "w_out", "g_ffn", "w_up", "w_conv", "b_conv", "w_down", "g_final")
    return (loss, grad_x[None], *[grad[n] for n in order], *[delta[n] for n in order], *[new_m[n] for n in order],
            *[new_v[n] for n in order])
```

```python
import functools

import jax
import jax.numpy as jnp
from jax import lax
from jax.experimental import pallas as pl
from jax.experimental.pallas import tpu as pltpu

F32 = jnp.float32
BF16 = jnp.bfloat16
MESH = pl.DeviceIdType.MESH

D = 1024
EPS = 1e-6
CHUNK = 64
POOL_W = 512
POOL_WINDOWS = (2, 4, 8, 16)
HEADS = 4
HK = 128
HV = 256
GATE_RANK = 16
D_FF = 2816
N_UP = 2 * D_FF
N_IN = 5648
QSCALE = HK ** -0.5
N_INR = 5760
OFF_GATE, OFF_V, OFF_OG, OFF_POOL, OFF_Q, OFF_K, OFF_GK = 0, 2048, 3072, 4096, 4608, 5120, 5632

ADAM_LR, ADAM_B1, ADAM_B2, ADAM_EPS, ADAM_WD, ADAM_STEP = 0.001, 0.9, 0.999, 1e-08, 0.01, 10

VMEM_LIMIT = 56 * 1024 * 1024


def _cp(*sem):
    return pltpu.CompilerParams(dimension_semantics=sem if sem else None, vmem_limit_bytes=VMEM_LIMIT)


def _dot(a, b):
    return jnp.dot(a, b, preferred_element_type=F32)


def _dot_nt(a, b):
    return lax.dot_general(a, b, (((1,), (1,)), ((), ())), preferred_element_type=F32)


def _dot_tn(a, b):
    return lax.dot_general(a, b, (((0,), (0,)), ((), ())), preferred_element_type=F32)


def _sigmoid(v):
    return 1.0 / (1.0 + jnp.exp(-v))


def _rows(shape):
    return lax.broadcasted_iota(jnp.int32, shape, 0)


def _pick_row(v, r):
    return jnp.sum(jnp.where(_rows(v.shape) == r, v, 0.0), axis=0, keepdims=True)


def _rmsnorm(x, g, name, ts):
    s = x.shape[0]

    def body(x_ref, g_ref, h_ref):
        xv = x_ref[...]
        r = lax.rsqrt(jnp.mean(xv * xv, axis=-1, keepdims=True) + EPS)
        h_ref[...] = (xv * r * g_ref[...]).astype(BF16)

    return pl.pallas_call(
        body, name=name, grid=(s // ts,),
        in_specs=[pl.BlockSpec((ts, D), lambda i: (i, 0)), pl.BlockSpec((1, D), lambda i: (0, 0))],
        out_specs=pl.BlockSpec((ts, D), lambda i: (i, 0)), out_shape=jax.ShapeDtypeStruct((s, D), BF16),
        compiler_params=_cp("arbitrary"),
    )(x, g)


MM_ROWS = 512


def _matmul_resident(h, w, name, tn, transposed=False):
    s = h.shape[0]
    if transposed:
        nj = w.shape[0] // tn
        w_spec = pl.BlockSpec((tn, D), lambda j: (j, 0))
    elif w.ndim == 3:
        nj, tn = w.shape[0], w.shape[2]
        w_spec = pl.BlockSpec((None, D, tn), lambda j: (j, 0, 0))
    else:
        nj = w.shape[1] // tn
        w_spec = pl.BlockSpec((D, tn), lambda j: (0, j))
    mm = _dot_nt if transposed else _dot
    rc = min(s, MM_ROWS)

    def body(h_ref, w_ref, z_ref):
        for r0 in range(0, s, rc):
            z_ref[r0:r0 + rc, :] = mm(h_ref[r0:r0 + rc, :], w_ref[...]).astype(BF16)

    return pl.pallas_call(
        body, name=name, grid=(nj,),
        in_specs=[pl.BlockSpec((s, D), lambda j: (0, 0)), w_spec],
        out_specs=pl.BlockSpec((s, tn), lambda j: (0, j)), out_shape=jax.ShapeDtypeStruct((s, nj * tn), BF16),
        compiler_params=_cp("arbitrary"),
    )(h, w)


def _matmul_nt_normbwd(dz, w, x, g, resid, name, ts, transposed=False):
    s = x.shape[0]

    def body(dz_ref, w_hbm, x_ref, g_ref, r_ref, o_ref, ob_ref, dg_ref, w_ref, sem):
        @pl.when(pl.program_id(0) == 0)
        def _():
            cp = pltpu.make_async_copy(w_hbm, w_ref, sem)
            cp.start()
            cp.wait()
            dg_ref[...] = jnp.zeros_like(dg_ref)

        if transposed:
            dh = _dot(dz_ref[...], w_ref[...])
        else:
            kc = w.shape[2]
            dh = _dot_nt(dz_ref[:, 0:kc], w_ref[0])
            for j in range(1, w.shape[0]):
                dh = dh + _dot_nt(dz_ref[:, j * kc:(j + 1) * kc], w_ref[j])
        xv = x_ref[...]
        r = lax.rsqrt(jnp.mean(xv * xv, axis=-1, keepdims=True) + EPS)
        xh = xv * r
        dg_ref[...] += jnp.sum(dh * xh, axis=0, keepdims=True)
        dxh = dh * g_ref[...]
        out = r_ref[...] + r * (dxh - xh * jnp.mean(dxh * xh, axis=-1, keepdims=True))
        o_ref[...] = out
        ob_ref[...] = out.astype(BF16)

    row = lambda i: (i, 0)
    kdim = dz.shape[1]
    return pl.pallas_call(
        body, name=name, grid=(s // ts,),
        in_specs=[pl.BlockSpec((ts, kdim), row), ANY, pl.BlockSpec((ts, D), row),
                  pl.BlockSpec((1, D), lambda i: (0, 0)), pl.BlockSpec((ts, D), row)],
        out_specs=[pl.BlockSpec((ts, D), row), pl.BlockSpec((ts, D), row), pl.BlockSpec((1, D), lambda i: (0, 0))],
        out_shape=[jax.ShapeDtypeStruct((s, D), F32), jax.ShapeDtypeStruct((s, D), BF16),
                   jax.ShapeDtypeStruct((1, D), F32)],
        scratch_shapes=[pltpu.VMEM(w.shape, BF16), pltpu.SemaphoreType.DMA],
        compiler_params=_cp("arbitrary"),
    )(dz, w, x, g, resid)


def _matmul_tn(a, b, name, tn, shard_major=False, tm=None):
    s, m = a.shape
    n = b.shape[1]
    tm = m if tm is None else tm
    ni, nj = m // tm, n // tn

    def body(a_ref, b_ref, o_ref):
        o_ref[...] = _dot_tn(a_ref[...], b_ref[...]).astype(BF16)

    if shard_major:
        out_spec = pl.BlockSpec((None, tm, tn), lambda i, j: (j, i, 0))
        out_shape = jax.ShapeDtypeStruct((nj, m, tn), BF16)
    else:
        out_spec = pl.BlockSpec((tm, tn), lambda i, j: (i, j))
        out_shape = jax.ShapeDtypeStruct((m, n), BF16)
    return pl.pallas_call(
        body, name=name, grid=(ni, nj),
        in_specs=[pl.BlockSpec((s, tm), lambda i, j: (0, i)), pl.BlockSpec((s, tn), lambda i, j: (0, j))],
        out_specs=out_spec, out_shape=out_shape,
        compiler_params=_cp("arbitrary", "arbitrary"),
    )(a, b)


def _pool_fwd(zr, wgrp, scale):
    s = zr.shape[0]

    def body(u_ref, w_ref, sc_ref, p_ref, pp_ref):
        row = _rows((s, 128))
        for gi, win in enumerate(POOL_WINDOWS):
            cs = slice(gi * 128, (gi + 1) * 128)
            u = u_ref[:, cs].astype(F32)
            acc, k = u, 1
            while k < win:
                acc = acc + jnp.where(row >= k, pltpu.roll(acc, k, 0), 0.0)
                k *= 2
            cnt = jnp.minimum(row + 1, win).astype(F32)
            p = (acc / cnt - u).astype(BF16)
            p_ref[:, cs] = p
            pp_ref[:, cs] = (_dot(p, w_ref[gi].astype(BF16)) * sc_ref[:, cs]).astype(BF16)

    return pl.pallas_call(
        body, name="pool_fwd", grid=(1,),
        in_specs=[pl.BlockSpec((s, POOL_W), lambda i: (0, OFF_POOL // POOL_W)),
                  pl.BlockSpec((4, 128, 128), lambda i: (0, 0, 0)), pl.BlockSpec((1, POOL_W), lambda i: (0, 0))],
        out_specs=[pl.BlockSpec((s, POOL_W), lambda i: (0, 0))] * 2,
        out_shape=[jax.ShapeDtypeStruct((s, POOL_W), BF16)] * 2,
        compiler_params=_cp("arbitrary"),
    )(zr, wgrp, scale)


def _pool_bwd(p, dpp, wgrp, scale):
    s = p.shape[0]

    def body(p_ref, dpp_ref, w_ref, sc_ref, dz_ref, dw_ref, dsc_ref):
        row = _rows((s, 128))
        for gi, win in enumerate(POOL_WINDOWS):
            cs = slice(gi * 128, (gi + 1) * 128)
            pv = p_ref[:, cs]
            wb = w_ref[gi].astype(BF16)
            dpp_v = dpp_ref[:, cs].astype(F32)
            dsc_ref[:, cs] = jnp.sum(dpp_v * _dot(pv, wb), axis=0, keepdims=True)
            dpm = (dpp_v * sc_ref[:, cs]).astype(BF16)
            dw_ref[gi] = _dot_tn(pv, dpm)
            dp = _dot_nt(dpm, wb)
            cnt = jnp.minimum(row + 1, win).astype(F32)
            acc, k = dp / cnt, 1
            while k < win:
                acc = acc + jnp.where(row < s - k, pltpu.roll(acc, s - k, 0), 0.0)
                k *= 2
            dz_ref[:, cs] = (acc - dp).astype(BF16)

    full = lambda i: (0, 0)
    return pl.pallas_call(
        body, name="pool_bwd", grid=(1,),
        in_specs=[pl.BlockSpec((s, POOL_W), full), pl.BlockSpec((s, POOL_W), full),
                  pl.BlockSpec((4, 128, 128), lambda i: (0, 0, 0)), pl.BlockSpec((1, POOL_W), full)],
        out_specs=[pl.BlockSpec((s, POOL_W), full), pl.BlockSpec((4, 128, 128), lambda i: (0, 0, 0)),
                   pl.BlockSpec((1, POOL_W), full)],
        out_shape=[jax.ShapeDtypeStruct((s, POOL_W), BF16), jax.ShapeDtypeStruct((4, 128, 128), F32),
                   jax.ShapeDtypeStruct((1, POOL_W), F32)],
        compiler_params=_cp("arbitrary"),
    )(p, dpp, wgrp, scale)


def _gla_decay(zgk_ref, wgk_ref, bgk_ref, rb):
    g = _dot(zgk_ref[...], wgk_ref[...].astype(BF16)) + bgk_ref[...]
    la = (jnp.minimum(g, 0.0) - jnp.log(1.0 + jnp.exp(-jnp.abs(g)))) * (1.0 / 16.0)
    rowm = _rows(la.shape) & (CHUNK - 1)
    bc, k = la, 1
    while k < CHUNK:
        bc = bc + jnp.where(rowm >= k, pltpu.roll(bc, k, 0), 0.0)
        k *= 2
    return g, jnp.exp(bc), jnp.exp(-bc)


GLA_HB = 4


def _gla_specs(rb, rmap):
    wk, wv = GLA_HB * HK, GLA_HB * HV
    return [pl.BlockSpec((rb, wk), lambda h, r: (rmap(h, r), OFF_Q // wk + h)),
            pl.BlockSpec((rb, wk), lambda h, r: (rmap(h, r), OFF_K // wk + h)),
            pl.BlockSpec((rb, wv), lambda h, r: (rmap(h, r), OFF_V // wv + h)),
            pl.BlockSpec((rb, 128), lambda h, r: (rmap(h, r), OFF_GK // 128))]


def _gla_fwd(zr, wgk, bgk, ghead, rb):
    s = zr.shape[0]
    nc = rb // CHUNK
    wk, wv = GLA_HB * HK, GLA_HB * HV

    def body(q_ref, k_ref, v_ref, zgk_ref, zog_ref, wgk_ref, bgk_ref, gh_ref, o_ref, og_ref, sp_ref, st_ref):
        @pl.when(pl.program_id(1) == 0)
        def _():
            st_ref[...] = jnp.zeros_like(st_ref)

        _, e_pos, e_neg = _gla_decay(zgk_ref, wgk_ref, bgk_ref, rb)
        lower = _rows((CHUNK, CHUNK)) >= lax.broadcasted_iota(jnp.int32, (CHUNK, CHUNK), 1)
        for c in range(nc):
            sl = slice(c * CHUNK, (c + 1) * CHUNK)
            for hh in range(GLA_HB):
                ck, cv = slice(hh * HK, (hh + 1) * HK), slice(hh * HV, (hh + 1) * HV)
                q = q_ref[sl, ck].astype(F32) * QSCALE
                k = k_ref[sl, ck].astype(F32)
                v = v_ref[sl, cv]
                ec, fc = e_pos[sl, ck], e_neg[sl, ck]
                qfw = (q * ec).astype(BF16)
                kfw_f = k * fc
                s_fw = _dot_nt(qfw, kfw_f.astype(BF16))
                s_bw = _dot_nt((q * fc).astype(BF16), (k * ec).astype(BF16))
                pm = jnp.where(lower, s_fw, s_bw).astype(BF16)
                st = st_ref[hh]
                stb = st.astype(BF16)
                sp_ref[c, hh] = stb
                o = _dot(pm, v) + _dot_nt(qfw, stb)
                e_last = _pick_row(ec, CHUNK - 1)
                kdec = (kfw_f * e_last).astype(BF16)
                st_ref[hh] = st * e_last + _dot_tn(v, kdec)
                r = lax.rsqrt(jnp.mean(o * o, axis=-1, keepdims=True) + EPS)
                zo = zog_ref[sl, cv].astype(F32)
                o_ref[sl, cv] = o.astype(BF16)
                og_ref[sl, cv] = (o * r * gh_ref[...] * zo * _sigmoid(zo)).astype(BF16)

    rmap = lambda h, r: r
    return pl.pallas_call(
        body, name="gla_fwd", grid=(HEADS // GLA_HB, s // rb),
        in_specs=_gla_specs(rb, rmap) + [
            pl.BlockSpec((rb, wv), lambda h, r: (r, OFF_OG // wv + h)),
            pl.BlockSpec((128, wk), lambda h, r: (0, h)), pl.BlockSpec((1, wk), lambda h, r: (0, h)),
            pl.BlockSpec((1, HV), lambda h, r: (0, 0))],
        out_specs=[pl.BlockSpec((rb, wv), lambda h, r: (r, h)), pl.BlockSpec((rb, wv), lambda h, r: (r, h)),
                   pl.BlockSpec((nc, GLA_HB, HV, HK), lambda h, r: (r, h, 0, 0))],
        out_shape=[jax.ShapeDtypeStruct((s, D), BF16), jax.ShapeDtypeStruct((s, D), BF16),
                   jax.ShapeDtypeStruct((s // CHUNK, HEADS, HV, HK), BF16)],
        scratch_shapes=[pltpu.VMEM((GLA_HB, HV, HK), F32)],
        compiler_params=_cp("arbitrary", "arbitrary"),
    )(zr, zr, zr, zr, zr, wgk, bgk, ghead)


def _gla_bwd(zr, do, sp, wgk, bgk, rb):
    s = zr.shape[0]
    nc = rb // CHUNK
    nr = s // rb
    wk, wv = GLA_HB * HK, GLA_HB * HV

    def body(q_ref, k_ref, v_ref, zgk_ref, do_ref, sp_ref, wgk_ref, bgk_ref, dq_ref, dk_ref, dv_ref, dg_ref,
             gt_ref, dbc_ref):
        @pl.when(pl.program_id(1) == 0)
        def _():
            gt_ref[...] = jnp.zeros_like(gt_ref)

        g, e_pos, e_neg = _gla_decay(zgk_ref, wgk_ref, bgk_ref, rb)
        lower = _rows((CHUNK, CHUNK)) >= lax.broadcasted_iota(jnp.int32, (CHUNK, CHUNK), 1)
        is_last = _rows((CHUNK, HK)) == CHUNK - 1
        for c in reversed(range(nc)):
            sl = slice(c * CHUNK, (c + 1) * CHUNK)
            for hh in range(GLA_HB):
                ck, cv = slice(hh * HK, (hh + 1) * HK), slice(hh * HV, (hh + 1) * HV)
                q = q_ref[sl, ck].astype(F32) * QSCALE
                k = k_ref[sl, ck].astype(F32)
                v = v_ref[sl, cv]
                dov = do_ref[sl, cv]
                ec, fc = e_pos[sl, ck], e_neg[sl, ck]
                qfw_f, kfw_f, qbw_f, kbw_f = q * ec, k * fc, q * fc, k * ec
                qfw, kfw, qbw, kbw = qfw_f.astype(BF16), kfw_f.astype(BF16), qbw_f.astype(BF16), kbw_f.astype(BF16)
                pm = jnp.where(lower, _dot_nt(qfw, kfw), _dot_nt(qbw, kbw)).astype(BF16)
                e_last = _pick_row(ec, CHUNK - 1)
                kdec = (kfw_f * e_last).astype(BF16)
                gt = gt_ref[hh]
                gtb = gt.astype(BF16)
                spv = sp_ref[c, hh]
                dp = _dot_nt(dov, v)
                dv_ref[sl, cv] = (_dot_tn(pm, dov) + _dot_nt(kdec, gtb)).astype(BF16)
                ds_fw = jnp.where(lower, dp, 0.0).astype(BF16)
                ds_bw = jnp.where(lower, 0.0, dp).astype(BF16)
                dqfw = _dot(ds_fw, kfw) + _dot(dov, spv)
                dkfw = _dot_tn(ds_fw, qfw)
                dqbw = _dot(ds_bw, kbw)
                dkbw = _dot_tn(ds_bw, qbw)
                dkdec = _dot(v, gtb)
                de_last = (jnp.sum(gt * spv.astype(F32), axis=0, keepdims=True)
                           + jnp.sum(dkdec * kfw_f, axis=0, keepdims=True))
                dkfw = dkfw + dkdec * e_last
                dq_ref[sl, ck] = ((dqfw * ec + dqbw * fc) * QSCALE).astype(BF16)
                dk_ref[sl, ck] = (dkfw * fc + dkbw * ec).astype(BF16)
                dbc = dqfw * qfw_f - dqbw * qbw_f + dkbw * kbw_f - dkfw * kfw_f
                dbc_ref[sl, ck] = dbc + jnp.where(is_last, de_last * e_last, 0.0)
                gt_ref[hh] = _dot_tn(dov, qfw) + gt * e_last
        rowm = _rows((rb, wk)) & (CHUNK - 1)
        dla, kk = dbc_ref[...], 1
        while kk < CHUNK:
            dla = dla + jnp.where(rowm < CHUNK - kk, pltpu.roll(dla, rb - kk, 0), 0.0)
            kk *= 2
        dg_ref[...] = dla * (1.0 / 16.0) * _sigmoid(-g)

    rmap = lambda h, r: nr - 1 - r
    rev = lambda h, r: (nr - 1 - r, h)
    return pl.pallas_call(
        body, name="gla_bwd", grid=(HEADS // GLA_HB, nr),
        in_specs=_gla_specs(rb, rmap) + [
            pl.BlockSpec((rb, wv), rev),
            pl.BlockSpec((nc, GLA_HB, HV, HK), lambda h, r: (nr - 1 - r, h, 0, 0)),
            pl.BlockSpec((128, wk), lambda h, r: (0, h)), pl.BlockSpec((1, wk), lambda h, r: (0, h))],
        out_specs=[pl.BlockSpec((rb, wk), rev), pl.BlockSpec((rb, wk), rev), pl.BlockSpec((rb, wv), rev),
                   pl.BlockSpec((rb, wk), rev)],
        out_shape=[jax.ShapeDtypeStruct((s, HEADS * HK), BF16), jax.ShapeDtypeStruct((s, HEADS * HK), BF16),
                   jax.ShapeDtypeStruct((s, D), BF16), jax.ShapeDtypeStruct((s, HEADS * HK), F32)],
        scratch_shapes=[pltpu.VMEM((GLA_HB, HV, HK), F32), pltpu.VMEM((rb, wk), F32)],
        compiler_params=_cp("arbitrary", "arbitrary"),
    )(zr, zr, zr, zr, do, sp, wgk, bgk)


def _gk_bwd(dgpre, zr, wgk, ts):
    s = zr.shape[0]

    def body(dg_ref, zgk_ref, w_ref, dz_ref, dw_ref, db_ref):
        @pl.when(pl.program_id(0) == 0)
        def _():
            dw_ref[...] = jnp.zeros_like(dw_ref)
            db_ref[...] = jnp.zeros_like(db_ref)

        dg = dg_ref[...]
        dgb = dg.astype(BF16)
        dz_ref[...] = _dot_nt(dgb, w_ref[...].astype(BF16)).astype(BF16)
        dw_ref[...] += _dot_tn(zgk_ref[...], dgb)
        db_ref[...] += jnp.sum(dg, axis=0, keepdims=True)

    return pl.pallas_call(
        body, name="gk_bwd", grid=(s // ts,),
        in_specs=[pl.BlockSpec((ts, 512), lambda i: (i, 0)), pl.BlockSpec((ts, 128), lambda i: (i, OFF_GK // 128)),
                  pl.BlockSpec((128, 512), lambda i: (0, 0))],
        out_specs=[pl.BlockSpec((ts, 128), lambda i: (i, 0)), pl.BlockSpec((128, 512), lambda i: (0, 0)),
                   pl.BlockSpec((1, 512), lambda i: (0, 0))],
        out_shape=[jax.ShapeDtypeStruct((s, 128), BF16), jax.ShapeDtypeStruct((128, 512), F32),
                   jax.ShapeDtypeStruct((1, 512), F32)],
        compiler_params=_cp("arbitrary"),
    )(dgpre, zr, wgk)


def _merge_fwd(x, zr, pp, og, bgate, wpp, wgla, wout, ts):
    s = x.shape[0]

    def body(x_ref, z0_ref, z1_ref, pp_ref, og_ref, bg_ref, wpp_ref, wgla_ref, wout_ref,
             x1_ref, mix_ref, yp_ref, yg_ref):
        ppv = pp_ref[...]
        yp = jnp.concatenate([_dot(ppv, wpp_ref[j]) for j in range(4)], axis=1)
        yg = _dot(og_ref[...], wgla_ref[...])
        g0 = _sigmoid(z0_ref[...].astype(F32) + bg_ref[:, :D])
        g1 = _sigmoid(z1_ref[...].astype(F32) + bg_ref[:, D:])
        mixed = (g0 * yp + g1 * yg).astype(BF16)
        x1_ref[...] = x_ref[...] + _dot(mixed, wout_ref[...])
        mix_ref[...] = mixed
        yp_ref[...] = yp.astype(BF16)
        yg_ref[...] = yg.astype(BF16)

    row = lambda i: (i, 0)
    const2 = lambda i: (0, 0)
    return pl.pallas_call(
        body, name="merge_fwd", grid=(s // ts,),
        in_specs=[pl.BlockSpec((ts, D), row), pl.BlockSpec((ts, D), lambda i: (i, 0)), pl.BlockSpec((ts, D), lambda i: (i, 1)),
                  pl.BlockSpec((ts, POOL_W), row), pl.BlockSpec((ts, D), row), pl.BlockSpec((1, 2 * D), const2),
                  pl.BlockSpec((4, POOL_W, 256), lambda i: (0, 0, 0)), pl.BlockSpec((D, D), const2),
                  pl.BlockSpec((D, D), const2)],
        out_specs=[pl.BlockSpec((ts, D), row)] * 4,
        out_shape=[jax.ShapeDtypeStruct((s, D), F32)] + [jax.ShapeDtypeStruct((s, D), BF16)] * 3,
        compiler_params=_cp("arbitrary"),
    )(x, zr, zr, pp, og, bgate, wpp, wgla, wout)


def _merge_bwd(dx1b, zr, yp, yg, o, bgate, ghead, wpp, wgla, wout, ts):
    s = dx1b.shape[0]

    def body(dx_ref, z0_ref, z1_ref, zog_ref, yp_ref, yg_ref, o_ref, bg_ref, gh_ref, wpp_ref, wgla_ref, wout_ref,
             dzg_ref, dyp_ref, dyg_ref, dpp_ref, do_ref, dzog_ref, dbg_ref, dgh_ref):
        @pl.when(pl.program_id(0) == 0)
        def _():
            dbg_ref[...] = jnp.zeros_like(dbg_ref)
            dgh_ref[...] = jnp.zeros_like(dgh_ref)

        dmix = _dot_nt(dx_ref[...], wout_ref[...])
        g0 = _sigmoid(z0_ref[...].astype(F32) + bg_ref[:, :D])
        g1 = _sigmoid(z1_ref[...].astype(F32) + bg_ref[:, D:])
        dypb = (dmix * g0).astype(BF16)
        dygb = (dmix * g1).astype(BF16)
        dz0 = dmix * yp_ref[...].astype(F32) * g0 * (1.0 - g0)
        dz1 = dmix * yg_ref[...].astype(F32) * g1 * (1.0 - g1)
        dzg_ref[:, :D] = dz0.astype(BF16)
        dzg_ref[:, D:] = dz1.astype(BF16)
        dbg_ref[:, :D] += jnp.sum(dz0, axis=0, keepdims=True)
        dbg_ref[:, D:] += jnp.sum(dz1, axis=0, keepdims=True)
        dyp_ref[...] = dypb
        dyg_ref[...] = dygb
        dpp = _dot_nt(dypb[:, 0:256], wpp_ref[0])
        for j in range(1, 4):
            dpp = dpp + _dot_nt(dypb[:, j * 256:(j + 1) * 256], wpp_ref[j])
        dpp_ref[...] = dpp.astype(BF16)
        dog = _dot_nt(dygb, wgla_ref[...])
        gh = gh_ref[...]
        dgh = jnp.zeros((1, HV), F32)
        for h in range(HEADS):
            cs = slice(h * HV, (h + 1) * HV)
            ov = o_ref[:, cs].astype(F32)
            r = lax.rsqrt(jnp.mean(ov * ov, axis=-1, keepdims=True) + EPS)
            oh = ov * r
            zo = zog_ref[:, cs].astype(F32)
            sg = _sigmoid(zo)
            dog_h = dog[:, cs]
            don = dog_h * zo * sg
            dzog_ref[:, cs] = (dog_h * oh * gh * sg * (1.0 + zo * (1.0 - sg))).astype(BF16)
            dgh = dgh + jnp.sum(don * oh, axis=0, keepdims=True)
            doh = don * gh
            do_ref[:, cs] = (r * (doh - oh * jnp.mean(doh * oh, axis=-1, keepdims=True))).astype(BF16)
        dgh_ref[...] += dgh

    row = lambda i: (i, 0)
    const2 = lambda i: (0, 0)
    return pl.pallas_call(
        body, name="merge_bwd", grid=(s // ts,),
        in_specs=[pl.BlockSpec((ts, D), row), pl.BlockSpec((ts, D), lambda i: (i, 0)), pl.BlockSpec((ts, D), lambda i: (i, 1)),
                  pl.BlockSpec((ts, D), lambda i: (i, OFF_OG // D)), pl.BlockSpec((ts, D), row), pl.BlockSpec((ts, D), row),
                  pl.BlockSpec((ts, D), row), pl.BlockSpec((1, 2 * D), const2), pl.BlockSpec((1, HV), const2),
                  pl.BlockSpec((4, POOL_W, 256), lambda i: (0, 0, 0)), pl.BlockSpec((D, D), const2),
                  pl.BlockSpec((D, D), const2)],
        out_specs=[pl.BlockSpec((ts, 2 * D), row), pl.BlockSpec((ts, D), row), pl.BlockSpec((ts, D), row),
                   pl.BlockSpec((ts, POOL_W), row), pl.BlockSpec((ts, D), row), pl.BlockSpec((ts, D), row),
                   pl.BlockSpec((1, 2 * D), const2), pl.BlockSpec((1, HV), const2)],
        out_shape=[jax.ShapeDtypeStruct((s, 2 * D), BF16), jax.ShapeDtypeStruct((s, D), BF16),
                   jax.ShapeDtypeStruct((s, D), BF16), jax.ShapeDtypeStruct((s, POOL_W), BF16),
                   jax.ShapeDtypeStruct((s, D), BF16), jax.ShapeDtypeStruct((s, D), BF16),
                   jax.ShapeDtypeStruct((1, 2 * D), F32), jax.ShapeDtypeStruct((1, HV), F32)],
        compiler_params=_cp("arbitrary"),
    )(dx1b, zr, zr, zr, yp, yg, o, bgate, ghead, wpp, wgla, wout)


HALO = 16
CCH = 1408


def _conv_taps(u_ref, halo_ref, cs, first, ts):
    u = u_ref[:, cs].astype(F32)
    hal = halo_ref[:, cs].astype(F32)
    h1 = jnp.where(first, 0.0, _pick_row(hal, HALO - 1))
    h2 = jnp.where(first, 0.0, _pick_row(hal, HALO - 2))
    row = _rows(u.shape)
    r1 = jnp.where(row == 0, h1, pltpu.roll(u, 1, 0))
    r2 = jnp.where(row == 0, h2, jnp.where(row == 1, h1, pltpu.roll(u, 2, 0)))
    return u, r1, r2


def _ffn_down_loss(u, x1, tgt, wconv, bconv, wdown, gfin, ts):
    s = x1.shape[0]

    def body(u_ref, halo_ref, x1_ref, t_ref, wc_ref, bc_ref, wd_ref, gf_ref, a_ref, c_ref, dx_ref, dxb_ref, ls_ref,
             dgf_ref):
        i = pl.program_id(0)

        @pl.when(i == 0)
        def _():
            ls_ref[...] = jnp.zeros_like(ls_ref)
            dgf_ref[...] = jnp.zeros_like(dgf_ref)

        first = i == 0
        acc = x1_ref[...]
        for hf in range(2):
            cg = slice(hf * CCH, (hf + 1) * CCH)
            cv = slice(D_FF + hf * CCH, D_FF + (hf + 1) * CCH)
            vals = []
            for cs in (cg, cv):
                u0, u1, u2 = _conv_taps(u_ref, halo_ref, cs, first, ts)
                vals.append(bc_ref[:, cs] + wc_ref[0:1, cs] * u2 + wc_ref[1:2, cs] * u1 + wc_ref[2:3, cs] * u0)
                c_ref[:, cs] = vals[-1].astype(BF16)
            a = (vals[0] * _sigmoid(vals[0]) * vals[1]).astype(BF16)
            a_ref[:, cg] = a
            acc = acc + _dot(a, wd_ref[cg, :])
        r = lax.rsqrt(jnp.mean(acc * acc, axis=-1, keepdims=True) + EPS)
        xh = acc * r
        gf = gf_ref[...]
        err = xh * gf - t_ref[...]
        ls_ref[...] += (0.5 / D) * jnp.sum(jnp.sum(err * err, axis=-1, keepdims=True), axis=0, keepdims=True)
        dy = err * (1.0 / D)
        dgf_ref[...] += jnp.sum(dy * xh, axis=0, keepdims=True)
        dxh = dy * gf
        dx = r * (dxh - xh * jnp.mean(dxh * xh, axis=-1, keepdims=True))
        dx_ref[...] = dx
        dxb_ref[...] = dx.astype(BF16)

    row = lambda i: (i, 0)
    const2 = lambda i: (0, 0)
    return pl.pallas_call(
        body, name="ffn_down_loss", grid=(s // ts,),
        in_specs=[pl.BlockSpec((ts, N_UP), row),
                  pl.BlockSpec((HALO, N_UP), lambda i: (jnp.maximum(i * (ts // HALO) - 1, 0), 0)),
                  pl.BlockSpec((ts, D), row), pl.BlockSpec((ts, D), row), pl.BlockSpec((3, N_UP), const2),
                  pl.BlockSpec((1, N_UP), const2), pl.BlockSpec((D_FF, D), const2), pl.BlockSpec((1, D), const2)],
        out_specs=[pl.BlockSpec((ts, D_FF), row), pl.BlockSpec((ts, N_UP), row), pl.BlockSpec((ts, D), row),
                   pl.BlockSpec((ts, D), row), pl.BlockSpec((1, 128), const2), pl.BlockSpec((1, D), const2)],
        out_shape=[jax.ShapeDtypeStruct((s, D_FF), BF16), jax.ShapeDtypeStruct((s, N_UP), BF16),
                   jax.ShapeDtypeStruct((s, D), F32), jax.ShapeDtypeStruct((s, D), BF16),
                   jax.ShapeDtypeStruct((1, 128), F32), jax.ShapeDtypeStruct((1, D), F32)],
        compiler_params=_cp("arbitrary"),
    )(u, u, x1, tgt, wconv, bconv, wdown, gfin)


def _ffn_bwd(dx2b, u, c, wconv, wdown, ts):
    s = dx2b.shape[0]
    nt = s // ts

    def body(dx_ref, u_ref, c_ref, wc_ref, wd_ref, du_ref, db_ref, dw_ref, nxt_ref):
        @pl.when(pl.program_id(0) == 0)
        def _():
            db_ref[...] = jnp.zeros_like(db_ref)
            dw_ref[...] = jnp.zeros_like(dw_ref)
            nxt_ref[...] = jnp.zeros_like(nxt_ref)

        dxv = dx_ref[...]
        row = _rows((ts, CCH))
        for hf in range(2):
            cg = slice(hf * CCH, (hf + 1) * CCH)
            cv = slice(D_FF + hf * CCH, D_FF + (hf + 1) * CCH)
            da = _dot_nt(dxv, wd_ref[cg, :])
            gate = c_ref[:, cg].astype(F32)
            val = c_ref[:, cv].astype(F32)
            sg = _sigmoid(gate)
            dcs = (da * val * sg * (1.0 + gate * (1.0 - sg)), da * gate * sg)
            for cs, dc in zip((cg, cv), dcs):
                n1 = nxt_ref[0:1, cs]
                n2 = nxt_ref[1:2, cs]
                f1 = jnp.where(row == ts - 1, n1, pltpu.roll(dc, ts - 1, 0))
                f2 = jnp.where(row == ts - 1, n2, jnp.where(row == ts - 2, n1, pltpu.roll(dc, ts - 2, 0)))
                uv = u_ref[:, cs].astype(F32)
                db_ref[:, cs] += jnp.sum(dc, axis=0, keepdims=True)
                dw_ref[0:1, cs] += jnp.sum(f2 * uv, axis=0, keepdims=True)
                dw_ref[1:2, cs] += jnp.sum(f1 * uv, axis=0, keepdims=True)
                dw_ref[2:3, cs] += jnp.sum(dc * uv, axis=0, keepdims=True)
                du_ref[:, cs] = (wc_ref[2:3, cs] * dc + wc_ref[1:2, cs] * f1 + wc_ref[0:1, cs] * f2).astype(BF16)
                nxt_ref[:, cs] = dc[0:8, :]

    rev = lambda i: (nt - 1 - i, 0)
    const2 = lambda i: (0, 0)
    return pl.pallas_call(
        body, name="ffn_bwd", grid=(nt,),
        in_specs=[pl.BlockSpec((ts, D), rev), pl.BlockSpec((ts, N_UP), rev), pl.BlockSpec((ts, N_UP), rev),
                  pl.BlockSpec((3, N_UP), const2), pl.BlockSpec((D_FF, D), const2)],
        out_specs=[pl.BlockSpec((ts, N_UP), rev), pl.BlockSpec((1, N_UP), const2), pl.BlockSpec((3, N_UP), const2)],
        out_shape=[jax.ShapeDtypeStruct((s, N_UP), BF16), jax.ShapeDtypeStruct((1, N_UP), F32),
                   jax.ShapeDtypeStruct((3, N_UP), F32)],
        scratch_shapes=[pltpu.VMEM((8, N_UP), F32)],
        compiler_params=_cp("arbitrary"),
    )(dx2b, u, c, wconv, wdown)


ANY = pl.BlockSpec(memory_space=pl.ANY)


def _place():
    x, y, c = lax.axis_index("x"), lax.axis_index("y"), lax.axis_index("c")
    chips = [(1 - x, y), (x, 1 - y), (1 - x, 1 - y)]
    return x, y, c, chips


def _half(shape, c, axis):
    size = shape[axis] // 2
    cut = pl.ds(pl.multiple_of(c * size, 8 if axis == 0 else 128), size)
    return (cut, slice(None)) if axis == 0 else (slice(None), cut)


def _half_shape(shape, axis):
    return (shape[0] // 2, shape[1]) if axis == 0 else (shape[0], shape[1] // 2)


def _remote(src, dst, send_sems, recv_sems, k, to):
    return pltpu.make_async_remote_copy(src_ref=src, dst_ref=dst, send_sem=send_sems.at[k], recv_sem=recv_sems.at[k],
                                        device_id=to, device_id_type=MESH)


def _all_gather_weights(big, axes, small):
    nb, ns = len(big), len(small)
    n = nb + ns
    n_sem = 6 * nb + 3 * ns

    def body(*refs):
        ins, outs = refs[:n], refs[n:2 * n]
        send_sems, recv_sems = refs[2 * n:]
        x, y, c, chips = _place()
        me = 2 * x + y
        sib = (x, y, 1 - c)
        started = []
        for a in range(nb):
            mine = _half(big[a].shape, c, axes[a])
            for k, ch in enumerate(chips):
                cp = _remote(ins[a].at[mine], outs[a].at[(me,) + mine], send_sems, recv_sems, 6 * a + k,
                             (ch[0], ch[1], c))
                cp.start()
                started.append(cp)
        for a in range(ns):
            for k, ch in enumerate(chips):
                cp = _remote(ins[nb + a], outs[nb + a].at[me], send_sems, recv_sems, 6 * nb + 3 * a + k,
                             (ch[0], ch[1], c))
                cp.start()
                started.append(cp)
        for a in range(nb):
            mine = _half(big[a].shape, c, axes[a])
            for k, ch in enumerate(chips):
                landed = outs[a].at[(2 * ch[0] + ch[1],) + mine]
                _remote(landed, landed, send_sems, recv_sems, 6 * a + k, sib).wait_recv()
                cp = _remote(landed, landed, send_sems, recv_sems, 6 * a + 3 + k, sib)
                cp.start()
                started.append(cp)
        for a in range(nb):
            other = _half(big[a].shape, 1 - c, axes[a])
            for k, ch in enumerate(chips):
                landed = outs[a].at[(2 * ch[0] + ch[1],) + other]
                _remote(landed, landed, send_sems, recv_sems, 6 * a + 3 + k, sib).wait_recv()
        for a in range(ns):
            for k, ch in enumerate(chips):
                landed = outs[nb + a].at[2 * ch[0] + ch[1]]
                _remote(landed, landed, send_sems, recv_sems, 6 * nb + 3 * a + k, sib).wait_recv()
        for cp in started:
            cp.wait_send()

    arrs = list(big) + list(small)
    return pl.pallas_call(
        body, name="all_gather_weights",
        in_specs=[ANY] * n, out_specs=[ANY] * n,
        out_shape=[jax.ShapeDtypeStruct((4,) + a.shape, a.dtype) for a in arrs],
        scratch_shapes=[pltpu.SemaphoreType.DMA((n_sem,)), pltpu.SemaphoreType.DMA((n_sem,))],
        compiler_params=pltpu.CompilerParams(has_side_effects=True),
    )(*arrs)


def _sibling_exchange(grads, axes, smalls, name):
    nb = len(grads)
    n = nb + len(smalls)

    def body(*refs):
        ins, outs = refs[:n], refs[n:2 * n]
        send_sems, recv_sems = refs[2 * n:]
        x, y, c, _ = _place()
        sib = (x, y, 1 - c)
        cps = []
        for a in range(nb):
            theirs = _half(grads[a].shape[1:], 1 - c, axes[a])
            cps.append(_remote(ins[a].at[(slice(None),) + theirs], outs[a], send_sems, recv_sems, a, sib))
        for a in range(nb, n):
            cps.append(_remote(ins[a], outs[a], send_sems, recv_sems, a, sib))
        for cp in cps:
            cp.start()
        for cp in cps:
            cp.wait()

    out_shape = [jax.ShapeDtypeStruct((4,) + _half_shape(g.shape[1:], ax), g.dtype) for g, ax in zip(grads, axes)]
    out_shape += [jax.ShapeDtypeStruct(a.shape, F32) for a in smalls]
    return pl.pallas_call(
        body, name=name, in_specs=[ANY] * n, out_specs=[ANY] * n, out_shape=out_shape,
        scratch_shapes=[pltpu.SemaphoreType.DMA((n,)), pltpu.SemaphoreType.DMA((n,))],
        compiler_params=pltpu.CompilerParams(has_side_effects=True),
    )(*grads, *smalls)


def _gather_share(lands, axes, name):
    n = len(lands)

    def body(*refs):
        outs = refs[n:2 * n]
        send_sems, recv_sems = refs[2 * n:]
        x, y, c, chips = _place()
        sib = (x, y, 1 - c)
        cps = []
        for a in range(n):
            mine = _half(lands[a].shape[1:], c, axes[a])
            for k, ch in enumerate(chips):
                landed = outs[a].at[(2 * ch[0] + ch[1],) + mine]
                cps.append(_remote(landed, landed, send_sems, recv_sems, 3 * a + k, sib))
        for cp in cps:
            cp.start()
        for a in range(n):
            other = _half(lands[a].shape[1:], 1 - c, axes[a])
            for k, ch in enumerate(chips):
                landed = outs[a].at[(2 * ch[0] + ch[1],) + other]
                _remote(landed, landed, send_sems, recv_sems, 3 * a + k, sib).wait_recv()
        for cp in cps:
            cp.wait_send()

    return pl.pallas_call(
        body, name=name, in_specs=[ANY] * n, out_specs=[ANY] * n,
        out_shape=[jax.ShapeDtypeStruct(a.shape, a.dtype) for a in lands],
        input_output_aliases={a: a for a in range(n)},
        scratch_shapes=[pltpu.SemaphoreType.DMA((3 * n,)), pltpu.SemaphoreType.DMA((3 * n,))],
        compiler_params=pltpu.CompilerParams(has_side_effects=True),
    )(*lands)


def _sibling_share(halves, name):
    n = len(halves)

    def body(*refs):
        ins, outs = refs[:n], refs[n:2 * n]
        send_sems, recv_sems = refs[2 * n:]
        x, y, c, _ = _place()
        cps = [_remote(ins[a], outs[a], send_sems, recv_sems, a, (x, y, 1 - c)) for a in range(n)]
        for cp in cps:
            cp.start()
        for cp in cps:
            cp.wait()

    return pl.pallas_call(
        body, name=name, in_specs=[ANY] * n, out_specs=[ANY] * n,
        out_shape=[jax.ShapeDtypeStruct(h.shape, F32) for h in halves],
        scratch_shapes=[pltpu.SemaphoreType.DMA((n,)), pltpu.SemaphoreType.DMA((n,))],
        compiler_params=pltpu.CompilerParams(has_side_effects=True),
    )(*halves)


HBM = pl.BlockSpec(memory_space=pltpu.HBM)
SEM = pl.BlockSpec(memory_space=pltpu.SEMAPHORE)
DATAFLOW = pltpu.SideEffectType.DATAFLOW_SIDE_EFFECTING


def _split_start(name, srcs, land_shapes, plan, n_copies, after):
    lands = [lax.empty(shp, dt) for shp, dt in land_shapes]
    bufs = list(srcs) + lands
    nb, ns = len(bufs), len(srcs)

    def body(*refs):
        send_sems, recv_sems, token = refs[nb + 1], refs[nb + 2], refs[-1]
        for k, (src, dst, to) in enumerate(plan(refs[:ns], refs[ns:nb])):
            _remote(src, dst, send_sems, recv_sems, k, to).start()
        token[...] = jnp.zeros_like(token)

    res = pl.pallas_call(
        body, name=name,
        out_shape=(pltpu.SemaphoreType.DMA((n_copies,)), pltpu.SemaphoreType.DMA((n_copies,)),
                   *[pltpu.HBM(b.shape, b.dtype) for b in bufs], jax.ShapeDtypeStruct((8, 128), F32)),
        in_specs=[HBM] * nb + [ANY],
        out_specs=(SEM, SEM, *[HBM] * nb, pl.BlockSpec(memory_space=pltpu.VMEM)),
        input_output_aliases={i: 2 + i for i in range(nb)},
        compiler_params=pltpu.CompilerParams(has_side_effects=DATAFLOW),
    )(*[pltpu.with_memory_space_constraint(b, pltpu.HBM) for b in bufs], after)
    return (res[0], res[1], list(res[2:2 + nb])), res[-1]


def _split_wait(name, handle, n_srcs, plan, after):
    send_sems, recv_sems, bufs = handle
    nb = len(bufs)

    def body(*refs):
        sends, recvs = refs[nb], refs[nb + 1]
        for k, (src, dst, to) in enumerate(plan(refs[:n_srcs], refs[n_srcs:nb])):
            cp = _remote(src, dst, sends, recvs, k, to)
            cp.wait_send()
            cp.wait_recv()

    res = pl.pallas_call(
        body, name=name, out_shape=[pltpu.HBM(b.shape, b.dtype) for b in bufs],
        in_specs=[HBM] * nb + [SEM, SEM, ANY], out_specs=[HBM] * nb,
        input_output_aliases={i: i for i in range(nb)},
        compiler_params=pltpu.CompilerParams(has_side_effects=DATAFLOW),
    )(*bufs, send_sems, recv_sems, after)
    return list(res[:n_srcs]), list(res[n_srcs:])


def _gather_plan(shapes, axes, n_whole=0):
    def plan(srcs, lands):
        x, y, c, chips = _place()
        out = []
        for a, (shape, axis) in enumerate(zip(shapes, axes)):
            mine = _half(shape, c, axis)
            for ch in chips:
                out.append((srcs[a].at[mine], lands[a].at[(2 * x + y,) + mine], (ch[0], ch[1], c)))
        for a in range(len(shapes), len(shapes) + n_whole):
            for ch in chips:
                out.append((srcs[a], lands[a].at[2 * x + y], (ch[0], ch[1], c)))
        return out
    return plan


def _sibling_plan(shapes, axes):
    def plan(srcs, lands):
        x, y, c, _ = _place()
        return [(srcs[a].at[(slice(None),) + _half(shape, 1 - c, axis)], lands[a], (x, y, 1 - c))
                for a, (shape, axis) in enumerate(zip(shapes, axes))]
    return plan


def _reduce_plan(n_big, n_small):
    def plan(srcs, lands):
        x, y, c, chips = _place()
        out = []
        for a in range(n_big):
            for k, ch in enumerate(chips):
                out.append((srcs[a].at[2 * ch[0] + ch[1]], lands[a].at[k], (ch[0], ch[1], c)))
        for a in range(n_big, n_big + n_small):
            for ch in chips:
                out.append((srcs[a], lands[a].at[2 * x + y], (ch[0], ch[1], c)))
        return out
    return plan


def _row_tile(rows, cols, mult):
    best = mult
    for t in range(mult, rows + 1, mult):
        if rows % t == 0 and t * cols * 4 <= (2 << 20):
            best = t
    return best if rows % best == 0 else rows


COL_TILE = 256


def _half_tiling(hshape, axis, mult):
    hr, hc = hshape
    if axis == 0:
        tr = _row_tile(hr, hc, mult)
        return tr, hc, hr // tr
    return hr, COL_TILE, hc // COL_TILE


def _tile_idx(axis, t):
    return (t, 0) if axis == 0 else (0, t)


def _chip_partial(place, g, t, axis, name):
    hshape = t.shape[1:]
    br, bc, nt = _half_tiling(hshape, axis, 16)

    def body(pl_ref, g_ref, t_ref, pf_ref, pb_ref):
        v = g_ref[...].astype(F32) + t_ref[...].astype(F32)
        pb_ref[...] = v.astype(BF16)

        @pl.when(pl.program_id(1) == pl_ref[0])
        def _():
            pf_ref[...] = v

    blk = (None, br, bc)
    return pl.pallas_call(
        body, name=name,
        grid_spec=pltpu.PrefetchScalarGridSpec(
            num_scalar_prefetch=1, grid=(nt, 4),
            in_specs=[pl.BlockSpec(blk, lambda i, j, p: (j,) + _tile_idx(axis, p[1] * nt + i)),
                      pl.BlockSpec(blk, lambda i, j, p: (j,) + _tile_idx(axis, i))],
            out_specs=[pl.BlockSpec((br, bc), lambda i, j, p: _tile_idx(axis, i)),
                       pl.BlockSpec(blk, lambda i, j, p: (j,) + _tile_idx(axis, i))]),
        out_shape=[jax.ShapeDtypeStruct(hshape, F32), jax.ShapeDtypeStruct((4,) + hshape, BF16)],
        compiler_params=_cp("arbitrary", "arbitrary"),
    )(place, g, t)


def _finish_half(pf, rb, axis, name):
    hshape = pf.shape
    br, bc, nt = _half_tiling(hshape, axis, 16)

    def body(pf_ref, rb_ref, o_ref):
        o_ref[...] = ((pf_ref[...] + rb_ref[0].astype(F32)) + rb_ref[1].astype(F32)) + rb_ref[2].astype(F32)

    return pl.pallas_call(
        body, name=name, grid=(nt,),
        in_specs=[pl.BlockSpec((br, bc), lambda i: _tile_idx(axis, i)),
                  pl.BlockSpec((3, br, bc), lambda i: (0,) + _tile_idx(axis, i))],
        out_specs=pl.BlockSpec((br, bc), lambda i: _tile_idx(axis, i)),
        out_shape=jax.ShapeDtypeStruct(hshape, F32),
        compiler_params=_cp("arbitrary"),
    )(pf, rb)


def _add2(a, b, name):
    def body(a_ref, b_ref, o_ref):
        o_ref[...] = a_ref[...] + b_ref[...]

    return pl.pallas_call(body, name=name, out_shape=jax.ShapeDtypeStruct(a.shape, F32))(a, b)


def _adam_math(w, g, m, v):
    m = ADAM_B1 * m + (1.0 - ADAM_B1) * g
    v = ADAM_B2 * v + (1.0 - ADAM_B2) * (g * g)
    m_hat = m / (1.0 - ADAM_B1 ** ADAM_STEP)
    v_hat = v / (1.0 - ADAM_B2 ** ADAM_STEP)
    return -ADAM_LR * (m_hat / (jnp.sqrt(v_hat) + ADAM_EPS) + ADAM_WD * w), m, v


def _adam_halves(place, w, mine, theirs, m, v, axis, name):
    br, bc, nt = _half_tiling(mine.shape, axis, 8)

    def body(pl_ref, w_ref, a_ref, b_ref, m_ref, v_ref, g_ref, d_ref, mo_ref, vo_ref):
        is_mine = pl.program_id(0) // nt == pl_ref[1]
        g = jnp.where(is_mine, a_ref[...], b_ref[...])
        d, mn, vn = _adam_math(w_ref[...], g, m_ref[...], v_ref[...])
        g_ref[...] = g
        d_ref[...] = d
        mo_ref[...] = mn
        vo_ref[...] = vn

    full = pl.BlockSpec((br, bc), lambda i, p: _tile_idx(axis, i))
    half = pl.BlockSpec((br, bc), lambda i, p: _tile_idx(axis, i % nt))
    return pl.pallas_call(
        body, name=name,
        grid_spec=pltpu.PrefetchScalarGridSpec(
            num_scalar_prefetch=1, grid=(2 * nt,), in_specs=[full, half, half, full, full], out_specs=[full] * 4),
        out_shape=[jax.ShapeDtypeStruct(w.shape, F32)] * 4, compiler_params=_cp("arbitrary"),
    )(place, w, mine, theirs, m, v)


def _add_many(xs, ys, name):
    n = len(xs)

    def body(*refs):
        for i in range(n):
            refs[2 * n + i][...] = refs[i][...] + refs[n + i][...]

    return pl.pallas_call(body, name=name, out_shape=[jax.ShapeDtypeStruct(a.shape, F32) for a in xs])(*xs, *ys)


def _adam_small(place, owns, landed, ws, ms, vs, widths):
    n, nw = len(owns), len(ws)

    def body(pl_ref, *refs):
        own_r, land_r = refs[:n], refs[n:2 * n]
        w_r, m_r, v_r = (refs[2 * n + k * nw:2 * n + (k + 1) * nw] for k in range(3))
        outs = refs[2 * n + 3 * nw:]
        g_o, d_o, m_o, v_o = outs[:n], outs[n:n + nw], outs[n + nw:n + 2 * nw], outs[n + 2 * nw:]
        for me in range(4):
            @pl.when(pl_ref[0] == me)
            def _(me=me):
                for i in range(n):
                    p = [own_r[i][...] if k == me else land_r[i][k] for k in range(4)]
                    g = ((p[0] + p[1]) + p[2]) + p[3]
                    if i < nw and widths[i]:
                        g = g[:, me * widths[i]:(me + 1) * widths[i]]
                    g_o[i][...] = g
                    if i < nw:
                        d, mn, vn = _adam_math(w_r[i][...], g, m_r[i][...], v_r[i][...])
                        d_o[i][...] = d
                        m_o[i][...] = mn
                        v_o[i][...] = vn

    g_shapes = [jax.ShapeDtypeStruct(ws[i].shape if i < nw else owns[i].shape, F32) for i in range(n)]
    w_shapes = [jax.ShapeDtypeStruct(w.shape, F32) for w in ws]
    whole = lambda a: pl.BlockSpec(a.shape, lambda i, p, nd=len(a.shape): (0,) * nd)
    ins = list(owns) + list(landed) + list(ws) + list(ms) + list(vs)
    out_shape = g_shapes + w_shapes * 3
    out = pl.pallas_call(
        body, name="adam_small",
        grid_spec=pltpu.PrefetchScalarGridSpec(num_scalar_prefetch=1, grid=(1,), in_specs=[whole(a) for a in ins],
                                               out_specs=[whole(a) for a in out_shape]),
        out_shape=out_shape, compiler_params=_cp("arbitrary"),
    )(place, *ins)
    return out[:n], out[n:n + nw], out[n + nw:n + 2 * nw], out[n + 2 * nw:]


def kernel(x, g_mix, w_in, b_gate, w_gk_up, b_gk, w_pool_grp, pool_scale, g_gla_head, w_pool_proj, w_gla_proj, w_out, g_ffn, w_up, w_conv, b_conv, w_down, g_final, loss_target, m_g_mix, m_w_in, m_b_gate, m_w_gk_up, m_b_gk, m_w_pool_grp, m_pool_scale, m_g_gla_head, m_w_pool_proj, m_w_gla_proj, m_w_out, m_g_ffn, m_w_up, m_w_conv, m_b_conv, m_w_down, m_g_final, v_g_mix, v_w_in, v_b_gate, v_w_gk_up, v_b_gk, v_w_pool_grp, v_pool_scale, v_g_gla_head, v_w_pool_proj, v_w_gla_proj, v_w_out, v_g_ffn, v_w_up, v_w_conv, v_b_conv, v_w_down, v_g_final):
    s = x.shape[1]
    ts = min(s, 512)
    tm = min(s, 256)
    cx, cy, cc = lax.axis_index("x"), lax.axis_index("y"), lax.axis_index("c")
    chip = 2 * cx + cy
    place = jnp.stack([chip, cc]).astype(jnp.int32)

    big_names = ("w_in", "w_pool_proj", "w_gla_proj", "w_out", "w_up", "w_down")
    axes = (1, 0, 0, 0, 0, 0)
    shards = dict(w_in=jnp.transpose(w_in[0]), w_pool_proj=w_pool_proj[0], w_gla_proj=w_gla_proj[0], w_out=w_out[0],
                  w_up=w_up[0], w_down=w_down[0])
    def fill_own(lands, mine):
        return [lax.dynamic_update_slice(g, o_[None], (chip, 0, 0)) for g, o_ in zip(lands, mine)]

    def gather_start(tag, halves, group_axes, whole, after):
        plan = _gather_plan([o_.shape for o_ in halves], group_axes, len(whole))
        srcs = list(halves) + list(whole)
        handle, token = _split_start("gather_" + tag + "_start", srcs, [((4,) + o_.shape, o_.dtype) for o_ in srcs], plan,
                                     3 * len(srcs), after)
        return (handle, plan, len(halves), len(srcs), group_axes), token

    def gather_finish(tag, started, after):
        handle, plan, n_halves, n, group_axes = started
        mine, lands = _split_wait("gather_" + tag + "_wait", handle, n, plan, after)
        lands[:n_halves] = _gather_share(lands[:n_halves], group_axes, "gather_" + tag + "_share")
        return fill_own(lands, mine)

    in_w, tok = gather_start("in", [shards["w_in"].astype(BF16)], axes[:1], [], g_mix)
    zero = tok[0, 0]
    own = [(shards[n] + zero).astype(BF16) for n in big_names[1:]]
    mix_w, tok = gather_start("mix", own[0:3], axes[1:4], [w_gk_up[0] + zero, w_conv[0] + zero], tok)
    ffn_w, tok = gather_start("ffn", own[3:5], axes[4:6], [], tok)
    xs, tgt = x[0], loss_target[0]
    wgrp = w_pool_grp[0]
    h = _rmsnorm(xs, g_mix + tok[0:1, 0:1], "norm_mix", ts)
    w_in_t = gather_finish("in", in_w, h)[0].reshape(N_IN, D)
    w_rt = jnp.concatenate([w_in_t[3600:], w_in_t[1536:3584], w_in_t[0:1536], w_in_t[3584:3600],
                            jnp.zeros((128 - GATE_RANK, D), BF16)], axis=0)
    nsh = N_IN // 4

    zr = _matmul_resident(h, w_rt, "in_proj", 1152, transposed=True)
    p, pp = _pool_fwd(zr, wgrp, pool_scale)
    wpp, wgla, wout, wgk4, wconv4 = gather_finish("mix", mix_w, pp)
    wgla, wout = wgla.reshape(D, D), wout.reshape(D, D)
    wgk_full = jnp.transpose(wgk4, (1, 0, 2)).reshape(GATE_RANK, 512)
    wconv_full = jnp.transpose(wconv4, (1, 0, 2)).reshape(3, N_UP)
    wgk_pad = jnp.concatenate([wgk_full, jnp.zeros((128 - GATE_RANK, 512), F32)], axis=0)
    o, og, sp = _gla_fwd(zr, wgk_pad, b_gk, g_gla_head, ts)
    x1, mixed, yp, yg = _merge_fwd(xs, zr, pp, og, b_gate, wpp, wgla, wout, ts)
    wup, wdown = gather_finish("ffn", ffn_w, x1)
    wdown = wdown.reshape(D_FF, D)
    h2 = _rmsnorm(x1, g_ffn, "norm_ffn", ts)
    u = _matmul_resident(h2, wup, "ffn_up", None)
    a, conv_out, dx2, dx2b, loss_part, dgfin = _ffn_down_loss(u, x1, tgt, wconv_full, b_conv, wdown,
                                                              g_final.reshape(1, D), tm)

    du, dbconv, dwconv = _ffn_bwd(dx2b, u, conv_out, wconv_full, wdown, tm)
    dw_down = _matmul_tn(a, dx2b, "dw_down", D, tm=1408)
    dw_up = _matmul_tn(h2, du, "dw_up", 1408, shard_major=True)

    def exchange_start(tag, grads, group_axes, after):
        plan = _sibling_plan([g.shape[1:] for g in grads], group_axes)
        lands = [((4,) + _half_shape(g.shape[1:], ax), g.dtype) for g, ax in zip(grads, group_axes)]
        handle, token = _split_start("sibling_" + tag + "_start", grads, lands, plan, len(grads), after)
        return (handle, plan, len(grads)), token

    def partials(tag, names, group_axes, exchange, after):
        handle, plan, n = exchange
        mine, theirs = _split_wait("sibling_" + tag + "_wait", handle, n, plan, after)
        return zip(*[_chip_partial(place, g, t, ax, "chip_partial_" + nm)
                     for nm, ax, g, t in zip(names, group_axes, mine, theirs)])

    ffn_names, ffn_axes = ("w_up", "w_down"), (0, 0)
    ffn_x, token = exchange_start("ffn", [dw_up, dw_down.reshape(4, 704, D)], ffn_axes, du)
    dx1, dx1b, dgffn = _matmul_nt_normbwd(du, wup, x1, g_ffn + token[0:1, 0:1], dx2, "ffn_up_bwd", ts)
    ffn_pf, ffn_pb = partials("ffn", ffn_names, ffn_axes, ffn_x, dx1b)
    ffn_plan = _reduce_plan(2, False)
    ffn_handle, token = _split_start("reduce_ffn_start", ffn_pb, [((3,) + p.shape[1:], BF16) for p in ffn_pb],
                                     ffn_plan, 6, ffn_pf[0])

    dzg, dyp, dyg, dpp, do, dzog, dbgate, dghead = _merge_bwd(dx1b, zr, yp, yg, o, b_gate + token[0:1, 0:1], g_gla_head,
                                                             wpp, wgla, wout, ts)
    dw_out = _matmul_tn(mixed, dx1b, "dw_out", D)
    dw_gla = _matmul_tn(og, dyg, "dw_gla", D)
    dw_pp = _matmul_tn(pp, dyp, "dw_pp", 256, shard_major=True)

    out_names, out_axes = ("w_pool_proj", "w_gla_proj", "w_out"), (0, 0, 0)
    out_x, token = exchange_start("out", [dw_pp, dw_gla.reshape(4, 256, D), dw_out.reshape(4, 256, D)], out_axes, dpp)
    dzp, dwgrp, dscale = _pool_bwd(p, dpp, wgrp, pool_scale + token[0:1, 0:1])
    out_pf, out_pb = partials("out", out_names, out_axes, out_x, dzp)
    out_plan = _reduce_plan(3, False)
    out_handle, token = _split_start("reduce_out_start", out_pb, [((3,) + p_.shape[1:], BF16) for p_ in out_pb],
                                     out_plan, 9, out_pf[0])
    dq, dk, dv, dgpre = _gla_bwd(zr, do, sp, wgk_pad, b_gk + token[0:1, 0:1], ts)
    dzgk, dwgk, dbgk = _gk_bwd(dgpre, zr, wgk_pad, ts)
    dzr = jnp.concatenate([dzg, dv, dzog, dzp, dq, dk, dzgk], axis=1)
    dw_rt = _matmul_tn(dzr, h, "dw_in", D, tm=1152)

    def grad_rows(lo, hi):
        out = []
        for seg_lo, seg_hi, at in ((0, 1536, OFF_POOL), (1536, 3584, OFF_V), (3584, 3600, OFF_GK), (3600, N_IN, OFF_GATE)):
            a_, b_ = max(lo, seg_lo), min(hi, seg_hi)
            if a_ < b_:
                out.append(dw_rt[at + a_ - seg_lo:at + b_ - seg_lo])
        return jnp.concatenate(out, axis=0)

    dw_in_t = jnp.stack([grad_rows(j * nsh, (j + 1) * nsh) for j in range(4)])

    in_sib = _sibling_exchange([dw_in_t], (1,), [], "sibling_exchange_in")
    in_pf, in_pb = _chip_partial(place, dw_in_t, in_sib[0], 1, "chip_partial_w_in")
    in_plan = _reduce_plan(1, 0)
    in_handle, token = _split_start("reduce_in_start", [in_pb], [((3,) + in_pb.shape[1:], BF16)], in_plan, 3, in_pf)
    grad_x, _, dgmix = _matmul_nt_normbwd(dzr, w_rt, xs, g_mix + token[0:1, 0:1], dx1, "in_proj_bwd", ts, transposed=True)
    small_names = ("g_mix", "b_gate", "w_gk_up", "b_gk", "w_pool_grp", "pool_scale", "g_gla_head", "g_ffn", "w_conv",
                   "b_conv", "g_final")
    small_mine = [dgmix, dbgate, dwgk[:GATE_RANK], dbgk, dwgrp.reshape(4 * 128, 128), dscale, dghead, dgffn, dwconv, dbconv,
                  dgfin, loss_part]
    small_sib = _sibling_exchange([], (), small_mine, "sibling_exchange_small")
    small_chip = _add_many(small_mine, small_sib, "chip_partial_small")
    small_plan = _reduce_plan(0, len(small_chip))
    small_handle, token = _split_start("reduce_small_start", small_chip, [((4,) + a_.shape, F32) for a_ in small_chip],
                                       small_plan, 3 * len(small_chip), small_mine[0])

    ms = dict(w_in=jnp.transpose(m_w_in[0]), w_pool_proj=m_w_pool_proj[0], w_gla_proj=m_w_gla_proj[0], w_out=m_w_out[0],
              w_up=m_w_up[0], w_down=m_w_down[0])
    vs = dict(w_in=jnp.transpose(v_w_in[0]), w_pool_proj=v_w_pool_proj[0], w_gla_proj=v_w_gla_proj[0], w_out=v_w_out[0],
              w_up=v_w_up[0], w_down=v_w_down[0])
    grad, delta, new_m, new_v = {}, {}, {}, {}

    def finish_and_update(names, group_axes, part_f, landed, tag):
        halves = [_finish_half(pf, rb, ax, "finish_" + n) for n, ax, pf, rb in zip(names, group_axes, part_f, landed)]
        sib_halves = _sibling_share(halves, "sibling_share_" + tag)
        for n, ax, mine, theirs in zip(names, group_axes, halves, sib_halves):
            res = _adam_halves(place, shards[n], mine, theirs, ms[n], vs[n], ax, "adam_" + n)
            if n == "w_in":
                res = [jnp.transpose(r_) for r_ in res]
            grad[n], delta[n], new_m[n], new_v[n] = [r_[None] for r_ in res]

    _, ffn_landed = _split_wait("reduce_ffn_wait", ffn_handle, 2, ffn_plan, token)
    _, out_landed = _split_wait("reduce_out_wait", out_handle, 3, out_plan, ffn_landed[0])
    finish_and_update(ffn_names + out_names, ffn_axes + out_axes, ffn_pf + out_pf, ffn_landed + out_landed, "rest")
    _, in_landed = _split_wait("reduce_in_wait", in_handle, 1, in_plan, delta["w_out"])
    finish_and_update(("w_in",), (1,), (in_pf,), in_landed, "in")
    small_sent, small_landed = _split_wait("reduce_small_wait", small_handle, len(small_chip), small_plan, delta["w_in"])
    given = dict(g_mix=(g_mix, m_g_mix, v_g_mix), b_gate=(b_gate, m_b_gate, v_b_gate), w_gk_up=(w_gk_up, m_w_gk_up, v_w_gk_up),
                 b_gk=(b_gk, m_b_gk, v_b_gk), w_pool_grp=(w_pool_grp, m_w_pool_grp, v_w_pool_grp),
                 pool_scale=(pool_scale, m_pool_scale, v_pool_scale), g_gla_head=(g_gla_head, m_g_gla_head, v_g_gla_head),
                 g_ffn=(g_ffn, m_g_ffn, v_g_ffn), w_conv=(w_conv, m_w_conv, v_w_conv), b_conv=(b_conv, m_b_conv, v_b_conv),
                 g_final=(g_final, m_g_final, v_g_final))
    flat2 = lambda a: a.reshape(-1, a.shape[-1])
    widths = [dict(w_gk_up=128, w_conv=1408).get(n) for n in small_names]
    totals, ds, mo, vo = _adam_small(place, small_sent, small_landed, *[[flat2(given[n][k]) for n in small_names] for k in range(3)],
                                     widths)
    loss = totals[-1][0, 0]
    for i, n in enumerate(small_names):
        shp = given[n][0].shape
        grad[n], delta[n], new_m[n], new_v[n] = [r_.reshape(shp) for r_ in (totals[i], ds[i], mo[i], vo[i])]

    order = ("g_mix", "w_in", "b_gate", "w_gk_up", "b_gk", "w_pool_grp", "pool_scale", "g_gla_head", "w_pool_proj",
             "w_gla_proj", "w_out", "g_ffn", "w_up", "w_conv", "b_conv", "w_down", "g_final")
    return (loss, grad_x[None], *[grad[n] for n in order], *[delta[n] for n in order], *[new_m[n] for n in order],
            *[new_v[n] for n in order])
```

```python
import functools

import jax
import jax.numpy as jnp
from jax import lax
from jax.experimental import pallas as pl
from jax.experimental.pallas import tpu as pltpu

F32 = jnp.float32
BF16 = jnp.bfloat16
MESH = pl.DeviceIdType.MESH

D = 1024
EPS = 1e-6
CHUNK = 64
POOL_W = 512
POOL_WINDOWS = (2, 4, 8, 16)
HEADS = 4
HK = 128
HV = 256
GATE_RANK = 16
D_FF = 2816
N_UP = 2 * D_FF
N_IN = 5648
QSCALE = HK ** -0.5
N_INR = 5760
OFF_GATE, OFF_V, OFF_OG, OFF_POOL, OFF_Q, OFF_K, OFF_GK = 0, 2048, 3072, 4096, 4608, 5120, 5632

ADAM_LR, ADAM_B1, ADAM_B2, ADAM_EPS, ADAM_WD, ADAM_STEP = 0.001, 0.9, 0.999, 1e-08, 0.01, 10

VMEM_LIMIT = 56 * 1024 * 1024


def _cp(*sem):
    return pltpu.CompilerParams(dimension_semantics=sem if sem else None, vmem_limit_bytes=VMEM_LIMIT)


def _dot(a, b):
    return jnp.dot(a, b, preferred_element_type=F32)


def _dot_nt(a, b):
    return lax.dot_general(a, b, (((1,), (1,)), ((), ())), preferred_element_type=F32)


def _dot_tn(a, b):
    return lax.dot_general(a, b, (((0,), (0,)), ((), ())), preferred_element_type=F32)


def _sigmoid(v):
    return 1.0 / (1.0 + jnp.exp(-v))


def _rows(shape):
    return lax.broadcasted_iota(jnp.int32, shape, 0)


def _pick_row(v, r):
    return jnp.sum(jnp.where(_rows(v.shape) == r, v, 0.0), axis=0, keepdims=True)


def _rmsnorm(x, g, name, ts):
    s = x.shape[0]

    def body(x_ref, g_ref, h_ref):
        xv = x_ref[...]
        r = lax.rsqrt(jnp.mean(xv * xv, axis=-1, keepdims=True) + EPS)
        h_ref[...] = (xv * r * g_ref[...]).astype(BF16)

    return pl.pallas_call(
        body, name=name, grid=(s // ts,),
        in_specs=[pl.BlockSpec((ts, D), lambda i: (i, 0)), pl.BlockSpec((1, D), lambda i: (0, 0))],
        out_specs=pl.BlockSpec((ts, D), lambda i: (i, 0)), out_shape=jax.ShapeDtypeStruct((s, D), BF16),
        compiler_params=_cp("arbitrary"),
    )(x, g)


MM_ROWS = 512


def _matmul_resident(h, w, name, tn, transposed=False):
    s = h.shape[0]
    if transposed:
        nj = w.shape[0] // tn
        w_spec = pl.BlockSpec((tn, D), lambda j: (j, 0))
    elif w.ndim == 3:
        nj, tn = w.shape[0], w.shape[2]
        w_spec = pl.BlockSpec((None, D, tn), lambda j: (j, 0, 0))
    else:
        nj = w.shape[1] // tn
        w_spec = pl.BlockSpec((D, tn), lambda j: (0, j))
    mm = _dot_nt if transposed else _dot
    rc = min(s, MM_ROWS)

    def body(h_ref, w_ref, z_ref):
        for r0 in range(0, s, rc):
            z_ref[r0:r0 + rc, :] = mm(h_ref[r0:r0 + rc, :], w_ref[...]).astype(BF16)

    return pl.pallas_call(
        body, name=name, grid=(nj,),
        in_specs=[pl.BlockSpec((s, D), lambda j: (0, 0)), w_spec],
        out_specs=pl.BlockSpec((s, tn), lambda j: (0, j)), out_shape=jax.ShapeDtypeStruct((s, nj * tn), BF16),
        compiler_params=_cp("arbitrary"),
    )(h, w)


def _matmul_nt_normbwd(dz, w, x, g, resid, name, ts, transposed=False):
    s = x.shape[0]

    def body(dz_ref, w_hbm, x_ref, g_ref, r_ref, o_ref, ob_ref, dg_ref, w_ref, sem):
        @pl.when(pl.program_id(0) == 0)
        def _():
            cp = pltpu.make_async_copy(w_hbm, w_ref, sem)
            cp.start()
            cp.wait()
            dg_ref[...] = jnp.zeros_like(dg_ref)

        if transposed:
            dh = _dot(dz_ref[...], w_ref[...])
        else:
            kc = w.shape[2]
            dh = _dot_nt(dz_ref[:, 0:kc], w_ref[0])
            for j in range(1, w.shape[0]):
                dh = dh + _dot_nt(dz_ref[:, j * kc:(j + 1) * kc], w_ref[j])
        xv = x_ref[...]
        r = lax.rsqrt(jnp.mean(xv * xv, axis=-1, keepdims=True) + EPS)
        xh = xv * r
        dg_ref[...] += jnp.sum(dh * xh, axis=0, keepdims=True)
        dxh = dh * g_ref[...]
        out = r_ref[...] + r * (dxh - xh * jnp.mean(dxh * xh, axis=-1, keepdims=True))
        o_ref[...] = out
        ob_ref[...] = out.astype(BF16)

    row = lambda i: (i, 0)
    kdim = dz.shape[1]
    return pl.pallas_call(
        body, name=name, grid=(s // ts,),
        in_specs=[pl.BlockSpec((ts, kdim), row), ANY, pl.BlockSpec((ts, D), row),
                  pl.BlockSpec((1, D), lambda i: (0, 0)), pl.BlockSpec((ts, D), row)],
        out_specs=[pl.BlockSpec((ts, D), row), pl.BlockSpec((ts, D), row), pl.BlockSpec((1, D), lambda i: (0, 0))],
        out_shape=[jax.ShapeDtypeStruct((s, D), F32), jax.ShapeDtypeStruct((s, D), BF16),
                   jax.ShapeDtypeStruct((1, D), F32)],
        scratch_shapes=[pltpu.VMEM(w.shape, BF16), pltpu.SemaphoreType.DMA],
        compiler_params=_cp("arbitrary"),
    )(dz, w, x, g, resid)


def _matmul_tn(a, b, name, tn, shard_major=False, tm=None):
    s, m = a.shape
    n = b.shape[1]
    tm = m if tm is None else tm
    ni, nj = m // tm, n // tn

    def body(a_ref, b_ref, o_ref):
        o_ref[...] = _dot_tn(a_ref[...], b_ref[...]).astype(BF16)

    if shard_major:
        out_spec = pl.BlockSpec((None, tm, tn), lambda i, j: (j, i, 0))
        out_shape = jax.ShapeDtypeStruct((nj, m, tn), BF16)
    else:
        out_spec = pl.BlockSpec((tm, tn), lambda i, j: (i, j))
        out_shape = jax.ShapeDtypeStruct((m, n), BF16)
    return pl.pallas_call(
        body, name=name, grid=(ni, nj),
        in_specs=[pl.BlockSpec((s, tm), lambda i, j: (0, i)), pl.BlockSpec((s, tn), lambda i, j: (0, j))],
        out_specs=out_spec, out_shape=out_shape,
        compiler_params=_cp("arbitrary", "arbitrary"),
    )(a, b)


def _pool_fwd(zr, wgrp, scale):
    s = zr.shape[0]

    def body(u_ref, w_ref, sc_ref, p_ref, pp_ref):
        row = _rows((s, 128))
        for gi, win in enumerate(POOL_WINDOWS):
            cs = slice(gi * 128, (gi + 1) * 128)
            u = u_ref[:, cs].astype(F32)
            acc, k = u, 1
            while k < win:
                acc = acc + jnp.where(row >= k, pltpu.roll(acc, k, 0), 0.0)
                k *= 2
            cnt = jnp.minimum(row + 1, win).astype(F32)
            p = (acc / cnt - u).astype(BF16)
            p_ref[:, cs] = p
            pp_ref[:, cs] = (_dot(p, w_ref[gi].astype(BF16)) * sc_ref[:, cs]).astype(BF16)

    return pl.pallas_call(
        body, name="pool_fwd", grid=(1,),
        in_specs=[pl.BlockSpec((s, POOL_W), lambda i: (0, OFF_POOL // POOL_W)),
                  pl.BlockSpec((4, 128, 128), lambda i: (0, 0, 0)), pl.BlockSpec((1, POOL_W), lambda i: (0, 0))],
        out_specs=[pl.BlockSpec((s, POOL_W), lambda i: (0, 0))] * 2,
        out_shape=[jax.ShapeDtypeStruct((s, POOL_W), BF16)] * 2,
        compiler_params=_cp("arbitrary"),
    )(zr, wgrp, scale)


def _pool_bwd(p, dpp, wgrp, scale):
    s = p.shape[0]

    def body(p_ref, dpp_ref, w_ref, sc_ref, dz_ref, dw_ref, dsc_ref):
        row = _rows((s, 128))
        for gi, win in enumerate(POOL_WINDOWS):
            cs = slice(gi * 128, (gi + 1) * 128)
            pv = p_ref[:, cs]
            wb = w_ref[gi].astype(BF16)
            dpp_v = dpp_ref[:, cs].astype(F32)
            dsc_ref[:, cs] = jnp.sum(dpp_v * _dot(pv, wb), axis=0, keepdims=True)
            dpm = (dpp_v * sc_ref[:, cs]).astype(BF16)
            dw_ref[gi] = _dot_tn(pv, dpm)
            dp = _dot_nt(dpm, wb)
            cnt = jnp.minimum(row + 1, win).astype(F32)
            acc, k = dp / cnt, 1
            while k < win:
                acc = acc + jnp.where(row < s - k, pltpu.roll(acc, s - k, 0), 0.0)
                k *= 2
            dz_ref[:, cs] = (acc - dp).astype(BF16)

    full = lambda i: (0, 0)
    return pl.pallas_call(
        body, name="pool_bwd", grid=(1,),
        in_specs=[pl.BlockSpec((s, POOL_W), full), pl.BlockSpec((s, POOL_W), full),
                  pl.BlockSpec((4, 128, 128), lambda i: (0, 0, 0)), pl.BlockSpec((1, POOL_W), full)],
        out_specs=[pl.BlockSpec((s, POOL_W), full), pl.BlockSpec((4, 128, 128), lambda i: (0, 0, 0)),
                   pl.BlockSpec((1, POOL_W), full)],
        out_shape=[jax.ShapeDtypeStruct((s, POOL_W), BF16), jax.ShapeDtypeStruct((4, 128, 128), F32),
                   jax.ShapeDtypeStruct((1, POOL_W), F32)],
        compiler_params=_cp("arbitrary"),
    )(p, dpp, wgrp, scale)


def _gla_decay(zgk_ref, wgk_ref, bgk_ref, rb):
    g = _dot(zgk_ref[...], wgk_ref[...].astype(BF16)) + bgk_ref[...]
    la = (jnp.minimum(g, 0.0) - jnp.log(1.0 + jnp.exp(-jnp.abs(g)))) * (1.0 / 16.0)
    rowm = _rows(la.shape) & (CHUNK - 1)
    bc, k = la, 1
    while k < CHUNK:
        bc = bc + jnp.where(rowm >= k, pltpu.roll(bc, k, 0), 0.0)
        k *= 2
    return g, jnp.exp(bc), jnp.exp(-bc)


GLA_HB = 4


def _gla_specs(rb, rmap):
    wk, wv = GLA_HB * HK, GLA_HB * HV
    return [pl.BlockSpec((rb, wk), lambda h, r: (rmap(h, r), OFF_Q // wk + h)),
            pl.BlockSpec((rb, wk), lambda h, r: (rmap(h, r), OFF_K // wk + h)),
            pl.BlockSpec((rb, wv), lambda h, r: (rmap(h, r), OFF_V // wv + h)),
            pl.BlockSpec((rb, 128), lambda h, r: (rmap(h, r), OFF_GK // 128))]


def _gla_fwd(zr, wgk, bgk, ghead, rb):
    s = zr.shape[0]
    nc = rb // CHUNK
    wk, wv = GLA_HB * HK, GLA_HB * HV

    def body(q_ref, k_ref, v_ref, zgk_ref, zog_ref, wgk_ref, bgk_ref, gh_ref, o_ref, og_ref, sp_ref, st_ref):
        @pl.when(pl.program_id(1) == 0)
        def _():
            st_ref[...] = jnp.zeros_like(st_ref)

        _, e_pos, e_neg = _gla_decay(zgk_ref, wgk_ref, bgk_ref, rb)
        lower = _rows((CHUNK, CHUNK)) >= lax.broadcasted_iota(jnp.int32, (CHUNK, CHUNK), 1)
        for c in range(nc):
            sl = slice(c * CHUNK, (c + 1) * CHUNK)
            for hh in range(GLA_HB):
                ck, cv = slice(hh * HK, (hh + 1) * HK), slice(hh * HV, (hh + 1) * HV)
                q = q_ref[sl, ck].astype(F32) * QSCALE
                k = k_ref[sl, ck].astype(F32)
                v = v_ref[sl, cv]
                ec, fc = e_pos[sl, ck], e_neg[sl, ck]
                qfw = (q * ec).astype(BF16)
                kfw_f = k * fc
                s_fw = _dot_nt(qfw, kfw_f.astype(BF16))
                s_bw = _dot_nt((q * fc).astype(BF16), (k * ec).astype(BF16))
                pm = jnp.where(lower, s_fw, s_bw).astype(BF16)
                st = st_ref[hh]
                stb = st.astype(BF16)
                sp_ref[c, hh] = stb
                o = _dot(pm, v) + _dot_nt(qfw, stb)
                e_last = _pick_row(ec, CHUNK - 1)
                kdec = (kfw_f * e_last).astype(BF16)
                st_ref[hh] = st * e_last + _dot_tn(v, kdec)
                r = lax.rsqrt(jnp.mean(o * o, axis=-1, keepdims=True) + EPS)
                zo = zog_ref[sl, cv].astype(F32)
                o_ref[sl, cv] = o.astype(BF16)
                og_ref[sl, cv] = (o * r * gh_ref[...] * zo * _sigmoid(zo)).astype(BF16)

    rmap = lambda h, r: r
    return pl.pallas_call(
        body, name="gla_fwd", grid=(HEADS // GLA_HB, s // rb),
        in_specs=_gla_specs(rb, rmap) + [
            pl.BlockSpec((rb, wv), lambda h, r: (r, OFF_OG // wv + h)),
            pl.BlockSpec((128, wk), lambda h, r: (0, h)), pl.BlockSpec((1, wk), lambda h, r: (0, h)),
            pl.BlockSpec((1, HV), lambda h, r: (0, 0))],
        out_specs=[pl.BlockSpec((rb, wv), lambda h, r: (r, h)), pl.BlockSpec((rb, wv), lambda h, r: (r, h)),
                   pl.BlockSpec((nc, GLA_HB, HV, HK), lambda h, r: (r, h, 0, 0))],
        out_shape=[jax.ShapeDtypeStruct((s, D), BF16), jax.ShapeDtypeStruct((s, D), BF16),
                   jax.ShapeDtypeStruct((s // CHUNK, HEADS, HV, HK), BF16)],
        scratch_shapes=[pltpu.VMEM((GLA_HB, HV, HK), F32)],
        compiler_params=_cp("arbitrary", "arbitrary"),
    )(zr, zr, zr, zr, zr, wgk, bgk, ghead)


def _gla_bwd(zr, do, sp, wgk, bgk, rb):
    s = zr.shape[0]
    nc = rb // CHUNK
    nr = s // rb
    wk, wv = GLA_HB * HK, GLA_HB * HV

    def body(q_ref, k_ref, v_ref, zgk_ref, do_ref, sp_ref, wgk_ref, bgk_ref, dq_ref, dk_ref, dv_ref, dg_ref,
             gt_ref, dbc_ref):
        @pl.when(pl.program_id(1) == 0)
        def _():
            gt_ref[...] = jnp.zeros_like(gt_ref)

        g, e_pos, e_neg = _gla_decay(zgk_ref, wgk_ref, bgk_ref, rb)
        lower = _rows((CHUNK, CHUNK)) >= lax.broadcasted_iota(jnp.int32, (CHUNK, CHUNK), 1)
        is_last = _rows((CHUNK, HK)) == CHUNK - 1
        for c in reversed(range(nc)):
            sl = slice(c * CHUNK, (c + 1) * CHUNK)
            for hh in range(GLA_HB):
                ck, cv = slice(hh * HK, (hh + 1) * HK), slice(hh * HV, (hh + 1) * HV)
                q = q_ref[sl, ck].astype(F32) * QSCALE
                k = k_ref[sl, ck].astype(F32)
                v = v_ref[sl, cv]
                dov = do_ref[sl, cv]
                ec, fc = e_pos[sl, ck], e_neg[sl, ck]
                qfw_f, kfw_f, qbw_f, kbw_f = q * ec, k * fc, q * fc, k * ec
                qfw, kfw, qbw, kbw = qfw_f.astype(BF16), kfw_f.astype(BF16), qbw_f.astype(BF16), kbw_f.astype(BF16)
                pm = jnp.where(lower, _dot_nt(qfw, kfw), _dot_nt(qbw, kbw)).astype(BF16)
                e_last = _pick_row(ec, CHUNK - 1)
                kdec = (kfw_f * e_last).astype(BF16)
                gt = gt_ref[hh]
                gtb = gt.astype(BF16)
                spv = sp_ref[c, hh]
                dp = _dot_nt(dov, v)
                dv_ref[sl, cv] = (_dot_tn(pm, dov) + _dot_nt(kdec, gtb)).astype(BF16)
                ds_fw = jnp.where(lower, dp, 0.0).astype(BF16)
                ds_bw = jnp.where(lower, 0.0, dp).astype(BF16)
                dqfw = _dot(ds_fw, kfw) + _dot(dov, spv)
                dkfw = _dot_tn(ds_fw, qfw)
                dqbw = _dot(ds_bw, kbw)
                dkbw = _dot_tn(ds_bw, qbw)
                dkdec = _dot(v, gtb)
                de_last = (jnp.sum(gt * spv.astype(F32), axis=0, keepdims=True)
                           + jnp.sum(dkdec * kfw_f, axis=0, keepdims=True))
                dkfw = dkfw + dkdec * e_last
                dq_ref[sl, ck] = ((dqfw * ec + dqbw * fc) * QSCALE).astype(BF16)
                dk_ref[sl, ck] = (dkfw * fc + dkbw * ec).astype(BF16)
                dbc = dqfw * qfw_f - dqbw * qbw_f + dkbw * kbw_f - dkfw * kfw_f
                dbc_ref[sl, ck] = dbc + jnp.where(is_last, de_last * e_last, 0.0)
                gt_ref[hh] = _dot_tn(dov, qfw) + gt * e_last
        rowm = _rows((rb, wk)) & (CHUNK - 1)
        dla, kk = dbc_ref[...], 1
        while kk < CHUNK:
            dla = dla + jnp.where(rowm < CHUNK - kk, pltpu.roll(dla, rb - kk, 0), 0.0)
            kk *= 2
        dg_ref[...] = dla * (1.0 / 16.0) * _sigmoid(-g)

    rmap = lambda h, r: nr - 1 - r
    rev = lambda h, r: (nr - 1 - r, h)
    return pl.pallas_call(
        body, name="gla_bwd", grid=(HEADS // GLA_HB, nr),
        in_specs=_gla_specs(rb, rmap) + [
            pl.BlockSpec((rb, wv), rev),
            pl.BlockSpec((nc, GLA_HB, HV, HK), lambda h, r: (nr - 1 - r, h, 0, 0)),
            pl.BlockSpec((128, wk), lambda h, r: (0, h)), pl.BlockSpec((1, wk), lambda h, r: (0, h))],
        out_specs=[pl.BlockSpec((rb, wk), rev), pl.BlockSpec((rb, wk), rev), pl.BlockSpec((rb, wv), rev),
                   pl.BlockSpec((rb, wk), rev)],
        out_shape=[jax.ShapeDtypeStruct((s, HEADS * HK), BF16), jax.ShapeDtypeStruct((s, HEADS * HK), BF16),
                   jax.ShapeDtypeStruct((s, D), BF16), jax.ShapeDtypeStruct((s, HEADS * HK), F32)],
        scratch_shapes=[pltpu.VMEM((GLA_HB, HV, HK), F32), pltpu.VMEM((rb, wk), F32)],
        compiler_params=_cp("arbitrary", "arbitrary"),
    )(zr, zr, zr, zr, do, sp, wgk, bgk)


def _gk_bwd(dgpre, zr, wgk, ts):
    s = zr.shape[0]

    def body(dg_ref, zgk_ref, w_ref, dz_ref, dw_ref, db_ref):
        @pl.when(pl.program_id(0) == 0)
        def _():
            dw_ref[...] = jnp.zeros_like(dw_ref)
            db_ref[...] = jnp.zeros_like(db_ref)

        dg = dg_ref[...]
        dgb = dg.astype(BF16)
        dz_ref[...] = _dot_nt(dgb, w_ref[...].astype(BF16)).astype(BF16)
        dw_ref[...] += _dot_tn(zgk_ref[...], dgb)
        db_ref[...] += jnp.sum(dg, axis=0, keepdims=True)

    return pl.pallas_call(
        body, name="gk_bwd", grid=(s // ts,),
        in_specs=[pl.BlockSpec((ts, 512), lambda i: (i, 0)), pl.BlockSpec((ts, 128), lambda i: (i, OFF_GK // 128)),
                  pl.BlockSpec((128, 512), lambda i: (0, 0))],
        out_specs=[pl.BlockSpec((ts, 128), lambda i: (i, 0)), pl.BlockSpec((128, 512), lambda i: (0, 0)),
                   pl.BlockSpec((1, 512), lambda i: (0, 0))],
        out_shape=[jax.ShapeDtypeStruct((s, 128), BF16), jax.ShapeDtypeStruct((128, 512), F32),
                   jax.ShapeDtypeStruct((1, 512), F32)],
        compiler_params=_cp("arbitrary"),
    )(dgpre, zr, wgk)


def _merge_fwd(x, zr, pp, og, bgate, wpp, wgla, wout, ts):
    s = x.shape[0]

    def body(x_ref, z0_ref, z1_ref, pp_ref, og_ref, bg_ref, wpp_ref, wgla_ref, wout_ref,
             x1_ref, mix_ref, yp_ref, yg_ref):
        ppv = pp_ref[...]
        yp = jnp.concatenate([_dot(ppv, wpp_ref[j]) for j in range(4)], axis=1)
        yg = _dot(og_ref[...], wgla_ref[...])
        g0 = _sigmoid(z0_ref[...].astype(F32) + bg_ref[:, :D])
        g1 = _sigmoid(z1_ref[...].astype(F32) + bg_ref[:, D:])
        mixed = (g0 * yp + g1 * yg).astype(BF16)
        x1_ref[...] = x_ref[...] + _dot(mixed, wout_ref[...])
        mix_ref[...] = mixed
        yp_ref[...] = yp.astype(BF16)
        yg_ref[...] = yg.astype(BF16)

    row = lambda i: (i, 0)
    const2 = lambda i: (0, 0)
    return pl.pallas_call(
        body, name="merge_fwd", grid=(s // ts,),
        in_specs=[pl.BlockSpec((ts, D), row), pl.BlockSpec((ts, D), lambda i: (i, 0)), pl.BlockSpec((ts, D), lambda i: (i, 1)),
                  pl.BlockSpec((ts, POOL_W), row), pl.BlockSpec((ts, D), row), pl.BlockSpec((1, 2 * D), const2),
                  pl.BlockSpec((4, POOL_W, 256), lambda i: (0, 0, 0)), pl.BlockSpec((D, D), const2),
                  pl.BlockSpec((D, D), const2)],
        out_specs=[pl.BlockSpec((ts, D), row)] * 4,
        out_shape=[jax.ShapeDtypeStruct((s, D), F32)] + [jax.ShapeDtypeStruct((s, D), BF16)] * 3,
        compiler_params=_cp("arbitrary"),
    )(x, zr, zr, pp, og, bgate, wpp, wgla, wout)


def _merge_bwd(dx1b, zr, yp, yg, o, bgate, ghead, wpp, wgla, wout, ts):
    s = dx1b.shape[0]

    def body(dx_ref, z0_ref, z1_ref, zog_ref, yp_ref, yg_ref, o_ref, bg_ref, gh_ref, wpp_ref, wgla_ref, wout_ref,
             dzg_ref, dyp_ref, dyg_ref, dpp_ref, do_ref, dzog_ref, dbg_ref, dgh_ref):
        @pl.when(pl.program_id(0) == 0)
        def _():
            dbg_ref[...] = jnp.zeros_like(dbg_ref)
            dgh_ref[...] = jnp.zeros_like(dgh_ref)

        dmix = _dot_nt(dx_ref[...], wout_ref[...])
        g0 = _sigmoid(z0_ref[...].astype(F32) + bg_ref[:, :D])
        g1 = _sigmoid(z1_ref[...].astype(F32) + bg_ref[:, D:])
        dypb = (dmix * g0).astype(BF16)
        dygb = (dmix * g1).astype(BF16)
        dz0 = dmix * yp_ref[...].astype(F32) * g0 * (1.0 - g0)
        dz1 = dmix * yg_ref[...].astype(F32) * g1 * (1.0 - g1)
        dzg_ref[:, :D] = dz0.astype(BF16)
        dzg_ref[:, D:] = dz1.astype(BF16)
        dbg_ref[:, :D] += jnp.sum(dz0, axis=0, keepdims=True)
        dbg_ref[:, D:] += jnp.sum(dz1, axis=0, keepdims=True)
        dyp_ref[...] = dypb
        dyg_ref[...] = dygb
        dpp = _dot_nt(dypb[:, 0:256], wpp_ref[0])
        for j in range(1, 4):
            dpp = dpp + _dot_nt(dypb[:, j * 256:(j + 1) * 256], wpp_ref[j])
        dpp_ref[...] = dpp.astype(BF16)
        dog = _dot_nt(dygb, wgla_ref[...])
        gh = gh_ref[...]
        dgh = jnp.zeros((1, HV), F32)
        for h in range(HEADS):
            cs = slice(h * HV, (h + 1) * HV)
            ov = o_ref[:, cs].astype(F32)
            r = lax.rsqrt(jnp.mean(ov * ov, axis=-1, keepdims=True) + EPS)
            oh = ov * r
            zo = zog_ref[:, cs].astype(F32)
            sg = _sigmoid(zo)
            dog_h = dog[:, cs]
            don = dog_h * zo * sg
            dzog_ref[:, cs] = (dog_h * oh * gh * sg * (1.0 + zo * (1.0 - sg))).astype(BF16)
            dgh = dgh + jnp.sum(don * oh, axis=0, keepdims=True)
            doh = don * gh
            do_ref[:, cs] = (r * (doh - oh * jnp.mean(doh * oh, axis=-1, keepdims=True))).astype(BF16)
        dgh_ref[...] += dgh

    row = lambda i: (i, 0)
    const2 = lambda i: (0, 0)
    return pl.pallas_call(
        body, name="merge_bwd", grid=(s // ts,),
        in_specs=[pl.BlockSpec((ts, D), row), pl.BlockSpec((ts, D), lambda i: (i, 0)), pl.BlockSpec((ts, D), lambda i: (i, 1)),
                  pl.BlockSpec((ts, D), lambda i: (i, OFF_OG // D)), pl.BlockSpec((ts, D), row), pl.BlockSpec((ts, D), row),
                  pl.BlockSpec((ts, D), row), pl.BlockSpec((1, 2 * D), const2), pl.BlockSpec((1, HV), const2),
                  pl.BlockSpec((4, POOL_W, 256), lambda i: (0, 0, 0)), pl.BlockSpec((D, D), const2),
                  pl.BlockSpec((D, D), const2)],
        out_specs=[pl.BlockSpec((ts, 2 * D), row), pl.BlockSpec((ts, D), row), pl.BlockSpec((ts, D), row),
                   pl.BlockSpec((ts, POOL_W), row), pl.BlockSpec((ts, D), row), pl.BlockSpec((ts, D), row),
                   pl.BlockSpec((1, 2 * D), const2), pl.BlockSpec((1, HV), const2)],
        out_shape=[jax.ShapeDtypeStruct((s, 2 * D), BF16), jax.ShapeDtypeStruct((s, D), BF16),
                   jax.ShapeDtypeStruct((s, D), BF16), jax.ShapeDtypeStruct((s, POOL_W), BF16),
                   jax.ShapeDtypeStruct((s, D), BF16), jax.ShapeDtypeStruct((s, D), BF16),
                   jax.ShapeDtypeStruct((1, 2 * D), F32), jax.ShapeDtypeStruct((1, HV), F32)],
        compiler_params=_cp("arbitrary"),
    )(dx1b, zr, zr, zr, yp, yg, o, bgate, ghead, wpp, wgla, wout)


HALO = 16
CCH = 1408


def _conv_taps(u_ref, halo_ref, cs, first, ts):
    u = u_ref[:, cs].astype(F32)
    hal = halo_ref[:, cs].astype(F32)
    h1 = jnp.where(first, 0.0, _pick_row(hal, HALO - 1))
    h2 = jnp.where(first, 0.0, _pick_row(hal, HALO - 2))
    row = _rows(u.shape)
    r1 = jnp.where(row == 0, h1, pltpu.roll(u, 1, 0))
    r2 = jnp.where(row == 0, h2, jnp.where(row == 1, h1, pltpu.roll(u, 2, 0)))
    return u, r1, r2


def _ffn_down_loss(u, x1, tgt, wconv, bconv, wdown, gfin, ts):
    s = x1.shape[0]

    def body(u_ref, halo_ref, x1_ref, t_ref, wc_ref, bc_ref, wd_ref, gf_ref, a_ref, c_ref, dx_ref, dxb_ref, ls_ref,
             dgf_ref):
        i = pl.program_id(0)

        @pl.when(i == 0)
        def _():
            ls_ref[...] = jnp.zeros_like(ls_ref)
            dgf_ref[...] = jnp.zeros_like(dgf_ref)

        first = i == 0
        acc = x1_ref[...]
        for hf in range(2):
            cg = slice(hf * CCH, (hf + 1) * CCH)
            cv = slice(D_FF + hf * CCH, D_FF + (hf + 1) * CCH)
            vals = []
            for cs in (cg, cv):
                u0, u1, u2 = _conv_taps(u_ref, halo_ref, cs, first, ts)
                vals.append(bc_ref[:, cs] + wc_ref[0:1, cs] * u2 + wc_ref[1:2, cs] * u1 + wc_ref[2:3, cs] * u0)
                c_ref[:, cs] = vals[-1].astype(BF16)
            a = (vals[0] * _sigmoid(vals[0]) * vals[1]).astype(BF16)
            a_ref[:, cg] = a
            acc = acc + _dot(a, wd_ref[cg, :])
        r = lax.rsqrt(jnp.mean(acc * acc, axis=-1, keepdims=True) + EPS)
        xh = acc * r
        gf = gf_ref[...]
        err = xh * gf - t_ref[...]
        ls_ref[...] += (0.5 / D) * jnp.sum(jnp.sum(err * err, axis=-1, keepdims=True), axis=0, keepdims=True)
        dy = err * (1.0 / D)
        dgf_ref[...] += jnp.sum(dy * xh, axis=0, keepdims=True)
        dxh = dy * gf
        dx = r * (dxh - xh * jnp.mean(dxh * xh, axis=-1, keepdims=True))
        dx_ref[...] = dx
        dxb_ref[...] = dx.astype(BF16)

    row = lambda i: (i, 0)
    const2 = lambda i: (0, 0)
    return pl.pallas_call(
        body, name="ffn_down_loss", grid=(s // ts,),
        in_specs=[pl.BlockSpec((ts, N_UP), row),
                  pl.BlockSpec((HALO, N_UP), lambda i: (jnp.maximum(i * (ts // HALO) - 1, 0), 0)),
                  pl.BlockSpec((ts, D), row), pl.BlockSpec((ts, D), row), pl.BlockSpec((3, N_UP), const2),
                  pl.BlockSpec((1, N_UP), const2), pl.BlockSpec((D_FF, D), const2), pl.BlockSpec((1, D), const2)],
        out_specs=[pl.BlockSpec((ts, D_FF), row), pl.BlockSpec((ts, N_UP), row), pl.BlockSpec((ts, D), row),
                   pl.BlockSpec((ts, D), row), pl.BlockSpec((1, 128), const2), pl.BlockSpec((1, D), const2)],
        out_shape=[jax.ShapeDtypeStruct((s, D_FF), BF16), jax.ShapeDtypeStruct((s, N_UP), BF16),
                   jax.ShapeDtypeStruct((s, D), F32), jax.ShapeDtypeStruct((s, D), BF16),
                   jax.ShapeDtypeStruct((1, 128), F32), jax.ShapeDtypeStruct((1, D), F32)],
        compiler_params=_cp("arbitrary"),
    )(u, u, x1, tgt, wconv, bconv, wdown, gfin)


def _ffn_bwd(dx2b, u, c, wconv, wdown, ts):
    s = dx2b.shape[0]
    nt = s // ts

    def body(dx_ref, u_ref, c_ref, wc_ref, wd_ref, du_ref, db_ref, dw_ref, nxt_ref):
        @pl.when(pl.program_id(0) == 0)
        def _():
            db_ref[...] = jnp.zeros_like(db_ref)
            dw_ref[...] = jnp.zeros_like(dw_ref)
            nxt_ref[...] = jnp.zeros_like(nxt_ref)

        dxv = dx_ref[...]
        row8 = _rows((8, CCH))
        for hf in range(2):
            cg = slice(hf * CCH, (hf + 1) * CCH)
            cv = slice(D_FF + hf * CCH, D_FF + (hf + 1) * CCH)
            da = _dot_nt(dxv, wd_ref[cg, :])
            gate = c_ref[:, cg].astype(F32)
            val = c_ref[:, cv].astype(F32)
            sg = _sigmoid(gate)
            dcs = (da * val * sg * (1.0 + gate * (1.0 - sg)), da * gate * sg)
            for cs, dc in zip((cg, cv), dcs):
                n1 = nxt_ref[0:1, cs]
                n2 = nxt_ref[1:2, cs]
                r1, r2 = pltpu.roll(dc, ts - 1, 0), pltpu.roll(dc, ts - 2, 0)
                f1 = jnp.concatenate([r1[:ts - 8], jnp.where(row8 == 7, n1, r1[ts - 8:])], axis=0)
                f2 = jnp.concatenate([r2[:ts - 8], jnp.where(row8 == 7, n2, jnp.where(row8 == 6, n1, r2[ts - 8:]))], axis=0)
                uv = u_ref[:, cs].astype(F32)
                db_ref[:, cs] += jnp.sum(dc, axis=0, keepdims=True)
                dw_ref[0:1, cs] += jnp.sum(f2 * uv, axis=0, keepdims=True)
                dw_ref[1:2, cs] += jnp.sum(f1 * uv, axis=0, keepdims=True)
                dw_ref[2:3, cs] += jnp.sum(dc * uv, axis=0, keepdims=True)
                du_ref[:, cs] = (wc_ref[2:3, cs] * dc + wc_ref[1:2, cs] * f1 + wc_ref[0:1, cs] * f2).astype(BF16)
                nxt_ref[:, cs] = dc[0:8, :]

    rev = lambda i: (nt - 1 - i, 0)
    const2 = lambda i: (0, 0)
    return pl.pallas_call(
        body, name="ffn_bwd", grid=(nt,),
        in_specs=[pl.BlockSpec((ts, D), rev), pl.BlockSpec((ts, N_UP), rev), pl.BlockSpec((ts, N_UP), rev),
                  pl.BlockSpec((3, N_UP), const2), pl.BlockSpec((D_FF, D), const2)],
        out_specs=[pl.BlockSpec((ts, N_UP), rev), pl.BlockSpec((1, N_UP), const2), pl.BlockSpec((3, N_UP), const2)],
        out_shape=[jax.ShapeDtypeStruct((s, N_UP), BF16), jax.ShapeDtypeStruct((1, N_UP), F32),
                   jax.ShapeDtypeStruct((3, N_UP), F32)],
        scratch_shapes=[pltpu.VMEM((8, N_UP), F32)],
        compiler_params=_cp("arbitrary"),
    )(dx2b, u, c, wconv, wdown)


ANY = pl.BlockSpec(memory_space=pl.ANY)


def _place():
    x, y, c = lax.axis_index("x"), lax.axis_index("y"), lax.axis_index("c")
    chips = [(1 - x, y), (x, 1 - y), (1 - x, 1 - y)]
    return x, y, c, chips


def _half(shape, c, axis):
    size = shape[axis] // 2
    cut = pl.ds(pl.multiple_of(c * size, 8 if axis == 0 else 128), size)
    return (cut, slice(None)) if axis == 0 else (slice(None), cut)


def _half_shape(shape, axis):
    return (shape[0] // 2, shape[1]) if axis == 0 else (shape[0], shape[1] // 2)


def _remote(src, dst, send_sems, recv_sems, k, to):
    return pltpu.make_async_remote_copy(src_ref=src, dst_ref=dst, send_sem=send_sems.at[k], recv_sem=recv_sems.at[k],
                                        device_id=to, device_id_type=MESH)


def _all_gather_weights(big, axes, small):
    nb, ns = len(big), len(small)
    n = nb + ns
    n_sem = 6 * nb + 3 * ns

    def body(*refs):
        ins, outs = refs[:n], refs[n:2 * n]
        send_sems, recv_sems = refs[2 * n:]
        x, y, c, chips = _place()
        me = 2 * x + y
        sib = (x, y, 1 - c)
        started = []
        for a in range(nb):
            mine = _half(big[a].shape, c, axes[a])
            for k, ch in enumerate(chips):
                cp = _remote(ins[a].at[mine], outs[a].at[(me,) + mine], send_sems, recv_sems, 6 * a + k,
                             (ch[0], ch[1], c))
                cp.start()
                started.append(cp)
        for a in range(ns):
            for k, ch in enumerate(chips):
                cp = _remote(ins[nb + a], outs[nb + a].at[me], send_sems, recv_sems, 6 * nb + 3 * a + k,
                             (ch[0], ch[1], c))
                cp.start()
                started.append(cp)
        for a in range(nb):
            mine = _half(big[a].shape, c, axes[a])
            for k, ch in enumerate(chips):
                landed = outs[a].at[(2 * ch[0] + ch[1],) + mine]
                _remote(landed, landed, send_sems, recv_sems, 6 * a + k, sib).wait_recv()
                cp = _remote(landed, landed, send_sems, recv_sems, 6 * a + 3 + k, sib)
                cp.start()
                started.append(cp)
        for a in range(nb):
            other = _half(big[a].shape, 1 - c, axes[a])
            for k, ch in enumerate(chips):
                landed = outs[a].at[(2 * ch[0] + ch[1],) + other]
                _remote(landed, landed, send_sems, recv_sems, 6 * a + 3 + k, sib).wait_recv()
        for a in range(ns):
            for k, ch in enumerate(chips):
                landed = outs[nb + a].at[2 * ch[0] + ch[1]]
                _remote(landed, landed, send_sems, recv_sems, 6 * nb + 3 * a + k, sib).wait_recv()
        for cp in started:
            cp.wait_send()

    arrs = list(big) + list(small)
    return pl.pallas_call(
        body, name="all_gather_weights",
        in_specs=[ANY] * n, out_specs=[ANY] * n,
        out_shape=[jax.ShapeDtypeStruct((4,) + a.shape, a.dtype) for a in arrs],
        scratch_shapes=[pltpu.SemaphoreType.DMA((n_sem,)), pltpu.SemaphoreType.DMA((n_sem,))],
        compiler_params=pltpu.CompilerParams(has_side_effects=True),
    )(*arrs)


def _sibling_exchange(grads, axes, smalls, name):
    nb = len(grads)
    n = nb + len(smalls)

    def body(*refs):
        ins, outs = refs[:n], refs[n:2 * n]
        send_sems, recv_sems = refs[2 * n:]
        x, y, c, _ = _place()
        sib = (x, y, 1 - c)
        cps = []
        for a in range(nb):
            theirs = _half(grads[a].shape[1:], 1 - c, axes[a])
            cps.append(_remote(ins[a].at[(slice(None),) + theirs], outs[a], send_sems, recv_sems, a, sib))
        for a in range(nb, n):
            cps.append(_remote(ins[a], outs[a], send_sems, recv_sems, a, sib))
        for cp in cps:
            cp.start()
        for cp in cps:
            cp.wait()

    out_shape = [jax.ShapeDtypeStruct((4,) + _half_shape(g.shape[1:], ax), g.dtype) for g, ax in zip(grads, axes)]
    out_shape += [jax.ShapeDtypeStruct(a.shape, F32) for a in smalls]
    return pl.pallas_call(
        body, name=name, in_specs=[ANY] * n, out_specs=[ANY] * n, out_shape=out_shape,
        scratch_shapes=[pltpu.SemaphoreType.DMA((n,)), pltpu.SemaphoreType.DMA((n,))],
        compiler_params=pltpu.CompilerParams(has_side_effects=True),
    )(*grads, *smalls)


def _gather_share(lands, axes, name):
    n = len(lands)

    def body(*refs):
        outs = refs[n:2 * n]
        send_sems, recv_sems = refs[2 * n:]
        x, y, c, chips = _place()
        sib = (x, y, 1 - c)
        cps = []
        for a in range(n):
            mine = _half(lands[a].shape[1:], c, axes[a])
            for k, ch in enumerate(chips):
                landed = outs[a].at[(2 * ch[0] + ch[1],) + mine]
                cps.append(_remote(landed, landed, send_sems, recv_sems, 3 * a + k, sib))
        for cp in cps:
            cp.start()
        for a in range(n):
            other = _half(lands[a].shape[1:], 1 - c, axes[a])
            for k, ch in enumerate(chips):
                landed = outs[a].at[(2 * ch[0] + ch[1],) + other]
                _remote(landed, landed, send_sems, recv_sems, 3 * a + k, sib).wait_recv()
        for cp in cps:
            cp.wait_send()

    return pl.pallas_call(
        body, name=name, in_specs=[ANY] * n, out_specs=[ANY] * n,
        out_shape=[jax.ShapeDtypeStruct(a.shape, a.dtype) for a in lands],
        input_output_aliases={a: a for a in range(n)},
        scratch_shapes=[pltpu.SemaphoreType.DMA((3 * n,)), pltpu.SemaphoreType.DMA((3 * n,))],
        compiler_params=pltpu.CompilerParams(has_side_effects=True),
    )(*lands)


def _sibling_share(halves, name):
    n = len(halves)

    def body(*refs):
        ins, outs = refs[:n], refs[n:2 * n]
        send_sems, recv_sems = refs[2 * n:]
        x, y, c, _ = _place()
        cps = [_remote(ins[a], outs[a], send_sems, recv_sems, a, (x, y, 1 - c)) for a in range(n)]
        for cp in cps:
            cp.start()
        for cp in cps:
            cp.wait()

    return pl.pallas_call(
        body, name=name, in_specs=[ANY] * n, out_specs=[ANY] * n,
        out_shape=[jax.ShapeDtypeStruct(h.shape, F32) for h in halves],
        scratch_shapes=[pltpu.SemaphoreType.DMA((n,)), pltpu.SemaphoreType.DMA((n,))],
        compiler_params=pltpu.CompilerParams(has_side_effects=True),
    )(*halves)


HBM = pl.BlockSpec(memory_space=pltpu.HBM)
SEM = pl.BlockSpec(memory_space=pltpu.SEMAPHORE)
DATAFLOW = pltpu.SideEffectType.DATAFLOW_SIDE_EFFECTING


def _split_start(name, srcs, land_shapes, plan, n_copies, after):
    lands = [lax.empty(shp, dt) for shp, dt in land_shapes]
    bufs = list(srcs) + lands
    nb, ns = len(bufs), len(srcs)

    def body(*refs):
        send_sems, recv_sems, token = refs[nb + 1], refs[nb + 2], refs[-1]
        for k, (src, dst, to) in enumerate(plan(refs[:ns], refs[ns:nb])):
            _remote(src, dst, send_sems, recv_sems, k, to).start()
        token[...] = jnp.zeros_like(token)

    res = pl.pallas_call(
        body, name=name,
        out_shape=(pltpu.SemaphoreType.DMA((n_copies,)), pltpu.SemaphoreType.DMA((n_copies,)),
                   *[pltpu.HBM(b.shape, b.dtype) for b in bufs], jax.ShapeDtypeStruct((8, 128), F32)),
        in_specs=[HBM] * nb + [ANY],
        out_specs=(SEM, SEM, *[HBM] * nb, pl.BlockSpec(memory_space=pltpu.VMEM)),
        input_output_aliases={i: 2 + i for i in range(nb)},
        compiler_params=pltpu.CompilerParams(has_side_effects=DATAFLOW),
    )(*[pltpu.with_memory_space_constraint(b, pltpu.HBM) for b in bufs], after)
    return (res[0], res[1], list(res[2:2 + nb])), res[-1]


def _split_wait(name, handle, n_srcs, plan, after):
    send_sems, recv_sems, bufs = handle
    nb = len(bufs)

    def body(*refs):
        sends, recvs = refs[nb], refs[nb + 1]
        for k, (src, dst, to) in enumerate(plan(refs[:n_srcs], refs[n_srcs:nb])):
            cp = _remote(src, dst, sends, recvs, k, to)
            cp.wait_send()
            cp.wait_recv()

    res = pl.pallas_call(
        body, name=name, out_shape=[pltpu.HBM(b.shape, b.dtype) for b in bufs],
        in_specs=[HBM] * nb + [SEM, SEM, ANY], out_specs=[HBM] * nb,
        input_output_aliases={i: i for i in range(nb)},
        compiler_params=pltpu.CompilerParams(has_side_effects=DATAFLOW),
    )(*bufs, send_sems, recv_sems, after)
    return list(res[:n_srcs]), list(res[n_srcs:])


def _gather_plan(shapes, axes, n_whole=0):
    def plan(srcs, lands):
        x, y, c, chips = _place()
        out = []
        for a, (shape, axis) in enumerate(zip(shapes, axes)):
            mine = _half(shape, c, axis)
            for ch in chips:
                out.append((srcs[a].at[mine], lands[a].at[(2 * x + y,) + mine], (ch[0], ch[1], c)))
        for a in range(len(shapes), len(shapes) + n_whole):
            for ch in chips:
                out.append((srcs[a], lands[a].at[2 * x + y], (ch[0], ch[1], c)))
        return out
    return plan


def _sibling_plan(shapes, axes):
    def plan(srcs, lands):
        x, y, c, _ = _place()
        return [(srcs[a].at[(slice(None),) + _half(shape, 1 - c, axis)], lands[a], (x, y, 1 - c))
                for a, (shape, axis) in enumerate(zip(shapes, axes))]
    return plan


def _reduce_plan(n_big, n_small):
    def plan(srcs, lands):
        x, y, c, chips = _place()
        out = []
        for a in range(n_big):
            for k, ch in enumerate(chips):
                out.append((srcs[a].at[2 * ch[0] + ch[1]], lands[a].at[k], (ch[0], ch[1], c)))
        for a in range(n_big, n_big + n_small):
            for ch in chips:
                out.append((srcs[a], lands[a].at[2 * x + y], (ch[0], ch[1], c)))
        return out
    return plan


def _row_tile(rows, cols, mult):
    best = mult
    for t in range(mult, rows + 1, mult):
        if rows % t == 0 and t * cols * 4 <= (2 << 20):
            best = t
    return best if rows % best == 0 else rows


COL_TILE = 256


def _half_tiling(hshape, axis, mult):
    hr, hc = hshape
    if axis == 0:
        tr = _row_tile(hr, hc, mult)
        return tr, hc, hr // tr
    return hr, COL_TILE, hc // COL_TILE


def _tile_idx(axis, t):
    return (t, 0) if axis == 0 else (0, t)


def _chip_partial(place, g, t, axis, name):
    hshape = t.shape[1:]
    br, bc, nt = _half_tiling(hshape, axis, 16)

    def body(pl_ref, g_ref, t_ref, pf_ref, pb_ref):
        v = g_ref[...].astype(F32) + t_ref[...].astype(F32)
        pb_ref[...] = v.astype(BF16)

        @pl.when(pl.program_id(1) == pl_ref[0])
        def _():
            pf_ref[...] = v

    blk = (None, br, bc)
    return pl.pallas_call(
        body, name=name,
        grid_spec=pltpu.PrefetchScalarGridSpec(
            num_scalar_prefetch=1, grid=(nt, 4),
            in_specs=[pl.BlockSpec(blk, lambda i, j, p: (j,) + _tile_idx(axis, p[1] * nt + i)),
                      pl.BlockSpec(blk, lambda i, j, p: (j,) + _tile_idx(axis, i))],
            out_specs=[pl.BlockSpec((br, bc), lambda i, j, p: _tile_idx(axis, i)),
                       pl.BlockSpec(blk, lambda i, j, p: (j,) + _tile_idx(axis, i))]),
        out_shape=[jax.ShapeDtypeStruct(hshape, F32), jax.ShapeDtypeStruct((4,) + hshape, BF16)],
        compiler_params=_cp("arbitrary", "arbitrary"),
    )(place, g, t)


def _finish_half(pf, rb, axis, name):
    hshape = pf.shape
    br, bc, nt = _half_tiling(hshape, axis, 16)

    def body(pf_ref, rb_ref, o_ref):
        o_ref[...] = ((pf_ref[...] + rb_ref[0].astype(F32)) + rb_ref[1].astype(F32)) + rb_ref[2].astype(F32)

    return pl.pallas_call(
        body, name=name, grid=(nt,),
        in_specs=[pl.BlockSpec((br, bc), lambda i: _tile_idx(axis, i)),
                  pl.BlockSpec((3, br, bc), lambda i: (0,) + _tile_idx(axis, i))],
        out_specs=pl.BlockSpec((br, bc), lambda i: _tile_idx(axis, i)),
        out_shape=jax.ShapeDtypeStruct(hshape, F32),
        compiler_params=_cp("arbitrary"),
    )(pf, rb)


def _add2(a, b, name):
    def body(a_ref, b_ref, o_ref):
        o_ref[...] = a_ref[...] + b_ref[...]

    return pl.pallas_call(body, name=name, out_shape=jax.ShapeDtypeStruct(a.shape, F32))(a, b)


def _adam_math(w, g, m, v):
    m = ADAM_B1 * m + (1.0 - ADAM_B1) * g
    v = ADAM_B2 * v + (1.0 - ADAM_B2) * (g * g)
    m_hat = m / (1.0 - ADAM_B1 ** ADAM_STEP)
    v_hat = v / (1.0 - ADAM_B2 ** ADAM_STEP)
    return -ADAM_LR * (m_hat / (jnp.sqrt(v_hat) + ADAM_EPS) + ADAM_WD * w), m, v


def _adam_halves(place, w, mine, theirs, m, v, axis, name):
    br, bc, nt = _half_tiling(mine.shape, axis, 8)

    def body(pl_ref, w_ref, a_ref, b_ref, m_ref, v_ref, g_ref, d_ref, mo_ref, vo_ref):
        is_mine = pl.program_id(0) // nt == pl_ref[1]
        g = jnp.where(is_mine, a_ref[...], b_ref[...])
        d, mn, vn = _adam_math(w_ref[...], g, m_ref[...], v_ref[...])
        g_ref[...] = g
        d_ref[...] = d
        mo_ref[...] = mn
        vo_ref[...] = vn

    full = pl.BlockSpec((br, bc), lambda i, p: _tile_idx(axis, i))
    mine_spec = pl.BlockSpec((br, bc), lambda i, p: _tile_idx(axis, jnp.where(i // nt == p[1], i % nt, nt - 1)))
    theirs_spec = pl.BlockSpec((br, bc), lambda i, p: _tile_idx(axis, jnp.where(i // nt == p[1], 0, i % nt)))
    return pl.pallas_call(
        body, name=name,
        grid_spec=pltpu.PrefetchScalarGridSpec(
            num_scalar_prefetch=1, grid=(2 * nt,), in_specs=[full, mine_spec, theirs_spec, full, full],
            out_specs=[full] * 4),
        out_shape=[jax.ShapeDtypeStruct(w.shape, F32)] * 4, compiler_params=_cp("arbitrary"),
    )(place, w, mine, theirs, m, v)


def _add_many(xs, ys, name):
    n = len(xs)

    def body(*refs):
        for i in range(n):
            refs[2 * n + i][...] = refs[i][...] + refs[n + i][...]

    return pl.pallas_call(body, name=name, out_shape=[jax.ShapeDtypeStruct(a.shape, F32) for a in xs])(*xs, *ys)


def _adam_small(place, owns, landed, ws, ms, vs, widths):
    n, nw = len(owns), len(ws)

    def body(pl_ref, *refs):
        own_r, land_r = refs[:n], refs[n:2 * n]
        w_r, m_r, v_r = (refs[2 * n + k * nw:2 * n + (k + 1) * nw] for k in range(3))
        outs = refs[2 * n + 3 * nw:]
        g_o, d_o, m_o, v_o = outs[:n], outs[n:n + nw], outs[n + nw:n + 2 * nw], outs[n + 2 * nw:]
        for me in range(4):
            @pl.when(pl_ref[0] == me)
            def _(me=me):
                for i in range(n):
                    p = [own_r[i][...] if k == me else land_r[i][k] for k in range(4)]
                    g = ((p[0] + p[1]) + p[2]) + p[3]
                    if i < nw and widths[i]:
                        g = g[:, me * widths[i]:(me + 1) * widths[i]]
                    g_o[i][...] = g
                    if i < nw:
                        d, mn, vn = _adam_math(w_r[i][...], g, m_r[i][...], v_r[i][...])
                        d_o[i][...] = d
                        m_o[i][...] = mn
                        v_o[i][...] = vn

    g_shapes = [jax.ShapeDtypeStruct(ws[i].shape if i < nw else owns[i].shape, F32) for i in range(n)]
    w_shapes = [jax.ShapeDtypeStruct(w.shape, F32) for w in ws]
    whole = lambda a: pl.BlockSpec(a.shape, lambda i, p, nd=len(a.shape): (0,) * nd)
    ins = list(owns) + list(landed) + list(ws) + list(ms) + list(vs)
    out_shape = g_shapes + w_shapes * 3
    out = pl.pallas_call(
        body, name="adam_small",
        grid_spec=pltpu.PrefetchScalarGridSpec(num_scalar_prefetch=1, grid=(1,), in_specs=[whole(a) for a in ins],
                                               out_specs=[whole(a) for a in out_shape]),
        out_shape=out_shape, compiler_params=_cp("arbitrary"),
    )(place, *ins)
    return out[:n], out[n:n + nw], out[n + nw:n + 2 * nw], out[n + 2 * nw:]


def kernel(x, g_mix, w_in, b_gate, w_gk_up, b_gk, w_pool_grp, pool_scale, g_gla_head, w_pool_proj, w_gla_proj, w_out, g_ffn, w_up, w_conv, b_conv, w_down, g_final, loss_target, m_g_mix, m_w_in, m_b_gate, m_w_gk_up, m_b_gk, m_w_pool_grp, m_pool_scale, m_g_gla_head, m_w_pool_proj, m_w_gla_proj, m_w_out, m_g_ffn, m_w_up, m_w_conv, m_b_conv, m_w_down, m_g_final, v_g_mix, v_w_in, v_b_gate, v_w_gk_up, v_b_gk, v_w_pool_grp, v_pool_scale, v_g_gla_head, v_w_pool_proj, v_w_gla_proj, v_w_out, v_g_ffn, v_w_up, v_w_conv, v_b_conv, v_w_down, v_g_final):
    s = x.shape[1]
    ts = min(s, 512)
    tm = min(s, 256)
    cx, cy, cc = lax.axis_index("x"), lax.axis_index("y"), lax.axis_index("c")
    chip = 2 * cx + cy
    place = jnp.stack([chip, cc]).astype(jnp.int32)

    big_names = ("w_in", "w_pool_proj", "w_gla_proj", "w_out", "w_up", "w_down")
    axes = (1, 0, 0, 0, 0, 0)
    shards = dict(w_in=jnp.transpose(w_in[0]), w_pool_proj=w_pool_proj[0], w_gla_proj=w_gla_proj[0], w_out=w_out[0],
                  w_up=w_up[0], w_down=w_down[0])
    def fill_own(lands, mine):
        return [lax.dynamic_update_slice(g, o_[None], (chip, 0, 0)) for g, o_ in zip(lands, mine)]

    def gather_start(tag, halves, group_axes, whole, after):
        plan = _gather_plan([o_.shape for o_ in halves], group_axes, len(whole))
        srcs = list(halves) + list(whole)
        handle, token = _split_start("gather_" + tag + "_start", srcs, [((4,) + o_.shape, o_.dtype) for o_ in srcs], plan,
                                     3 * len(srcs), after)
        return (handle, plan, len(halves), len(srcs), group_axes), token

    def gather_finish(tag, started, after):
        handle, plan, n_halves, n, group_axes = started
        mine, lands = _split_wait("gather_" + tag + "_wait", handle, n, plan, after)
        lands[:n_halves] = _gather_share(lands[:n_halves], group_axes, "gather_" + tag + "_share")
        return fill_own(lands, mine)

    in_w, tok = gather_start("in", [shards["w_in"].astype(BF16)], axes[:1], [], g_mix)
    zero = tok[0, 0]
    own = [(shards[n] + zero).astype(BF16) for n in big_names[1:]]
    mix_w, tok = gather_start("mix", own[0:3], axes[1:4], [w_gk_up[0] + zero, w_conv[0] + zero], tok)
    ffn_w, tok = gather_start("ffn", own[3:5], axes[4:6], [], tok)
    xs, tgt = x[0], loss_target[0]
    wgrp = w_pool_grp[0]
    h = _rmsnorm(xs, g_mix + tok[0:1, 0:1], "norm_mix", ts)
    m_in_t, v_in_t = jnp.transpose(m_w_in[0]), jnp.transpose(v_w_in[0])
    h, m_in_t, v_in_t = lax.optimization_barrier((h, m_in_t, v_in_t))
    w_in_t = gather_finish("in", in_w, h)[0].reshape(N_IN, D)
    w_rt = jnp.concatenate([w_in_t[3600:], w_in_t[1536:3584], w_in_t[0:1536], w_in_t[3584:3600],
                            jnp.zeros((128 - GATE_RANK, D), BF16)], axis=0)
    nsh = N_IN // 4

    zr = _matmul_resident(h, w_rt, "in_proj", 1152, transposed=True)
    p, pp = _pool_fwd(zr, wgrp, pool_scale)
    wpp, wgla, wout, wgk4, wconv4 = gather_finish("mix", mix_w, pp)
    wgla, wout = wgla.reshape(D, D), wout.reshape(D, D)
    wgk_full = jnp.transpose(wgk4, (1, 0, 2)).reshape(GATE_RANK, 512)
    wconv_full = jnp.transpose(wconv4, (1, 0, 2)).reshape(3, N_UP)
    wgk_pad = jnp.concatenate([wgk_full, jnp.zeros((128 - GATE_RANK, 512), F32)], axis=0)
    o, og, sp = _gla_fwd(zr, wgk_pad, b_gk, g_gla_head, ts)
    x1, mixed, yp, yg = _merge_fwd(xs, zr, pp, og, b_gate, wpp, wgla, wout, ts)
    wup, wdown = gather_finish("ffn", ffn_w, x1)
    wdown = wdown.reshape(D_FF, D)
    h2 = _rmsnorm(x1, g_ffn, "norm_ffn", ts)
    u = _matmul_resident(h2, wup, "ffn_up", None)
    a, conv_out, dx2, dx2b, loss_part, dgfin = _ffn_down_loss(u, x1, tgt, wconv_full, b_conv, wdown,
                                                              g_final.reshape(1, D), tm)

    du, dbconv, dwconv = _ffn_bwd(dx2b, u, conv_out, wconv_full, wdown, tm)
    dw_down = _matmul_tn(a, dx2b, "dw_down", D, tm=1408)
    dw_up = _matmul_tn(h2, du, "dw_up", 1408, shard_major=True)

    def exchange_start(tag, grads, group_axes, after):
        plan = _sibling_plan([g.shape[1:] for g in grads], group_axes)
        lands = [((4,) + _half_shape(g.shape[1:], ax), g.dtype) for g, ax in zip(grads, group_axes)]
        handle, token = _split_start("sibling_" + tag + "_start", grads, lands, plan, len(grads), after)
        return (handle, plan, len(grads)), token

    def partials(tag, names, group_axes, exchange, after):
        handle, plan, n = exchange
        mine, theirs = _split_wait("sibling_" + tag + "_wait", handle, n, plan, after)
        return zip(*[_chip_partial(place, g, t, ax, "chip_partial_" + nm)
                     for nm, ax, g, t in zip(names, group_axes, mine, theirs)])

    ffn_names, ffn_axes = ("w_up", "w_down"), (0, 0)
    ffn_x, token = exchange_start("ffn", [dw_up, dw_down.reshape(4, 704, D)], ffn_axes, du)
    dx1, dx1b, dgffn = _matmul_nt_normbwd(du, wup, x1, g_ffn + token[0:1, 0:1], dx2, "ffn_up_bwd", ts)
    ffn_pf, ffn_pb = partials("ffn", ffn_names, ffn_axes, ffn_x, dx1b)
    ffn_plan = _reduce_plan(2, False)
    ffn_handle, token = _split_start("reduce_ffn_start", ffn_pb, [((3,) + p.shape[1:], BF16) for p in ffn_pb],
                                     ffn_plan, 6, ffn_pf[0])

    dzg, dyp, dyg, dpp, do, dzog, dbgate, dghead = _merge_bwd(dx1b, zr, yp, yg, o, b_gate + token[0:1, 0:1], g_gla_head,
                                                             wpp, wgla, wout, ts)
    dw_out = _matmul_tn(mixed, dx1b, "dw_out", D)
    dw_gla = _matmul_tn(og, dyg, "dw_gla", D)
    dw_pp = _matmul_tn(pp, dyp, "dw_pp", 256, shard_major=True)

    out_names, out_axes = ("w_pool_proj", "w_gla_proj", "w_out"), (0, 0, 0)
    out_x, token = exchange_start("out", [dw_pp, dw_gla.reshape(4, 256, D), dw_out.reshape(4, 256, D)], out_axes, dpp)
    dzp, dwgrp, dscale = _pool_bwd(p, dpp, wgrp, pool_scale + token[0:1, 0:1])
    out_pf, out_pb = partials("out", out_names, out_axes, out_x, dzp)
    out_plan = _reduce_plan(3, False)
    out_handle, token = _split_start("reduce_out_start", out_pb, [((3,) + p_.shape[1:], BF16) for p_ in out_pb],
                                     out_plan, 9, out_pf[0])
    dq, dk, dv, dgpre = _gla_bwd(zr, do, sp, wgk_pad, b_gk + token[0:1, 0:1], ts)
    dzgk, dwgk, dbgk = _gk_bwd(dgpre, zr, wgk_pad, ts)
    dzr = jnp.concatenate([dzg, dv, dzog, dzp, dq, dk, dzgk], axis=1)
    dw_rt = _matmul_tn(dzr, h, "dw_in", D, tm=1152)

    def grad_rows(lo, hi):
        out = []
        for seg_lo, seg_hi, at in ((0, 1536, OFF_POOL), (1536, 3584, OFF_V), (3584, 3600, OFF_GK), (3600, N_IN, OFF_GATE)):
            a_, b_ = max(lo, seg_lo), min(hi, seg_hi)
            if a_ < b_:
                out.append(dw_rt[at + a_ - seg_lo:at + b_ - seg_lo])
        return jnp.concatenate(out, axis=0)

    dw_in_t = jnp.stack([grad_rows(j * nsh, (j + 1) * nsh) for j in range(4)])

    in_sib = _sibling_exchange([dw_in_t], (1,), [], "sibling_exchange_in")
    in_pf, in_pb = _chip_partial(place, dw_in_t, in_sib[0], 1, "chip_partial_w_in")
    in_plan = _reduce_plan(1, 0)
    in_handle, token = _split_start("reduce_in_start", [in_pb], [((3,) + in_pb.shape[1:], BF16)], in_plan, 3, in_pf)
    grad_x, _, dgmix = _matmul_nt_normbwd(dzr, w_rt, xs, g_mix + token[0:1, 0:1], dx1, "in_proj_bwd", ts, transposed=True)
    small_names = ("g_mix", "b_gate", "w_gk_up", "b_gk", "w_pool_grp", "pool_scale", "g_gla_head", "g_ffn", "w_conv",
                   "b_conv", "g_final")
    small_mine = [dgmix, dbgate, dwgk[:GATE_RANK], dbgk, dwgrp.reshape(4 * 128, 128), dscale, dghead, dgffn, dwconv, dbconv,
                  dgfin, loss_part]
    small_sib = _sibling_exchange([], (), small_mine, "sibling_exchange_small")
    small_chip = _add_many(small_mine, small_sib, "chip_partial_small")
    small_plan = _reduce_plan(0, len(small_chip))
    small_handle, token = _split_start("reduce_small_start", small_chip, [((4,) + a_.shape, F32) for a_ in small_chip],
                                       small_plan, 3 * len(small_chip), small_mine[0])

    ms = dict(w_in=m_in_t, w_pool_proj=m_w_pool_proj[0], w_gla_proj=m_w_gla_proj[0], w_out=m_w_out[0],
              w_up=m_w_up[0], w_down=m_w_down[0])
    vs = dict(w_in=v_in_t, w_pool_proj=v_w_pool_proj[0], w_gla_proj=v_w_gla_proj[0], w_out=v_w_out[0],
              w_up=v_w_up[0], w_down=v_w_down[0])
    grad, delta, new_m, new_v = {}, {}, {}, {}

    def finish_and_update(names, group_axes, part_f, landed, tag):
        halves = [_finish_half(pf, rb, ax, "finish_" + n) for n, ax, pf, rb in zip(names, group_axes, part_f, landed)]
        sib_halves = _sibling_share(halves, "sibling_share_" + tag)
        for n, ax, mine, theirs in zip(names, group_axes, halves, sib_halves):
            res = _adam_halves(place, shards[n], mine, theirs, ms[n], vs[n], ax, "adam_" + n)
            if n == "w_in":
                res = [jnp.transpose(r_) for r_ in res]
            grad[n], delta[n], new_m[n], new_v[n] = [r_[None] for r_ in res]

    _, ffn_landed = _split_wait("reduce_ffn_wait", ffn_handle, 2, ffn_plan, token)
    _, out_landed = _split_wait("reduce_out_wait", out_handle, 3, out_plan, ffn_landed[0])
    finish_and_update(ffn_names + out_names, ffn_axes + out_axes, ffn_pf + out_pf, ffn_landed + out_landed, "rest")
    _, in_landed = _split_wait("reduce_in_wait", in_handle, 1, in_plan, delta["w_out"])
    finish_and_update(("w_in",), (1,), (in_pf,), in_landed, "in")
    small_sent, small_landed = _split_wait("reduce_small_wait", small_handle, len(small_chip), small_plan, delta["w_in"])
    given = dict(g_mix=(g_mix, m_g_mix, v_g_mix), b_gate=(b_gate, m_b_gate, v_b_gate), w_gk_up=(w_gk_up, m_w_gk_up, v_w_gk_up),
                 b_gk=(b_gk, m_b_gk, v_b_gk), w_pool_grp=(w_pool_grp, m_w_pool_grp, v_w_pool_grp),
                 pool_scale=(pool_scale, m_pool_scale, v_pool_scale), g_gla_head=(g_gla_head, m_g_gla_head, v_g_gla_head),
                 g_ffn=(g_ffn, m_g_ffn, v_g_ffn), w_conv=(w_conv, m_w_conv, v_w_conv), b_conv=(b_conv, m_b_conv, v_b_conv),
                 g_final=(g_final, m_g_final, v_g_final))
    flat2 = lambda a: a.reshape(-1, a.shape[-1])
    widths = [dict(w_gk_up=128, w_conv=1408).get(n) for n in small_names]
    totals, ds, mo, vo = _adam_small(place, small_sent, small_landed, *[[flat2(given[n][k]) for n in small_names] for k in range(3)],
                                     widths)
    loss = totals[-1][0, 0]
    for i, n in enumerate(small_names):
        shp = given[n][0].shape
        grad[n], delta[n], new_m[n], new_v[n] = [r_.reshape(shp) for r_ in (totals[i], ds[i], mo[i], vo[i])]

    order = ("g_mix", "w_in", "b_gate", "w_gk_up", "b_gk", "w_pool_grp", "pool_scale", "g_gla_head", "w_pool_proj",
             "w_gla_proj", "w_out", "g_ffn", "w_up", "w_conv", "b_conv", "w_down", "g_final")
    return (loss, grad_x[None], *[grad[n] for n in order], *[delta[n] for n in order], *[new_m[n] for n in order],
            *[new_v[n] for n in order])
```

```python
import functools

import jax
import jax.numpy as jnp
from jax import lax
from jax.experimental import pallas as pl
from jax.experimental.pallas import tpu as pltpu

F32 = jnp.float32
BF16 = jnp.bfloat16
MESH = pl.DeviceIdType.MESH

D = 1024
EPS = 1e-6
CHUNK = 64
POOL_W = 512
POOL_WINDOWS = (2, 4, 8, 16)
HEADS = 4
HK = 128
HV = 256
GATE_RANK = 16
D_FF = 2816
N_UP = 2 * D_FF
N_IN = 5648
QSCALE = HK ** -0.5
N_INR = 5760
OFF_GATE, OFF_V, OFF_OG, OFF_POOL, OFF_Q, OFF_K, OFF_GK = 0, 2048, 3072, 4096, 4608, 5120, 5632

ADAM_LR, ADAM_B1, ADAM_B2, ADAM_EPS, ADAM_WD, ADAM_STEP = 0.001, 0.9, 0.999, 1e-08, 0.01, 10

VMEM_LIMIT = 56 * 1024 * 1024


def _cp(*sem):
    return pltpu.CompilerParams(dimension_semantics=sem if sem else None, vmem_limit_bytes=VMEM_LIMIT)


def _dot(a, b):
    return jnp.dot(a, b, preferred_element_type=F32)


def _dot_nt(a, b):
    return lax.dot_general(a, b, (((1,), (1,)), ((), ())), preferred_element_type=F32)


def _dot_tn(a, b):
    return lax.dot_general(a, b, (((0,), (0,)), ((), ())), preferred_element_type=F32)


def _sigmoid(v):
    return 1.0 / (1.0 + jnp.exp(-v))


def _rows(shape):
    return lax.broadcasted_iota(jnp.int32, shape, 0)


def _pick_row(v, r):
    return jnp.sum(jnp.where(_rows(v.shape) == r, v, 0.0), axis=0, keepdims=True)


def _rmsnorm(x, g, after, name, ts):
    s = x.shape[0]

    def body(x_ref, g_ref, after_ref, h_ref):
        xv = x_ref[...]
        r = lax.rsqrt(jnp.mean(xv * xv, axis=-1, keepdims=True) + EPS)
        h_ref[...] = (xv * r * g_ref[...]).astype(BF16)

    return pl.pallas_call(
        body, name=name, grid=(s // ts,),
        in_specs=[pl.BlockSpec((ts, D), lambda i: (i, 0)), pl.BlockSpec((1, D), lambda i: (0, 0)), ANY],
        out_specs=pl.BlockSpec((ts, D), lambda i: (i, 0)), out_shape=jax.ShapeDtypeStruct((s, D), BF16),
        compiler_params=_cp("arbitrary"),
    )(x, g, after)


MM_ROWS = 512


def _matmul_resident(h, w, name, tn, transposed=False):
    s = h.shape[0]
    if transposed:
        nj = w.shape[0] // tn
        w_spec = pl.BlockSpec((tn, D), lambda j: (j, 0))
    elif w.ndim == 3:
        nj, tn = w.shape[0], w.shape[2]
        w_spec = pl.BlockSpec((None, D, tn), lambda j: (j, 0, 0))
    else:
        nj = w.shape[1] // tn
        w_spec = pl.BlockSpec((D, tn), lambda j: (0, j))
    mm = _dot_nt if transposed else _dot
    rc = min(s, MM_ROWS)

    def body(h_ref, w_ref, z_ref):
        for r0 in range(0, s, rc):
            z_ref[r0:r0 + rc, :] = mm(h_ref[r0:r0 + rc, :], w_ref[...]).astype(BF16)

    return pl.pallas_call(
        body, name=name, grid=(nj,),
        in_specs=[pl.BlockSpec((s, D), lambda j: (0, 0)), w_spec],
        out_specs=pl.BlockSpec((s, tn), lambda j: (0, j)), out_shape=jax.ShapeDtypeStruct((s, nj * tn), BF16),
        compiler_params=_cp("arbitrary"),
    )(h, w)


def _matmul_nt_normbwd(dz, w, x, g, resid, after, name, ts, transposed=False):
    s = x.shape[0]

    def body(dz_ref, w_hbm, x_ref, g_ref, r_ref, after_ref, o_ref, ob_ref, dg_ref, w_ref, sem):
        @pl.when(pl.program_id(0) == 0)
        def _():
            cp = pltpu.make_async_copy(w_hbm, w_ref, sem)
            cp.start()
            cp.wait()
            dg_ref[...] = jnp.zeros_like(dg_ref)

        if transposed:
            dh = _dot(dz_ref[...], w_ref[...])
        else:
            kc = w.shape[2]
            dh = _dot_nt(dz_ref[:, 0:kc], w_ref[0])
            for j in range(1, w.shape[0]):
                dh = dh + _dot_nt(dz_ref[:, j * kc:(j + 1) * kc], w_ref[j])
        xv = x_ref[...]
        r = lax.rsqrt(jnp.mean(xv * xv, axis=-1, keepdims=True) + EPS)
        xh = xv * r
        dg_ref[...] += jnp.sum(dh * xh, axis=0, keepdims=True)
        dxh = dh * g_ref[...]
        out = r_ref[...] + r * (dxh - xh * jnp.mean(dxh * xh, axis=-1, keepdims=True))
        o_ref[...] = out
        ob_ref[...] = out.astype(BF16)

    row = lambda i: (i, 0)
    kdim = dz.shape[1]
    return pl.pallas_call(
        body, name=name, grid=(s // ts,),
        in_specs=[pl.BlockSpec((ts, kdim), row), ANY, pl.BlockSpec((ts, D), row),
                  pl.BlockSpec((1, D), lambda i: (0, 0)), pl.BlockSpec((ts, D), row), ANY],
        out_specs=[pl.BlockSpec((ts, D), row), pl.BlockSpec((ts, D), row), pl.BlockSpec((1, D), lambda i: (0, 0))],
        out_shape=[jax.ShapeDtypeStruct((s, D), F32), jax.ShapeDtypeStruct((s, D), BF16),
                   jax.ShapeDtypeStruct((1, D), F32)],
        scratch_shapes=[pltpu.VMEM(w.shape, BF16), pltpu.SemaphoreType.DMA],
        compiler_params=_cp("arbitrary"),
    )(dz, w, x, g, resid, after)


def _matmul_tn(a, b, name, tn, shard_major=False, tm=None):
    s, m = a.shape
    n = b.shape[1]
    tm = m if tm is None else tm
    ni, nj = m // tm, n // tn

    def body(a_ref, b_ref, o_ref):
        o_ref[...] = _dot_tn(a_ref[...], b_ref[...]).astype(BF16)

    if shard_major:
        out_spec = pl.BlockSpec((None, tm, tn), lambda i, j: (j, i, 0))
        out_shape = jax.ShapeDtypeStruct((nj, m, tn), BF16)
    else:
        out_spec = pl.BlockSpec((tm, tn), lambda i, j: (i, j))
        out_shape = jax.ShapeDtypeStruct((m, n), BF16)
    return pl.pallas_call(
        body, name=name, grid=(ni, nj),
        in_specs=[pl.BlockSpec((s, tm), lambda i, j: (0, i)), pl.BlockSpec((s, tn), lambda i, j: (0, j))],
        out_specs=out_spec, out_shape=out_shape,
        compiler_params=_cp("arbitrary", "arbitrary"),
    )(a, b)


def _pool_fwd(zr, wgrp, scale):
    s = zr.shape[0]

    def body(u_ref, w_ref, sc_ref, p_ref, pp_ref):
        row = _rows((s, 128))
        for gi, win in enumerate(POOL_WINDOWS):
            cs = slice(gi * 128, (gi + 1) * 128)
            u = u_ref[:, cs].astype(F32)
            acc, k = u, 1
            while k < win:
                acc = acc + jnp.where(row >= k, pltpu.roll(acc, k, 0), 0.0)
                k *= 2
            cnt = jnp.minimum(row + 1, win).astype(F32)
            p = (acc / cnt - u).astype(BF16)
            p_ref[:, cs] = p
            pp_ref[:, cs] = (_dot(p, w_ref[gi].astype(BF16)) * sc_ref[:, cs]).astype(BF16)

    return pl.pallas_call(
        body, name="pool_fwd", grid=(1,),
        in_specs=[pl.BlockSpec((s, POOL_W), lambda i: (0, OFF_POOL // POOL_W)),
                  pl.BlockSpec((4, 128, 128), lambda i: (0, 0, 0)), pl.BlockSpec((1, POOL_W), lambda i: (0, 0))],
        out_specs=[pl.BlockSpec((s, POOL_W), lambda i: (0, 0))] * 2,
        out_shape=[jax.ShapeDtypeStruct((s, POOL_W), BF16)] * 2,
        compiler_params=_cp("arbitrary"),
    )(zr, wgrp, scale)


def _pool_bwd(p, dpp, wgrp, scale, after):
    s = p.shape[0]

    def body(p_ref, dpp_ref, w_ref, sc_ref, after_ref, dz_ref, dw_ref, dsc_ref):
        row = _rows((s, 128))
        for gi, win in enumerate(POOL_WINDOWS):
            cs = slice(gi * 128, (gi + 1) * 128)
            pv = p_ref[:, cs]
            wb = w_ref[gi].astype(BF16)
            dpp_v = dpp_ref[:, cs].astype(F32)
            dsc_ref[:, cs] = jnp.sum(dpp_v * _dot(pv, wb), axis=0, keepdims=True)
            dpm = (dpp_v * sc_ref[:, cs]).astype(BF16)
            dw_ref[gi] = _dot_tn(pv, dpm)
            dp = _dot_nt(dpm, wb)
            cnt = jnp.minimum(row + 1, win).astype(F32)
            acc, k = dp / cnt, 1
            while k < win:
                acc = acc + jnp.where(row < s - k, pltpu.roll(acc, s - k, 0), 0.0)
                k *= 2
            dz_ref[:, cs] = (acc - dp).astype(BF16)

    full = lambda i: (0, 0)
    return pl.pallas_call(
        body, name="pool_bwd", grid=(1,),
        in_specs=[pl.BlockSpec((s, POOL_W), full), pl.BlockSpec((s, POOL_W), full),
                  pl.BlockSpec((4, 128, 128), lambda i: (0, 0, 0)), pl.BlockSpec((1, POOL_W), full), ANY],
        out_specs=[pl.BlockSpec((s, POOL_W), full), pl.BlockSpec((4, 128, 128), lambda i: (0, 0, 0)),
                   pl.BlockSpec((1, POOL_W), full)],
        out_shape=[jax.ShapeDtypeStruct((s, POOL_W), BF16), jax.ShapeDtypeStruct((4, 128, 128), F32),
                   jax.ShapeDtypeStruct((1, POOL_W), F32)],
        compiler_params=_cp("arbitrary"),
    )(p, dpp, wgrp, scale, after)


def _gla_decay(zgk_ref, wgk_ref, bgk_ref, rb):
    g = _dot(zgk_ref[...], wgk_ref[...].astype(BF16)) + bgk_ref[...]
    la = (jnp.minimum(g, 0.0) - jnp.log(1.0 + jnp.exp(-jnp.abs(g)))) * (1.0 / 16.0)
    rowm = _rows(la.shape) & (CHUNK - 1)
    bc, k = la, 1
    while k < CHUNK:
        bc = bc + jnp.where(rowm >= k, pltpu.roll(bc, k, 0), 0.0)
        k *= 2
    return g, jnp.exp(bc), jnp.exp(-bc)


GLA_HB = 4


def _gla_specs(rb, rmap):
    wk, wv = GLA_HB * HK, GLA_HB * HV
    return [pl.BlockSpec((rb, wk), lambda h, r: (rmap(h, r), OFF_Q // wk + h)),
            pl.BlockSpec((rb, wk), lambda h, r: (rmap(h, r), OFF_K // wk + h)),
            pl.BlockSpec((rb, wv), lambda h, r: (rmap(h, r), OFF_V // wv + h)),
            pl.BlockSpec((rb, 128), lambda h, r: (rmap(h, r), OFF_GK // 128))]


def _gla_fwd(zr, wgk, bgk, ghead, rb):
    s = zr.shape[0]
    nc = rb // CHUNK
    wk, wv = GLA_HB * HK, GLA_HB * HV

    def body(q_ref, k_ref, v_ref, zgk_ref, zog_ref, wgk_ref, bgk_ref, gh_ref, o_ref, og_ref, sp_ref, st_ref):
        @pl.when(pl.program_id(1) == 0)
        def _():
            st_ref[...] = jnp.zeros_like(st_ref)

        _, e_pos, e_neg = _gla_decay(zgk_ref, wgk_ref, bgk_ref, rb)
        lower = _rows((CHUNK, CHUNK)) >= lax.broadcasted_iota(jnp.int32, (CHUNK, CHUNK), 1)
        for c in range(nc):
            sl = slice(c * CHUNK, (c + 1) * CHUNK)
            for hh in range(GLA_HB):
                ck, cv = slice(hh * HK, (hh + 1) * HK), slice(hh * HV, (hh + 1) * HV)
                q = q_ref[sl, ck].astype(F32) * QSCALE
                k = k_ref[sl, ck].astype(F32)
                v = v_ref[sl, cv]
                ec, fc = e_pos[sl, ck], e_neg[sl, ck]
                qfw = (q * ec).astype(BF16)
                kfw_f = k * fc
                s_fw = _dot_nt(qfw, kfw_f.astype(BF16))
                s_bw = _dot_nt((q * fc).astype(BF16), (k * ec).astype(BF16))
                pm = jnp.where(lower, s_fw, s_bw).astype(BF16)
                st = st_ref[hh]
                stb = st.astype(BF16)
                sp_ref[c, hh] = stb
                o = _dot(pm, v) + _dot_nt(qfw, stb)
                e_last = _pick_row(ec, CHUNK - 1)
                kdec = (kfw_f * e_last).astype(BF16)
                st_ref[hh] = st * e_last + _dot_tn(v, kdec)
                r = lax.rsqrt(jnp.mean(o * o, axis=-1, keepdims=True) + EPS)
                zo = zog_ref[sl, cv].astype(F32)
                o_ref[sl, cv] = o.astype(BF16)
                og_ref[sl, cv] = (o * r * gh_ref[...] * zo * _sigmoid(zo)).astype(BF16)

    rmap = lambda h, r: r
    return pl.pallas_call(
        body, name="gla_fwd", grid=(HEADS // GLA_HB, s // rb),
        in_specs=_gla_specs(rb, rmap) + [
            pl.BlockSpec((rb, wv), lambda h, r: (r, OFF_OG // wv + h)),
            pl.BlockSpec((128, wk), lambda h, r: (0, h)), pl.BlockSpec((1, wk), lambda h, r: (0, h)),
            pl.BlockSpec((1, HV), lambda h, r: (0, 0))],
        out_specs=[pl.BlockSpec((rb, wv), lambda h, r: (r, h)), pl.BlockSpec((rb, wv), lambda h, r: (r, h)),
                   pl.BlockSpec((nc, GLA_HB, HV, HK), lambda h, r: (r, h, 0, 0))],
        out_shape=[jax.ShapeDtypeStruct((s, D), BF16), jax.ShapeDtypeStruct((s, D), BF16),
                   jax.ShapeDtypeStruct((s // CHUNK, HEADS, HV, HK), BF16)],
        scratch_shapes=[pltpu.VMEM((GLA_HB, HV, HK), F32)],
        compiler_params=_cp("arbitrary", "arbitrary"),
    )(zr, zr, zr, zr, zr, wgk, bgk, ghead)


def _gla_bwd(zr, do, sp, wgk, bgk, after, rb):
    s = zr.shape[0]
    nc = rb // CHUNK
    nr = s // rb
    wk, wv = GLA_HB * HK, GLA_HB * HV

    def body(q_ref, k_ref, v_ref, zgk_ref, do_ref, sp_ref, wgk_ref, bgk_ref, after_ref, dq_ref, dk_ref, dv_ref, dg_ref,
             gt_ref, dbc_ref):
        @pl.when(pl.program_id(1) == 0)
        def _():
            gt_ref[...] = jnp.zeros_like(gt_ref)

        g, e_pos, e_neg = _gla_decay(zgk_ref, wgk_ref, bgk_ref, rb)
        lower = _rows((CHUNK, CHUNK)) >= lax.broadcasted_iota(jnp.int32, (CHUNK, CHUNK), 1)
        is_last = _rows((CHUNK, HK)) == CHUNK - 1
        for c in reversed(range(nc)):
            sl = slice(c * CHUNK, (c + 1) * CHUNK)
            for hh in range(GLA_HB):
                ck, cv = slice(hh * HK, (hh + 1) * HK), slice(hh * HV, (hh + 1) * HV)
                q = q_ref[sl, ck].astype(F32) * QSCALE
                k = k_ref[sl, ck].astype(F32)
                v = v_ref[sl, cv]
                dov = do_ref[sl, cv]
                ec, fc = e_pos[sl, ck], e_neg[sl, ck]
                qfw_f, kfw_f, qbw_f, kbw_f = q * ec, k * fc, q * fc, k * ec
                qfw, kfw, qbw, kbw = qfw_f.astype(BF16), kfw_f.astype(BF16), qbw_f.astype(BF16), kbw_f.astype(BF16)
                pm = jnp.where(lower, _dot_nt(qfw, kfw), _dot_nt(qbw, kbw)).astype(BF16)
                e_last = _pick_row(ec, CHUNK - 1)
                kdec = (kfw_f * e_last).astype(BF16)
                gt = gt_ref[hh]
                gtb = gt.astype(BF16)
                spv = sp_ref[c, hh]
                dp = _dot_nt(dov, v)
                dv_ref[sl, cv] = (_dot_tn(pm, dov) + _dot_nt(kdec, gtb)).astype(BF16)
                ds_fw = jnp.where(lower, dp, 0.0).astype(BF16)
                ds_bw = jnp.where(lower, 0.0, dp).astype(BF16)
                dqfw = _dot(ds_fw, kfw) + _dot(dov, spv)
                dkfw = _dot_tn(ds_fw, qfw)
                dqbw = _dot(ds_bw, kbw)
                dkbw = _dot_tn(ds_bw, qbw)
                dkdec = _dot(v, gtb)
                de_last = (jnp.sum(gt * spv.astype(F32), axis=0, keepdims=True)
                           + jnp.sum(dkdec * kfw_f, axis=0, keepdims=True))
                dkfw = dkfw + dkdec * e_last
                dq_ref[sl, ck] = ((dqfw * ec + dqbw * fc) * QSCALE).astype(BF16)
                dk_ref[sl, ck] = (dkfw * fc + dkbw * ec).astype(BF16)
                dbc = dqfw * qfw_f - dqbw * qbw_f + dkbw * kbw_f - dkfw * kfw_f
                dbc_ref[sl, ck] = dbc + jnp.where(is_last, de_last * e_last, 0.0)
                gt_ref[hh] = _dot_tn(dov, qfw) + gt * e_last
        rowm = _rows((rb, wk)) & (CHUNK - 1)
        dla, kk = dbc_ref[...], 1
        while kk < CHUNK:
            dla = dla + jnp.where(rowm < CHUNK - kk, pltpu.roll(dla, rb - kk, 0), 0.0)
            kk *= 2
        dg_ref[...] = dla * (1.0 / 16.0) * _sigmoid(-g)

    rmap = lambda h, r: nr - 1 - r
    rev = lambda h, r: (nr - 1 - r, h)
    return pl.pallas_call(
        body, name="gla_bwd", grid=(HEADS // GLA_HB, nr),
        in_specs=_gla_specs(rb, rmap) + [
            pl.BlockSpec((rb, wv), rev),
            pl.BlockSpec((nc, GLA_HB, HV, HK), lambda h, r: (nr - 1 - r, h, 0, 0)),
            pl.BlockSpec((128, wk), lambda h, r: (0, h)), pl.BlockSpec((1, wk), lambda h, r: (0, h)), ANY],
        out_specs=[pl.BlockSpec((rb, wk), rev), pl.BlockSpec((rb, wk), rev), pl.BlockSpec((rb, wv), rev),
                   pl.BlockSpec((rb, wk), rev)],
        out_shape=[jax.ShapeDtypeStruct((s, HEADS * HK), BF16), jax.ShapeDtypeStruct((s, HEADS * HK), BF16),
                   jax.ShapeDtypeStruct((s, D), BF16), jax.ShapeDtypeStruct((s, HEADS * HK), F32)],
        scratch_shapes=[pltpu.VMEM((GLA_HB, HV, HK), F32), pltpu.VMEM((rb, wk), F32)],
        compiler_params=_cp("arbitrary", "arbitrary"),
    )(zr, zr, zr, zr, do, sp, wgk, bgk, after)


def _gk_bwd(dgpre, zr, wgk, ts):
    s = zr.shape[0]

    def body(dg_ref, zgk_ref, w_ref, dz_ref, dw_ref, db_ref):
        @pl.when(pl.program_id(0) == 0)
        def _():
            dw_ref[...] = jnp.zeros_like(dw_ref)
            db_ref[...] = jnp.zeros_like(db_ref)

        dg = dg_ref[...]
        dgb = dg.astype(BF16)
        dz_ref[...] = _dot_nt(dgb, w_ref[...].astype(BF16)).astype(BF16)
        dw_ref[...] += _dot_tn(zgk_ref[...], dgb)
        db_ref[...] += jnp.sum(dg, axis=0, keepdims=True)

    return pl.pallas_call(
        body, name="gk_bwd", grid=(s // ts,),
        in_specs=[pl.BlockSpec((ts, 512), lambda i: (i, 0)), pl.BlockSpec((ts, 128), lambda i: (i, OFF_GK // 128)),
                  pl.BlockSpec((128, 512), lambda i: (0, 0))],
        out_specs=[pl.BlockSpec((ts, 128), lambda i: (i, 0)), pl.BlockSpec((128, 512), lambda i: (0, 0)),
                   pl.BlockSpec((1, 512), lambda i: (0, 0))],
        out_shape=[jax.ShapeDtypeStruct((s, 128), BF16), jax.ShapeDtypeStruct((128, 512), F32),
                   jax.ShapeDtypeStruct((1, 512), F32)],
        compiler_params=_cp("arbitrary"),
    )(dgpre, zr, wgk)


def _merge_fwd(x, zr, pp, og, bgate, wpp, wgla, wout, gffn, ts):
    s = x.shape[0]

    def body(x_ref, z0_ref, z1_ref, pp_ref, og_ref, bg_ref, wpp_ref, wgla_ref, wout_ref, gf_ref,
             x1_ref, mix_ref, yp_ref, yg_ref, h2_ref):
        ppv = pp_ref[...]
        yp = jnp.concatenate([_dot(ppv, wpp_ref[j]) for j in range(4)], axis=1)
        yg = _dot(og_ref[...], wgla_ref[...])
        g0 = _sigmoid(z0_ref[...].astype(F32) + bg_ref[:, :D])
        g1 = _sigmoid(z1_ref[...].astype(F32) + bg_ref[:, D:])
        mixed = (g0 * yp + g1 * yg).astype(BF16)
        x1 = x_ref[...] + _dot(mixed, wout_ref[...])
        x1_ref[...] = x1
        mix_ref[...] = mixed
        yp_ref[...] = yp.astype(BF16)
        yg_ref[...] = yg.astype(BF16)
        r = lax.rsqrt(jnp.mean(x1 * x1, axis=-1, keepdims=True) + EPS)
        h2_ref[...] = (x1 * r * gf_ref[...]).astype(BF16)

    row = lambda i: (i, 0)
    const2 = lambda i: (0, 0)
    return pl.pallas_call(
        body, name="merge_fwd", grid=(s // ts,),
        in_specs=[pl.BlockSpec((ts, D), row), pl.BlockSpec((ts, D), lambda i: (i, 0)), pl.BlockSpec((ts, D), lambda i: (i, 1)),
                  pl.BlockSpec((ts, POOL_W), row), pl.BlockSpec((ts, D), row), pl.BlockSpec((1, 2 * D), const2),
                  pl.BlockSpec((4, POOL_W, 256), lambda i: (0, 0, 0)), pl.BlockSpec((D, D), const2),
                  pl.BlockSpec((D, D), const2), pl.BlockSpec((1, D), const2)],
        out_specs=[pl.BlockSpec((ts, D), row)] * 5,
        out_shape=[jax.ShapeDtypeStruct((s, D), F32)] + [jax.ShapeDtypeStruct((s, D), BF16)] * 4,
        compiler_params=_cp("arbitrary"),
    )(x, zr, zr, pp, og, bgate, wpp, wgla, wout, gffn)


def _merge_bwd(dx1b, zr, yp, yg, o, bgate, ghead, wpp, wgla, wout, after, ts):
    s = dx1b.shape[0]

    def body(dx_ref, z0_ref, z1_ref, zog_ref, yp_ref, yg_ref, o_ref, bg_ref, gh_ref, wpp_ref, wgla_ref, wout_ref, after_ref,
             dzg_ref, dyp_ref, dyg_ref, dpp_ref, do_ref, dzog_ref, dbg_ref, dgh_ref):
        @pl.when(pl.program_id(0) == 0)
        def _():
            dbg_ref[...] = jnp.zeros_like(dbg_ref)
            dgh_ref[...] = jnp.zeros_like(dgh_ref)

        dmix = _dot_nt(dx_ref[...], wout_ref[...])
        g0 = _sigmoid(z0_ref[...].astype(F32) + bg_ref[:, :D])
        g1 = _sigmoid(z1_ref[...].astype(F32) + bg_ref[:, D:])
        dypb = (dmix * g0).astype(BF16)
        dygb = (dmix * g1).astype(BF16)
        dz0 = dmix * yp_ref[...].astype(F32) * g0 * (1.0 - g0)
        dz1 = dmix * yg_ref[...].astype(F32) * g1 * (1.0 - g1)
        dzg_ref[:, :D] = dz0.astype(BF16)
        dzg_ref[:, D:] = dz1.astype(BF16)
        dbg_ref[:, :D] += jnp.sum(dz0, axis=0, keepdims=True)
        dbg_ref[:, D:] += jnp.sum(dz1, axis=0, keepdims=True)
        dyp_ref[...] = dypb
        dyg_ref[...] = dygb
        dpp = _dot_nt(dypb[:, 0:256], wpp_ref[0])
        for j in range(1, 4):
            dpp = dpp + _dot_nt(dypb[:, j * 256:(j + 1) * 256], wpp_ref[j])
        dpp_ref[...] = dpp.astype(BF16)
        dog = _dot_nt(dygb, wgla_ref[...])
        gh = gh_ref[...]
        dgh = jnp.zeros((1, HV), F32)
        for h in range(HEADS):
            cs = slice(h * HV, (h + 1) * HV)
            ov = o_ref[:, cs].astype(F32)
            r = lax.rsqrt(jnp.mean(ov * ov, axis=-1, keepdims=True) + EPS)
            oh = ov * r
            zo = zog_ref[:, cs].astype(F32)
            sg = _sigmoid(zo)
            dog_h = dog[:, cs]
            don = dog_h * zo * sg
            dzog_ref[:, cs] = (dog_h * oh * gh * sg * (1.0 + zo * (1.0 - sg))).astype(BF16)
            dgh = dgh + jnp.sum(don * oh, axis=0, keepdims=True)
            doh = don * gh
            do_ref[:, cs] = (r * (doh - oh * jnp.mean(doh * oh, axis=-1, keepdims=True))).astype(BF16)
        dgh_ref[...] += dgh

    row = lambda i: (i, 0)
    const2 = lambda i: (0, 0)
    return pl.pallas_call(
        body, name="merge_bwd", grid=(s // ts,),
        in_specs=[pl.BlockSpec((ts, D), row), pl.BlockSpec((ts, D), lambda i: (i, 0)), pl.BlockSpec((ts, D), lambda i: (i, 1)),
                  pl.BlockSpec((ts, D), lambda i: (i, OFF_OG // D)), pl.BlockSpec((ts, D), row), pl.BlockSpec((ts, D), row),
                  pl.BlockSpec((ts, D), row), pl.BlockSpec((1, 2 * D), const2), pl.BlockSpec((1, HV), const2),
                  pl.BlockSpec((4, POOL_W, 256), lambda i: (0, 0, 0)), pl.BlockSpec((D, D), const2),
                  pl.BlockSpec((D, D), const2), ANY],
        out_specs=[pl.BlockSpec((ts, 2 * D), row), pl.BlockSpec((ts, D), row), pl.BlockSpec((ts, D), row),
                   pl.BlockSpec((ts, POOL_W), row), pl.BlockSpec((ts, D), row), pl.BlockSpec((ts, D), row),
                   pl.BlockSpec((1, 2 * D), const2), pl.BlockSpec((1, HV), const2)],
        out_shape=[jax.ShapeDtypeStruct((s, 2 * D), BF16), jax.ShapeDtypeStruct((s, D), BF16),
                   jax.ShapeDtypeStruct((s, D), BF16), jax.ShapeDtypeStruct((s, POOL_W), BF16),
                   jax.ShapeDtypeStruct((s, D), BF16), jax.ShapeDtypeStruct((s, D), BF16),
                   jax.ShapeDtypeStruct((1, 2 * D), F32), jax.ShapeDtypeStruct((1, HV), F32)],
        compiler_params=_cp("arbitrary"),
    )(dx1b, zr, zr, zr, yp, yg, o, bgate, ghead, wpp, wgla, wout, after)


HALO = 16
CCH = 1408


def _conv_taps(u_ref, halo_ref, cs, first, ts):
    u = u_ref[:, cs].astype(F32)
    hal = halo_ref[:, cs].astype(F32)
    h1 = jnp.where(first, 0.0, _pick_row(hal, HALO - 1))
    h2 = jnp.where(first, 0.0, _pick_row(hal, HALO - 2))
    row8 = _rows((8, u.shape[1]))
    r1, r2 = pltpu.roll(u, 1, 0), pltpu.roll(u, 2, 0)
    r1 = jnp.concatenate([jnp.where(row8 == 0, h1, r1[:8]), r1[8:]], axis=0)
    r2 = jnp.concatenate([jnp.where(row8 == 0, h2, jnp.where(row8 == 1, h1, r2[:8])), r2[8:]], axis=0)
    return u, r1, r2


def _ffn_down_loss(u, x1, tgt, wconv, bconv, wdown, gfin, ts):
    s = x1.shape[0]

    def body(u_ref, halo_ref, x1_ref, t_ref, wc_ref, bc_ref, wd_ref, gf_ref, a_ref, c_ref, dx_ref, dxb_ref, ls_ref,
             dgf_ref):
        i = pl.program_id(0)

        @pl.when(i == 0)
        def _():
            ls_ref[...] = jnp.zeros_like(ls_ref)
            dgf_ref[...] = jnp.zeros_like(dgf_ref)

        first = i == 0
        acc = x1_ref[...]
        for hf in range(2):
            cg = slice(hf * CCH, (hf + 1) * CCH)
            cv = slice(D_FF + hf * CCH, D_FF + (hf + 1) * CCH)
            vals = []
            for cs in (cg, cv):
                u0, u1, u2 = _conv_taps(u_ref, halo_ref, cs, first, ts)
                vals.append(bc_ref[:, cs] + wc_ref[0:1, cs] * u2 + wc_ref[1:2, cs] * u1 + wc_ref[2:3, cs] * u0)
                c_ref[:, cs] = vals[-1].astype(BF16)
            a = (vals[0] * _sigmoid(vals[0]) * vals[1]).astype(BF16)
            a_ref[:, cg] = a
            acc = acc + _dot(a, wd_ref[cg, :])
        r = lax.rsqrt(jnp.mean(acc * acc, axis=-1, keepdims=True) + EPS)
        xh = acc * r
        gf = gf_ref[...]
        err = xh * gf - t_ref[...]
        ls_ref[...] += (0.5 / D) * jnp.sum(jnp.sum(err * err, axis=-1, keepdims=True), axis=0, keepdims=True)
        dy = err * (1.0 / D)
        dgf_ref[...] += jnp.sum(dy * xh, axis=0, keepdims=True)
        dxh = dy * gf
        dx = r * (dxh - xh * jnp.mean(dxh * xh, axis=-1, keepdims=True))
        dx_ref[...] = dx
        dxb_ref[...] = dx.astype(BF16)

    row = lambda i: (i, 0)
    const2 = lambda i: (0, 0)
    return pl.pallas_call(
        body, name="ffn_down_loss", grid=(s // ts,),
        in_specs=[pl.BlockSpec((ts, N_UP), row),
                  pl.BlockSpec((HALO, N_UP), lambda i: (jnp.maximum(i * (ts // HALO) - 1, 0), 0)),
                  pl.BlockSpec((ts, D), row), pl.BlockSpec((ts, D), row), pl.BlockSpec((3, N_UP), const2),
                  pl.BlockSpec((1, N_UP), const2), pl.BlockSpec((D_FF, D), const2), pl.BlockSpec((1, D), const2)],
        out_specs=[pl.BlockSpec((ts, D_FF), row), pl.BlockSpec((ts, N_UP), row), pl.BlockSpec((ts, D), row),
                   pl.BlockSpec((ts, D), row), pl.BlockSpec((1, 128), const2), pl.BlockSpec((1, D), const2)],
        out_shape=[jax.ShapeDtypeStruct((s, D_FF), BF16), jax.ShapeDtypeStruct((s, N_UP), BF16),
                   jax.ShapeDtypeStruct((s, D), F32), jax.ShapeDtypeStruct((s, D), BF16),
                   jax.ShapeDtypeStruct((1, 128), F32), jax.ShapeDtypeStruct((1, D), F32)],
        compiler_params=_cp("arbitrary"),
    )(u, u, x1, tgt, wconv, bconv, wdown, gfin)


def _ffn_bwd(dx2b, u, c, wconv, wdown, ts):
    s = dx2b.shape[0]
    nt = s // ts

    def body(dx_ref, u_ref, c_ref, wc_ref, wd_ref, du_ref, db_ref, dw_ref, nxt_ref):
        @pl.when(pl.program_id(0) == 0)
        def _():
            db_ref[...] = jnp.zeros_like(db_ref)
            dw_ref[...] = jnp.zeros_like(dw_ref)
            nxt_ref[...] = jnp.zeros_like(nxt_ref)

        dxv = dx_ref[...]
        row8 = _rows((8, CCH))
        for hf in range(2):
            cg = slice(hf * CCH, (hf + 1) * CCH)
            cv = slice(D_FF + hf * CCH, D_FF + (hf + 1) * CCH)
            da = _dot_nt(dxv, wd_ref[cg, :])
            gate = c_ref[:, cg].astype(F32)
            val = c_ref[:, cv].astype(F32)
            sg = _sigmoid(gate)
            dcs = (da * val * sg * (1.0 + gate * (1.0 - sg)), da * gate * sg)
            for cs, dc in zip((cg, cv), dcs):
                n1 = nxt_ref[0:1, cs]
                n2 = nxt_ref[1:2, cs]
                r1, r2 = pltpu.roll(dc, ts - 1, 0), pltpu.roll(dc, ts - 2, 0)
                f1 = jnp.concatenate([r1[:ts - 8], jnp.where(row8 == 7, n1, r1[ts - 8:])], axis=0)
                f2 = jnp.concatenate([r2[:ts - 8], jnp.where(row8 == 7, n2, jnp.where(row8 == 6, n1, r2[ts - 8:]))], axis=0)
                uv = u_ref[:, cs].astype(F32)
                db_ref[:, cs] += jnp.sum(dc, axis=0, keepdims=True)
                dw_ref[0:1, cs] += jnp.sum(f2 * uv, axis=0, keepdims=True)
                dw_ref[1:2, cs] += jnp.sum(f1 * uv, axis=0, keepdims=True)
                dw_ref[2:3, cs] += jnp.sum(dc * uv, axis=0, keepdims=True)
                du_ref[:, cs] = (wc_ref[2:3, cs] * dc + wc_ref[1:2, cs] * f1 + wc_ref[0:1, cs] * f2).astype(BF16)
                nxt_ref[:, cs] = dc[0:8, :]

    rev = lambda i: (nt - 1 - i, 0)
    const2 = lambda i: (0, 0)
    return pl.pallas_call(
        body, name="ffn_bwd", grid=(nt,),
        in_specs=[pl.BlockSpec((ts, D), rev), pl.BlockSpec((ts, N_UP), rev), pl.BlockSpec((ts, N_UP), rev),
                  pl.BlockSpec((3, N_UP), const2), pl.BlockSpec((D_FF, D), const2)],
        out_specs=[pl.BlockSpec((ts, N_UP), rev), pl.BlockSpec((1, N_UP), const2), pl.BlockSpec((3, N_UP), const2)],
        out_shape=[jax.ShapeDtypeStruct((s, N_UP), BF16), jax.ShapeDtypeStruct((1, N_UP), F32),
                   jax.ShapeDtypeStruct((3, N_UP), F32)],
        scratch_shapes=[pltpu.VMEM((8, N_UP), F32)],
        compiler_params=_cp("arbitrary"),
    )(dx2b, u, c, wconv, wdown)


ANY = pl.BlockSpec(memory_space=pl.ANY)


def _place():
    x, y, c = lax.axis_index("x"), lax.axis_index("y"), lax.axis_index("c")
    chips = [(1 - x, y), (x, 1 - y), (1 - x, 1 - y)]
    return x, y, c, chips


def _half(shape, c, axis):
    size = shape[axis] // 2
    cut = pl.ds(pl.multiple_of(c * size, 8 if axis == 0 else 128), size)
    return (cut, slice(None)) if axis == 0 else (slice(None), cut)


def _half_shape(shape, axis):
    return (shape[0] // 2, shape[1]) if axis == 0 else (shape[0], shape[1] // 2)


def _remote(src, dst, send_sems, recv_sems, k, to):
    return pltpu.make_async_remote_copy(src_ref=src, dst_ref=dst, send_sem=send_sems.at[k], recv_sem=recv_sems.at[k],
                                        device_id=to, device_id_type=MESH)


def _all_gather_weights(big, axes, small):
    nb, ns = len(big), len(small)
    n = nb + ns
    n_sem = 6 * nb + 3 * ns

    def body(*refs):
        ins, outs = refs[:n], refs[n:2 * n]
        send_sems, recv_sems = refs[2 * n:]
        x, y, c, chips = _place()
        me = 2 * x + y
        sib = (x, y, 1 - c)
        started = []
        for a in range(nb):
            mine = _half(big[a].shape, c, axes[a])
            for k, ch in enumerate(chips):
                cp = _remote(ins[a].at[mine], outs[a].at[(me,) + mine], send_sems, recv_sems, 6 * a + k,
                             (ch[0], ch[1], c))
                cp.start()
                started.append(cp)
        for a in range(ns):
            for k, ch in enumerate(chips):
                cp = _remote(ins[nb + a], outs[nb + a].at[me], send_sems, recv_sems, 6 * nb + 3 * a + k,
                             (ch[0], ch[1], c))
                cp.start()
                started.append(cp)
        for a in range(nb):
            mine = _half(big[a].shape, c, axes[a])
            for k, ch in enumerate(chips):
                landed = outs[a].at[(2 * ch[0] + ch[1],) + mine]
                _remote(landed, landed, send_sems, recv_sems, 6 * a + k, sib).wait_recv()
                cp = _remote(landed, landed, send_sems, recv_sems, 6 * a + 3 + k, sib)
                cp.start()
                started.append(cp)
        for a in range(nb):
            other = _half(big[a].shape, 1 - c, axes[a])
            for k, ch in enumerate(chips):
                landed = outs[a].at[(2 * ch[0] + ch[1],) + other]
                _remote(landed, landed, send_sems, recv_sems, 6 * a + 3 + k, sib).wait_recv()
        for a in range(ns):
            for k, ch in enumerate(chips):
                landed = outs[nb + a].at[2 * ch[0] + ch[1]]
                _remote(landed, landed, send_sems, recv_sems, 6 * nb + 3 * a + k, sib).wait_recv()
        for cp in started:
            cp.wait_send()

    arrs = list(big) + list(small)
    return pl.pallas_call(
        body, name="all_gather_weights",
        in_specs=[ANY] * n, out_specs=[ANY] * n,
        out_shape=[jax.ShapeDtypeStruct((4,) + a.shape, a.dtype) for a in arrs],
        scratch_shapes=[pltpu.SemaphoreType.DMA((n_sem,)), pltpu.SemaphoreType.DMA((n_sem,))],
        compiler_params=pltpu.CompilerParams(has_side_effects=True),
    )(*arrs)


def _sibling_exchange(grads, axes, smalls, name):
    nb = len(grads)
    n = nb + len(smalls)

    def body(*refs):
        ins, outs = refs[:n], refs[n:2 * n]
        send_sems, recv_sems = refs[2 * n:]
        x, y, c, _ = _place()
        sib = (x, y, 1 - c)
        cps = []
        for a in range(nb):
            theirs = _half(grads[a].shape[1:], 1 - c, axes[a])
            cps.append(_remote(ins[a].at[(slice(None),) + theirs], outs[a], send_sems, recv_sems, a, sib))
        for a in range(nb, n):
            cps.append(_remote(ins[a], outs[a], send_sems, recv_sems, a, sib))
        for cp in cps:
            cp.start()
        for cp in cps:
            cp.wait()

    out_shape = [jax.ShapeDtypeStruct((4,) + _half_shape(g.shape[1:], ax), g.dtype) for g, ax in zip(grads, axes)]
    out_shape += [jax.ShapeDtypeStruct(a.shape, F32) for a in smalls]
    return pl.pallas_call(
        body, name=name, in_specs=[ANY] * n, out_specs=[ANY] * n, out_shape=out_shape,
        scratch_shapes=[pltpu.SemaphoreType.DMA((n,)), pltpu.SemaphoreType.DMA((n,))],
        compiler_params=pltpu.CompilerParams(has_side_effects=True),
    )(*grads, *smalls)


def _gather_share(lands, axes, name):
    n = len(lands)

    def body(*refs):
        outs = refs[n:2 * n]
        send_sems, recv_sems = refs[2 * n:]
        x, y, c, chips = _place()
        sib = (x, y, 1 - c)
        cps = []
        for a in range(n):
            mine = _half(lands[a].shape[1:], c, axes[a])
            for k, ch in enumerate(chips):
                landed = outs[a].at[(2 * ch[0] + ch[1],) + mine]
                cps.append(_remote(landed, landed, send_sems, recv_sems, 3 * a + k, sib))
        for cp in cps:
            cp.start()
        for a in range(n):
            other = _half(lands[a].shape[1:], 1 - c, axes[a])
            for k, ch in enumerate(chips):
                landed = outs[a].at[(2 * ch[0] + ch[1],) + other]
                _remote(landed, landed, send_sems, recv_sems, 3 * a + k, sib).wait_recv()
        for cp in cps:
            cp.wait_send()

    return pl.pallas_call(
        body, name=name, in_specs=[ANY] * n, out_specs=[ANY] * n,
        out_shape=[jax.ShapeDtypeStruct(a.shape, a.dtype) for a in lands],
        input_output_aliases={a: a for a in range(n)},
        scratch_shapes=[pltpu.SemaphoreType.DMA((3 * n,)), pltpu.SemaphoreType.DMA((3 * n,))],
        compiler_params=pltpu.CompilerParams(has_side_effects=True),
    )(*lands)


def _sibling_share(halves, name):
    n = len(halves)

    def body(*refs):
        ins, outs = refs[:n], refs[n:2 * n]
        send_sems, recv_sems = refs[2 * n:]
        x, y, c, _ = _place()
        cps = [_remote(ins[a], outs[a], send_sems, recv_sems, a, (x, y, 1 - c)) for a in range(n)]
        for cp in cps:
            cp.start()
        for cp in cps:
            cp.wait()

    return pl.pallas_call(
        body, name=name, in_specs=[ANY] * n, out_specs=[ANY] * n,
        out_shape=[jax.ShapeDtypeStruct(h.shape, F32) for h in halves],
        scratch_shapes=[pltpu.SemaphoreType.DMA((n,)), pltpu.SemaphoreType.DMA((n,))],
        compiler_params=pltpu.CompilerParams(has_side_effects=True),
    )(*halves)


HBM = pl.BlockSpec(memory_space=pltpu.HBM)
SEM = pl.BlockSpec(memory_space=pltpu.SEMAPHORE)
DATAFLOW = pltpu.SideEffectType.DATAFLOW_SIDE_EFFECTING


def _split_start(name, srcs, land_shapes, plan, n_copies, after):
    lands = [lax.empty(shp, dt) for shp, dt in land_shapes]
    bufs = list(srcs) + lands
    nb, ns = len(bufs), len(srcs)

    def body(*refs):
        send_sems, recv_sems, token = refs[nb + 1], refs[nb + 2], refs[-1]
        for k, (src, dst, to) in enumerate(plan(refs[:ns], refs[ns:nb])):
            _remote(src, dst, send_sems, recv_sems, k, to).start()
        token[...] = jnp.zeros_like(token)

    res = pl.pallas_call(
        body, name=name,
        out_shape=(pltpu.SemaphoreType.DMA((n_copies,)), pltpu.SemaphoreType.DMA((n_copies,)),
                   *[pltpu.HBM(b.shape, b.dtype) for b in bufs], jax.ShapeDtypeStruct((8, 128), F32)),
        in_specs=[HBM] * nb + [ANY],
        out_specs=(SEM, SEM, *[HBM] * nb, pl.BlockSpec(memory_space=pltpu.VMEM)),
        input_output_aliases={i: 2 + i for i in range(nb)},
        compiler_params=pltpu.CompilerParams(has_side_effects=DATAFLOW),
    )(*[pltpu.with_memory_space_constraint(b, pltpu.HBM) for b in bufs], after)
    return (res[0], res[1], list(res[2:2 + nb])), res[-1]


def _split_wait(name, handle, n_srcs, plan, after):
    send_sems, recv_sems, bufs = handle
    nb = len(bufs)

    def body(*refs):
        sends, recvs = refs[nb], refs[nb + 1]
        for k, (src, dst, to) in enumerate(plan(refs[:n_srcs], refs[n_srcs:nb])):
            cp = _remote(src, dst, sends, recvs, k, to)
            cp.wait_send()
            cp.wait_recv()

    res = pl.pallas_call(
        body, name=name, out_shape=[pltpu.HBM(b.shape, b.dtype) for b in bufs],
        in_specs=[HBM] * nb + [SEM, SEM, ANY], out_specs=[HBM] * nb,
        input_output_aliases={i: i for i in range(nb)},
        compiler_params=pltpu.CompilerParams(has_side_effects=DATAFLOW),
    )(*bufs, send_sems, recv_sems, after)
    return list(res[:n_srcs]), list(res[n_srcs:])


def _gather_plan(shapes, axes, n_whole=0):
    def plan(srcs, lands):
        x, y, c, chips = _place()
        out = []
        for a, (shape, axis) in enumerate(zip(shapes, axes)):
            mine = _half(shape, c, axis)
            for ch in chips:
                out.append((srcs[a].at[mine], lands[a].at[(2 * x + y,) + mine], (ch[0], ch[1], c)))
        for a in range(len(shapes), len(shapes) + n_whole):
            for ch in chips:
                out.append((srcs[a], lands[a].at[2 * x + y], (ch[0], ch[1], c)))
        return out
    return plan


def _sibling_plan(shapes, axes):
    def plan(srcs, lands):
        x, y, c, _ = _place()
        return [(srcs[a].at[(slice(None),) + _half(shape, 1 - c, axis)], lands[a], (x, y, 1 - c))
                for a, (shape, axis) in enumerate(zip(shapes, axes))]
    return plan


def _reduce_plan(n_big, n_small):
    def plan(srcs, lands):
        x, y, c, chips = _place()
        out = []
        for a in range(n_big):
            for k, ch in enumerate(chips):
                out.append((srcs[a].at[2 * ch[0] + ch[1]], lands[a].at[k], (ch[0], ch[1], c)))
        for a in range(n_big, n_big + n_small):
            for ch in chips:
                out.append((srcs[a], lands[a].at[2 * x + y], (ch[0], ch[1], c)))
        return out
    return plan


def _row_tile(rows, cols, mult):
    best = mult
    for t in range(mult, rows + 1, mult):
        if rows % t == 0 and t * cols * 4 <= (2 << 20):
            best = t
    return best if rows % best == 0 else rows


COL_TILE = 256


def _half_tiling(hshape, axis, mult):
    hr, hc = hshape
    if axis == 0:
        tr = _row_tile(hr, hc, mult)
        return tr, hc, hr // tr
    return hr, COL_TILE, hc // COL_TILE


def _tile_idx(axis, t):
    return (t, 0) if axis == 0 else (0, t)


def _chip_partial(place, g, t, axis, name):
    hshape = t.shape[1:]
    br, bc, nt = _half_tiling(hshape, axis, 16)

    def body(pl_ref, g_ref, t_ref, pf_ref, pb_ref):
        v = g_ref[...].astype(F32) + t_ref[...].astype(F32)
        pb_ref[...] = v.astype(BF16)

        @pl.when(pl.program_id(1) == pl_ref[0])
        def _():
            pf_ref[...] = v

    blk = (None, br, bc)
    return pl.pallas_call(
        body, name=name,
        grid_spec=pltpu.PrefetchScalarGridSpec(
            num_scalar_prefetch=1, grid=(nt, 4),
            in_specs=[pl.BlockSpec(blk, lambda i, j, p: (j,) + _tile_idx(axis, p[1] * nt + i)),
                      pl.BlockSpec(blk, lambda i, j, p: (j,) + _tile_idx(axis, i))],
            out_specs=[pl.BlockSpec((br, bc), lambda i, j, p: _tile_idx(axis, i)),
                       pl.BlockSpec(blk, lambda i, j, p: (j,) + _tile_idx(axis, i))]),
        out_shape=[jax.ShapeDtypeStruct(hshape, F32), jax.ShapeDtypeStruct((4,) + hshape, BF16)],
        compiler_params=_cp("arbitrary", "arbitrary"),
    )(place, g, t)


def _finish_half(pf, rb, axis, name):
    hshape = pf.shape
    br, bc, nt = _half_tiling(hshape, axis, 16)

    def body(pf_ref, rb_ref, o_ref):
        o_ref[...] = ((pf_ref[...] + rb_ref[0].astype(F32)) + rb_ref[1].astype(F32)) + rb_ref[2].astype(F32)

    return pl.pallas_call(
        body, name=name, grid=(nt,),
        in_specs=[pl.BlockSpec((br, bc), lambda i: _tile_idx(axis, i)),
                  pl.BlockSpec((3, br, bc), lambda i: (0,) + _tile_idx(axis, i))],
        out_specs=pl.BlockSpec((br, bc), lambda i: _tile_idx(axis, i)),
        out_shape=jax.ShapeDtypeStruct(hshape, F32),
        compiler_params=_cp("arbitrary"),
    )(pf, rb)


def _add2(a, b, name):
    def body(a_ref, b_ref, o_ref):
        o_ref[...] = a_ref[...] + b_ref[...]

    return pl.pallas_call(body, name=name, out_shape=jax.ShapeDtypeStruct(a.shape, F32))(a, b)


def _adam_math(w, g, m, v):
    m = ADAM_B1 * m + (1.0 - ADAM_B1) * g
    v = ADAM_B2 * v + (1.0 - ADAM_B2) * (g * g)
    m_hat = m / (1.0 - ADAM_B1 ** ADAM_STEP)
    v_hat = v / (1.0 - ADAM_B2 ** ADAM_STEP)
    return -ADAM_LR * (m_hat / (jnp.sqrt(v_hat) + ADAM_EPS) + ADAM_WD * w), m, v


def _adam_halves(place, w, mine, theirs, m, v, axis, name):
    br, bc, nt = _half_tiling(mine.shape, axis, 8)

    def body(pl_ref, w_ref, a_ref, b_ref, m_ref, v_ref, g_ref, d_ref, mo_ref, vo_ref):
        is_mine = pl.program_id(0) // nt == pl_ref[1]
        g = jnp.where(is_mine, a_ref[...], b_ref[...])
        d, mn, vn = _adam_math(w_ref[...], g, m_ref[...], v_ref[...])
        g_ref[...] = g
        d_ref[...] = d
        mo_ref[...] = mn
        vo_ref[...] = vn

    full = pl.BlockSpec((br, bc), lambda i, p: _tile_idx(axis, i))
    mine_spec = pl.BlockSpec((br, bc), lambda i, p: _tile_idx(axis, jnp.where(i // nt == p[1], i % nt, nt - 1)))
    theirs_spec = pl.BlockSpec((br, bc), lambda i, p: _tile_idx(axis, jnp.where(i // nt == p[1], 0, i % nt)))
    return pl.pallas_call(
        body, name=name,
        grid_spec=pltpu.PrefetchScalarGridSpec(
            num_scalar_prefetch=1, grid=(2 * nt,), in_specs=[full, mine_spec, theirs_spec, full, full],
            out_specs=[full] * 4),
        out_shape=[jax.ShapeDtypeStruct(w.shape, F32)] * 4, compiler_params=_cp("arbitrary"),
    )(place, w, mine, theirs, m, v)


def _add_many(xs, ys, name):
    n = len(xs)

    def body(*refs):
        for i in range(n):
            refs[2 * n + i][...] = refs[i][...] + refs[n + i][...]

    return pl.pallas_call(body, name=name, out_shape=[jax.ShapeDtypeStruct(a.shape, F32) for a in xs])(*xs, *ys)


def _adam_small(place, owns, landed, ws, ms, vs, widths):
    n, nw = len(owns), len(ws)

    def body(pl_ref, *refs):
        own_r, land_r = refs[:n], refs[n:2 * n]
        w_r, m_r, v_r = (refs[2 * n + k * nw:2 * n + (k + 1) * nw] for k in range(3))
        outs = refs[2 * n + 3 * nw:]
        g_o, d_o, m_o, v_o = outs[:n], outs[n:n + nw], outs[n + nw:n + 2 * nw], outs[n + 2 * nw:]
        for me in range(4):
            @pl.when(pl_ref[0] == me)
            def _(me=me):
                for i in range(n):
                    p = [own_r[i][...] if k == me else land_r[i][k] for k in range(4)]
                    g = ((p[0] + p[1]) + p[2]) + p[3]
                    if i < nw and widths[i]:
                        g = g[:, me * widths[i]:(me + 1) * widths[i]]
                    g_o[i][...] = g
                    if i < nw:
                        d, mn, vn = _adam_math(w_r[i][...], g, m_r[i][...], v_r[i][...])
                        d_o[i][...] = d
                        m_o[i][...] = mn
                        v_o[i][...] = vn

    g_shapes = [jax.ShapeDtypeStruct(ws[i].shape if i < nw else owns[i].shape, F32) for i in range(n)]
    w_shapes = [jax.ShapeDtypeStruct(w.shape, F32) for w in ws]
    whole = lambda a: pl.BlockSpec(a.shape, lambda i, p, nd=len(a.shape): (0,) * nd)
    ins = list(owns) + list(landed) + list(ws) + list(ms) + list(vs)
    out_shape = g_shapes + w_shapes * 3
    out = pl.pallas_call(
        body, name="adam_small",
        grid_spec=pltpu.PrefetchScalarGridSpec(num_scalar_prefetch=1, grid=(1,), in_specs=[whole(a) for a in ins],
                                               out_specs=[whole(a) for a in out_shape]),
        out_shape=out_shape, compiler_params=_cp("arbitrary"),
    )(place, *ins)
    return out[:n], out[n:n + nw], out[n + nw:n + 2 * nw], out[n + 2 * nw:]


def kernel(x, g_mix, w_in, b_gate, w_gk_up, b_gk, w_pool_grp, pool_scale, g_gla_head, w_pool_proj, w_gla_proj, w_out, g_ffn, w_up, w_conv, b_conv, w_down, g_final, loss_target, m_g_mix, m_w_in, m_b_gate, m_w_gk_up, m_b_gk, m_w_pool_grp, m_pool_scale, m_g_gla_head, m_w_pool_proj, m_w_gla_proj, m_w_out, m_g_ffn, m_w_up, m_w_conv, m_b_conv, m_w_down, m_g_final, v_g_mix, v_w_in, v_b_gate, v_w_gk_up, v_b_gk, v_w_pool_grp, v_pool_scale, v_g_gla_head, v_w_pool_proj, v_w_gla_proj, v_w_out, v_g_ffn, v_w_up, v_w_conv, v_b_conv, v_w_down, v_g_final):
    s = x.shape[1]
    ts = min(s, 512)
    tm = min(s, 256)
    cx, cy, cc = lax.axis_index("x"), lax.axis_index("y"), lax.axis_index("c")
    chip = 2 * cx + cy
    place = jnp.stack([chip, cc]).astype(jnp.int32)

    big_names = ("w_in", "w_pool_proj", "w_gla_proj", "w_out", "w_up", "w_down")
    axes = (1, 0, 0, 0, 0, 0)
    shards = dict(w_in=jnp.transpose(w_in[0]), w_pool_proj=w_pool_proj[0], w_gla_proj=w_gla_proj[0], w_out=w_out[0],
                  w_up=w_up[0], w_down=w_down[0])
    def fill_own(lands, mine):
        return [lax.dynamic_update_slice(g, o_[None], (chip, 0, 0)) for g, o_ in zip(lands, mine)]

    def gather_start(tag, halves, group_axes, whole, after):
        plan = _gather_plan([o_.shape for o_ in halves], group_axes, len(whole))
        srcs = list(halves) + list(whole)
        handle, token = _split_start("gather_" + tag + "_start", srcs, [((4,) + o_.shape, o_.dtype) for o_ in srcs], plan,
                                     3 * len(srcs), after)
        return (handle, plan, len(halves), len(srcs), group_axes), token

    def gather_finish(tag, started, after):
        handle, plan, n_halves, n, group_axes = started
        mine, lands = _split_wait("gather_" + tag + "_wait", handle, n, plan, after)
        lands[:n_halves] = _gather_share(lands[:n_halves], group_axes, "gather_" + tag + "_share")
        return fill_own(lands, mine)

    in_w, tok = gather_start("in", [shards["w_in"].astype(BF16)], axes[:1], [], g_mix)
    zero = tok[0, 0]
    own = [(shards[n] + zero).astype(BF16) for n in big_names[1:]]
    mix_w, tok = gather_start("mix", own[0:3], axes[1:4], [w_gk_up[0] + zero, w_conv[0] + zero], tok)
    ffn_w, tok = gather_start("ffn", own[3:5], axes[4:6], [], tok)
    xs, tgt = x[0], loss_target[0]
    wgrp = w_pool_grp[0]
    h = _rmsnorm(xs, g_mix, tok, "norm_mix", ts)
    m_in_t, v_in_t = jnp.transpose(m_w_in[0]), jnp.transpose(v_w_in[0])
    h, m_in_t, v_in_t = lax.optimization_barrier((h, m_in_t, v_in_t))
    w_in_t = gather_finish("in", in_w, h)[0].reshape(N_IN, D)
    w_rt = jnp.concatenate([w_in_t[3600:], w_in_t[1536:3584], w_in_t[0:1536], w_in_t[3584:3600],
                            jnp.zeros((128 - GATE_RANK, D), BF16)], axis=0)
    nsh = N_IN // 4

    zr = _matmul_resident(h, w_rt, "in_proj", 1152, transposed=True)
    p, pp = _pool_fwd(zr, wgrp, pool_scale)
    wpp, wgla, wout, wgk4, wconv4 = gather_finish("mix", mix_w, pp)
    wgla, wout = wgla.reshape(D, D), wout.reshape(D, D)
    wgk_full = jnp.transpose(wgk4, (1, 0, 2)).reshape(GATE_RANK, 512)
    wconv_full = jnp.transpose(wconv4, (1, 0, 2)).reshape(3, N_UP)
    wgk_pad = jnp.concatenate([wgk_full, jnp.zeros((128 - GATE_RANK, 512), F32)], axis=0)
    o, og, sp = _gla_fwd(zr, wgk_pad, b_gk, g_gla_head, ts)
    x1, mixed, yp, yg, h2 = _merge_fwd(xs, zr, pp, og, b_gate, wpp, wgla, wout, g_ffn, ts)
    wup, wdown = gather_finish("ffn", ffn_w, x1)
    wdown = wdown.reshape(D_FF, D)
    u = _matmul_resident(h2, wup, "ffn_up", None)
    a, conv_out, dx2, dx2b, loss_part, dgfin = _ffn_down_loss(u, x1, tgt, wconv_full, b_conv, wdown,
                                                              g_final.reshape(1, D), tm)

    du, dbconv, dwconv = _ffn_bwd(dx2b, u, conv_out, wconv_full, wdown, tm)
    dw_down = _matmul_tn(a, dx2b, "dw_down", D, tm=1408)
    dw_up = _matmul_tn(h2, du, "dw_up", 1408, shard_major=True)

    def exchange_start(tag, grads, group_axes, after):
        plan = _sibling_plan([g.shape[1:] for g in grads], group_axes)
        lands = [((4,) + _half_shape(g.shape[1:], ax), g.dtype) for g, ax in zip(grads, group_axes)]
        handle, token = _split_start("sibling_" + tag + "_start", grads, lands, plan, len(grads), after)
        return (handle, plan, len(grads)), token

    def partials(tag, names, group_axes, exchange, after):
        handle, plan, n = exchange
        mine, theirs = _split_wait("sibling_" + tag + "_wait", handle, n, plan, after)
        return zip(*[_chip_partial(place, g, t, ax, "chip_partial_" + nm)
                     for nm, ax, g, t in zip(names, group_axes, mine, theirs)])

    ffn_names, ffn_axes = ("w_up", "w_down"), (0, 0)
    ffn_x, token = exchange_start("ffn", [dw_up, dw_down.reshape(4, 704, D)], ffn_axes, du)
    dx1, dx1b, dgffn = _matmul_nt_normbwd(du, wup, x1, g_ffn, dx2, token, "ffn_up_bwd", ts)
    ffn_pf, ffn_pb = partials("ffn", ffn_names, ffn_axes, ffn_x, dx1b)
    ffn_plan = _reduce_plan(2, False)
    ffn_handle, token = _split_start("reduce_ffn_start", ffn_pb, [((3,) + p.shape[1:], BF16) for p in ffn_pb],
                                     ffn_plan, 6, ffn_pf[0])

    dzg, dyp, dyg, dpp, do, dzog, dbgate, dghead = _merge_bwd(dx1b, zr, yp, yg, o, b_gate, g_gla_head, wpp, wgla, wout,
                                                             token, ts)
    dw_out = _matmul_tn(mixed, dx1b, "dw_out", D)
    dw_gla = _matmul_tn(og, dyg, "dw_gla", D)
    dw_pp = _matmul_tn(pp, dyp, "dw_pp", 256, shard_major=True)

    out_names, out_axes = ("w_pool_proj", "w_gla_proj", "w_out"), (0, 0, 0)
    out_x, token = exchange_start("out", [dw_pp, dw_gla.reshape(4, 256, D), dw_out.reshape(4, 256, D)], out_axes, dpp)
    dzp, dwgrp, dscale = _pool_bwd(p, dpp, wgrp, pool_scale, token)
    out_pf, out_pb = partials("out", out_names, out_axes, out_x, dzp)
    out_plan = _reduce_plan(3, False)
    out_handle, token = _split_start("reduce_out_start", out_pb, [((3,) + p_.shape[1:], BF16) for p_ in out_pb],
                                     out_plan, 9, out_pf[0])
    dq, dk, dv, dgpre = _gla_bwd(zr, do, sp, wgk_pad, b_gk, token, ts)
    dzgk, dwgk, dbgk = _gk_bwd(dgpre, zr, wgk_pad, ts)
    dzr = jnp.concatenate([dzg, dv, dzog, dzp, dq, dk, dzgk], axis=1)
    dw_rt = _matmul_tn(dzr, h, "dw_in", D, tm=1152)

    def grad_rows(lo, hi):
        out = []
        for seg_lo, seg_hi, at in ((0, 1536, OFF_POOL), (1536, 3584, OFF_V), (3584, 3600, OFF_GK), (3600, N_IN, OFF_GATE)):
            a_, b_ = max(lo, seg_lo), min(hi, seg_hi)
            if a_ < b_:
                out.append(dw_rt[at + a_ - seg_lo:at + b_ - seg_lo])
        return jnp.concatenate(out, axis=0)

    dw_in_t = jnp.stack([grad_rows(j * nsh, (j + 1) * nsh) for j in range(4)])

    in_sib = _sibling_exchange([dw_in_t], (1,), [], "sibling_exchange_in")
    in_pf, in_pb = _chip_partial(place, dw_in_t, in_sib[0], 1, "chip_partial_w_in")
    in_plan = _reduce_plan(1, 0)
    in_handle, token = _split_start("reduce_in_start", [in_pb], [((3,) + in_pb.shape[1:], BF16)], in_plan, 3, in_pf)
    grad_x, _, dgmix = _matmul_nt_normbwd(dzr, w_rt, xs, g_mix, dx1, token, "in_proj_bwd", ts, transposed=True)
    small_names = ("g_mix", "b_gate", "w_gk_up", "b_gk", "w_pool_grp", "pool_scale", "g_gla_head", "g_ffn", "w_conv",
                   "b_conv", "g_final")
    small_mine = [dgmix, dbgate, dwgk[:GATE_RANK], dbgk, dwgrp.reshape(4 * 128, 128), dscale, dghead, dgffn, dwconv, dbconv,
                  dgfin, loss_part]
    small_sib = _sibling_exchange([], (), small_mine, "sibling_exchange_small")
    small_chip = _add_many(small_mine, small_sib, "chip_partial_small")
    small_plan = _reduce_plan(0, len(small_chip))
    small_handle, token = _split_start("reduce_small_start", small_chip, [((4,) + a_.shape, F32) for a_ in small_chip],
                                       small_plan, 3 * len(small_chip), small_mine[0])

    ms = dict(w_in=m_in_t, w_pool_proj=m_w_pool_proj[0], w_gla_proj=m_w_gla_proj[0], w_out=m_w_out[0],
              w_up=m_w_up[0], w_down=m_w_down[0])
    vs = dict(w_in=v_in_t, w_pool_proj=v_w_pool_proj[0], w_gla_proj=v_w_gla_proj[0], w_out=v_w_out[0],
              w_up=v_w_up[0], w_down=v_w_down[0])
    grad, delta, new_m, new_v = {}, {}, {}, {}

    def finish_and_update(names, group_axes, part_f, landed, tag):
        halves = [_finish_half(pf, rb, ax, "finish_" + n) for n, ax, pf, rb in zip(names, group_axes, part_f, landed)]
        sib_halves = _sibling_share(halves, "sibling_share_" + tag)
        for n, ax, mine, theirs in zip(names, group_axes, halves, sib_halves):
            res = _adam_halves(place, shards[n], mine, theirs, ms[n], vs[n], ax, "adam_" + n)
            if n == "w_in":
                res = [jnp.transpose(r_) for r_ in res]
            grad[n], delta[n], new_m[n], new_v[n] = [r_[None] for r_ in res]

    _, ffn_landed = _split_wait("reduce_ffn_wait", ffn_handle, 2, ffn_plan, token)
    _, out_landed = _split_wait("reduce_out_wait", out_handle, 3, out_plan, ffn_landed[0])
    finish_and_update(ffn_names + out_names, ffn_axes + out_axes, ffn_pf + out_pf, ffn_landed + out_landed, "rest")
    _, in_landed = _split_wait("reduce_in_wait", in_handle, 1, in_plan, delta["w_out"])
    finish_and_update(("w_in",), (1,), (in_pf,), in_landed, "in")
    small_sent, small_landed = _split_wait("reduce_small_wait", small_handle, len(small_chip), small_plan, delta["w_in"])
    given = dict(g_mix=(g_mix, m_g_mix, v_g_mix), b_gate=(b_gate, m_b_gate, v_b_gate), w_gk_up=(w_gk_up, m_w_gk_up, v_w_gk_up),
                 b_gk=(b_gk, m_b_gk, v_b_gk), w_pool_grp=(w_pool_grp, m_w_pool_grp, v_w_pool_grp),
                 pool_scale=(pool_scale, m_pool_scale, v_pool_scale), g_gla_head=(g_gla_head, m_g_gla_head, v_g_gla_head),
                 g_ffn=(g_ffn, m_g_ffn, v_g_ffn), w_conv=(w_conv, m_w_conv, v_w_conv), b_conv=(b_conv, m_b_conv, v_b_conv),
                 g_final=(g_final, m_g_final, v_g_final))
    flat2 = lambda a: a.reshape(-1, a.shape[-1])
    widths = [dict(w_gk_up=128, w_conv=1408).get(n) for n in small_names]
    totals, ds, mo, vo = _adam_small(place, small_sent, small_landed, *[[flat2(given[n][k]) for n in small_names] for k in range(3)],
                                     widths)
    loss = totals[-1][0, 0]
    for i, n in enumerate(small_names):
        shp = given[n][0].shape
        grad[n], delta[n], new_m[n], new_v[n] = [r_.reshape(shp) for r_ in (totals[i], ds[i], mo[i], vo[i])]

    order = ("g_mix", "w_in", "b_gate", "w_gk_up", "b_gk", "w_pool_grp", "pool_scale", "g_gla_head", "w_pool_proj",
             "w_gla_proj", "w_out", "g_ffn", "w_up", "w_conv", "b_conv", "w_down", "g_final")
    return (loss, grad_x[None], *[grad[n] for n in order], *[delta[n] for n in order], *[new_m[n] for n in order],
            *[new_v[n] for n in order])
```

```python
import functools

import jax
import jax.numpy as jnp
from jax import lax
from jax.experimental import pallas as pl
from jax.experimental.pallas import tpu as pltpu

F32 = jnp.float32
BF16 = jnp.bfloat16
MESH = pl.DeviceIdType.MESH

D = 1024
EPS = 1e-6
CHUNK = 64
POOL_W = 512
POOL_WINDOWS = (2, 4, 8, 16)
HEADS = 4
HK = 128
HV = 256
GATE_RANK = 16
D_FF = 2816
N_UP = 2 * D_FF
N_IN = 5648
QSCALE = HK ** -0.5
N_INR = 5760
OFF_GATE, OFF_V, OFF_OG, OFF_POOL, OFF_Q, OFF_K, OFF_GK = 0, 2048, 3072, 4096, 4608, 5120, 5632

ADAM_LR, ADAM_B1, ADAM_B2, ADAM_EPS, ADAM_WD, ADAM_STEP = 0.001, 0.9, 0.999, 1e-08, 0.01, 10

VMEM_LIMIT = 56 * 1024 * 1024


def _cp(*sem):
    return pltpu.CompilerParams(dimension_semantics=sem if sem else None, vmem_limit_bytes=VMEM_LIMIT)


def _dot(a, b):
    return jnp.dot(a, b, preferred_element_type=F32)


def _dot_nt(a, b):
    return lax.dot_general(a, b, (((1,), (1,)), ((), ())), preferred_element_type=F32)


def _dot_tn(a, b):
    return lax.dot_general(a, b, (((0,), (0,)), ((), ())), preferred_element_type=F32)


def _sigmoid(v):
    return 1.0 / (1.0 + jnp.exp(-v))


def _rows(shape):
    return lax.broadcasted_iota(jnp.int32, shape, 0)


def _pick_row(v, r):
    return jnp.sum(jnp.where(_rows(v.shape) == r, v, 0.0), axis=0, keepdims=True)


def _rmsnorm(x, g, after, name, ts):
    s = x.shape[0]

    def body(x_ref, g_ref, after_ref, h_ref):
        xv = x_ref[...]
        r = lax.rsqrt(jnp.mean(xv * xv, axis=-1, keepdims=True) + EPS)
        h_ref[...] = (xv * r * g_ref[...]).astype(BF16)

    return pl.pallas_call(
        body, name=name, grid=(s // ts,),
        in_specs=[pl.BlockSpec((ts, D), lambda i: (i, 0)), pl.BlockSpec((1, D), lambda i: (0, 0)), ANY],
        out_specs=pl.BlockSpec((ts, D), lambda i: (i, 0)), out_shape=jax.ShapeDtypeStruct((s, D), BF16),
        compiler_params=_cp("arbitrary"),
    )(x, g, after)


MM_ROWS = 512


def _matmul_resident(h, w, name, tn, transposed=False):
    s = h.shape[0]
    if transposed:
        nj = w.shape[0] // tn
        w_spec = pl.BlockSpec((tn, D), lambda j: (j, 0))
    elif w.ndim == 3:
        nj, tn = w.shape[0], w.shape[2]
        w_spec = pl.BlockSpec((None, D, tn), lambda j: (j, 0, 0))
    else:
        nj = w.shape[1] // tn
        w_spec = pl.BlockSpec((D, tn), lambda j: (0, j))
    mm = _dot_nt if transposed else _dot
    rc = min(s, MM_ROWS)

    def body(h_ref, w_ref, z_ref):
        for r0 in range(0, s, rc):
            z_ref[r0:r0 + rc, :] = mm(h_ref[r0:r0 + rc, :], w_ref[...]).astype(BF16)

    return pl.pallas_call(
        body, name=name, grid=(nj,),
        in_specs=[pl.BlockSpec((s, D), lambda j: (0, 0)), w_spec],
        out_specs=pl.BlockSpec((s, tn), lambda j: (0, j)), out_shape=jax.ShapeDtypeStruct((s, nj * tn), BF16),
        compiler_params=_cp("arbitrary"),
    )(h, w)


def _matmul_nt_normbwd(dz, w, x, g, resid, after, name, ts, transposed=False):
    s = x.shape[0]

    def body(dz_ref, w_hbm, x_ref, g_ref, r_ref, after_ref, o_ref, ob_ref, dg_ref, w_ref, sem):
        @pl.when(pl.program_id(0) == 0)
        def _():
            cp = pltpu.make_async_copy(w_hbm, w_ref, sem)
            cp.start()
            cp.wait()
            dg_ref[...] = jnp.zeros_like(dg_ref)

        if transposed:
            dh = _dot(dz_ref[...], w_ref[...])
        else:
            kc = w.shape[2]
            dh = _dot_nt(dz_ref[:, 0:kc], w_ref[0])
            for j in range(1, w.shape[0]):
                dh = dh + _dot_nt(dz_ref[:, j * kc:(j + 1) * kc], w_ref[j])
        xv = x_ref[...]
        r = lax.rsqrt(jnp.mean(xv * xv, axis=-1, keepdims=True) + EPS)
        xh = xv * r
        dg_ref[...] += jnp.sum(dh * xh, axis=0, keepdims=True)
        dxh = dh * g_ref[...]
        out = r_ref[...] + r * (dxh - xh * jnp.mean(dxh * xh, axis=-1, keepdims=True))
        o_ref[...] = out
        ob_ref[...] = out.astype(BF16)

    row = lambda i: (i, 0)
    kdim = dz.shape[1]
    return pl.pallas_call(
        body, name=name, grid=(s // ts,),
        in_specs=[pl.BlockSpec((ts, kdim), row), ANY, pl.BlockSpec((ts, D), row),
                  pl.BlockSpec((1, D), lambda i: (0, 0)), pl.BlockSpec((ts, D), row), ANY],
        out_specs=[pl.BlockSpec((ts, D), row), pl.BlockSpec((ts, D), row), pl.BlockSpec((1, D), lambda i: (0, 0))],
        out_shape=[jax.ShapeDtypeStruct((s, D), F32), jax.ShapeDtypeStruct((s, D), BF16),
                   jax.ShapeDtypeStruct((1, D), F32)],
        scratch_shapes=[pltpu.VMEM(w.shape, BF16), pltpu.SemaphoreType.DMA],
        compiler_params=_cp("arbitrary"),
    )(dz, w, x, g, resid, after)


def _matmul_tn(a, b, name, tn, shard_major=False, tm=None):
    s, m = a.shape
    n = b.shape[1]
    tm = m if tm is None else tm
    ni, nj = m // tm, n // tn

    def body(a_ref, b_ref, o_ref):
        o_ref[...] = _dot_tn(a_ref[...], b_ref[...]).astype(BF16)

    if shard_major:
        out_spec = pl.BlockSpec((None, tm, tn), lambda i, j: (j, i, 0))
        out_shape = jax.ShapeDtypeStruct((nj, m, tn), BF16)
    else:
        out_spec = pl.BlockSpec((tm, tn), lambda i, j: (i, j))
        out_shape = jax.ShapeDtypeStruct((m, n), BF16)
    return pl.pallas_call(
        body, name=name, grid=(ni, nj),
        in_specs=[pl.BlockSpec((s, tm), lambda i, j: (0, i)), pl.BlockSpec((s, tn), lambda i, j: (0, j))],
        out_specs=out_spec, out_shape=out_shape,
        compiler_params=_cp("arbitrary", "arbitrary"),
    )(a, b)


def _pool_fwd(zr, wgrp, scale):
    s = zr.shape[0]

    def body(u_ref, w_ref, sc_ref, p_ref, pp_ref):
        row = _rows((s, 128))
        for gi, win in enumerate(POOL_WINDOWS):
            cs = slice(gi * 128, (gi + 1) * 128)
            u = u_ref[:, cs].astype(F32)
            acc, k = u, 1
            while k < win:
                acc = acc + jnp.where(row >= k, pltpu.roll(acc, k, 0), 0.0)
                k *= 2
            cnt = jnp.minimum(row + 1, win).astype(F32)
            p = (acc / cnt - u).astype(BF16)
            p_ref[:, cs] = p
            pp_ref[:, cs] = (_dot(p, w_ref[gi].astype(BF16)) * sc_ref[:, cs]).astype(BF16)

    return pl.pallas_call(
        body, name="pool_fwd", grid=(1,),
        in_specs=[pl.BlockSpec((s, POOL_W), lambda i: (0, OFF_POOL // POOL_W)),
                  pl.BlockSpec((4, 128, 128), lambda i: (0, 0, 0)), pl.BlockSpec((1, POOL_W), lambda i: (0, 0))],
        out_specs=[pl.BlockSpec((s, POOL_W), lambda i: (0, 0))] * 2,
        out_shape=[jax.ShapeDtypeStruct((s, POOL_W), BF16)] * 2,
        compiler_params=_cp("arbitrary"),
    )(zr, wgrp, scale)


def _pool_bwd(p, dpp, wgrp, scale, after):
    s = p.shape[0]

    def body(p_ref, dpp_ref, w_ref, sc_ref, after_ref, dz_ref, dw_ref, dsc_ref):
        row = _rows((s, 128))
        for gi, win in enumerate(POOL_WINDOWS):
            cs = slice(gi * 128, (gi + 1) * 128)
            pv = p_ref[:, cs]
            wb = w_ref[gi].astype(BF16)
            dpp_v = dpp_ref[:, cs].astype(F32)
            dsc_ref[:, cs] = jnp.sum(dpp_v * _dot(pv, wb), axis=0, keepdims=True)
            dpm = (dpp_v * sc_ref[:, cs]).astype(BF16)
            dw_ref[gi] = _dot_tn(pv, dpm)
            dp = _dot_nt(dpm, wb)
            cnt = jnp.minimum(row + 1, win).astype(F32)
            acc, k = dp / cnt, 1
            while k < win:
                acc = acc + jnp.where(row < s - k, pltpu.roll(acc, s - k, 0), 0.0)
                k *= 2
            dz_ref[:, cs] = (acc - dp).astype(BF16)

    full = lambda i: (0, 0)
    return pl.pallas_call(
        body, name="pool_bwd", grid=(1,),
        in_specs=[pl.BlockSpec((s, POOL_W), full), pl.BlockSpec((s, POOL_W), full),
                  pl.BlockSpec((4, 128, 128), lambda i: (0, 0, 0)), pl.BlockSpec((1, POOL_W), full), ANY],
        out_specs=[pl.BlockSpec((s, POOL_W), full), pl.BlockSpec((4, 128, 128), lambda i: (0, 0, 0)),
                   pl.BlockSpec((1, POOL_W), full)],
        out_shape=[jax.ShapeDtypeStruct((s, POOL_W), BF16), jax.ShapeDtypeStruct((4, 128, 128), F32),
                   jax.ShapeDtypeStruct((1, POOL_W), F32)],
        compiler_params=_cp("arbitrary"),
    )(p, dpp, wgrp, scale, after)


def _gla_decay(zgk_ref, wgk_ref, bgk_ref, rb):
    g = _dot(zgk_ref[...], wgk_ref[...].astype(BF16)) + bgk_ref[...]
    la = (jnp.minimum(g, 0.0) - jnp.log(1.0 + jnp.exp(-jnp.abs(g)))) * (1.0 / 16.0)
    rowm = _rows(la.shape) & (CHUNK - 1)
    bc, k = la, 1
    while k < CHUNK:
        bc = bc + jnp.where(rowm >= k, pltpu.roll(bc, k, 0), 0.0)
        k *= 2
    return g, jnp.exp(bc), jnp.exp(-bc)


GLA_HB = 4


def _gla_specs(rb, rmap):
    wk, wv = GLA_HB * HK, GLA_HB * HV
    return [pl.BlockSpec((rb, wk), lambda h, r: (rmap(h, r), OFF_Q // wk + h)),
            pl.BlockSpec((rb, wk), lambda h, r: (rmap(h, r), OFF_K // wk + h)),
            pl.BlockSpec((rb, wv), lambda h, r: (rmap(h, r), OFF_V // wv + h)),
            pl.BlockSpec((rb, 128), lambda h, r: (rmap(h, r), OFF_GK // 128))]


def _gla_fwd(zr, wgk, bgk, ghead, rb):
    s = zr.shape[0]
    nc = rb // CHUNK
    wk, wv = GLA_HB * HK, GLA_HB * HV

    def body(q_ref, k_ref, v_ref, zgk_ref, zog_ref, wgk_ref, bgk_ref, gh_ref, o_ref, og_ref, sp_ref, st_ref):
        @pl.when(pl.program_id(1) == 0)
        def _():
            st_ref[...] = jnp.zeros_like(st_ref)

        _, e_pos, e_neg = _gla_decay(zgk_ref, wgk_ref, bgk_ref, rb)
        lower = _rows((CHUNK, CHUNK)) >= lax.broadcasted_iota(jnp.int32, (CHUNK, CHUNK), 1)
        for c in range(nc):
            sl = slice(c * CHUNK, (c + 1) * CHUNK)
            for hh in range(GLA_HB):
                ck, cv = slice(hh * HK, (hh + 1) * HK), slice(hh * HV, (hh + 1) * HV)
                q = q_ref[sl, ck].astype(F32) * QSCALE
                k = k_ref[sl, ck].astype(F32)
                v = v_ref[sl, cv]
                ec, fc = e_pos[sl, ck], e_neg[sl, ck]
                qfw = (q * ec).astype(BF16)
                kfw_f = k * fc
                s_fw = _dot_nt(qfw, kfw_f.astype(BF16))
                s_bw = _dot_nt((q * fc).astype(BF16), (k * ec).astype(BF16))
                pm = jnp.where(lower, s_fw, s_bw).astype(BF16)
                st = st_ref[hh]
                stb = st.astype(BF16)
                sp_ref[c, hh] = stb
                o = _dot(pm, v) + _dot_nt(qfw, stb)
                e_last = _pick_row(ec, CHUNK - 1)
                kdec = (kfw_f * e_last).astype(BF16)
                st_ref[hh] = st * e_last + _dot_tn(v, kdec)
                r = lax.rsqrt(jnp.mean(o * o, axis=-1, keepdims=True) + EPS)
                zo = zog_ref[sl, cv].astype(F32)
                o_ref[sl, cv] = o.astype(BF16)
                og_ref[sl, cv] = (o * r * gh_ref[...] * zo * _sigmoid(zo)).astype(BF16)

    rmap = lambda h, r: r
    return pl.pallas_call(
        body, name="gla_fwd", grid=(HEADS // GLA_HB, s // rb),
        in_specs=_gla_specs(rb, rmap) + [
            pl.BlockSpec((rb, wv), lambda h, r: (r, OFF_OG // wv + h)),
            pl.BlockSpec((128, wk), lambda h, r: (0, h)), pl.BlockSpec((1, wk), lambda h, r: (0, h)),
            pl.BlockSpec((1, HV), lambda h, r: (0, 0))],
        out_specs=[pl.BlockSpec((rb, wv), lambda h, r: (r, h)), pl.BlockSpec((rb, wv), lambda h, r: (r, h)),
                   pl.BlockSpec((nc, GLA_HB, HV, HK), lambda h, r: (r, h, 0, 0))],
        out_shape=[jax.ShapeDtypeStruct((s, D), BF16), jax.ShapeDtypeStruct((s, D), BF16),
                   jax.ShapeDtypeStruct((s // CHUNK, HEADS, HV, HK), BF16)],
        scratch_shapes=[pltpu.VMEM((GLA_HB, HV, HK), F32)],
        compiler_params=_cp("arbitrary", "arbitrary"),
    )(zr, zr, zr, zr, zr, wgk, bgk, ghead)


def _gla_bwd(zr, do, sp, wgk, bgk, after, rb):
    s = zr.shape[0]
    nc = rb // CHUNK
    nr = s // rb
    wk, wv = GLA_HB * HK, GLA_HB * HV

    def body(q_ref, k_ref, v_ref, zgk_ref, do_ref, sp_ref, wgk_ref, bgk_ref, after_ref, dq_ref, dk_ref, dv_ref, dg_ref,
             gt_ref, dbc_ref):
        @pl.when(pl.program_id(1) == 0)
        def _():
            gt_ref[...] = jnp.zeros_like(gt_ref)

        g, e_pos, e_neg = _gla_decay(zgk_ref, wgk_ref, bgk_ref, rb)
        lower = _rows((CHUNK, CHUNK)) >= lax.broadcasted_iota(jnp.int32, (CHUNK, CHUNK), 1)
        is_last = _rows((CHUNK, HK)) == CHUNK - 1
        for c in reversed(range(nc)):
            sl = slice(c * CHUNK, (c + 1) * CHUNK)
            for hh in range(GLA_HB):
                ck, cv = slice(hh * HK, (hh + 1) * HK), slice(hh * HV, (hh + 1) * HV)
                q = q_ref[sl, ck].astype(F32) * QSCALE
                k = k_ref[sl, ck].astype(F32)
                v = v_ref[sl, cv]
                dov = do_ref[sl, cv]
                ec, fc = e_pos[sl, ck], e_neg[sl, ck]
                qfw_f, kfw_f, qbw_f, kbw_f = q * ec, k * fc, q * fc, k * ec
                qfw, kfw, qbw, kbw = qfw_f.astype(BF16), kfw_f.astype(BF16), qbw_f.astype(BF16), kbw_f.astype(BF16)
                pm = jnp.where(lower, _dot_nt(qfw, kfw), _dot_nt(qbw, kbw)).astype(BF16)
                e_last = _pick_row(ec, CHUNK - 1)
                kdec = (kfw_f * e_last).astype(BF16)
                gt = gt_ref[hh]
                gtb = gt.astype(BF16)
                spv = sp_ref[c, hh]
                dp = _dot_nt(dov, v)
                dv_ref[sl, cv] = (_dot_tn(pm, dov) + _dot_nt(kdec, gtb)).astype(BF16)
                ds_fw = jnp.where(lower, dp, 0.0).astype(BF16)
                ds_bw = jnp.where(lower, 0.0, dp).astype(BF16)
                dqfw = _dot(ds_fw, kfw) + _dot(dov, spv)
                dkfw = _dot_tn(ds_fw, qfw)
                dqbw = _dot(ds_bw, kbw)
                dkbw = _dot_tn(ds_bw, qbw)
                dkdec = _dot(v, gtb)
                de_last = (jnp.sum(gt * spv.astype(F32), axis=0, keepdims=True)
                           + jnp.sum(dkdec * kfw_f, axis=0, keepdims=True))
                dkfw = dkfw + dkdec * e_last
                dq_ref[sl, ck] = ((dqfw * ec + dqbw * fc) * QSCALE).astype(BF16)
                dk_ref[sl, ck] = (dkfw * fc + dkbw * ec).astype(BF16)
                dbc = dqfw * qfw_f - dqbw * qbw_f + dkbw * kbw_f - dkfw * kfw_f
                dbc_ref[sl, ck] = dbc + jnp.where(is_last, de_last * e_last, 0.0)
                gt_ref[hh] = _dot_tn(dov, qfw) + gt * e_last
        rowm = _rows((rb, wk)) & (CHUNK - 1)
        dla, kk = dbc_ref[...], 1
        while kk < CHUNK:
            dla = dla + jnp.where(rowm < CHUNK - kk, pltpu.roll(dla, rb - kk, 0), 0.0)
            kk *= 2
        dg_ref[...] = dla * (1.0 / 16.0) * _sigmoid(-g)

    rmap = lambda h, r: nr - 1 - r
    rev = lambda h, r: (nr - 1 - r, h)
    return pl.pallas_call(
        body, name="gla_bwd", grid=(HEADS // GLA_HB, nr),
        in_specs=_gla_specs(rb, rmap) + [
            pl.BlockSpec((rb, wv), rev),
            pl.BlockSpec((nc, GLA_HB, HV, HK), lambda h, r: (nr - 1 - r, h, 0, 0)),
            pl.BlockSpec((128, wk), lambda h, r: (0, h)), pl.BlockSpec((1, wk), lambda h, r: (0, h)), ANY],
        out_specs=[pl.BlockSpec((rb, wk), rev), pl.BlockSpec((rb, wk), rev), pl.BlockSpec((rb, wv), rev),
                   pl.BlockSpec((rb, wk), rev)],
        out_shape=[jax.ShapeDtypeStruct((s, HEADS * HK), BF16), jax.ShapeDtypeStruct((s, HEADS * HK), BF16),
                   jax.ShapeDtypeStruct((s, D), BF16), jax.ShapeDtypeStruct((s, HEADS * HK), F32)],
        scratch_shapes=[pltpu.VMEM((GLA_HB, HV, HK), F32), pltpu.VMEM((rb, wk), F32)],
        compiler_params=_cp("arbitrary", "arbitrary"),
    )(zr, zr, zr, zr, do, sp, wgk, bgk, after)


def _gk_bwd(dgpre, zr, wgk, after, ts):
    s = zr.shape[0]

    def body(dg_ref, zgk_ref, w_ref, after_ref, dz_ref, dw_ref, db_ref):
        @pl.when(pl.program_id(0) == 0)
        def _():
            dw_ref[...] = jnp.zeros_like(dw_ref)
            db_ref[...] = jnp.zeros_like(db_ref)

        dg = dg_ref[...]
        dgb = dg.astype(BF16)
        dz_ref[...] = _dot_nt(dgb, w_ref[...].astype(BF16)).astype(BF16)
        dw_ref[...] += _dot_tn(zgk_ref[...], dgb)
        db_ref[...] += jnp.sum(dg, axis=0, keepdims=True)

    return pl.pallas_call(
        body, name="gk_bwd", grid=(s // ts,),
        in_specs=[pl.BlockSpec((ts, 512), lambda i: (i, 0)), pl.BlockSpec((ts, 128), lambda i: (i, OFF_GK // 128)),
                  pl.BlockSpec((128, 512), lambda i: (0, 0)), ANY],
        out_specs=[pl.BlockSpec((ts, 128), lambda i: (i, 0)), pl.BlockSpec((128, 512), lambda i: (0, 0)),
                   pl.BlockSpec((1, 512), lambda i: (0, 0))],
        out_shape=[jax.ShapeDtypeStruct((s, 128), BF16), jax.ShapeDtypeStruct((128, 512), F32),
                   jax.ShapeDtypeStruct((1, 512), F32)],
        compiler_params=_cp("arbitrary"),
    )(dgpre, zr, wgk, after)


def _merge_fwd(x, zr, pp, og, bgate, wpp, wgla, wout, gffn, ts):
    s = x.shape[0]

    def body(x_ref, z0_ref, z1_ref, pp_ref, og_ref, bg_ref, wpp_ref, wgla_ref, wout_ref, gf_ref,
             x1_ref, mix_ref, yp_ref, yg_ref, h2_ref):
        ppv = pp_ref[...]
        yp = jnp.concatenate([_dot(ppv, wpp_ref[j]) for j in range(4)], axis=1)
        yg = _dot(og_ref[...], wgla_ref[...])
        g0 = _sigmoid(z0_ref[...].astype(F32) + bg_ref[:, :D])
        g1 = _sigmoid(z1_ref[...].astype(F32) + bg_ref[:, D:])
        mixed = (g0 * yp + g1 * yg).astype(BF16)
        x1 = x_ref[...] + _dot(mixed, wout_ref[...])
        x1_ref[...] = x1
        mix_ref[...] = mixed
        yp_ref[...] = yp.astype(BF16)
        yg_ref[...] = yg.astype(BF16)
        r = lax.rsqrt(jnp.mean(x1 * x1, axis=-1, keepdims=True) + EPS)
        h2_ref[...] = (x1 * r * gf_ref[...]).astype(BF16)

    row = lambda i: (i, 0)
    const2 = lambda i: (0, 0)
    return pl.pallas_call(
        body, name="merge_fwd", grid=(s // ts,),
        in_specs=[pl.BlockSpec((ts, D), row), pl.BlockSpec((ts, D), lambda i: (i, 0)), pl.BlockSpec((ts, D), lambda i: (i, 1)),
                  pl.BlockSpec((ts, POOL_W), row), pl.BlockSpec((ts, D), row), pl.BlockSpec((1, 2 * D), const2),
                  pl.BlockSpec((4, POOL_W, 256), lambda i: (0, 0, 0)), pl.BlockSpec((D, D), const2),
                  pl.BlockSpec((D, D), const2), pl.BlockSpec((1, D), const2)],
        out_specs=[pl.BlockSpec((ts, D), row)] * 5,
        out_shape=[jax.ShapeDtypeStruct((s, D), F32)] + [jax.ShapeDtypeStruct((s, D), BF16)] * 4,
        compiler_params=_cp("arbitrary"),
    )(x, zr, zr, pp, og, bgate, wpp, wgla, wout, gffn)


def _merge_bwd(dx1b, zr, yp, yg, o, bgate, ghead, wpp, wgla, wout, after, ts):
    s = dx1b.shape[0]

    def body(dx_ref, z0_ref, z1_ref, zog_ref, yp_ref, yg_ref, o_ref, bg_ref, gh_ref, wpp_ref, wgla_ref, wout_ref, after_ref,
             dzg_ref, dyp_ref, dyg_ref, dpp_ref, do_ref, dzog_ref, dbg_ref, dgh_ref):
        @pl.when(pl.program_id(0) == 0)
        def _():
            dbg_ref[...] = jnp.zeros_like(dbg_ref)
            dgh_ref[...] = jnp.zeros_like(dgh_ref)

        dmix = _dot_nt(dx_ref[...], wout_ref[...])
        g0 = _sigmoid(z0_ref[...].astype(F32) + bg_ref[:, :D])
        g1 = _sigmoid(z1_ref[...].astype(F32) + bg_ref[:, D:])
        dypb = (dmix * g0).astype(BF16)
        dygb = (dmix * g1).astype(BF16)
        dz0 = dmix * yp_ref[...].astype(F32) * g0 * (1.0 - g0)
        dz1 = dmix * yg_ref[...].astype(F32) * g1 * (1.0 - g1)
        dzg_ref[:, :D] = dz0.astype(BF16)
        dzg_ref[:, D:] = dz1.astype(BF16)
        dbg_ref[:, :D] += jnp.sum(dz0, axis=0, keepdims=True)
        dbg_ref[:, D:] += jnp.sum(dz1, axis=0, keepdims=True)
        dyp_ref[...] = dypb
        dyg_ref[...] = dygb
        dpp = _dot_nt(dypb[:, 0:256], wpp_ref[0])
        for j in range(1, 4):
            dpp = dpp + _dot_nt(dypb[:, j * 256:(j + 1) * 256], wpp_ref[j])
        dpp_ref[...] = dpp.astype(BF16)
        dog = _dot_nt(dygb, wgla_ref[...])
        gh = gh_ref[...]
        dgh = jnp.zeros((1, HV), F32)
        for h in range(HEADS):
            cs = slice(h * HV, (h + 1) * HV)
            ov = o_ref[:, cs].astype(F32)
            r = lax.rsqrt(jnp.mean(ov * ov, axis=-1, keepdims=True) + EPS)
            oh = ov * r
            zo = zog_ref[:, cs].astype(F32)
            sg = _sigmoid(zo)
            dog_h = dog[:, cs]
            don = dog_h * zo * sg
            dzog_ref[:, cs] = (dog_h * oh * gh * sg * (1.0 + zo * (1.0 - sg))).astype(BF16)
            dgh = dgh + jnp.sum(don * oh, axis=0, keepdims=True)
            doh = don * gh
            do_ref[:, cs] = (r * (doh - oh * jnp.mean(doh * oh, axis=-1, keepdims=True))).astype(BF16)
        dgh_ref[...] += dgh

    row = lambda i: (i, 0)
    const2 = lambda i: (0, 0)
    return pl.pallas_call(
        body, name="merge_bwd", grid=(s // ts,),
        in_specs=[pl.BlockSpec((ts, D), row), pl.BlockSpec((ts, D), lambda i: (i, 0)), pl.BlockSpec((ts, D), lambda i: (i, 1)),
                  pl.BlockSpec((ts, D), lambda i: (i, OFF_OG // D)), pl.BlockSpec((ts, D), row), pl.BlockSpec((ts, D), row),
                  pl.BlockSpec((ts, D), row), pl.BlockSpec((1, 2 * D), const2), pl.BlockSpec((1, HV), const2),
                  pl.BlockSpec((4, POOL_W, 256), lambda i: (0, 0, 0)), pl.BlockSpec((D, D), const2),
                  pl.BlockSpec((D, D), const2), ANY],
        out_specs=[pl.BlockSpec((ts, 2 * D), row), pl.BlockSpec((ts, D), row), pl.BlockSpec((ts, D), row),
                   pl.BlockSpec((ts, POOL_W), row), pl.BlockSpec((ts, D), row), pl.BlockSpec((ts, D), row),
                   pl.BlockSpec((1, 2 * D), const2), pl.BlockSpec((1, HV), const2)],
        out_shape=[jax.ShapeDtypeStruct((s, 2 * D), BF16), jax.ShapeDtypeStruct((s, D), BF16),
                   jax.ShapeDtypeStruct((s, D), BF16), jax.ShapeDtypeStruct((s, POOL_W), BF16),
                   jax.ShapeDtypeStruct((s, D), BF16), jax.ShapeDtypeStruct((s, D), BF16),
                   jax.ShapeDtypeStruct((1, 2 * D), F32), jax.ShapeDtypeStruct((1, HV), F32)],
        compiler_params=_cp("arbitrary"),
    )(dx1b, zr, zr, zr, yp, yg, o, bgate, ghead, wpp, wgla, wout, after)


HALO = 16
CCH = 256


def _conv_taps(u_ref, halo_ref, cs, first, ts):
    u = u_ref[:, cs].astype(F32)
    hal = halo_ref[:, cs].astype(F32)
    h1 = jnp.where(first, 0.0, _pick_row(hal, HALO - 1))
    h2 = jnp.where(first, 0.0, _pick_row(hal, HALO - 2))
    row8 = _rows((8, u.shape[1]))
    r1, r2 = pltpu.roll(u, 1, 0), pltpu.roll(u, 2, 0)
    r1 = jnp.concatenate([jnp.where(row8 == 0, h1, r1[:8]), r1[8:]], axis=0)
    r2 = jnp.concatenate([jnp.where(row8 == 0, h2, jnp.where(row8 == 1, h1, r2[:8])), r2[8:]], axis=0)
    return u, r1, r2


def _ffn_down_loss(u, x1, tgt, wconv, bconv, wdown, gfin, ts):
    s = x1.shape[0]

    def body(u_ref, halo_ref, x1_ref, t_ref, wc_ref, bc_ref, wd_ref, gf_ref, a_ref, c_ref, dx_ref, dxb_ref, ls_ref,
             dgf_ref):
        i = pl.program_id(0)

        @pl.when(i == 0)
        def _():
            ls_ref[...] = jnp.zeros_like(ls_ref)
            dgf_ref[...] = jnp.zeros_like(dgf_ref)

        first = i == 0
        acc = x1_ref[...]
        for hf in range(D_FF // CCH):
            cg = slice(hf * CCH, (hf + 1) * CCH)
            cv = slice(D_FF + hf * CCH, D_FF + (hf + 1) * CCH)
            vals = []
            for cs in (cg, cv):
                u0, u1, u2 = _conv_taps(u_ref, halo_ref, cs, first, ts)
                vals.append(bc_ref[:, cs] + wc_ref[0:1, cs] * u2 + wc_ref[1:2, cs] * u1 + wc_ref[2:3, cs] * u0)
                c_ref[:, cs] = vals[-1].astype(BF16)
            a = (vals[0] * _sigmoid(vals[0]) * vals[1]).astype(BF16)
            a_ref[:, cg] = a
            acc = acc + _dot(a, wd_ref[cg, :])
        r = lax.rsqrt(jnp.mean(acc * acc, axis=-1, keepdims=True) + EPS)
        xh = acc * r
        gf = gf_ref[...]
        err = xh * gf - t_ref[...]
        ls_ref[...] += (0.5 / D) * jnp.sum(jnp.sum(err * err, axis=-1, keepdims=True), axis=0, keepdims=True)
        dy = err * (1.0 / D)
        dgf_ref[...] += jnp.sum(dy * xh, axis=0, keepdims=True)
        dxh = dy * gf
        dx = r * (dxh - xh * jnp.mean(dxh * xh, axis=-1, keepdims=True))
        dx_ref[...] = dx
        dxb_ref[...] = dx.astype(BF16)

    row = lambda i: (i, 0)
    const2 = lambda i: (0, 0)
    return pl.pallas_call(
        body, name="ffn_down_loss", grid=(s // ts,),
        in_specs=[pl.BlockSpec((ts, N_UP), row),
                  pl.BlockSpec((HALO, N_UP), lambda i: (jnp.maximum(i * (ts // HALO) - 1, 0), 0)),
                  pl.BlockSpec((ts, D), row), pl.BlockSpec((ts, D), row), pl.BlockSpec((3, N_UP), const2),
                  pl.BlockSpec((1, N_UP), const2), pl.BlockSpec((D_FF, D), const2), pl.BlockSpec((1, D), const2)],
        out_specs=[pl.BlockSpec((ts, D_FF), row), pl.BlockSpec((ts, N_UP), row), pl.BlockSpec((ts, D), row),
                   pl.BlockSpec((ts, D), row), pl.BlockSpec((1, 128), const2), pl.BlockSpec((1, D), const2)],
        out_shape=[jax.ShapeDtypeStruct((s, D_FF), BF16), jax.ShapeDtypeStruct((s, N_UP), BF16),
                   jax.ShapeDtypeStruct((s, D), F32), jax.ShapeDtypeStruct((s, D), BF16),
                   jax.ShapeDtypeStruct((1, 128), F32), jax.ShapeDtypeStruct((1, D), F32)],
        compiler_params=_cp("arbitrary"),
    )(u, u, x1, tgt, wconv, bconv, wdown, gfin)


def _ffn_bwd(dx2b, u, c, wconv, wdown, ts):
    s = dx2b.shape[0]
    nt = s // ts

    def body(dx_ref, u_ref, c_ref, wc_ref, wd_ref, du_ref, db_ref, dw_ref, nxt_ref):
        @pl.when(pl.program_id(0) == 0)
        def _():
            db_ref[...] = jnp.zeros_like(db_ref)
            dw_ref[...] = jnp.zeros_like(dw_ref)
            nxt_ref[...] = jnp.zeros_like(nxt_ref)

        dxv = dx_ref[...]
        row8 = _rows((8, CCH))
        for hf in range(D_FF // CCH):
            cg = slice(hf * CCH, (hf + 1) * CCH)
            cv = slice(D_FF + hf * CCH, D_FF + (hf + 1) * CCH)
            da = _dot_nt(dxv, wd_ref[cg, :])
            gate = c_ref[:, cg].astype(F32)
            val = c_ref[:, cv].astype(F32)
            sg = _sigmoid(gate)
            dcs = (da * val * sg * (1.0 + gate * (1.0 - sg)), da * gate * sg)
            for cs, dc in zip((cg, cv), dcs):
                n1 = nxt_ref[0:1, cs]
                n2 = nxt_ref[1:2, cs]
                r1, r2 = pltpu.roll(dc, ts - 1, 0), pltpu.roll(dc, ts - 2, 0)
                f1 = jnp.concatenate([r1[:ts - 8], jnp.where(row8 == 7, n1, r1[ts - 8:])], axis=0)
                f2 = jnp.concatenate([r2[:ts - 8], jnp.where(row8 == 7, n2, jnp.where(row8 == 6, n1, r2[ts - 8:]))], axis=0)
                uv = u_ref[:, cs].astype(F32)
                db_ref[:, cs] += jnp.sum(dc, axis=0, keepdims=True)
                dw_ref[0:1, cs] += jnp.sum(f2 * uv, axis=0, keepdims=True)
                dw_ref[1:2, cs] += jnp.sum(f1 * uv, axis=0, keepdims=True)
                dw_ref[2:3, cs] += jnp.sum(dc * uv, axis=0, keepdims=True)
                du_ref[:, cs] = (wc_ref[2:3, cs] * dc + wc_ref[1:2, cs] * f1 + wc_ref[0:1, cs] * f2).astype(BF16)
                nxt_ref[:, cs] = dc[0:8, :]

    rev = lambda i: (nt - 1 - i, 0)
    const2 = lambda i: (0, 0)
    return pl.pallas_call(
        body, name="ffn_bwd", grid=(nt,),
        in_specs=[pl.BlockSpec((ts, D), rev), pl.BlockSpec((ts, N_UP), rev), pl.BlockSpec((ts, N_UP), rev),
                  pl.BlockSpec((3, N_UP), const2), pl.BlockSpec((D_FF, D), const2)],
        out_specs=[pl.BlockSpec((ts, N_UP), rev), pl.BlockSpec((1, N_UP), const2), pl.BlockSpec((3, N_UP), const2)],
        out_shape=[jax.ShapeDtypeStruct((s, N_UP), BF16), jax.ShapeDtypeStruct((1, N_UP), F32),
                   jax.ShapeDtypeStruct((3, N_UP), F32)],
        scratch_shapes=[pltpu.VMEM((8, N_UP), F32)],
        compiler_params=_cp("arbitrary"),
    )(dx2b, u, c, wconv, wdown)


ANY = pl.BlockSpec(memory_space=pl.ANY)


def _place():
    x, y, c = lax.axis_index("x"), lax.axis_index("y"), lax.axis_index("c")
    chips = [(1 - x, y), (x, 1 - y), (1 - x, 1 - y)]
    return x, y, c, chips


def _half(shape, c, axis):
    size = shape[axis] // 2
    cut = pl.ds(pl.multiple_of(c * size, 8 if axis == 0 else 128), size)
    return (cut, slice(None)) if axis == 0 else (slice(None), cut)


def _half_shape(shape, axis):
    return (shape[0] // 2, shape[1]) if axis == 0 else (shape[0], shape[1] // 2)


def _remote(src, dst, send_sems, recv_sems, k, to):
    return pltpu.make_async_remote_copy(src_ref=src, dst_ref=dst, send_sem=send_sems.at[k], recv_sem=recv_sems.at[k],
                                        device_id=to, device_id_type=MESH)


def _sibling_exchange(grads, axes, smalls, name):
    nb = len(grads)
    n = nb + len(smalls)

    def body(*refs):
        ins, outs = refs[:n], refs[n:2 * n]
        send_sems, recv_sems = refs[2 * n:]
        x, y, c, _ = _place()
        sib = (x, y, 1 - c)
        cps = []
        for a in range(nb):
            theirs = _half(grads[a].shape[1:], 1 - c, axes[a])
            cps.append(_remote(ins[a].at[(slice(None),) + theirs], outs[a], send_sems, recv_sems, a, sib))
        for a in range(nb, n):
            cps.append(_remote(ins[a], outs[a], send_sems, recv_sems, a, sib))
        for cp in cps:
            cp.start()
        for cp in cps:
            cp.wait()

    out_shape = [jax.ShapeDtypeStruct((4,) + _half_shape(g.shape[1:], ax), g.dtype) for g, ax in zip(grads, axes)]
    out_shape += [jax.ShapeDtypeStruct(a.shape, F32) for a in smalls]
    return pl.pallas_call(
        body, name=name, in_specs=[ANY] * n, out_specs=[ANY] * n, out_shape=out_shape,
        scratch_shapes=[pltpu.SemaphoreType.DMA((n,)), pltpu.SemaphoreType.DMA((n,))],
        compiler_params=pltpu.CompilerParams(has_side_effects=True),
    )(*grads, *smalls)


def _gather_share(lands, axes, name):
    n = len(lands)

    def body(*refs):
        outs = refs[n:2 * n]
        send_sems, recv_sems = refs[2 * n:]
        x, y, c, chips = _place()
        sib = (x, y, 1 - c)
        cps = []
        for a in range(n):
            mine = _half(lands[a].shape[1:], c, axes[a])
            for k, ch in enumerate(chips):
                landed = outs[a].at[(2 * ch[0] + ch[1],) + mine]
                cps.append(_remote(landed, landed, send_sems, recv_sems, 3 * a + k, sib))
        for cp in cps:
            cp.start()
        for a in range(n):
            other = _half(lands[a].shape[1:], 1 - c, axes[a])
            for k, ch in enumerate(chips):
                landed = outs[a].at[(2 * ch[0] + ch[1],) + other]
                _remote(landed, landed, send_sems, recv_sems, 3 * a + k, sib).wait_recv()
        for cp in cps:
            cp.wait_send()

    return pl.pallas_call(
        body, name=name, in_specs=[ANY] * n, out_specs=[ANY] * n,
        out_shape=[jax.ShapeDtypeStruct(a.shape, a.dtype) for a in lands],
        input_output_aliases={a: a for a in range(n)},
        scratch_shapes=[pltpu.SemaphoreType.DMA((3 * n,)), pltpu.SemaphoreType.DMA((3 * n,))],
        compiler_params=pltpu.CompilerParams(has_side_effects=True),
    )(*lands)


def _sibling_share(halves, name):
    n = len(halves)

    def body(*refs):
        ins, outs = refs[:n], refs[n:2 * n]
        send_sems, recv_sems = refs[2 * n:]
        x, y, c, _ = _place()
        cps = [_remote(ins[a], outs[a], send_sems, recv_sems, a, (x, y, 1 - c)) for a in range(n)]
        for cp in cps:
            cp.start()
        for cp in cps:
            cp.wait()

    return pl.pallas_call(
        body, name=name, in_specs=[ANY] * n, out_specs=[ANY] * n,
        out_shape=[jax.ShapeDtypeStruct(h.shape, F32) for h in halves],
        scratch_shapes=[pltpu.SemaphoreType.DMA((n,)), pltpu.SemaphoreType.DMA((n,))],
        compiler_params=pltpu.CompilerParams(has_side_effects=True),
    )(*halves)


HBM = pl.BlockSpec(memory_space=pltpu.HBM)
SEM = pl.BlockSpec(memory_space=pltpu.SEMAPHORE)
DATAFLOW = pltpu.SideEffectType.DATAFLOW_SIDE_EFFECTING


def _split_start(name, srcs, land_shapes, plan, n_copies, after):
    lands = [lax.empty(shp, dt) for shp, dt in land_shapes]
    bufs = list(srcs) + lands
    nb, ns = len(bufs), len(srcs)

    def body(*refs):
        send_sems, recv_sems, token = refs[nb + 1], refs[nb + 2], refs[-1]
        for k, (src, dst, to) in enumerate(plan(refs[:ns], refs[ns:nb])):
            _remote(src, dst, send_sems, recv_sems, k, to).start()
        token[...] = jnp.zeros_like(token)

    res = pl.pallas_call(
        body, name=name,
        out_shape=(pltpu.SemaphoreType.DMA((n_copies,)), pltpu.SemaphoreType.DMA((n_copies,)),
                   *[pltpu.HBM(b.shape, b.dtype) for b in bufs], jax.ShapeDtypeStruct((8, 128), F32)),
        in_specs=[HBM] * nb + [ANY],
        out_specs=(SEM, SEM, *[HBM] * nb, pl.BlockSpec(memory_space=pltpu.VMEM)),
        input_output_aliases={i: 2 + i for i in range(nb)},
        compiler_params=pltpu.CompilerParams(has_side_effects=DATAFLOW),
    )(*[pltpu.with_memory_space_constraint(b, pltpu.HBM) for b in bufs], after)
    return (res[0], res[1], list(res[2:2 + nb])), res[-1]


def _split_wait(name, handle, n_srcs, plan, after):
    send_sems, recv_sems, bufs = handle
    nb = len(bufs)

    def body(*refs):
        sends, recvs = refs[nb], refs[nb + 1]
        for k, (src, dst, to) in enumerate(plan(refs[:n_srcs], refs[n_srcs:nb])):
            cp = _remote(src, dst, sends, recvs, k, to)
            cp.wait_send()
            cp.wait_recv()

    res = pl.pallas_call(
        body, name=name, out_shape=[pltpu.HBM(b.shape, b.dtype) for b in bufs],
        in_specs=[HBM] * nb + [SEM, SEM, ANY], out_specs=[HBM] * nb,
        input_output_aliases={i: i for i in range(nb)},
        compiler_params=pltpu.CompilerParams(has_side_effects=DATAFLOW),
    )(*bufs, send_sems, recv_sems, after)
    return list(res[:n_srcs]), list(res[n_srcs:])


def _gather_plan(shapes, axes, n_whole=0):
    def plan(srcs, lands):
        x, y, c, chips = _place()
        out = []
        for a, (shape, axis) in enumerate(zip(shapes, axes)):
            mine = _half(shape, c, axis)
            for ch in chips:
                out.append((srcs[a].at[mine], lands[a].at[(2 * x + y,) + mine], (ch[0], ch[1], c)))
        for a in range(len(shapes), len(shapes) + n_whole):
            for ch in chips:
                out.append((srcs[a], lands[a].at[2 * x + y], (ch[0], ch[1], c)))
        return out
    return plan


def _sibling_plan(shapes, axes):
    def plan(srcs, lands):
        x, y, c, _ = _place()
        return [(srcs[a].at[(slice(None),) + _half(shape, 1 - c, axis)], lands[a], (x, y, 1 - c))
                for a, (shape, axis) in enumerate(zip(shapes, axes))]
    return plan


def _reduce_plan(n_big, n_small):
    def plan(srcs, lands):
        x, y, c, chips = _place()
        out = []
        for a in range(n_big):
            for k, ch in enumerate(chips):
                out.append((srcs[a].at[2 * ch[0] + ch[1]], lands[a].at[k], (ch[0], ch[1], c)))
        for a in range(n_big, n_big + n_small):
            for ch in chips:
                out.append((srcs[a], lands[a].at[2 * x + y], (ch[0], ch[1], c)))
        return out
    return plan


def _row_tile(rows, cols, mult):
    best = mult
    for t in range(mult, rows + 1, mult):
        if rows % t == 0 and t * cols * 4 <= (2 << 20):
            best = t
    return best if rows % best == 0 else rows


COL_TILE = 256


def _half_tiling(hshape, axis, mult):
    hr, hc = hshape
    if axis == 0:
        tr = _row_tile(hr, hc, mult)
        return tr, hc, hr // tr
    return hr, COL_TILE, hc // COL_TILE


def _tile_idx(axis, t):
    return (t, 0) if axis == 0 else (0, t)


def _chip_partial(place, g, t, axis, name):
    hshape = t.shape[1:]
    br, bc, nt = _half_tiling(hshape, axis, 16)

    def body(pl_ref, g_ref, t_ref, pf_ref, pb_ref):
        v = g_ref[...].astype(F32) + t_ref[...].astype(F32)
        pb_ref[...] = v.astype(BF16)

        @pl.when(pl.program_id(1) == pl_ref[0])
        def _():
            pf_ref[...] = v

    blk = (None, br, bc)
    return pl.pallas_call(
        body, name=name,
        grid_spec=pltpu.PrefetchScalarGridSpec(
            num_scalar_prefetch=1, grid=(nt, 4),
            in_specs=[pl.BlockSpec(blk, lambda i, j, p: (j,) + _tile_idx(axis, p[1] * nt + i)),
                      pl.BlockSpec(blk, lambda i, j, p: (j,) + _tile_idx(axis, i))],
            out_specs=[pl.BlockSpec((br, bc), lambda i, j, p: _tile_idx(axis, i)),
                       pl.BlockSpec(blk, lambda i, j, p: (j,) + _tile_idx(axis, i))]),
        out_shape=[jax.ShapeDtypeStruct(hshape, F32), jax.ShapeDtypeStruct((4,) + hshape, BF16)],
        compiler_params=_cp("arbitrary", "arbitrary"),
    )(place, g, t)


def _finish_half(pf, rb, axis, name):
    hshape = pf.shape
    br, bc, nt = _half_tiling(hshape, axis, 16)

    def body(pf_ref, rb_ref, o_ref):
        o_ref[...] = ((pf_ref[...] + rb_ref[0].astype(F32)) + rb_ref[1].astype(F32)) + rb_ref[2].astype(F32)

    return pl.pallas_call(
        body, name=name, grid=(nt,),
        in_specs=[pl.BlockSpec((br, bc), lambda i: _tile_idx(axis, i)),
                  pl.BlockSpec((3, br, bc), lambda i: (0,) + _tile_idx(axis, i))],
        out_specs=pl.BlockSpec((br, bc), lambda i: _tile_idx(axis, i)),
        out_shape=jax.ShapeDtypeStruct(hshape, F32),
        compiler_params=_cp("arbitrary"),
    )(pf, rb)


def _adam_math(w, g, m, v):
    m = ADAM_B1 * m + (1.0 - ADAM_B1) * g
    v = ADAM_B2 * v + (1.0 - ADAM_B2) * (g * g)
    m_hat = m / (1.0 - ADAM_B1 ** ADAM_STEP)
    v_hat = v / (1.0 - ADAM_B2 ** ADAM_STEP)
    return -ADAM_LR * (m_hat / (jnp.sqrt(v_hat) + ADAM_EPS) + ADAM_WD * w), m, v


def _adam_halves(place, w, mine, theirs, m, v, axis, name):
    br, bc, nt = _half_tiling(mine.shape, axis, 8)

    def body(pl_ref, w_ref, a_ref, b_ref, m_ref, v_ref, g_ref, d_ref, mo_ref, vo_ref):
        is_mine = pl.program_id(0) // nt == pl_ref[1]
        g = jnp.where(is_mine, a_ref[...], b_ref[...])
        d, mn, vn = _adam_math(w_ref[...], g, m_ref[...], v_ref[...])
        g_ref[...] = g
        d_ref[...] = d
        mo_ref[...] = mn
        vo_ref[...] = vn

    full = pl.BlockSpec((br, bc), lambda i, p: _tile_idx(axis, i))
    mine_spec = pl.BlockSpec((br, bc), lambda i, p: _tile_idx(axis, jnp.where(i // nt == p[1], i % nt, nt - 1)))
    theirs_spec = pl.BlockSpec((br, bc), lambda i, p: _tile_idx(axis, jnp.where(i // nt == p[1], 0, i % nt)))
    return pl.pallas_call(
        body, name=name,
        grid_spec=pltpu.PrefetchScalarGridSpec(
            num_scalar_prefetch=1, grid=(2 * nt,), in_specs=[full, mine_spec, theirs_spec, full, full],
            out_specs=[full] * 4),
        out_shape=[jax.ShapeDtypeStruct(w.shape, F32)] * 4, compiler_params=_cp("arbitrary"),
    )(place, w, mine, theirs, m, v)


def _add_many(xs, ys, name):
    n = len(xs)

    def body(*refs):
        for i in range(n):
            refs[2 * n + i][...] = refs[i][...] + refs[n + i][...]

    return pl.pallas_call(body, name=name, out_shape=[jax.ShapeDtypeStruct(a.shape, F32) for a in xs])(*xs, *ys)


def _adam_small(place, owns, landed, ws, ms, vs, widths):
    n, nw = len(owns), len(ws)

    def body(pl_ref, *refs):
        own_r, land_r = refs[:n], refs[n:2 * n]
        w_r, m_r, v_r = (refs[2 * n + k * nw:2 * n + (k + 1) * nw] for k in range(3))
        outs = refs[2 * n + 3 * nw:]
        g_o, d_o, m_o, v_o = outs[:n], outs[n:n + nw], outs[n + nw:n + 2 * nw], outs[n + 2 * nw:]
        for me in range(4):
            @pl.when(pl_ref[0] == me)
            def _(me=me):
                for i in range(n):
                    p = [own_r[i][...] if k == me else land_r[i][k] for k in range(4)]
                    g = ((p[0] + p[1]) + p[2]) + p[3]
                    if i < nw and widths[i]:
                        g = g[:, me * widths[i]:(me + 1) * widths[i]]
                    g_o[i][...] = g
                    if i < nw:
                        d, mn, vn = _adam_math(w_r[i][...], g, m_r[i][...], v_r[i][...])
                        d_o[i][...] = d
                        m_o[i][...] = mn
                        v_o[i][...] = vn

    g_shapes = [jax.ShapeDtypeStruct(ws[i].shape if i < nw else owns[i].shape, F32) for i in range(n)]
    w_shapes = [jax.ShapeDtypeStruct(w.shape, F32) for w in ws]
    whole = lambda a: pl.BlockSpec(a.shape, lambda i, p, nd=len(a.shape): (0,) * nd)
    ins = list(owns) + list(landed) + list(ws) + list(ms) + list(vs)
    out_shape = g_shapes + w_shapes * 3
    out = pl.pallas_call(
        body, name="adam_small",
        grid_spec=pltpu.PrefetchScalarGridSpec(num_scalar_prefetch=1, grid=(1,), in_specs=[whole(a) for a in ins],
                                               out_specs=[whole(a) for a in out_shape]),
        out_shape=out_shape, compiler_params=_cp("arbitrary"),
    )(place, *ins)
    return out[:n], out[n:n + nw], out[n + nw:n + 2 * nw], out[n + 2 * nw:]


def kernel(x, g_mix, w_in, b_gate, w_gk_up, b_gk, w_pool_grp, pool_scale, g_gla_head, w_pool_proj, w_gla_proj, w_out, g_ffn, w_up, w_conv, b_conv, w_down, g_final, loss_target, m_g_mix, m_w_in, m_b_gate, m_w_gk_up, m_b_gk, m_w_pool_grp, m_pool_scale, m_g_gla_head, m_w_pool_proj, m_w_gla_proj, m_w_out, m_g_ffn, m_w_up, m_w_conv, m_b_conv, m_w_down, m_g_final, v_g_mix, v_w_in, v_b_gate, v_w_gk_up, v_b_gk, v_w_pool_grp, v_pool_scale, v_g_gla_head, v_w_pool_proj, v_w_gla_proj, v_w_out, v_g_ffn, v_w_up, v_w_conv, v_b_conv, v_w_down, v_g_final):
    s = x.shape[1]
    ts = min(s, 512)
    tm = min(s, 256)
    cx, cy, cc = lax.axis_index("x"), lax.axis_index("y"), lax.axis_index("c")
    chip = 2 * cx + cy
    place = jnp.stack([chip, cc]).astype(jnp.int32)

    big_names = ("w_in", "w_pool_proj", "w_gla_proj", "w_out", "w_up", "w_down")
    axes = (1, 0, 0, 0, 0, 0)
    shards = dict(w_in=jnp.transpose(w_in[0]), w_pool_proj=w_pool_proj[0], w_gla_proj=w_gla_proj[0], w_out=w_out[0],
                  w_up=w_up[0], w_down=w_down[0])
    def fill_own(lands, mine):
        return [lax.dynamic_update_slice(g, o_[None], (chip, 0, 0)) for g, o_ in zip(lands, mine)]

    def gather_start(tag, halves, group_axes, whole, after):
        plan = _gather_plan([o_.shape for o_ in halves], group_axes, len(whole))
        srcs = list(halves) + list(whole)
        handle, token = _split_start("gather_" + tag + "_start", srcs, [((4,) + o_.shape, o_.dtype) for o_ in srcs], plan,
                                     3 * len(srcs), after)
        return (handle, plan, len(halves), len(srcs), group_axes), token

    def gather_finish(tag, started, after):
        handle, plan, n_halves, n, group_axes = started
        mine, lands = _split_wait("gather_" + tag + "_wait", handle, n, plan, after)
        lands[:n_halves] = _gather_share(lands[:n_halves], group_axes, "gather_" + tag + "_share")
        return fill_own(lands, mine)

    in_w, tok = gather_start("in", [shards["w_in"].astype(BF16)], axes[:1], [], g_mix)
    zero = tok[0, 0]
    own = [(shards[n] + zero).astype(BF16) for n in big_names[1:]]
    mix_w, tok = gather_start("mix", own[0:3], axes[1:4], [w_gk_up[0] + zero, w_conv[0] + zero], tok)
    ffn_w, tok = gather_start("ffn", own[3:5], axes[4:6], [], tok)
    xs, tgt = x[0], loss_target[0]
    wgrp = w_pool_grp[0]
    h = _rmsnorm(xs, g_mix, tok, "norm_mix", ts)
    m_in_t, v_in_t = jnp.transpose(m_w_in[0]), jnp.transpose(v_w_in[0])
    h, m_in_t, v_in_t = lax.optimization_barrier((h, m_in_t, v_in_t))
    w_in_t = gather_finish("in", in_w, h)[0].reshape(N_IN, D)
    w_rt = jnp.concatenate([w_in_t[3600:], w_in_t[1536:3584], w_in_t[0:1536], w_in_t[3584:3600],
                            jnp.zeros((128 - GATE_RANK, D), BF16)], axis=0)
    nsh = N_IN // 4

    zr = _matmul_resident(h, w_rt, "in_proj", 1152, transposed=True)
    p, pp = _pool_fwd(zr, wgrp, pool_scale)
    wpp, wgla, wout, wgk4, wconv4 = gather_finish("mix", mix_w, pp)
    wgla, wout = wgla.reshape(D, D), wout.reshape(D, D)
    wgk_full = jnp.transpose(wgk4, (1, 0, 2)).reshape(GATE_RANK, 512)
    wconv_full = jnp.transpose(wconv4, (1, 0, 2)).reshape(3, N_UP)
    wgk_pad = jnp.concatenate([wgk_full, jnp.zeros((128 - GATE_RANK, 512), F32)], axis=0)
    o, og, sp = _gla_fwd(zr, wgk_pad, b_gk, g_gla_head, ts)
    x1, mixed, yp, yg, h2 = _merge_fwd(xs, zr, pp, og, b_gate, wpp, wgla, wout, g_ffn, ts)
    wup, wdown = gather_finish("ffn", ffn_w, x1)
    wdown = wdown.reshape(D_FF, D)
    u = _matmul_resident(h2, wup, "ffn_up", None)
    a, conv_out, dx2, dx2b, loss_part, dgfin = _ffn_down_loss(u, x1, tgt, wconv_full, b_conv, wdown,
                                                              g_final.reshape(1, D), tm)

    du, dbconv, dwconv = _ffn_bwd(dx2b, u, conv_out, wconv_full, wdown, tm)
    dw_down = _matmul_tn(a, dx2b, "dw_down", D, tm=1408)
    dw_up = _matmul_tn(h2, du, "dw_up", 1408, shard_major=True)

    def exchange_start(tag, grads, group_axes, after):
        plan = _sibling_plan([g.shape[1:] for g in grads], group_axes)
        lands = [((4,) + _half_shape(g.shape[1:], ax), g.dtype) for g, ax in zip(grads, group_axes)]
        handle, token = _split_start("sibling_" + tag + "_start", grads, lands, plan, len(grads), after)
        return (handle, plan, len(grads)), token

    def partials(tag, names, group_axes, exchange, after):
        handle, plan, n = exchange
        mine, theirs = _split_wait("sibling_" + tag + "_wait", handle, n, plan, after)
        return zip(*[_chip_partial(place, g, t, ax, "chip_partial_" + nm)
                     for nm, ax, g, t in zip(names, group_axes, mine, theirs)])

    dx1, dx1b, dgffn = _matmul_nt_normbwd(du, wup, x1, g_ffn, dx2, du, "ffn_up_bwd", ts)
    dzg, dyp, dyg, dpp, do, dzog, dbgate, dghead = _merge_bwd(dx1b, zr, yp, yg, o, b_gate, g_gla_head, wpp, wgla, wout,
                                                             dx1b, ts)
    dw_out = _matmul_tn(mixed, dx1b, "dw_out", D)
    dw_gla = _matmul_tn(og, dyg, "dw_gla", D)
    dw_pp = _matmul_tn(pp, dyp, "dw_pp", 256, shard_major=True)

    rest_names, rest_axes = ("w_up", "w_down", "w_pool_proj", "w_gla_proj", "w_out"), (0, 0, 0, 0, 0)
    rest_x, token = exchange_start("rest", [dw_up, dw_down.reshape(4, 704, D), dw_pp, dw_gla.reshape(4, 256, D),
                                            dw_out.reshape(4, 256, D)], rest_axes, dpp)
    dzp, dwgrp, dscale = _pool_bwd(p, dpp, wgrp, pool_scale, token)
    dq, dk, dv, dgpre = _gla_bwd(zr, do, sp, wgk_pad, b_gk, dzp, ts)
    rest_pf, rest_pb = partials("rest", rest_names, rest_axes, rest_x, dq)
    rest_plan = _reduce_plan(5, 0)
    rest_handle, token = _split_start("reduce_rest_start", rest_pb, [((3,) + p_.shape[1:], BF16) for p_ in rest_pb],
                                      rest_plan, 15, rest_pf[0])
    dzgk, dwgk, dbgk = _gk_bwd(dgpre, zr, wgk_pad, token, ts)
    dzr = jnp.concatenate([dzg, dv, dzog, dzp, dq, dk, dzgk], axis=1)
    dw_rt = _matmul_tn(dzr, h, "dw_in", D, tm=1152)

    def grad_rows(lo, hi):
        out = []
        for seg_lo, seg_hi, at in ((0, 1536, OFF_POOL), (1536, 3584, OFF_V), (3584, 3600, OFF_GK), (3600, N_IN, OFF_GATE)):
            a_, b_ = max(lo, seg_lo), min(hi, seg_hi)
            if a_ < b_:
                out.append(dw_rt[at + a_ - seg_lo:at + b_ - seg_lo])
        return jnp.concatenate(out, axis=0)

    dw_in_t = jnp.stack([grad_rows(j * nsh, (j + 1) * nsh) for j in range(4)])

    in_sib = _sibling_exchange([dw_in_t], (1,), [], "sibling_exchange_in")
    in_pf, in_pb = _chip_partial(place, dw_in_t, in_sib[0], 1, "chip_partial_w_in")
    in_plan = _reduce_plan(1, 0)
    in_handle, token = _split_start("reduce_in_start", [in_pb], [((3,) + in_pb.shape[1:], BF16)], in_plan, 3, in_pf)
    grad_x, _, dgmix = _matmul_nt_normbwd(dzr, w_rt, xs, g_mix, dx1, token, "in_proj_bwd", ts, transposed=True)
    small_names = ("g_mix", "b_gate", "w_gk_up", "b_gk", "w_pool_grp", "pool_scale", "g_gla_head", "g_ffn", "w_conv",
                   "b_conv", "g_final")
    small_mine = [dgmix, dbgate, dwgk[:GATE_RANK], dbgk, dwgrp.reshape(4 * 128, 128), dscale, dghead, dgffn, dwconv, dbconv,
                  dgfin, loss_part]
    small_sib = _sibling_exchange([], (), small_mine, "sibling_exchange_small")
    small_chip = _add_many(small_mine, small_sib, "chip_partial_small")
    small_plan = _reduce_plan(0, len(small_chip))
    small_handle, token = _split_start("reduce_small_start", small_chip, [((4,) + a_.shape, F32) for a_ in small_chip],
                                       small_plan, 3 * len(small_chip), small_mine[0])

    ms = dict(w_in=m_in_t, w_pool_proj=m_w_pool_proj[0], w_gla_proj=m_w_gla_proj[0], w_out=m_w_out[0],
              w_up=m_w_up[0], w_down=m_w_down[0])
    vs = dict(w_in=v_in_t, w_pool_proj=v_w_pool_proj[0], w_gla_proj=v_w_gla_proj[0], w_out=v_w_out[0],
              w_up=v_w_up[0], w_down=v_w_down[0])
    grad, delta, new_m, new_v = {}, {}, {}, {}

    def finish_and_update(names, group_axes, part_f, landed, tag):
        halves = [_finish_half(pf, rb, ax, "finish_" + n) for n, ax, pf, rb in zip(names, group_axes, part_f, landed)]
        sib_halves = _sibling_share(halves, "sibling_share_" + tag)
        for n, ax, mine, theirs in zip(names, group_axes, halves, sib_halves):
            res = _adam_halves(place, shards[n], mine, theirs, ms[n], vs[n], ax, "adam_" + n)
            if n == "w_in":
                res = [jnp.transpose(r_) for r_ in res]
            grad[n], delta[n], new_m[n], new_v[n] = [r_[None] for r_ in res]

    _, rest_landed = _split_wait("reduce_rest_wait", rest_handle, 5, rest_plan, token)
    finish_and_update(rest_names, rest_axes, rest_pf, rest_landed, "rest")
    _, in_landed = _split_wait("reduce_in_wait", in_handle, 1, in_plan, delta["w_out"])
    finish_and_update(("w_in",), (1,), (in_pf,), in_landed, "in")
    small_sent, small_landed = _split_wait("reduce_small_wait", small_handle, len(small_chip), small_plan, delta["w_in"])
    given = dict(g_mix=(g_mix, m_g_mix, v_g_mix), b_gate=(b_gate, m_b_gate, v_b_gate), w_gk_up=(w_gk_up, m_w_gk_up, v_w_gk_up),
                 b_gk=(b_gk, m_b_gk, v_b_gk), w_pool_grp=(w_pool_grp, m_w_pool_grp, v_w_pool_grp),
                 pool_scale=(pool_scale, m_pool_scale, v_pool_scale), g_gla_head=(g_gla_head, m_g_gla_head, v_g_gla_head),
                 g_ffn=(g_ffn, m_g_ffn, v_g_ffn), w_conv=(w_conv, m_w_conv, v_w_conv), b_conv=(b_conv, m_b_conv, v_b_conv),
                 g_final=(g_final, m_g_final, v_g_final))
    flat2 = lambda a: a.reshape(-1, a.shape[-1])
    widths = [dict(w_gk_up=128, w_conv=1408).get(n) for n in small_names]
    totals, ds, mo, vo = _adam_small(place, small_sent, small_landed, *[[flat2(given[n][k]) for n in small_names] for k in range(3)],
                                     widths)
    loss = totals[-1][0, 0]
    for i, n in enumerate(small_names):
        shp = given[n][0].shape
        grad[n], delta[n], new_m[n], new_v[n] = [r_.reshape(shp) for r_ in (totals[i], ds[i], mo[i], vo[i])]

    order = ("g_mix", "w_in", "b_gate", "w_gk_up", "b_gk", "w_pool_grp", "pool_scale", "g_gla_head", "w_pool_proj",
             "w_gla_proj", "w_out", "g_ffn", "w_up", "w_conv", "b_conv", "w_down", "g_final")
    return (loss, grad_x[None], *[grad[n] for n in order], *[delta[n] for n in order], *[new_m[n] for n in order],
            *[new_v[n] for n in order])
```

```python
import functools

import jax
import jax.numpy as jnp
from jax import lax
from jax.experimental import pallas as pl
from jax.experimental.pallas import tpu as pltpu

F32 = jnp.float32
BF16 = jnp.bfloat16
MESH = pl.DeviceIdType.MESH

D = 1024
EPS = 1e-6
CHUNK = 64
POOL_W = 512
POOL_WINDOWS = (2, 4, 8, 16)
HEADS = 4
HK = 128
HV = 256
GATE_RANK = 16
D_FF = 2816
N_UP = 2 * D_FF
N_IN = 5648
QSCALE = HK ** -0.5
N_INR = 5760
OFF_GATE, OFF_V, OFF_OG, OFF_POOL, OFF_Q, OFF_K, OFF_GK = 0, 2048, 3072, 4096, 4608, 5120, 5632

ADAM_LR, ADAM_B1, ADAM_B2, ADAM_EPS, ADAM_WD, ADAM_STEP = 0.001, 0.9, 0.999, 1e-08, 0.01, 10

VMEM_LIMIT = 56 * 1024 * 1024


def _cp(*sem):
    return pltpu.CompilerParams(dimension_semantics=sem if sem else None, vmem_limit_bytes=VMEM_LIMIT)


def _dot(a, b):
    return jnp.dot(a, b, preferred_element_type=F32)


def _dot_nt(a, b):
    return lax.dot_general(a, b, (((1,), (1,)), ((), ())), preferred_element_type=F32)


def _dot_tn(a, b):
    return lax.dot_general(a, b, (((0,), (0,)), ((), ())), preferred_element_type=F32)


def _sigmoid(v):
    return 1.0 / (1.0 + jnp.exp(-v))


def _rows(shape):
    return lax.broadcasted_iota(jnp.int32, shape, 0)


def _pick_row(v, r):
    return jnp.sum(jnp.where(_rows(v.shape) == r, v, 0.0), axis=0, keepdims=True)


def _rmsnorm(x, g, after, name, ts):
    s = x.shape[0]

    def body(x_ref, g_ref, after_ref, h_ref):
        xv = x_ref[...]
        r = lax.rsqrt(jnp.mean(xv * xv, axis=-1, keepdims=True) + EPS)
        h_ref[...] = (xv * r * g_ref[...]).astype(BF16)

    return pl.pallas_call(
        body, name=name, grid=(s // ts,),
        in_specs=[pl.BlockSpec((ts, D), lambda i: (i, 0)), pl.BlockSpec((1, D), lambda i: (0, 0)), ANY],
        out_specs=pl.BlockSpec((ts, D), lambda i: (i, 0)), out_shape=jax.ShapeDtypeStruct((s, D), BF16),
        compiler_params=_cp("arbitrary"),
    )(x, g, after)


MM_ROWS = 512


def _matmul_resident(h, w, name, tn, transposed=False):
    s = h.shape[0]
    if transposed:
        nj = w.shape[0] // tn
        w_spec = pl.BlockSpec((tn, D), lambda j: (j, 0))
    elif w.ndim == 3:
        nj, tn = w.shape[0], w.shape[2]
        w_spec = pl.BlockSpec((None, D, tn), lambda j: (j, 0, 0))
    else:
        nj = w.shape[1] // tn
        w_spec = pl.BlockSpec((D, tn), lambda j: (0, j))
    mm = _dot_nt if transposed else _dot
    rc = min(s, MM_ROWS)

    def body(h_ref, w_ref, z_ref):
        for r0 in range(0, s, rc):
            z_ref[r0:r0 + rc, :] = mm(h_ref[r0:r0 + rc, :], w_ref[...]).astype(BF16)

    return pl.pallas_call(
        body, name=name, grid=(nj,),
        in_specs=[pl.BlockSpec((s, D), lambda j: (0, 0)), w_spec],
        out_specs=pl.BlockSpec((s, tn), lambda j: (0, j)), out_shape=jax.ShapeDtypeStruct((s, nj * tn), BF16),
        compiler_params=_cp("arbitrary"),
    )(h, w)


def _matmul_nt_normbwd(dz, w, x, g, resid, after, name, ts, transposed=False):
    s = x.shape[0]

    def body(dz_ref, w_hbm, x_ref, g_ref, r_ref, after_ref, o_ref, ob_ref, dg_ref, w_ref, sem):
        @pl.when(pl.program_id(0) == 0)
        def _():
            cp = pltpu.make_async_copy(w_hbm, w_ref, sem)
            cp.start()
            cp.wait()
            dg_ref[...] = jnp.zeros_like(dg_ref)

        if transposed:
            dh = _dot(dz_ref[...], w_ref[...])
        else:
            kc = w.shape[2]
            dh = _dot_nt(dz_ref[:, 0:kc], w_ref[0])
            for j in range(1, w.shape[0]):
                dh = dh + _dot_nt(dz_ref[:, j * kc:(j + 1) * kc], w_ref[j])
        xv = x_ref[...]
        r = lax.rsqrt(jnp.mean(xv * xv, axis=-1, keepdims=True) + EPS)
        xh = xv * r
        dg_ref[...] += jnp.sum(dh * xh, axis=0, keepdims=True)
        dxh = dh * g_ref[...]
        out = r_ref[...] + r * (dxh - xh * jnp.mean(dxh * xh, axis=-1, keepdims=True))
        o_ref[...] = out
        ob_ref[...] = out.astype(BF16)

    row = lambda i: (i, 0)
    kdim = dz.shape[1]
    return pl.pallas_call(
        body, name=name, grid=(s // ts,),
        in_specs=[pl.BlockSpec((ts, kdim), row), ANY, pl.BlockSpec((ts, D), row),
                  pl.BlockSpec((1, D), lambda i: (0, 0)), pl.BlockSpec((ts, D), row), ANY],
        out_specs=[pl.BlockSpec((ts, D), row), pl.BlockSpec((ts, D), row), pl.BlockSpec((1, D), lambda i: (0, 0))],
        out_shape=[jax.ShapeDtypeStruct((s, D), F32), jax.ShapeDtypeStruct((s, D), BF16),
                   jax.ShapeDtypeStruct((1, D), F32)],
        scratch_shapes=[pltpu.VMEM(w.shape, BF16), pltpu.SemaphoreType.DMA],
        compiler_params=_cp("arbitrary"),
    )(dz, w, x, g, resid, after)


def _matmul_tn(a, b, name, tn, shard_major=False, tm=None):
    s, m = a.shape
    n = b.shape[1]
    tm = m if tm is None else tm
    ni, nj = m // tm, n // tn

    def body(a_ref, b_ref, o_ref):
        o_ref[...] = _dot_tn(a_ref[...], b_ref[...]).astype(BF16)

    if shard_major:
        out_spec = pl.BlockSpec((None, tm, tn), lambda i, j: (j, i, 0))
        out_shape = jax.ShapeDtypeStruct((nj, m, tn), BF16)
    else:
        out_spec = pl.BlockSpec((tm, tn), lambda i, j: (i, j))
        out_shape = jax.ShapeDtypeStruct((m, n), BF16)
    return pl.pallas_call(
        body, name=name, grid=(ni, nj),
        in_specs=[pl.BlockSpec((s, tm), lambda i, j: (0, i)), pl.BlockSpec((s, tn), lambda i, j: (0, j))],
        out_specs=out_spec, out_shape=out_shape,
        compiler_params=_cp("arbitrary", "arbitrary"),
    )(a, b)


def _pool_fwd(zr, wgrp, scale):
    s = zr.shape[0]

    def body(u_ref, w_ref, sc_ref, p_ref, pp_ref):
        row = _rows((s, 128))
        for gi, win in enumerate(POOL_WINDOWS):
            cs = slice(gi * 128, (gi + 1) * 128)
            u = u_ref[:, cs].astype(F32)
            acc, k = u, 1
            while k < win:
                acc = acc + jnp.where(row >= k, pltpu.roll(acc, k, 0), 0.0)
                k *= 2
            cnt = jnp.minimum(row + 1, win).astype(F32)
            p = (acc / cnt - u).astype(BF16)
            p_ref[:, cs] = p
            pp_ref[:, cs] = (_dot(p, w_ref[gi].astype(BF16)) * sc_ref[:, cs]).astype(BF16)

    return pl.pallas_call(
        body, name="pool_fwd", grid=(1,),
        in_specs=[pl.BlockSpec((s, POOL_W), lambda i: (0, OFF_POOL // POOL_W)),
                  pl.BlockSpec((4, 128, 128), lambda i: (0, 0, 0)), pl.BlockSpec((1, POOL_W), lambda i: (0, 0))],
        out_specs=[pl.BlockSpec((s, POOL_W), lambda i: (0, 0))] * 2,
        out_shape=[jax.ShapeDtypeStruct((s, POOL_W), BF16)] * 2,
        compiler_params=_cp("arbitrary"),
    )(zr, wgrp, scale)


def _pool_bwd(p, dpp, wgrp, scale, after):
    s = p.shape[0]

    def body(p_ref, dpp_ref, w_ref, sc_ref, after_ref, dz_ref, dw_ref, dsc_ref):
        row = _rows((s, 128))
        for gi, win in enumerate(POOL_WINDOWS):
            cs = slice(gi * 128, (gi + 1) * 128)
            pv = p_ref[:, cs]
            wb = w_ref[gi].astype(BF16)
            dpp_v = dpp_ref[:, cs].astype(F32)
            dsc_ref[:, cs] = jnp.sum(dpp_v * _dot(pv, wb), axis=0, keepdims=True)
            dpm = (dpp_v * sc_ref[:, cs]).astype(BF16)
            dw_ref[gi] = _dot_tn(pv, dpm)
            dp = _dot_nt(dpm, wb)
            cnt = jnp.minimum(row + 1, win).astype(F32)
            acc, k = dp / cnt, 1
            while k < win:
                acc = acc + jnp.where(row < s - k, pltpu.roll(acc, s - k, 0), 0.0)
                k *= 2
            dz_ref[:, cs] = (acc - dp).astype(BF16)

    full = lambda i: (0, 0)
    return pl.pallas_call(
        body, name="pool_bwd", grid=(1,),
        in_specs=[pl.BlockSpec((s, POOL_W), full), pl.BlockSpec((s, POOL_W), full),
                  pl.BlockSpec((4, 128, 128), lambda i: (0, 0, 0)), pl.BlockSpec((1, POOL_W), full), ANY],
        out_specs=[pl.BlockSpec((s, POOL_W), full), pl.BlockSpec((4, 128, 128), lambda i: (0, 0, 0)),
                   pl.BlockSpec((1, POOL_W), full)],
        out_shape=[jax.ShapeDtypeStruct((s, POOL_W), BF16), jax.ShapeDtypeStruct((4, 128, 128), F32),
                   jax.ShapeDtypeStruct((1, POOL_W), F32)],
        compiler_params=_cp("arbitrary"),
    )(p, dpp, wgrp, scale, after)


def _gla_decay(zgk_ref, wgk_ref, bgk_ref, rb):
    g = _dot(zgk_ref[...], wgk_ref[...].astype(BF16)) + bgk_ref[...]
    la = (jnp.minimum(g, 0.0) - jnp.log(1.0 + jnp.exp(-jnp.abs(g)))) * (1.0 / 16.0)
    rowm = _rows(la.shape) & (CHUNK - 1)
    bc, k = la, 1
    while k < CHUNK:
        bc = bc + jnp.where(rowm >= k, pltpu.roll(bc, k, 0), 0.0)
        k *= 2
    return g, jnp.exp(bc), jnp.exp(-bc)


GLA_HB = 4


def _gla_specs(rb, rmap):
    wk, wv = GLA_HB * HK, GLA_HB * HV
    return [pl.BlockSpec((rb, wk), lambda h, r: (rmap(h, r), OFF_Q // wk + h)),
            pl.BlockSpec((rb, wk), lambda h, r: (rmap(h, r), OFF_K // wk + h)),
            pl.BlockSpec((rb, wv), lambda h, r: (rmap(h, r), OFF_V // wv + h)),
            pl.BlockSpec((rb, 128), lambda h, r: (rmap(h, r), OFF_GK // 128))]


def _gla_fwd(zr, wgk, bgk, ghead, rb):
    s = zr.shape[0]
    nc = rb // CHUNK
    wk, wv = GLA_HB * HK, GLA_HB * HV

    def body(q_ref, k_ref, v_ref, zgk_ref, zog_ref, wgk_ref, bgk_ref, gh_ref, o_ref, og_ref, sp_ref, st_ref):
        @pl.when(pl.program_id(1) == 0)
        def _():
            st_ref[...] = jnp.zeros_like(st_ref)

        _, e_pos, e_neg = _gla_decay(zgk_ref, wgk_ref, bgk_ref, rb)
        lower = _rows((CHUNK, CHUNK)) >= lax.broadcasted_iota(jnp.int32, (CHUNK, CHUNK), 1)
        for c in range(nc):
            sl = slice(c * CHUNK, (c + 1) * CHUNK)
            for hh in range(GLA_HB):
                ck, cv = slice(hh * HK, (hh + 1) * HK), slice(hh * HV, (hh + 1) * HV)
                q = q_ref[sl, ck].astype(F32) * QSCALE
                k = k_ref[sl, ck].astype(F32)
                v = v_ref[sl, cv]
                ec, fc = e_pos[sl, ck], e_neg[sl, ck]
                qfw = (q * ec).astype(BF16)
                kfw_f = k * fc
                s_fw = _dot_nt(qfw, kfw_f.astype(BF16))
                s_bw = _dot_nt((q * fc).astype(BF16), (k * ec).astype(BF16))
                pm = jnp.where(lower, s_fw, s_bw).astype(BF16)
                st = st_ref[hh]
                stb = st.astype(BF16)
                sp_ref[c, hh] = stb
                o = _dot(pm, v) + _dot_nt(qfw, stb)
                e_last = _pick_row(ec, CHUNK - 1)
                kdec = (kfw_f * e_last).astype(BF16)
                st_ref[hh] = st * e_last + _dot_tn(v, kdec)
                r = lax.rsqrt(jnp.mean(o * o, axis=-1, keepdims=True) + EPS)
                zo = zog_ref[sl, cv].astype(F32)
                o_ref[sl, cv] = o.astype(BF16)
                og_ref[sl, cv] = (o * r * gh_ref[...] * zo * _sigmoid(zo)).astype(BF16)

    rmap = lambda h, r: r
    return pl.pallas_call(
        body, name="gla_fwd", grid=(HEADS // GLA_HB, s // rb),
        in_specs=_gla_specs(rb, rmap) + [
            pl.BlockSpec((rb, wv), lambda h, r: (r, OFF_OG // wv + h)),
            pl.BlockSpec((128, wk), lambda h, r: (0, h)), pl.BlockSpec((1, wk), lambda h, r: (0, h)),
            pl.BlockSpec((1, HV), lambda h, r: (0, 0))],
        out_specs=[pl.BlockSpec((rb, wv), lambda h, r: (r, h)), pl.BlockSpec((rb, wv), lambda h, r: (r, h)),
                   pl.BlockSpec((nc, GLA_HB, HV, HK), lambda h, r: (r, h, 0, 0))],
        out_shape=[jax.ShapeDtypeStruct((s, D), BF16), jax.ShapeDtypeStruct((s, D), BF16),
                   jax.ShapeDtypeStruct((s // CHUNK, HEADS, HV, HK), BF16)],
        scratch_shapes=[pltpu.VMEM((GLA_HB, HV, HK), F32)],
        compiler_params=_cp("arbitrary", "arbitrary"),
    )(zr, zr, zr, zr, zr, wgk, bgk, ghead)


def _gla_bwd(zr, do, sp, wgk, bgk, after, rb):
    s = zr.shape[0]
    nc = rb // CHUNK
    nr = s // rb
    wk, wv = GLA_HB * HK, GLA_HB * HV

    def body(q_ref, k_ref, v_ref, zgk_ref, do_ref, sp_ref, wgk_ref, bgk_ref, after_ref, dq_ref, dk_ref, dv_ref, dg_ref,
             gt_ref, dbc_ref):
        @pl.when(pl.program_id(1) == 0)
        def _():
            gt_ref[...] = jnp.zeros_like(gt_ref)

        g, e_pos, e_neg = _gla_decay(zgk_ref, wgk_ref, bgk_ref, rb)
        lower = _rows((CHUNK, CHUNK)) >= lax.broadcasted_iota(jnp.int32, (CHUNK, CHUNK), 1)
        is_last = _rows((CHUNK, HK)) == CHUNK - 1
        for c in reversed(range(nc)):
            sl = slice(c * CHUNK, (c + 1) * CHUNK)
            for hh in range(GLA_HB):
                ck, cv = slice(hh * HK, (hh + 1) * HK), slice(hh * HV, (hh + 1) * HV)
                q = q_ref[sl, ck].astype(F32) * QSCALE
                k = k_ref[sl, ck].astype(F32)
                v = v_ref[sl, cv]
                dov = do_ref[sl, cv]
                ec, fc = e_pos[sl, ck], e_neg[sl, ck]
                qfw_f, kfw_f, qbw_f, kbw_f = q * ec, k * fc, q * fc, k * ec
                qfw, kfw, qbw, kbw = qfw_f.astype(BF16), kfw_f.astype(BF16), qbw_f.astype(BF16), kbw_f.astype(BF16)
                pm = jnp.where(lower, _dot_nt(qfw, kfw), _dot_nt(qbw, kbw)).astype(BF16)
                e_last = _pick_row(ec, CHUNK - 1)
                kdec = (kfw_f * e_last).astype(BF16)
                gt = gt_ref[hh]
                gtb = gt.astype(BF16)
                spv = sp_ref[c, hh]
                dp = _dot_nt(dov, v)
                dv_ref[sl, cv] = (_dot_tn(pm, dov) + _dot_nt(kdec, gtb)).astype(BF16)
                ds_fw = jnp.where(lower, dp, 0.0).astype(BF16)
                ds_bw = jnp.where(lower, 0.0, dp).astype(BF16)
                dqfw = _dot(ds_fw, kfw) + _dot(dov, spv)
                dkfw = _dot_tn(ds_fw, qfw)
                dqbw = _dot(ds_bw, kbw)
                dkbw = _dot_tn(ds_bw, qbw)
                dkdec = _dot(v, gtb)
                de_last = (jnp.sum(gt * spv.astype(F32), axis=0, keepdims=True)
                           + jnp.sum(dkdec * kfw_f, axis=0, keepdims=True))
                dkfw = dkfw + dkdec * e_last
                dq_ref[sl, ck] = ((dqfw * ec + dqbw * fc) * QSCALE).astype(BF16)
                dk_ref[sl, ck] = (dkfw * fc + dkbw * ec).astype(BF16)
                dbc = dqfw * qfw_f - dqbw * qbw_f + dkbw * kbw_f - dkfw * kfw_f
                dbc_ref[sl, ck] = dbc + jnp.where(is_last, de_last * e_last, 0.0)
                gt_ref[hh] = _dot_tn(dov, qfw) + gt * e_last
        rowm = _rows((rb, wk)) & (CHUNK - 1)
        dla, kk = dbc_ref[...], 1
        while kk < CHUNK:
            dla = dla + jnp.where(rowm < CHUNK - kk, pltpu.roll(dla, rb - kk, 0), 0.0)
            kk *= 2
        dg_ref[...] = dla * (1.0 / 16.0) * _sigmoid(-g)

    rmap = lambda h, r: nr - 1 - r
    rev = lambda h, r: (nr - 1 - r, h)
    return pl.pallas_call(
        body, name="gla_bwd", grid=(HEADS // GLA_HB, nr),
        in_specs=_gla_specs(rb, rmap) + [
            pl.BlockSpec((rb, wv), rev),
            pl.BlockSpec((nc, GLA_HB, HV, HK), lambda h, r: (nr - 1 - r, h, 0, 0)),
            pl.BlockSpec((128, wk), lambda h, r: (0, h)), pl.BlockSpec((1, wk), lambda h, r: (0, h)), ANY],
        out_specs=[pl.BlockSpec((rb, wk), rev), pl.BlockSpec((rb, wk), rev), pl.BlockSpec((rb, wv), rev),
                   pl.BlockSpec((rb, wk), rev)],
        out_shape=[jax.ShapeDtypeStruct((s, HEADS * HK), BF16), jax.ShapeDtypeStruct((s, HEADS * HK), BF16),
                   jax.ShapeDtypeStruct((s, D), BF16), jax.ShapeDtypeStruct((s, HEADS * HK), F32)],
        scratch_shapes=[pltpu.VMEM((GLA_HB, HV, HK), F32), pltpu.VMEM((rb, wk), F32)],
        compiler_params=_cp("arbitrary", "arbitrary"),
    )(zr, zr, zr, zr, do, sp, wgk, bgk, after)


def _gk_bwd(dgpre, zr, wgk, after, ts):
    s = zr.shape[0]

    def body(dg_ref, zgk_ref, w_ref, after_ref, dz_ref, dw_ref, db_ref):
        @pl.when(pl.program_id(0) == 0)
        def _():
            dw_ref[...] = jnp.zeros_like(dw_ref)
            db_ref[...] = jnp.zeros_like(db_ref)

        dg = dg_ref[...]
        dgb = dg.astype(BF16)
        dz_ref[...] = _dot_nt(dgb, w_ref[...].astype(BF16)).astype(BF16)
        dw_ref[...] += _dot_tn(zgk_ref[...], dgb)
        db_ref[...] += jnp.sum(dg, axis=0, keepdims=True)

    return pl.pallas_call(
        body, name="gk_bwd", grid=(s // ts,),
        in_specs=[pl.BlockSpec((ts, 512), lambda i: (i, 0)), pl.BlockSpec((ts, 128), lambda i: (i, OFF_GK // 128)),
                  pl.BlockSpec((128, 512), lambda i: (0, 0)), ANY],
        out_specs=[pl.BlockSpec((ts, 128), lambda i: (i, 0)), pl.BlockSpec((128, 512), lambda i: (0, 0)),
                   pl.BlockSpec((1, 512), lambda i: (0, 0))],
        out_shape=[jax.ShapeDtypeStruct((s, 128), BF16), jax.ShapeDtypeStruct((128, 512), F32),
                   jax.ShapeDtypeStruct((1, 512), F32)],
        compiler_params=_cp("arbitrary"),
    )(dgpre, zr, wgk, after)


def _merge_fwd(x, zr, pp, og, bgate, wpp, wgla, wout, gffn, ts):
    s = x.shape[0]

    def body(x_ref, z0_ref, z1_ref, pp_ref, og_ref, bg_ref, wpp_ref, wgla_ref, wout_ref, gf_ref,
             x1_ref, mix_ref, yp_ref, yg_ref, h2_ref):
        ppv = pp_ref[...]
        yp = jnp.concatenate([_dot(ppv, wpp_ref[j]) for j in range(4)], axis=1)
        yg = _dot(og_ref[...], wgla_ref[...])
        g0 = _sigmoid(z0_ref[...].astype(F32) + bg_ref[:, :D])
        g1 = _sigmoid(z1_ref[...].astype(F32) + bg_ref[:, D:])
        mixed = (g0 * yp + g1 * yg).astype(BF16)
        x1 = x_ref[...] + _dot(mixed, wout_ref[...])
        x1_ref[...] = x1
        mix_ref[...] = mixed
        yp_ref[...] = yp.astype(BF16)
        yg_ref[...] = yg.astype(BF16)
        r = lax.rsqrt(jnp.mean(x1 * x1, axis=-1, keepdims=True) + EPS)
        h2_ref[...] = (x1 * r * gf_ref[...]).astype(BF16)

    row = lambda i: (i, 0)
    const2 = lambda i: (0, 0)
    return pl.pallas_call(
        body, name="merge_fwd", grid=(s // ts,),
        in_specs=[pl.BlockSpec((ts, D), row), pl.BlockSpec((ts, D), lambda i: (i, 0)), pl.BlockSpec((ts, D), lambda i: (i, 1)),
                  pl.BlockSpec((ts, POOL_W), row), pl.BlockSpec((ts, D), row), pl.BlockSpec((1, 2 * D), const2),
                  pl.BlockSpec((4, POOL_W, 256), lambda i: (0, 0, 0)), pl.BlockSpec((D, D), const2),
                  pl.BlockSpec((D, D), const2), pl.BlockSpec((1, D), const2)],
        out_specs=[pl.BlockSpec((ts, D), row)] * 5,
        out_shape=[jax.ShapeDtypeStruct((s, D), F32)] + [jax.ShapeDtypeStruct((s, D), BF16)] * 4,
        compiler_params=_cp("arbitrary"),
    )(x, zr, zr, pp, og, bgate, wpp, wgla, wout, gffn)


def _merge_bwd(dx1b, zr, yp, yg, o, bgate, ghead, wpp, wgla, wout, after, ts):
    s = dx1b.shape[0]

    def body(dx_ref, z0_ref, z1_ref, zog_ref, yp_ref, yg_ref, o_ref, bg_ref, gh_ref, wpp_ref, wgla_ref, wout_ref, after_ref,
             dzg_ref, dyp_ref, dyg_ref, dpp_ref, do_ref, dzog_ref, dbg_ref, dgh_ref):
        @pl.when(pl.program_id(0) == 0)
        def _():
            dbg_ref[...] = jnp.zeros_like(dbg_ref)
            dgh_ref[...] = jnp.zeros_like(dgh_ref)

        dmix = _dot_nt(dx_ref[...], wout_ref[...])
        g0 = _sigmoid(z0_ref[...].astype(F32) + bg_ref[:, :D])
        g1 = _sigmoid(z1_ref[...].astype(F32) + bg_ref[:, D:])
        dypb = (dmix * g0).astype(BF16)
        dygb = (dmix * g1).astype(BF16)
        dz0 = dmix * yp_ref[...].astype(F32) * g0 * (1.0 - g0)
        dz1 = dmix * yg_ref[...].astype(F32) * g1 * (1.0 - g1)
        dzg_ref[:, :D] = dz0.astype(BF16)
        dzg_ref[:, D:] = dz1.astype(BF16)
        dbg_ref[:, :D] += jnp.sum(dz0, axis=0, keepdims=True)
        dbg_ref[:, D:] += jnp.sum(dz1, axis=0, keepdims=True)
        dyp_ref[...] = dypb
        dyg_ref[...] = dygb
        dpp = _dot_nt(dypb[:, 0:256], wpp_ref[0])
        for j in range(1, 4):
            dpp = dpp + _dot_nt(dypb[:, j * 256:(j + 1) * 256], wpp_ref[j])
        dpp_ref[...] = dpp.astype(BF16)
        dog = _dot_nt(dygb, wgla_ref[...])
        gh = gh_ref[...]
        dgh = jnp.zeros((1, HV), F32)
        for h in range(HEADS):
            cs = slice(h * HV, (h + 1) * HV)
            ov = o_ref[:, cs].astype(F32)
            r = lax.rsqrt(jnp.mean(ov * ov, axis=-1, keepdims=True) + EPS)
            oh = ov * r
            zo = zog_ref[:, cs].astype(F32)
            sg = _sigmoid(zo)
            dog_h = dog[:, cs]
            don = dog_h * zo * sg
            dzog_ref[:, cs] = (dog_h * oh * gh * sg * (1.0 + zo * (1.0 - sg))).astype(BF16)
            dgh = dgh + jnp.sum(don * oh, axis=0, keepdims=True)
            doh = don * gh
            do_ref[:, cs] = (r * (doh - oh * jnp.mean(doh * oh, axis=-1, keepdims=True))).astype(BF16)
        dgh_ref[...] += dgh

    row = lambda i: (i, 0)
    const2 = lambda i: (0, 0)
    return pl.pallas_call(
        body, name="merge_bwd", grid=(s // ts,),
        in_specs=[pl.BlockSpec((ts, D), row), pl.BlockSpec((ts, D), lambda i: (i, 0)), pl.BlockSpec((ts, D), lambda i: (i, 1)),
                  pl.BlockSpec((ts, D), lambda i: (i, OFF_OG // D)), pl.BlockSpec((ts, D), row), pl.BlockSpec((ts, D), row),
                  pl.BlockSpec((ts, D), row), pl.BlockSpec((1, 2 * D), const2), pl.BlockSpec((1, HV), const2),
                  pl.BlockSpec((4, POOL_W, 256), lambda i: (0, 0, 0)), pl.BlockSpec((D, D), const2),
                  pl.BlockSpec((D, D), const2), ANY],
        out_specs=[pl.BlockSpec((ts, 2 * D), row), pl.BlockSpec((ts, D), row), pl.BlockSpec((ts, D), row),
                   pl.BlockSpec((ts, POOL_W), row), pl.BlockSpec((ts, D), row), pl.BlockSpec((ts, D), row),
                   pl.BlockSpec((1, 2 * D), const2), pl.BlockSpec((1, HV), const2)],
        out_shape=[jax.ShapeDtypeStruct((s, 2 * D), BF16), jax.ShapeDtypeStruct((s, D), BF16),
                   jax.ShapeDtypeStruct((s, D), BF16), jax.ShapeDtypeStruct((s, POOL_W), BF16),
                   jax.ShapeDtypeStruct((s, D), BF16), jax.ShapeDtypeStruct((s, D), BF16),
                   jax.ShapeDtypeStruct((1, 2 * D), F32), jax.ShapeDtypeStruct((1, HV), F32)],
        compiler_params=_cp("arbitrary"),
    )(dx1b, zr, zr, zr, yp, yg, o, bgate, ghead, wpp, wgla, wout, after)


HALO = 16
CCH = 1408


def _conv_taps(u_ref, halo_ref, cs, first, ts):
    u = u_ref[:, cs].astype(F32)
    hal = halo_ref[:, cs].astype(F32)
    h1 = jnp.where(first, 0.0, _pick_row(hal, HALO - 1))
    h2 = jnp.where(first, 0.0, _pick_row(hal, HALO - 2))
    row8 = _rows((8, u.shape[1]))
    r1, r2 = pltpu.roll(u, 1, 0), pltpu.roll(u, 2, 0)
    r1 = jnp.concatenate([jnp.where(row8 == 0, h1, r1[:8]), r1[8:]], axis=0)
    r2 = jnp.concatenate([jnp.where(row8 == 0, h2, jnp.where(row8 == 1, h1, r2[:8])), r2[8:]], axis=0)
    return u, r1, r2


def _ffn_down_loss(u, x1, tgt, wconv, bconv, wdown, gfin, ts):
    s = x1.shape[0]

    def body(u_ref, halo_ref, x1_ref, t_ref, wc_ref, bc_ref, wd_ref, gf_ref, a_ref, c_ref, dx_ref, dxb_ref, ls_ref,
             dgf_ref):
        i = pl.program_id(0)

        @pl.when(i == 0)
        def _():
            ls_ref[...] = jnp.zeros_like(ls_ref)
            dgf_ref[...] = jnp.zeros_like(dgf_ref)

        first = i == 0
        acc = x1_ref[...]
        for hf in range(D_FF // CCH):
            cg = slice(hf * CCH, (hf + 1) * CCH)
            cv = slice(D_FF + hf * CCH, D_FF + (hf + 1) * CCH)
            vals = []
            for cs in (cg, cv):
                u0, u1, u2 = _conv_taps(u_ref, halo_ref, cs, first, ts)
                vals.append(bc_ref[:, cs] + wc_ref[0:1, cs] * u2 + wc_ref[1:2, cs] * u1 + wc_ref[2:3, cs] * u0)
                c_ref[:, cs] = vals[-1].astype(BF16)
            a = (vals[0] * _sigmoid(vals[0]) * vals[1]).astype(BF16)
            a_ref[:, cg] = a
            acc = acc + _dot(a, wd_ref[cg, :])
        r = lax.rsqrt(jnp.mean(acc * acc, axis=-1, keepdims=True) + EPS)
        xh = acc * r
        gf = gf_ref[...]
        err = xh * gf - t_ref[...]
        ls_ref[...] += (0.5 / D) * jnp.sum(jnp.sum(err * err, axis=-1, keepdims=True), axis=0, keepdims=True)
        dy = err * (1.0 / D)
        dgf_ref[...] += jnp.sum(dy * xh, axis=0, keepdims=True)
        dxh = dy * gf
        dx = r * (dxh - xh * jnp.mean(dxh * xh, axis=-1, keepdims=True))
        dx_ref[...] = dx
        dxb_ref[...] = dx.astype(BF16)

    row = lambda i: (i, 0)
    const2 = lambda i: (0, 0)
    return pl.pallas_call(
        body, name="ffn_down_loss", grid=(s // ts,),
        in_specs=[pl.BlockSpec((ts, N_UP), row),
                  pl.BlockSpec((HALO, N_UP), lambda i: (jnp.maximum(i * (ts // HALO) - 1, 0), 0)),
                  pl.BlockSpec((ts, D), row), pl.BlockSpec((ts, D), row), pl.BlockSpec((3, N_UP), const2),
                  pl.BlockSpec((1, N_UP), const2), pl.BlockSpec((D_FF, D), const2), pl.BlockSpec((1, D), const2)],
        out_specs=[pl.BlockSpec((ts, D_FF), row), pl.BlockSpec((ts, N_UP), row), pl.BlockSpec((ts, D), row),
                   pl.BlockSpec((ts, D), row), pl.BlockSpec((1, 128), const2), pl.BlockSpec((1, D), const2)],
        out_shape=[jax.ShapeDtypeStruct((s, D_FF), BF16), jax.ShapeDtypeStruct((s, N_UP), BF16),
                   jax.ShapeDtypeStruct((s, D), F32), jax.ShapeDtypeStruct((s, D), BF16),
                   jax.ShapeDtypeStruct((1, 128), F32), jax.ShapeDtypeStruct((1, D), F32)],
        compiler_params=_cp("arbitrary"),
    )(u, u, x1, tgt, wconv, bconv, wdown, gfin)


def _ffn_bwd(dx2b, u, c, wconv, wdown, ts):
    s = dx2b.shape[0]
    nt = s // ts

    def body(dx_ref, u_ref, c_ref, wc_ref, wd_ref, du_ref, db_ref, dw_ref, nxt_ref):
        @pl.when(pl.program_id(0) == 0)
        def _():
            db_ref[...] = jnp.zeros_like(db_ref)
            dw_ref[...] = jnp.zeros_like(dw_ref)
            nxt_ref[...] = jnp.zeros_like(nxt_ref)

        dxv = dx_ref[...]
        row8 = _rows((8, CCH))
        for hf in range(D_FF // CCH):
            cg = slice(hf * CCH, (hf + 1) * CCH)
            cv = slice(D_FF + hf * CCH, D_FF + (hf + 1) * CCH)
            da = _dot_nt(dxv, wd_ref[cg, :])
            gate = c_ref[:, cg].astype(F32)
            val = c_ref[:, cv].astype(F32)
            sg = _sigmoid(gate)
            dcs = (da * val * sg * (1.0 + gate * (1.0 - sg)), da * gate * sg)
            for cs, dc in zip((cg, cv), dcs):
                n1 = nxt_ref[0:1, cs]
                n2 = nxt_ref[1:2, cs]
                r1, r2 = pltpu.roll(dc, ts - 1, 0), pltpu.roll(dc, ts - 2, 0)
                f1 = jnp.concatenate([r1[:ts - 8], jnp.where(row8 == 7, n1, r1[ts - 8:])], axis=0)
                f2 = jnp.concatenate([r2[:ts - 8], jnp.where(row8 == 7, n2, jnp.where(row8 == 6, n1, r2[ts - 8:]))], axis=0)
                uv = u_ref[:, cs].astype(F32)
                db_ref[:, cs] += jnp.sum(dc, axis=0, keepdims=True)
                dw_ref[0:1, cs] += jnp.sum(f2 * uv, axis=0, keepdims=True)
                dw_ref[1:2, cs] += jnp.sum(f1 * uv, axis=0, keepdims=True)
                dw_ref[2:3, cs] += jnp.sum(dc * uv, axis=0, keepdims=True)
                du_ref[:, cs] = (wc_ref[2:3, cs] * dc + wc_ref[1:2, cs] * f1 + wc_ref[0:1, cs] * f2).astype(BF16)
                nxt_ref[:, cs] = dc[0:8, :]

    rev = lambda i: (nt - 1 - i, 0)
    const2 = lambda i: (0, 0)
    return pl.pallas_call(
        body, name="ffn_bwd", grid=(nt,),
        in_specs=[pl.BlockSpec((ts, D), rev), pl.BlockSpec((ts, N_UP), rev), pl.BlockSpec((ts, N_UP), rev),
                  pl.BlockSpec((3, N_UP), const2), pl.BlockSpec((D_FF, D), const2)],
        out_specs=[pl.BlockSpec((ts, N_UP), rev), pl.BlockSpec((1, N_UP), const2), pl.BlockSpec((3, N_UP), const2)],
        out_shape=[jax.ShapeDtypeStruct((s, N_UP), BF16), jax.ShapeDtypeStruct((1, N_UP), F32),
                   jax.ShapeDtypeStruct((3, N_UP), F32)],
        scratch_shapes=[pltpu.VMEM((8, N_UP), F32)],
        compiler_params=_cp("arbitrary"),
    )(dx2b, u, c, wconv, wdown)


ANY = pl.BlockSpec(memory_space=pl.ANY)


def _place():
    x, y, c = lax.axis_index("x"), lax.axis_index("y"), lax.axis_index("c")
    chips = [(1 - x, y), (x, 1 - y), (1 - x, 1 - y)]
    return x, y, c, chips


def _half(shape, c, axis):
    size = shape[axis] // 2
    cut = pl.ds(pl.multiple_of(c * size, 8 if axis == 0 else 128), size)
    return (cut, slice(None)) if axis == 0 else (slice(None), cut)


def _half_shape(shape, axis):
    return (shape[0] // 2, shape[1]) if axis == 0 else (shape[0], shape[1] // 2)


def _remote(src, dst, send_sems, recv_sems, k, to):
    return pltpu.make_async_remote_copy(src_ref=src, dst_ref=dst, send_sem=send_sems.at[k], recv_sem=recv_sems.at[k],
                                        device_id=to, device_id_type=MESH)


def _sibling_exchange(grads, axes, smalls, name):
    nb = len(grads)
    n = nb + len(smalls)

    def body(*refs):
        ins, outs = refs[:n], refs[n:2 * n]
        send_sems, recv_sems = refs[2 * n:]
        x, y, c, _ = _place()
        sib = (x, y, 1 - c)
        cps = []
        for a in range(nb):
            theirs = _half(grads[a].shape[1:], 1 - c, axes[a])
            cps.append(_remote(ins[a].at[(slice(None),) + theirs], outs[a], send_sems, recv_sems, a, sib))
        for a in range(nb, n):
            cps.append(_remote(ins[a], outs[a], send_sems, recv_sems, a, sib))
        for cp in cps:
            cp.start()
        for cp in cps:
            cp.wait()

    out_shape = [jax.ShapeDtypeStruct((4,) + _half_shape(g.shape[1:], ax), g.dtype) for g, ax in zip(grads, axes)]
    out_shape += [jax.ShapeDtypeStruct(a.shape, F32) for a in smalls]
    return pl.pallas_call(
        body, name=name, in_specs=[ANY] * n, out_specs=[ANY] * n, out_shape=out_shape,
        scratch_shapes=[pltpu.SemaphoreType.DMA((n,)), pltpu.SemaphoreType.DMA((n,))],
        compiler_params=pltpu.CompilerParams(has_side_effects=True),
    )(*grads, *smalls)


def _gather_share(lands, axes, name):
    n = len(lands)

    def body(*refs):
        outs = refs[n:2 * n]
        send_sems, recv_sems = refs[2 * n:]
        x, y, c, chips = _place()
        sib = (x, y, 1 - c)
        cps = []
        for a in range(n):
            mine = _half(lands[a].shape[1:], c, axes[a])
            for k, ch in enumerate(chips):
                landed = outs[a].at[(2 * ch[0] + ch[1],) + mine]
                cps.append(_remote(landed, landed, send_sems, recv_sems, 3 * a + k, sib))
        for cp in cps:
            cp.start()
        for a in range(n):
            other = _half(lands[a].shape[1:], 1 - c, axes[a])
            for k, ch in enumerate(chips):
                landed = outs[a].at[(2 * ch[0] + ch[1],) + other]
                _remote(landed, landed, send_sems, recv_sems, 3 * a + k, sib).wait_recv()
        for cp in cps:
            cp.wait_send()

    return pl.pallas_call(
        body, name=name, in_specs=[ANY] * n, out_specs=[ANY] * n,
        out_shape=[jax.ShapeDtypeStruct(a.shape, a.dtype) for a in lands],
        input_output_aliases={a: a for a in range(n)},
        scratch_shapes=[pltpu.SemaphoreType.DMA((3 * n,)), pltpu.SemaphoreType.DMA((3 * n,))],
        compiler_params=pltpu.CompilerParams(has_side_effects=True),
    )(*lands)


def _sibling_share(halves, name):
    n = len(halves)

    def body(*refs):
        ins, outs = refs[:n], refs[n:2 * n]
        send_sems, recv_sems = refs[2 * n:]
        x, y, c, _ = _place()
        cps = [_remote(ins[a], outs[a], send_sems, recv_sems, a, (x, y, 1 - c)) for a in range(n)]
        for cp in cps:
            cp.start()
        for cp in cps:
            cp.wait()

    return pl.pallas_call(
        body, name=name, in_specs=[ANY] * n, out_specs=[ANY] * n,
        out_shape=[jax.ShapeDtypeStruct(h.shape, F32) for h in halves],
        scratch_shapes=[pltpu.SemaphoreType.DMA((n,)), pltpu.SemaphoreType.DMA((n,))],
        compiler_params=pltpu.CompilerParams(has_side_effects=True),
    )(*halves)


HBM = pl.BlockSpec(memory_space=pltpu.HBM)
SEM = pl.BlockSpec(memory_space=pltpu.SEMAPHORE)
DATAFLOW = pltpu.SideEffectType.DATAFLOW_SIDE_EFFECTING


def _split_start(name, srcs, land_shapes, plan, n_copies, after):
    lands = [lax.empty(shp, dt) for shp, dt in land_shapes]
    bufs = list(srcs) + lands
    nb, ns = len(bufs), len(srcs)

    def body(*refs):
        send_sems, recv_sems, token = refs[nb + 1], refs[nb + 2], refs[-1]
        for k, (src, dst, to) in enumerate(plan(refs[:ns], refs[ns:nb])):
            _remote(src, dst, send_sems, recv_sems, k, to).start()
        token[...] = jnp.zeros_like(token)

    res = pl.pallas_call(
        body, name=name,
        out_shape=(pltpu.SemaphoreType.DMA((n_copies,)), pltpu.SemaphoreType.DMA((n_copies,)),
                   *[pltpu.HBM(b.shape, b.dtype) for b in bufs], jax.ShapeDtypeStruct((8, 128), F32)),
        in_specs=[HBM] * nb + [ANY],
        out_specs=(SEM, SEM, *[HBM] * nb, pl.BlockSpec(memory_space=pltpu.VMEM)),
        input_output_aliases={i: 2 + i for i in range(nb)},
        compiler_params=pltpu.CompilerParams(has_side_effects=DATAFLOW),
    )(*[pltpu.with_memory_space_constraint(b, pltpu.HBM) for b in bufs], after)
    return (res[0], res[1], list(res[2:2 + nb])), res[-1]


def _split_wait(name, handle, n_srcs, plan, after):
    send_sems, recv_sems, bufs = handle
    nb = len(bufs)

    def body(*refs):
        sends, recvs = refs[nb], refs[nb + 1]
        for k, (src, dst, to) in enumerate(plan(refs[:n_srcs], refs[n_srcs:nb])):
            cp = _remote(src, dst, sends, recvs, k, to)
            cp.wait_send()
            cp.wait_recv()

    res = pl.pallas_call(
        body, name=name, out_shape=[pltpu.HBM(b.shape, b.dtype) for b in bufs],
        in_specs=[HBM] * nb + [SEM, SEM, ANY], out_specs=[HBM] * nb,
        input_output_aliases={i: i for i in range(nb)},
        compiler_params=pltpu.CompilerParams(has_side_effects=DATAFLOW),
    )(*bufs, send_sems, recv_sems, after)
    return list(res[:n_srcs]), list(res[n_srcs:])


def _gather_plan(shapes, axes, n_whole=0):
    def plan(srcs, lands):
        x, y, c, chips = _place()
        out = []
        for a, (shape, axis) in enumerate(zip(shapes, axes)):
            mine = _half(shape, c, axis)
            for ch in chips:
                out.append((srcs[a].at[mine], lands[a].at[(2 * x + y,) + mine], (ch[0], ch[1], c)))
        for a in range(len(shapes), len(shapes) + n_whole):
            for ch in chips:
                out.append((srcs[a], lands[a].at[2 * x + y], (ch[0], ch[1], c)))
        return out
    return plan


def _sibling_plan(shapes, axes):
    def plan(srcs, lands):
        x, y, c, _ = _place()
        return [(srcs[a].at[(slice(None),) + _half(shape, 1 - c, axis)], lands[a], (x, y, 1 - c))
                for a, (shape, axis) in enumerate(zip(shapes, axes))]
    return plan


def _reduce_plan(n_big, n_small):
    def plan(srcs, lands):
        x, y, c, chips = _place()
        out = []
        for a in range(n_big):
            for k, ch in enumerate(chips):
                out.append((srcs[a].at[2 * ch[0] + ch[1]], lands[a].at[k], (ch[0], ch[1], c)))
        for a in range(n_big, n_big + n_small):
            for ch in chips:
                out.append((srcs[a], lands[a].at[2 * x + y], (ch[0], ch[1], c)))
        return out
    return plan


def _row_tile(rows, cols, mult):
    best = mult
    for t in range(mult, rows + 1, mult):
        if rows % t == 0 and t * cols * 4 <= (2 << 20):
            best = t
    return best if rows % best == 0 else rows


COL_TILE = 256


def _half_tiling(hshape, axis, mult):
    hr, hc = hshape
    if axis == 0:
        tr = _row_tile(hr, hc, mult)
        return tr, hc, hr // tr
    return hr, COL_TILE, hc // COL_TILE


def _tile_idx(axis, t):
    return (t, 0) if axis == 0 else (0, t)


def _chip_partial(place, g, t, axis, name):
    hshape = t.shape[1:]
    br, bc, nt = _half_tiling(hshape, axis, 16)

    def body(pl_ref, g_ref, t_ref, pf_ref, pb_ref):
        v = g_ref[...].astype(F32) + t_ref[...].astype(F32)
        pb_ref[...] = v.astype(BF16)

        @pl.when(pl.program_id(1) == pl_ref[0])
        def _():
            pf_ref[...] = v

    blk = (None, br, bc)
    return pl.pallas_call(
        body, name=name,
        grid_spec=pltpu.PrefetchScalarGridSpec(
            num_scalar_prefetch=1, grid=(nt, 4),
            in_specs=[pl.BlockSpec(blk, lambda i, j, p: (j,) + _tile_idx(axis, p[1] * nt + i)),
                      pl.BlockSpec(blk, lambda i, j, p: (j,) + _tile_idx(axis, i))],
            out_specs=[pl.BlockSpec((br, bc), lambda i, j, p: _tile_idx(axis, i)),
                       pl.BlockSpec(blk, lambda i, j, p: (j,) + _tile_idx(axis, i))]),
        out_shape=[jax.ShapeDtypeStruct(hshape, F32), jax.ShapeDtypeStruct((4,) + hshape, BF16)],
        compiler_params=_cp("arbitrary", "arbitrary"),
    )(place, g, t)


def _finish_half(pf, rb, axis, name):
    hshape = pf.shape
    br, bc, nt = _half_tiling(hshape, axis, 16)

    def body(pf_ref, rb_ref, o_ref):
        o_ref[...] = ((pf_ref[...] + rb_ref[0].astype(F32)) + rb_ref[1].astype(F32)) + rb_ref[2].astype(F32)

    return pl.pallas_call(
        body, name=name, grid=(nt,),
        in_specs=[pl.BlockSpec((br, bc), lambda i: _tile_idx(axis, i)),
                  pl.BlockSpec((3, br, bc), lambda i: (0,) + _tile_idx(axis, i))],
        out_specs=pl.BlockSpec((br, bc), lambda i: _tile_idx(axis, i)),
        out_shape=jax.ShapeDtypeStruct(hshape, F32),
        compiler_params=_cp("arbitrary"),
    )(pf, rb)


def _adam_math(w, g, m, v):
    m = ADAM_B1 * m + (1.0 - ADAM_B1) * g
    v = ADAM_B2 * v + (1.0 - ADAM_B2) * (g * g)
    m_hat = m / (1.0 - ADAM_B1 ** ADAM_STEP)
    v_hat = v / (1.0 - ADAM_B2 ** ADAM_STEP)
    return -ADAM_LR * (m_hat / (jnp.sqrt(v_hat) + ADAM_EPS) + ADAM_WD * w), m, v


def _adam_halves(place, w, mine, theirs, m, v, axis, name):
    br, bc, nt = _half_tiling(mine.shape, axis, 8)

    def body(pl_ref, w_ref, a_ref, b_ref, m_ref, v_ref, g_ref, d_ref, mo_ref, vo_ref):
        is_mine = pl.program_id(0) // nt == pl_ref[1]
        g = jnp.where(is_mine, a_ref[...], b_ref[...])
        d, mn, vn = _adam_math(w_ref[...], g, m_ref[...], v_ref[...])
        g_ref[...] = g
        d_ref[...] = d
        mo_ref[...] = mn
        vo_ref[...] = vn

    full = pl.BlockSpec((br, bc), lambda i, p: _tile_idx(axis, i))
    mine_spec = pl.BlockSpec((br, bc), lambda i, p: _tile_idx(axis, jnp.where(i // nt == p[1], i % nt, nt - 1)))
    theirs_spec = pl.BlockSpec((br, bc), lambda i, p: _tile_idx(axis, jnp.where(i // nt == p[1], 0, i % nt)))
    return pl.pallas_call(
        body, name=name,
        grid_spec=pltpu.PrefetchScalarGridSpec(
            num_scalar_prefetch=1, grid=(2 * nt,), in_specs=[full, mine_spec, theirs_spec, full, full],
            out_specs=[full] * 4),
        out_shape=[jax.ShapeDtypeStruct(w.shape, F32)] * 4, compiler_params=_cp("arbitrary"),
    )(place, w, mine, theirs, m, v)


def _add_many(xs, ys, name):
    n = len(xs)

    def body(*refs):
        for i in range(n):
            refs[2 * n + i][...] = refs[i][...] + refs[n + i][...]

    return pl.pallas_call(body, name=name, out_shape=[jax.ShapeDtypeStruct(a.shape, F32) for a in xs])(*xs, *ys)


def _adam_small(place, owns, landed, ws, ms, vs, widths):
    n, nw = len(owns), len(ws)

    def body(pl_ref, *refs):
        own_r, land_r = refs[:n], refs[n:2 * n]
        w_r, m_r, v_r = (refs[2 * n + k * nw:2 * n + (k + 1) * nw] for k in range(3))
        outs = refs[2 * n + 3 * nw:]
        g_o, d_o, m_o, v_o = outs[:n], outs[n:n + nw], outs[n + nw:n + 2 * nw], outs[n + 2 * nw:]
        for me in range(4):
            @pl.when(pl_ref[0] == me)
            def _(me=me):
                for i in range(n):
                    p = [own_r[i][...] if k == me else land_r[i][k] for k in range(4)]
                    g = ((p[0] + p[1]) + p[2]) + p[3]
                    if i < nw and widths[i]:
                        g = g[:, me * widths[i]:(me + 1) * widths[i]]
                    g_o[i][...] = g
                    if i < nw:
                        d, mn, vn = _adam_math(w_r[i][...], g, m_r[i][...], v_r[i][...])
                        d_o[i][...] = d
                        m_o[i][...] = mn
                        v_o[i][...] = vn

    g_shapes = [jax.ShapeDtypeStruct(ws[i].shape if i < nw else owns[i].shape, F32) for i in range(n)]
    w_shapes = [jax.ShapeDtypeStruct(w.shape, F32) for w in ws]
    whole = lambda a: pl.BlockSpec(a.shape, lambda i, p, nd=len(a.shape): (0,) * nd)
    ins = list(owns) + list(landed) + list(ws) + list(ms) + list(vs)
    out_shape = g_shapes + w_shapes * 3
    out = pl.pallas_call(
        body, name="adam_small",
        grid_spec=pltpu.PrefetchScalarGridSpec(num_scalar_prefetch=1, grid=(1,), in_specs=[whole(a) for a in ins],
                                               out_specs=[whole(a) for a in out_shape]),
        out_shape=out_shape, compiler_params=_cp("arbitrary"),
    )(place, *ins)
    return out[:n], out[n:n + nw], out[n + nw:n + 2 * nw], out[n + 2 * nw:]


def kernel(x, g_mix, w_in, b_gate, w_gk_up, b_gk, w_pool_grp, pool_scale, g_gla_head, w_pool_proj, w_gla_proj, w_out, g_ffn, w_up, w_conv, b_conv, w_down, g_final, loss_target, m_g_mix, m_w_in, m_b_gate, m_w_gk_up, m_b_gk, m_w_pool_grp, m_pool_scale, m_g_gla_head, m_w_pool_proj, m_w_gla_proj, m_w_out, m_g_ffn, m_w_up, m_w_conv, m_b_conv, m_w_down, m_g_final, v_g_mix, v_w_in, v_b_gate, v_w_gk_up, v_b_gk, v_w_pool_grp, v_pool_scale, v_g_gla_head, v_w_pool_proj, v_w_gla_proj, v_w_out, v_g_ffn, v_w_up, v_w_conv, v_b_conv, v_w_down, v_g_final):
    s = x.shape[1]
    ts = min(s, 512)
    tm = min(s, 256)
    cx, cy, cc = lax.axis_index("x"), lax.axis_index("y"), lax.axis_index("c")
    chip = 2 * cx + cy
    place = jnp.stack([chip, cc]).astype(jnp.int32)

    big_names = ("w_in", "w_pool_proj", "w_gla_proj", "w_out", "w_up", "w_down")
    axes = (1, 0, 0, 0, 0, 0)
    shards = dict(w_in=jnp.transpose(w_in[0]), w_pool_proj=w_pool_proj[0], w_gla_proj=w_gla_proj[0], w_out=w_out[0],
                  w_up=w_up[0], w_down=w_down[0])
    def fill_own(lands, mine):
        return [lax.dynamic_update_slice(g, o_[None], (chip, 0, 0)) for g, o_ in zip(lands, mine)]

    def gather_start(tag, halves, group_axes, whole, after):
        plan = _gather_plan([o_.shape for o_ in halves], group_axes, len(whole))
        srcs = list(halves) + list(whole)
        handle, token = _split_start("gather_" + tag + "_start", srcs, [((4,) + o_.shape, o_.dtype) for o_ in srcs], plan,
                                     3 * len(srcs), after)
        return (handle, plan, len(halves), len(srcs), group_axes), token

    def gather_finish(tag, started, after):
        handle, plan, n_halves, n, group_axes = started
        mine, lands = _split_wait("gather_" + tag + "_wait", handle, n, plan, after)
        lands[:n_halves] = _gather_share(lands[:n_halves], group_axes, "gather_" + tag + "_share")
        return fill_own(lands, mine)

    in_w, tok = gather_start("in", [shards["w_in"].astype(BF16)], axes[:1], [], g_mix)
    zero = tok[0, 0]
    own = [(shards[n] + zero).astype(BF16) for n in big_names[1:]]
    mix_w, tok = gather_start("mix", own[0:3], axes[1:4], [w_gk_up[0] + zero, w_conv[0] + zero], tok)
    ffn_w, tok = gather_start("ffn", own[3:5], axes[4:6], [], tok)
    xs, tgt = x[0], loss_target[0]
    wgrp = w_pool_grp[0]
    h = _rmsnorm(xs, g_mix, tok, "norm_mix", ts)
    m_in_t, v_in_t = jnp.transpose(m_w_in[0]), jnp.transpose(v_w_in[0])
    h, m_in_t, v_in_t = lax.optimization_barrier((h, m_in_t, v_in_t))
    w_in_t = gather_finish("in", in_w, h)[0].reshape(N_IN, D)
    w_rt = jnp.concatenate([w_in_t[3600:], w_in_t[1536:3584], w_in_t[0:1536], w_in_t[3584:3600],
                            jnp.zeros((128 - GATE_RANK, D), BF16)], axis=0)
    nsh = N_IN // 4

    zr = _matmul_resident(h, w_rt, "in_proj", 1152, transposed=True)
    p, pp = _pool_fwd(zr, wgrp, pool_scale)
    wpp, wgla, wout, wgk4, wconv4 = gather_finish("mix", mix_w, pp)
    wgla, wout = wgla.reshape(D, D), wout.reshape(D, D)
    wgk_full = jnp.transpose(wgk4, (1, 0, 2)).reshape(GATE_RANK, 512)
    wconv_full = jnp.transpose(wconv4, (1, 0, 2)).reshape(3, N_UP)
    wgk_pad = jnp.concatenate([wgk_full, jnp.zeros((128 - GATE_RANK, 512), F32)], axis=0)
    o, og, sp = _gla_fwd(zr, wgk_pad, b_gk, g_gla_head, ts)
    x1, mixed, yp, yg, h2 = _merge_fwd(xs, zr, pp, og, b_gate, wpp, wgla, wout, g_ffn, ts)
    wup, wdown = gather_finish("ffn", ffn_w, x1)
    wdown = wdown.reshape(D_FF, D)
    u = _matmul_resident(h2, wup, "ffn_up", None)
    a, conv_out, dx2, dx2b, loss_part, dgfin = _ffn_down_loss(u, x1, tgt, wconv_full, b_conv, wdown,
                                                              g_final.reshape(1, D), tm)

    du, dbconv, dwconv = _ffn_bwd(dx2b, u, conv_out, wconv_full, wdown, tm)
    dw_down = _matmul_tn(a, dx2b, "dw_down", D, tm=1408)
    dw_up = _matmul_tn(h2, du, "dw_up", 1408, shard_major=True)

    def exchange_start(tag, grads, group_axes, after):
        plan = _sibling_plan([g.shape[1:] for g in grads], group_axes)
        lands = [((4,) + _half_shape(g.shape[1:], ax), g.dtype) for g, ax in zip(grads, group_axes)]
        handle, token = _split_start("sibling_" + tag + "_start", grads, lands, plan, len(grads), after)
        return (handle, plan, len(grads)), token

    def partials(tag, names, group_axes, exchange, after):
        handle, plan, n = exchange
        mine, theirs = _split_wait("sibling_" + tag + "_wait", handle, n, plan, after)
        return zip(*[_chip_partial(place, g, t, ax, "chip_partial_" + nm)
                     for nm, ax, g, t in zip(names, group_axes, mine, theirs)])

    ffn_names, ffn_axes = ("w_up", "w_down"), (0, 0)
    ffn_x, token = exchange_start("ffn", [dw_up, dw_down.reshape(4, 704, D)], ffn_axes, du)
    dx1, dx1b, dgffn = _matmul_nt_normbwd(du, wup, x1, g_ffn, dx2, token, "ffn_up_bwd", ts)
    ffn_pf, ffn_pb = partials("ffn", ffn_names, ffn_axes, ffn_x, dx1b)
    ffn_plan = _reduce_plan(2, 0)
    ffn_handle, token = _split_start("reduce_ffn_start", ffn_pb, [((3,) + p.shape[1:], BF16) for p in ffn_pb],
                                     ffn_plan, 6, ffn_pf[0])

    dzg, dyp, dyg, dpp, do, dzog, dbgate, dghead = _merge_bwd(dx1b, zr, yp, yg, o, b_gate, g_gla_head, wpp, wgla, wout,
                                                             token, ts)
    dw_out = _matmul_tn(mixed, dx1b, "dw_out", D)
    dw_gla = _matmul_tn(og, dyg, "dw_gla", D)
    dw_pp = _matmul_tn(pp, dyp, "dw_pp", 256, shard_major=True)

    out_names, out_axes = ("w_pool_proj", "w_gla_proj", "w_out"), (0, 0, 0)
    out_x, token = exchange_start("out", [dw_pp, dw_gla.reshape(4, 256, D), dw_out.reshape(4, 256, D)], out_axes, dpp)
    dzp, dwgrp, dscale = _pool_bwd(p, dpp, wgrp, pool_scale, token)
    out_pf, out_pb = partials("out", out_names, out_axes, out_x, dzp)
    out_plan = _reduce_plan(3, 0)
    out_handle, token = _split_start("reduce_out_start", out_pb, [((3,) + p_.shape[1:], BF16) for p_ in out_pb],
                                     out_plan, 9, out_pf[0])
    dq, dk, dv, dgpre = _gla_bwd(zr, do, sp, wgk_pad, b_gk, token, ts)
    dzgk, dwgk, dbgk = _gk_bwd(dgpre, zr, wgk_pad, dgpre, ts)
    dzr = jnp.concatenate([dzg, dv, dzog, dzp, dq, dk, dzgk], axis=1)
    dw_rt = _matmul_tn(dzr, h, "dw_in", D, tm=1152)

    def grad_rows(lo, hi):
        out = []
        for seg_lo, seg_hi, at in ((0, 1536, OFF_POOL), (1536, 3584, OFF_V), (3584, 3600, OFF_GK), (3600, N_IN, OFF_GATE)):
            a_, b_ = max(lo, seg_lo), min(hi, seg_hi)
            if a_ < b_:
                out.append(dw_rt[at + a_ - seg_lo:at + b_ - seg_lo])
        return jnp.concatenate(out, axis=0)

    dw_in_t = jnp.stack([grad_rows(j * nsh, (j + 1) * nsh) for j in range(4)])

    in_sib = _sibling_exchange([dw_in_t], (1,), [], "sibling_exchange_in")
    in_pf, in_pb = _chip_partial(place, dw_in_t, in_sib[0], 1, "chip_partial_w_in")
    in_plan = _reduce_plan(1, 0)
    in_handle, token = _split_start("reduce_in_start", [in_pb], [((3,) + in_pb.shape[1:], BF16)], in_plan, 3, in_pf)
    grad_x, _, dgmix = _matmul_nt_normbwd(dzr, w_rt, xs, g_mix, dx1, token, "in_proj_bwd", ts, transposed=True)
    small_names = ("g_mix", "b_gate", "w_gk_up", "b_gk", "w_pool_grp", "pool_scale", "g_gla_head", "g_ffn", "w_conv",
                   "b_conv", "g_final")
    small_mine = [dgmix, dbgate, dwgk[:GATE_RANK], dbgk, dwgrp.reshape(4 * 128, 128), dscale, dghead, dgffn, dwconv, dbconv,
                  dgfin, loss_part]
    small_sib = _sibling_exchange([], (), small_mine, "sibling_exchange_small")
    small_chip = _add_many(small_mine, small_sib, "chip_partial_small")
    small_plan = _reduce_plan(0, len(small_chip))
    small_handle, token = _split_start("reduce_small_start", small_chip, [((4,) + a_.shape, F32) for a_ in small_chip],
                                       small_plan, 3 * len(small_chip), small_mine[0])

    ms = dict(w_in=m_in_t, w_pool_proj=m_w_pool_proj[0], w_gla_proj=m_w_gla_proj[0], w_out=m_w_out[0],
              w_up=m_w_up[0], w_down=m_w_down[0])
    vs = dict(w_in=v_in_t, w_pool_proj=v_w_pool_proj[0], w_gla_proj=v_w_gla_proj[0], w_out=v_w_out[0],
              w_up=v_w_up[0], w_down=v_w_down[0])
    grad, delta, new_m, new_v = {}, {}, {}, {}

    def finish_and_update(names, group_axes, part_f, landed, tag):
        halves = [_finish_half(pf, rb, ax, "finish_" + n) for n, ax, pf, rb in zip(names, group_axes, part_f, landed)]
        sib_halves = _sibling_share(halves, "sibling_share_" + tag)
        for n, ax, mine, theirs in zip(names, group_axes, halves, sib_halves):
            res = _adam_halves(place, shards[n], mine, theirs, ms[n], vs[n], ax, "adam_" + n)
            if n == "w_in":
                res = [jnp.transpose(r_) for r_ in res]
            grad[n], delta[n], new_m[n], new_v[n] = [r_[None] for r_ in res]

    _, ffn_landed = _split_wait("reduce_ffn_wait", ffn_handle, 2, ffn_plan, token)
    _, out_landed = _split_wait("reduce_out_wait", out_handle, 3, out_plan, ffn_landed[0])
    finish_and_update(ffn_names + out_names, ffn_axes + out_axes, ffn_pf + out_pf, ffn_landed + out_landed, "rest")
    _, in_landed = _split_wait("reduce_in_wait", in_handle, 1, in_plan, delta["w_out"])
    finish_and_update(("w_in",), (1,), (in_pf,), in_landed, "in")
    small_sent, small_landed = _split_wait("reduce_small_wait", small_handle, len(small_chip), small_plan, delta["w_in"])
    given = dict(g_mix=(g_mix, m_g_mix, v_g_mix), b_gate=(b_gate, m_b_gate, v_b_gate), w_gk_up=(w_gk_up, m_w_gk_up, v_w_gk_up),
                 b_gk=(b_gk, m_b_gk, v_b_gk), w_pool_grp=(w_pool_grp, m_w_pool_grp, v_w_pool_grp),
                 pool_scale=(pool_scale, m_pool_scale, v_pool_scale), g_gla_head=(g_gla_head, m_g_gla_head, v_g_gla_head),
                 g_ffn=(g_ffn, m_g_ffn, v_g_ffn), w_conv=(w_conv, m_w_conv, v_w_conv), b_conv=(b_conv, m_b_conv, v_b_conv),
                 g_final=(g_final, m_g_final, v_g_final))
    flat2 = lambda a: a.reshape(-1, a.shape[-1])
    widths = [dict(w_gk_up=128, w_conv=1408).get(n) for n in small_names]
    totals, ds, mo, vo = _adam_small(place, small_sent, small_landed, *[[flat2(given[n][k]) for n in small_names] for k in range(3)],
                                     widths)
    loss = totals[-1][0, 0]
    for i, n in enumerate(small_names):
        shp = given[n][0].shape
        grad[n], delta[n], new_m[n], new_v[n] = [r_.reshape(shp) for r_ in (totals[i], ds[i], mo[i], vo[i])]

    order = ("g_mix", "w_in", "b_gate", "w_gk_up", "b_gk", "w_pool_grp", "pool_scale", "g_gla_head", "w_pool_proj",
             "w_gla_proj", "w_out", "g_ffn", "w_up", "w_conv", "b_conv", "w_down", "g_final")
    return (loss, grad_x[None], *[grad[n] for n in order], *[delta[n] for n in order], *[new_m[n] for n in order],
            *[new_v[n] for n in order])
```

```python
import functools

import jax
import jax.numpy as jnp
from jax import lax
from jax.experimental import pallas as pl
from jax.experimental.pallas import tpu as pltpu

F32 = jnp.float32
BF16 = jnp.bfloat16
MESH = pl.DeviceIdType.MESH

D = 1024
EPS = 1e-6
CHUNK = 64
POOL_W = 512
POOL_WINDOWS = (2, 4, 8, 16)
HEADS = 4
HK = 128
HV = 256
GATE_RANK = 16
D_FF = 2816
N_UP = 2 * D_FF
N_IN = 5648
QSCALE = HK ** -0.5
N_INR = 5760
OFF_GATE, OFF_V, OFF_OG, OFF_POOL, OFF_Q, OFF_K, OFF_GK = 0, 2048, 3072, 4096, 4608, 5120, 5632

ADAM_LR, ADAM_B1, ADAM_B2, ADAM_EPS, ADAM_WD, ADAM_STEP = 0.001, 0.9, 0.999, 1e-08, 0.01, 10

VMEM_LIMIT = 56 * 1024 * 1024


def _cp(*sem):
    return pltpu.CompilerParams(dimension_semantics=sem if sem else None, vmem_limit_bytes=VMEM_LIMIT)


def _dot(a, b):
    return jnp.dot(a, b, preferred_element_type=F32)


def _dot_nt(a, b):
    return lax.dot_general(a, b, (((1,), (1,)), ((), ())), preferred_element_type=F32)


def _dot_tn(a, b):
    return lax.dot_general(a, b, (((0,), (0,)), ((), ())), preferred_element_type=F32)


def _sigmoid(v):
    return 1.0 / (1.0 + jnp.exp(-v))


def _rows(shape):
    return lax.broadcasted_iota(jnp.int32, shape, 0)


def _pick_row(v, r):
    return jnp.sum(jnp.where(_rows(v.shape) == r, v, 0.0), axis=0, keepdims=True)


def _rmsnorm(x, g, after, name, ts):
    s = x.shape[0]

    def body(x_ref, g_ref, after_ref, h_ref):
        xv = x_ref[...]
        r = lax.rsqrt(jnp.mean(xv * xv, axis=-1, keepdims=True) + EPS)
        h_ref[...] = (xv * r * g_ref[...]).astype(BF16)

    return pl.pallas_call(
        body, name=name, grid=(s // ts,),
        in_specs=[pl.BlockSpec((ts, D), lambda i: (i, 0)), pl.BlockSpec((1, D), lambda i: (0, 0)), ANY],
        out_specs=pl.BlockSpec((ts, D), lambda i: (i, 0)), out_shape=jax.ShapeDtypeStruct((s, D), BF16),
        compiler_params=_cp("arbitrary"),
    )(x, g, after)


MM_ROWS = 512


def _matmul_resident(h, w, name, tn, transposed=False):
    s = h.shape[0]
    if transposed:
        nj = w.shape[0] // tn
        w_spec = pl.BlockSpec((tn, D), lambda j: (j, 0))
    elif w.ndim == 3:
        nj, tn = w.shape[0], w.shape[2]
        w_spec = pl.BlockSpec((None, D, tn), lambda j: (j, 0, 0))
    else:
        nj = w.shape[1] // tn
        w_spec = pl.BlockSpec((D, tn), lambda j: (0, j))
    mm = _dot_nt if transposed else _dot
    rc = min(s, MM_ROWS)

    def body(h_ref, w_ref, z_ref):
        for r0 in range(0, s, rc):
            z_ref[r0:r0 + rc, :] = mm(h_ref[r0:r0 + rc, :], w_ref[...]).astype(BF16)

    return pl.pallas_call(
        body, name=name, grid=(nj,),
        in_specs=[pl.BlockSpec((s, D), lambda j: (0, 0)), w_spec],
        out_specs=pl.BlockSpec((s, tn), lambda j: (0, j)), out_shape=jax.ShapeDtypeStruct((s, nj * tn), BF16),
        compiler_params=_cp("arbitrary"),
    )(h, w)


def _matmul_nt_normbwd(dz, w, x, g, resid, after, name, ts, transposed=False):
    s = x.shape[0]

    def body(dz_ref, w_hbm, x_ref, g_ref, r_ref, after_ref, o_ref, ob_ref, dg_ref, w_ref, sem):
        @pl.when(pl.program_id(0) == 0)
        def _():
            cp = pltpu.make_async_copy(w_hbm, w_ref, sem)
            cp.start()
            cp.wait()
            dg_ref[...] = jnp.zeros_like(dg_ref)

        if transposed:
            dh = _dot(dz_ref[...], w_ref[...])
        else:
            kc = w.shape[2]
            dh = _dot_nt(dz_ref[:, 0:kc], w_ref[0])
            for j in range(1, w.shape[0]):
                dh = dh + _dot_nt(dz_ref[:, j * kc:(j + 1) * kc], w_ref[j])
        xv = x_ref[...]
        r = lax.rsqrt(jnp.mean(xv * xv, axis=-1, keepdims=True) + EPS)
        xh = xv * r
        dg_ref[...] += jnp.sum(dh * xh, axis=0, keepdims=True)
        dxh = dh * g_ref[...]
        out = r_ref[...] + r * (dxh - xh * jnp.mean(dxh * xh, axis=-1, keepdims=True))
        o_ref[...] = out
        ob_ref[...] = out.astype(BF16)

    row = lambda i: (i, 0)
    kdim = dz.shape[1]
    return pl.pallas_call(
        body, name=name, grid=(s // ts,),
        in_specs=[pl.BlockSpec((ts, kdim), row), ANY, pl.BlockSpec((ts, D), row),
                  pl.BlockSpec((1, D), lambda i: (0, 0)), pl.BlockSpec((ts, D), row), ANY],
        out_specs=[pl.BlockSpec((ts, D), row), pl.BlockSpec((ts, D), row), pl.BlockSpec((1, D), lambda i: (0, 0))],
        out_shape=[jax.ShapeDtypeStruct((s, D), F32), jax.ShapeDtypeStruct((s, D), BF16),
                   jax.ShapeDtypeStruct((1, D), F32)],
        scratch_shapes=[pltpu.VMEM(w.shape, BF16), pltpu.SemaphoreType.DMA],
        compiler_params=_cp("arbitrary"),
    )(dz, w, x, g, resid, after)


def _matmul_tn(a, b, name, tn, shard_major=False, tm=None):
    s, m = a.shape
    n = b.shape[1]
    tm = m if tm is None else tm
    ni, nj = m // tm, n // tn

    def body(a_ref, b_ref, o_ref):
        o_ref[...] = _dot_tn(a_ref[...], b_ref[...]).astype(BF16)

    if shard_major:
        out_spec = pl.BlockSpec((None, tm, tn), lambda i, j: (j, i, 0))
        out_shape = jax.ShapeDtypeStruct((nj, m, tn), BF16)
    else:
        out_spec = pl.BlockSpec((tm, tn), lambda i, j: (i, j))
        out_shape = jax.ShapeDtypeStruct((m, n), BF16)
    return pl.pallas_call(
        body, name=name, grid=(ni, nj),
        in_specs=[pl.BlockSpec((s, tm), lambda i, j: (0, i)), pl.BlockSpec((s, tn), lambda i, j: (0, j))],
        out_specs=out_spec, out_shape=out_shape,
        compiler_params=_cp("arbitrary", "arbitrary"),
    )(a, b)


def _pool_fwd(zr, wgrp, scale):
    s = zr.shape[0]

    def body(u_ref, w_ref, sc_ref, p_ref, pp_ref):
        row = _rows((s, 128))
        for gi, win in enumerate(POOL_WINDOWS):
            cs = slice(gi * 128, (gi + 1) * 128)
            u = u_ref[:, cs].astype(F32)
            acc, k = u, 1
            while k < win:
                acc = acc + jnp.where(row >= k, pltpu.roll(acc, k, 0), 0.0)
                k *= 2
            cnt = jnp.minimum(row + 1, win).astype(F32)
            p = (acc / cnt - u).astype(BF16)
            p_ref[:, cs] = p
            pp_ref[:, cs] = (_dot(p, w_ref[gi].astype(BF16)) * sc_ref[:, cs]).astype(BF16)

    return pl.pallas_call(
        body, name="pool_fwd", grid=(1,),
        in_specs=[pl.BlockSpec((s, POOL_W), lambda i: (0, OFF_POOL // POOL_W)),
                  pl.BlockSpec((4, 128, 128), lambda i: (0, 0, 0)), pl.BlockSpec((1, POOL_W), lambda i: (0, 0))],
        out_specs=[pl.BlockSpec((s, POOL_W), lambda i: (0, 0))] * 2,
        out_shape=[jax.ShapeDtypeStruct((s, POOL_W), BF16)] * 2,
        compiler_params=_cp("arbitrary"),
    )(zr, wgrp, scale)


def _pool_bwd(p, dpp, wgrp, scale, after):
    s = p.shape[0]

    def body(p_ref, dpp_ref, w_ref, sc_ref, after_ref, dz_ref, dw_ref, dsc_ref):
        row = _rows((s, 128))
        for gi, win in enumerate(POOL_WINDOWS):
            cs = slice(gi * 128, (gi + 1) * 128)
            pv = p_ref[:, cs]
            wb = w_ref[gi].astype(BF16)
            dpp_v = dpp_ref[:, cs].astype(F32)
            dsc_ref[:, cs] = jnp.sum(dpp_v * _dot(pv, wb), axis=0, keepdims=True)
            dpm = (dpp_v * sc_ref[:, cs]).astype(BF16)
            dw_ref[gi] = _dot_tn(pv, dpm)
            dp = _dot_nt(dpm, wb)
            cnt = jnp.minimum(row + 1, win).astype(F32)
            acc, k = dp / cnt, 1
            while k < win:
                acc = acc + jnp.where(row < s - k, pltpu.roll(acc, s - k, 0), 0.0)
                k *= 2
            dz_ref[:, cs] = (acc - dp).astype(BF16)

    full = lambda i: (0, 0)
    return pl.pallas_call(
        body, name="pool_bwd", grid=(1,),
        in_specs=[pl.BlockSpec((s, POOL_W), full), pl.BlockSpec((s, POOL_W), full),
                  pl.BlockSpec((4, 128, 128), lambda i: (0, 0, 0)), pl.BlockSpec((1, POOL_W), full), ANY],
        out_specs=[pl.BlockSpec((s, POOL_W), full), pl.BlockSpec((4, 128, 128), lambda i: (0, 0, 0)),
                   pl.BlockSpec((1, POOL_W), full)],
        out_shape=[jax.ShapeDtypeStruct((s, POOL_W), BF16), jax.ShapeDtypeStruct((4, 128, 128), F32),
                   jax.ShapeDtypeStruct((1, POOL_W), F32)],
        compiler_params=_cp("arbitrary"),
    )(p, dpp, wgrp, scale, after)


def _gla_decay(zgk_ref, wgk_ref, bgk_ref, rb):
    g = _dot(zgk_ref[...], wgk_ref[...].astype(BF16)) + bgk_ref[...]
    la = (jnp.minimum(g, 0.0) - jnp.log(1.0 + jnp.exp(-jnp.abs(g)))) * (1.0 / 16.0)
    rowm = _rows(la.shape) & (CHUNK - 1)
    bc, k = la, 1
    while k < CHUNK:
        bc = bc + jnp.where(rowm >= k, pltpu.roll(bc, k, 0), 0.0)
        k *= 2
    return g, jnp.exp(bc), jnp.exp(-bc)


GLA_HB = 4


def _gla_specs(rb, rmap):
    wk, wv = GLA_HB * HK, GLA_HB * HV
    return [pl.BlockSpec((rb, wk), lambda h, r: (rmap(h, r), OFF_Q // wk + h)),
            pl.BlockSpec((rb, wk), lambda h, r: (rmap(h, r), OFF_K // wk + h)),
            pl.BlockSpec((rb, wv), lambda h, r: (rmap(h, r), OFF_V // wv + h)),
            pl.BlockSpec((rb, 128), lambda h, r: (rmap(h, r), OFF_GK // 128))]


def _gla_fwd(zr, wgk, bgk, ghead, rb):
    s = zr.shape[0]
    nc = rb // CHUNK
    wk, wv = GLA_HB * HK, GLA_HB * HV

    def body(q_ref, k_ref, v_ref, zgk_ref, zog_ref, wgk_ref, bgk_ref, gh_ref, o_ref, og_ref, sp_ref, st_ref):
        @pl.when(pl.program_id(1) == 0)
        def _():
            st_ref[...] = jnp.zeros_like(st_ref)

        _, e_pos, e_neg = _gla_decay(zgk_ref, wgk_ref, bgk_ref, rb)
        lower = _rows((CHUNK, CHUNK)) >= lax.broadcasted_iota(jnp.int32, (CHUNK, CHUNK), 1)
        for c in range(nc):
            sl = slice(c * CHUNK, (c + 1) * CHUNK)
            for hh in range(GLA_HB):
                ck, cv = slice(hh * HK, (hh + 1) * HK), slice(hh * HV, (hh + 1) * HV)
                q = q_ref[sl, ck].astype(F32) * QSCALE
                k = k_ref[sl, ck].astype(F32)
                v = v_ref[sl, cv]
                ec, fc = e_pos[sl, ck], e_neg[sl, ck]
                qfw = (q * ec).astype(BF16)
                kfw_f = k * fc
                s_fw = _dot_nt(qfw, kfw_f.astype(BF16))
                s_bw = _dot_nt((q * fc).astype(BF16), (k * ec).astype(BF16))
                pm = jnp.where(lower, s_fw, s_bw).astype(BF16)
                st = st_ref[hh]
                stb = st.astype(BF16)
                sp_ref[c, hh] = stb
                o = _dot(pm, v) + _dot_nt(qfw, stb)
                e_last = _pick_row(ec, CHUNK - 1)
                kdec = (kfw_f * e_last).astype(BF16)
                st_ref[hh] = st * e_last + _dot_tn(v, kdec)
                r = lax.rsqrt(jnp.mean(o * o, axis=-1, keepdims=True) + EPS)
                zo = zog_ref[sl, cv].astype(F32)
                o_ref[sl, cv] = o.astype(BF16)
                og_ref[sl, cv] = (o * r * gh_ref[...] * zo * _sigmoid(zo)).astype(BF16)

    rmap = lambda h, r: r
    return pl.pallas_call(
        body, name="gla_fwd", grid=(HEADS // GLA_HB, s // rb),
        in_specs=_gla_specs(rb, rmap) + [
            pl.BlockSpec((rb, wv), lambda h, r: (r, OFF_OG // wv + h)),
            pl.BlockSpec((128, wk), lambda h, r: (0, h)), pl.BlockSpec((1, wk), lambda h, r: (0, h)),
            pl.BlockSpec((1, HV), lambda h, r: (0, 0))],
        out_specs=[pl.BlockSpec((rb, wv), lambda h, r: (r, h)), pl.BlockSpec((rb, wv), lambda h, r: (r, h)),
                   pl.BlockSpec((nc, GLA_HB, HV, HK), lambda h, r: (r, h, 0, 0))],
        out_shape=[jax.ShapeDtypeStruct((s, D), BF16), jax.ShapeDtypeStruct((s, D), BF16),
                   jax.ShapeDtypeStruct((s // CHUNK, HEADS, HV, HK), BF16)],
        scratch_shapes=[pltpu.VMEM((GLA_HB, HV, HK), F32)],
        compiler_params=_cp("arbitrary", "arbitrary"),
    )(zr, zr, zr, zr, zr, wgk, bgk, ghead)


def _gla_bwd(zr, do, sp, wgk, bgk, after, rb):
    s = zr.shape[0]
    nc = rb // CHUNK
    nr = s // rb
    wk, wv = GLA_HB * HK, GLA_HB * HV

    def body(q_ref, k_ref, v_ref, zgk_ref, do_ref, sp_ref, wgk_ref, bgk_ref, after_ref, dq_ref, dk_ref, dv_ref, dg_ref,
             gt_ref, dbc_ref):
        @pl.when(pl.program_id(1) == 0)
        def _():
            gt_ref[...] = jnp.zeros_like(gt_ref)

        g, e_pos, e_neg = _gla_decay(zgk_ref, wgk_ref, bgk_ref, rb)
        lower = _rows((CHUNK, CHUNK)) >= lax.broadcasted_iota(jnp.int32, (CHUNK, CHUNK), 1)
        is_last = _rows((CHUNK, HK)) == CHUNK - 1
        for c in reversed(range(nc)):
            sl = slice(c * CHUNK, (c + 1) * CHUNK)
            for hh in range(GLA_HB):
                ck, cv = slice(hh * HK, (hh + 1) * HK), slice(hh * HV, (hh + 1) * HV)
                q = q_ref[sl, ck].astype(F32) * QSCALE
                k = k_ref[sl, ck].astype(F32)
                v = v_ref[sl, cv]
                dov = do_ref[sl, cv]
                ec, fc = e_pos[sl, ck], e_neg[sl, ck]
                qfw_f, kfw_f, qbw_f, kbw_f = q * ec, k * fc, q * fc, k * ec
                qfw, kfw, qbw, kbw = qfw_f.astype(BF16), kfw_f.astype(BF16), qbw_f.astype(BF16), kbw_f.astype(BF16)
                pm = jnp.where(lower, _dot_nt(qfw, kfw), _dot_nt(qbw, kbw)).astype(BF16)
                e_last = _pick_row(ec, CHUNK - 1)
                kdec = (kfw_f * e_last).astype(BF16)
                gt = gt_ref[hh]
                gtb = gt.astype(BF16)
                spv = sp_ref[c, hh]
                dp = _dot_nt(dov, v)
                dv_ref[sl, cv] = (_dot_tn(pm, dov) + _dot_nt(kdec, gtb)).astype(BF16)
                ds_fw = jnp.where(lower, dp, 0.0).astype(BF16)
                ds_bw = jnp.where(lower, 0.0, dp).astype(BF16)
                dqfw = _dot(ds_fw, kfw) + _dot(dov, spv)
                dkfw = _dot_tn(ds_fw, qfw)
                dqbw = _dot(ds_bw, kbw)
                dkbw = _dot_tn(ds_bw, qbw)
                dkdec = _dot(v, gtb)
                de_last = (jnp.sum(gt * spv.astype(F32), axis=0, keepdims=True)
                           + jnp.sum(dkdec * kfw_f, axis=0, keepdims=True))
                dkfw = dkfw + dkdec * e_last
                dq_ref[sl, ck] = ((dqfw * ec + dqbw * fc) * QSCALE).astype(BF16)
                dk_ref[sl, ck] = (dkfw * fc + dkbw * ec).astype(BF16)
                dbc = dqfw * qfw_f - dqbw * qbw_f + dkbw * kbw_f - dkfw * kfw_f
                dbc_ref[sl, ck] = dbc + jnp.where(is_last, de_last * e_last, 0.0)
                gt_ref[hh] = _dot_tn(dov, qfw) + gt * e_last
        rowm = _rows((rb, wk)) & (CHUNK - 1)
        dla, kk = dbc_ref[...], 1
        while kk < CHUNK:
            dla = dla + jnp.where(rowm < CHUNK - kk, pltpu.roll(dla, rb - kk, 0), 0.0)
            kk *= 2
        dg_ref[...] = dla * (1.0 / 16.0) * _sigmoid(-g)

    rmap = lambda h, r: nr - 1 - r
    rev = lambda h, r: (nr - 1 - r, h)
    return pl.pallas_call(
        body, name="gla_bwd", grid=(HEADS // GLA_HB, nr),
        in_specs=_gla_specs(rb, rmap) + [
            pl.BlockSpec((rb, wv), rev),
            pl.BlockSpec((nc, GLA_HB, HV, HK), lambda h, r: (nr - 1 - r, h, 0, 0)),
            pl.BlockSpec((128, wk), lambda h, r: (0, h)), pl.BlockSpec((1, wk), lambda h, r: (0, h)), ANY],
        out_specs=[pl.BlockSpec((rb, wk), rev), pl.BlockSpec((rb, wk), rev), pl.BlockSpec((rb, wv), rev),
                   pl.BlockSpec((rb, wk), rev)],
        out_shape=[jax.ShapeDtypeStruct((s, HEADS * HK), BF16), jax.ShapeDtypeStruct((s, HEADS * HK), BF16),
                   jax.ShapeDtypeStruct((s, D), BF16), jax.ShapeDtypeStruct((s, HEADS * HK), F32)],
        scratch_shapes=[pltpu.VMEM((GLA_HB, HV, HK), F32), pltpu.VMEM((rb, wk), F32)],
        compiler_params=_cp("arbitrary", "arbitrary"),
    )(zr, zr, zr, zr, do, sp, wgk, bgk, after)


def _gk_bwd(dgpre, zr, wgk, after, ts):
    s = zr.shape[0]

    def body(dg_ref, zgk_ref, w_ref, after_ref, dz_ref, dw_ref, db_ref):
        @pl.when(pl.program_id(0) == 0)
        def _():
            dw_ref[...] = jnp.zeros_like(dw_ref)
            db_ref[...] = jnp.zeros_like(db_ref)

        dg = dg_ref[...]
        dgb = dg.astype(BF16)
        dz_ref[...] = _dot_nt(dgb, w_ref[...].astype(BF16)).astype(BF16)
        dw_ref[...] += _dot_tn(zgk_ref[...], dgb)
        db_ref[...] += jnp.sum(dg, axis=0, keepdims=True)

    return pl.pallas_call(
        body, name="gk_bwd", grid=(s // ts,),
        in_specs=[pl.BlockSpec((ts, 512), lambda i: (i, 0)), pl.BlockSpec((ts, 128), lambda i: (i, OFF_GK // 128)),
                  pl.BlockSpec((128, 512), lambda i: (0, 0)), ANY],
        out_specs=[pl.BlockSpec((ts, 128), lambda i: (i, 0)), pl.BlockSpec((128, 512), lambda i: (0, 0)),
                   pl.BlockSpec((1, 512), lambda i: (0, 0))],
        out_shape=[jax.ShapeDtypeStruct((s, 128), BF16), jax.ShapeDtypeStruct((128, 512), F32),
                   jax.ShapeDtypeStruct((1, 512), F32)],
        compiler_params=_cp("arbitrary"),
    )(dgpre, zr, wgk, after)


def _merge_fwd(x, zr, pp, og, bgate, wpp, wgla, wout, gffn, after, ts):
    s = x.shape[0]

    def body(x_ref, z0_ref, z1_ref, pp_ref, og_ref, bg_ref, wpp_ref, wgla_ref, wout_ref, gf_ref, after_ref,
             x1_ref, mix_ref, yp_ref, yg_ref, h2_ref):
        ppv = pp_ref[...]
        yp = jnp.concatenate([_dot(ppv, wpp_ref[j]) for j in range(4)], axis=1)
        yg = _dot(og_ref[...], wgla_ref[...])
        g0 = _sigmoid(z0_ref[...].astype(F32) + bg_ref[:, :D])
        g1 = _sigmoid(z1_ref[...].astype(F32) + bg_ref[:, D:])
        mixed = (g0 * yp + g1 * yg).astype(BF16)
        x1 = x_ref[...] + _dot(mixed, wout_ref[...])
        x1_ref[...] = x1
        mix_ref[...] = mixed
        yp_ref[...] = yp.astype(BF16)
        yg_ref[...] = yg.astype(BF16)
        r = lax.rsqrt(jnp.mean(x1 * x1, axis=-1, keepdims=True) + EPS)
        h2_ref[...] = (x1 * r * gf_ref[...]).astype(BF16)

    row = lambda i: (i, 0)
    const2 = lambda i: (0, 0)
    return pl.pallas_call(
        body, name="merge_fwd", grid=(s // ts,),
        in_specs=[pl.BlockSpec((ts, D), row), pl.BlockSpec((ts, D), lambda i: (i, 0)), pl.BlockSpec((ts, D), lambda i: (i, 1)),
                  pl.BlockSpec((ts, POOL_W), row), pl.BlockSpec((ts, D), row), pl.BlockSpec((1, 2 * D), const2),
                  pl.BlockSpec((4, POOL_W, 256), lambda i: (0, 0, 0)), pl.BlockSpec((D, D), const2),
                  pl.BlockSpec((D, D), const2), pl.BlockSpec((1, D), const2), ANY],
        out_specs=[pl.BlockSpec((ts, D), row)] * 5,
        out_shape=[jax.ShapeDtypeStruct((s, D), F32)] + [jax.ShapeDtypeStruct((s, D), BF16)] * 4,
        compiler_params=_cp("arbitrary"),
    )(x, zr, zr, pp, og, bgate, wpp, wgla, wout, gffn, after)


def _merge_bwd(dx1b, zr, yp, yg, o, bgate, ghead, wpp, wgla, wout, after, ts):
    s = dx1b.shape[0]

    def body(dx_ref, z0_ref, z1_ref, zog_ref, yp_ref, yg_ref, o_ref, bg_ref, gh_ref, wpp_ref, wgla_ref, wout_ref, after_ref,
             dzg_ref, dyp_ref, dyg_ref, dpp_ref, do_ref, dzog_ref, dbg_ref, dgh_ref):
        @pl.when(pl.program_id(0) == 0)
        def _():
            dbg_ref[...] = jnp.zeros_like(dbg_ref)
            dgh_ref[...] = jnp.zeros_like(dgh_ref)

        dmix = _dot_nt(dx_ref[...], wout_ref[...])
        g0 = _sigmoid(z0_ref[...].astype(F32) + bg_ref[:, :D])
        g1 = _sigmoid(z1_ref[...].astype(F32) + bg_ref[:, D:])
        dypb = (dmix * g0).astype(BF16)
        dygb = (dmix * g1).astype(BF16)
        dz0 = dmix * yp_ref[...].astype(F32) * g0 * (1.0 - g0)
        dz1 = dmix * yg_ref[...].astype(F32) * g1 * (1.0 - g1)
        dzg_ref[:, :D] = dz0.astype(BF16)
        dzg_ref[:, D:] = dz1.astype(BF16)
        dbg_ref[:, :D] += jnp.sum(dz0, axis=0, keepdims=True)
        dbg_ref[:, D:] += jnp.sum(dz1, axis=0, keepdims=True)
        dyp_ref[...] = dypb
        dyg_ref[...] = dygb
        dpp = _dot_nt(dypb[:, 0:256], wpp_ref[0])
        for j in range(1, 4):
            dpp = dpp + _dot_nt(dypb[:, j * 256:(j + 1) * 256], wpp_ref[j])
        dpp_ref[...] = dpp.astype(BF16)
        dog = _dot_nt(dygb, wgla_ref[...])
        gh = gh_ref[...]
        dgh = jnp.zeros((1, HV), F32)
        for h in range(HEADS):
            cs = slice(h * HV, (h + 1) * HV)
            ov = o_ref[:, cs].astype(F32)
            r = lax.rsqrt(jnp.mean(ov * ov, axis=-1, keepdims=True) + EPS)
            oh = ov * r
            zo = zog_ref[:, cs].astype(F32)
            sg = _sigmoid(zo)
            dog_h = dog[:, cs]
            don = dog_h * zo * sg
            dzog_ref[:, cs] = (dog_h * oh * gh * sg * (1.0 + zo * (1.0 - sg))).astype(BF16)
            dgh = dgh + jnp.sum(don * oh, axis=0, keepdims=True)
            doh = don * gh
            do_ref[:, cs] = (r * (doh - oh * jnp.mean(doh * oh, axis=-1, keepdims=True))).astype(BF16)
        dgh_ref[...] += dgh

    row = lambda i: (i, 0)
    const2 = lambda i: (0, 0)
    return pl.pallas_call(
        body, name="merge_bwd", grid=(s // ts,),
        in_specs=[pl.BlockSpec((ts, D), row), pl.BlockSpec((ts, D), lambda i: (i, 0)), pl.BlockSpec((ts, D), lambda i: (i, 1)),
                  pl.BlockSpec((ts, D), lambda i: (i, OFF_OG // D)), pl.BlockSpec((ts, D), row), pl.BlockSpec((ts, D), row),
                  pl.BlockSpec((ts, D), row), pl.BlockSpec((1, 2 * D), const2), pl.BlockSpec((1, HV), const2),
                  pl.BlockSpec((4, POOL_W, 256), lambda i: (0, 0, 0)), pl.BlockSpec((D, D), const2),
                  pl.BlockSpec((D, D), const2), ANY],
        out_specs=[pl.BlockSpec((ts, 2 * D), row), pl.BlockSpec((ts, D), row), pl.BlockSpec((ts, D), row),
                   pl.BlockSpec((ts, POOL_W), row), pl.BlockSpec((ts, D), row), pl.BlockSpec((ts, D), row),
                   pl.BlockSpec((1, 2 * D), const2), pl.BlockSpec((1, HV), const2)],
        out_shape=[jax.ShapeDtypeStruct((s, 2 * D), BF16), jax.ShapeDtypeStruct((s, D), BF16),
                   jax.ShapeDtypeStruct((s, D), BF16), jax.ShapeDtypeStruct((s, POOL_W), BF16),
                   jax.ShapeDtypeStruct((s, D), BF16), jax.ShapeDtypeStruct((s, D), BF16),
                   jax.ShapeDtypeStruct((1, 2 * D), F32), jax.ShapeDtypeStruct((1, HV), F32)],
        compiler_params=_cp("arbitrary"),
    )(dx1b, zr, zr, zr, yp, yg, o, bgate, ghead, wpp, wgla, wout, after)


HALO = 16
CCH = 1408


def _conv_taps(u_ref, halo_ref, cs, first, ts):
    u = u_ref[:, cs].astype(F32)
    hal = halo_ref[:, cs].astype(F32)
    h1 = jnp.where(first, 0.0, _pick_row(hal, HALO - 1))
    h2 = jnp.where(first, 0.0, _pick_row(hal, HALO - 2))
    row8 = _rows((8, u.shape[1]))
    r1, r2 = pltpu.roll(u, 1, 0), pltpu.roll(u, 2, 0)
    r1 = jnp.concatenate([jnp.where(row8 == 0, h1, r1[:8]), r1[8:]], axis=0)
    r2 = jnp.concatenate([jnp.where(row8 == 0, h2, jnp.where(row8 == 1, h1, r2[:8])), r2[8:]], axis=0)
    return u, r1, r2


def _ffn_down_loss(u, x1, tgt, wconv, bconv, wdown, gfin, ts):
    s = x1.shape[0]

    def body(u_ref, halo_ref, x1_ref, t_ref, wc_ref, bc_ref, wd_ref, gf_ref, a_ref, c_ref, dx_ref, dxb_ref, ls_ref,
             dgf_ref):
        i = pl.program_id(0)

        @pl.when(i == 0)
        def _():
            ls_ref[...] = jnp.zeros_like(ls_ref)
            dgf_ref[...] = jnp.zeros_like(dgf_ref)

        first = i == 0
        acc = x1_ref[...]
        for hf in range(D_FF // CCH):
            cg = slice(hf * CCH, (hf + 1) * CCH)
            cv = slice(D_FF + hf * CCH, D_FF + (hf + 1) * CCH)
            vals = []
            for cs in (cg, cv):
                u0, u1, u2 = _conv_taps(u_ref, halo_ref, cs, first, ts)
                vals.append(bc_ref[:, cs] + wc_ref[0:1, cs] * u2 + wc_ref[1:2, cs] * u1 + wc_ref[2:3, cs] * u0)
                c_ref[:, cs] = vals[-1].astype(BF16)
            a = (vals[0] * _sigmoid(vals[0]) * vals[1]).astype(BF16)
            a_ref[:, cg] = a
            acc = acc + _dot(a, wd_ref[cg, :])
        r = lax.rsqrt(jnp.mean(acc * acc, axis=-1, keepdims=True) + EPS)
        xh = acc * r
        gf = gf_ref[...]
        err = xh * gf - t_ref[...]
        ls_ref[...] += (0.5 / D) * jnp.sum(jnp.sum(err * err, axis=-1, keepdims=True), axis=0, keepdims=True)
        dy = err * (1.0 / D)
        dgf_ref[...] += jnp.sum(dy * xh, axis=0, keepdims=True)
        dxh = dy * gf
        dx = r * (dxh - xh * jnp.mean(dxh * xh, axis=-1, keepdims=True))
        dx_ref[...] = dx
        dxb_ref[...] = dx.astype(BF16)

    row = lambda i: (i, 0)
    const2 = lambda i: (0, 0)
    return pl.pallas_call(
        body, name="ffn_down_loss", grid=(s // ts,),
        in_specs=[pl.BlockSpec((ts, N_UP), row),
                  pl.BlockSpec((HALO, N_UP), lambda i: (jnp.maximum(i * (ts // HALO) - 1, 0), 0)),
                  pl.BlockSpec((ts, D), row), pl.BlockSpec((ts, D), row), pl.BlockSpec((3, N_UP), const2),
                  pl.BlockSpec((1, N_UP), const2), pl.BlockSpec((D_FF, D), const2), pl.BlockSpec((1, D), const2)],
        out_specs=[pl.BlockSpec((ts, D_FF), row), pl.BlockSpec((ts, N_UP), row), pl.BlockSpec((ts, D), row),
                   pl.BlockSpec((ts, D), row), pl.BlockSpec((1, 128), const2), pl.BlockSpec((1, D), const2)],
        out_shape=[jax.ShapeDtypeStruct((s, D_FF), BF16), jax.ShapeDtypeStruct((s, N_UP), BF16),
                   jax.ShapeDtypeStruct((s, D), F32), jax.ShapeDtypeStruct((s, D), BF16),
                   jax.ShapeDtypeStruct((1, 128), F32), jax.ShapeDtypeStruct((1, D), F32)],
        compiler_params=_cp("arbitrary"),
    )(u, u, x1, tgt, wconv, bconv, wdown, gfin)


def _ffn_bwd(dx2b, u, c, wconv, wdown, ts):
    s = dx2b.shape[0]
    nt = s // ts

    def body(dx_ref, u_ref, c_ref, wc_ref, wd_ref, du_ref, db_ref, dw_ref, nxt_ref):
        @pl.when(pl.program_id(0) == 0)
        def _():
            db_ref[...] = jnp.zeros_like(db_ref)
            dw_ref[...] = jnp.zeros_like(dw_ref)
            nxt_ref[...] = jnp.zeros_like(nxt_ref)

        dxv = dx_ref[...]
        row8 = _rows((8, CCH))
        for hf in range(D_FF // CCH):
            cg = slice(hf * CCH, (hf + 1) * CCH)
            cv = slice(D_FF + hf * CCH, D_FF + (hf + 1) * CCH)
            da = _dot_nt(dxv, wd_ref[cg, :])
            gate = c_ref[:, cg].astype(F32)
            val = c_ref[:, cv].astype(F32)
            sg = _sigmoid(gate)
            dcs = (da * val * sg * (1.0 + gate * (1.0 - sg)), da * gate * sg)
            for cs, dc in zip((cg, cv), dcs):
                n1 = nxt_ref[0:1, cs]
                n2 = nxt_ref[1:2, cs]
                r1, r2 = pltpu.roll(dc, ts - 1, 0), pltpu.roll(dc, ts - 2, 0)
                f1 = jnp.concatenate([r1[:ts - 8], jnp.where(row8 == 7, n1, r1[ts - 8:])], axis=0)
                f2 = jnp.concatenate([r2[:ts - 8], jnp.where(row8 == 7, n2, jnp.where(row8 == 6, n1, r2[ts - 8:]))], axis=0)
                uv = u_ref[:, cs].astype(F32)
                db_ref[:, cs] += jnp.sum(dc, axis=0, keepdims=True)
                dw_ref[0:1, cs] += jnp.sum(f2 * uv, axis=0, keepdims=True)
                dw_ref[1:2, cs] += jnp.sum(f1 * uv, axis=0, keepdims=True)
                dw_ref[2:3, cs] += jnp.sum(dc * uv, axis=0, keepdims=True)
                du_ref[:, cs] = (wc_ref[2:3, cs] * dc + wc_ref[1:2, cs] * f1 + wc_ref[0:1, cs] * f2).astype(BF16)
                nxt_ref[:, cs] = dc[0:8, :]

    rev = lambda i: (nt - 1 - i, 0)
    const2 = lambda i: (0, 0)
    return pl.pallas_call(
        body, name="ffn_bwd", grid=(nt,),
        in_specs=[pl.BlockSpec((ts, D), rev), pl.BlockSpec((ts, N_UP), rev), pl.BlockSpec((ts, N_UP), rev),
                  pl.BlockSpec((3, N_UP), const2), pl.BlockSpec((D_FF, D), const2)],
        out_specs=[pl.BlockSpec((ts, N_UP), rev), pl.BlockSpec((1, N_UP), const2), pl.BlockSpec((3, N_UP), const2)],
        out_shape=[jax.ShapeDtypeStruct((s, N_UP), BF16), jax.ShapeDtypeStruct((1, N_UP), F32),
                   jax.ShapeDtypeStruct((3, N_UP), F32)],
        scratch_shapes=[pltpu.VMEM((8, N_UP), F32)],
        compiler_params=_cp("arbitrary"),
    )(dx2b, u, c, wconv, wdown)


ANY = pl.BlockSpec(memory_space=pl.ANY)


def _place():
    x, y, c = lax.axis_index("x"), lax.axis_index("y"), lax.axis_index("c")
    chips = [(1 - x, y), (x, 1 - y), (1 - x, 1 - y)]
    return x, y, c, chips


def _half(shape, c, axis):
    size = shape[axis] // 2
    cut = pl.ds(pl.multiple_of(c * size, 8 if axis == 0 else 128), size)
    return (cut, slice(None)) if axis == 0 else (slice(None), cut)


def _half_shape(shape, axis):
    return (shape[0] // 2, shape[1]) if axis == 0 else (shape[0], shape[1] // 2)


def _remote(src, dst, send_sems, recv_sems, k, to):
    return pltpu.make_async_remote_copy(src_ref=src, dst_ref=dst, send_sem=send_sems.at[k], recv_sem=recv_sems.at[k],
                                        device_id=to, device_id_type=MESH)


def _sibling_exchange(grads, axes, smalls, name):
    nb = len(grads)
    n = nb + len(smalls)

    def body(*refs):
        ins, outs = refs[:n], refs[n:2 * n]
        send_sems, recv_sems = refs[2 * n:]
        x, y, c, _ = _place()
        sib = (x, y, 1 - c)
        cps = []
        for a in range(nb):
            theirs = _half(grads[a].shape[1:], 1 - c, axes[a])
            cps.append(_remote(ins[a].at[(slice(None),) + theirs], outs[a], send_sems, recv_sems, a, sib))
        for a in range(nb, n):
            cps.append(_remote(ins[a], outs[a], send_sems, recv_sems, a, sib))
        for cp in cps:
            cp.start()
        for cp in cps:
            cp.wait()

    out_shape = [jax.ShapeDtypeStruct((4,) + _half_shape(g.shape[1:], ax), g.dtype) for g, ax in zip(grads, axes)]
    out_shape += [jax.ShapeDtypeStruct(a.shape, F32) for a in smalls]
    return pl.pallas_call(
        body, name=name, in_specs=[ANY] * n, out_specs=[ANY] * n, out_shape=out_shape,
        scratch_shapes=[pltpu.SemaphoreType.DMA((n,)), pltpu.SemaphoreType.DMA((n,))],
        compiler_params=pltpu.CompilerParams(has_side_effects=True),
    )(*grads, *smalls)


def _gather_share(lands, axes, name):
    n = len(lands)

    def body(*refs):
        outs = refs[n:2 * n]
        send_sems, recv_sems = refs[2 * n:]
        x, y, c, chips = _place()
        sib = (x, y, 1 - c)
        cps = []
        for a in range(n):
            mine = _half(lands[a].shape[1:], c, axes[a])
            for k, ch in enumerate(chips):
                landed = outs[a].at[(2 * ch[0] + ch[1],) + mine]
                cps.append(_remote(landed, landed, send_sems, recv_sems, 3 * a + k, sib))
        for cp in cps:
            cp.start()
        for a in range(n):
            other = _half(lands[a].shape[1:], 1 - c, axes[a])
            for k, ch in enumerate(chips):
                landed = outs[a].at[(2 * ch[0] + ch[1],) + other]
                _remote(landed, landed, send_sems, recv_sems, 3 * a + k, sib).wait_recv()
        for cp in cps:
            cp.wait_send()

    return pl.pallas_call(
        body, name=name, in_specs=[ANY] * n, out_specs=[ANY] * n,
        out_shape=[jax.ShapeDtypeStruct(a.shape, a.dtype) for a in lands],
        input_output_aliases={a: a for a in range(n)},
        scratch_shapes=[pltpu.SemaphoreType.DMA((3 * n,)), pltpu.SemaphoreType.DMA((3 * n,))],
        compiler_params=pltpu.CompilerParams(has_side_effects=True),
    )(*lands)


def _sibling_share(halves, name):
    n = len(halves)

    def body(*refs):
        ins, outs = refs[:n], refs[n:2 * n]
        send_sems, recv_sems = refs[2 * n:]
        x, y, c, _ = _place()
        cps = [_remote(ins[a], outs[a], send_sems, recv_sems, a, (x, y, 1 - c)) for a in range(n)]
        for cp in cps:
            cp.start()
        for cp in cps:
            cp.wait()

    return pl.pallas_call(
        body, name=name, in_specs=[ANY] * n, out_specs=[ANY] * n,
        out_shape=[jax.ShapeDtypeStruct(h.shape, F32) for h in halves],
        scratch_shapes=[pltpu.SemaphoreType.DMA((n,)), pltpu.SemaphoreType.DMA((n,))],
        compiler_params=pltpu.CompilerParams(has_side_effects=True),
    )(*halves)


HBM = pl.BlockSpec(memory_space=pltpu.HBM)
SEM = pl.BlockSpec(memory_space=pltpu.SEMAPHORE)
DATAFLOW = pltpu.SideEffectType.DATAFLOW_SIDE_EFFECTING


def _split_start(name, srcs, land_shapes, plan, n_copies, after):
    lands = [lax.empty(*ls) if isinstance(ls, tuple) else ls for ls in land_shapes]
    bufs = list(srcs) + lands
    nb, ns = len(bufs), len(srcs)

    def body(*refs):
        send_sems, recv_sems, token = refs[nb + 1], refs[nb + 2], refs[-1]
        for k, (src, dst, to) in enumerate(plan(refs[:ns], refs[ns:nb])):
            _remote(src, dst, send_sems, recv_sems, k, to).start()
        token[...] = jnp.zeros_like(token)

    res = pl.pallas_call(
        body, name=name,
        out_shape=(pltpu.SemaphoreType.DMA((n_copies,)), pltpu.SemaphoreType.DMA((n_copies,)),
                   *[pltpu.HBM(b.shape, b.dtype) for b in bufs], jax.ShapeDtypeStruct((8, 128), F32)),
        in_specs=[HBM] * nb + [ANY],
        out_specs=(SEM, SEM, *[HBM] * nb, pl.BlockSpec(memory_space=pltpu.VMEM)),
        input_output_aliases={i: 2 + i for i in range(nb)},
        compiler_params=pltpu.CompilerParams(has_side_effects=DATAFLOW),
    )(*[pltpu.with_memory_space_constraint(b, pltpu.HBM) for b in bufs], after)
    return (res[0], res[1], list(res[2:2 + nb])), res[-1]


def _split_wait(name, handle, n_srcs, plan, after):
    send_sems, recv_sems, bufs = handle
    nb = len(bufs)

    def body(*refs):
        sends, recvs = refs[nb], refs[nb + 1]
        for k, (src, dst, to) in enumerate(plan(refs[:n_srcs], refs[n_srcs:nb])):
            cp = _remote(src, dst, sends, recvs, k, to)
            cp.wait_send()
            cp.wait_recv()

    res = pl.pallas_call(
        body, name=name, out_shape=[pltpu.HBM(b.shape, b.dtype) for b in bufs],
        in_specs=[HBM] * nb + [SEM, SEM, ANY], out_specs=[HBM] * nb,
        input_output_aliases={i: i for i in range(nb)},
        compiler_params=pltpu.CompilerParams(has_side_effects=DATAFLOW),
    )(*bufs, send_sems, recv_sems, after)
    return list(res[:n_srcs]), list(res[n_srcs:])


def _gather_plan(shapes, axes, n_whole=0):
    def plan(srcs, lands):
        x, y, c, chips = _place()
        out = []
        for a, (shape, axis) in enumerate(zip(shapes, axes)):
            mine = _half(shape, c, axis)
            for ch in chips:
                out.append((srcs[a].at[mine], lands[a].at[(2 * x + y,) + mine], (ch[0], ch[1], c)))
        for a in range(len(shapes), len(shapes) + n_whole):
            for ch in chips:
                out.append((srcs[a], lands[a].at[2 * x + y], (ch[0], ch[1], c)))
        return out
    return plan


def _share_plan(shapes, axes):
    def plan(srcs, lands):
        x, y, c, chips = _place()
        out = []
        for a, (shape, axis) in enumerate(zip(shapes, axes)):
            mine = _half(shape, c, axis)
            for ch in chips:
                landed = lands[a].at[(2 * ch[0] + ch[1],) + mine]
                out.append((landed, landed, (x, y, 1 - c)))
        return out
    return plan


def _sibling_plan(shapes, axes):
    def plan(srcs, lands):
        x, y, c, _ = _place()
        return [(srcs[a].at[(slice(None),) + _half(shape, 1 - c, axis)], lands[a], (x, y, 1 - c))
                for a, (shape, axis) in enumerate(zip(shapes, axes))]
    return plan


def _reduce_plan(n_big, n_small):
    def plan(srcs, lands):
        x, y, c, chips = _place()
        out = []
        for a in range(n_big):
            for k, ch in enumerate(chips):
                out.append((srcs[a].at[2 * ch[0] + ch[1]], lands[a].at[k], (ch[0], ch[1], c)))
        for a in range(n_big, n_big + n_small):
            for ch in chips:
                out.append((srcs[a], lands[a].at[2 * x + y], (ch[0], ch[1], c)))
        return out
    return plan


def _row_tile(rows, cols, mult):
    best = mult
    for t in range(mult, rows + 1, mult):
        if rows % t == 0 and t * cols * 4 <= (2 << 20):
            best = t
    return best if rows % best == 0 else rows


COL_TILE = 256


def _half_tiling(hshape, axis, mult):
    hr, hc = hshape
    if axis == 0:
        tr = _row_tile(hr, hc, mult)
        return tr, hc, hr // tr
    return hr, COL_TILE, hc // COL_TILE


def _tile_idx(axis, t):
    return (t, 0) if axis == 0 else (0, t)


def _chip_partial(place, g, t, axis, name):
    hshape = t.shape[1:]
    br, bc, nt = _half_tiling(hshape, axis, 16)

    def body(pl_ref, g_ref, t_ref, pf_ref, pb_ref):
        v = g_ref[...].astype(F32) + t_ref[...].astype(F32)
        pb_ref[...] = v.astype(BF16)

        @pl.when(pl.program_id(1) == pl_ref[0])
        def _():
            pf_ref[...] = v

    blk = (None, br, bc)
    return pl.pallas_call(
        body, name=name,
        grid_spec=pltpu.PrefetchScalarGridSpec(
            num_scalar_prefetch=1, grid=(nt, 4),
            in_specs=[pl.BlockSpec(blk, lambda i, j, p: (j,) + _tile_idx(axis, p[1] * nt + i)),
                      pl.BlockSpec(blk, lambda i, j, p: (j,) + _tile_idx(axis, i))],
            out_specs=[pl.BlockSpec((br, bc), lambda i, j, p: _tile_idx(axis, i)),
                       pl.BlockSpec(blk, lambda i, j, p: (j,) + _tile_idx(axis, i))]),
        out_shape=[jax.ShapeDtypeStruct(hshape, F32), jax.ShapeDtypeStruct((4,) + hshape, BF16)],
        compiler_params=_cp("arbitrary", "arbitrary"),
    )(place, g, t)


def _finish_half(pf, rb, axis, name):
    hshape = pf.shape
    br, bc, nt = _half_tiling(hshape, axis, 16)

    def body(pf_ref, rb_ref, o_ref):
        o_ref[...] = ((pf_ref[...] + rb_ref[0].astype(F32)) + rb_ref[1].astype(F32)) + rb_ref[2].astype(F32)

    return pl.pallas_call(
        body, name=name, grid=(nt,),
        in_specs=[pl.BlockSpec((br, bc), lambda i: _tile_idx(axis, i)),
                  pl.BlockSpec((3, br, bc), lambda i: (0,) + _tile_idx(axis, i))],
        out_specs=pl.BlockSpec((br, bc), lambda i: _tile_idx(axis, i)),
        out_shape=jax.ShapeDtypeStruct(hshape, F32),
        compiler_params=_cp("arbitrary"),
    )(pf, rb)


def _adam_math(w, g, m, v):
    m = ADAM_B1 * m + (1.0 - ADAM_B1) * g
    v = ADAM_B2 * v + (1.0 - ADAM_B2) * (g * g)
    m_hat = m / (1.0 - ADAM_B1 ** ADAM_STEP)
    v_hat = v / (1.0 - ADAM_B2 ** ADAM_STEP)
    return -ADAM_LR * (m_hat / (jnp.sqrt(v_hat) + ADAM_EPS) + ADAM_WD * w), m, v


def _adam_halves(place, w, mine, theirs, m, v, axis, name):
    br, bc, nt = _half_tiling(mine.shape, axis, 8)

    def body(pl_ref, w_ref, a_ref, b_ref, m_ref, v_ref, g_ref, d_ref, mo_ref, vo_ref):
        is_mine = pl.program_id(0) // nt == pl_ref[1]
        g = jnp.where(is_mine, a_ref[...], b_ref[...])
        d, mn, vn = _adam_math(w_ref[...], g, m_ref[...], v_ref[...])
        g_ref[...] = g
        d_ref[...] = d
        mo_ref[...] = mn
        vo_ref[...] = vn

    full = pl.BlockSpec((br, bc), lambda i, p: _tile_idx(axis, i))
    mine_spec = pl.BlockSpec((br, bc), lambda i, p: _tile_idx(axis, jnp.where(i // nt == p[1], i % nt, nt - 1)))
    theirs_spec = pl.BlockSpec((br, bc), lambda i, p: _tile_idx(axis, jnp.where(i // nt == p[1], 0, i % nt)))
    return pl.pallas_call(
        body, name=name,
        grid_spec=pltpu.PrefetchScalarGridSpec(
            num_scalar_prefetch=1, grid=(2 * nt,), in_specs=[full, mine_spec, theirs_spec, full, full],
            out_specs=[full] * 4),
        out_shape=[jax.ShapeDtypeStruct(w.shape, F32)] * 4, compiler_params=_cp("arbitrary"),
    )(place, w, mine, theirs, m, v)


def _add_many(xs, ys, name):
    n = len(xs)

    def body(*refs):
        for i in range(n):
            refs[2 * n + i][...] = refs[i][...] + refs[n + i][...]

    return pl.pallas_call(body, name=name, out_shape=[jax.ShapeDtypeStruct(a.shape, F32) for a in xs])(*xs, *ys)


def _adam_small(place, owns, landed, ws, ms, vs, widths):
    n, nw = len(owns), len(ws)

    def body(pl_ref, *refs):
        own_r, land_r = refs[:n], refs[n:2 * n]
        w_r, m_r, v_r = (refs[2 * n + k * nw:2 * n + (k + 1) * nw] for k in range(3))
        outs = refs[2 * n + 3 * nw:]
        g_o, d_o, m_o, v_o = outs[:n], outs[n:n + nw], outs[n + nw:n + 2 * nw], outs[n + 2 * nw:]
        for me in range(4):
            @pl.when(pl_ref[0] == me)
            def _(me=me):
                for i in range(n):
                    p = [own_r[i][...] if k == me else land_r[i][k] for k in range(4)]
                    g = ((p[0] + p[1]) + p[2]) + p[3]
                    if i < nw and widths[i]:
                        g = g[:, me * widths[i]:(me + 1) * widths[i]]
                    g_o[i][...] = g
                    if i < nw:
                        d, mn, vn = _adam_math(w_r[i][...], g, m_r[i][...], v_r[i][...])
                        d_o[i][...] = d
                        m_o[i][...] = mn
                        v_o[i][...] = vn

    g_shapes = [jax.ShapeDtypeStruct(ws[i].shape if i < nw else owns[i].shape, F32) for i in range(n)]
    w_shapes = [jax.ShapeDtypeStruct(w.shape, F32) for w in ws]
    whole = lambda a: pl.BlockSpec(a.shape, lambda i, p, nd=len(a.shape): (0,) * nd)
    ins = list(owns) + list(landed) + list(ws) + list(ms) + list(vs)
    out_shape = g_shapes + w_shapes * 3
    out = pl.pallas_call(
        body, name="adam_small",
        grid_spec=pltpu.PrefetchScalarGridSpec(num_scalar_prefetch=1, grid=(1,), in_specs=[whole(a) for a in ins],
                                               out_specs=[whole(a) for a in out_shape]),
        out_shape=out_shape, compiler_params=_cp("arbitrary"),
    )(place, *ins)
    return out[:n], out[n:n + nw], out[n + nw:n + 2 * nw], out[n + 2 * nw:]


def kernel(x, g_mix, w_in, b_gate, w_gk_up, b_gk, w_pool_grp, pool_scale, g_gla_head, w_pool_proj, w_gla_proj, w_out, g_ffn, w_up, w_conv, b_conv, w_down, g_final, loss_target, m_g_mix, m_w_in, m_b_gate, m_w_gk_up, m_b_gk, m_w_pool_grp, m_pool_scale, m_g_gla_head, m_w_pool_proj, m_w_gla_proj, m_w_out, m_g_ffn, m_w_up, m_w_conv, m_b_conv, m_w_down, m_g_final, v_g_mix, v_w_in, v_b_gate, v_w_gk_up, v_b_gk, v_w_pool_grp, v_pool_scale, v_g_gla_head, v_w_pool_proj, v_w_gla_proj, v_w_out, v_g_ffn, v_w_up, v_w_conv, v_b_conv, v_w_down, v_g_final):
    s = x.shape[1]
    ts = min(s, 512)
    tm = min(s, 256)
    cx, cy, cc = lax.axis_index("x"), lax.axis_index("y"), lax.axis_index("c")
    chip = 2 * cx + cy
    place = jnp.stack([chip, cc]).astype(jnp.int32)

    big_names = ("w_in", "w_pool_proj", "w_gla_proj", "w_out", "w_up", "w_down")
    axes = (1, 0, 0, 0, 0, 0)
    shards = dict(w_in=jnp.transpose(w_in[0]), w_pool_proj=w_pool_proj[0], w_gla_proj=w_gla_proj[0], w_out=w_out[0],
                  w_up=w_up[0], w_down=w_down[0])
    def fill_own(lands, mine):
        return [lax.dynamic_update_slice(g, o_[None], (chip, 0, 0)) for g, o_ in zip(lands, mine)]

    def gather_start(tag, halves, group_axes, whole, after):
        plan = _gather_plan([o_.shape for o_ in halves], group_axes, len(whole))
        srcs = list(halves) + list(whole)
        handle, token = _split_start("gather_" + tag + "_start", srcs, [((4,) + o_.shape, o_.dtype) for o_ in srcs], plan,
                                     3 * len(srcs), after)
        return (handle, plan, len(halves), len(srcs), group_axes), token

    def gather_finish(tag, started, after):
        handle, plan, n_halves, n, group_axes = started
        mine, lands = _split_wait("gather_" + tag + "_wait", handle, n, plan, after)
        lands[:n_halves] = _gather_share(lands[:n_halves], group_axes, "gather_" + tag + "_share")
        return fill_own(lands, mine)

    in_w, tok = gather_start("in", [shards["w_in"].astype(BF16)], axes[:1], [], g_mix)
    zero = tok[0, 0]
    own = [(shards[n] + zero).astype(BF16) for n in big_names[1:]]
    mix_w, tok = gather_start("mix", own[0:3], axes[1:4], [w_gk_up[0] + zero, w_conv[0] + zero], tok)
    ffn_w, tok = gather_start("ffn", own[3:5], axes[4:6], [], tok)
    xs, tgt = x[0], loss_target[0]
    wgrp = w_pool_grp[0]
    h = _rmsnorm(xs, g_mix, tok, "norm_mix", ts)
    m_in_t, v_in_t = jnp.transpose(m_w_in[0]), jnp.transpose(v_w_in[0])
    h, m_in_t, v_in_t = lax.optimization_barrier((h, m_in_t, v_in_t))
    w_in_t = gather_finish("in", in_w, h)[0].reshape(N_IN, D)
    w_rt = jnp.concatenate([w_in_t[3600:], w_in_t[1536:3584], w_in_t[0:1536], w_in_t[3584:3600],
                            jnp.zeros((128 - GATE_RANK, D), BF16)], axis=0)
    nsh = N_IN // 4

    zr = _matmul_resident(h, w_rt, "in_proj", 1152, transposed=True)
    p, pp = _pool_fwd(zr, wgrp, pool_scale)
    wpp, wgla, wout, wgk4, wconv4 = gather_finish("mix", mix_w, pp)
    wgla, wout = wgla.reshape(D, D), wout.reshape(D, D)
    wgk_full = jnp.transpose(wgk4, (1, 0, 2)).reshape(GATE_RANK, 512)
    wconv_full = jnp.transpose(wconv4, (1, 0, 2)).reshape(3, N_UP)
    wgk_pad = jnp.concatenate([wgk_full, jnp.zeros((128 - GATE_RANK, 512), F32)], axis=0)
    o, og, sp = _gla_fwd(zr, wgk_pad, b_gk, g_gla_head, ts)
    ffn_handle, ffn_plan_w, _, ffn_n, _ = ffn_w
    ffn_mine, ffn_lands = _split_wait("gather_ffn_wait", ffn_handle, ffn_n, ffn_plan_w, og)
    share_plan = _share_plan([o_.shape for o_ in ffn_mine], axes[4:6])
    share, tok = _split_start("gather_ffn_share_start", [], ffn_lands, share_plan, 3 * ffn_n, og)
    x1, mixed, yp, yg, h2 = _merge_fwd(xs, zr, pp, og, b_gate, wpp, wgla, wout, g_ffn, tok, ts)
    wup, wdown = fill_own(_split_wait("gather_ffn_share_wait", share, 0, share_plan, x1)[1], ffn_mine)
    wdown = wdown.reshape(D_FF, D)
    u = _matmul_resident(h2, wup, "ffn_up", None)
    a, conv_out, dx2, dx2b, loss_part, dgfin = _ffn_down_loss(u, x1, tgt, wconv_full, b_conv, wdown,
                                                              g_final.reshape(1, D), tm)

    du, dbconv, dwconv = _ffn_bwd(dx2b, u, conv_out, wconv_full, wdown, tm)
    dw_down = _matmul_tn(a, dx2b, "dw_down", D, tm=1408)
    dw_up = _matmul_tn(h2, du, "dw_up", 1408, shard_major=True)

    def exchange_start(tag, grads, group_axes, after):
        plan = _sibling_plan([g.shape[1:] for g in grads], group_axes)
        lands = [((4,) + _half_shape(g.shape[1:], ax), g.dtype) for g, ax in zip(grads, group_axes)]
        handle, token = _split_start("sibling_" + tag + "_start", grads, lands, plan, len(grads), after)
        return (handle, plan, len(grads)), token

    def partials(tag, names, group_axes, exchange, after):
        handle, plan, n = exchange
        mine, theirs = _split_wait("sibling_" + tag + "_wait", handle, n, plan, after)
        return zip(*[_chip_partial(place, g, t, ax, "chip_partial_" + nm)
                     for nm, ax, g, t in zip(names, group_axes, mine, theirs)])

    ffn_names, ffn_axes = ("w_up", "w_down"), (0, 0)
    ffn_x, token = exchange_start("ffn", [dw_up, dw_down.reshape(4, 704, D)], ffn_axes, du)
    dx1, dx1b, dgffn = _matmul_nt_normbwd(du, wup, x1, g_ffn, dx2, token, "ffn_up_bwd", ts)
    ffn_pf, ffn_pb = partials("ffn", ffn_names, ffn_axes, ffn_x, dx1b)
    ffn_plan = _reduce_plan(2, 0)
    ffn_handle, token = _split_start("reduce_ffn_start", ffn_pb, [((3,) + p.shape[1:], BF16) for p in ffn_pb],
                                     ffn_plan, 6, ffn_pf[0])

    dzg, dyp, dyg, dpp, do, dzog, dbgate, dghead = _merge_bwd(dx1b, zr, yp, yg, o, b_gate, g_gla_head, wpp, wgla, wout,
                                                             token, ts)
    dw_out = _matmul_tn(mixed, dx1b, "dw_out", D)
    dw_gla = _matmul_tn(og, dyg, "dw_gla", D)
    dw_pp = _matmul_tn(pp, dyp, "dw_pp", 256, shard_major=True)

    out_names, out_axes = ("w_pool_proj", "w_gla_proj", "w_out"), (0, 0, 0)
    out_x, token = exchange_start("out", [dw_pp, dw_gla.reshape(4, 256, D), dw_out.reshape(4, 256, D)], out_axes, dpp)
    dzp, dwgrp, dscale = _pool_bwd(p, dpp, wgrp, pool_scale, token)
    out_pf, out_pb = partials("out", out_names, out_axes, out_x, dzp)
    out_plan = _reduce_plan(3, 0)
    out_handle, token = _split_start("reduce_out_start", out_pb, [((3,) + p_.shape[1:], BF16) for p_ in out_pb],
                                     out_plan, 9, out_pf[0])
    dq, dk, dv, dgpre = _gla_bwd(zr, do, sp, wgk_pad, b_gk, token, ts)
    dzgk, dwgk, dbgk = _gk_bwd(dgpre, zr, wgk_pad, dgpre, ts)
    dzr = jnp.concatenate([dzg, dv, dzog, dzp, dq, dk, dzgk], axis=1)
    dw_rt = _matmul_tn(dzr, h, "dw_in", D, tm=1152)

    def grad_rows(lo, hi):
        out = []
        for seg_lo, seg_hi, at in ((0, 1536, OFF_POOL), (1536, 3584, OFF_V), (3584, 3600, OFF_GK), (3600, N_IN, OFF_GATE)):
            a_, b_ = max(lo, seg_lo), min(hi, seg_hi)
            if a_ < b_:
                out.append(dw_rt[at + a_ - seg_lo:at + b_ - seg_lo])
        return jnp.concatenate(out, axis=0)

    dw_in_t = jnp.stack([grad_rows(j * nsh, (j + 1) * nsh) for j in range(4)])

    in_sib = _sibling_exchange([dw_in_t], (1,), [], "sibling_exchange_in")
    in_pf, in_pb = _chip_partial(place, dw_in_t, in_sib[0], 1, "chip_partial_w_in")
    in_plan = _reduce_plan(1, 0)
    in_handle, token = _split_start("reduce_in_start", [in_pb], [((3,) + in_pb.shape[1:], BF16)], in_plan, 3, in_pf)
    grad_x, _, dgmix = _matmul_nt_normbwd(dzr, w_rt, xs, g_mix, dx1, token, "in_proj_bwd", ts, transposed=True)
    small_names = ("g_mix", "b_gate", "w_gk_up", "b_gk", "w_pool_grp", "pool_scale", "g_gla_head", "g_ffn", "w_conv",
                   "b_conv", "g_final")
    small_mine = [dgmix, dbgate, dwgk[:GATE_RANK], dbgk, dwgrp.reshape(4 * 128, 128), dscale, dghead, dgffn, dwconv, dbconv,
                  dgfin, loss_part]
    small_sib = _sibling_exchange([], (), small_mine, "sibling_exchange_small")
    small_chip = _add_many(small_mine, small_sib, "chip_partial_small")
    small_plan = _reduce_plan(0, len(small_chip))
    small_handle, token = _split_start("reduce_small_start", small_chip, [((4,) + a_.shape, F32) for a_ in small_chip],
                                       small_plan, 3 * len(small_chip), small_mine[0])

    ms = dict(w_in=m_in_t, w_pool_proj=m_w_pool_proj[0], w_gla_proj=m_w_gla_proj[0], w_out=m_w_out[0],
              w_up=m_w_up[0], w_down=m_w_down[0])
    vs = dict(w_in=v_in_t, w_pool_proj=v_w_pool_proj[0], w_gla_proj=v_w_gla_proj[0], w_out=v_w_out[0],
              w_up=v_w_up[0], w_down=v_w_down[0])
    grad, delta, new_m, new_v = {}, {}, {}, {}

    def finish_and_update(names, group_axes, part_f, landed, tag):
        halves = [_finish_half(pf, rb, ax, "finish_" + n) for n, ax, pf, rb in zip(names, group_axes, part_f, landed)]
        sib_halves = _sibling_share(halves, "sibling_share_" + tag)
        for n, ax, mine, theirs in zip(names, group_axes, halves, sib_halves):
            res = _adam_halves(place, shards[n], mine, theirs, ms[n], vs[n], ax, "adam_" + n)
            if n == "w_in":
                res = [jnp.transpose(r_) for r_ in res]
            grad[n], delta[n], new_m[n], new_v[n] = [r_[None] for r_ in res]

    _, ffn_landed = _split_wait("reduce_ffn_wait", ffn_handle, 2, ffn_plan, token)
    _, out_landed = _split_wait("reduce_out_wait", out_handle, 3, out_plan, ffn_landed[0])
    finish_and_update(ffn_names + out_names, ffn_axes + out_axes, ffn_pf + out_pf, ffn_landed + out_landed, "rest")
    _, in_landed = _split_wait("reduce_in_wait", in_handle, 1, in_plan, delta["w_out"])
    finish_and_update(("w_in",), (1,), (in_pf,), in_landed, "in")
    small_sent, small_landed = _split_wait("reduce_small_wait", small_handle, len(small_chip), small_plan, delta["w_in"])
    given = dict(g_mix=(g_mix, m_g_mix, v_g_mix), b_gate=(b_gate, m_b_gate, v_b_gate), w_gk_up=(w_gk_up, m_w_gk_up, v_w_gk_up),
                 b_gk=(b_gk, m_b_gk, v_b_gk), w_pool_grp=(w_pool_grp, m_w_pool_grp, v_w_pool_grp),
                 pool_scale=(pool_scale, m_pool_scale, v_pool_scale), g_gla_head=(g_gla_head, m_g_gla_head, v_g_gla_head),
                 g_ffn=(g_ffn, m_g_ffn, v_g_ffn), w_conv=(w_conv, m_w_conv, v_w_conv), b_conv=(b_conv, m_b_conv, v_b_conv),
                 g_final=(g_final, m_g_final, v_g_final))
    flat2 = lambda a: a.reshape(-1, a.shape[-1])
    widths = [dict(w_gk_up=128, w_conv=1408).get(n) for n in small_names]
    totals, ds, mo, vo = _adam_small(place, small_sent, small_landed, *[[flat2(given[n][k]) for n in small_names] for k in range(3)],
                                     widths)
    loss = totals[-1][0, 0]
    for i, n in enumerate(small_names):
        shp = given[n][0].shape
        grad[n], delta[n], new_m[n], new_v[n] = [r_.reshape(shp) for r_ in (totals[i], ds[i], mo[i], vo[i])]

    order = ("g_mix", "w_in", "b_gate", "w_gk_up", "b_gk", "w_pool_grp", "pool_scale", "g_gla_head", "w_pool_proj",
             "w_gla_proj", "w_out", "g_ffn", "w_up", "w_conv", "b_conv", "w_down", "g_final")
    return (loss, grad_x[None], *[grad[n] for n in order], *[delta[n] for n in order], *[new_m[n] for n in order],
            *[new_v[n] for n in order])
```

```python
import functools

import jax
import jax.numpy as jnp
from jax import lax
from jax.experimental import pallas as pl
from jax.experimental.pallas import tpu as pltpu

F32 = jnp.float32
BF16 = jnp.bfloat16
MESH = pl.DeviceIdType.MESH

D = 1024
EPS = 1e-6
CHUNK = 64
POOL_W = 512
POOL_WINDOWS = (2, 4, 8, 16)
HEADS = 4
HK = 128
HV = 256
GATE_RANK = 16
D_FF = 2816
N_UP = 2 * D_FF
N_IN = 5648
QSCALE = HK ** -0.5
N_INR = 5760
OFF_GATE, OFF_V, OFF_OG, OFF_POOL, OFF_Q, OFF_K, OFF_GK = 0, 2048, 3072, 4096, 4608, 5120, 5632

ADAM_LR, ADAM_B1, ADAM_B2, ADAM_EPS, ADAM_WD, ADAM_STEP = 0.001, 0.9, 0.999, 1e-08, 0.01, 10

VMEM_LIMIT = 56 * 1024 * 1024


def _cp(*sem):
    return pltpu.CompilerParams(dimension_semantics=sem if sem else None, vmem_limit_bytes=VMEM_LIMIT)


def _dot(a, b):
    return jnp.dot(a, b, preferred_element_type=F32)


def _dot_nt(a, b):
    return lax.dot_general(a, b, (((1,), (1,)), ((), ())), preferred_element_type=F32)


def _dot_tn(a, b):
    return lax.dot_general(a, b, (((0,), (0,)), ((), ())), preferred_element_type=F32)


def _sigmoid(v):
    return 1.0 / (1.0 + jnp.exp(-v))


def _rows(shape):
    return lax.broadcasted_iota(jnp.int32, shape, 0)


def _pick_row(v, r):
    return jnp.sum(jnp.where(_rows(v.shape) == r, v, 0.0), axis=0, keepdims=True)


def _rmsnorm(x, g, after, name, ts):
    s = x.shape[0]

    def body(x_ref, g_ref, after_ref, h_ref):
        xv = x_ref[...]
        r = lax.rsqrt(jnp.mean(xv * xv, axis=-1, keepdims=True) + EPS)
        h_ref[...] = (xv * r * g_ref[...]).astype(BF16)

    return pl.pallas_call(
        body, name=name, grid=(s // ts,),
        in_specs=[pl.BlockSpec((ts, D), lambda i: (i, 0)), pl.BlockSpec((1, D), lambda i: (0, 0)), ANY],
        out_specs=pl.BlockSpec((ts, D), lambda i: (i, 0)), out_shape=jax.ShapeDtypeStruct((s, D), BF16),
        compiler_params=_cp("arbitrary"),
    )(x, g, after)


MM_ROWS = 512


def _matmul_resident(h, w, name, tn, transposed=False):
    s = h.shape[0]
    if transposed:
        nj = w.shape[0] // tn
        w_spec = pl.BlockSpec((tn, D), lambda j: (j, 0))
    elif w.ndim == 3:
        nj, tn = w.shape[0], w.shape[2]
        w_spec = pl.BlockSpec((None, D, tn), lambda j: (j, 0, 0))
    else:
        nj = w.shape[1] // tn
        w_spec = pl.BlockSpec((D, tn), lambda j: (0, j))
    mm = _dot_nt if transposed else _dot
    rc = min(s, MM_ROWS)

    def body(h_ref, w_ref, z_ref):
        for r0 in range(0, s, rc):
            z_ref[r0:r0 + rc, :] = mm(h_ref[r0:r0 + rc, :], w_ref[...]).astype(BF16)

    return pl.pallas_call(
        body, name=name, grid=(nj,),
        in_specs=[pl.BlockSpec((s, D), lambda j: (0, 0)), w_spec],
        out_specs=pl.BlockSpec((s, tn), lambda j: (0, j)), out_shape=jax.ShapeDtypeStruct((s, nj * tn), BF16),
        compiler_params=_cp("arbitrary"),
    )(h, w)


PROJ_PIECES = ((3600, 2048, OFF_GATE), (1536, 2048, OFF_V), (0, 1536, OFF_POOL), (3584, GATE_RANK, OFF_GK))


def _load_projection(w_hbm, w_ref, sems):
    cps = [pltpu.make_async_copy(w_hbm.at[pl.ds(src, n)], w_ref.at[pl.ds(dst, n)], sems.at[i])
           for i, (src, n, dst) in enumerate(PROJ_PIECES)]
    for cp in cps:
        cp.start()
    w_ref[OFF_GK + GATE_RANK:, :] = jnp.zeros((N_INR - OFF_GK - GATE_RANK, D), BF16)
    for cp in cps:
        cp.wait()


def _in_proj(h, w_nat, tn):
    s = h.shape[0]
    rc = min(s, MM_ROWS)

    def body(h_ref, w_hbm, z_ref, w_ref, sems):
        j = pl.program_id(0)

        @pl.when(j == 0)
        def _():
            _load_projection(w_hbm, w_ref, sems)

        wt = w_ref[pl.ds(pl.multiple_of(j * tn, 128), tn), :]
        for r0 in range(0, s, rc):
            z_ref[r0:r0 + rc, :] = _dot_nt(h_ref[r0:r0 + rc, :], wt).astype(BF16)

    return pl.pallas_call(
        body, name="in_proj", grid=(N_INR // tn,),
        in_specs=[pl.BlockSpec((s, D), lambda j: (0, 0)), ANY],
        out_specs=pl.BlockSpec((s, tn), lambda j: (0, j)), out_shape=jax.ShapeDtypeStruct((s, N_INR), BF16),
        scratch_shapes=[pltpu.VMEM((N_INR, D), BF16), pltpu.SemaphoreType.DMA((len(PROJ_PIECES),))],
        compiler_params=_cp("arbitrary"),
    )(h, w_nat)


def _matmul_nt_normbwd(dz, w, x, g, resid, after, name, ts, transposed=False):
    s = x.shape[0]
    w_vmem = (N_INR, D) if transposed else w.shape

    def body(dz_ref, w_hbm, x_ref, g_ref, r_ref, after_ref, o_ref, ob_ref, dg_ref, w_ref, sems):
        @pl.when(pl.program_id(0) == 0)
        def _():
            if transposed:
                _load_projection(w_hbm, w_ref, sems)
            else:
                cp = pltpu.make_async_copy(w_hbm, w_ref, sems.at[0])
                cp.start()
                cp.wait()
            dg_ref[...] = jnp.zeros_like(dg_ref)

        if transposed:
            dh = _dot(dz_ref[...], w_ref[...])
        else:
            kc = w.shape[2]
            dh = _dot_nt(dz_ref[:, 0:kc], w_ref[0])
            for j in range(1, w.shape[0]):
                dh = dh + _dot_nt(dz_ref[:, j * kc:(j + 1) * kc], w_ref[j])
        xv = x_ref[...]
        r = lax.rsqrt(jnp.mean(xv * xv, axis=-1, keepdims=True) + EPS)
        xh = xv * r
        dg_ref[...] += jnp.sum(dh * xh, axis=0, keepdims=True)
        dxh = dh * g_ref[...]
        out = r_ref[...] + r * (dxh - xh * jnp.mean(dxh * xh, axis=-1, keepdims=True))
        o_ref[...] = out
        ob_ref[...] = out.astype(BF16)

    row = lambda i: (i, 0)
    kdim = dz.shape[1]
    return pl.pallas_call(
        body, name=name, grid=(s // ts,),
        in_specs=[pl.BlockSpec((ts, kdim), row), ANY, pl.BlockSpec((ts, D), row),
                  pl.BlockSpec((1, D), lambda i: (0, 0)), pl.BlockSpec((ts, D), row), ANY],
        out_specs=[pl.BlockSpec((ts, D), row), pl.BlockSpec((ts, D), row), pl.BlockSpec((1, D), lambda i: (0, 0))],
        out_shape=[jax.ShapeDtypeStruct((s, D), F32), jax.ShapeDtypeStruct((s, D), BF16),
                   jax.ShapeDtypeStruct((1, D), F32)],
        scratch_shapes=[pltpu.VMEM(w_vmem, BF16), pltpu.SemaphoreType.DMA((len(PROJ_PIECES),))],
        compiler_params=_cp("arbitrary"),
    )(dz, w, x, g, resid, after)


def _matmul_tn(a, b, name, tn, shard_major=False, tm=None):
    s, m = a.shape
    n = b.shape[1]
    tm = m if tm is None else tm
    ni, nj = m // tm, n // tn

    def body(a_ref, b_ref, o_ref):
        o_ref[...] = _dot_tn(a_ref[...], b_ref[...]).astype(BF16)

    if shard_major:
        out_spec = pl.BlockSpec((None, tm, tn), lambda i, j: (j, i, 0))
        out_shape = jax.ShapeDtypeStruct((nj, m, tn), BF16)
    else:
        out_spec = pl.BlockSpec((tm, tn), lambda i, j: (i, j))
        out_shape = jax.ShapeDtypeStruct((m, n), BF16)
    return pl.pallas_call(
        body, name=name, grid=(ni, nj),
        in_specs=[pl.BlockSpec((s, tm), lambda i, j: (0, i)), pl.BlockSpec((s, tn), lambda i, j: (0, j))],
        out_specs=out_spec, out_shape=out_shape,
        compiler_params=_cp("arbitrary", "arbitrary"),
    )(a, b)


def _pool_fwd(zr, wgrp, scale):
    s = zr.shape[0]

    def body(u_ref, w_ref, sc_ref, p_ref, pp_ref):
        row = _rows((s, 128))
        for gi, win in enumerate(POOL_WINDOWS):
            cs = slice(gi * 128, (gi + 1) * 128)
            u = u_ref[:, cs].astype(F32)
            acc, k = u, 1
            while k < win:
                acc = acc + jnp.where(row >= k, pltpu.roll(acc, k, 0), 0.0)
                k *= 2
            cnt = jnp.minimum(row + 1, win).astype(F32)
            p = (acc / cnt - u).astype(BF16)
            p_ref[:, cs] = p
            pp_ref[:, cs] = (_dot(p, w_ref[gi].astype(BF16)) * sc_ref[:, cs]).astype(BF16)

    return pl.pallas_call(
        body, name="pool_fwd", grid=(1,),
        in_specs=[pl.BlockSpec((s, POOL_W), lambda i: (0, OFF_POOL // POOL_W)),
                  pl.BlockSpec((4, 128, 128), lambda i: (0, 0, 0)), pl.BlockSpec((1, POOL_W), lambda i: (0, 0))],
        out_specs=[pl.BlockSpec((s, POOL_W), lambda i: (0, 0))] * 2,
        out_shape=[jax.ShapeDtypeStruct((s, POOL_W), BF16)] * 2,
        compiler_params=_cp("arbitrary"),
    )(zr, wgrp, scale)


def _pool_bwd(p, dpp, wgrp, scale, after):
    s = p.shape[0]

    def body(p_ref, dpp_ref, w_ref, sc_ref, after_ref, dz_ref, dw_ref, dsc_ref):
        row = _rows((s, 128))
        for gi, win in enumerate(POOL_WINDOWS):
            cs = slice(gi * 128, (gi + 1) * 128)
            pv = p_ref[:, cs]
            wb = w_ref[gi].astype(BF16)
            dpp_v = dpp_ref[:, cs].astype(F32)
            dsc_ref[:, cs] = jnp.sum(dpp_v * _dot(pv, wb), axis=0, keepdims=True)
            dpm = (dpp_v * sc_ref[:, cs]).astype(BF16)
            dw_ref[gi] = _dot_tn(pv, dpm)
            dp = _dot_nt(dpm, wb)
            cnt = jnp.minimum(row + 1, win).astype(F32)
            acc, k = dp / cnt, 1
            while k < win:
                acc = acc + jnp.where(row < s - k, pltpu.roll(acc, s - k, 0), 0.0)
                k *= 2
            dz_ref[:, cs] = (acc - dp).astype(BF16)

    full = lambda i: (0, 0)
    return pl.pallas_call(
        body, name="pool_bwd", grid=(1,),
        in_specs=[pl.BlockSpec((s, POOL_W), full), pl.BlockSpec((s, POOL_W), full),
                  pl.BlockSpec((4, 128, 128), lambda i: (0, 0, 0)), pl.BlockSpec((1, POOL_W), full), ANY],
        out_specs=[pl.BlockSpec((s, POOL_W), full), pl.BlockSpec((4, 128, 128), lambda i: (0, 0, 0)),
                   pl.BlockSpec((1, POOL_W), full)],
        out_shape=[jax.ShapeDtypeStruct((s, POOL_W), BF16), jax.ShapeDtypeStruct((4, 128, 128), F32),
                   jax.ShapeDtypeStruct((1, POOL_W), F32)],
        compiler_params=_cp("arbitrary"),
    )(p, dpp, wgrp, scale, after)


def _gla_decay(zgk_ref, wgk_ref, bgk_ref, rb):
    g = _dot(zgk_ref[...], wgk_ref[...].astype(BF16)) + bgk_ref[...]
    la = (jnp.minimum(g, 0.0) - jnp.log(1.0 + jnp.exp(-jnp.abs(g)))) * (1.0 / 16.0)
    rowm = _rows(la.shape) & (CHUNK - 1)
    bc, k = la, 1
    while k < CHUNK:
        bc = bc + jnp.where(rowm >= k, pltpu.roll(bc, k, 0), 0.0)
        k *= 2
    return g, jnp.exp(bc), jnp.exp(-bc)


GLA_HB = 4


def _gla_specs(rb, rmap):
    wk, wv = GLA_HB * HK, GLA_HB * HV
    return [pl.BlockSpec((rb, wk), lambda h, r: (rmap(h, r), OFF_Q // wk + h)),
            pl.BlockSpec((rb, wk), lambda h, r: (rmap(h, r), OFF_K // wk + h)),
            pl.BlockSpec((rb, wv), lambda h, r: (rmap(h, r), OFF_V // wv + h)),
            pl.BlockSpec((rb, 128), lambda h, r: (rmap(h, r), OFF_GK // 128))]


def _gla_fwd(zr, wgk, bgk, ghead, rb):
    s = zr.shape[0]
    nc = rb // CHUNK
    wk, wv = GLA_HB * HK, GLA_HB * HV

    def body(q_ref, k_ref, v_ref, zgk_ref, zog_ref, wgk_ref, bgk_ref, gh_ref, o_ref, og_ref, sp_ref, st_ref):
        @pl.when(pl.program_id(1) == 0)
        def _():
            st_ref[...] = jnp.zeros_like(st_ref)

        _, e_pos, e_neg = _gla_decay(zgk_ref, wgk_ref, bgk_ref, rb)
        lower = _rows((CHUNK, CHUNK)) >= lax.broadcasted_iota(jnp.int32, (CHUNK, CHUNK), 1)
        for c in range(nc):
            sl = slice(c * CHUNK, (c + 1) * CHUNK)
            for hh in range(GLA_HB):
                ck, cv = slice(hh * HK, (hh + 1) * HK), slice(hh * HV, (hh + 1) * HV)
                q = q_ref[sl, ck].astype(F32) * QSCALE
                k = k_ref[sl, ck].astype(F32)
                v = v_ref[sl, cv]
                ec, fc = e_pos[sl, ck], e_neg[sl, ck]
                qfw = (q * ec).astype(BF16)
                kfw_f = k * fc
                s_fw = _dot_nt(qfw, kfw_f.astype(BF16))
                s_bw = _dot_nt((q * fc).astype(BF16), (k * ec).astype(BF16))
                pm = jnp.where(lower, s_fw, s_bw).astype(BF16)
                st = st_ref[hh]
                stb = st.astype(BF16)
                sp_ref[c, hh] = stb
                o = _dot(pm, v) + _dot_nt(qfw, stb)
                e_last = _pick_row(ec, CHUNK - 1)
                kdec = (kfw_f * e_last).astype(BF16)
                st_ref[hh] = st * e_last + _dot_tn(v, kdec)
                r = lax.rsqrt(jnp.mean(o * o, axis=-1, keepdims=True) + EPS)
                zo = zog_ref[sl, cv].astype(F32)
                o_ref[sl, cv] = o.astype(BF16)
                og_ref[sl, cv] = (o * r * gh_ref[...] * zo * _sigmoid(zo)).astype(BF16)

    rmap = lambda h, r: r
    return pl.pallas_call(
        body, name="gla_fwd", grid=(HEADS // GLA_HB, s // rb),
        in_specs=_gla_specs(rb, rmap) + [
            pl.BlockSpec((rb, wv), lambda h, r: (r, OFF_OG // wv + h)),
            pl.BlockSpec((128, wk), lambda h, r: (0, h)), pl.BlockSpec((1, wk), lambda h, r: (0, h)),
            pl.BlockSpec((1, HV), lambda h, r: (0, 0))],
        out_specs=[pl.BlockSpec((rb, wv), lambda h, r: (r, h)), pl.BlockSpec((rb, wv), lambda h, r: (r, h)),
                   pl.BlockSpec((nc, GLA_HB, HV, HK), lambda h, r: (r, h, 0, 0))],
        out_shape=[jax.ShapeDtypeStruct((s, D), BF16), jax.ShapeDtypeStruct((s, D), BF16),
                   jax.ShapeDtypeStruct((s // CHUNK, HEADS, HV, HK), BF16)],
        scratch_shapes=[pltpu.VMEM((GLA_HB, HV, HK), F32)],
        compiler_params=_cp("arbitrary", "arbitrary"),
    )(zr, zr, zr, zr, zr, wgk, bgk, ghead)


def _gla_bwd(zr, do, sp, wgk, bgk, after, rb):
    s = zr.shape[0]
    nc = rb // CHUNK
    nr = s // rb
    wk, wv = GLA_HB * HK, GLA_HB * HV

    def body(q_ref, k_ref, v_ref, zgk_ref, do_ref, sp_ref, wgk_ref, bgk_ref, after_ref, dq_ref, dk_ref, dv_ref, dg_ref,
             gt_ref, dbc_ref):
        @pl.when(pl.program_id(1) == 0)
        def _():
            gt_ref[...] = jnp.zeros_like(gt_ref)

        g, e_pos, e_neg = _gla_decay(zgk_ref, wgk_ref, bgk_ref, rb)
        lower = _rows((CHUNK, CHUNK)) >= lax.broadcasted_iota(jnp.int32, (CHUNK, CHUNK), 1)
        is_last = _rows((CHUNK, HK)) == CHUNK - 1
        for c in reversed(range(nc)):
            sl = slice(c * CHUNK, (c + 1) * CHUNK)
            for hh in range(GLA_HB):
                ck, cv = slice(hh * HK, (hh + 1) * HK), slice(hh * HV, (hh + 1) * HV)
                q = q_ref[sl, ck].astype(F32) * QSCALE
                k = k_ref[sl, ck].astype(F32)
                v = v_ref[sl, cv]
                dov = do_ref[sl, cv]
                ec, fc = e_pos[sl, ck], e_neg[sl, ck]
                qfw_f, kfw_f, qbw_f, kbw_f = q * ec, k * fc, q * fc, k * ec
                qfw, kfw, qbw, kbw = qfw_f.astype(BF16), kfw_f.astype(BF16), qbw_f.astype(BF16), kbw_f.astype(BF16)
                pm = jnp.where(lower, _dot_nt(qfw, kfw), _dot_nt(qbw, kbw)).astype(BF16)
                e_last = _pick_row(ec, CHUNK - 1)
                kdec = (kfw_f * e_last).astype(BF16)
                gt = gt_ref[hh]
                gtb = gt.astype(BF16)
                spv = sp_ref[c, hh]
                dp = _dot_nt(dov, v)
                dv_ref[sl, cv] = (_dot_tn(pm, dov) + _dot_nt(kdec, gtb)).astype(BF16)
                ds_fw = jnp.where(lower, dp, 0.0).astype(BF16)
                ds_bw = jnp.where(lower, 0.0, dp).astype(BF16)
                dqfw = _dot(ds_fw, kfw) + _dot(dov, spv)
                dkfw = _dot_tn(ds_fw, qfw)
                dqbw = _dot(ds_bw, kbw)
                dkbw = _dot_tn(ds_bw, qbw)
                dkdec = _dot(v, gtb)
                de_last = (jnp.sum(gt * spv.astype(F32), axis=0, keepdims=True)
                           + jnp.sum(dkdec * kfw_f, axis=0, keepdims=True))
                dkfw = dkfw + dkdec * e_last
                dq_ref[sl, ck] = ((dqfw * ec + dqbw * fc) * QSCALE).astype(BF16)
                dk_ref[sl, ck] = (dkfw * fc + dkbw * ec).astype(BF16)
                dbc = dqfw * qfw_f - dqbw * qbw_f + dkbw * kbw_f - dkfw * kfw_f
                dbc_ref[sl, ck] = dbc + jnp.where(is_last, de_last * e_last, 0.0)
                gt_ref[hh] = _dot_tn(dov, qfw) + gt * e_last
        rowm = _rows((rb, wk)) & (CHUNK - 1)
        dla, kk = dbc_ref[...], 1
        while kk < CHUNK:
            dla = dla + jnp.where(rowm < CHUNK - kk, pltpu.roll(dla, rb - kk, 0), 0.0)
            kk *= 2
        dg_ref[...] = dla * (1.0 / 16.0) * _sigmoid(-g)

    rmap = lambda h, r: nr - 1 - r
    rev = lambda h, r: (nr - 1 - r, h)
    return pl.pallas_call(
        body, name="gla_bwd", grid=(HEADS // GLA_HB, nr),
        in_specs=_gla_specs(rb, rmap) + [
            pl.BlockSpec((rb, wv), rev),
            pl.BlockSpec((nc, GLA_HB, HV, HK), lambda h, r: (nr - 1 - r, h, 0, 0)),
            pl.BlockSpec((128, wk), lambda h, r: (0, h)), pl.BlockSpec((1, wk), lambda h, r: (0, h)), ANY],
        out_specs=[pl.BlockSpec((rb, wk), rev), pl.BlockSpec((rb, wk), rev), pl.BlockSpec((rb, wv), rev),
                   pl.BlockSpec((rb, wk), rev)],
        out_shape=[jax.ShapeDtypeStruct((s, HEADS * HK), BF16), jax.ShapeDtypeStruct((s, HEADS * HK), BF16),
                   jax.ShapeDtypeStruct((s, D), BF16), jax.ShapeDtypeStruct((s, HEADS * HK), F32)],
        scratch_shapes=[pltpu.VMEM((GLA_HB, HV, HK), F32), pltpu.VMEM((rb, wk), F32)],
        compiler_params=_cp("arbitrary", "arbitrary"),
    )(zr, zr, zr, zr, do, sp, wgk, bgk, after)


def _gk_bwd(dgpre, zr, wgk, after, ts):
    s = zr.shape[0]

    def body(dg_ref, zgk_ref, w_ref, after_ref, dz_ref, dw_ref, db_ref):
        @pl.when(pl.program_id(0) == 0)
        def _():
            dw_ref[...] = jnp.zeros_like(dw_ref)
            db_ref[...] = jnp.zeros_like(db_ref)

        dg = dg_ref[...]
        dgb = dg.astype(BF16)
        dz_ref[...] = _dot_nt(dgb, w_ref[...].astype(BF16)).astype(BF16)
        dw_ref[...] += _dot_tn(zgk_ref[...], dgb)
        db_ref[...] += jnp.sum(dg, axis=0, keepdims=True)

    return pl.pallas_call(
        body, name="gk_bwd", grid=(s // ts,),
        in_specs=[pl.BlockSpec((ts, 512), lambda i: (i, 0)), pl.BlockSpec((ts, 128), lambda i: (i, OFF_GK // 128)),
                  pl.BlockSpec((128, 512), lambda i: (0, 0)), ANY],
        out_specs=[pl.BlockSpec((ts, 128), lambda i: (i, 0)), pl.BlockSpec((128, 512), lambda i: (0, 0)),
                   pl.BlockSpec((1, 512), lambda i: (0, 0))],
        out_shape=[jax.ShapeDtypeStruct((s, 128), BF16), jax.ShapeDtypeStruct((128, 512), F32),
                   jax.ShapeDtypeStruct((1, 512), F32)],
        compiler_params=_cp("arbitrary"),
    )(dgpre, zr, wgk, after)


def _merge_fwd(x, zr, pp, og, bgate, wpp, wgla, wout, gffn, after, ts):
    s = x.shape[0]

    def body(x_ref, z0_ref, z1_ref, pp_ref, og_ref, bg_ref, wpp_ref, wgla_ref, wout_ref, gf_ref, after_ref,
             x1_ref, mix_ref, yp_ref, yg_ref, h2_ref):
        ppv = pp_ref[...]
        yp = jnp.concatenate([_dot(ppv, wpp_ref[j]) for j in range(4)], axis=1)
        yg = _dot(og_ref[...], wgla_ref[...])
        g0 = _sigmoid(z0_ref[...].astype(F32) + bg_ref[:, :D])
        g1 = _sigmoid(z1_ref[...].astype(F32) + bg_ref[:, D:])
        mixed = (g0 * yp + g1 * yg).astype(BF16)
        x1 = x_ref[...] + _dot(mixed, wout_ref[...])
        x1_ref[...] = x1
        mix_ref[...] = mixed
        yp_ref[...] = yp.astype(BF16)
        yg_ref[...] = yg.astype(BF16)
        r = lax.rsqrt(jnp.mean(x1 * x1, axis=-1, keepdims=True) + EPS)
        h2_ref[...] = (x1 * r * gf_ref[...]).astype(BF16)

    row = lambda i: (i, 0)
    const2 = lambda i: (0, 0)
    return pl.pallas_call(
        body, name="merge_fwd", grid=(s // ts,),
        in_specs=[pl.BlockSpec((ts, D), row), pl.BlockSpec((ts, D), lambda i: (i, 0)), pl.BlockSpec((ts, D), lambda i: (i, 1)),
                  pl.BlockSpec((ts, POOL_W), row), pl.BlockSpec((ts, D), row), pl.BlockSpec((1, 2 * D), const2),
                  pl.BlockSpec((4, POOL_W, 256), lambda i: (0, 0, 0)), pl.BlockSpec((D, D), const2),
                  pl.BlockSpec((D, D), const2), pl.BlockSpec((1, D), const2), ANY],
        out_specs=[pl.BlockSpec((ts, D), row)] * 5,
        out_shape=[jax.ShapeDtypeStruct((s, D), F32)] + [jax.ShapeDtypeStruct((s, D), BF16)] * 4,
        compiler_params=_cp("arbitrary"),
    )(x, zr, zr, pp, og, bgate, wpp, wgla, wout, gffn, after)


def _merge_bwd(dx1b, zr, yp, yg, o, bgate, ghead, wpp, wgla, wout, after, ts):
    s = dx1b.shape[0]

    def body(dx_ref, z0_ref, z1_ref, zog_ref, yp_ref, yg_ref, o_ref, bg_ref, gh_ref, wpp_ref, wgla_ref, wout_ref, after_ref,
             dzg_ref, dyp_ref, dyg_ref, dpp_ref, do_ref, dzog_ref, dbg_ref, dgh_ref):
        @pl.when(pl.program_id(0) == 0)
        def _():
            dbg_ref[...] = jnp.zeros_like(dbg_ref)
            dgh_ref[...] = jnp.zeros_like(dgh_ref)

        dmix = _dot_nt(dx_ref[...], wout_ref[...])
        g0 = _sigmoid(z0_ref[...].astype(F32) + bg_ref[:, :D])
        g1 = _sigmoid(z1_ref[...].astype(F32) + bg_ref[:, D:])
        dypb = (dmix * g0).astype(BF16)
        dygb = (dmix * g1).astype(BF16)
        dz0 = dmix * yp_ref[...].astype(F32) * g0 * (1.0 - g0)
        dz1 = dmix * yg_ref[...].astype(F32) * g1 * (1.0 - g1)
        dzg_ref[:, :D] = dz0.astype(BF16)
        dzg_ref[:, D:] = dz1.astype(BF16)
        dbg_ref[:, :D] += jnp.sum(dz0, axis=0, keepdims=True)
        dbg_ref[:, D:] += jnp.sum(dz1, axis=0, keepdims=True)
        dyp_ref[...] = dypb
        dyg_ref[...] = dygb
        dpp = _dot_nt(dypb[:, 0:256], wpp_ref[0])
        for j in range(1, 4):
            dpp = dpp + _dot_nt(dypb[:, j * 256:(j + 1) * 256], wpp_ref[j])
        dpp_ref[...] = dpp.astype(BF16)
        dog = _dot_nt(dygb, wgla_ref[...])
        gh = gh_ref[...]
        dgh = jnp.zeros((1, HV), F32)
        for h in range(HEADS):
            cs = slice(h * HV, (h + 1) * HV)
            ov = o_ref[:, cs].astype(F32)
            r = lax.rsqrt(jnp.mean(ov * ov, axis=-1, keepdims=True) + EPS)
            oh = ov * r
            zo = zog_ref[:, cs].astype(F32)
            sg = _sigmoid(zo)
            dog_h = dog[:, cs]
            don = dog_h * zo * sg
            dzog_ref[:, cs] = (dog_h * oh * gh * sg * (1.0 + zo * (1.0 - sg))).astype(BF16)
            dgh = dgh + jnp.sum(don * oh, axis=0, keepdims=True)
            doh = don * gh
            do_ref[:, cs] = (r * (doh - oh * jnp.mean(doh * oh, axis=-1, keepdims=True))).astype(BF16)
        dgh_ref[...] += dgh

    row = lambda i: (i, 0)
    const2 = lambda i: (0, 0)
    return pl.pallas_call(
        body, name="merge_bwd", grid=(s // ts,),
        in_specs=[pl.BlockSpec((ts, D), row), pl.BlockSpec((ts, D), lambda i: (i, 0)), pl.BlockSpec((ts, D), lambda i: (i, 1)),
                  pl.BlockSpec((ts, D), lambda i: (i, OFF_OG // D)), pl.BlockSpec((ts, D), row), pl.BlockSpec((ts, D), row),
                  pl.BlockSpec((ts, D), row), pl.BlockSpec((1, 2 * D), const2), pl.BlockSpec((1, HV), const2),
                  pl.BlockSpec((4, POOL_W, 256), lambda i: (0, 0, 0)), pl.BlockSpec((D, D), const2),
                  pl.BlockSpec((D, D), const2), ANY],
        out_specs=[pl.BlockSpec((ts, 2 * D), row), pl.BlockSpec((ts, D), row), pl.BlockSpec((ts, D), row),
                   pl.BlockSpec((ts, POOL_W), row), pl.BlockSpec((ts, D), row), pl.BlockSpec((ts, D), row),
                   pl.BlockSpec((1, 2 * D), const2), pl.BlockSpec((1, HV), const2)],
        out_shape=[jax.ShapeDtypeStruct((s, 2 * D), BF16), jax.ShapeDtypeStruct((s, D), BF16),
                   jax.ShapeDtypeStruct((s, D), BF16), jax.ShapeDtypeStruct((s, POOL_W), BF16),
                   jax.ShapeDtypeStruct((s, D), BF16), jax.ShapeDtypeStruct((s, D), BF16),
                   jax.ShapeDtypeStruct((1, 2 * D), F32), jax.ShapeDtypeStruct((1, HV), F32)],
        compiler_params=_cp("arbitrary"),
    )(dx1b, zr, zr, zr, yp, yg, o, bgate, ghead, wpp, wgla, wout, after)


HALO = 16
CCH = 1408


def _conv_taps(u_ref, halo_ref, cs, first, ts):
    u = u_ref[:, cs].astype(F32)
    hal = halo_ref[:, cs].astype(F32)
    h1 = jnp.where(first, 0.0, _pick_row(hal, HALO - 1))
    h2 = jnp.where(first, 0.0, _pick_row(hal, HALO - 2))
    row8 = _rows((8, u.shape[1]))
    r1, r2 = pltpu.roll(u, 1, 0), pltpu.roll(u, 2, 0)
    r1 = jnp.concatenate([jnp.where(row8 == 0, h1, r1[:8]), r1[8:]], axis=0)
    r2 = jnp.concatenate([jnp.where(row8 == 0, h2, jnp.where(row8 == 1, h1, r2[:8])), r2[8:]], axis=0)
    return u, r1, r2


def _ffn_down_loss(u, x1, tgt, wconv, bconv, wdown, gfin, ts):
    s = x1.shape[0]

    def body(u_ref, halo_ref, x1_ref, t_ref, wc_ref, bc_ref, wd_ref, gf_ref, a_ref, c_ref, dx_ref, dxb_ref, ls_ref,
             dgf_ref):
        i = pl.program_id(0)

        @pl.when(i == 0)
        def _():
            ls_ref[...] = jnp.zeros_like(ls_ref)
            dgf_ref[...] = jnp.zeros_like(dgf_ref)

        first = i == 0
        acc = x1_ref[...]
        for hf in range(D_FF // CCH):
            cg = slice(hf * CCH, (hf + 1) * CCH)
            cv = slice(D_FF + hf * CCH, D_FF + (hf + 1) * CCH)
            vals = []
            for cs in (cg, cv):
                u0, u1, u2 = _conv_taps(u_ref, halo_ref, cs, first, ts)
                vals.append(bc_ref[:, cs] + wc_ref[0:1, cs] * u2 + wc_ref[1:2, cs] * u1 + wc_ref[2:3, cs] * u0)
                c_ref[:, cs] = vals[-1].astype(BF16)
            a = (vals[0] * _sigmoid(vals[0]) * vals[1]).astype(BF16)
            a_ref[:, cg] = a
            acc = acc + _dot(a, wd_ref[cg, :])
        r = lax.rsqrt(jnp.mean(acc * acc, axis=-1, keepdims=True) + EPS)
        xh = acc * r
        gf = gf_ref[...]
        err = xh * gf - t_ref[...]
        ls_ref[...] += (0.5 / D) * jnp.sum(jnp.sum(err * err, axis=-1, keepdims=True), axis=0, keepdims=True)
        dy = err * (1.0 / D)
        dgf_ref[...] += jnp.sum(dy * xh, axis=0, keepdims=True)
        dxh = dy * gf
        dx = r * (dxh - xh * jnp.mean(dxh * xh, axis=-1, keepdims=True))
        dx_ref[...] = dx
        dxb_ref[...] = dx.astype(BF16)

    row = lambda i: (i, 0)
    const2 = lambda i: (0, 0)
    return pl.pallas_call(
        body, name="ffn_down_loss", grid=(s // ts,),
        in_specs=[pl.BlockSpec((ts, N_UP), row),
                  pl.BlockSpec((HALO, N_UP), lambda i: (jnp.maximum(i * (ts // HALO) - 1, 0), 0)),
                  pl.BlockSpec((ts, D), row), pl.BlockSpec((ts, D), row), pl.BlockSpec((3, N_UP), const2),
                  pl.BlockSpec((1, N_UP), const2), pl.BlockSpec((D_FF, D), const2), pl.BlockSpec((1, D), const2)],
        out_specs=[pl.BlockSpec((ts, D_FF), row), pl.BlockSpec((ts, N_UP), row), pl.BlockSpec((ts, D), row),
                   pl.BlockSpec((ts, D), row), pl.BlockSpec((1, 128), const2), pl.BlockSpec((1, D), const2)],
        out_shape=[jax.ShapeDtypeStruct((s, D_FF), BF16), jax.ShapeDtypeStruct((s, N_UP), BF16),
                   jax.ShapeDtypeStruct((s, D), F32), jax.ShapeDtypeStruct((s, D), BF16),
                   jax.ShapeDtypeStruct((1, 128), F32), jax.ShapeDtypeStruct((1, D), F32)],
        compiler_params=_cp("arbitrary"),
    )(u, u, x1, tgt, wconv, bconv, wdown, gfin)


def _ffn_bwd(dx2b, u, c, wconv, wdown, ts):
    s = dx2b.shape[0]
    nt = s // ts

    def body(dx_ref, u_ref, c_ref, wc_ref, wd_ref, du_ref, db_ref, dw_ref, nxt_ref):
        @pl.when(pl.program_id(0) == 0)
        def _():
            db_ref[...] = jnp.zeros_like(db_ref)
            dw_ref[...] = jnp.zeros_like(dw_ref)
            nxt_ref[...] = jnp.zeros_like(nxt_ref)

        dxv = dx_ref[...]
        row8 = _rows((8, CCH))
        for hf in range(D_FF // CCH):
            cg = slice(hf * CCH, (hf + 1) * CCH)
            cv = slice(D_FF + hf * CCH, D_FF + (hf + 1) * CCH)
            da = _dot_nt(dxv, wd_ref[cg, :])
            gate = c_ref[:, cg].astype(F32)
            val = c_ref[:, cv].astype(F32)
            sg = _sigmoid(gate)
            dcs = (da * val * sg * (1.0 + gate * (1.0 - sg)), da * gate * sg)
            for cs, dc in zip((cg, cv), dcs):
                n1 = nxt_ref[0:1, cs]
                n2 = nxt_ref[1:2, cs]
                r1, r2 = pltpu.roll(dc, ts - 1, 0), pltpu.roll(dc, ts - 2, 0)
                f1 = jnp.concatenate([r1[:ts - 8], jnp.where(row8 == 7, n1, r1[ts - 8:])], axis=0)
                f2 = jnp.concatenate([r2[:ts - 8], jnp.where(row8 == 7, n2, jnp.where(row8 == 6, n1, r2[ts - 8:]))], axis=0)
                uv = u_ref[:, cs].astype(F32)
                db_ref[:, cs] += jnp.sum(dc, axis=0, keepdims=True)
                dw_ref[0:1, cs] += jnp.sum(f2 * uv, axis=0, keepdims=True)
                dw_ref[1:2, cs] += jnp.sum(f1 * uv, axis=0, keepdims=True)
                dw_ref[2:3, cs] += jnp.sum(dc * uv, axis=0, keepdims=True)
                du_ref[:, cs] = (wc_ref[2:3, cs] * dc + wc_ref[1:2, cs] * f1 + wc_ref[0:1, cs] * f2).astype(BF16)
                nxt_ref[:, cs] = dc[0:8, :]

    rev = lambda i: (nt - 1 - i, 0)
    const2 = lambda i: (0, 0)
    return pl.pallas_call(
        body, name="ffn_bwd", grid=(nt,),
        in_specs=[pl.BlockSpec((ts, D), rev), pl.BlockSpec((ts, N_UP), rev), pl.BlockSpec((ts, N_UP), rev),
                  pl.BlockSpec((3, N_UP), const2), pl.BlockSpec((D_FF, D), const2)],
        out_specs=[pl.BlockSpec((ts, N_UP), rev), pl.BlockSpec((1, N_UP), const2), pl.BlockSpec((3, N_UP), const2)],
        out_shape=[jax.ShapeDtypeStruct((s, N_UP), BF16), jax.ShapeDtypeStruct((1, N_UP), F32),
                   jax.ShapeDtypeStruct((3, N_UP), F32)],
        scratch_shapes=[pltpu.VMEM((8, N_UP), F32)],
        compiler_params=_cp("arbitrary"),
    )(dx2b, u, c, wconv, wdown)


ANY = pl.BlockSpec(memory_space=pl.ANY)


def _place():
    x, y, c = lax.axis_index("x"), lax.axis_index("y"), lax.axis_index("c")
    chips = [(1 - x, y), (x, 1 - y), (1 - x, 1 - y)]
    return x, y, c, chips


def _half(shape, c, axis):
    size = shape[axis] // 2
    cut = pl.ds(pl.multiple_of(c * size, 8 if axis == 0 else 128), size)
    return (cut, slice(None)) if axis == 0 else (slice(None), cut)


def _half_shape(shape, axis):
    return (shape[0] // 2, shape[1]) if axis == 0 else (shape[0], shape[1] // 2)


def _remote(src, dst, send_sems, recv_sems, k, to):
    return pltpu.make_async_remote_copy(src_ref=src, dst_ref=dst, send_sem=send_sems.at[k], recv_sem=recv_sems.at[k],
                                        device_id=to, device_id_type=MESH)


def _sibling_exchange(grads, axes, smalls, name):
    nb = len(grads)
    n = nb + len(smalls)

    def body(*refs):
        ins, outs = refs[:n], refs[n:2 * n]
        send_sems, recv_sems = refs[2 * n:]
        x, y, c, _ = _place()
        sib = (x, y, 1 - c)
        cps = []
        for a in range(nb):
            theirs = _half(grads[a].shape[1:], 1 - c, axes[a])
            cps.append(_remote(ins[a].at[(slice(None),) + theirs], outs[a], send_sems, recv_sems, a, sib))
        for a in range(nb, n):
            cps.append(_remote(ins[a], outs[a], send_sems, recv_sems, a, sib))
        for cp in cps:
            cp.start()
        for cp in cps:
            cp.wait()

    out_shape = [jax.ShapeDtypeStruct((4,) + _half_shape(g.shape[1:], ax), g.dtype) for g, ax in zip(grads, axes)]
    out_shape += [jax.ShapeDtypeStruct(a.shape, F32) for a in smalls]
    return pl.pallas_call(
        body, name=name, in_specs=[ANY] * n, out_specs=[ANY] * n, out_shape=out_shape,
        scratch_shapes=[pltpu.SemaphoreType.DMA((n,)), pltpu.SemaphoreType.DMA((n,))],
        compiler_params=pltpu.CompilerParams(has_side_effects=True),
    )(*grads, *smalls)


def _gather_share(lands, axes, name):
    n = len(lands)

    def body(*refs):
        outs = refs[n:2 * n]
        send_sems, recv_sems = refs[2 * n:]
        x, y, c, chips = _place()
        sib = (x, y, 1 - c)
        cps = []
        for a in range(n):
            mine = _half(lands[a].shape[1:], c, axes[a])
            for k, ch in enumerate(chips):
                landed = outs[a].at[(2 * ch[0] + ch[1],) + mine]
                cps.append(_remote(landed, landed, send_sems, recv_sems, 3 * a + k, sib))
        for cp in cps:
            cp.start()
        for a in range(n):
            other = _half(lands[a].shape[1:], 1 - c, axes[a])
            for k, ch in enumerate(chips):
                landed = outs[a].at[(2 * ch[0] + ch[1],) + other]
                _remote(landed, landed, send_sems, recv_sems, 3 * a + k, sib).wait_recv()
        for cp in cps:
            cp.wait_send()

    return pl.pallas_call(
        body, name=name, in_specs=[ANY] * n, out_specs=[ANY] * n,
        out_shape=[jax.ShapeDtypeStruct(a.shape, a.dtype) for a in lands],
        input_output_aliases={a: a for a in range(n)},
        scratch_shapes=[pltpu.SemaphoreType.DMA((3 * n,)), pltpu.SemaphoreType.DMA((3 * n,))],
        compiler_params=pltpu.CompilerParams(has_side_effects=True),
    )(*lands)


def _sibling_share(halves, name):
    n = len(halves)

    def body(*refs):
        ins, outs = refs[:n], refs[n:2 * n]
        send_sems, recv_sems = refs[2 * n:]
        x, y, c, _ = _place()
        cps = [_remote(ins[a], outs[a], send_sems, recv_sems, a, (x, y, 1 - c)) for a in range(n)]
        for cp in cps:
            cp.start()
        for cp in cps:
            cp.wait()

    return pl.pallas_call(
        body, name=name, in_specs=[ANY] * n, out_specs=[ANY] * n,
        out_shape=[jax.ShapeDtypeStruct(h.shape, F32) for h in halves],
        scratch_shapes=[pltpu.SemaphoreType.DMA((n,)), pltpu.SemaphoreType.DMA((n,))],
        compiler_params=pltpu.CompilerParams(has_side_effects=True),
    )(*halves)


HBM = pl.BlockSpec(memory_space=pltpu.HBM)
SEM = pl.BlockSpec(memory_space=pltpu.SEMAPHORE)
DATAFLOW = pltpu.SideEffectType.DATAFLOW_SIDE_EFFECTING


def _split_start(name, srcs, land_shapes, plan, n_copies, after):
    lands = [lax.empty(*ls) if isinstance(ls, tuple) else ls for ls in land_shapes]
    bufs = list(srcs) + lands
    nb, ns = len(bufs), len(srcs)

    def body(*refs):
        send_sems, recv_sems, token = refs[nb + 1], refs[nb + 2], refs[-1]
        for k, (src, dst, to) in enumerate(plan(refs[:ns], refs[ns:nb])):
            _remote(src, dst, send_sems, recv_sems, k, to).start()
        token[...] = jnp.zeros_like(token)

    res = pl.pallas_call(
        body, name=name,
        out_shape=(pltpu.SemaphoreType.DMA((n_copies,)), pltpu.SemaphoreType.DMA((n_copies,)),
                   *[pltpu.HBM(b.shape, b.dtype) for b in bufs], jax.ShapeDtypeStruct((8, 128), F32)),
        in_specs=[HBM] * nb + [ANY],
        out_specs=(SEM, SEM, *[HBM] * nb, pl.BlockSpec(memory_space=pltpu.VMEM)),
        input_output_aliases={i: 2 + i for i in range(nb)},
        compiler_params=pltpu.CompilerParams(has_side_effects=DATAFLOW),
    )(*[pltpu.with_memory_space_constraint(b, pltpu.HBM) for b in bufs], after)
    return (res[0], res[1], list(res[2:2 + nb])), res[-1]


def _split_wait(name, handle, n_srcs, plan, after):
    send_sems, recv_sems, bufs = handle
    nb = len(bufs)

    def body(*refs):
        sends, recvs = refs[nb], refs[nb + 1]
        for k, (src, dst, to) in enumerate(plan(refs[:n_srcs], refs[n_srcs:nb])):
            cp = _remote(src, dst, sends, recvs, k, to)
            cp.wait_send()
            cp.wait_recv()

    res = pl.pallas_call(
        body, name=name, out_shape=[pltpu.HBM(b.shape, b.dtype) for b in bufs],
        in_specs=[HBM] * nb + [SEM, SEM, ANY], out_specs=[HBM] * nb,
        input_output_aliases={i: i for i in range(nb)},
        compiler_params=pltpu.CompilerParams(has_side_effects=DATAFLOW),
    )(*bufs, send_sems, recv_sems, after)
    return list(res[:n_srcs]), list(res[n_srcs:])


def _gather_plan(shapes, axes, n_whole=0):
    def plan(srcs, lands):
        x, y, c, chips = _place()
        out = []
        for a, (shape, axis) in enumerate(zip(shapes, axes)):
            mine = _half(shape, c, axis)
            for ch in chips:
                out.append((srcs[a].at[mine], lands[a].at[(2 * x + y,) + mine], (ch[0], ch[1], c)))
        for a in range(len(shapes), len(shapes) + n_whole):
            for ch in chips:
                out.append((srcs[a], lands[a].at[2 * x + y], (ch[0], ch[1], c)))
        return out
    return plan


def _share_plan(shapes, axes):
    def plan(srcs, lands):
        x, y, c, chips = _place()
        out = []
        for a, (shape, axis) in enumerate(zip(shapes, axes)):
            mine = _half(shape, c, axis)
            for ch in chips:
                landed = lands[a].at[(2 * ch[0] + ch[1],) + mine]
                out.append((landed, landed, (x, y, 1 - c)))
        return out
    return plan


def _sibling_plan(shapes, axes):
    def plan(srcs, lands):
        x, y, c, _ = _place()
        return [(srcs[a].at[(slice(None),) + _half(shape, 1 - c, axis)], lands[a], (x, y, 1 - c))
                for a, (shape, axis) in enumerate(zip(shapes, axes))]
    return plan


def _reduce_plan(n_big, n_small):
    def plan(srcs, lands):
        x, y, c, chips = _place()
        out = []
        for a in range(n_big):
            for k, ch in enumerate(chips):
                out.append((srcs[a].at[2 * ch[0] + ch[1]], lands[a].at[k], (ch[0], ch[1], c)))
        for a in range(n_big, n_big + n_small):
            for ch in chips:
                out.append((srcs[a], lands[a].at[2 * x + y], (ch[0], ch[1], c)))
        return out
    return plan


def _row_tile(rows, cols, mult):
    best = mult
    for t in range(mult, rows + 1, mult):
        if rows % t == 0 and t * cols * 4 <= (2 << 20):
            best = t
    return best if rows % best == 0 else rows


COL_TILE = 256


def _half_tiling(hshape, axis, mult):
    hr, hc = hshape
    if axis == 0:
        tr = _row_tile(hr, hc, mult)
        return tr, hc, hr // tr
    return hr, COL_TILE, hc // COL_TILE


def _tile_idx(axis, t):
    return (t, 0) if axis == 0 else (0, t)


def _chip_partial(place, g, t, axis, name):
    hshape = t.shape[1:]
    br, bc, nt = _half_tiling(hshape, axis, 16)

    def body(pl_ref, g_ref, t_ref, pf_ref, pb_ref):
        v = g_ref[...].astype(F32) + t_ref[...].astype(F32)
        pb_ref[...] = v.astype(BF16)

        @pl.when(pl.program_id(1) == pl_ref[0])
        def _():
            pf_ref[...] = v

    blk = (None, br, bc)
    return pl.pallas_call(
        body, name=name,
        grid_spec=pltpu.PrefetchScalarGridSpec(
            num_scalar_prefetch=1, grid=(nt, 4),
            in_specs=[pl.BlockSpec(blk, lambda i, j, p: (j,) + _tile_idx(axis, p[1] * nt + i)),
                      pl.BlockSpec(blk, lambda i, j, p: (j,) + _tile_idx(axis, i))],
            out_specs=[pl.BlockSpec((br, bc), lambda i, j, p: _tile_idx(axis, i)),
                       pl.BlockSpec(blk, lambda i, j, p: (j,) + _tile_idx(axis, i))]),
        out_shape=[jax.ShapeDtypeStruct(hshape, F32), jax.ShapeDtypeStruct((4,) + hshape, BF16)],
        compiler_params=_cp("arbitrary", "arbitrary"),
    )(place, g, t)


def _finish_half(pf, rb, axis, name):
    hshape = pf.shape
    br, bc, nt = _half_tiling(hshape, axis, 16)

    def body(pf_ref, rb_ref, o_ref):
        o_ref[...] = ((pf_ref[...] + rb_ref[0].astype(F32)) + rb_ref[1].astype(F32)) + rb_ref[2].astype(F32)

    return pl.pallas_call(
        body, name=name, grid=(nt,),
        in_specs=[pl.BlockSpec((br, bc), lambda i: _tile_idx(axis, i)),
                  pl.BlockSpec((3, br, bc), lambda i: (0,) + _tile_idx(axis, i))],
        out_specs=pl.BlockSpec((br, bc), lambda i: _tile_idx(axis, i)),
        out_shape=jax.ShapeDtypeStruct(hshape, F32),
        compiler_params=_cp("arbitrary"),
    )(pf, rb)


def _adam_math(w, g, m, v):
    m = ADAM_B1 * m + (1.0 - ADAM_B1) * g
    v = ADAM_B2 * v + (1.0 - ADAM_B2) * (g * g)
    m_hat = m / (1.0 - ADAM_B1 ** ADAM_STEP)
    v_hat = v / (1.0 - ADAM_B2 ** ADAM_STEP)
    return -ADAM_LR * (m_hat / (jnp.sqrt(v_hat) + ADAM_EPS) + ADAM_WD * w), m, v


def _adam_halves(place, w, mine, theirs, m, v, axis, name):
    br, bc, nt = _half_tiling(mine.shape, axis, 8)

    def body(pl_ref, w_ref, a_ref, b_ref, m_ref, v_ref, g_ref, d_ref, mo_ref, vo_ref):
        is_mine = pl.program_id(0) // nt == pl_ref[1]
        g = jnp.where(is_mine, a_ref[...], b_ref[...])
        d, mn, vn = _adam_math(w_ref[...], g, m_ref[...], v_ref[...])
        g_ref[...] = g
        d_ref[...] = d
        mo_ref[...] = mn
        vo_ref[...] = vn

    full = pl.BlockSpec((br, bc), lambda i, p: _tile_idx(axis, i))
    mine_spec = pl.BlockSpec((br, bc), lambda i, p: _tile_idx(axis, jnp.where(i // nt == p[1], i % nt, nt - 1)))
    theirs_spec = pl.BlockSpec((br, bc), lambda i, p: _tile_idx(axis, jnp.where(i // nt == p[1], 0, i % nt)))
    return pl.pallas_call(
        body, name=name,
        grid_spec=pltpu.PrefetchScalarGridSpec(
            num_scalar_prefetch=1, grid=(2 * nt,), in_specs=[full, mine_spec, theirs_spec, full, full],
            out_specs=[full] * 4),
        out_shape=[jax.ShapeDtypeStruct(w.shape, F32)] * 4, compiler_params=_cp("arbitrary"),
    )(place, w, mine, theirs, m, v)


def _add_many(xs, ys, name):
    n = len(xs)

    def body(*refs):
        for i in range(n):
            refs[2 * n + i][...] = refs[i][...] + refs[n + i][...]

    return pl.pallas_call(body, name=name, out_shape=[jax.ShapeDtypeStruct(a.shape, F32) for a in xs])(*xs, *ys)


def _adam_small(place, owns, landed, ws, ms, vs, widths):
    n, nw = len(owns), len(ws)

    def body(pl_ref, *refs):
        own_r, land_r = refs[:n], refs[n:2 * n]
        w_r, m_r, v_r = (refs[2 * n + k * nw:2 * n + (k + 1) * nw] for k in range(3))
        outs = refs[2 * n + 3 * nw:]
        g_o, d_o, m_o, v_o = outs[:n], outs[n:n + nw], outs[n + nw:n + 2 * nw], outs[n + 2 * nw:]
        for me in range(4):
            @pl.when(pl_ref[0] == me)
            def _(me=me):
                for i in range(n):
                    p = [own_r[i][...] if k == me else land_r[i][k] for k in range(4)]
                    g = ((p[0] + p[1]) + p[2]) + p[3]
                    if i < nw and widths[i]:
                        g = g[:, me * widths[i]:(me + 1) * widths[i]]
                    g_o[i][...] = g
                    if i < nw:
                        d, mn, vn = _adam_math(w_r[i][...], g, m_r[i][...], v_r[i][...])
                        d_o[i][...] = d
                        m_o[i][...] = mn
                        v_o[i][...] = vn

    g_shapes = [jax.ShapeDtypeStruct(ws[i].shape if i < nw else owns[i].shape, F32) for i in range(n)]
    w_shapes = [jax.ShapeDtypeStruct(w.shape, F32) for w in ws]
    whole = lambda a: pl.BlockSpec(a.shape, lambda i, p, nd=len(a.shape): (0,) * nd)
    ins = list(owns) + list(landed) + list(ws) + list(ms) + list(vs)
    out_shape = g_shapes + w_shapes * 3
    out = pl.pallas_call(
        body, name="adam_small",
        grid_spec=pltpu.PrefetchScalarGridSpec(num_scalar_prefetch=1, grid=(1,), in_specs=[whole(a) for a in ins],
                                               out_specs=[whole(a) for a in out_shape]),
        out_shape=out_shape, compiler_params=_cp("arbitrary"),
    )(place, *ins)
    return out[:n], out[n:n + nw], out[n + nw:n + 2 * nw], out[n + 2 * nw:]


def kernel(x, g_mix, w_in, b_gate, w_gk_up, b_gk, w_pool_grp, pool_scale, g_gla_head, w_pool_proj, w_gla_proj, w_out, g_ffn, w_up, w_conv, b_conv, w_down, g_final, loss_target, m_g_mix, m_w_in, m_b_gate, m_w_gk_up, m_b_gk, m_w_pool_grp, m_pool_scale, m_g_gla_head, m_w_pool_proj, m_w_gla_proj, m_w_out, m_g_ffn, m_w_up, m_w_conv, m_b_conv, m_w_down, m_g_final, v_g_mix, v_w_in, v_b_gate, v_w_gk_up, v_b_gk, v_w_pool_grp, v_pool_scale, v_g_gla_head, v_w_pool_proj, v_w_gla_proj, v_w_out, v_g_ffn, v_w_up, v_w_conv, v_b_conv, v_w_down, v_g_final):
    s = x.shape[1]
    ts = min(s, 512)
    tm = min(s, 256)
    cx, cy, cc = lax.axis_index("x"), lax.axis_index("y"), lax.axis_index("c")
    chip = 2 * cx + cy
    place = jnp.stack([chip, cc]).astype(jnp.int32)

    big_names = ("w_in", "w_pool_proj", "w_gla_proj", "w_out", "w_up", "w_down")
    axes = (1, 0, 0, 0, 0, 0)
    shards = dict(w_in=jnp.transpose(w_in[0]), w_pool_proj=w_pool_proj[0], w_gla_proj=w_gla_proj[0], w_out=w_out[0],
                  w_up=w_up[0], w_down=w_down[0])
    def fill_own(lands, mine):
        return [lax.dynamic_update_slice(g, o_[None], (chip, 0, 0)) for g, o_ in zip(lands, mine)]

    def gather_start(tag, halves, group_axes, whole, after):
        plan = _gather_plan([o_.shape for o_ in halves], group_axes, len(whole))
        srcs = list(halves) + list(whole)
        handle, token = _split_start("gather_" + tag + "_start", srcs, [((4,) + o_.shape, o_.dtype) for o_ in srcs], plan,
                                     3 * len(srcs), after)
        return (handle, plan, len(halves), len(srcs), group_axes), token

    def gather_finish(tag, started, after):
        handle, plan, n_halves, n, group_axes = started
        mine, lands = _split_wait("gather_" + tag + "_wait", handle, n, plan, after)
        lands[:n_halves] = _gather_share(lands[:n_halves], group_axes, "gather_" + tag + "_share")
        return fill_own(lands, mine)

    in_w, tok = gather_start("in", [shards["w_in"].astype(BF16)], axes[:1], [], g_mix)
    zero = tok[0, 0]
    own = [(shards[n] + zero).astype(BF16) for n in big_names[1:]]
    mix_w, tok = gather_start("mix", own[0:3], axes[1:4], [w_gk_up[0] + zero, w_conv[0] + zero], tok)
    ffn_w, tok = gather_start("ffn", own[3:5], axes[4:6], [], tok)
    xs, tgt = x[0], loss_target[0]
    wgrp = w_pool_grp[0]
    h = _rmsnorm(xs, g_mix, tok, "norm_mix", ts)
    m_in_t, v_in_t = jnp.transpose(m_w_in[0]), jnp.transpose(v_w_in[0])
    h, m_in_t, v_in_t = lax.optimization_barrier((h, m_in_t, v_in_t))
    w_in_t = gather_finish("in", in_w, h)[0].reshape(N_IN, D)
    nsh = N_IN // 4

    zr = _in_proj(h, w_in_t, 1152)
    p, pp = _pool_fwd(zr, wgrp, pool_scale)
    wpp, wgla, wout, wgk4, wconv4 = gather_finish("mix", mix_w, pp)
    wgla, wout = wgla.reshape(D, D), wout.reshape(D, D)
    wgk_full = jnp.transpose(wgk4, (1, 0, 2)).reshape(GATE_RANK, 512)
    wconv_full = jnp.transpose(wconv4, (1, 0, 2)).reshape(3, N_UP)
    wgk_pad = jnp.concatenate([wgk_full, jnp.zeros((128 - GATE_RANK, 512), F32)], axis=0)
    o, og, sp = _gla_fwd(zr, wgk_pad, b_gk, g_gla_head, ts)
    ffn_handle, ffn_plan_w, _, ffn_n, _ = ffn_w
    ffn_mine, ffn_lands = _split_wait("gather_ffn_wait", ffn_handle, ffn_n, ffn_plan_w, og)
    share_plan = _share_plan([o_.shape for o_ in ffn_mine], axes[4:6])
    share, tok = _split_start("gather_ffn_share_start", [], ffn_lands, share_plan, 3 * ffn_n, og)
    x1, mixed, yp, yg, h2 = _merge_fwd(xs, zr, pp, og, b_gate, wpp, wgla, wout, g_ffn, tok, ts)
    wup, wdown = fill_own(_split_wait("gather_ffn_share_wait", share, 0, share_plan, x1)[1], ffn_mine)
    wdown = wdown.reshape(D_FF, D)
    u = _matmul_resident(h2, wup, "ffn_up", None)
    a, conv_out, dx2, dx2b, loss_part, dgfin = _ffn_down_loss(u, x1, tgt, wconv_full, b_conv, wdown,
                                                              g_final.reshape(1, D), tm)

    du, dbconv, dwconv = _ffn_bwd(dx2b, u, conv_out, wconv_full, wdown, tm)
    dw_down = _matmul_tn(a, dx2b, "dw_down", D, tm=1408)
    dw_up = _matmul_tn(h2, du, "dw_up", 1408, shard_major=True)

    def exchange_start(tag, grads, group_axes, after):
        plan = _sibling_plan([g.shape[1:] for g in grads], group_axes)
        lands = [((4,) + _half_shape(g.shape[1:], ax), g.dtype) for g, ax in zip(grads, group_axes)]
        handle, token = _split_start("sibling_" + tag + "_start", grads, lands, plan, len(grads), after)
        return (handle, plan, len(grads)), token

    def partials(tag, names, group_axes, exchange, after):
        handle, plan, n = exchange
        mine, theirs = _split_wait("sibling_" + tag + "_wait", handle, n, plan, after)
        return zip(*[_chip_partial(place, g, t, ax, "chip_partial_" + nm)
                     for nm, ax, g, t in zip(names, group_axes, mine, theirs)])

    ffn_names, ffn_axes = ("w_up", "w_down"), (0, 0)
    ffn_x, token = exchange_start("ffn", [dw_up, dw_down.reshape(4, 704, D)], ffn_axes, du)
    dx1, dx1b, dgffn = _matmul_nt_normbwd(du, wup, x1, g_ffn, dx2, token, "ffn_up_bwd", ts)
    ffn_pf, ffn_pb = partials("ffn", ffn_names, ffn_axes, ffn_x, dx1b)
    ffn_plan = _reduce_plan(2, 0)
    ffn_handle, token = _split_start("reduce_ffn_start", ffn_pb, [((3,) + p.shape[1:], BF16) for p in ffn_pb],
                                     ffn_plan, 6, ffn_pf[0])

    dzg, dyp, dyg, dpp, do, dzog, dbgate, dghead = _merge_bwd(dx1b, zr, yp, yg, o, b_gate, g_gla_head, wpp, wgla, wout,
                                                             token, ts)
    dw_out = _matmul_tn(mixed, dx1b, "dw_out", D)
    dw_gla = _matmul_tn(og, dyg, "dw_gla", D)
    dw_pp = _matmul_tn(pp, dyp, "dw_pp", 256, shard_major=True)

    out_names, out_axes = ("w_pool_proj", "w_gla_proj", "w_out"), (0, 0, 0)
    out_x, token = exchange_start("out", [dw_pp, dw_gla.reshape(4, 256, D), dw_out.reshape(4, 256, D)], out_axes, dpp)
    dzp, dwgrp, dscale = _pool_bwd(p, dpp, wgrp, pool_scale, token)
    out_pf, out_pb = partials("out", out_names, out_axes, out_x, dzp)
    out_plan = _reduce_plan(3, 0)
    out_handle, token = _split_start("reduce_out_start", out_pb, [((3,) + p_.shape[1:], BF16) for p_ in out_pb],
                                     out_plan, 9, out_pf[0])
    dq, dk, dv, dgpre = _gla_bwd(zr, do, sp, wgk_pad, b_gk, token, ts)
    dzgk, dwgk, dbgk = _gk_bwd(dgpre, zr, wgk_pad, dgpre, ts)
    dzr = jnp.concatenate([dzg, dv, dzog, dzp, dq, dk, dzgk], axis=1)
    dw_rt = _matmul_tn(dzr, h, "dw_in", D, tm=1152)

    def grad_rows(lo, hi):
        out = []
        for seg_lo, seg_hi, at in ((0, 1536, OFF_POOL), (1536, 3584, OFF_V), (3584, 3600, OFF_GK), (3600, N_IN, OFF_GATE)):
            a_, b_ = max(lo, seg_lo), min(hi, seg_hi)
            if a_ < b_:
                out.append(dw_rt[at + a_ - seg_lo:at + b_ - seg_lo])
        return jnp.concatenate(out, axis=0)

    dw_in_t = jnp.stack([grad_rows(j * nsh, (j + 1) * nsh) for j in range(4)])

    in_sib = _sibling_exchange([dw_in_t], (1,), [], "sibling_exchange_in")
    in_pf, in_pb = _chip_partial(place, dw_in_t, in_sib[0], 1, "chip_partial_w_in")
    in_plan = _reduce_plan(1, 0)
    in_handle, token = _split_start("reduce_in_start", [in_pb], [((3,) + in_pb.shape[1:], BF16)], in_plan, 3, in_pf)
    grad_x, _, dgmix = _matmul_nt_normbwd(dzr, w_in_t, xs, g_mix, dx1, token, "in_proj_bwd", ts, transposed=True)
    small_names = ("g_mix", "b_gate", "w_gk_up", "b_gk", "w_pool_grp", "pool_scale", "g_gla_head", "g_ffn", "w_conv",
                   "b_conv", "g_final")
    small_mine = [dgmix, dbgate, dwgk[:GATE_RANK], dbgk, dwgrp.reshape(4 * 128, 128), dscale, dghead, dgffn, dwconv, dbconv,
                  dgfin, loss_part]
    small_sib = _sibling_exchange([], (), small_mine, "sibling_exchange_small")
    small_chip = _add_many(small_mine, small_sib, "chip_partial_small")
    small_plan = _reduce_plan(0, len(small_chip))
    small_handle, token = _split_start("reduce_small_start", small_chip, [((4,) + a_.shape, F32) for a_ in small_chip],
                                       small_plan, 3 * len(small_chip), small_mine[0])

    ms = dict(w_in=m_in_t, w_pool_proj=m_w_pool_proj[0], w_gla_proj=m_w_gla_proj[0], w_out=m_w_out[0],
              w_up=m_w_up[0], w_down=m_w_down[0])
    vs = dict(w_in=v_in_t, w_pool_proj=v_w_pool_proj[0], w_gla_proj=v_w_gla_proj[0], w_out=v_w_out[0],
              w_up=v_w_up[0], w_down=v_w_down[0])
    grad, delta, new_m, new_v = {}, {}, {}, {}

    def finish_and_update(names, group_axes, part_f, landed, tag):
        halves = [_finish_half(pf, rb, ax, "finish_" + n) for n, ax, pf, rb in zip(names, group_axes, part_f, landed)]
        sib_halves = _sibling_share(halves, "sibling_share_" + tag)
        for n, ax, mine, theirs in zip(names, group_axes, halves, sib_halves):
            res = _adam_halves(place, shards[n], mine, theirs, ms[n], vs[n], ax, "adam_" + n)
            if n == "w_in":
                res = [jnp.transpose(r_) for r_ in res]
            grad[n], delta[n], new_m[n], new_v[n] = [r_[None] for r_ in res]

    _, ffn_landed = _split_wait("reduce_ffn_wait", ffn_handle, 2, ffn_plan, token)
    _, out_landed = _split_wait("reduce_out_wait", out_handle, 3, out_plan, ffn_landed[0])
    finish_and_update(ffn_names + out_names, ffn_axes + out_axes, ffn_pf + out_pf, ffn_landed + out_landed, "rest")
    _, in_landed = _split_wait("reduce_in_wait", in_handle, 1, in_plan, delta["w_out"])
    finish_and_update(("w_in",), (1,), (in_pf,), in_landed, "in")
    small_sent, small_landed = _split_wait("reduce_small_wait", small_handle, len(small_chip), small_plan, delta["w_in"])
    given = dict(g_mix=(g_mix, m_g_mix, v_g_mix), b_gate=(b_gate, m_b_gate, v_b_gate), w_gk_up=(w_gk_up, m_w_gk_up, v_w_gk_up),
                 b_gk=(b_gk, m_b_gk, v_b_gk), w_pool_grp=(w_pool_grp, m_w_pool_grp, v_w_pool_grp),
                 pool_scale=(pool_scale, m_pool_scale, v_pool_scale), g_gla_head=(g_gla_head, m_g_gla_head, v_g_gla_head),
                 g_ffn=(g_ffn, m_g_ffn, v_g_ffn), w_conv=(w_conv, m_w_conv, v_w_conv), b_conv=(b_conv, m_b_conv, v_b_conv),
                 g_final=(g_final, m_g_final, v_g_final))
    flat2 = lambda a: a.reshape(-1, a.shape[-1])
    widths = [dict(w_gk_up=128, w_conv=1408).get(n) for n in small_names]
    totals, ds, mo, vo = _adam_small(place, small_sent, small_landed, *[[flat2(given[n][k]) for n in small_names] for k in range(3)],
                                     widths)
    loss = totals[-1][0, 0]
    for i, n in enumerate(small_names):
        shp = given[n][0].shape
        grad[n], delta[n], new_m[n], new_v[n] = [r_.reshape(shp) for r_ in (totals[i], ds[i], mo[i], vo[i])]

    order = ("g_mix", "w_in", "b_gate", "w_gk_up", "b_gk", "w_pool_grp", "pool_scale", "g_gla_head", "w_pool_proj",
             "w_gla_proj", "w_out", "g_ffn", "w_up", "w_conv", "b_conv", "w_down", "g_final")
    return (loss, grad_x[None], *[grad[n] for n in order], *[delta[n] for n in order], *[new_m[n] for n in order],
            *[new_v[n] for n in order])
```

```python
import functools

import jax
import jax.numpy as jnp
from jax import lax
from jax.experimental import pallas as pl
from jax.experimental.pallas import tpu as pltpu

F32 = jnp.float32
BF16 = jnp.bfloat16
MESH = pl.DeviceIdType.MESH

D = 1024
EPS = 1e-6
CHUNK = 64
POOL_W = 512
POOL_WINDOWS = (2, 4, 8, 16)
HEADS = 4
HK = 128
HV = 256
GATE_RANK = 16
D_FF = 2816
N_UP = 2 * D_FF
N_IN = 5648
QSCALE = HK ** -0.5
N_INR = 5760
OFF_GATE, OFF_V, OFF_OG, OFF_POOL, OFF_Q, OFF_K, OFF_GK = 0, 2048, 3072, 4096, 4608, 5120, 5632

ADAM_LR, ADAM_B1, ADAM_B2, ADAM_EPS, ADAM_WD, ADAM_STEP = 0.001, 0.9, 0.999, 1e-08, 0.01, 10

VMEM_LIMIT = 56 * 1024 * 1024


def _cp(*sem):
    return pltpu.CompilerParams(dimension_semantics=sem if sem else None, vmem_limit_bytes=VMEM_LIMIT)


def _dot(a, b):
    return jnp.dot(a, b, preferred_element_type=F32)


def _dot_nt(a, b):
    return lax.dot_general(a, b, (((1,), (1,)), ((), ())), preferred_element_type=F32)


def _dot_tn(a, b):
    return lax.dot_general(a, b, (((0,), (0,)), ((), ())), preferred_element_type=F32)


def _sigmoid(v):
    return 1.0 / (1.0 + jnp.exp(-v))


def _rows(shape):
    return lax.broadcasted_iota(jnp.int32, shape, 0)


def _pick_row(v, r):
    return jnp.sum(jnp.where(_rows(v.shape) == r, v, 0.0), axis=0, keepdims=True)


def _rmsnorm(x, g, after, name, ts):
    s = x.shape[0]

    def body(x_ref, g_ref, after_ref, h_ref):
        xv = x_ref[...]
        r = lax.rsqrt(jnp.mean(xv * xv, axis=-1, keepdims=True) + EPS)
        h_ref[...] = (xv * r * g_ref[...]).astype(BF16)

    return pl.pallas_call(
        body, name=name, grid=(s // ts,),
        in_specs=[pl.BlockSpec((ts, D), lambda i: (i, 0)), pl.BlockSpec((1, D), lambda i: (0, 0)), ANY],
        out_specs=pl.BlockSpec((ts, D), lambda i: (i, 0)), out_shape=jax.ShapeDtypeStruct((s, D), BF16),
        compiler_params=_cp("arbitrary"),
    )(x, g, after)


MM_ROWS = 512


def _matmul_resident(h, w, after, name):
    s = h.shape[0]
    nj, tn = w.shape[0], w.shape[2]
    rc = min(s, MM_ROWS)

    def body(h_ref, w_ref, after_ref, z_ref):
        for r0 in range(0, s, rc):
            z_ref[r0:r0 + rc, :] = _dot(h_ref[r0:r0 + rc, :], w_ref[...]).astype(BF16)

    return pl.pallas_call(
        body, name=name, grid=(nj,),
        in_specs=[pl.BlockSpec((s, D), lambda j: (0, 0)), pl.BlockSpec((None, D, tn), lambda j: (j, 0, 0)), ANY],
        out_specs=pl.BlockSpec((s, tn), lambda j: (0, j)), out_shape=jax.ShapeDtypeStruct((s, nj * tn), BF16),
        compiler_params=_cp("arbitrary"),
    )(h, w, after)


PROJ_PIECES = ((3600, 2048, OFF_GATE), (1536, 2048, OFF_V), (0, 1536, OFF_POOL), (3584, GATE_RANK, OFF_GK))


def _load_projection(w_hbm, w_ref, sems):
    cps = [pltpu.make_async_copy(w_hbm.at[pl.ds(src, n)], w_ref.at[pl.ds(dst, n)], sems.at[i])
           for i, (src, n, dst) in enumerate(PROJ_PIECES)]
    for cp in cps:
        cp.start()
    w_ref[OFF_GK + GATE_RANK:, :] = jnp.zeros((N_INR - OFF_GK - GATE_RANK, D), BF16)
    for cp in cps:
        cp.wait()


def _in_proj(h, w_nat, tn):
    s = h.shape[0]
    rc = min(s, MM_ROWS)

    def body(h_ref, w_hbm, z_ref, w_ref, sems):
        j = pl.program_id(0)

        @pl.when(j == 0)
        def _():
            _load_projection(w_hbm, w_ref, sems)

        wt = w_ref[pl.ds(pl.multiple_of(j * tn, 128), tn), :]
        for r0 in range(0, s, rc):
            z_ref[r0:r0 + rc, :] = _dot_nt(h_ref[r0:r0 + rc, :], wt).astype(BF16)

    return pl.pallas_call(
        body, name="in_proj", grid=(N_INR // tn,),
        in_specs=[pl.BlockSpec((s, D), lambda j: (0, 0)), ANY],
        out_specs=pl.BlockSpec((s, tn), lambda j: (0, j)), out_shape=jax.ShapeDtypeStruct((s, N_INR), BF16),
        scratch_shapes=[pltpu.VMEM((N_INR, D), BF16), pltpu.SemaphoreType.DMA((len(PROJ_PIECES),))],
        compiler_params=_cp("arbitrary"),
    )(h, w_nat)


def _matmul_nt_normbwd(dz, w, x, g, resid, after, name, ts, transposed=False):
    s = x.shape[0]
    w_vmem = (N_INR, D) if transposed else w.shape

    def body(dz_ref, w_hbm, x_ref, g_ref, r_ref, after_ref, o_ref, ob_ref, dg_ref, w_ref, sems):
        @pl.when(pl.program_id(0) == 0)
        def _():
            if transposed:
                _load_projection(w_hbm, w_ref, sems)
            else:
                cp = pltpu.make_async_copy(w_hbm, w_ref, sems.at[0])
                cp.start()
                cp.wait()
            dg_ref[...] = jnp.zeros_like(dg_ref)

        if transposed:
            dh = _dot(dz_ref[...], w_ref[...])
        else:
            kc = w.shape[2]
            dh = _dot_nt(dz_ref[:, 0:kc], w_ref[0])
            for j in range(1, w.shape[0]):
                dh = dh + _dot_nt(dz_ref[:, j * kc:(j + 1) * kc], w_ref[j])
        xv = x_ref[...]
        r = lax.rsqrt(jnp.mean(xv * xv, axis=-1, keepdims=True) + EPS)
        xh = xv * r
        dg_ref[...] += jnp.sum(dh * xh, axis=0, keepdims=True)
        dxh = dh * g_ref[...]
        out = r_ref[...] + r * (dxh - xh * jnp.mean(dxh * xh, axis=-1, keepdims=True))
        o_ref[...] = out
        ob_ref[...] = out.astype(BF16)

    row = lambda i: (i, 0)
    kdim = dz.shape[1]
    return pl.pallas_call(
        body, name=name, grid=(s // ts,),
        in_specs=[pl.BlockSpec((ts, kdim), row), ANY, pl.BlockSpec((ts, D), row),
                  pl.BlockSpec((1, D), lambda i: (0, 0)), pl.BlockSpec((ts, D), row), ANY],
        out_specs=[pl.BlockSpec((ts, D), row), pl.BlockSpec((ts, D), row), pl.BlockSpec((1, D), lambda i: (0, 0))],
        out_shape=[jax.ShapeDtypeStruct((s, D), F32), jax.ShapeDtypeStruct((s, D), BF16),
                   jax.ShapeDtypeStruct((1, D), F32)],
        scratch_shapes=[pltpu.VMEM(w_vmem, BF16), pltpu.SemaphoreType.DMA((len(PROJ_PIECES),))],
        compiler_params=_cp("arbitrary"),
    )(dz, w, x, g, resid, after)


def _matmul_tn(a, b, name, tn, shard_major=False, tm=None):
    s, m = a.shape
    n = b.shape[1]
    tm = m if tm is None else tm
    ni, nj = m // tm, n // tn

    def body(a_ref, b_ref, o_ref):
        o_ref[...] = _dot_tn(a_ref[...], b_ref[...]).astype(BF16)

    if shard_major:
        out_spec = pl.BlockSpec((None, tm, tn), lambda i, j: (j, i, 0))
        out_shape = jax.ShapeDtypeStruct((nj, m, tn), BF16)
    else:
        out_spec = pl.BlockSpec((tm, tn), lambda i, j: (i, j))
        out_shape = jax.ShapeDtypeStruct((m, n), BF16)
    return pl.pallas_call(
        body, name=name, grid=(ni, nj),
        in_specs=[pl.BlockSpec((s, tm), lambda i, j: (0, i)), pl.BlockSpec((s, tn), lambda i, j: (0, j))],
        out_specs=out_spec, out_shape=out_shape,
        compiler_params=_cp("arbitrary", "arbitrary"),
    )(a, b)


def _pool_fwd(zr, wgrp, scale):
    s = zr.shape[0]

    def body(u_ref, w_ref, sc_ref, p_ref, pp_ref):
        row = _rows((s, 128))
        for gi, win in enumerate(POOL_WINDOWS):
            cs = slice(gi * 128, (gi + 1) * 128)
            u = u_ref[:, cs].astype(F32)
            acc, k = u, 1
            while k < win:
                acc = acc + jnp.where(row >= k, pltpu.roll(acc, k, 0), 0.0)
                k *= 2
            cnt = jnp.minimum(row + 1, win).astype(F32)
            p = (acc / cnt - u).astype(BF16)
            p_ref[:, cs] = p
            pp_ref[:, cs] = (_dot(p, w_ref[gi].astype(BF16)) * sc_ref[:, cs]).astype(BF16)

    return pl.pallas_call(
        body, name="pool_fwd", grid=(1,),
        in_specs=[pl.BlockSpec((s, POOL_W), lambda i: (0, OFF_POOL // POOL_W)),
                  pl.BlockSpec((4, 128, 128), lambda i: (0, 0, 0)), pl.BlockSpec((1, POOL_W), lambda i: (0, 0))],
        out_specs=[pl.BlockSpec((s, POOL_W), lambda i: (0, 0))] * 2,
        out_shape=[jax.ShapeDtypeStruct((s, POOL_W), BF16)] * 2,
        compiler_params=_cp("arbitrary"),
    )(zr, wgrp, scale)


def _pool_bwd(p, dpp, wgrp, scale, after):
    s = p.shape[0]

    def body(p_ref, dpp_ref, w_ref, sc_ref, after_ref, dz_ref, dw_ref, dsc_ref):
        row = _rows((s, 128))
        for gi, win in enumerate(POOL_WINDOWS):
            cs = slice(gi * 128, (gi + 1) * 128)
            pv = p_ref[:, cs]
            wb = w_ref[gi].astype(BF16)
            dpp_v = dpp_ref[:, cs].astype(F32)
            dsc_ref[:, cs] = jnp.sum(dpp_v * _dot(pv, wb), axis=0, keepdims=True)
            dpm = (dpp_v * sc_ref[:, cs]).astype(BF16)
            dw_ref[gi] = _dot_tn(pv, dpm)
            dp = _dot_nt(dpm, wb)
            cnt = jnp.minimum(row + 1, win).astype(F32)
            acc, k = dp / cnt, 1
            while k < win:
                acc = acc + jnp.where(row < s - k, pltpu.roll(acc, s - k, 0), 0.0)
                k *= 2
            dz_ref[:, cs] = (acc - dp).astype(BF16)

    full = lambda i: (0, 0)
    return pl.pallas_call(
        body, name="pool_bwd", grid=(1,),
        in_specs=[pl.BlockSpec((s, POOL_W), full), pl.BlockSpec((s, POOL_W), full),
                  pl.BlockSpec((4, 128, 128), lambda i: (0, 0, 0)), pl.BlockSpec((1, POOL_W), full), ANY],
        out_specs=[pl.BlockSpec((s, POOL_W), full), pl.BlockSpec((4, 128, 128), lambda i: (0, 0, 0)),
                   pl.BlockSpec((1, POOL_W), full)],
        out_shape=[jax.ShapeDtypeStruct((s, POOL_W), BF16), jax.ShapeDtypeStruct((4, 128, 128), F32),
                   jax.ShapeDtypeStruct((1, POOL_W), F32)],
        compiler_params=_cp("arbitrary"),
    )(p, dpp, wgrp, scale, after)


def _gla_decay(zgk_ref, wgk_ref, bgk_ref, rb):
    g = _dot(zgk_ref[...], wgk_ref[...].astype(BF16)) + bgk_ref[...]
    la = (jnp.minimum(g, 0.0) - jnp.log(1.0 + jnp.exp(-jnp.abs(g)))) * (1.0 / 16.0)
    rowm = _rows(la.shape) & (CHUNK - 1)
    bc, k = la, 1
    while k < CHUNK:
        bc = bc + jnp.where(rowm >= k, pltpu.roll(bc, k, 0), 0.0)
        k *= 2
    return g, jnp.exp(bc), jnp.exp(-bc)


GLA_HB = 4


def _gla_specs(rb, rmap):
    wk, wv = GLA_HB * HK, GLA_HB * HV
    return [pl.BlockSpec((rb, wk), lambda h, r: (rmap(h, r), OFF_Q // wk + h)),
            pl.BlockSpec((rb, wk), lambda h, r: (rmap(h, r), OFF_K // wk + h)),
            pl.BlockSpec((rb, wv), lambda h, r: (rmap(h, r), OFF_V // wv + h)),
            pl.BlockSpec((rb, 128), lambda h, r: (rmap(h, r), OFF_GK // 128))]


def _gla_fwd(zr, wgk, bgk, ghead, rb):
    s = zr.shape[0]
    nc = rb // CHUNK
    wk, wv = GLA_HB * HK, GLA_HB * HV

    def body(q_ref, k_ref, v_ref, zgk_ref, zog_ref, wgk_ref, bgk_ref, gh_ref, o_ref, og_ref, sp_ref, st_ref):
        @pl.when(pl.program_id(1) == 0)
        def _():
            st_ref[...] = jnp.zeros_like(st_ref)

        _, e_pos, e_neg = _gla_decay(zgk_ref, wgk_ref, bgk_ref, rb)
        lower = _rows((CHUNK, CHUNK)) >= lax.broadcasted_iota(jnp.int32, (CHUNK, CHUNK), 1)
        for c in range(nc):
            sl = slice(c * CHUNK, (c + 1) * CHUNK)
            for hh in range(GLA_HB):
                ck, cv = slice(hh * HK, (hh + 1) * HK), slice(hh * HV, (hh + 1) * HV)
                q = q_ref[sl, ck].astype(F32) * QSCALE
                k = k_ref[sl, ck].astype(F32)
                v = v_ref[sl, cv]
                ec, fc = e_pos[sl, ck], e_neg[sl, ck]
                qfw = (q * ec).astype(BF16)
                kfw_f = k * fc
                s_fw = _dot_nt(qfw, kfw_f.astype(BF16))
                s_bw = _dot_nt((q * fc).astype(BF16), (k * ec).astype(BF16))
                pm = jnp.where(lower, s_fw, s_bw).astype(BF16)
                st = st_ref[hh]
                stb = st.astype(BF16)
                sp_ref[c, hh] = stb
                o = _dot(pm, v) + _dot_nt(qfw, stb)
                e_last = _pick_row(ec, CHUNK - 1)
                kdec = (kfw_f * e_last).astype(BF16)
                st_ref[hh] = st * e_last + _dot_tn(v, kdec)
                r = lax.rsqrt(jnp.mean(o * o, axis=-1, keepdims=True) + EPS)
                zo = zog_ref[sl, cv].astype(F32)
                o_ref[sl, cv] = o.astype(BF16)
                og_ref[sl, cv] = (o * r * gh_ref[...] * zo * _sigmoid(zo)).astype(BF16)

    rmap = lambda h, r: r
    return pl.pallas_call(
        body, name="gla_fwd", grid=(HEADS // GLA_HB, s // rb),
        in_specs=_gla_specs(rb, rmap) + [
            pl.BlockSpec((rb, wv), lambda h, r: (r, OFF_OG // wv + h)),
            pl.BlockSpec((128, wk), lambda h, r: (0, h)), pl.BlockSpec((1, wk), lambda h, r: (0, h)),
            pl.BlockSpec((1, HV), lambda h, r: (0, 0))],
        out_specs=[pl.BlockSpec((rb, wv), lambda h, r: (r, h)), pl.BlockSpec((rb, wv), lambda h, r: (r, h)),
                   pl.BlockSpec((nc, GLA_HB, HV, HK), lambda h, r: (r, h, 0, 0))],
        out_shape=[jax.ShapeDtypeStruct((s, D), BF16), jax.ShapeDtypeStruct((s, D), BF16),
                   jax.ShapeDtypeStruct((s // CHUNK, HEADS, HV, HK), BF16)],
        scratch_shapes=[pltpu.VMEM((GLA_HB, HV, HK), F32)],
        compiler_params=_cp("arbitrary", "arbitrary"),
    )(zr, zr, zr, zr, zr, wgk, bgk, ghead)


def _gla_bwd(zr, do, sp, wgk, bgk, after, rb):
    s = zr.shape[0]
    nc = rb // CHUNK
    nr = s // rb
    wk, wv = GLA_HB * HK, GLA_HB * HV

    def body(q_ref, k_ref, v_ref, zgk_ref, do_ref, sp_ref, wgk_ref, bgk_ref, after_ref, dq_ref, dk_ref, dv_ref, dg_ref,
             gt_ref, dbc_ref):
        @pl.when(pl.program_id(1) == 0)
        def _():
            gt_ref[...] = jnp.zeros_like(gt_ref)

        g, e_pos, e_neg = _gla_decay(zgk_ref, wgk_ref, bgk_ref, rb)
        lower = _rows((CHUNK, CHUNK)) >= lax.broadcasted_iota(jnp.int32, (CHUNK, CHUNK), 1)
        is_last = _rows((CHUNK, HK)) == CHUNK - 1
        for c in reversed(range(nc)):
            sl = slice(c * CHUNK, (c + 1) * CHUNK)
            for hh in range(GLA_HB):
                ck, cv = slice(hh * HK, (hh + 1) * HK), slice(hh * HV, (hh + 1) * HV)
                q = q_ref[sl, ck].astype(F32) * QSCALE
                k = k_ref[sl, ck].astype(F32)
                v = v_ref[sl, cv]
                dov = do_ref[sl, cv]
                ec, fc = e_pos[sl, ck], e_neg[sl, ck]
                qfw_f, kfw_f, qbw_f, kbw_f = q * ec, k * fc, q * fc, k * ec
                qfw, kfw, qbw, kbw = qfw_f.astype(BF16), kfw_f.astype(BF16), qbw_f.astype(BF16), kbw_f.astype(BF16)
                pm = jnp.where(lower, _dot_nt(qfw, kfw), _dot_nt(qbw, kbw)).astype(BF16)
                e_last = _pick_row(ec, CHUNK - 1)
                kdec = (kfw_f * e_last).astype(BF16)
                gt = gt_ref[hh]
                gtb = gt.astype(BF16)
                spv = sp_ref[c, hh]
                dp = _dot_nt(dov, v)
                dv_ref[sl, cv] = (_dot_tn(pm, dov) + _dot_nt(kdec, gtb)).astype(BF16)
                ds_fw = jnp.where(lower, dp, 0.0).astype(BF16)
                ds_bw = jnp.where(lower, 0.0, dp).astype(BF16)
                dqfw = _dot(ds_fw, kfw) + _dot(dov, spv)
                dkfw = _dot_tn(ds_fw, qfw)
                dqbw = _dot(ds_bw, kbw)
                dkbw = _dot_tn(ds_bw, qbw)
                dkdec = _dot(v, gtb)
                de_last = (jnp.sum(gt * spv.astype(F32), axis=0, keepdims=True)
                           + jnp.sum(dkdec * kfw_f, axis=0, keepdims=True))
                dkfw = dkfw + dkdec * e_last
                dq_ref[sl, ck] = ((dqfw * ec + dqbw * fc) * QSCALE).astype(BF16)
                dk_ref[sl, ck] = (dkfw * fc + dkbw * ec).astype(BF16)
                dbc = dqfw * qfw_f - dqbw * qbw_f + dkbw * kbw_f - dkfw * kfw_f
                dbc_ref[sl, ck] = dbc + jnp.where(is_last, de_last * e_last, 0.0)
                gt_ref[hh] = _dot_tn(dov, qfw) + gt * e_last
        rowm = _rows((rb, wk)) & (CHUNK - 1)
        dla, kk = dbc_ref[...], 1
        while kk < CHUNK:
            dla = dla + jnp.where(rowm < CHUNK - kk, pltpu.roll(dla, rb - kk, 0), 0.0)
            kk *= 2
        dg_ref[...] = dla * (1.0 / 16.0) * _sigmoid(-g)

    rmap = lambda h, r: nr - 1 - r
    rev = lambda h, r: (nr - 1 - r, h)
    return pl.pallas_call(
        body, name="gla_bwd", grid=(HEADS // GLA_HB, nr),
        in_specs=_gla_specs(rb, rmap) + [
            pl.BlockSpec((rb, wv), rev),
            pl.BlockSpec((nc, GLA_HB, HV, HK), lambda h, r: (nr - 1 - r, h, 0, 0)),
            pl.BlockSpec((128, wk), lambda h, r: (0, h)), pl.BlockSpec((1, wk), lambda h, r: (0, h)), ANY],
        out_specs=[pl.BlockSpec((rb, wk), rev), pl.BlockSpec((rb, wk), rev), pl.BlockSpec((rb, wv), rev),
                   pl.BlockSpec((rb, wk), rev)],
        out_shape=[jax.ShapeDtypeStruct((s, HEADS * HK), BF16), jax.ShapeDtypeStruct((s, HEADS * HK), BF16),
                   jax.ShapeDtypeStruct((s, D), BF16), jax.ShapeDtypeStruct((s, HEADS * HK), F32)],
        scratch_shapes=[pltpu.VMEM((GLA_HB, HV, HK), F32), pltpu.VMEM((rb, wk), F32)],
        compiler_params=_cp("arbitrary", "arbitrary"),
    )(zr, zr, zr, zr, do, sp, wgk, bgk, after)


def _gk_bwd(dgpre, zr, wgk, after, ts):
    s = zr.shape[0]

    def body(dg_ref, zgk_ref, w_ref, after_ref, dz_ref, dw_ref, db_ref):
        @pl.when(pl.program_id(0) == 0)
        def _():
            dw_ref[...] = jnp.zeros_like(dw_ref)
            db_ref[...] = jnp.zeros_like(db_ref)

        dg = dg_ref[...]
        dgb = dg.astype(BF16)
        dz_ref[...] = _dot_nt(dgb, w_ref[...].astype(BF16)).astype(BF16)
        dw_ref[...] += _dot_tn(zgk_ref[...], dgb)
        db_ref[...] += jnp.sum(dg, axis=0, keepdims=True)

    return pl.pallas_call(
        body, name="gk_bwd", grid=(s // ts,),
        in_specs=[pl.BlockSpec((ts, 512), lambda i: (i, 0)), pl.BlockSpec((ts, 128), lambda i: (i, OFF_GK // 128)),
                  pl.BlockSpec((128, 512), lambda i: (0, 0)), ANY],
        out_specs=[pl.BlockSpec((ts, 128), lambda i: (i, 0)), pl.BlockSpec((128, 512), lambda i: (0, 0)),
                   pl.BlockSpec((1, 512), lambda i: (0, 0))],
        out_shape=[jax.ShapeDtypeStruct((s, 128), BF16), jax.ShapeDtypeStruct((128, 512), F32),
                   jax.ShapeDtypeStruct((1, 512), F32)],
        compiler_params=_cp("arbitrary"),
    )(dgpre, zr, wgk, after)


def _merge_fwd(x, zr, pp, og, bgate, wpp, wgla, wout, gffn, after, ts):
    s = x.shape[0]

    def body(x_ref, z0_ref, z1_ref, pp_ref, og_ref, bg_ref, wpp_ref, wgla_ref, wout_ref, gf_ref, after_ref,
             x1_ref, mix_ref, yp_ref, yg_ref, h2_ref):
        ppv = pp_ref[...]
        yp = jnp.concatenate([_dot(ppv, wpp_ref[j]) for j in range(4)], axis=1)
        yg = _dot(og_ref[...], wgla_ref[...])
        g0 = _sigmoid(z0_ref[...].astype(F32) + bg_ref[:, :D])
        g1 = _sigmoid(z1_ref[...].astype(F32) + bg_ref[:, D:])
        mixed = (g0 * yp + g1 * yg).astype(BF16)
        x1 = x_ref[...] + _dot(mixed, wout_ref[...])
        x1_ref[...] = x1
        mix_ref[...] = mixed
        yp_ref[...] = yp.astype(BF16)
        yg_ref[...] = yg.astype(BF16)
        r = lax.rsqrt(jnp.mean(x1 * x1, axis=-1, keepdims=True) + EPS)
        h2_ref[...] = (x1 * r * gf_ref[...]).astype(BF16)

    row = lambda i: (i, 0)
    const2 = lambda i: (0, 0)
    return pl.pallas_call(
        body, name="merge_fwd", grid=(s // ts,),
        in_specs=[pl.BlockSpec((ts, D), row), pl.BlockSpec((ts, D), lambda i: (i, 0)), pl.BlockSpec((ts, D), lambda i: (i, 1)),
                  pl.BlockSpec((ts, POOL_W), row), pl.BlockSpec((ts, D), row), pl.BlockSpec((1, 2 * D), const2),
                  pl.BlockSpec((4, POOL_W, 256), lambda i: (0, 0, 0)), pl.BlockSpec((D, D), const2),
                  pl.BlockSpec((D, D), const2), pl.BlockSpec((1, D), const2), ANY],
        out_specs=[pl.BlockSpec((ts, D), row)] * 5,
        out_shape=[jax.ShapeDtypeStruct((s, D), F32)] + [jax.ShapeDtypeStruct((s, D), BF16)] * 4,
        compiler_params=_cp("arbitrary"),
    )(x, zr, zr, pp, og, bgate, wpp, wgla, wout, gffn, after)


def _merge_bwd(dx1b, zr, yp, yg, o, bgate, ghead, wpp, wgla, wout, after, ts):
    s = dx1b.shape[0]

    def body(dx_ref, z0_ref, z1_ref, zog_ref, yp_ref, yg_ref, o_ref, bg_ref, gh_ref, wpp_ref, wgla_ref, wout_ref, after_ref,
             dzg_ref, dyp_ref, dyg_ref, dpp_ref, do_ref, dzog_ref, dbg_ref, dgh_ref):
        @pl.when(pl.program_id(0) == 0)
        def _():
            dbg_ref[...] = jnp.zeros_like(dbg_ref)
            dgh_ref[...] = jnp.zeros_like(dgh_ref)

        dmix = _dot_nt(dx_ref[...], wout_ref[...])
        g0 = _sigmoid(z0_ref[...].astype(F32) + bg_ref[:, :D])
        g1 = _sigmoid(z1_ref[...].astype(F32) + bg_ref[:, D:])
        dypb = (dmix * g0).astype(BF16)
        dygb = (dmix * g1).astype(BF16)
        dz0 = dmix * yp_ref[...].astype(F32) * g0 * (1.0 - g0)
        dz1 = dmix * yg_ref[...].astype(F32) * g1 * (1.0 - g1)
        dzg_ref[:, :D] = dz0.astype(BF16)
        dzg_ref[:, D:] = dz1.astype(BF16)
        dbg_ref[:, :D] += jnp.sum(dz0, axis=0, keepdims=True)
        dbg_ref[:, D:] += jnp.sum(dz1, axis=0, keepdims=True)
        dyp_ref[...] = dypb
        dyg_ref[...] = dygb
        dpp = _dot_nt(dypb[:, 0:256], wpp_ref[0])
        for j in range(1, 4):
            dpp = dpp + _dot_nt(dypb[:, j * 256:(j + 1) * 256], wpp_ref[j])
        dpp_ref[...] = dpp.astype(BF16)
        dog = _dot_nt(dygb, wgla_ref[...])
        gh = gh_ref[...]
        dgh = jnp.zeros((1, HV), F32)
        for h in range(HEADS):
            cs = slice(h * HV, (h + 1) * HV)
            ov = o_ref[:, cs].astype(F32)
            r = lax.rsqrt(jnp.mean(ov * ov, axis=-1, keepdims=True) + EPS)
            oh = ov * r
            zo = zog_ref[:, cs].astype(F32)
            sg = _sigmoid(zo)
            dog_h = dog[:, cs]
            don = dog_h * zo * sg
            dzog_ref[:, cs] = (dog_h * oh * gh * sg * (1.0 + zo * (1.0 - sg))).astype(BF16)
            dgh = dgh + jnp.sum(don * oh, axis=0, keepdims=True)
            doh = don * gh
            do_ref[:, cs] = (r * (doh - oh * jnp.mean(doh * oh, axis=-1, keepdims=True))).astype(BF16)
        dgh_ref[...] += dgh

    row = lambda i: (i, 0)
    const2 = lambda i: (0, 0)
    return pl.pallas_call(
        body, name="merge_bwd", grid=(s // ts,),
        in_specs=[pl.BlockSpec((ts, D), row), pl.BlockSpec((ts, D), lambda i: (i, 0)), pl.BlockSpec((ts, D), lambda i: (i, 1)),
                  pl.BlockSpec((ts, D), lambda i: (i, OFF_OG // D)), pl.BlockSpec((ts, D), row), pl.BlockSpec((ts, D), row),
                  pl.BlockSpec((ts, D), row), pl.BlockSpec((1, 2 * D), const2), pl.BlockSpec((1, HV), const2),
                  pl.BlockSpec((4, POOL_W, 256), lambda i: (0, 0, 0)), pl.BlockSpec((D, D), const2),
                  pl.BlockSpec((D, D), const2), ANY],
        out_specs=[pl.BlockSpec((ts, 2 * D), row), pl.BlockSpec((ts, D), row), pl.BlockSpec((ts, D), row),
                   pl.BlockSpec((ts, POOL_W), row), pl.BlockSpec((ts, D), row), pl.BlockSpec((ts, D), row),
                   pl.BlockSpec((1, 2 * D), const2), pl.BlockSpec((1, HV), const2)],
        out_shape=[jax.ShapeDtypeStruct((s, 2 * D), BF16), jax.ShapeDtypeStruct((s, D), BF16),
                   jax.ShapeDtypeStruct((s, D), BF16), jax.ShapeDtypeStruct((s, POOL_W), BF16),
                   jax.ShapeDtypeStruct((s, D), BF16), jax.ShapeDtypeStruct((s, D), BF16),
                   jax.ShapeDtypeStruct((1, 2 * D), F32), jax.ShapeDtypeStruct((1, HV), F32)],
        compiler_params=_cp("arbitrary"),
    )(dx1b, zr, zr, zr, yp, yg, o, bgate, ghead, wpp, wgla, wout, after)


HALO = 16
CCH = 1408


def _conv_taps(u_ref, halo_ref, cs, first, ts):
    u = u_ref[:, cs].astype(F32)
    hal = halo_ref[:, cs].astype(F32)
    h1 = jnp.where(first, 0.0, _pick_row(hal, HALO - 1))
    h2 = jnp.where(first, 0.0, _pick_row(hal, HALO - 2))
    row8 = _rows((8, u.shape[1]))
    r1, r2 = pltpu.roll(u, 1, 0), pltpu.roll(u, 2, 0)
    r1 = jnp.concatenate([jnp.where(row8 == 0, h1, r1[:8]), r1[8:]], axis=0)
    r2 = jnp.concatenate([jnp.where(row8 == 0, h2, jnp.where(row8 == 1, h1, r2[:8])), r2[8:]], axis=0)
    return u, r1, r2


def _ffn_down_loss(u, x1, tgt, wconv, bconv, wdown, gfin, ts):
    s = x1.shape[0]

    def body(u_ref, halo_ref, x1_ref, t_ref, wc_ref, bc_ref, wd_ref, gf_ref, a_ref, c_ref, dx_ref, dxb_ref, ls_ref,
             dgf_ref):
        i = pl.program_id(0)

        @pl.when(i == 0)
        def _():
            ls_ref[...] = jnp.zeros_like(ls_ref)
            dgf_ref[...] = jnp.zeros_like(dgf_ref)

        first = i == 0
        acc = x1_ref[...]
        for hf in range(D_FF // CCH):
            cg = slice(hf * CCH, (hf + 1) * CCH)
            cv = slice(D_FF + hf * CCH, D_FF + (hf + 1) * CCH)
            vals = []
            for cs in (cg, cv):
                u0, u1, u2 = _conv_taps(u_ref, halo_ref, cs, first, ts)
                vals.append(bc_ref[:, cs] + wc_ref[0:1, cs] * u2 + wc_ref[1:2, cs] * u1 + wc_ref[2:3, cs] * u0)
                c_ref[:, cs] = vals[-1].astype(BF16)
            a = (vals[0] * _sigmoid(vals[0]) * vals[1]).astype(BF16)
            a_ref[:, cg] = a
            acc = acc + _dot(a, wd_ref[cg, :])
        r = lax.rsqrt(jnp.mean(acc * acc, axis=-1, keepdims=True) + EPS)
        xh = acc * r
        gf = gf_ref[...]
        err = xh * gf - t_ref[...]
        ls_ref[...] += (0.5 / D) * jnp.sum(jnp.sum(err * err, axis=-1, keepdims=True), axis=0, keepdims=True)
        dy = err * (1.0 / D)
        dgf_ref[...] += jnp.sum(dy * xh, axis=0, keepdims=True)
        dxh = dy * gf
        dx = r * (dxh - xh * jnp.mean(dxh * xh, axis=-1, keepdims=True))
        dx_ref[...] = dx
        dxb_ref[...] = dx.astype(BF16)

    row = lambda i: (i, 0)
    const2 = lambda i: (0, 0)
    return pl.pallas_call(
        body, name="ffn_down_loss", grid=(s // ts,),
        in_specs=[pl.BlockSpec((ts, N_UP), row),
                  pl.BlockSpec((HALO, N_UP), lambda i: (jnp.maximum(i * (ts // HALO) - 1, 0), 0)),
                  pl.BlockSpec((ts, D), row), pl.BlockSpec((ts, D), row), pl.BlockSpec((3, N_UP), const2),
                  pl.BlockSpec((1, N_UP), const2), pl.BlockSpec((D_FF, D), const2), pl.BlockSpec((1, D), const2)],
        out_specs=[pl.BlockSpec((ts, D_FF), row), pl.BlockSpec((ts, N_UP), row), pl.BlockSpec((ts, D), row),
                   pl.BlockSpec((ts, D), row), pl.BlockSpec((1, 128), const2), pl.BlockSpec((1, D), const2)],
        out_shape=[jax.ShapeDtypeStruct((s, D_FF), BF16), jax.ShapeDtypeStruct((s, N_UP), BF16),
                   jax.ShapeDtypeStruct((s, D), F32), jax.ShapeDtypeStruct((s, D), BF16),
                   jax.ShapeDtypeStruct((1, 128), F32), jax.ShapeDtypeStruct((1, D), F32)],
        compiler_params=_cp("arbitrary"),
    )(u, u, x1, tgt, wconv, bconv, wdown, gfin)


def _ffn_bwd(dx2b, u, c, wconv, wdown, ts):
    s = dx2b.shape[0]
    nt = s // ts

    def body(dx_ref, u_ref, c_ref, wc_ref, wd_ref, du_ref, db_ref, dw_ref, nxt_ref):
        @pl.when(pl.program_id(0) == 0)
        def _():
            db_ref[...] = jnp.zeros_like(db_ref)
            dw_ref[...] = jnp.zeros_like(dw_ref)
            nxt_ref[...] = jnp.zeros_like(nxt_ref)

        dxv = dx_ref[...]
        row8 = _rows((8, CCH))
        for hf in range(D_FF // CCH):
            cg = slice(hf * CCH, (hf + 1) * CCH)
            cv = slice(D_FF + hf * CCH, D_FF + (hf + 1) * CCH)
            da = _dot_nt(dxv, wd_ref[cg, :])
            gate = c_ref[:, cg].astype(F32)
            val = c_ref[:, cv].astype(F32)
            sg = _sigmoid(gate)
            dcs = (da * val * sg * (1.0 + gate * (1.0 - sg)), da * gate * sg)
            for cs, dc in zip((cg, cv), dcs):
                n1 = nxt_ref[0:1, cs]
                n2 = nxt_ref[1:2, cs]
                r1, r2 = pltpu.roll(dc, ts - 1, 0), pltpu.roll(dc, ts - 2, 0)
                f1 = jnp.concatenate([r1[:ts - 8], jnp.where(row8 == 7, n1, r1[ts - 8:])], axis=0)
                f2 = jnp.concatenate([r2[:ts - 8], jnp.where(row8 == 7, n2, jnp.where(row8 == 6, n1, r2[ts - 8:]))], axis=0)
                uv = u_ref[:, cs].astype(F32)
                db_ref[:, cs] += jnp.sum(dc, axis=0, keepdims=True)
                dw_ref[0:1, cs] += jnp.sum(f2 * uv, axis=0, keepdims=True)
                dw_ref[1:2, cs] += jnp.sum(f1 * uv, axis=0, keepdims=True)
                dw_ref[2:3, cs] += jnp.sum(dc * uv, axis=0, keepdims=True)
                du_ref[:, cs] = (wc_ref[2:3, cs] * dc + wc_ref[1:2, cs] * f1 + wc_ref[0:1, cs] * f2).astype(BF16)
                nxt_ref[:, cs] = dc[0:8, :]

    rev = lambda i: (nt - 1 - i, 0)
    const2 = lambda i: (0, 0)
    return pl.pallas_call(
        body, name="ffn_bwd", grid=(nt,),
        in_specs=[pl.BlockSpec((ts, D), rev), pl.BlockSpec((ts, N_UP), rev), pl.BlockSpec((ts, N_UP), rev),
                  pl.BlockSpec((3, N_UP), const2), pl.BlockSpec((D_FF, D), const2)],
        out_specs=[pl.BlockSpec((ts, N_UP), rev), pl.BlockSpec((1, N_UP), const2), pl.BlockSpec((3, N_UP), const2)],
        out_shape=[jax.ShapeDtypeStruct((s, N_UP), BF16), jax.ShapeDtypeStruct((1, N_UP), F32),
                   jax.ShapeDtypeStruct((3, N_UP), F32)],
        scratch_shapes=[pltpu.VMEM((8, N_UP), F32)],
        compiler_params=_cp("arbitrary"),
    )(dx2b, u, c, wconv, wdown)


ANY = pl.BlockSpec(memory_space=pl.ANY)


def _place():
    x, y, c = lax.axis_index("x"), lax.axis_index("y"), lax.axis_index("c")
    chips = [(1 - x, y), (x, 1 - y), (1 - x, 1 - y)]
    return x, y, c, chips


def _half(shape, c, axis):
    size = shape[axis] // 2
    cut = pl.ds(pl.multiple_of(c * size, 8 if axis == 0 else 128), size)
    return (cut, slice(None)) if axis == 0 else (slice(None), cut)


def _half_shape(shape, axis):
    return (shape[0] // 2, shape[1]) if axis == 0 else (shape[0], shape[1] // 2)


def _remote(src, dst, send_sems, recv_sems, k, to):
    return pltpu.make_async_remote_copy(src_ref=src, dst_ref=dst, send_sem=send_sems.at[k], recv_sem=recv_sems.at[k],
                                        device_id=to, device_id_type=MESH)


def _sibling_exchange(grads, axes, smalls, name):
    nb = len(grads)
    n = nb + len(smalls)

    def body(*refs):
        ins, outs = refs[:n], refs[n:2 * n]
        send_sems, recv_sems = refs[2 * n:]
        x, y, c, _ = _place()
        sib = (x, y, 1 - c)
        cps = []
        for a in range(nb):
            theirs = _half(grads[a].shape[1:], 1 - c, axes[a])
            cps.append(_remote(ins[a].at[(slice(None),) + theirs], outs[a], send_sems, recv_sems, a, sib))
        for a in range(nb, n):
            cps.append(_remote(ins[a], outs[a], send_sems, recv_sems, a, sib))
        for cp in cps:
            cp.start()
        for cp in cps:
            cp.wait()

    out_shape = [jax.ShapeDtypeStruct((4,) + _half_shape(g.shape[1:], ax), g.dtype) for g, ax in zip(grads, axes)]
    out_shape += [jax.ShapeDtypeStruct(a.shape, F32) for a in smalls]
    return pl.pallas_call(
        body, name=name, in_specs=[ANY] * n, out_specs=[ANY] * n, out_shape=out_shape,
        scratch_shapes=[pltpu.SemaphoreType.DMA((n,)), pltpu.SemaphoreType.DMA((n,))],
        compiler_params=pltpu.CompilerParams(has_side_effects=True),
    )(*grads, *smalls)


def _gather_share(lands, axes, name):
    n = len(lands)

    def body(*refs):
        outs = refs[n:2 * n]
        send_sems, recv_sems = refs[2 * n:]
        x, y, c, chips = _place()
        sib = (x, y, 1 - c)
        cps = []
        for a in range(n):
            mine = _half(lands[a].shape[1:], c, axes[a])
            for k, ch in enumerate(chips):
                landed = outs[a].at[(2 * ch[0] + ch[1],) + mine]
                cps.append(_remote(landed, landed, send_sems, recv_sems, 3 * a + k, sib))
        for cp in cps:
            cp.start()
        for a in range(n):
            other = _half(lands[a].shape[1:], 1 - c, axes[a])
            for k, ch in enumerate(chips):
                landed = outs[a].at[(2 * ch[0] + ch[1],) + other]
                _remote(landed, landed, send_sems, recv_sems, 3 * a + k, sib).wait_recv()
        for cp in cps:
            cp.wait_send()

    return pl.pallas_call(
        body, name=name, in_specs=[ANY] * n, out_specs=[ANY] * n,
        out_shape=[jax.ShapeDtypeStruct(a.shape, a.dtype) for a in lands],
        input_output_aliases={a: a for a in range(n)},
        scratch_shapes=[pltpu.SemaphoreType.DMA((3 * n,)), pltpu.SemaphoreType.DMA((3 * n,))],
        compiler_params=pltpu.CompilerParams(has_side_effects=True),
    )(*lands)


def _sibling_share(halves, name):
    n = len(halves)

    def body(*refs):
        ins, outs = refs[:n], refs[n:2 * n]
        send_sems, recv_sems = refs[2 * n:]
        x, y, c, _ = _place()
        cps = [_remote(ins[a], outs[a], send_sems, recv_sems, a, (x, y, 1 - c)) for a in range(n)]
        for cp in cps:
            cp.start()
        for cp in cps:
            cp.wait()

    return pl.pallas_call(
        body, name=name, in_specs=[ANY] * n, out_specs=[ANY] * n,
        out_shape=[jax.ShapeDtypeStruct(h.shape, F32) for h in halves],
        scratch_shapes=[pltpu.SemaphoreType.DMA((n,)), pltpu.SemaphoreType.DMA((n,))],
        compiler_params=pltpu.CompilerParams(has_side_effects=True),
    )(*halves)


HBM = pl.BlockSpec(memory_space=pltpu.HBM)
SEM = pl.BlockSpec(memory_space=pltpu.SEMAPHORE)
DATAFLOW = pltpu.SideEffectType.DATAFLOW_SIDE_EFFECTING


def _split_start(name, srcs, land_shapes, plan, n_copies, after):
    lands = [lax.empty(*ls) if isinstance(ls, tuple) else ls for ls in land_shapes]
    bufs = list(srcs) + lands
    nb, ns = len(bufs), len(srcs)

    def body(*refs):
        send_sems, recv_sems, token = refs[nb + 1], refs[nb + 2], refs[-1]
        for k, (src, dst, to) in enumerate(plan(refs[:ns], refs[ns:nb])):
            _remote(src, dst, send_sems, recv_sems, k, to).start()
        token[...] = jnp.zeros_like(token)

    res = pl.pallas_call(
        body, name=name,
        out_shape=(pltpu.SemaphoreType.DMA((n_copies,)), pltpu.SemaphoreType.DMA((n_copies,)),
                   *[pltpu.HBM(b.shape, b.dtype) for b in bufs], jax.ShapeDtypeStruct((8, 128), F32)),
        in_specs=[HBM] * nb + [ANY],
        out_specs=(SEM, SEM, *[HBM] * nb, pl.BlockSpec(memory_space=pltpu.VMEM)),
        input_output_aliases={i: 2 + i for i in range(nb)},
        compiler_params=pltpu.CompilerParams(has_side_effects=DATAFLOW),
    )(*[pltpu.with_memory_space_constraint(b, pltpu.HBM) for b in bufs], after)
    return (res[0], res[1], list(res[2:2 + nb])), res[-1]


def _split_wait(name, handle, n_srcs, plan, after):
    send_sems, recv_sems, bufs = handle
    nb = len(bufs)

    def body(*refs):
        sends, recvs = refs[nb], refs[nb + 1]
        for k, (src, dst, to) in enumerate(plan(refs[:n_srcs], refs[n_srcs:nb])):
            cp = _remote(src, dst, sends, recvs, k, to)
            cp.wait_send()
            cp.wait_recv()

    res = pl.pallas_call(
        body, name=name, out_shape=[pltpu.HBM(b.shape, b.dtype) for b in bufs],
        in_specs=[HBM] * nb + [SEM, SEM, ANY], out_specs=[HBM] * nb,
        input_output_aliases={i: i for i in range(nb)},
        compiler_params=pltpu.CompilerParams(has_side_effects=DATAFLOW),
    )(*bufs, send_sems, recv_sems, after)
    return list(res[:n_srcs]), list(res[n_srcs:])


def _gather_plan(shapes, axes, n_whole=0):
    def plan(srcs, lands):
        x, y, c, chips = _place()
        out = []
        for a, (shape, axis) in enumerate(zip(shapes, axes)):
            mine = _half(shape, c, axis)
            for ch in chips:
                out.append((srcs[a].at[mine], lands[a].at[(2 * x + y,) + mine], (ch[0], ch[1], c)))
        for a in range(len(shapes), len(shapes) + n_whole):
            for ch in chips:
                out.append((srcs[a], lands[a].at[2 * x + y], (ch[0], ch[1], c)))
        return out
    return plan


def _share_plan(shapes, axes):
    def plan(srcs, lands):
        x, y, c, chips = _place()
        out = []
        for a, (shape, axis) in enumerate(zip(shapes, axes)):
            mine = _half(shape, c, axis)
            for ch in chips:
                landed = lands[a].at[(2 * ch[0] + ch[1],) + mine]
                out.append((landed, landed, (x, y, 1 - c)))
        return out
    return plan


def _sibling_plan(shapes, axes):
    def plan(srcs, lands):
        x, y, c, _ = _place()
        return [(srcs[a].at[(slice(None),) + _half(shape, 1 - c, axis)], lands[a], (x, y, 1 - c))
                for a, (shape, axis) in enumerate(zip(shapes, axes))]
    return plan


def _reduce_plan(n_big, n_small):
    def plan(srcs, lands):
        x, y, c, chips = _place()
        out = []
        for a in range(n_big):
            for k, ch in enumerate(chips):
                out.append((srcs[a].at[2 * ch[0] + ch[1]], lands[a].at[k], (ch[0], ch[1], c)))
        for a in range(n_big, n_big + n_small):
            for ch in chips:
                out.append((srcs[a], lands[a].at[2 * x + y], (ch[0], ch[1], c)))
        return out
    return plan


def _row_tile(rows, cols, mult):
    best = mult
    for t in range(mult, rows + 1, mult):
        if rows % t == 0 and t * cols * 4 <= (2 << 20):
            best = t
    return best if rows % best == 0 else rows


COL_TILE = 256


def _half_tiling(hshape, axis, mult):
    hr, hc = hshape
    if axis == 0:
        tr = _row_tile(hr, hc, mult)
        return tr, hc, hr // tr
    return hr, COL_TILE, hc // COL_TILE


def _tile_idx(axis, t):
    return (t, 0) if axis == 0 else (0, t)


def _chip_partial(place, g, t, axis, name):
    hshape = t.shape[1:]
    br, bc, nt = _half_tiling(hshape, axis, 16)

    def body(pl_ref, g_ref, t_ref, pf_ref, pb_ref):
        v = g_ref[...].astype(F32) + t_ref[...].astype(F32)
        pb_ref[...] = v.astype(BF16)

        @pl.when(pl.program_id(1) == pl_ref[0])
        def _():
            pf_ref[...] = v

    blk = (None, br, bc)
    return pl.pallas_call(
        body, name=name,
        grid_spec=pltpu.PrefetchScalarGridSpec(
            num_scalar_prefetch=1, grid=(nt, 4),
            in_specs=[pl.BlockSpec(blk, lambda i, j, p: (j,) + _tile_idx(axis, p[1] * nt + i)),
                      pl.BlockSpec(blk, lambda i, j, p: (j,) + _tile_idx(axis, i))],
            out_specs=[pl.BlockSpec((br, bc), lambda i, j, p: _tile_idx(axis, i)),
                       pl.BlockSpec(blk, lambda i, j, p: (j,) + _tile_idx(axis, i))]),
        out_shape=[jax.ShapeDtypeStruct(hshape, F32), jax.ShapeDtypeStruct((4,) + hshape, BF16)],
        compiler_params=_cp("arbitrary", "arbitrary"),
    )(place, g, t)


def _finish_half(pf, rb, axis, name):
    hshape = pf.shape
    br, bc, nt = _half_tiling(hshape, axis, 16)

    def body(pf_ref, rb_ref, o_ref):
        o_ref[...] = ((pf_ref[...] + rb_ref[0].astype(F32)) + rb_ref[1].astype(F32)) + rb_ref[2].astype(F32)

    return pl.pallas_call(
        body, name=name, grid=(nt,),
        in_specs=[pl.BlockSpec((br, bc), lambda i: _tile_idx(axis, i)),
                  pl.BlockSpec((3, br, bc), lambda i: (0,) + _tile_idx(axis, i))],
        out_specs=pl.BlockSpec((br, bc), lambda i: _tile_idx(axis, i)),
        out_shape=jax.ShapeDtypeStruct(hshape, F32),
        compiler_params=_cp("arbitrary"),
    )(pf, rb)


def _adam_math(w, g, m, v):
    m = ADAM_B1 * m + (1.0 - ADAM_B1) * g
    v = ADAM_B2 * v + (1.0 - ADAM_B2) * (g * g)
    m_hat = m / (1.0 - ADAM_B1 ** ADAM_STEP)
    v_hat = v / (1.0 - ADAM_B2 ** ADAM_STEP)
    return -ADAM_LR * (m_hat / (jnp.sqrt(v_hat) + ADAM_EPS) + ADAM_WD * w), m, v


def _adam_halves(place, w, mine, theirs, m, v, axis, name):
    br, bc, nt = _half_tiling(mine.shape, axis, 8)

    def body(pl_ref, w_ref, a_ref, b_ref, m_ref, v_ref, g_ref, d_ref, mo_ref, vo_ref):
        is_mine = pl.program_id(0) // nt == pl_ref[1]
        g = jnp.where(is_mine, a_ref[...], b_ref[...])
        d, mn, vn = _adam_math(w_ref[...], g, m_ref[...], v_ref[...])
        g_ref[...] = g
        d_ref[...] = d
        mo_ref[...] = mn
        vo_ref[...] = vn

    full = pl.BlockSpec((br, bc), lambda i, p: _tile_idx(axis, i))
    mine_spec = pl.BlockSpec((br, bc), lambda i, p: _tile_idx(axis, jnp.where(i // nt == p[1], i % nt, nt - 1)))
    theirs_spec = pl.BlockSpec((br, bc), lambda i, p: _tile_idx(axis, jnp.where(i // nt == p[1], 0, i % nt)))
    return pl.pallas_call(
        body, name=name,
        grid_spec=pltpu.PrefetchScalarGridSpec(
            num_scalar_prefetch=1, grid=(2 * nt,), in_specs=[full, mine_spec, theirs_spec, full, full],
            out_specs=[full] * 4),
        out_shape=[jax.ShapeDtypeStruct(w.shape, F32)] * 4, compiler_params=_cp("arbitrary"),
    )(place, w, mine, theirs, m, v)


def _add_many(xs, ys, name):
    n = len(xs)

    def body(*refs):
        for i in range(n):
            refs[2 * n + i][...] = refs[i][...] + refs[n + i][...]

    return pl.pallas_call(body, name=name, out_shape=[jax.ShapeDtypeStruct(a.shape, F32) for a in xs])(*xs, *ys)


def _adam_small(place, owns, landed, ws, ms, vs, widths):
    n, nw = len(owns), len(ws)

    def body(pl_ref, *refs):
        own_r, land_r = refs[:n], refs[n:2 * n]
        w_r, m_r, v_r = (refs[2 * n + k * nw:2 * n + (k + 1) * nw] for k in range(3))
        outs = refs[2 * n + 3 * nw:]
        g_o, d_o, m_o, v_o = outs[:n], outs[n:n + nw], outs[n + nw:n + 2 * nw], outs[n + 2 * nw:]
        for me in range(4):
            @pl.when(pl_ref[0] == me)
            def _(me=me):
                for i in range(n):
                    p = [own_r[i][...] if k == me else land_r[i][k] for k in range(4)]
                    g = ((p[0] + p[1]) + p[2]) + p[3]
                    if i < nw and widths[i]:
                        g = g[:, me * widths[i]:(me + 1) * widths[i]]
                    g_o[i][...] = g
                    if i < nw:
                        d, mn, vn = _adam_math(w_r[i][...], g, m_r[i][...], v_r[i][...])
                        d_o[i][...] = d
                        m_o[i][...] = mn
                        v_o[i][...] = vn

    g_shapes = [jax.ShapeDtypeStruct(ws[i].shape if i < nw else owns[i].shape, F32) for i in range(n)]
    w_shapes = [jax.ShapeDtypeStruct(w.shape, F32) for w in ws]
    whole = lambda a: pl.BlockSpec(a.shape, lambda i, p, nd=len(a.shape): (0,) * nd)
    ins = list(owns) + list(landed) + list(ws) + list(ms) + list(vs)
    out_shape = g_shapes + w_shapes * 3
    out = pl.pallas_call(
        body, name="adam_small",
        grid_spec=pltpu.PrefetchScalarGridSpec(num_scalar_prefetch=1, grid=(1,), in_specs=[whole(a) for a in ins],
                                               out_specs=[whole(a) for a in out_shape]),
        out_shape=out_shape, compiler_params=_cp("arbitrary"),
    )(place, *ins)
    return out[:n], out[n:n + nw], out[n + nw:n + 2 * nw], out[n + 2 * nw:]


def kernel(x, g_mix, w_in, b_gate, w_gk_up, b_gk, w_pool_grp, pool_scale, g_gla_head, w_pool_proj, w_gla_proj, w_out, g_ffn, w_up, w_conv, b_conv, w_down, g_final, loss_target, m_g_mix, m_w_in, m_b_gate, m_w_gk_up, m_b_gk, m_w_pool_grp, m_pool_scale, m_g_gla_head, m_w_pool_proj, m_w_gla_proj, m_w_out, m_g_ffn, m_w_up, m_w_conv, m_b_conv, m_w_down, m_g_final, v_g_mix, v_w_in, v_b_gate, v_w_gk_up, v_b_gk, v_w_pool_grp, v_pool_scale, v_g_gla_head, v_w_pool_proj, v_w_gla_proj, v_w_out, v_g_ffn, v_w_up, v_w_conv, v_b_conv, v_w_down, v_g_final):
    s = x.shape[1]
    ts = min(s, 512)
    tm = min(s, 256)
    cx, cy, cc = lax.axis_index("x"), lax.axis_index("y"), lax.axis_index("c")
    chip = 2 * cx + cy
    place = jnp.stack([chip, cc]).astype(jnp.int32)

    big_names = ("w_in", "w_pool_proj", "w_gla_proj", "w_out", "w_up", "w_down")
    axes = (1, 0, 0, 0, 0, 0)
    shards = dict(w_in=jnp.transpose(w_in[0]), w_pool_proj=w_pool_proj[0], w_gla_proj=w_gla_proj[0], w_out=w_out[0],
                  w_up=w_up[0], w_down=w_down[0])
    def fill_own(lands, mine):
        return [lax.dynamic_update_slice(g, o_[None], (chip, 0, 0)) for g, o_ in zip(lands, mine)]

    def gather_start(tag, halves, group_axes, whole, after):
        plan = _gather_plan([o_.shape for o_ in halves], group_axes, len(whole))
        srcs = list(halves) + list(whole)
        handle, token = _split_start("gather_" + tag + "_start", srcs, [((4,) + o_.shape, o_.dtype) for o_ in srcs], plan,
                                     3 * len(srcs), after)
        return (handle, plan, len(halves), len(srcs), group_axes), token

    def gather_finish(tag, started, after):
        handle, plan, n_halves, n, group_axes = started
        mine, lands = _split_wait("gather_" + tag + "_wait", handle, n, plan, after)
        lands[:n_halves] = _gather_share(lands[:n_halves], group_axes, "gather_" + tag + "_share")
        return fill_own(lands, mine)

    in_w, tok = gather_start("in", [shards["w_in"].astype(BF16)], axes[:1], [], g_mix)
    zero = tok[0, 0]
    own = [(shards[n] + zero).astype(BF16) for n in big_names[1:]]
    mix_w, tok = gather_start("mix", own[0:3], axes[1:4], [w_gk_up[0] + zero, w_conv[0] + zero], tok)
    up_w, tok = gather_start("up", own[3:4], axes[4:5], [], tok)
    down_w, tok = gather_start("down", own[4:5], axes[5:6], [], tok)

    def forward_start(tag, started, after):
        handle, plan, _, n, group_axes = started
        mine, lands = _split_wait("gather_" + tag + "_wait", handle, n, plan, after)
        plan = _share_plan([o_.shape for o_ in mine], group_axes)
        share, token = _split_start("gather_" + tag + "_share_start", [], lands, plan, 3 * n, after)
        return (share, plan, mine), token

    def forward_done(tag, forwarded, after):
        share, plan, mine = forwarded
        return fill_own(_split_wait("gather_" + tag + "_share_wait", share, 0, plan, after)[1], mine)
    xs, tgt = x[0], loss_target[0]
    wgrp = w_pool_grp[0]
    h = _rmsnorm(xs, g_mix, tok, "norm_mix", ts)
    m_in_t, v_in_t = jnp.transpose(m_w_in[0]), jnp.transpose(v_w_in[0])
    h, m_in_t, v_in_t = lax.optimization_barrier((h, m_in_t, v_in_t))
    w_in_t = gather_finish("in", in_w, h)[0].reshape(N_IN, D)
    nsh = N_IN // 4

    zr = _in_proj(h, w_in_t, 1152)
    p, pp = _pool_fwd(zr, wgrp, pool_scale)
    wpp, wgla, wout, wgk4, wconv4 = gather_finish("mix", mix_w, pp)
    wgla, wout = wgla.reshape(D, D), wout.reshape(D, D)
    wgk_full = jnp.transpose(wgk4, (1, 0, 2)).reshape(GATE_RANK, 512)
    wconv_full = jnp.transpose(wconv4, (1, 0, 2)).reshape(3, N_UP)
    wgk_pad = jnp.concatenate([wgk_full, jnp.zeros((128 - GATE_RANK, 512), F32)], axis=0)
    o, og, sp = _gla_fwd(zr, wgk_pad, b_gk, g_gla_head, ts)
    up_f, tok = forward_start("up", up_w, og)
    x1, mixed, yp, yg, h2 = _merge_fwd(xs, zr, pp, og, b_gate, wpp, wgla, wout, g_ffn, tok, ts)
    wup, = forward_done("up", up_f, x1)
    down_f, tok = forward_start("down", down_w, x1)
    u = _matmul_resident(h2, wup, tok, "ffn_up")
    wdown = forward_done("down", down_f, u)[0].reshape(D_FF, D)
    a, conv_out, dx2, dx2b, loss_part, dgfin = _ffn_down_loss(u, x1, tgt, wconv_full, b_conv, wdown,
                                                              g_final.reshape(1, D), tm)

    du, dbconv, dwconv = _ffn_bwd(dx2b, u, conv_out, wconv_full, wdown, tm)
    dw_down = _matmul_tn(a, dx2b, "dw_down", D, tm=1408)
    dw_up = _matmul_tn(h2, du, "dw_up", 1408, shard_major=True)

    def exchange_start(tag, grads, group_axes, after):
        plan = _sibling_plan([g.shape[1:] for g in grads], group_axes)
        lands = [((4,) + _half_shape(g.shape[1:], ax), g.dtype) for g, ax in zip(grads, group_axes)]
        handle, token = _split_start("sibling_" + tag + "_start", grads, lands, plan, len(grads), after)
        return (handle, plan, len(grads)), token

    def partials(tag, names, group_axes, exchange, after):
        handle, plan, n = exchange
        mine, theirs = _split_wait("sibling_" + tag + "_wait", handle, n, plan, after)
        return zip(*[_chip_partial(place, g, t, ax, "chip_partial_" + nm)
                     for nm, ax, g, t in zip(names, group_axes, mine, theirs)])

    ffn_names, ffn_axes = ("w_up", "w_down"), (0, 0)
    ffn_x, token = exchange_start("ffn", [dw_up, dw_down.reshape(4, 704, D)], ffn_axes, du)
    dx1, dx1b, dgffn = _matmul_nt_normbwd(du, wup, x1, g_ffn, dx2, token, "ffn_up_bwd", ts)
    ffn_pf, ffn_pb = partials("ffn", ffn_names, ffn_axes, ffn_x, dx1b)
    ffn_plan = _reduce_plan(2, 0)
    ffn_handle, token = _split_start("reduce_ffn_start", ffn_pb, [((3,) + p.shape[1:], BF16) for p in ffn_pb],
                                     ffn_plan, 6, ffn_pf[0])

    dzg, dyp, dyg, dpp, do, dzog, dbgate, dghead = _merge_bwd(dx1b, zr, yp, yg, o, b_gate, g_gla_head, wpp, wgla, wout,
                                                             token, ts)
    dw_out = _matmul_tn(mixed, dx1b, "dw_out", D)
    dw_gla = _matmul_tn(og, dyg, "dw_gla", D)
    dw_pp = _matmul_tn(pp, dyp, "dw_pp", 256, shard_major=True)

    out_names, out_axes = ("w_pool_proj", "w_gla_proj", "w_out"), (0, 0, 0)
    out_x, token = exchange_start("out", [dw_pp, dw_gla.reshape(4, 256, D), dw_out.reshape(4, 256, D)], out_axes, dpp)
    dzp, dwgrp, dscale = _pool_bwd(p, dpp, wgrp, pool_scale, token)
    out_pf, out_pb = partials("out", out_names, out_axes, out_x, dzp)
    out_plan = _reduce_plan(3, 0)
    out_handle, token = _split_start("reduce_out_start", out_pb, [((3,) + p_.shape[1:], BF16) for p_ in out_pb],
                                     out_plan, 9, out_pf[0])
    dq, dk, dv, dgpre = _gla_bwd(zr, do, sp, wgk_pad, b_gk, token, ts)
    dzgk, dwgk, dbgk = _gk_bwd(dgpre, zr, wgk_pad, dgpre, ts)
    dzr = jnp.concatenate([dzg, dv, dzog, dzp, dq, dk, dzgk], axis=1)
    dw_rt = _matmul_tn(dzr, h, "dw_in", D, tm=1152)

    def grad_rows(lo, hi):
        out = []
        for seg_lo, seg_hi, at in ((0, 1536, OFF_POOL), (1536, 3584, OFF_V), (3584, 3600, OFF_GK), (3600, N_IN, OFF_GATE)):
            a_, b_ = max(lo, seg_lo), min(hi, seg_hi)
            if a_ < b_:
                out.append(dw_rt[at + a_ - seg_lo:at + b_ - seg_lo])
        return jnp.concatenate(out, axis=0)

    dw_in_t = jnp.stack([grad_rows(j * nsh, (j + 1) * nsh) for j in range(4)])

    in_sib = _sibling_exchange([dw_in_t], (1,), [], "sibling_exchange_in")
    in_pf, in_pb = _chip_partial(place, dw_in_t, in_sib[0], 1, "chip_partial_w_in")
    in_plan = _reduce_plan(1, 0)
    in_handle, token = _split_start("reduce_in_start", [in_pb], [((3,) + in_pb.shape[1:], BF16)], in_plan, 3, in_pf)
    grad_x, _, dgmix = _matmul_nt_normbwd(dzr, w_in_t, xs, g_mix, dx1, token, "in_proj_bwd", ts, transposed=True)
    small_names = ("g_mix", "b_gate", "w_gk_up", "b_gk", "w_pool_grp", "pool_scale", "g_gla_head", "g_ffn", "w_conv",
                   "b_conv", "g_final")
    small_mine = [dgmix, dbgate, dwgk[:GATE_RANK], dbgk, dwgrp.reshape(4 * 128, 128), dscale, dghead, dgffn, dwconv, dbconv,
                  dgfin, loss_part]
    small_sib = _sibling_exchange([], (), small_mine, "sibling_exchange_small")
    small_chip = _add_many(small_mine, small_sib, "chip_partial_small")
    small_plan = _reduce_plan(0, len(small_chip))
    small_handle, token = _split_start("reduce_small_start", small_chip, [((4,) + a_.shape, F32) for a_ in small_chip],
                                       small_plan, 3 * len(small_chip), small_mine[0])

    ms = dict(w_in=m_in_t, w_pool_proj=m_w_pool_proj[0], w_gla_proj=m_w_gla_proj[0], w_out=m_w_out[0],
              w_up=m_w_up[0], w_down=m_w_down[0])
    vs = dict(w_in=v_in_t, w_pool_proj=v_w_pool_proj[0], w_gla_proj=v_w_gla_proj[0], w_out=v_w_out[0],
              w_up=v_w_up[0], w_down=v_w_down[0])
    grad, delta, new_m, new_v = {}, {}, {}, {}

    def finish_and_update(names, group_axes, part_f, landed, tag):
        halves = [_finish_half(pf, rb, ax, "finish_" + n) for n, ax, pf, rb in zip(names, group_axes, part_f, landed)]
        sib_halves = _sibling_share(halves, "sibling_share_" + tag)
        for n, ax, mine, theirs in zip(names, group_axes, halves, sib_halves):
            res = _adam_halves(place, shards[n], mine, theirs, ms[n], vs[n], ax, "adam_" + n)
            if n == "w_in":
                res = [jnp.transpose(r_) for r_ in res]
            grad[n], delta[n], new_m[n], new_v[n] = [r_[None] for r_ in res]

    _, ffn_landed = _split_wait("reduce_ffn_wait", ffn_handle, 2, ffn_plan, token)
    _, out_landed = _split_wait("reduce_out_wait", out_handle, 3, out_plan, ffn_landed[0])
    finish_and_update(ffn_names + out_names, ffn_axes + out_axes, ffn_pf + out_pf, ffn_landed + out_landed, "rest")
    _, in_landed = _split_wait("reduce_in_wait", in_handle, 1, in_plan, delta["w_out"])
    finish_and_update(("w_in",), (1,), (in_pf,), in_landed, "in")
    small_sent, small_landed = _split_wait("reduce_small_wait", small_handle, len(small_chip), small_plan, delta["w_in"])
    given = dict(g_mix=(g_mix, m_g_mix, v_g_mix), b_gate=(b_gate, m_b_gate, v_b_gate), w_gk_up=(w_gk_up, m_w_gk_up, v_w_gk_up),
                 b_gk=(b_gk, m_b_gk, v_b_gk), w_pool_grp=(w_pool_grp, m_w_pool_grp, v_w_pool_grp),
                 pool_scale=(pool_scale, m_pool_scale, v_pool_scale), g_gla_head=(g_gla_head, m_g_gla_head, v_g_gla_head),
                 g_ffn=(g_ffn, m_g_ffn, v_g_ffn), w_conv=(w_conv, m_w_conv, v_w_conv), b_conv=(b_conv, m_b_conv, v_b_conv),
                 g_final=(g_final, m_g_final, v_g_final))
    flat2 = lambda a: a.reshape(-1, a.shape[-1])
    widths = [dict(w_gk_up=128, w_conv=1408).get(n) for n in small_names]
    totals, ds, mo, vo = _adam_small(place, small_sent, small_landed, *[[flat2(given[n][k]) for n in small_names] for k in range(3)],
                                     widths)
    loss = totals[-1][0, 0]
    for i, n in enumerate(small_names):
        shp = given[n][0].shape
        grad[n], delta[n], new_m[n], new_v[n] = [r_.reshape(shp) for r_ in (totals[i], ds[i], mo[i], vo[i])]

    order = ("g_mix", "w_in", "b_gate", "w_gk_up", "b_gk", "w_pool_grp", "pool_scale", "g_gla_head", "w_pool_proj",
             "w_gla_proj", "w_out", "g_ffn", "w_up", "w_conv", "b_conv", "w_down", "g_final")
    return (loss, grad_x[None], *[grad[n] for n in order], *[delta[n] for n in order], *[new_m[n] for n in order],
            *[new_v[n] for n in order])
```

```python
import functools

import jax
import jax.numpy as jnp
from jax import lax
from jax.experimental import pallas as pl
from jax.experimental.pallas import tpu as pltpu

F32 = jnp.float32
BF16 = jnp.bfloat16
MESH = pl.DeviceIdType.MESH

D = 1024
EPS = 1e-6
CHUNK = 64
POOL_W = 512
POOL_WINDOWS = (2, 4, 8, 16)
HEADS = 4
HK = 128
HV = 256
GATE_RANK = 16
D_FF = 2816
N_UP = 2 * D_FF
N_IN = 5648
QSCALE = HK ** -0.5
N_INR = 5760
OFF_GATE, OFF_V, OFF_OG, OFF_POOL, OFF_Q, OFF_K, OFF_GK = 0, 2048, 3072, 4096, 4608, 5120, 5632

ADAM_LR, ADAM_B1, ADAM_B2, ADAM_EPS, ADAM_WD, ADAM_STEP = 0.001, 0.9, 0.999, 1e-08, 0.01, 10

VMEM_LIMIT = 56 * 1024 * 1024


def _cp(*sem):
    return pltpu.CompilerParams(dimension_semantics=sem if sem else None, vmem_limit_bytes=VMEM_LIMIT)


def _dot(a, b):
    return jnp.dot(a, b, preferred_element_type=F32)


def _dot_nt(a, b):
    return lax.dot_general(a, b, (((1,), (1,)), ((), ())), preferred_element_type=F32)


def _dot_tn(a, b):
    return lax.dot_general(a, b, (((0,), (0,)), ((), ())), preferred_element_type=F32)


def _sigmoid(v):
    return 1.0 / (1.0 + jnp.exp(-v))


def _rows(shape):
    return lax.broadcasted_iota(jnp.int32, shape, 0)


def _pick_row(v, r):
    return jnp.sum(jnp.where(_rows(v.shape) == r, v, 0.0), axis=0, keepdims=True)


def _rmsnorm(x, g, after, name, ts):
    s = x.shape[0]

    def body(x_ref, g_ref, after_ref, h_ref):
        xv = x_ref[...]
        r = lax.rsqrt(jnp.mean(xv * xv, axis=-1, keepdims=True) + EPS)
        h_ref[...] = (xv * r * g_ref[...]).astype(BF16)

    return pl.pallas_call(
        body, name=name, grid=(s // ts,),
        in_specs=[pl.BlockSpec((ts, D), lambda i: (i, 0)), pl.BlockSpec((1, D), lambda i: (0, 0)), ANY],
        out_specs=pl.BlockSpec((ts, D), lambda i: (i, 0)), out_shape=jax.ShapeDtypeStruct((s, D), BF16),
        compiler_params=_cp("arbitrary"),
    )(x, g, after)


MM_ROWS = 512


def _matmul_resident(h, w, after, name):
    s = h.shape[0]
    nj, tn = w.shape[0], w.shape[2]
    rc = min(s, MM_ROWS)

    def body(h_ref, w_ref, after_ref, z_ref):
        for r0 in range(0, s, rc):
            z_ref[r0:r0 + rc, :] = _dot(h_ref[r0:r0 + rc, :], w_ref[...]).astype(BF16)

    return pl.pallas_call(
        body, name=name, grid=(nj,),
        in_specs=[pl.BlockSpec((s, D), lambda j: (0, 0)), pl.BlockSpec((None, D, tn), lambda j: (j, 0, 0)), ANY],
        out_specs=pl.BlockSpec((s, tn), lambda j: (0, j)), out_shape=jax.ShapeDtypeStruct((s, nj * tn), BF16),
        compiler_params=_cp("arbitrary"),
    )(h, w, after)


PROJ_PIECES = ((3600, 2048, OFF_GATE), (1536, 2048, OFF_V), (0, 1536, OFF_POOL), (3584, GATE_RANK, OFF_GK))


def _projection_copies(w_hbm, w_ref, sems):
    return [pltpu.make_async_copy(w_hbm.at[pl.ds(src, n)], w_ref.at[pl.ds(dst, n)], sems.at[i])
            for i, (src, n, dst) in enumerate(PROJ_PIECES)]


def _load_projection(w_hbm, w_ref, sems):
    cps = _projection_copies(w_hbm, w_ref, sems)
    for cp in cps:
        cp.start()
    w_ref[OFF_GK + GATE_RANK:, :] = jnp.zeros((N_INR - OFF_GK - GATE_RANK, D), BF16)
    for cp in cps:
        cp.wait()


def _in_proj(h, w_nat, tn):
    s = h.shape[0]
    rc = min(s, MM_ROWS)
    nj = N_INR // tn
    first_use = [dst // tn for _, _, dst in PROJ_PIECES]

    def body(h_ref, w_hbm, z_ref, w_ref, sems):
        j = pl.program_id(0)
        cps = _projection_copies(w_hbm, w_ref, sems)

        @pl.when(j == 0)
        def _():
            for cp in cps:
                cp.start()
            w_ref[OFF_GK + GATE_RANK:, :] = jnp.zeros((N_INR - OFF_GK - GATE_RANK, D), BF16)

        for step in range(nj):
            due = [cp for cp, at in zip(cps, first_use) if at == step]
            if due:
                @pl.when(j == step)
                def _(due=due):
                    for cp in due:
                        cp.wait()

        wt = w_ref[pl.ds(pl.multiple_of(j * tn, 128), tn), :]
        for r0 in range(0, s, rc):
            z_ref[r0:r0 + rc, :] = _dot_nt(h_ref[r0:r0 + rc, :], wt).astype(BF16)

    return pl.pallas_call(
        body, name="in_proj", grid=(nj,),
        in_specs=[pl.BlockSpec((s, D), lambda j: (0, 0)), ANY],
        out_specs=pl.BlockSpec((s, tn), lambda j: (0, j)), out_shape=jax.ShapeDtypeStruct((s, N_INR), BF16),
        scratch_shapes=[pltpu.VMEM((N_INR, D), BF16), pltpu.SemaphoreType.DMA((len(PROJ_PIECES),))],
        compiler_params=_cp("arbitrary"),
    )(h, w_nat)


def _matmul_nt_normbwd(dz, w, x, g, resid, after, name, ts, transposed=False):
    s = x.shape[0]
    w_vmem = (N_INR, D) if transposed else (D, w.shape[0] * w.shape[2])
    n_sems = len(PROJ_PIECES) if transposed else w.shape[0]

    def body(dz_ref, w_hbm, x_ref, g_ref, r_ref, after_ref, o_ref, ob_ref, dg_ref, w_ref, sems):
        @pl.when(pl.program_id(0) == 0)
        def _():
            if transposed:
                _load_projection(w_hbm, w_ref, sems)
            else:
                kc = w.shape[2]
                cps = [pltpu.make_async_copy(w_hbm.at[j], w_ref.at[:, pl.ds(j * kc, kc)], sems.at[j])
                       for j in range(w.shape[0])]
                for cp in cps:
                    cp.start()
                for cp in cps:
                    cp.wait()
            dg_ref[...] = jnp.zeros_like(dg_ref)

        dh = _dot(dz_ref[...], w_ref[...]) if transposed else _dot_nt(dz_ref[...], w_ref[...])
        xv = x_ref[...]
        r = lax.rsqrt(jnp.mean(xv * xv, axis=-1, keepdims=True) + EPS)
        xh = xv * r
        dg_ref[...] += jnp.sum(dh * xh, axis=0, keepdims=True)
        dxh = dh * g_ref[...]
        out = r_ref[...] + r * (dxh - xh * jnp.mean(dxh * xh, axis=-1, keepdims=True))
        o_ref[...] = out
        ob_ref[...] = out.astype(BF16)

    row = lambda i: (i, 0)
    kdim = dz.shape[1]
    return pl.pallas_call(
        body, name=name, grid=(s // ts,),
        in_specs=[pl.BlockSpec((ts, kdim), row), ANY, pl.BlockSpec((ts, D), row),
                  pl.BlockSpec((1, D), lambda i: (0, 0)), pl.BlockSpec((ts, D), row), ANY],
        out_specs=[pl.BlockSpec((ts, D), row), pl.BlockSpec((ts, D), row), pl.BlockSpec((1, D), lambda i: (0, 0))],
        out_shape=[jax.ShapeDtypeStruct((s, D), F32), jax.ShapeDtypeStruct((s, D), BF16),
                   jax.ShapeDtypeStruct((1, D), F32)],
        scratch_shapes=[pltpu.VMEM(w_vmem, BF16), pltpu.SemaphoreType.DMA((n_sems,))],
        compiler_params=_cp("arbitrary"),
    )(dz, w, x, g, resid, after)


def _matmul_tn(a, b, name, tn, shard_major=False, tm=None):
    s, m = a.shape
    n = b.shape[1]
    tm = m if tm is None else tm
    ni, nj = m // tm, n // tn

    def body(a_ref, b_ref, o_ref):
        o_ref[...] = _dot_tn(a_ref[...], b_ref[...]).astype(BF16)

    if shard_major:
        out_spec = pl.BlockSpec((None, tm, tn), lambda i, j: (j, i, 0))
        out_shape = jax.ShapeDtypeStruct((nj, m, tn), BF16)
    else:
        out_spec = pl.BlockSpec((tm, tn), lambda i, j: (i, j))
        out_shape = jax.ShapeDtypeStruct((m, n), BF16)
    return pl.pallas_call(
        body, name=name, grid=(ni, nj),
        in_specs=[pl.BlockSpec((s, tm), lambda i, j: (0, i)), pl.BlockSpec((s, tn), lambda i, j: (0, j))],
        out_specs=out_spec, out_shape=out_shape,
        compiler_params=_cp("arbitrary", "arbitrary"),
    )(a, b)


def _pool_fwd(zr, wgrp, scale):
    s = zr.shape[0]

    def body(u_ref, w_ref, sc_ref, p_ref, pp_ref):
        row = _rows((s, 128))
        for gi, win in enumerate(POOL_WINDOWS):
            cs = slice(gi * 128, (gi + 1) * 128)
            u = u_ref[:, cs].astype(F32)
            acc, k = u, 1
            while k < win:
                acc = acc + jnp.where(row >= k, pltpu.roll(acc, k, 0), 0.0)
                k *= 2
            cnt = jnp.minimum(row + 1, win).astype(F32)
            p = (acc / cnt - u).astype(BF16)
            p_ref[:, cs] = p
            pp_ref[:, cs] = (_dot(p, w_ref[gi].astype(BF16)) * sc_ref[:, cs]).astype(BF16)

    return pl.pallas_call(
        body, name="pool_fwd", grid=(1,),
        in_specs=[pl.BlockSpec((s, POOL_W), lambda i: (0, OFF_POOL // POOL_W)),
                  pl.BlockSpec((4, 128, 128), lambda i: (0, 0, 0)), pl.BlockSpec((1, POOL_W), lambda i: (0, 0))],
        out_specs=[pl.BlockSpec((s, POOL_W), lambda i: (0, 0))] * 2,
        out_shape=[jax.ShapeDtypeStruct((s, POOL_W), BF16)] * 2,
        compiler_params=_cp("arbitrary"),
    )(zr, wgrp, scale)


def _pool_bwd(p, dpp, wgrp, scale, after):
    s = p.shape[0]

    def body(p_ref, dpp_ref, w_ref, sc_ref, after_ref, dz_ref, dw_ref, dsc_ref):
        row = _rows((s, 128))
        for gi, win in enumerate(POOL_WINDOWS):
            cs = slice(gi * 128, (gi + 1) * 128)
            pv = p_ref[:, cs]
            wb = w_ref[gi].astype(BF16)
            dpp_v = dpp_ref[:, cs].astype(F32)
            dsc_ref[:, cs] = jnp.sum(dpp_v * _dot(pv, wb), axis=0, keepdims=True)
            dpm = (dpp_v * sc_ref[:, cs]).astype(BF16)
            dw_ref[gi] = _dot_tn(pv, dpm)
            dp = _dot_nt(dpm, wb)
            cnt = jnp.minimum(row + 1, win).astype(F32)
            acc, k = dp / cnt, 1
            while k < win:
                acc = acc + jnp.where(row < s - k, pltpu.roll(acc, s - k, 0), 0.0)
                k *= 2
            dz_ref[:, cs] = (acc - dp).astype(BF16)

    full = lambda i: (0, 0)
    return pl.pallas_call(
        body, name="pool_bwd", grid=(1,),
        in_specs=[pl.BlockSpec((s, POOL_W), full), pl.BlockSpec((s, POOL_W), full),
                  pl.BlockSpec((4, 128, 128), lambda i: (0, 0, 0)), pl.BlockSpec((1, POOL_W), full), ANY],
        out_specs=[pl.BlockSpec((s, POOL_W), full), pl.BlockSpec((4, 128, 128), lambda i: (0, 0, 0)),
                   pl.BlockSpec((1, POOL_W), full)],
        out_shape=[jax.ShapeDtypeStruct((s, POOL_W), BF16), jax.ShapeDtypeStruct((4, 128, 128), F32),
                   jax.ShapeDtypeStruct((1, POOL_W), F32)],
        compiler_params=_cp("arbitrary"),
    )(p, dpp, wgrp, scale, after)


def _gla_decay(zgk_ref, wgk_ref, bgk_ref, rb):
    g = _dot(zgk_ref[...], wgk_ref[...].astype(BF16)) + bgk_ref[...]
    la = (jnp.minimum(g, 0.0) - jnp.log(1.0 + jnp.exp(-jnp.abs(g)))) * (1.0 / 16.0)
    rowm = _rows(la.shape) & (CHUNK - 1)
    bc, k = la, 1
    while k < CHUNK:
        bc = bc + jnp.where(rowm >= k, pltpu.roll(bc, k, 0), 0.0)
        k *= 2
    return g, jnp.exp(bc), jnp.exp(-bc)


GLA_HB = 4


def _gla_specs(rb, rmap):
    wk, wv = GLA_HB * HK, GLA_HB * HV
    return [pl.BlockSpec((rb, wk), lambda h, r: (rmap(h, r), OFF_Q // wk + h)),
            pl.BlockSpec((rb, wk), lambda h, r: (rmap(h, r), OFF_K // wk + h)),
            pl.BlockSpec((rb, wv), lambda h, r: (rmap(h, r), OFF_V // wv + h)),
            pl.BlockSpec((rb, 128), lambda h, r: (rmap(h, r), OFF_GK // 128))]


def _gla_fwd(zr, wgk, bgk, ghead, rb):
    s = zr.shape[0]
    nc = rb // CHUNK
    wk, wv = GLA_HB * HK, GLA_HB * HV

    def body(q_ref, k_ref, v_ref, zgk_ref, zog_ref, wgk_ref, bgk_ref, gh_ref, o_ref, og_ref, sp_ref, st_ref):
        @pl.when(pl.program_id(1) == 0)
        def _():
            st_ref[...] = jnp.zeros_like(st_ref)

        _, e_pos, e_neg = _gla_decay(zgk_ref, wgk_ref, bgk_ref, rb)
        lower = _rows((CHUNK, CHUNK)) >= lax.broadcasted_iota(jnp.int32, (CHUNK, CHUNK), 1)
        for c in range(nc):
            sl = slice(c * CHUNK, (c + 1) * CHUNK)
            for hh in range(GLA_HB):
                ck, cv = slice(hh * HK, (hh + 1) * HK), slice(hh * HV, (hh + 1) * HV)
                q = q_ref[sl, ck].astype(F32) * QSCALE
                k = k_ref[sl, ck].astype(F32)
                v = v_ref[sl, cv]
                ec, fc = e_pos[sl, ck], e_neg[sl, ck]
                qfw = (q * ec).astype(BF16)
                kfw_f = k * fc
                s_fw = _dot_nt(qfw, kfw_f.astype(BF16))
                s_bw = _dot_nt((q * fc).astype(BF16), (k * ec).astype(BF16))
                pm = jnp.where(lower, s_fw, s_bw).astype(BF16)
                st = st_ref[hh]
                stb = st.astype(BF16)
                sp_ref[c, hh] = stb
                o = _dot(pm, v) + _dot_nt(qfw, stb)
                e_last = _pick_row(ec, CHUNK - 1)
                kdec = (kfw_f * e_last).astype(BF16)
                st_ref[hh] = st * e_last + _dot_tn(v, kdec)
                r = lax.rsqrt(jnp.mean(o * o, axis=-1, keepdims=True) + EPS)
                zo = zog_ref[sl, cv].astype(F32)
                o_ref[sl, cv] = o.astype(BF16)
                og_ref[sl, cv] = (o * r * gh_ref[...] * zo * _sigmoid(zo)).astype(BF16)

    rmap = lambda h, r: r
    return pl.pallas_call(
        body, name="gla_fwd", grid=(HEADS // GLA_HB, s // rb),
        in_specs=_gla_specs(rb, rmap) + [
            pl.BlockSpec((rb, wv), lambda h, r: (r, OFF_OG // wv + h)),
            pl.BlockSpec((128, wk), lambda h, r: (0, h)), pl.BlockSpec((1, wk), lambda h, r: (0, h)),
            pl.BlockSpec((1, HV), lambda h, r: (0, 0))],
        out_specs=[pl.BlockSpec((rb, wv), lambda h, r: (r, h)), pl.BlockSpec((rb, wv), lambda h, r: (r, h)),
                   pl.BlockSpec((nc, GLA_HB, HV, HK), lambda h, r: (r, h, 0, 0))],
        out_shape=[jax.ShapeDtypeStruct((s, D), BF16), jax.ShapeDtypeStruct((s, D), BF16),
                   jax.ShapeDtypeStruct((s // CHUNK, HEADS, HV, HK), BF16)],
        scratch_shapes=[pltpu.VMEM((GLA_HB, HV, HK), F32)],
        compiler_params=_cp("arbitrary", "arbitrary"),
    )(zr, zr, zr, zr, zr, wgk, bgk, ghead)


def _gla_bwd(zr, do, sp, wgk, bgk, after, rb):
    s = zr.shape[0]
    nc = rb // CHUNK
    nr = s // rb
    wk, wv = GLA_HB * HK, GLA_HB * HV

    def body(q_ref, k_ref, v_ref, zgk_ref, do_ref, sp_ref, wgk_ref, bgk_ref, after_ref, dq_ref, dk_ref, dv_ref, dg_ref,
             gt_ref, dbc_ref):
        @pl.when(pl.program_id(1) == 0)
        def _():
            gt_ref[...] = jnp.zeros_like(gt_ref)

        g, e_pos, e_neg = _gla_decay(zgk_ref, wgk_ref, bgk_ref, rb)
        lower = _rows((CHUNK, CHUNK)) >= lax.broadcasted_iota(jnp.int32, (CHUNK, CHUNK), 1)
        is_last = _rows((CHUNK, HK)) == CHUNK - 1
        for c in reversed(range(nc)):
            sl = slice(c * CHUNK, (c + 1) * CHUNK)
            for hh in range(GLA_HB):
                ck, cv = slice(hh * HK, (hh + 1) * HK), slice(hh * HV, (hh + 1) * HV)
                q = q_ref[sl, ck].astype(F32) * QSCALE
                k = k_ref[sl, ck].astype(F32)
                v = v_ref[sl, cv]
                dov = do_ref[sl, cv]
                ec, fc = e_pos[sl, ck], e_neg[sl, ck]
                qfw_f, kfw_f, qbw_f, kbw_f = q * ec, k * fc, q * fc, k * ec
                qfw, kfw, qbw, kbw = qfw_f.astype(BF16), kfw_f.astype(BF16), qbw_f.astype(BF16), kbw_f.astype(BF16)
                pm = jnp.where(lower, _dot_nt(qfw, kfw), _dot_nt(qbw, kbw)).astype(BF16)
                e_last = _pick_row(ec, CHUNK - 1)
                kdec = (kfw_f * e_last).astype(BF16)
                gt = gt_ref[hh]
                gtb = gt.astype(BF16)
                spv = sp_ref[c, hh]
                dp = _dot_nt(dov, v)
                dv_ref[sl, cv] = (_dot_tn(pm, dov) + _dot_nt(kdec, gtb)).astype(BF16)
                ds_fw = jnp.where(lower, dp, 0.0).astype(BF16)
                ds_bw = jnp.where(lower, 0.0, dp).astype(BF16)
                dqfw = _dot(ds_fw, kfw) + _dot(dov, spv)
                dkfw = _dot_tn(ds_fw, qfw)
                dqbw = _dot(ds_bw, kbw)
                dkbw = _dot_tn(ds_bw, qbw)
                dkdec = _dot(v, gtb)
                de_last = (jnp.sum(gt * spv.astype(F32), axis=0, keepdims=True)
                           + jnp.sum(dkdec * kfw_f, axis=0, keepdims=True))
                dkfw = dkfw + dkdec * e_last
                dq_ref[sl, ck] = ((dqfw * ec + dqbw * fc) * QSCALE).astype(BF16)
                dk_ref[sl, ck] = (dkfw * fc + dkbw * ec).astype(BF16)
                dbc = dqfw * qfw_f - dqbw * qbw_f + dkbw * kbw_f - dkfw * kfw_f
                dbc_ref[sl, ck] = dbc + jnp.where(is_last, de_last * e_last, 0.0)
                gt_ref[hh] = _dot_tn(dov, qfw) + gt * e_last
        rowm = _rows((rb, wk)) & (CHUNK - 1)
        dla, kk = dbc_ref[...], 1
        while kk < CHUNK:
            dla = dla + jnp.where(rowm < CHUNK - kk, pltpu.roll(dla, rb - kk, 0), 0.0)
            kk *= 2
        dg_ref[...] = dla * (1.0 / 16.0) * _sigmoid(-g)

    rmap = lambda h, r: nr - 1 - r
    rev = lambda h, r: (nr - 1 - r, h)
    return pl.pallas_call(
        body, name="gla_bwd", grid=(HEADS // GLA_HB, nr),
        in_specs=_gla_specs(rb, rmap) + [
            pl.BlockSpec((rb, wv), rev),
            pl.BlockSpec((nc, GLA_HB, HV, HK), lambda h, r: (nr - 1 - r, h, 0, 0)),
            pl.BlockSpec((128, wk), lambda h, r: (0, h)), pl.BlockSpec((1, wk), lambda h, r: (0, h)), ANY],
        out_specs=[pl.BlockSpec((rb, wk), rev), pl.BlockSpec((rb, wk), rev), pl.BlockSpec((rb, wv), rev),
                   pl.BlockSpec((rb, wk), rev)],
        out_shape=[jax.ShapeDtypeStruct((s, HEADS * HK), BF16), jax.ShapeDtypeStruct((s, HEADS * HK), BF16),
                   jax.ShapeDtypeStruct((s, D), BF16), jax.ShapeDtypeStruct((s, HEADS * HK), F32)],
        scratch_shapes=[pltpu.VMEM((GLA_HB, HV, HK), F32), pltpu.VMEM((rb, wk), F32)],
        compiler_params=_cp("arbitrary", "arbitrary"),
    )(zr, zr, zr, zr, do, sp, wgk, bgk, after)


def _gk_bwd(dgpre, zr, wgk, after, ts):
    s = zr.shape[0]

    def body(dg_ref, zgk_ref, w_ref, after_ref, dz_ref, dw_ref, db_ref):
        @pl.when(pl.program_id(0) == 0)
        def _():
            dw_ref[...] = jnp.zeros_like(dw_ref)
            db_ref[...] = jnp.zeros_like(db_ref)

        dg = dg_ref[...]
        dgb = dg.astype(BF16)
        dz_ref[...] = _dot_nt(dgb, w_ref[...].astype(BF16)).astype(BF16)
        dw_ref[...] += _dot_tn(zgk_ref[...], dgb)
        db_ref[...] += jnp.sum(dg, axis=0, keepdims=True)

    return pl.pallas_call(
        body, name="gk_bwd", grid=(s // ts,),
        in_specs=[pl.BlockSpec((ts, 512), lambda i: (i, 0)), pl.BlockSpec((ts, 128), lambda i: (i, OFF_GK // 128)),
                  pl.BlockSpec((128, 512), lambda i: (0, 0)), ANY],
        out_specs=[pl.BlockSpec((ts, 128), lambda i: (i, 0)), pl.BlockSpec((128, 512), lambda i: (0, 0)),
                   pl.BlockSpec((1, 512), lambda i: (0, 0))],
        out_shape=[jax.ShapeDtypeStruct((s, 128), BF16), jax.ShapeDtypeStruct((128, 512), F32),
                   jax.ShapeDtypeStruct((1, 512), F32)],
        compiler_params=_cp("arbitrary"),
    )(dgpre, zr, wgk, after)


def _merge_fwd(x, zr, pp, og, bgate, wpp, wgla, wout, gffn, after, ts):
    s = x.shape[0]

    def body(x_ref, z0_ref, z1_ref, pp_ref, og_ref, bg_ref, wpp_ref, wgla_ref, wout_ref, gf_ref, after_ref,
             x1_ref, mix_ref, yp_ref, yg_ref, h2_ref):
        ppv = pp_ref[...]
        yp = jnp.concatenate([_dot(ppv, wpp_ref[j]) for j in range(4)], axis=1)
        yg = _dot(og_ref[...], wgla_ref[...])
        g0 = _sigmoid(z0_ref[...].astype(F32) + bg_ref[:, :D])
        g1 = _sigmoid(z1_ref[...].astype(F32) + bg_ref[:, D:])
        mixed = (g0 * yp + g1 * yg).astype(BF16)
        x1 = x_ref[...] + _dot(mixed, wout_ref[...])
        x1_ref[...] = x1
        mix_ref[...] = mixed
        yp_ref[...] = yp.astype(BF16)
        yg_ref[...] = yg.astype(BF16)
        r = lax.rsqrt(jnp.mean(x1 * x1, axis=-1, keepdims=True) + EPS)
        h2_ref[...] = (x1 * r * gf_ref[...]).astype(BF16)

    row = lambda i: (i, 0)
    const2 = lambda i: (0, 0)
    return pl.pallas_call(
        body, name="merge_fwd", grid=(s // ts,),
        in_specs=[pl.BlockSpec((ts, D), row), pl.BlockSpec((ts, D), lambda i: (i, 0)), pl.BlockSpec((ts, D), lambda i: (i, 1)),
                  pl.BlockSpec((ts, POOL_W), row), pl.BlockSpec((ts, D), row), pl.BlockSpec((1, 2 * D), const2),
                  pl.BlockSpec((4, POOL_W, 256), lambda i: (0, 0, 0)), pl.BlockSpec((D, D), const2),
                  pl.BlockSpec((D, D), const2), pl.BlockSpec((1, D), const2), ANY],
        out_specs=[pl.BlockSpec((ts, D), row)] * 5,
        out_shape=[jax.ShapeDtypeStruct((s, D), F32)] + [jax.ShapeDtypeStruct((s, D), BF16)] * 4,
        compiler_params=_cp("arbitrary"),
    )(x, zr, zr, pp, og, bgate, wpp, wgla, wout, gffn, after)


def _merge_bwd(dx1b, zr, yp, yg, o, bgate, ghead, wpp, wgla, wout, after, ts):
    s = dx1b.shape[0]

    def body(dx_ref, z0_ref, z1_ref, zog_ref, yp_ref, yg_ref, o_ref, bg_ref, gh_ref, wpp_ref, wgla_ref, wout_ref, after_ref,
             dzg_ref, dyp_ref, dyg_ref, dpp_ref, do_ref, dzog_ref, dbg_ref, dgh_ref):
        @pl.when(pl.program_id(0) == 0)
        def _():
            dbg_ref[...] = jnp.zeros_like(dbg_ref)
            dgh_ref[...] = jnp.zeros_like(dgh_ref)

        dmix = _dot_nt(dx_ref[...], wout_ref[...])
        g0 = _sigmoid(z0_ref[...].astype(F32) + bg_ref[:, :D])
        g1 = _sigmoid(z1_ref[...].astype(F32) + bg_ref[:, D:])
        dypb = (dmix * g0).astype(BF16)
        dygb = (dmix * g1).astype(BF16)
        dz0 = dmix * yp_ref[...].astype(F32) * g0 * (1.0 - g0)
        dz1 = dmix * yg_ref[...].astype(F32) * g1 * (1.0 - g1)
        dzg_ref[:, :D] = dz0.astype(BF16)
        dzg_ref[:, D:] = dz1.astype(BF16)
        dbg_ref[:, :D] += jnp.sum(dz0, axis=0, keepdims=True)
        dbg_ref[:, D:] += jnp.sum(dz1, axis=0, keepdims=True)
        dyp_ref[...] = dypb
        dyg_ref[...] = dygb
        dpp = _dot_nt(dypb[:, 0:256], wpp_ref[0])
        for j in range(1, 4):
            dpp = dpp + _dot_nt(dypb[:, j * 256:(j + 1) * 256], wpp_ref[j])
        dpp_ref[...] = dpp.astype(BF16)
        dog = _dot_nt(dygb, wgla_ref[...])
        gh = gh_ref[...]
        dgh = jnp.zeros((1, HV), F32)
        for h in range(HEADS):
            cs = slice(h * HV, (h + 1) * HV)
            ov = o_ref[:, cs].astype(F32)
            r = lax.rsqrt(jnp.mean(ov * ov, axis=-1, keepdims=True) + EPS)
            oh = ov * r
            zo = zog_ref[:, cs].astype(F32)
            sg = _sigmoid(zo)
            dog_h = dog[:, cs]
            don = dog_h * zo * sg
            dzog_ref[:, cs] = (dog_h * oh * gh * sg * (1.0 + zo * (1.0 - sg))).astype(BF16)
            dgh = dgh + jnp.sum(don * oh, axis=0, keepdims=True)
            doh = don * gh
            do_ref[:, cs] = (r * (doh - oh * jnp.mean(doh * oh, axis=-1, keepdims=True))).astype(BF16)
        dgh_ref[...] += dgh

    row = lambda i: (i, 0)
    const2 = lambda i: (0, 0)
    return pl.pallas_call(
        body, name="merge_bwd", grid=(s // ts,),
        in_specs=[pl.BlockSpec((ts, D), row), pl.BlockSpec((ts, D), lambda i: (i, 0)), pl.BlockSpec((ts, D), lambda i: (i, 1)),
                  pl.BlockSpec((ts, D), lambda i: (i, OFF_OG // D)), pl.BlockSpec((ts, D), row), pl.BlockSpec((ts, D), row),
                  pl.BlockSpec((ts, D), row), pl.BlockSpec((1, 2 * D), const2), pl.BlockSpec((1, HV), const2),
                  pl.BlockSpec((4, POOL_W, 256), lambda i: (0, 0, 0)), pl.BlockSpec((D, D), const2),
                  pl.BlockSpec((D, D), const2), ANY],
        out_specs=[pl.BlockSpec((ts, 2 * D), row), pl.BlockSpec((ts, D), row), pl.BlockSpec((ts, D), row),
                   pl.BlockSpec((ts, POOL_W), row), pl.BlockSpec((ts, D), row), pl.BlockSpec((ts, D), row),
                   pl.BlockSpec((1, 2 * D), const2), pl.BlockSpec((1, HV), const2)],
        out_shape=[jax.ShapeDtypeStruct((s, 2 * D), BF16), jax.ShapeDtypeStruct((s, D), BF16),
                   jax.ShapeDtypeStruct((s, D), BF16), jax.ShapeDtypeStruct((s, POOL_W), BF16),
                   jax.ShapeDtypeStruct((s, D), BF16), jax.ShapeDtypeStruct((s, D), BF16),
                   jax.ShapeDtypeStruct((1, 2 * D), F32), jax.ShapeDtypeStruct((1, HV), F32)],
        compiler_params=_cp("arbitrary"),
    )(dx1b, zr, zr, zr, yp, yg, o, bgate, ghead, wpp, wgla, wout, after)


HALO = 16
CCH = 1408


def _conv_taps(u_ref, halo_ref, cs, first, ts):
    u = u_ref[:, cs].astype(F32)
    hal = halo_ref[:, cs].astype(F32)
    h1 = jnp.where(first, 0.0, _pick_row(hal, HALO - 1))
    h2 = jnp.where(first, 0.0, _pick_row(hal, HALO - 2))
    row8 = _rows((8, u.shape[1]))
    r1, r2 = pltpu.roll(u, 1, 0), pltpu.roll(u, 2, 0)
    r1 = jnp.concatenate([jnp.where(row8 == 0, h1, r1[:8]), r1[8:]], axis=0)
    r2 = jnp.concatenate([jnp.where(row8 == 0, h2, jnp.where(row8 == 1, h1, r2[:8])), r2[8:]], axis=0)
    return u, r1, r2


def _ffn_down_loss(u, x1, tgt, wconv, bconv, wdown, gfin, ts):
    s = x1.shape[0]

    def body(u_ref, halo_ref, x1_ref, t_ref, wc_ref, bc_ref, wd_ref, gf_ref, a_ref, c_ref, dx_ref, dxb_ref, ls_ref,
             dgf_ref):
        i = pl.program_id(0)

        @pl.when(i == 0)
        def _():
            ls_ref[...] = jnp.zeros_like(ls_ref)
            dgf_ref[...] = jnp.zeros_like(dgf_ref)

        first = i == 0
        acc = x1_ref[...]
        for hf in range(D_FF // CCH):
            cg = slice(hf * CCH, (hf + 1) * CCH)
            cv = slice(D_FF + hf * CCH, D_FF + (hf + 1) * CCH)
            vals = []
            for cs in (cg, cv):
                u0, u1, u2 = _conv_taps(u_ref, halo_ref, cs, first, ts)
                vals.append(bc_ref[:, cs] + wc_ref[0:1, cs] * u2 + wc_ref[1:2, cs] * u1 + wc_ref[2:3, cs] * u0)
                c_ref[:, cs] = vals[-1].astype(BF16)
            a = (vals[0] * _sigmoid(vals[0]) * vals[1]).astype(BF16)
            a_ref[:, cg] = a
            acc = acc + _dot(a, wd_ref[cg, :])
        r = lax.rsqrt(jnp.mean(acc * acc, axis=-1, keepdims=True) + EPS)
        xh = acc * r
        gf = gf_ref[...]
        err = xh * gf - t_ref[...]
        ls_ref[...] += (0.5 / D) * jnp.sum(jnp.sum(err * err, axis=-1, keepdims=True), axis=0, keepdims=True)
        dy = err * (1.0 / D)
        dgf_ref[...] += jnp.sum(dy * xh, axis=0, keepdims=True)
        dxh = dy * gf
        dx = r * (dxh - xh * jnp.mean(dxh * xh, axis=-1, keepdims=True))
        dx_ref[...] = dx
        dxb_ref[...] = dx.astype(BF16)

    row = lambda i: (i, 0)
    const2 = lambda i: (0, 0)
    return pl.pallas_call(
        body, name="ffn_down_loss", grid=(s // ts,),
        in_specs=[pl.BlockSpec((ts, N_UP), row),
                  pl.BlockSpec((HALO, N_UP), lambda i: (jnp.maximum(i * (ts // HALO) - 1, 0), 0)),
                  pl.BlockSpec((ts, D), row), pl.BlockSpec((ts, D), row), pl.BlockSpec((3, N_UP), const2),
                  pl.BlockSpec((1, N_UP), const2), pl.BlockSpec((D_FF, D), const2), pl.BlockSpec((1, D), const2)],
        out_specs=[pl.BlockSpec((ts, D_FF), row), pl.BlockSpec((ts, N_UP), row), pl.BlockSpec((ts, D), row),
                   pl.BlockSpec((ts, D), row), pl.BlockSpec((1, 128), const2), pl.BlockSpec((1, D), const2)],
        out_shape=[jax.ShapeDtypeStruct((s, D_FF), BF16), jax.ShapeDtypeStruct((s, N_UP), BF16),
                   jax.ShapeDtypeStruct((s, D), F32), jax.ShapeDtypeStruct((s, D), BF16),
                   jax.ShapeDtypeStruct((1, 128), F32), jax.ShapeDtypeStruct((1, D), F32)],
        compiler_params=_cp("arbitrary"),
    )(u, u, x1, tgt, wconv, bconv, wdown, gfin)


def _ffn_bwd(dx2b, u, c, wconv, wdown, ts):
    s = dx2b.shape[0]
    nt = s // ts

    def body(dx_ref, u_ref, c_ref, wc_ref, wd_ref, du_ref, db_ref, dw_ref, nxt_ref):
        @pl.when(pl.program_id(0) == 0)
        def _():
            db_ref[...] = jnp.zeros_like(db_ref)
            dw_ref[...] = jnp.zeros_like(dw_ref)
            nxt_ref[...] = jnp.zeros_like(nxt_ref)

        dxv = dx_ref[...]
        row8 = _rows((8, CCH))
        for hf in range(D_FF // CCH):
            cg = slice(hf * CCH, (hf + 1) * CCH)
            cv = slice(D_FF + hf * CCH, D_FF + (hf + 1) * CCH)
            da = _dot_nt(dxv, wd_ref[cg, :])
            gate = c_ref[:, cg].astype(F32)
            val = c_ref[:, cv].astype(F32)
            sg = _sigmoid(gate)
            dcs = (da * val * sg * (1.0 + gate * (1.0 - sg)), da * gate * sg)
            for cs, dc in zip((cg, cv), dcs):
                n1 = nxt_ref[0:1, cs]
                n2 = nxt_ref[1:2, cs]
                r1, r2 = pltpu.roll(dc, ts - 1, 0), pltpu.roll(dc, ts - 2, 0)
                f1 = jnp.concatenate([r1[:ts - 8], jnp.where(row8 == 7, n1, r1[ts - 8:])], axis=0)
                f2 = jnp.concatenate([r2[:ts - 8], jnp.where(row8 == 7, n2, jnp.where(row8 == 6, n1, r2[ts - 8:]))], axis=0)
                uv = u_ref[:, cs].astype(F32)
                db_ref[:, cs] += jnp.sum(dc, axis=0, keepdims=True)
                dw_ref[0:1, cs] += jnp.sum(f2 * uv, axis=0, keepdims=True)
                dw_ref[1:2, cs] += jnp.sum(f1 * uv, axis=0, keepdims=True)
                dw_ref[2:3, cs] += jnp.sum(dc * uv, axis=0, keepdims=True)
                du_ref[:, cs] = (wc_ref[2:3, cs] * dc + wc_ref[1:2, cs] * f1 + wc_ref[0:1, cs] * f2).astype(BF16)
                nxt_ref[:, cs] = dc[0:8, :]

    rev = lambda i: (nt - 1 - i, 0)
    const2 = lambda i: (0, 0)
    return pl.pallas_call(
        body, name="ffn_bwd", grid=(nt,),
        in_specs=[pl.BlockSpec((ts, D), rev), pl.BlockSpec((ts, N_UP), rev), pl.BlockSpec((ts, N_UP), rev),
                  pl.BlockSpec((3, N_UP), const2), pl.BlockSpec((D_FF, D), const2)],
        out_specs=[pl.BlockSpec((ts, N_UP), rev), pl.BlockSpec((1, N_UP), const2), pl.BlockSpec((3, N_UP), const2)],
        out_shape=[jax.ShapeDtypeStruct((s, N_UP), BF16), jax.ShapeDtypeStruct((1, N_UP), F32),
                   jax.ShapeDtypeStruct((3, N_UP), F32)],
        scratch_shapes=[pltpu.VMEM((8, N_UP), F32)],
        compiler_params=_cp("arbitrary"),
    )(dx2b, u, c, wconv, wdown)


ANY = pl.BlockSpec(memory_space=pl.ANY)


def _place():
    x, y, c = lax.axis_index("x"), lax.axis_index("y"), lax.axis_index("c")
    chips = [(1 - x, y), (x, 1 - y), (1 - x, 1 - y)]
    return x, y, c, chips


def _half(shape, c, axis):
    size = shape[axis] // 2
    cut = pl.ds(pl.multiple_of(c * size, 8 if axis == 0 else 128), size)
    return (cut, slice(None)) if axis == 0 else (slice(None), cut)


def _half_shape(shape, axis):
    return (shape[0] // 2, shape[1]) if axis == 0 else (shape[0], shape[1] // 2)


def _remote(src, dst, send_sems, recv_sems, k, to):
    return pltpu.make_async_remote_copy(src_ref=src, dst_ref=dst, send_sem=send_sems.at[k], recv_sem=recv_sems.at[k],
                                        device_id=to, device_id_type=MESH)


def _sibling_exchange(grads, axes, smalls, name):
    nb = len(grads)
    n = nb + len(smalls)

    def body(*refs):
        ins, outs = refs[:n], refs[n:2 * n]
        send_sems, recv_sems = refs[2 * n:]
        x, y, c, _ = _place()
        sib = (x, y, 1 - c)
        cps = []
        for a in range(nb):
            theirs = _half(grads[a].shape[1:], 1 - c, axes[a])
            cps.append(_remote(ins[a].at[(slice(None),) + theirs], outs[a], send_sems, recv_sems, a, sib))
        for a in range(nb, n):
            cps.append(_remote(ins[a], outs[a], send_sems, recv_sems, a, sib))
        for cp in cps:
            cp.start()
        for cp in cps:
            cp.wait()

    out_shape = [jax.ShapeDtypeStruct((4,) + _half_shape(g.shape[1:], ax), g.dtype) for g, ax in zip(grads, axes)]
    out_shape += [jax.ShapeDtypeStruct(a.shape, F32) for a in smalls]
    return pl.pallas_call(
        body, name=name, in_specs=[ANY] * n, out_specs=[ANY] * n, out_shape=out_shape,
        scratch_shapes=[pltpu.SemaphoreType.DMA((n,)), pltpu.SemaphoreType.DMA((n,))],
        compiler_params=pltpu.CompilerParams(has_side_effects=True),
    )(*grads, *smalls)


def _gather_share(lands, axes, name):
    n = len(lands)

    def body(*refs):
        outs = refs[n:2 * n]
        send_sems, recv_sems = refs[2 * n:]
        x, y, c, chips = _place()
        sib = (x, y, 1 - c)
        cps = []
        for a in range(n):
            mine = _half(lands[a].shape[1:], c, axes[a])
            for k, ch in enumerate(chips):
                landed = outs[a].at[(2 * ch[0] + ch[1],) + mine]
                cps.append(_remote(landed, landed, send_sems, recv_sems, 3 * a + k, sib))
        for cp in cps:
            cp.start()
        for a in range(n):
            other = _half(lands[a].shape[1:], 1 - c, axes[a])
            for k, ch in enumerate(chips):
                landed = outs[a].at[(2 * ch[0] + ch[1],) + other]
                _remote(landed, landed, send_sems, recv_sems, 3 * a + k, sib).wait_recv()
        for cp in cps:
            cp.wait_send()

    return pl.pallas_call(
        body, name=name, in_specs=[ANY] * n, out_specs=[ANY] * n,
        out_shape=[jax.ShapeDtypeStruct(a.shape, a.dtype) for a in lands],
        input_output_aliases={a: a for a in range(n)},
        scratch_shapes=[pltpu.SemaphoreType.DMA((3 * n,)), pltpu.SemaphoreType.DMA((3 * n,))],
        compiler_params=pltpu.CompilerParams(has_side_effects=True),
    )(*lands)


def _sibling_share(halves, name):
    n = len(halves)

    def body(*refs):
        ins, outs = refs[:n], refs[n:2 * n]
        send_sems, recv_sems = refs[2 * n:]
        x, y, c, _ = _place()
        cps = [_remote(ins[a], outs[a], send_sems, recv_sems, a, (x, y, 1 - c)) for a in range(n)]
        for cp in cps:
            cp.start()
        for cp in cps:
            cp.wait()

    return pl.pallas_call(
        body, name=name, in_specs=[ANY] * n, out_specs=[ANY] * n,
        out_shape=[jax.ShapeDtypeStruct(h.shape, F32) for h in halves],
        scratch_shapes=[pltpu.SemaphoreType.DMA((n,)), pltpu.SemaphoreType.DMA((n,))],
        compiler_params=pltpu.CompilerParams(has_side_effects=True),
    )(*halves)


HBM = pl.BlockSpec(memory_space=pltpu.HBM)
SEM = pl.BlockSpec(memory_space=pltpu.SEMAPHORE)
DATAFLOW = pltpu.SideEffectType.DATAFLOW_SIDE_EFFECTING


def _split_start(name, srcs, land_shapes, plan, n_copies, after):
    lands = [lax.empty(*ls) if isinstance(ls, tuple) else ls for ls in land_shapes]
    bufs = list(srcs) + lands
    nb, ns = len(bufs), len(srcs)

    def body(*refs):
        send_sems, recv_sems, token = refs[nb + 1], refs[nb + 2], refs[-1]
        for k, (src, dst, to) in enumerate(plan(refs[:ns], refs[ns:nb])):
            _remote(src, dst, send_sems, recv_sems, k, to).start()
        token[...] = jnp.zeros_like(token)

    res = pl.pallas_call(
        body, name=name,
        out_shape=(pltpu.SemaphoreType.DMA((n_copies,)), pltpu.SemaphoreType.DMA((n_copies,)),
                   *[pltpu.HBM(b.shape, b.dtype) for b in bufs], jax.ShapeDtypeStruct((8, 128), F32)),
        in_specs=[HBM] * nb + [ANY],
        out_specs=(SEM, SEM, *[HBM] * nb, pl.BlockSpec(memory_space=pltpu.VMEM)),
        input_output_aliases={i: 2 + i for i in range(nb)},
        compiler_params=pltpu.CompilerParams(has_side_effects=DATAFLOW),
    )(*[pltpu.with_memory_space_constraint(b, pltpu.HBM) for b in bufs], after)
    return (res[0], res[1], list(res[2:2 + nb])), res[-1]


def _split_wait(name, handle, n_srcs, plan, after):
    send_sems, recv_sems, bufs = handle
    nb = len(bufs)

    def body(*refs):
        sends, recvs = refs[nb], refs[nb + 1]
        for k, (src, dst, to) in enumerate(plan(refs[:n_srcs], refs[n_srcs:nb])):
            cp = _remote(src, dst, sends, recvs, k, to)
            cp.wait_send()
            cp.wait_recv()

    res = pl.pallas_call(
        body, name=name, out_shape=[pltpu.HBM(b.shape, b.dtype) for b in bufs],
        in_specs=[HBM] * nb + [SEM, SEM, ANY], out_specs=[HBM] * nb,
        input_output_aliases={i: i for i in range(nb)},
        compiler_params=pltpu.CompilerParams(has_side_effects=DATAFLOW),
    )(*bufs, send_sems, recv_sems, after)
    return list(res[:n_srcs]), list(res[n_srcs:])


def _gather_plan(shapes, axes, n_whole=0):
    def plan(srcs, lands):
        x, y, c, chips = _place()
        out = []
        for a, (shape, axis) in enumerate(zip(shapes, axes)):
            mine = _half(shape, c, axis)
            for ch in chips:
                out.append((srcs[a].at[mine], lands[a].at[(2 * x + y,) + mine], (ch[0], ch[1], c)))
        for a in range(len(shapes), len(shapes) + n_whole):
            for ch in chips:
                out.append((srcs[a], lands[a].at[2 * x + y], (ch[0], ch[1], c)))
        return out
    return plan


def _share_plan(shapes, axes):
    def plan(srcs, lands):
        x, y, c, chips = _place()
        out = []
        for a, (shape, axis) in enumerate(zip(shapes, axes)):
            mine = _half(shape, c, axis)
            for ch in chips:
                landed = lands[a].at[(2 * ch[0] + ch[1],) + mine]
                out.append((landed, landed, (x, y, 1 - c)))
        return out
    return plan


def _sibling_plan(shapes, axes):
    def plan(srcs, lands):
        x, y, c, _ = _place()
        return [(srcs[a].at[(slice(None),) + _half(shape, 1 - c, axis)], lands[a], (x, y, 1 - c))
                for a, (shape, axis) in enumerate(zip(shapes, axes))]
    return plan


def _reduce_plan(n_big, n_small):
    def plan(srcs, lands):
        x, y, c, chips = _place()
        out = []
        for a in range(n_big):
            for k, ch in enumerate(chips):
                out.append((srcs[a].at[2 * ch[0] + ch[1]], lands[a].at[k], (ch[0], ch[1], c)))
        for a in range(n_big, n_big + n_small):
            for ch in chips:
                out.append((srcs[a], lands[a].at[2 * x + y], (ch[0], ch[1], c)))
        return out
    return plan


def _row_tile(rows, cols, mult):
    best = mult
    for t in range(mult, rows + 1, mult):
        if rows % t == 0 and t * cols * 4 <= (2 << 20):
            best = t
    return best if rows % best == 0 else rows


COL_TILE = 256


def _half_tiling(hshape, axis, mult):
    hr, hc = hshape
    if axis == 0:
        tr = _row_tile(hr, hc, mult)
        return tr, hc, hr // tr
    return hr, COL_TILE, hc // COL_TILE


def _tile_idx(axis, t):
    return (t, 0) if axis == 0 else (0, t)


def _chip_partial(place, g, t, axis, name):
    hshape = t.shape[1:]
    br, bc, nt = _half_tiling(hshape, axis, 16)

    def body(pl_ref, g_ref, t_ref, pf_ref, pb_ref):
        v = g_ref[...].astype(F32) + t_ref[...].astype(F32)
        pb_ref[...] = v.astype(BF16)

        @pl.when(pl.program_id(1) == pl_ref[0])
        def _():
            pf_ref[...] = v

    blk = (None, br, bc)
    return pl.pallas_call(
        body, name=name,
        grid_spec=pltpu.PrefetchScalarGridSpec(
            num_scalar_prefetch=1, grid=(nt, 4),
            in_specs=[pl.BlockSpec(blk, lambda i, j, p: (j,) + _tile_idx(axis, p[1] * nt + i)),
                      pl.BlockSpec(blk, lambda i, j, p: (j,) + _tile_idx(axis, i))],
            out_specs=[pl.BlockSpec((br, bc), lambda i, j, p: _tile_idx(axis, i)),
                       pl.BlockSpec(blk, lambda i, j, p: (j,) + _tile_idx(axis, i))]),
        out_shape=[jax.ShapeDtypeStruct(hshape, F32), jax.ShapeDtypeStruct((4,) + hshape, BF16)],
        compiler_params=_cp("arbitrary", "arbitrary"),
    )(place, g, t)


def _finish_half(pf, rb, axis, name):
    hshape = pf.shape
    br, bc, nt = _half_tiling(hshape, axis, 16)

    def body(pf_ref, rb_ref, o_ref):
        o_ref[...] = ((pf_ref[...] + rb_ref[0].astype(F32)) + rb_ref[1].astype(F32)) + rb_ref[2].astype(F32)

    return pl.pallas_call(
        body, name=name, grid=(nt,),
        in_specs=[pl.BlockSpec((br, bc), lambda i: _tile_idx(axis, i)),
                  pl.BlockSpec((3, br, bc), lambda i: (0,) + _tile_idx(axis, i))],
        out_specs=pl.BlockSpec((br, bc), lambda i: _tile_idx(axis, i)),
        out_shape=jax.ShapeDtypeStruct(hshape, F32),
        compiler_params=_cp("arbitrary"),
    )(pf, rb)


def _adam_math(w, g, m, v):
    m = ADAM_B1 * m + (1.0 - ADAM_B1) * g
    v = ADAM_B2 * v + (1.0 - ADAM_B2) * (g * g)
    m_hat = m / (1.0 - ADAM_B1 ** ADAM_STEP)
    v_hat = v / (1.0 - ADAM_B2 ** ADAM_STEP)
    return -ADAM_LR * (m_hat / (jnp.sqrt(v_hat) + ADAM_EPS) + ADAM_WD * w), m, v


def _adam_halves(place, w, mine, theirs, m, v, axis, name):
    br, bc, nt = _half_tiling(mine.shape, axis, 8)

    def body(pl_ref, w_ref, a_ref, b_ref, m_ref, v_ref, g_ref, d_ref, mo_ref, vo_ref):
        is_mine = pl.program_id(0) // nt == pl_ref[1]
        g = jnp.where(is_mine, a_ref[...], b_ref[...])
        d, mn, vn = _adam_math(w_ref[...], g, m_ref[...], v_ref[...])
        g_ref[...] = g
        d_ref[...] = d
        mo_ref[...] = mn
        vo_ref[...] = vn

    full = pl.BlockSpec((br, bc), lambda i, p: _tile_idx(axis, i))
    mine_spec = pl.BlockSpec((br, bc), lambda i, p: _tile_idx(axis, jnp.where(i // nt == p[1], i % nt, nt - 1)))
    theirs_spec = pl.BlockSpec((br, bc), lambda i, p: _tile_idx(axis, jnp.where(i // nt == p[1], 0, i % nt)))
    return pl.pallas_call(
        body, name=name,
        grid_spec=pltpu.PrefetchScalarGridSpec(
            num_scalar_prefetch=1, grid=(2 * nt,), in_specs=[full, mine_spec, theirs_spec, full, full],
            out_specs=[full] * 4),
        out_shape=[jax.ShapeDtypeStruct(w.shape, F32)] * 4, compiler_params=_cp("arbitrary"),
    )(place, w, mine, theirs, m, v)


def _add_many(xs, ys, name):
    n = len(xs)

    def body(*refs):
        for i in range(n):
            refs[2 * n + i][...] = refs[i][...] + refs[n + i][...]

    return pl.pallas_call(body, name=name, out_shape=[jax.ShapeDtypeStruct(a.shape, F32) for a in xs])(*xs, *ys)


def _adam_small(place, owns, landed, ws, ms, vs, widths):
    n, nw = len(owns), len(ws)

    def body(pl_ref, *refs):
        own_r, land_r = refs[:n], refs[n:2 * n]
        w_r, m_r, v_r = (refs[2 * n + k * nw:2 * n + (k + 1) * nw] for k in range(3))
        outs = refs[2 * n + 3 * nw:]
        g_o, d_o, m_o, v_o = outs[:n], outs[n:n + nw], outs[n + nw:n + 2 * nw], outs[n + 2 * nw:]
        for me in range(4):
            @pl.when(pl_ref[0] == me)
            def _(me=me):
                for i in range(n):
                    p = [own_r[i][...] if k == me else land_r[i][k] for k in range(4)]
                    g = ((p[0] + p[1]) + p[2]) + p[3]
                    if i < nw and widths[i]:
                        g = g[:, me * widths[i]:(me + 1) * widths[i]]
                    g_o[i][...] = g
                    if i < nw:
                        d, mn, vn = _adam_math(w_r[i][...], g, m_r[i][...], v_r[i][...])
                        d_o[i][...] = d
                        m_o[i][...] = mn
                        v_o[i][...] = vn

    g_shapes = [jax.ShapeDtypeStruct(ws[i].shape if i < nw else owns[i].shape, F32) for i in range(n)]
    w_shapes = [jax.ShapeDtypeStruct(w.shape, F32) for w in ws]
    whole = lambda a: pl.BlockSpec(a.shape, lambda i, p, nd=len(a.shape): (0,) * nd)
    ins = list(owns) + list(landed) + list(ws) + list(ms) + list(vs)
    out_shape = g_shapes + w_shapes * 3
    out = pl.pallas_call(
        body, name="adam_small",
        grid_spec=pltpu.PrefetchScalarGridSpec(num_scalar_prefetch=1, grid=(1,), in_specs=[whole(a) for a in ins],
                                               out_specs=[whole(a) for a in out_shape]),
        out_shape=out_shape, compiler_params=_cp("arbitrary"),
    )(place, *ins)
    return out[:n], out[n:n + nw], out[n + nw:n + 2 * nw], out[n + 2 * nw:]


def kernel(x, g_mix, w_in, b_gate, w_gk_up, b_gk, w_pool_grp, pool_scale, g_gla_head, w_pool_proj, w_gla_proj, w_out, g_ffn, w_up, w_conv, b_conv, w_down, g_final, loss_target, m_g_mix, m_w_in, m_b_gate, m_w_gk_up, m_b_gk, m_w_pool_grp, m_pool_scale, m_g_gla_head, m_w_pool_proj, m_w_gla_proj, m_w_out, m_g_ffn, m_w_up, m_w_conv, m_b_conv, m_w_down, m_g_final, v_g_mix, v_w_in, v_b_gate, v_w_gk_up, v_b_gk, v_w_pool_grp, v_pool_scale, v_g_gla_head, v_w_pool_proj, v_w_gla_proj, v_w_out, v_g_ffn, v_w_up, v_w_conv, v_b_conv, v_w_down, v_g_final):
    s = x.shape[1]
    ts = min(s, 512)
    tm = min(s, 256)
    cx, cy, cc = lax.axis_index("x"), lax.axis_index("y"), lax.axis_index("c")
    chip = 2 * cx + cy
    place = jnp.stack([chip, cc]).astype(jnp.int32)

    big_names = ("w_in", "w_pool_proj", "w_gla_proj", "w_out", "w_up", "w_down")
    axes = (1, 0, 0, 0, 0, 0)
    shards = dict(w_in=jnp.transpose(w_in[0]), w_pool_proj=w_pool_proj[0], w_gla_proj=w_gla_proj[0], w_out=w_out[0],
                  w_up=w_up[0], w_down=w_down[0])
    def fill_own(lands, mine):
        return [lax.dynamic_update_slice(g, o_[None], (chip, 0, 0)) for g, o_ in zip(lands, mine)]

    def gather_start(tag, halves, group_axes, whole, after):
        plan = _gather_plan([o_.shape for o_ in halves], group_axes, len(whole))
        srcs = list(halves) + list(whole)
        handle, token = _split_start("gather_" + tag + "_start", srcs, [((4,) + o_.shape, o_.dtype) for o_ in srcs], plan,
                                     3 * len(srcs), after)
        return (handle, plan, len(halves), len(srcs), group_axes), token

    def gather_finish(tag, started, after):
        handle, plan, n_halves, n, group_axes = started
        mine, lands = _split_wait("gather_" + tag + "_wait", handle, n, plan, after)
        lands[:n_halves] = _gather_share(lands[:n_halves], group_axes, "gather_" + tag + "_share")
        return fill_own(lands, mine)

    in_w, tok = gather_start("in", [shards["w_in"].astype(BF16)], axes[:1], [], g_mix)
    zero = tok[0, 0]
    own = [(shards[n] + zero).astype(BF16) for n in big_names[1:]]
    mix_w, tok = gather_start("mix", own[0:3], axes[1:4], [w_gk_up[0] + zero, w_conv[0] + zero], tok)
    up_w, tok = gather_start("up", own[3:4], axes[4:5], [], tok)
    down_w, tok = gather_start("down", own[4:5], axes[5:6], [], tok)

    def forward_start(tag, started, after):
        handle, plan, _, n, group_axes = started
        mine, lands = _split_wait("gather_" + tag + "_wait", handle, n, plan, after)
        plan = _share_plan([o_.shape for o_ in mine], group_axes)
        share, token = _split_start("gather_" + tag + "_share_start", [], lands, plan, 3 * n, after)
        return (share, plan, mine), token

    def forward_done(tag, forwarded, after):
        share, plan, mine = forwarded
        return fill_own(_split_wait("gather_" + tag + "_share_wait", share, 0, plan, after)[1], mine)
    xs, tgt = x[0], loss_target[0]
    wgrp = w_pool_grp[0]
    h = _rmsnorm(xs, g_mix, tok, "norm_mix", ts)
    m_in_t, v_in_t = jnp.transpose(m_w_in[0]), jnp.transpose(v_w_in[0])
    h, m_in_t, v_in_t = lax.optimization_barrier((h, m_in_t, v_in_t))
    w_in_t = gather_finish("in", in_w, h)[0].reshape(N_IN, D)
    nsh = N_IN // 4

    zr = _in_proj(h, w_in_t, 1152)
    p, pp = _pool_fwd(zr, wgrp, pool_scale)
    wpp, wgla, wout, wgk4, wconv4 = gather_finish("mix", mix_w, pp)
    wgla, wout = wgla.reshape(D, D), wout.reshape(D, D)
    wgk_full = jnp.transpose(wgk4, (1, 0, 2)).reshape(GATE_RANK, 512)
    wconv_full = jnp.transpose(wconv4, (1, 0, 2)).reshape(3, N_UP)
    wgk_pad = jnp.concatenate([wgk_full, jnp.zeros((128 - GATE_RANK, 512), F32)], axis=0)
    o, og, sp = _gla_fwd(zr, wgk_pad, b_gk, g_gla_head, ts)
    up_f, tok = forward_start("up", up_w, og)
    x1, mixed, yp, yg, h2 = _merge_fwd(xs, zr, pp, og, b_gate, wpp, wgla, wout, g_ffn, tok, ts)
    wup, = forward_done("up", up_f, x1)
    down_f, tok = forward_start("down", down_w, x1)
    u = _matmul_resident(h2, wup, tok, "ffn_up")
    wdown = forward_done("down", down_f, u)[0].reshape(D_FF, D)
    a, conv_out, dx2, dx2b, loss_part, dgfin = _ffn_down_loss(u, x1, tgt, wconv_full, b_conv, wdown,
                                                              g_final.reshape(1, D), tm)

    du, dbconv, dwconv = _ffn_bwd(dx2b, u, conv_out, wconv_full, wdown, tm)
    dw_down = _matmul_tn(a, dx2b, "dw_down", D, tm=1408)
    dw_up = _matmul_tn(h2, du, "dw_up", 1408, shard_major=True)

    def exchange_start(tag, grads, group_axes, after):
        plan = _sibling_plan([g.shape[1:] for g in grads], group_axes)
        lands = [((4,) + _half_shape(g.shape[1:], ax), g.dtype) for g, ax in zip(grads, group_axes)]
        handle, token = _split_start("sibling_" + tag + "_start", grads, lands, plan, len(grads), after)
        return (handle, plan, len(grads)), token

    def partials(tag, names, group_axes, exchange, after):
        handle, plan, n = exchange
        mine, theirs = _split_wait("sibling_" + tag + "_wait", handle, n, plan, after)
        return zip(*[_chip_partial(place, g, t, ax, "chip_partial_" + nm)
                     for nm, ax, g, t in zip(names, group_axes, mine, theirs)])

    ffn_names, ffn_axes = ("w_up", "w_down"), (0, 0)
    ffn_x, token = exchange_start("ffn", [dw_up, dw_down.reshape(4, 704, D)], ffn_axes, du)
    dx1, dx1b, dgffn = _matmul_nt_normbwd(du, wup, x1, g_ffn, dx2, token, "ffn_up_bwd", ts)
    ffn_pf, ffn_pb = partials("ffn", ffn_names, ffn_axes, ffn_x, dx1b)
    ffn_plan = _reduce_plan(2, 0)
    ffn_handle, token = _split_start("reduce_ffn_start", ffn_pb, [((3,) + p.shape[1:], BF16) for p in ffn_pb],
                                     ffn_plan, 6, ffn_pf[0])

    dzg, dyp, dyg, dpp, do, dzog, dbgate, dghead = _merge_bwd(dx1b, zr, yp, yg, o, b_gate, g_gla_head, wpp, wgla, wout,
                                                             token, ts)
    dw_out = _matmul_tn(mixed, dx1b, "dw_out", D, tm=512)
    dw_gla = _matmul_tn(og, dyg, "dw_gla", D, tm=512)
    dw_pp = _matmul_tn(pp, dyp, "dw_pp", 256, shard_major=True)

    out_names, out_axes = ("w_pool_proj", "w_gla_proj", "w_out"), (0, 0, 0)
    out_x, token = exchange_start("out", [dw_pp, dw_gla.reshape(4, 256, D), dw_out.reshape(4, 256, D)], out_axes, dpp)
    dzp, dwgrp, dscale = _pool_bwd(p, dpp, wgrp, pool_scale, token)
    out_pf, out_pb = partials("out", out_names, out_axes, out_x, dzp)
    out_plan = _reduce_plan(3, 0)
    out_handle, token = _split_start("reduce_out_start", out_pb, [((3,) + p_.shape[1:], BF16) for p_ in out_pb],
                                     out_plan, 9, out_pf[0])
    dq, dk, dv, dgpre = _gla_bwd(zr, do, sp, wgk_pad, b_gk, token, ts)
    dzgk, dwgk, dbgk = _gk_bwd(dgpre, zr, wgk_pad, dgpre, ts)
    dzr = jnp.concatenate([dzg, dv, dzog, dzp, dq, dk, dzgk], axis=1)
    dw_rt = _matmul_tn(dzr, h, "dw_in", D, tm=1152)

    def grad_rows(lo, hi):
        out = []
        for seg_lo, seg_hi, at in ((0, 1536, OFF_POOL), (1536, 3584, OFF_V), (3584, 3600, OFF_GK), (3600, N_IN, OFF_GATE)):
            a_, b_ = max(lo, seg_lo), min(hi, seg_hi)
            if a_ < b_:
                out.append(dw_rt[at + a_ - seg_lo:at + b_ - seg_lo])
        return jnp.concatenate(out, axis=0)

    dw_in_t = jnp.stack([grad_rows(j * nsh, (j + 1) * nsh) for j in range(4)])

    in_sib = _sibling_exchange([dw_in_t], (1,), [], "sibling_exchange_in")
    in_pf, in_pb = _chip_partial(place, dw_in_t, in_sib[0], 1, "chip_partial_w_in")
    in_plan = _reduce_plan(1, 0)
    in_handle, token = _split_start("reduce_in_start", [in_pb], [((3,) + in_pb.shape[1:], BF16)], in_plan, 3, in_pf)
    grad_x, _, dgmix = _matmul_nt_normbwd(dzr, w_in_t, xs, g_mix, dx1, token, "in_proj_bwd", ts, transposed=True)
    small_names = ("g_mix", "b_gate", "w_gk_up", "b_gk", "w_pool_grp", "pool_scale", "g_gla_head", "g_ffn", "w_conv",
                   "b_conv", "g_final")
    small_mine = [dgmix, dbgate, dwgk[:GATE_RANK], dbgk, dwgrp.reshape(4 * 128, 128), dscale, dghead, dgffn, dwconv, dbconv,
                  dgfin, loss_part]
    small_sib = _sibling_exchange([], (), small_mine, "sibling_exchange_small")
    small_chip = _add_many(small_mine, small_sib, "chip_partial_small")
    small_plan = _reduce_plan(0, len(small_chip))
    small_handle, token = _split_start("reduce_small_start", small_chip, [((4,) + a_.shape, F32) for a_ in small_chip],
                                       small_plan, 3 * len(small_chip), small_mine[0])

    ms = dict(w_in=m_in_t, w_pool_proj=m_w_pool_proj[0], w_gla_proj=m_w_gla_proj[0], w_out=m_w_out[0],
              w_up=m_w_up[0], w_down=m_w_down[0])
    vs = dict(w_in=v_in_t, w_pool_proj=v_w_pool_proj[0], w_gla_proj=v_w_gla_proj[0], w_out=v_w_out[0],
              w_up=v_w_up[0], w_down=v_w_down[0])
    grad, delta, new_m, new_v = {}, {}, {}, {}

    def finish_and_update(names, group_axes, part_f, landed, tag):
        halves = [_finish_half(pf, rb, ax, "finish_" + n) for n, ax, pf, rb in zip(names, group_axes, part_f, landed)]
        sib_halves = _sibling_share(halves, "sibling_share_" + tag)
        for n, ax, mine, theirs in zip(names, group_axes, halves, sib_halves):
            res = _adam_halves(place, shards[n], mine, theirs, ms[n], vs[n], ax, "adam_" + n)
            if n == "w_in":
                res = [jnp.transpose(r_) for r_ in res]
            grad[n], delta[n], new_m[n], new_v[n] = [r_[None] for r_ in res]

    _, ffn_landed = _split_wait("reduce_ffn_wait", ffn_handle, 2, ffn_plan, token)
    _, out_landed = _split_wait("reduce_out_wait", out_handle, 3, out_plan, ffn_landed[0])
    finish_and_update(ffn_names + out_names, ffn_axes + out_axes, ffn_pf + out_pf, ffn_landed + out_landed, "rest")
    _, in_landed = _split_wait("reduce_in_wait", in_handle, 1, in_plan, delta["w_out"])
    finish_and_update(("w_in",), (1,), (in_pf,), in_landed, "in")
    small_sent, small_landed = _split_wait("reduce_small_wait", small_handle, len(small_chip), small_plan, delta["w_in"])
    given = dict(g_mix=(g_mix, m_g_mix, v_g_mix), b_gate=(b_gate, m_b_gate, v_b_gate), w_gk_up=(w_gk_up, m_w_gk_up, v_w_gk_up),
                 b_gk=(b_gk, m_b_gk, v_b_gk), w_pool_grp=(w_pool_grp, m_w_pool_grp, v_w_pool_grp),
                 pool_scale=(pool_scale, m_pool_scale, v_pool_scale), g_gla_head=(g_gla_head, m_g_gla_head, v_g_gla_head),
                 g_ffn=(g_ffn, m_g_ffn, v_g_ffn), w_conv=(w_conv, m_w_conv, v_w_conv), b_conv=(b_conv, m_b_conv, v_b_conv),
                 g_final=(g_final, m_g_final, v_g_final))
    flat2 = lambda a: a.reshape(-1, a.shape[-1])
    widths = [dict(w_gk_up=128, w_conv=1408).get(n) for n in small_names]
    totals, ds, mo, vo = _adam_small(place, small_sent, small_landed, *[[flat2(given[n][k]) for n in small_names] for k in range(3)],
                                     widths)
    loss = totals[-1][0, 0]
    for i, n in enumerate(small_names):
        shp = given[n][0].shape
        grad[n], delta[n], new_m[n], new_v[n] = [r_.reshape(shp) for r_ in (totals[i], ds[i], mo[i], vo[i])]

    order = ("g_mix", "w_in", "b_gate", "w_gk_up", "b_gk", "w_pool_grp", "pool_scale", "g_gla_head", "w_pool_proj",
             "w_gla_proj", "w_out", "g_ffn", "w_up", "w_conv", "b_conv", "w_down", "g_final")
    return (loss, grad_x[None], *[grad[n] for n in order], *[delta[n] for n in order], *[new_m[n] for n in order],
            *[new_v[n] for n in order])
```

```python
import functools

import jax
import jax.numpy as jnp
from jax import lax
from jax.experimental import pallas as pl
from jax.experimental.pallas import tpu as pltpu

F32 = jnp.float32
BF16 = jnp.bfloat16
MESH = pl.DeviceIdType.MESH

D = 1024
EPS = 1e-6
CHUNK = 64
POOL_W = 512
POOL_WINDOWS = (2, 4, 8, 16)
HEADS = 4
HK = 128
HV = 256
GATE_RANK = 16
D_FF = 2816
N_UP = 2 * D_FF
N_IN = 5648
QSCALE = HK ** -0.5
N_INR = 5760
OFF_GATE, OFF_V, OFF_OG, OFF_POOL, OFF_Q, OFF_K, OFF_GK = 0, 2048, 3072, 4096, 4608, 5120, 5632

ADAM_LR, ADAM_B1, ADAM_B2, ADAM_EPS, ADAM_WD, ADAM_STEP = 0.001, 0.9, 0.999, 1e-08, 0.01, 10

VMEM_LIMIT = 56 * 1024 * 1024


def _cp(*sem):
    return pltpu.CompilerParams(dimension_semantics=sem if sem else None, vmem_limit_bytes=VMEM_LIMIT)


def _dot(a, b):
    return jnp.dot(a, b, preferred_element_type=F32)


def _dot_nt(a, b):
    return lax.dot_general(a, b, (((1,), (1,)), ((), ())), preferred_element_type=F32)


def _dot_tn(a, b):
    return lax.dot_general(a, b, (((0,), (0,)), ((), ())), preferred_element_type=F32)


def _sigmoid(v):
    return 1.0 / (1.0 + jnp.exp(-v))


def _rows(shape):
    return lax.broadcasted_iota(jnp.int32, shape, 0)


def _pick_row(v, r):
    return jnp.sum(jnp.where(_rows(v.shape) == r, v, 0.0), axis=0, keepdims=True)


def _rmsnorm(x, g, after, name, ts):
    s = x.shape[0]

    def body(x_ref, g_ref, after_ref, h_ref):
        xv = x_ref[...]
        r = lax.rsqrt(jnp.mean(xv * xv, axis=-1, keepdims=True) + EPS)
        h_ref[...] = (xv * r * g_ref[...]).astype(BF16)

    return pl.pallas_call(
        body, name=name, grid=(s // ts,),
        in_specs=[pl.BlockSpec((ts, D), lambda i: (i, 0)), pl.BlockSpec((1, D), lambda i: (0, 0)), ANY],
        out_specs=pl.BlockSpec((ts, D), lambda i: (i, 0)), out_shape=jax.ShapeDtypeStruct((s, D), BF16),
        compiler_params=_cp("arbitrary"),
    )(x, g, after)


MM_ROWS = 512


def _matmul_resident(h, w, after, name):
    s = h.shape[0]
    nj, tn = w.shape[0], w.shape[2]
    rc = min(s, MM_ROWS)

    def body(h_ref, w_ref, after_ref, z_ref):
        for r0 in range(0, s, rc):
            z_ref[r0:r0 + rc, :] = _dot(h_ref[r0:r0 + rc, :], w_ref[...]).astype(BF16)

    return pl.pallas_call(
        body, name=name, grid=(nj,),
        in_specs=[pl.BlockSpec((s, D), lambda j: (0, 0)), pl.BlockSpec((None, D, tn), lambda j: (j, 0, 0)), ANY],
        out_specs=pl.BlockSpec((s, tn), lambda j: (0, j)), out_shape=jax.ShapeDtypeStruct((s, nj * tn), BF16),
        compiler_params=_cp("arbitrary"),
    )(h, w, after)


PROJ_PIECES = ((3600, 2048, OFF_GATE), (1536, 2048, OFF_V), (0, 1536, OFF_POOL), (3584, GATE_RANK, OFF_GK))


def _projection_copies(w_hbm, w_ref, sems):
    return [pltpu.make_async_copy(w_hbm.at[pl.ds(src, n)], w_ref.at[pl.ds(dst, n)], sems.at[i])
            for i, (src, n, dst) in enumerate(PROJ_PIECES)]


def _load_projection(w_hbm, w_ref, sems):
    cps = _projection_copies(w_hbm, w_ref, sems)
    for cp in cps:
        cp.start()
    w_ref[OFF_GK + GATE_RANK:, :] = jnp.zeros((N_INR - OFF_GK - GATE_RANK, D), BF16)
    for cp in cps:
        cp.wait()


def _in_proj(h, w_nat, tn):
    s = h.shape[0]
    rc = min(s, MM_ROWS)
    nj = N_INR // tn
    first_use = [dst // tn for _, _, dst in PROJ_PIECES]

    def body(h_ref, w_hbm, z_ref, w_ref, sems):
        j = pl.program_id(0)
        cps = _projection_copies(w_hbm, w_ref, sems)

        @pl.when(j == 0)
        def _():
            for cp in cps:
                cp.start()
            w_ref[OFF_GK + GATE_RANK:, :] = jnp.zeros((N_INR - OFF_GK - GATE_RANK, D), BF16)

        for step in range(nj):
            due = [cp for cp, at in zip(cps, first_use) if at == step]
            if due:
                @pl.when(j == step)
                def _(due=due):
                    for cp in due:
                        cp.wait()

        wt = w_ref[pl.ds(pl.multiple_of(j * tn, 128), tn), :]
        for r0 in range(0, s, rc):
            z_ref[r0:r0 + rc, :] = _dot_nt(h_ref[r0:r0 + rc, :], wt).astype(BF16)

    return pl.pallas_call(
        body, name="in_proj", grid=(nj,),
        in_specs=[pl.BlockSpec((s, D), lambda j: (0, 0)), ANY],
        out_specs=pl.BlockSpec((s, tn), lambda j: (0, j)), out_shape=jax.ShapeDtypeStruct((s, N_INR), BF16),
        scratch_shapes=[pltpu.VMEM((N_INR, D), BF16), pltpu.SemaphoreType.DMA((len(PROJ_PIECES),))],
        compiler_params=_cp("arbitrary"),
    )(h, w_nat)


def _matmul_nt_normbwd(dz, w, x, g, resid, after, name, ts, transposed=False):
    s = x.shape[0]
    w_vmem = (N_INR, D) if transposed else (D, w.shape[0] * w.shape[2])
    n_sems = len(PROJ_PIECES) if transposed else w.shape[0]

    def body(dz_ref, w_hbm, x_ref, g_ref, r_ref, after_ref, o_ref, ob_ref, dg_ref, w_ref, sems):
        @pl.when(pl.program_id(0) == 0)
        def _():
            if transposed:
                _load_projection(w_hbm, w_ref, sems)
            else:
                kc = w.shape[2]
                cps = [pltpu.make_async_copy(w_hbm.at[j], w_ref.at[:, pl.ds(j * kc, kc)], sems.at[j])
                       for j in range(w.shape[0])]
                for cp in cps:
                    cp.start()
                for cp in cps:
                    cp.wait()
            dg_ref[...] = jnp.zeros_like(dg_ref)

        dh = _dot(dz_ref[...], w_ref[...]) if transposed else _dot_nt(dz_ref[...], w_ref[...])
        xv = x_ref[...]
        r = lax.rsqrt(jnp.mean(xv * xv, axis=-1, keepdims=True) + EPS)
        xh = xv * r
        dg_ref[...] += jnp.sum(dh * xh, axis=0, keepdims=True)
        dxh = dh * g_ref[...]
        out = r_ref[...] + r * (dxh - xh * jnp.mean(dxh * xh, axis=-1, keepdims=True))
        o_ref[...] = out
        ob_ref[...] = out.astype(BF16)

    row = lambda i: (i, 0)
    kdim = dz.shape[1]
    return pl.pallas_call(
        body, name=name, grid=(s // ts,),
        in_specs=[pl.BlockSpec((ts, kdim), row), ANY, pl.BlockSpec((ts, D), row),
                  pl.BlockSpec((1, D), lambda i: (0, 0)), pl.BlockSpec((ts, D), row), ANY],
        out_specs=[pl.BlockSpec((ts, D), row), pl.BlockSpec((ts, D), row), pl.BlockSpec((1, D), lambda i: (0, 0))],
        out_shape=[jax.ShapeDtypeStruct((s, D), F32), jax.ShapeDtypeStruct((s, D), BF16),
                   jax.ShapeDtypeStruct((1, D), F32)],
        scratch_shapes=[pltpu.VMEM(w_vmem, BF16), pltpu.SemaphoreType.DMA((n_sems,))],
        compiler_params=_cp("arbitrary"),
    )(dz, w, x, g, resid, after)


def _matmul_tn(a, b, name, tn, shard_major=False, tm=None):
    s, m = a.shape
    n = b.shape[1]
    tm = m if tm is None else tm
    ni, nj = m // tm, n // tn

    def body(a_ref, b_ref, o_ref):
        o_ref[...] = _dot_tn(a_ref[...], b_ref[...]).astype(BF16)

    if shard_major:
        out_spec = pl.BlockSpec((None, tm, tn), lambda i, j: (j, i, 0))
        out_shape = jax.ShapeDtypeStruct((nj, m, tn), BF16)
    else:
        out_spec = pl.BlockSpec((tm, tn), lambda i, j: (i, j))
        out_shape = jax.ShapeDtypeStruct((m, n), BF16)
    return pl.pallas_call(
        body, name=name, grid=(ni, nj),
        in_specs=[pl.BlockSpec((s, tm), lambda i, j: (0, i)), pl.BlockSpec((s, tn), lambda i, j: (0, j))],
        out_specs=out_spec, out_shape=out_shape,
        compiler_params=_cp("arbitrary", "arbitrary"),
    )(a, b)


def _pool_fwd(zr, wgrp, scale):
    s = zr.shape[0]

    def body(u_ref, w_ref, sc_ref, p_ref, pp_ref):
        row = _rows((s, 128))
        for gi, win in enumerate(POOL_WINDOWS):
            cs = slice(gi * 128, (gi + 1) * 128)
            u = u_ref[:, cs].astype(F32)
            acc, k = u, 1
            while k < win:
                acc = acc + jnp.where(row >= k, pltpu.roll(acc, k, 0), 0.0)
                k *= 2
            cnt = jnp.minimum(row + 1, win).astype(F32)
            p = (acc / cnt - u).astype(BF16)
            p_ref[:, cs] = p
            pp_ref[:, cs] = (_dot(p, w_ref[gi].astype(BF16)) * sc_ref[:, cs]).astype(BF16)

    return pl.pallas_call(
        body, name="pool_fwd", grid=(1,),
        in_specs=[pl.BlockSpec((s, POOL_W), lambda i: (0, OFF_POOL // POOL_W)),
                  pl.BlockSpec((4, 128, 128), lambda i: (0, 0, 0)), pl.BlockSpec((1, POOL_W), lambda i: (0, 0))],
        out_specs=[pl.BlockSpec((s, POOL_W), lambda i: (0, 0))] * 2,
        out_shape=[jax.ShapeDtypeStruct((s, POOL_W), BF16)] * 2,
        compiler_params=_cp("arbitrary"),
    )(zr, wgrp, scale)


def _pool_bwd(p, dpp, wgrp, scale, after):
    s = p.shape[0]

    def body(p_ref, dpp_ref, w_ref, sc_ref, after_ref, dz_ref, dw_ref, dsc_ref):
        row = _rows((s, 128))
        for gi, win in enumerate(POOL_WINDOWS):
            cs = slice(gi * 128, (gi + 1) * 128)
            pv = p_ref[:, cs]
            wb = w_ref[gi].astype(BF16)
            dpp_v = dpp_ref[:, cs].astype(F32)
            dsc_ref[:, cs] = jnp.sum(dpp_v * _dot(pv, wb), axis=0, keepdims=True)
            dpm = (dpp_v * sc_ref[:, cs]).astype(BF16)
            dw_ref[gi] = _dot_tn(pv, dpm)
            dp = _dot_nt(dpm, wb)
            cnt = jnp.minimum(row + 1, win).astype(F32)
            acc, k = dp / cnt, 1
            while k < win:
                acc = acc + jnp.where(row < s - k, pltpu.roll(acc, s - k, 0), 0.0)
                k *= 2
            dz_ref[:, cs] = (acc - dp).astype(BF16)

    full = lambda i: (0, 0)
    return pl.pallas_call(
        body, name="pool_bwd", grid=(1,),
        in_specs=[pl.BlockSpec((s, POOL_W), full), pl.BlockSpec((s, POOL_W), full),
                  pl.BlockSpec((4, 128, 128), lambda i: (0, 0, 0)), pl.BlockSpec((1, POOL_W), full), ANY],
        out_specs=[pl.BlockSpec((s, POOL_W), full), pl.BlockSpec((4, 128, 128), lambda i: (0, 0, 0)),
                   pl.BlockSpec((1, POOL_W), full)],
        out_shape=[jax.ShapeDtypeStruct((s, POOL_W), BF16), jax.ShapeDtypeStruct((4, 128, 128), F32),
                   jax.ShapeDtypeStruct((1, POOL_W), F32)],
        compiler_params=_cp("arbitrary"),
    )(p, dpp, wgrp, scale, after)


def _gla_decay(zgk_ref, wgk_ref, bgk_ref, rb):
    g = _dot(zgk_ref[...], wgk_ref[...].astype(BF16)) + bgk_ref[...]
    la = (jnp.minimum(g, 0.0) - jnp.log(1.0 + jnp.exp(-jnp.abs(g)))) * (1.0 / 16.0)
    rowm = _rows(la.shape) & (CHUNK - 1)
    bc, k = la, 1
    while k < CHUNK:
        bc = bc + jnp.where(rowm >= k, pltpu.roll(bc, k, 0), 0.0)
        k *= 2
    return g, jnp.exp(bc), jnp.exp(-bc)


GLA_HB = 4


def _gla_specs(rb, rmap):
    wk, wv = GLA_HB * HK, GLA_HB * HV
    return [pl.BlockSpec((rb, wk), lambda h, r: (rmap(h, r), OFF_Q // wk + h)),
            pl.BlockSpec((rb, wk), lambda h, r: (rmap(h, r), OFF_K // wk + h)),
            pl.BlockSpec((rb, wv), lambda h, r: (rmap(h, r), OFF_V // wv + h)),
            pl.BlockSpec((rb, 128), lambda h, r: (rmap(h, r), OFF_GK // 128))]


def _gla_fwd(zr, wgk, bgk, ghead, rb):
    s = zr.shape[0]
    nc = rb // CHUNK
    wk, wv = GLA_HB * HK, GLA_HB * HV

    def body(q_ref, k_ref, v_ref, zgk_ref, zog_ref, wgk_ref, bgk_ref, gh_ref, o_ref, og_ref, sp_ref, st_ref):
        @pl.when(pl.program_id(1) == 0)
        def _():
            st_ref[...] = jnp.zeros_like(st_ref)

        _, e_pos, e_neg = _gla_decay(zgk_ref, wgk_ref, bgk_ref, rb)
        lower = _rows((CHUNK, CHUNK)) >= lax.broadcasted_iota(jnp.int32, (CHUNK, CHUNK), 1)
        for c in range(nc):
            sl = slice(c * CHUNK, (c + 1) * CHUNK)
            for hh in range(GLA_HB):
                ck, cv = slice(hh * HK, (hh + 1) * HK), slice(hh * HV, (hh + 1) * HV)
                q = q_ref[sl, ck].astype(F32) * QSCALE
                k = k_ref[sl, ck].astype(F32)
                v = v_ref[sl, cv]
                ec, fc = e_pos[sl, ck], e_neg[sl, ck]
                qfw = (q * ec).astype(BF16)
                kfw_f = k * fc
                s_fw = _dot_nt(qfw, kfw_f.astype(BF16))
                s_bw = _dot_nt((q * fc).astype(BF16), (k * ec).astype(BF16))
                pm = jnp.where(lower, s_fw, s_bw).astype(BF16)
                st = st_ref[hh]
                stb = st.astype(BF16)
                sp_ref[c, hh] = stb
                o = _dot(pm, v) + _dot_nt(qfw, stb)
                e_last = _pick_row(ec, CHUNK - 1)
                kdec = (kfw_f * e_last).astype(BF16)
                st_ref[hh] = st * e_last + _dot_tn(v, kdec)
                r = lax.rsqrt(jnp.mean(o * o, axis=-1, keepdims=True) + EPS)
                zo = zog_ref[sl, cv].astype(F32)
                o_ref[sl, cv] = o.astype(BF16)
                og_ref[sl, cv] = (o * r * gh_ref[...] * zo * _sigmoid(zo)).astype(BF16)

    rmap = lambda h, r: r
    return pl.pallas_call(
        body, name="gla_fwd", grid=(HEADS // GLA_HB, s // rb),
        in_specs=_gla_specs(rb, rmap) + [
            pl.BlockSpec((rb, wv), lambda h, r: (r, OFF_OG // wv + h)),
            pl.BlockSpec((128, wk), lambda h, r: (0, h)), pl.BlockSpec((1, wk), lambda h, r: (0, h)),
            pl.BlockSpec((1, HV), lambda h, r: (0, 0))],
        out_specs=[pl.BlockSpec((rb, wv), lambda h, r: (r, h)), pl.BlockSpec((rb, wv), lambda h, r: (r, h)),
                   pl.BlockSpec((nc, GLA_HB, HV, HK), lambda h, r: (r, h, 0, 0))],
        out_shape=[jax.ShapeDtypeStruct((s, D), BF16), jax.ShapeDtypeStruct((s, D), BF16),
                   jax.ShapeDtypeStruct((s // CHUNK, HEADS, HV, HK), BF16)],
        scratch_shapes=[pltpu.VMEM((GLA_HB, HV, HK), F32)],
        compiler_params=_cp("arbitrary", "arbitrary"),
    )(zr, zr, zr, zr, zr, wgk, bgk, ghead)


def _gla_bwd(zr, do, sp, wgk, bgk, after, rb):
    s = zr.shape[0]
    nc = rb // CHUNK
    nr = s // rb
    wk, wv = GLA_HB * HK, GLA_HB * HV

    def body(q_ref, k_ref, v_ref, zgk_ref, do_ref, sp_ref, wgk_ref, bgk_ref, after_ref, dq_ref, dk_ref, dv_ref, dg_ref,
             gt_ref, dbc_ref):
        @pl.when(pl.program_id(1) == 0)
        def _():
            gt_ref[...] = jnp.zeros_like(gt_ref)

        g, e_pos, e_neg = _gla_decay(zgk_ref, wgk_ref, bgk_ref, rb)
        lower = _rows((CHUNK, CHUNK)) >= lax.broadcasted_iota(jnp.int32, (CHUNK, CHUNK), 1)
        is_last = _rows((CHUNK, HK)) == CHUNK - 1
        for c in reversed(range(nc)):
            sl = slice(c * CHUNK, (c + 1) * CHUNK)
            for hh in range(GLA_HB):
                ck, cv = slice(hh * HK, (hh + 1) * HK), slice(hh * HV, (hh + 1) * HV)
                q = q_ref[sl, ck].astype(F32) * QSCALE
                k = k_ref[sl, ck].astype(F32)
                v = v_ref[sl, cv]
                dov = do_ref[sl, cv]
                ec, fc = e_pos[sl, ck], e_neg[sl, ck]
                qfw_f, kfw_f, qbw_f, kbw_f = q * ec, k * fc, q * fc, k * ec
                qfw, kfw, qbw, kbw = qfw_f.astype(BF16), kfw_f.astype(BF16), qbw_f.astype(BF16), kbw_f.astype(BF16)
                pm = jnp.where(lower, _dot_nt(qfw, kfw), _dot_nt(qbw, kbw)).astype(BF16)
                e_last = _pick_row(ec, CHUNK - 1)
                kdec = (kfw_f * e_last).astype(BF16)
                gt = gt_ref[hh]
                gtb = gt.astype(BF16)
                spv = sp_ref[c, hh]
                dp = _dot_nt(dov, v)
                dv_ref[sl, cv] = (_dot_tn(pm, dov) + _dot_nt(kdec, gtb)).astype(BF16)
                ds_fw = jnp.where(lower, dp, 0.0).astype(BF16)
                ds_bw = jnp.where(lower, 0.0, dp).astype(BF16)
                dqfw = _dot(ds_fw, kfw) + _dot(dov, spv)
                dkfw = _dot_tn(ds_fw, qfw)
                dqbw = _dot(ds_bw, kbw)
                dkbw = _dot_tn(ds_bw, qbw)
                dkdec = _dot(v, gtb)
                de_last = (jnp.sum(gt * spv.astype(F32), axis=0, keepdims=True)
                           + jnp.sum(dkdec * kfw_f, axis=0, keepdims=True))
                dkfw = dkfw + dkdec * e_last
                dq_ref[sl, ck] = ((dqfw * ec + dqbw * fc) * QSCALE).astype(BF16)
                dk_ref[sl, ck] = (dkfw * fc + dkbw * ec).astype(BF16)
                dbc = dqfw * qfw_f - dqbw * qbw_f + dkbw * kbw_f - dkfw * kfw_f
                dbc_ref[sl, ck] = dbc + jnp.where(is_last, de_last * e_last, 0.0)
                gt_ref[hh] = _dot_tn(dov, qfw) + gt * e_last
        rowm = _rows((rb, wk)) & (CHUNK - 1)
        dla, kk = dbc_ref[...], 1
        while kk < CHUNK:
            dla = dla + jnp.where(rowm < CHUNK - kk, pltpu.roll(dla, rb - kk, 0), 0.0)
            kk *= 2
        dg_ref[...] = dla * (1.0 / 16.0) * _sigmoid(-g)

    rmap = lambda h, r: nr - 1 - r
    rev = lambda h, r: (nr - 1 - r, h)
    return pl.pallas_call(
        body, name="gla_bwd", grid=(HEADS // GLA_HB, nr),
        in_specs=_gla_specs(rb, rmap) + [
            pl.BlockSpec((rb, wv), rev),
            pl.BlockSpec((nc, GLA_HB, HV, HK), lambda h, r: (nr - 1 - r, h, 0, 0)),
            pl.BlockSpec((128, wk), lambda h, r: (0, h)), pl.BlockSpec((1, wk), lambda h, r: (0, h)), ANY],
        out_specs=[pl.BlockSpec((rb, wk), rev), pl.BlockSpec((rb, wk), rev), pl.BlockSpec((rb, wv), rev),
                   pl.BlockSpec((rb, wk), rev)],
        out_shape=[jax.ShapeDtypeStruct((s, HEADS * HK), BF16), jax.ShapeDtypeStruct((s, HEADS * HK), BF16),
                   jax.ShapeDtypeStruct((s, D), BF16), jax.ShapeDtypeStruct((s, HEADS * HK), F32)],
        scratch_shapes=[pltpu.VMEM((GLA_HB, HV, HK), F32), pltpu.VMEM((rb, wk), F32)],
        compiler_params=_cp("arbitrary", "arbitrary"),
    )(zr, zr, zr, zr, do, sp, wgk, bgk, after)


def _gk_bwd(dgpre, zr, wgk, after, ts):
    s = zr.shape[0]

    def body(dg_ref, zgk_ref, w_ref, after_ref, dz_ref, dw_ref, db_ref):
        @pl.when(pl.program_id(0) == 0)
        def _():
            dw_ref[...] = jnp.zeros_like(dw_ref)
            db_ref[...] = jnp.zeros_like(db_ref)

        dg = dg_ref[...]
        dgb = dg.astype(BF16)
        dz_ref[...] = _dot_nt(dgb, w_ref[...].astype(BF16)).astype(BF16)
        dw_ref[...] += _dot_tn(zgk_ref[...], dgb)
        db_ref[...] += jnp.sum(dg, axis=0, keepdims=True)

    return pl.pallas_call(
        body, name="gk_bwd", grid=(s // ts,),
        in_specs=[pl.BlockSpec((ts, 512), lambda i: (i, 0)), pl.BlockSpec((ts, 128), lambda i: (i, OFF_GK // 128)),
                  pl.BlockSpec((128, 512), lambda i: (0, 0)), ANY],
        out_specs=[pl.BlockSpec((ts, 128), lambda i: (i, 0)), pl.BlockSpec((128, 512), lambda i: (0, 0)),
                   pl.BlockSpec((1, 512), lambda i: (0, 0))],
        out_shape=[jax.ShapeDtypeStruct((s, 128), BF16), jax.ShapeDtypeStruct((128, 512), F32),
                   jax.ShapeDtypeStruct((1, 512), F32)],
        compiler_params=_cp("arbitrary"),
    )(dgpre, zr, wgk, after)


def _merge_fwd(x, zr, pp, og, bgate, wpp, wgla, wout, gffn, after, ts):
    s = x.shape[0]

    def body(x_ref, z0_ref, z1_ref, pp_ref, og_ref, bg_ref, wpp_ref, wgla_ref, wout_ref, gf_ref, after_ref,
             x1_ref, mix_ref, yp_ref, yg_ref, h2_ref):
        ppv = pp_ref[...]
        yp = jnp.concatenate([_dot(ppv, wpp_ref[j]) for j in range(4)], axis=1)
        yg = _dot(og_ref[...], wgla_ref[...])
        g0 = _sigmoid(z0_ref[...].astype(F32) + bg_ref[:, :D])
        g1 = _sigmoid(z1_ref[...].astype(F32) + bg_ref[:, D:])
        mixed = (g0 * yp + g1 * yg).astype(BF16)
        x1 = x_ref[...] + _dot(mixed, wout_ref[...])
        x1_ref[...] = x1
        mix_ref[...] = mixed
        yp_ref[...] = yp.astype(BF16)
        yg_ref[...] = yg.astype(BF16)
        r = lax.rsqrt(jnp.mean(x1 * x1, axis=-1, keepdims=True) + EPS)
        h2_ref[...] = (x1 * r * gf_ref[...]).astype(BF16)

    row = lambda i: (i, 0)
    const2 = lambda i: (0, 0)
    return pl.pallas_call(
        body, name="merge_fwd", grid=(s // ts,),
        in_specs=[pl.BlockSpec((ts, D), row), pl.BlockSpec((ts, D), lambda i: (i, 0)), pl.BlockSpec((ts, D), lambda i: (i, 1)),
                  pl.BlockSpec((ts, POOL_W), row), pl.BlockSpec((ts, D), row), pl.BlockSpec((1, 2 * D), const2),
                  pl.BlockSpec((4, POOL_W, 256), lambda i: (0, 0, 0)), pl.BlockSpec((D, D), const2),
                  pl.BlockSpec((D, D), const2), pl.BlockSpec((1, D), const2), ANY],
        out_specs=[pl.BlockSpec((ts, D), row)] * 5,
        out_shape=[jax.ShapeDtypeStruct((s, D), F32)] + [jax.ShapeDtypeStruct((s, D), BF16)] * 4,
        compiler_params=_cp("arbitrary"),
    )(x, zr, zr, pp, og, bgate, wpp, wgla, wout, gffn, after)


def _merge_bwd(dx1b, zr, yp, yg, o, bgate, ghead, wpp, wgla, wout, after, ts):
    s = dx1b.shape[0]

    def body(dx_ref, z0_ref, z1_ref, zog_ref, yp_ref, yg_ref, o_ref, bg_ref, gh_ref, wpp_ref, wgla_ref, wout_ref, after_ref,
             dzg_ref, dyp_ref, dyg_ref, dpp_ref, do_ref, dzog_ref, dbg_ref, dgh_ref):
        @pl.when(pl.program_id(0) == 0)
        def _():
            dbg_ref[...] = jnp.zeros_like(dbg_ref)
            dgh_ref[...] = jnp.zeros_like(dgh_ref)

        dmix = _dot_nt(dx_ref[...], wout_ref[...])
        g0 = _sigmoid(z0_ref[...].astype(F32) + bg_ref[:, :D])
        g1 = _sigmoid(z1_ref[...].astype(F32) + bg_ref[:, D:])
        dypb = (dmix * g0).astype(BF16)
        dygb = (dmix * g1).astype(BF16)
        dz0 = dmix * yp_ref[...].astype(F32) * g0 * (1.0 - g0)
        dz1 = dmix * yg_ref[...].astype(F32) * g1 * (1.0 - g1)
        dzg_ref[:, :D] = dz0.astype(BF16)
        dzg_ref[:, D:] = dz1.astype(BF16)
        dbg_ref[:, :D] += jnp.sum(dz0, axis=0, keepdims=True)
        dbg_ref[:, D:] += jnp.sum(dz1, axis=0, keepdims=True)
        dyp_ref[...] = dypb
        dyg_ref[...] = dygb
        dpp = _dot_nt(dypb[:, 0:256], wpp_ref[0])
        for j in range(1, 4):
            dpp = dpp + _dot_nt(dypb[:, j * 256:(j + 1) * 256], wpp_ref[j])
        dpp_ref[...] = dpp.astype(BF16)
        dog = _dot_nt(dygb, wgla_ref[...])
        gh = gh_ref[...]
        dgh = jnp.zeros((1, HV), F32)
        for h in range(HEADS):
            cs = slice(h * HV, (h + 1) * HV)
            ov = o_ref[:, cs].astype(F32)
            r = lax.rsqrt(jnp.mean(ov * ov, axis=-1, keepdims=True) + EPS)
            oh = ov * r
            zo = zog_ref[:, cs].astype(F32)
            sg = _sigmoid(zo)
            dog_h = dog[:, cs]
            don = dog_h * zo * sg
            dzog_ref[:, cs] = (dog_h * oh * gh * sg * (1.0 + zo * (1.0 - sg))).astype(BF16)
            dgh = dgh + jnp.sum(don * oh, axis=0, keepdims=True)
            doh = don * gh
            do_ref[:, cs] = (r * (doh - oh * jnp.mean(doh * oh, axis=-1, keepdims=True))).astype(BF16)
        dgh_ref[...] += dgh

    row = lambda i: (i, 0)
    const2 = lambda i: (0, 0)
    return pl.pallas_call(
        body, name="merge_bwd", grid=(s // ts,),
        in_specs=[pl.BlockSpec((ts, D), row), pl.BlockSpec((ts, D), lambda i: (i, 0)), pl.BlockSpec((ts, D), lambda i: (i, 1)),
                  pl.BlockSpec((ts, D), lambda i: (i, OFF_OG // D)), pl.BlockSpec((ts, D), row), pl.BlockSpec((ts, D), row),
                  pl.BlockSpec((ts, D), row), pl.BlockSpec((1, 2 * D), const2), pl.BlockSpec((1, HV), const2),
                  pl.BlockSpec((4, POOL_W, 256), lambda i: (0, 0, 0)), pl.BlockSpec((D, D), const2),
                  pl.BlockSpec((D, D), const2), ANY],
        out_specs=[pl.BlockSpec((ts, 2 * D), row), pl.BlockSpec((ts, D), row), pl.BlockSpec((ts, D), row),
                   pl.BlockSpec((ts, POOL_W), row), pl.BlockSpec((ts, D), row), pl.BlockSpec((ts, D), row),
                   pl.BlockSpec((1, 2 * D), const2), pl.BlockSpec((1, HV), const2)],
        out_shape=[jax.ShapeDtypeStruct((s, 2 * D), BF16), jax.ShapeDtypeStruct((s, D), BF16),
                   jax.ShapeDtypeStruct((s, D), BF16), jax.ShapeDtypeStruct((s, POOL_W), BF16),
                   jax.ShapeDtypeStruct((s, D), BF16), jax.ShapeDtypeStruct((s, D), BF16),
                   jax.ShapeDtypeStruct((1, 2 * D), F32), jax.ShapeDtypeStruct((1, HV), F32)],
        compiler_params=_cp("arbitrary"),
    )(dx1b, zr, zr, zr, yp, yg, o, bgate, ghead, wpp, wgla, wout, after)


HALO = 16
CCH = 1408


def _conv_taps(u_ref, halo_ref, cs, first, ts):
    u = u_ref[:, cs].astype(F32)
    hal = halo_ref[:, cs].astype(F32)
    h1 = jnp.where(first, 0.0, _pick_row(hal, HALO - 1))
    h2 = jnp.where(first, 0.0, _pick_row(hal, HALO - 2))
    row8 = _rows((8, u.shape[1]))
    r1, r2 = pltpu.roll(u, 1, 0), pltpu.roll(u, 2, 0)
    r1 = jnp.concatenate([jnp.where(row8 == 0, h1, r1[:8]), r1[8:]], axis=0)
    r2 = jnp.concatenate([jnp.where(row8 == 0, h2, jnp.where(row8 == 1, h1, r2[:8])), r2[8:]], axis=0)
    return u, r1, r2


def _ffn_down_loss(u, x1, tgt, wconv, bconv, wdown, gfin, ts):
    s = x1.shape[0]

    def body(u_ref, halo_ref, x1_ref, t_ref, wc_ref, bc_ref, wd_ref, gf_ref, a_ref, c_ref, dx_ref, dxb_ref, ls_ref,
             dgf_ref):
        i = pl.program_id(0)

        @pl.when(i == 0)
        def _():
            ls_ref[...] = jnp.zeros_like(ls_ref)
            dgf_ref[...] = jnp.zeros_like(dgf_ref)

        first = i == 0
        acc = x1_ref[...]
        for hf in range(D_FF // CCH):
            cg = slice(hf * CCH, (hf + 1) * CCH)
            cv = slice(D_FF + hf * CCH, D_FF + (hf + 1) * CCH)
            vals = []
            for cs in (cg, cv):
                u0, u1, u2 = _conv_taps(u_ref, halo_ref, cs, first, ts)
                vals.append(bc_ref[:, cs] + wc_ref[0:1, cs] * u2 + wc_ref[1:2, cs] * u1 + wc_ref[2:3, cs] * u0)
                c_ref[:, cs] = vals[-1].astype(BF16)
            a = (vals[0] * _sigmoid(vals[0]) * vals[1]).astype(BF16)
            a_ref[:, cg] = a
            acc = acc + _dot(a, wd_ref[cg, :])
        r = lax.rsqrt(jnp.mean(acc * acc, axis=-1, keepdims=True) + EPS)
        xh = acc * r
        gf = gf_ref[...]
        err = xh * gf - t_ref[...]
        ls_ref[...] += (0.5 / D) * jnp.sum(jnp.sum(err * err, axis=-1, keepdims=True), axis=0, keepdims=True)
        dy = err * (1.0 / D)
        dgf_ref[...] += jnp.sum(dy * xh, axis=0, keepdims=True)
        dxh = dy * gf
        dx = r * (dxh - xh * jnp.mean(dxh * xh, axis=-1, keepdims=True))
        dx_ref[...] = dx
        dxb_ref[...] = dx.astype(BF16)

    row = lambda i: (i, 0)
    const2 = lambda i: (0, 0)
    return pl.pallas_call(
        body, name="ffn_down_loss", grid=(s // ts,),
        in_specs=[pl.BlockSpec((ts, N_UP), row),
                  pl.BlockSpec((HALO, N_UP), lambda i: (jnp.maximum(i * (ts // HALO) - 1, 0), 0)),
                  pl.BlockSpec((ts, D), row), pl.BlockSpec((ts, D), row), pl.BlockSpec((3, N_UP), const2),
                  pl.BlockSpec((1, N_UP), const2), pl.BlockSpec((D_FF, D), const2), pl.BlockSpec((1, D), const2)],
        out_specs=[pl.BlockSpec((ts, D_FF), row), pl.BlockSpec((ts, N_UP), row), pl.BlockSpec((ts, D), row),
                   pl.BlockSpec((ts, D), row), pl.BlockSpec((1, 128), const2), pl.BlockSpec((1, D), const2)],
        out_shape=[jax.ShapeDtypeStruct((s, D_FF), BF16), jax.ShapeDtypeStruct((s, N_UP), BF16),
                   jax.ShapeDtypeStruct((s, D), F32), jax.ShapeDtypeStruct((s, D), BF16),
                   jax.ShapeDtypeStruct((1, 128), F32), jax.ShapeDtypeStruct((1, D), F32)],
        compiler_params=_cp("arbitrary"),
    )(u, u, x1, tgt, wconv, bconv, wdown, gfin)


def _ffn_bwd(dx2b, u, c, wconv, wdown, ts):
    s = dx2b.shape[0]
    nt = s // ts

    def body(dx_ref, u_ref, c_ref, wc_ref, wd_ref, du_ref, db_ref, dw_ref, nxt_ref):
        @pl.when(pl.program_id(0) == 0)
        def _():
            db_ref[...] = jnp.zeros_like(db_ref)
            dw_ref[...] = jnp.zeros_like(dw_ref)
            nxt_ref[...] = jnp.zeros_like(nxt_ref)

        dxv = dx_ref[...]
        row8 = _rows((8, CCH))
        for hf in range(D_FF // CCH):
            cg = slice(hf * CCH, (hf + 1) * CCH)
            cv = slice(D_FF + hf * CCH, D_FF + (hf + 1) * CCH)
            da = _dot_nt(dxv, wd_ref[cg, :])
            gate = c_ref[:, cg].astype(F32)
            val = c_ref[:, cv].astype(F32)
            sg = _sigmoid(gate)
            dcs = (da * val * sg * (1.0 + gate * (1.0 - sg)), da * gate * sg)
            for cs, dc in zip((cg, cv), dcs):
                n1 = nxt_ref[0:1, cs]
                n2 = nxt_ref[1:2, cs]
                r1, r2 = pltpu.roll(dc, ts - 1, 0), pltpu.roll(dc, ts - 2, 0)
                f1 = jnp.concatenate([r1[:ts - 8], jnp.where(row8 == 7, n1, r1[ts - 8:])], axis=0)
                f2 = jnp.concatenate([r2[:ts - 8], jnp.where(row8 == 7, n2, jnp.where(row8 == 6, n1, r2[ts - 8:]))], axis=0)
                uv = u_ref[:, cs].astype(F32)
                db_ref[:, cs] += jnp.sum(dc, axis=0, keepdims=True)
                dw_ref[0:1, cs] += jnp.sum(f2 * uv, axis=0, keepdims=True)
                dw_ref[1:2, cs] += jnp.sum(f1 * uv, axis=0, keepdims=True)
                dw_ref[2:3, cs] += jnp.sum(dc * uv, axis=0, keepdims=True)
                du_ref[:, cs] = (wc_ref[2:3, cs] * dc + wc_ref[1:2, cs] * f1 + wc_ref[0:1, cs] * f2).astype(BF16)
                nxt_ref[:, cs] = dc[0:8, :]

    rev = lambda i: (nt - 1 - i, 0)
    const2 = lambda i: (0, 0)
    return pl.pallas_call(
        body, name="ffn_bwd", grid=(nt,),
        in_specs=[pl.BlockSpec((ts, D), rev), pl.BlockSpec((ts, N_UP), rev), pl.BlockSpec((ts, N_UP), rev),
                  pl.BlockSpec((3, N_UP), const2), pl.BlockSpec((D_FF, D), const2)],
        out_specs=[pl.BlockSpec((ts, N_UP), rev), pl.BlockSpec((1, N_UP), const2), pl.BlockSpec((3, N_UP), const2)],
        out_shape=[jax.ShapeDtypeStruct((s, N_UP), BF16), jax.ShapeDtypeStruct((1, N_UP), F32),
                   jax.ShapeDtypeStruct((3, N_UP), F32)],
        scratch_shapes=[pltpu.VMEM((8, N_UP), F32)],
        compiler_params=_cp("arbitrary"),
    )(dx2b, u, c, wconv, wdown)


ANY = pl.BlockSpec(memory_space=pl.ANY)


def _place():
    x, y, c = lax.axis_index("x"), lax.axis_index("y"), lax.axis_index("c")
    chips = [(1 - x, y), (x, 1 - y), (1 - x, 1 - y)]
    return x, y, c, chips


def _half(shape, c, axis):
    size = shape[axis] // 2
    cut = pl.ds(pl.multiple_of(c * size, 8 if axis == 0 else 128), size)
    return (cut, slice(None)) if axis == 0 else (slice(None), cut)


def _half_shape(shape, axis):
    return (shape[0] // 2, shape[1]) if axis == 0 else (shape[0], shape[1] // 2)


def _remote(src, dst, send_sems, recv_sems, k, to):
    return pltpu.make_async_remote_copy(src_ref=src, dst_ref=dst, send_sem=send_sems.at[k], recv_sem=recv_sems.at[k],
                                        device_id=to, device_id_type=MESH)


def _sibling_exchange(grads, axes, smalls, name):
    nb = len(grads)
    n = nb + len(smalls)

    def body(*refs):
        ins, outs = refs[:n], refs[n:2 * n]
        send_sems, recv_sems = refs[2 * n:]
        x, y, c, _ = _place()
        sib = (x, y, 1 - c)
        cps = []
        for a in range(nb):
            theirs = _half(grads[a].shape[1:], 1 - c, axes[a])
            cps.append(_remote(ins[a].at[(slice(None),) + theirs], outs[a], send_sems, recv_sems, a, sib))
        for a in range(nb, n):
            cps.append(_remote(ins[a], outs[a], send_sems, recv_sems, a, sib))
        for cp in cps:
            cp.start()
        for cp in cps:
            cp.wait()

    out_shape = [jax.ShapeDtypeStruct((4,) + _half_shape(g.shape[1:], ax), g.dtype) for g, ax in zip(grads, axes)]
    out_shape += [jax.ShapeDtypeStruct(a.shape, F32) for a in smalls]
    return pl.pallas_call(
        body, name=name, in_specs=[ANY] * n, out_specs=[ANY] * n, out_shape=out_shape,
        scratch_shapes=[pltpu.SemaphoreType.DMA((n,)), pltpu.SemaphoreType.DMA((n,))],
        compiler_params=pltpu.CompilerParams(has_side_effects=True),
    )(*grads, *smalls)


def _gather_share(lands, axes, name):
    n = len(lands)

    def body(*refs):
        outs = refs[n:2 * n]
        send_sems, recv_sems = refs[2 * n:]
        x, y, c, chips = _place()
        sib = (x, y, 1 - c)
        cps = []
        for a in range(n):
            mine = _half(lands[a].shape[1:], c, axes[a])
            for k, ch in enumerate(chips):
                landed = outs[a].at[(2 * ch[0] + ch[1],) + mine]
                cps.append(_remote(landed, landed, send_sems, recv_sems, 3 * a + k, sib))
        for cp in cps:
            cp.start()
        for a in range(n):
            other = _half(lands[a].shape[1:], 1 - c, axes[a])
            for k, ch in enumerate(chips):
                landed = outs[a].at[(2 * ch[0] + ch[1],) + other]
                _remote(landed, landed, send_sems, recv_sems, 3 * a + k, sib).wait_recv()
        for cp in cps:
            cp.wait_send()

    return pl.pallas_call(
        body, name=name, in_specs=[ANY] * n, out_specs=[ANY] * n,
        out_shape=[jax.ShapeDtypeStruct(a.shape, a.dtype) for a in lands],
        input_output_aliases={a: a for a in range(n)},
        scratch_shapes=[pltpu.SemaphoreType.DMA((3 * n,)), pltpu.SemaphoreType.DMA((3 * n,))],
        compiler_params=pltpu.CompilerParams(has_side_effects=True),
    )(*lands)


def _sibling_share(halves, name):
    n = len(halves)

    def body(*refs):
        ins, outs = refs[:n], refs[n:2 * n]
        send_sems, recv_sems = refs[2 * n:]
        x, y, c, _ = _place()
        cps = [_remote(ins[a], outs[a], send_sems, recv_sems, a, (x, y, 1 - c)) for a in range(n)]
        for cp in cps:
            cp.start()
        for cp in cps:
            cp.wait()

    return pl.pallas_call(
        body, name=name, in_specs=[ANY] * n, out_specs=[ANY] * n,
        out_shape=[jax.ShapeDtypeStruct(h.shape, F32) for h in halves],
        scratch_shapes=[pltpu.SemaphoreType.DMA((n,)), pltpu.SemaphoreType.DMA((n,))],
        compiler_params=pltpu.CompilerParams(has_side_effects=True),
    )(*halves)


HBM = pl.BlockSpec(memory_space=pltpu.HBM)
SEM = pl.BlockSpec(memory_space=pltpu.SEMAPHORE)
DATAFLOW = pltpu.SideEffectType.DATAFLOW_SIDE_EFFECTING


def _split_start(name, srcs, land_shapes, plan, n_copies, after):
    lands = [lax.empty(*ls) if isinstance(ls, tuple) else ls for ls in land_shapes]
    bufs = list(srcs) + lands
    nb, ns = len(bufs), len(srcs)

    def body(*refs):
        send_sems, recv_sems, token = refs[nb + 1], refs[nb + 2], refs[-1]
        for k, (src, dst, to) in enumerate(plan(refs[:ns], refs[ns:nb])):
            _remote(src, dst, send_sems, recv_sems, k, to).start()
        token[...] = jnp.zeros_like(token)

    res = pl.pallas_call(
        body, name=name,
        out_shape=(pltpu.SemaphoreType.DMA((n_copies,)), pltpu.SemaphoreType.DMA((n_copies,)),
                   *[pltpu.HBM(b.shape, b.dtype) for b in bufs], jax.ShapeDtypeStruct((8, 128), F32)),
        in_specs=[HBM] * nb + [ANY],
        out_specs=(SEM, SEM, *[HBM] * nb, pl.BlockSpec(memory_space=pltpu.VMEM)),
        input_output_aliases={i: 2 + i for i in range(nb)},
        compiler_params=pltpu.CompilerParams(has_side_effects=DATAFLOW),
    )(*[pltpu.with_memory_space_constraint(b, pltpu.HBM) for b in bufs], after)
    return (res[0], res[1], list(res[2:2 + nb])), res[-1]


def _split_wait(name, handle, n_srcs, plan, after, first=0):
    send_sems, recv_sems, bufs = handle
    nb = len(bufs)

    def body(*refs):
        sends, recvs = refs[nb], refs[nb + 1]
        for k, (src, dst, to) in enumerate(plan(refs[:n_srcs], refs[n_srcs:nb])):
            cp = _remote(src, dst, sends, recvs, first + k, to)
            cp.wait_send()
            cp.wait_recv()

    res = pl.pallas_call(
        body, name=name, out_shape=[pltpu.HBM(b.shape, b.dtype) for b in bufs],
        in_specs=[HBM] * nb + [SEM, SEM, ANY], out_specs=[HBM] * nb,
        input_output_aliases={i: i for i in range(nb)},
        compiler_params=pltpu.CompilerParams(has_side_effects=DATAFLOW),
    )(*bufs, send_sems, recv_sems, after)
    return list(res[:n_srcs]), list(res[n_srcs:])


def _gather_plan(shapes, axes, n_whole=0):
    def plan(srcs, lands):
        x, y, c, chips = _place()
        out = []
        for a, (shape, axis) in enumerate(zip(shapes, axes)):
            mine = _half(shape, c, axis)
            for ch in chips:
                out.append((srcs[a].at[mine], lands[a].at[(2 * x + y,) + mine], (ch[0], ch[1], c)))
        for a in range(len(shapes), len(shapes) + n_whole):
            for ch in chips:
                out.append((srcs[a], lands[a].at[2 * x + y], (ch[0], ch[1], c)))
        return out
    return plan


def _share_plan(shapes, axes):
    def plan(srcs, lands):
        x, y, c, chips = _place()
        out = []
        for a, (shape, axis) in enumerate(zip(shapes, axes)):
            mine = _half(shape, c, axis)
            for ch in chips:
                landed = lands[a].at[(2 * ch[0] + ch[1],) + mine]
                out.append((landed, landed, (x, y, 1 - c)))
        return out
    return plan


def _sibling_plan(shapes, axes):
    def plan(srcs, lands):
        x, y, c, _ = _place()
        return [(srcs[a].at[(slice(None),) + _half(shape, 1 - c, axis)], lands[a], (x, y, 1 - c))
                for a, (shape, axis) in enumerate(zip(shapes, axes))]
    return plan


def _reduce_plan(n_big, n_small):
    def plan(srcs, lands):
        x, y, c, chips = _place()
        out = []
        for a in range(n_big):
            for k, ch in enumerate(chips):
                out.append((srcs[a].at[2 * ch[0] + ch[1]], lands[a].at[k], (ch[0], ch[1], c)))
        for a in range(n_big, n_big + n_small):
            for ch in chips:
                out.append((srcs[a], lands[a].at[2 * x + y], (ch[0], ch[1], c)))
        return out
    return plan


def _row_tile(rows, cols, mult):
    best = mult
    for t in range(mult, rows + 1, mult):
        if rows % t == 0 and t * cols * 4 <= (2 << 20):
            best = t
    return best if rows % best == 0 else rows


COL_TILE = 256


def _half_tiling(hshape, axis, mult):
    hr, hc = hshape
    if axis == 0:
        tr = _row_tile(hr, hc, mult)
        return tr, hc, hr // tr
    return hr, COL_TILE, hc // COL_TILE


def _tile_idx(axis, t):
    return (t, 0) if axis == 0 else (0, t)


def _chip_partial(place, g, t, axis, name):
    hshape = t.shape[1:]
    br, bc, nt = _half_tiling(hshape, axis, 16)

    def body(pl_ref, g_ref, t_ref, pf_ref, pb_ref):
        v = g_ref[...].astype(F32) + t_ref[...].astype(F32)
        pb_ref[...] = v.astype(BF16)

        @pl.when(pl.program_id(1) == pl_ref[0])
        def _():
            pf_ref[...] = v

    blk = (None, br, bc)
    return pl.pallas_call(
        body, name=name,
        grid_spec=pltpu.PrefetchScalarGridSpec(
            num_scalar_prefetch=1, grid=(nt, 4),
            in_specs=[pl.BlockSpec(blk, lambda i, j, p: (j,) + _tile_idx(axis, p[1] * nt + i)),
                      pl.BlockSpec(blk, lambda i, j, p: (j,) + _tile_idx(axis, i))],
            out_specs=[pl.BlockSpec((br, bc), lambda i, j, p: _tile_idx(axis, i)),
                       pl.BlockSpec(blk, lambda i, j, p: (j,) + _tile_idx(axis, i))]),
        out_shape=[jax.ShapeDtypeStruct(hshape, F32), jax.ShapeDtypeStruct((4,) + hshape, BF16)],
        compiler_params=_cp("arbitrary", "arbitrary"),
    )(place, g, t)


def _finish_half(pf, rb, axis, name):
    hshape = pf.shape
    br, bc, nt = _half_tiling(hshape, axis, 16)

    def body(pf_ref, rb_ref, o_ref):
        o_ref[...] = ((pf_ref[...] + rb_ref[0].astype(F32)) + rb_ref[1].astype(F32)) + rb_ref[2].astype(F32)

    return pl.pallas_call(
        body, name=name, grid=(nt,),
        in_specs=[pl.BlockSpec((br, bc), lambda i: _tile_idx(axis, i)),
                  pl.BlockSpec((3, br, bc), lambda i: (0,) + _tile_idx(axis, i))],
        out_specs=pl.BlockSpec((br, bc), lambda i: _tile_idx(axis, i)),
        out_shape=jax.ShapeDtypeStruct(hshape, F32),
        compiler_params=_cp("arbitrary"),
    )(pf, rb)


def _adam_math(w, g, m, v):
    m = ADAM_B1 * m + (1.0 - ADAM_B1) * g
    v = ADAM_B2 * v + (1.0 - ADAM_B2) * (g * g)
    m_hat = m / (1.0 - ADAM_B1 ** ADAM_STEP)
    v_hat = v / (1.0 - ADAM_B2 ** ADAM_STEP)
    return -ADAM_LR * (m_hat / (jnp.sqrt(v_hat) + ADAM_EPS) + ADAM_WD * w), m, v


def _adam_halves(place, w, mine, theirs, m, v, axis, name):
    br, bc, nt = _half_tiling(mine.shape, axis, 8)

    def body(pl_ref, w_ref, a_ref, b_ref, m_ref, v_ref, g_ref, d_ref, mo_ref, vo_ref):
        is_mine = pl.program_id(0) // nt == pl_ref[1]
        g = jnp.where(is_mine, a_ref[...], b_ref[...])
        d, mn, vn = _adam_math(w_ref[...], g, m_ref[...], v_ref[...])
        g_ref[...] = g
        d_ref[...] = d
        mo_ref[...] = mn
        vo_ref[...] = vn

    full = pl.BlockSpec((br, bc), lambda i, p: _tile_idx(axis, i))
    mine_spec = pl.BlockSpec((br, bc), lambda i, p: _tile_idx(axis, jnp.where(i // nt == p[1], i % nt, nt - 1)))
    theirs_spec = pl.BlockSpec((br, bc), lambda i, p: _tile_idx(axis, jnp.where(i // nt == p[1], 0, i % nt)))
    return pl.pallas_call(
        body, name=name,
        grid_spec=pltpu.PrefetchScalarGridSpec(
            num_scalar_prefetch=1, grid=(2 * nt,), in_specs=[full, mine_spec, theirs_spec, full, full],
            out_specs=[full] * 4),
        out_shape=[jax.ShapeDtypeStruct(w.shape, F32)] * 4, compiler_params=_cp("arbitrary"),
    )(place, w, mine, theirs, m, v)


def _add_many(xs, ys, name):
    n = len(xs)

    def body(*refs):
        for i in range(n):
            refs[2 * n + i][...] = refs[i][...] + refs[n + i][...]

    return pl.pallas_call(body, name=name, out_shape=[jax.ShapeDtypeStruct(a.shape, F32) for a in xs])(*xs, *ys)


def _adam_small(place, owns, landed, ws, ms, vs, widths):
    n, nw = len(owns), len(ws)

    def body(pl_ref, *refs):
        own_r, land_r = refs[:n], refs[n:2 * n]
        w_r, m_r, v_r = (refs[2 * n + k * nw:2 * n + (k + 1) * nw] for k in range(3))
        outs = refs[2 * n + 3 * nw:]
        g_o, d_o, m_o, v_o = outs[:n], outs[n:n + nw], outs[n + nw:n + 2 * nw], outs[n + 2 * nw:]
        for me in range(4):
            @pl.when(pl_ref[0] == me)
            def _(me=me):
                for i in range(n):
                    p = [own_r[i][...] if k == me else land_r[i][k] for k in range(4)]
                    g = ((p[0] + p[1]) + p[2]) + p[3]
                    if i < nw and widths[i]:
                        g = g[:, me * widths[i]:(me + 1) * widths[i]]
                    g_o[i][...] = g
                    if i < nw:
                        d, mn, vn = _adam_math(w_r[i][...], g, m_r[i][...], v_r[i][...])
                        d_o[i][...] = d
                        m_o[i][...] = mn
                        v_o[i][...] = vn

    g_shapes = [jax.ShapeDtypeStruct(ws[i].shape if i < nw else owns[i].shape, F32) for i in range(n)]
    w_shapes = [jax.ShapeDtypeStruct(w.shape, F32) for w in ws]
    whole = lambda a: pl.BlockSpec(a.shape, lambda i, p, nd=len(a.shape): (0,) * nd)
    ins = list(owns) + list(landed) + list(ws) + list(ms) + list(vs)
    out_shape = g_shapes + w_shapes * 3
    out = pl.pallas_call(
        body, name="adam_small",
        grid_spec=pltpu.PrefetchScalarGridSpec(num_scalar_prefetch=1, grid=(1,), in_specs=[whole(a) for a in ins],
                                               out_specs=[whole(a) for a in out_shape]),
        out_shape=out_shape, compiler_params=_cp("arbitrary"),
    )(place, *ins)
    return out[:n], out[n:n + nw], out[n + nw:n + 2 * nw], out[n + 2 * nw:]


def kernel(x, g_mix, w_in, b_gate, w_gk_up, b_gk, w_pool_grp, pool_scale, g_gla_head, w_pool_proj, w_gla_proj, w_out, g_ffn, w_up, w_conv, b_conv, w_down, g_final, loss_target, m_g_mix, m_w_in, m_b_gate, m_w_gk_up, m_b_gk, m_w_pool_grp, m_pool_scale, m_g_gla_head, m_w_pool_proj, m_w_gla_proj, m_w_out, m_g_ffn, m_w_up, m_w_conv, m_b_conv, m_w_down, m_g_final, v_g_mix, v_w_in, v_b_gate, v_w_gk_up, v_b_gk, v_w_pool_grp, v_pool_scale, v_g_gla_head, v_w_pool_proj, v_w_gla_proj, v_w_out, v_g_ffn, v_w_up, v_w_conv, v_b_conv, v_w_down, v_g_final):
    s = x.shape[1]
    ts = min(s, 512)
    tm = min(s, 256)
    cx, cy, cc = lax.axis_index("x"), lax.axis_index("y"), lax.axis_index("c")
    chip = 2 * cx + cy
    place = jnp.stack([chip, cc]).astype(jnp.int32)

    big_names = ("w_in", "w_pool_proj", "w_gla_proj", "w_out", "w_up", "w_down")
    axes = (1, 0, 0, 0, 0, 0)
    shards = dict(w_in=jnp.transpose(w_in[0]), w_pool_proj=w_pool_proj[0], w_gla_proj=w_gla_proj[0], w_out=w_out[0],
                  w_up=w_up[0], w_down=w_down[0])
    def fill_own(lands, mine):
        return [lax.dynamic_update_slice(g, o_[None], (chip, 0, 0)) for g, o_ in zip(lands, mine)]

    def gather_start(tag, groups, after):
        plans = [_gather_plan([o_.shape for o_ in halves], gaxes, len(whole)) for halves, gaxes, whole in groups]
        srcs = [list(halves) + list(whole) for halves, _, whole in groups]
        flat = [a_ for g_ in srcs for a_ in g_]

        def plan_all(src_refs, land_refs):
            out, off = [], 0
            for plan, g_ in zip(plans, srcs):
                out += plan(src_refs[off:off + len(g_)], land_refs[off:off + len(g_)])
                off += len(g_)
            return out

        (send, recv, bufs), token = _split_start("gather_" + tag + "_start", flat,
                                                 [((4,) + a_.shape, a_.dtype) for a_ in flat], plan_all, 3 * len(flat), after)
        started, off = [], 0
        for plan, g_, (halves, gaxes, _) in zip(plans, srcs, groups):
            sub = bufs[off:off + len(g_)] + bufs[len(flat) + off:len(flat) + off + len(g_)]
            started.append(((send, recv, sub), plan, len(halves), len(g_), gaxes, 3 * off))
            off += len(g_)
        return started, token

    def gather_finish(tag, started, after):
        handle, plan, n_halves, n, group_axes, first = started
        mine, lands = _split_wait("gather_" + tag + "_wait", handle, n, plan, after, first)
        lands[:n_halves] = _gather_share(lands[:n_halves], group_axes, "gather_" + tag + "_share")
        return fill_own(lands, mine)

    (in_w,), tok = gather_start("in", [([shards["w_in"].astype(BF16)], axes[:1], [])], g_mix)
    zero = tok[0, 0]
    own = [(shards[n] + zero).astype(BF16) for n in big_names[1:]]
    (mix_w, up_w, down_w), tok = gather_start(
        "rest", [(own[0:3], axes[1:4], [w_gk_up[0] + zero, w_conv[0] + zero]), (own[3:4], axes[4:5], []),
                 (own[4:5], axes[5:6], [])], tok)

    def forward_start(tag, started, after):
        handle, plan, _, n, group_axes, first = started
        mine, lands = _split_wait("gather_" + tag + "_wait", handle, n, plan, after, first)
        plan = _share_plan([o_.shape for o_ in mine], group_axes)
        share, token = _split_start("gather_" + tag + "_share_start", [], lands, plan, 3 * n, after)
        return (share, plan, mine), token

    def forward_done(tag, forwarded, after):
        share, plan, mine = forwarded
        return fill_own(_split_wait("gather_" + tag + "_share_wait", share, 0, plan, after)[1], mine)
    xs, tgt = x[0], loss_target[0]
    wgrp = w_pool_grp[0]
    h = _rmsnorm(xs, g_mix, tok, "norm_mix", ts)
    m_in_t, v_in_t = jnp.transpose(m_w_in[0]), jnp.transpose(v_w_in[0])
    h, m_in_t, v_in_t = lax.optimization_barrier((h, m_in_t, v_in_t))
    w_in_t = gather_finish("in", in_w, h)[0].reshape(N_IN, D)
    nsh = N_IN // 4

    zr = _in_proj(h, w_in_t, 1152)
    p, pp = _pool_fwd(zr, wgrp, pool_scale)
    wpp, wgla, wout, wgk4, wconv4 = gather_finish("mix", mix_w, pp)
    wgla, wout = wgla.reshape(D, D), wout.reshape(D, D)
    wgk_full = jnp.transpose(wgk4, (1, 0, 2)).reshape(GATE_RANK, 512)
    wconv_full = jnp.transpose(wconv4, (1, 0, 2)).reshape(3, N_UP)
    wgk_pad = jnp.concatenate([wgk_full, jnp.zeros((128 - GATE_RANK, 512), F32)], axis=0)
    o, og, sp = _gla_fwd(zr, wgk_pad, b_gk, g_gla_head, ts)
    up_f, tok = forward_start("up", up_w, og)
    x1, mixed, yp, yg, h2 = _merge_fwd(xs, zr, pp, og, b_gate, wpp, wgla, wout, g_ffn, tok, ts)
    wup, = forward_done("up", up_f, x1)
    down_f, tok = forward_start("down", down_w, x1)
    u = _matmul_resident(h2, wup, tok, "ffn_up")
    wdown = forward_done("down", down_f, u)[0].reshape(D_FF, D)
    a, conv_out, dx2, dx2b, loss_part, dgfin = _ffn_down_loss(u, x1, tgt, wconv_full, b_conv, wdown,
                                                              g_final.reshape(1, D), tm)

    du, dbconv, dwconv = _ffn_bwd(dx2b, u, conv_out, wconv_full, wdown, tm)
    dw_down = _matmul_tn(a, dx2b, "dw_down", D, tm=1408)
    dw_up = _matmul_tn(h2, du, "dw_up", 1408, shard_major=True)

    def exchange_start(tag, grads, group_axes, after):
        plan = _sibling_plan([g.shape[1:] for g in grads], group_axes)
        lands = [((4,) + _half_shape(g.shape[1:], ax), g.dtype) for g, ax in zip(grads, group_axes)]
        handle, token = _split_start("sibling_" + tag + "_start", grads, lands, plan, len(grads), after)
        return (handle, plan, len(grads)), token

    def partials(tag, names, group_axes, exchange, after):
        handle, plan, n = exchange
        mine, theirs = _split_wait("sibling_" + tag + "_wait", handle, n, plan, after)
        return zip(*[_chip_partial(place, g, t, ax, "chip_partial_" + nm)
                     for nm, ax, g, t in zip(names, group_axes, mine, theirs)])

    ffn_names, ffn_axes = ("w_up", "w_down"), (0, 0)
    ffn_x, token = exchange_start("ffn", [dw_up, dw_down.reshape(4, 704, D)], ffn_axes, du)
    dx1, dx1b, dgffn = _matmul_nt_normbwd(du, wup, x1, g_ffn, dx2, token, "ffn_up_bwd", ts)
    ffn_pf, ffn_pb = partials("ffn", ffn_names, ffn_axes, ffn_x, dx1b)
    ffn_plan = _reduce_plan(2, 0)
    ffn_handle, token = _split_start("reduce_ffn_start", ffn_pb, [((3,) + p.shape[1:], BF16) for p in ffn_pb],
                                     ffn_plan, 6, ffn_pf[0])

    dzg, dyp, dyg, dpp, do, dzog, dbgate, dghead = _merge_bwd(dx1b, zr, yp, yg, o, b_gate, g_gla_head, wpp, wgla, wout,
                                                             token, ts)
    dw_out = _matmul_tn(mixed, dx1b, "dw_out", D, tm=512)
    dw_gla = _matmul_tn(og, dyg, "dw_gla", D, tm=512)
    dw_pp = _matmul_tn(pp, dyp, "dw_pp", 256, shard_major=True)

    out_names, out_axes = ("w_pool_proj", "w_gla_proj", "w_out"), (0, 0, 0)
    out_x, token = exchange_start("out", [dw_pp, dw_gla.reshape(4, 256, D), dw_out.reshape(4, 256, D)], out_axes, dpp)
    dzp, dwgrp, dscale = _pool_bwd(p, dpp, wgrp, pool_scale, token)
    out_pf, out_pb = partials("out", out_names, out_axes, out_x, dzp)
    out_plan = _reduce_plan(3, 0)
    out_handle, token = _split_start("reduce_out_start", out_pb, [((3,) + p_.shape[1:], BF16) for p_ in out_pb],
                                     out_plan, 9, out_pf[0])
    dq, dk, dv, dgpre = _gla_bwd(zr, do, sp, wgk_pad, b_gk, token, ts)
    dzgk, dwgk, dbgk = _gk_bwd(dgpre, zr, wgk_pad, dgpre, ts)
    dzr = jnp.concatenate([dzg, dv, dzog, dzp, dq, dk, dzgk], axis=1)
    dw_rt = _matmul_tn(dzr, h, "dw_in", D, tm=1152)

    def grad_rows(lo, hi):
        out = []
        for seg_lo, seg_hi, at in ((0, 1536, OFF_POOL), (1536, 3584, OFF_V), (3584, 3600, OFF_GK), (3600, N_IN, OFF_GATE)):
            a_, b_ = max(lo, seg_lo), min(hi, seg_hi)
            if a_ < b_:
                out.append(dw_rt[at + a_ - seg_lo:at + b_ - seg_lo])
        return jnp.concatenate(out, axis=0)

    dw_in_t = jnp.stack([grad_rows(j * nsh, (j + 1) * nsh) for j in range(4)])

    in_sib = _sibling_exchange([dw_in_t], (1,), [], "sibling_exchange_in")
    in_pf, in_pb = _chip_partial(place, dw_in_t, in_sib[0], 1, "chip_partial_w_in")
    in_plan = _reduce_plan(1, 0)
    in_handle, token = _split_start("reduce_in_start", [in_pb], [((3,) + in_pb.shape[1:], BF16)], in_plan, 3, in_pf)
    grad_x, _, dgmix = _matmul_nt_normbwd(dzr, w_in_t, xs, g_mix, dx1, token, "in_proj_bwd", ts, transposed=True)
    small_names = ("g_mix", "b_gate", "w_gk_up", "b_gk", "w_pool_grp", "pool_scale", "g_gla_head", "g_ffn", "w_conv",
                   "b_conv", "g_final")
    small_mine = [dgmix, dbgate, dwgk[:GATE_RANK], dbgk, dwgrp.reshape(4 * 128, 128), dscale, dghead, dgffn, dwconv, dbconv,
                  dgfin, loss_part]
    small_sib = _sibling_exchange([], (), small_mine, "sibling_exchange_small")
    small_chip = _add_many(small_mine, small_sib, "chip_partial_small")
    small_plan = _reduce_plan(0, len(small_chip))
    small_handle, token = _split_start("reduce_small_start", small_chip, [((4,) + a_.shape, F32) for a_ in small_chip],
                                       small_plan, 3 * len(small_chip), small_mine[0])

    ms = dict(w_in=m_in_t, w_pool_proj=m_w_pool_proj[0], w_gla_proj=m_w_gla_proj[0], w_out=m_w_out[0],
              w_up=m_w_up[0], w_down=m_w_down[0])
    vs = dict(w_in=v_in_t, w_pool_proj=v_w_pool_proj[0], w_gla_proj=v_w_gla_proj[0], w_out=v_w_out[0],
              w_up=v_w_up[0], w_down=v_w_down[0])
    grad, delta, new_m, new_v = {}, {}, {}, {}

    def finish_and_update(names, group_axes, part_f, landed, tag):
        halves = [_finish_half(pf, rb, ax, "finish_" + n) for n, ax, pf, rb in zip(names, group_axes, part_f, landed)]
        sib_halves = _sibling_share(halves, "sibling_share_" + tag)
        for n, ax, mine, theirs in zip(names, group_axes, halves, sib_halves):
            res = _adam_halves(place, shards[n], mine, theirs, ms[n], vs[n], ax, "adam_" + n)
            if n == "w_in":
                res = [jnp.transpose(r_) for r_ in res]
            grad[n], delta[n], new_m[n], new_v[n] = [r_[None] for r_ in res]

    _, ffn_landed = _split_wait("reduce_ffn_wait", ffn_handle, 2, ffn_plan, token)
    _, out_landed = _split_wait("reduce_out_wait", out_handle, 3, out_plan, ffn_landed[0])
    finish_and_update(ffn_names + out_names, ffn_axes + out_axes, ffn_pf + out_pf, ffn_landed + out_landed, "rest")
    _, in_landed = _split_wait("reduce_in_wait", in_handle, 1, in_plan, delta["w_out"])
    finish_and_update(("w_in",), (1,), (in_pf,), in_landed, "in")
    small_sent, small_landed = _split_wait("reduce_small_wait", small_handle, len(small_chip), small_plan, delta["w_in"])
    given = dict(g_mix=(g_mix, m_g_mix, v_g_mix), b_gate=(b_gate, m_b_gate, v_b_gate), w_gk_up=(w_gk_up, m_w_gk_up, v_w_gk_up),
                 b_gk=(b_gk, m_b_gk, v_b_gk), w_pool_grp=(w_pool_grp, m_w_pool_grp, v_w_pool_grp),
                 pool_scale=(pool_scale, m_pool_scale, v_pool_scale), g_gla_head=(g_gla_head, m_g_gla_head, v_g_gla_head),
                 g_ffn=(g_ffn, m_g_ffn, v_g_ffn), w_conv=(w_conv, m_w_conv, v_w_conv), b_conv=(b_conv, m_b_conv, v_b_conv),
                 g_final=(g_final, m_g_final, v_g_final))
    flat2 = lambda a: a.reshape(-1, a.shape[-1])
    widths = [dict(w_gk_up=128, w_conv=1408).get(n) for n in small_names]
    totals, ds, mo, vo = _adam_small(place, small_sent, small_landed, *[[flat2(given[n][k]) for n in small_names] for k in range(3)],
                                     widths)
    loss = totals[-1][0, 0]
    for i, n in enumerate(small_names):
        shp = given[n][0].shape
        grad[n], delta[n], new_m[n], new_v[n] = [r_.reshape(shp) for r_ in (totals[i], ds[i], mo[i], vo[i])]

    order = ("g_mix", "w_in", "b_gate", "w_gk_up", "b_gk", "w_pool_grp", "pool_scale", "g_gla_head", "w_pool_proj",
             "w_gla_proj", "w_out", "g_ffn", "w_up", "w_conv", "b_conv", "w_down", "g_final")
    return (loss, grad_x[None], *[grad[n] for n in order], *[delta[n] for n in order], *[new_m[n] for n in order],
            *[new_v[n] for n in order])
```

```python
import jax
import jax.numpy as jnp
from jax import lax
from jax.experimental import pallas as pl
from jax.experimental.pallas import tpu as pltpu

F32 = jnp.float32
BF16 = jnp.bfloat16
MESH = pl.DeviceIdType.MESH

D = 1024
EPS = 1e-6
CHUNK = 64
POOL_W = 512
POOL_WINDOWS = (2, 4, 8, 16)
HEADS = 4
HK = 128
HV = 256
GATE_RANK = 16
D_FF = 2816
N_UP = 2 * D_FF
N_IN = 5648
QSCALE = HK ** -0.5
N_INR = 5760
OFF_GATE, OFF_V, OFF_OG, OFF_POOL, OFF_Q, OFF_K, OFF_GK = 0, 2048, 3072, 4096, 4608, 5120, 5632

ADAM_LR, ADAM_B1, ADAM_B2, ADAM_EPS, ADAM_WD, ADAM_STEP = 0.001, 0.9, 0.999, 1e-08, 0.01, 10

VMEM_LIMIT = 56 * 1024 * 1024
PROJ_TILE = N_INR // 5
UP_SHARD = N_UP // 4


def _cp(*sem):
    return pltpu.CompilerParams(dimension_semantics=sem if sem else None, vmem_limit_bytes=VMEM_LIMIT)


def _dot(a, b):
    return jnp.dot(a, b, preferred_element_type=F32)


def _dot_nt(a, b):
    return lax.dot_general(a, b, (((1,), (1,)), ((), ())), preferred_element_type=F32)


def _dot_tn(a, b):
    return lax.dot_general(a, b, (((0,), (0,)), ((), ())), preferred_element_type=F32)


def _sigmoid(v):
    return 1.0 / (1.0 + jnp.exp(-v))


def _rows(shape):
    return lax.broadcasted_iota(jnp.int32, shape, 0)


def _pick_row(v, r):
    return jnp.sum(jnp.where(_rows(v.shape) == r, v, 0.0), axis=0, keepdims=True)


def _rmsnorm(x, g, after, name, ts):
    s = x.shape[0]

    def body(x_ref, g_ref, after_ref, h_ref):
        xv = x_ref[...]
        r = lax.rsqrt(jnp.mean(xv * xv, axis=-1, keepdims=True) + EPS)
        h_ref[...] = (xv * r * g_ref[...]).astype(BF16)

    return pl.pallas_call(
        body, name=name, grid=(s // ts,),
        in_specs=[pl.BlockSpec((ts, D), lambda i: (i, 0)), pl.BlockSpec((1, D), lambda i: (0, 0)), ANY],
        out_specs=pl.BlockSpec((ts, D), lambda i: (i, 0)), out_shape=jax.ShapeDtypeStruct((s, D), BF16),
        compiler_params=_cp("arbitrary"),
    )(x, g, after)


MM_ROWS = 512


def _matmul_resident(h, w, after, name):
    s = h.shape[0]
    nj, tn = w.shape[0], w.shape[2]
    rc = min(s, MM_ROWS)

    def body(h_ref, w_ref, after_ref, z_ref):
        for r0 in range(0, s, rc):
            z_ref[r0:r0 + rc, :] = _dot(h_ref[r0:r0 + rc, :], w_ref[...]).astype(BF16)

    return pl.pallas_call(
        body, name=name, grid=(nj,),
        in_specs=[pl.BlockSpec((s, D), lambda j: (0, 0)), pl.BlockSpec((None, D, tn), lambda j: (j, 0, 0)), ANY],
        out_specs=pl.BlockSpec((s, tn), lambda j: (0, j)), out_shape=jax.ShapeDtypeStruct((s, nj * tn), BF16),
        compiler_params=_cp("arbitrary"),
    )(h, w, after)


PROJ_PIECES = ((3600, 2048, OFF_GATE), (1536, 2048, OFF_V), (0, 1536, OFF_POOL), (3584, GATE_RANK, OFF_GK))


def _projection_copies(w_hbm, w_ref, sems):
    return [pltpu.make_async_copy(w_hbm.at[pl.ds(src, n)], w_ref.at[pl.ds(dst, n)], sems.at[i])
            for i, (src, n, dst) in enumerate(PROJ_PIECES)]


def _load_projection(w_hbm, w_ref, sems):
    cps = _projection_copies(w_hbm, w_ref, sems)
    for cp in cps:
        cp.start()
    w_ref[OFF_GK + GATE_RANK:, :] = jnp.zeros((N_INR - OFF_GK - GATE_RANK, D), BF16)
    for cp in cps:
        cp.wait()


def _in_proj(h, w_nat, tn):
    s = h.shape[0]
    rc = min(s, MM_ROWS)
    nj = N_INR // tn
    first_use = [dst // tn for _, _, dst in PROJ_PIECES]

    def body(h_ref, w_hbm, z_ref, w_ref, sems):
        j = pl.program_id(0)
        cps = _projection_copies(w_hbm, w_ref, sems)

        @pl.when(j == 0)
        def _():
            for cp in cps:
                cp.start()
            w_ref[OFF_GK + GATE_RANK:, :] = jnp.zeros((N_INR - OFF_GK - GATE_RANK, D), BF16)

        for step in range(nj):
            due = [cp for cp, at in zip(cps, first_use) if at == step]
            if due:
                @pl.when(j == step)
                def _(due=due):
                    for cp in due:
                        cp.wait()

        wt = w_ref[pl.ds(pl.multiple_of(j * tn, 128), tn), :]
        for r0 in range(0, s, rc):
            z_ref[r0:r0 + rc, :] = _dot_nt(h_ref[r0:r0 + rc, :], wt).astype(BF16)

    return pl.pallas_call(
        body, name="in_proj", grid=(nj,),
        in_specs=[pl.BlockSpec((s, D), lambda j: (0, 0)), ANY],
        out_specs=pl.BlockSpec((s, tn), lambda j: (0, j)), out_shape=jax.ShapeDtypeStruct((s, N_INR), BF16),
        scratch_shapes=[pltpu.VMEM((N_INR, D), BF16), pltpu.SemaphoreType.DMA((len(PROJ_PIECES),))],
        compiler_params=_cp("arbitrary"),
    )(h, w_nat)


def _matmul_nt_normbwd(dz, w, x, g, resid, after, name, ts, transposed=False):
    s = x.shape[0]
    w_vmem = (N_INR, D) if transposed else (D, w.shape[0] * w.shape[2])
    n_sems = len(PROJ_PIECES) if transposed else w.shape[0]

    def body(dz_ref, w_hbm, x_ref, g_ref, r_ref, after_ref, o_ref, ob_ref, dg_ref, w_ref, sems):
        @pl.when(pl.program_id(0) == 0)
        def _():
            if transposed:
                _load_projection(w_hbm, w_ref, sems)
            else:
                kc = w.shape[2]
                cps = [pltpu.make_async_copy(w_hbm.at[j], w_ref.at[:, pl.ds(j * kc, kc)], sems.at[j])
                       for j in range(w.shape[0])]
                for cp in cps:
                    cp.start()
                for cp in cps:
                    cp.wait()
            dg_ref[...] = jnp.zeros_like(dg_ref)

        dh = _dot(dz_ref[...], w_ref[...]) if transposed else _dot_nt(dz_ref[...], w_ref[...])
        xv = x_ref[...]
        r = lax.rsqrt(jnp.mean(xv * xv, axis=-1, keepdims=True) + EPS)
        xh = xv * r
        dg_ref[...] += jnp.sum(dh * xh, axis=0, keepdims=True)
        dxh = dh * g_ref[...]
        out = r_ref[...] + r * (dxh - xh * jnp.mean(dxh * xh, axis=-1, keepdims=True))
        o_ref[...] = out
        ob_ref[...] = out.astype(BF16)

    row = lambda i: (i, 0)
    kdim = dz.shape[1]
    return pl.pallas_call(
        body, name=name, grid=(s // ts,),
        in_specs=[pl.BlockSpec((ts, kdim), row), ANY, pl.BlockSpec((ts, D), row),
                  pl.BlockSpec((1, D), lambda i: (0, 0)), pl.BlockSpec((ts, D), row), ANY],
        out_specs=[pl.BlockSpec((ts, D), row), pl.BlockSpec((ts, D), row), pl.BlockSpec((1, D), lambda i: (0, 0))],
        out_shape=[jax.ShapeDtypeStruct((s, D), F32), jax.ShapeDtypeStruct((s, D), BF16),
                   jax.ShapeDtypeStruct((1, D), F32)],
        scratch_shapes=[pltpu.VMEM(w_vmem, BF16), pltpu.SemaphoreType.DMA((n_sems,))],
        compiler_params=_cp("arbitrary"),
    )(dz, w, x, g, resid, after)


def _matmul_tn(a, b, name, tn, shard_major=False, tm=None):
    s, m = a.shape
    n = b.shape[1]
    tm = m if tm is None else tm
    ni, nj = m // tm, n // tn

    def body(a_ref, b_ref, o_ref):
        o_ref[...] = _dot_tn(a_ref[...], b_ref[...]).astype(BF16)

    if shard_major:
        out_spec = pl.BlockSpec((None, tm, tn), lambda i, j: (j, i, 0))
        out_shape = jax.ShapeDtypeStruct((nj, m, tn), BF16)
    else:
        out_spec = pl.BlockSpec((tm, tn), lambda i, j: (i, j))
        out_shape = jax.ShapeDtypeStruct((m, n), BF16)
    return pl.pallas_call(
        body, name=name, grid=(ni, nj),
        in_specs=[pl.BlockSpec((s, tm), lambda i, j: (0, i)), pl.BlockSpec((s, tn), lambda i, j: (0, j))],
        out_specs=out_spec, out_shape=out_shape,
        compiler_params=_cp("arbitrary", "arbitrary"),
    )(a, b)


def _pool_fwd(zr, wgrp, scale):
    s = zr.shape[0]

    def body(u_ref, w_ref, sc_ref, p_ref, pp_ref):
        row = _rows((s, 128))
        for gi, win in enumerate(POOL_WINDOWS):
            cs = slice(gi * 128, (gi + 1) * 128)
            u = u_ref[:, cs].astype(F32)
            acc, k = u, 1
            while k < win:
                acc = acc + jnp.where(row >= k, pltpu.roll(acc, k, 0), 0.0)
                k *= 2
            cnt = jnp.minimum(row + 1, win).astype(F32)
            p = (acc / cnt - u).astype(BF16)
            p_ref[:, cs] = p
            pp_ref[:, cs] = (_dot(p, w_ref[gi].astype(BF16)) * sc_ref[:, cs]).astype(BF16)

    return pl.pallas_call(
        body, name="pool_fwd", grid=(1,),
        in_specs=[pl.BlockSpec((s, POOL_W), lambda i: (0, OFF_POOL // POOL_W)),
                  pl.BlockSpec((4, 128, 128), lambda i: (0, 0, 0)), pl.BlockSpec((1, POOL_W), lambda i: (0, 0))],
        out_specs=[pl.BlockSpec((s, POOL_W), lambda i: (0, 0))] * 2,
        out_shape=[jax.ShapeDtypeStruct((s, POOL_W), BF16)] * 2,
        compiler_params=_cp("arbitrary"),
    )(zr, wgrp, scale)


def _pool_bwd(p, dpp, wgrp, scale, after):
    s = p.shape[0]

    def body(p_ref, dpp_ref, w_ref, sc_ref, after_ref, dz_ref, dw_ref, dsc_ref):
        row = _rows((s, 128))
        for gi, win in enumerate(POOL_WINDOWS):
            cs = slice(gi * 128, (gi + 1) * 128)
            pv = p_ref[:, cs]
            wb = w_ref[gi].astype(BF16)
            dpp_v = dpp_ref[:, cs].astype(F32)
            dsc_ref[:, cs] = jnp.sum(dpp_v * _dot(pv, wb), axis=0, keepdims=True)
            dpm = (dpp_v * sc_ref[:, cs]).astype(BF16)
            dw_ref[gi] = _dot_tn(pv, dpm)
            dp = _dot_nt(dpm, wb)
            cnt = jnp.minimum(row + 1, win).astype(F32)
            acc, k = dp / cnt, 1
            while k < win:
                acc = acc + jnp.where(row < s - k, pltpu.roll(acc, s - k, 0), 0.0)
                k *= 2
            dz_ref[:, cs] = (acc - dp).astype(BF16)

    full = lambda i: (0, 0)
    return pl.pallas_call(
        body, name="pool_bwd", grid=(1,),
        in_specs=[pl.BlockSpec((s, POOL_W), full), pl.BlockSpec((s, POOL_W), full),
                  pl.BlockSpec((4, 128, 128), lambda i: (0, 0, 0)), pl.BlockSpec((1, POOL_W), full), ANY],
        out_specs=[pl.BlockSpec((s, POOL_W), full), pl.BlockSpec((4, 128, 128), lambda i: (0, 0, 0)),
                   pl.BlockSpec((1, POOL_W), full)],
        out_shape=[jax.ShapeDtypeStruct((s, POOL_W), BF16), jax.ShapeDtypeStruct((4, 128, 128), F32),
                   jax.ShapeDtypeStruct((1, POOL_W), F32)],
        compiler_params=_cp("arbitrary"),
    )(p, dpp, wgrp, scale, after)


def _gla_decay(zgk_ref, wgk_ref, bgk_ref, rb):
    g = _dot(zgk_ref[...], wgk_ref[...].astype(BF16)) + bgk_ref[...]
    la = (jnp.minimum(g, 0.0) - jnp.log(1.0 + jnp.exp(-jnp.abs(g)))) * (1.0 / 16.0)
    rowm = _rows(la.shape) & (CHUNK - 1)
    bc, k = la, 1
    while k < CHUNK:
        bc = bc + jnp.where(rowm >= k, pltpu.roll(bc, k, 0), 0.0)
        k *= 2
    return g, jnp.exp(bc), jnp.exp(-bc)


GLA_HB = 4


def _gla_specs(rb, rmap):
    wk, wv = GLA_HB * HK, GLA_HB * HV
    return [pl.BlockSpec((rb, wk), lambda h, r: (rmap(h, r), OFF_Q // wk + h)),
            pl.BlockSpec((rb, wk), lambda h, r: (rmap(h, r), OFF_K // wk + h)),
            pl.BlockSpec((rb, wv), lambda h, r: (rmap(h, r), OFF_V // wv + h)),
            pl.BlockSpec((rb, 128), lambda h, r: (rmap(h, r), OFF_GK // 128))]


def _gla_fwd(zr, wgk, bgk, ghead, rb):
    s = zr.shape[0]
    nc = rb // CHUNK
    wk, wv = GLA_HB * HK, GLA_HB * HV

    def body(q_ref, k_ref, v_ref, zgk_ref, zog_ref, wgk_ref, bgk_ref, gh_ref, o_ref, og_ref, sp_ref, st_ref):
        @pl.when(pl.program_id(1) == 0)
        def _():
            st_ref[...] = jnp.zeros_like(st_ref)

        _, e_pos, e_neg = _gla_decay(zgk_ref, wgk_ref, bgk_ref, rb)
        lower = _rows((CHUNK, CHUNK)) >= lax.broadcasted_iota(jnp.int32, (CHUNK, CHUNK), 1)
        for c in range(nc):
            sl = slice(c * CHUNK, (c + 1) * CHUNK)
            for hh in range(GLA_HB):
                ck, cv = slice(hh * HK, (hh + 1) * HK), slice(hh * HV, (hh + 1) * HV)
                q = q_ref[sl, ck].astype(F32) * QSCALE
                k = k_ref[sl, ck].astype(F32)
                v = v_ref[sl, cv]
                ec, fc = e_pos[sl, ck], e_neg[sl, ck]
                qfw = (q * ec).astype(BF16)
                kfw_f = k * fc
                s_fw = _dot_nt(qfw, kfw_f.astype(BF16))
                s_bw = _dot_nt((q * fc).astype(BF16), (k * ec).astype(BF16))
                pm = jnp.where(lower, s_fw, s_bw).astype(BF16)
                st = st_ref[hh]
                stb = st.astype(BF16)
                sp_ref[c, hh] = stb
                o = _dot(pm, v) + _dot_nt(qfw, stb)
                e_last = _pick_row(ec, CHUNK - 1)
                kdec = (kfw_f * e_last).astype(BF16)
                st_ref[hh] = st * e_last + _dot_tn(v, kdec)
                r = lax.rsqrt(jnp.mean(o * o, axis=-1, keepdims=True) + EPS)
                zo = zog_ref[sl, cv].astype(F32)
                o_ref[sl, cv] = o.astype(BF16)
                og_ref[sl, cv] = (o * r * gh_ref[...] * zo * _sigmoid(zo)).astype(BF16)

    rmap = lambda h, r: r
    return pl.pallas_call(
        body, name="gla_fwd", grid=(HEADS // GLA_HB, s // rb),
        in_specs=_gla_specs(rb, rmap) + [
            pl.BlockSpec((rb, wv), lambda h, r: (r, OFF_OG // wv + h)),
            pl.BlockSpec((128, wk), lambda h, r: (0, h)), pl.BlockSpec((1, wk), lambda h, r: (0, h)),
            pl.BlockSpec((1, HV), lambda h, r: (0, 0))],
        out_specs=[pl.BlockSpec((rb, wv), lambda h, r: (r, h)), pl.BlockSpec((rb, wv), lambda h, r: (r, h)),
                   pl.BlockSpec((nc, GLA_HB, HV, HK), lambda h, r: (r, h, 0, 0))],
        out_shape=[jax.ShapeDtypeStruct((s, D), BF16), jax.ShapeDtypeStruct((s, D), BF16),
                   jax.ShapeDtypeStruct((s // CHUNK, HEADS, HV, HK), BF16)],
        scratch_shapes=[pltpu.VMEM((GLA_HB, HV, HK), F32)],
        compiler_params=_cp("arbitrary", "arbitrary"),
    )(zr, zr, zr, zr, zr, wgk, bgk, ghead)


def _gla_bwd(zr, do, sp, wgk, bgk, after, rb):
    s = zr.shape[0]
    nc = rb // CHUNK
    nr = s // rb
    wk, wv = GLA_HB * HK, GLA_HB * HV

    def body(q_ref, k_ref, v_ref, zgk_ref, do_ref, sp_ref, wgk_ref, bgk_ref, after_ref, dq_ref, dk_ref, dv_ref, dg_ref,
             gt_ref, dbc_ref):
        @pl.when(pl.program_id(1) == 0)
        def _():
            gt_ref[...] = jnp.zeros_like(gt_ref)

        g, e_pos, e_neg = _gla_decay(zgk_ref, wgk_ref, bgk_ref, rb)
        lower = _rows((CHUNK, CHUNK)) >= lax.broadcasted_iota(jnp.int32, (CHUNK, CHUNK), 1)
        is_last = _rows((CHUNK, HK)) == CHUNK - 1
        for c in reversed(range(nc)):
            sl = slice(c * CHUNK, (c + 1) * CHUNK)
            for hh in range(GLA_HB):
                ck, cv = slice(hh * HK, (hh + 1) * HK), slice(hh * HV, (hh + 1) * HV)
                q = q_ref[sl, ck].astype(F32) * QSCALE
                k = k_ref[sl, ck].astype(F32)
                v = v_ref[sl, cv]
                dov = do_ref[sl, cv]
                ec, fc = e_pos[sl, ck], e_neg[sl, ck]
                qfw_f, kfw_f, qbw_f, kbw_f = q * ec, k * fc, q * fc, k * ec
                qfw, kfw, qbw, kbw = qfw_f.astype(BF16), kfw_f.astype(BF16), qbw_f.astype(BF16), kbw_f.astype(BF16)
                pm = jnp.where(lower, _dot_nt(qfw, kfw), _dot_nt(qbw, kbw)).astype(BF16)
                e_last = _pick_row(ec, CHUNK - 1)
                kdec = (kfw_f * e_last).astype(BF16)
                gt = gt_ref[hh]
                gtb = gt.astype(BF16)
                spv = sp_ref[c, hh]
                dp = _dot_nt(dov, v)
                dv_ref[sl, cv] = (_dot_tn(pm, dov) + _dot_nt(kdec, gtb)).astype(BF16)
                ds_fw = jnp.where(lower, dp, 0.0).astype(BF16)
                ds_bw = jnp.where(lower, 0.0, dp).astype(BF16)
                dqfw = _dot(ds_fw, kfw) + _dot(dov, spv)
                dkfw = _dot_tn(ds_fw, qfw)
                dqbw = _dot(ds_bw, kbw)
                dkbw = _dot_tn(ds_bw, qbw)
                dkdec = _dot(v, gtb)
                de_last = (jnp.sum(gt * spv.astype(F32), axis=0, keepdims=True)
                           + jnp.sum(dkdec * kfw_f, axis=0, keepdims=True))
                dkfw = dkfw + dkdec * e_last
                dq_ref[sl, ck] = ((dqfw * ec + dqbw * fc) * QSCALE).astype(BF16)
                dk_ref[sl, ck] = (dkfw * fc + dkbw * ec).astype(BF16)
                dbc = dqfw * qfw_f - dqbw * qbw_f + dkbw * kbw_f - dkfw * kfw_f
                dbc_ref[sl, ck] = dbc + jnp.where(is_last, de_last * e_last, 0.0)
                gt_ref[hh] = _dot_tn(dov, qfw) + gt * e_last
        rowm = _rows((rb, wk)) & (CHUNK - 1)
        dla, kk = dbc_ref[...], 1
        while kk < CHUNK:
            dla = dla + jnp.where(rowm < CHUNK - kk, pltpu.roll(dla, rb - kk, 0), 0.0)
            kk *= 2
        dg_ref[...] = dla * (1.0 / 16.0) * _sigmoid(-g)

    rmap = lambda h, r: nr - 1 - r
    rev = lambda h, r: (nr - 1 - r, h)
    return pl.pallas_call(
        body, name="gla_bwd", grid=(HEADS // GLA_HB, nr),
        in_specs=_gla_specs(rb, rmap) + [
            pl.BlockSpec((rb, wv), rev),
            pl.BlockSpec((nc, GLA_HB, HV, HK), lambda h, r: (nr - 1 - r, h, 0, 0)),
            pl.BlockSpec((128, wk), lambda h, r: (0, h)), pl.BlockSpec((1, wk), lambda h, r: (0, h)), ANY],
        out_specs=[pl.BlockSpec((rb, wk), rev), pl.BlockSpec((rb, wk), rev), pl.BlockSpec((rb, wv), rev),
                   pl.BlockSpec((rb, wk), rev)],
        out_shape=[jax.ShapeDtypeStruct((s, HEADS * HK), BF16), jax.ShapeDtypeStruct((s, HEADS * HK), BF16),
                   jax.ShapeDtypeStruct((s, D), BF16), jax.ShapeDtypeStruct((s, HEADS * HK), F32)],
        scratch_shapes=[pltpu.VMEM((GLA_HB, HV, HK), F32), pltpu.VMEM((rb, wk), F32)],
        compiler_params=_cp("arbitrary", "arbitrary"),
    )(zr, zr, zr, zr, do, sp, wgk, bgk, after)


def _gk_bwd(dgpre, zr, wgk, after, ts):
    s = zr.shape[0]

    def body(dg_ref, zgk_ref, w_ref, after_ref, dz_ref, dw_ref, db_ref):
        @pl.when(pl.program_id(0) == 0)
        def _():
            dw_ref[...] = jnp.zeros_like(dw_ref)
            db_ref[...] = jnp.zeros_like(db_ref)

        dg = dg_ref[...]
        dgb = dg.astype(BF16)
        dz_ref[...] = _dot_nt(dgb, w_ref[...].astype(BF16)).astype(BF16)
        dw_ref[...] += _dot_tn(zgk_ref[...], dgb)
        db_ref[...] += jnp.sum(dg, axis=0, keepdims=True)

    return pl.pallas_call(
        body, name="gk_bwd", grid=(s // ts,),
        in_specs=[pl.BlockSpec((ts, 512), lambda i: (i, 0)), pl.BlockSpec((ts, 128), lambda i: (i, OFF_GK // 128)),
                  pl.BlockSpec((128, 512), lambda i: (0, 0)), ANY],
        out_specs=[pl.BlockSpec((ts, 128), lambda i: (i, 0)), pl.BlockSpec((128, 512), lambda i: (0, 0)),
                   pl.BlockSpec((1, 512), lambda i: (0, 0))],
        out_shape=[jax.ShapeDtypeStruct((s, 128), BF16), jax.ShapeDtypeStruct((128, 512), F32),
                   jax.ShapeDtypeStruct((1, 512), F32)],
        compiler_params=_cp("arbitrary"),
    )(dgpre, zr, wgk, after)


def _merge_fwd(x, zr, pp, og, bgate, wpp, wgla, wout, gffn, after, ts):
    s = x.shape[0]

    def body(x_ref, z0_ref, z1_ref, pp_ref, og_ref, bg_ref, wpp_ref, wgla_ref, wout_ref, gf_ref, after_ref,
             x1_ref, mix_ref, yp_ref, yg_ref, h2_ref):
        ppv = pp_ref[...]
        yp = jnp.concatenate([_dot(ppv, wpp_ref[j]) for j in range(4)], axis=1)
        yg = _dot(og_ref[...], wgla_ref[...])
        g0 = _sigmoid(z0_ref[...].astype(F32) + bg_ref[:, :D])
        g1 = _sigmoid(z1_ref[...].astype(F32) + bg_ref[:, D:])
        mixed = (g0 * yp + g1 * yg).astype(BF16)
        x1 = x_ref[...] + _dot(mixed, wout_ref[...])
        x1_ref[...] = x1
        mix_ref[...] = mixed
        yp_ref[...] = yp.astype(BF16)
        yg_ref[...] = yg.astype(BF16)
        r = lax.rsqrt(jnp.mean(x1 * x1, axis=-1, keepdims=True) + EPS)
        h2_ref[...] = (x1 * r * gf_ref[...]).astype(BF16)

    row = lambda i: (i, 0)
    const2 = lambda i: (0, 0)
    return pl.pallas_call(
        body, name="merge_fwd", grid=(s // ts,),
        in_specs=[pl.BlockSpec((ts, D), row), pl.BlockSpec((ts, D), lambda i: (i, 0)), pl.BlockSpec((ts, D), lambda i: (i, 1)),
                  pl.BlockSpec((ts, POOL_W), row), pl.BlockSpec((ts, D), row), pl.BlockSpec((1, 2 * D), const2),
                  pl.BlockSpec((4, POOL_W, 256), lambda i: (0, 0, 0)), pl.BlockSpec((D, D), const2),
                  pl.BlockSpec((D, D), const2), pl.BlockSpec((1, D), const2), ANY],
        out_specs=[pl.BlockSpec((ts, D), row)] * 5,
        out_shape=[jax.ShapeDtypeStruct((s, D), F32)] + [jax.ShapeDtypeStruct((s, D), BF16)] * 4,
        compiler_params=_cp("arbitrary"),
    )(x, zr, zr, pp, og, bgate, wpp, wgla, wout, gffn, after)


def _merge_bwd(dx1b, zr, yp, yg, o, bgate, ghead, wpp, wgla, wout, after, ts):
    s = dx1b.shape[0]

    def body(dx_ref, z0_ref, z1_ref, zog_ref, yp_ref, yg_ref, o_ref, bg_ref, gh_ref, wpp_ref, wgla_ref, wout_ref, after_ref,
             dzg_ref, dyp_ref, dyg_ref, dpp_ref, do_ref, dzog_ref, dbg_ref, dgh_ref):
        @pl.when(pl.program_id(0) == 0)
        def _():
            dbg_ref[...] = jnp.zeros_like(dbg_ref)
            dgh_ref[...] = jnp.zeros_like(dgh_ref)

        dmix = _dot_nt(dx_ref[...], wout_ref[...])
        g0 = _sigmoid(z0_ref[...].astype(F32) + bg_ref[:, :D])
        g1 = _sigmoid(z1_ref[...].astype(F32) + bg_ref[:, D:])
        dypb = (dmix * g0).astype(BF16)
        dygb = (dmix * g1).astype(BF16)
        dz0 = dmix * yp_ref[...].astype(F32) * g0 * (1.0 - g0)
        dz1 = dmix * yg_ref[...].astype(F32) * g1 * (1.0 - g1)
        dzg_ref[:, :D] = dz0.astype(BF16)
        dzg_ref[:, D:] = dz1.astype(BF16)
        dbg_ref[:, :D] += jnp.sum(dz0, axis=0, keepdims=True)
        dbg_ref[:, D:] += jnp.sum(dz1, axis=0, keepdims=True)
        dyp_ref[...] = dypb
        dyg_ref[...] = dygb
        dpp = _dot_nt(dypb[:, 0:256], wpp_ref[0])
        for j in range(1, 4):
            dpp = dpp + _dot_nt(dypb[:, j * 256:(j + 1) * 256], wpp_ref[j])
        dpp_ref[...] = dpp.astype(BF16)
        dog = _dot_nt(dygb, wgla_ref[...])
        gh = gh_ref[...]
        dgh = jnp.zeros((1, HV), F32)
        for h in range(HEADS):
            cs = slice(h * HV, (h + 1) * HV)
            ov = o_ref[:, cs].astype(F32)
            r = lax.rsqrt(jnp.mean(ov * ov, axis=-1, keepdims=True) + EPS)
            oh = ov * r
            zo = zog_ref[:, cs].astype(F32)
            sg = _sigmoid(zo)
            dog_h = dog[:, cs]
            don = dog_h * zo * sg
            dzog_ref[:, cs] = (dog_h * oh * gh * sg * (1.0 + zo * (1.0 - sg))).astype(BF16)
            dgh = dgh + jnp.sum(don * oh, axis=0, keepdims=True)
            doh = don * gh
            do_ref[:, cs] = (r * (doh - oh * jnp.mean(doh * oh, axis=-1, keepdims=True))).astype(BF16)
        dgh_ref[...] += dgh

    row = lambda i: (i, 0)
    const2 = lambda i: (0, 0)
    return pl.pallas_call(
        body, name="merge_bwd", grid=(s // ts,),
        in_specs=[pl.BlockSpec((ts, D), row), pl.BlockSpec((ts, D), lambda i: (i, 0)), pl.BlockSpec((ts, D), lambda i: (i, 1)),
                  pl.BlockSpec((ts, D), lambda i: (i, OFF_OG // D)), pl.BlockSpec((ts, D), row), pl.BlockSpec((ts, D), row),
                  pl.BlockSpec((ts, D), row), pl.BlockSpec((1, 2 * D), const2), pl.BlockSpec((1, HV), const2),
                  pl.BlockSpec((4, POOL_W, 256), lambda i: (0, 0, 0)), pl.BlockSpec((D, D), const2),
                  pl.BlockSpec((D, D), const2), ANY],
        out_specs=[pl.BlockSpec((ts, 2 * D), row), pl.BlockSpec((ts, D), row), pl.BlockSpec((ts, D), row),
                   pl.BlockSpec((ts, POOL_W), row), pl.BlockSpec((ts, D), row), pl.BlockSpec((ts, D), row),
                   pl.BlockSpec((1, 2 * D), const2), pl.BlockSpec((1, HV), const2)],
        out_shape=[jax.ShapeDtypeStruct((s, 2 * D), BF16), jax.ShapeDtypeStruct((s, D), BF16),
                   jax.ShapeDtypeStruct((s, D), BF16), jax.ShapeDtypeStruct((s, POOL_W), BF16),
                   jax.ShapeDtypeStruct((s, D), BF16), jax.ShapeDtypeStruct((s, D), BF16),
                   jax.ShapeDtypeStruct((1, 2 * D), F32), jax.ShapeDtypeStruct((1, HV), F32)],
        compiler_params=_cp("arbitrary"),
    )(dx1b, zr, zr, zr, yp, yg, o, bgate, ghead, wpp, wgla, wout, after)


HALO = 16
CCH = D_FF // 2


def _conv_taps(u_ref, halo_ref, cs, first, ts):
    u = u_ref[:, cs].astype(F32)
    hal = halo_ref[:, cs].astype(F32)
    h1 = jnp.where(first, 0.0, _pick_row(hal, HALO - 1))
    h2 = jnp.where(first, 0.0, _pick_row(hal, HALO - 2))
    row8 = _rows((8, u.shape[1]))
    r1, r2 = pltpu.roll(u, 1, 0), pltpu.roll(u, 2, 0)
    r1 = jnp.concatenate([jnp.where(row8 == 0, h1, r1[:8]), r1[8:]], axis=0)
    r2 = jnp.concatenate([jnp.where(row8 == 0, h2, jnp.where(row8 == 1, h1, r2[:8])), r2[8:]], axis=0)
    return u, r1, r2


def _ffn_down_loss(u, x1, tgt, wconv, bconv, wdown, gfin, ts):
    s = x1.shape[0]

    def body(u_ref, halo_ref, x1_ref, t_ref, wc_ref, bc_ref, wd_ref, gf_ref, a_ref, c_ref, dx_ref, dxb_ref, ls_ref,
             dgf_ref):
        i = pl.program_id(0)

        @pl.when(i == 0)
        def _():
            ls_ref[...] = jnp.zeros_like(ls_ref)
            dgf_ref[...] = jnp.zeros_like(dgf_ref)

        first = i == 0
        acc = x1_ref[...]
        for hf in range(D_FF // CCH):
            cg = slice(hf * CCH, (hf + 1) * CCH)
            cv = slice(D_FF + hf * CCH, D_FF + (hf + 1) * CCH)
            vals = []
            for cs in (cg, cv):
                u0, u1, u2 = _conv_taps(u_ref, halo_ref, cs, first, ts)
                vals.append(bc_ref[:, cs] + wc_ref[0:1, cs] * u2 + wc_ref[1:2, cs] * u1 + wc_ref[2:3, cs] * u0)
                c_ref[:, cs] = vals[-1].astype(BF16)
            a = (vals[0] * _sigmoid(vals[0]) * vals[1]).astype(BF16)
            a_ref[:, cg] = a
            acc = acc + _dot(a, wd_ref[cg, :])
        r = lax.rsqrt(jnp.mean(acc * acc, axis=-1, keepdims=True) + EPS)
        xh = acc * r
        gf = gf_ref[...]
        err = xh * gf - t_ref[...]
        ls_ref[...] += (0.5 / D) * jnp.sum(jnp.sum(err * err, axis=-1, keepdims=True), axis=0, keepdims=True)
        dy = err * (1.0 / D)
        dgf_ref[...] += jnp.sum(dy * xh, axis=0, keepdims=True)
        dxh = dy * gf
        dx = r * (dxh - xh * jnp.mean(dxh * xh, axis=-1, keepdims=True))
        dx_ref[...] = dx
        dxb_ref[...] = dx.astype(BF16)

    row = lambda i: (i, 0)
    const2 = lambda i: (0, 0)
    return pl.pallas_call(
        body, name="ffn_down_loss", grid=(s // ts,),
        in_specs=[pl.BlockSpec((ts, N_UP), row),
                  pl.BlockSpec((HALO, N_UP), lambda i: (jnp.maximum(i * (ts // HALO) - 1, 0), 0)),
                  pl.BlockSpec((ts, D), row), pl.BlockSpec((ts, D), row), pl.BlockSpec((3, N_UP), const2),
                  pl.BlockSpec((1, N_UP), const2), pl.BlockSpec((D_FF, D), const2), pl.BlockSpec((1, D), const2)],
        out_specs=[pl.BlockSpec((ts, D_FF), row), pl.BlockSpec((ts, N_UP), row), pl.BlockSpec((ts, D), row),
                   pl.BlockSpec((ts, D), row), pl.BlockSpec((1, 128), const2), pl.BlockSpec((1, D), const2)],
        out_shape=[jax.ShapeDtypeStruct((s, D_FF), BF16), jax.ShapeDtypeStruct((s, N_UP), BF16),
                   jax.ShapeDtypeStruct((s, D), F32), jax.ShapeDtypeStruct((s, D), BF16),
                   jax.ShapeDtypeStruct((1, 128), F32), jax.ShapeDtypeStruct((1, D), F32)],
        compiler_params=_cp("arbitrary"),
    )(u, u, x1, tgt, wconv, bconv, wdown, gfin)


def _ffn_bwd(dx2b, u, c, wconv, wdown, ts):
    s = dx2b.shape[0]
    nt = s // ts

    def body(dx_ref, u_ref, c_ref, wc_ref, wd_ref, du_ref, db_ref, dw_ref, nxt_ref):
        @pl.when(pl.program_id(0) == 0)
        def _():
            db_ref[...] = jnp.zeros_like(db_ref)
            dw_ref[...] = jnp.zeros_like(dw_ref)
            nxt_ref[...] = jnp.zeros_like(nxt_ref)

        dxv = dx_ref[...]
        row8 = _rows((8, CCH))
        for hf in range(D_FF // CCH):
            cg = slice(hf * CCH, (hf + 1) * CCH)
            cv = slice(D_FF + hf * CCH, D_FF + (hf + 1) * CCH)
            da = _dot_nt(dxv, wd_ref[cg, :])
            gate = c_ref[:, cg].astype(F32)
            val = c_ref[:, cv].astype(F32)
            sg = _sigmoid(gate)
            dcs = (da * val * sg * (1.0 + gate * (1.0 - sg)), da * gate * sg)
            for cs, dc in zip((cg, cv), dcs):
                n1 = nxt_ref[0:1, cs]
                n2 = nxt_ref[1:2, cs]
                r1, r2 = pltpu.roll(dc, ts - 1, 0), pltpu.roll(dc, ts - 2, 0)
                f1 = jnp.concatenate([r1[:ts - 8], jnp.where(row8 == 7, n1, r1[ts - 8:])], axis=0)
                f2 = jnp.concatenate([r2[:ts - 8], jnp.where(row8 == 7, n2, jnp.where(row8 == 6, n1, r2[ts - 8:]))], axis=0)
                uv = u_ref[:, cs].astype(F32)
                db_ref[:, cs] += jnp.sum(dc, axis=0, keepdims=True)
                dw_ref[0:1, cs] += jnp.sum(f2 * uv, axis=0, keepdims=True)
                dw_ref[1:2, cs] += jnp.sum(f1 * uv, axis=0, keepdims=True)
                dw_ref[2:3, cs] += jnp.sum(dc * uv, axis=0, keepdims=True)
                du_ref[:, cs] = (wc_ref[2:3, cs] * dc + wc_ref[1:2, cs] * f1 + wc_ref[0:1, cs] * f2).astype(BF16)
                nxt_ref[:, cs] = dc[0:8, :]

    rev = lambda i: (nt - 1 - i, 0)
    const2 = lambda i: (0, 0)
    return pl.pallas_call(
        body, name="ffn_bwd", grid=(nt,),
        in_specs=[pl.BlockSpec((ts, D), rev), pl.BlockSpec((ts, N_UP), rev), pl.BlockSpec((ts, N_UP), rev),
                  pl.BlockSpec((3, N_UP), const2), pl.BlockSpec((D_FF, D), const2)],
        out_specs=[pl.BlockSpec((ts, N_UP), rev), pl.BlockSpec((1, N_UP), const2), pl.BlockSpec((3, N_UP), const2)],
        out_shape=[jax.ShapeDtypeStruct((s, N_UP), BF16), jax.ShapeDtypeStruct((1, N_UP), F32),
                   jax.ShapeDtypeStruct((3, N_UP), F32)],
        scratch_shapes=[pltpu.VMEM((8, N_UP), F32)],
        compiler_params=_cp("arbitrary"),
    )(dx2b, u, c, wconv, wdown)


ANY = pl.BlockSpec(memory_space=pl.ANY)


def _place():
    x, y, c = lax.axis_index("x"), lax.axis_index("y"), lax.axis_index("c")
    chips = [(1 - x, y), (x, 1 - y), (1 - x, 1 - y)]
    return x, y, c, chips


def _half(shape, c, axis):
    size = shape[axis] // 2
    cut = pl.ds(pl.multiple_of(c * size, 8 if axis == 0 else 128), size)
    return (cut, slice(None)) if axis == 0 else (slice(None), cut)


def _half_shape(shape, axis):
    return (shape[0] // 2, shape[1]) if axis == 0 else (shape[0], shape[1] // 2)


def _remote(src, dst, send_sems, recv_sems, k, to):
    return pltpu.make_async_remote_copy(src_ref=src, dst_ref=dst, send_sem=send_sems.at[k], recv_sem=recv_sems.at[k],
                                        device_id=to, device_id_type=MESH)


def _sibling_exchange(grads, axes, smalls, name):
    nb = len(grads)
    n = nb + len(smalls)

    def body(*refs):
        ins, outs = refs[:n], refs[n:2 * n]
        send_sems, recv_sems = refs[2 * n:]
        x, y, c, _ = _place()
        sib = (x, y, 1 - c)
        cps = []
        for a in range(nb):
            theirs = _half(grads[a].shape[1:], 1 - c, axes[a])
            cps.append(_remote(ins[a].at[(slice(None),) + theirs], outs[a], send_sems, recv_sems, a, sib))
        for a in range(nb, n):
            cps.append(_remote(ins[a], outs[a], send_sems, recv_sems, a, sib))
        for cp in cps:
            cp.start()
        for cp in cps:
            cp.wait()

    out_shape = [jax.ShapeDtypeStruct((4,) + _half_shape(g.shape[1:], ax), g.dtype) for g, ax in zip(grads, axes)]
    out_shape += [jax.ShapeDtypeStruct(a.shape, F32) for a in smalls]
    return pl.pallas_call(
        body, name=name, in_specs=[ANY] * n, out_specs=[ANY] * n, out_shape=out_shape,
        scratch_shapes=[pltpu.SemaphoreType.DMA((n,)), pltpu.SemaphoreType.DMA((n,))],
        compiler_params=pltpu.CompilerParams(has_side_effects=True),
    )(*grads, *smalls)


def _gather_share(lands, axes, name):
    n = len(lands)

    def body(*refs):
        outs = refs[n:2 * n]
        send_sems, recv_sems = refs[2 * n:]
        x, y, c, chips = _place()
        sib = (x, y, 1 - c)
        cps = []
        for a in range(n):
            mine = _half(lands[a].shape[1:], c, axes[a])
            for k, ch in enumerate(chips):
                landed = outs[a].at[(2 * ch[0] + ch[1],) + mine]
                cps.append(_remote(landed, landed, send_sems, recv_sems, 3 * a + k, sib))
        for cp in cps:
            cp.start()
        for a in range(n):
            other = _half(lands[a].shape[1:], 1 - c, axes[a])
            for k, ch in enumerate(chips):
                landed = outs[a].at[(2 * ch[0] + ch[1],) + other]
                _remote(landed, landed, send_sems, recv_sems, 3 * a + k, sib).wait_recv()
        for cp in cps:
            cp.wait_send()

    return pl.pallas_call(
        body, name=name, in_specs=[ANY] * n, out_specs=[ANY] * n,
        out_shape=[jax.ShapeDtypeStruct(a.shape, a.dtype) for a in lands],
        input_output_aliases={a: a for a in range(n)},
        scratch_shapes=[pltpu.SemaphoreType.DMA((3 * n,)), pltpu.SemaphoreType.DMA((3 * n,))],
        compiler_params=pltpu.CompilerParams(has_side_effects=True),
    )(*lands)


def _sibling_share(halves, name):
    n = len(halves)

    def body(*refs):
        ins, outs = refs[:n], refs[n:2 * n]
        send_sems, recv_sems = refs[2 * n:]
        x, y, c, _ = _place()
        cps = [_remote(ins[a], outs[a], send_sems, recv_sems, a, (x, y, 1 - c)) for a in range(n)]
        for cp in cps:
            cp.start()
        for cp in cps:
            cp.wait()

    return pl.pallas_call(
        body, name=name, in_specs=[ANY] * n, out_specs=[ANY] * n,
        out_shape=[jax.ShapeDtypeStruct(h.shape, F32) for h in halves],
        scratch_shapes=[pltpu.SemaphoreType.DMA((n,)), pltpu.SemaphoreType.DMA((n,))],
        compiler_params=pltpu.CompilerParams(has_side_effects=True),
    )(*halves)


HBM = pl.BlockSpec(memory_space=pltpu.HBM)
SEM = pl.BlockSpec(memory_space=pltpu.SEMAPHORE)
DATAFLOW = pltpu.SideEffectType.DATAFLOW_SIDE_EFFECTING


def _split_start(name, srcs, land_shapes, plan, n_copies, after):
    lands = [lax.empty(*ls) if isinstance(ls, tuple) else ls for ls in land_shapes]
    bufs = list(srcs) + lands
    nb, ns = len(bufs), len(srcs)

    def body(*refs):
        send_sems, recv_sems, token = refs[nb + 1], refs[nb + 2], refs[-1]
        for k, (src, dst, to) in enumerate(plan(refs[:ns], refs[ns:nb])):
            _remote(src, dst, send_sems, recv_sems, k, to).start()
        token[...] = jnp.zeros_like(token)

    res = pl.pallas_call(
        body, name=name,
        out_shape=(pltpu.SemaphoreType.DMA((n_copies,)), pltpu.SemaphoreType.DMA((n_copies,)),
                   *[pltpu.HBM(b.shape, b.dtype) for b in bufs], jax.ShapeDtypeStruct((8, 128), F32)),
        in_specs=[HBM] * nb + [ANY],
        out_specs=(SEM, SEM, *[HBM] * nb, pl.BlockSpec(memory_space=pltpu.VMEM)),
        input_output_aliases={i: 2 + i for i in range(nb)},
        compiler_params=pltpu.CompilerParams(has_side_effects=DATAFLOW),
    )(*[pltpu.with_memory_space_constraint(b, pltpu.HBM) for b in bufs], after)
    return (res[0], res[1], list(res[2:2 + nb])), res[-1]


def _split_wait(name, handle, n_srcs, plan, after):
    send_sems, recv_sems, bufs = handle
    nb = len(bufs)

    def body(*refs):
        sends, recvs = refs[nb], refs[nb + 1]
        for k, (src, dst, to) in enumerate(plan(refs[:n_srcs], refs[n_srcs:nb])):
            cp = _remote(src, dst, sends, recvs, k, to)
            cp.wait_send()
            cp.wait_recv()

    res = pl.pallas_call(
        body, name=name, out_shape=[pltpu.HBM(b.shape, b.dtype) for b in bufs],
        in_specs=[HBM] * nb + [SEM, SEM, ANY], out_specs=[HBM] * nb,
        input_output_aliases={i: i for i in range(nb)},
        compiler_params=pltpu.CompilerParams(has_side_effects=DATAFLOW),
    )(*bufs, send_sems, recv_sems, after)
    return list(res[:n_srcs]), list(res[n_srcs:])


def _gather_plan(shapes, axes, n_whole=0):
    def plan(srcs, lands):
        x, y, c, chips = _place()
        out = []
        for a, (shape, axis) in enumerate(zip(shapes, axes)):
            mine = _half(shape, c, axis)
            for ch in chips:
                out.append((srcs[a].at[mine], lands[a].at[(2 * x + y,) + mine], (ch[0], ch[1], c)))
        for a in range(len(shapes), len(shapes) + n_whole):
            for ch in chips:
                out.append((srcs[a], lands[a].at[2 * x + y], (ch[0], ch[1], c)))
        return out
    return plan


def _share_plan(shapes, axes):
    def plan(srcs, lands):
        x, y, c, chips = _place()
        out = []
        for a, (shape, axis) in enumerate(zip(shapes, axes)):
            mine = _half(shape, c, axis)
            for ch in chips:
                landed = lands[a].at[(2 * ch[0] + ch[1],) + mine]
                out.append((landed, landed, (x, y, 1 - c)))
        return out
    return plan


def _sibling_plan(shapes, axes):
    def plan(srcs, lands):
        x, y, c, _ = _place()
        return [(srcs[a].at[(slice(None),) + _half(shape, 1 - c, axis)], lands[a], (x, y, 1 - c))
                for a, (shape, axis) in enumerate(zip(shapes, axes))]
    return plan


def _reduce_plan(n_big, n_small):
    def plan(srcs, lands):
        x, y, c, chips = _place()
        out = []
        for a in range(n_big):
            for k, ch in enumerate(chips):
                out.append((srcs[a].at[2 * ch[0] + ch[1]], lands[a].at[k], (ch[0], ch[1], c)))
        for a in range(n_big, n_big + n_small):
            for ch in chips:
                out.append((srcs[a], lands[a].at[2 * x + y], (ch[0], ch[1], c)))
        return out
    return plan


def _row_tile(rows, cols, mult):
    best = mult
    for t in range(mult, rows + 1, mult):
        if rows % t == 0 and t * cols * 4 <= (2 << 20):
            best = t
    return best if rows % best == 0 else rows


COL_TILE = 256


def _half_tiling(hshape, axis, mult):
    hr, hc = hshape
    if axis == 0:
        tr = _row_tile(hr, hc, mult)
        return tr, hc, hr // tr
    return hr, COL_TILE, hc // COL_TILE


def _tile_idx(axis, t):
    return (t, 0) if axis == 0 else (0, t)


def _chip_partial(place, g, t, axis, name):
    hshape = t.shape[1:]
    br, bc, nt = _half_tiling(hshape, axis, 16)

    def body(pl_ref, g_ref, t_ref, pf_ref, pb_ref):
        v = g_ref[...].astype(F32) + t_ref[...].astype(F32)
        pb_ref[...] = v.astype(BF16)

        @pl.when(pl.program_id(1) == pl_ref[0])
        def _():
            pf_ref[...] = v

    blk = (None, br, bc)
    return pl.pallas_call(
        body, name=name,
        grid_spec=pltpu.PrefetchScalarGridSpec(
            num_scalar_prefetch=1, grid=(nt, 4),
            in_specs=[pl.BlockSpec(blk, lambda i, j, p: (j,) + _tile_idx(axis, p[1] * nt + i)),
                      pl.BlockSpec(blk, lambda i, j, p: (j,) + _tile_idx(axis, i))],
            out_specs=[pl.BlockSpec((br, bc), lambda i, j, p: _tile_idx(axis, i)),
                       pl.BlockSpec(blk, lambda i, j, p: (j,) + _tile_idx(axis, i))]),
        out_shape=[jax.ShapeDtypeStruct(hshape, F32), jax.ShapeDtypeStruct((4,) + hshape, BF16)],
        compiler_params=_cp("arbitrary", "arbitrary"),
    )(place, g, t)


def _finish_half(pf, rb, axis, name):
    hshape = pf.shape
    br, bc, nt = _half_tiling(hshape, axis, 16)

    def body(pf_ref, rb_ref, o_ref):
        o_ref[...] = ((pf_ref[...] + rb_ref[0].astype(F32)) + rb_ref[1].astype(F32)) + rb_ref[2].astype(F32)

    return pl.pallas_call(
        body, name=name, grid=(nt,),
        in_specs=[pl.BlockSpec((br, bc), lambda i: _tile_idx(axis, i)),
                  pl.BlockSpec((3, br, bc), lambda i: (0,) + _tile_idx(axis, i))],
        out_specs=pl.BlockSpec((br, bc), lambda i: _tile_idx(axis, i)),
        out_shape=jax.ShapeDtypeStruct(hshape, F32),
        compiler_params=_cp("arbitrary"),
    )(pf, rb)


def _adam_math(w, g, m, v):
    m = ADAM_B1 * m + (1.0 - ADAM_B1) * g
    v = ADAM_B2 * v + (1.0 - ADAM_B2) * (g * g)
    m_hat = m / (1.0 - ADAM_B1 ** ADAM_STEP)
    v_hat = v / (1.0 - ADAM_B2 ** ADAM_STEP)
    return -ADAM_LR * (m_hat / (jnp.sqrt(v_hat) + ADAM_EPS) + ADAM_WD * w), m, v


def _adam_halves(place, w, mine, theirs, m, v, axis, name):
    br, bc, nt = _half_tiling(mine.shape, axis, 8)

    def body(pl_ref, w_ref, a_ref, b_ref, m_ref, v_ref, g_ref, d_ref, mo_ref, vo_ref):
        is_mine = pl.program_id(0) // nt == pl_ref[1]
        g = jnp.where(is_mine, a_ref[...], b_ref[...])
        d, mn, vn = _adam_math(w_ref[...], g, m_ref[...], v_ref[...])
        g_ref[...] = g
        d_ref[...] = d
        mo_ref[...] = mn
        vo_ref[...] = vn

    full = pl.BlockSpec((br, bc), lambda i, p: _tile_idx(axis, i))
    mine_spec = pl.BlockSpec((br, bc), lambda i, p: _tile_idx(axis, jnp.where(i // nt == p[1], i % nt, nt - 1)))
    theirs_spec = pl.BlockSpec((br, bc), lambda i, p: _tile_idx(axis, jnp.where(i // nt == p[1], 0, i % nt)))
    return pl.pallas_call(
        body, name=name,
        grid_spec=pltpu.PrefetchScalarGridSpec(
            num_scalar_prefetch=1, grid=(2 * nt,), in_specs=[full, mine_spec, theirs_spec, full, full],
            out_specs=[full] * 4),
        out_shape=[jax.ShapeDtypeStruct(w.shape, F32)] * 4, compiler_params=_cp("arbitrary"),
    )(place, w, mine, theirs, m, v)


def _add_many(xs, ys, name):
    n = len(xs)

    def body(*refs):
        for i in range(n):
            refs[2 * n + i][...] = refs[i][...] + refs[n + i][...]

    return pl.pallas_call(body, name=name, out_shape=[jax.ShapeDtypeStruct(a.shape, F32) for a in xs])(*xs, *ys)


def _adam_small(place, owns, landed, ws, ms, vs, widths):
    n, nw = len(owns), len(ws)

    def body(pl_ref, *refs):
        own_r, land_r = refs[:n], refs[n:2 * n]
        w_r, m_r, v_r = (refs[2 * n + k * nw:2 * n + (k + 1) * nw] for k in range(3))
        outs = refs[2 * n + 3 * nw:]
        g_o, d_o, m_o, v_o = outs[:n], outs[n:n + nw], outs[n + nw:n + 2 * nw], outs[n + 2 * nw:]
        for me in range(4):
            @pl.when(pl_ref[0] == me)
            def _(me=me):
                for i in range(n):
                    p = [own_r[i][...] if k == me else land_r[i][k] for k in range(4)]
                    g = ((p[0] + p[1]) + p[2]) + p[3]
                    if i < nw and widths[i]:
                        g = g[:, me * widths[i]:(me + 1) * widths[i]]
                    g_o[i][...] = g
                    if i < nw:
                        d, mn, vn = _adam_math(w_r[i][...], g, m_r[i][...], v_r[i][...])
                        d_o[i][...] = d
                        m_o[i][...] = mn
                        v_o[i][...] = vn

    g_shapes = [jax.ShapeDtypeStruct(ws[i].shape if i < nw else owns[i].shape, F32) for i in range(n)]
    w_shapes = [jax.ShapeDtypeStruct(w.shape, F32) for w in ws]
    whole = lambda a: pl.BlockSpec(a.shape, lambda i, p, nd=len(a.shape): (0,) * nd)
    ins = list(owns) + list(landed) + list(ws) + list(ms) + list(vs)
    out_shape = g_shapes + w_shapes * 3
    out = pl.pallas_call(
        body, name="adam_small",
        grid_spec=pltpu.PrefetchScalarGridSpec(num_scalar_prefetch=1, grid=(1,), in_specs=[whole(a) for a in ins],
                                               out_specs=[whole(a) for a in out_shape]),
        out_shape=out_shape, compiler_params=_cp("arbitrary"),
    )(place, *ins)
    return out[:n], out[n:n + nw], out[n + nw:n + 2 * nw], out[n + 2 * nw:]


def kernel(x, g_mix, w_in, b_gate, w_gk_up, b_gk, w_pool_grp, pool_scale, g_gla_head, w_pool_proj, w_gla_proj, w_out, g_ffn, w_up, w_conv, b_conv, w_down, g_final, loss_target, m_g_mix, m_w_in, m_b_gate, m_w_gk_up, m_b_gk, m_w_pool_grp, m_pool_scale, m_g_gla_head, m_w_pool_proj, m_w_gla_proj, m_w_out, m_g_ffn, m_w_up, m_w_conv, m_b_conv, m_w_down, m_g_final, v_g_mix, v_w_in, v_b_gate, v_w_gk_up, v_b_gk, v_w_pool_grp, v_pool_scale, v_g_gla_head, v_w_pool_proj, v_w_gla_proj, v_w_out, v_g_ffn, v_w_up, v_w_conv, v_b_conv, v_w_down, v_g_final):
    s = x.shape[1]
    ts = min(s, 512)
    tm = min(s, 256)
    cx, cy, cc = lax.axis_index("x"), lax.axis_index("y"), lax.axis_index("c")
    chip = 2 * cx + cy
    place = jnp.stack([chip, cc]).astype(jnp.int32)

    big_names = ("w_in", "w_pool_proj", "w_gla_proj", "w_out", "w_up", "w_down")
    axes = (1, 0, 0, 0, 0, 0)
    shards = dict(w_in=jnp.transpose(w_in[0]), w_pool_proj=w_pool_proj[0], w_gla_proj=w_gla_proj[0], w_out=w_out[0],
                  w_up=w_up[0], w_down=w_down[0])
    def fill_own(lands, mine):
        return [lax.dynamic_update_slice(g, o_[None], (chip, 0, 0)) for g, o_ in zip(lands, mine)]

    def gather_start(tag, halves, group_axes, whole, after):
        plan = _gather_plan([o_.shape for o_ in halves], group_axes, len(whole))
        srcs = list(halves) + list(whole)
        handle, token = _split_start("gather_" + tag + "_start", srcs, [((4,) + o_.shape, o_.dtype) for o_ in srcs], plan,
                                     3 * len(srcs), after)
        return (handle, plan, len(halves), len(srcs), group_axes), token

    def gather_finish(tag, started, after):
        handle, plan, n_halves, n, group_axes = started
        mine, lands = _split_wait("gather_" + tag + "_wait", handle, n, plan, after)
        lands[:n_halves] = _gather_share(lands[:n_halves], group_axes, "gather_" + tag + "_share")
        return fill_own(lands, mine)

    in_w, tok = gather_start("in", [jnp.transpose(w_in[0].astype(BF16))], axes[:1], [], g_mix)
    zero = tok[0, 0]
    own = [(shards[n] + zero).astype(BF16) for n in big_names[1:]]
    mix_w, tok = gather_start("mix", own[0:3], axes[1:4], [w_gk_up[0] + zero, w_conv[0] + zero], tok)
    up_w, tok = gather_start("up", own[3:4], axes[4:5], [], tok)
    down_w, tok = gather_start("down", own[4:5], axes[5:6], [], tok)

    def forward_start(tag, started, after):
        handle, plan, _, n, group_axes = started
        mine, lands = _split_wait("gather_" + tag + "_wait", handle, n, plan, after)
        plan = _share_plan([o_.shape for o_ in mine], group_axes)
        share, token = _split_start("gather_" + tag + "_share_start", [], lands, plan, 3 * n, after)
        return (share, plan, mine), token

    def forward_done(tag, forwarded, after):
        share, plan, mine = forwarded
        return fill_own(_split_wait("gather_" + tag + "_share_wait", share, 0, plan, after)[1], mine)
    xs, tgt = x[0], loss_target[0]
    wgrp = w_pool_grp[0]
    h = _rmsnorm(xs, g_mix, tok, "norm_mix", ts)
    m_in_t, v_in_t = jnp.transpose(m_w_in[0]), jnp.transpose(v_w_in[0])
    h, m_in_t, v_in_t = lax.optimization_barrier((h, m_in_t, v_in_t))
    w_in_t = gather_finish("in", in_w, h)[0].reshape(N_IN, D)
    nsh = N_IN // 4

    zr = _in_proj(h, w_in_t, PROJ_TILE)
    p, pp = _pool_fwd(zr, wgrp, pool_scale)
    wpp, wgla, wout, wgk4, wconv4 = gather_finish("mix", mix_w, pp)
    wgla, wout = wgla.reshape(D, D), wout.reshape(D, D)
    wgk_full = jnp.transpose(wgk4, (1, 0, 2)).reshape(GATE_RANK, 512)
    wconv_full = jnp.transpose(wconv4, (1, 0, 2)).reshape(3, N_UP)
    wgk_pad = jnp.concatenate([wgk_full, jnp.zeros((128 - GATE_RANK, 512), F32)], axis=0)
    o, og, sp = _gla_fwd(zr, wgk_pad, b_gk, g_gla_head, ts)
    up_f, tok = forward_start("up", up_w, og)
    x1, mixed, yp, yg, h2 = _merge_fwd(xs, zr, pp, og, b_gate, wpp, wgla, wout, g_ffn, tok, ts)
    wup, = forward_done("up", up_f, x1)
    down_f, tok = forward_start("down", down_w, x1)
    u = _matmul_resident(h2, wup, tok, "ffn_up")
    wdown = forward_done("down", down_f, u)[0].reshape(D_FF, D)
    a, conv_out, dx2, dx2b, loss_part, dgfin = _ffn_down_loss(u, x1, tgt, wconv_full, b_conv, wdown,
                                                              g_final.reshape(1, D), tm)

    du, dbconv, dwconv = _ffn_bwd(dx2b, u, conv_out, wconv_full, wdown, tm)
    dw_down = _matmul_tn(a, dx2b, "dw_down", D, tm=D_FF // 2)
    dw_up = _matmul_tn(h2, du, "dw_up", UP_SHARD, shard_major=True)

    def exchange_start(tag, grads, group_axes, after):
        plan = _sibling_plan([g.shape[1:] for g in grads], group_axes)
        lands = [((4,) + _half_shape(g.shape[1:], ax), g.dtype) for g, ax in zip(grads, group_axes)]
        handle, token = _split_start("sibling_" + tag + "_start", grads, lands, plan, len(grads), after)
        return (handle, plan, len(grads)), token

    def partials(tag, names, group_axes, exchange, after):
        handle, plan, n = exchange
        mine, theirs = _split_wait("sibling_" + tag + "_wait", handle, n, plan, after)
        return zip(*[_chip_partial(place, g, t, ax, "chip_partial_" + nm)
                     for nm, ax, g, t in zip(names, group_axes, mine, theirs)])

    ffn_names, ffn_axes = ("w_up", "w_down"), (0, 0)
    ffn_x, token = exchange_start("ffn", [dw_up, dw_down.reshape(4, 704, D)], ffn_axes, du)
    dx1, dx1b, dgffn = _matmul_nt_normbwd(du, wup, x1, g_ffn, dx2, token, "ffn_up_bwd", ts)
    ffn_pf, ffn_pb = partials("ffn", ffn_names, ffn_axes, ffn_x, dx1b)
    ffn_plan = _reduce_plan(2, 0)
    ffn_handle, token = _split_start("reduce_ffn_start", ffn_pb, [((3,) + p.shape[1:], BF16) for p in ffn_pb],
                                     ffn_plan, 6, ffn_pf[0])

    dzg, dyp, dyg, dpp, do, dzog, dbgate, dghead = _merge_bwd(dx1b, zr, yp, yg, o, b_gate, g_gla_head, wpp, wgla, wout,
                                                             token, ts)
    dw_out = _matmul_tn(mixed, dx1b, "dw_out", D, tm=512)
    dw_gla = _matmul_tn(og, dyg, "dw_gla", D, tm=512)
    dw_pp = _matmul_tn(pp, dyp, "dw_pp", 256, shard_major=True)

    out_names, out_axes = ("w_pool_proj", "w_gla_proj", "w_out"), (0, 0, 0)
    out_x, token = exchange_start("out", [dw_pp, dw_gla.reshape(4, 256, D), dw_out.reshape(4, 256, D)], out_axes, dpp)
    dzp, dwgrp, dscale = _pool_bwd(p, dpp, wgrp, pool_scale, token)
    out_pf, out_pb = partials("out", out_names, out_axes, out_x, dzp)
    out_plan = _reduce_plan(3, 0)
    out_handle, token = _split_start("reduce_out_start", out_pb, [((3,) + p_.shape[1:], BF16) for p_ in out_pb],
                                     out_plan, 9, out_pf[0])
    dq, dk, dv, dgpre = _gla_bwd(zr, do, sp, wgk_pad, b_gk, token, ts)
    dzgk, dwgk, dbgk = _gk_bwd(dgpre, zr, wgk_pad, dgpre, ts)
    dzr = jnp.concatenate([dzg, dv, dzog, dzp, dq, dk, dzgk], axis=1)
    dw_rt = _matmul_tn(dzr, h, "dw_in", D, tm=PROJ_TILE)

    def grad_rows(lo, hi):
        out = []
        for seg_lo, seg_hi, at in ((0, 1536, OFF_POOL), (1536, 3584, OFF_V), (3584, 3600, OFF_GK), (3600, N_IN, OFF_GATE)):
            a_, b_ = max(lo, seg_lo), min(hi, seg_hi)
            if a_ < b_:
                out.append(dw_rt[at + a_ - seg_lo:at + b_ - seg_lo])
        return jnp.concatenate(out, axis=0)

    dw_in_t = jnp.stack([grad_rows(j * nsh, (j + 1) * nsh) for j in range(4)])

    in_sib = _sibling_exchange([dw_in_t], (1,), [], "sibling_exchange_in")
    in_pf, in_pb = _chip_partial(place, dw_in_t, in_sib[0], 1, "chip_partial_w_in")
    in_plan = _reduce_plan(1, 0)
    in_handle, token = _split_start("reduce_in_start", [in_pb], [((3,) + in_pb.shape[1:], BF16)], in_plan, 3, in_pf)
    grad_x, _, dgmix = _matmul_nt_normbwd(dzr, w_in_t, xs, g_mix, dx1, token, "in_proj_bwd", ts, transposed=True)
    small_names = ("g_mix", "b_gate", "w_gk_up", "b_gk", "w_pool_grp", "pool_scale", "g_gla_head", "g_ffn", "w_conv",
                   "b_conv", "g_final")
    small_mine = [dgmix, dbgate, dwgk[:GATE_RANK], dbgk, dwgrp.reshape(4 * 128, 128), dscale, dghead, dgffn, dwconv, dbconv,
                  dgfin, loss_part]
    small_sib = _sibling_exchange([], (), small_mine, "sibling_exchange_small")
    small_chip = _add_many(small_mine, small_sib, "chip_partial_small")
    small_plan = _reduce_plan(0, len(small_chip))
    small_handle, token = _split_start("reduce_small_start", small_chip, [((4,) + a_.shape, F32) for a_ in small_chip],
                                       small_plan, 3 * len(small_chip), small_mine[0])

    ms = dict(w_in=m_in_t, w_pool_proj=m_w_pool_proj[0], w_gla_proj=m_w_gla_proj[0], w_out=m_w_out[0],
              w_up=m_w_up[0], w_down=m_w_down[0])
    vs = dict(w_in=v_in_t, w_pool_proj=v_w_pool_proj[0], w_gla_proj=v_w_gla_proj[0], w_out=v_w_out[0],
              w_up=v_w_up[0], w_down=v_w_down[0])
    grad, delta, new_m, new_v = {}, {}, {}, {}

    def finish_and_update(names, group_axes, part_f, landed, tag):
        halves = [_finish_half(pf, rb, ax, "finish_" + n) for n, ax, pf, rb in zip(names, group_axes, part_f, landed)]
        sib_halves = _sibling_share(halves, "sibling_share_" + tag)
        for n, ax, mine, theirs in zip(names, group_axes, halves, sib_halves):
            res = _adam_halves(place, shards[n], mine, theirs, ms[n], vs[n], ax, "adam_" + n)
            if n == "w_in":
                res = [jnp.transpose(r_) for r_ in res]
            grad[n], delta[n], new_m[n], new_v[n] = [r_[None] for r_ in res]

    _, ffn_landed = _split_wait("reduce_ffn_wait", ffn_handle, 2, ffn_plan, token)
    _, out_landed = _split_wait("reduce_out_wait", out_handle, 3, out_plan, ffn_landed[0])
    finish_and_update(ffn_names + out_names, ffn_axes + out_axes, ffn_pf + out_pf, ffn_landed + out_landed, "rest")
    _, in_landed = _split_wait("reduce_in_wait", in_handle, 1, in_plan, delta["w_out"])
    finish_and_update(("w_in",), (1,), (in_pf,), in_landed, "in")
    small_sent, small_landed = _split_wait("reduce_small_wait", small_handle, len(small_chip), small_plan, delta["w_in"])
    given = dict(g_mix=(g_mix, m_g_mix, v_g_mix), b_gate=(b_gate, m_b_gate, v_b_gate), w_gk_up=(w_gk_up, m_w_gk_up, v_w_gk_up),
                 b_gk=(b_gk, m_b_gk, v_b_gk), w_pool_grp=(w_pool_grp, m_w_pool_grp, v_w_pool_grp),
                 pool_scale=(pool_scale, m_pool_scale, v_pool_scale), g_gla_head=(g_gla_head, m_g_gla_head, v_g_gla_head),
                 g_ffn=(g_ffn, m_g_ffn, v_g_ffn), w_conv=(w_conv, m_w_conv, v_w_conv), b_conv=(b_conv, m_b_conv, v_b_conv),
                 g_final=(g_final, m_g_final, v_g_final))
    flat2 = lambda a: a.reshape(-1, a.shape[-1])
    widths = [dict(w_gk_up=HK, w_conv=UP_SHARD).get(n) for n in small_names]
    totals, ds, mo, vo = _adam_small(place, small_sent, small_landed, *[[flat2(given[n][k]) for n in small_names] for k in range(3)],
                                     widths)
    loss = totals[-1][0, 0]
    for i, n in enumerate(small_names):
        shp = given[n][0].shape
        grad[n], delta[n], new_m[n], new_v[n] = [r_.reshape(shp) for r_ in (totals[i], ds[i], mo[i], vo[i])]

    order = ("g_mix", "w_in", "b_gate", "w_gk_up", "b_gk", "w_pool_grp", "pool_scale", "g_gla_head", "w_pool_proj",
             "w_gla_proj", "w_out", "g_ffn", "w_up", "w_conv", "b_conv", "w_down", "g_final")
    return (loss, grad_x[None], *[grad[n] for n in order], *[delta[n] for n in order], *[new_m[n] for n in order],
            *[new_v[n] for n in order])
```

```python
import jax
import jax.numpy as jnp
from jax import lax
from jax.experimental import pallas as pl
from jax.experimental.pallas import tpu as pltpu

F32 = jnp.float32
BF16 = jnp.bfloat16
MESH = pl.DeviceIdType.MESH

D = 1024
EPS = 1e-6
CHUNK = 64
POOL_W = 512
POOL_WINDOWS = (2, 4, 8, 16)
HEADS = 4
HK = 128
HV = 256
GATE_RANK = 16
D_FF = 2816
N_UP = 2 * D_FF
N_IN = 5648
QSCALE = HK ** -0.5
N_INR = 5760
OFF_GATE, OFF_V, OFF_OG, OFF_POOL, OFF_Q, OFF_K, OFF_GK = 0, 2048, 3072, 4096, 4608, 5120, 5632

ADAM_LR, ADAM_B1, ADAM_B2, ADAM_EPS, ADAM_WD, ADAM_STEP = 0.001, 0.9, 0.999, 1e-08, 0.01, 10

VMEM_LIMIT = 56 * 1024 * 1024
PROJ_TILE = N_INR // 5
UP_SHARD = N_UP // 4


def _cp(*sem):
    return pltpu.CompilerParams(dimension_semantics=sem if sem else None, vmem_limit_bytes=VMEM_LIMIT)


def _dot(a, b):
    return jnp.dot(a, b, preferred_element_type=F32)


def _dot_nt(a, b):
    return lax.dot_general(a, b, (((1,), (1,)), ((), ())), preferred_element_type=F32)


def _dot_tn(a, b):
    return lax.dot_general(a, b, (((0,), (0,)), ((), ())), preferred_element_type=F32)


def _sigmoid(v):
    return 1.0 / (1.0 + jnp.exp(-v))


def _rows(shape):
    return lax.broadcasted_iota(jnp.int32, shape, 0)


def _pick_row(v, r):
    return jnp.sum(jnp.where(_rows(v.shape) == r, v, 0.0), axis=0, keepdims=True)


def _rmsnorm(x, g, after, name, ts):
    s = x.shape[0]

    def body(x_ref, g_ref, after_ref, h_ref):
        xv = x_ref[...]
        r = lax.rsqrt(jnp.mean(xv * xv, axis=-1, keepdims=True) + EPS)
        h_ref[...] = (xv * r * g_ref[...]).astype(BF16)

    return pl.pallas_call(
        body, name=name, grid=(s // ts,),
        in_specs=[pl.BlockSpec((ts, D), lambda i: (i, 0)), pl.BlockSpec((1, D), lambda i: (0, 0)), ANY],
        out_specs=pl.BlockSpec((ts, D), lambda i: (i, 0)), out_shape=jax.ShapeDtypeStruct((s, D), BF16),
        compiler_params=_cp("arbitrary"),
    )(x, g, after)


MM_ROWS = 512


def _matmul_resident(h, w, after, name):
    s = h.shape[0]
    nj, tn = w.shape[0], w.shape[2]
    rc = min(s, MM_ROWS)

    def body(h_ref, w_ref, after_ref, z_ref):
        for r0 in range(0, s, rc):
            z_ref[r0:r0 + rc, :] = _dot(h_ref[r0:r0 + rc, :], w_ref[...]).astype(BF16)

    return pl.pallas_call(
        body, name=name, grid=(nj,),
        in_specs=[pl.BlockSpec((s, D), lambda j: (0, 0)), pl.BlockSpec((None, D, tn), lambda j: (j, 0, 0)), ANY],
        out_specs=pl.BlockSpec((s, tn), lambda j: (0, j)), out_shape=jax.ShapeDtypeStruct((s, nj * tn), BF16),
        compiler_params=_cp("arbitrary"),
    )(h, w, after)


PROJ_PIECES = ((3600, 2048, OFF_GATE), (1536, 2048, OFF_V), (0, 1536, OFF_POOL), (3584, GATE_RANK, OFF_GK))


def _projection_copies(w_hbm, w_ref, sems):
    return [pltpu.make_async_copy(w_hbm.at[pl.ds(src, n)], w_ref.at[pl.ds(dst, n)], sems.at[i])
            for i, (src, n, dst) in enumerate(PROJ_PIECES)]


def _load_projection(w_hbm, w_ref, sems):
    cps = _projection_copies(w_hbm, w_ref, sems)
    for cp in cps:
        cp.start()
    w_ref[OFF_GK + GATE_RANK:, :] = jnp.zeros((N_INR - OFF_GK - GATE_RANK, D), BF16)
    for cp in cps:
        cp.wait()


def _in_proj(h, w_nat, tn):
    s = h.shape[0]
    rc = min(s, MM_ROWS)
    nj = N_INR // tn
    first_use = [dst // tn for _, _, dst in PROJ_PIECES]

    def body(h_ref, w_hbm, z_ref, w_ref, sems):
        j = pl.program_id(0)
        cps = _projection_copies(w_hbm, w_ref, sems)

        @pl.when(j == 0)
        def _():
            for cp in cps:
                cp.start()
            w_ref[OFF_GK + GATE_RANK:, :] = jnp.zeros((N_INR - OFF_GK - GATE_RANK, D), BF16)

        for step in range(nj):
            due = [cp for cp, at in zip(cps, first_use) if at == step]
            if due:
                @pl.when(j == step)
                def _(due=due):
                    for cp in due:
                        cp.wait()

        wt = w_ref[pl.ds(pl.multiple_of(j * tn, 128), tn), :]
        for r0 in range(0, s, rc):
            z_ref[r0:r0 + rc, :] = _dot_nt(h_ref[r0:r0 + rc, :], wt).astype(BF16)

    return pl.pallas_call(
        body, name="in_proj", grid=(nj,),
        in_specs=[pl.BlockSpec((s, D), lambda j: (0, 0)), ANY],
        out_specs=pl.BlockSpec((s, tn), lambda j: (0, j)), out_shape=jax.ShapeDtypeStruct((s, N_INR), BF16),
        scratch_shapes=[pltpu.VMEM((N_INR, D), BF16), pltpu.SemaphoreType.DMA((len(PROJ_PIECES),))],
        compiler_params=_cp("arbitrary"),
    )(h, w_nat)


def _matmul_nt_normbwd(dz, w, x, g, resid, after, name, ts, transposed=False):
    s = x.shape[0]
    w_vmem = (N_INR, D) if transposed else (D, w.shape[0] * w.shape[2])
    n_sems = len(PROJ_PIECES) if transposed else w.shape[0]

    def body(dz_ref, w_hbm, x_ref, g_ref, r_ref, after_ref, o_ref, ob_ref, dg_ref, w_ref, sems):
        @pl.when(pl.program_id(0) == 0)
        def _():
            if transposed:
                _load_projection(w_hbm, w_ref, sems)
            else:
                kc = w.shape[2]
                cps = [pltpu.make_async_copy(w_hbm.at[j], w_ref.at[:, pl.ds(j * kc, kc)], sems.at[j])
                       for j in range(w.shape[0])]
                for cp in cps:
                    cp.start()
                for cp in cps:
                    cp.wait()
            dg_ref[...] = jnp.zeros_like(dg_ref)

        dh = _dot(dz_ref[...], w_ref[...]) if transposed else _dot_nt(dz_ref[...], w_ref[...])
        xv = x_ref[...]
        r = lax.rsqrt(jnp.mean(xv * xv, axis=-1, keepdims=True) + EPS)
        xh = xv * r
        dg_ref[...] += jnp.sum(dh * xh, axis=0, keepdims=True)
        dxh = dh * g_ref[...]
        out = r_ref[...] + r * (dxh - xh * jnp.mean(dxh * xh, axis=-1, keepdims=True))
        o_ref[...] = out
        ob_ref[...] = out.astype(BF16)

    row = lambda i: (i, 0)
    kdim = dz.shape[1]
    return pl.pallas_call(
        body, name=name, grid=(s // ts,),
        in_specs=[pl.BlockSpec((ts, kdim), row), ANY, pl.BlockSpec((ts, D), row),
                  pl.BlockSpec((1, D), lambda i: (0, 0)), pl.BlockSpec((ts, D), row), ANY],
        out_specs=[pl.BlockSpec((ts, D), row), pl.BlockSpec((ts, D), row), pl.BlockSpec((1, D), lambda i: (0, 0))],
        out_shape=[jax.ShapeDtypeStruct((s, D), F32), jax.ShapeDtypeStruct((s, D), BF16),
                   jax.ShapeDtypeStruct((1, D), F32)],
        scratch_shapes=[pltpu.VMEM(w_vmem, BF16), pltpu.SemaphoreType.DMA((n_sems,))],
        compiler_params=_cp("arbitrary"),
    )(dz, w, x, g, resid, after)


def _matmul_tn(a, b, name, tn, shard_major=False, tm=None):
    s, m = a.shape
    n = b.shape[1]
    tm = m if tm is None else tm
    ni, nj = m // tm, n // tn

    def body(a_ref, b_ref, o_ref):
        o_ref[...] = _dot_tn(a_ref[...], b_ref[...]).astype(BF16)

    if shard_major:
        out_spec = pl.BlockSpec((None, tm, tn), lambda i, j: (j, i, 0))
        out_shape = jax.ShapeDtypeStruct((nj, m, tn), BF16)
    else:
        out_spec = pl.BlockSpec((tm, tn), lambda i, j: (i, j))
        out_shape = jax.ShapeDtypeStruct((m, n), BF16)
    return pl.pallas_call(
        body, name=name, grid=(ni, nj),
        in_specs=[pl.BlockSpec((s, tm), lambda i, j: (0, i)), pl.BlockSpec((s, tn), lambda i, j: (0, j))],
        out_specs=out_spec, out_shape=out_shape,
        compiler_params=_cp("arbitrary", "arbitrary"),
    )(a, b)


def _pool_fwd(zr, wgrp, scale):
    s = zr.shape[0]

    def body(u_ref, w_ref, sc_ref, p_ref, pp_ref):
        row = _rows((s, 128))
        for gi, win in enumerate(POOL_WINDOWS):
            cs = slice(gi * 128, (gi + 1) * 128)
            u = u_ref[:, cs].astype(F32)
            acc, k = u, 1
            while k < win:
                acc = acc + jnp.where(row >= k, pltpu.roll(acc, k, 0), 0.0)
                k *= 2
            cnt = jnp.minimum(row + 1, win).astype(F32)
            p = (acc / cnt - u).astype(BF16)
            p_ref[:, cs] = p
            pp_ref[:, cs] = (_dot(p, w_ref[gi].astype(BF16)) * sc_ref[:, cs]).astype(BF16)

    return pl.pallas_call(
        body, name="pool_fwd", grid=(1,),
        in_specs=[pl.BlockSpec((s, POOL_W), lambda i: (0, OFF_POOL // POOL_W)),
                  pl.BlockSpec((4, 128, 128), lambda i: (0, 0, 0)), pl.BlockSpec((1, POOL_W), lambda i: (0, 0))],
        out_specs=[pl.BlockSpec((s, POOL_W), lambda i: (0, 0))] * 2,
        out_shape=[jax.ShapeDtypeStruct((s, POOL_W), BF16)] * 2,
        compiler_params=_cp("arbitrary"),
    )(zr, wgrp, scale)


def _pool_bwd(p, dpp, wgrp, scale, after):
    s = p.shape[0]

    def body(p_ref, dpp_ref, w_ref, sc_ref, after_ref, dz_ref, dw_ref, dsc_ref):
        row = _rows((s, 128))
        for gi, win in enumerate(POOL_WINDOWS):
            cs = slice(gi * 128, (gi + 1) * 128)
            pv = p_ref[:, cs]
            wb = w_ref[gi].astype(BF16)
            dpp_v = dpp_ref[:, cs].astype(F32)
            dsc_ref[:, cs] = jnp.sum(dpp_v * _dot(pv, wb), axis=0, keepdims=True)
            dpm = (dpp_v * sc_ref[:, cs]).astype(BF16)
            dw_ref[gi] = _dot_tn(pv, dpm)
            dp = _dot_nt(dpm, wb)
            cnt = jnp.minimum(row + 1, win).astype(F32)
            acc, k = dp / cnt, 1
            while k < win:
                acc = acc + jnp.where(row < s - k, pltpu.roll(acc, s - k, 0), 0.0)
                k *= 2
            dz_ref[:, cs] = (acc - dp).astype(BF16)

    full = lambda i: (0, 0)
    return pl.pallas_call(
        body, name="pool_bwd", grid=(1,),
        in_specs=[pl.BlockSpec((s, POOL_W), full), pl.BlockSpec((s, POOL_W), full),
                  pl.BlockSpec((4, 128, 128), lambda i: (0, 0, 0)), pl.BlockSpec((1, POOL_W), full), ANY],
        out_specs=[pl.BlockSpec((s, POOL_W), full), pl.BlockSpec((4, 128, 128), lambda i: (0, 0, 0)),
                   pl.BlockSpec((1, POOL_W), full)],
        out_shape=[jax.ShapeDtypeStruct((s, POOL_W), BF16), jax.ShapeDtypeStruct((4, 128, 128), F32),
                   jax.ShapeDtypeStruct((1, POOL_W), F32)],
        compiler_params=_cp("arbitrary"),
    )(p, dpp, wgrp, scale, after)


def _gla_decay(zgk_ref, wgk_ref, bgk_ref, rb):
    g = _dot(zgk_ref[...], wgk_ref[...].astype(BF16)) + bgk_ref[...]
    la = (jnp.minimum(g, 0.0) - jnp.log(1.0 + jnp.exp(-jnp.abs(g)))) * (1.0 / 16.0)
    rowm = _rows(la.shape) & (CHUNK - 1)
    bc, k = la, 1
    while k < CHUNK:
        bc = bc + jnp.where(rowm >= k, pltpu.roll(bc, k, 0), 0.0)
        k *= 2
    return g, jnp.exp(bc), jnp.exp(-bc)


GLA_HB = 4


def _gla_specs(rb, rmap):
    wk, wv = GLA_HB * HK, GLA_HB * HV
    return [pl.BlockSpec((rb, wk), lambda h, r: (rmap(h, r), OFF_Q // wk + h)),
            pl.BlockSpec((rb, wk), lambda h, r: (rmap(h, r), OFF_K // wk + h)),
            pl.BlockSpec((rb, wv), lambda h, r: (rmap(h, r), OFF_V // wv + h)),
            pl.BlockSpec((rb, 128), lambda h, r: (rmap(h, r), OFF_GK // 128))]


def _gla_fwd(zr, wgk, bgk, ghead, rb):
    s = zr.shape[0]
    nc = rb // CHUNK
    wk, wv = GLA_HB * HK, GLA_HB * HV

    def body(q_ref, k_ref, v_ref, zgk_ref, zog_ref, wgk_ref, bgk_ref, gh_ref, o_ref, og_ref, sp_ref, st_ref, kv_ref):
        @pl.when(pl.program_id(1) == 0)
        def _():
            st_ref[...] = jnp.zeros_like(st_ref)

        _, e_pos, e_neg = _gla_decay(zgk_ref, wgk_ref, bgk_ref, rb)
        lower = _rows((CHUNK, CHUNK)) >= lax.broadcasted_iota(jnp.int32, (CHUNK, CHUNK), 1)
        pairs = [(c, hh) for c in range(nc) for hh in range(GLA_HB)]
        rows = lambda c: slice(c * CHUNK, (c + 1) * CHUNK)
        cols_k = lambda hh: slice(hh * HK, (hh + 1) * HK)
        cols_v = lambda hh: slice(hh * HV, (hh + 1) * HV)
        qfws, pms, e_lasts = {}, {}, {}
        for c, hh in pairs:
            q = q_ref[rows(c), cols_k(hh)].astype(F32) * QSCALE
            k = k_ref[rows(c), cols_k(hh)].astype(F32)
            ec, fc = e_pos[rows(c), cols_k(hh)], e_neg[rows(c), cols_k(hh)]
            qfw = (q * ec).astype(BF16)
            kfw_f = k * fc
            s_fw = _dot_nt(qfw, kfw_f.astype(BF16))
            s_bw = _dot_nt((q * fc).astype(BF16), (k * ec).astype(BF16))
            e_last = _pick_row(ec, CHUNK - 1)
            kv_ref[c, hh] = _dot_tn(v_ref[rows(c), cols_v(hh)], (kfw_f * e_last).astype(BF16))
            qfws[c, hh], pms[c, hh], e_lasts[c, hh] = qfw, jnp.where(lower, s_fw, s_bw).astype(BF16), e_last
        for hh in range(GLA_HB):
            st = st_ref[hh]
            for c in range(nc):
                sp_ref[c, hh] = st.astype(BF16)
                st = st * e_lasts[c, hh] + kv_ref[c, hh]
            st_ref[hh] = st
        for c, hh in pairs:
            o = _dot(pms[c, hh], v_ref[rows(c), cols_v(hh)]) + _dot_nt(qfws[c, hh], sp_ref[c, hh])
            r = lax.rsqrt(jnp.mean(o * o, axis=-1, keepdims=True) + EPS)
            zo = zog_ref[rows(c), cols_v(hh)].astype(F32)
            o_ref[rows(c), cols_v(hh)] = o.astype(BF16)
            og_ref[rows(c), cols_v(hh)] = (o * r * gh_ref[...] * zo * _sigmoid(zo)).astype(BF16)

    rmap = lambda h, r: r
    return pl.pallas_call(
        body, name="gla_fwd", grid=(HEADS // GLA_HB, s // rb),
        in_specs=_gla_specs(rb, rmap) + [
            pl.BlockSpec((rb, wv), lambda h, r: (r, OFF_OG // wv + h)),
            pl.BlockSpec((128, wk), lambda h, r: (0, h)), pl.BlockSpec((1, wk), lambda h, r: (0, h)),
            pl.BlockSpec((1, HV), lambda h, r: (0, 0))],
        out_specs=[pl.BlockSpec((rb, wv), lambda h, r: (r, h)), pl.BlockSpec((rb, wv), lambda h, r: (r, h)),
                   pl.BlockSpec((nc, GLA_HB, HV, HK), lambda h, r: (r, h, 0, 0))],
        out_shape=[jax.ShapeDtypeStruct((s, D), BF16), jax.ShapeDtypeStruct((s, D), BF16),
                   jax.ShapeDtypeStruct((s // CHUNK, HEADS, HV, HK), BF16)],
        scratch_shapes=[pltpu.VMEM((GLA_HB, HV, HK), F32), pltpu.VMEM((nc, GLA_HB, HV, HK), F32)],
        compiler_params=_cp("arbitrary", "arbitrary"),
    )(zr, zr, zr, zr, zr, wgk, bgk, ghead)


def _gla_bwd(zr, do, sp, wgk, bgk, after, rb):
    s = zr.shape[0]
    nc = rb // CHUNK
    nr = s // rb
    wk, wv = GLA_HB * HK, GLA_HB * HV

    def body(q_ref, k_ref, v_ref, zgk_ref, do_ref, sp_ref, wgk_ref, bgk_ref, after_ref, dq_ref, dk_ref, dv_ref, dg_ref,
             gt_ref, dbc_ref, gs_ref):
        @pl.when(pl.program_id(1) == 0)
        def _():
            gt_ref[...] = jnp.zeros_like(gt_ref)

        g, e_pos, e_neg = _gla_decay(zgk_ref, wgk_ref, bgk_ref, rb)
        lower = _rows((CHUNK, CHUNK)) >= lax.broadcasted_iota(jnp.int32, (CHUNK, CHUNK), 1)
        is_last = _rows((CHUNK, HK)) == CHUNK - 1
        pairs = [(c, hh) for c in range(nc) for hh in range(GLA_HB)]
        rows = lambda c: slice(c * CHUNK, (c + 1) * CHUNK)
        cols_k = lambda hh: slice(hh * HK, (hh + 1) * HK)
        cols_v = lambda hh: slice(hh * HV, (hh + 1) * HV)
        e_lasts = {}
        for c, hh in pairs:
            ec = e_pos[rows(c), cols_k(hh)]
            qfw = (q_ref[rows(c), cols_k(hh)].astype(F32) * QSCALE * ec).astype(BF16)
            gs_ref[c, hh] = _dot_tn(do_ref[rows(c), cols_v(hh)], qfw)
            e_lasts[c, hh] = _pick_row(ec, CHUNK - 1)
        for hh in range(GLA_HB):
            gt = gt_ref[hh]
            for c in reversed(range(nc)):
                own = gs_ref[c, hh]
                gs_ref[c, hh] = gt
                gt = own + gt * e_lasts[c, hh]
            gt_ref[hh] = gt
        def decayed(c, hh):
            q = q_ref[rows(c), cols_k(hh)].astype(F32) * QSCALE
            k = k_ref[rows(c), cols_k(hh)].astype(F32)
            ec, fc = e_pos[rows(c), cols_k(hh)], e_neg[rows(c), cols_k(hh)]
            return ec, fc, q * ec, k * fc, q * fc, k * ec

        pms, dss = {}, {}
        for c, hh in pairs:
            _, _, qfw_f, kfw_f, qbw_f, kbw_f = decayed(c, hh)
            s_fw = _dot_nt(qfw_f.astype(BF16), kfw_f.astype(BF16))
            s_bw = _dot_nt(qbw_f.astype(BF16), kbw_f.astype(BF16))
            dp = _dot_nt(do_ref[rows(c), cols_v(hh)], v_ref[rows(c), cols_v(hh)])
            pms[c, hh] = jnp.where(lower, s_fw, s_bw).astype(BF16)
            dss[c, hh] = (jnp.where(lower, dp, 0.0).astype(BF16), jnp.where(lower, 0.0, dp).astype(BF16))
        for c, hh in pairs:
            sl, ck, cv = rows(c), cols_k(hh), cols_v(hh)
            v = v_ref[sl, cv]
            dov = do_ref[sl, cv]
            ec, fc, qfw_f, kfw_f, qbw_f, kbw_f = decayed(c, hh)
            qfw, kfw, qbw, kbw = qfw_f.astype(BF16), kfw_f.astype(BF16), qbw_f.astype(BF16), kbw_f.astype(BF16)
            pm = pms[c, hh]
            e_last = e_lasts[c, hh]
            kdec = (kfw_f * e_last).astype(BF16)
            gt = gs_ref[c, hh]
            gtb = gt.astype(BF16)
            spv = sp_ref[c, hh]
            dv_ref[sl, cv] = (_dot_tn(pm, dov) + _dot_nt(kdec, gtb)).astype(BF16)
            ds_fw, ds_bw = dss[c, hh]
            dqfw = _dot(ds_fw, kfw) + _dot(dov, spv)
            dkfw = _dot_tn(ds_fw, qfw)
            dqbw = _dot(ds_bw, kbw)
            dkbw = _dot_tn(ds_bw, qbw)
            dkdec = _dot(v, gtb)
            de_last = (jnp.sum(gt * spv.astype(F32), axis=0, keepdims=True)
                       + jnp.sum(dkdec * kfw_f, axis=0, keepdims=True))
            dkfw = dkfw + dkdec * e_last
            dq_ref[sl, ck] = ((dqfw * ec + dqbw * fc) * QSCALE).astype(BF16)
            dk_ref[sl, ck] = (dkfw * fc + dkbw * ec).astype(BF16)
            dbc = dqfw * qfw_f - dqbw * qbw_f + dkbw * kbw_f - dkfw * kfw_f
            dbc_ref[sl, ck] = dbc + jnp.where(is_last, de_last * e_last, 0.0)
        rowm = _rows((rb, wk)) & (CHUNK - 1)
        dla, kk = dbc_ref[...], 1
        while kk < CHUNK:
            dla = dla + jnp.where(rowm < CHUNK - kk, pltpu.roll(dla, rb - kk, 0), 0.0)
            kk *= 2
        dg_ref[...] = dla * (1.0 / 16.0) * _sigmoid(-g)

    rmap = lambda h, r: nr - 1 - r
    rev = lambda h, r: (nr - 1 - r, h)
    return pl.pallas_call(
        body, name="gla_bwd", grid=(HEADS // GLA_HB, nr),
        in_specs=_gla_specs(rb, rmap) + [
            pl.BlockSpec((rb, wv), rev),
            pl.BlockSpec((nc, GLA_HB, HV, HK), lambda h, r: (nr - 1 - r, h, 0, 0)),
            pl.BlockSpec((128, wk), lambda h, r: (0, h)), pl.BlockSpec((1, wk), lambda h, r: (0, h)), ANY],
        out_specs=[pl.BlockSpec((rb, wk), rev), pl.BlockSpec((rb, wk), rev), pl.BlockSpec((rb, wv), rev),
                   pl.BlockSpec((rb, wk), rev)],
        out_shape=[jax.ShapeDtypeStruct((s, HEADS * HK), BF16), jax.ShapeDtypeStruct((s, HEADS * HK), BF16),
                   jax.ShapeDtypeStruct((s, D), BF16), jax.ShapeDtypeStruct((s, HEADS * HK), F32)],
        scratch_shapes=[pltpu.VMEM((GLA_HB, HV, HK), F32), pltpu.VMEM((rb, wk), F32),
                        pltpu.VMEM((nc, GLA_HB, HV, HK), F32)],
        compiler_params=_cp("arbitrary", "arbitrary"),
    )(zr, zr, zr, zr, do, sp, wgk, bgk, after)


def _gk_bwd(dgpre, zr, wgk, after, ts):
    s = zr.shape[0]

    def body(dg_ref, zgk_ref, w_ref, after_ref, dz_ref, dw_ref, db_ref):
        @pl.when(pl.program_id(0) == 0)
        def _():
            dw_ref[...] = jnp.zeros_like(dw_ref)
            db_ref[...] = jnp.zeros_like(db_ref)

        dg = dg_ref[...]
        dgb = dg.astype(BF16)
        dz_ref[...] = _dot_nt(dgb, w_ref[...].astype(BF16)).astype(BF16)
        dw_ref[...] += _dot_tn(zgk_ref[...], dgb)
        db_ref[...] += jnp.sum(dg, axis=0, keepdims=True)

    return pl.pallas_call(
        body, name="gk_bwd", grid=(s // ts,),
        in_specs=[pl.BlockSpec((ts, 512), lambda i: (i, 0)), pl.BlockSpec((ts, 128), lambda i: (i, OFF_GK // 128)),
                  pl.BlockSpec((128, 512), lambda i: (0, 0)), ANY],
        out_specs=[pl.BlockSpec((ts, 128), lambda i: (i, 0)), pl.BlockSpec((128, 512), lambda i: (0, 0)),
                   pl.BlockSpec((1, 512), lambda i: (0, 0))],
        out_shape=[jax.ShapeDtypeStruct((s, 128), BF16), jax.ShapeDtypeStruct((128, 512), F32),
                   jax.ShapeDtypeStruct((1, 512), F32)],
        compiler_params=_cp("arbitrary"),
    )(dgpre, zr, wgk, after)


def _merge_fwd(x, zr, pp, og, bgate, wpp, wgla, wout, gffn, after, ts):
    s = x.shape[0]

    def body(x_ref, z0_ref, z1_ref, pp_ref, og_ref, bg_ref, wpp_ref, wgla_ref, wout_ref, gf_ref, after_ref,
             x1_ref, mix_ref, yp_ref, yg_ref, h2_ref):
        ppv = pp_ref[...]
        yp = jnp.concatenate([_dot(ppv, wpp_ref[j]) for j in range(4)], axis=1)
        yg = _dot(og_ref[...], wgla_ref[...])
        g0 = _sigmoid(z0_ref[...].astype(F32) + bg_ref[:, :D])
        g1 = _sigmoid(z1_ref[...].astype(F32) + bg_ref[:, D:])
        mixed = (g0 * yp + g1 * yg).astype(BF16)
        x1 = x_ref[...] + _dot(mixed, wout_ref[...])
        x1_ref[...] = x1
        mix_ref[...] = mixed
        yp_ref[...] = yp.astype(BF16)
        yg_ref[...] = yg.astype(BF16)
        r = lax.rsqrt(jnp.mean(x1 * x1, axis=-1, keepdims=True) + EPS)
        h2_ref[...] = (x1 * r * gf_ref[...]).astype(BF16)

    row = lambda i: (i, 0)
    const2 = lambda i: (0, 0)
    return pl.pallas_call(
        body, name="merge_fwd", grid=(s // ts,),
        in_specs=[pl.BlockSpec((ts, D), row), pl.BlockSpec((ts, D), lambda i: (i, 0)), pl.BlockSpec((ts, D), lambda i: (i, 1)),
                  pl.BlockSpec((ts, POOL_W), row), pl.BlockSpec((ts, D), row), pl.BlockSpec((1, 2 * D), const2),
                  pl.BlockSpec((4, POOL_W, 256), lambda i: (0, 0, 0)), pl.BlockSpec((D, D), const2),
                  pl.BlockSpec((D, D), const2), pl.BlockSpec((1, D), const2), ANY],
        out_specs=[pl.BlockSpec((ts, D), row)] * 5,
        out_shape=[jax.ShapeDtypeStruct((s, D), F32)] + [jax.ShapeDtypeStruct((s, D), BF16)] * 4,
        compiler_params=_cp("arbitrary"),
    )(x, zr, zr, pp, og, bgate, wpp, wgla, wout, gffn, after)


def _merge_bwd(dx1b, zr, yp, yg, o, bgate, ghead, wpp, wgla, wout, after, ts):
    s = dx1b.shape[0]

    def body(dx_ref, z0_ref, z1_ref, zog_ref, yp_ref, yg_ref, o_ref, bg_ref, gh_ref, wpp_ref, wgla_ref, wout_ref, after_ref,
             dzg_ref, dyp_ref, dyg_ref, dpp_ref, do_ref, dzog_ref, dbg_ref, dgh_ref):
        @pl.when(pl.program_id(0) == 0)
        def _():
            dbg_ref[...] = jnp.zeros_like(dbg_ref)
            dgh_ref[...] = jnp.zeros_like(dgh_ref)

        dmix = _dot_nt(dx_ref[...], wout_ref[...])
        g0 = _sigmoid(z0_ref[...].astype(F32) + bg_ref[:, :D])
        g1 = _sigmoid(z1_ref[...].astype(F32) + bg_ref[:, D:])
        dypb = (dmix * g0).astype(BF16)
        dygb = (dmix * g1).astype(BF16)
        dz0 = dmix * yp_ref[...].astype(F32) * g0 * (1.0 - g0)
        dz1 = dmix * yg_ref[...].astype(F32) * g1 * (1.0 - g1)
        dzg_ref[:, :D] = dz0.astype(BF16)
        dzg_ref[:, D:] = dz1.astype(BF16)
        dbg_ref[:, :D] += jnp.sum(dz0, axis=0, keepdims=True)
        dbg_ref[:, D:] += jnp.sum(dz1, axis=0, keepdims=True)
        dyp_ref[...] = dypb
        dyg_ref[...] = dygb
        dpp = _dot_nt(dypb[:, 0:256], wpp_ref[0])
        for j in range(1, 4):
            dpp = dpp + _dot_nt(dypb[:, j * 256:(j + 1) * 256], wpp_ref[j])
        dpp_ref[...] = dpp.astype(BF16)
        dog = _dot_nt(dygb, wgla_ref[...])
        gh = gh_ref[...]
        dgh = jnp.zeros((1, HV), F32)
        for h in range(HEADS):
            cs = slice(h * HV, (h + 1) * HV)
            ov = o_ref[:, cs].astype(F32)
            r = lax.rsqrt(jnp.mean(ov * ov, axis=-1, keepdims=True) + EPS)
            oh = ov * r
            zo = zog_ref[:, cs].astype(F32)
            sg = _sigmoid(zo)
            dog_h = dog[:, cs]
            don = dog_h * zo * sg
            dzog_ref[:, cs] = (dog_h * oh * gh * sg * (1.0 + zo * (1.0 - sg))).astype(BF16)
            dgh = dgh + jnp.sum(don * oh, axis=0, keepdims=True)
            doh = don * gh
            do_ref[:, cs] = (r * (doh - oh * jnp.mean(doh * oh, axis=-1, keepdims=True))).astype(BF16)
        dgh_ref[...] += dgh

    row = lambda i: (i, 0)
    const2 = lambda i: (0, 0)
    return pl.pallas_call(
        body, name="merge_bwd", grid=(s // ts,),
        in_specs=[pl.BlockSpec((ts, D), row), pl.BlockSpec((ts, D), lambda i: (i, 0)), pl.BlockSpec((ts, D), lambda i: (i, 1)),
                  pl.BlockSpec((ts, D), lambda i: (i, OFF_OG // D)), pl.BlockSpec((ts, D), row), pl.BlockSpec((ts, D), row),
                  pl.BlockSpec((ts, D), row), pl.BlockSpec((1, 2 * D), const2), pl.BlockSpec((1, HV), const2),
                  pl.BlockSpec((4, POOL_W, 256), lambda i: (0, 0, 0)), pl.BlockSpec((D, D), const2),
                  pl.BlockSpec((D, D), const2), ANY],
        out_specs=[pl.BlockSpec((ts, 2 * D), row), pl.BlockSpec((ts, D), row), pl.BlockSpec((ts, D), row),
                   pl.BlockSpec((ts, POOL_W), row), pl.BlockSpec((ts, D), row), pl.BlockSpec((ts, D), row),
                   pl.BlockSpec((1, 2 * D), const2), pl.BlockSpec((1, HV), const2)],
        out_shape=[jax.ShapeDtypeStruct((s, 2 * D), BF16), jax.ShapeDtypeStruct((s, D), BF16),
                   jax.ShapeDtypeStruct((s, D), BF16), jax.ShapeDtypeStruct((s, POOL_W), BF16),
                   jax.ShapeDtypeStruct((s, D), BF16), jax.ShapeDtypeStruct((s, D), BF16),
                   jax.ShapeDtypeStruct((1, 2 * D), F32), jax.ShapeDtypeStruct((1, HV), F32)],
        compiler_params=_cp("arbitrary"),
    )(dx1b, zr, zr, zr, yp, yg, o, bgate, ghead, wpp, wgla, wout, after)


HALO = 16
CCH = D_FF // 2


def _conv_taps(u_ref, halo_ref, cs, first, ts):
    u = u_ref[:, cs].astype(F32)
    hal = halo_ref[:, cs].astype(F32)
    h1 = jnp.where(first, 0.0, _pick_row(hal, HALO - 1))
    h2 = jnp.where(first, 0.0, _pick_row(hal, HALO - 2))
    row8 = _rows((8, u.shape[1]))
    r1, r2 = pltpu.roll(u, 1, 0), pltpu.roll(u, 2, 0)
    r1 = jnp.concatenate([jnp.where(row8 == 0, h1, r1[:8]), r1[8:]], axis=0)
    r2 = jnp.concatenate([jnp.where(row8 == 0, h2, jnp.where(row8 == 1, h1, r2[:8])), r2[8:]], axis=0)
    return u, r1, r2


def _ffn_down_loss(u, x1, tgt, wconv, bconv, wdown, gfin, ts):
    s = x1.shape[0]

    def body(u_ref, halo_ref, x1_ref, t_ref, wc_ref, bc_ref, wd_ref, gf_ref, a_ref, c_ref, dx_ref, dxb_ref, ls_ref,
             dgf_ref):
        i = pl.program_id(0)

        @pl.when(i == 0)
        def _():
            ls_ref[...] = jnp.zeros_like(ls_ref)
            dgf_ref[...] = jnp.zeros_like(dgf_ref)

        first = i == 0
        acc = x1_ref[...]
        for hf in range(D_FF // CCH):
            cg = slice(hf * CCH, (hf + 1) * CCH)
            cv = slice(D_FF + hf * CCH, D_FF + (hf + 1) * CCH)
            vals = []
            for cs in (cg, cv):
                u0, u1, u2 = _conv_taps(u_ref, halo_ref, cs, first, ts)
                vals.append(bc_ref[:, cs] + wc_ref[0:1, cs] * u2 + wc_ref[1:2, cs] * u1 + wc_ref[2:3, cs] * u0)
                c_ref[:, cs] = vals[-1].astype(BF16)
            a = (vals[0] * _sigmoid(vals[0]) * vals[1]).astype(BF16)
            a_ref[:, cg] = a
            acc = acc + _dot(a, wd_ref[cg, :])
        r = lax.rsqrt(jnp.mean(acc * acc, axis=-1, keepdims=True) + EPS)
        xh = acc * r
        gf = gf_ref[...]
        err = xh * gf - t_ref[...]
        ls_ref[...] += (0.5 / D) * jnp.sum(jnp.sum(err * err, axis=-1, keepdims=True), axis=0, keepdims=True)
        dy = err * (1.0 / D)
        dgf_ref[...] += jnp.sum(dy * xh, axis=0, keepdims=True)
        dxh = dy * gf
        dx = r * (dxh - xh * jnp.mean(dxh * xh, axis=-1, keepdims=True))
        dx_ref[...] = dx
        dxb_ref[...] = dx.astype(BF16)

    row = lambda i: (i, 0)
    const2 = lambda i: (0, 0)
    return pl.pallas_call(
        body, name="ffn_down_loss", grid=(s // ts,),
        in_specs=[pl.BlockSpec((ts, N_UP), row),
                  pl.BlockSpec((HALO, N_UP), lambda i: (jnp.maximum(i * (ts // HALO) - 1, 0), 0)),
                  pl.BlockSpec((ts, D), row), pl.BlockSpec((ts, D), row), pl.BlockSpec((3, N_UP), const2),
                  pl.BlockSpec((1, N_UP), const2), pl.BlockSpec((D_FF, D), const2), pl.BlockSpec((1, D), const2)],
        out_specs=[pl.BlockSpec((ts, D_FF), row), pl.BlockSpec((ts, N_UP), row), pl.BlockSpec((ts, D), row),
                   pl.BlockSpec((ts, D), row), pl.BlockSpec((1, 128), const2), pl.BlockSpec((1, D), const2)],
        out_shape=[jax.ShapeDtypeStruct((s, D_FF), BF16), jax.ShapeDtypeStruct((s, N_UP), BF16),
                   jax.ShapeDtypeStruct((s, D), F32), jax.ShapeDtypeStruct((s, D), BF16),
                   jax.ShapeDtypeStruct((1, 128), F32), jax.ShapeDtypeStruct((1, D), F32)],
        compiler_params=_cp("arbitrary"),
    )(u, u, x1, tgt, wconv, bconv, wdown, gfin)


def _ffn_bwd(dx2b, u, c, wconv, wdown, ts):
    s = dx2b.shape[0]
    nt = s // ts

    def body(dx_ref, u_ref, c_ref, wc_ref, wd_ref, du_ref, db_ref, dw_ref, nxt_ref):
        @pl.when(pl.program_id(0) == 0)
        def _():
            db_ref[...] = jnp.zeros_like(db_ref)
            dw_ref[...] = jnp.zeros_like(dw_ref)
            nxt_ref[...] = jnp.zeros_like(nxt_ref)

        dxv = dx_ref[...]
        row8 = _rows((8, CCH))
        for hf in range(D_FF // CCH):
            cg = slice(hf * CCH, (hf + 1) * CCH)
            cv = slice(D_FF + hf * CCH, D_FF + (hf + 1) * CCH)
            da = _dot_nt(dxv, wd_ref[cg, :])
            gate = c_ref[:, cg].astype(F32)
            val = c_ref[:, cv].astype(F32)
            sg = _sigmoid(gate)
            dcs = (da * val * sg * (1.0 + gate * (1.0 - sg)), da * gate * sg)
            for cs, dc in zip((cg, cv), dcs):
                n1 = nxt_ref[0:1, cs]
                n2 = nxt_ref[1:2, cs]
                r1, r2 = pltpu.roll(dc, ts - 1, 0), pltpu.roll(dc, ts - 2, 0)
                f1 = jnp.concatenate([r1[:ts - 8], jnp.where(row8 == 7, n1, r1[ts - 8:])], axis=0)
                f2 = jnp.concatenate([r2[:ts - 8], jnp.where(row8 == 7, n2, jnp.where(row8 == 6, n1, r2[ts - 8:]))], axis=0)
                uv = u_ref[:, cs].astype(F32)
                db_ref[:, cs] += jnp.sum(dc, axis=0, keepdims=True)
                dw_ref[0:1, cs] += jnp.sum(f2 * uv, axis=0, keepdims=True)
                dw_ref[1:2, cs] += jnp.sum(f1 * uv, axis=0, keepdims=True)
                dw_ref[2:3, cs] += jnp.sum(dc * uv, axis=0, keepdims=True)
                du_ref[:, cs] = (wc_ref[2:3, cs] * dc + wc_ref[1:2, cs] * f1 + wc_ref[0:1, cs] * f2).astype(BF16)
                nxt_ref[:, cs] = dc[0:8, :]

    rev = lambda i: (nt - 1 - i, 0)
    const2 = lambda i: (0, 0)
    return pl.pallas_call(
        body, name="ffn_bwd", grid=(nt,),
        in_specs=[pl.BlockSpec((ts, D), rev), pl.BlockSpec((ts, N_UP), rev), pl.BlockSpec((ts, N_UP), rev),
                  pl.BlockSpec((3, N_UP), const2), pl.BlockSpec((D_FF, D), const2)],
        out_specs=[pl.BlockSpec((ts, N_UP), rev), pl.BlockSpec((1, N_UP), const2), pl.BlockSpec((3, N_UP), const2)],
        out_shape=[jax.ShapeDtypeStruct((s, N_UP), BF16), jax.ShapeDtypeStruct((1, N_UP), F32),
                   jax.ShapeDtypeStruct((3, N_UP), F32)],
        scratch_shapes=[pltpu.VMEM((8, N_UP), F32)],
        compiler_params=_cp("arbitrary"),
    )(dx2b, u, c, wconv, wdown)


ANY = pl.BlockSpec(memory_space=pl.ANY)


def _place():
    x, y, c = lax.axis_index("x"), lax.axis_index("y"), lax.axis_index("c")
    chips = [(1 - x, y), (x, 1 - y), (1 - x, 1 - y)]
    return x, y, c, chips


def _half(shape, c, axis):
    size = shape[axis] // 2
    cut = pl.ds(pl.multiple_of(c * size, 8 if axis == 0 else 128), size)
    return (cut, slice(None)) if axis == 0 else (slice(None), cut)


def _half_shape(shape, axis):
    return (shape[0] // 2, shape[1]) if axis == 0 else (shape[0], shape[1] // 2)


def _remote(src, dst, send_sems, recv_sems, k, to):
    return pltpu.make_async_remote_copy(src_ref=src, dst_ref=dst, send_sem=send_sems.at[k], recv_sem=recv_sems.at[k],
                                        device_id=to, device_id_type=MESH)


def _sibling_exchange(grads, axes, smalls, name):
    nb = len(grads)
    n = nb + len(smalls)

    def body(*refs):
        ins, outs = refs[:n], refs[n:2 * n]
        send_sems, recv_sems = refs[2 * n:]
        x, y, c, _ = _place()
        sib = (x, y, 1 - c)
        cps = []
        for a in range(nb):
            theirs = _half(grads[a].shape[1:], 1 - c, axes[a])
            cps.append(_remote(ins[a].at[(slice(None),) + theirs], outs[a], send_sems, recv_sems, a, sib))
        for a in range(nb, n):
            cps.append(_remote(ins[a], outs[a], send_sems, recv_sems, a, sib))
        for cp in cps:
            cp.start()
        for cp in cps:
            cp.wait()

    out_shape = [jax.ShapeDtypeStruct((4,) + _half_shape(g.shape[1:], ax), g.dtype) for g, ax in zip(grads, axes)]
    out_shape += [jax.ShapeDtypeStruct(a.shape, F32) for a in smalls]
    return pl.pallas_call(
        body, name=name, in_specs=[ANY] * n, out_specs=[ANY] * n, out_shape=out_shape,
        scratch_shapes=[pltpu.SemaphoreType.DMA((n,)), pltpu.SemaphoreType.DMA((n,))],
        compiler_params=pltpu.CompilerParams(has_side_effects=True),
    )(*grads, *smalls)


def _gather_share(lands, axes, name):
    n = len(lands)

    def body(*refs):
        outs = refs[n:2 * n]
        send_sems, recv_sems = refs[2 * n:]
        x, y, c, chips = _place()
        sib = (x, y, 1 - c)
        cps = []
        for a in range(n):
            mine = _half(lands[a].shape[1:], c, axes[a])
            for k, ch in enumerate(chips):
                landed = outs[a].at[(2 * ch[0] + ch[1],) + mine]
                cps.append(_remote(landed, landed, send_sems, recv_sems, 3 * a + k, sib))
        for cp in cps:
            cp.start()
        for a in range(n):
            other = _half(lands[a].shape[1:], 1 - c, axes[a])
            for k, ch in enumerate(chips):
                landed = outs[a].at[(2 * ch[0] + ch[1],) + other]
                _remote(landed, landed, send_sems, recv_sems, 3 * a + k, sib).wait_recv()
        for cp in cps:
            cp.wait_send()

    return pl.pallas_call(
        body, name=name, in_specs=[ANY] * n, out_specs=[ANY] * n,
        out_shape=[jax.ShapeDtypeStruct(a.shape, a.dtype) for a in lands],
        input_output_aliases={a: a for a in range(n)},
        scratch_shapes=[pltpu.SemaphoreType.DMA((3 * n,)), pltpu.SemaphoreType.DMA((3 * n,))],
        compiler_params=pltpu.CompilerParams(has_side_effects=True),
    )(*lands)


def _sibling_share(halves, name):
    n = len(halves)

    def body(*refs):
        ins, outs = refs[:n], refs[n:2 * n]
        send_sems, recv_sems = refs[2 * n:]
        x, y, c, _ = _place()
        cps = [_remote(ins[a], outs[a], send_sems, recv_sems, a, (x, y, 1 - c)) for a in range(n)]
        for cp in cps:
            cp.start()
        for cp in cps:
            cp.wait()

    return pl.pallas_call(
        body, name=name, in_specs=[ANY] * n, out_specs=[ANY] * n,
        out_shape=[jax.ShapeDtypeStruct(h.shape, F32) for h in halves],
        scratch_shapes=[pltpu.SemaphoreType.DMA((n,)), pltpu.SemaphoreType.DMA((n,))],
        compiler_params=pltpu.CompilerParams(has_side_effects=True),
    )(*halves)


HBM = pl.BlockSpec(memory_space=pltpu.HBM)
SEM = pl.BlockSpec(memory_space=pltpu.SEMAPHORE)
DATAFLOW = pltpu.SideEffectType.DATAFLOW_SIDE_EFFECTING


def _split_start(name, srcs, land_shapes, plan, n_copies, after):
    lands = [lax.empty(*ls) if isinstance(ls, tuple) else ls for ls in land_shapes]
    bufs = list(srcs) + lands
    nb, ns = len(bufs), len(srcs)

    def body(*refs):
        send_sems, recv_sems, token = refs[nb + 1], refs[nb + 2], refs[-1]
        for k, (src, dst, to) in enumerate(plan(refs[:ns], refs[ns:nb])):
            _remote(src, dst, send_sems, recv_sems, k, to).start()
        token[...] = jnp.zeros_like(token)

    res = pl.pallas_call(
        body, name=name,
        out_shape=(pltpu.SemaphoreType.DMA((n_copies,)), pltpu.SemaphoreType.DMA((n_copies,)),
                   *[pltpu.HBM(b.shape, b.dtype) for b in bufs], jax.ShapeDtypeStruct((8, 128), F32)),
        in_specs=[HBM] * nb + [ANY],
        out_specs=(SEM, SEM, *[HBM] * nb, pl.BlockSpec(memory_space=pltpu.VMEM)),
        input_output_aliases={i: 2 + i for i in range(nb)},
        compiler_params=pltpu.CompilerParams(has_side_effects=DATAFLOW),
    )(*[pltpu.with_memory_space_constraint(b, pltpu.HBM) for b in bufs], after)
    return (res[0], res[1], list(res[2:2 + nb])), res[-1]


def _split_wait(name, handle, n_srcs, plan, after):
    send_sems, recv_sems, bufs = handle
    nb = len(bufs)

    def body(*refs):
        sends, recvs = refs[nb], refs[nb + 1]
        for k, (src, dst, to) in enumerate(plan(refs[:n_srcs], refs[n_srcs:nb])):
            cp = _remote(src, dst, sends, recvs, k, to)
            cp.wait_send()
            cp.wait_recv()

    res = pl.pallas_call(
        body, name=name, out_shape=[pltpu.HBM(b.shape, b.dtype) for b in bufs],
        in_specs=[HBM] * nb + [SEM, SEM, ANY], out_specs=[HBM] * nb,
        input_output_aliases={i: i for i in range(nb)},
        compiler_params=pltpu.CompilerParams(has_side_effects=DATAFLOW),
    )(*bufs, send_sems, recv_sems, after)
    return list(res[:n_srcs]), list(res[n_srcs:])


def _gather_plan(shapes, axes, n_whole=0):
    def plan(srcs, lands):
        x, y, c, chips = _place()
        out = []
        for a, (shape, axis) in enumerate(zip(shapes, axes)):
            mine = _half(shape, c, axis)
            for ch in chips:
                out.append((srcs[a].at[mine], lands[a].at[(2 * x + y,) + mine], (ch[0], ch[1], c)))
        for a in range(len(shapes), len(shapes) + n_whole):
            for ch in chips:
                out.append((srcs[a], lands[a].at[2 * x + y], (ch[0], ch[1], c)))
        return out
    return plan


def _share_plan(shapes, axes):
    def plan(srcs, lands):
        x, y, c, chips = _place()
        out = []
        for a, (shape, axis) in enumerate(zip(shapes, axes)):
            mine = _half(shape, c, axis)
            for ch in chips:
                landed = lands[a].at[(2 * ch[0] + ch[1],) + mine]
                out.append((landed, landed, (x, y, 1 - c)))
        return out
    return plan


def _sibling_plan(shapes, axes):
    def plan(srcs, lands):
        x, y, c, _ = _place()
        return [(srcs[a].at[(slice(None),) + _half(shape, 1 - c, axis)], lands[a], (x, y, 1 - c))
                for a, (shape, axis) in enumerate(zip(shapes, axes))]
    return plan


def _reduce_plan(n_big, n_small):
    def plan(srcs, lands):
        x, y, c, chips = _place()
        out = []
        for a in range(n_big):
            for k, ch in enumerate(chips):
                out.append((srcs[a].at[2 * ch[0] + ch[1]], lands[a].at[k], (ch[0], ch[1], c)))
        for a in range(n_big, n_big + n_small):
            for ch in chips:
                out.append((srcs[a], lands[a].at[2 * x + y], (ch[0], ch[1], c)))
        return out
    return plan


def _row_tile(rows, cols, mult):
    best = mult
    for t in range(mult, rows + 1, mult):
        if rows % t == 0 and t * cols * 4 <= (2 << 20):
            best = t
    return best if rows % best == 0 else rows


COL_TILE = 256


def _half_tiling(hshape, axis, mult):
    hr, hc = hshape
    if axis == 0:
        tr = _row_tile(hr, hc, mult)
        return tr, hc, hr // tr
    return hr, COL_TILE, hc // COL_TILE


def _tile_idx(axis, t):
    return (t, 0) if axis == 0 else (0, t)


def _chip_partial(place, g, t, axis, name):
    hshape = t.shape[1:]
    br, bc, nt = _half_tiling(hshape, axis, 16)

    def body(pl_ref, g_ref, t_ref, pf_ref, pb_ref):
        v = g_ref[...].astype(F32) + t_ref[...].astype(F32)
        pb_ref[...] = v.astype(BF16)

        @pl.when(pl.program_id(1) == pl_ref[0])
        def _():
            pf_ref[...] = v

    blk = (None, br, bc)
    return pl.pallas_call(
        body, name=name,
        grid_spec=pltpu.PrefetchScalarGridSpec(
            num_scalar_prefetch=1, grid=(nt, 4),
            in_specs=[pl.BlockSpec(blk, lambda i, j, p: (j,) + _tile_idx(axis, p[1] * nt + i)),
                      pl.BlockSpec(blk, lambda i, j, p: (j,) + _tile_idx(axis, i))],
            out_specs=[pl.BlockSpec((br, bc), lambda i, j, p: _tile_idx(axis, i)),
                       pl.BlockSpec(blk, lambda i, j, p: (j,) + _tile_idx(axis, i))]),
        out_shape=[jax.ShapeDtypeStruct(hshape, F32), jax.ShapeDtypeStruct((4,) + hshape, BF16)],
        compiler_params=_cp("arbitrary", "arbitrary"),
    )(place, g, t)


def _finish_half(pf, rb, axis, name):
    hshape = pf.shape
    br, bc, nt = _half_tiling(hshape, axis, 16)

    def body(pf_ref, rb_ref, o_ref):
        o_ref[...] = ((pf_ref[...] + rb_ref[0].astype(F32)) + rb_ref[1].astype(F32)) + rb_ref[2].astype(F32)

    return pl.pallas_call(
        body, name=name, grid=(nt,),
        in_specs=[pl.BlockSpec((br, bc), lambda i: _tile_idx(axis, i)),
                  pl.BlockSpec((3, br, bc), lambda i: (0,) + _tile_idx(axis, i))],
        out_specs=pl.BlockSpec((br, bc), lambda i: _tile_idx(axis, i)),
        out_shape=jax.ShapeDtypeStruct(hshape, F32),
        compiler_params=_cp("arbitrary"),
    )(pf, rb)


def _adam_math(w, g, m, v):
    m = ADAM_B1 * m + (1.0 - ADAM_B1) * g
    v = ADAM_B2 * v + (1.0 - ADAM_B2) * (g * g)
    m_hat = m / (1.0 - ADAM_B1 ** ADAM_STEP)
    v_hat = v / (1.0 - ADAM_B2 ** ADAM_STEP)
    return -ADAM_LR * (m_hat / (jnp.sqrt(v_hat) + ADAM_EPS) + ADAM_WD * w), m, v


def _adam_halves(place, w, mine, theirs, m, v, axis, name):
    br, bc, nt = _half_tiling(mine.shape, axis, 8)

    def body(pl_ref, w_ref, a_ref, b_ref, m_ref, v_ref, g_ref, d_ref, mo_ref, vo_ref):
        is_mine = pl.program_id(0) // nt == pl_ref[1]
        g = jnp.where(is_mine, a_ref[...], b_ref[...])
        d, mn, vn = _adam_math(w_ref[...], g, m_ref[...], v_ref[...])
        g_ref[...] = g
        d_ref[...] = d
        mo_ref[...] = mn
        vo_ref[...] = vn

    full = pl.BlockSpec((br, bc), lambda i, p: _tile_idx(axis, i))
    mine_spec = pl.BlockSpec((br, bc), lambda i, p: _tile_idx(axis, jnp.where(i // nt == p[1], i % nt, nt - 1)))
    theirs_spec = pl.BlockSpec((br, bc), lambda i, p: _tile_idx(axis, jnp.where(i // nt == p[1], 0, i % nt)))
    return pl.pallas_call(
        body, name=name,
        grid_spec=pltpu.PrefetchScalarGridSpec(
            num_scalar_prefetch=1, grid=(2 * nt,), in_specs=[full, mine_spec, theirs_spec, full, full],
            out_specs=[full] * 4),
        out_shape=[jax.ShapeDtypeStruct(w.shape, F32)] * 4, compiler_params=_cp("arbitrary"),
    )(place, w, mine, theirs, m, v)


def _add_many(xs, ys, name):
    n = len(xs)

    def body(*refs):
        for i in range(n):
            refs[2 * n + i][...] = refs[i][...] + refs[n + i][...]

    return pl.pallas_call(body, name=name, out_shape=[jax.ShapeDtypeStruct(a.shape, F32) for a in xs])(*xs, *ys)


def _adam_small(place, owns, landed, ws, ms, vs, widths):
    n, nw = len(owns), len(ws)

    def body(pl_ref, *refs):
        own_r, land_r = refs[:n], refs[n:2 * n]
        w_r, m_r, v_r = (refs[2 * n + k * nw:2 * n + (k + 1) * nw] for k in range(3))
        outs = refs[2 * n + 3 * nw:]
        g_o, d_o, m_o, v_o = outs[:n], outs[n:n + nw], outs[n + nw:n + 2 * nw], outs[n + 2 * nw:]
        for me in range(4):
            @pl.when(pl_ref[0] == me)
            def _(me=me):
                for i in range(n):
                    p = [own_r[i][...] if k == me else land_r[i][k] for k in range(4)]
                    g = ((p[0] + p[1]) + p[2]) + p[3]
                    if i < nw and widths[i]:
                        g = g[:, me * widths[i]:(me + 1) * widths[i]]
                    g_o[i][...] = g
                    if i < nw:
                        d, mn, vn = _adam_math(w_r[i][...], g, m_r[i][...], v_r[i][...])
                        d_o[i][...] = d
                        m_o[i][...] = mn
                        v_o[i][...] = vn

    g_shapes = [jax.ShapeDtypeStruct(ws[i].shape if i < nw else owns[i].shape, F32) for i in range(n)]
    w_shapes = [jax.ShapeDtypeStruct(w.shape, F32) for w in ws]
    whole = lambda a: pl.BlockSpec(a.shape, lambda i, p, nd=len(a.shape): (0,) * nd)
    ins = list(owns) + list(landed) + list(ws) + list(ms) + list(vs)
    out_shape = g_shapes + w_shapes * 3
    out = pl.pallas_call(
        body, name="adam_small",
        grid_spec=pltpu.PrefetchScalarGridSpec(num_scalar_prefetch=1, grid=(1,), in_specs=[whole(a) for a in ins],
                                               out_specs=[whole(a) for a in out_shape]),
        out_shape=out_shape, compiler_params=_cp("arbitrary"),
    )(place, *ins)
    return out[:n], out[n:n + nw], out[n + nw:n + 2 * nw], out[n + 2 * nw:]


def kernel(x, g_mix, w_in, b_gate, w_gk_up, b_gk, w_pool_grp, pool_scale, g_gla_head, w_pool_proj, w_gla_proj, w_out, g_ffn, w_up, w_conv, b_conv, w_down, g_final, loss_target, m_g_mix, m_w_in, m_b_gate, m_w_gk_up, m_b_gk, m_w_pool_grp, m_pool_scale, m_g_gla_head, m_w_pool_proj, m_w_gla_proj, m_w_out, m_g_ffn, m_w_up, m_w_conv, m_b_conv, m_w_down, m_g_final, v_g_mix, v_w_in, v_b_gate, v_w_gk_up, v_b_gk, v_w_pool_grp, v_pool_scale, v_g_gla_head, v_w_pool_proj, v_w_gla_proj, v_w_out, v_g_ffn, v_w_up, v_w_conv, v_b_conv, v_w_down, v_g_final):
    s = x.shape[1]
    ts = min(s, 512)
    tm = min(s, 256)
    cx, cy, cc = lax.axis_index("x"), lax.axis_index("y"), lax.axis_index("c")
    chip = 2 * cx + cy
    place = jnp.stack([chip, cc]).astype(jnp.int32)

    big_names = ("w_in", "w_pool_proj", "w_gla_proj", "w_out", "w_up", "w_down")
    axes = (1, 0, 0, 0, 0, 0)
    shards = dict(w_in=jnp.transpose(w_in[0]), w_pool_proj=w_pool_proj[0], w_gla_proj=w_gla_proj[0], w_out=w_out[0],
                  w_up=w_up[0], w_down=w_down[0])
    def fill_own(lands, mine):
        return [lax.dynamic_update_slice(g, o_[None], (chip, 0, 0)) for g, o_ in zip(lands, mine)]

    def gather_start(tag, halves, group_axes, whole, after):
        plan = _gather_plan([o_.shape for o_ in halves], group_axes, len(whole))
        srcs = list(halves) + list(whole)
        handle, token = _split_start("gather_" + tag + "_start", srcs, [((4,) + o_.shape, o_.dtype) for o_ in srcs], plan,
                                     3 * len(srcs), after)
        return (handle, plan, len(halves), len(srcs), group_axes), token

    def gather_finish(tag, started, after):
        handle, plan, n_halves, n, group_axes = started
        mine, lands = _split_wait("gather_" + tag + "_wait", handle, n, plan, after)
        lands[:n_halves] = _gather_share(lands[:n_halves], group_axes, "gather_" + tag + "_share")
        return fill_own(lands, mine)

    in_w, tok = gather_start("in", [jnp.transpose(w_in[0].astype(BF16))], axes[:1], [], g_mix)
    zero = tok[0, 0]
    own = [(shards[n] + zero).astype(BF16) for n in big_names[1:]]
    mix_w, tok = gather_start("mix", own[0:3], axes[1:4], [w_gk_up[0] + zero, w_conv[0] + zero], tok)
    up_w, tok = gather_start("up", own[3:4], axes[4:5], [], tok)
    down_w, tok = gather_start("down", own[4:5], axes[5:6], [], tok)

    def forward_start(tag, started, after):
        handle, plan, _, n, group_axes = started
        mine, lands = _split_wait("gather_" + tag + "_wait", handle, n, plan, after)
        plan = _share_plan([o_.shape for o_ in mine], group_axes)
        share, token = _split_start("gather_" + tag + "_share_start", [], lands, plan, 3 * n, after)
        return (share, plan, mine), token

    def forward_done(tag, forwarded, after):
        share, plan, mine = forwarded
        return fill_own(_split_wait("gather_" + tag + "_share_wait", share, 0, plan, after)[1], mine)
    xs, tgt = x[0], loss_target[0]
    wgrp = w_pool_grp[0]
    h = _rmsnorm(xs, g_mix, tok, "norm_mix", ts)
    m_in_t, v_in_t = jnp.transpose(m_w_in[0]), jnp.transpose(v_w_in[0])
    h, m_in_t, v_in_t = lax.optimization_barrier((h, m_in_t, v_in_t))
    w_in_t = gather_finish("in", in_w, h)[0].reshape(N_IN, D)
    nsh = N_IN // 4

    zr = _in_proj(h, w_in_t, PROJ_TILE)
    p, pp = _pool_fwd(zr, wgrp, pool_scale)
    wpp, wgla, wout, wgk4, wconv4 = gather_finish("mix", mix_w, pp)
    wgla, wout = wgla.reshape(D, D), wout.reshape(D, D)
    wgk_full = jnp.transpose(wgk4, (1, 0, 2)).reshape(GATE_RANK, 512)
    wconv_full = jnp.transpose(wconv4, (1, 0, 2)).reshape(3, N_UP)
    wgk_pad = jnp.concatenate([wgk_full, jnp.zeros((128 - GATE_RANK, 512), F32)], axis=0)
    o, og, sp = _gla_fwd(zr, wgk_pad, b_gk, g_gla_head, ts)
    up_f, tok = forward_start("up", up_w, og)
    x1, mixed, yp, yg, h2 = _merge_fwd(xs, zr, pp, og, b_gate, wpp, wgla, wout, g_ffn, tok, ts)
    wup, = forward_done("up", up_f, x1)
    down_f, tok = forward_start("down", down_w, x1)
    u = _matmul_resident(h2, wup, tok, "ffn_up")
    wdown = forward_done("down", down_f, u)[0].reshape(D_FF, D)
    a, conv_out, dx2, dx2b, loss_part, dgfin = _ffn_down_loss(u, x1, tgt, wconv_full, b_conv, wdown,
                                                              g_final.reshape(1, D), tm)

    du, dbconv, dwconv = _ffn_bwd(dx2b, u, conv_out, wconv_full, wdown, tm)
    dw_down = _matmul_tn(a, dx2b, "dw_down", D, tm=D_FF // 2)
    dw_up = _matmul_tn(h2, du, "dw_up", UP_SHARD, shard_major=True)

    def exchange_start(tag, grads, group_axes, after):
        plan = _sibling_plan([g.shape[1:] for g in grads], group_axes)
        lands = [((4,) + _half_shape(g.shape[1:], ax), g.dtype) for g, ax in zip(grads, group_axes)]
        handle, token = _split_start("sibling_" + tag + "_start", grads, lands, plan, len(grads), after)
        return (handle, plan, len(grads)), token

    def partials(tag, names, group_axes, exchange, after):
        handle, plan, n = exchange
        mine, theirs = _split_wait("sibling_" + tag + "_wait", handle, n, plan, after)
        return zip(*[_chip_partial(place, g, t, ax, "chip_partial_" + nm)
                     for nm, ax, g, t in zip(names, group_axes, mine, theirs)])

    ffn_names, ffn_axes = ("w_up", "w_down"), (0, 0)
    ffn_x, token = exchange_start("ffn", [dw_up, dw_down.reshape(4, 704, D)], ffn_axes, du)
    dx1, dx1b, dgffn = _matmul_nt_normbwd(du, wup, x1, g_ffn, dx2, token, "ffn_up_bwd", ts)
    ffn_pf, ffn_pb = partials("ffn", ffn_names, ffn_axes, ffn_x, dx1b)
    ffn_plan = _reduce_plan(2, 0)
    ffn_handle, token = _split_start("reduce_ffn_start", ffn_pb, [((3,) + p.shape[1:], BF16) for p in ffn_pb],
                                     ffn_plan, 6, ffn_pf[0])

    dzg, dyp, dyg, dpp, do, dzog, dbgate, dghead = _merge_bwd(dx1b, zr, yp, yg, o, b_gate, g_gla_head, wpp, wgla, wout,
                                                             token, ts)
    dw_out = _matmul_tn(mixed, dx1b, "dw_out", D, tm=512)
    dw_gla = _matmul_tn(og, dyg, "dw_gla", D, tm=512)
    dw_pp = _matmul_tn(pp, dyp, "dw_pp", 256, shard_major=True)

    out_names, out_axes = ("w_pool_proj", "w_gla_proj", "w_out"), (0, 0, 0)
    out_x, token = exchange_start("out", [dw_pp, dw_gla.reshape(4, 256, D), dw_out.reshape(4, 256, D)], out_axes, dpp)
    dzp, dwgrp, dscale = _pool_bwd(p, dpp, wgrp, pool_scale, token)
    out_pf, out_pb = partials("out", out_names, out_axes, out_x, dzp)
    out_plan = _reduce_plan(3, 0)
    out_handle, token = _split_start("reduce_out_start", out_pb, [((3,) + p_.shape[1:], BF16) for p_ in out_pb],
                                     out_plan, 9, out_pf[0])
    dq, dk, dv, dgpre = _gla_bwd(zr, do, sp, wgk_pad, b_gk, token, ts)
    dzgk, dwgk, dbgk = _gk_bwd(dgpre, zr, wgk_pad, dgpre, ts)
    dzr = jnp.concatenate([dzg, dv, dzog, dzp, dq, dk, dzgk], axis=1)
    dw_rt = _matmul_tn(dzr, h, "dw_in", D, tm=PROJ_TILE)

    def grad_rows(lo, hi):
        out = []
        for seg_lo, seg_hi, at in ((0, 1536, OFF_POOL), (1536, 3584, OFF_V), (3584, 3600, OFF_GK), (3600, N_IN, OFF_GATE)):
            a_, b_ = max(lo, seg_lo), min(hi, seg_hi)
            if a_ < b_:
                out.append(dw_rt[at + a_ - seg_lo:at + b_ - seg_lo])
        return jnp.concatenate(out, axis=0)

    dw_in_t = jnp.stack([grad_rows(j * nsh, (j + 1) * nsh) for j in range(4)])

    in_sib = _sibling_exchange([dw_in_t], (1,), [], "sibling_exchange_in")
    in_pf, in_pb = _chip_partial(place, dw_in_t, in_sib[0], 1, "chip_partial_w_in")
    in_plan = _reduce_plan(1, 0)
    in_handle, token = _split_start("reduce_in_start", [in_pb], [((3,) + in_pb.shape[1:], BF16)], in_plan, 3, in_pf)
    grad_x, _, dgmix = _matmul_nt_normbwd(dzr, w_in_t, xs, g_mix, dx1, token, "in_proj_bwd", ts, transposed=True)
    small_names = ("g_mix", "b_gate", "w_gk_up", "b_gk", "w_pool_grp", "pool_scale", "g_gla_head", "g_ffn", "w_conv",
                   "b_conv", "g_final")
    small_mine = [dgmix, dbgate, dwgk[:GATE_RANK], dbgk, dwgrp.reshape(4 * 128, 128), dscale, dghead, dgffn, dwconv, dbconv,
                  dgfin, loss_part]
    small_sib = _sibling_exchange([], (), small_mine, "sibling_exchange_small")
    small_chip = _add_many(small_mine, small_sib, "chip_partial_small")
    small_plan = _reduce_plan(0, len(small_chip))
    small_handle, token = _split_start("reduce_small_start", small_chip, [((4,) + a_.shape, F32) for a_ in small_chip],
                                       small_plan, 3 * len(small_chip), small_mine[0])

    ms = dict(w_in=m_in_t, w_pool_proj=m_w_pool_proj[0], w_gla_proj=m_w_gla_proj[0], w_out=m_w_out[0],
              w_up=m_w_up[0], w_down=m_w_down[0])
    vs = dict(w_in=v_in_t, w_pool_proj=v_w_pool_proj[0], w_gla_proj=v_w_gla_proj[0], w_out=v_w_out[0],
              w_up=v_w_up[0], w_down=v_w_down[0])
    grad, delta, new_m, new_v = {}, {}, {}, {}

    def finish_and_update(names, group_axes, part_f, landed, tag):
        halves = [_finish_half(pf, rb, ax, "finish_" + n) for n, ax, pf, rb in zip(names, group_axes, part_f, landed)]
        sib_halves = _sibling_share(halves, "sibling_share_" + tag)
        for n, ax, mine, theirs in zip(names, group_axes, halves, sib_halves):
            res = _adam_halves(place, shards[n], mine, theirs, ms[n], vs[n], ax, "adam_" + n)
            if n == "w_in":
                res = [jnp.transpose(r_) for r_ in res]
            grad[n], delta[n], new_m[n], new_v[n] = [r_[None] for r_ in res]

    _, ffn_landed = _split_wait("reduce_ffn_wait", ffn_handle, 2, ffn_plan, token)
    _, out_landed = _split_wait("reduce_out_wait", out_handle, 3, out_plan, ffn_landed[0])
    finish_and_update(ffn_names + out_names, ffn_axes + out_axes, ffn_pf + out_pf, ffn_landed + out_landed, "rest")
    _, in_landed = _split_wait("reduce_in_wait", in_handle, 1, in_plan, delta["w_out"])
    finish_and_update(("w_in",), (1,), (in_pf,), in_landed, "in")
    small_sent, small_landed = _split_wait("reduce_small_wait", small_handle, len(small_chip), small_plan, delta["w_in"])
    given = dict(g_mix=(g_mix, m_g_mix, v_g_mix), b_gate=(b_gate, m_b_gate, v_b_gate), w_gk_up=(w_gk_up, m_w_gk_up, v_w_gk_up),
                 b_gk=(b_gk, m_b_gk, v_b_gk), w_pool_grp=(w_pool_grp, m_w_pool_grp, v_w_pool_grp),
                 pool_scale=(pool_scale, m_pool_scale, v_pool_scale), g_gla_head=(g_gla_head, m_g_gla_head, v_g_gla_head),
                 g_ffn=(g_ffn, m_g_ffn, v_g_ffn), w_conv=(w_conv, m_w_conv, v_w_conv), b_conv=(b_conv, m_b_conv, v_b_conv),
                 g_final=(g_final, m_g_final, v_g_final))
    flat2 = lambda a: a.reshape(-1, a.shape[-1])
    widths = [dict(w_gk_up=HK, w_conv=UP_SHARD).get(n) for n in small_names]
    totals, ds, mo, vo = _adam_small(place, small_sent, small_landed, *[[flat2(given[n][k]) for n in small_names] for k in range(3)],
                                     widths)
    loss = totals[-1][0, 0]
    for i, n in enumerate(small_names):
        shp = given[n][0].shape
        grad[n], delta[n], new_m[n], new_v[n] = [r_.reshape(shp) for r_ in (totals[i], ds[i], mo[i], vo[i])]

    order = ("g_mix", "w_in", "b_gate", "w_gk_up", "b_gk", "w_pool_grp", "pool_scale", "g_gla_head", "w_pool_proj",
             "w_gla_proj", "w_out", "g_ffn", "w_up", "w_conv", "b_conv", "w_down", "g_final")
    return (loss, grad_x[None], *[grad[n] for n in order], *[delta[n] for n in order], *[new_m[n] for n in order],
            *[new_v[n] for n in order])
```

```python
import jax
import jax.numpy as jnp
from jax import lax
from jax.experimental import pallas as pl
from jax.experimental.pallas import tpu as pltpu

F32 = jnp.float32
BF16 = jnp.bfloat16
MESH = pl.DeviceIdType.MESH

D = 1024
EPS = 1e-6
CHUNK = 64
POOL_W = 512
POOL_WINDOWS = (2, 4, 8, 16)
HEADS = 4
HK = 128
HV = 256
GATE_RANK = 16
D_FF = 2816
N_UP = 2 * D_FF
N_IN = 5648
QSCALE = HK ** -0.5
N_INR = 5760
OFF_GATE, OFF_V, OFF_OG, OFF_POOL, OFF_Q, OFF_K, OFF_GK = 0, 2048, 3072, 4096, 4608, 5120, 5632

ADAM_LR, ADAM_B1, ADAM_B2, ADAM_EPS, ADAM_WD, ADAM_STEP = 0.001, 0.9, 0.999, 1e-08, 0.01, 10

VMEM_LIMIT = 56 * 1024 * 1024
PROJ_TILE = N_INR // 5
UP_SHARD = N_UP // 4


def _cp(*sem):
    return pltpu.CompilerParams(dimension_semantics=sem if sem else None, vmem_limit_bytes=VMEM_LIMIT)


def _dot(a, b):
    return jnp.dot(a, b, preferred_element_type=F32)


def _dot_nt(a, b):
    return lax.dot_general(a, b, (((1,), (1,)), ((), ())), preferred_element_type=F32)


def _dot_tn(a, b):
    return lax.dot_general(a, b, (((0,), (0,)), ((), ())), preferred_element_type=F32)


def _sigmoid(v):
    return 1.0 / (1.0 + jnp.exp(-v))


def _rows(shape):
    return lax.broadcasted_iota(jnp.int32, shape, 0)


def _pick_row(v, r):
    return jnp.sum(jnp.where(_rows(v.shape) == r, v, 0.0), axis=0, keepdims=True)


def _rmsnorm(x, g, after, name, ts):
    s = x.shape[0]

    def body(x_ref, g_ref, after_ref, h_ref):
        xv = x_ref[...]
        r = lax.rsqrt(jnp.mean(xv * xv, axis=-1, keepdims=True) + EPS)
        h_ref[...] = (xv * r * g_ref[...]).astype(BF16)

    return pl.pallas_call(
        body, name=name, grid=(s // ts,),
        in_specs=[pl.BlockSpec((ts, D), lambda i: (i, 0)), pl.BlockSpec((1, D), lambda i: (0, 0)), ANY],
        out_specs=pl.BlockSpec((ts, D), lambda i: (i, 0)), out_shape=jax.ShapeDtypeStruct((s, D), BF16),
        compiler_params=_cp("arbitrary"),
    )(x, g, after)


MM_ROWS = 512
SUB_ROWS = 256


def _matmul_resident(h, w, after, name):
    s = h.shape[0]
    nj, tn = w.shape[0], w.shape[2]
    rc = min(s, MM_ROWS)

    def body(h_ref, w_ref, after_ref, z_ref):
        for r0 in range(0, s, rc):
            z_ref[r0:r0 + rc, :] = _dot(h_ref[r0:r0 + rc, :], w_ref[...]).astype(BF16)

    return pl.pallas_call(
        body, name=name, grid=(nj,),
        in_specs=[pl.BlockSpec((s, D), lambda j: (0, 0)), pl.BlockSpec((None, D, tn), lambda j: (j, 0, 0)), ANY],
        out_specs=pl.BlockSpec((s, tn), lambda j: (0, j)), out_shape=jax.ShapeDtypeStruct((s, nj * tn), BF16),
        compiler_params=_cp("arbitrary"),
    )(h, w, after)


PROJ_PIECES = ((3600, 2048, OFF_GATE), (1536, 2048, OFF_V), (0, 1536, OFF_POOL), (3584, GATE_RANK, OFF_GK))


def _projection_copies(w_hbm, w_ref, sems):
    return [pltpu.make_async_copy(w_hbm.at[pl.ds(src, n)], w_ref.at[pl.ds(dst, n)], sems.at[i])
            for i, (src, n, dst) in enumerate(PROJ_PIECES)]


def _load_projection(w_hbm, w_ref, sems):
    cps = _projection_copies(w_hbm, w_ref, sems)
    for cp in cps:
        cp.start()
    w_ref[OFF_GK + GATE_RANK:, :] = jnp.zeros((N_INR - OFF_GK - GATE_RANK, D), BF16)
    for cp in cps:
        cp.wait()


def _in_proj(h, w_nat, tn):
    s = h.shape[0]
    rc = min(s, MM_ROWS)
    nj = N_INR // tn
    first_use = [dst // tn for _, _, dst in PROJ_PIECES]

    def body(h_ref, w_hbm, z_ref, w_ref, sems):
        j = pl.program_id(0)
        cps = _projection_copies(w_hbm, w_ref, sems)

        @pl.when(j == 0)
        def _():
            for cp in cps:
                cp.start()
            w_ref[OFF_GK + GATE_RANK:, :] = jnp.zeros((N_INR - OFF_GK - GATE_RANK, D), BF16)

        for step in range(nj):
            due = [cp for cp, at in zip(cps, first_use) if at == step]
            if due:
                @pl.when(j == step)
                def _(due=due):
                    for cp in due:
                        cp.wait()

        wt = w_ref[pl.ds(pl.multiple_of(j * tn, 128), tn), :]
        for r0 in range(0, s, rc):
            z_ref[r0:r0 + rc, :] = _dot_nt(h_ref[r0:r0 + rc, :], wt).astype(BF16)

    return pl.pallas_call(
        body, name="in_proj", grid=(nj,),
        in_specs=[pl.BlockSpec((s, D), lambda j: (0, 0)), ANY],
        out_specs=pl.BlockSpec((s, tn), lambda j: (0, j)), out_shape=jax.ShapeDtypeStruct((s, N_INR), BF16),
        scratch_shapes=[pltpu.VMEM((N_INR, D), BF16), pltpu.SemaphoreType.DMA((len(PROJ_PIECES),))],
        compiler_params=_cp("arbitrary"),
    )(h, w_nat)


def _matmul_nt_normbwd(dz, w, x, g, resid, after, name, ts, transposed=False):
    s = x.shape[0]
    w_vmem = (N_INR, D) if transposed else (D, w.shape[0] * w.shape[2])
    n_sems = len(PROJ_PIECES) if transposed else w.shape[0]

    def body(dz_ref, w_hbm, x_ref, g_ref, r_ref, after_ref, o_ref, ob_ref, dg_ref, w_ref, sems):
        @pl.when(pl.program_id(0) == 0)
        def _():
            if transposed:
                _load_projection(w_hbm, w_ref, sems)
            else:
                kc = w.shape[2]
                cps = [pltpu.make_async_copy(w_hbm.at[j], w_ref.at[:, pl.ds(j * kc, kc)], sems.at[j])
                       for j in range(w.shape[0])]
                for cp in cps:
                    cp.start()
                for cp in cps:
                    cp.wait()
            dg_ref[...] = jnp.zeros_like(dg_ref)

        dh = _dot(dz_ref[...], w_ref[...]) if transposed else _dot_nt(dz_ref[...], w_ref[...])
        xv = x_ref[...]
        r = lax.rsqrt(jnp.mean(xv * xv, axis=-1, keepdims=True) + EPS)
        xh = xv * r
        dg_ref[...] += jnp.sum(dh * xh, axis=0, keepdims=True)
        dxh = dh * g_ref[...]
        out = r_ref[...] + r * (dxh - xh * jnp.mean(dxh * xh, axis=-1, keepdims=True))
        o_ref[...] = out
        ob_ref[...] = out.astype(BF16)

    row = lambda i: (i, 0)
    kdim = dz.shape[1]
    return pl.pallas_call(
        body, name=name, grid=(s // ts,),
        in_specs=[pl.BlockSpec((ts, kdim), row), ANY, pl.BlockSpec((ts, D), row),
                  pl.BlockSpec((1, D), lambda i: (0, 0)), pl.BlockSpec((ts, D), row), ANY],
        out_specs=[pl.BlockSpec((ts, D), row), pl.BlockSpec((ts, D), row), pl.BlockSpec((1, D), lambda i: (0, 0))],
        out_shape=[jax.ShapeDtypeStruct((s, D), F32), jax.ShapeDtypeStruct((s, D), BF16),
                   jax.ShapeDtypeStruct((1, D), F32)],
        scratch_shapes=[pltpu.VMEM(w_vmem, BF16), pltpu.SemaphoreType.DMA((n_sems,))],
        compiler_params=_cp("arbitrary"),
    )(dz, w, x, g, resid, after)


def _matmul_tn(a, b, name, tn, shard_major=False, tm=None):
    s, m = a.shape
    n = b.shape[1]
    tm = m if tm is None else tm
    ni, nj = m // tm, n // tn

    def body(a_ref, b_ref, o_ref):
        o_ref[...] = _dot_tn(a_ref[...], b_ref[...]).astype(BF16)

    if shard_major:
        out_spec = pl.BlockSpec((None, tm, tn), lambda i, j: (j, i, 0))
        out_shape = jax.ShapeDtypeStruct((nj, m, tn), BF16)
    else:
        out_spec = pl.BlockSpec((tm, tn), lambda i, j: (i, j))
        out_shape = jax.ShapeDtypeStruct((m, n), BF16)
    return pl.pallas_call(
        body, name=name, grid=(ni, nj),
        in_specs=[pl.BlockSpec((s, tm), lambda i, j: (0, i)), pl.BlockSpec((s, tn), lambda i, j: (0, j))],
        out_specs=out_spec, out_shape=out_shape,
        compiler_params=_cp("arbitrary", "arbitrary"),
    )(a, b)


def _pool_fwd(zr, wgrp, scale):
    s = zr.shape[0]

    def body(u_ref, w_ref, sc_ref, p_ref, pp_ref):
        row = _rows((s, 128))
        for gi, win in enumerate(POOL_WINDOWS):
            cs = slice(gi * 128, (gi + 1) * 128)
            u = u_ref[:, cs].astype(F32)
            acc, k = u, 1
            while k < win:
                acc = acc + jnp.where(row >= k, pltpu.roll(acc, k, 0), 0.0)
                k *= 2
            cnt = jnp.minimum(row + 1, win).astype(F32)
            p = (acc / cnt - u).astype(BF16)
            p_ref[:, cs] = p
            pp_ref[:, cs] = (_dot(p, w_ref[gi].astype(BF16)) * sc_ref[:, cs]).astype(BF16)

    return pl.pallas_call(
        body, name="pool_fwd", grid=(1,),
        in_specs=[pl.BlockSpec((s, POOL_W), lambda i: (0, OFF_POOL // POOL_W)),
                  pl.BlockSpec((4, 128, 128), lambda i: (0, 0, 0)), pl.BlockSpec((1, POOL_W), lambda i: (0, 0))],
        out_specs=[pl.BlockSpec((s, POOL_W), lambda i: (0, 0))] * 2,
        out_shape=[jax.ShapeDtypeStruct((s, POOL_W), BF16)] * 2,
        compiler_params=_cp("arbitrary"),
    )(zr, wgrp, scale)


def _pool_bwd(p, dpp, wgrp, scale, after):
    s = p.shape[0]

    def body(p_ref, dpp_ref, w_ref, sc_ref, after_ref, dz_ref, dw_ref, dsc_ref):
        row = _rows((s, 128))
        for gi, win in enumerate(POOL_WINDOWS):
            cs = slice(gi * 128, (gi + 1) * 128)
            pv = p_ref[:, cs]
            wb = w_ref[gi].astype(BF16)
            dpp_v = dpp_ref[:, cs].astype(F32)
            dsc_ref[:, cs] = jnp.sum(dpp_v * _dot(pv, wb), axis=0, keepdims=True)
            dpm = (dpp_v * sc_ref[:, cs]).astype(BF16)
            dw_ref[gi] = _dot_tn(pv, dpm)
            dp = _dot_nt(dpm, wb)
            cnt = jnp.minimum(row + 1, win).astype(F32)
            acc, k = dp / cnt, 1
            while k < win:
                acc = acc + jnp.where(row < s - k, pltpu.roll(acc, s - k, 0), 0.0)
                k *= 2
            dz_ref[:, cs] = (acc - dp).astype(BF16)

    full = lambda i: (0, 0)
    return pl.pallas_call(
        body, name="pool_bwd", grid=(1,),
        in_specs=[pl.BlockSpec((s, POOL_W), full), pl.BlockSpec((s, POOL_W), full),
                  pl.BlockSpec((4, 128, 128), lambda i: (0, 0, 0)), pl.BlockSpec((1, POOL_W), full), ANY],
        out_specs=[pl.BlockSpec((s, POOL_W), full), pl.BlockSpec((4, 128, 128), lambda i: (0, 0, 0)),
                   pl.BlockSpec((1, POOL_W), full)],
        out_shape=[jax.ShapeDtypeStruct((s, POOL_W), BF16), jax.ShapeDtypeStruct((4, 128, 128), F32),
                   jax.ShapeDtypeStruct((1, POOL_W), F32)],
        compiler_params=_cp("arbitrary"),
    )(p, dpp, wgrp, scale, after)


def _gla_decay(zgk_ref, wgk_ref, bgk_ref, rb):
    g = _dot(zgk_ref[...], wgk_ref[...].astype(BF16)) + bgk_ref[...]
    la = (jnp.minimum(g, 0.0) - jnp.log(1.0 + jnp.exp(-jnp.abs(g)))) * (1.0 / 16.0)
    rowm = _rows(la.shape) & (CHUNK - 1)
    bc, k = la, 1
    while k < CHUNK:
        bc = bc + jnp.where(rowm >= k, pltpu.roll(bc, k, 0), 0.0)
        k *= 2
    return g, jnp.exp(bc), jnp.exp(-bc)


GLA_HB = 4


def _gla_specs(rb, rmap):
    wk, wv = GLA_HB * HK, GLA_HB * HV
    return [pl.BlockSpec((rb, wk), lambda h, r: (rmap(h, r), OFF_Q // wk + h)),
            pl.BlockSpec((rb, wk), lambda h, r: (rmap(h, r), OFF_K // wk + h)),
            pl.BlockSpec((rb, wv), lambda h, r: (rmap(h, r), OFF_V // wv + h)),
            pl.BlockSpec((rb, 128), lambda h, r: (rmap(h, r), OFF_GK // 128))]


def _gla_fwd(zr, wgk, bgk, ghead, rb):
    s = zr.shape[0]
    nc = rb // CHUNK
    wk, wv = GLA_HB * HK, GLA_HB * HV

    def body(q_ref, k_ref, v_ref, zgk_ref, zog_ref, wgk_ref, bgk_ref, gh_ref, o_ref, og_ref, sp_ref, st_ref, kv_ref):
        @pl.when(pl.program_id(1) == 0)
        def _():
            st_ref[...] = jnp.zeros_like(st_ref)

        _, e_pos, e_neg = _gla_decay(zgk_ref, wgk_ref, bgk_ref, rb)
        lower = _rows((CHUNK, CHUNK)) >= lax.broadcasted_iota(jnp.int32, (CHUNK, CHUNK), 1)
        pairs = [(c, hh) for c in range(nc) for hh in range(GLA_HB)]
        rows = lambda c: slice(c * CHUNK, (c + 1) * CHUNK)
        cols_k = lambda hh: slice(hh * HK, (hh + 1) * HK)
        cols_v = lambda hh: slice(hh * HV, (hh + 1) * HV)
        qfws, pms, e_lasts = {}, {}, {}
        for c, hh in pairs:
            q = q_ref[rows(c), cols_k(hh)].astype(F32) * QSCALE
            k = k_ref[rows(c), cols_k(hh)].astype(F32)
            ec, fc = e_pos[rows(c), cols_k(hh)], e_neg[rows(c), cols_k(hh)]
            qfw = (q * ec).astype(BF16)
            kfw_f = k * fc
            s_fw = _dot_nt(qfw, kfw_f.astype(BF16))
            s_bw = _dot_nt((q * fc).astype(BF16), (k * ec).astype(BF16))
            e_last = _pick_row(ec, CHUNK - 1)
            kv_ref[c, hh] = _dot_tn(v_ref[rows(c), cols_v(hh)], (kfw_f * e_last).astype(BF16))
            qfws[c, hh], pms[c, hh], e_lasts[c, hh] = qfw, jnp.where(lower, s_fw, s_bw).astype(BF16), e_last
        for hh in range(GLA_HB):
            st = st_ref[hh]
            for c in range(nc):
                sp_ref[c, hh] = st.astype(BF16)
                st = st * e_lasts[c, hh] + kv_ref[c, hh]
            st_ref[hh] = st
        for c, hh in pairs:
            o = _dot(pms[c, hh], v_ref[rows(c), cols_v(hh)]) + _dot_nt(qfws[c, hh], sp_ref[c, hh])
            r = lax.rsqrt(jnp.mean(o * o, axis=-1, keepdims=True) + EPS)
            zo = zog_ref[rows(c), cols_v(hh)].astype(F32)
            o_ref[rows(c), cols_v(hh)] = o.astype(BF16)
            og_ref[rows(c), cols_v(hh)] = (o * r * gh_ref[...] * zo * _sigmoid(zo)).astype(BF16)

    rmap = lambda h, r: r
    return pl.pallas_call(
        body, name="gla_fwd", grid=(HEADS // GLA_HB, s // rb),
        in_specs=_gla_specs(rb, rmap) + [
            pl.BlockSpec((rb, wv), lambda h, r: (r, OFF_OG // wv + h)),
            pl.BlockSpec((128, wk), lambda h, r: (0, h)), pl.BlockSpec((1, wk), lambda h, r: (0, h)),
            pl.BlockSpec((1, HV), lambda h, r: (0, 0))],
        out_specs=[pl.BlockSpec((rb, wv), lambda h, r: (r, h)), pl.BlockSpec((rb, wv), lambda h, r: (r, h)),
                   pl.BlockSpec((nc, GLA_HB, HV, HK), lambda h, r: (r, h, 0, 0))],
        out_shape=[jax.ShapeDtypeStruct((s, D), BF16), jax.ShapeDtypeStruct((s, D), BF16),
                   jax.ShapeDtypeStruct((s // CHUNK, HEADS, HV, HK), BF16)],
        scratch_shapes=[pltpu.VMEM((GLA_HB, HV, HK), F32), pltpu.VMEM((nc, GLA_HB, HV, HK), F32)],
        compiler_params=_cp("arbitrary", "arbitrary"),
    )(zr, zr, zr, zr, zr, wgk, bgk, ghead)


def _gla_bwd(zr, do, sp, wgk, bgk, after, rb):
    s = zr.shape[0]
    nc = rb // CHUNK
    nr = s // rb
    wk, wv = GLA_HB * HK, GLA_HB * HV

    def body(q_ref, k_ref, v_ref, zgk_ref, do_ref, sp_ref, wgk_ref, bgk_ref, after_ref, dq_ref, dk_ref, dv_ref, dg_ref,
             gt_ref, dbc_ref, gs_ref):
        @pl.when(pl.program_id(1) == 0)
        def _():
            gt_ref[...] = jnp.zeros_like(gt_ref)

        g, e_pos, e_neg = _gla_decay(zgk_ref, wgk_ref, bgk_ref, rb)
        lower = _rows((CHUNK, CHUNK)) >= lax.broadcasted_iota(jnp.int32, (CHUNK, CHUNK), 1)
        is_last = _rows((CHUNK, HK)) == CHUNK - 1
        pairs = [(c, hh) for c in range(nc) for hh in range(GLA_HB)]
        rows = lambda c: slice(c * CHUNK, (c + 1) * CHUNK)
        cols_k = lambda hh: slice(hh * HK, (hh + 1) * HK)
        cols_v = lambda hh: slice(hh * HV, (hh + 1) * HV)
        e_lasts = {}
        for c, hh in pairs:
            ec = e_pos[rows(c), cols_k(hh)]
            qfw = (q_ref[rows(c), cols_k(hh)].astype(F32) * QSCALE * ec).astype(BF16)
            gs_ref[c, hh] = _dot_tn(do_ref[rows(c), cols_v(hh)], qfw)
            e_lasts[c, hh] = _pick_row(ec, CHUNK - 1)
        for hh in range(GLA_HB):
            gt = gt_ref[hh]
            for c in reversed(range(nc)):
                own = gs_ref[c, hh]
                gs_ref[c, hh] = gt
                gt = own + gt * e_lasts[c, hh]
            gt_ref[hh] = gt
        def decayed(c, hh):
            q = q_ref[rows(c), cols_k(hh)].astype(F32) * QSCALE
            k = k_ref[rows(c), cols_k(hh)].astype(F32)
            ec, fc = e_pos[rows(c), cols_k(hh)], e_neg[rows(c), cols_k(hh)]
            return ec, fc, q * ec, k * fc, q * fc, k * ec

        pms, dss = {}, {}
        for c, hh in pairs:
            _, _, qfw_f, kfw_f, qbw_f, kbw_f = decayed(c, hh)
            s_fw = _dot_nt(qfw_f.astype(BF16), kfw_f.astype(BF16))
            s_bw = _dot_nt(qbw_f.astype(BF16), kbw_f.astype(BF16))
            dp = _dot_nt(do_ref[rows(c), cols_v(hh)], v_ref[rows(c), cols_v(hh)])
            pms[c, hh] = jnp.where(lower, s_fw, s_bw).astype(BF16)
            dss[c, hh] = (jnp.where(lower, dp, 0.0).astype(BF16), jnp.where(lower, 0.0, dp).astype(BF16))
        for c, hh in pairs:
            sl, ck, cv = rows(c), cols_k(hh), cols_v(hh)
            v = v_ref[sl, cv]
            dov = do_ref[sl, cv]
            ec, fc, qfw_f, kfw_f, qbw_f, kbw_f = decayed(c, hh)
            qfw, kfw, qbw, kbw = qfw_f.astype(BF16), kfw_f.astype(BF16), qbw_f.astype(BF16), kbw_f.astype(BF16)
            pm = pms[c, hh]
            e_last = e_lasts[c, hh]
            kdec = (kfw_f * e_last).astype(BF16)
            gt = gs_ref[c, hh]
            gtb = gt.astype(BF16)
            spv = sp_ref[c, hh]
            dv_ref[sl, cv] = (_dot_tn(pm, dov) + _dot_nt(kdec, gtb)).astype(BF16)
            ds_fw, ds_bw = dss[c, hh]
            dqfw = _dot(ds_fw, kfw) + _dot(dov, spv)
            dkfw = _dot_tn(ds_fw, qfw)
            dqbw = _dot(ds_bw, kbw)
            dkbw = _dot_tn(ds_bw, qbw)
            dkdec = _dot(v, gtb)
            de_last = (jnp.sum(gt * spv.astype(F32), axis=0, keepdims=True)
                       + jnp.sum(dkdec * kfw_f, axis=0, keepdims=True))
            dkfw = dkfw + dkdec * e_last
            dq_ref[sl, ck] = ((dqfw * ec + dqbw * fc) * QSCALE).astype(BF16)
            dk_ref[sl, ck] = (dkfw * fc + dkbw * ec).astype(BF16)
            dbc = dqfw * qfw_f - dqbw * qbw_f + dkbw * kbw_f - dkfw * kfw_f
            dbc_ref[sl, ck] = dbc + jnp.where(is_last, de_last * e_last, 0.0)
        rowm = _rows((rb, wk)) & (CHUNK - 1)
        dla, kk = dbc_ref[...], 1
        while kk < CHUNK:
            dla = dla + jnp.where(rowm < CHUNK - kk, pltpu.roll(dla, rb - kk, 0), 0.0)
            kk *= 2
        dg_ref[...] = dla * (1.0 / 16.0) * _sigmoid(-g)

    rmap = lambda h, r: nr - 1 - r
    rev = lambda h, r: (nr - 1 - r, h)
    return pl.pallas_call(
        body, name="gla_bwd", grid=(HEADS // GLA_HB, nr),
        in_specs=_gla_specs(rb, rmap) + [
            pl.BlockSpec((rb, wv), rev),
            pl.BlockSpec((nc, GLA_HB, HV, HK), lambda h, r: (nr - 1 - r, h, 0, 0)),
            pl.BlockSpec((128, wk), lambda h, r: (0, h)), pl.BlockSpec((1, wk), lambda h, r: (0, h)), ANY],
        out_specs=[pl.BlockSpec((rb, wk), rev), pl.BlockSpec((rb, wk), rev), pl.BlockSpec((rb, wv), rev),
                   pl.BlockSpec((rb, wk), rev)],
        out_shape=[jax.ShapeDtypeStruct((s, HEADS * HK), BF16), jax.ShapeDtypeStruct((s, HEADS * HK), BF16),
                   jax.ShapeDtypeStruct((s, D), BF16), jax.ShapeDtypeStruct((s, HEADS * HK), F32)],
        scratch_shapes=[pltpu.VMEM((GLA_HB, HV, HK), F32), pltpu.VMEM((rb, wk), F32),
                        pltpu.VMEM((nc, GLA_HB, HV, HK), F32)],
        compiler_params=_cp("arbitrary", "arbitrary"),
    )(zr, zr, zr, zr, do, sp, wgk, bgk, after)


def _gk_bwd(dgpre, zr, wgk, after, ts):
    s = zr.shape[0]

    def body(dg_ref, zgk_ref, w_ref, after_ref, dz_ref, dw_ref, db_ref):
        @pl.when(pl.program_id(0) == 0)
        def _():
            dw_ref[...] = jnp.zeros_like(dw_ref)
            db_ref[...] = jnp.zeros_like(db_ref)

        dg = dg_ref[...]
        dgb = dg.astype(BF16)
        dz_ref[...] = _dot_nt(dgb, w_ref[...].astype(BF16)).astype(BF16)
        dw_ref[...] += _dot_tn(zgk_ref[...], dgb)
        db_ref[...] += jnp.sum(dg, axis=0, keepdims=True)

    return pl.pallas_call(
        body, name="gk_bwd", grid=(s // ts,),
        in_specs=[pl.BlockSpec((ts, 512), lambda i: (i, 0)), pl.BlockSpec((ts, 128), lambda i: (i, OFF_GK // 128)),
                  pl.BlockSpec((128, 512), lambda i: (0, 0)), ANY],
        out_specs=[pl.BlockSpec((ts, 128), lambda i: (i, 0)), pl.BlockSpec((128, 512), lambda i: (0, 0)),
                   pl.BlockSpec((1, 512), lambda i: (0, 0))],
        out_shape=[jax.ShapeDtypeStruct((s, 128), BF16), jax.ShapeDtypeStruct((128, 512), F32),
                   jax.ShapeDtypeStruct((1, 512), F32)],
        compiler_params=_cp("arbitrary"),
    )(dgpre, zr, wgk, after)


def _merge_fwd(x, zr, pp, og, bgate, wpp, wgla, wout, gffn, after, ts):
    s = x.shape[0]

    def body(x_ref, z0_ref, z1_ref, pp_ref, og_ref, bg_ref, wpp_ref, wgla_ref, wout_ref, gf_ref, after_ref,
             x1_ref, mix_ref, yp_ref, yg_ref, h2_ref):
        ppv = pp_ref[...]
        yp = jnp.concatenate([_dot(ppv, wpp_ref[j]) for j in range(4)], axis=1)
        yg = _dot(og_ref[...], wgla_ref[...])
        g0 = _sigmoid(z0_ref[...].astype(F32) + bg_ref[:, :D])
        g1 = _sigmoid(z1_ref[...].astype(F32) + bg_ref[:, D:])
        mixed = (g0 * yp + g1 * yg).astype(BF16)
        x1 = x_ref[...] + _dot(mixed, wout_ref[...])
        x1_ref[...] = x1
        mix_ref[...] = mixed
        yp_ref[...] = yp.astype(BF16)
        yg_ref[...] = yg.astype(BF16)
        r = lax.rsqrt(jnp.mean(x1 * x1, axis=-1, keepdims=True) + EPS)
        h2_ref[...] = (x1 * r * gf_ref[...]).astype(BF16)

    row = lambda i: (i, 0)
    const2 = lambda i: (0, 0)
    return pl.pallas_call(
        body, name="merge_fwd", grid=(s // ts,),
        in_specs=[pl.BlockSpec((ts, D), row), pl.BlockSpec((ts, D), lambda i: (i, 0)), pl.BlockSpec((ts, D), lambda i: (i, 1)),
                  pl.BlockSpec((ts, POOL_W), row), pl.BlockSpec((ts, D), row), pl.BlockSpec((1, 2 * D), const2),
                  pl.BlockSpec((4, POOL_W, 256), lambda i: (0, 0, 0)), pl.BlockSpec((D, D), const2),
                  pl.BlockSpec((D, D), const2), pl.BlockSpec((1, D), const2), ANY],
        out_specs=[pl.BlockSpec((ts, D), row)] * 5,
        out_shape=[jax.ShapeDtypeStruct((s, D), F32)] + [jax.ShapeDtypeStruct((s, D), BF16)] * 4,
        compiler_params=_cp("arbitrary"),
    )(x, zr, zr, pp, og, bgate, wpp, wgla, wout, gffn, after)


def _merge_bwd(dx1b, zr, yp, yg, o, bgate, ghead, wpp, wgla, wout, after, ts):
    s = dx1b.shape[0]

    def body(dx_ref, z0_ref, z1_ref, zog_ref, yp_ref, yg_ref, o_ref, bg_ref, gh_ref, wpp_ref, wgla_ref, wout_ref, after_ref,
             dzg_ref, dyp_ref, dyg_ref, dpp_ref, do_ref, dzog_ref, dbg_ref, dgh_ref):
        @pl.when(pl.program_id(0) == 0)
        def _():
            dbg_ref[...] = jnp.zeros_like(dbg_ref)
            dgh_ref[...] = jnp.zeros_like(dgh_ref)

        subs = [slice(r0, r0 + SUB_ROWS) for r0 in range(0, ts, SUB_ROWS)] if ts > SUB_ROWS else [slice(0, ts)]
        dmix = [_dot_nt(dx_ref[rs, :], wout_ref[...]) for rs in subs]
        dyp, dyg = [], []
        dbg0, dbg1 = jnp.zeros((1, D), F32), jnp.zeros((1, D), F32)
        for rs, dm in zip(subs, dmix):
            g0 = _sigmoid(z0_ref[rs, :].astype(F32) + bg_ref[:, :D])
            g1 = _sigmoid(z1_ref[rs, :].astype(F32) + bg_ref[:, D:])
            dypb = (dm * g0).astype(BF16)
            dygb = (dm * g1).astype(BF16)
            dz0 = dm * yp_ref[rs, :].astype(F32) * g0 * (1.0 - g0)
            dz1 = dm * yg_ref[rs, :].astype(F32) * g1 * (1.0 - g1)
            dzg_ref[rs, :D] = dz0.astype(BF16)
            dzg_ref[rs, D:] = dz1.astype(BF16)
            dbg0 = dbg0 + jnp.sum(dz0, axis=0, keepdims=True)
            dbg1 = dbg1 + jnp.sum(dz1, axis=0, keepdims=True)
            dyp_ref[rs, :] = dypb
            dyg_ref[rs, :] = dygb
            dyp.append(dypb)
            dyg.append(dygb)
        dbg_ref[:, :D] += dbg0
        dbg_ref[:, D:] += dbg1
        dogs = []
        for rs, dypb, dygb in zip(subs, dyp, dyg):
            dpp = _dot_nt(dypb[:, 0:256], wpp_ref[0])
            for j in range(1, 4):
                dpp = dpp + _dot_nt(dypb[:, j * 256:(j + 1) * 256], wpp_ref[j])
            dpp_ref[rs, :] = dpp.astype(BF16)
            dogs.append(_dot_nt(dygb, wgla_ref[...]))
        gh = gh_ref[...]
        dgh = jnp.zeros((1, HV), F32)
        for rs, dog in zip(subs, dogs):
            for h in range(HEADS):
                cs = slice(h * HV, (h + 1) * HV)
                ov = o_ref[rs, cs].astype(F32)
                r = lax.rsqrt(jnp.mean(ov * ov, axis=-1, keepdims=True) + EPS)
                oh = ov * r
                zo = zog_ref[rs, cs].astype(F32)
                sg = _sigmoid(zo)
                dog_h = dog[:, cs]
                don = dog_h * zo * sg
                dzog_ref[rs, cs] = (dog_h * oh * gh * sg * (1.0 + zo * (1.0 - sg))).astype(BF16)
                dgh = dgh + jnp.sum(don * oh, axis=0, keepdims=True)
                doh = don * gh
                do_ref[rs, cs] = (r * (doh - oh * jnp.mean(doh * oh, axis=-1, keepdims=True))).astype(BF16)
        dgh_ref[...] += dgh

    row = lambda i: (i, 0)
    const2 = lambda i: (0, 0)
    return pl.pallas_call(
        body, name="merge_bwd", grid=(s // ts,),
        in_specs=[pl.BlockSpec((ts, D), row), pl.BlockSpec((ts, D), lambda i: (i, 0)), pl.BlockSpec((ts, D), lambda i: (i, 1)),
                  pl.BlockSpec((ts, D), lambda i: (i, OFF_OG // D)), pl.BlockSpec((ts, D), row), pl.BlockSpec((ts, D), row),
                  pl.BlockSpec((ts, D), row), pl.BlockSpec((1, 2 * D), const2), pl.BlockSpec((1, HV), const2),
                  pl.BlockSpec((4, POOL_W, 256), lambda i: (0, 0, 0)), pl.BlockSpec((D, D), const2),
                  pl.BlockSpec((D, D), const2), ANY],
        out_specs=[pl.BlockSpec((ts, 2 * D), row), pl.BlockSpec((ts, D), row), pl.BlockSpec((ts, D), row),
                   pl.BlockSpec((ts, POOL_W), row), pl.BlockSpec((ts, D), row), pl.BlockSpec((ts, D), row),
                   pl.BlockSpec((1, 2 * D), const2), pl.BlockSpec((1, HV), const2)],
        out_shape=[jax.ShapeDtypeStruct((s, 2 * D), BF16), jax.ShapeDtypeStruct((s, D), BF16),
                   jax.ShapeDtypeStruct((s, D), BF16), jax.ShapeDtypeStruct((s, POOL_W), BF16),
                   jax.ShapeDtypeStruct((s, D), BF16), jax.ShapeDtypeStruct((s, D), BF16),
                   jax.ShapeDtypeStruct((1, 2 * D), F32), jax.ShapeDtypeStruct((1, HV), F32)],
        compiler_params=_cp("arbitrary"),
    )(dx1b, zr, zr, zr, yp, yg, o, bgate, ghead, wpp, wgla, wout, after)


HALO = 16
CCH = D_FF // 2


def _conv_taps(u_ref, halo_ref, cs, first, ts):
    u = u_ref[:, cs].astype(F32)
    hal = halo_ref[:, cs].astype(F32)
    h1 = jnp.where(first, 0.0, _pick_row(hal, HALO - 1))
    h2 = jnp.where(first, 0.0, _pick_row(hal, HALO - 2))
    row8 = _rows((8, u.shape[1]))
    r1, r2 = pltpu.roll(u, 1, 0), pltpu.roll(u, 2, 0)
    r1 = jnp.concatenate([jnp.where(row8 == 0, h1, r1[:8]), r1[8:]], axis=0)
    r2 = jnp.concatenate([jnp.where(row8 == 0, h2, jnp.where(row8 == 1, h1, r2[:8])), r2[8:]], axis=0)
    return u, r1, r2


def _ffn_down_loss(u, x1, tgt, wconv, bconv, wdown, gfin, ts):
    s = x1.shape[0]

    def body(u_ref, halo_ref, x1_ref, t_ref, wc_ref, bc_ref, wd_ref, gf_ref, a_ref, c_ref, dx_ref, dxb_ref, ls_ref,
             dgf_ref):
        i = pl.program_id(0)

        @pl.when(i == 0)
        def _():
            ls_ref[...] = jnp.zeros_like(ls_ref)
            dgf_ref[...] = jnp.zeros_like(dgf_ref)

        first = i == 0
        acc = x1_ref[...]
        for hf in range(D_FF // CCH):
            cg = slice(hf * CCH, (hf + 1) * CCH)
            cv = slice(D_FF + hf * CCH, D_FF + (hf + 1) * CCH)
            vals = []
            for cs in (cg, cv):
                u0, u1, u2 = _conv_taps(u_ref, halo_ref, cs, first, ts)
                vals.append(bc_ref[:, cs] + wc_ref[0:1, cs] * u2 + wc_ref[1:2, cs] * u1 + wc_ref[2:3, cs] * u0)
                c_ref[:, cs] = vals[-1].astype(BF16)
            a = (vals[0] * _sigmoid(vals[0]) * vals[1]).astype(BF16)
            a_ref[:, cg] = a
            acc = acc + _dot(a, wd_ref[cg, :])
        r = lax.rsqrt(jnp.mean(acc * acc, axis=-1, keepdims=True) + EPS)
        xh = acc * r
        gf = gf_ref[...]
        err = xh * gf - t_ref[...]
        ls_ref[...] += (0.5 / D) * jnp.sum(jnp.sum(err * err, axis=-1, keepdims=True), axis=0, keepdims=True)
        dy = err * (1.0 / D)
        dgf_ref[...] += jnp.sum(dy * xh, axis=0, keepdims=True)
        dxh = dy * gf
        dx = r * (dxh - xh * jnp.mean(dxh * xh, axis=-1, keepdims=True))
        dx_ref[...] = dx
        dxb_ref[...] = dx.astype(BF16)

    row = lambda i: (i, 0)
    const2 = lambda i: (0, 0)
    return pl.pallas_call(
        body, name="ffn_down_loss", grid=(s // ts,),
        in_specs=[pl.BlockSpec((ts, N_UP), row),
                  pl.BlockSpec((HALO, N_UP), lambda i: (jnp.maximum(i * (ts // HALO) - 1, 0), 0)),
                  pl.BlockSpec((ts, D), row), pl.BlockSpec((ts, D), row), pl.BlockSpec((3, N_UP), const2),
                  pl.BlockSpec((1, N_UP), const2), pl.BlockSpec((D_FF, D), const2), pl.BlockSpec((1, D), const2)],
        out_specs=[pl.BlockSpec((ts, D_FF), row), pl.BlockSpec((ts, N_UP), row), pl.BlockSpec((ts, D), row),
                   pl.BlockSpec((ts, D), row), pl.BlockSpec((1, 128), const2), pl.BlockSpec((1, D), const2)],
        out_shape=[jax.ShapeDtypeStruct((s, D_FF), BF16), jax.ShapeDtypeStruct((s, N_UP), BF16),
                   jax.ShapeDtypeStruct((s, D), F32), jax.ShapeDtypeStruct((s, D), BF16),
                   jax.ShapeDtypeStruct((1, 128), F32), jax.ShapeDtypeStruct((1, D), F32)],
        compiler_params=_cp("arbitrary"),
    )(u, u, x1, tgt, wconv, bconv, wdown, gfin)


def _ffn_bwd(dx2b, u, c, wconv, wdown, ts):
    s = dx2b.shape[0]
    nt = s // ts

    def body(dx_ref, u_ref, c_ref, wc_ref, wd_ref, du_ref, db_ref, dw_ref, nxt_ref):
        @pl.when(pl.program_id(0) == 0)
        def _():
            db_ref[...] = jnp.zeros_like(db_ref)
            dw_ref[...] = jnp.zeros_like(dw_ref)
            nxt_ref[...] = jnp.zeros_like(nxt_ref)

        dxv = dx_ref[...]
        row8 = _rows((8, CCH))
        for hf in range(D_FF // CCH):
            cg = slice(hf * CCH, (hf + 1) * CCH)
            cv = slice(D_FF + hf * CCH, D_FF + (hf + 1) * CCH)
            da = _dot_nt(dxv, wd_ref[cg, :])
            gate = c_ref[:, cg].astype(F32)
            val = c_ref[:, cv].astype(F32)
            sg = _sigmoid(gate)
            dcs = (da * val * sg * (1.0 + gate * (1.0 - sg)), da * gate * sg)
            for cs, dc in zip((cg, cv), dcs):
                n1 = nxt_ref[0:1, cs]
                n2 = nxt_ref[1:2, cs]
                r1, r2 = pltpu.roll(dc, ts - 1, 0), pltpu.roll(dc, ts - 2, 0)
                f1 = jnp.concatenate([r1[:ts - 8], jnp.where(row8 == 7, n1, r1[ts - 8:])], axis=0)
                f2 = jnp.concatenate([r2[:ts - 8], jnp.where(row8 == 7, n2, jnp.where(row8 == 6, n1, r2[ts - 8:]))], axis=0)
                uv = u_ref[:, cs].astype(F32)
                db_ref[:, cs] += jnp.sum(dc, axis=0, keepdims=True)
                dw_ref[0:1, cs] += jnp.sum(f2 * uv, axis=0, keepdims=True)
                dw_ref[1:2, cs] += jnp.sum(f1 * uv, axis=0, keepdims=True)
                dw_ref[2:3, cs] += jnp.sum(dc * uv, axis=0, keepdims=True)
                du_ref[:, cs] = (wc_ref[2:3, cs] * dc + wc_ref[1:2, cs] * f1 + wc_ref[0:1, cs] * f2).astype(BF16)
                nxt_ref[:, cs] = dc[0:8, :]

    rev = lambda i: (nt - 1 - i, 0)
    const2 = lambda i: (0, 0)
    return pl.pallas_call(
        body, name="ffn_bwd", grid=(nt,),
        in_specs=[pl.BlockSpec((ts, D), rev), pl.BlockSpec((ts, N_UP), rev), pl.BlockSpec((ts, N_UP), rev),
                  pl.BlockSpec((3, N_UP), const2), pl.BlockSpec((D_FF, D), const2)],
        out_specs=[pl.BlockSpec((ts, N_UP), rev), pl.BlockSpec((1, N_UP), const2), pl.BlockSpec((3, N_UP), const2)],
        out_shape=[jax.ShapeDtypeStruct((s, N_UP), BF16), jax.ShapeDtypeStruct((1, N_UP), F32),
                   jax.ShapeDtypeStruct((3, N_UP), F32)],
        scratch_shapes=[pltpu.VMEM((8, N_UP), F32)],
        compiler_params=_cp("arbitrary"),
    )(dx2b, u, c, wconv, wdown)


ANY = pl.BlockSpec(memory_space=pl.ANY)


def _place():
    x, y, c = lax.axis_index("x"), lax.axis_index("y"), lax.axis_index("c")
    chips = [(1 - x, y), (x, 1 - y), (1 - x, 1 - y)]
    return x, y, c, chips


def _half(shape, c, axis):
    size = shape[axis] // 2
    cut = pl.ds(pl.multiple_of(c * size, 8 if axis == 0 else 128), size)
    return (cut, slice(None)) if axis == 0 else (slice(None), cut)


def _half_shape(shape, axis):
    return (shape[0] // 2, shape[1]) if axis == 0 else (shape[0], shape[1] // 2)


def _remote(src, dst, send_sems, recv_sems, k, to):
    return pltpu.make_async_remote_copy(src_ref=src, dst_ref=dst, send_sem=send_sems.at[k], recv_sem=recv_sems.at[k],
                                        device_id=to, device_id_type=MESH)


def _sibling_exchange(grads, axes, smalls, name):
    nb = len(grads)
    n = nb + len(smalls)

    def body(*refs):
        ins, outs = refs[:n], refs[n:2 * n]
        send_sems, recv_sems = refs[2 * n:]
        x, y, c, _ = _place()
        sib = (x, y, 1 - c)
        cps = []
        for a in range(nb):
            theirs = _half(grads[a].shape[1:], 1 - c, axes[a])
            cps.append(_remote(ins[a].at[(slice(None),) + theirs], outs[a], send_sems, recv_sems, a, sib))
        for a in range(nb, n):
            cps.append(_remote(ins[a], outs[a], send_sems, recv_sems, a, sib))
        for cp in cps:
            cp.start()
        for cp in cps:
            cp.wait()

    out_shape = [jax.ShapeDtypeStruct((4,) + _half_shape(g.shape[1:], ax), g.dtype) for g, ax in zip(grads, axes)]
    out_shape += [jax.ShapeDtypeStruct(a.shape, F32) for a in smalls]
    return pl.pallas_call(
        body, name=name, in_specs=[ANY] * n, out_specs=[ANY] * n, out_shape=out_shape,
        scratch_shapes=[pltpu.SemaphoreType.DMA((n,)), pltpu.SemaphoreType.DMA((n,))],
        compiler_params=pltpu.CompilerParams(has_side_effects=True),
    )(*grads, *smalls)


def _gather_share(lands, axes, name):
    n = len(lands)

    def body(*refs):
        outs = refs[n:2 * n]
        send_sems, recv_sems = refs[2 * n:]
        x, y, c, chips = _place()
        sib = (x, y, 1 - c)
        cps = []
        for a in range(n):
            mine = _half(lands[a].shape[1:], c, axes[a])
            for k, ch in enumerate(chips):
                landed = outs[a].at[(2 * ch[0] + ch[1],) + mine]
                cps.append(_remote(landed, landed, send_sems, recv_sems, 3 * a + k, sib))
        for cp in cps:
            cp.start()
        for a in range(n):
            other = _half(lands[a].shape[1:], 1 - c, axes[a])
            for k, ch in enumerate(chips):
                landed = outs[a].at[(2 * ch[0] + ch[1],) + other]
                _remote(landed, landed, send_sems, recv_sems, 3 * a + k, sib).wait_recv()
        for cp in cps:
            cp.wait_send()

    return pl.pallas_call(
        body, name=name, in_specs=[ANY] * n, out_specs=[ANY] * n,
        out_shape=[jax.ShapeDtypeStruct(a.shape, a.dtype) for a in lands],
        input_output_aliases={a: a for a in range(n)},
        scratch_shapes=[pltpu.SemaphoreType.DMA((3 * n,)), pltpu.SemaphoreType.DMA((3 * n,))],
        compiler_params=pltpu.CompilerParams(has_side_effects=True),
    )(*lands)


def _sibling_share(halves, name):
    n = len(halves)

    def body(*refs):
        ins, outs = refs[:n], refs[n:2 * n]
        send_sems, recv_sems = refs[2 * n:]
        x, y, c, _ = _place()
        cps = [_remote(ins[a], outs[a], send_sems, recv_sems, a, (x, y, 1 - c)) for a in range(n)]
        for cp in cps:
            cp.start()
        for cp in cps:
            cp.wait()

    return pl.pallas_call(
        body, name=name, in_specs=[ANY] * n, out_specs=[ANY] * n,
        out_shape=[jax.ShapeDtypeStruct(h.shape, F32) for h in halves],
        scratch_shapes=[pltpu.SemaphoreType.DMA((n,)), pltpu.SemaphoreType.DMA((n,))],
        compiler_params=pltpu.CompilerParams(has_side_effects=True),
    )(*halves)


HBM = pl.BlockSpec(memory_space=pltpu.HBM)
SEM = pl.BlockSpec(memory_space=pltpu.SEMAPHORE)
DATAFLOW = pltpu.SideEffectType.DATAFLOW_SIDE_EFFECTING


def _split_start(name, srcs, land_shapes, plan, n_copies, after):
    lands = [lax.empty(*ls) if isinstance(ls, tuple) else ls for ls in land_shapes]
    bufs = list(srcs) + lands
    nb, ns = len(bufs), len(srcs)

    def body(*refs):
        send_sems, recv_sems, token = refs[nb + 1], refs[nb + 2], refs[-1]
        for k, (src, dst, to) in enumerate(plan(refs[:ns], refs[ns:nb])):
            _remote(src, dst, send_sems, recv_sems, k, to).start()
        token[...] = jnp.zeros_like(token)

    res = pl.pallas_call(
        body, name=name,
        out_shape=(pltpu.SemaphoreType.DMA((n_copies,)), pltpu.SemaphoreType.DMA((n_copies,)),
                   *[pltpu.HBM(b.shape, b.dtype) for b in bufs], jax.ShapeDtypeStruct((8, 128), F32)),
        in_specs=[HBM] * nb + [ANY],
        out_specs=(SEM, SEM, *[HBM] * nb, pl.BlockSpec(memory_space=pltpu.VMEM)),
        input_output_aliases={i: 2 + i for i in range(nb)},
        compiler_params=pltpu.CompilerParams(has_side_effects=DATAFLOW),
    )(*[pltpu.with_memory_space_constraint(b, pltpu.HBM) for b in bufs], after)
    return (res[0], res[1], list(res[2:2 + nb])), res[-1]


def _split_wait(name, handle, n_srcs, plan, after):
    send_sems, recv_sems, bufs = handle
    nb = len(bufs)

    def body(*refs):
        sends, recvs = refs[nb], refs[nb + 1]
        for k, (src, dst, to) in enumerate(plan(refs[:n_srcs], refs[n_srcs:nb])):
            cp = _remote(src, dst, sends, recvs, k, to)
            cp.wait_send()
            cp.wait_recv()

    res = pl.pallas_call(
        body, name=name, out_shape=[pltpu.HBM(b.shape, b.dtype) for b in bufs],
        in_specs=[HBM] * nb + [SEM, SEM, ANY], out_specs=[HBM] * nb,
        input_output_aliases={i: i for i in range(nb)},
        compiler_params=pltpu.CompilerParams(has_side_effects=DATAFLOW),
    )(*bufs, send_sems, recv_sems, after)
    return list(res[:n_srcs]), list(res[n_srcs:])


def _gather_plan(shapes, axes, n_whole=0):
    def plan(srcs, lands):
        x, y, c, chips = _place()
        out = []
        for a, (shape, axis) in enumerate(zip(shapes, axes)):
            mine = _half(shape, c, axis)
            for ch in chips:
                out.append((srcs[a].at[mine], lands[a].at[(2 * x + y,) + mine], (ch[0], ch[1], c)))
        for a in range(len(shapes), len(shapes) + n_whole):
            for ch in chips:
                out.append((srcs[a], lands[a].at[2 * x + y], (ch[0], ch[1], c)))
        return out
    return plan


def _share_plan(shapes, axes):
    def plan(srcs, lands):
        x, y, c, chips = _place()
        out = []
        for a, (shape, axis) in enumerate(zip(shapes, axes)):
            mine = _half(shape, c, axis)
            for ch in chips:
                landed = lands[a].at[(2 * ch[0] + ch[1],) + mine]
                out.append((landed, landed, (x, y, 1 - c)))
        return out
    return plan


def _sibling_plan(shapes, axes):
    def plan(srcs, lands):
        x, y, c, _ = _place()
        return [(srcs[a].at[(slice(None),) + _half(shape, 1 - c, axis)], lands[a], (x, y, 1 - c))
                for a, (shape, axis) in enumerate(zip(shapes, axes))]
    return plan


def _reduce_plan(n_big, n_small):
    def plan(srcs, lands):
        x, y, c, chips = _place()
        out = []
        for a in range(n_big):
            for k, ch in enumerate(chips):
                out.append((srcs[a].at[2 * ch[0] + ch[1]], lands[a].at[k], (ch[0], ch[1], c)))
        for a in range(n_big, n_big + n_small):
            for ch in chips:
                out.append((srcs[a], lands[a].at[2 * x + y], (ch[0], ch[1], c)))
        return out
    return plan


def _row_tile(rows, cols, mult):
    best = mult
    for t in range(mult, rows + 1, mult):
        if rows % t == 0 and t * cols * 4 <= (2 << 20):
            best = t
    return best if rows % best == 0 else rows


COL_TILE = 256


def _half_tiling(hshape, axis, mult):
    hr, hc = hshape
    if axis == 0:
        tr = _row_tile(hr, hc, mult)
        return tr, hc, hr // tr
    return hr, COL_TILE, hc // COL_TILE


def _tile_idx(axis, t):
    return (t, 0) if axis == 0 else (0, t)


def _chip_partial(place, g, t, axis, name):
    hshape = t.shape[1:]
    br, bc, nt = _half_tiling(hshape, axis, 16)

    def body(pl_ref, g_ref, t_ref, pf_ref, pb_ref):
        v = g_ref[...].astype(F32) + t_ref[...].astype(F32)
        pb_ref[...] = v.astype(BF16)

        @pl.when(pl.program_id(1) == pl_ref[0])
        def _():
            pf_ref[...] = v

    blk = (None, br, bc)
    return pl.pallas_call(
        body, name=name,
        grid_spec=pltpu.PrefetchScalarGridSpec(
            num_scalar_prefetch=1, grid=(nt, 4),
            in_specs=[pl.BlockSpec(blk, lambda i, j, p: (j,) + _tile_idx(axis, p[1] * nt + i)),
                      pl.BlockSpec(blk, lambda i, j, p: (j,) + _tile_idx(axis, i))],
            out_specs=[pl.BlockSpec((br, bc), lambda i, j, p: _tile_idx(axis, i)),
                       pl.BlockSpec(blk, lambda i, j, p: (j,) + _tile_idx(axis, i))]),
        out_shape=[jax.ShapeDtypeStruct(hshape, F32), jax.ShapeDtypeStruct((4,) + hshape, BF16)],
        compiler_params=_cp("arbitrary", "arbitrary"),
    )(place, g, t)


def _finish_half(pf, rb, axis, name):
    hshape = pf.shape
    br, bc, nt = _half_tiling(hshape, axis, 16)

    def body(pf_ref, rb_ref, o_ref):
        o_ref[...] = ((pf_ref[...] + rb_ref[0].astype(F32)) + rb_ref[1].astype(F32)) + rb_ref[2].astype(F32)

    return pl.pallas_call(
        body, name=name, grid=(nt,),
        in_specs=[pl.BlockSpec((br, bc), lambda i: _tile_idx(axis, i)),
                  pl.BlockSpec((3, br, bc), lambda i: (0,) + _tile_idx(axis, i))],
        out_specs=pl.BlockSpec((br, bc), lambda i: _tile_idx(axis, i)),
        out_shape=jax.ShapeDtypeStruct(hshape, F32),
        compiler_params=_cp("arbitrary"),
    )(pf, rb)


def _adam_math(w, g, m, v):
    m = ADAM_B1 * m + (1.0 - ADAM_B1) * g
    v = ADAM_B2 * v + (1.0 - ADAM_B2) * (g * g)
    m_hat = m / (1.0 - ADAM_B1 ** ADAM_STEP)
    v_hat = v / (1.0 - ADAM_B2 ** ADAM_STEP)
    return -ADAM_LR * (m_hat / (jnp.sqrt(v_hat) + ADAM_EPS) + ADAM_WD * w), m, v


def _adam_halves(place, w, mine, theirs, m, v, axis, name):
    br, bc, nt = _half_tiling(mine.shape, axis, 8)

    def body(pl_ref, w_ref, a_ref, b_ref, m_ref, v_ref, g_ref, d_ref, mo_ref, vo_ref):
        is_mine = pl.program_id(0) // nt == pl_ref[1]
        g = jnp.where(is_mine, a_ref[...], b_ref[...])
        d, mn, vn = _adam_math(w_ref[...], g, m_ref[...], v_ref[...])
        g_ref[...] = g
        d_ref[...] = d
        mo_ref[...] = mn
        vo_ref[...] = vn

    full = pl.BlockSpec((br, bc), lambda i, p: _tile_idx(axis, i))
    mine_spec = pl.BlockSpec((br, bc), lambda i, p: _tile_idx(axis, jnp.where(i // nt == p[1], i % nt, nt - 1)))
    theirs_spec = pl.BlockSpec((br, bc), lambda i, p: _tile_idx(axis, jnp.where(i // nt == p[1], 0, i % nt)))
    return pl.pallas_call(
        body, name=name,
        grid_spec=pltpu.PrefetchScalarGridSpec(
            num_scalar_prefetch=1, grid=(2 * nt,), in_specs=[full, mine_spec, theirs_spec, full, full],
            out_specs=[full] * 4),
        out_shape=[jax.ShapeDtypeStruct(w.shape, F32)] * 4, compiler_params=_cp("arbitrary"),
    )(place, w, mine, theirs, m, v)


def _add_many(xs, ys, name):
    n = len(xs)

    def body(*refs):
        for i in range(n):
            refs[2 * n + i][...] = refs[i][...] + refs[n + i][...]

    return pl.pallas_call(body, name=name, out_shape=[jax.ShapeDtypeStruct(a.shape, F32) for a in xs])(*xs, *ys)


def _adam_small(place, owns, landed, ws, ms, vs, widths):
    n, nw = len(owns), len(ws)

    def body(pl_ref, *refs):
        own_r, land_r = refs[:n], refs[n:2 * n]
        w_r, m_r, v_r = (refs[2 * n + k * nw:2 * n + (k + 1) * nw] for k in range(3))
        outs = refs[2 * n + 3 * nw:]
        g_o, d_o, m_o, v_o = outs[:n], outs[n:n + nw], outs[n + nw:n + 2 * nw], outs[n + 2 * nw:]
        for me in range(4):
            @pl.when(pl_ref[0] == me)
            def _(me=me):
                for i in range(n):
                    p = [own_r[i][...] if k == me else land_r[i][k] for k in range(4)]
                    g = ((p[0] + p[1]) + p[2]) + p[3]
                    if i < nw and widths[i]:
                        g = g[:, me * widths[i]:(me + 1) * widths[i]]
                    g_o[i][...] = g
                    if i < nw:
                        d, mn, vn = _adam_math(w_r[i][...], g, m_r[i][...], v_r[i][...])
                        d_o[i][...] = d
                        m_o[i][...] = mn
                        v_o[i][...] = vn

    g_shapes = [jax.ShapeDtypeStruct(ws[i].shape if i < nw else owns[i].shape, F32) for i in range(n)]
    w_shapes = [jax.ShapeDtypeStruct(w.shape, F32) for w in ws]
    whole = lambda a: pl.BlockSpec(a.shape, lambda i, p, nd=len(a.shape): (0,) * nd)
    ins = list(owns) + list(landed) + list(ws) + list(ms) + list(vs)
    out_shape = g_shapes + w_shapes * 3
    out = pl.pallas_call(
        body, name="adam_small",
        grid_spec=pltpu.PrefetchScalarGridSpec(num_scalar_prefetch=1, grid=(1,), in_specs=[whole(a) for a in ins],
                                               out_specs=[whole(a) for a in out_shape]),
        out_shape=out_shape, compiler_params=_cp("arbitrary"),
    )(place, *ins)
    return out[:n], out[n:n + nw], out[n + nw:n + 2 * nw], out[n + 2 * nw:]


def kernel(x, g_mix, w_in, b_gate, w_gk_up, b_gk, w_pool_grp, pool_scale, g_gla_head, w_pool_proj, w_gla_proj, w_out, g_ffn, w_up, w_conv, b_conv, w_down, g_final, loss_target, m_g_mix, m_w_in, m_b_gate, m_w_gk_up, m_b_gk, m_w_pool_grp, m_pool_scale, m_g_gla_head, m_w_pool_proj, m_w_gla_proj, m_w_out, m_g_ffn, m_w_up, m_w_conv, m_b_conv, m_w_down, m_g_final, v_g_mix, v_w_in, v_b_gate, v_w_gk_up, v_b_gk, v_w_pool_grp, v_pool_scale, v_g_gla_head, v_w_pool_proj, v_w_gla_proj, v_w_out, v_g_ffn, v_w_up, v_w_conv, v_b_conv, v_w_down, v_g_final):
    s = x.shape[1]
    ts = min(s, 512)
    tm = min(s, 256)
    cx, cy, cc = lax.axis_index("x"), lax.axis_index("y"), lax.axis_index("c")
    chip = 2 * cx + cy
    place = jnp.stack([chip, cc]).astype(jnp.int32)

    big_names = ("w_in", "w_pool_proj", "w_gla_proj", "w_out", "w_up", "w_down")
    axes = (1, 0, 0, 0, 0, 0)
    shards = dict(w_in=jnp.transpose(w_in[0]), w_pool_proj=w_pool_proj[0], w_gla_proj=w_gla_proj[0], w_out=w_out[0],
                  w_up=w_up[0], w_down=w_down[0])
    def fill_own(lands, mine):
        return [lax.dynamic_update_slice(g, o_[None], (chip, 0, 0)) for g, o_ in zip(lands, mine)]

    def gather_start(tag, halves, group_axes, whole, after):
        plan = _gather_plan([o_.shape for o_ in halves], group_axes, len(whole))
        srcs = list(halves) + list(whole)
        handle, token = _split_start("gather_" + tag + "_start", srcs, [((4,) + o_.shape, o_.dtype) for o_ in srcs], plan,
                                     3 * len(srcs), after)
        return (handle, plan, len(halves), len(srcs), group_axes), token

    def gather_finish(tag, started, after):
        handle, plan, n_halves, n, group_axes = started
        mine, lands = _split_wait("gather_" + tag + "_wait", handle, n, plan, after)
        lands[:n_halves] = _gather_share(lands[:n_halves], group_axes, "gather_" + tag + "_share")
        return fill_own(lands, mine)

    in_w, tok = gather_start("in", [jnp.transpose(w_in[0].astype(BF16))], axes[:1], [], g_mix)
    zero = tok[0, 0]
    own = [(shards[n] + zero).astype(BF16) for n in big_names[1:]]
    mix_w, tok = gather_start("mix", own[0:3], axes[1:4], [w_gk_up[0] + zero, w_conv[0] + zero], tok)
    up_w, tok = gather_start("up", own[3:4], axes[4:5], [], tok)
    down_w, tok = gather_start("down", own[4:5], axes[5:6], [], tok)

    def forward_start(tag, started, after):
        handle, plan, _, n, group_axes = started
        mine, lands = _split_wait("gather_" + tag + "_wait", handle, n, plan, after)
        plan = _share_plan([o_.shape for o_ in mine], group_axes)
        share, token = _split_start("gather_" + tag + "_share_start", [], lands, plan, 3 * n, after)
        return (share, plan, mine), token

    def forward_done(tag, forwarded, after):
        share, plan, mine = forwarded
        return fill_own(_split_wait("gather_" + tag + "_share_wait", share, 0, plan, after)[1], mine)
    xs, tgt = x[0], loss_target[0]
    wgrp = w_pool_grp[0]
    h = _rmsnorm(xs, g_mix, tok, "norm_mix", ts)
    m_in_t, v_in_t = jnp.transpose(m_w_in[0]), jnp.transpose(v_w_in[0])
    h, m_in_t, v_in_t = lax.optimization_barrier((h, m_in_t, v_in_t))
    w_in_t = gather_finish("in", in_w, h)[0].reshape(N_IN, D)
    nsh = N_IN // 4

    zr = _in_proj(h, w_in_t, PROJ_TILE)
    p, pp = _pool_fwd(zr, wgrp, pool_scale)
    wpp, wgla, wout, wgk4, wconv4 = gather_finish("mix", mix_w, pp)
    wgla, wout = wgla.reshape(D, D), wout.reshape(D, D)
    wgk_full = jnp.transpose(wgk4, (1, 0, 2)).reshape(GATE_RANK, 512)
    wconv_full = jnp.transpose(wconv4, (1, 0, 2)).reshape(3, N_UP)
    wgk_pad = jnp.concatenate([wgk_full, jnp.zeros((128 - GATE_RANK, 512), F32)], axis=0)
    o, og, sp = _gla_fwd(zr, wgk_pad, b_gk, g_gla_head, ts)
    up_f, tok = forward_start("up", up_w, og)
    x1, mixed, yp, yg, h2 = _merge_fwd(xs, zr, pp, og, b_gate, wpp, wgla, wout, g_ffn, tok, ts)
    wup, = forward_done("up", up_f, x1)
    down_f, tok = forward_start("down", down_w, x1)
    u = _matmul_resident(h2, wup, tok, "ffn_up")
    wdown = forward_done("down", down_f, u)[0].reshape(D_FF, D)
    a, conv_out, dx2, dx2b, loss_part, dgfin = _ffn_down_loss(u, x1, tgt, wconv_full, b_conv, wdown,
                                                              g_final.reshape(1, D), tm)

    du, dbconv, dwconv = _ffn_bwd(dx2b, u, conv_out, wconv_full, wdown, tm)
    dw_down = _matmul_tn(a, dx2b, "dw_down", D, tm=D_FF // 2)
    dw_up = _matmul_tn(h2, du, "dw_up", UP_SHARD, shard_major=True)

    def exchange_start(tag, grads, group_axes, after):
        plan = _sibling_plan([g.shape[1:] for g in grads], group_axes)
        lands = [((4,) + _half_shape(g.shape[1:], ax), g.dtype) for g, ax in zip(grads, group_axes)]
        handle, token = _split_start("sibling_" + tag + "_start", grads, lands, plan, len(grads), after)
        return (handle, plan, len(grads)), token

    def partials(tag, names, group_axes, exchange, after):
        handle, plan, n = exchange
        mine, theirs = _split_wait("sibling_" + tag + "_wait", handle, n, plan, after)
        return zip(*[_chip_partial(place, g, t, ax, "chip_partial_" + nm)
                     for nm, ax, g, t in zip(names, group_axes, mine, theirs)])

    ffn_names, ffn_axes = ("w_up", "w_down"), (0, 0)
    ffn_x, token = exchange_start("ffn", [dw_up, dw_down.reshape(4, 704, D)], ffn_axes, du)
    dx1, dx1b, dgffn = _matmul_nt_normbwd(du, wup, x1, g_ffn, dx2, token, "ffn_up_bwd", ts)
    ffn_pf, ffn_pb = partials("ffn", ffn_names, ffn_axes, ffn_x, dx1b)
    ffn_plan = _reduce_plan(2, 0)
    ffn_handle, token = _split_start("reduce_ffn_start", ffn_pb, [((3,) + p.shape[1:], BF16) for p in ffn_pb],
                                     ffn_plan, 6, ffn_pf[0])

    dzg, dyp, dyg, dpp, do, dzog, dbgate, dghead = _merge_bwd(dx1b, zr, yp, yg, o, b_gate, g_gla_head, wpp, wgla, wout,
                                                             token, ts)
    dw_out = _matmul_tn(mixed, dx1b, "dw_out", D, tm=512)
    dw_gla = _matmul_tn(og, dyg, "dw_gla", D, tm=512)
    dw_pp = _matmul_tn(pp, dyp, "dw_pp", 256, shard_major=True)

    out_names, out_axes = ("w_pool_proj", "w_gla_proj", "w_out"), (0, 0, 0)
    out_x, token = exchange_start("out", [dw_pp, dw_gla.reshape(4, 256, D), dw_out.reshape(4, 256, D)], out_axes, dpp)
    dzp, dwgrp, dscale = _pool_bwd(p, dpp, wgrp, pool_scale, token)
    out_pf, out_pb = partials("out", out_names, out_axes, out_x, dzp)
    out_plan = _reduce_plan(3, 0)
    out_handle, token = _split_start("reduce_out_start", out_pb, [((3,) + p_.shape[1:], BF16) for p_ in out_pb],
                                     out_plan, 9, out_pf[0])
    dq, dk, dv, dgpre = _gla_bwd(zr, do, sp, wgk_pad, b_gk, token, ts)
    dzgk, dwgk, dbgk = _gk_bwd(dgpre, zr, wgk_pad, dgpre, ts)
    dzr = jnp.concatenate([dzg, dv, dzog, dzp, dq, dk, dzgk], axis=1)
    dw_rt = _matmul_tn(dzr, h, "dw_in", D, tm=PROJ_TILE)

    def grad_rows(lo, hi):
        out = []
        for seg_lo, seg_hi, at in ((0, 1536, OFF_POOL), (1536, 3584, OFF_V), (3584, 3600, OFF_GK), (3600, N_IN, OFF_GATE)):
            a_, b_ = max(lo, seg_lo), min(hi, seg_hi)
            if a_ < b_:
                out.append(dw_rt[at + a_ - seg_lo:at + b_ - seg_lo])
        return jnp.concatenate(out, axis=0)

    dw_in_t = jnp.stack([grad_rows(j * nsh, (j + 1) * nsh) for j in range(4)])

    in_sib = _sibling_exchange([dw_in_t], (1,), [], "sibling_exchange_in")
    in_pf, in_pb = _chip_partial(place, dw_in_t, in_sib[0], 1, "chip_partial_w_in")
    in_plan = _reduce_plan(1, 0)
    in_handle, token = _split_start("reduce_in_start", [in_pb], [((3,) + in_pb.shape[1:], BF16)], in_plan, 3, in_pf)
    grad_x, _, dgmix = _matmul_nt_normbwd(dzr, w_in_t, xs, g_mix, dx1, token, "in_proj_bwd", ts, transposed=True)
    small_names = ("g_mix", "b_gate", "w_gk_up", "b_gk", "w_pool_grp", "pool_scale", "g_gla_head", "g_ffn", "w_conv",
                   "b_conv", "g_final")
    small_mine = [dgmix, dbgate, dwgk[:GATE_RANK], dbgk, dwgrp.reshape(4 * 128, 128), dscale, dghead, dgffn, dwconv, dbconv,
                  dgfin, loss_part]
    small_sib = _sibling_exchange([], (), small_mine, "sibling_exchange_small")
    small_chip = _add_many(small_mine, small_sib, "chip_partial_small")
    small_plan = _reduce_plan(0, len(small_chip))
    small_handle, token = _split_start("reduce_small_start", small_chip, [((4,) + a_.shape, F32) for a_ in small_chip],
                                       small_plan, 3 * len(small_chip), small_mine[0])

    ms = dict(w_in=m_in_t, w_pool_proj=m_w_pool_proj[0], w_gla_proj=m_w_gla_proj[0], w_out=m_w_out[0],
              w_up=m_w_up[0], w_down=m_w_down[0])
    vs = dict(w_in=v_in_t, w_pool_proj=v_w_pool_proj[0], w_gla_proj=v_w_gla_proj[0], w_out=v_w_out[0],
              w_up=v_w_up[0], w_down=v_w_down[0])
    grad, delta, new_m, new_v = {}, {}, {}, {}

    def finish_and_update(names, group_axes, part_f, landed, tag):
        halves = [_finish_half(pf, rb, ax, "finish_" + n) for n, ax, pf, rb in zip(names, group_axes, part_f, landed)]
        sib_halves = _sibling_share(halves, "sibling_share_" + tag)
        for n, ax, mine, theirs in zip(names, group_axes, halves, sib_halves):
            res = _adam_halves(place, shards[n], mine, theirs, ms[n], vs[n], ax, "adam_" + n)
            if n == "w_in":
                res = [jnp.transpose(r_) for r_ in res]
            grad[n], delta[n], new_m[n], new_v[n] = [r_[None] for r_ in res]

    _, ffn_landed = _split_wait("reduce_ffn_wait", ffn_handle, 2, ffn_plan, token)
    _, out_landed = _split_wait("reduce_out_wait", out_handle, 3, out_plan, ffn_landed[0])
    finish_and_update(ffn_names + out_names, ffn_axes + out_axes, ffn_pf + out_pf, ffn_landed + out_landed, "rest")
    _, in_landed = _split_wait("reduce_in_wait", in_handle, 1, in_plan, delta["w_out"])
    finish_and_update(("w_in",), (1,), (in_pf,), in_landed, "in")
    small_sent, small_landed = _split_wait("reduce_small_wait", small_handle, len(small_chip), small_plan, delta["w_in"])
    given = dict(g_mix=(g_mix, m_g_mix, v_g_mix), b_gate=(b_gate, m_b_gate, v_b_gate), w_gk_up=(w_gk_up, m_w_gk_up, v_w_gk_up),
                 b_gk=(b_gk, m_b_gk, v_b_gk), w_pool_grp=(w_pool_grp, m_w_pool_grp, v_w_pool_grp),
                 pool_scale=(pool_scale, m_pool_scale, v_pool_scale), g_gla_head=(g_gla_head, m_g_gla_head, v_g_gla_head),
                 g_ffn=(g_ffn, m_g_ffn, v_g_ffn), w_conv=(w_conv, m_w_conv, v_w_conv), b_conv=(b_conv, m_b_conv, v_b_conv),
                 g_final=(g_final, m_g_final, v_g_final))
    flat2 = lambda a: a.reshape(-1, a.shape[-1])
    widths = [dict(w_gk_up=HK, w_conv=UP_SHARD).get(n) for n in small_names]
    totals, ds, mo, vo = _adam_small(place, small_sent, small_landed, *[[flat2(given[n][k]) for n in small_names] for k in range(3)],
                                     widths)
    loss = totals[-1][0, 0]
    for i, n in enumerate(small_names):
        shp = given[n][0].shape
        grad[n], delta[n], new_m[n], new_v[n] = [r_.reshape(shp) for r_ in (totals[i], ds[i], mo[i], vo[i])]

    order = ("g_mix", "w_in", "b_gate", "w_gk_up", "b_gk", "w_pool_grp", "pool_scale", "g_gla_head", "w_pool_proj",
             "w_gla_proj", "w_out", "g_ffn", "w_up", "w_conv", "b_conv", "w_down", "g_final")
    return (loss, grad_x[None], *[grad[n] for n in order], *[delta[n] for n in order], *[new_m[n] for n in order],
            *[new_v[n] for n in order])
```

```python
import jax
import jax.numpy as jnp
from jax import lax
from jax.experimental import pallas as pl
from jax.experimental.pallas import tpu as pltpu

F32 = jnp.float32
BF16 = jnp.bfloat16
MESH = pl.DeviceIdType.MESH

D = 1024
EPS = 1e-6
CHUNK = 64
POOL_W = 512
POOL_WINDOWS = (2, 4, 8, 16)
HEADS = 4
HK = 128
HV = 256
GATE_RANK = 16
D_FF = 2816
N_UP = 2 * D_FF
N_IN = 5648
QSCALE = HK ** -0.5
N_INR = 5760
OFF_GATE, OFF_V, OFF_OG, OFF_POOL, OFF_Q, OFF_K, OFF_GK = 0, 2048, 3072, 4096, 4608, 5120, 5632

ADAM_LR, ADAM_B1, ADAM_B2, ADAM_EPS, ADAM_WD, ADAM_STEP = 0.001, 0.9, 0.999, 1e-08, 0.01, 10

VMEM_LIMIT = 56 * 1024 * 1024
PROJ_TILE = N_INR // 5
UP_SHARD = N_UP // 4


def _cp(*sem):
    return pltpu.CompilerParams(dimension_semantics=sem if sem else None, vmem_limit_bytes=VMEM_LIMIT)


def _dot(a, b):
    return jnp.dot(a, b, preferred_element_type=F32)


def _dot_nt(a, b):
    return lax.dot_general(a, b, (((1,), (1,)), ((), ())), preferred_element_type=F32)


def _dot_tn(a, b):
    return lax.dot_general(a, b, (((0,), (0,)), ((), ())), preferred_element_type=F32)


def _sigmoid(v):
    return 1.0 / (1.0 + jnp.exp(-v))


def _rows(shape):
    return lax.broadcasted_iota(jnp.int32, shape, 0)


def _pick_row(v, r):
    return jnp.sum(jnp.where(_rows(v.shape) == r, v, 0.0), axis=0, keepdims=True)


def _rmsnorm(x, g, after, name, ts):
    s = x.shape[0]

    def body(x_ref, g_ref, after_ref, h_ref):
        xv = x_ref[...]
        r = lax.rsqrt(jnp.mean(xv * xv, axis=-1, keepdims=True) + EPS)
        h_ref[...] = (xv * r * g_ref[...]).astype(BF16)

    return pl.pallas_call(
        body, name=name, grid=(s // ts,),
        in_specs=[pl.BlockSpec((ts, D), lambda i: (i, 0)), pl.BlockSpec((1, D), lambda i: (0, 0)), ANY],
        out_specs=pl.BlockSpec((ts, D), lambda i: (i, 0)), out_shape=jax.ShapeDtypeStruct((s, D), BF16),
        compiler_params=_cp("arbitrary"),
    )(x, g, after)


MM_ROWS = 512


def _matmul_resident(h, w, after, name):
    s = h.shape[0]
    nj, tn = w.shape[0], w.shape[2]
    rc = min(s, MM_ROWS)

    def body(h_ref, w_ref, after_ref, z_ref):
        for r0 in range(0, s, rc):
            z_ref[r0:r0 + rc, :] = _dot(h_ref[r0:r0 + rc, :], w_ref[...]).astype(BF16)

    return pl.pallas_call(
        body, name=name, grid=(nj,),
        in_specs=[pl.BlockSpec((s, D), lambda j: (0, 0)), pl.BlockSpec((None, D, tn), lambda j: (j, 0, 0)), ANY],
        out_specs=pl.BlockSpec((s, tn), lambda j: (0, j)), out_shape=jax.ShapeDtypeStruct((s, nj * tn), BF16),
        compiler_params=_cp("arbitrary"),
    )(h, w, after)


PROJ_PIECES = ((3600, 2048, OFF_GATE), (1536, 2048, OFF_V), (0, 1536, OFF_POOL), (3584, GATE_RANK, OFF_GK))


def _projection_copies(w_hbm, w_ref, sems):
    return [pltpu.make_async_copy(w_hbm.at[pl.ds(src, n)], w_ref.at[pl.ds(dst, n)], sems.at[i])
            for i, (src, n, dst) in enumerate(PROJ_PIECES)]


def _load_projection(w_hbm, w_ref, sems):
    cps = _projection_copies(w_hbm, w_ref, sems)
    for cp in cps:
        cp.start()
    w_ref[OFF_GK + GATE_RANK:, :] = jnp.zeros((N_INR - OFF_GK - GATE_RANK, D), BF16)
    for cp in cps:
        cp.wait()


def _in_proj(h, w_nat, tn):
    s = h.shape[0]
    rc = min(s, MM_ROWS)
    nj = N_INR // tn
    first_use = [dst // tn for _, _, dst in PROJ_PIECES]

    def body(h_ref, w_hbm, z_ref, w_ref, sems):
        j = pl.program_id(0)
        cps = _projection_copies(w_hbm, w_ref, sems)

        @pl.when(j == 0)
        def _():
            for cp in cps:
                cp.start()
            w_ref[OFF_GK + GATE_RANK:, :] = jnp.zeros((N_INR - OFF_GK - GATE_RANK, D), BF16)

        for step in range(nj):
            due = [cp for cp, at in zip(cps, first_use) if at == step]
            if due:
                @pl.when(j == step)
                def _(due=due):
                    for cp in due:
                        cp.wait()

        wt = w_ref[pl.ds(pl.multiple_of(j * tn, 128), tn), :]
        for r0 in range(0, s, rc):
            z_ref[r0:r0 + rc, :] = _dot_nt(h_ref[r0:r0 + rc, :], wt).astype(BF16)

    return pl.pallas_call(
        body, name="in_proj", grid=(nj,),
        in_specs=[pl.BlockSpec((s, D), lambda j: (0, 0)), ANY],
        out_specs=pl.BlockSpec((s, tn), lambda j: (0, j)), out_shape=jax.ShapeDtypeStruct((s, N_INR), BF16),
        scratch_shapes=[pltpu.VMEM((N_INR, D), BF16), pltpu.SemaphoreType.DMA((len(PROJ_PIECES),))],
        compiler_params=_cp("arbitrary"),
    )(h, w_nat)


def _matmul_nt_normbwd(dz, w, x, g, resid, after, name, ts, transposed=False):
    s = x.shape[0]
    w_vmem = (N_INR, D) if transposed else (D, w.shape[0] * w.shape[2])
    n_sems = len(PROJ_PIECES) if transposed else w.shape[0]

    def body(dz_ref, w_hbm, x_ref, g_ref, r_ref, after_ref, o_ref, ob_ref, dg_ref, w_ref, sems):
        @pl.when(pl.program_id(0) == 0)
        def _():
            if transposed:
                _load_projection(w_hbm, w_ref, sems)
            else:
                kc = w.shape[2]
                cps = [pltpu.make_async_copy(w_hbm.at[j], w_ref.at[:, pl.ds(j * kc, kc)], sems.at[j])
                       for j in range(w.shape[0])]
                for cp in cps:
                    cp.start()
                for cp in cps:
                    cp.wait()
            dg_ref[...] = jnp.zeros_like(dg_ref)

        dh = _dot(dz_ref[...], w_ref[...]) if transposed else _dot_nt(dz_ref[...], w_ref[...])
        xv = x_ref[...]
        r = lax.rsqrt(jnp.mean(xv * xv, axis=-1, keepdims=True) + EPS)
        xh = xv * r
        dg_ref[...] += jnp.sum(dh * xh, axis=0, keepdims=True)
        dxh = dh * g_ref[...]
        out = r_ref[...] + r * (dxh - xh * jnp.mean(dxh * xh, axis=-1, keepdims=True))
        o_ref[...] = out
        ob_ref[...] = out.astype(BF16)

    row = lambda i: (i, 0)
    kdim = dz.shape[1]
    return pl.pallas_call(
        body, name=name, grid=(s // ts,),
        in_specs=[pl.BlockSpec((ts, kdim), row), ANY, pl.BlockSpec((ts, D), row),
                  pl.BlockSpec((1, D), lambda i: (0, 0)), pl.BlockSpec((ts, D), row), ANY],
        out_specs=[pl.BlockSpec((ts, D), row), pl.BlockSpec((ts, D), row), pl.BlockSpec((1, D), lambda i: (0, 0))],
        out_shape=[jax.ShapeDtypeStruct((s, D), F32), jax.ShapeDtypeStruct((s, D), BF16),
                   jax.ShapeDtypeStruct((1, D), F32)],
        scratch_shapes=[pltpu.VMEM(w_vmem, BF16), pltpu.SemaphoreType.DMA((n_sems,))],
        compiler_params=_cp("arbitrary"),
    )(dz, w, x, g, resid, after)


def _matmul_tn(a, b, name, tn, shard_major=False, tm=None):
    s, m = a.shape
    n = b.shape[1]
    tm = m if tm is None else tm
    ni, nj = m // tm, n // tn

    def body(a_ref, b_ref, o_ref):
        o_ref[...] = _dot_tn(a_ref[...], b_ref[...]).astype(BF16)

    if shard_major:
        out_spec = pl.BlockSpec((None, tm, tn), lambda i, j: (j, i, 0))
        out_shape = jax.ShapeDtypeStruct((nj, m, tn), BF16)
    else:
        out_spec = pl.BlockSpec((tm, tn), lambda i, j: (i, j))
        out_shape = jax.ShapeDtypeStruct((m, n), BF16)
    return pl.pallas_call(
        body, name=name, grid=(ni, nj),
        in_specs=[pl.BlockSpec((s, tm), lambda i, j: (0, i)), pl.BlockSpec((s, tn), lambda i, j: (0, j))],
        out_specs=out_spec, out_shape=out_shape,
        compiler_params=_cp("arbitrary", "arbitrary"),
    )(a, b)


def _pool_fwd(zr, wgrp, scale):
    s = zr.shape[0]

    def body(u_ref, w_ref, sc_ref, p_ref, pp_ref):
        row = _rows((s, 128))
        for gi, win in enumerate(POOL_WINDOWS):
            cs = slice(gi * 128, (gi + 1) * 128)
            u = u_ref[:, cs].astype(F32)
            acc, k = u, 1
            while k < win:
                acc = acc + jnp.where(row >= k, pltpu.roll(acc, k, 0), 0.0)
                k *= 2
            cnt = jnp.minimum(row + 1, win).astype(F32)
            p = (acc / cnt - u).astype(BF16)
            p_ref[:, cs] = p
            pp_ref[:, cs] = (_dot(p, w_ref[gi].astype(BF16)) * sc_ref[:, cs]).astype(BF16)

    return pl.pallas_call(
        body, name="pool_fwd", grid=(1,),
        in_specs=[pl.BlockSpec((s, POOL_W), lambda i: (0, OFF_POOL // POOL_W)),
                  pl.BlockSpec((4, 128, 128), lambda i: (0, 0, 0)), pl.BlockSpec((1, POOL_W), lambda i: (0, 0))],
        out_specs=[pl.BlockSpec((s, POOL_W), lambda i: (0, 0))] * 2,
        out_shape=[jax.ShapeDtypeStruct((s, POOL_W), BF16)] * 2,
        compiler_params=_cp("arbitrary"),
    )(zr, wgrp, scale)


def _pool_bwd(p, dpp, wgrp, scale, after, dz):
    s = p.shape[0]

    def body(p_ref, dpp_ref, w_ref, sc_ref, after_ref, dz_in, dz_ref, dw_ref, dsc_ref):
        row = _rows((s, 128))
        for gi, win in enumerate(POOL_WINDOWS):
            cs = slice(gi * 128, (gi + 1) * 128)
            pv = p_ref[:, cs]
            wb = w_ref[gi].astype(BF16)
            dpp_v = dpp_ref[:, cs].astype(F32)
            dsc_ref[:, cs] = jnp.sum(dpp_v * _dot(pv, wb), axis=0, keepdims=True)
            dpm = (dpp_v * sc_ref[:, cs]).astype(BF16)
            dw_ref[gi] = _dot_tn(pv, dpm)
            dp = _dot_nt(dpm, wb)
            cnt = jnp.minimum(row + 1, win).astype(F32)
            acc, k = dp / cnt, 1
            while k < win:
                acc = acc + jnp.where(row < s - k, pltpu.roll(acc, s - k, 0), 0.0)
                k *= 2
            dz_ref[:, cs] = (acc - dp).astype(BF16)

    full = lambda i: (0, 0)
    return pl.pallas_call(
        body, name="pool_bwd", grid=(1,),
        in_specs=[pl.BlockSpec((s, POOL_W), full), pl.BlockSpec((s, POOL_W), full),
                  pl.BlockSpec((4, 128, 128), lambda i: (0, 0, 0)), pl.BlockSpec((1, POOL_W), full), ANY, ANY],
        out_specs=[pl.BlockSpec((s, POOL_W), lambda i: (0, OFF_POOL // POOL_W)),
                   pl.BlockSpec((4, 128, 128), lambda i: (0, 0, 0)), pl.BlockSpec((1, POOL_W), full)],
        out_shape=[jax.ShapeDtypeStruct(dz.shape, BF16), jax.ShapeDtypeStruct((4, 128, 128), F32),
                   jax.ShapeDtypeStruct((1, POOL_W), F32)],
        input_output_aliases={5: 0},
        compiler_params=_cp("arbitrary"),
    )(p, dpp, wgrp, scale, after, dz)


def _gla_decay(zgk_ref, wgk_ref, bgk_ref, rb):
    g = _dot(zgk_ref[...], wgk_ref[...].astype(BF16)) + bgk_ref[...]
    la = (jnp.minimum(g, 0.0) - jnp.log(1.0 + jnp.exp(-jnp.abs(g)))) * (1.0 / 16.0)
    rowm = _rows(la.shape) & (CHUNK - 1)
    bc, k = la, 1
    while k < CHUNK:
        bc = bc + jnp.where(rowm >= k, pltpu.roll(bc, k, 0), 0.0)
        k *= 2
    return g, jnp.exp(bc), jnp.exp(-bc)


GLA_HB = 4


def _gla_specs(rb, rmap):
    wk, wv = GLA_HB * HK, GLA_HB * HV
    return [pl.BlockSpec((rb, wk), lambda h, r: (rmap(h, r), OFF_Q // wk + h)),
            pl.BlockSpec((rb, wk), lambda h, r: (rmap(h, r), OFF_K // wk + h)),
            pl.BlockSpec((rb, wv), lambda h, r: (rmap(h, r), OFF_V // wv + h)),
            pl.BlockSpec((rb, 128), lambda h, r: (rmap(h, r), OFF_GK // 128))]


def _gla_fwd(zr, wgk, bgk, ghead, rb):
    s = zr.shape[0]
    nc = rb // CHUNK
    wk, wv = GLA_HB * HK, GLA_HB * HV

    def body(q_ref, k_ref, v_ref, zgk_ref, zog_ref, wgk_ref, bgk_ref, gh_ref, o_ref, og_ref, sp_ref, st_ref, kv_ref):
        @pl.when(pl.program_id(1) == 0)
        def _():
            st_ref[...] = jnp.zeros_like(st_ref)

        _, e_pos, e_neg = _gla_decay(zgk_ref, wgk_ref, bgk_ref, rb)
        lower = _rows((CHUNK, CHUNK)) >= lax.broadcasted_iota(jnp.int32, (CHUNK, CHUNK), 1)
        pairs = [(c, hh) for c in range(nc) for hh in range(GLA_HB)]
        rows = lambda c: slice(c * CHUNK, (c + 1) * CHUNK)
        cols_k = lambda hh: slice(hh * HK, (hh + 1) * HK)
        cols_v = lambda hh: slice(hh * HV, (hh + 1) * HV)
        qfws, pms, e_lasts = {}, {}, {}
        for c, hh in pairs:
            q = q_ref[rows(c), cols_k(hh)].astype(F32) * QSCALE
            k = k_ref[rows(c), cols_k(hh)].astype(F32)
            ec, fc = e_pos[rows(c), cols_k(hh)], e_neg[rows(c), cols_k(hh)]
            qfw = (q * ec).astype(BF16)
            kfw_f = k * fc
            s_fw = _dot_nt(qfw, kfw_f.astype(BF16))
            s_bw = _dot_nt((q * fc).astype(BF16), (k * ec).astype(BF16))
            e_last = _pick_row(ec, CHUNK - 1)
            kv_ref[c, hh] = _dot_tn(v_ref[rows(c), cols_v(hh)], (kfw_f * e_last).astype(BF16))
            qfws[c, hh], pms[c, hh], e_lasts[c, hh] = qfw, jnp.where(lower, s_fw, s_bw).astype(BF16), e_last
        for hh in range(GLA_HB):
            st = st_ref[hh]
            for c in range(nc):
                sp_ref[c, hh] = st.astype(BF16)
                st = st * e_lasts[c, hh] + kv_ref[c, hh]
            st_ref[hh] = st
        for c, hh in pairs:
            o = _dot(pms[c, hh], v_ref[rows(c), cols_v(hh)]) + _dot_nt(qfws[c, hh], sp_ref[c, hh])
            r = lax.rsqrt(jnp.mean(o * o, axis=-1, keepdims=True) + EPS)
            zo = zog_ref[rows(c), cols_v(hh)].astype(F32)
            o_ref[rows(c), cols_v(hh)] = o.astype(BF16)
            og_ref[rows(c), cols_v(hh)] = (o * r * gh_ref[...] * zo * _sigmoid(zo)).astype(BF16)

    rmap = lambda h, r: r
    return pl.pallas_call(
        body, name="gla_fwd", grid=(HEADS // GLA_HB, s // rb),
        in_specs=_gla_specs(rb, rmap) + [
            pl.BlockSpec((rb, wv), lambda h, r: (r, OFF_OG // wv + h)),
            pl.BlockSpec((128, wk), lambda h, r: (0, h)), pl.BlockSpec((1, wk), lambda h, r: (0, h)),
            pl.BlockSpec((1, HV), lambda h, r: (0, 0))],
        out_specs=[pl.BlockSpec((rb, wv), lambda h, r: (r, h)), pl.BlockSpec((rb, wv), lambda h, r: (r, h)),
                   pl.BlockSpec((nc, GLA_HB, HV, HK), lambda h, r: (r, h, 0, 0))],
        out_shape=[jax.ShapeDtypeStruct((s, D), BF16), jax.ShapeDtypeStruct((s, D), BF16),
                   jax.ShapeDtypeStruct((s // CHUNK, HEADS, HV, HK), BF16)],
        scratch_shapes=[pltpu.VMEM((GLA_HB, HV, HK), F32), pltpu.VMEM((nc, GLA_HB, HV, HK), F32)],
        compiler_params=_cp("arbitrary", "arbitrary"),
    )(zr, zr, zr, zr, zr, wgk, bgk, ghead)


def _gla_bwd(zr, do, sp, wgk, bgk, after, dz, rb):
    s = zr.shape[0]
    nc = rb // CHUNK
    nr = s // rb
    wk, wv = GLA_HB * HK, GLA_HB * HV

    def body(q_ref, k_ref, v_ref, zgk_ref, do_ref, sp_ref, wgk_ref, bgk_ref, after_ref, dz_in, dq_ref, dk_ref, dv_ref,
             dg_ref, gt_ref, dbc_ref, gs_ref):
        @pl.when(pl.program_id(1) == 0)
        def _():
            gt_ref[...] = jnp.zeros_like(gt_ref)

        g, e_pos, e_neg = _gla_decay(zgk_ref, wgk_ref, bgk_ref, rb)
        lower = _rows((CHUNK, CHUNK)) >= lax.broadcasted_iota(jnp.int32, (CHUNK, CHUNK), 1)
        is_last = _rows((CHUNK, HK)) == CHUNK - 1
        pairs = [(c, hh) for c in range(nc) for hh in range(GLA_HB)]
        rows = lambda c: slice(c * CHUNK, (c + 1) * CHUNK)
        cols_k = lambda hh: slice(hh * HK, (hh + 1) * HK)
        cols_v = lambda hh: slice(hh * HV, (hh + 1) * HV)
        e_lasts = {}
        for c, hh in pairs:
            ec = e_pos[rows(c), cols_k(hh)]
            qfw = (q_ref[rows(c), cols_k(hh)].astype(F32) * QSCALE * ec).astype(BF16)
            gs_ref[c, hh] = _dot_tn(do_ref[rows(c), cols_v(hh)], qfw)
            e_lasts[c, hh] = _pick_row(ec, CHUNK - 1)
        for hh in range(GLA_HB):
            gt = gt_ref[hh]
            for c in reversed(range(nc)):
                own = gs_ref[c, hh]
                gs_ref[c, hh] = gt
                gt = own + gt * e_lasts[c, hh]
            gt_ref[hh] = gt
        def decayed(c, hh):
            q = q_ref[rows(c), cols_k(hh)].astype(F32) * QSCALE
            k = k_ref[rows(c), cols_k(hh)].astype(F32)
            ec, fc = e_pos[rows(c), cols_k(hh)], e_neg[rows(c), cols_k(hh)]
            return ec, fc, q * ec, k * fc, q * fc, k * ec

        pms, dss = {}, {}
        for c, hh in pairs:
            _, _, qfw_f, kfw_f, qbw_f, kbw_f = decayed(c, hh)
            s_fw = _dot_nt(qfw_f.astype(BF16), kfw_f.astype(BF16))
            s_bw = _dot_nt(qbw_f.astype(BF16), kbw_f.astype(BF16))
            dp = _dot_nt(do_ref[rows(c), cols_v(hh)], v_ref[rows(c), cols_v(hh)])
            pms[c, hh] = jnp.where(lower, s_fw, s_bw).astype(BF16)
            dss[c, hh] = (jnp.where(lower, dp, 0.0).astype(BF16), jnp.where(lower, 0.0, dp).astype(BF16))
        for c, hh in pairs:
            sl, ck, cv = rows(c), cols_k(hh), cols_v(hh)
            v = v_ref[sl, cv]
            dov = do_ref[sl, cv]
            ec, fc, qfw_f, kfw_f, qbw_f, kbw_f = decayed(c, hh)
            qfw, kfw, qbw, kbw = qfw_f.astype(BF16), kfw_f.astype(BF16), qbw_f.astype(BF16), kbw_f.astype(BF16)
            pm = pms[c, hh]
            e_last = e_lasts[c, hh]
            kdec = (kfw_f * e_last).astype(BF16)
            gt = gs_ref[c, hh]
            gtb = gt.astype(BF16)
            spv = sp_ref[c, hh]
            dv_ref[sl, cv] = (_dot_tn(pm, dov) + _dot_nt(kdec, gtb)).astype(BF16)
            ds_fw, ds_bw = dss[c, hh]
            dqfw = _dot(ds_fw, kfw) + _dot(dov, spv)
            dkfw = _dot_tn(ds_fw, qfw)
            dqbw = _dot(ds_bw, kbw)
            dkbw = _dot_tn(ds_bw, qbw)
            dkdec = _dot(v, gtb)
            de_last = (jnp.sum(gt * spv.astype(F32), axis=0, keepdims=True)
                       + jnp.sum(dkdec * kfw_f, axis=0, keepdims=True))
            dkfw = dkfw + dkdec * e_last
            dq_ref[sl, ck] = ((dqfw * ec + dqbw * fc) * QSCALE).astype(BF16)
            dk_ref[sl, ck] = (dkfw * fc + dkbw * ec).astype(BF16)
            dbc = dqfw * qfw_f - dqbw * qbw_f + dkbw * kbw_f - dkfw * kfw_f
            dbc_ref[sl, ck] = dbc + jnp.where(is_last, de_last * e_last, 0.0)
        rowm = _rows((rb, wk)) & (CHUNK - 1)
        dla, kk = dbc_ref[...], 1
        while kk < CHUNK:
            dla = dla + jnp.where(rowm < CHUNK - kk, pltpu.roll(dla, rb - kk, 0), 0.0)
            kk *= 2
        dg_ref[...] = dla * (1.0 / 16.0) * _sigmoid(-g)

    rmap = lambda h, r: nr - 1 - r
    rev = lambda h, r: (nr - 1 - r, h)
    return pl.pallas_call(
        body, name="gla_bwd", grid=(HEADS // GLA_HB, nr),
        in_specs=_gla_specs(rb, rmap) + [
            pl.BlockSpec((rb, wv), rev),
            pl.BlockSpec((nc, GLA_HB, HV, HK), lambda h, r: (nr - 1 - r, h, 0, 0)),
            pl.BlockSpec((128, wk), lambda h, r: (0, h)), pl.BlockSpec((1, wk), lambda h, r: (0, h)), ANY, ANY],
        out_specs=[pl.BlockSpec((rb, wk), rev), pl.BlockSpec((rb, wk), rev),
                   pl.BlockSpec((rb, wv), lambda h, r: (nr - 1 - r, OFF_V // wv + h)), pl.BlockSpec((rb, wk), rev)],
        out_shape=[jax.ShapeDtypeStruct((s, HEADS * HK), BF16), jax.ShapeDtypeStruct((s, HEADS * HK), BF16),
                   jax.ShapeDtypeStruct(dz.shape, BF16), jax.ShapeDtypeStruct((s, HEADS * HK), F32)],
        scratch_shapes=[pltpu.VMEM((GLA_HB, HV, HK), F32), pltpu.VMEM((rb, wk), F32),
                        pltpu.VMEM((nc, GLA_HB, HV, HK), F32)],
        input_output_aliases={9: 2},
        compiler_params=_cp("arbitrary", "arbitrary"),
    )(zr, zr, zr, zr, do, sp, wgk, bgk, after, dz)


def _gk_bwd(dgpre, zr, wgk, after, dz, ts):
    s = zr.shape[0]

    def body(dg_ref, zgk_ref, w_ref, after_ref, dz_in, dz_ref, dw_ref, db_ref):
        @pl.when(pl.program_id(0) == 0)
        def _():
            dw_ref[...] = jnp.zeros_like(dw_ref)
            db_ref[...] = jnp.zeros_like(db_ref)

        dg = dg_ref[...]
        dgb = dg.astype(BF16)
        dz_ref[...] = _dot_nt(dgb, w_ref[...].astype(BF16)).astype(BF16)
        dw_ref[...] += _dot_tn(zgk_ref[...], dgb)
        db_ref[...] += jnp.sum(dg, axis=0, keepdims=True)

    return pl.pallas_call(
        body, name="gk_bwd", grid=(s // ts,),
        in_specs=[pl.BlockSpec((ts, 512), lambda i: (i, 0)), pl.BlockSpec((ts, 128), lambda i: (i, OFF_GK // 128)),
                  pl.BlockSpec((128, 512), lambda i: (0, 0)), ANY, ANY],
        out_specs=[pl.BlockSpec((ts, 128), lambda i: (i, OFF_GK // 128)), pl.BlockSpec((128, 512), lambda i: (0, 0)),
                   pl.BlockSpec((1, 512), lambda i: (0, 0))],
        out_shape=[jax.ShapeDtypeStruct(dz.shape, BF16), jax.ShapeDtypeStruct((128, 512), F32),
                   jax.ShapeDtypeStruct((1, 512), F32)],
        input_output_aliases={4: 0},
        compiler_params=_cp("arbitrary"),
    )(dgpre, zr, wgk, after, dz)


def _merge_fwd(x, zr, pp, og, bgate, wpp, wgla, wout, gffn, after, ts):
    s = x.shape[0]

    def body(x_ref, z0_ref, z1_ref, pp_ref, og_ref, bg_ref, wpp_ref, wgla_ref, wout_ref, gf_ref, after_ref,
             x1_ref, mix_ref, yp_ref, yg_ref, h2_ref):
        ppv = pp_ref[...]
        yp = jnp.concatenate([_dot(ppv, wpp_ref[j]) for j in range(4)], axis=1)
        yg = _dot(og_ref[...], wgla_ref[...])
        g0 = _sigmoid(z0_ref[...].astype(F32) + bg_ref[:, :D])
        g1 = _sigmoid(z1_ref[...].astype(F32) + bg_ref[:, D:])
        mixed = (g0 * yp + g1 * yg).astype(BF16)
        x1 = x_ref[...] + _dot(mixed, wout_ref[...])
        x1_ref[...] = x1
        mix_ref[...] = mixed
        yp_ref[...] = yp.astype(BF16)
        yg_ref[...] = yg.astype(BF16)
        r = lax.rsqrt(jnp.mean(x1 * x1, axis=-1, keepdims=True) + EPS)
        h2_ref[...] = (x1 * r * gf_ref[...]).astype(BF16)

    row = lambda i: (i, 0)
    const2 = lambda i: (0, 0)
    return pl.pallas_call(
        body, name="merge_fwd", grid=(s // ts,),
        in_specs=[pl.BlockSpec((ts, D), row), pl.BlockSpec((ts, D), lambda i: (i, 0)), pl.BlockSpec((ts, D), lambda i: (i, 1)),
                  pl.BlockSpec((ts, POOL_W), row), pl.BlockSpec((ts, D), row), pl.BlockSpec((1, 2 * D), const2),
                  pl.BlockSpec((4, POOL_W, 256), lambda i: (0, 0, 0)), pl.BlockSpec((D, D), const2),
                  pl.BlockSpec((D, D), const2), pl.BlockSpec((1, D), const2), ANY],
        out_specs=[pl.BlockSpec((ts, D), row)] * 5,
        out_shape=[jax.ShapeDtypeStruct((s, D), F32)] + [jax.ShapeDtypeStruct((s, D), BF16)] * 4,
        compiler_params=_cp("arbitrary"),
    )(x, zr, zr, pp, og, bgate, wpp, wgla, wout, gffn, after)


def _merge_bwd(dx1b, zr, yp, yg, o, bgate, ghead, wpp, wgla, wout, after, ts):
    s = dx1b.shape[0]

    def body(dx_ref, z0_ref, z1_ref, zog_ref, yp_ref, yg_ref, o_ref, bg_ref, gh_ref, wpp_ref, wgla_ref, wout_ref, after_ref,
             dzg_ref, dyp_ref, dyg_ref, dpp_ref, do_ref, dzog_ref, dbg_ref, dgh_ref):
        @pl.when(pl.program_id(0) == 0)
        def _():
            dbg_ref[...] = jnp.zeros_like(dbg_ref)
            dgh_ref[...] = jnp.zeros_like(dgh_ref)

        dmix = _dot_nt(dx_ref[...], wout_ref[...])
        g0 = _sigmoid(z0_ref[...].astype(F32) + bg_ref[:, :D])
        g1 = _sigmoid(z1_ref[...].astype(F32) + bg_ref[:, D:])
        dypb = (dmix * g0).astype(BF16)
        dygb = (dmix * g1).astype(BF16)
        dz0 = dmix * yp_ref[...].astype(F32) * g0 * (1.0 - g0)
        dz1 = dmix * yg_ref[...].astype(F32) * g1 * (1.0 - g1)
        dzg_ref[:, :D] = dz0.astype(BF16)
        dzg_ref[:, D:] = dz1.astype(BF16)
        dbg_ref[:, :D] += jnp.sum(dz0, axis=0, keepdims=True)
        dbg_ref[:, D:] += jnp.sum(dz1, axis=0, keepdims=True)
        dyp_ref[...] = dypb
        dyg_ref[...] = dygb
        dpp = _dot_nt(dypb[:, 0:256], wpp_ref[0])
        for j in range(1, 4):
            dpp = dpp + _dot_nt(dypb[:, j * 256:(j + 1) * 256], wpp_ref[j])
        dpp_ref[...] = dpp.astype(BF16)
        dog = _dot_nt(dygb, wgla_ref[...])
        gh = gh_ref[...]
        dgh = jnp.zeros((1, HV), F32)
        for h in range(HEADS):
            cs = slice(h * HV, (h + 1) * HV)
            ov = o_ref[:, cs].astype(F32)
            r = lax.rsqrt(jnp.mean(ov * ov, axis=-1, keepdims=True) + EPS)
            oh = ov * r
            zo = zog_ref[:, cs].astype(F32)
            sg = _sigmoid(zo)
            dog_h = dog[:, cs]
            don = dog_h * zo * sg
            dzog_ref[:, cs] = (dog_h * oh * gh * sg * (1.0 + zo * (1.0 - sg))).astype(BF16)
            dgh = dgh + jnp.sum(don * oh, axis=0, keepdims=True)
            doh = don * gh
            do_ref[:, cs] = (r * (doh - oh * jnp.mean(doh * oh, axis=-1, keepdims=True))).astype(BF16)
        dgh_ref[...] += dgh

    row = lambda i: (i, 0)
    const2 = lambda i: (0, 0)
    return pl.pallas_call(
        body, name="merge_bwd", grid=(s // ts,),
        in_specs=[pl.BlockSpec((ts, D), row), pl.BlockSpec((ts, D), lambda i: (i, 0)), pl.BlockSpec((ts, D), lambda i: (i, 1)),
                  pl.BlockSpec((ts, D), lambda i: (i, OFF_OG // D)), pl.BlockSpec((ts, D), row), pl.BlockSpec((ts, D), row),
                  pl.BlockSpec((ts, D), row), pl.BlockSpec((1, 2 * D), const2), pl.BlockSpec((1, HV), const2),
                  pl.BlockSpec((4, POOL_W, 256), lambda i: (0, 0, 0)), pl.BlockSpec((D, D), const2),
                  pl.BlockSpec((D, D), const2), ANY],
        out_specs=[pl.BlockSpec((ts, 2 * D), row), pl.BlockSpec((ts, D), row), pl.BlockSpec((ts, D), row),
                   pl.BlockSpec((ts, POOL_W), row), pl.BlockSpec((ts, D), row), pl.BlockSpec((ts, D), row),
                   pl.BlockSpec((1, 2 * D), const2), pl.BlockSpec((1, HV), const2)],
        out_shape=[jax.ShapeDtypeStruct((s, N_INR), BF16), jax.ShapeDtypeStruct((s, D), BF16),
                   jax.ShapeDtypeStruct((s, D), BF16), jax.ShapeDtypeStruct((s, POOL_W), BF16),
                   jax.ShapeDtypeStruct((s, D), BF16), jax.ShapeDtypeStruct((s, D), BF16),
                   jax.ShapeDtypeStruct((1, 2 * D), F32), jax.ShapeDtypeStruct((1, HV), F32)],
        compiler_params=_cp("arbitrary"),
    )(dx1b, zr, zr, zr, yp, yg, o, bgate, ghead, wpp, wgla, wout, after)


HALO = 16
CCH = D_FF // 2


def _conv_taps(u_ref, halo_ref, cs, first, ts):
    u = u_ref[:, cs].astype(F32)
    hal = halo_ref[:, cs].astype(F32)
    h1 = jnp.where(first, 0.0, _pick_row(hal, HALO - 1))
    h2 = jnp.where(first, 0.0, _pick_row(hal, HALO - 2))
    row8 = _rows((8, u.shape[1]))
    r1, r2 = pltpu.roll(u, 1, 0), pltpu.roll(u, 2, 0)
    r1 = jnp.concatenate([jnp.where(row8 == 0, h1, r1[:8]), r1[8:]], axis=0)
    r2 = jnp.concatenate([jnp.where(row8 == 0, h2, jnp.where(row8 == 1, h1, r2[:8])), r2[8:]], axis=0)
    return u, r1, r2


def _ffn_down_loss(u, x1, tgt, wconv, bconv, wdown, gfin, ts):
    s = x1.shape[0]

    def body(u_ref, halo_ref, x1_ref, t_ref, wc_ref, bc_ref, wd_ref, gf_ref, a_ref, c_ref, dx_ref, dxb_ref, ls_ref,
             dgf_ref):
        i = pl.program_id(0)

        @pl.when(i == 0)
        def _():
            ls_ref[...] = jnp.zeros_like(ls_ref)
            dgf_ref[...] = jnp.zeros_like(dgf_ref)

        first = i == 0
        acc = x1_ref[...]
        for hf in range(D_FF // CCH):
            cg = slice(hf * CCH, (hf + 1) * CCH)
            cv = slice(D_FF + hf * CCH, D_FF + (hf + 1) * CCH)
            vals = []
            for cs in (cg, cv):
                u0, u1, u2 = _conv_taps(u_ref, halo_ref, cs, first, ts)
                vals.append(bc_ref[:, cs] + wc_ref[0:1, cs] * u2 + wc_ref[1:2, cs] * u1 + wc_ref[2:3, cs] * u0)
                c_ref[:, cs] = vals[-1].astype(BF16)
            a = (vals[0] * _sigmoid(vals[0]) * vals[1]).astype(BF16)
            a_ref[:, cg] = a
            acc = acc + _dot(a, wd_ref[cg, :])
        r = lax.rsqrt(jnp.mean(acc * acc, axis=-1, keepdims=True) + EPS)
        xh = acc * r
        gf = gf_ref[...]
        err = xh * gf - t_ref[...]
        ls_ref[...] += (0.5 / D) * jnp.sum(jnp.sum(err * err, axis=-1, keepdims=True), axis=0, keepdims=True)
        dy = err * (1.0 / D)
        dgf_ref[...] += jnp.sum(dy * xh, axis=0, keepdims=True)
        dxh = dy * gf
        dx = r * (dxh - xh * jnp.mean(dxh * xh, axis=-1, keepdims=True))
        dx_ref[...] = dx
        dxb_ref[...] = dx.astype(BF16)

    row = lambda i: (i, 0)
    const2 = lambda i: (0, 0)
    return pl.pallas_call(
        body, name="ffn_down_loss", grid=(s // ts,),
        in_specs=[pl.BlockSpec((ts, N_UP), row),
                  pl.BlockSpec((HALO, N_UP), lambda i: (jnp.maximum(i * (ts // HALO) - 1, 0), 0)),
                  pl.BlockSpec((ts, D), row), pl.BlockSpec((ts, D), row), pl.BlockSpec((3, N_UP), const2),
                  pl.BlockSpec((1, N_UP), const2), pl.BlockSpec((D_FF, D), const2), pl.BlockSpec((1, D), const2)],
        out_specs=[pl.BlockSpec((ts, D_FF), row), pl.BlockSpec((ts, N_UP), row), pl.BlockSpec((ts, D), row),
                   pl.BlockSpec((ts, D), row), pl.BlockSpec((1, 128), const2), pl.BlockSpec((1, D), const2)],
        out_shape=[jax.ShapeDtypeStruct((s, D_FF), BF16), jax.ShapeDtypeStruct((s, N_UP), BF16),
                   jax.ShapeDtypeStruct((s, D), F32), jax.ShapeDtypeStruct((s, D), BF16),
                   jax.ShapeDtypeStruct((1, 128), F32), jax.ShapeDtypeStruct((1, D), F32)],
        compiler_params=_cp("arbitrary"),
    )(u, u, x1, tgt, wconv, bconv, wdown, gfin)


def _ffn_bwd(dx2b, u, c, wconv, wdown, ts):
    s = dx2b.shape[0]
    nt = s // ts

    def body(dx_ref, u_ref, c_ref, wc_ref, wd_ref, du_ref, db_ref, dw_ref, nxt_ref):
        @pl.when(pl.program_id(0) == 0)
        def _():
            db_ref[...] = jnp.zeros_like(db_ref)
            dw_ref[...] = jnp.zeros_like(dw_ref)
            nxt_ref[...] = jnp.zeros_like(nxt_ref)

        dxv = dx_ref[...]
        row8 = _rows((8, CCH))
        for hf in range(D_FF // CCH):
            cg = slice(hf * CCH, (hf + 1) * CCH)
            cv = slice(D_FF + hf * CCH, D_FF + (hf + 1) * CCH)
            da = _dot_nt(dxv, wd_ref[cg, :])
            gate = c_ref[:, cg].astype(F32)
            val = c_ref[:, cv].astype(F32)
            sg = _sigmoid(gate)
            dcs = (da * val * sg * (1.0 + gate * (1.0 - sg)), da * gate * sg)
            for cs, dc in zip((cg, cv), dcs):
                n1 = nxt_ref[0:1, cs]
                n2 = nxt_ref[1:2, cs]
                r1, r2 = pltpu.roll(dc, ts - 1, 0), pltpu.roll(dc, ts - 2, 0)
                f1 = jnp.concatenate([r1[:ts - 8], jnp.where(row8 == 7, n1, r1[ts - 8:])], axis=0)
                f2 = jnp.concatenate([r2[:ts - 8], jnp.where(row8 == 7, n2, jnp.where(row8 == 6, n1, r2[ts - 8:]))], axis=0)
                uv = u_ref[:, cs].astype(F32)
                db_ref[:, cs] += jnp.sum(dc, axis=0, keepdims=True)
                dw_ref[0:1, cs] += jnp.sum(f2 * uv, axis=0, keepdims=True)
                dw_ref[1:2, cs] += jnp.sum(f1 * uv, axis=0, keepdims=True)
                dw_ref[2:3, cs] += jnp.sum(dc * uv, axis=0, keepdims=True)
                du_ref[:, cs] = (wc_ref[2:3, cs] * dc + wc_ref[1:2, cs] * f1 + wc_ref[0:1, cs] * f2).astype(BF16)
                nxt_ref[:, cs] = dc[0:8, :]

    rev = lambda i: (nt - 1 - i, 0)
    const2 = lambda i: (0, 0)
    return pl.pallas_call(
        body, name="ffn_bwd", grid=(nt,),
        in_specs=[pl.BlockSpec((ts, D), rev), pl.BlockSpec((ts, N_UP), rev), pl.BlockSpec((ts, N_UP), rev),
                  pl.BlockSpec((3, N_UP), const2), pl.BlockSpec((D_FF, D), const2)],
        out_specs=[pl.BlockSpec((ts, N_UP), rev), pl.BlockSpec((1, N_UP), const2), pl.BlockSpec((3, N_UP), const2)],
        out_shape=[jax.ShapeDtypeStruct((s, N_UP), BF16), jax.ShapeDtypeStruct((1, N_UP), F32),
                   jax.ShapeDtypeStruct((3, N_UP), F32)],
        scratch_shapes=[pltpu.VMEM((8, N_UP), F32)],
        compiler_params=_cp("arbitrary"),
    )(dx2b, u, c, wconv, wdown)


ANY = pl.BlockSpec(memory_space=pl.ANY)


def _place():
    x, y, c = lax.axis_index("x"), lax.axis_index("y"), lax.axis_index("c")
    chips = [(1 - x, y), (x, 1 - y), (1 - x, 1 - y)]
    return x, y, c, chips


def _half(shape, c, axis):
    size = shape[axis] // 2
    cut = pl.ds(pl.multiple_of(c * size, 8 if axis == 0 else 128), size)
    return (cut, slice(None)) if axis == 0 else (slice(None), cut)


def _half_shape(shape, axis):
    return (shape[0] // 2, shape[1]) if axis == 0 else (shape[0], shape[1] // 2)


def _remote(src, dst, send_sems, recv_sems, k, to):
    return pltpu.make_async_remote_copy(src_ref=src, dst_ref=dst, send_sem=send_sems.at[k], recv_sem=recv_sems.at[k],
                                        device_id=to, device_id_type=MESH)


def _sibling_exchange(grads, axes, smalls, name):
    nb = len(grads)
    n = nb + len(smalls)

    def body(*refs):
        ins, outs = refs[:n], refs[n:2 * n]
        send_sems, recv_sems = refs[2 * n:]
        x, y, c, _ = _place()
        sib = (x, y, 1 - c)
        cps = []
        for a in range(nb):
            theirs = _half(grads[a].shape[1:], 1 - c, axes[a])
            cps.append(_remote(ins[a].at[(slice(None),) + theirs], outs[a], send_sems, recv_sems, a, sib))
        for a in range(nb, n):
            cps.append(_remote(ins[a], outs[a], send_sems, recv_sems, a, sib))
        for cp in cps:
            cp.start()
        for cp in cps:
            cp.wait()

    out_shape = [jax.ShapeDtypeStruct((4,) + _half_shape(g.shape[1:], ax), g.dtype) for g, ax in zip(grads, axes)]
    out_shape += [jax.ShapeDtypeStruct(a.shape, F32) for a in smalls]
    return pl.pallas_call(
        body, name=name, in_specs=[ANY] * n, out_specs=[ANY] * n, out_shape=out_shape,
        scratch_shapes=[pltpu.SemaphoreType.DMA((n,)), pltpu.SemaphoreType.DMA((n,))],
        compiler_params=pltpu.CompilerParams(has_side_effects=True),
    )(*grads, *smalls)


def _gather_share(lands, axes, name):
    n = len(lands)

    def body(*refs):
        outs = refs[n:2 * n]
        send_sems, recv_sems = refs[2 * n:]
        x, y, c, chips = _place()
        sib = (x, y, 1 - c)
        cps = []
        for a in range(n):
            mine = _half(lands[a].shape[1:], c, axes[a])
            for k, ch in enumerate(chips):
                landed = outs[a].at[(2 * ch[0] + ch[1],) + mine]
                cps.append(_remote(landed, landed, send_sems, recv_sems, 3 * a + k, sib))
        for cp in cps:
            cp.start()
        for a in range(n):
            other = _half(lands[a].shape[1:], 1 - c, axes[a])
            for k, ch in enumerate(chips):
                landed = outs[a].at[(2 * ch[0] + ch[1],) + other]
                _remote(landed, landed, send_sems, recv_sems, 3 * a + k, sib).wait_recv()
        for cp in cps:
            cp.wait_send()

    return pl.pallas_call(
        body, name=name, in_specs=[ANY] * n, out_specs=[ANY] * n,
        out_shape=[jax.ShapeDtypeStruct(a.shape, a.dtype) for a in lands],
        input_output_aliases={a: a for a in range(n)},
        scratch_shapes=[pltpu.SemaphoreType.DMA((3 * n,)), pltpu.SemaphoreType.DMA((3 * n,))],
        compiler_params=pltpu.CompilerParams(has_side_effects=True),
    )(*lands)


def _sibling_share(halves, name):
    n = len(halves)

    def body(*refs):
        ins, outs = refs[:n], refs[n:2 * n]
        send_sems, recv_sems = refs[2 * n:]
        x, y, c, _ = _place()
        cps = [_remote(ins[a], outs[a], send_sems, recv_sems, a, (x, y, 1 - c)) for a in range(n)]
        for cp in cps:
            cp.start()
        for cp in cps:
            cp.wait()

    return pl.pallas_call(
        body, name=name, in_specs=[ANY] * n, out_specs=[ANY] * n,
        out_shape=[jax.ShapeDtypeStruct(h.shape, F32) for h in halves],
        scratch_shapes=[pltpu.SemaphoreType.DMA((n,)), pltpu.SemaphoreType.DMA((n,))],
        compiler_params=pltpu.CompilerParams(has_side_effects=True),
    )(*halves)


HBM = pl.BlockSpec(memory_space=pltpu.HBM)
SEM = pl.BlockSpec(memory_space=pltpu.SEMAPHORE)
DATAFLOW = pltpu.SideEffectType.DATAFLOW_SIDE_EFFECTING


def _split_start(name, srcs, land_shapes, plan, n_copies, after):
    lands = [lax.empty(*ls) if isinstance(ls, tuple) else ls for ls in land_shapes]
    bufs = list(srcs) + lands
    nb, ns = len(bufs), len(srcs)

    def body(*refs):
        send_sems, recv_sems, token = refs[nb + 1], refs[nb + 2], refs[-1]
        for k, (src, dst, to) in enumerate(plan(refs[:ns], refs[ns:nb])):
            _remote(src, dst, send_sems, recv_sems, k, to).start()
        token[...] = jnp.zeros_like(token)

    res = pl.pallas_call(
        body, name=name,
        out_shape=(pltpu.SemaphoreType.DMA((n_copies,)), pltpu.SemaphoreType.DMA((n_copies,)),
                   *[pltpu.HBM(b.shape, b.dtype) for b in bufs], jax.ShapeDtypeStruct((8, 128), F32)),
        in_specs=[HBM] * nb + [ANY],
        out_specs=(SEM, SEM, *[HBM] * nb, pl.BlockSpec(memory_space=pltpu.VMEM)),
        input_output_aliases={i: 2 + i for i in range(nb)},
        compiler_params=pltpu.CompilerParams(has_side_effects=DATAFLOW),
    )(*[pltpu.with_memory_space_constraint(b, pltpu.HBM) for b in bufs], after)
    return (res[0], res[1], list(res[2:2 + nb])), res[-1]


def _split_wait(name, handle, n_srcs, plan, after):
    send_sems, recv_sems, bufs = handle
    nb = len(bufs)

    def body(*refs):
        sends, recvs = refs[nb], refs[nb + 1]
        for k, (src, dst, to) in enumerate(plan(refs[:n_srcs], refs[n_srcs:nb])):
            cp = _remote(src, dst, sends, recvs, k, to)
            cp.wait_send()
            cp.wait_recv()

    res = pl.pallas_call(
        body, name=name, out_shape=[pltpu.HBM(b.shape, b.dtype) for b in bufs],
        in_specs=[HBM] * nb + [SEM, SEM, ANY], out_specs=[HBM] * nb,
        input_output_aliases={i: i for i in range(nb)},
        compiler_params=pltpu.CompilerParams(has_side_effects=DATAFLOW),
    )(*bufs, send_sems, recv_sems, after)
    return list(res[:n_srcs]), list(res[n_srcs:])


def _gather_plan(shapes, axes, n_whole=0):
    def plan(srcs, lands):
        x, y, c, chips = _place()
        out = []
        for a, (shape, axis) in enumerate(zip(shapes, axes)):
            mine = _half(shape, c, axis)
            for ch in chips:
                out.append((srcs[a].at[mine], lands[a].at[(2 * x + y,) + mine], (ch[0], ch[1], c)))
        for a in range(len(shapes), len(shapes) + n_whole):
            for ch in chips:
                out.append((srcs[a], lands[a].at[2 * x + y], (ch[0], ch[1], c)))
        return out
    return plan


def _share_plan(shapes, axes):
    def plan(srcs, lands):
        x, y, c, chips = _place()
        out = []
        for a, (shape, axis) in enumerate(zip(shapes, axes)):
            mine = _half(shape, c, axis)
            for ch in chips:
                landed = lands[a].at[(2 * ch[0] + ch[1],) + mine]
                out.append((landed, landed, (x, y, 1 - c)))
        return out
    return plan


def _sibling_plan(shapes, axes):
    def plan(srcs, lands):
        x, y, c, _ = _place()
        return [(srcs[a].at[(slice(None),) + _half(shape, 1 - c, axis)], lands[a], (x, y, 1 - c))
                for a, (shape, axis) in enumerate(zip(shapes, axes))]
    return plan


def _reduce_plan(n_big, n_small):
    def plan(srcs, lands):
        x, y, c, chips = _place()
        out = []
        for a in range(n_big):
            for k, ch in enumerate(chips):
                out.append((srcs[a].at[2 * ch[0] + ch[1]], lands[a].at[k], (ch[0], ch[1], c)))
        for a in range(n_big, n_big + n_small):
            for ch in chips:
                out.append((srcs[a], lands[a].at[2 * x + y], (ch[0], ch[1], c)))
        return out
    return plan


def _row_tile(rows, cols, mult):
    best = mult
    for t in range(mult, rows + 1, mult):
        if rows % t == 0 and t * cols * 4 <= (2 << 20):
            best = t
    return best if rows % best == 0 else rows


COL_TILE = 256


def _half_tiling(hshape, axis, mult):
    hr, hc = hshape
    if axis == 0:
        tr = _row_tile(hr, hc, mult)
        return tr, hc, hr // tr
    return hr, COL_TILE, hc // COL_TILE


def _tile_idx(axis, t):
    return (t, 0) if axis == 0 else (0, t)


def _chip_partial(place, g, t, axis, name):
    hshape = t.shape[1:]
    br, bc, nt = _half_tiling(hshape, axis, 16)

    def body(pl_ref, g_ref, t_ref, pf_ref, pb_ref):
        v = g_ref[...].astype(F32) + t_ref[...].astype(F32)
        pb_ref[...] = v.astype(BF16)

        @pl.when(pl.program_id(1) == pl_ref[0])
        def _():
            pf_ref[...] = v

    blk = (None, br, bc)
    return pl.pallas_call(
        body, name=name,
        grid_spec=pltpu.PrefetchScalarGridSpec(
            num_scalar_prefetch=1, grid=(nt, 4),
            in_specs=[pl.BlockSpec(blk, lambda i, j, p: (j,) + _tile_idx(axis, p[1] * nt + i)),
                      pl.BlockSpec(blk, lambda i, j, p: (j,) + _tile_idx(axis, i))],
            out_specs=[pl.BlockSpec((br, bc), lambda i, j, p: _tile_idx(axis, i)),
                       pl.BlockSpec(blk, lambda i, j, p: (j,) + _tile_idx(axis, i))]),
        out_shape=[jax.ShapeDtypeStruct(hshape, F32), jax.ShapeDtypeStruct((4,) + hshape, BF16)],
        compiler_params=_cp("arbitrary", "arbitrary"),
    )(place, g, t)


def _finish_half(pf, rb, axis, name):
    hshape = pf.shape
    br, bc, nt = _half_tiling(hshape, axis, 16)

    def body(pf_ref, rb_ref, o_ref):
        o_ref[...] = ((pf_ref[...] + rb_ref[0].astype(F32)) + rb_ref[1].astype(F32)) + rb_ref[2].astype(F32)

    return pl.pallas_call(
        body, name=name, grid=(nt,),
        in_specs=[pl.BlockSpec((br, bc), lambda i: _tile_idx(axis, i)),
                  pl.BlockSpec((3, br, bc), lambda i: (0,) + _tile_idx(axis, i))],
        out_specs=pl.BlockSpec((br, bc), lambda i: _tile_idx(axis, i)),
        out_shape=jax.ShapeDtypeStruct(hshape, F32),
        compiler_params=_cp("arbitrary"),
    )(pf, rb)


def _adam_math(w, g, m, v):
    m = ADAM_B1 * m + (1.0 - ADAM_B1) * g
    v = ADAM_B2 * v + (1.0 - ADAM_B2) * (g * g)
    m_hat = m / (1.0 - ADAM_B1 ** ADAM_STEP)
    v_hat = v / (1.0 - ADAM_B2 ** ADAM_STEP)
    return -ADAM_LR * (m_hat / (jnp.sqrt(v_hat) + ADAM_EPS) + ADAM_WD * w), m, v


def _adam_halves(place, w, mine, theirs, m, v, axis, name):
    br, bc, nt = _half_tiling(mine.shape, axis, 8)

    def body(pl_ref, w_ref, a_ref, b_ref, m_ref, v_ref, g_ref, d_ref, mo_ref, vo_ref):
        is_mine = pl.program_id(0) // nt == pl_ref[1]
        g = jnp.where(is_mine, a_ref[...], b_ref[...])
        d, mn, vn = _adam_math(w_ref[...], g, m_ref[...], v_ref[...])
        g_ref[...] = g
        d_ref[...] = d
        mo_ref[...] = mn
        vo_ref[...] = vn

    full = pl.BlockSpec((br, bc), lambda i, p: _tile_idx(axis, i))
    mine_spec = pl.BlockSpec((br, bc), lambda i, p: _tile_idx(axis, jnp.where(i // nt == p[1], i % nt, nt - 1)))
    theirs_spec = pl.BlockSpec((br, bc), lambda i, p: _tile_idx(axis, jnp.where(i // nt == p[1], 0, i % nt)))
    return pl.pallas_call(
        body, name=name,
        grid_spec=pltpu.PrefetchScalarGridSpec(
            num_scalar_prefetch=1, grid=(2 * nt,), in_specs=[full, mine_spec, theirs_spec, full, full],
            out_specs=[full] * 4),
        out_shape=[jax.ShapeDtypeStruct(w.shape, F32)] * 4, compiler_params=_cp("arbitrary"),
    )(place, w, mine, theirs, m, v)


def _add_many(xs, ys, name):
    n = len(xs)

    def body(*refs):
        for i in range(n):
            refs[2 * n + i][...] = refs[i][...] + refs[n + i][...]

    return pl.pallas_call(body, name=name, out_shape=[jax.ShapeDtypeStruct(a.shape, F32) for a in xs])(*xs, *ys)


def _adam_small(place, owns, landed, ws, ms, vs, widths):
    n, nw = len(owns), len(ws)

    def body(pl_ref, *refs):
        own_r, land_r = refs[:n], refs[n:2 * n]
        w_r, m_r, v_r = (refs[2 * n + k * nw:2 * n + (k + 1) * nw] for k in range(3))
        outs = refs[2 * n + 3 * nw:]
        g_o, d_o, m_o, v_o = outs[:n], outs[n:n + nw], outs[n + nw:n + 2 * nw], outs[n + 2 * nw:]
        for me in range(4):
            @pl.when(pl_ref[0] == me)
            def _(me=me):
                for i in range(n):
                    p = [own_r[i][...] if k == me else land_r[i][k] for k in range(4)]
                    g = ((p[0] + p[1]) + p[2]) + p[3]
                    if i < nw and widths[i]:
                        g = g[:, me * widths[i]:(me + 1) * widths[i]]
                    g_o[i][...] = g
                    if i < nw:
                        d, mn, vn = _adam_math(w_r[i][...], g, m_r[i][...], v_r[i][...])
                        d_o[i][...] = d
                        m_o[i][...] = mn
                        v_o[i][...] = vn

    g_shapes = [jax.ShapeDtypeStruct(ws[i].shape if i < nw else owns[i].shape, F32) for i in range(n)]
    w_shapes = [jax.ShapeDtypeStruct(w.shape, F32) for w in ws]
    whole = lambda a: pl.BlockSpec(a.shape, lambda i, p, nd=len(a.shape): (0,) * nd)
    ins = list(owns) + list(landed) + list(ws) + list(ms) + list(vs)
    out_shape = g_shapes + w_shapes * 3
    out = pl.pallas_call(
        body, name="adam_small",
        grid_spec=pltpu.PrefetchScalarGridSpec(num_scalar_prefetch=1, grid=(1,), in_specs=[whole(a) for a in ins],
                                               out_specs=[whole(a) for a in out_shape]),
        out_shape=out_shape, compiler_params=_cp("arbitrary"),
    )(place, *ins)
    return out[:n], out[n:n + nw], out[n + nw:n + 2 * nw], out[n + 2 * nw:]


def kernel(x, g_mix, w_in, b_gate, w_gk_up, b_gk, w_pool_grp, pool_scale, g_gla_head, w_pool_proj, w_gla_proj, w_out, g_ffn, w_up, w_conv, b_conv, w_down, g_final, loss_target, m_g_mix, m_w_in, m_b_gate, m_w_gk_up, m_b_gk, m_w_pool_grp, m_pool_scale, m_g_gla_head, m_w_pool_proj, m_w_gla_proj, m_w_out, m_g_ffn, m_w_up, m_w_conv, m_b_conv, m_w_down, m_g_final, v_g_mix, v_w_in, v_b_gate, v_w_gk_up, v_b_gk, v_w_pool_grp, v_pool_scale, v_g_gla_head, v_w_pool_proj, v_w_gla_proj, v_w_out, v_g_ffn, v_w_up, v_w_conv, v_b_conv, v_w_down, v_g_final):
    s = x.shape[1]
    ts = min(s, 512)
    tm = min(s, 256)
    cx, cy, cc = lax.axis_index("x"), lax.axis_index("y"), lax.axis_index("c")
    chip = 2 * cx + cy
    place = jnp.stack([chip, cc]).astype(jnp.int32)

    big_names = ("w_in", "w_pool_proj", "w_gla_proj", "w_out", "w_up", "w_down")
    axes = (1, 0, 0, 0, 0, 0)
    shards = dict(w_in=jnp.transpose(w_in[0]), w_pool_proj=w_pool_proj[0], w_gla_proj=w_gla_proj[0], w_out=w_out[0],
                  w_up=w_up[0], w_down=w_down[0])
    def fill_own(lands, mine):
        return [lax.dynamic_update_slice(g, o_[None], (chip, 0, 0)) for g, o_ in zip(lands, mine)]

    def gather_start(tag, halves, group_axes, whole, after):
        plan = _gather_plan([o_.shape for o_ in halves], group_axes, len(whole))
        srcs = list(halves) + list(whole)
        handle, token = _split_start("gather_" + tag + "_start", srcs, [((4,) + o_.shape, o_.dtype) for o_ in srcs], plan,
                                     3 * len(srcs), after)
        return (handle, plan, len(halves), len(srcs), group_axes), token

    def gather_finish(tag, started, after):
        handle, plan, n_halves, n, group_axes = started
        mine, lands = _split_wait("gather_" + tag + "_wait", handle, n, plan, after)
        lands[:n_halves] = _gather_share(lands[:n_halves], group_axes, "gather_" + tag + "_share")
        return fill_own(lands, mine)

    in_w, tok = gather_start("in", [jnp.transpose(w_in[0].astype(BF16))], axes[:1], [], g_mix)
    zero = tok[0, 0]
    own = [(shards[n] + zero).astype(BF16) for n in big_names[1:]]
    mix_w, tok = gather_start("mix", own[0:3], axes[1:4], [w_gk_up[0] + zero, w_conv[0] + zero], tok)
    up_w, tok = gather_start("up", own[3:4], axes[4:5], [], tok)
    down_w, tok = gather_start("down", own[4:5], axes[5:6], [], tok)

    def forward_start(tag, started, after):
        handle, plan, _, n, group_axes = started
        mine, lands = _split_wait("gather_" + tag + "_wait", handle, n, plan, after)
        plan = _share_plan([o_.shape for o_ in mine], group_axes)
        share, token = _split_start("gather_" + tag + "_share_start", [], lands, plan, 3 * n, after)
        return (share, plan, mine), token

    def forward_done(tag, forwarded, after):
        share, plan, mine = forwarded
        return fill_own(_split_wait("gather_" + tag + "_share_wait", share, 0, plan, after)[1], mine)
    xs, tgt = x[0], loss_target[0]
    wgrp = w_pool_grp[0]
    h = _rmsnorm(xs, g_mix, tok, "norm_mix", ts)
    m_in_t, v_in_t = jnp.transpose(m_w_in[0]), jnp.transpose(v_w_in[0])
    h, m_in_t, v_in_t = lax.optimization_barrier((h, m_in_t, v_in_t))
    w_in_t = gather_finish("in", in_w, h)[0].reshape(N_IN, D)
    nsh = N_IN // 4

    zr = _in_proj(h, w_in_t, PROJ_TILE)
    p, pp = _pool_fwd(zr, wgrp, pool_scale)
    wpp, wgla, wout, wgk4, wconv4 = gather_finish("mix", mix_w, pp)
    wgla, wout = wgla.reshape(D, D), wout.reshape(D, D)
    wgk_full = jnp.transpose(wgk4, (1, 0, 2)).reshape(GATE_RANK, 512)
    wconv_full = jnp.transpose(wconv4, (1, 0, 2)).reshape(3, N_UP)
    wgk_pad = jnp.concatenate([wgk_full, jnp.zeros((128 - GATE_RANK, 512), F32)], axis=0)
    o, og, sp = _gla_fwd(zr, wgk_pad, b_gk, g_gla_head, ts)
    up_f, tok = forward_start("up", up_w, og)
    x1, mixed, yp, yg, h2 = _merge_fwd(xs, zr, pp, og, b_gate, wpp, wgla, wout, g_ffn, tok, ts)
    wup, = forward_done("up", up_f, x1)
    down_f, tok = forward_start("down", down_w, x1)
    u = _matmul_resident(h2, wup, tok, "ffn_up")
    wdown = forward_done("down", down_f, u)[0].reshape(D_FF, D)
    a, conv_out, dx2, dx2b, loss_part, dgfin = _ffn_down_loss(u, x1, tgt, wconv_full, b_conv, wdown,
                                                              g_final.reshape(1, D), tm)

    du, dbconv, dwconv = _ffn_bwd(dx2b, u, conv_out, wconv_full, wdown, tm)
    dw_down = _matmul_tn(a, dx2b, "dw_down", D, tm=D_FF // 2)
    dw_up = _matmul_tn(h2, du, "dw_up", UP_SHARD, shard_major=True)

    def exchange_start(tag, grads, group_axes, after):
        plan = _sibling_plan([g.shape[1:] for g in grads], group_axes)
        lands = [((4,) + _half_shape(g.shape[1:], ax), g.dtype) for g, ax in zip(grads, group_axes)]
        handle, token = _split_start("sibling_" + tag + "_start", grads, lands, plan, len(grads), after)
        return (handle, plan, len(grads)), token

    def partials(tag, names, group_axes, exchange, after):
        handle, plan, n = exchange
        mine, theirs = _split_wait("sibling_" + tag + "_wait", handle, n, plan, after)
        return zip(*[_chip_partial(place, g, t, ax, "chip_partial_" + nm)
                     for nm, ax, g, t in zip(names, group_axes, mine, theirs)])

    ffn_names, ffn_axes = ("w_up", "w_down"), (0, 0)
    ffn_x, token = exchange_start("ffn", [dw_up, dw_down.reshape(4, 704, D)], ffn_axes, du)
    dx1, dx1b, dgffn = _matmul_nt_normbwd(du, wup, x1, g_ffn, dx2, token, "ffn_up_bwd", ts)
    ffn_pf, ffn_pb = partials("ffn", ffn_names, ffn_axes, ffn_x, dx1b)
    ffn_plan = _reduce_plan(2, 0)
    ffn_handle, token = _split_start("reduce_ffn_start", ffn_pb, [((3,) + p.shape[1:], BF16) for p in ffn_pb],
                                     ffn_plan, 6, ffn_pf[0])

    dzr, dyp, dyg, dpp, do, dzog, dbgate, dghead = _merge_bwd(dx1b, zr, yp, yg, o, b_gate, g_gla_head, wpp, wgla, wout,
                                                             token, ts)
    dzr = lax.dynamic_update_slice(dzr, dzog, (0, OFF_OG))
    dw_out = _matmul_tn(mixed, dx1b, "dw_out", D, tm=512)
    dw_gla = _matmul_tn(og, dyg, "dw_gla", D, tm=512)
    dw_pp = _matmul_tn(pp, dyp, "dw_pp", 256, shard_major=True)

    out_names, out_axes = ("w_pool_proj", "w_gla_proj", "w_out"), (0, 0, 0)
    out_x, token = exchange_start("out", [dw_pp, dw_gla.reshape(4, 256, D), dw_out.reshape(4, 256, D)], out_axes, dpp)
    dzr, dwgrp, dscale = _pool_bwd(p, dpp, wgrp, pool_scale, token, dzr)
    out_pf, out_pb = partials("out", out_names, out_axes, out_x, dwgrp)
    out_plan = _reduce_plan(3, 0)
    out_handle, token = _split_start("reduce_out_start", out_pb, [((3,) + p_.shape[1:], BF16) for p_ in out_pb],
                                     out_plan, 9, out_pf[0])
    dq, dk, dzr, dgpre = _gla_bwd(zr, do, sp, wgk_pad, b_gk, token, dzr, ts)
    dzr, dwgk, dbgk = _gk_bwd(dgpre, zr, wgk_pad, dgpre, dzr, ts)
    dzr = lax.dynamic_update_slice(lax.dynamic_update_slice(dzr, dq, (0, OFF_Q)), dk, (0, OFF_K))
    dw_rt = _matmul_tn(dzr, h, "dw_in", D, tm=PROJ_TILE)

    def grad_rows(lo, hi):
        out = []
        for seg_lo, seg_hi, at in ((0, 1536, OFF_POOL), (1536, 3584, OFF_V), (3584, 3600, OFF_GK), (3600, N_IN, OFF_GATE)):
            a_, b_ = max(lo, seg_lo), min(hi, seg_hi)
            if a_ < b_:
                out.append(dw_rt[at + a_ - seg_lo:at + b_ - seg_lo])
        return jnp.concatenate(out, axis=0)

    dw_in_t = jnp.stack([grad_rows(j * nsh, (j + 1) * nsh) for j in range(4)])

    in_sib = _sibling_exchange([dw_in_t], (1,), [], "sibling_exchange_in")
    in_pf, in_pb = _chip_partial(place, dw_in_t, in_sib[0], 1, "chip_partial_w_in")
    in_plan = _reduce_plan(1, 0)
    in_handle, token = _split_start("reduce_in_start", [in_pb], [((3,) + in_pb.shape[1:], BF16)], in_plan, 3, in_pf)
    grad_x, _, dgmix = _matmul_nt_normbwd(dzr, w_in_t, xs, g_mix, dx1, token, "in_proj_bwd", ts, transposed=True)
    small_names = ("g_mix", "b_gate", "w_gk_up", "b_gk", "w_pool_grp", "pool_scale", "g_gla_head", "g_ffn", "w_conv",
                   "b_conv", "g_final")
    small_mine = [dgmix, dbgate, dwgk[:GATE_RANK], dbgk, dwgrp.reshape(4 * 128, 128), dscale, dghead, dgffn, dwconv, dbconv,
                  dgfin, loss_part]
    small_sib = _sibling_exchange([], (), small_mine, "sibling_exchange_small")
    small_chip = _add_many(small_mine, small_sib, "chip_partial_small")
    small_plan = _reduce_plan(0, len(small_chip))
    small_handle, token = _split_start("reduce_small_start", small_chip, [((4,) + a_.shape, F32) for a_ in small_chip],
                                       small_plan, 3 * len(small_chip), small_mine[0])

    ms = dict(w_in=m_in_t, w_pool_proj=m_w_pool_proj[0], w_gla_proj=m_w_gla_proj[0], w_out=m_w_out[0],
              w_up=m_w_up[0], w_down=m_w_down[0])
    vs = dict(w_in=v_in_t, w_pool_proj=v_w_pool_proj[0], w_gla_proj=v_w_gla_proj[0], w_out=v_w_out[0],
              w_up=v_w_up[0], w_down=v_w_down[0])
    grad, delta, new_m, new_v = {}, {}, {}, {}

    def finish_and_update(names, group_axes, part_f, landed, tag):
        halves = [_finish_half(pf, rb, ax, "finish_" + n) for n, ax, pf, rb in zip(names, group_axes, part_f, landed)]
        sib_halves = _sibling_share(halves, "sibling_share_" + tag)
        for n, ax, mine, theirs in zip(names, group_axes, halves, sib_halves):
            res = _adam_halves(place, shards[n], mine, theirs, ms[n], vs[n], ax, "adam_" + n)
            if n == "w_in":
                res = [jnp.transpose(r_) for r_ in res]
            grad[n], delta[n], new_m[n], new_v[n] = [r_[None] for r_ in res]

    _, ffn_landed = _split_wait("reduce_ffn_wait", ffn_handle, 2, ffn_plan, token)
    _, out_landed = _split_wait("reduce_out_wait", out_handle, 3, out_plan, ffn_landed[0])
    finish_and_update(ffn_names + out_names, ffn_axes + out_axes, ffn_pf + out_pf, ffn_landed + out_landed, "rest")
    _, in_landed = _split_wait("reduce_in_wait", in_handle, 1, in_plan, delta["w_out"])
    finish_and_update(("w_in",), (1,), (in_pf,), in_landed, "in")
    small_sent, small_landed = _split_wait("reduce_small_wait", small_handle, len(small_chip), small_plan, delta["w_in"])
    given = dict(g_mix=(g_mix, m_g_mix, v_g_mix), b_gate=(b_gate, m_b_gate, v_b_gate), w_gk_up=(w_gk_up, m_w_gk_up, v_w_gk_up),
                 b_gk=(b_gk, m_b_gk, v_b_gk), w_pool_grp=(w_pool_grp, m_w_pool_grp, v_w_pool_grp),
                 pool_scale=(pool_scale, m_pool_scale, v_pool_scale), g_gla_head=(g_gla_head, m_g_gla_head, v_g_gla_head),
                 g_ffn=(g_ffn, m_g_ffn, v_g_ffn), w_conv=(w_conv, m_w_conv, v_w_conv), b_conv=(b_conv, m_b_conv, v_b_conv),
                 g_final=(g_final, m_g_final, v_g_final))
    flat2 = lambda a: a.reshape(-1, a.shape[-1])
    widths = [dict(w_gk_up=HK, w_conv=UP_SHARD).get(n) for n in small_names]
    totals, ds, mo, vo = _adam_small(place, small_sent, small_landed, *[[flat2(given[n][k]) for n in small_names] for k in range(3)],
                                     widths)
    loss = totals[-1][0, 0]
    for i, n in enumerate(small_names):
        shp = given[n][0].shape
        grad[n], delta[n], new_m[n], new_v[n] = [r_.reshape(shp) for r_ in (totals[i], ds[i], mo[i], vo[i])]

    order = ("g_mix", "w_in", "b_gate", "w_gk_up", "b_gk", "w_pool_grp", "pool_scale", "g_gla_head", "w_pool_proj",
             "w_gla_proj", "w_out", "g_ffn", "w_up", "w_conv", "b_conv", "w_down", "g_final")
    return (loss, grad_x[None], *[grad[n] for n in order], *[delta[n] for n in order], *[new_m[n] for n in order],
            *[new_v[n] for n in order])
```

```python
import jax
import jax.numpy as jnp
from jax import lax
from jax.experimental import pallas as pl
from jax.experimental.pallas import tpu as pltpu

F32 = jnp.float32
BF16 = jnp.bfloat16
MESH = pl.DeviceIdType.MESH

D = 1024
EPS = 1e-6
CHUNK = 64
POOL_W = 512
POOL_WINDOWS = (2, 4, 8, 16)
HEADS = 4
HK = 128
HV = 256
GATE_RANK = 16
D_FF = 2816
N_UP = 2 * D_FF
N_IN = 5648
QSCALE = HK ** -0.5
N_INR = 5760
OFF_GATE, OFF_V, OFF_OG, OFF_POOL, OFF_Q, OFF_K, OFF_GK = 0, 2048, 3072, 4096, 4608, 5120, 5632

ADAM_LR, ADAM_B1, ADAM_B2, ADAM_EPS, ADAM_WD, ADAM_STEP = 0.001, 0.9, 0.999, 1e-08, 0.01, 10

VMEM_LIMIT = 56 * 1024 * 1024
PROJ_TILE = N_INR // 5
UP_SHARD = N_UP // 4


def _cp(*sem):
    return pltpu.CompilerParams(dimension_semantics=sem if sem else None, vmem_limit_bytes=VMEM_LIMIT)


def _dot(a, b):
    return jnp.dot(a, b, preferred_element_type=F32)


def _dot_nt(a, b):
    return lax.dot_general(a, b, (((1,), (1,)), ((), ())), preferred_element_type=F32)


def _dot_tn(a, b):
    return lax.dot_general(a, b, (((0,), (0,)), ((), ())), preferred_element_type=F32)


def _sigmoid(v):
    return 1.0 / (1.0 + jnp.exp(-v))


def _rows(shape):
    return lax.broadcasted_iota(jnp.int32, shape, 0)


def _pick_row(v, r):
    return jnp.sum(jnp.where(_rows(v.shape) == r, v, 0.0), axis=0, keepdims=True)


def _rmsnorm(x, g, after, name, ts):
    s = x.shape[0]

    def body(x_ref, g_ref, after_ref, h_ref):
        xv = x_ref[...]
        r = lax.rsqrt(jnp.mean(xv * xv, axis=-1, keepdims=True) + EPS)
        h_ref[...] = (xv * r * g_ref[...]).astype(BF16)

    return pl.pallas_call(
        body, name=name, grid=(s // ts,),
        in_specs=[pl.BlockSpec((ts, D), lambda i: (i, 0)), pl.BlockSpec((1, D), lambda i: (0, 0)), ANY],
        out_specs=pl.BlockSpec((ts, D), lambda i: (i, 0)), out_shape=jax.ShapeDtypeStruct((s, D), BF16),
        compiler_params=_cp("arbitrary"),
    )(x, g, after)


MM_ROWS = 512


def _matmul_resident(h, w, after, name):
    s = h.shape[0]
    nj, tn = w.shape[0], w.shape[2]
    rc = min(s, MM_ROWS)

    def body(h_ref, w_ref, after_ref, z_ref):
        for r0 in range(0, s, rc):
            z_ref[r0:r0 + rc, :] = _dot(h_ref[r0:r0 + rc, :], w_ref[...]).astype(BF16)

    return pl.pallas_call(
        body, name=name, grid=(nj,),
        in_specs=[pl.BlockSpec((s, D), lambda j: (0, 0)), pl.BlockSpec((None, D, tn), lambda j: (j, 0, 0)), ANY],
        out_specs=pl.BlockSpec((s, tn), lambda j: (0, j)), out_shape=jax.ShapeDtypeStruct((s, nj * tn), BF16),
        compiler_params=_cp("arbitrary"),
    )(h, w, after)


PROJ_PIECES = ((3600, 2048, OFF_GATE), (1536, 2048, OFF_V), (0, 1536, OFF_POOL), (3584, GATE_RANK, OFF_GK))


def _projection_copies(w_hbm, w_ref, sems):
    return [pltpu.make_async_copy(w_hbm.at[pl.ds(src, n)], w_ref.at[pl.ds(dst, n)], sems.at[i])
            for i, (src, n, dst) in enumerate(PROJ_PIECES)]


def _load_projection(w_hbm, w_ref, sems):
    cps = _projection_copies(w_hbm, w_ref, sems)
    for cp in cps:
        cp.start()
    w_ref[OFF_GK + GATE_RANK:, :] = jnp.zeros((N_INR - OFF_GK - GATE_RANK, D), BF16)
    for cp in cps:
        cp.wait()


def _in_proj(h, w_nat, tn):
    s = h.shape[0]
    rc = min(s, MM_ROWS)
    nj = N_INR // tn
    first_use = [dst // tn for _, _, dst in PROJ_PIECES]

    def body(h_ref, w_hbm, z_ref, w_ref, sems):
        j = pl.program_id(0)
        cps = _projection_copies(w_hbm, w_ref, sems)

        @pl.when(j == 0)
        def _():
            for cp in cps:
                cp.start()
            w_ref[OFF_GK + GATE_RANK:, :] = jnp.zeros((N_INR - OFF_GK - GATE_RANK, D), BF16)

        for step in range(nj):
            due = [cp for cp, at in zip(cps, first_use) if at == step]
            if due:
                @pl.when(j == step)
                def _(due=due):
                    for cp in due:
                        cp.wait()

        wt = w_ref[pl.ds(pl.multiple_of(j * tn, 128), tn), :]
        for r0 in range(0, s, rc):
            z_ref[r0:r0 + rc, :] = _dot_nt(h_ref[r0:r0 + rc, :], wt).astype(BF16)

    return pl.pallas_call(
        body, name="in_proj", grid=(nj,),
        in_specs=[pl.BlockSpec((s, D), lambda j: (0, 0)), ANY],
        out_specs=pl.BlockSpec((s, tn), lambda j: (0, j)), out_shape=jax.ShapeDtypeStruct((s, N_INR), BF16),
        scratch_shapes=[pltpu.VMEM((N_INR, D), BF16), pltpu.SemaphoreType.DMA((len(PROJ_PIECES),))],
        compiler_params=_cp("arbitrary"),
    )(h, w_nat)


def _matmul_nt_normbwd(dz, w, x, g, resid, after, name, ts, transposed=False):
    s = x.shape[0]
    w_vmem = (N_INR, D) if transposed else (D, w.shape[0] * w.shape[2])
    n_sems = len(PROJ_PIECES) if transposed else w.shape[0]

    def body(dz_ref, w_hbm, x_ref, g_ref, r_ref, after_ref, o_ref, ob_ref, dg_ref, w_ref, sems):
        @pl.when(pl.program_id(0) == 0)
        def _():
            if transposed:
                _load_projection(w_hbm, w_ref, sems)
            else:
                kc = w.shape[2]
                cps = [pltpu.make_async_copy(w_hbm.at[j], w_ref.at[:, pl.ds(j * kc, kc)], sems.at[j])
                       for j in range(w.shape[0])]
                for cp in cps:
                    cp.start()
                for cp in cps:
                    cp.wait()
            dg_ref[...] = jnp.zeros_like(dg_ref)

        dh = _dot(dz_ref[...], w_ref[...]) if transposed else _dot_nt(dz_ref[...], w_ref[...])
        xv = x_ref[...]
        r = lax.rsqrt(jnp.mean(xv * xv, axis=-1, keepdims=True) + EPS)
        xh = xv * r
        dg_ref[...] += jnp.sum(dh * xh, axis=0, keepdims=True)
        dxh = dh * g_ref[...]
        out = r_ref[...] + r * (dxh - xh * jnp.mean(dxh * xh, axis=-1, keepdims=True))
        o_ref[...] = out
        ob_ref[...] = out.astype(BF16)

    row = lambda i: (i, 0)
    kdim = dz.shape[1]
    return pl.pallas_call(
        body, name=name, grid=(s // ts,),
        in_specs=[pl.BlockSpec((ts, kdim), row), ANY, pl.BlockSpec((ts, D), row),
                  pl.BlockSpec((1, D), lambda i: (0, 0)), pl.BlockSpec((ts, D), row), ANY],
        out_specs=[pl.BlockSpec((ts, D), row), pl.BlockSpec((ts, D), row), pl.BlockSpec((1, D), lambda i: (0, 0))],
        out_shape=[jax.ShapeDtypeStruct((s, D), F32), jax.ShapeDtypeStruct((s, D), BF16),
                   jax.ShapeDtypeStruct((1, D), F32)],
        scratch_shapes=[pltpu.VMEM(w_vmem, BF16), pltpu.SemaphoreType.DMA((n_sems,))],
        compiler_params=_cp("arbitrary"),
    )(dz, w, x, g, resid, after)


def _matmul_tn(a, b, name, tn, shard_major=False, tm=None):
    s, m = a.shape
    n = b.shape[1]
    tm = m if tm is None else tm
    ni, nj = m // tm, n // tn

    def body(a_ref, b_ref, o_ref):
        o_ref[...] = _dot_tn(a_ref[...], b_ref[...]).astype(BF16)

    if shard_major:
        out_spec = pl.BlockSpec((None, tm, tn), lambda i, j: (j, i, 0))
        out_shape = jax.ShapeDtypeStruct((nj, m, tn), BF16)
    else:
        out_spec = pl.BlockSpec((tm, tn), lambda i, j: (i, j))
        out_shape = jax.ShapeDtypeStruct((m, n), BF16)
    return pl.pallas_call(
        body, name=name, grid=(ni, nj),
        in_specs=[pl.BlockSpec((s, tm), lambda i, j: (0, i)), pl.BlockSpec((s, tn), lambda i, j: (0, j))],
        out_specs=out_spec, out_shape=out_shape,
        compiler_params=_cp("arbitrary", "arbitrary"),
    )(a, b)


def _pool_fwd(zr, wgrp, scale):
    s = zr.shape[0]

    def body(u_ref, w_ref, sc_ref, p_ref, pp_ref):
        row = _rows((s, 128))
        for gi, win in enumerate(POOL_WINDOWS):
            cs = slice(gi * 128, (gi + 1) * 128)
            u = u_ref[:, cs].astype(F32)
            acc, k = u, 1
            while k < win:
                acc = acc + jnp.where(row >= k, pltpu.roll(acc, k, 0), 0.0)
                k *= 2
            cnt = jnp.minimum(row + 1, win).astype(F32)
            p = (acc / cnt - u).astype(BF16)
            p_ref[:, cs] = p
            pp_ref[:, cs] = (_dot(p, w_ref[gi].astype(BF16)) * sc_ref[:, cs]).astype(BF16)

    return pl.pallas_call(
        body, name="pool_fwd", grid=(1,),
        in_specs=[pl.BlockSpec((s, POOL_W), lambda i: (0, OFF_POOL // POOL_W)),
                  pl.BlockSpec((4, 128, 128), lambda i: (0, 0, 0)), pl.BlockSpec((1, POOL_W), lambda i: (0, 0))],
        out_specs=[pl.BlockSpec((s, POOL_W), lambda i: (0, 0))] * 2,
        out_shape=[jax.ShapeDtypeStruct((s, POOL_W), BF16)] * 2,
        compiler_params=_cp("arbitrary"),
    )(zr, wgrp, scale)


def _pool_bwd(p, dpp, wgrp, scale, after, dz):
    s = p.shape[0]

    def body(p_ref, dpp_ref, w_ref, sc_ref, after_ref, dz_in, dz_ref, dw_ref, dsc_ref):
        row = _rows((s, 128))
        for gi, win in enumerate(POOL_WINDOWS):
            cs = slice(gi * 128, (gi + 1) * 128)
            pv = p_ref[:, cs]
            wb = w_ref[gi].astype(BF16)
            dpp_v = dpp_ref[:, cs].astype(F32)
            dsc_ref[:, cs] = jnp.sum(dpp_v * _dot(pv, wb), axis=0, keepdims=True)
            dpm = (dpp_v * sc_ref[:, cs]).astype(BF16)
            dw_ref[gi] = _dot_tn(pv, dpm)
            dp = _dot_nt(dpm, wb)
            cnt = jnp.minimum(row + 1, win).astype(F32)
            acc, k = dp / cnt, 1
            while k < win:
                acc = acc + jnp.where(row < s - k, pltpu.roll(acc, s - k, 0), 0.0)
                k *= 2
            dz_ref[:, cs] = (acc - dp).astype(BF16)

    full = lambda i: (0, 0)
    return pl.pallas_call(
        body, name="pool_bwd", grid=(1,),
        in_specs=[pl.BlockSpec((s, POOL_W), full), pl.BlockSpec((s, POOL_W), full),
                  pl.BlockSpec((4, 128, 128), lambda i: (0, 0, 0)), pl.BlockSpec((1, POOL_W), full), ANY, ANY],
        out_specs=[pl.BlockSpec((s, POOL_W), lambda i: (0, OFF_POOL // POOL_W)),
                   pl.BlockSpec((4, 128, 128), lambda i: (0, 0, 0)), pl.BlockSpec((1, POOL_W), full)],
        out_shape=[jax.ShapeDtypeStruct(dz.shape, BF16), jax.ShapeDtypeStruct((4, 128, 128), F32),
                   jax.ShapeDtypeStruct((1, POOL_W), F32)],
        input_output_aliases={5: 0},
        compiler_params=_cp("arbitrary"),
    )(p, dpp, wgrp, scale, after, dz)


def _gla_decay(zgk_ref, wgk_ref, bgk_ref, rb):
    g = _dot(zgk_ref[...], wgk_ref[...].astype(BF16)) + bgk_ref[...]
    la = (jnp.minimum(g, 0.0) - jnp.log(1.0 + jnp.exp(-jnp.abs(g)))) * (1.0 / 16.0)
    rowm = _rows(la.shape) & (CHUNK - 1)
    bc, k = la, 1
    while k < CHUNK:
        bc = bc + jnp.where(rowm >= k, pltpu.roll(bc, k, 0), 0.0)
        k *= 2
    return g, jnp.exp(bc), jnp.exp(-bc)


GLA_HB = 4


def _gla_specs(rb, rmap):
    wk, wv = GLA_HB * HK, GLA_HB * HV
    return [pl.BlockSpec((rb, wk), lambda h, r: (rmap(h, r), OFF_Q // wk + h)),
            pl.BlockSpec((rb, wk), lambda h, r: (rmap(h, r), OFF_K // wk + h)),
            pl.BlockSpec((rb, wv), lambda h, r: (rmap(h, r), OFF_V // wv + h)),
            pl.BlockSpec((rb, 128), lambda h, r: (rmap(h, r), OFF_GK // 128))]


def _gla_fwd(zr, wgk, bgk, ghead, rb):
    s = zr.shape[0]
    nc = rb // CHUNK
    wk, wv = GLA_HB * HK, GLA_HB * HV

    def body(q_ref, k_ref, v_ref, zgk_ref, zog_ref, wgk_ref, bgk_ref, gh_ref, o_ref, og_ref, sp_ref, st_ref, kv_ref):
        @pl.when(pl.program_id(1) == 0)
        def _():
            st_ref[...] = jnp.zeros_like(st_ref)

        _, e_pos, e_neg = _gla_decay(zgk_ref, wgk_ref, bgk_ref, rb)
        lower = _rows((CHUNK, CHUNK)) >= lax.broadcasted_iota(jnp.int32, (CHUNK, CHUNK), 1)
        pairs = [(c, hh) for c in range(nc) for hh in range(GLA_HB)]
        rows = lambda c: slice(c * CHUNK, (c + 1) * CHUNK)
        cols_k = lambda hh: slice(hh * HK, (hh + 1) * HK)
        cols_v = lambda hh: slice(hh * HV, (hh + 1) * HV)
        qfws, pms, e_lasts = {}, {}, {}
        for c, hh in pairs:
            q = q_ref[rows(c), cols_k(hh)].astype(F32) * QSCALE
            k = k_ref[rows(c), cols_k(hh)].astype(F32)
            ec, fc = e_pos[rows(c), cols_k(hh)], e_neg[rows(c), cols_k(hh)]
            qfw = (q * ec).astype(BF16)
            kfw_f = k * fc
            s_fw = _dot_nt(qfw, kfw_f.astype(BF16))
            s_bw = _dot_nt((q * fc).astype(BF16), (k * ec).astype(BF16))
            e_last = _pick_row(ec, CHUNK - 1)
            kv_ref[c, hh] = _dot_tn(v_ref[rows(c), cols_v(hh)], (kfw_f * e_last).astype(BF16))
            qfws[c, hh], pms[c, hh], e_lasts[c, hh] = qfw, jnp.where(lower, s_fw, s_bw).astype(BF16), e_last
        for hh in range(GLA_HB):
            st = st_ref[hh]
            for c in range(nc):
                sp_ref[c, hh] = st.astype(BF16)
                st = st * e_lasts[c, hh] + kv_ref[c, hh]
            st_ref[hh] = st
        for c, hh in pairs:
            o = _dot(pms[c, hh], v_ref[rows(c), cols_v(hh)]) + _dot_nt(qfws[c, hh], sp_ref[c, hh])
            r = lax.rsqrt(jnp.mean(o * o, axis=-1, keepdims=True) + EPS)
            zo = zog_ref[rows(c), cols_v(hh)].astype(F32)
            o_ref[rows(c), cols_v(hh)] = o.astype(BF16)
            og_ref[rows(c), cols_v(hh)] = (o * r * gh_ref[...] * zo * _sigmoid(zo)).astype(BF16)

    rmap = lambda h, r: r
    return pl.pallas_call(
        body, name="gla_fwd", grid=(HEADS // GLA_HB, s // rb),
        in_specs=_gla_specs(rb, rmap) + [
            pl.BlockSpec((rb, wv), lambda h, r: (r, OFF_OG // wv + h)),
            pl.BlockSpec((128, wk), lambda h, r: (0, h)), pl.BlockSpec((1, wk), lambda h, r: (0, h)),
            pl.BlockSpec((1, HV), lambda h, r: (0, 0))],
        out_specs=[pl.BlockSpec((rb, wv), lambda h, r: (r, h)), pl.BlockSpec((rb, wv), lambda h, r: (r, h)),
                   pl.BlockSpec((nc, GLA_HB, HV, HK), lambda h, r: (r, h, 0, 0))],
        out_shape=[jax.ShapeDtypeStruct((s, D), BF16), jax.ShapeDtypeStruct((s, D), BF16),
                   jax.ShapeDtypeStruct((s // CHUNK, HEADS, HV, HK), BF16)],
        scratch_shapes=[pltpu.VMEM((GLA_HB, HV, HK), F32), pltpu.VMEM((nc, GLA_HB, HV, HK), F32)],
        compiler_params=_cp("arbitrary", "arbitrary"),
    )(zr, zr, zr, zr, zr, wgk, bgk, ghead)


def _gla_bwd(zr, do, sp, wgk, bgk, after, dz, rb):
    s = zr.shape[0]
    nc = rb // CHUNK
    nr = s // rb
    wk, wv = GLA_HB * HK, GLA_HB * HV

    def body(q_ref, k_ref, v_ref, zgk_ref, do_ref, sp_ref, wgk_ref, bgk_ref, after_ref, dz_in, dq_ref, dk_ref, dv_ref,
             dg_ref, gt_ref, dbc_ref, gs_ref):
        @pl.when(pl.program_id(1) == 0)
        def _():
            gt_ref[...] = jnp.zeros_like(gt_ref)

        g, e_pos, e_neg = _gla_decay(zgk_ref, wgk_ref, bgk_ref, rb)
        lower = _rows((CHUNK, CHUNK)) >= lax.broadcasted_iota(jnp.int32, (CHUNK, CHUNK), 1)
        is_last = _rows((CHUNK, HK)) == CHUNK - 1
        pairs = [(c, hh) for c in range(nc) for hh in range(GLA_HB)]
        rows = lambda c: slice(c * CHUNK, (c + 1) * CHUNK)
        cols_k = lambda hh: slice(hh * HK, (hh + 1) * HK)
        cols_v = lambda hh: slice(hh * HV, (hh + 1) * HV)
        e_lasts = {}
        for c, hh in pairs:
            ec = e_pos[rows(c), cols_k(hh)]
            qfw = (q_ref[rows(c), cols_k(hh)].astype(F32) * QSCALE * ec).astype(BF16)
            gs_ref[c, hh] = _dot_tn(do_ref[rows(c), cols_v(hh)], qfw)
            e_lasts[c, hh] = _pick_row(ec, CHUNK - 1)
        for hh in range(GLA_HB):
            gt = gt_ref[hh]
            for c in reversed(range(nc)):
                own = gs_ref[c, hh]
                gs_ref[c, hh] = gt
                gt = own + gt * e_lasts[c, hh]
            gt_ref[hh] = gt
        def decayed(c, hh):
            q = q_ref[rows(c), cols_k(hh)].astype(F32) * QSCALE
            k = k_ref[rows(c), cols_k(hh)].astype(F32)
            ec, fc = e_pos[rows(c), cols_k(hh)], e_neg[rows(c), cols_k(hh)]
            return ec, fc, q * ec, k * fc, q * fc, k * ec

        pms, dss = {}, {}
        for c, hh in pairs:
            _, _, qfw_f, kfw_f, qbw_f, kbw_f = decayed(c, hh)
            s_fw = _dot_nt(qfw_f.astype(BF16), kfw_f.astype(BF16))
            s_bw = _dot_nt(qbw_f.astype(BF16), kbw_f.astype(BF16))
            dp = _dot_nt(do_ref[rows(c), cols_v(hh)], v_ref[rows(c), cols_v(hh)])
            pms[c, hh] = jnp.where(lower, s_fw, s_bw).astype(BF16)
            dss[c, hh] = (jnp.where(lower, dp, 0.0).astype(BF16), jnp.where(lower, 0.0, dp).astype(BF16))
        for c, hh in pairs:
            sl, ck, cv = rows(c), cols_k(hh), cols_v(hh)
            v = v_ref[sl, cv]
            dov = do_ref[sl, cv]
            ec, fc, qfw_f, kfw_f, qbw_f, kbw_f = decayed(c, hh)
            qfw, kfw, qbw, kbw = qfw_f.astype(BF16), kfw_f.astype(BF16), qbw_f.astype(BF16), kbw_f.astype(BF16)
            pm = pms[c, hh]
            e_last = e_lasts[c, hh]
            kdec = (kfw_f * e_last).astype(BF16)
            gt = gs_ref[c, hh]
            gtb = gt.astype(BF16)
            spv = sp_ref[c, hh]
            dv_ref[sl, cv] = (_dot_tn(pm, dov) + _dot_nt(kdec, gtb)).astype(BF16)
            ds_fw, ds_bw = dss[c, hh]
            dqfw = _dot(ds_fw, kfw) + _dot(dov, spv)
            dkfw = _dot_tn(ds_fw, qfw)
            dqbw = _dot(ds_bw, kbw)
            dkbw = _dot_tn(ds_bw, qbw)
            dkdec = _dot(v, gtb)
            de_last = (jnp.sum(gt * spv.astype(F32), axis=0, keepdims=True)
                       + jnp.sum(dkdec * kfw_f, axis=0, keepdims=True))
            dkfw = dkfw + dkdec * e_last
            dq_ref[sl, ck] = ((dqfw * ec + dqbw * fc) * QSCALE).astype(BF16)
            dk_ref[sl, ck] = (dkfw * fc + dkbw * ec).astype(BF16)
            dbc = dqfw * qfw_f - dqbw * qbw_f + dkbw * kbw_f - dkfw * kfw_f
            dbc_ref[sl, ck] = dbc + jnp.where(is_last, de_last * e_last, 0.0)
        rowm = _rows((rb, wk)) & (CHUNK - 1)
        dla, kk = dbc_ref[...], 1
        while kk < CHUNK:
            dla = dla + jnp.where(rowm < CHUNK - kk, pltpu.roll(dla, rb - kk, 0), 0.0)
            kk *= 2
        dg_ref[...] = dla * (1.0 / 16.0) * _sigmoid(-g)

    rmap = lambda h, r: nr - 1 - r
    rev = lambda h, r: (nr - 1 - r, h)
    return pl.pallas_call(
        body, name="gla_bwd", grid=(HEADS // GLA_HB, nr),
        in_specs=_gla_specs(rb, rmap) + [
            pl.BlockSpec((rb, wv), rev),
            pl.BlockSpec((nc, GLA_HB, HV, HK), lambda h, r: (nr - 1 - r, h, 0, 0)),
            pl.BlockSpec((128, wk), lambda h, r: (0, h)), pl.BlockSpec((1, wk), lambda h, r: (0, h)), ANY, ANY],
        out_specs=[pl.BlockSpec((rb, wk), rev), pl.BlockSpec((rb, wk), rev),
                   pl.BlockSpec((rb, wv), lambda h, r: (nr - 1 - r, OFF_V // wv + h)), pl.BlockSpec((rb, wk), rev)],
        out_shape=[jax.ShapeDtypeStruct((s, HEADS * HK), BF16), jax.ShapeDtypeStruct((s, HEADS * HK), BF16),
                   jax.ShapeDtypeStruct(dz.shape, BF16), jax.ShapeDtypeStruct((s, HEADS * HK), F32)],
        scratch_shapes=[pltpu.VMEM((GLA_HB, HV, HK), F32), pltpu.VMEM((rb, wk), F32),
                        pltpu.VMEM((nc, GLA_HB, HV, HK), F32)],
        input_output_aliases={9: 2},
        compiler_params=_cp("arbitrary", "arbitrary"),
    )(zr, zr, zr, zr, do, sp, wgk, bgk, after, dz)


def _gk_bwd(dgpre, zr, wgk, after, dz, ts):
    s = zr.shape[0]

    def body(dg_ref, zgk_ref, w_ref, after_ref, dz_in, dz_ref, dw_ref, db_ref):
        @pl.when(pl.program_id(0) == 0)
        def _():
            dw_ref[...] = jnp.zeros_like(dw_ref)
            db_ref[...] = jnp.zeros_like(db_ref)

        dg = dg_ref[...]
        dgb = dg.astype(BF16)
        dz_ref[...] = _dot_nt(dgb, w_ref[...].astype(BF16)).astype(BF16)
        dw_ref[...] += _dot_tn(zgk_ref[...], dgb)
        db_ref[...] += jnp.sum(dg, axis=0, keepdims=True)

    return pl.pallas_call(
        body, name="gk_bwd", grid=(s // ts,),
        in_specs=[pl.BlockSpec((ts, 512), lambda i: (i, 0)), pl.BlockSpec((ts, 128), lambda i: (i, OFF_GK // 128)),
                  pl.BlockSpec((128, 512), lambda i: (0, 0)), ANY, ANY],
        out_specs=[pl.BlockSpec((ts, 128), lambda i: (i, OFF_GK // 128)), pl.BlockSpec((128, 512), lambda i: (0, 0)),
                   pl.BlockSpec((1, 512), lambda i: (0, 0))],
        out_shape=[jax.ShapeDtypeStruct(dz.shape, BF16), jax.ShapeDtypeStruct((128, 512), F32),
                   jax.ShapeDtypeStruct((1, 512), F32)],
        input_output_aliases={4: 0},
        compiler_params=_cp("arbitrary"),
    )(dgpre, zr, wgk, after, dz)


def _merge_fwd(x, zr, pp, og, bgate, wpp, wgla, wout, gffn, after, ts):
    s = x.shape[0]

    def body(x_ref, z0_ref, z1_ref, pp_ref, og_ref, bg_ref, wpp_ref, wgla_ref, wout_ref, gf_ref, after_ref,
             x1_ref, mix_ref, yp_ref, yg_ref, h2_ref):
        ppv = pp_ref[...]
        yp = jnp.concatenate([_dot(ppv, wpp_ref[j]) for j in range(4)], axis=1)
        yg = _dot(og_ref[...], wgla_ref[...])
        g0 = _sigmoid(z0_ref[...].astype(F32) + bg_ref[:, :D])
        g1 = _sigmoid(z1_ref[...].astype(F32) + bg_ref[:, D:])
        mixed = (g0 * yp + g1 * yg).astype(BF16)
        x1 = x_ref[...] + _dot(mixed, wout_ref[...])
        x1_ref[...] = x1
        mix_ref[...] = mixed
        yp_ref[...] = yp.astype(BF16)
        yg_ref[...] = yg.astype(BF16)
        r = lax.rsqrt(jnp.mean(x1 * x1, axis=-1, keepdims=True) + EPS)
        h2_ref[...] = (x1 * r * gf_ref[...]).astype(BF16)

    row = lambda i: (i, 0)
    const2 = lambda i: (0, 0)
    return pl.pallas_call(
        body, name="merge_fwd", grid=(s // ts,),
        in_specs=[pl.BlockSpec((ts, D), row), pl.BlockSpec((ts, D), lambda i: (i, 0)), pl.BlockSpec((ts, D), lambda i: (i, 1)),
                  pl.BlockSpec((ts, POOL_W), row), pl.BlockSpec((ts, D), row), pl.BlockSpec((1, 2 * D), const2),
                  pl.BlockSpec((4, POOL_W, 256), lambda i: (0, 0, 0)), pl.BlockSpec((D, D), const2),
                  pl.BlockSpec((D, D), const2), pl.BlockSpec((1, D), const2), ANY],
        out_specs=[pl.BlockSpec((ts, D), row)] * 5,
        out_shape=[jax.ShapeDtypeStruct((s, D), F32)] + [jax.ShapeDtypeStruct((s, D), BF16)] * 4,
        compiler_params=_cp("arbitrary"),
    )(x, zr, zr, pp, og, bgate, wpp, wgla, wout, gffn, after)


def _merge_bwd(dx1b, zr, yp, yg, o, bgate, ghead, wpp, wgla, wout, after, ts):
    s = dx1b.shape[0]

    def body(dx_ref, z0_ref, z1_ref, zog_ref, yp_ref, yg_ref, o_ref, bg_ref, gh_ref, wpp_ref, wgla_ref, wout_ref, after_ref,
             dzg_ref, dyp_ref, dyg_ref, dpp_ref, do_ref, dzog_ref, dbg_ref, dgh_ref):
        @pl.when(pl.program_id(0) == 0)
        def _():
            dbg_ref[...] = jnp.zeros_like(dbg_ref)
            dgh_ref[...] = jnp.zeros_like(dgh_ref)

        dmix = _dot_nt(dx_ref[...], wout_ref[...])
        g0 = _sigmoid(z0_ref[...].astype(F32) + bg_ref[:, :D])
        g1 = _sigmoid(z1_ref[...].astype(F32) + bg_ref[:, D:])
        dypb = (dmix * g0).astype(BF16)
        dygb = (dmix * g1).astype(BF16)
        dz0 = dmix * yp_ref[...].astype(F32) * g0 * (1.0 - g0)
        dz1 = dmix * yg_ref[...].astype(F32) * g1 * (1.0 - g1)
        dzg_ref[:, :D] = dz0.astype(BF16)
        dzg_ref[:, D:] = dz1.astype(BF16)
        dbg_ref[:, :D] += jnp.sum(dz0, axis=0, keepdims=True)
        dbg_ref[:, D:] += jnp.sum(dz1, axis=0, keepdims=True)
        dyp_ref[...] = dypb
        dyg_ref[...] = dygb
        dpp = _dot_nt(dypb[:, 0:256], wpp_ref[0])
        for j in range(1, 4):
            dpp = dpp + _dot_nt(dypb[:, j * 256:(j + 1) * 256], wpp_ref[j])
        dpp_ref[...] = dpp.astype(BF16)
        dog = _dot_nt(dygb, wgla_ref[...])
        gh = gh_ref[...]
        dgh = jnp.zeros((1, HV), F32)
        for h in range(HEADS):
            cs = slice(h * HV, (h + 1) * HV)
            ov = o_ref[:, cs].astype(F32)
            r = lax.rsqrt(jnp.mean(ov * ov, axis=-1, keepdims=True) + EPS)
            oh = ov * r
            zo = zog_ref[:, cs].astype(F32)
            sg = _sigmoid(zo)
            dog_h = dog[:, cs]
            don = dog_h * zo * sg
            dzog_ref[:, cs] = (dog_h * oh * gh * sg * (1.0 + zo * (1.0 - sg))).astype(BF16)
            dgh = dgh + jnp.sum(don * oh, axis=0, keepdims=True)
            doh = don * gh
            do_ref[:, cs] = (r * (doh - oh * jnp.mean(doh * oh, axis=-1, keepdims=True))).astype(BF16)
        dgh_ref[...] += dgh

    row = lambda i: (i, 0)
    const2 = lambda i: (0, 0)
    return pl.pallas_call(
        body, name="merge_bwd", grid=(s // ts,),
        in_specs=[pl.BlockSpec((ts, D), row), pl.BlockSpec((ts, D), lambda i: (i, 0)), pl.BlockSpec((ts, D), lambda i: (i, 1)),
                  pl.BlockSpec((ts, D), lambda i: (i, OFF_OG // D)), pl.BlockSpec((ts, D), row), pl.BlockSpec((ts, D), row),
                  pl.BlockSpec((ts, D), row), pl.BlockSpec((1, 2 * D), const2), pl.BlockSpec((1, HV), const2),
                  pl.BlockSpec((4, POOL_W, 256), lambda i: (0, 0, 0)), pl.BlockSpec((D, D), const2),
                  pl.BlockSpec((D, D), const2), ANY],
        out_specs=[pl.BlockSpec((ts, 2 * D), row), pl.BlockSpec((ts, D), row), pl.BlockSpec((ts, D), row),
                   pl.BlockSpec((ts, POOL_W), row), pl.BlockSpec((ts, D), row), pl.BlockSpec((ts, D), row),
                   pl.BlockSpec((1, 2 * D), const2), pl.BlockSpec((1, HV), const2)],
        out_shape=[jax.ShapeDtypeStruct((s, N_INR), BF16), jax.ShapeDtypeStruct((s, D), BF16),
                   jax.ShapeDtypeStruct((s, D), BF16), jax.ShapeDtypeStruct((s, POOL_W), BF16),
                   jax.ShapeDtypeStruct((s, D), BF16), jax.ShapeDtypeStruct((s, D), BF16),
                   jax.ShapeDtypeStruct((1, 2 * D), F32), jax.ShapeDtypeStruct((1, HV), F32)],
        compiler_params=_cp("arbitrary"),
    )(dx1b, zr, zr, zr, yp, yg, o, bgate, ghead, wpp, wgla, wout, after)


HALO = 16
CCH = D_FF // 2


def _conv_taps(u_ref, halo_ref, cs, first, ts):
    u = u_ref[:, cs].astype(F32)
    hal = halo_ref[:, cs].astype(F32)
    h1 = jnp.where(first, 0.0, _pick_row(hal, HALO - 1))
    h2 = jnp.where(first, 0.0, _pick_row(hal, HALO - 2))
    row8 = _rows((8, u.shape[1]))
    r1, r2 = pltpu.roll(u, 1, 0), pltpu.roll(u, 2, 0)
    r1 = jnp.concatenate([jnp.where(row8 == 0, h1, r1[:8]), r1[8:]], axis=0)
    r2 = jnp.concatenate([jnp.where(row8 == 0, h2, jnp.where(row8 == 1, h1, r2[:8])), r2[8:]], axis=0)
    return u, r1, r2


def _ffn_down_loss(u, x1, tgt, wconv, bconv, wdown, gfin, ts):
    s = x1.shape[0]

    def body(u_ref, halo_ref, x1_ref, t_ref, wc_ref, bc_ref, wd_ref, gf_ref, a_ref, c_ref, dx_ref, dxb_ref, ls_ref,
             dgf_ref):
        i = pl.program_id(0)

        @pl.when(i == 0)
        def _():
            ls_ref[...] = jnp.zeros_like(ls_ref)
            dgf_ref[...] = jnp.zeros_like(dgf_ref)

        first = i == 0
        acc = x1_ref[...]
        for hf in range(D_FF // CCH):
            cg = slice(hf * CCH, (hf + 1) * CCH)
            cv = slice(D_FF + hf * CCH, D_FF + (hf + 1) * CCH)
            vals = []
            for cs in (cg, cv):
                u0, u1, u2 = _conv_taps(u_ref, halo_ref, cs, first, ts)
                vals.append(bc_ref[:, cs] + wc_ref[0:1, cs] * u2 + wc_ref[1:2, cs] * u1 + wc_ref[2:3, cs] * u0)
                c_ref[:, cs] = vals[-1].astype(BF16)
            a = (vals[0] * _sigmoid(vals[0]) * vals[1]).astype(BF16)
            a_ref[:, cg] = a
            acc = acc + _dot(a, wd_ref[cg, :])
        r = lax.rsqrt(jnp.mean(acc * acc, axis=-1, keepdims=True) + EPS)
        xh = acc * r
        gf = gf_ref[...]
        err = xh * gf - t_ref[...]
        ls_ref[...] += (0.5 / D) * jnp.sum(jnp.sum(err * err, axis=-1, keepdims=True), axis=0, keepdims=True)
        dy = err * (1.0 / D)
        dgf_ref[...] += jnp.sum(dy * xh, axis=0, keepdims=True)
        dxh = dy * gf
        dx = r * (dxh - xh * jnp.mean(dxh * xh, axis=-1, keepdims=True))
        dx_ref[...] = dx
        dxb_ref[...] = dx.astype(BF16)

    row = lambda i: (i, 0)
    const2 = lambda i: (0, 0)
    return pl.pallas_call(
        body, name="ffn_down_loss", grid=(s // ts,),
        in_specs=[pl.BlockSpec((ts, N_UP), row),
                  pl.BlockSpec((HALO, N_UP), lambda i: (jnp.maximum(i * (ts // HALO) - 1, 0), 0)),
                  pl.BlockSpec((ts, D), row), pl.BlockSpec((ts, D), row), pl.BlockSpec((3, N_UP), const2),
                  pl.BlockSpec((1, N_UP), const2), pl.BlockSpec((D_FF, D), const2), pl.BlockSpec((1, D), const2)],
        out_specs=[pl.BlockSpec((ts, D_FF), row), pl.BlockSpec((ts, N_UP), row), pl.BlockSpec((ts, D), row),
                   pl.BlockSpec((ts, D), row), pl.BlockSpec((1, 128), const2), pl.BlockSpec((1, D), const2)],
        out_shape=[jax.ShapeDtypeStruct((s, D_FF), BF16), jax.ShapeDtypeStruct((s, N_UP), BF16),
                   jax.ShapeDtypeStruct((s, D), F32), jax.ShapeDtypeStruct((s, D), BF16),
                   jax.ShapeDtypeStruct((1, 128), F32), jax.ShapeDtypeStruct((1, D), F32)],
        compiler_params=_cp("arbitrary"),
    )(u, u, x1, tgt, wconv, bconv, wdown, gfin)


def _ffn_bwd(dx2b, u, c, wconv, wdown, ts):
    s = dx2b.shape[0]
    nt = s // ts

    def body(dx_ref, u_ref, c_ref, wc_ref, wd_ref, du_ref, db_ref, dw_ref, nxt_ref):
        @pl.when(pl.program_id(0) == 0)
        def _():
            db_ref[...] = jnp.zeros_like(db_ref)
            dw_ref[...] = jnp.zeros_like(dw_ref)
            nxt_ref[...] = jnp.zeros_like(nxt_ref)

        dxv = dx_ref[...]
        row8 = _rows((8, CCH))
        for hf in range(D_FF // CCH):
            cg = slice(hf * CCH, (hf + 1) * CCH)
            cv = slice(D_FF + hf * CCH, D_FF + (hf + 1) * CCH)
            da = _dot_nt(dxv, wd_ref[cg, :])
            gate = c_ref[:, cg].astype(F32)
            val = c_ref[:, cv].astype(F32)
            sg = _sigmoid(gate)
            dcs = (da * val * sg * (1.0 + gate * (1.0 - sg)), da * gate * sg)
            for cs, dc in zip((cg, cv), dcs):
                n1 = nxt_ref[0:1, cs]
                n2 = nxt_ref[1:2, cs]
                r1, r2 = pltpu.roll(dc, ts - 1, 0), pltpu.roll(dc, ts - 2, 0)
                f1 = jnp.concatenate([r1[:ts - 8], jnp.where(row8 == 7, n1, r1[ts - 8:])], axis=0)
                f2 = jnp.concatenate([r2[:ts - 8], jnp.where(row8 == 7, n2, jnp.where(row8 == 6, n1, r2[ts - 8:]))], axis=0)
                uv = u_ref[:, cs].astype(F32)
                db_ref[:, cs] += jnp.sum(dc, axis=0, keepdims=True)
                dw_ref[0:1, cs] += jnp.sum(f2 * uv, axis=0, keepdims=True)
                dw_ref[1:2, cs] += jnp.sum(f1 * uv, axis=0, keepdims=True)
                dw_ref[2:3, cs] += jnp.sum(dc * uv, axis=0, keepdims=True)
                du_ref[:, cs] = (wc_ref[2:3, cs] * dc + wc_ref[1:2, cs] * f1 + wc_ref[0:1, cs] * f2).astype(BF16)
                nxt_ref[:, cs] = dc[0:8, :]

    rev = lambda i: (nt - 1 - i, 0)
    const2 = lambda i: (0, 0)
    return pl.pallas_call(
        body, name="ffn_bwd", grid=(nt,),
        in_specs=[pl.BlockSpec((ts, D), rev), pl.BlockSpec((ts, N_UP), rev), pl.BlockSpec((ts, N_UP), rev),
                  pl.BlockSpec((3, N_UP), const2), pl.BlockSpec((D_FF, D), const2)],
        out_specs=[pl.BlockSpec((ts, N_UP), rev), pl.BlockSpec((1, N_UP), const2), pl.BlockSpec((3, N_UP), const2)],
        out_shape=[jax.ShapeDtypeStruct((s, N_UP), BF16), jax.ShapeDtypeStruct((1, N_UP), F32),
                   jax.ShapeDtypeStruct((3, N_UP), F32)],
        scratch_shapes=[pltpu.VMEM((8, N_UP), F32)],
        compiler_params=_cp("arbitrary"),
    )(dx2b, u, c, wconv, wdown)


ANY = pl.BlockSpec(memory_space=pl.ANY)


def _place():
    x, y, c = lax.axis_index("x"), lax.axis_index("y"), lax.axis_index("c")
    chips = [(1 - x, y), (x, 1 - y), (1 - x, 1 - y)]
    return x, y, c, chips


def _half(shape, c, axis):
    size = shape[axis] // 2
    cut = pl.ds(pl.multiple_of(c * size, 8 if axis == 0 else 128), size)
    return (cut, slice(None)) if axis == 0 else (slice(None), cut)


def _half_shape(shape, axis):
    return (shape[0] // 2, shape[1]) if axis == 0 else (shape[0], shape[1] // 2)


def _remote(src, dst, send_sems, recv_sems, k, to):
    return pltpu.make_async_remote_copy(src_ref=src, dst_ref=dst, send_sem=send_sems.at[k], recv_sem=recv_sems.at[k],
                                        device_id=to, device_id_type=MESH)


def _sibling_exchange(grads, axes, smalls, name):
    nb = len(grads)
    n = nb + len(smalls)

    def body(*refs):
        ins, outs = refs[:n], refs[n:2 * n]
        send_sems, recv_sems = refs[2 * n:]
        x, y, c, _ = _place()
        sib = (x, y, 1 - c)
        cps = []
        for a in range(nb):
            theirs = _half(grads[a].shape[1:], 1 - c, axes[a])
            cps.append(_remote(ins[a].at[(slice(None),) + theirs], outs[a], send_sems, recv_sems, a, sib))
        for a in range(nb, n):
            cps.append(_remote(ins[a], outs[a], send_sems, recv_sems, a, sib))
        for cp in cps:
            cp.start()
        for cp in cps:
            cp.wait()

    out_shape = [jax.ShapeDtypeStruct((4,) + _half_shape(g.shape[1:], ax), g.dtype) for g, ax in zip(grads, axes)]
    out_shape += [jax.ShapeDtypeStruct(a.shape, F32) for a in smalls]
    return pl.pallas_call(
        body, name=name, in_specs=[ANY] * n, out_specs=[ANY] * n, out_shape=out_shape,
        scratch_shapes=[pltpu.SemaphoreType.DMA((n,)), pltpu.SemaphoreType.DMA((n,))],
        compiler_params=pltpu.CompilerParams(has_side_effects=True),
    )(*grads, *smalls)


def _gather_share(lands, axes, name):
    n = len(lands)

    def body(*refs):
        outs = refs[n:2 * n]
        send_sems, recv_sems = refs[2 * n:]
        x, y, c, chips = _place()
        sib = (x, y, 1 - c)
        cps = []
        for a in range(n):
            mine = _half(lands[a].shape[1:], c, axes[a])
            for k, ch in enumerate(chips):
                landed = outs[a].at[(2 * ch[0] + ch[1],) + mine]
                cps.append(_remote(landed, landed, send_sems, recv_sems, 3 * a + k, sib))
        for cp in cps:
            cp.start()
        for a in range(n):
            other = _half(lands[a].shape[1:], 1 - c, axes[a])
            for k, ch in enumerate(chips):
                landed = outs[a].at[(2 * ch[0] + ch[1],) + other]
                _remote(landed, landed, send_sems, recv_sems, 3 * a + k, sib).wait_recv()
        for cp in cps:
            cp.wait_send()

    return pl.pallas_call(
        body, name=name, in_specs=[ANY] * n, out_specs=[ANY] * n,
        out_shape=[jax.ShapeDtypeStruct(a.shape, a.dtype) for a in lands],
        input_output_aliases={a: a for a in range(n)},
        scratch_shapes=[pltpu.SemaphoreType.DMA((3 * n,)), pltpu.SemaphoreType.DMA((3 * n,))],
        compiler_params=pltpu.CompilerParams(has_side_effects=True),
    )(*lands)


def _sibling_share(halves, name):
    n = len(halves)

    def body(*refs):
        ins, outs = refs[:n], refs[n:2 * n]
        send_sems, recv_sems = refs[2 * n:]
        x, y, c, _ = _place()
        cps = [_remote(ins[a], outs[a], send_sems, recv_sems, a, (x, y, 1 - c)) for a in range(n)]
        for cp in cps:
            cp.start()
        for cp in cps:
            cp.wait()

    return pl.pallas_call(
        body, name=name, in_specs=[ANY] * n, out_specs=[ANY] * n,
        out_shape=[jax.ShapeDtypeStruct(h.shape, F32) for h in halves],
        scratch_shapes=[pltpu.SemaphoreType.DMA((n,)), pltpu.SemaphoreType.DMA((n,))],
        compiler_params=pltpu.CompilerParams(has_side_effects=True),
    )(*halves)


HBM = pl.BlockSpec(memory_space=pltpu.HBM)
SEM = pl.BlockSpec(memory_space=pltpu.SEMAPHORE)
DATAFLOW = pltpu.SideEffectType.DATAFLOW_SIDE_EFFECTING


def _split_start(name, srcs, land_shapes, plan, n_copies, after):
    lands = [lax.empty(*ls) if isinstance(ls, tuple) else ls for ls in land_shapes]
    bufs = list(srcs) + lands
    nb, ns = len(bufs), len(srcs)

    def body(*refs):
        send_sems, recv_sems, token = refs[nb + 1], refs[nb + 2], refs[-1]
        for k, (src, dst, to) in enumerate(plan(refs[:ns], refs[ns:nb])):
            _remote(src, dst, send_sems, recv_sems, k, to).start()
        token[...] = jnp.zeros_like(token)

    res = pl.pallas_call(
        body, name=name,
        out_shape=(pltpu.SemaphoreType.DMA((n_copies,)), pltpu.SemaphoreType.DMA((n_copies,)),
                   *[pltpu.HBM(b.shape, b.dtype) for b in bufs], jax.ShapeDtypeStruct((8, 128), F32)),
        in_specs=[HBM] * nb + [ANY],
        out_specs=(SEM, SEM, *[HBM] * nb, pl.BlockSpec(memory_space=pltpu.VMEM)),
        input_output_aliases={i: 2 + i for i in range(nb)},
        compiler_params=pltpu.CompilerParams(has_side_effects=DATAFLOW),
    )(*[pltpu.with_memory_space_constraint(b, pltpu.HBM) for b in bufs], after)
    return (res[0], res[1], list(res[2:2 + nb])), res[-1]


def _split_wait(name, handle, n_srcs, plan, after):
    send_sems, recv_sems, bufs = handle
    nb = len(bufs)

    def body(*refs):
        sends, recvs = refs[nb], refs[nb + 1]
        for k, (src, dst, to) in enumerate(plan(refs[:n_srcs], refs[n_srcs:nb])):
            cp = _remote(src, dst, sends, recvs, k, to)
            cp.wait_send()
            cp.wait_recv()

    res = pl.pallas_call(
        body, name=name, out_shape=[pltpu.HBM(b.shape, b.dtype) for b in bufs],
        in_specs=[HBM] * nb + [SEM, SEM, ANY], out_specs=[HBM] * nb,
        input_output_aliases={i: i for i in range(nb)},
        compiler_params=pltpu.CompilerParams(has_side_effects=DATAFLOW),
    )(*bufs, send_sems, recv_sems, after)
    return list(res[:n_srcs]), list(res[n_srcs:])


def _gather_plan(shapes, axes, n_whole=0):
    def plan(srcs, lands):
        x, y, c, chips = _place()
        out = []
        for a, (shape, axis) in enumerate(zip(shapes, axes)):
            mine = _half(shape, c, axis)
            for ch in chips:
                out.append((srcs[a].at[mine], lands[a].at[(2 * x + y,) + mine], (ch[0], ch[1], c)))
        for a in range(len(shapes), len(shapes) + n_whole):
            for ch in chips:
                out.append((srcs[a], lands[a].at[2 * x + y], (ch[0], ch[1], c)))
        return out
    return plan


def _share_plan(shapes, axes):
    def plan(srcs, lands):
        x, y, c, chips = _place()
        out = []
        for a, (shape, axis) in enumerate(zip(shapes, axes)):
            mine = _half(shape, c, axis)
            for ch in chips:
                landed = lands[a].at[(2 * ch[0] + ch[1],) + mine]
                out.append((landed, landed, (x, y, 1 - c)))
        return out
    return plan


def _sibling_plan(shapes, axes):
    def plan(srcs, lands):
        x, y, c, _ = _place()
        return [(srcs[a].at[(slice(None),) + _half(shape, 1 - c, axis)], lands[a], (x, y, 1 - c))
                for a, (shape, axis) in enumerate(zip(shapes, axes))]
    return plan


def _whole_to_sibling_plan(n):
    def plan(srcs, lands):
        x, y, c, _ = _place()
        return [(srcs[a], lands[a], (x, y, 1 - c)) for a in range(n)]
    return plan


def _reduce_plan(n_big, n_small):
    def plan(srcs, lands):
        x, y, c, chips = _place()
        out = []
        for a in range(n_big):
            for k, ch in enumerate(chips):
                out.append((srcs[a].at[2 * ch[0] + ch[1]], lands[a].at[k], (ch[0], ch[1], c)))
        for a in range(n_big, n_big + n_small):
            for ch in chips:
                out.append((srcs[a], lands[a].at[2 * x + y], (ch[0], ch[1], c)))
        return out
    return plan


def _row_tile(rows, cols, mult):
    best = mult
    for t in range(mult, rows + 1, mult):
        if rows % t == 0 and t * cols * 4 <= (2 << 20):
            best = t
    return best if rows % best == 0 else rows


COL_TILE = 256


def _half_tiling(hshape, axis, mult):
    hr, hc = hshape
    if axis == 0:
        tr = _row_tile(hr, hc, mult)
        return tr, hc, hr // tr
    return hr, COL_TILE, hc // COL_TILE


def _tile_idx(axis, t):
    return (t, 0) if axis == 0 else (0, t)


def _chip_partial(place, g, t, axis, name):
    hshape = t.shape[1:]
    br, bc, nt = _half_tiling(hshape, axis, 16)

    def body(pl_ref, g_ref, t_ref, pf_ref, pb_ref):
        v = g_ref[...].astype(F32) + t_ref[...].astype(F32)
        pb_ref[...] = v.astype(BF16)

        @pl.when(pl.program_id(1) == pl_ref[0])
        def _():
            pf_ref[...] = v

    blk = (None, br, bc)
    return pl.pallas_call(
        body, name=name,
        grid_spec=pltpu.PrefetchScalarGridSpec(
            num_scalar_prefetch=1, grid=(nt, 4),
            in_specs=[pl.BlockSpec(blk, lambda i, j, p: (j,) + _tile_idx(axis, p[1] * nt + i)),
                      pl.BlockSpec(blk, lambda i, j, p: (j,) + _tile_idx(axis, i))],
            out_specs=[pl.BlockSpec((br, bc), lambda i, j, p: _tile_idx(axis, i)),
                       pl.BlockSpec(blk, lambda i, j, p: (j,) + _tile_idx(axis, i))]),
        out_shape=[jax.ShapeDtypeStruct(hshape, F32), jax.ShapeDtypeStruct((4,) + hshape, BF16)],
        compiler_params=_cp("arbitrary", "arbitrary"),
    )(place, g, t)


def _finish_half(pf, rb, axis, name):
    hshape = pf.shape
    br, bc, nt = _half_tiling(hshape, axis, 16)

    def body(pf_ref, rb_ref, o_ref):
        o_ref[...] = ((pf_ref[...] + rb_ref[0].astype(F32)) + rb_ref[1].astype(F32)) + rb_ref[2].astype(F32)

    return pl.pallas_call(
        body, name=name, grid=(nt,),
        in_specs=[pl.BlockSpec((br, bc), lambda i: _tile_idx(axis, i)),
                  pl.BlockSpec((3, br, bc), lambda i: (0,) + _tile_idx(axis, i))],
        out_specs=pl.BlockSpec((br, bc), lambda i: _tile_idx(axis, i)),
        out_shape=jax.ShapeDtypeStruct(hshape, F32),
        compiler_params=_cp("arbitrary"),
    )(pf, rb)


def _adam_math(w, g, m, v):
    m = ADAM_B1 * m + (1.0 - ADAM_B1) * g
    v = ADAM_B2 * v + (1.0 - ADAM_B2) * (g * g)
    m_hat = m / (1.0 - ADAM_B1 ** ADAM_STEP)
    v_hat = v / (1.0 - ADAM_B2 ** ADAM_STEP)
    return -ADAM_LR * (m_hat / (jnp.sqrt(v_hat) + ADAM_EPS) + ADAM_WD * w), m, v


def _adam_halves(place, w, mine, theirs, m, v, axis, name):
    br, bc, nt = _half_tiling(mine.shape, axis, 8)

    def body(pl_ref, w_ref, a_ref, b_ref, m_ref, v_ref, g_ref, d_ref, mo_ref, vo_ref):
        is_mine = pl.program_id(0) // nt == pl_ref[1]
        g = jnp.where(is_mine, a_ref[...], b_ref[...])
        d, mn, vn = _adam_math(w_ref[...], g, m_ref[...], v_ref[...])
        g_ref[...] = g
        d_ref[...] = d
        mo_ref[...] = mn
        vo_ref[...] = vn

    full = pl.BlockSpec((br, bc), lambda i, p: _tile_idx(axis, i))
    mine_spec = pl.BlockSpec((br, bc), lambda i, p: _tile_idx(axis, jnp.where(i // nt == p[1], i % nt, nt - 1)))
    theirs_spec = pl.BlockSpec((br, bc), lambda i, p: _tile_idx(axis, jnp.where(i // nt == p[1], 0, i % nt)))
    return pl.pallas_call(
        body, name=name,
        grid_spec=pltpu.PrefetchScalarGridSpec(
            num_scalar_prefetch=1, grid=(2 * nt,), in_specs=[full, mine_spec, theirs_spec, full, full],
            out_specs=[full] * 4),
        out_shape=[jax.ShapeDtypeStruct(w.shape, F32)] * 4, compiler_params=_cp("arbitrary"),
    )(place, w, mine, theirs, m, v)


def _add_many(xs, ys, name):
    n = len(xs)

    def body(*refs):
        for i in range(n):
            refs[2 * n + i][...] = refs[i][...] + refs[n + i][...]

    return pl.pallas_call(body, name=name, out_shape=[jax.ShapeDtypeStruct(a.shape, F32) for a in xs])(*xs, *ys)


def _adam_small(place, owns, landed, ws, ms, vs, widths):
    n, nw = len(owns), len(ws)

    def body(pl_ref, *refs):
        own_r, land_r = refs[:n], refs[n:2 * n]
        w_r, m_r, v_r = (refs[2 * n + k * nw:2 * n + (k + 1) * nw] for k in range(3))
        outs = refs[2 * n + 3 * nw:]
        g_o, d_o, m_o, v_o = outs[:n], outs[n:n + nw], outs[n + nw:n + 2 * nw], outs[n + 2 * nw:]
        for me in range(4):
            @pl.when(pl_ref[0] == me)
            def _(me=me):
                for i in range(n):
                    p = [own_r[i][...] if k == me else land_r[i][k] for k in range(4)]
                    g = ((p[0] + p[1]) + p[2]) + p[3]
                    if i < nw and widths[i]:
                        g = g[:, me * widths[i]:(me + 1) * widths[i]]
                    g_o[i][...] = g
                    if i < nw:
                        d, mn, vn = _adam_math(w_r[i][...], g, m_r[i][...], v_r[i][...])
                        d_o[i][...] = d
                        m_o[i][...] = mn
                        v_o[i][...] = vn

    g_shapes = [jax.ShapeDtypeStruct(ws[i].shape if i < nw else owns[i].shape, F32) for i in range(n)]
    w_shapes = [jax.ShapeDtypeStruct(w.shape, F32) for w in ws]
    whole = lambda a: pl.BlockSpec(a.shape, lambda i, p, nd=len(a.shape): (0,) * nd)
    ins = list(owns) + list(landed) + list(ws) + list(ms) + list(vs)
    out_shape = g_shapes + w_shapes * 3
    out = pl.pallas_call(
        body, name="adam_small",
        grid_spec=pltpu.PrefetchScalarGridSpec(num_scalar_prefetch=1, grid=(1,), in_specs=[whole(a) for a in ins],
                                               out_specs=[whole(a) for a in out_shape]),
        out_shape=out_shape, compiler_params=_cp("arbitrary"),
    )(place, *ins)
    return out[:n], out[n:n + nw], out[n + nw:n + 2 * nw], out[n + 2 * nw:]


def kernel(x, g_mix, w_in, b_gate, w_gk_up, b_gk, w_pool_grp, pool_scale, g_gla_head, w_pool_proj, w_gla_proj, w_out, g_ffn, w_up, w_conv, b_conv, w_down, g_final, loss_target, m_g_mix, m_w_in, m_b_gate, m_w_gk_up, m_b_gk, m_w_pool_grp, m_pool_scale, m_g_gla_head, m_w_pool_proj, m_w_gla_proj, m_w_out, m_g_ffn, m_w_up, m_w_conv, m_b_conv, m_w_down, m_g_final, v_g_mix, v_w_in, v_b_gate, v_w_gk_up, v_b_gk, v_w_pool_grp, v_pool_scale, v_g_gla_head, v_w_pool_proj, v_w_gla_proj, v_w_out, v_g_ffn, v_w_up, v_w_conv, v_b_conv, v_w_down, v_g_final):
    s = x.shape[1]
    ts = min(s, 512)
    tm = min(s, 256)
    cx, cy, cc = lax.axis_index("x"), lax.axis_index("y"), lax.axis_index("c")
    chip = 2 * cx + cy
    place = jnp.stack([chip, cc]).astype(jnp.int32)

    big_names = ("w_in", "w_pool_proj", "w_gla_proj", "w_out", "w_up", "w_down")
    axes = (1, 0, 0, 0, 0, 0)
    shards = dict(w_in=jnp.transpose(w_in[0]), w_pool_proj=w_pool_proj[0], w_gla_proj=w_gla_proj[0], w_out=w_out[0],
                  w_up=w_up[0], w_down=w_down[0])
    def fill_own(lands, mine):
        return [lax.dynamic_update_slice(g, o_[None], (chip, 0, 0)) for g, o_ in zip(lands, mine)]

    def gather_start(tag, halves, group_axes, whole, after):
        plan = _gather_plan([o_.shape for o_ in halves], group_axes, len(whole))
        srcs = list(halves) + list(whole)
        handle, token = _split_start("gather_" + tag + "_start", srcs, [((4,) + o_.shape, o_.dtype) for o_ in srcs], plan,
                                     3 * len(srcs), after)
        return (handle, plan, len(halves), len(srcs), group_axes), token

    def gather_finish(tag, started, after):
        handle, plan, n_halves, n, group_axes = started
        mine, lands = _split_wait("gather_" + tag + "_wait", handle, n, plan, after)
        lands[:n_halves] = _gather_share(lands[:n_halves], group_axes, "gather_" + tag + "_share")
        return fill_own(lands, mine)

    in_w, tok = gather_start("in", [jnp.transpose(w_in[0].astype(BF16))], axes[:1], [], g_mix)
    zero = tok[0, 0]
    own = [(shards[n] + zero).astype(BF16) for n in big_names[1:]]
    mix_w, tok = gather_start("mix", own[0:3], axes[1:4], [w_gk_up[0] + zero, w_conv[0] + zero], tok)
    up_w, tok = gather_start("up", own[3:4], axes[4:5], [], tok)
    down_w, tok = gather_start("down", own[4:5], axes[5:6], [], tok)

    def forward_start(tag, started, after):
        handle, plan, _, n, group_axes = started
        mine, lands = _split_wait("gather_" + tag + "_wait", handle, n, plan, after)
        plan = _share_plan([o_.shape for o_ in mine], group_axes)
        share, token = _split_start("gather_" + tag + "_share_start", [], lands, plan, 3 * n, after)
        return (share, plan, mine), token

    def forward_done(tag, forwarded, after):
        share, plan, mine = forwarded
        return fill_own(_split_wait("gather_" + tag + "_share_wait", share, 0, plan, after)[1], mine)
    xs, tgt = x[0], loss_target[0]
    wgrp = w_pool_grp[0]
    h = _rmsnorm(xs, g_mix, tok, "norm_mix", ts)
    m_in_t, v_in_t = jnp.transpose(m_w_in[0]), jnp.transpose(v_w_in[0])
    h, m_in_t, v_in_t = lax.optimization_barrier((h, m_in_t, v_in_t))
    w_in_t = gather_finish("in", in_w, h)[0].reshape(N_IN, D)
    nsh = N_IN // 4

    zr = _in_proj(h, w_in_t, PROJ_TILE)
    p, pp = _pool_fwd(zr, wgrp, pool_scale)
    wpp, wgla, wout, wgk4, wconv4 = gather_finish("mix", mix_w, pp)
    wgla, wout = wgla.reshape(D, D), wout.reshape(D, D)
    wgk_full = jnp.transpose(wgk4, (1, 0, 2)).reshape(GATE_RANK, 512)
    wconv_full = jnp.transpose(wconv4, (1, 0, 2)).reshape(3, N_UP)
    wgk_pad = jnp.concatenate([wgk_full, jnp.zeros((128 - GATE_RANK, 512), F32)], axis=0)
    o, og, sp = _gla_fwd(zr, wgk_pad, b_gk, g_gla_head, ts)
    up_f, tok = forward_start("up", up_w, og)
    x1, mixed, yp, yg, h2 = _merge_fwd(xs, zr, pp, og, b_gate, wpp, wgla, wout, g_ffn, tok, ts)
    wup, = forward_done("up", up_f, x1)
    down_f, tok = forward_start("down", down_w, x1)
    u = _matmul_resident(h2, wup, tok, "ffn_up")
    wdown = forward_done("down", down_f, u)[0].reshape(D_FF, D)
    a, conv_out, dx2, dx2b, loss_part, dgfin = _ffn_down_loss(u, x1, tgt, wconv_full, b_conv, wdown,
                                                              g_final.reshape(1, D), tm)

    du, dbconv, dwconv = _ffn_bwd(dx2b, u, conv_out, wconv_full, wdown, tm)
    dw_down = _matmul_tn(a, dx2b, "dw_down", D, tm=D_FF // 2)
    dw_up = _matmul_tn(h2, du, "dw_up", UP_SHARD, shard_major=True)

    def exchange_start(tag, grads, group_axes, after):
        plan = _sibling_plan([g.shape[1:] for g in grads], group_axes)
        lands = [((4,) + _half_shape(g.shape[1:], ax), g.dtype) for g, ax in zip(grads, group_axes)]
        handle, token = _split_start("sibling_" + tag + "_start", grads, lands, plan, len(grads), after)
        return (handle, plan, len(grads)), token

    def partials(tag, names, group_axes, exchange, after):
        handle, plan, n = exchange
        mine, theirs = _split_wait("sibling_" + tag + "_wait", handle, n, plan, after)
        return zip(*[_chip_partial(place, g, t, ax, "chip_partial_" + nm)
                     for nm, ax, g, t in zip(names, group_axes, mine, theirs)])

    ffn_names, ffn_axes = ("w_up", "w_down"), (0, 0)
    ffn_x, token = exchange_start("ffn", [dw_up, dw_down.reshape(4, 704, D)], ffn_axes, du)
    dx1, dx1b, dgffn = _matmul_nt_normbwd(du, wup, x1, g_ffn, dx2, token, "ffn_up_bwd", ts)
    ffn_pf, ffn_pb = partials("ffn", ffn_names, ffn_axes, ffn_x, dx1b)
    ffn_plan = _reduce_plan(2, 0)
    ffn_handle, token = _split_start("reduce_ffn_start", ffn_pb, [((3,) + p.shape[1:], BF16) for p in ffn_pb],
                                     ffn_plan, 6, ffn_pf[0])

    dzr, dyp, dyg, dpp, do, dzog, dbgate, dghead = _merge_bwd(dx1b, zr, yp, yg, o, b_gate, g_gla_head, wpp, wgla, wout,
                                                             token, ts)
    dzr = lax.dynamic_update_slice(dzr, dzog, (0, OFF_OG))
    dw_out = _matmul_tn(mixed, dx1b, "dw_out", D, tm=512)
    dw_gla = _matmul_tn(og, dyg, "dw_gla", D, tm=512)
    dw_pp = _matmul_tn(pp, dyp, "dw_pp", 256, shard_major=True)

    out_names, out_axes = ("w_pool_proj", "w_gla_proj", "w_out"), (0, 0, 0)
    out_x, token = exchange_start("out", [dw_pp, dw_gla.reshape(4, 256, D), dw_out.reshape(4, 256, D)], out_axes, dpp)
    dzr, dwgrp, dscale = _pool_bwd(p, dpp, wgrp, pool_scale, token, dzr)
    out_pf, out_pb = partials("out", out_names, out_axes, out_x, dwgrp)
    out_plan = _reduce_plan(3, 0)
    out_handle, token = _split_start("reduce_out_start", out_pb, [((3,) + p_.shape[1:], BF16) for p_ in out_pb],
                                     out_plan, 9, out_pf[0])
    dq, dk, dzr, dgpre = _gla_bwd(zr, do, sp, wgk_pad, b_gk, token, dzr, ts)
    dzr, dwgk, dbgk = _gk_bwd(dgpre, zr, wgk_pad, dgpre, dzr, ts)
    dzr = lax.dynamic_update_slice(lax.dynamic_update_slice(dzr, dq, (0, OFF_Q)), dk, (0, OFF_K))
    dw_rt = _matmul_tn(dzr, h, "dw_in", D, tm=PROJ_TILE)

    def grad_rows(lo, hi):
        out = []
        for seg_lo, seg_hi, at in ((0, 1536, OFF_POOL), (1536, 3584, OFF_V), (3584, 3600, OFF_GK), (3600, N_IN, OFF_GATE)):
            a_, b_ = max(lo, seg_lo), min(hi, seg_hi)
            if a_ < b_:
                out.append(dw_rt[at + a_ - seg_lo:at + b_ - seg_lo])
        return jnp.concatenate(out, axis=0)

    dw_in_t = jnp.stack([grad_rows(j * nsh, (j + 1) * nsh) for j in range(4)])

    ms = dict(w_in=m_in_t, w_pool_proj=m_w_pool_proj[0], w_gla_proj=m_w_gla_proj[0], w_out=m_w_out[0],
              w_up=m_w_up[0], w_down=m_w_down[0])
    vs = dict(w_in=v_in_t, w_pool_proj=v_w_pool_proj[0], w_gla_proj=v_w_gla_proj[0], w_out=v_w_out[0],
              w_up=v_w_up[0], w_down=v_w_down[0])
    grad, delta, new_m, new_v = {}, {}, {}, {}

    def finish(names, group_axes, part_f, landed):
        return [_finish_half(pf, rb, ax, "finish_" + n) for n, ax, pf, rb in zip(names, group_axes, part_f, landed)]

    def update(names, group_axes, halves, sib_halves):
        for n, ax, mine, theirs in zip(names, group_axes, halves, sib_halves):
            res = _adam_halves(place, shards[n], mine, theirs, ms[n], vs[n], ax, "adam_" + n)
            if n == "w_in":
                res = [jnp.transpose(r_) for r_ in res]
            grad[n], delta[n], new_m[n], new_v[n] = [r_[None] for r_ in res]

    rest_names, rest_axes = ffn_names + out_names, ffn_axes + out_axes
    _, ffn_landed = _split_wait("reduce_ffn_wait", ffn_handle, 2, ffn_plan, dw_rt)
    _, out_landed = _split_wait("reduce_out_wait", out_handle, 3, out_plan, ffn_landed[0])
    rest_halves = finish(rest_names, rest_axes, ffn_pf + out_pf, ffn_landed + out_landed)
    rest_plan = _whole_to_sibling_plan(len(rest_halves))
    rest_share, token = _split_start("sibling_share_rest_start", rest_halves, [(h_.shape, F32) for h_ in rest_halves],
                                     rest_plan, len(rest_halves), out_landed[0])
    in_x, token = exchange_start("in", [dw_in_t], (1,), token)
    grad_x, _, dgmix = _matmul_nt_normbwd(dzr, w_in_t, xs, g_mix, dx1, token, "in_proj_bwd", ts, transposed=True)
    (in_pf,), (in_pb,) = partials("in", ("w_in",), (1,), in_x, grad_x)
    in_plan = _reduce_plan(1, 0)
    in_handle, token = _split_start("reduce_in_start", [in_pb], [((3,) + in_pb.shape[1:], BF16)], in_plan, 3, in_pf)
    small_names = ("g_mix", "b_gate", "w_gk_up", "b_gk", "w_pool_grp", "pool_scale", "g_gla_head", "g_ffn", "w_conv",
                   "b_conv", "g_final")
    small_mine = [dgmix, dbgate, dwgk[:GATE_RANK], dbgk, dwgrp.reshape(4 * 128, 128), dscale, dghead, dgffn, dwconv, dbconv,
                  dgfin, loss_part]
    small_sib = _sibling_exchange([], (), small_mine, "sibling_exchange_small")
    small_chip = _add_many(small_mine, small_sib, "chip_partial_small")
    small_plan = _reduce_plan(0, len(small_chip))
    small_handle, token = _split_start("reduce_small_start", small_chip, [((4,) + a_.shape, F32) for a_ in small_chip],
                                       small_plan, 3 * len(small_chip), token)

    rest_halves, rest_sib = _split_wait("sibling_share_rest_wait", rest_share, len(rest_halves), rest_plan, token)
    update(rest_names, rest_axes, rest_halves, rest_sib)
    updated = lax.optimization_barrier([delta[n] for n in rest_names])
    _, in_landed = _split_wait("reduce_in_wait", in_handle, 1, in_plan, updated[0])
    in_halves = finish(("w_in",), (1,), (in_pf,), in_landed)
    update(("w_in",), (1,), in_halves, _sibling_share(in_halves, "sibling_share_in"))
    small_sent, small_landed = _split_wait("reduce_small_wait", small_handle, len(small_chip), small_plan, delta["w_in"])
    given = dict(g_mix=(g_mix, m_g_mix, v_g_mix), b_gate=(b_gate, m_b_gate, v_b_gate), w_gk_up=(w_gk_up, m_w_gk_up, v_w_gk_up),
                 b_gk=(b_gk, m_b_gk, v_b_gk), w_pool_grp=(w_pool_grp, m_w_pool_grp, v_w_pool_grp),
                 pool_scale=(pool_scale, m_pool_scale, v_pool_scale), g_gla_head=(g_gla_head, m_g_gla_head, v_g_gla_head),
                 g_ffn=(g_ffn, m_g_ffn, v_g_ffn), w_conv=(w_conv, m_w_conv, v_w_conv), b_conv=(b_conv, m_b_conv, v_b_conv),
                 g_final=(g_final, m_g_final, v_g_final))
    flat2 = lambda a: a.reshape(-1, a.shape[-1])
    widths = [dict(w_gk_up=HK, w_conv=UP_SHARD).get(n) for n in small_names]
    totals, ds, mo, vo = _adam_small(place, small_sent, small_landed, *[[flat2(given[n][k]) for n in small_names] for k in range(3)],
                                     widths)
    loss = totals[-1][0, 0]
    for i, n in enumerate(small_names):
        shp = given[n][0].shape
        grad[n], delta[n], new_m[n], new_v[n] = [r_.reshape(shp) for r_ in (totals[i], ds[i], mo[i], vo[i])]

    order = ("g_mix", "w_in", "b_gate", "w_gk_up", "b_gk", "w_pool_grp", "pool_scale", "g_gla_head", "w_pool_proj",
             "w_gla_proj", "w_out", "g_ffn", "w_up", "w_conv", "b_conv", "w_down", "g_final")
    return (loss, grad_x[None], *[grad[n] for n in order], *[delta[n] for n in order], *[new_m[n] for n in order],
            *[new_v[n] for n in order])
```

```python
import jax
import jax.numpy as jnp
from jax import lax
from jax.experimental import pallas as pl
from jax.experimental.pallas import tpu as pltpu

F32 = jnp.float32
BF16 = jnp.bfloat16
MESH = pl.DeviceIdType.MESH

D = 1024
EPS = 1e-6
CHUNK = 64
POOL_W = 512
POOL_WINDOWS = (2, 4, 8, 16)
HEADS = 4
HK = 128
HV = 256
GATE_RANK = 16
D_FF = 2816
N_UP = 2 * D_FF
N_IN = 5648
QSCALE = HK ** -0.5
N_INR = 5760
OFF_GATE, OFF_V, OFF_OG, OFF_POOL, OFF_Q, OFF_K, OFF_GK = 0, 2048, 3072, 4096, 4608, 5120, 5632

ADAM_LR, ADAM_B1, ADAM_B2, ADAM_EPS, ADAM_WD, ADAM_STEP = 0.001, 0.9, 0.999, 1e-08, 0.01, 10

VMEM_LIMIT = 56 * 1024 * 1024
PROJ_TILE = N_INR // 5
UP_SHARD = N_UP // 4


def _cp(*sem):
    return pltpu.CompilerParams(dimension_semantics=sem if sem else None, vmem_limit_bytes=VMEM_LIMIT)


def _dot(a, b):
    return jnp.dot(a, b, preferred_element_type=F32)


def _dot_nt(a, b):
    return lax.dot_general(a, b, (((1,), (1,)), ((), ())), preferred_element_type=F32)


def _dot_tn(a, b):
    return lax.dot_general(a, b, (((0,), (0,)), ((), ())), preferred_element_type=F32)


def _sigmoid(v):
    return 1.0 / (1.0 + jnp.exp(-v))


def _rows(shape):
    return lax.broadcasted_iota(jnp.int32, shape, 0)


def _pick_row(v, r):
    return jnp.sum(jnp.where(_rows(v.shape) == r, v, 0.0), axis=0, keepdims=True)


def _rmsnorm(x, g, after, name, ts):
    s = x.shape[0]

    def body(x_ref, g_ref, after_ref, h_ref):
        xv = x_ref[...]
        r = lax.rsqrt(jnp.mean(xv * xv, axis=-1, keepdims=True) + EPS)
        h_ref[...] = (xv * r * g_ref[...]).astype(BF16)

    return pl.pallas_call(
        body, name=name, grid=(s // ts,),
        in_specs=[pl.BlockSpec((ts, D), lambda i: (i, 0)), pl.BlockSpec((1, D), lambda i: (0, 0)), ANY],
        out_specs=pl.BlockSpec((ts, D), lambda i: (i, 0)), out_shape=jax.ShapeDtypeStruct((s, D), BF16),
        compiler_params=_cp("arbitrary"),
    )(x, g, after)


MM_ROWS = 512


def _matmul_resident(h, w, after, name):
    s = h.shape[0]
    nj, tn = w.shape[0], w.shape[2]
    rc = min(s, MM_ROWS)

    def body(h_ref, w_ref, after_ref, z_ref):
        for r0 in range(0, s, rc):
            z_ref[r0:r0 + rc, :] = _dot(h_ref[r0:r0 + rc, :], w_ref[...]).astype(BF16)

    return pl.pallas_call(
        body, name=name, grid=(nj,),
        in_specs=[pl.BlockSpec((s, D), lambda j: (0, 0)), pl.BlockSpec((None, D, tn), lambda j: (j, 0, 0)), ANY],
        out_specs=pl.BlockSpec((s, tn), lambda j: (0, j)), out_shape=jax.ShapeDtypeStruct((s, nj * tn), BF16),
        compiler_params=_cp("arbitrary"),
    )(h, w, after)


PROJ_PIECES = ((3600, 2048, OFF_GATE), (1536, 2048, OFF_V), (0, 1536, OFF_POOL), (3584, GATE_RANK, OFF_GK))


def _projection_copies(w_hbm, w_ref, sems):
    return [pltpu.make_async_copy(w_hbm.at[pl.ds(src, n)], w_ref.at[pl.ds(dst, n)], sems.at[i])
            for i, (src, n, dst) in enumerate(PROJ_PIECES)]


def _load_projection(w_hbm, w_ref, sems):
    cps = _projection_copies(w_hbm, w_ref, sems)
    for cp in cps:
        cp.start()
    w_ref[OFF_GK + GATE_RANK:, :] = jnp.zeros((N_INR - OFF_GK - GATE_RANK, D), BF16)
    for cp in cps:
        cp.wait()


def _in_proj(h, w_nat, tn):
    s = h.shape[0]
    rc = min(s, MM_ROWS)
    nj = N_INR // tn
    first_use = [dst // tn for _, _, dst in PROJ_PIECES]

    def body(h_ref, w_hbm, z_ref, w_ref, sems):
        j = pl.program_id(0)
        cps = _projection_copies(w_hbm, w_ref, sems)

        @pl.when(j == 0)
        def _():
            for cp in cps:
                cp.start()
            w_ref[OFF_GK + GATE_RANK:, :] = jnp.zeros((N_INR - OFF_GK - GATE_RANK, D), BF16)

        for step in range(nj):
            due = [cp for cp, at in zip(cps, first_use) if at == step]
            if due:
                @pl.when(j == step)
                def _(due=due):
                    for cp in due:
                        cp.wait()

        wt = w_ref[pl.ds(pl.multiple_of(j * tn, 128), tn), :]
        for r0 in range(0, s, rc):
            z_ref[r0:r0 + rc, :] = _dot_nt(h_ref[r0:r0 + rc, :], wt).astype(BF16)

    return pl.pallas_call(
        body, name="in_proj", grid=(nj,),
        in_specs=[pl.BlockSpec((s, D), lambda j: (0, 0)), ANY],
        out_specs=pl.BlockSpec((s, tn), lambda j: (0, j)), out_shape=jax.ShapeDtypeStruct((s, N_INR), BF16),
        scratch_shapes=[pltpu.VMEM((N_INR, D), BF16), pltpu.SemaphoreType.DMA((len(PROJ_PIECES),))],
        compiler_params=_cp("arbitrary"),
    )(h, w_nat)


def _matmul_nt_normbwd(dz, w, x, g, resid, after, name, ts, transposed=False):
    s = x.shape[0]
    w_vmem = (N_INR, D) if transposed else (D, w.shape[0] * w.shape[2])
    n_sems = len(PROJ_PIECES) if transposed else w.shape[0]

    def body(dz_ref, w_hbm, x_ref, g_ref, r_ref, after_ref, o_ref, ob_ref, dg_ref, w_ref, sems):
        @pl.when(pl.program_id(0) == 0)
        def _():
            if transposed:
                _load_projection(w_hbm, w_ref, sems)
            else:
                kc = w.shape[2]
                cps = [pltpu.make_async_copy(w_hbm.at[j], w_ref.at[:, pl.ds(j * kc, kc)], sems.at[j])
                       for j in range(w.shape[0])]
                for cp in cps:
                    cp.start()
                for cp in cps:
                    cp.wait()
            dg_ref[...] = jnp.zeros_like(dg_ref)

        dh = _dot(dz_ref[...], w_ref[...]) if transposed else _dot_nt(dz_ref[...], w_ref[...])
        xv = x_ref[...]
        r = lax.rsqrt(jnp.mean(xv * xv, axis=-1, keepdims=True) + EPS)
        xh = xv * r
        dg_ref[...] += jnp.sum(dh * xh, axis=0, keepdims=True)
        dxh = dh * g_ref[...]
        out = r_ref[...] + r * (dxh - xh * jnp.mean(dxh * xh, axis=-1, keepdims=True))
        o_ref[...] = out
        ob_ref[...] = out.astype(BF16)

    row = lambda i: (i, 0)
    kdim = dz.shape[1]
    return pl.pallas_call(
        body, name=name, grid=(s // ts,),
        in_specs=[pl.BlockSpec((ts, kdim), row), ANY, pl.BlockSpec((ts, D), row),
                  pl.BlockSpec((1, D), lambda i: (0, 0)), pl.BlockSpec((ts, D), row), ANY],
        out_specs=[pl.BlockSpec((ts, D), row), pl.BlockSpec((ts, D), row), pl.BlockSpec((1, D), lambda i: (0, 0))],
        out_shape=[jax.ShapeDtypeStruct((s, D), F32), jax.ShapeDtypeStruct((s, D), BF16),
                   jax.ShapeDtypeStruct((1, D), F32)],
        scratch_shapes=[pltpu.VMEM(w_vmem, BF16), pltpu.SemaphoreType.DMA((n_sems,))],
        compiler_params=_cp("arbitrary"),
    )(dz, w, x, g, resid, after)


def _matmul_tn(a, b, name, tn, shard_major=False, tm=None):
    s, m = a.shape
    n = b.shape[1]
    tm = m if tm is None else tm
    ni, nj = m // tm, n // tn

    def body(a_ref, b_ref, o_ref):
        o_ref[...] = _dot_tn(a_ref[...], b_ref[...]).astype(BF16)

    if shard_major:
        out_spec = pl.BlockSpec((None, tm, tn), lambda i, j: (j, i, 0))
        out_shape = jax.ShapeDtypeStruct((nj, m, tn), BF16)
    else:
        out_spec = pl.BlockSpec((tm, tn), lambda i, j: (i, j))
        out_shape = jax.ShapeDtypeStruct((m, n), BF16)
    return pl.pallas_call(
        body, name=name, grid=(ni, nj),
        in_specs=[pl.BlockSpec((s, tm), lambda i, j: (0, i)), pl.BlockSpec((s, tn), lambda i, j: (0, j))],
        out_specs=out_spec, out_shape=out_shape,
        compiler_params=_cp("arbitrary", "arbitrary"),
    )(a, b)


def _pool_fwd(zr, wgrp, scale):
    s = zr.shape[0]

    def body(u_ref, w_ref, sc_ref, p_ref, pp_ref):
        row = _rows((s, 128))
        for gi, win in enumerate(POOL_WINDOWS):
            cs = slice(gi * 128, (gi + 1) * 128)
            u = u_ref[:, cs].astype(F32)
            acc, k = u, 1
            while k < win:
                acc = acc + jnp.where(row >= k, pltpu.roll(acc, k, 0), 0.0)
                k *= 2
            cnt = jnp.minimum(row + 1, win).astype(F32)
            p = (acc / cnt - u).astype(BF16)
            p_ref[:, cs] = p
            pp_ref[:, cs] = (_dot(p, w_ref[gi].astype(BF16)) * sc_ref[:, cs]).astype(BF16)

    return pl.pallas_call(
        body, name="pool_fwd", grid=(1,),
        in_specs=[pl.BlockSpec((s, POOL_W), lambda i: (0, OFF_POOL // POOL_W)),
                  pl.BlockSpec((4, 128, 128), lambda i: (0, 0, 0)), pl.BlockSpec((1, POOL_W), lambda i: (0, 0))],
        out_specs=[pl.BlockSpec((s, POOL_W), lambda i: (0, 0))] * 2,
        out_shape=[jax.ShapeDtypeStruct((s, POOL_W), BF16)] * 2,
        compiler_params=_cp("arbitrary"),
    )(zr, wgrp, scale)


def _pool_bwd(p, dpp, wgrp, scale, after, dz):
    s = p.shape[0]

    def body(p_ref, dpp_ref, w_ref, sc_ref, after_ref, dz_in, dz_ref, dw_ref, dsc_ref):
        row = _rows((s, 128))
        for gi, win in enumerate(POOL_WINDOWS):
            cs = slice(gi * 128, (gi + 1) * 128)
            pv = p_ref[:, cs]
            wb = w_ref[gi].astype(BF16)
            dpp_v = dpp_ref[:, cs].astype(F32)
            dsc_ref[:, cs] = jnp.sum(dpp_v * _dot(pv, wb), axis=0, keepdims=True)
            dpm = (dpp_v * sc_ref[:, cs]).astype(BF16)
            dw_ref[gi] = _dot_tn(pv, dpm)
            dp = _dot_nt(dpm, wb)
            cnt = jnp.minimum(row + 1, win).astype(F32)
            acc, k = dp / cnt, 1
            while k < win:
                acc = acc + jnp.where(row < s - k, pltpu.roll(acc, s - k, 0), 0.0)
                k *= 2
            dz_ref[:, cs] = (acc - dp).astype(BF16)

    full = lambda i: (0, 0)
    return pl.pallas_call(
        body, name="pool_bwd", grid=(1,),
        in_specs=[pl.BlockSpec((s, POOL_W), full), pl.BlockSpec((s, POOL_W), full),
                  pl.BlockSpec((4, 128, 128), lambda i: (0, 0, 0)), pl.BlockSpec((1, POOL_W), full), ANY, ANY],
        out_specs=[pl.BlockSpec((s, POOL_W), lambda i: (0, OFF_POOL // POOL_W)),
                   pl.BlockSpec((4, 128, 128), lambda i: (0, 0, 0)), pl.BlockSpec((1, POOL_W), full)],
        out_shape=[jax.ShapeDtypeStruct(dz.shape, BF16), jax.ShapeDtypeStruct((4, 128, 128), F32),
                   jax.ShapeDtypeStruct((1, POOL_W), F32)],
        input_output_aliases={5: 0},
        compiler_params=_cp("arbitrary"),
    )(p, dpp, wgrp, scale, after, dz)


def _gla_decay(zgk_ref, wgk_ref, bgk_ref, rb):
    g = _dot(zgk_ref[...], wgk_ref[...].astype(BF16)) + bgk_ref[...]
    la = (jnp.minimum(g, 0.0) - jnp.log(1.0 + jnp.exp(-jnp.abs(g)))) * (1.0 / 16.0)
    rowm = _rows(la.shape) & (CHUNK - 1)
    bc, k = la, 1
    while k < CHUNK:
        bc = bc + jnp.where(rowm >= k, pltpu.roll(bc, k, 0), 0.0)
        k *= 2
    return g, jnp.exp(bc), jnp.exp(-bc)


GLA_HB = 4


def _gla_specs(rb, rmap):
    wk, wv = GLA_HB * HK, GLA_HB * HV
    return [pl.BlockSpec((rb, wk), lambda h, r: (rmap(h, r), OFF_Q // wk + h)),
            pl.BlockSpec((rb, wk), lambda h, r: (rmap(h, r), OFF_K // wk + h)),
            pl.BlockSpec((rb, wv), lambda h, r: (rmap(h, r), OFF_V // wv + h)),
            pl.BlockSpec((rb, 128), lambda h, r: (rmap(h, r), OFF_GK // 128))]


def _gla_fwd(zr, wgk, bgk, ghead, rb):
    s = zr.shape[0]
    nc = rb // CHUNK
    wk, wv = GLA_HB * HK, GLA_HB * HV

    def body(q_ref, k_ref, v_ref, zgk_ref, zog_ref, wgk_ref, bgk_ref, gh_ref, o_ref, og_ref, sp_ref, st_ref, kv_ref):
        @pl.when(pl.program_id(1) == 0)
        def _():
            st_ref[...] = jnp.zeros_like(st_ref)

        _, e_pos, e_neg = _gla_decay(zgk_ref, wgk_ref, bgk_ref, rb)
        lower = _rows((CHUNK, CHUNK)) >= lax.broadcasted_iota(jnp.int32, (CHUNK, CHUNK), 1)
        pairs = [(c, hh) for c in range(nc) for hh in range(GLA_HB)]
        rows = lambda c: slice(c * CHUNK, (c + 1) * CHUNK)
        cols_k = lambda hh: slice(hh * HK, (hh + 1) * HK)
        cols_v = lambda hh: slice(hh * HV, (hh + 1) * HV)
        qfws, pms, e_lasts = {}, {}, {}
        for c, hh in pairs:
            q = q_ref[rows(c), cols_k(hh)].astype(F32) * QSCALE
            k = k_ref[rows(c), cols_k(hh)].astype(F32)
            ec, fc = e_pos[rows(c), cols_k(hh)], e_neg[rows(c), cols_k(hh)]
            qfw = (q * ec).astype(BF16)
            kfw_f = k * fc
            s_fw = _dot_nt(qfw, kfw_f.astype(BF16))
            s_bw = _dot_nt((q * fc).astype(BF16), (k * ec).astype(BF16))
            e_last = _pick_row(ec, CHUNK - 1)
            kv_ref[c, hh] = _dot_tn(v_ref[rows(c), cols_v(hh)], (kfw_f * e_last).astype(BF16))
            qfws[c, hh], pms[c, hh], e_lasts[c, hh] = qfw, jnp.where(lower, s_fw, s_bw).astype(BF16), e_last
        for hh in range(GLA_HB):
            st = st_ref[hh]
            for c in range(nc):
                sp_ref[c, hh] = st.astype(BF16)
                st = st * e_lasts[c, hh] + kv_ref[c, hh]
            st_ref[hh] = st
        for c, hh in pairs:
            o = _dot(pms[c, hh], v_ref[rows(c), cols_v(hh)]) + _dot_nt(qfws[c, hh], sp_ref[c, hh])
            r = lax.rsqrt(jnp.mean(o * o, axis=-1, keepdims=True) + EPS)
            zo = zog_ref[rows(c), cols_v(hh)].astype(F32)
            o_ref[rows(c), cols_v(hh)] = o.astype(BF16)
            og_ref[rows(c), cols_v(hh)] = (o * r * gh_ref[...] * zo * _sigmoid(zo)).astype(BF16)

    rmap = lambda h, r: r
    return pl.pallas_call(
        body, name="gla_fwd", grid=(HEADS // GLA_HB, s // rb),
        in_specs=_gla_specs(rb, rmap) + [
            pl.BlockSpec((rb, wv), lambda h, r: (r, OFF_OG // wv + h)),
            pl.BlockSpec((128, wk), lambda h, r: (0, h)), pl.BlockSpec((1, wk), lambda h, r: (0, h)),
            pl.BlockSpec((1, HV), lambda h, r: (0, 0))],
        out_specs=[pl.BlockSpec((rb, wv), lambda h, r: (r, h)), pl.BlockSpec((rb, wv), lambda h, r: (r, h)),
                   pl.BlockSpec((nc, GLA_HB, HV, HK), lambda h, r: (r, h, 0, 0))],
        out_shape=[jax.ShapeDtypeStruct((s, D), BF16), jax.ShapeDtypeStruct((s, D), BF16),
                   jax.ShapeDtypeStruct((s // CHUNK, HEADS, HV, HK), BF16)],
        scratch_shapes=[pltpu.VMEM((GLA_HB, HV, HK), F32), pltpu.VMEM((nc, GLA_HB, HV, HK), F32)],
        compiler_params=_cp("arbitrary", "arbitrary"),
    )(zr, zr, zr, zr, zr, wgk, bgk, ghead)


def _gla_bwd(zr, do, sp, wgk, bgk, after, dz, rb):
    s = zr.shape[0]
    nc = rb // CHUNK
    nr = s // rb
    wk, wv = GLA_HB * HK, GLA_HB * HV

    def body(q_ref, k_ref, v_ref, zgk_ref, do_ref, sp_ref, wgk_ref, bgk_ref, after_ref, dz_in, dq_ref, dk_ref, dv_ref,
             dg_ref, gt_ref, dbc_ref, gs_ref):
        @pl.when(pl.program_id(1) == 0)
        def _():
            gt_ref[...] = jnp.zeros_like(gt_ref)

        g, e_pos, e_neg = _gla_decay(zgk_ref, wgk_ref, bgk_ref, rb)
        lower = _rows((CHUNK, CHUNK)) >= lax.broadcasted_iota(jnp.int32, (CHUNK, CHUNK), 1)
        is_last = _rows((CHUNK, HK)) == CHUNK - 1
        pairs = [(c, hh) for c in range(nc) for hh in range(GLA_HB)]
        rows = lambda c: slice(c * CHUNK, (c + 1) * CHUNK)
        cols_k = lambda hh: slice(hh * HK, (hh + 1) * HK)
        cols_v = lambda hh: slice(hh * HV, (hh + 1) * HV)
        e_lasts = {}
        for c, hh in pairs:
            ec = e_pos[rows(c), cols_k(hh)]
            qfw = (q_ref[rows(c), cols_k(hh)].astype(F32) * QSCALE * ec).astype(BF16)
            gs_ref[c, hh] = _dot_tn(do_ref[rows(c), cols_v(hh)], qfw)
            e_lasts[c, hh] = _pick_row(ec, CHUNK - 1)
        for hh in range(GLA_HB):
            gt = gt_ref[hh]
            for c in reversed(range(nc)):
                own = gs_ref[c, hh]
                gs_ref[c, hh] = gt
                gt = own + gt * e_lasts[c, hh]
            gt_ref[hh] = gt
        def decayed(c, hh):
            q = q_ref[rows(c), cols_k(hh)].astype(F32) * QSCALE
            k = k_ref[rows(c), cols_k(hh)].astype(F32)
            ec, fc = e_pos[rows(c), cols_k(hh)], e_neg[rows(c), cols_k(hh)]
            return ec, fc, q * ec, k * fc, q * fc, k * ec

        pms, dss = {}, {}
        for c, hh in pairs:
            _, _, qfw_f, kfw_f, qbw_f, kbw_f = decayed(c, hh)
            s_fw = _dot_nt(qfw_f.astype(BF16), kfw_f.astype(BF16))
            s_bw = _dot_nt(qbw_f.astype(BF16), kbw_f.astype(BF16))
            dp = _dot_nt(do_ref[rows(c), cols_v(hh)], v_ref[rows(c), cols_v(hh)])
            pms[c, hh] = jnp.where(lower, s_fw, s_bw).astype(BF16)
            dss[c, hh] = (jnp.where(lower, dp, 0.0).astype(BF16), jnp.where(lower, 0.0, dp).astype(BF16))
        for c, hh in pairs:
            sl, ck, cv = rows(c), cols_k(hh), cols_v(hh)
            v = v_ref[sl, cv]
            dov = do_ref[sl, cv]
            ec, fc, qfw_f, kfw_f, qbw_f, kbw_f = decayed(c, hh)
            qfw, kfw, qbw, kbw = qfw_f.astype(BF16), kfw_f.astype(BF16), qbw_f.astype(BF16), kbw_f.astype(BF16)
            pm = pms[c, hh]
            e_last = e_lasts[c, hh]
            kdec = (kfw_f * e_last).astype(BF16)
            gt = gs_ref[c, hh]
            gtb = gt.astype(BF16)
            spv = sp_ref[c, hh]
            dv_ref[sl, cv] = (_dot_tn(pm, dov) + _dot_nt(kdec, gtb)).astype(BF16)
            ds_fw, ds_bw = dss[c, hh]
            dqfw = _dot(ds_fw, kfw) + _dot(dov, spv)
            dkfw = _dot_tn(ds_fw, qfw)
            dqbw = _dot(ds_bw, kbw)
            dkbw = _dot_tn(ds_bw, qbw)
            dkdec = _dot(v, gtb)
            de_last = (jnp.sum(gt * spv.astype(F32), axis=0, keepdims=True)
                       + jnp.sum(dkdec * kfw_f, axis=0, keepdims=True))
            dkfw = dkfw + dkdec * e_last
            dq_ref[sl, ck] = ((dqfw * ec + dqbw * fc) * QSCALE).astype(BF16)
            dk_ref[sl, ck] = (dkfw * fc + dkbw * ec).astype(BF16)
            dbc = dqfw * qfw_f - dqbw * qbw_f + dkbw * kbw_f - dkfw * kfw_f
            dbc_ref[sl, ck] = dbc + jnp.where(is_last, de_last * e_last, 0.0)
        rowm = _rows((rb, wk)) & (CHUNK - 1)
        dla, kk = dbc_ref[...], 1
        while kk < CHUNK:
            dla = dla + jnp.where(rowm < CHUNK - kk, pltpu.roll(dla, rb - kk, 0), 0.0)
            kk *= 2
        dg_ref[...] = dla * (1.0 / 16.0) * _sigmoid(-g)

    rmap = lambda h, r: nr - 1 - r
    rev = lambda h, r: (nr - 1 - r, h)
    return pl.pallas_call(
        body, name="gla_bwd", grid=(HEADS // GLA_HB, nr),
        in_specs=_gla_specs(rb, rmap) + [
            pl.BlockSpec((rb, wv), rev),
            pl.BlockSpec((nc, GLA_HB, HV, HK), lambda h, r: (nr - 1 - r, h, 0, 0)),
            pl.BlockSpec((128, wk), lambda h, r: (0, h)), pl.BlockSpec((1, wk), lambda h, r: (0, h)), ANY, ANY],
        out_specs=[pl.BlockSpec((rb, wk), rev), pl.BlockSpec((rb, wk), rev),
                   pl.BlockSpec((rb, wv), lambda h, r: (nr - 1 - r, OFF_V // wv + h)), pl.BlockSpec((rb, wk), rev)],
        out_shape=[jax.ShapeDtypeStruct((s, HEADS * HK), BF16), jax.ShapeDtypeStruct((s, HEADS * HK), BF16),
                   jax.ShapeDtypeStruct(dz.shape, BF16), jax.ShapeDtypeStruct((s, HEADS * HK), F32)],
        scratch_shapes=[pltpu.VMEM((GLA_HB, HV, HK), F32), pltpu.VMEM((rb, wk), F32),
                        pltpu.VMEM((nc, GLA_HB, HV, HK), F32)],
        input_output_aliases={9: 2},
        compiler_params=_cp("arbitrary", "arbitrary"),
    )(zr, zr, zr, zr, do, sp, wgk, bgk, after, dz)


def _gk_bwd(dgpre, zr, wgk, after, dz, ts):
    s = zr.shape[0]

    def body(dg_ref, zgk_ref, w_ref, after_ref, dz_in, dz_ref, dw_ref, db_ref):
        @pl.when(pl.program_id(0) == 0)
        def _():
            dw_ref[...] = jnp.zeros_like(dw_ref)
            db_ref[...] = jnp.zeros_like(db_ref)

        dg = dg_ref[...]
        dgb = dg.astype(BF16)
        dz_ref[...] = _dot_nt(dgb, w_ref[...].astype(BF16)).astype(BF16)
        dw_ref[...] += _dot_tn(zgk_ref[...], dgb)
        db_ref[...] += jnp.sum(dg, axis=0, keepdims=True)

    return pl.pallas_call(
        body, name="gk_bwd", grid=(s // ts,),
        in_specs=[pl.BlockSpec((ts, 512), lambda i: (i, 0)), pl.BlockSpec((ts, 128), lambda i: (i, OFF_GK // 128)),
                  pl.BlockSpec((128, 512), lambda i: (0, 0)), ANY, ANY],
        out_specs=[pl.BlockSpec((ts, 128), lambda i: (i, OFF_GK // 128)), pl.BlockSpec((128, 512), lambda i: (0, 0)),
                   pl.BlockSpec((1, 512), lambda i: (0, 0))],
        out_shape=[jax.ShapeDtypeStruct(dz.shape, BF16), jax.ShapeDtypeStruct((128, 512), F32),
                   jax.ShapeDtypeStruct((1, 512), F32)],
        input_output_aliases={4: 0},
        compiler_params=_cp("arbitrary"),
    )(dgpre, zr, wgk, after, dz)


def _merge_fwd(x, zr, pp, og, bgate, wpp, wgla, wout, gffn, after, ts):
    s = x.shape[0]

    def body(x_ref, z0_ref, z1_ref, pp_ref, og_ref, bg_ref, wpp_ref, wgla_ref, wout_ref, gf_ref, after_ref,
             x1_ref, mix_ref, yp_ref, yg_ref, h2_ref):
        ppv = pp_ref[...]
        yp = jnp.concatenate([_dot(ppv, wpp_ref[j]) for j in range(4)], axis=1)
        yg = _dot(og_ref[...], wgla_ref[...])
        g0 = _sigmoid(z0_ref[...].astype(F32) + bg_ref[:, :D])
        g1 = _sigmoid(z1_ref[...].astype(F32) + bg_ref[:, D:])
        mixed = (g0 * yp + g1 * yg).astype(BF16)
        x1 = x_ref[...] + _dot(mixed, wout_ref[...])
        x1_ref[...] = x1
        mix_ref[...] = mixed
        yp_ref[...] = yp.astype(BF16)
        yg_ref[...] = yg.astype(BF16)
        r = lax.rsqrt(jnp.mean(x1 * x1, axis=-1, keepdims=True) + EPS)
        h2_ref[...] = (x1 * r * gf_ref[...]).astype(BF16)

    row = lambda i: (i, 0)
    const2 = lambda i: (0, 0)
    return pl.pallas_call(
        body, name="merge_fwd", grid=(s // ts,),
        in_specs=[pl.BlockSpec((ts, D), row), pl.BlockSpec((ts, D), lambda i: (i, 0)), pl.BlockSpec((ts, D), lambda i: (i, 1)),
                  pl.BlockSpec((ts, POOL_W), row), pl.BlockSpec((ts, D), row), pl.BlockSpec((1, 2 * D), const2),
                  pl.BlockSpec((4, POOL_W, 256), lambda i: (0, 0, 0)), pl.BlockSpec((D, D), const2),
                  pl.BlockSpec((D, D), const2), pl.BlockSpec((1, D), const2), ANY],
        out_specs=[pl.BlockSpec((ts, D), row)] * 5,
        out_shape=[jax.ShapeDtypeStruct((s, D), F32)] + [jax.ShapeDtypeStruct((s, D), BF16)] * 4,
        compiler_params=_cp("arbitrary"),
    )(x, zr, zr, pp, og, bgate, wpp, wgla, wout, gffn, after)


def _merge_bwd(dx1b, zr, yp, yg, o, bgate, ghead, wpp, wgla, wout, after, ts):
    s = dx1b.shape[0]

    def body(dx_ref, z0_ref, z1_ref, zog_ref, yp_ref, yg_ref, o_ref, bg_ref, gh_ref, wpp_ref, wgla_ref, wout_ref, after_ref,
             dzg_ref, dyp_ref, dyg_ref, dpp_ref, do_ref, dzog_ref, dbg_ref, dgh_ref):
        @pl.when(pl.program_id(0) == 0)
        def _():
            dbg_ref[...] = jnp.zeros_like(dbg_ref)
            dgh_ref[...] = jnp.zeros_like(dgh_ref)

        dmix = _dot_nt(dx_ref[...], wout_ref[...])
        g0 = _sigmoid(z0_ref[...].astype(F32) + bg_ref[:, :D])
        g1 = _sigmoid(z1_ref[...].astype(F32) + bg_ref[:, D:])
        dypb = (dmix * g0).astype(BF16)
        dygb = (dmix * g1).astype(BF16)
        dz0 = dmix * yp_ref[...].astype(F32) * g0 * (1.0 - g0)
        dz1 = dmix * yg_ref[...].astype(F32) * g1 * (1.0 - g1)
        dzg_ref[:, :D] = dz0.astype(BF16)
        dzg_ref[:, D:] = dz1.astype(BF16)
        dbg_ref[:, :D] += jnp.sum(dz0, axis=0, keepdims=True)
        dbg_ref[:, D:] += jnp.sum(dz1, axis=0, keepdims=True)
        dyp_ref[...] = dypb
        dyg_ref[...] = dygb
        dpp = _dot_nt(dypb[:, 0:256], wpp_ref[0])
        for j in range(1, 4):
            dpp = dpp + _dot_nt(dypb[:, j * 256:(j + 1) * 256], wpp_ref[j])
        dpp_ref[...] = dpp.astype(BF16)
        dog = _dot_nt(dygb, wgla_ref[...])
        gh = gh_ref[...]
        dgh = jnp.zeros((1, HV), F32)
        for h in range(HEADS):
            cs = slice(h * HV, (h + 1) * HV)
            ov = o_ref[:, cs].astype(F32)
            r = lax.rsqrt(jnp.mean(ov * ov, axis=-1, keepdims=True) + EPS)
            oh = ov * r
            zo = zog_ref[:, cs].astype(F32)
            sg = _sigmoid(zo)
            dog_h = dog[:, cs]
            don = dog_h * zo * sg
            dzog_ref[:, cs] = (dog_h * oh * gh * sg * (1.0 + zo * (1.0 - sg))).astype(BF16)
            dgh = dgh + jnp.sum(don * oh, axis=0, keepdims=True)
            doh = don * gh
            do_ref[:, cs] = (r * (doh - oh * jnp.mean(doh * oh, axis=-1, keepdims=True))).astype(BF16)
        dgh_ref[...] += dgh

    row = lambda i: (i, 0)
    const2 = lambda i: (0, 0)
    return pl.pallas_call(
        body, name="merge_bwd", grid=(s // ts,),
        in_specs=[pl.BlockSpec((ts, D), row), pl.BlockSpec((ts, D), lambda i: (i, 0)), pl.BlockSpec((ts, D), lambda i: (i, 1)),
                  pl.BlockSpec((ts, D), lambda i: (i, OFF_OG // D)), pl.BlockSpec((ts, D), row), pl.BlockSpec((ts, D), row),
                  pl.BlockSpec((ts, D), row), pl.BlockSpec((1, 2 * D), const2), pl.BlockSpec((1, HV), const2),
                  pl.BlockSpec((4, POOL_W, 256), lambda i: (0, 0, 0)), pl.BlockSpec((D, D), const2),
                  pl.BlockSpec((D, D), const2), ANY],
        out_specs=[pl.BlockSpec((ts, 2 * D), row), pl.BlockSpec((ts, D), row), pl.BlockSpec((ts, D), row),
                   pl.BlockSpec((ts, POOL_W), row), pl.BlockSpec((ts, D), row), pl.BlockSpec((ts, D), row),
                   pl.BlockSpec((1, 2 * D), const2), pl.BlockSpec((1, HV), const2)],
        out_shape=[jax.ShapeDtypeStruct((s, N_INR), BF16), jax.ShapeDtypeStruct((s, D), BF16),
                   jax.ShapeDtypeStruct((s, D), BF16), jax.ShapeDtypeStruct((s, POOL_W), BF16),
                   jax.ShapeDtypeStruct((s, D), BF16), jax.ShapeDtypeStruct((s, D), BF16),
                   jax.ShapeDtypeStruct((1, 2 * D), F32), jax.ShapeDtypeStruct((1, HV), F32)],
        compiler_params=_cp("arbitrary"),
    )(dx1b, zr, zr, zr, yp, yg, o, bgate, ghead, wpp, wgla, wout, after)


HALO = 16
CCH = D_FF // 2


def _conv_taps(u_ref, halo_ref, cs, first, ts):
    u = u_ref[:, cs].astype(F32)
    hal = halo_ref[:, cs].astype(F32)
    h1 = jnp.where(first, 0.0, _pick_row(hal, HALO - 1))
    h2 = jnp.where(first, 0.0, _pick_row(hal, HALO - 2))
    row8 = _rows((8, u.shape[1]))
    r1, r2 = pltpu.roll(u, 1, 0), pltpu.roll(u, 2, 0)
    r1 = jnp.concatenate([jnp.where(row8 == 0, h1, r1[:8]), r1[8:]], axis=0)
    r2 = jnp.concatenate([jnp.where(row8 == 0, h2, jnp.where(row8 == 1, h1, r2[:8])), r2[8:]], axis=0)
    return u, r1, r2


def _ffn_down_loss(u, x1, tgt, wconv, bconv, wdown, gfin, ts):
    s = x1.shape[0]

    def body(u_ref, halo_ref, x1_ref, t_ref, wc_ref, bc_ref, wd_ref, gf_ref, a_ref, c_ref, dx_ref, dxb_ref, ls_ref,
             dgf_ref):
        i = pl.program_id(0)

        @pl.when(i == 0)
        def _():
            ls_ref[...] = jnp.zeros_like(ls_ref)
            dgf_ref[...] = jnp.zeros_like(dgf_ref)

        first = i == 0
        acc = x1_ref[...]
        for hf in range(D_FF // CCH):
            cg = slice(hf * CCH, (hf + 1) * CCH)
            cv = slice(D_FF + hf * CCH, D_FF + (hf + 1) * CCH)
            vals = []
            for cs in (cg, cv):
                u0, u1, u2 = _conv_taps(u_ref, halo_ref, cs, first, ts)
                vals.append(bc_ref[:, cs] + wc_ref[0:1, cs] * u2 + wc_ref[1:2, cs] * u1 + wc_ref[2:3, cs] * u0)
                c_ref[:, cs] = vals[-1].astype(BF16)
            a = (vals[0] * _sigmoid(vals[0]) * vals[1]).astype(BF16)
            a_ref[:, cg] = a
            acc = acc + _dot(a, wd_ref[cg, :])
        r = lax.rsqrt(jnp.mean(acc * acc, axis=-1, keepdims=True) + EPS)
        xh = acc * r
        gf = gf_ref[...]
        err = xh * gf - t_ref[...]
        ls_ref[...] += (0.5 / D) * jnp.sum(jnp.sum(err * err, axis=-1, keepdims=True), axis=0, keepdims=True)
        dy = err * (1.0 / D)
        dgf_ref[...] += jnp.sum(dy * xh, axis=0, keepdims=True)
        dxh = dy * gf
        dx = r * (dxh - xh * jnp.mean(dxh * xh, axis=-1, keepdims=True))
        dx_ref[...] = dx
        dxb_ref[...] = dx.astype(BF16)

    row = lambda i: (i, 0)
    const2 = lambda i: (0, 0)
    return pl.pallas_call(
        body, name="ffn_down_loss", grid=(s // ts,),
        in_specs=[pl.BlockSpec((ts, N_UP), row),
                  pl.BlockSpec((HALO, N_UP), lambda i: (jnp.maximum(i * (ts // HALO) - 1, 0), 0)),
                  pl.BlockSpec((ts, D), row), pl.BlockSpec((ts, D), row), pl.BlockSpec((3, N_UP), const2),
                  pl.BlockSpec((1, N_UP), const2), pl.BlockSpec((D_FF, D), const2), pl.BlockSpec((1, D), const2)],
        out_specs=[pl.BlockSpec((ts, D_FF), row), pl.BlockSpec((ts, N_UP), row), pl.BlockSpec((ts, D), row),
                   pl.BlockSpec((ts, D), row), pl.BlockSpec((1, 128), const2), pl.BlockSpec((1, D), const2)],
        out_shape=[jax.ShapeDtypeStruct((s, D_FF), BF16), jax.ShapeDtypeStruct((s, N_UP), BF16),
                   jax.ShapeDtypeStruct((s, D), F32), jax.ShapeDtypeStruct((s, D), BF16),
                   jax.ShapeDtypeStruct((1, 128), F32), jax.ShapeDtypeStruct((1, D), F32)],
        compiler_params=_cp("arbitrary"),
    )(u, u, x1, tgt, wconv, bconv, wdown, gfin)


def _ffn_bwd(dx2b, u, c, wconv, wdown, ts):
    s = dx2b.shape[0]
    nt = s // ts

    def body(dx_ref, u_ref, c_ref, wc_ref, wd_ref, du_ref, db_ref, dw_ref, nxt_ref):
        @pl.when(pl.program_id(0) == 0)
        def _():
            db_ref[...] = jnp.zeros_like(db_ref)
            dw_ref[...] = jnp.zeros_like(dw_ref)
            nxt_ref[...] = jnp.zeros_like(nxt_ref)

        dxv = dx_ref[...]
        row8 = _rows((8, CCH))
        for hf in range(D_FF // CCH):
            cg = slice(hf * CCH, (hf + 1) * CCH)
            cv = slice(D_FF + hf * CCH, D_FF + (hf + 1) * CCH)
            da = _dot_nt(dxv, wd_ref[cg, :])
            gate = c_ref[:, cg].astype(F32)
            val = c_ref[:, cv].astype(F32)
            sg = _sigmoid(gate)
            dcs = (da * val * sg * (1.0 + gate * (1.0 - sg)), da * gate * sg)
            for cs, dc in zip((cg, cv), dcs):
                n1 = nxt_ref[0:1, cs]
                n2 = nxt_ref[1:2, cs]
                r1, r2 = pltpu.roll(dc, ts - 1, 0), pltpu.roll(dc, ts - 2, 0)
                f1 = jnp.concatenate([r1[:ts - 8], jnp.where(row8 == 7, n1, r1[ts - 8:])], axis=0)
                f2 = jnp.concatenate([r2[:ts - 8], jnp.where(row8 == 7, n2, jnp.where(row8 == 6, n1, r2[ts - 8:]))], axis=0)
                uv = u_ref[:, cs].astype(F32)
                db_ref[:, cs] += jnp.sum(dc, axis=0, keepdims=True)
                dw_ref[0:1, cs] += jnp.sum(f2 * uv, axis=0, keepdims=True)
                dw_ref[1:2, cs] += jnp.sum(f1 * uv, axis=0, keepdims=True)
                dw_ref[2:3, cs] += jnp.sum(dc * uv, axis=0, keepdims=True)
                du_ref[:, cs] = (wc_ref[2:3, cs] * dc + wc_ref[1:2, cs] * f1 + wc_ref[0:1, cs] * f2).astype(BF16)
                nxt_ref[:, cs] = dc[0:8, :]

    rev = lambda i: (nt - 1 - i, 0)
    const2 = lambda i: (0, 0)
    return pl.pallas_call(
        body, name="ffn_bwd", grid=(nt,),
        in_specs=[pl.BlockSpec((ts, D), rev), pl.BlockSpec((ts, N_UP), rev), pl.BlockSpec((ts, N_UP), rev),
                  pl.BlockSpec((3, N_UP), const2), pl.BlockSpec((D_FF, D), const2)],
        out_specs=[pl.BlockSpec((ts, N_UP), rev), pl.BlockSpec((1, N_UP), const2), pl.BlockSpec((3, N_UP), const2)],
        out_shape=[jax.ShapeDtypeStruct((s, N_UP), BF16), jax.ShapeDtypeStruct((1, N_UP), F32),
                   jax.ShapeDtypeStruct((3, N_UP), F32)],
        scratch_shapes=[pltpu.VMEM((8, N_UP), F32)],
        compiler_params=_cp("arbitrary"),
    )(dx2b, u, c, wconv, wdown)


ANY = pl.BlockSpec(memory_space=pl.ANY)


def _place():
    x, y, c = lax.axis_index("x"), lax.axis_index("y"), lax.axis_index("c")
    chips = [(1 - x, y), (x, 1 - y), (1 - x, 1 - y)]
    return x, y, c, chips


def _half(shape, c, axis):
    size = shape[axis] // 2
    cut = pl.ds(pl.multiple_of(c * size, 8 if axis == 0 else 128), size)
    return (cut, slice(None)) if axis == 0 else (slice(None), cut)


def _half_shape(shape, axis):
    return (shape[0] // 2, shape[1]) if axis == 0 else (shape[0], shape[1] // 2)


def _remote(src, dst, send_sems, recv_sems, k, to):
    return pltpu.make_async_remote_copy(src_ref=src, dst_ref=dst, send_sem=send_sems.at[k], recv_sem=recv_sems.at[k],
                                        device_id=to, device_id_type=MESH)


def _sibling_exchange(grads, axes, smalls, name):
    nb = len(grads)
    n = nb + len(smalls)

    def body(*refs):
        ins, outs = refs[:n], refs[n:2 * n]
        send_sems, recv_sems = refs[2 * n:]
        x, y, c, _ = _place()
        sib = (x, y, 1 - c)
        cps = []
        for a in range(nb):
            theirs = _half(grads[a].shape[1:], 1 - c, axes[a])
            cps.append(_remote(ins[a].at[(slice(None),) + theirs], outs[a], send_sems, recv_sems, a, sib))
        for a in range(nb, n):
            cps.append(_remote(ins[a], outs[a], send_sems, recv_sems, a, sib))
        for cp in cps:
            cp.start()
        for cp in cps:
            cp.wait()

    out_shape = [jax.ShapeDtypeStruct((4,) + _half_shape(g.shape[1:], ax), g.dtype) for g, ax in zip(grads, axes)]
    out_shape += [jax.ShapeDtypeStruct(a.shape, F32) for a in smalls]
    return pl.pallas_call(
        body, name=name, in_specs=[ANY] * n, out_specs=[ANY] * n, out_shape=out_shape,
        scratch_shapes=[pltpu.SemaphoreType.DMA((n,)), pltpu.SemaphoreType.DMA((n,))],
        compiler_params=pltpu.CompilerParams(has_side_effects=True),
    )(*grads, *smalls)


def _gather_share(lands, axes, name):
    n = len(lands)

    def body(*refs):
        outs = refs[n:2 * n]
        send_sems, recv_sems = refs[2 * n:]
        x, y, c, chips = _place()
        sib = (x, y, 1 - c)
        cps = []
        for a in range(n):
            mine = _half(lands[a].shape[1:], c, axes[a])
            for k, ch in enumerate(chips):
                landed = outs[a].at[(2 * ch[0] + ch[1],) + mine]
                cps.append(_remote(landed, landed, send_sems, recv_sems, 3 * a + k, sib))
        for cp in cps:
            cp.start()
        for a in range(n):
            other = _half(lands[a].shape[1:], 1 - c, axes[a])
            for k, ch in enumerate(chips):
                landed = outs[a].at[(2 * ch[0] + ch[1],) + other]
                _remote(landed, landed, send_sems, recv_sems, 3 * a + k, sib).wait_recv()
        for cp in cps:
            cp.wait_send()

    return pl.pallas_call(
        body, name=name, in_specs=[ANY] * n, out_specs=[ANY] * n,
        out_shape=[jax.ShapeDtypeStruct(a.shape, a.dtype) for a in lands],
        input_output_aliases={a: a for a in range(n)},
        scratch_shapes=[pltpu.SemaphoreType.DMA((3 * n,)), pltpu.SemaphoreType.DMA((3 * n,))],
        compiler_params=pltpu.CompilerParams(has_side_effects=True),
    )(*lands)


def _sibling_share(halves, name):
    n = len(halves)

    def body(*refs):
        ins, outs = refs[:n], refs[n:2 * n]
        send_sems, recv_sems = refs[2 * n:]
        x, y, c, _ = _place()
        cps = [_remote(ins[a], outs[a], send_sems, recv_sems, a, (x, y, 1 - c)) for a in range(n)]
        for cp in cps:
            cp.start()
        for cp in cps:
            cp.wait()

    return pl.pallas_call(
        body, name=name, in_specs=[ANY] * n, out_specs=[ANY] * n,
        out_shape=[jax.ShapeDtypeStruct(h.shape, F32) for h in halves],
        scratch_shapes=[pltpu.SemaphoreType.DMA((n,)), pltpu.SemaphoreType.DMA((n,))],
        compiler_params=pltpu.CompilerParams(has_side_effects=True),
    )(*halves)


HBM = pl.BlockSpec(memory_space=pltpu.HBM)
SEM = pl.BlockSpec(memory_space=pltpu.SEMAPHORE)
DATAFLOW = pltpu.SideEffectType.DATAFLOW_SIDE_EFFECTING


def _split_start(name, srcs, land_shapes, plan, n_copies, after):
    lands = [lax.empty(*ls) if isinstance(ls, tuple) else ls for ls in land_shapes]
    bufs = list(srcs) + lands
    nb, ns = len(bufs), len(srcs)

    def body(*refs):
        send_sems, recv_sems, token = refs[nb + 1], refs[nb + 2], refs[-1]
        for k, (src, dst, to) in enumerate(plan(refs[:ns], refs[ns:nb])):
            _remote(src, dst, send_sems, recv_sems, k, to).start()
        token[...] = jnp.zeros_like(token)

    res = pl.pallas_call(
        body, name=name,
        out_shape=(pltpu.SemaphoreType.DMA((n_copies,)), pltpu.SemaphoreType.DMA((n_copies,)),
                   *[pltpu.HBM(b.shape, b.dtype) for b in bufs], jax.ShapeDtypeStruct((8, 128), F32)),
        in_specs=[HBM] * nb + [ANY],
        out_specs=(SEM, SEM, *[HBM] * nb, pl.BlockSpec(memory_space=pltpu.VMEM)),
        input_output_aliases={i: 2 + i for i in range(nb)},
        compiler_params=pltpu.CompilerParams(has_side_effects=DATAFLOW),
    )(*[pltpu.with_memory_space_constraint(b, pltpu.HBM) for b in bufs], after)
    return (res[0], res[1], list(res[2:2 + nb])), res[-1]


def _split_wait(name, handle, n_srcs, plan, after):
    send_sems, recv_sems, bufs = handle
    nb = len(bufs)

    def body(*refs):
        sends, recvs = refs[nb], refs[nb + 1]
        for k, (src, dst, to) in enumerate(plan(refs[:n_srcs], refs[n_srcs:nb])):
            cp = _remote(src, dst, sends, recvs, k, to)
            cp.wait_send()
            cp.wait_recv()

    res = pl.pallas_call(
        body, name=name, out_shape=[pltpu.HBM(b.shape, b.dtype) for b in bufs],
        in_specs=[HBM] * nb + [SEM, SEM, ANY], out_specs=[HBM] * nb,
        input_output_aliases={i: i for i in range(nb)},
        compiler_params=pltpu.CompilerParams(has_side_effects=DATAFLOW),
    )(*bufs, send_sems, recv_sems, after)
    return list(res[:n_srcs]), list(res[n_srcs:])


def _gather_plan(shapes, axes, n_whole=0):
    def plan(srcs, lands):
        x, y, c, chips = _place()
        out = []
        for a, (shape, axis) in enumerate(zip(shapes, axes)):
            mine = _half(shape, c, axis)
            for ch in chips:
                out.append((srcs[a].at[mine], lands[a].at[(2 * x + y,) + mine], (ch[0], ch[1], c)))
        for a in range(len(shapes), len(shapes) + n_whole):
            for ch in chips:
                out.append((srcs[a], lands[a].at[2 * x + y], (ch[0], ch[1], c)))
        return out
    return plan


def _share_plan(shapes, axes):
    def plan(srcs, lands):
        x, y, c, chips = _place()
        out = []
        for a, (shape, axis) in enumerate(zip(shapes, axes)):
            mine = _half(shape, c, axis)
            for ch in chips:
                landed = lands[a].at[(2 * ch[0] + ch[1],) + mine]
                out.append((landed, landed, (x, y, 1 - c)))
        return out
    return plan


def _sibling_plan(shapes, axes):
    def plan(srcs, lands):
        x, y, c, _ = _place()
        return [(srcs[a].at[(slice(None),) + _half(shape, 1 - c, axis)], lands[a], (x, y, 1 - c))
                for a, (shape, axis) in enumerate(zip(shapes, axes))]
    return plan


def _whole_to_sibling_plan(n):
    def plan(srcs, lands):
        x, y, c, _ = _place()
        return [(srcs[a], lands[a], (x, y, 1 - c)) for a in range(n)]
    return plan


def _reduce_plan(n_big, n_small):
    def plan(srcs, lands):
        x, y, c, chips = _place()
        out = []
        for a in range(n_big):
            for k, ch in enumerate(chips):
                out.append((srcs[a].at[2 * ch[0] + ch[1]], lands[a].at[k], (ch[0], ch[1], c)))
        for a in range(n_big, n_big + n_small):
            for ch in chips:
                out.append((srcs[a], lands[a].at[2 * x + y], (ch[0], ch[1], c)))
        return out
    return plan


def _row_tile(rows, cols, mult):
    best = mult
    for t in range(mult, rows + 1, mult):
        if rows % t == 0 and t * cols * 4 <= (2 << 20):
            best = t
    return best if rows % best == 0 else rows


COL_TILE = 256


def _half_tiling(hshape, axis, mult):
    hr, hc = hshape
    if axis == 0:
        tr = _row_tile(hr, hc, mult)
        return tr, hc, hr // tr
    return hr, COL_TILE, hc // COL_TILE


def _tile_idx(axis, t):
    return (t, 0) if axis == 0 else (0, t)


def _chip_partial(place, g, t, axis, name):
    hshape = t.shape[1:]
    br, bc, nt = _half_tiling(hshape, axis, 16)

    def body(pl_ref, g_ref, t_ref, pf_ref, pb_ref):
        v = g_ref[...].astype(F32) + t_ref[...].astype(F32)
        pb_ref[...] = v.astype(BF16)

        @pl.when(pl.program_id(1) == pl_ref[0])
        def _():
            pf_ref[...] = v

    blk = (None, br, bc)
    return pl.pallas_call(
        body, name=name,
        grid_spec=pltpu.PrefetchScalarGridSpec(
            num_scalar_prefetch=1, grid=(nt, 4),
            in_specs=[pl.BlockSpec(blk, lambda i, j, p: (j,) + _tile_idx(axis, p[1] * nt + i)),
                      pl.BlockSpec(blk, lambda i, j, p: (j,) + _tile_idx(axis, i))],
            out_specs=[pl.BlockSpec((br, bc), lambda i, j, p: _tile_idx(axis, i)),
                       pl.BlockSpec(blk, lambda i, j, p: (j,) + _tile_idx(axis, i))]),
        out_shape=[jax.ShapeDtypeStruct(hshape, F32), jax.ShapeDtypeStruct((4,) + hshape, BF16)],
        compiler_params=_cp("arbitrary", "arbitrary"),
    )(place, g, t)


def _finish_half(pf, rb, axis, name):
    hshape = pf.shape
    br, bc, nt = _half_tiling(hshape, axis, 16)

    def body(pf_ref, rb_ref, o_ref):
        o_ref[...] = ((pf_ref[...] + rb_ref[0].astype(F32)) + rb_ref[1].astype(F32)) + rb_ref[2].astype(F32)

    return pl.pallas_call(
        body, name=name, grid=(nt,),
        in_specs=[pl.BlockSpec((br, bc), lambda i: _tile_idx(axis, i)),
                  pl.BlockSpec((3, br, bc), lambda i: (0,) + _tile_idx(axis, i))],
        out_specs=pl.BlockSpec((br, bc), lambda i: _tile_idx(axis, i)),
        out_shape=jax.ShapeDtypeStruct(hshape, F32),
        compiler_params=_cp("arbitrary"),
    )(pf, rb)


def _adam_math(w, g, m, v):
    m = ADAM_B1 * m + (1.0 - ADAM_B1) * g
    v = ADAM_B2 * v + (1.0 - ADAM_B2) * (g * g)
    m_hat = m / (1.0 - ADAM_B1 ** ADAM_STEP)
    v_hat = v / (1.0 - ADAM_B2 ** ADAM_STEP)
    return -ADAM_LR * (m_hat / (jnp.sqrt(v_hat) + ADAM_EPS) + ADAM_WD * w), m, v


def _adam_halves(place, w, mine, theirs, m, v, axis, name):
    br, bc, nt = _half_tiling(mine.shape, axis, 8)

    def body(pl_ref, w_ref, a_ref, b_ref, m_ref, v_ref, g_ref, d_ref, mo_ref, vo_ref):
        is_mine = pl.program_id(0) // nt == pl_ref[1]
        g = jnp.where(is_mine, a_ref[...], b_ref[...])
        d, mn, vn = _adam_math(w_ref[...], g, m_ref[...], v_ref[...])
        g_ref[...] = g
        d_ref[...] = d
        mo_ref[...] = mn
        vo_ref[...] = vn

    full = pl.BlockSpec((br, bc), lambda i, p: _tile_idx(axis, i))
    mine_spec = pl.BlockSpec((br, bc), lambda i, p: _tile_idx(axis, jnp.where(i // nt == p[1], i % nt, nt - 1)))
    theirs_spec = pl.BlockSpec((br, bc), lambda i, p: _tile_idx(axis, jnp.where(i // nt == p[1], 0, i % nt)))
    return pl.pallas_call(
        body, name=name,
        grid_spec=pltpu.PrefetchScalarGridSpec(
            num_scalar_prefetch=1, grid=(2 * nt,), in_specs=[full, mine_spec, theirs_spec, full, full],
            out_specs=[full] * 4),
        out_shape=[jax.ShapeDtypeStruct(w.shape, F32)] * 4, compiler_params=_cp("arbitrary"),
    )(place, w, mine, theirs, m, v)


def _add_many(xs, ys, name):
    n = len(xs)

    def body(*refs):
        for i in range(n):
            refs[2 * n + i][...] = refs[i][...] + refs[n + i][...]

    return pl.pallas_call(body, name=name, out_shape=[jax.ShapeDtypeStruct(a.shape, F32) for a in xs])(*xs, *ys)


def _adam_small(place, owns, landed, ws, ms, vs, widths):
    n, nw = len(owns), len(ws)

    def body(pl_ref, *refs):
        own_r, land_r = refs[:n], refs[n:2 * n]
        w_r, m_r, v_r = (refs[2 * n + k * nw:2 * n + (k + 1) * nw] for k in range(3))
        outs = refs[2 * n + 3 * nw:]
        g_o, d_o, m_o, v_o = outs[:n], outs[n:n + nw], outs[n + nw:n + 2 * nw], outs[n + 2 * nw:]
        for me in range(4):
            @pl.when(pl_ref[0] == me)
            def _(me=me):
                for i in range(n):
                    p = [own_r[i][...] if k == me else land_r[i][k] for k in range(4)]
                    g = ((p[0] + p[1]) + p[2]) + p[3]
                    if i < nw and widths[i]:
                        g = g[:, me * widths[i]:(me + 1) * widths[i]]
                    g_o[i][...] = g
                    if i < nw:
                        d, mn, vn = _adam_math(w_r[i][...], g, m_r[i][...], v_r[i][...])
                        d_o[i][...] = d
                        m_o[i][...] = mn
                        v_o[i][...] = vn

    g_shapes = [jax.ShapeDtypeStruct(ws[i].shape if i < nw else owns[i].shape, F32) for i in range(n)]
    w_shapes = [jax.ShapeDtypeStruct(w.shape, F32) for w in ws]
    whole = lambda a: pl.BlockSpec(a.shape, lambda i, p, nd=len(a.shape): (0,) * nd)
    ins = list(owns) + list(landed) + list(ws) + list(ms) + list(vs)
    out_shape = g_shapes + w_shapes * 3
    out = pl.pallas_call(
        body, name="adam_small",
        grid_spec=pltpu.PrefetchScalarGridSpec(num_scalar_prefetch=1, grid=(1,), in_specs=[whole(a) for a in ins],
                                               out_specs=[whole(a) for a in out_shape]),
        out_shape=out_shape, compiler_params=_cp("arbitrary"),
    )(place, *ins)
    return out[:n], out[n:n + nw], out[n + nw:n + 2 * nw], out[n + 2 * nw:]


def kernel(x, g_mix, w_in, b_gate, w_gk_up, b_gk, w_pool_grp, pool_scale, g_gla_head, w_pool_proj, w_gla_proj, w_out, g_ffn, w_up, w_conv, b_conv, w_down, g_final, loss_target, m_g_mix, m_w_in, m_b_gate, m_w_gk_up, m_b_gk, m_w_pool_grp, m_pool_scale, m_g_gla_head, m_w_pool_proj, m_w_gla_proj, m_w_out, m_g_ffn, m_w_up, m_w_conv, m_b_conv, m_w_down, m_g_final, v_g_mix, v_w_in, v_b_gate, v_w_gk_up, v_b_gk, v_w_pool_grp, v_pool_scale, v_g_gla_head, v_w_pool_proj, v_w_gla_proj, v_w_out, v_g_ffn, v_w_up, v_w_conv, v_b_conv, v_w_down, v_g_final):
    s = x.shape[1]
    ts = min(s, 512)
    tm = min(s, 256)
    cx, cy, cc = lax.axis_index("x"), lax.axis_index("y"), lax.axis_index("c")
    chip = 2 * cx + cy
    place = jnp.stack([chip, cc]).astype(jnp.int32)

    big_names = ("w_in", "w_pool_proj", "w_gla_proj", "w_out", "w_up", "w_down")
    axes = (1, 0, 0, 0, 0, 0)
    shards = dict(w_in=jnp.transpose(w_in[0]), w_pool_proj=w_pool_proj[0], w_gla_proj=w_gla_proj[0], w_out=w_out[0],
                  w_up=w_up[0], w_down=w_down[0])
    def fill_own(lands, mine):
        return [lax.dynamic_update_slice(g, o_[None], (chip, 0, 0)) for g, o_ in zip(lands, mine)]

    def gather_start(tag, halves, group_axes, whole, after):
        plan = _gather_plan([o_.shape for o_ in halves], group_axes, len(whole))
        srcs = list(halves) + list(whole)
        handle, token = _split_start("gather_" + tag + "_start", srcs, [((4,) + o_.shape, o_.dtype) for o_ in srcs], plan,
                                     3 * len(srcs), after)
        return (handle, plan, len(halves), len(srcs), group_axes), token

    def gather_finish(tag, started, after):
        handle, plan, n_halves, n, group_axes = started
        mine, lands = _split_wait("gather_" + tag + "_wait", handle, n, plan, after)
        lands[:n_halves] = _gather_share(lands[:n_halves], group_axes, "gather_" + tag + "_share")
        return fill_own(lands, mine)

    in_w, tok = gather_start("in", [jnp.transpose(w_in[0].astype(BF16))], axes[:1], [], g_mix)
    zero = tok[0, 0]
    own = [(shards[n] + zero).astype(BF16) for n in big_names[1:]]
    mix_w, tok = gather_start("mix", own[0:3], axes[1:4], [w_gk_up[0] + zero, w_conv[0] + zero], tok)
    up_w, tok = gather_start("up", own[3:4], axes[4:5], [], tok)
    down_w, tok = gather_start("down", own[4:5], axes[5:6], [], tok)

    def forward_start(tag, started, after):
        handle, plan, _, n, group_axes = started
        mine, lands = _split_wait("gather_" + tag + "_wait", handle, n, plan, after)
        plan = _share_plan([o_.shape for o_ in mine], group_axes)
        share, token = _split_start("gather_" + tag + "_share_start", [], lands, plan, 3 * n, after)
        return (share, plan, mine), token

    def forward_done(tag, forwarded, after):
        share, plan, mine = forwarded
        return fill_own(_split_wait("gather_" + tag + "_share_wait", share, 0, plan, after)[1], mine)
    xs, tgt = x[0], loss_target[0]
    wgrp = w_pool_grp[0]
    h = _rmsnorm(xs, g_mix, tok, "norm_mix", ts)
    m_in_t, v_in_t = jnp.transpose(m_w_in[0]), jnp.transpose(v_w_in[0])
    h, m_in_t, v_in_t = lax.optimization_barrier((h, m_in_t, v_in_t))
    w_in_t = gather_finish("in", in_w, h)[0].reshape(N_IN, D)
    nsh = N_IN // 4

    zr = _in_proj(h, w_in_t, PROJ_TILE)
    p, pp = _pool_fwd(zr, wgrp, pool_scale)
    wpp, wgla, wout, wgk4, wconv4 = gather_finish("mix", mix_w, pp)
    wgla, wout = wgla.reshape(D, D), wout.reshape(D, D)
    wgk_full = jnp.transpose(wgk4, (1, 0, 2)).reshape(GATE_RANK, 512)
    wconv_full = jnp.transpose(wconv4, (1, 0, 2)).reshape(3, N_UP)
    wgk_pad = jnp.concatenate([wgk_full, jnp.zeros((128 - GATE_RANK, 512), F32)], axis=0)
    o, og, sp = _gla_fwd(zr, wgk_pad, b_gk, g_gla_head, ts)
    up_f, tok = forward_start("up", up_w, og)
    x1, mixed, yp, yg, h2 = _merge_fwd(xs, zr, pp, og, b_gate, wpp, wgla, wout, g_ffn, tok, ts)
    wup, = forward_done("up", up_f, x1)
    down_f, tok = forward_start("down", down_w, x1)
    u = _matmul_resident(h2, wup, tok, "ffn_up")
    wdown = forward_done("down", down_f, u)[0].reshape(D_FF, D)
    a, conv_out, dx2, dx2b, loss_part, dgfin = _ffn_down_loss(u, x1, tgt, wconv_full, b_conv, wdown,
                                                              g_final.reshape(1, D), tm)

    du, dbconv, dwconv = _ffn_bwd(dx2b, u, conv_out, wconv_full, wdown, tm)
    dw_down = _matmul_tn(a, dx2b, "dw_down", D, tm=D_FF // 2)
    dw_up = _matmul_tn(h2, du, "dw_up", UP_SHARD, shard_major=True)

    def exchange_start(tag, grads, group_axes, after):
        plan = _sibling_plan([g.shape[1:] for g in grads], group_axes)
        lands = [((4,) + _half_shape(g.shape[1:], ax), g.dtype) for g, ax in zip(grads, group_axes)]
        handle, token = _split_start("sibling_" + tag + "_start", grads, lands, plan, len(grads), after)
        return (handle, plan, len(grads)), token

    def partials(tag, names, group_axes, exchange, after):
        handle, plan, n = exchange
        mine, theirs = _split_wait("sibling_" + tag + "_wait", handle, n, plan, after)
        return zip(*[_chip_partial(place, g, t, ax, "chip_partial_" + nm)
                     for nm, ax, g, t in zip(names, group_axes, mine, theirs)])

    ffn_names, ffn_axes = ("w_up", "w_down"), (0, 0)
    ffn_x, token = exchange_start("ffn", [dw_up, dw_down.reshape(4, 704, D)], ffn_axes, du)
    dx1, dx1b, dgffn = _matmul_nt_normbwd(du, wup, x1, g_ffn, dx2, token, "ffn_up_bwd", ts)
    ffn_pf, ffn_pb = partials("ffn", ffn_names, ffn_axes, ffn_x, dx1b)
    ffn_plan = _reduce_plan(2, 0)
    ffn_handle, token = _split_start("reduce_ffn_start", ffn_pb, [((3,) + p.shape[1:], BF16) for p in ffn_pb],
                                     ffn_plan, 6, ffn_pf[0])

    dzr, dyp, dyg, dpp, do, dzog, dbgate, dghead = _merge_bwd(dx1b, zr, yp, yg, o, b_gate, g_gla_head, wpp, wgla, wout,
                                                             token, ts)
    dzr = lax.dynamic_update_slice(dzr, dzog, (0, OFF_OG))
    dw_out = _matmul_tn(mixed, dx1b, "dw_out", D, tm=512)
    dw_gla = _matmul_tn(og, dyg, "dw_gla", D, tm=512)
    dw_pp = _matmul_tn(pp, dyp, "dw_pp", 256, shard_major=True)

    out_names, out_axes = ("w_pool_proj", "w_gla_proj", "w_out"), (0, 0, 0)
    out_x, token = exchange_start("out", [dw_pp, dw_gla.reshape(4, 256, D), dw_out.reshape(4, 256, D)], out_axes, dpp)
    dzr, dwgrp, dscale = _pool_bwd(p, dpp, wgrp, pool_scale, token, dzr)
    out_pf, out_pb = partials("out", out_names, out_axes, out_x, dwgrp)
    out_plan = _reduce_plan(3, 0)
    out_handle, token = _split_start("reduce_out_start", out_pb, [((3,) + p_.shape[1:], BF16) for p_ in out_pb],
                                     out_plan, 9, out_pf[0])
    dq, dk, dzr, dgpre = _gla_bwd(zr, do, sp, wgk_pad, b_gk, token, dzr, ts)
    dzr, dwgk, dbgk = _gk_bwd(dgpre, zr, wgk_pad, dgpre, dzr, ts)
    dzr = lax.dynamic_update_slice(lax.dynamic_update_slice(dzr, dq, (0, OFF_Q)), dk, (0, OFF_K))
    dw_rt = _matmul_tn(dzr, h, "dw_in", D, tm=PROJ_TILE)

    def grad_rows(lo, hi):
        out = []
        for seg_lo, seg_hi, at in ((0, 1536, OFF_POOL), (1536, 3584, OFF_V), (3584, 3600, OFF_GK), (3600, N_IN, OFF_GATE)):
            a_, b_ = max(lo, seg_lo), min(hi, seg_hi)
            if a_ < b_:
                out.append(dw_rt[at + a_ - seg_lo:at + b_ - seg_lo])
        return jnp.concatenate(out, axis=0)

    dw_in_t = jnp.stack([grad_rows(j * nsh, (j + 1) * nsh) for j in range(4)])

    ms = dict(w_in=m_in_t, w_pool_proj=m_w_pool_proj[0], w_gla_proj=m_w_gla_proj[0], w_out=m_w_out[0],
              w_up=m_w_up[0], w_down=m_w_down[0])
    vs = dict(w_in=v_in_t, w_pool_proj=v_w_pool_proj[0], w_gla_proj=v_w_gla_proj[0], w_out=v_w_out[0],
              w_up=v_w_up[0], w_down=v_w_down[0])
    grad, delta, new_m, new_v = {}, {}, {}, {}

    def finish(names, group_axes, part_f, landed):
        return [_finish_half(pf, rb, ax, "finish_" + n) for n, ax, pf, rb in zip(names, group_axes, part_f, landed)]

    def update(names, group_axes, halves, sib_halves):
        for n, ax, mine, theirs in zip(names, group_axes, halves, sib_halves):
            res = _adam_halves(place, shards[n], mine, theirs, ms[n], vs[n], ax, "adam_" + n)
            if n == "w_in":
                res = [jnp.transpose(r_) for r_ in res]
            grad[n], delta[n], new_m[n], new_v[n] = [r_[None] for r_ in res]

    rest_names, rest_axes = ffn_names + out_names, ffn_axes + out_axes
    in_x, token = exchange_start("in", [dw_in_t], (1,), dw_rt)
    _, ffn_landed = _split_wait("reduce_ffn_wait", ffn_handle, 2, ffn_plan, token)
    _, out_landed = _split_wait("reduce_out_wait", out_handle, 3, out_plan, ffn_landed[0])
    rest_halves = lax.optimization_barrier(finish(rest_names, rest_axes, ffn_pf + out_pf, ffn_landed + out_landed))
    (in_pf,), (in_pb,) = partials("in", ("w_in",), (1,), in_x, rest_halves[-1])
    in_plan = _reduce_plan(1, 0)
    in_handle, token = _split_start("reduce_in_start", [in_pb], [((3,) + in_pb.shape[1:], BF16)], in_plan, 3, in_pf)
    rest_plan = _whole_to_sibling_plan(len(rest_halves))
    rest_share, token = _split_start("sibling_share_rest_start", rest_halves, [(h_.shape, F32) for h_ in rest_halves],
                                     rest_plan, len(rest_halves), token)
    grad_x, _, dgmix = _matmul_nt_normbwd(dzr, w_in_t, xs, g_mix, dx1, token, "in_proj_bwd", ts, transposed=True)
    small_names = ("g_mix", "b_gate", "w_gk_up", "b_gk", "w_pool_grp", "pool_scale", "g_gla_head", "g_ffn", "w_conv",
                   "b_conv", "g_final")
    small_mine = [dgmix, dbgate, dwgk[:GATE_RANK], dbgk, dwgrp.reshape(4 * 128, 128), dscale, dghead, dgffn, dwconv, dbconv,
                  dgfin, loss_part]
    small_sib = _sibling_exchange([], (), small_mine, "sibling_exchange_small")
    small_chip = _add_many(small_mine, small_sib, "chip_partial_small")
    small_plan = _reduce_plan(0, len(small_chip))
    small_handle, token = _split_start("reduce_small_start", small_chip, [((4,) + a_.shape, F32) for a_ in small_chip],
                                       small_plan, 3 * len(small_chip), small_mine[0])

    rest_halves, rest_sib = _split_wait("sibling_share_rest_wait", rest_share, len(rest_halves), rest_plan, token)
    n_ffn = len(ffn_names)
    update(out_names, out_axes, rest_halves[n_ffn:], rest_sib[n_ffn:])
    updated = lax.optimization_barrier([delta[n] for n in out_names])
    _, in_landed = _split_wait("reduce_in_wait", in_handle, 1, in_plan, updated[0])
    in_halves = finish(("w_in",), (1,), (in_pf,), in_landed)
    update(("w_in",), (1,), in_halves, _sibling_share(in_halves, "sibling_share_in"))
    ffn_halves, _ = lax.optimization_barrier((rest_halves[:n_ffn], delta["w_in"]))
    update(ffn_names, ffn_axes, ffn_halves, rest_sib[:n_ffn])
    small_sent, small_landed = _split_wait("reduce_small_wait", small_handle, len(small_chip), small_plan, delta["w_in"])
    given = dict(g_mix=(g_mix, m_g_mix, v_g_mix), b_gate=(b_gate, m_b_gate, v_b_gate), w_gk_up=(w_gk_up, m_w_gk_up, v_w_gk_up),
                 b_gk=(b_gk, m_b_gk, v_b_gk), w_pool_grp=(w_pool_grp, m_w_pool_grp, v_w_pool_grp),
                 pool_scale=(pool_scale, m_pool_scale, v_pool_scale), g_gla_head=(g_gla_head, m_g_gla_head, v_g_gla_head),
                 g_ffn=(g_ffn, m_g_ffn, v_g_ffn), w_conv=(w_conv, m_w_conv, v_w_conv), b_conv=(b_conv, m_b_conv, v_b_conv),
                 g_final=(g_final, m_g_final, v_g_final))
    flat2 = lambda a: a.reshape(-1, a.shape[-1])
    widths = [dict(w_gk_up=HK, w_conv=UP_SHARD).get(n) for n in small_names]
    totals, ds, mo, vo = _adam_small(place, small_sent, small_landed, *[[flat2(given[n][k]) for n in small_names] for k in range(3)],
                                     widths)
    loss = totals[-1][0, 0]
    for i, n in enumerate(small_names):
        shp = given[n][0].shape
        grad[n], delta[n], new_m[n], new_v[n] = [r_.reshape(shp) for r_ in (totals[i], ds[i], mo[i], vo[i])]

    order = ("g_mix", "w_in", "b_gate", "w_gk_up", "b_gk", "w_pool_grp", "pool_scale", "g_gla_head", "w_pool_proj",
             "w_gla_proj", "w_out", "g_ffn", "w_up", "w_conv", "b_conv", "w_down", "g_final")
    return (loss, grad_x[None], *[grad[n] for n in order], *[delta[n] for n in order], *[new_m[n] for n in order],
            *[new_v[n] for n in order])
```

```python
import jax
import jax.numpy as jnp
from jax import lax
from jax.experimental import pallas as pl
from jax.experimental.pallas import tpu as pltpu

F32 = jnp.float32
BF16 = jnp.bfloat16
MESH = pl.DeviceIdType.MESH

D = 1024
EPS = 1e-6
CHUNK = 64
POOL_W = 512
POOL_WINDOWS = (2, 4, 8, 16)
HEADS = 4
HK = 128
HV = 256
GATE_RANK = 16
D_FF = 2816
N_UP = 2 * D_FF
N_IN = 5648
QSCALE = HK ** -0.5
N_INR = 5760
OFF_GATE, OFF_V, OFF_OG, OFF_POOL, OFF_Q, OFF_K, OFF_GK = 0, 2048, 3072, 4096, 4608, 5120, 5632

ADAM_LR, ADAM_B1, ADAM_B2, ADAM_EPS, ADAM_WD, ADAM_STEP = 0.001, 0.9, 0.999, 1e-08, 0.01, 10

VMEM_LIMIT = 56 * 1024 * 1024
PROJ_TILE = N_INR // 5
UP_SHARD = N_UP // 4


def _cp(*sem):
    return pltpu.CompilerParams(dimension_semantics=sem if sem else None, vmem_limit_bytes=VMEM_LIMIT)


def _dot(a, b):
    return jnp.dot(a, b, preferred_element_type=F32)


def _dot_nt(a, b):
    return lax.dot_general(a, b, (((1,), (1,)), ((), ())), preferred_element_type=F32)


def _dot_tn(a, b):
    return lax.dot_general(a, b, (((0,), (0,)), ((), ())), preferred_element_type=F32)


def _sigmoid(v):
    return 1.0 / (1.0 + jnp.exp(-v))


def _rows(shape):
    return lax.broadcasted_iota(jnp.int32, shape, 0)


def _pick_row(v, r):
    return jnp.sum(jnp.where(_rows(v.shape) == r, v, 0.0), axis=0, keepdims=True)


def _rmsnorm(x, g, after, name, ts):
    s = x.shape[0]

    def body(x_ref, g_ref, after_ref, h_ref):
        xv = x_ref[...]
        r = lax.rsqrt(jnp.mean(xv * xv, axis=-1, keepdims=True) + EPS)
        h_ref[...] = (xv * r * g_ref[...]).astype(BF16)

    return pl.pallas_call(
        body, name=name, grid=(s // ts,),
        in_specs=[pl.BlockSpec((ts, D), lambda i: (i, 0)), pl.BlockSpec((1, D), lambda i: (0, 0)), ANY],
        out_specs=pl.BlockSpec((ts, D), lambda i: (i, 0)), out_shape=jax.ShapeDtypeStruct((s, D), BF16),
        compiler_params=_cp("arbitrary"),
    )(x, g, after)


MM_ROWS = 512


def _matmul_resident(h, w, after, name):
    s = h.shape[0]
    nj, tn = w.shape[0], w.shape[2]
    rc = min(s, MM_ROWS)

    def body(h_ref, w_ref, after_ref, z_ref):
        for r0 in range(0, s, rc):
            z_ref[r0:r0 + rc, :] = _dot(h_ref[r0:r0 + rc, :], w_ref[...]).astype(BF16)

    return pl.pallas_call(
        body, name=name, grid=(nj,),
        in_specs=[pl.BlockSpec((s, D), lambda j: (0, 0)), pl.BlockSpec((None, D, tn), lambda j: (j, 0, 0)), ANY],
        out_specs=pl.BlockSpec((s, tn), lambda j: (0, j)), out_shape=jax.ShapeDtypeStruct((s, nj * tn), BF16),
        compiler_params=_cp("arbitrary"),
    )(h, w, after)


PROJ_PIECES = ((3600, 2048, OFF_GATE), (1536, 2048, OFF_V), (0, 1536, OFF_POOL), (3584, GATE_RANK, OFF_GK))


def _projection_copies(w_hbm, w_ref, sems):
    return [pltpu.make_async_copy(w_hbm.at[pl.ds(src, n)], w_ref.at[pl.ds(dst, n)], sems.at[i])
            for i, (src, n, dst) in enumerate(PROJ_PIECES)]


def _load_projection(w_hbm, w_ref, sems):
    cps = _projection_copies(w_hbm, w_ref, sems)
    for cp in cps:
        cp.start()
    w_ref[OFF_GK + GATE_RANK:, :] = jnp.zeros((N_INR - OFF_GK - GATE_RANK, D), BF16)
    for cp in cps:
        cp.wait()


def _in_proj(h, w_nat, tn):
    s = h.shape[0]
    rc = min(s, MM_ROWS)
    nj = N_INR // tn
    first_use = [dst // tn for _, _, dst in PROJ_PIECES]

    def body(h_ref, w_hbm, z_ref, w_ref, sems):
        j = pl.program_id(0)
        cps = _projection_copies(w_hbm, w_ref, sems)

        @pl.when(j == 0)
        def _():
            for cp in cps:
                cp.start()
            w_ref[OFF_GK + GATE_RANK:, :] = jnp.zeros((N_INR - OFF_GK - GATE_RANK, D), BF16)

        for step in range(nj):
            due = [cp for cp, at in zip(cps, first_use) if at == step]
            if due:
                @pl.when(j == step)
                def _(due=due):
                    for cp in due:
                        cp.wait()

        wt = w_ref[pl.ds(pl.multiple_of(j * tn, 128), tn), :]
        for r0 in range(0, s, rc):
            z_ref[r0:r0 + rc, :] = _dot_nt(h_ref[r0:r0 + rc, :], wt).astype(BF16)

    return pl.pallas_call(
        body, name="in_proj", grid=(nj,),
        in_specs=[pl.BlockSpec((s, D), lambda j: (0, 0)), ANY],
        out_specs=pl.BlockSpec((s, tn), lambda j: (0, j)), out_shape=jax.ShapeDtypeStruct((s, N_INR), BF16),
        scratch_shapes=[pltpu.VMEM((N_INR, D), BF16), pltpu.SemaphoreType.DMA((len(PROJ_PIECES),))],
        compiler_params=_cp("arbitrary"),
    )(h, w_nat)


def _matmul_nt_normbwd(dz, w, x, g, resid, after, name, ts, transposed=False):
    s = x.shape[0]
    w_vmem = (N_INR, D) if transposed else (D, w.shape[0] * w.shape[2])
    n_sems = len(PROJ_PIECES) if transposed else w.shape[0]

    def body(dz_ref, w_hbm, x_ref, g_ref, r_ref, after_ref, o_ref, ob_ref, dg_ref, w_ref, sems):
        @pl.when(pl.program_id(0) == 0)
        def _():
            if transposed:
                _load_projection(w_hbm, w_ref, sems)
            else:
                kc = w.shape[2]
                cps = [pltpu.make_async_copy(w_hbm.at[j], w_ref.at[:, pl.ds(j * kc, kc)], sems.at[j])
                       for j in range(w.shape[0])]
                for cp in cps:
                    cp.start()
                for cp in cps:
                    cp.wait()
            dg_ref[...] = jnp.zeros_like(dg_ref)

        dh = _dot(dz_ref[...], w_ref[...]) if transposed else _dot_nt(dz_ref[...], w_ref[...])
        xv = x_ref[...]
        r = lax.rsqrt(jnp.mean(xv * xv, axis=-1, keepdims=True) + EPS)
        xh = xv * r
        dg_ref[...] += jnp.sum(dh * xh, axis=0, keepdims=True)
        dxh = dh * g_ref[...]
        out = r_ref[...] + r * (dxh - xh * jnp.mean(dxh * xh, axis=-1, keepdims=True))
        o_ref[...] = out
        ob_ref[...] = out.astype(BF16)

    row = lambda i: (i, 0)
    kdim = dz.shape[1]
    return pl.pallas_call(
        body, name=name, grid=(s // ts,),
        in_specs=[pl.BlockSpec((ts, kdim), row), ANY, pl.BlockSpec((ts, D), row),
                  pl.BlockSpec((1, D), lambda i: (0, 0)), pl.BlockSpec((ts, D), row), ANY],
        out_specs=[pl.BlockSpec((ts, D), row), pl.BlockSpec((ts, D), row), pl.BlockSpec((1, D), lambda i: (0, 0))],
        out_shape=[jax.ShapeDtypeStruct((s, D), F32), jax.ShapeDtypeStruct((s, D), BF16),
                   jax.ShapeDtypeStruct((1, D), F32)],
        scratch_shapes=[pltpu.VMEM(w_vmem, BF16), pltpu.SemaphoreType.DMA((n_sems,))],
        compiler_params=_cp("arbitrary"),
    )(dz, w, x, g, resid, after)


def _matmul_tn(a, b, name, tn, shard_major=False, tm=None):
    s, m = a.shape
    n = b.shape[1]
    tm = m if tm is None else tm
    ni, nj = m // tm, n // tn

    def body(a_ref, b_ref, o_ref):
        o_ref[...] = _dot_tn(a_ref[...], b_ref[...]).astype(BF16)

    if shard_major:
        out_spec = pl.BlockSpec((None, tm, tn), lambda i, j: (j, i, 0))
        out_shape = jax.ShapeDtypeStruct((nj, m, tn), BF16)
    else:
        out_spec = pl.BlockSpec((tm, tn), lambda i, j: (i, j))
        out_shape = jax.ShapeDtypeStruct((m, n), BF16)
    return pl.pallas_call(
        body, name=name, grid=(ni, nj),
        in_specs=[pl.BlockSpec((s, tm), lambda i, j: (0, i)), pl.BlockSpec((s, tn), lambda i, j: (0, j))],
        out_specs=out_spec, out_shape=out_shape,
        compiler_params=_cp("arbitrary", "arbitrary"),
    )(a, b)


def _pool_fwd(zr, wgrp, scale):
    s = zr.shape[0]

    def body(u_ref, w_ref, sc_ref, p_ref, pp_ref):
        row = _rows((s, 128))
        for gi, win in enumerate(POOL_WINDOWS):
            cs = slice(gi * 128, (gi + 1) * 128)
            u = u_ref[:, cs].astype(F32)
            acc, k = u, 1
            while k < win:
                acc = acc + jnp.where(row >= k, pltpu.roll(acc, k, 0), 0.0)
                k *= 2
            cnt = jnp.minimum(row + 1, win).astype(F32)
            p = (acc / cnt - u).astype(BF16)
            p_ref[:, cs] = p
            pp_ref[:, cs] = (_dot(p, w_ref[gi].astype(BF16)) * sc_ref[:, cs]).astype(BF16)

    return pl.pallas_call(
        body, name="pool_fwd", grid=(1,),
        in_specs=[pl.BlockSpec((s, POOL_W), lambda i: (0, OFF_POOL // POOL_W)),
                  pl.BlockSpec((4, 128, 128), lambda i: (0, 0, 0)), pl.BlockSpec((1, POOL_W), lambda i: (0, 0))],
        out_specs=[pl.BlockSpec((s, POOL_W), lambda i: (0, 0))] * 2,
        out_shape=[jax.ShapeDtypeStruct((s, POOL_W), BF16)] * 2,
        compiler_params=_cp("arbitrary"),
    )(zr, wgrp, scale)


def _pool_bwd(p, dpp, wgrp, scale, after, dz):
    s = p.shape[0]

    def body(p_ref, dpp_ref, w_ref, sc_ref, after_ref, dz_in, dz_ref, dw_ref, dsc_ref):
        row = _rows((s, 128))
        for gi, win in enumerate(POOL_WINDOWS):
            cs = slice(gi * 128, (gi + 1) * 128)
            pv = p_ref[:, cs]
            wb = w_ref[gi].astype(BF16)
            dpp_v = dpp_ref[:, cs].astype(F32)
            dsc_ref[:, cs] = jnp.sum(dpp_v * _dot(pv, wb), axis=0, keepdims=True)
            dpm = (dpp_v * sc_ref[:, cs]).astype(BF16)
            dw_ref[gi] = _dot_tn(pv, dpm)
            dp = _dot_nt(dpm, wb)
            cnt = jnp.minimum(row + 1, win).astype(F32)
            acc, k = dp / cnt, 1
            while k < win:
                acc = acc + jnp.where(row < s - k, pltpu.roll(acc, s - k, 0), 0.0)
                k *= 2
            dz_ref[:, cs] = (acc - dp).astype(BF16)

    full = lambda i: (0, 0)
    return pl.pallas_call(
        body, name="pool_bwd", grid=(1,),
        in_specs=[pl.BlockSpec((s, POOL_W), full), pl.BlockSpec((s, POOL_W), full),
                  pl.BlockSpec((4, 128, 128), lambda i: (0, 0, 0)), pl.BlockSpec((1, POOL_W), full), ANY, ANY],
        out_specs=[pl.BlockSpec((s, POOL_W), lambda i: (0, OFF_POOL // POOL_W)),
                   pl.BlockSpec((4, 128, 128), lambda i: (0, 0, 0)), pl.BlockSpec((1, POOL_W), full)],
        out_shape=[jax.ShapeDtypeStruct(dz.shape, BF16), jax.ShapeDtypeStruct((4, 128, 128), F32),
                   jax.ShapeDtypeStruct((1, POOL_W), F32)],
        input_output_aliases={5: 0},
        compiler_params=_cp("arbitrary"),
    )(p, dpp, wgrp, scale, after, dz)


def _gla_decay(zgk_ref, wgk_ref, bgk_ref, rb):
    g = _dot(zgk_ref[...], wgk_ref[...].astype(BF16)) + bgk_ref[...]
    la = (jnp.minimum(g, 0.0) - jnp.log(1.0 + jnp.exp(-jnp.abs(g)))) * (1.0 / 16.0)
    rowm = _rows(la.shape) & (CHUNK - 1)
    bc, k = la, 1
    while k < CHUNK:
        bc = bc + jnp.where(rowm >= k, pltpu.roll(bc, k, 0), 0.0)
        k *= 2
    return g, jnp.exp(bc), jnp.exp(-bc)


GLA_HB = 4


def _gla_specs(rb, rmap):
    wk, wv = GLA_HB * HK, GLA_HB * HV
    return [pl.BlockSpec((rb, wk), lambda h, r: (rmap(h, r), OFF_Q // wk + h)),
            pl.BlockSpec((rb, wk), lambda h, r: (rmap(h, r), OFF_K // wk + h)),
            pl.BlockSpec((rb, wv), lambda h, r: (rmap(h, r), OFF_V // wv + h)),
            pl.BlockSpec((rb, 128), lambda h, r: (rmap(h, r), OFF_GK // 128))]


def _gla_fwd(zr, wgk, bgk, ghead, rb):
    s = zr.shape[0]
    nc = rb // CHUNK
    wk, wv = GLA_HB * HK, GLA_HB * HV

    def body(q_ref, k_ref, v_ref, zgk_ref, zog_ref, wgk_ref, bgk_ref, gh_ref, o_ref, og_ref, sp_ref, st_ref, kv_ref):
        @pl.when(pl.program_id(1) == 0)
        def _():
            st_ref[...] = jnp.zeros_like(st_ref)

        _, e_pos, e_neg = _gla_decay(zgk_ref, wgk_ref, bgk_ref, rb)
        lower = _rows((CHUNK, CHUNK)) >= lax.broadcasted_iota(jnp.int32, (CHUNK, CHUNK), 1)
        pairs = [(c, hh) for c in range(nc) for hh in range(GLA_HB)]
        rows = lambda c: slice(c * CHUNK, (c + 1) * CHUNK)
        cols_k = lambda hh: slice(hh * HK, (hh + 1) * HK)
        cols_v = lambda hh: slice(hh * HV, (hh + 1) * HV)
        qfws, pms, e_lasts = {}, {}, {}
        for c, hh in pairs:
            q = q_ref[rows(c), cols_k(hh)].astype(F32) * QSCALE
            k = k_ref[rows(c), cols_k(hh)].astype(F32)
            ec, fc = e_pos[rows(c), cols_k(hh)], e_neg[rows(c), cols_k(hh)]
            qfw = (q * ec).astype(BF16)
            kfw_f = k * fc
            s_fw = _dot_nt(qfw, kfw_f.astype(BF16))
            s_bw = _dot_nt((q * fc).astype(BF16), (k * ec).astype(BF16))
            e_last = _pick_row(ec, CHUNK - 1)
            kv_ref[c, hh] = _dot_tn(v_ref[rows(c), cols_v(hh)], (kfw_f * e_last).astype(BF16))
            qfws[c, hh], pms[c, hh], e_lasts[c, hh] = qfw, jnp.where(lower, s_fw, s_bw).astype(BF16), e_last
        for hh in range(GLA_HB):
            st = st_ref[hh]
            for c in range(nc):
                sp_ref[c, hh] = st.astype(BF16)
                st = st * e_lasts[c, hh] + kv_ref[c, hh]
            st_ref[hh] = st
        for c, hh in pairs:
            o = _dot(pms[c, hh], v_ref[rows(c), cols_v(hh)]) + _dot_nt(qfws[c, hh], sp_ref[c, hh])
            r = lax.rsqrt(jnp.mean(o * o, axis=-1, keepdims=True) + EPS)
            zo = zog_ref[rows(c), cols_v(hh)].astype(F32)
            o_ref[rows(c), cols_v(hh)] = o.astype(BF16)
            og_ref[rows(c), cols_v(hh)] = (o * r * gh_ref[...] * zo * _sigmoid(zo)).astype(BF16)

    rmap = lambda h, r: r
    return pl.pallas_call(
        body, name="gla_fwd", grid=(HEADS // GLA_HB, s // rb),
        in_specs=_gla_specs(rb, rmap) + [
            pl.BlockSpec((rb, wv), lambda h, r: (r, OFF_OG // wv + h)),
            pl.BlockSpec((128, wk), lambda h, r: (0, h)), pl.BlockSpec((1, wk), lambda h, r: (0, h)),
            pl.BlockSpec((1, HV), lambda h, r: (0, 0))],
        out_specs=[pl.BlockSpec((rb, wv), lambda h, r: (r, h)), pl.BlockSpec((rb, wv), lambda h, r: (r, h)),
                   pl.BlockSpec((nc, GLA_HB, HV, HK), lambda h, r: (r, h, 0, 0))],
        out_shape=[jax.ShapeDtypeStruct((s, D), BF16), jax.ShapeDtypeStruct((s, D), BF16),
                   jax.ShapeDtypeStruct((s // CHUNK, HEADS, HV, HK), BF16)],
        scratch_shapes=[pltpu.VMEM((GLA_HB, HV, HK), F32), pltpu.VMEM((nc, GLA_HB, HV, HK), F32)],
        compiler_params=_cp("arbitrary", "arbitrary"),
    )(zr, zr, zr, zr, zr, wgk, bgk, ghead)


def _gla_bwd(zr, do, sp, wgk, bgk, after, dz, rb):
    s = zr.shape[0]
    nc = rb // CHUNK
    nr = s // rb
    wk, wv = GLA_HB * HK, GLA_HB * HV

    def body(q_ref, k_ref, v_ref, zgk_ref, do_ref, sp_ref, wgk_ref, bgk_ref, after_ref, dz_in, dq_ref, dk_ref, dv_ref,
             dg_ref, gt_ref, dbc_ref, gs_ref):
        @pl.when(pl.program_id(1) == 0)
        def _():
            gt_ref[...] = jnp.zeros_like(gt_ref)

        g, e_pos, e_neg = _gla_decay(zgk_ref, wgk_ref, bgk_ref, rb)
        lower = _rows((CHUNK, CHUNK)) >= lax.broadcasted_iota(jnp.int32, (CHUNK, CHUNK), 1)
        is_last = _rows((CHUNK, HK)) == CHUNK - 1
        pairs = [(c, hh) for c in range(nc) for hh in range(GLA_HB)]
        rows = lambda c: slice(c * CHUNK, (c + 1) * CHUNK)
        cols_k = lambda hh: slice(hh * HK, (hh + 1) * HK)
        cols_v = lambda hh: slice(hh * HV, (hh + 1) * HV)
        e_lasts = {}
        for c, hh in pairs:
            ec = e_pos[rows(c), cols_k(hh)]
            qfw = (q_ref[rows(c), cols_k(hh)].astype(F32) * QSCALE * ec).astype(BF16)
            gs_ref[c, hh] = _dot_tn(do_ref[rows(c), cols_v(hh)], qfw)
            e_lasts[c, hh] = _pick_row(ec, CHUNK - 1)
        for hh in range(GLA_HB):
            gt = gt_ref[hh]
            for c in reversed(range(nc)):
                own = gs_ref[c, hh]
                gs_ref[c, hh] = gt
                gt = own + gt * e_lasts[c, hh]
            gt_ref[hh] = gt
        def decayed(c, hh):
            q = q_ref[rows(c), cols_k(hh)].astype(F32) * QSCALE
            k = k_ref[rows(c), cols_k(hh)].astype(F32)
            ec, fc = e_pos[rows(c), cols_k(hh)], e_neg[rows(c), cols_k(hh)]
            return ec, fc, q * ec, k * fc, q * fc, k * ec

        pms, dss = {}, {}
        for c, hh in pairs:
            _, _, qfw_f, kfw_f, qbw_f, kbw_f = decayed(c, hh)
            s_fw = _dot_nt(qfw_f.astype(BF16), kfw_f.astype(BF16))
            s_bw = _dot_nt(qbw_f.astype(BF16), kbw_f.astype(BF16))
            dp = _dot_nt(do_ref[rows(c), cols_v(hh)], v_ref[rows(c), cols_v(hh)])
            pms[c, hh] = jnp.where(lower, s_fw, s_bw).astype(BF16)
            dss[c, hh] = (jnp.where(lower, dp, 0.0).astype(BF16), jnp.where(lower, 0.0, dp).astype(BF16))
        for c, hh in pairs:
            sl, ck, cv = rows(c), cols_k(hh), cols_v(hh)
            v = v_ref[sl, cv]
            dov = do_ref[sl, cv]
            ec, fc, qfw_f, kfw_f, qbw_f, kbw_f = decayed(c, hh)
            qfw, kfw, qbw, kbw = qfw_f.astype(BF16), kfw_f.astype(BF16), qbw_f.astype(BF16), kbw_f.astype(BF16)
            pm = pms[c, hh]
            e_last = e_lasts[c, hh]
            kdec = (kfw_f * e_last).astype(BF16)
            gt = gs_ref[c, hh]
            gtb = gt.astype(BF16)
            spv = sp_ref[c, hh]
            dv_ref[sl, cv] = (_dot_tn(pm, dov) + _dot_nt(kdec, gtb)).astype(BF16)
            ds_fw, ds_bw = dss[c, hh]
            dqfw = _dot(ds_fw, kfw) + _dot(dov, spv)
            dkfw = _dot_tn(ds_fw, qfw)
            dqbw = _dot(ds_bw, kbw)
            dkbw = _dot_tn(ds_bw, qbw)
            dkdec = _dot(v, gtb)
            de_last = (jnp.sum(gt * spv.astype(F32), axis=0, keepdims=True)
                       + jnp.sum(dkdec * kfw_f, axis=0, keepdims=True))
            dkfw = dkfw + dkdec * e_last
            dq_ref[sl, ck] = ((dqfw * ec + dqbw * fc) * QSCALE).astype(BF16)
            dk_ref[sl, ck] = (dkfw * fc + dkbw * ec).astype(BF16)
            dbc = dqfw * qfw_f - dqbw * qbw_f + dkbw * kbw_f - dkfw * kfw_f
            dbc_ref[sl, ck] = dbc + jnp.where(is_last, de_last * e_last, 0.0)
        rowm = _rows((rb, wk)) & (CHUNK - 1)
        dla, kk = dbc_ref[...], 1
        while kk < CHUNK:
            dla = dla + jnp.where(rowm < CHUNK - kk, pltpu.roll(dla, rb - kk, 0), 0.0)
            kk *= 2
        dg_ref[...] = dla * (1.0 / 16.0) * _sigmoid(-g)

    rmap = lambda h, r: nr - 1 - r
    rev = lambda h, r: (nr - 1 - r, h)
    return pl.pallas_call(
        body, name="gla_bwd", grid=(HEADS // GLA_HB, nr),
        in_specs=_gla_specs(rb, rmap) + [
            pl.BlockSpec((rb, wv), rev),
            pl.BlockSpec((nc, GLA_HB, HV, HK), lambda h, r: (nr - 1 - r, h, 0, 0)),
            pl.BlockSpec((128, wk), lambda h, r: (0, h)), pl.BlockSpec((1, wk), lambda h, r: (0, h)), ANY, ANY],
        out_specs=[pl.BlockSpec((rb, wk), rev), pl.BlockSpec((rb, wk), rev),
                   pl.BlockSpec((rb, wv), lambda h, r: (nr - 1 - r, OFF_V // wv + h)), pl.BlockSpec((rb, wk), rev)],
        out_shape=[jax.ShapeDtypeStruct((s, HEADS * HK), BF16), jax.ShapeDtypeStruct((s, HEADS * HK), BF16),
                   jax.ShapeDtypeStruct(dz.shape, BF16), jax.ShapeDtypeStruct((s, HEADS * HK), F32)],
        scratch_shapes=[pltpu.VMEM((GLA_HB, HV, HK), F32), pltpu.VMEM((rb, wk), F32),
                        pltpu.VMEM((nc, GLA_HB, HV, HK), F32)],
        input_output_aliases={9: 2},
        compiler_params=_cp("arbitrary", "arbitrary"),
    )(zr, zr, zr, zr, do, sp, wgk, bgk, after, dz)


def _gk_bwd(dgpre, zr, wgk, after, dz, ts):
    s = zr.shape[0]

    def body(dg_ref, zgk_ref, w_ref, after_ref, dz_in, dz_ref, dw_ref, db_ref):
        @pl.when(pl.program_id(0) == 0)
        def _():
            dw_ref[...] = jnp.zeros_like(dw_ref)
            db_ref[...] = jnp.zeros_like(db_ref)

        dg = dg_ref[...]
        dgb = dg.astype(BF16)
        dz_ref[...] = _dot_nt(dgb, w_ref[...].astype(BF16)).astype(BF16)
        dw_ref[...] += _dot_tn(zgk_ref[...], dgb)
        db_ref[...] += jnp.sum(dg, axis=0, keepdims=True)

    return pl.pallas_call(
        body, name="gk_bwd", grid=(s // ts,),
        in_specs=[pl.BlockSpec((ts, 512), lambda i: (i, 0)), pl.BlockSpec((ts, 128), lambda i: (i, OFF_GK // 128)),
                  pl.BlockSpec((128, 512), lambda i: (0, 0)), ANY, ANY],
        out_specs=[pl.BlockSpec((ts, 128), lambda i: (i, OFF_GK // 128)), pl.BlockSpec((128, 512), lambda i: (0, 0)),
                   pl.BlockSpec((1, 512), lambda i: (0, 0))],
        out_shape=[jax.ShapeDtypeStruct(dz.shape, BF16), jax.ShapeDtypeStruct((128, 512), F32),
                   jax.ShapeDtypeStruct((1, 512), F32)],
        input_output_aliases={4: 0},
        compiler_params=_cp("arbitrary"),
    )(dgpre, zr, wgk, after, dz)


def _merge_fwd(x, zr, pp, og, bgate, wpp, wgla, wout, gffn, after, ts):
    s = x.shape[0]

    def body(x_ref, z0_ref, z1_ref, pp_ref, og_ref, bg_ref, wpp_ref, wgla_ref, wout_ref, gf_ref, after_ref,
             x1_ref, mix_ref, yp_ref, yg_ref, h2_ref):
        ppv = pp_ref[...]
        yp = jnp.concatenate([_dot(ppv, wpp_ref[j]) for j in range(4)], axis=1)
        yg = _dot(og_ref[...], wgla_ref[...])
        g0 = _sigmoid(z0_ref[...].astype(F32) + bg_ref[:, :D])
        g1 = _sigmoid(z1_ref[...].astype(F32) + bg_ref[:, D:])
        mixed = (g0 * yp + g1 * yg).astype(BF16)
        x1 = x_ref[...] + _dot(mixed, wout_ref[...])
        x1_ref[...] = x1
        mix_ref[...] = mixed
        yp_ref[...] = yp.astype(BF16)
        yg_ref[...] = yg.astype(BF16)
        r = lax.rsqrt(jnp.mean(x1 * x1, axis=-1, keepdims=True) + EPS)
        h2_ref[...] = (x1 * r * gf_ref[...]).astype(BF16)

    row = lambda i: (i, 0)
    const2 = lambda i: (0, 0)
    return pl.pallas_call(
        body, name="merge_fwd", grid=(s // ts,),
        in_specs=[pl.BlockSpec((ts, D), row), pl.BlockSpec((ts, D), lambda i: (i, 0)), pl.BlockSpec((ts, D), lambda i: (i, 1)),
                  pl.BlockSpec((ts, POOL_W), row), pl.BlockSpec((ts, D), row), pl.BlockSpec((1, 2 * D), const2),
                  pl.BlockSpec((4, POOL_W, 256), lambda i: (0, 0, 0)), pl.BlockSpec((D, D), const2),
                  pl.BlockSpec((D, D), const2), pl.BlockSpec((1, D), const2), ANY],
        out_specs=[pl.BlockSpec((ts, D), row)] * 5,
        out_shape=[jax.ShapeDtypeStruct((s, D), F32)] + [jax.ShapeDtypeStruct((s, D), BF16)] * 4,
        compiler_params=_cp("arbitrary"),
    )(x, zr, zr, pp, og, bgate, wpp, wgla, wout, gffn, after)


def _merge_bwd(dx1b, zr, yp, yg, o, bgate, ghead, wpp, wgla, wout, after, ts):
    s = dx1b.shape[0]

    def body(dx_ref, z0_ref, z1_ref, zog_ref, yp_ref, yg_ref, o_ref, bg_ref, gh_ref, wpp_ref, wgla_ref, wout_ref, after_ref,
             dzg_ref, dyp_ref, dyg_ref, dpp_ref, do_ref, dzog_ref, dbg_ref, dgh_ref):
        @pl.when(pl.program_id(0) == 0)
        def _():
            dbg_ref[...] = jnp.zeros_like(dbg_ref)
            dgh_ref[...] = jnp.zeros_like(dgh_ref)

        dmix = _dot_nt(dx_ref[...], wout_ref[...])
        g0 = _sigmoid(z0_ref[...].astype(F32) + bg_ref[:, :D])
        g1 = _sigmoid(z1_ref[...].astype(F32) + bg_ref[:, D:])
        dypb = (dmix * g0).astype(BF16)
        dygb = (dmix * g1).astype(BF16)
        dz0 = dmix * yp_ref[...].astype(F32) * g0 * (1.0 - g0)
        dz1 = dmix * yg_ref[...].astype(F32) * g1 * (1.0 - g1)
        dzg_ref[:, :D] = dz0.astype(BF16)
        dzg_ref[:, D:] = dz1.astype(BF16)
        dbg_ref[:, :D] += jnp.sum(dz0, axis=0, keepdims=True)
        dbg_ref[:, D:] += jnp.sum(dz1, axis=0, keepdims=True)
        dyp_ref[...] = dypb
        dyg_ref[...] = dygb
        dpp = _dot_nt(dypb[:, 0:256], wpp_ref[0])
        for j in range(1, 4):
            dpp = dpp + _dot_nt(dypb[:, j * 256:(j + 1) * 256], wpp_ref[j])
        dpp_ref[...] = dpp.astype(BF16)
        dog = _dot_nt(dygb, wgla_ref[...])
        gh = gh_ref[...]
        dgh = jnp.zeros((1, HV), F32)
        for h in range(HEADS):
            cs = slice(h * HV, (h + 1) * HV)
            ov = o_ref[:, cs].astype(F32)
            r = lax.rsqrt(jnp.mean(ov * ov, axis=-1, keepdims=True) + EPS)
            oh = ov * r
            zo = zog_ref[:, cs].astype(F32)
            sg = _sigmoid(zo)
            dog_h = dog[:, cs]
            don = dog_h * zo * sg
            dzog_ref[:, cs] = (dog_h * oh * gh * sg * (1.0 + zo * (1.0 - sg))).astype(BF16)
            dgh = dgh + jnp.sum(don * oh, axis=0, keepdims=True)
            doh = don * gh
            do_ref[:, cs] = (r * (doh - oh * jnp.mean(doh * oh, axis=-1, keepdims=True))).astype(BF16)
        dgh_ref[...] += dgh

    row = lambda i: (i, 0)
    const2 = lambda i: (0, 0)
    return pl.pallas_call(
        body, name="merge_bwd", grid=(s // ts,),
        in_specs=[pl.BlockSpec((ts, D), row), pl.BlockSpec((ts, D), lambda i: (i, 0)), pl.BlockSpec((ts, D), lambda i: (i, 1)),
                  pl.BlockSpec((ts, D), lambda i: (i, OFF_OG // D)), pl.BlockSpec((ts, D), row), pl.BlockSpec((ts, D), row),
                  pl.BlockSpec((ts, D), row), pl.BlockSpec((1, 2 * D), const2), pl.BlockSpec((1, HV), const2),
                  pl.BlockSpec((4, POOL_W, 256), lambda i: (0, 0, 0)), pl.BlockSpec((D, D), const2),
                  pl.BlockSpec((D, D), const2), ANY],
        out_specs=[pl.BlockSpec((ts, 2 * D), row), pl.BlockSpec((ts, D), row), pl.BlockSpec((ts, D), row),
                   pl.BlockSpec((ts, POOL_W), row), pl.BlockSpec((ts, D), row), pl.BlockSpec((ts, D), row),
                   pl.BlockSpec((1, 2 * D), const2), pl.BlockSpec((1, HV), const2)],
        out_shape=[jax.ShapeDtypeStruct((s, N_INR), BF16), jax.ShapeDtypeStruct((s, D), BF16),
                   jax.ShapeDtypeStruct((s, D), BF16), jax.ShapeDtypeStruct((s, POOL_W), BF16),
                   jax.ShapeDtypeStruct((s, D), BF16), jax.ShapeDtypeStruct((s, D), BF16),
                   jax.ShapeDtypeStruct((1, 2 * D), F32), jax.ShapeDtypeStruct((1, HV), F32)],
        compiler_params=_cp("arbitrary"),
    )(dx1b, zr, zr, zr, yp, yg, o, bgate, ghead, wpp, wgla, wout, after)


HALO = 16
CCH = D_FF // 2


def _conv_taps(u_ref, halo_ref, cs, first, ts):
    u = u_ref[:, cs].astype(F32)
    hal = halo_ref[:, cs].astype(F32)
    h1 = jnp.where(first, 0.0, _pick_row(hal, HALO - 1))
    h2 = jnp.where(first, 0.0, _pick_row(hal, HALO - 2))
    row8 = _rows((8, u.shape[1]))
    r1, r2 = pltpu.roll(u, 1, 0), pltpu.roll(u, 2, 0)
    r1 = jnp.concatenate([jnp.where(row8 == 0, h1, r1[:8]), r1[8:]], axis=0)
    r2 = jnp.concatenate([jnp.where(row8 == 0, h2, jnp.where(row8 == 1, h1, r2[:8])), r2[8:]], axis=0)
    return u, r1, r2


def _ffn_down_loss(u, x1, tgt, wconv, bconv, wdown, gfin, ts):
    s = x1.shape[0]

    def body(u_ref, halo_ref, x1_ref, t_ref, wc_ref, bc_ref, wd_ref, gf_ref, a_ref, c_ref, dx_ref, dxb_ref, ls_ref,
             dgf_ref):
        i = pl.program_id(0)

        @pl.when(i == 0)
        def _():
            ls_ref[...] = jnp.zeros_like(ls_ref)
            dgf_ref[...] = jnp.zeros_like(dgf_ref)

        first = i == 0
        acc = x1_ref[...]
        for hf in range(D_FF // CCH):
            cg = slice(hf * CCH, (hf + 1) * CCH)
            cv = slice(D_FF + hf * CCH, D_FF + (hf + 1) * CCH)
            vals = []
            for cs in (cg, cv):
                u0, u1, u2 = _conv_taps(u_ref, halo_ref, cs, first, ts)
                vals.append(bc_ref[:, cs] + wc_ref[0:1, cs] * u2 + wc_ref[1:2, cs] * u1 + wc_ref[2:3, cs] * u0)
                c_ref[:, cs] = vals[-1].astype(BF16)
            a = (vals[0] * _sigmoid(vals[0]) * vals[1]).astype(BF16)
            a_ref[:, cg] = a
            acc = acc + _dot(a, wd_ref[cg, :])
        r = lax.rsqrt(jnp.mean(acc * acc, axis=-1, keepdims=True) + EPS)
        xh = acc * r
        gf = gf_ref[...]
        err = xh * gf - t_ref[...]
        ls_ref[...] += (0.5 / D) * jnp.sum(jnp.sum(err * err, axis=-1, keepdims=True), axis=0, keepdims=True)
        dy = err * (1.0 / D)
        dgf_ref[...] += jnp.sum(dy * xh, axis=0, keepdims=True)
        dxh = dy * gf
        dx = r * (dxh - xh * jnp.mean(dxh * xh, axis=-1, keepdims=True))
        dx_ref[...] = dx
        dxb_ref[...] = dx.astype(BF16)

    row = lambda i: (i, 0)
    const2 = lambda i: (0, 0)
    return pl.pallas_call(
        body, name="ffn_down_loss", grid=(s // ts,),
        in_specs=[pl.BlockSpec((ts, N_UP), row),
                  pl.BlockSpec((HALO, N_UP), lambda i: (jnp.maximum(i * (ts // HALO) - 1, 0), 0)),
                  pl.BlockSpec((ts, D), row), pl.BlockSpec((ts, D), row), pl.BlockSpec((3, N_UP), const2),
                  pl.BlockSpec((1, N_UP), const2), pl.BlockSpec((D_FF, D), const2), pl.BlockSpec((1, D), const2)],
        out_specs=[pl.BlockSpec((ts, D_FF), row), pl.BlockSpec((ts, N_UP), row), pl.BlockSpec((ts, D), row),
                   pl.BlockSpec((ts, D), row), pl.BlockSpec((1, 128), const2), pl.BlockSpec((1, D), const2)],
        out_shape=[jax.ShapeDtypeStruct((s, D_FF), BF16), jax.ShapeDtypeStruct((s, N_UP), BF16),
                   jax.ShapeDtypeStruct((s, D), F32), jax.ShapeDtypeStruct((s, D), BF16),
                   jax.ShapeDtypeStruct((1, 128), F32), jax.ShapeDtypeStruct((1, D), F32)],
        compiler_params=_cp("arbitrary"),
    )(u, u, x1, tgt, wconv, bconv, wdown, gfin)


def _ffn_bwd(dx2b, u, c, wconv, wdown, ts):
    s = dx2b.shape[0]
    nt = s // ts

    def body(dx_ref, u_ref, c_ref, wc_ref, wd_ref, du_ref, db_ref, dw_ref, nxt_ref):
        @pl.when(pl.program_id(0) == 0)
        def _():
            db_ref[...] = jnp.zeros_like(db_ref)
            dw_ref[...] = jnp.zeros_like(dw_ref)
            nxt_ref[...] = jnp.zeros_like(nxt_ref)

        dxv = dx_ref[...]
        row8 = _rows((8, CCH))
        for hf in range(D_FF // CCH):
            cg = slice(hf * CCH, (hf + 1) * CCH)
            cv = slice(D_FF + hf * CCH, D_FF + (hf + 1) * CCH)
            da = _dot_nt(dxv, wd_ref[cg, :])
            gate = c_ref[:, cg].astype(F32)
            val = c_ref[:, cv].astype(F32)
            sg = _sigmoid(gate)
            dcs = (da * val * sg * (1.0 + gate * (1.0 - sg)), da * gate * sg)
            for cs, dc in zip((cg, cv), dcs):
                n1 = nxt_ref[0:1, cs]
                n2 = nxt_ref[1:2, cs]
                r1, r2 = pltpu.roll(dc, ts - 1, 0), pltpu.roll(dc, ts - 2, 0)
                f1 = jnp.concatenate([r1[:ts - 8], jnp.where(row8 == 7, n1, r1[ts - 8:])], axis=0)
                f2 = jnp.concatenate([r2[:ts - 8], jnp.where(row8 == 7, n2, jnp.where(row8 == 6, n1, r2[ts - 8:]))], axis=0)
                uv = u_ref[:, cs].astype(F32)
                db_ref[:, cs] += jnp.sum(dc, axis=0, keepdims=True)
                dw_ref[0:1, cs] += jnp.sum(f2 * uv, axis=0, keepdims=True)
                dw_ref[1:2, cs] += jnp.sum(f1 * uv, axis=0, keepdims=True)
                dw_ref[2:3, cs] += jnp.sum(dc * uv, axis=0, keepdims=True)
                du_ref[:, cs] = (wc_ref[2:3, cs] * dc + wc_ref[1:2, cs] * f1 + wc_ref[0:1, cs] * f2).astype(BF16)
                nxt_ref[:, cs] = dc[0:8, :]

    rev = lambda i: (nt - 1 - i, 0)
    const2 = lambda i: (0, 0)
    return pl.pallas_call(
        body, name="ffn_bwd", grid=(nt,),
        in_specs=[pl.BlockSpec((ts, D), rev), pl.BlockSpec((ts, N_UP), rev), pl.BlockSpec((ts, N_UP), rev),
                  pl.BlockSpec((3, N_UP), const2), pl.BlockSpec((D_FF, D), const2)],
        out_specs=[pl.BlockSpec((ts, N_UP), rev), pl.BlockSpec((1, N_UP), const2), pl.BlockSpec((3, N_UP), const2)],
        out_shape=[jax.ShapeDtypeStruct((s, N_UP), BF16), jax.ShapeDtypeStruct((1, N_UP), F32),
                   jax.ShapeDtypeStruct((3, N_UP), F32)],
        scratch_shapes=[pltpu.VMEM((8, N_UP), F32)],
        compiler_params=_cp("arbitrary"),
    )(dx2b, u, c, wconv, wdown)


ANY = pl.BlockSpec(memory_space=pl.ANY)


def _place():
    x, y, c = lax.axis_index("x"), lax.axis_index("y"), lax.axis_index("c")
    chips = [(1 - x, y), (x, 1 - y), (1 - x, 1 - y)]
    return x, y, c, chips


def _half(shape, c, axis):
    size = shape[axis] // 2
    cut = pl.ds(pl.multiple_of(c * size, 8 if axis == 0 else 128), size)
    return (cut, slice(None)) if axis == 0 else (slice(None), cut)


def _half_shape(shape, axis):
    return (shape[0] // 2, shape[1]) if axis == 0 else (shape[0], shape[1] // 2)


def _remote(src, dst, send_sems, recv_sems, k, to):
    return pltpu.make_async_remote_copy(src_ref=src, dst_ref=dst, send_sem=send_sems.at[k], recv_sem=recv_sems.at[k],
                                        device_id=to, device_id_type=MESH)


def _sibling_exchange(grads, axes, smalls, name):
    nb = len(grads)
    n = nb + len(smalls)

    def body(*refs):
        ins, outs = refs[:n], refs[n:2 * n]
        send_sems, recv_sems = refs[2 * n:]
        x, y, c, _ = _place()
        sib = (x, y, 1 - c)
        cps = []
        for a in range(nb):
            theirs = _half(grads[a].shape[1:], 1 - c, axes[a])
            cps.append(_remote(ins[a].at[(slice(None),) + theirs], outs[a], send_sems, recv_sems, a, sib))
        for a in range(nb, n):
            cps.append(_remote(ins[a], outs[a], send_sems, recv_sems, a, sib))
        for cp in cps:
            cp.start()
        for cp in cps:
            cp.wait()

    out_shape = [jax.ShapeDtypeStruct((4,) + _half_shape(g.shape[1:], ax), g.dtype) for g, ax in zip(grads, axes)]
    out_shape += [jax.ShapeDtypeStruct(a.shape, F32) for a in smalls]
    return pl.pallas_call(
        body, name=name, in_specs=[ANY] * n, out_specs=[ANY] * n, out_shape=out_shape,
        scratch_shapes=[pltpu.SemaphoreType.DMA((n,)), pltpu.SemaphoreType.DMA((n,))],
        compiler_params=pltpu.CompilerParams(has_side_effects=True),
    )(*grads, *smalls)


def _gather_share(lands, axes, name):
    n = len(lands)

    def body(*refs):
        outs = refs[n:2 * n]
        send_sems, recv_sems = refs[2 * n:]
        x, y, c, chips = _place()
        sib = (x, y, 1 - c)
        cps = []
        for a in range(n):
            mine = _half(lands[a].shape[1:], c, axes[a])
            for k, ch in enumerate(chips):
                landed = outs[a].at[(2 * ch[0] + ch[1],) + mine]
                cps.append(_remote(landed, landed, send_sems, recv_sems, 3 * a + k, sib))
        for cp in cps:
            cp.start()
        for a in range(n):
            other = _half(lands[a].shape[1:], 1 - c, axes[a])
            for k, ch in enumerate(chips):
                landed = outs[a].at[(2 * ch[0] + ch[1],) + other]
                _remote(landed, landed, send_sems, recv_sems, 3 * a + k, sib).wait_recv()
        for cp in cps:
            cp.wait_send()

    return pl.pallas_call(
        body, name=name, in_specs=[ANY] * n, out_specs=[ANY] * n,
        out_shape=[jax.ShapeDtypeStruct(a.shape, a.dtype) for a in lands],
        input_output_aliases={a: a for a in range(n)},
        scratch_shapes=[pltpu.SemaphoreType.DMA((3 * n,)), pltpu.SemaphoreType.DMA((3 * n,))],
        compiler_params=pltpu.CompilerParams(has_side_effects=True),
    )(*lands)


def _sibling_share(halves, name):
    n = len(halves)

    def body(*refs):
        ins, outs = refs[:n], refs[n:2 * n]
        send_sems, recv_sems = refs[2 * n:]
        x, y, c, _ = _place()
        cps = [_remote(ins[a], outs[a], send_sems, recv_sems, a, (x, y, 1 - c)) for a in range(n)]
        for cp in cps:
            cp.start()
        for cp in cps:
            cp.wait()

    return pl.pallas_call(
        body, name=name, in_specs=[ANY] * n, out_specs=[ANY] * n,
        out_shape=[jax.ShapeDtypeStruct(h.shape, F32) for h in halves],
        scratch_shapes=[pltpu.SemaphoreType.DMA((n,)), pltpu.SemaphoreType.DMA((n,))],
        compiler_params=pltpu.CompilerParams(has_side_effects=True),
    )(*halves)


HBM = pl.BlockSpec(memory_space=pltpu.HBM)
SEM = pl.BlockSpec(memory_space=pltpu.SEMAPHORE)
DATAFLOW = pltpu.SideEffectType.DATAFLOW_SIDE_EFFECTING


def _split_start(name, srcs, land_shapes, plan, n_copies, after):
    lands = [lax.empty(*ls) if isinstance(ls, tuple) else ls for ls in land_shapes]
    bufs = list(srcs) + lands
    nb, ns = len(bufs), len(srcs)

    def body(*refs):
        send_sems, recv_sems, token = refs[nb + 1], refs[nb + 2], refs[-1]
        for k, (src, dst, to) in enumerate(plan(refs[:ns], refs[ns:nb])):
            _remote(src, dst, send_sems, recv_sems, k, to).start()
        token[...] = jnp.zeros_like(token)

    res = pl.pallas_call(
        body, name=name,
        out_shape=(pltpu.SemaphoreType.DMA((n_copies,)), pltpu.SemaphoreType.DMA((n_copies,)),
                   *[pltpu.HBM(b.shape, b.dtype) for b in bufs], jax.ShapeDtypeStruct((8, 128), F32)),
        in_specs=[HBM] * nb + [ANY],
        out_specs=(SEM, SEM, *[HBM] * nb, pl.BlockSpec(memory_space=pltpu.VMEM)),
        input_output_aliases={i: 2 + i for i in range(nb)},
        compiler_params=pltpu.CompilerParams(has_side_effects=DATAFLOW),
    )(*[pltpu.with_memory_space_constraint(b, pltpu.HBM) for b in bufs], after)
    return (res[0], res[1], list(res[2:2 + nb])), res[-1]


def _split_wait(name, handle, n_srcs, plan, after):
    send_sems, recv_sems, bufs = handle
    nb = len(bufs)

    def body(*refs):
        sends, recvs = refs[nb], refs[nb + 1]
        for k, (src, dst, to) in enumerate(plan(refs[:n_srcs], refs[n_srcs:nb])):
            cp = _remote(src, dst, sends, recvs, k, to)
            cp.wait_send()
            cp.wait_recv()

    res = pl.pallas_call(
        body, name=name, out_shape=[pltpu.HBM(b.shape, b.dtype) for b in bufs],
        in_specs=[HBM] * nb + [SEM, SEM, ANY], out_specs=[HBM] * nb,
        input_output_aliases={i: i for i in range(nb)},
        compiler_params=pltpu.CompilerParams(has_side_effects=DATAFLOW),
    )(*bufs, send_sems, recv_sems, after)
    return list(res[:n_srcs]), list(res[n_srcs:])


def _gather_plan(shapes, axes, n_whole=0):
    def plan(srcs, lands):
        x, y, c, chips = _place()
        me = 2 * x + y
        out = []
        for a, (shape, axis) in enumerate(zip(shapes, axes)):
            own = lands[a].at[(me,) + _half(shape, c, axis)]
            for ch in chips:
                out.append((own, own, (ch[0], ch[1], c)))
        for a in range(len(shapes), len(shapes) + n_whole):
            for ch in chips:
                out.append((lands[a].at[me], lands[a].at[me], (ch[0], ch[1], c)))
        return out
    return plan


def _share_plan(shapes, axes):
    def plan(srcs, lands):
        x, y, c, chips = _place()
        out = []
        for a, (shape, axis) in enumerate(zip(shapes, axes)):
            mine = _half(shape, c, axis)
            for ch in chips:
                landed = lands[a].at[(2 * ch[0] + ch[1],) + mine]
                out.append((landed, landed, (x, y, 1 - c)))
        return out
    return plan


def _sibling_plan(shapes, axes):
    def plan(srcs, lands):
        x, y, c, _ = _place()
        return [(srcs[a].at[(slice(None),) + _half(shape, 1 - c, axis)], lands[a], (x, y, 1 - c))
                for a, (shape, axis) in enumerate(zip(shapes, axes))]
    return plan


def _whole_to_sibling_plan(n):
    def plan(srcs, lands):
        x, y, c, _ = _place()
        return [(srcs[a], lands[a], (x, y, 1 - c)) for a in range(n)]
    return plan


def _reduce_plan(n_big, n_small):
    def plan(srcs, lands):
        x, y, c, chips = _place()
        out = []
        for a in range(n_big):
            for k, ch in enumerate(chips):
                out.append((srcs[a].at[2 * ch[0] + ch[1]], lands[a].at[k], (ch[0], ch[1], c)))
        for a in range(n_big, n_big + n_small):
            for ch in chips:
                out.append((srcs[a], lands[a].at[2 * x + y], (ch[0], ch[1], c)))
        return out
    return plan


def _row_tile(rows, cols, mult):
    best = mult
    for t in range(mult, rows + 1, mult):
        if rows % t == 0 and t * cols * 4 <= (2 << 20):
            best = t
    return best if rows % best == 0 else rows


COL_TILE = 256


def _half_tiling(hshape, axis, mult):
    hr, hc = hshape
    if axis == 0:
        tr = _row_tile(hr, hc, mult)
        return tr, hc, hr // tr
    return hr, COL_TILE, hc // COL_TILE


def _tile_idx(axis, t):
    return (t, 0) if axis == 0 else (0, t)


def _chip_partial(place, g, t, axis, name):
    hshape = t.shape[1:]
    br, bc, nt = _half_tiling(hshape, axis, 16)

    def body(pl_ref, g_ref, t_ref, pf_ref, pb_ref):
        v = g_ref[...].astype(F32) + t_ref[...].astype(F32)
        pb_ref[...] = v.astype(BF16)

        @pl.when(pl.program_id(1) == pl_ref[0])
        def _():
            pf_ref[...] = v

    blk = (None, br, bc)
    return pl.pallas_call(
        body, name=name,
        grid_spec=pltpu.PrefetchScalarGridSpec(
            num_scalar_prefetch=1, grid=(nt, 4),
            in_specs=[pl.BlockSpec(blk, lambda i, j, p: (j,) + _tile_idx(axis, p[1] * nt + i)),
                      pl.BlockSpec(blk, lambda i, j, p: (j,) + _tile_idx(axis, i))],
            out_specs=[pl.BlockSpec((br, bc), lambda i, j, p: _tile_idx(axis, i)),
                       pl.BlockSpec(blk, lambda i, j, p: (j,) + _tile_idx(axis, i))]),
        out_shape=[jax.ShapeDtypeStruct(hshape, F32), jax.ShapeDtypeStruct((4,) + hshape, BF16)],
        compiler_params=_cp("arbitrary", "arbitrary"),
    )(place, g, t)


def _finish_half(pf, rb, axis, name):
    hshape = pf.shape
    br, bc, nt = _half_tiling(hshape, axis, 16)

    def body(pf_ref, rb_ref, o_ref):
        o_ref[...] = ((pf_ref[...] + rb_ref[0].astype(F32)) + rb_ref[1].astype(F32)) + rb_ref[2].astype(F32)

    return pl.pallas_call(
        body, name=name, grid=(nt,),
        in_specs=[pl.BlockSpec((br, bc), lambda i: _tile_idx(axis, i)),
                  pl.BlockSpec((3, br, bc), lambda i: (0,) + _tile_idx(axis, i))],
        out_specs=pl.BlockSpec((br, bc), lambda i: _tile_idx(axis, i)),
        out_shape=jax.ShapeDtypeStruct(hshape, F32),
        compiler_params=_cp("arbitrary"),
    )(pf, rb)


def _adam_math(w, g, m, v):
    m = ADAM_B1 * m + (1.0 - ADAM_B1) * g
    v = ADAM_B2 * v + (1.0 - ADAM_B2) * (g * g)
    m_hat = m / (1.0 - ADAM_B1 ** ADAM_STEP)
    v_hat = v / (1.0 - ADAM_B2 ** ADAM_STEP)
    return -ADAM_LR * (m_hat / (jnp.sqrt(v_hat) + ADAM_EPS) + ADAM_WD * w), m, v


def _adam_halves(place, w, mine, theirs, m, v, axis, name):
    br, bc, nt = _half_tiling(mine.shape, axis, 8)

    def body(pl_ref, w_ref, a_ref, b_ref, m_ref, v_ref, g_ref, d_ref, mo_ref, vo_ref):
        is_mine = pl.program_id(0) // nt == pl_ref[1]
        g = jnp.where(is_mine, a_ref[...], b_ref[...])
        d, mn, vn = _adam_math(w_ref[...], g, m_ref[...], v_ref[...])
        g_ref[...] = g
        d_ref[...] = d
        mo_ref[...] = mn
        vo_ref[...] = vn

    full = pl.BlockSpec((br, bc), lambda i, p: _tile_idx(axis, i))
    mine_spec = pl.BlockSpec((br, bc), lambda i, p: _tile_idx(axis, jnp.where(i // nt == p[1], i % nt, nt - 1)))
    theirs_spec = pl.BlockSpec((br, bc), lambda i, p: _tile_idx(axis, jnp.where(i // nt == p[1], 0, i % nt)))
    return pl.pallas_call(
        body, name=name,
        grid_spec=pltpu.PrefetchScalarGridSpec(
            num_scalar_prefetch=1, grid=(2 * nt,), in_specs=[full, mine_spec, theirs_spec, full, full],
            out_specs=[full] * 4),
        out_shape=[jax.ShapeDtypeStruct(w.shape, F32)] * 4, compiler_params=_cp("arbitrary"),
    )(place, w, mine, theirs, m, v)


def _add_many(xs, ys, name):
    n = len(xs)

    def body(*refs):
        for i in range(n):
            refs[2 * n + i][...] = refs[i][...] + refs[n + i][...]

    return pl.pallas_call(body, name=name, out_shape=[jax.ShapeDtypeStruct(a.shape, F32) for a in xs])(*xs, *ys)


def _adam_small(place, owns, landed, ws, ms, vs, widths):
    n, nw = len(owns), len(ws)

    def body(pl_ref, *refs):
        own_r, land_r = refs[:n], refs[n:2 * n]
        w_r, m_r, v_r = (refs[2 * n + k * nw:2 * n + (k + 1) * nw] for k in range(3))
        outs = refs[2 * n + 3 * nw:]
        g_o, d_o, m_o, v_o = outs[:n], outs[n:n + nw], outs[n + nw:n + 2 * nw], outs[n + 2 * nw:]
        for me in range(4):
            @pl.when(pl_ref[0] == me)
            def _(me=me):
                for i in range(n):
                    p = [own_r[i][...] if k == me else land_r[i][k] for k in range(4)]
                    g = ((p[0] + p[1]) + p[2]) + p[3]
                    if i < nw and widths[i]:
                        g = g[:, me * widths[i]:(me + 1) * widths[i]]
                    g_o[i][...] = g
                    if i < nw:
                        d, mn, vn = _adam_math(w_r[i][...], g, m_r[i][...], v_r[i][...])
                        d_o[i][...] = d
                        m_o[i][...] = mn
                        v_o[i][...] = vn

    g_shapes = [jax.ShapeDtypeStruct(ws[i].shape if i < nw else owns[i].shape, F32) for i in range(n)]
    w_shapes = [jax.ShapeDtypeStruct(w.shape, F32) for w in ws]
    whole = lambda a: pl.BlockSpec(a.shape, lambda i, p, nd=len(a.shape): (0,) * nd)
    ins = list(owns) + list(landed) + list(ws) + list(ms) + list(vs)
    out_shape = g_shapes + w_shapes * 3
    out = pl.pallas_call(
        body, name="adam_small",
        grid_spec=pltpu.PrefetchScalarGridSpec(num_scalar_prefetch=1, grid=(1,), in_specs=[whole(a) for a in ins],
                                               out_specs=[whole(a) for a in out_shape]),
        out_shape=out_shape, compiler_params=_cp("arbitrary"),
    )(place, *ins)
    return out[:n], out[n:n + nw], out[n + nw:n + 2 * nw], out[n + 2 * nw:]


def kernel(x, g_mix, w_in, b_gate, w_gk_up, b_gk, w_pool_grp, pool_scale, g_gla_head, w_pool_proj, w_gla_proj, w_out, g_ffn, w_up, w_conv, b_conv, w_down, g_final, loss_target, m_g_mix, m_w_in, m_b_gate, m_w_gk_up, m_b_gk, m_w_pool_grp, m_pool_scale, m_g_gla_head, m_w_pool_proj, m_w_gla_proj, m_w_out, m_g_ffn, m_w_up, m_w_conv, m_b_conv, m_w_down, m_g_final, v_g_mix, v_w_in, v_b_gate, v_w_gk_up, v_b_gk, v_w_pool_grp, v_pool_scale, v_g_gla_head, v_w_pool_proj, v_w_gla_proj, v_w_out, v_g_ffn, v_w_up, v_w_conv, v_b_conv, v_w_down, v_g_final):
    s = x.shape[1]
    ts = min(s, 512)
    tm = min(s, 256)
    cx, cy, cc = lax.axis_index("x"), lax.axis_index("y"), lax.axis_index("c")
    chip = 2 * cx + cy
    place = jnp.stack([chip, cc]).astype(jnp.int32)

    big_names = ("w_in", "w_pool_proj", "w_gla_proj", "w_out", "w_up", "w_down")
    axes = (1, 0, 0, 0, 0, 0)
    shards = dict(w_in=jnp.transpose(w_in[0]), w_pool_proj=w_pool_proj[0], w_gla_proj=w_gla_proj[0], w_out=w_out[0],
                  w_up=w_up[0], w_down=w_down[0])
    def gather_start(tag, halves, group_axes, whole, after):
        plan = _gather_plan([o_.shape for o_ in halves], group_axes, len(whole))
        lands = [lax.dynamic_update_slice(lax.empty((4,) + o_.shape, o_.dtype), o_[None], (chip, 0, 0))
                 for o_ in list(halves) + list(whole)]
        handle, token = _split_start("gather_" + tag + "_start", [], lands, plan, 3 * len(lands), after)
        return (handle, plan, len(halves), group_axes), token

    def gather_finish(tag, started, after):
        handle, plan, n_halves, group_axes = started
        lands = _split_wait("gather_" + tag + "_wait", handle, 0, plan, after)[1]
        lands[:n_halves] = _gather_share(lands[:n_halves], group_axes, "gather_" + tag + "_share")
        return lands

    in_w, tok = gather_start("in", [jnp.transpose(w_in[0].astype(BF16))], axes[:1], [], g_mix)
    zero = tok[0, 0]
    own = [(shards[n] + zero).astype(BF16) for n in big_names[1:]]
    mix_w, tok = gather_start("mix", own[0:3], axes[1:4], [w_gk_up[0] + zero, w_conv[0] + zero], tok)
    up_w, tok = gather_start("up", own[3:4], axes[4:5], [], tok)
    down_w, tok = gather_start("down", own[4:5], axes[5:6], [], tok)

    def forward_start(tag, started, after):
        handle, plan, n_halves, group_axes = started
        lands = _split_wait("gather_" + tag + "_wait", handle, 0, plan, after)[1]
        plan = _share_plan([l_.shape[1:] for l_ in lands[:n_halves]], group_axes)
        share, token = _split_start("gather_" + tag + "_share_start", [], lands[:n_halves], plan, 3 * n_halves, after)
        return (share, plan, lands[n_halves:]), token

    def forward_done(tag, forwarded, after):
        share, plan, _ = forwarded
        return _split_wait("gather_" + tag + "_share_wait", share, 0, plan, after)[1]
    xs, tgt = x[0], loss_target[0]
    wgrp = w_pool_grp[0]
    h = _rmsnorm(xs, g_mix, tok, "norm_mix", ts)
    m_in_t, v_in_t = jnp.transpose(m_w_in[0]), jnp.transpose(v_w_in[0])
    h, m_in_t, v_in_t = lax.optimization_barrier((h, m_in_t, v_in_t))
    w_in_t = gather_finish("in", in_w, h)[0].reshape(N_IN, D)
    nsh = N_IN // 4

    zr = _in_proj(h, w_in_t, PROJ_TILE)
    p, pp = _pool_fwd(zr, wgrp, pool_scale)
    mix_f, tok = forward_start("mix", mix_w, pp)
    wgk4, wconv4 = mix_f[2]
    wgk_full = jnp.transpose(wgk4, (1, 0, 2)).reshape(GATE_RANK, 512) + tok[0, 0]
    wconv_full = jnp.transpose(wconv4, (1, 0, 2)).reshape(3, N_UP)
    wgk_pad = jnp.concatenate([wgk_full, jnp.zeros((128 - GATE_RANK, 512), F32)], axis=0)
    o, og, sp = _gla_fwd(zr, wgk_pad, b_gk, g_gla_head, ts)
    wpp, wgla, wout = forward_done("mix", mix_f, og)
    wgla, wout = wgla.reshape(D, D), wout.reshape(D, D)
    up_f, tok = forward_start("up", up_w, og)
    x1, mixed, yp, yg, h2 = _merge_fwd(xs, zr, pp, og, b_gate, wpp, wgla, wout, g_ffn, tok, ts)
    wup, = forward_done("up", up_f, x1)
    down_f, tok = forward_start("down", down_w, x1)
    u = _matmul_resident(h2, wup, tok, "ffn_up")
    wdown = forward_done("down", down_f, u)[0].reshape(D_FF, D)
    a, conv_out, dx2, dx2b, loss_part, dgfin = _ffn_down_loss(u, x1, tgt, wconv_full, b_conv, wdown,
                                                              g_final.reshape(1, D), tm)

    du, dbconv, dwconv = _ffn_bwd(dx2b, u, conv_out, wconv_full, wdown, tm)
    dw_down = _matmul_tn(a, dx2b, "dw_down", D, tm=D_FF // 2)
    dw_up = _matmul_tn(h2, du, "dw_up", UP_SHARD, shard_major=True)

    def exchange_start(tag, grads, group_axes, after):
        plan = _sibling_plan([g.shape[1:] for g in grads], group_axes)
        lands = [((4,) + _half_shape(g.shape[1:], ax), g.dtype) for g, ax in zip(grads, group_axes)]
        handle, token = _split_start("sibling_" + tag + "_start", grads, lands, plan, len(grads), after)
        return (handle, plan, len(grads)), token

    def partials(tag, names, group_axes, exchange, after):
        handle, plan, n = exchange
        mine, theirs = _split_wait("sibling_" + tag + "_wait", handle, n, plan, after)
        return zip(*[_chip_partial(place, g, t, ax, "chip_partial_" + nm)
                     for nm, ax, g, t in zip(names, group_axes, mine, theirs)])

    ffn_names, ffn_axes = ("w_up", "w_down"), (0, 0)
    ffn_x, token = exchange_start("ffn", [dw_up, dw_down.reshape(4, 704, D)], ffn_axes, du)
    dx1, dx1b, dgffn = _matmul_nt_normbwd(du, wup, x1, g_ffn, dx2, token, "ffn_up_bwd", ts)
    ffn_pf, ffn_pb = partials("ffn", ffn_names, ffn_axes, ffn_x, dx1b)
    ffn_plan = _reduce_plan(2, 0)
    ffn_handle, token = _split_start("reduce_ffn_start", ffn_pb, [((3,) + p.shape[1:], BF16) for p in ffn_pb],
                                     ffn_plan, 6, ffn_pf[0])

    dzr, dyp, dyg, dpp, do, dzog, dbgate, dghead = _merge_bwd(dx1b, zr, yp, yg, o, b_gate, g_gla_head, wpp, wgla, wout,
                                                             token, ts)
    dzr = lax.dynamic_update_slice(dzr, dzog, (0, OFF_OG))
    dw_out = _matmul_tn(mixed, dx1b, "dw_out", D, tm=512)
    dw_gla = _matmul_tn(og, dyg, "dw_gla", D, tm=512)
    dw_pp = _matmul_tn(pp, dyp, "dw_pp", 256, shard_major=True)

    out_names, out_axes = ("w_pool_proj", "w_gla_proj", "w_out"), (0, 0, 0)
    out_x, token = exchange_start("out", [dw_pp, dw_gla.reshape(4, 256, D), dw_out.reshape(4, 256, D)], out_axes, dpp)
    dzr, dwgrp, dscale = _pool_bwd(p, dpp, wgrp, pool_scale, token, dzr)
    out_pf, out_pb = partials("out", out_names, out_axes, out_x, dwgrp)
    out_plan = _reduce_plan(3, 0)
    out_handle, token = _split_start("reduce_out_start", out_pb, [((3,) + p_.shape[1:], BF16) for p_ in out_pb],
                                     out_plan, 9, out_pf[0])
    dq, dk, dzr, dgpre = _gla_bwd(zr, do, sp, wgk_pad, b_gk, token, dzr, ts)
    dzr, dwgk, dbgk = _gk_bwd(dgpre, zr, wgk_pad, dgpre, dzr, ts)
    dzr = lax.dynamic_update_slice(lax.dynamic_update_slice(dzr, dq, (0, OFF_Q)), dk, (0, OFF_K))
    dw_rt = _matmul_tn(dzr, h, "dw_in", D, tm=PROJ_TILE)

    def grad_rows(lo, hi):
        out = []
        for seg_lo, seg_hi, at in ((0, 1536, OFF_POOL), (1536, 3584, OFF_V), (3584, 3600, OFF_GK), (3600, N_IN, OFF_GATE)):
            a_, b_ = max(lo, seg_lo), min(hi, seg_hi)
            if a_ < b_:
                out.append(dw_rt[at + a_ - seg_lo:at + b_ - seg_lo])
        return jnp.concatenate(out, axis=0)

    dw_in_t = jnp.stack([grad_rows(j * nsh, (j + 1) * nsh) for j in range(4)])

    ms = dict(w_in=m_in_t, w_pool_proj=m_w_pool_proj[0], w_gla_proj=m_w_gla_proj[0], w_out=m_w_out[0],
              w_up=m_w_up[0], w_down=m_w_down[0])
    vs = dict(w_in=v_in_t, w_pool_proj=v_w_pool_proj[0], w_gla_proj=v_w_gla_proj[0], w_out=v_w_out[0],
              w_up=v_w_up[0], w_down=v_w_down[0])
    grad, delta, new_m, new_v = {}, {}, {}, {}

    def finish(names, group_axes, part_f, landed):
        return [_finish_half(pf, rb, ax, "finish_" + n) for n, ax, pf, rb in zip(names, group_axes, part_f, landed)]

    def update(names, group_axes, halves, sib_halves):
        for n, ax, mine, theirs in zip(names, group_axes, halves, sib_halves):
            res = _adam_halves(place, shards[n], mine, theirs, ms[n], vs[n], ax, "adam_" + n)
            if n == "w_in":
                res = [jnp.transpose(r_) for r_ in res]
            grad[n], delta[n], new_m[n], new_v[n] = [r_[None] for r_ in res]

    rest_names, rest_axes = ffn_names + out_names, ffn_axes + out_axes
    in_x, token = exchange_start("in", [dw_in_t], (1,), dw_rt)
    _, ffn_landed = _split_wait("reduce_ffn_wait", ffn_handle, 2, ffn_plan, token)
    _, out_landed = _split_wait("reduce_out_wait", out_handle, 3, out_plan, ffn_landed[0])
    rest_halves = lax.optimization_barrier(finish(rest_names, rest_axes, ffn_pf + out_pf, ffn_landed + out_landed))
    (in_pf,), (in_pb,) = partials("in", ("w_in",), (1,), in_x, rest_halves[-1])
    in_plan = _reduce_plan(1, 0)
    in_handle, token = _split_start("reduce_in_start", [in_pb], [((3,) + in_pb.shape[1:], BF16)], in_plan, 3, in_pf)
    rest_plan = _whole_to_sibling_plan(len(rest_halves))
    rest_share, token = _split_start("sibling_share_rest_start", rest_halves, [(h_.shape, F32) for h_ in rest_halves],
                                     rest_plan, len(rest_halves), token)
    grad_x, _, dgmix = _matmul_nt_normbwd(dzr, w_in_t, xs, g_mix, dx1, token, "in_proj_bwd", ts, transposed=True)
    small_names = ("g_mix", "b_gate", "w_gk_up", "b_gk", "w_pool_grp", "pool_scale", "g_gla_head", "g_ffn", "w_conv",
                   "b_conv", "g_final")
    small_mine = [dgmix, dbgate, dwgk[:GATE_RANK], dbgk, dwgrp.reshape(4 * 128, 128), dscale, dghead, dgffn, dwconv, dbconv,
                  dgfin, loss_part]
    small_sib = _sibling_exchange([], (), small_mine, "sibling_exchange_small")
    small_chip = _add_many(small_mine, small_sib, "chip_partial_small")
    small_plan = _reduce_plan(0, len(small_chip))
    small_handle, token = _split_start("reduce_small_start", small_chip, [((4,) + a_.shape, F32) for a_ in small_chip],
                                       small_plan, 3 * len(small_chip), small_mine[0])

    rest_halves, rest_sib = _split_wait("sibling_share_rest_wait", rest_share, len(rest_halves), rest_plan, token)
    n_ffn = len(ffn_names)
    update(out_names, out_axes, rest_halves[n_ffn:], rest_sib[n_ffn:])
    updated = lax.optimization_barrier([delta[n] for n in out_names])
    _, in_landed = _split_wait("reduce_in_wait", in_handle, 1, in_plan, updated[0])
    in_halves = finish(("w_in",), (1,), (in_pf,), in_landed)
    update(("w_in",), (1,), in_halves, _sibling_share(in_halves, "sibling_share_in"))
    ffn_halves, _ = lax.optimization_barrier((rest_halves[:n_ffn], delta["w_in"]))
    update(ffn_names, ffn_axes, ffn_halves, rest_sib[:n_ffn])
    small_sent, small_landed = _split_wait("reduce_small_wait", small_handle, len(small_chip), small_plan, delta["w_in"])
    given = dict(g_mix=(g_mix, m_g_mix, v_g_mix), b_gate=(b_gate, m_b_gate, v_b_gate), w_gk_up=(w_gk_up, m_w_gk_up, v_w_gk_up),
                 b_gk=(b_gk, m_b_gk, v_b_gk), w_pool_grp=(w_pool_grp, m_w_pool_grp, v_w_pool_grp),
                 pool_scale=(pool_scale, m_pool_scale, v_pool_scale), g_gla_head=(g_gla_head, m_g_gla_head, v_g_gla_head),
                 g_ffn=(g_ffn, m_g_ffn, v_g_ffn), w_conv=(w_conv, m_w_conv, v_w_conv), b_conv=(b_conv, m_b_conv, v_b_conv),
                 g_final=(g_final, m_g_final, v_g_final))
    flat2 = lambda a: a.reshape(-1, a.shape[-1])
    widths = [dict(w_gk_up=HK, w_conv=UP_SHARD).get(n) for n in small_names]
    totals, ds, mo, vo = _adam_small(place, small_sent, small_landed, *[[flat2(given[n][k]) for n in small_names] for k in range(3)],
                                     widths)
    loss = totals[-1][0, 0]
    for i, n in enumerate(small_names):
        shp = given[n][0].shape
        grad[n], delta[n], new_m[n], new_v[n] = [r_.reshape(shp) for r_ in (totals[i], ds[i], mo[i], vo[i])]

    order = ("g_mix", "w_in", "b_gate", "w_gk_up", "b_gk", "w_pool_grp", "pool_scale", "g_gla_head", "w_pool_proj",
             "w_gla_proj", "w_out", "g_ffn", "w_up", "w_conv", "b_conv", "w_down", "g_final")
    return (loss, grad_x[None], *[grad[n] for n in order], *[delta[n] for n in order], *[new_m[n] for n in order],
            *[new_v[n] for n in order])
```

```python
import jax
import jax.numpy as jnp
from jax import lax
from jax.experimental import pallas as pl
from jax.experimental.pallas import tpu as pltpu

F32 = jnp.float32
BF16 = jnp.bfloat16
MESH = pl.DeviceIdType.MESH

D = 1024
EPS = 1e-6
CHUNK = 64
POOL_W = 512
POOL_WINDOWS = (2, 4, 8, 16)
HEADS = 4
HK = 128
HV = 256
GATE_RANK = 16
D_FF = 2816
N_UP = 2 * D_FF
N_IN = 5648
QSCALE = HK ** -0.5
N_INR = 5760
OFF_GATE, OFF_V, OFF_OG, OFF_POOL, OFF_Q, OFF_K, OFF_GK = 0, 2048, 3072, 4096, 4608, 5120, 5632

ADAM_LR, ADAM_B1, ADAM_B2, ADAM_EPS, ADAM_WD, ADAM_STEP = 0.001, 0.9, 0.999, 1e-08, 0.01, 10

VMEM_LIMIT = 56 * 1024 * 1024
PROJ_TILE = N_INR // 5
UP_SHARD = N_UP // 4


def _cp(*sem):
    return pltpu.CompilerParams(dimension_semantics=sem if sem else None, vmem_limit_bytes=VMEM_LIMIT)


def _dot(a, b):
    return jnp.dot(a, b, preferred_element_type=F32)


def _dot_nt(a, b):
    return lax.dot_general(a, b, (((1,), (1,)), ((), ())), preferred_element_type=F32)


def _dot_tn(a, b):
    return lax.dot_general(a, b, (((0,), (0,)), ((), ())), preferred_element_type=F32)


def _sigmoid(v):
    return 1.0 / (1.0 + jnp.exp(-v))


def _rows(shape):
    return lax.broadcasted_iota(jnp.int32, shape, 0)


def _pick_row(v, r):
    return jnp.sum(jnp.where(_rows(v.shape) == r, v, 0.0), axis=0, keepdims=True)


def _rmsnorm(x, g, after, name, ts):
    s = x.shape[0]

    def body(x_ref, g_ref, after_ref, h_ref):
        xv = x_ref[...]
        r = lax.rsqrt(jnp.mean(xv * xv, axis=-1, keepdims=True) + EPS)
        h_ref[...] = (xv * r * g_ref[...]).astype(BF16)

    return pl.pallas_call(
        body, name=name, grid=(s // ts,),
        in_specs=[pl.BlockSpec((ts, D), lambda i: (i, 0)), pl.BlockSpec((1, D), lambda i: (0, 0)), ANY],
        out_specs=pl.BlockSpec((ts, D), lambda i: (i, 0)), out_shape=jax.ShapeDtypeStruct((s, D), BF16),
        compiler_params=_cp("arbitrary"),
    )(x, g, after)


MM_ROWS = 512


def _matmul_resident(h, w, after, name):
    s = h.shape[0]
    nj, tn = w.shape[0], w.shape[2]
    rc = min(s, MM_ROWS)

    def body(h_ref, w_ref, after_ref, z_ref):
        for r0 in range(0, s, rc):
            z_ref[r0:r0 + rc, :] = _dot(h_ref[r0:r0 + rc, :], w_ref[...]).astype(BF16)

    return pl.pallas_call(
        body, name=name, grid=(nj,),
        in_specs=[pl.BlockSpec((s, D), lambda j: (0, 0)), pl.BlockSpec((None, D, tn), lambda j: (j, 0, 0)), ANY],
        out_specs=pl.BlockSpec((s, tn), lambda j: (0, j)), out_shape=jax.ShapeDtypeStruct((s, nj * tn), BF16),
        compiler_params=_cp("arbitrary"),
    )(h, w, after)


PROJ_PIECES = ((3600, 2048, OFF_GATE), (1536, 2048, OFF_V), (0, 1536, OFF_POOL), (3584, GATE_RANK, OFF_GK))


def _split_by_shard(pieces, rows_per_shard):
    out = []
    for src, n, dst in pieces:
        while n > 0:
            j, r = divmod(src, rows_per_shard)
            m = min(n, rows_per_shard - r)
            out.append((j, r, m, dst))
            src, n, dst = src + m, n - m, dst + m
    return tuple(out)


PROJ_SEGMENTS = _split_by_shard(PROJ_PIECES, N_IN // 4)


def _in_proj(h, w4, tn):
    s = h.shape[0]
    rc = min(s, MM_ROWS)
    nj = N_INR // tn
    first_use = [dst // tn for _, _, _, dst in PROJ_SEGMENTS]

    def body(h_ref, w_hbm, z_ref, wo_hbm, w_ref, stage, sems, out_sem):
        j = pl.program_id(0)
        cps = [pltpu.make_async_copy(w_hbm.at[k], stage.at[k], sems.at[k]) for k in range(4)]
        out_cp = pltpu.make_async_copy(w_ref, wo_hbm, out_sem.at[0])

        @pl.when(j == 0)
        def _():
            for cp in cps:
                cp.start()
            w_ref[OFF_GK + GATE_RANK:, :] = jnp.zeros((N_INR - OFF_GK - GATE_RANK, D), BF16)

        landed = set()
        for step in range(nj):
            due = [seg for seg, at in zip(PROJ_SEGMENTS, first_use) if at == step]
            if due:
                fresh = sorted({seg[0] for seg in due} - landed)
                landed.update(fresh)

                @pl.when(j == step)
                def _(due=due, fresh=fresh, last=step == max(first_use)):
                    for k in fresh:
                        cps[k].wait()
                    for k, r, n, dst in due:
                        w_ref[dst:dst + n, :] = stage[k, r:r + n, :]
                    if last:
                        out_cp.start()

        wt = w_ref[pl.ds(pl.multiple_of(j * tn, 128), tn), :]
        for r0 in range(0, s, rc):
            z_ref[r0:r0 + rc, :] = _dot_nt(h_ref[r0:r0 + rc, :], wt).astype(BF16)

        @pl.when(j == nj - 1)
        def _():
            out_cp.wait()

    return pl.pallas_call(
        body, name="in_proj", grid=(nj,),
        in_specs=[pl.BlockSpec((s, D), lambda j: (0, 0)), ANY],
        out_specs=[pl.BlockSpec((s, tn), lambda j: (0, j)), ANY],
        out_shape=[jax.ShapeDtypeStruct((s, N_INR), BF16), jax.ShapeDtypeStruct((N_INR, D), BF16)],
        scratch_shapes=[pltpu.VMEM((N_INR, D), BF16), pltpu.VMEM(w4.shape, BF16), pltpu.SemaphoreType.DMA((4,)),
                        pltpu.SemaphoreType.DMA((1,))],
        compiler_params=_cp("arbitrary"),
    )(h, w4)


def _matmul_nt_normbwd(dz, w, x, g, resid, after, name, ts, transposed=False):
    s = x.shape[0]
    w_vmem = w.shape if transposed else (D, w.shape[0] * w.shape[2])
    n_sems = 1 if transposed else w.shape[0]

    def body(dz_ref, w_hbm, x_ref, g_ref, r_ref, after_ref, o_ref, ob_ref, dg_ref, w_ref, sems):
        @pl.when(pl.program_id(0) == 0)
        def _():
            if transposed:
                cps = [pltpu.make_async_copy(w_hbm, w_ref, sems.at[0])]
            else:
                kc = w.shape[2]
                cps = [pltpu.make_async_copy(w_hbm.at[j], w_ref.at[:, pl.ds(j * kc, kc)], sems.at[j])
                       for j in range(w.shape[0])]
            for cp in cps:
                cp.start()
            for cp in cps:
                cp.wait()
            dg_ref[...] = jnp.zeros_like(dg_ref)

        dh = _dot(dz_ref[...], w_ref[...]) if transposed else _dot_nt(dz_ref[...], w_ref[...])
        xv = x_ref[...]
        r = lax.rsqrt(jnp.mean(xv * xv, axis=-1, keepdims=True) + EPS)
        xh = xv * r
        dg_ref[...] += jnp.sum(dh * xh, axis=0, keepdims=True)
        dxh = dh * g_ref[...]
        out = r_ref[...] + r * (dxh - xh * jnp.mean(dxh * xh, axis=-1, keepdims=True))
        o_ref[...] = out
        ob_ref[...] = out.astype(BF16)

    row = lambda i: (i, 0)
    kdim = dz.shape[1]
    return pl.pallas_call(
        body, name=name, grid=(s // ts,),
        in_specs=[pl.BlockSpec((ts, kdim), row), ANY, pl.BlockSpec((ts, D), row),
                  pl.BlockSpec((1, D), lambda i: (0, 0)), pl.BlockSpec((ts, D), row), ANY],
        out_specs=[pl.BlockSpec((ts, D), row), pl.BlockSpec((ts, D), row), pl.BlockSpec((1, D), lambda i: (0, 0))],
        out_shape=[jax.ShapeDtypeStruct((s, D), F32), jax.ShapeDtypeStruct((s, D), BF16),
                   jax.ShapeDtypeStruct((1, D), F32)],
        scratch_shapes=[pltpu.VMEM(w_vmem, BF16), pltpu.SemaphoreType.DMA((n_sems,))],
        compiler_params=_cp("arbitrary"),
    )(dz, w, x, g, resid, after)


def _matmul_tn(a, b, name, tn, shard_major=False, tm=None):
    s, m = a.shape
    n = b.shape[1]
    tm = m if tm is None else tm
    ni, nj = m // tm, n // tn

    def body(a_ref, b_ref, o_ref):
        o_ref[...] = _dot_tn(a_ref[...], b_ref[...]).astype(BF16)

    if shard_major:
        out_spec = pl.BlockSpec((None, tm, tn), lambda i, j: (j, i, 0))
        out_shape = jax.ShapeDtypeStruct((nj, m, tn), BF16)
    else:
        out_spec = pl.BlockSpec((tm, tn), lambda i, j: (i, j))
        out_shape = jax.ShapeDtypeStruct((m, n), BF16)
    return pl.pallas_call(
        body, name=name, grid=(ni, nj),
        in_specs=[pl.BlockSpec((s, tm), lambda i, j: (0, i)), pl.BlockSpec((s, tn), lambda i, j: (0, j))],
        out_specs=out_spec, out_shape=out_shape,
        compiler_params=_cp("arbitrary", "arbitrary"),
    )(a, b)


def _pool_fwd(zr, wgrp, scale):
    s = zr.shape[0]

    def body(u_ref, w_ref, sc_ref, p_ref, pp_ref):
        row = _rows((s, 128))
        for gi, win in enumerate(POOL_WINDOWS):
            cs = slice(gi * 128, (gi + 1) * 128)
            u = u_ref[:, cs].astype(F32)
            acc, k = u, 1
            while k < win:
                acc = acc + jnp.where(row >= k, pltpu.roll(acc, k, 0), 0.0)
                k *= 2
            cnt = jnp.minimum(row + 1, win).astype(F32)
            p = (acc / cnt - u).astype(BF16)
            p_ref[:, cs] = p
            pp_ref[:, cs] = (_dot(p, w_ref[gi].astype(BF16)) * sc_ref[:, cs]).astype(BF16)

    return pl.pallas_call(
        body, name="pool_fwd", grid=(1,),
        in_specs=[pl.BlockSpec((s, POOL_W), lambda i: (0, OFF_POOL // POOL_W)),
                  pl.BlockSpec((4, 128, 128), lambda i: (0, 0, 0)), pl.BlockSpec((1, POOL_W), lambda i: (0, 0))],
        out_specs=[pl.BlockSpec((s, POOL_W), lambda i: (0, 0))] * 2,
        out_shape=[jax.ShapeDtypeStruct((s, POOL_W), BF16)] * 2,
        compiler_params=_cp("arbitrary"),
    )(zr, wgrp, scale)


def _pool_bwd(p, dpp, wgrp, scale, after, dz):
    s = p.shape[0]

    def body(p_ref, dpp_ref, w_ref, sc_ref, after_ref, dz_in, dz_ref, dw_ref, dsc_ref):
        row = _rows((s, 128))
        for gi, win in enumerate(POOL_WINDOWS):
            cs = slice(gi * 128, (gi + 1) * 128)
            pv = p_ref[:, cs]
            wb = w_ref[gi].astype(BF16)
            dpp_v = dpp_ref[:, cs].astype(F32)
            dsc_ref[:, cs] = jnp.sum(dpp_v * _dot(pv, wb), axis=0, keepdims=True)
            dpm = (dpp_v * sc_ref[:, cs]).astype(BF16)
            dw_ref[gi] = _dot_tn(pv, dpm)
            dp = _dot_nt(dpm, wb)
            cnt = jnp.minimum(row + 1, win).astype(F32)
            acc, k = dp / cnt, 1
            while k < win:
                acc = acc + jnp.where(row < s - k, pltpu.roll(acc, s - k, 0), 0.0)
                k *= 2
            dz_ref[:, cs] = (acc - dp).astype(BF16)

    full = lambda i: (0, 0)
    return pl.pallas_call(
        body, name="pool_bwd", grid=(1,),
        in_specs=[pl.BlockSpec((s, POOL_W), full), pl.BlockSpec((s, POOL_W), full),
                  pl.BlockSpec((4, 128, 128), lambda i: (0, 0, 0)), pl.BlockSpec((1, POOL_W), full), ANY, ANY],
        out_specs=[pl.BlockSpec((s, POOL_W), lambda i: (0, OFF_POOL // POOL_W)),
                   pl.BlockSpec((4, 128, 128), lambda i: (0, 0, 0)), pl.BlockSpec((1, POOL_W), full)],
        out_shape=[jax.ShapeDtypeStruct(dz.shape, BF16), jax.ShapeDtypeStruct((4, 128, 128), F32),
                   jax.ShapeDtypeStruct((1, POOL_W), F32)],
        input_output_aliases={5: 0},
        compiler_params=_cp("arbitrary"),
    )(p, dpp, wgrp, scale, after, dz)


def _gla_decay(zgk_ref, wgk_ref, bgk_ref, rb):
    g = _dot(zgk_ref[...], wgk_ref[...].astype(BF16)) + bgk_ref[...]
    la = (jnp.minimum(g, 0.0) - jnp.log(1.0 + jnp.exp(-jnp.abs(g)))) * (1.0 / 16.0)
    rowm = _rows(la.shape) & (CHUNK - 1)
    bc, k = la, 1
    while k < CHUNK:
        bc = bc + jnp.where(rowm >= k, pltpu.roll(bc, k, 0), 0.0)
        k *= 2
    return g, jnp.exp(bc), jnp.exp(-bc)


GLA_HB = 4


def _gla_specs(rb, rmap):
    wk, wv = GLA_HB * HK, GLA_HB * HV
    return [pl.BlockSpec((rb, wk), lambda h, r: (rmap(h, r), OFF_Q // wk + h)),
            pl.BlockSpec((rb, wk), lambda h, r: (rmap(h, r), OFF_K // wk + h)),
            pl.BlockSpec((rb, wv), lambda h, r: (rmap(h, r), OFF_V // wv + h)),
            pl.BlockSpec((rb, 128), lambda h, r: (rmap(h, r), OFF_GK // 128))]


def _gla_fwd(zr, wgk, bgk, ghead, rb):
    s = zr.shape[0]
    nc = rb // CHUNK
    wk, wv = GLA_HB * HK, GLA_HB * HV

    def body(q_ref, k_ref, v_ref, zgk_ref, zog_ref, wgk_ref, bgk_ref, gh_ref, o_ref, og_ref, sp_ref, st_ref, kv_ref):
        @pl.when(pl.program_id(1) == 0)
        def _():
            st_ref[...] = jnp.zeros_like(st_ref)

        _, e_pos, e_neg = _gla_decay(zgk_ref, wgk_ref, bgk_ref, rb)
        lower = _rows((CHUNK, CHUNK)) >= lax.broadcasted_iota(jnp.int32, (CHUNK, CHUNK), 1)
        pairs = [(c, hh) for c in range(nc) for hh in range(GLA_HB)]
        rows = lambda c: slice(c * CHUNK, (c + 1) * CHUNK)
        cols_k = lambda hh: slice(hh * HK, (hh + 1) * HK)
        cols_v = lambda hh: slice(hh * HV, (hh + 1) * HV)
        qfws, pms, e_lasts = {}, {}, {}
        for c, hh in pairs:
            q = q_ref[rows(c), cols_k(hh)].astype(F32) * QSCALE
            k = k_ref[rows(c), cols_k(hh)].astype(F32)
            ec, fc = e_pos[rows(c), cols_k(hh)], e_neg[rows(c), cols_k(hh)]
            qfw = (q * ec).astype(BF16)
            kfw_f = k * fc
            s_fw = _dot_nt(qfw, kfw_f.astype(BF16))
            s_bw = _dot_nt((q * fc).astype(BF16), (k * ec).astype(BF16))
            e_last = _pick_row(ec, CHUNK - 1)
            kv_ref[c, hh] = _dot_tn(v_ref[rows(c), cols_v(hh)], (kfw_f * e_last).astype(BF16))
            qfws[c, hh], pms[c, hh], e_lasts[c, hh] = qfw, jnp.where(lower, s_fw, s_bw).astype(BF16), e_last
        for hh in range(GLA_HB):
            st = st_ref[hh]
            for c in range(nc):
                sp_ref[c, hh] = st.astype(BF16)
                st = st * e_lasts[c, hh] + kv_ref[c, hh]
            st_ref[hh] = st
        for c, hh in pairs:
            o = _dot(pms[c, hh], v_ref[rows(c), cols_v(hh)]) + _dot_nt(qfws[c, hh], sp_ref[c, hh])
            r = lax.rsqrt(jnp.mean(o * o, axis=-1, keepdims=True) + EPS)
            zo = zog_ref[rows(c), cols_v(hh)].astype(F32)
            o_ref[rows(c), cols_v(hh)] = o.astype(BF16)
            og_ref[rows(c), cols_v(hh)] = (o * r * gh_ref[...] * zo * _sigmoid(zo)).astype(BF16)

    rmap = lambda h, r: r
    return pl.pallas_call(
        body, name="gla_fwd", grid=(HEADS // GLA_HB, s // rb),
        in_specs=_gla_specs(rb, rmap) + [
            pl.BlockSpec((rb, wv), lambda h, r: (r, OFF_OG // wv + h)),
            pl.BlockSpec((128, wk), lambda h, r: (0, h)), pl.BlockSpec((1, wk), lambda h, r: (0, h)),
            pl.BlockSpec((1, HV), lambda h, r: (0, 0))],
        out_specs=[pl.BlockSpec((rb, wv), lambda h, r: (r, h)), pl.BlockSpec((rb, wv), lambda h, r: (r, h)),
                   pl.BlockSpec((nc, GLA_HB, HV, HK), lambda h, r: (r, h, 0, 0))],
        out_shape=[jax.ShapeDtypeStruct((s, D), BF16), jax.ShapeDtypeStruct((s, D), BF16),
                   jax.ShapeDtypeStruct((s // CHUNK, HEADS, HV, HK), BF16)],
        scratch_shapes=[pltpu.VMEM((GLA_HB, HV, HK), F32), pltpu.VMEM((nc, GLA_HB, HV, HK), F32)],
        compiler_params=_cp("arbitrary", "arbitrary"),
    )(zr, zr, zr, zr, zr, wgk, bgk, ghead)


def _gla_bwd(zr, do, sp, wgk, bgk, after, dz, rb):
    s = zr.shape[0]
    nc = rb // CHUNK
    nr = s // rb
    wk, wv = GLA_HB * HK, GLA_HB * HV

    def body(q_ref, k_ref, v_ref, zgk_ref, do_ref, sp_ref, wgk_ref, bgk_ref, after_ref, dz_in, dq_ref, dk_ref, dv_ref,
             dg_ref, gt_ref, dbc_ref, gs_ref):
        @pl.when(pl.program_id(1) == 0)
        def _():
            gt_ref[...] = jnp.zeros_like(gt_ref)

        g, e_pos, e_neg = _gla_decay(zgk_ref, wgk_ref, bgk_ref, rb)
        lower = _rows((CHUNK, CHUNK)) >= lax.broadcasted_iota(jnp.int32, (CHUNK, CHUNK), 1)
        is_last = _rows((CHUNK, HK)) == CHUNK - 1
        pairs = [(c, hh) for c in range(nc) for hh in range(GLA_HB)]
        rows = lambda c: slice(c * CHUNK, (c + 1) * CHUNK)
        cols_k = lambda hh: slice(hh * HK, (hh + 1) * HK)
        cols_v = lambda hh: slice(hh * HV, (hh + 1) * HV)
        e_lasts = {}
        for c, hh in pairs:
            ec = e_pos[rows(c), cols_k(hh)]
            qfw = (q_ref[rows(c), cols_k(hh)].astype(F32) * QSCALE * ec).astype(BF16)
            gs_ref[c, hh] = _dot_tn(do_ref[rows(c), cols_v(hh)], qfw)
            e_lasts[c, hh] = _pick_row(ec, CHUNK - 1)
        for hh in range(GLA_HB):
            gt = gt_ref[hh]
            for c in reversed(range(nc)):
                own = gs_ref[c, hh]
                gs_ref[c, hh] = gt
                gt = own + gt * e_lasts[c, hh]
            gt_ref[hh] = gt
        def decayed(c, hh):
            q = q_ref[rows(c), cols_k(hh)].astype(F32) * QSCALE
            k = k_ref[rows(c), cols_k(hh)].astype(F32)
            ec, fc = e_pos[rows(c), cols_k(hh)], e_neg[rows(c), cols_k(hh)]
            return ec, fc, q * ec, k * fc, q * fc, k * ec

        pms, dss = {}, {}
        for c, hh in pairs:
            _, _, qfw_f, kfw_f, qbw_f, kbw_f = decayed(c, hh)
            s_fw = _dot_nt(qfw_f.astype(BF16), kfw_f.astype(BF16))
            s_bw = _dot_nt(qbw_f.astype(BF16), kbw_f.astype(BF16))
            dp = _dot_nt(do_ref[rows(c), cols_v(hh)], v_ref[rows(c), cols_v(hh)])
            pms[c, hh] = jnp.where(lower, s_fw, s_bw).astype(BF16)
            dss[c, hh] = (jnp.where(lower, dp, 0.0).astype(BF16), jnp.where(lower, 0.0, dp).astype(BF16))
        for c, hh in pairs:
            sl, ck, cv = rows(c), cols_k(hh), cols_v(hh)
            v = v_ref[sl, cv]
            dov = do_ref[sl, cv]
            ec, fc, qfw_f, kfw_f, qbw_f, kbw_f = decayed(c, hh)
            qfw, kfw, qbw, kbw = qfw_f.astype(BF16), kfw_f.astype(BF16), qbw_f.astype(BF16), kbw_f.astype(BF16)
            pm = pms[c, hh]
            e_last = e_lasts[c, hh]
            kdec = (kfw_f * e_last).astype(BF16)
            gt = gs_ref[c, hh]
            gtb = gt.astype(BF16)
            spv = sp_ref[c, hh]
            dv_ref[sl, cv] = (_dot_tn(pm, dov) + _dot_nt(kdec, gtb)).astype(BF16)
            ds_fw, ds_bw = dss[c, hh]
            dqfw = _dot(ds_fw, kfw) + _dot(dov, spv)
            dkfw = _dot_tn(ds_fw, qfw)
            dqbw = _dot(ds_bw, kbw)
            dkbw = _dot_tn(ds_bw, qbw)
            dkdec = _dot(v, gtb)
            de_last = (jnp.sum(gt * spv.astype(F32), axis=0, keepdims=True)
                       + jnp.sum(dkdec * kfw_f, axis=0, keepdims=True))
            dkfw = dkfw + dkdec * e_last
            dq_ref[sl, ck] = ((dqfw * ec + dqbw * fc) * QSCALE).astype(BF16)
            dk_ref[sl, ck] = (dkfw * fc + dkbw * ec).astype(BF16)
            dbc = dqfw * qfw_f - dqbw * qbw_f + dkbw * kbw_f - dkfw * kfw_f
            dbc_ref[sl, ck] = dbc + jnp.where(is_last, de_last * e_last, 0.0)
        rowm = _rows((rb, wk)) & (CHUNK - 1)
        dla, kk = dbc_ref[...], 1
        while kk < CHUNK:
            dla = dla + jnp.where(rowm < CHUNK - kk, pltpu.roll(dla, rb - kk, 0), 0.0)
            kk *= 2
        dg_ref[...] = dla * (1.0 / 16.0) * _sigmoid(-g)

    rmap = lambda h, r: nr - 1 - r
    rev = lambda h, r: (nr - 1 - r, h)
    return pl.pallas_call(
        body, name="gla_bwd", grid=(HEADS // GLA_HB, nr),
        in_specs=_gla_specs(rb, rmap) + [
            pl.BlockSpec((rb, wv), rev),
            pl.BlockSpec((nc, GLA_HB, HV, HK), lambda h, r: (nr - 1 - r, h, 0, 0)),
            pl.BlockSpec((128, wk), lambda h, r: (0, h)), pl.BlockSpec((1, wk), lambda h, r: (0, h)), ANY, ANY],
        out_specs=[pl.BlockSpec((rb, wk), rev), pl.BlockSpec((rb, wk), rev),
                   pl.BlockSpec((rb, wv), lambda h, r: (nr - 1 - r, OFF_V // wv + h)), pl.BlockSpec((rb, wk), rev)],
        out_shape=[jax.ShapeDtypeStruct((s, HEADS * HK), BF16), jax.ShapeDtypeStruct((s, HEADS * HK), BF16),
                   jax.ShapeDtypeStruct(dz.shape, BF16), jax.ShapeDtypeStruct((s, HEADS * HK), F32)],
        scratch_shapes=[pltpu.VMEM((GLA_HB, HV, HK), F32), pltpu.VMEM((rb, wk), F32),
                        pltpu.VMEM((nc, GLA_HB, HV, HK), F32)],
        input_output_aliases={9: 2},
        compiler_params=_cp("arbitrary", "arbitrary"),
    )(zr, zr, zr, zr, do, sp, wgk, bgk, after, dz)


def _gk_bwd(dgpre, zr, wgk, after, dz, ts):
    s = zr.shape[0]

    def body(dg_ref, zgk_ref, w_ref, after_ref, dz_in, dz_ref, dw_ref, db_ref):
        @pl.when(pl.program_id(0) == 0)
        def _():
            dw_ref[...] = jnp.zeros_like(dw_ref)
            db_ref[...] = jnp.zeros_like(db_ref)

        dg = dg_ref[...]
        dgb = dg.astype(BF16)
        dz_ref[...] = _dot_nt(dgb, w_ref[...].astype(BF16)).astype(BF16)
        dw_ref[...] += _dot_tn(zgk_ref[...], dgb)
        db_ref[...] += jnp.sum(dg, axis=0, keepdims=True)

    return pl.pallas_call(
        body, name="gk_bwd", grid=(s // ts,),
        in_specs=[pl.BlockSpec((ts, 512), lambda i: (i, 0)), pl.BlockSpec((ts, 128), lambda i: (i, OFF_GK // 128)),
                  pl.BlockSpec((128, 512), lambda i: (0, 0)), ANY, ANY],
        out_specs=[pl.BlockSpec((ts, 128), lambda i: (i, OFF_GK // 128)), pl.BlockSpec((128, 512), lambda i: (0, 0)),
                   pl.BlockSpec((1, 512), lambda i: (0, 0))],
        out_shape=[jax.ShapeDtypeStruct(dz.shape, BF16), jax.ShapeDtypeStruct((128, 512), F32),
                   jax.ShapeDtypeStruct((1, 512), F32)],
        input_output_aliases={4: 0},
        compiler_params=_cp("arbitrary"),
    )(dgpre, zr, wgk, after, dz)


def _merge_fwd(x, zr, pp, og, bgate, wpp, wgla, wout, gffn, after, ts):
    s = x.shape[0]

    def body(x_ref, z0_ref, z1_ref, pp_ref, og_ref, bg_ref, wpp_ref, wgla_ref, wout_ref, gf_ref, after_ref,
             x1_ref, mix_ref, yp_ref, yg_ref, h2_ref):
        ppv = pp_ref[...]
        yp = jnp.concatenate([_dot(ppv, wpp_ref[j]) for j in range(4)], axis=1)
        yg = _dot(og_ref[...], wgla_ref[...])
        g0 = _sigmoid(z0_ref[...].astype(F32) + bg_ref[:, :D])
        g1 = _sigmoid(z1_ref[...].astype(F32) + bg_ref[:, D:])
        mixed = (g0 * yp + g1 * yg).astype(BF16)
        x1 = x_ref[...] + _dot(mixed, wout_ref[...])
        x1_ref[...] = x1
        mix_ref[...] = mixed
        yp_ref[...] = yp.astype(BF16)
        yg_ref[...] = yg.astype(BF16)
        r = lax.rsqrt(jnp.mean(x1 * x1, axis=-1, keepdims=True) + EPS)
        h2_ref[...] = (x1 * r * gf_ref[...]).astype(BF16)

    row = lambda i: (i, 0)
    const2 = lambda i: (0, 0)
    return pl.pallas_call(
        body, name="merge_fwd", grid=(s // ts,),
        in_specs=[pl.BlockSpec((ts, D), row), pl.BlockSpec((ts, D), lambda i: (i, 0)), pl.BlockSpec((ts, D), lambda i: (i, 1)),
                  pl.BlockSpec((ts, POOL_W), row), pl.BlockSpec((ts, D), row), pl.BlockSpec((1, 2 * D), const2),
                  pl.BlockSpec((4, POOL_W, 256), lambda i: (0, 0, 0)), pl.BlockSpec((D, D), const2),
                  pl.BlockSpec((D, D), const2), pl.BlockSpec((1, D), const2), ANY],
        out_specs=[pl.BlockSpec((ts, D), row)] * 5,
        out_shape=[jax.ShapeDtypeStruct((s, D), F32)] + [jax.ShapeDtypeStruct((s, D), BF16)] * 4,
        compiler_params=_cp("arbitrary"),
    )(x, zr, zr, pp, og, bgate, wpp, wgla, wout, gffn, after)


def _merge_bwd(dx1b, zr, yp, yg, o, bgate, ghead, wpp, wgla, wout, after, ts):
    s = dx1b.shape[0]

    def body(dx_ref, z0_ref, z1_ref, zog_ref, yp_ref, yg_ref, o_ref, bg_ref, gh_ref, wpp_ref, wgla_ref, wout_ref, after_ref,
             dzg_ref, dyp_ref, dyg_ref, dpp_ref, do_ref, dzog_ref, dbg_ref, dgh_ref):
        @pl.when(pl.program_id(0) == 0)
        def _():
            dbg_ref[...] = jnp.zeros_like(dbg_ref)
            dgh_ref[...] = jnp.zeros_like(dgh_ref)

        dmix = _dot_nt(dx_ref[...], wout_ref[...])
        g0 = _sigmoid(z0_ref[...].astype(F32) + bg_ref[:, :D])
        g1 = _sigmoid(z1_ref[...].astype(F32) + bg_ref[:, D:])
        dypb = (dmix * g0).astype(BF16)
        dygb = (dmix * g1).astype(BF16)
        dz0 = dmix * yp_ref[...].astype(F32) * g0 * (1.0 - g0)
        dz1 = dmix * yg_ref[...].astype(F32) * g1 * (1.0 - g1)
        dzg_ref[:, :D] = dz0.astype(BF16)
        dzg_ref[:, D:] = dz1.astype(BF16)
        dbg_ref[:, :D] += jnp.sum(dz0, axis=0, keepdims=True)
        dbg_ref[:, D:] += jnp.sum(dz1, axis=0, keepdims=True)
        dyp_ref[...] = dypb
        dyg_ref[...] = dygb
        dpp = _dot_nt(dypb[:, 0:256], wpp_ref[0])
        for j in range(1, 4):
            dpp = dpp + _dot_nt(dypb[:, j * 256:(j + 1) * 256], wpp_ref[j])
        dpp_ref[...] = dpp.astype(BF16)
        dog = _dot_nt(dygb, wgla_ref[...])
        gh = gh_ref[...]
        dgh = jnp.zeros((1, HV), F32)
        for h in range(HEADS):
            cs = slice(h * HV, (h + 1) * HV)
            ov = o_ref[:, cs].astype(F32)
            r = lax.rsqrt(jnp.mean(ov * ov, axis=-1, keepdims=True) + EPS)
            oh = ov * r
            zo = zog_ref[:, cs].astype(F32)
            sg = _sigmoid(zo)
            dog_h = dog[:, cs]
            don = dog_h * zo * sg
            dzog_ref[:, cs] = (dog_h * oh * gh * sg * (1.0 + zo * (1.0 - sg))).astype(BF16)
            dgh = dgh + jnp.sum(don * oh, axis=0, keepdims=True)
            doh = don * gh
            do_ref[:, cs] = (r * (doh - oh * jnp.mean(doh * oh, axis=-1, keepdims=True))).astype(BF16)
        dgh_ref[...] += dgh

    row = lambda i: (i, 0)
    const2 = lambda i: (0, 0)
    return pl.pallas_call(
        body, name="merge_bwd", grid=(s // ts,),
        in_specs=[pl.BlockSpec((ts, D), row), pl.BlockSpec((ts, D), lambda i: (i, 0)), pl.BlockSpec((ts, D), lambda i: (i, 1)),
                  pl.BlockSpec((ts, D), lambda i: (i, OFF_OG // D)), pl.BlockSpec((ts, D), row), pl.BlockSpec((ts, D), row),
                  pl.BlockSpec((ts, D), row), pl.BlockSpec((1, 2 * D), const2), pl.BlockSpec((1, HV), const2),
                  pl.BlockSpec((4, POOL_W, 256), lambda i: (0, 0, 0)), pl.BlockSpec((D, D), const2),
                  pl.BlockSpec((D, D), const2), ANY],
        out_specs=[pl.BlockSpec((ts, 2 * D), row), pl.BlockSpec((ts, D), row), pl.BlockSpec((ts, D), row),
                   pl.BlockSpec((ts, POOL_W), row), pl.BlockSpec((ts, D), row), pl.BlockSpec((ts, D), row),
                   pl.BlockSpec((1, 2 * D), const2), pl.BlockSpec((1, HV), const2)],
        out_shape=[jax.ShapeDtypeStruct((s, N_INR), BF16), jax.ShapeDtypeStruct((s, D), BF16),
                   jax.ShapeDtypeStruct((s, D), BF16), jax.ShapeDtypeStruct((s, POOL_W), BF16),
                   jax.ShapeDtypeStruct((s, D), BF16), jax.ShapeDtypeStruct((s, D), BF16),
                   jax.ShapeDtypeStruct((1, 2 * D), F32), jax.ShapeDtypeStruct((1, HV), F32)],
        compiler_params=_cp("arbitrary"),
    )(dx1b, zr, zr, zr, yp, yg, o, bgate, ghead, wpp, wgla, wout, after)


HALO = 16
CCH = D_FF // 2


def _conv_taps(u_ref, halo_ref, cs, first, ts):
    u = u_ref[:, cs].astype(F32)
    hal = halo_ref[:, cs].astype(F32)
    h1 = jnp.where(first, 0.0, _pick_row(hal, HALO - 1))
    h2 = jnp.where(first, 0.0, _pick_row(hal, HALO - 2))
    row8 = _rows((8, u.shape[1]))
    r1, r2 = pltpu.roll(u, 1, 0), pltpu.roll(u, 2, 0)
    r1 = jnp.concatenate([jnp.where(row8 == 0, h1, r1[:8]), r1[8:]], axis=0)
    r2 = jnp.concatenate([jnp.where(row8 == 0, h2, jnp.where(row8 == 1, h1, r2[:8])), r2[8:]], axis=0)
    return u, r1, r2


def _ffn_down_loss(u, x1, tgt, wconv, bconv, wdown, gfin, ts):
    s = x1.shape[0]

    def body(u_ref, halo_ref, x1_ref, t_ref, wc_ref, bc_ref, wd_ref, gf_ref, a_ref, c_ref, dx_ref, dxb_ref, ls_ref,
             dgf_ref):
        i = pl.program_id(0)

        @pl.when(i == 0)
        def _():
            ls_ref[...] = jnp.zeros_like(ls_ref)
            dgf_ref[...] = jnp.zeros_like(dgf_ref)

        first = i == 0
        acc = x1_ref[...]
        for hf in range(D_FF // CCH):
            cg = slice(hf * CCH, (hf + 1) * CCH)
            cv = slice(D_FF + hf * CCH, D_FF + (hf + 1) * CCH)
            vals = []
            for cs in (cg, cv):
                u0, u1, u2 = _conv_taps(u_ref, halo_ref, cs, first, ts)
                vals.append(bc_ref[:, cs] + wc_ref[0:1, cs] * u2 + wc_ref[1:2, cs] * u1 + wc_ref[2:3, cs] * u0)
                c_ref[:, cs] = vals[-1].astype(BF16)
            a = (vals[0] * _sigmoid(vals[0]) * vals[1]).astype(BF16)
            a_ref[:, cg] = a
            acc = acc + _dot(a, wd_ref[cg, :])
        r = lax.rsqrt(jnp.mean(acc * acc, axis=-1, keepdims=True) + EPS)
        xh = acc * r
        gf = gf_ref[...]
        err = xh * gf - t_ref[...]
        ls_ref[...] += (0.5 / D) * jnp.sum(jnp.sum(err * err, axis=-1, keepdims=True), axis=0, keepdims=True)
        dy = err * (1.0 / D)
        dgf_ref[...] += jnp.sum(dy * xh, axis=0, keepdims=True)
        dxh = dy * gf
        dx = r * (dxh - xh * jnp.mean(dxh * xh, axis=-1, keepdims=True))
        dx_ref[...] = dx
        dxb_ref[...] = dx.astype(BF16)

    row = lambda i: (i, 0)
    const2 = lambda i: (0, 0)
    return pl.pallas_call(
        body, name="ffn_down_loss", grid=(s // ts,),
        in_specs=[pl.BlockSpec((ts, N_UP), row),
                  pl.BlockSpec((HALO, N_UP), lambda i: (jnp.maximum(i * (ts // HALO) - 1, 0), 0)),
                  pl.BlockSpec((ts, D), row), pl.BlockSpec((ts, D), row), pl.BlockSpec((3, N_UP), const2),
                  pl.BlockSpec((1, N_UP), const2), pl.BlockSpec((D_FF, D), const2), pl.BlockSpec((1, D), const2)],
        out_specs=[pl.BlockSpec((ts, D_FF), row), pl.BlockSpec((ts, N_UP), row), pl.BlockSpec((ts, D), row),
                   pl.BlockSpec((ts, D), row), pl.BlockSpec((1, 128), const2), pl.BlockSpec((1, D), const2)],
        out_shape=[jax.ShapeDtypeStruct((s, D_FF), BF16), jax.ShapeDtypeStruct((s, N_UP), BF16),
                   jax.ShapeDtypeStruct((s, D), F32), jax.ShapeDtypeStruct((s, D), BF16),
                   jax.ShapeDtypeStruct((1, 128), F32), jax.ShapeDtypeStruct((1, D), F32)],
        compiler_params=_cp("arbitrary"),
    )(u, u, x1, tgt, wconv, bconv, wdown, gfin)


def _ffn_bwd(dx2b, u, c, wconv, wdown, ts):
    s = dx2b.shape[0]
    nt = s // ts

    def body(dx_ref, u_ref, c_ref, wc_ref, wd_ref, du_ref, db_ref, dw_ref, nxt_ref):
        @pl.when(pl.program_id(0) == 0)
        def _():
            db_ref[...] = jnp.zeros_like(db_ref)
            dw_ref[...] = jnp.zeros_like(dw_ref)
            nxt_ref[...] = jnp.zeros_like(nxt_ref)

        dxv = dx_ref[...]
        row8 = _rows((8, CCH))
        for hf in range(D_FF // CCH):
            cg = slice(hf * CCH, (hf + 1) * CCH)
            cv = slice(D_FF + hf * CCH, D_FF + (hf + 1) * CCH)
            da = _dot_nt(dxv, wd_ref[cg, :])
            gate = c_ref[:, cg].astype(F32)
            val = c_ref[:, cv].astype(F32)
            sg = _sigmoid(gate)
            dcs = (da * val * sg * (1.0 + gate * (1.0 - sg)), da * gate * sg)
            for cs, dc in zip((cg, cv), dcs):
                n1 = nxt_ref[0:1, cs]
                n2 = nxt_ref[1:2, cs]
                r1, r2 = pltpu.roll(dc, ts - 1, 0), pltpu.roll(dc, ts - 2, 0)
                f1 = jnp.concatenate([r1[:ts - 8], jnp.where(row8 == 7, n1, r1[ts - 8:])], axis=0)
                f2 = jnp.concatenate([r2[:ts - 8], jnp.where(row8 == 7, n2, jnp.where(row8 == 6, n1, r2[ts - 8:]))], axis=0)
                uv = u_ref[:, cs].astype(F32)
                db_ref[:, cs] += jnp.sum(dc, axis=0, keepdims=True)
                dw_ref[0:1, cs] += jnp.sum(f2 * uv, axis=0, keepdims=True)
                dw_ref[1:2, cs] += jnp.sum(f1 * uv, axis=0, keepdims=True)
                dw_ref[2:3, cs] += jnp.sum(dc * uv, axis=0, keepdims=True)
                du_ref[:, cs] = (wc_ref[2:3, cs] * dc + wc_ref[1:2, cs] * f1 + wc_ref[0:1, cs] * f2).astype(BF16)
                nxt_ref[:, cs] = dc[0:8, :]

    rev = lambda i: (nt - 1 - i, 0)
    const2 = lambda i: (0, 0)
    return pl.pallas_call(
        body, name="ffn_bwd", grid=(nt,),
        in_specs=[pl.BlockSpec((ts, D), rev), pl.BlockSpec((ts, N_UP), rev), pl.BlockSpec((ts, N_UP), rev),
                  pl.BlockSpec((3, N_UP), const2), pl.BlockSpec((D_FF, D), const2)],
        out_specs=[pl.BlockSpec((ts, N_UP), rev), pl.BlockSpec((1, N_UP), const2), pl.BlockSpec((3, N_UP), const2)],
        out_shape=[jax.ShapeDtypeStruct((s, N_UP), BF16), jax.ShapeDtypeStruct((1, N_UP), F32),
                   jax.ShapeDtypeStruct((3, N_UP), F32)],
        scratch_shapes=[pltpu.VMEM((8, N_UP), F32)],
        compiler_params=_cp("arbitrary"),
    )(dx2b, u, c, wconv, wdown)


ANY = pl.BlockSpec(memory_space=pl.ANY)


def _place():
    x, y, c = lax.axis_index("x"), lax.axis_index("y"), lax.axis_index("c")
    chips = [(1 - x, y), (x, 1 - y), (1 - x, 1 - y)]
    return x, y, c, chips


def _half(shape, c, axis):
    size = shape[axis] // 2
    cut = pl.ds(pl.multiple_of(c * size, 8 if axis == 0 else 128), size)
    return (cut, slice(None)) if axis == 0 else (slice(None), cut)


def _half_shape(shape, axis):
    return (shape[0] // 2, shape[1]) if axis == 0 else (shape[0], shape[1] // 2)


def _remote(src, dst, send_sems, recv_sems, k, to):
    return pltpu.make_async_remote_copy(src_ref=src, dst_ref=dst, send_sem=send_sems.at[k], recv_sem=recv_sems.at[k],
                                        device_id=to, device_id_type=MESH)


def _sibling_exchange(grads, axes, smalls, name):
    nb = len(grads)
    n = nb + len(smalls)

    def body(*refs):
        ins, outs = refs[:n], refs[n:2 * n]
        send_sems, recv_sems = refs[2 * n:]
        x, y, c, _ = _place()
        sib = (x, y, 1 - c)
        cps = []
        for a in range(nb):
            theirs = _half(grads[a].shape[1:], 1 - c, axes[a])
            cps.append(_remote(ins[a].at[(slice(None),) + theirs], outs[a], send_sems, recv_sems, a, sib))
        for a in range(nb, n):
            cps.append(_remote(ins[a], outs[a], send_sems, recv_sems, a, sib))
        for cp in cps:
            cp.start()
        for cp in cps:
            cp.wait()

    out_shape = [jax.ShapeDtypeStruct((4,) + _half_shape(g.shape[1:], ax), g.dtype) for g, ax in zip(grads, axes)]
    out_shape += [jax.ShapeDtypeStruct(a.shape, F32) for a in smalls]
    return pl.pallas_call(
        body, name=name, in_specs=[ANY] * n, out_specs=[ANY] * n, out_shape=out_shape,
        scratch_shapes=[pltpu.SemaphoreType.DMA((n,)), pltpu.SemaphoreType.DMA((n,))],
        compiler_params=pltpu.CompilerParams(has_side_effects=True),
    )(*grads, *smalls)


def _gather_share(lands, axes, name):
    n = len(lands)

    def body(*refs):
        outs = refs[n:2 * n]
        send_sems, recv_sems = refs[2 * n:]
        x, y, c, chips = _place()
        sib = (x, y, 1 - c)
        cps = []
        for a in range(n):
            mine = _half(lands[a].shape[1:], c, axes[a])
            for k, ch in enumerate(chips):
                landed = outs[a].at[(2 * ch[0] + ch[1],) + mine]
                cps.append(_remote(landed, landed, send_sems, recv_sems, 3 * a + k, sib))
        for cp in cps:
            cp.start()
        for a in range(n):
            other = _half(lands[a].shape[1:], 1 - c, axes[a])
            for k, ch in enumerate(chips):
                landed = outs[a].at[(2 * ch[0] + ch[1],) + other]
                _remote(landed, landed, send_sems, recv_sems, 3 * a + k, sib).wait_recv()
        for cp in cps:
            cp.wait_send()

    return pl.pallas_call(
        body, name=name, in_specs=[ANY] * n, out_specs=[ANY] * n,
        out_shape=[jax.ShapeDtypeStruct(a.shape, a.dtype) for a in lands],
        input_output_aliases={a: a for a in range(n)},
        scratch_shapes=[pltpu.SemaphoreType.DMA((3 * n,)), pltpu.SemaphoreType.DMA((3 * n,))],
        compiler_params=pltpu.CompilerParams(has_side_effects=True),
    )(*lands)


def _sibling_share(halves, name):
    n = len(halves)

    def body(*refs):
        ins, outs = refs[:n], refs[n:2 * n]
        send_sems, recv_sems = refs[2 * n:]
        x, y, c, _ = _place()
        cps = [_remote(ins[a], outs[a], send_sems, recv_sems, a, (x, y, 1 - c)) for a in range(n)]
        for cp in cps:
            cp.start()
        for cp in cps:
            cp.wait()

    return pl.pallas_call(
        body, name=name, in_specs=[ANY] * n, out_specs=[ANY] * n,
        out_shape=[jax.ShapeDtypeStruct(h.shape, F32) for h in halves],
        scratch_shapes=[pltpu.SemaphoreType.DMA((n,)), pltpu.SemaphoreType.DMA((n,))],
        compiler_params=pltpu.CompilerParams(has_side_effects=True),
    )(*halves)


HBM = pl.BlockSpec(memory_space=pltpu.HBM)
SEM = pl.BlockSpec(memory_space=pltpu.SEMAPHORE)
DATAFLOW = pltpu.SideEffectType.DATAFLOW_SIDE_EFFECTING


def _split_start(name, srcs, land_shapes, plan, n_copies, after):
    lands = [lax.empty(*ls) if isinstance(ls, tuple) else ls for ls in land_shapes]
    bufs = list(srcs) + lands
    nb, ns = len(bufs), len(srcs)

    def body(*refs):
        send_sems, recv_sems, token = refs[nb + 1], refs[nb + 2], refs[-1]
        for k, (src, dst, to) in enumerate(plan(refs[:ns], refs[ns:nb])):
            _remote(src, dst, send_sems, recv_sems, k, to).start()
        token[...] = jnp.zeros_like(token)

    res = pl.pallas_call(
        body, name=name,
        out_shape=(pltpu.SemaphoreType.DMA((n_copies,)), pltpu.SemaphoreType.DMA((n_copies,)),
                   *[pltpu.HBM(b.shape, b.dtype) for b in bufs], jax.ShapeDtypeStruct((8, 128), F32)),
        in_specs=[HBM] * nb + [ANY],
        out_specs=(SEM, SEM, *[HBM] * nb, pl.BlockSpec(memory_space=pltpu.VMEM)),
        input_output_aliases={i: 2 + i for i in range(nb)},
        compiler_params=pltpu.CompilerParams(has_side_effects=DATAFLOW),
    )(*[pltpu.with_memory_space_constraint(b, pltpu.HBM) for b in bufs], after)
    return (res[0], res[1], list(res[2:2 + nb])), res[-1]


def _split_wait(name, handle, n_srcs, plan, after):
    send_sems, recv_sems, bufs = handle
    nb = len(bufs)

    def body(*refs):
        sends, recvs = refs[nb], refs[nb + 1]
        for k, (src, dst, to) in enumerate(plan(refs[:n_srcs], refs[n_srcs:nb])):
            cp = _remote(src, dst, sends, recvs, k, to)
            cp.wait_send()
            cp.wait_recv()

    res = pl.pallas_call(
        body, name=name, out_shape=[pltpu.HBM(b.shape, b.dtype) for b in bufs],
        in_specs=[HBM] * nb + [SEM, SEM, ANY], out_specs=[HBM] * nb,
        input_output_aliases={i: i for i in range(nb)},
        compiler_params=pltpu.CompilerParams(has_side_effects=DATAFLOW),
    )(*bufs, send_sems, recv_sems, after)
    return list(res[:n_srcs]), list(res[n_srcs:])


def _gather_plan(shapes, axes, n_whole=0):
    def plan(srcs, lands):
        x, y, c, chips = _place()
        me = 2 * x + y
        out = []
        for a, (shape, axis) in enumerate(zip(shapes, axes)):
            own = lands[a].at[(me,) + _half(shape, c, axis)]
            for ch in chips:
                out.append((own, own, (ch[0], ch[1], c)))
        for a in range(len(shapes), len(shapes) + n_whole):
            for ch in chips:
                out.append((lands[a].at[me], lands[a].at[me], (ch[0], ch[1], c)))
        return out
    return plan


def _share_plan(shapes, axes):
    def plan(srcs, lands):
        x, y, c, chips = _place()
        out = []
        for a, (shape, axis) in enumerate(zip(shapes, axes)):
            mine = _half(shape, c, axis)
            for ch in chips:
                landed = lands[a].at[(2 * ch[0] + ch[1],) + mine]
                out.append((landed, landed, (x, y, 1 - c)))
        return out
    return plan


def _sibling_plan(shapes, axes):
    def plan(srcs, lands):
        x, y, c, _ = _place()
        return [(srcs[a].at[(slice(None),) + _half(shape, 1 - c, axis)], lands[a], (x, y, 1 - c))
                for a, (shape, axis) in enumerate(zip(shapes, axes))]
    return plan


def _whole_to_sibling_plan(n):
    def plan(srcs, lands):
        x, y, c, _ = _place()
        return [(srcs[a], lands[a], (x, y, 1 - c)) for a in range(n)]
    return plan


def _reduce_plan(n_big, n_small):
    def plan(srcs, lands):
        x, y, c, chips = _place()
        out = []
        for a in range(n_big):
            for k, ch in enumerate(chips):
                out.append((srcs[a].at[2 * ch[0] + ch[1]], lands[a].at[k], (ch[0], ch[1], c)))
        for a in range(n_big, n_big + n_small):
            for ch in chips:
                out.append((srcs[a], lands[a].at[2 * x + y], (ch[0], ch[1], c)))
        return out
    return plan


def _row_tile(rows, cols, mult):
    best = mult
    for t in range(mult, rows + 1, mult):
        if rows % t == 0 and t * cols * 4 <= (2 << 20):
            best = t
    return best if rows % best == 0 else rows


COL_TILE = 256


def _half_tiling(hshape, axis, mult):
    hr, hc = hshape
    if axis == 0:
        tr = _row_tile(hr, hc, mult)
        return tr, hc, hr // tr
    return hr, COL_TILE, hc // COL_TILE


def _tile_idx(axis, t):
    return (t, 0) if axis == 0 else (0, t)


def _chip_partial(place, g, t, axis, name):
    hshape = t.shape[1:]
    br, bc, nt = _half_tiling(hshape, axis, 16)

    def body(pl_ref, g_ref, t_ref, pf_ref, pb_ref):
        v = g_ref[...].astype(F32) + t_ref[...].astype(F32)
        pb_ref[...] = v.astype(BF16)

        @pl.when(pl.program_id(1) == pl_ref[0])
        def _():
            pf_ref[...] = v

    blk = (None, br, bc)
    return pl.pallas_call(
        body, name=name,
        grid_spec=pltpu.PrefetchScalarGridSpec(
            num_scalar_prefetch=1, grid=(nt, 4),
            in_specs=[pl.BlockSpec(blk, lambda i, j, p: (j,) + _tile_idx(axis, p[1] * nt + i)),
                      pl.BlockSpec(blk, lambda i, j, p: (j,) + _tile_idx(axis, i))],
            out_specs=[pl.BlockSpec((br, bc), lambda i, j, p: _tile_idx(axis, i)),
                       pl.BlockSpec(blk, lambda i, j, p: (j,) + _tile_idx(axis, i))]),
        out_shape=[jax.ShapeDtypeStruct(hshape, F32), jax.ShapeDtypeStruct((4,) + hshape, BF16)],
        compiler_params=_cp("arbitrary", "arbitrary"),
    )(place, g, t)


def _finish_half(pf, rb, axis, name):
    hshape = pf.shape
    br, bc, nt = _half_tiling(hshape, axis, 16)

    def body(pf_ref, rb_ref, o_ref):
        o_ref[...] = ((pf_ref[...] + rb_ref[0].astype(F32)) + rb_ref[1].astype(F32)) + rb_ref[2].astype(F32)

    return pl.pallas_call(
        body, name=name, grid=(nt,),
        in_specs=[pl.BlockSpec((br, bc), lambda i: _tile_idx(axis, i)),
                  pl.BlockSpec((3, br, bc), lambda i: (0,) + _tile_idx(axis, i))],
        out_specs=pl.BlockSpec((br, bc), lambda i: _tile_idx(axis, i)),
        out_shape=jax.ShapeDtypeStruct(hshape, F32),
        compiler_params=_cp("arbitrary"),
    )(pf, rb)


def _adam_math(w, g, m, v):
    m = ADAM_B1 * m + (1.0 - ADAM_B1) * g
    v = ADAM_B2 * v + (1.0 - ADAM_B2) * (g * g)
    m_hat = m / (1.0 - ADAM_B1 ** ADAM_STEP)
    v_hat = v / (1.0 - ADAM_B2 ** ADAM_STEP)
    return -ADAM_LR * (m_hat / (jnp.sqrt(v_hat) + ADAM_EPS) + ADAM_WD * w), m, v


def _adam_halves(place, w, mine, theirs, m, v, axis, name):
    br, bc, nt = _half_tiling(mine.shape, axis, 8)

    def body(pl_ref, w_ref, a_ref, b_ref, m_ref, v_ref, g_ref, d_ref, mo_ref, vo_ref):
        is_mine = pl.program_id(0) // nt == pl_ref[1]
        g = jnp.where(is_mine, a_ref[...], b_ref[...])
        d, mn, vn = _adam_math(w_ref[...], g, m_ref[...], v_ref[...])
        g_ref[...] = g
        d_ref[...] = d
        mo_ref[...] = mn
        vo_ref[...] = vn

    full = pl.BlockSpec((br, bc), lambda i, p: _tile_idx(axis, i))
    mine_spec = pl.BlockSpec((br, bc), lambda i, p: _tile_idx(axis, jnp.where(i // nt == p[1], i % nt, nt - 1)))
    theirs_spec = pl.BlockSpec((br, bc), lambda i, p: _tile_idx(axis, jnp.where(i // nt == p[1], 0, i % nt)))
    return pl.pallas_call(
        body, name=name,
        grid_spec=pltpu.PrefetchScalarGridSpec(
            num_scalar_prefetch=1, grid=(2 * nt,), in_specs=[full, mine_spec, theirs_spec, full, full],
            out_specs=[full] * 4),
        out_shape=[jax.ShapeDtypeStruct(w.shape, F32)] * 4, compiler_params=_cp("arbitrary"),
    )(place, w, mine, theirs, m, v)


def _add_many(xs, ys, name):
    n = len(xs)

    def body(*refs):
        for i in range(n):
            refs[2 * n + i][...] = refs[i][...] + refs[n + i][...]

    return pl.pallas_call(body, name=name, out_shape=[jax.ShapeDtypeStruct(a.shape, F32) for a in xs])(*xs, *ys)


def _adam_small(place, owns, landed, ws, ms, vs, widths):
    n, nw = len(owns), len(ws)

    def body(pl_ref, *refs):
        own_r, land_r = refs[:n], refs[n:2 * n]
        w_r, m_r, v_r = (refs[2 * n + k * nw:2 * n + (k + 1) * nw] for k in range(3))
        outs = refs[2 * n + 3 * nw:]
        g_o, d_o, m_o, v_o = outs[:n], outs[n:n + nw], outs[n + nw:n + 2 * nw], outs[n + 2 * nw:]
        for me in range(4):
            @pl.when(pl_ref[0] == me)
            def _(me=me):
                for i in range(n):
                    p = [own_r[i][...] if k == me else land_r[i][k] for k in range(4)]
                    g = ((p[0] + p[1]) + p[2]) + p[3]
                    if i < nw and widths[i]:
                        g = g[:, me * widths[i]:(me + 1) * widths[i]]
                    g_o[i][...] = g
                    if i < nw:
                        d, mn, vn = _adam_math(w_r[i][...], g, m_r[i][...], v_r[i][...])
                        d_o[i][...] = d
                        m_o[i][...] = mn
                        v_o[i][...] = vn

    g_shapes = [jax.ShapeDtypeStruct(ws[i].shape if i < nw else owns[i].shape, F32) for i in range(n)]
    w_shapes = [jax.ShapeDtypeStruct(w.shape, F32) for w in ws]
    whole = lambda a: pl.BlockSpec(a.shape, lambda i, p, nd=len(a.shape): (0,) * nd)
    ins = list(owns) + list(landed) + list(ws) + list(ms) + list(vs)
    out_shape = g_shapes + w_shapes * 3
    out = pl.pallas_call(
        body, name="adam_small",
        grid_spec=pltpu.PrefetchScalarGridSpec(num_scalar_prefetch=1, grid=(1,), in_specs=[whole(a) for a in ins],
                                               out_specs=[whole(a) for a in out_shape]),
        out_shape=out_shape, compiler_params=_cp("arbitrary"),
    )(place, *ins)
    return out[:n], out[n:n + nw], out[n + nw:n + 2 * nw], out[n + 2 * nw:]


def kernel(x, g_mix, w_in, b_gate, w_gk_up, b_gk, w_pool_grp, pool_scale, g_gla_head, w_pool_proj, w_gla_proj, w_out, g_ffn, w_up, w_conv, b_conv, w_down, g_final, loss_target, m_g_mix, m_w_in, m_b_gate, m_w_gk_up, m_b_gk, m_w_pool_grp, m_pool_scale, m_g_gla_head, m_w_pool_proj, m_w_gla_proj, m_w_out, m_g_ffn, m_w_up, m_w_conv, m_b_conv, m_w_down, m_g_final, v_g_mix, v_w_in, v_b_gate, v_w_gk_up, v_b_gk, v_w_pool_grp, v_pool_scale, v_g_gla_head, v_w_pool_proj, v_w_gla_proj, v_w_out, v_g_ffn, v_w_up, v_w_conv, v_b_conv, v_w_down, v_g_final):
    s = x.shape[1]
    ts = min(s, 512)
    tm = min(s, 256)
    cx, cy, cc = lax.axis_index("x"), lax.axis_index("y"), lax.axis_index("c")
    chip = 2 * cx + cy
    place = jnp.stack([chip, cc]).astype(jnp.int32)

    big_names = ("w_in", "w_pool_proj", "w_gla_proj", "w_out", "w_up", "w_down")
    axes = (1, 0, 0, 0, 0, 0)
    shards = dict(w_in=jnp.transpose(w_in[0]), w_pool_proj=w_pool_proj[0], w_gla_proj=w_gla_proj[0], w_out=w_out[0],
                  w_up=w_up[0], w_down=w_down[0])
    def gather_start(tag, halves, group_axes, whole, after):
        plan = _gather_plan([o_.shape for o_ in halves], group_axes, len(whole))
        lands = [lax.dynamic_update_slice(lax.empty((4,) + o_.shape, o_.dtype), o_[None], (chip, 0, 0))
                 for o_ in list(halves) + list(whole)]
        handle, token = _split_start("gather_" + tag + "_start", [], lands, plan, 3 * len(lands), after)
        return (handle, plan, len(halves), group_axes), token

    def gather_finish(tag, started, after):
        handle, plan, n_halves, group_axes = started
        lands = _split_wait("gather_" + tag + "_wait", handle, 0, plan, after)[1]
        lands[:n_halves] = _gather_share(lands[:n_halves], group_axes, "gather_" + tag + "_share")
        return lands

    in_w, tok = gather_start("in", [jnp.transpose(w_in[0].astype(BF16))], axes[:1], [], g_mix)
    zero = tok[0, 0]
    own = [(shards[n] + zero).astype(BF16) for n in big_names[1:]]
    mix_w, tok = gather_start("mix", own[0:3], axes[1:4], [w_gk_up[0] + zero, w_conv[0] + zero], tok)
    up_w, tok = gather_start("up", own[3:4], axes[4:5], [], tok)
    down_w, tok = gather_start("down", own[4:5], axes[5:6], [], tok)

    def forward_start(tag, started, after):
        handle, plan, n_halves, group_axes = started
        lands = _split_wait("gather_" + tag + "_wait", handle, 0, plan, after)[1]
        plan = _share_plan([l_.shape[1:] for l_ in lands[:n_halves]], group_axes)
        share, token = _split_start("gather_" + tag + "_share_start", [], lands[:n_halves], plan, 3 * n_halves, after)
        return (share, plan, lands[n_halves:]), token

    def forward_done(tag, forwarded, after):
        share, plan, _ = forwarded
        return _split_wait("gather_" + tag + "_share_wait", share, 0, plan, after)[1]
    xs, tgt = x[0], loss_target[0]
    wgrp = w_pool_grp[0]
    h = _rmsnorm(xs, g_mix, tok, "norm_mix", ts)
    m_in_t, v_in_t = jnp.transpose(m_w_in[0]), jnp.transpose(v_w_in[0])
    h, m_in_t, v_in_t = lax.optimization_barrier((h, m_in_t, v_in_t))
    w_in_t = gather_finish("in", in_w, h)[0]
    nsh = N_IN // 4

    zr, w_in_rt = _in_proj(h, w_in_t, PROJ_TILE)
    p, pp = _pool_fwd(zr, wgrp, pool_scale)
    mix_f, tok = forward_start("mix", mix_w, pp)
    wgk4, wconv4 = mix_f[2]
    wgk_full = jnp.transpose(wgk4, (1, 0, 2)).reshape(GATE_RANK, 512) + tok[0, 0]
    wconv_full = jnp.transpose(wconv4, (1, 0, 2)).reshape(3, N_UP)
    wgk_pad = jnp.concatenate([wgk_full, jnp.zeros((128 - GATE_RANK, 512), F32)], axis=0)
    o, og, sp = _gla_fwd(zr, wgk_pad, b_gk, g_gla_head, ts)
    wpp, wgla, wout = forward_done("mix", mix_f, og)
    wgla, wout = wgla.reshape(D, D), wout.reshape(D, D)
    up_f, tok = forward_start("up", up_w, og)
    x1, mixed, yp, yg, h2 = _merge_fwd(xs, zr, pp, og, b_gate, wpp, wgla, wout, g_ffn, tok, ts)
    wup, = forward_done("up", up_f, x1)
    down_f, tok = forward_start("down", down_w, x1)
    u = _matmul_resident(h2, wup, tok, "ffn_up")
    wdown = forward_done("down", down_f, u)[0].reshape(D_FF, D)
    a, conv_out, dx2, dx2b, loss_part, dgfin = _ffn_down_loss(u, x1, tgt, wconv_full, b_conv, wdown,
                                                              g_final.reshape(1, D), tm)

    du, dbconv, dwconv = _ffn_bwd(dx2b, u, conv_out, wconv_full, wdown, tm)
    dw_down = _matmul_tn(a, dx2b, "dw_down", D, tm=D_FF // 2)
    dw_up = _matmul_tn(h2, du, "dw_up", UP_SHARD, shard_major=True)

    def exchange_start(tag, grads, group_axes, after):
        plan = _sibling_plan([g.shape[1:] for g in grads], group_axes)
        lands = [((4,) + _half_shape(g.shape[1:], ax), g.dtype) for g, ax in zip(grads, group_axes)]
        handle, token = _split_start("sibling_" + tag + "_start", grads, lands, plan, len(grads), after)
        return (handle, plan, len(grads)), token

    def partials(tag, names, group_axes, exchange, after):
        handle, plan, n = exchange
        mine, theirs = _split_wait("sibling_" + tag + "_wait", handle, n, plan, after)
        return zip(*[_chip_partial(place, g, t, ax, "chip_partial_" + nm)
                     for nm, ax, g, t in zip(names, group_axes, mine, theirs)])

    ffn_names, ffn_axes = ("w_up", "w_down"), (0, 0)
    ffn_x, token = exchange_start("ffn", [dw_up, dw_down.reshape(4, 704, D)], ffn_axes, du)
    dx1, dx1b, dgffn = _matmul_nt_normbwd(du, wup, x1, g_ffn, dx2, token, "ffn_up_bwd", ts)
    ffn_pf, ffn_pb = partials("ffn", ffn_names, ffn_axes, ffn_x, dx1b)
    ffn_plan = _reduce_plan(2, 0)
    ffn_handle, token = _split_start("reduce_ffn_start", ffn_pb, [((3,) + p.shape[1:], BF16) for p in ffn_pb],
                                     ffn_plan, 6, ffn_pf[0])

    dzr, dyp, dyg, dpp, do, dzog, dbgate, dghead = _merge_bwd(dx1b, zr, yp, yg, o, b_gate, g_gla_head, wpp, wgla, wout,
                                                             token, ts)
    dzr = lax.dynamic_update_slice(dzr, dzog, (0, OFF_OG))
    dw_out = _matmul_tn(mixed, dx1b, "dw_out", D, tm=512)
    dw_gla = _matmul_tn(og, dyg, "dw_gla", D, tm=512)
    dw_pp = _matmul_tn(pp, dyp, "dw_pp", 256, shard_major=True)

    out_names, out_axes = ("w_pool_proj", "w_gla_proj", "w_out"), (0, 0, 0)
    out_x, token = exchange_start("out", [dw_pp, dw_gla.reshape(4, 256, D), dw_out.reshape(4, 256, D)], out_axes, dpp)
    dzr, dwgrp, dscale = _pool_bwd(p, dpp, wgrp, pool_scale, token, dzr)
    out_pf, out_pb = partials("out", out_names, out_axes, out_x, dwgrp)
    out_plan = _reduce_plan(3, 0)
    out_handle, token = _split_start("reduce_out_start", out_pb, [((3,) + p_.shape[1:], BF16) for p_ in out_pb],
                                     out_plan, 9, out_pf[0])
    dq, dk, dzr, dgpre = _gla_bwd(zr, do, sp, wgk_pad, b_gk, token, dzr, ts)
    dzr, dwgk, dbgk = _gk_bwd(dgpre, zr, wgk_pad, dgpre, dzr, ts)
    dzr = lax.dynamic_update_slice(lax.dynamic_update_slice(dzr, dq, (0, OFF_Q)), dk, (0, OFF_K))
    dw_rt = _matmul_tn(dzr, h, "dw_in", D, tm=PROJ_TILE)

    def grad_rows(lo, hi):
        out = []
        for seg_lo, seg_hi, at in ((0, 1536, OFF_POOL), (1536, 3584, OFF_V), (3584, 3600, OFF_GK), (3600, N_IN, OFF_GATE)):
            a_, b_ = max(lo, seg_lo), min(hi, seg_hi)
            if a_ < b_:
                out.append(dw_rt[at + a_ - seg_lo:at + b_ - seg_lo])
        return jnp.concatenate(out, axis=0)

    dw_in_t = jnp.stack([grad_rows(j * nsh, (j + 1) * nsh) for j in range(4)])

    ms = dict(w_in=m_in_t, w_pool_proj=m_w_pool_proj[0], w_gla_proj=m_w_gla_proj[0], w_out=m_w_out[0],
              w_up=m_w_up[0], w_down=m_w_down[0])
    vs = dict(w_in=v_in_t, w_pool_proj=v_w_pool_proj[0], w_gla_proj=v_w_gla_proj[0], w_out=v_w_out[0],
              w_up=v_w_up[0], w_down=v_w_down[0])
    grad, delta, new_m, new_v = {}, {}, {}, {}

    def finish(names, group_axes, part_f, landed):
        return [_finish_half(pf, rb, ax, "finish_" + n) for n, ax, pf, rb in zip(names, group_axes, part_f, landed)]

    def update(names, group_axes, halves, sib_halves):
        for n, ax, mine, theirs in zip(names, group_axes, halves, sib_halves):
            res = _adam_halves(place, shards[n], mine, theirs, ms[n], vs[n], ax, "adam_" + n)
            if n == "w_in":
                res = [jnp.transpose(r_) for r_ in res]
            grad[n], delta[n], new_m[n], new_v[n] = [r_[None] for r_ in res]

    rest_names, rest_axes = ffn_names + out_names, ffn_axes + out_axes
    in_x, token = exchange_start("in", [dw_in_t], (1,), dw_rt)
    _, ffn_landed = _split_wait("reduce_ffn_wait", ffn_handle, 2, ffn_plan, token)
    _, out_landed = _split_wait("reduce_out_wait", out_handle, 3, out_plan, ffn_landed[0])
    rest_halves = lax.optimization_barrier(finish(rest_names, rest_axes, ffn_pf + out_pf, ffn_landed + out_landed))
    (in_pf,), (in_pb,) = partials("in", ("w_in",), (1,), in_x, rest_halves[-1])
    in_plan = _reduce_plan(1, 0)
    in_handle, token = _split_start("reduce_in_start", [in_pb], [((3,) + in_pb.shape[1:], BF16)], in_plan, 3, in_pf)
    rest_plan = _whole_to_sibling_plan(len(rest_halves))
    rest_share, token = _split_start("sibling_share_rest_start", rest_halves, [(h_.shape, F32) for h_ in rest_halves],
                                     rest_plan, len(rest_halves), token)
    grad_x, _, dgmix = _matmul_nt_normbwd(dzr, w_in_rt, xs, g_mix, dx1, token, "in_proj_bwd", ts, transposed=True)
    small_names = ("g_mix", "b_gate", "w_gk_up", "b_gk", "w_pool_grp", "pool_scale", "g_gla_head", "g_ffn", "w_conv",
                   "b_conv", "g_final")
    small_mine = [dgmix, dbgate, dwgk[:GATE_RANK], dbgk, dwgrp.reshape(4 * 128, 128), dscale, dghead, dgffn, dwconv, dbconv,
                  dgfin, loss_part]
    small_sib = _sibling_exchange([], (), small_mine, "sibling_exchange_small")
    small_chip = _add_many(small_mine, small_sib, "chip_partial_small")
    small_plan = _reduce_plan(0, len(small_chip))
    small_handle, token = _split_start("reduce_small_start", small_chip, [((4,) + a_.shape, F32) for a_ in small_chip],
                                       small_plan, 3 * len(small_chip), small_mine[0])

    rest_halves, rest_sib = _split_wait("sibling_share_rest_wait", rest_share, len(rest_halves), rest_plan, token)
    n_ffn = len(ffn_names)
    update(out_names, out_axes, rest_halves[n_ffn:], rest_sib[n_ffn:])
    updated = lax.optimization_barrier([delta[n] for n in out_names])
    _, in_landed = _split_wait("reduce_in_wait", in_handle, 1, in_plan, updated[0])
    in_halves = finish(("w_in",), (1,), (in_pf,), in_landed)
    update(("w_in",), (1,), in_halves, _sibling_share(in_halves, "sibling_share_in"))
    ffn_halves, _ = lax.optimization_barrier((rest_halves[:n_ffn], delta["w_in"]))
    update(ffn_names, ffn_axes, ffn_halves, rest_sib[:n_ffn])
    small_sent, small_landed = _split_wait("reduce_small_wait", small_handle, len(small_chip), small_plan, delta["w_in"])
    given = dict(g_mix=(g_mix, m_g_mix, v_g_mix), b_gate=(b_gate, m_b_gate, v_b_gate), w_gk_up=(w_gk_up, m_w_gk_up, v_w_gk_up),
                 b_gk=(b_gk, m_b_gk, v_b_gk), w_pool_grp=(w_pool_grp, m_w_pool_grp, v_w_pool_grp),
                 pool_scale=(pool_scale, m_pool_scale, v_pool_scale), g_gla_head=(g_gla_head, m_g_gla_head, v_g_gla_head),
                 g_ffn=(g_ffn, m_g_ffn, v_g_ffn), w_conv=(w_conv, m_w_conv, v_w_conv), b_conv=(b_conv, m_b_conv, v_b_conv),
                 g_final=(g_final, m_g_final, v_g_final))
    flat2 = lambda a: a.reshape(-1, a.shape[-1])
    widths = [dict(w_gk_up=HK, w_conv=UP_SHARD).get(n) for n in small_names]
    totals, ds, mo, vo = _adam_small(place, small_sent, small_landed, *[[flat2(given[n][k]) for n in small_names] for k in range(3)],
                                     widths)
    loss = totals[-1][0, 0]
    for i, n in enumerate(small_names):
        shp = given[n][0].shape
        grad[n], delta[n], new_m[n], new_v[n] = [r_.reshape(shp) for r_ in (totals[i], ds[i], mo[i], vo[i])]

    order = ("g_mix", "w_in", "b_gate", "w_gk_up", "b_gk", "w_pool_grp", "pool_scale", "g_gla_head", "w_pool_proj",
             "w_gla_proj", "w_out", "g_ffn", "w_up", "w_conv", "b_conv", "w_down", "g_final")
    return (loss, grad_x[None], *[grad[n] for n in order], *[delta[n] for n in order], *[new_m[n] for n in order],
            *[new_v[n] for n in order])
```

```python
import jax
import jax.numpy as jnp
from jax import lax
from jax.experimental import pallas as pl
from jax.experimental.pallas import tpu as pltpu

F32 = jnp.float32
BF16 = jnp.bfloat16
MESH = pl.DeviceIdType.MESH

D = 1024
EPS = 1e-6
CHUNK = 64
POOL_W = 512
POOL_WINDOWS = (2, 4, 8, 16)
HEADS = 4
HK = 128
HV = 256
GATE_RANK = 16
D_FF = 2816
N_UP = 2 * D_FF
N_IN = 5648
QSCALE = HK ** -0.5
N_INR = 5760
OFF_GATE, OFF_V, OFF_OG, OFF_POOL, OFF_Q, OFF_K, OFF_GK = 0, 2048, 3072, 4096, 4608, 5120, 5632

ADAM_LR, ADAM_B1, ADAM_B2, ADAM_EPS, ADAM_WD, ADAM_STEP = 0.001, 0.9, 0.999, 1e-08, 0.01, 10

VMEM_LIMIT = 56 * 1024 * 1024
PROJ_TILE = N_INR // 5
UP_SHARD = N_UP // 4


def _cp(*sem):
    return pltpu.CompilerParams(dimension_semantics=sem if sem else None, vmem_limit_bytes=VMEM_LIMIT)


def _dot(a, b):
    return jnp.dot(a, b, preferred_element_type=F32)


def _dot_nt(a, b):
    return lax.dot_general(a, b, (((1,), (1,)), ((), ())), preferred_element_type=F32)


def _dot_tn(a, b):
    return lax.dot_general(a, b, (((0,), (0,)), ((), ())), preferred_element_type=F32)


def _sigmoid(v):
    return 1.0 / (1.0 + jnp.exp(-v))


def _rows(shape):
    return lax.broadcasted_iota(jnp.int32, shape, 0)


def _pick_row(v, r):
    return jnp.sum(jnp.where(_rows(v.shape) == r, v, 0.0), axis=0, keepdims=True)


def _rmsnorm(x, g, after, name, ts):
    s = x.shape[0]

    def body(x_ref, g_ref, after_ref, h_ref):
        xv = x_ref[...]
        r = lax.rsqrt(jnp.mean(xv * xv, axis=-1, keepdims=True) + EPS)
        h_ref[...] = (xv * r * g_ref[...]).astype(BF16)

    return pl.pallas_call(
        body, name=name, grid=(s // ts,),
        in_specs=[pl.BlockSpec((ts, D), lambda i: (i, 0)), pl.BlockSpec((1, D), lambda i: (0, 0)), ANY],
        out_specs=pl.BlockSpec((ts, D), lambda i: (i, 0)), out_shape=jax.ShapeDtypeStruct((s, D), BF16),
        compiler_params=_cp("arbitrary"),
    )(x, g, after)


MM_ROWS = 512


def _matmul_resident(h, w, after, name):
    s = h.shape[0]
    nj, tn = w.shape[0], w.shape[2]
    rc = min(s, MM_ROWS)

    def body(h_ref, w_ref, after_ref, z_ref):
        for r0 in range(0, s, rc):
            z_ref[r0:r0 + rc, :] = _dot(h_ref[r0:r0 + rc, :], w_ref[...]).astype(BF16)

    return pl.pallas_call(
        body, name=name, grid=(nj,),
        in_specs=[pl.BlockSpec((s, D), lambda j: (0, 0)), pl.BlockSpec((None, D, tn), lambda j: (j, 0, 0)), ANY],
        out_specs=pl.BlockSpec((s, tn), lambda j: (0, j)), out_shape=jax.ShapeDtypeStruct((s, nj * tn), BF16),
        compiler_params=_cp("arbitrary"),
    )(h, w, after)


PROJ_PIECES = ((3600, 2048, OFF_GATE), (1536, 2048, OFF_V), (0, 1536, OFF_POOL), (3584, GATE_RANK, OFF_GK))


def _split_by_shard(pieces, rows_per_shard):
    out = []
    for src, n, dst in pieces:
        while n > 0:
            j, r = divmod(src, rows_per_shard)
            m = min(n, rows_per_shard - r)
            out.append((j, r, m, dst))
            src, n, dst = src + m, n - m, dst + m
    return tuple(out)


PROJ_SEGMENTS = _split_by_shard(PROJ_PIECES, N_IN // 4)


def _in_proj(h, w4, tn):
    s = h.shape[0]
    rc = min(s, MM_ROWS)
    nj = N_INR // tn
    first_use = [dst // tn for _, _, _, dst in PROJ_SEGMENTS]

    def body(h_ref, w_hbm, z_ref, wo_hbm, w_ref, stage, sems, out_sem):
        j = pl.program_id(0)
        cps = [pltpu.make_async_copy(w_hbm.at[k], stage.at[k], sems.at[k]) for k in range(4)]
        out_cp = pltpu.make_async_copy(w_ref, wo_hbm, out_sem.at[0])

        @pl.when(j == 0)
        def _():
            for cp in cps:
                cp.start()
            w_ref[OFF_GK + GATE_RANK:, :] = jnp.zeros((N_INR - OFF_GK - GATE_RANK, D), BF16)

        landed = set()
        for step in range(nj):
            due = [seg for seg, at in zip(PROJ_SEGMENTS, first_use) if at == step]
            if due:
                fresh = sorted({seg[0] for seg in due} - landed)
                landed.update(fresh)

                @pl.when(j == step)
                def _(due=due, fresh=fresh, last=step == max(first_use)):
                    for k in fresh:
                        cps[k].wait()
                    for k, r, n, dst in due:
                        w_ref[dst:dst + n, :] = stage[k, r:r + n, :]
                    if last:
                        out_cp.start()

        wt = w_ref[pl.ds(pl.multiple_of(j * tn, 128), tn), :]
        for r0 in range(0, s, rc):
            z_ref[r0:r0 + rc, :] = _dot_nt(h_ref[r0:r0 + rc, :], wt).astype(BF16)

        @pl.when(j == nj - 1)
        def _():
            out_cp.wait()

    return pl.pallas_call(
        body, name="in_proj", grid=(nj,),
        in_specs=[pl.BlockSpec((s, D), lambda j: (0, 0)), ANY],
        out_specs=[pl.BlockSpec((s, tn), lambda j: (0, j)), ANY],
        out_shape=[jax.ShapeDtypeStruct((s, N_INR), BF16), jax.ShapeDtypeStruct((N_INR, D), BF16)],
        scratch_shapes=[pltpu.VMEM((N_INR, D), BF16), pltpu.VMEM(w4.shape, BF16), pltpu.SemaphoreType.DMA((4,)),
                        pltpu.SemaphoreType.DMA((1,))],
        compiler_params=_cp("arbitrary"),
    )(h, w4)


def _matmul_nt_normbwd(dz, w, x, g, resid, after, name, ts, transposed=False):
    s = x.shape[0]
    w_vmem = w.shape if transposed else (D, w.shape[0] * w.shape[2])
    n_sems = 1 if transposed else w.shape[0]

    def body(dz_ref, w_hbm, x_ref, g_ref, r_ref, after_ref, o_ref, ob_ref, dg_ref, w_ref, sems):
        @pl.when(pl.program_id(0) == 0)
        def _():
            if transposed:
                cps = [pltpu.make_async_copy(w_hbm, w_ref, sems.at[0])]
            else:
                kc = w.shape[2]
                cps = [pltpu.make_async_copy(w_hbm.at[j], w_ref.at[:, pl.ds(j * kc, kc)], sems.at[j])
                       for j in range(w.shape[0])]
            for cp in cps:
                cp.start()
            for cp in cps:
                cp.wait()
            dg_ref[...] = jnp.zeros_like(dg_ref)

        dh = _dot(dz_ref[...], w_ref[...]) if transposed else _dot_nt(dz_ref[...], w_ref[...])
        xv = x_ref[...]
        r = lax.rsqrt(jnp.mean(xv * xv, axis=-1, keepdims=True) + EPS)
        xh = xv * r
        dg_ref[...] += jnp.sum(dh * xh, axis=0, keepdims=True)
        dxh = dh * g_ref[...]
        out = r_ref[...] + r * (dxh - xh * jnp.mean(dxh * xh, axis=-1, keepdims=True))
        o_ref[...] = out
        ob_ref[...] = out.astype(BF16)

    row = lambda i: (i, 0)
    kdim = dz.shape[1]
    return pl.pallas_call(
        body, name=name, grid=(s // ts,),
        in_specs=[pl.BlockSpec((ts, kdim), row), ANY, pl.BlockSpec((ts, D), row),
                  pl.BlockSpec((1, D), lambda i: (0, 0)), pl.BlockSpec((ts, D), row), ANY],
        out_specs=[pl.BlockSpec((ts, D), row), pl.BlockSpec((ts, D), row), pl.BlockSpec((1, D), lambda i: (0, 0))],
        out_shape=[jax.ShapeDtypeStruct((s, D), F32), jax.ShapeDtypeStruct((s, D), BF16),
                   jax.ShapeDtypeStruct((1, D), F32)],
        scratch_shapes=[pltpu.VMEM(w_vmem, BF16), pltpu.SemaphoreType.DMA((n_sems,))],
        compiler_params=_cp("arbitrary"),
    )(dz, w, x, g, resid, after)


def _matmul_tn(a, b, name, tn, shard_major=False, tm=None):
    s, m = a.shape
    n = b.shape[1]
    tm = m if tm is None else tm
    ni, nj = m // tm, n // tn

    def body(a_ref, b_ref, o_ref):
        o_ref[...] = _dot_tn(a_ref[...], b_ref[...]).astype(BF16)

    if shard_major:
        out_spec = pl.BlockSpec((None, tm, tn), lambda i, j: (j, i, 0))
        out_shape = jax.ShapeDtypeStruct((nj, m, tn), BF16)
    else:
        out_spec = pl.BlockSpec((tm, tn), lambda i, j: (i, j))
        out_shape = jax.ShapeDtypeStruct((m, n), BF16)
    return pl.pallas_call(
        body, name=name, grid=(ni, nj),
        in_specs=[pl.BlockSpec((s, tm), lambda i, j: (0, i)), pl.BlockSpec((s, tn), lambda i, j: (0, j))],
        out_specs=out_spec, out_shape=out_shape,
        compiler_params=_cp("arbitrary", "arbitrary"),
    )(a, b)


def _pool_fwd(zr, wgrp, scale):
    s = zr.shape[0]

    def body(u_ref, w_ref, sc_ref, p_ref, pp_ref):
        row = _rows((s, 128))
        for gi, win in enumerate(POOL_WINDOWS):
            cs = slice(gi * 128, (gi + 1) * 128)
            u = u_ref[:, cs].astype(F32)
            acc, k = u, 1
            while k < win:
                acc = acc + jnp.where(row >= k, pltpu.roll(acc, k, 0), 0.0)
                k *= 2
            cnt = jnp.minimum(row + 1, win).astype(F32)
            p = (acc / cnt - u).astype(BF16)
            p_ref[:, cs] = p
            pp_ref[:, cs] = (_dot(p, w_ref[gi].astype(BF16)) * sc_ref[:, cs]).astype(BF16)

    return pl.pallas_call(
        body, name="pool_fwd", grid=(1,),
        in_specs=[pl.BlockSpec((s, POOL_W), lambda i: (0, OFF_POOL // POOL_W)),
                  pl.BlockSpec((4, 128, 128), lambda i: (0, 0, 0)), pl.BlockSpec((1, POOL_W), lambda i: (0, 0))],
        out_specs=[pl.BlockSpec((s, POOL_W), lambda i: (0, 0))] * 2,
        out_shape=[jax.ShapeDtypeStruct((s, POOL_W), BF16)] * 2,
        compiler_params=_cp("arbitrary"),
    )(zr, wgrp, scale)


def _pool_bwd(p, dpp, wgrp, scale, after, dz):
    s = p.shape[0]

    def body(p_ref, dpp_ref, w_ref, sc_ref, after_ref, dz_in, dz_ref, dw_ref, dsc_ref):
        row = _rows((s, 128))
        for gi, win in enumerate(POOL_WINDOWS):
            cs = slice(gi * 128, (gi + 1) * 128)
            pv = p_ref[:, cs]
            wb = w_ref[gi].astype(BF16)
            dpp_v = dpp_ref[:, cs].astype(F32)
            dsc_ref[:, cs] = jnp.sum(dpp_v * _dot(pv, wb), axis=0, keepdims=True)
            dpm = (dpp_v * sc_ref[:, cs]).astype(BF16)
            dw_ref[gi] = _dot_tn(pv, dpm)
            dp = _dot_nt(dpm, wb)
            cnt = jnp.minimum(row + 1, win).astype(F32)
            acc, k = dp / cnt, 1
            while k < win:
                acc = acc + jnp.where(row < s - k, pltpu.roll(acc, s - k, 0), 0.0)
                k *= 2
            dz_ref[:, cs] = (acc - dp).astype(BF16)

    full = lambda i: (0, 0)
    return pl.pallas_call(
        body, name="pool_bwd", grid=(1,),
        in_specs=[pl.BlockSpec((s, POOL_W), full), pl.BlockSpec((s, POOL_W), full),
                  pl.BlockSpec((4, 128, 128), lambda i: (0, 0, 0)), pl.BlockSpec((1, POOL_W), full), ANY, ANY],
        out_specs=[pl.BlockSpec((s, POOL_W), lambda i: (0, OFF_POOL // POOL_W)),
                   pl.BlockSpec((4, 128, 128), lambda i: (0, 0, 0)), pl.BlockSpec((1, POOL_W), full)],
        out_shape=[jax.ShapeDtypeStruct(dz.shape, BF16), jax.ShapeDtypeStruct((4, 128, 128), F32),
                   jax.ShapeDtypeStruct((1, POOL_W), F32)],
        input_output_aliases={5: 0},
        compiler_params=_cp("arbitrary"),
    )(p, dpp, wgrp, scale, after, dz)


def _gla_decay(zgk_ref, wgk_ref, bgk_ref, rb):
    g = _dot(zgk_ref[...], wgk_ref[...].astype(BF16)) + bgk_ref[...]
    la = (jnp.minimum(g, 0.0) - jnp.log(1.0 + jnp.exp(-jnp.abs(g)))) * (1.0 / 16.0)
    rowm = _rows(la.shape) & (CHUNK - 1)
    bc, k = la, 1
    while k < CHUNK:
        bc = bc + jnp.where(rowm >= k, pltpu.roll(bc, k, 0), 0.0)
        k *= 2
    return g, jnp.exp(bc), jnp.exp(-bc)


GLA_HB = 4


def _gla_specs(rb, rmap):
    wk, wv = GLA_HB * HK, GLA_HB * HV
    return [pl.BlockSpec((rb, wk), lambda h, r: (rmap(h, r), OFF_Q // wk + h)),
            pl.BlockSpec((rb, wk), lambda h, r: (rmap(h, r), OFF_K // wk + h)),
            pl.BlockSpec((rb, wv), lambda h, r: (rmap(h, r), OFF_V // wv + h)),
            pl.BlockSpec((rb, 128), lambda h, r: (rmap(h, r), OFF_GK // 128))]


def _gla_fwd(zr, wgk, bgk, ghead, rb):
    s = zr.shape[0]
    nc = rb // CHUNK
    wk, wv = GLA_HB * HK, GLA_HB * HV

    def body(q_ref, k_ref, v_ref, zgk_ref, zog_ref, wgk_ref, bgk_ref, gh_ref, o_ref, og_ref, sp_ref, st_ref, kv_ref):
        @pl.when(pl.program_id(1) == 0)
        def _():
            st_ref[...] = jnp.zeros_like(st_ref)

        _, e_pos, e_neg = _gla_decay(zgk_ref, wgk_ref, bgk_ref, rb)
        lower = _rows((CHUNK, CHUNK)) >= lax.broadcasted_iota(jnp.int32, (CHUNK, CHUNK), 1)
        pairs = [(c, hh) for c in range(nc) for hh in range(GLA_HB)]
        rows = lambda c: slice(c * CHUNK, (c + 1) * CHUNK)
        cols_k = lambda hh: slice(hh * HK, (hh + 1) * HK)
        cols_v = lambda hh: slice(hh * HV, (hh + 1) * HV)
        qfws, pms, e_lasts = {}, {}, {}
        for c, hh in pairs:
            q = q_ref[rows(c), cols_k(hh)].astype(F32) * QSCALE
            k = k_ref[rows(c), cols_k(hh)].astype(F32)
            ec, fc = e_pos[rows(c), cols_k(hh)], e_neg[rows(c), cols_k(hh)]
            qfw = (q * ec).astype(BF16)
            kfw_f = k * fc
            s_fw = _dot_nt(qfw, kfw_f.astype(BF16))
            s_bw = _dot_nt((q * fc).astype(BF16), (k * ec).astype(BF16))
            e_last = _pick_row(ec, CHUNK - 1)
            kv_ref[c, hh] = _dot_tn(v_ref[rows(c), cols_v(hh)], (kfw_f * e_last).astype(BF16))
            qfws[c, hh], pms[c, hh], e_lasts[c, hh] = qfw, jnp.where(lower, s_fw, s_bw).astype(BF16), e_last
        for hh in range(GLA_HB):
            st = st_ref[hh]
            for c in range(nc):
                sp_ref[c, hh] = st.astype(BF16)
                st = st * e_lasts[c, hh] + kv_ref[c, hh]
            st_ref[hh] = st
        for c, hh in pairs:
            o = _dot(pms[c, hh], v_ref[rows(c), cols_v(hh)]) + _dot_nt(qfws[c, hh], sp_ref[c, hh])
            r = lax.rsqrt(jnp.mean(o * o, axis=-1, keepdims=True) + EPS)
            zo = zog_ref[rows(c), cols_v(hh)].astype(F32)
            o_ref[rows(c), cols_v(hh)] = o.astype(BF16)
            og_ref[rows(c), cols_v(hh)] = (o * r * gh_ref[...] * zo * _sigmoid(zo)).astype(BF16)

    rmap = lambda h, r: r
    return pl.pallas_call(
        body, name="gla_fwd", grid=(HEADS // GLA_HB, s // rb),
        in_specs=_gla_specs(rb, rmap) + [
            pl.BlockSpec((rb, wv), lambda h, r: (r, OFF_OG // wv + h)),
            pl.BlockSpec((128, wk), lambda h, r: (0, h)), pl.BlockSpec((1, wk), lambda h, r: (0, h)),
            pl.BlockSpec((1, HV), lambda h, r: (0, 0))],
        out_specs=[pl.BlockSpec((rb, wv), lambda h, r: (r, h)), pl.BlockSpec((rb, wv), lambda h, r: (r, h)),
                   pl.BlockSpec((nc, GLA_HB, HV, HK), lambda h, r: (r, h, 0, 0))],
        out_shape=[jax.ShapeDtypeStruct((s, D), BF16), jax.ShapeDtypeStruct((s, D), BF16),
                   jax.ShapeDtypeStruct((s // CHUNK, HEADS, HV, HK), BF16)],
        scratch_shapes=[pltpu.VMEM((GLA_HB, HV, HK), F32), pltpu.VMEM((nc, GLA_HB, HV, HK), F32)],
        compiler_params=_cp("arbitrary", "arbitrary"),
    )(zr, zr, zr, zr, zr, wgk, bgk, ghead)


def _gla_bwd(zr, do, sp, wgk, bgk, after, dz, rb):
    s = zr.shape[0]
    nc = rb // CHUNK
    nr = s // rb
    wk, wv = GLA_HB * HK, GLA_HB * HV

    def body(q_ref, k_ref, v_ref, zgk_ref, do_ref, sp_ref, wgk_ref, bgk_ref, after_ref, dz_in, dq_ref, dk_ref, dv_ref,
             dg_ref, gt_ref, dbc_ref, gs_ref):
        @pl.when(pl.program_id(1) == 0)
        def _():
            gt_ref[...] = jnp.zeros_like(gt_ref)

        g, e_pos, e_neg = _gla_decay(zgk_ref, wgk_ref, bgk_ref, rb)
        lower = _rows((CHUNK, CHUNK)) >= lax.broadcasted_iota(jnp.int32, (CHUNK, CHUNK), 1)
        is_last = _rows((CHUNK, HK)) == CHUNK - 1
        pairs = [(c, hh) for c in range(nc) for hh in range(GLA_HB)]
        rows = lambda c: slice(c * CHUNK, (c + 1) * CHUNK)
        cols_k = lambda hh: slice(hh * HK, (hh + 1) * HK)
        cols_v = lambda hh: slice(hh * HV, (hh + 1) * HV)
        e_lasts = {}
        for c, hh in pairs:
            ec = e_pos[rows(c), cols_k(hh)]
            qfw = (q_ref[rows(c), cols_k(hh)].astype(F32) * QSCALE * ec).astype(BF16)
            gs_ref[c, hh] = _dot_tn(do_ref[rows(c), cols_v(hh)], qfw)
            e_lasts[c, hh] = _pick_row(ec, CHUNK - 1)
        for hh in range(GLA_HB):
            gt = gt_ref[hh]
            for c in reversed(range(nc)):
                own = gs_ref[c, hh]
                gs_ref[c, hh] = gt
                gt = own + gt * e_lasts[c, hh]
            gt_ref[hh] = gt
        def decayed(c, hh):
            q = q_ref[rows(c), cols_k(hh)].astype(F32) * QSCALE
            k = k_ref[rows(c), cols_k(hh)].astype(F32)
            ec, fc = e_pos[rows(c), cols_k(hh)], e_neg[rows(c), cols_k(hh)]
            return ec, fc, q * ec, k * fc, q * fc, k * ec

        pms, dss = {}, {}
        for c, hh in pairs:
            _, _, qfw_f, kfw_f, qbw_f, kbw_f = decayed(c, hh)
            s_fw = _dot_nt(qfw_f.astype(BF16), kfw_f.astype(BF16))
            s_bw = _dot_nt(qbw_f.astype(BF16), kbw_f.astype(BF16))
            dp = _dot_nt(do_ref[rows(c), cols_v(hh)], v_ref[rows(c), cols_v(hh)])
            pms[c, hh] = jnp.where(lower, s_fw, s_bw).astype(BF16)
            dss[c, hh] = (jnp.where(lower, dp, 0.0).astype(BF16), jnp.where(lower, 0.0, dp).astype(BF16))
        for c, hh in pairs:
            sl, ck, cv = rows(c), cols_k(hh), cols_v(hh)
            v = v_ref[sl, cv]
            dov = do_ref[sl, cv]
            ec, fc, qfw_f, kfw_f, qbw_f, kbw_f = decayed(c, hh)
            qfw, kfw, qbw, kbw = qfw_f.astype(BF16), kfw_f.astype(BF16), qbw_f.astype(BF16), kbw_f.astype(BF16)
            pm = pms[c, hh]
            e_last = e_lasts[c, hh]
            kdec = (kfw_f * e_last).astype(BF16)
            gt = gs_ref[c, hh]
            gtb = gt.astype(BF16)
            spv = sp_ref[c, hh]
            dv_ref[sl, cv] = (_dot_tn(pm, dov) + _dot_nt(kdec, gtb)).astype(BF16)
            ds_fw, ds_bw = dss[c, hh]
            dqfw = _dot(ds_fw, kfw) + _dot(dov, spv)
            dkfw = _dot_tn(ds_fw, qfw)
            dqbw = _dot(ds_bw, kbw)
            dkbw = _dot_tn(ds_bw, qbw)
            dkdec = _dot(v, gtb)
            de_last = (jnp.sum(gt * spv.astype(F32), axis=0, keepdims=True)
                       + jnp.sum(dkdec * kfw_f, axis=0, keepdims=True))
            dkfw = dkfw + dkdec * e_last
            dq_ref[sl, ck] = ((dqfw * ec + dqbw * fc) * QSCALE).astype(BF16)
            dk_ref[sl, ck] = (dkfw * fc + dkbw * ec).astype(BF16)
            dbc = dqfw * qfw_f - dqbw * qbw_f + dkbw * kbw_f - dkfw * kfw_f
            dbc_ref[sl, ck] = dbc + jnp.where(is_last, de_last * e_last, 0.0)
        rowm = _rows((rb, wk)) & (CHUNK - 1)
        dla, kk = dbc_ref[...], 1
        while kk < CHUNK:
            dla = dla + jnp.where(rowm < CHUNK - kk, pltpu.roll(dla, rb - kk, 0), 0.0)
            kk *= 2
        dg_ref[...] = dla * (1.0 / 16.0) * _sigmoid(-g)

    rmap = lambda h, r: nr - 1 - r
    rev = lambda h, r: (nr - 1 - r, h)
    return pl.pallas_call(
        body, name="gla_bwd", grid=(HEADS // GLA_HB, nr),
        in_specs=_gla_specs(rb, rmap) + [
            pl.BlockSpec((rb, wv), rev),
            pl.BlockSpec((nc, GLA_HB, HV, HK), lambda h, r: (nr - 1 - r, h, 0, 0)),
            pl.BlockSpec((128, wk), lambda h, r: (0, h)), pl.BlockSpec((1, wk), lambda h, r: (0, h)), ANY, ANY],
        out_specs=[pl.BlockSpec((rb, wk), rev), pl.BlockSpec((rb, wk), rev),
                   pl.BlockSpec((rb, wv), lambda h, r: (nr - 1 - r, OFF_V // wv + h)), pl.BlockSpec((rb, wk), rev)],
        out_shape=[jax.ShapeDtypeStruct((s, HEADS * HK), BF16), jax.ShapeDtypeStruct((s, HEADS * HK), BF16),
                   jax.ShapeDtypeStruct(dz.shape, BF16), jax.ShapeDtypeStruct((s, HEADS * HK), F32)],
        scratch_shapes=[pltpu.VMEM((GLA_HB, HV, HK), F32), pltpu.VMEM((rb, wk), F32),
                        pltpu.VMEM((nc, GLA_HB, HV, HK), F32)],
        input_output_aliases={9: 2},
        compiler_params=_cp("arbitrary", "arbitrary"),
    )(zr, zr, zr, zr, do, sp, wgk, bgk, after, dz)


def _gk_bwd(dgpre, zr, wgk, after, dz, ts):
    s = zr.shape[0]

    def body(dg_ref, zgk_ref, w_ref, after_ref, dz_in, dz_ref, dw_ref, db_ref):
        @pl.when(pl.program_id(0) == 0)
        def _():
            dw_ref[...] = jnp.zeros_like(dw_ref)
            db_ref[...] = jnp.zeros_like(db_ref)

        dg = dg_ref[...]
        dgb = dg.astype(BF16)
        dz_ref[...] = _dot_nt(dgb, w_ref[...].astype(BF16)).astype(BF16)
        dw_ref[...] += _dot_tn(zgk_ref[...], dgb)
        db_ref[...] += jnp.sum(dg, axis=0, keepdims=True)

    return pl.pallas_call(
        body, name="gk_bwd", grid=(s // ts,),
        in_specs=[pl.BlockSpec((ts, 512), lambda i: (i, 0)), pl.BlockSpec((ts, 128), lambda i: (i, OFF_GK // 128)),
                  pl.BlockSpec((128, 512), lambda i: (0, 0)), ANY, ANY],
        out_specs=[pl.BlockSpec((ts, 128), lambda i: (i, OFF_GK // 128)), pl.BlockSpec((128, 512), lambda i: (0, 0)),
                   pl.BlockSpec((1, 512), lambda i: (0, 0))],
        out_shape=[jax.ShapeDtypeStruct(dz.shape, BF16), jax.ShapeDtypeStruct((128, 512), F32),
                   jax.ShapeDtypeStruct((1, 512), F32)],
        input_output_aliases={4: 0},
        compiler_params=_cp("arbitrary"),
    )(dgpre, zr, wgk, after, dz)


def _merge_fwd(x, zr, pp, og, bgate, wpp, wgla, wout, gffn, after, ts):
    s = x.shape[0]

    def body(x_ref, z0_ref, z1_ref, pp_ref, og_ref, bg_ref, wpp_ref, wgla_ref, wout_ref, gf_ref, after_ref,
             x1_ref, mix_ref, yp_ref, yg_ref, h2_ref):
        ppv = pp_ref[...]
        yp = jnp.concatenate([_dot(ppv, wpp_ref[j]) for j in range(4)], axis=1)
        yg = _dot(og_ref[...], wgla_ref[...])
        g0 = _sigmoid(z0_ref[...].astype(F32) + bg_ref[:, :D])
        g1 = _sigmoid(z1_ref[...].astype(F32) + bg_ref[:, D:])
        mixed = (g0 * yp + g1 * yg).astype(BF16)
        x1 = x_ref[...] + _dot(mixed, wout_ref[...])
        x1_ref[...] = x1
        mix_ref[...] = mixed
        yp_ref[...] = yp.astype(BF16)
        yg_ref[...] = yg.astype(BF16)
        r = lax.rsqrt(jnp.mean(x1 * x1, axis=-1, keepdims=True) + EPS)
        h2_ref[...] = (x1 * r * gf_ref[...]).astype(BF16)

    row = lambda i: (i, 0)
    const2 = lambda i: (0, 0)
    return pl.pallas_call(
        body, name="merge_fwd", grid=(s // ts,),
        in_specs=[pl.BlockSpec((ts, D), row), pl.BlockSpec((ts, D), lambda i: (i, 0)), pl.BlockSpec((ts, D), lambda i: (i, 1)),
                  pl.BlockSpec((ts, POOL_W), row), pl.BlockSpec((ts, D), row), pl.BlockSpec((1, 2 * D), const2),
                  pl.BlockSpec((4, POOL_W, 256), lambda i: (0, 0, 0)), pl.BlockSpec((D, D), const2),
                  pl.BlockSpec((D, D), const2), pl.BlockSpec((1, D), const2), ANY],
        out_specs=[pl.BlockSpec((ts, D), row)] * 5,
        out_shape=[jax.ShapeDtypeStruct((s, D), F32)] + [jax.ShapeDtypeStruct((s, D), BF16)] * 4,
        compiler_params=_cp("arbitrary"),
    )(x, zr, zr, pp, og, bgate, wpp, wgla, wout, gffn, after)


def _merge_bwd(dx1b, zr, yp, yg, o, bgate, ghead, wpp, wgla, wout, after, ts):
    s = dx1b.shape[0]

    def body(dx_ref, z0_ref, z1_ref, zog_ref, yp_ref, yg_ref, o_ref, bg_ref, gh_ref, wpp_ref, wgla_ref, wout_ref, after_ref,
             dzg_ref, dyp_ref, dyg_ref, dpp_ref, do_ref, dzog_ref, dbg_ref, dgh_ref):
        @pl.when(pl.program_id(0) == 0)
        def _():
            dbg_ref[...] = jnp.zeros_like(dbg_ref)
            dgh_ref[...] = jnp.zeros_like(dgh_ref)

        dmix = _dot_nt(dx_ref[...], wout_ref[...])
        g0 = _sigmoid(z0_ref[...].astype(F32) + bg_ref[:, :D])
        g1 = _sigmoid(z1_ref[...].astype(F32) + bg_ref[:, D:])
        dypb = (dmix * g0).astype(BF16)
        dygb = (dmix * g1).astype(BF16)
        dz0 = dmix * yp_ref[...].astype(F32) * g0 * (1.0 - g0)
        dz1 = dmix * yg_ref[...].astype(F32) * g1 * (1.0 - g1)
        dzg_ref[:, :D] = dz0.astype(BF16)
        dzg_ref[:, D:] = dz1.astype(BF16)
        dbg_ref[:, :D] += jnp.sum(dz0, axis=0, keepdims=True)
        dbg_ref[:, D:] += jnp.sum(dz1, axis=0, keepdims=True)
        dyp_ref[...] = dypb
        dyg_ref[...] = dygb
        dpp = _dot_nt(dypb[:, 0:256], wpp_ref[0])
        for j in range(1, 4):
            dpp = dpp + _dot_nt(dypb[:, j * 256:(j + 1) * 256], wpp_ref[j])
        dpp_ref[...] = dpp.astype(BF16)
        dog = _dot_nt(dygb, wgla_ref[...])
        gh = gh_ref[...]
        dgh = jnp.zeros((1, HV), F32)
        for h in range(HEADS):
            cs = slice(h * HV, (h + 1) * HV)
            ov = o_ref[:, cs].astype(F32)
            r = lax.rsqrt(jnp.mean(ov * ov, axis=-1, keepdims=True) + EPS)
            oh = ov * r
            zo = zog_ref[:, cs].astype(F32)
            sg = _sigmoid(zo)
            dog_h = dog[:, cs]
            don = dog_h * zo * sg
            dzog_ref[:, cs] = (dog_h * oh * gh * sg * (1.0 + zo * (1.0 - sg))).astype(BF16)
            dgh = dgh + jnp.sum(don * oh, axis=0, keepdims=True)
            doh = don * gh
            do_ref[:, cs] = (r * (doh - oh * jnp.mean(doh * oh, axis=-1, keepdims=True))).astype(BF16)
        dgh_ref[...] += dgh

    row = lambda i: (i, 0)
    const2 = lambda i: (0, 0)
    return pl.pallas_call(
        body, name="merge_bwd", grid=(s // ts,),
        in_specs=[pl.BlockSpec((ts, D), row), pl.BlockSpec((ts, D), lambda i: (i, 0)), pl.BlockSpec((ts, D), lambda i: (i, 1)),
                  pl.BlockSpec((ts, D), lambda i: (i, OFF_OG // D)), pl.BlockSpec((ts, D), row), pl.BlockSpec((ts, D), row),
                  pl.BlockSpec((ts, D), row), pl.BlockSpec((1, 2 * D), const2), pl.BlockSpec((1, HV), const2),
                  pl.BlockSpec((4, POOL_W, 256), lambda i: (0, 0, 0)), pl.BlockSpec((D, D), const2),
                  pl.BlockSpec((D, D), const2), ANY],
        out_specs=[pl.BlockSpec((ts, 2 * D), row), pl.BlockSpec((ts, D), row), pl.BlockSpec((ts, D), row),
                   pl.BlockSpec((ts, POOL_W), row), pl.BlockSpec((ts, D), row), pl.BlockSpec((ts, D), row),
                   pl.BlockSpec((1, 2 * D), const2), pl.BlockSpec((1, HV), const2)],
        out_shape=[jax.ShapeDtypeStruct((s, N_INR), BF16), jax.ShapeDtypeStruct((s, D), BF16),
                   jax.ShapeDtypeStruct((s, D), BF16), jax.ShapeDtypeStruct((s, POOL_W), BF16),
                   jax.ShapeDtypeStruct((s, D), BF16), jax.ShapeDtypeStruct((s, D), BF16),
                   jax.ShapeDtypeStruct((1, 2 * D), F32), jax.ShapeDtypeStruct((1, HV), F32)],
        compiler_params=_cp("arbitrary"),
    )(dx1b, zr, zr, zr, yp, yg, o, bgate, ghead, wpp, wgla, wout, after)


HALO = 16
CCH = D_FF // 2


def _conv_taps(u_ref, halo_ref, cs, first, ts):
    u = u_ref[:, cs].astype(F32)
    hal = halo_ref[:, cs].astype(F32)
    h1 = jnp.where(first, 0.0, _pick_row(hal, HALO - 1))
    h2 = jnp.where(first, 0.0, _pick_row(hal, HALO - 2))
    row8 = _rows((8, u.shape[1]))
    r1, r2 = pltpu.roll(u, 1, 0), pltpu.roll(u, 2, 0)
    r1 = jnp.concatenate([jnp.where(row8 == 0, h1, r1[:8]), r1[8:]], axis=0)
    r2 = jnp.concatenate([jnp.where(row8 == 0, h2, jnp.where(row8 == 1, h1, r2[:8])), r2[8:]], axis=0)
    return u, r1, r2


def _ffn_down_loss(u, x1, tgt, wconv, bconv, wdown, gfin, ts):
    s = x1.shape[0]

    def body(u_ref, halo_ref, x1_ref, t_ref, wc_ref, bc_ref, wd_ref, gf_ref, a_ref, c_ref, dx_ref, dxb_ref, ls_ref,
             dgf_ref):
        i = pl.program_id(0)

        @pl.when(i == 0)
        def _():
            ls_ref[...] = jnp.zeros_like(ls_ref)
            dgf_ref[...] = jnp.zeros_like(dgf_ref)

        first = i == 0
        acc = x1_ref[...]
        for hf in range(D_FF // CCH):
            cg = slice(hf * CCH, (hf + 1) * CCH)
            cv = slice(D_FF + hf * CCH, D_FF + (hf + 1) * CCH)
            vals = []
            for cs in (cg, cv):
                u0, u1, u2 = _conv_taps(u_ref, halo_ref, cs, first, ts)
                vals.append(bc_ref[:, cs] + wc_ref[0:1, cs] * u2 + wc_ref[1:2, cs] * u1 + wc_ref[2:3, cs] * u0)
                c_ref[:, cs] = vals[-1].astype(BF16)
            a = (vals[0] * _sigmoid(vals[0]) * vals[1]).astype(BF16)
            a_ref[:, cg] = a
            acc = acc + _dot(a, wd_ref[cg, :])
        r = lax.rsqrt(jnp.mean(acc * acc, axis=-1, keepdims=True) + EPS)
        xh = acc * r
        gf = gf_ref[...]
        err = xh * gf - t_ref[...]
        ls_ref[...] += (0.5 / D) * jnp.sum(jnp.sum(err * err, axis=-1, keepdims=True), axis=0, keepdims=True)
        dy = err * (1.0 / D)
        dgf_ref[...] += jnp.sum(dy * xh, axis=0, keepdims=True)
        dxh = dy * gf
        dx = r * (dxh - xh * jnp.mean(dxh * xh, axis=-1, keepdims=True))
        dx_ref[...] = dx
        dxb_ref[...] = dx.astype(BF16)

    row = lambda i: (i, 0)
    const2 = lambda i: (0, 0)
    return pl.pallas_call(
        body, name="ffn_down_loss", grid=(s // ts,),
        in_specs=[pl.BlockSpec((ts, N_UP), row),
                  pl.BlockSpec((HALO, N_UP), lambda i: (jnp.maximum(i * (ts // HALO) - 1, 0), 0)),
                  pl.BlockSpec((ts, D), row), pl.BlockSpec((ts, D), row), pl.BlockSpec((3, N_UP), const2),
                  pl.BlockSpec((1, N_UP), const2), pl.BlockSpec((D_FF, D), const2), pl.BlockSpec((1, D), const2)],
        out_specs=[pl.BlockSpec((ts, D_FF), row), pl.BlockSpec((ts, N_UP), row), pl.BlockSpec((ts, D), row),
                   pl.BlockSpec((ts, D), row), pl.BlockSpec((1, 128), const2), pl.BlockSpec((1, D), const2)],
        out_shape=[jax.ShapeDtypeStruct((s, D_FF), BF16), jax.ShapeDtypeStruct((s, N_UP), BF16),
                   jax.ShapeDtypeStruct((s, D), F32), jax.ShapeDtypeStruct((s, D), BF16),
                   jax.ShapeDtypeStruct((1, 128), F32), jax.ShapeDtypeStruct((1, D), F32)],
        compiler_params=_cp("arbitrary"),
    )(u, u, x1, tgt, wconv, bconv, wdown, gfin)


def _ffn_bwd(dx2b, u, c, wconv, wdown, ts):
    s = dx2b.shape[0]
    nt = s // ts

    def body(dx_ref, u_ref, c_ref, wc_ref, wd_ref, du_ref, db_ref, dw_ref, nxt_ref):
        @pl.when(pl.program_id(0) == 0)
        def _():
            db_ref[...] = jnp.zeros_like(db_ref)
            dw_ref[...] = jnp.zeros_like(dw_ref)
            nxt_ref[...] = jnp.zeros_like(nxt_ref)

        dxv = dx_ref[...]
        row8 = _rows((8, CCH))
        for hf in range(D_FF // CCH):
            cg = slice(hf * CCH, (hf + 1) * CCH)
            cv = slice(D_FF + hf * CCH, D_FF + (hf + 1) * CCH)
            da = _dot_nt(dxv, wd_ref[cg, :])
            gate = c_ref[:, cg].astype(F32)
            val = c_ref[:, cv].astype(F32)
            sg = _sigmoid(gate)
            dcs = (da * val * sg * (1.0 + gate * (1.0 - sg)), da * gate * sg)
            for cs, dc in zip((cg, cv), dcs):
                n1 = nxt_ref[0:1, cs]
                n2 = nxt_ref[1:2, cs]
                r1, r2 = pltpu.roll(dc, ts - 1, 0), pltpu.roll(dc, ts - 2, 0)
                f1 = jnp.concatenate([r1[:ts - 8], jnp.where(row8 == 7, n1, r1[ts - 8:])], axis=0)
                f2 = jnp.concatenate([r2[:ts - 8], jnp.where(row8 == 7, n2, jnp.where(row8 == 6, n1, r2[ts - 8:]))], axis=0)
                uv = u_ref[:, cs].astype(F32)
                db_ref[:, cs] += jnp.sum(dc, axis=0, keepdims=True)
                dw_ref[0:1, cs] += jnp.sum(f2 * uv, axis=0, keepdims=True)
                dw_ref[1:2, cs] += jnp.sum(f1 * uv, axis=0, keepdims=True)
                dw_ref[2:3, cs] += jnp.sum(dc * uv, axis=0, keepdims=True)
                du_ref[:, cs] = (wc_ref[2:3, cs] * dc + wc_ref[1:2, cs] * f1 + wc_ref[0:1, cs] * f2).astype(BF16)
                nxt_ref[:, cs] = dc[0:8, :]

    rev = lambda i: (nt - 1 - i, 0)
    const2 = lambda i: (0, 0)
    return pl.pallas_call(
        body, name="ffn_bwd", grid=(nt,),
        in_specs=[pl.BlockSpec((ts, D), rev), pl.BlockSpec((ts, N_UP), rev), pl.BlockSpec((ts, N_UP), rev),
                  pl.BlockSpec((3, N_UP), const2), pl.BlockSpec((D_FF, D), const2)],
        out_specs=[pl.BlockSpec((ts, N_UP), rev), pl.BlockSpec((1, N_UP), const2), pl.BlockSpec((3, N_UP), const2)],
        out_shape=[jax.ShapeDtypeStruct((s, N_UP), BF16), jax.ShapeDtypeStruct((1, N_UP), F32),
                   jax.ShapeDtypeStruct((3, N_UP), F32)],
        scratch_shapes=[pltpu.VMEM((8, N_UP), F32)],
        compiler_params=_cp("arbitrary"),
    )(dx2b, u, c, wconv, wdown)


ANY = pl.BlockSpec(memory_space=pl.ANY)


def _place():
    x, y, c = lax.axis_index("x"), lax.axis_index("y"), lax.axis_index("c")
    chips = [(1 - x, y), (x, 1 - y), (1 - x, 1 - y)]
    return x, y, c, chips


def _half(shape, c, axis):
    size = shape[axis] // 2
    cut = pl.ds(pl.multiple_of(c * size, 8 if axis == 0 else 128), size)
    return (cut, slice(None)) if axis == 0 else (slice(None), cut)


def _half_shape(shape, axis):
    return (shape[0] // 2, shape[1]) if axis == 0 else (shape[0], shape[1] // 2)


def _remote(src, dst, send_sems, recv_sems, k, to):
    return pltpu.make_async_remote_copy(src_ref=src, dst_ref=dst, send_sem=send_sems.at[k], recv_sem=recv_sems.at[k],
                                        device_id=to, device_id_type=MESH)


def _sibling_exchange(grads, axes, smalls, name):
    nb = len(grads)
    n = nb + len(smalls)

    def body(*refs):
        ins, outs = refs[:n], refs[n:2 * n]
        send_sems, recv_sems = refs[2 * n:]
        x, y, c, _ = _place()
        sib = (x, y, 1 - c)
        cps = []
        for a in range(nb):
            theirs = _half(grads[a].shape[1:], 1 - c, axes[a])
            cps.append(_remote(ins[a].at[(slice(None),) + theirs], outs[a], send_sems, recv_sems, a, sib))
        for a in range(nb, n):
            cps.append(_remote(ins[a], outs[a], send_sems, recv_sems, a, sib))
        for cp in cps:
            cp.start()
        for cp in cps:
            cp.wait()

    out_shape = [jax.ShapeDtypeStruct((4,) + _half_shape(g.shape[1:], ax), g.dtype) for g, ax in zip(grads, axes)]
    out_shape += [jax.ShapeDtypeStruct(a.shape, F32) for a in smalls]
    return pl.pallas_call(
        body, name=name, in_specs=[ANY] * n, out_specs=[ANY] * n, out_shape=out_shape,
        scratch_shapes=[pltpu.SemaphoreType.DMA((n,)), pltpu.SemaphoreType.DMA((n,))],
        compiler_params=pltpu.CompilerParams(has_side_effects=True),
    )(*grads, *smalls)


def _gather_share(lands, axes, name):
    n = len(lands)

    def body(*refs):
        outs = refs[n:2 * n]
        send_sems, recv_sems = refs[2 * n:]
        x, y, c, chips = _place()
        sib = (x, y, 1 - c)
        cps = []
        for a in range(n):
            mine = _half(lands[a].shape[1:], c, axes[a])
            for k, ch in enumerate(chips):
                landed = outs[a].at[(2 * ch[0] + ch[1],) + mine]
                cps.append(_remote(landed, landed, send_sems, recv_sems, 3 * a + k, sib))
        for cp in cps:
            cp.start()
        for a in range(n):
            other = _half(lands[a].shape[1:], 1 - c, axes[a])
            for k, ch in enumerate(chips):
                landed = outs[a].at[(2 * ch[0] + ch[1],) + other]
                _remote(landed, landed, send_sems, recv_sems, 3 * a + k, sib).wait_recv()
        for cp in cps:
            cp.wait_send()

    return pl.pallas_call(
        body, name=name, in_specs=[ANY] * n, out_specs=[ANY] * n,
        out_shape=[jax.ShapeDtypeStruct(a.shape, a.dtype) for a in lands],
        input_output_aliases={a: a for a in range(n)},
        scratch_shapes=[pltpu.SemaphoreType.DMA((3 * n,)), pltpu.SemaphoreType.DMA((3 * n,))],
        compiler_params=pltpu.CompilerParams(has_side_effects=True),
    )(*lands)


def _sibling_share(halves, name):
    n = len(halves)

    def body(*refs):
        ins, outs = refs[:n], refs[n:2 * n]
        send_sems, recv_sems = refs[2 * n:]
        x, y, c, _ = _place()
        cps = [_remote(ins[a], outs[a], send_sems, recv_sems, a, (x, y, 1 - c)) for a in range(n)]
        for cp in cps:
            cp.start()
        for cp in cps:
            cp.wait()

    return pl.pallas_call(
        body, name=name, in_specs=[ANY] * n, out_specs=[ANY] * n,
        out_shape=[jax.ShapeDtypeStruct(h.shape, F32) for h in halves],
        scratch_shapes=[pltpu.SemaphoreType.DMA((n,)), pltpu.SemaphoreType.DMA((n,))],
        compiler_params=pltpu.CompilerParams(has_side_effects=True),
    )(*halves)


HBM = pl.BlockSpec(memory_space=pltpu.HBM)
SEM = pl.BlockSpec(memory_space=pltpu.SEMAPHORE)
DATAFLOW = pltpu.SideEffectType.DATAFLOW_SIDE_EFFECTING


def _split_start(name, srcs, land_shapes, plan, n_copies, after):
    lands = [lax.empty(*ls) if isinstance(ls, tuple) else ls for ls in land_shapes]
    bufs = list(srcs) + lands
    nb, ns = len(bufs), len(srcs)

    def body(*refs):
        send_sems, recv_sems, token = refs[nb + 1], refs[nb + 2], refs[-1]
        for k, (src, dst, to) in enumerate(plan(refs[:ns], refs[ns:nb])):
            _remote(src, dst, send_sems, recv_sems, k, to).start()
        token[...] = jnp.zeros_like(token)

    res = pl.pallas_call(
        body, name=name,
        out_shape=(pltpu.SemaphoreType.DMA((n_copies,)), pltpu.SemaphoreType.DMA((n_copies,)),
                   *[pltpu.HBM(b.shape, b.dtype) for b in bufs], jax.ShapeDtypeStruct((8, 128), F32)),
        in_specs=[HBM] * nb + [ANY],
        out_specs=(SEM, SEM, *[HBM] * nb, pl.BlockSpec(memory_space=pltpu.VMEM)),
        input_output_aliases={i: 2 + i for i in range(nb)},
        compiler_params=pltpu.CompilerParams(has_side_effects=DATAFLOW),
    )(*[pltpu.with_memory_space_constraint(b, pltpu.HBM) for b in bufs], after)
    return (res[0], res[1], list(res[2:2 + nb])), res[-1]


def _split_relay(name, handle, plan, relay_plan, n_relay, after):
    send_sems, recv_sems, bufs = handle
    nb = len(bufs)

    def body(*refs):
        sends, recvs = refs[nb], refs[nb + 1]
        for k, (src, dst, to) in enumerate(plan((), refs[:nb])):
            _remote(src, dst, sends, recvs, k, to).wait_recv()
        relay_sends, relay_recvs, token = refs[nb + 3], refs[nb + 4], refs[-1]
        for k, (src, dst, to) in enumerate(relay_plan((), refs[:nb])):
            _remote(src, dst, relay_sends, relay_recvs, k, to).start()
        token[...] = jnp.zeros_like(token)

    res = pl.pallas_call(
        body, name=name,
        out_shape=(pltpu.SemaphoreType.DMA((n_relay,)), pltpu.SemaphoreType.DMA((n_relay,)),
                   *[pltpu.HBM(b.shape, b.dtype) for b in bufs], jax.ShapeDtypeStruct((8, 128), F32)),
        in_specs=[HBM] * nb + [SEM, SEM, ANY],
        out_specs=(SEM, SEM, *[HBM] * nb, pl.BlockSpec(memory_space=pltpu.VMEM)),
        input_output_aliases={i: 2 + i for i in range(nb)},
        compiler_params=pltpu.CompilerParams(has_side_effects=DATAFLOW),
    )(*bufs, send_sems, recv_sems, after)
    passed = list(res[2:2 + nb])
    return (send_sems, recv_sems, passed), (res[0], res[1], passed), res[-1]


def _split_wait(name, handle, n_srcs, plan, after, arrived=False):
    send_sems, recv_sems, bufs = handle
    nb = len(bufs)

    def body(*refs):
        sends, recvs = refs[nb], refs[nb + 1]
        for k, (src, dst, to) in enumerate(plan(refs[:n_srcs], refs[n_srcs:nb])):
            cp = _remote(src, dst, sends, recvs, k, to)
            cp.wait_send()
            if arrived:
                continue
            cp.wait_recv()

    res = pl.pallas_call(
        body, name=name, out_shape=[pltpu.HBM(b.shape, b.dtype) for b in bufs],
        in_specs=[HBM] * nb + [SEM, SEM, ANY], out_specs=[HBM] * nb,
        input_output_aliases={i: i for i in range(nb)},
        compiler_params=pltpu.CompilerParams(has_side_effects=DATAFLOW),
    )(*bufs, send_sems, recv_sems, after)
    return list(res[:n_srcs]), list(res[n_srcs:])


def _relay_plan(shapes, axes):
    def plan(srcs, lands):
        x, y, c, _ = _place()
        first = c == 0
        from_x, from_y = jnp.where(first, 1 - x, x), jnp.where(first, y, 1 - y)
        to = (jnp.where(first, x, 1 - x), jnp.where(first, 1 - y, y), c)
        out = []
        for a, (shape, axis) in enumerate(zip(shapes, axes)):
            got = lands[a].at[(2 * from_x + from_y,) + _half(shape, c, axis)]
            out.append((got, got, to))
        return out
    return plan


def _gather_plan(shapes, axes, n_whole=0, relayed=False):
    def plan(srcs, lands):
        x, y, c, chips = _place()
        me = 2 * x + y
        out = []
        for a, (shape, axis) in enumerate(zip(shapes, axes)):
            own = lands[a].at[(me,) + _half(shape, c, axis)]
            for ch in chips[:2] if relayed else chips:
                out.append((own, own, (ch[0], ch[1], c)))
        for a in range(len(shapes), len(shapes) + n_whole):
            for ch in chips:
                out.append((lands[a].at[me], lands[a].at[me], (ch[0], ch[1], c)))
        return out
    return plan


def _share_plan(shapes, axes):
    def plan(srcs, lands):
        x, y, c, chips = _place()
        out = []
        for a, (shape, axis) in enumerate(zip(shapes, axes)):
            mine = _half(shape, c, axis)
            for ch in chips:
                landed = lands[a].at[(2 * ch[0] + ch[1],) + mine]
                out.append((landed, landed, (x, y, 1 - c)))
        return out
    return plan


def _sibling_plan(shapes, axes):
    def plan(srcs, lands):
        x, y, c, _ = _place()
        return [(srcs[a].at[(slice(None),) + _half(shape, 1 - c, axis)], lands[a], (x, y, 1 - c))
                for a, (shape, axis) in enumerate(zip(shapes, axes))]
    return plan


def _whole_to_sibling_plan(n):
    def plan(srcs, lands):
        x, y, c, _ = _place()
        return [(srcs[a], lands[a], (x, y, 1 - c)) for a in range(n)]
    return plan


def _reduce_plan(n_big, n_small):
    def plan(srcs, lands):
        x, y, c, chips = _place()
        out = []
        for a in range(n_big):
            for k, ch in enumerate(chips):
                out.append((srcs[a].at[2 * ch[0] + ch[1]], lands[a].at[k], (ch[0], ch[1], c)))
        for a in range(n_big, n_big + n_small):
            for ch in chips:
                out.append((srcs[a], lands[a].at[2 * x + y], (ch[0], ch[1], c)))
        return out
    return plan


def _row_tile(rows, cols, mult):
    best = mult
    for t in range(mult, rows + 1, mult):
        if rows % t == 0 and t * cols * 4 <= (2 << 20):
            best = t
    return best if rows % best == 0 else rows


COL_TILE = 256


def _half_tiling(hshape, axis, mult):
    hr, hc = hshape
    if axis == 0:
        tr = _row_tile(hr, hc, mult)
        return tr, hc, hr // tr
    return hr, COL_TILE, hc // COL_TILE


def _tile_idx(axis, t):
    return (t, 0) if axis == 0 else (0, t)


def _chip_partial(place, g, t, axis, name):
    hshape = t.shape[1:]
    br, bc, nt = _half_tiling(hshape, axis, 16)

    def body(pl_ref, g_ref, t_ref, pf_ref, pb_ref):
        v = g_ref[...].astype(F32) + t_ref[...].astype(F32)
        pb_ref[...] = v.astype(BF16)

        @pl.when(pl.program_id(1) == pl_ref[0])
        def _():
            pf_ref[...] = v

    blk = (None, br, bc)
    return pl.pallas_call(
        body, name=name,
        grid_spec=pltpu.PrefetchScalarGridSpec(
            num_scalar_prefetch=1, grid=(nt, 4),
            in_specs=[pl.BlockSpec(blk, lambda i, j, p: (j,) + _tile_idx(axis, p[1] * nt + i)),
                      pl.BlockSpec(blk, lambda i, j, p: (j,) + _tile_idx(axis, i))],
            out_specs=[pl.BlockSpec((br, bc), lambda i, j, p: _tile_idx(axis, i)),
                       pl.BlockSpec(blk, lambda i, j, p: (j,) + _tile_idx(axis, i))]),
        out_shape=[jax.ShapeDtypeStruct(hshape, F32), jax.ShapeDtypeStruct((4,) + hshape, BF16)],
        compiler_params=_cp("arbitrary", "arbitrary"),
    )(place, g, t)


def _finish_half(pf, rb, axis, name):
    hshape = pf.shape
    br, bc, nt = _half_tiling(hshape, axis, 16)

    def body(pf_ref, rb_ref, o_ref):
        o_ref[...] = ((pf_ref[...] + rb_ref[0].astype(F32)) + rb_ref[1].astype(F32)) + rb_ref[2].astype(F32)

    return pl.pallas_call(
        body, name=name, grid=(nt,),
        in_specs=[pl.BlockSpec((br, bc), lambda i: _tile_idx(axis, i)),
                  pl.BlockSpec((3, br, bc), lambda i: (0,) + _tile_idx(axis, i))],
        out_specs=pl.BlockSpec((br, bc), lambda i: _tile_idx(axis, i)),
        out_shape=jax.ShapeDtypeStruct(hshape, F32),
        compiler_params=_cp("arbitrary"),
    )(pf, rb)


def _adam_math(w, g, m, v):
    m = ADAM_B1 * m + (1.0 - ADAM_B1) * g
    v = ADAM_B2 * v + (1.0 - ADAM_B2) * (g * g)
    m_hat = m / (1.0 - ADAM_B1 ** ADAM_STEP)
    v_hat = v / (1.0 - ADAM_B2 ** ADAM_STEP)
    return -ADAM_LR * (m_hat / (jnp.sqrt(v_hat) + ADAM_EPS) + ADAM_WD * w), m, v


def _adam_halves(place, w, mine, theirs, m, v, axis, name):
    br, bc, nt = _half_tiling(mine.shape, axis, 8)

    def body(pl_ref, w_ref, a_ref, b_ref, m_ref, v_ref, g_ref, d_ref, mo_ref, vo_ref):
        is_mine = pl.program_id(0) // nt == pl_ref[1]
        g = jnp.where(is_mine, a_ref[...], b_ref[...])
        d, mn, vn = _adam_math(w_ref[...], g, m_ref[...], v_ref[...])
        g_ref[...] = g
        d_ref[...] = d
        mo_ref[...] = mn
        vo_ref[...] = vn

    full = pl.BlockSpec((br, bc), lambda i, p: _tile_idx(axis, i))
    mine_spec = pl.BlockSpec((br, bc), lambda i, p: _tile_idx(axis, jnp.where(i // nt == p[1], i % nt, nt - 1)))
    theirs_spec = pl.BlockSpec((br, bc), lambda i, p: _tile_idx(axis, jnp.where(i // nt == p[1], 0, i % nt)))
    return pl.pallas_call(
        body, name=name,
        grid_spec=pltpu.PrefetchScalarGridSpec(
            num_scalar_prefetch=1, grid=(2 * nt,), in_specs=[full, mine_spec, theirs_spec, full, full],
            out_specs=[full] * 4),
        out_shape=[jax.ShapeDtypeStruct(w.shape, F32)] * 4, compiler_params=_cp("arbitrary"),
    )(place, w, mine, theirs, m, v)


def _add_many(xs, ys, name):
    n = len(xs)

    def body(*refs):
        for i in range(n):
            refs[2 * n + i][...] = refs[i][...] + refs[n + i][...]

    return pl.pallas_call(body, name=name, out_shape=[jax.ShapeDtypeStruct(a.shape, F32) for a in xs])(*xs, *ys)


def _adam_small(place, owns, landed, ws, ms, vs, widths):
    n, nw = len(owns), len(ws)

    def body(pl_ref, *refs):
        own_r, land_r = refs[:n], refs[n:2 * n]
        w_r, m_r, v_r = (refs[2 * n + k * nw:2 * n + (k + 1) * nw] for k in range(3))
        outs = refs[2 * n + 3 * nw:]
        g_o, d_o, m_o, v_o = outs[:n], outs[n:n + nw], outs[n + nw:n + 2 * nw], outs[n + 2 * nw:]
        for me in range(4):
            @pl.when(pl_ref[0] == me)
            def _(me=me):
                for i in range(n):
                    p = [own_r[i][...] if k == me else land_r[i][k] for k in range(4)]
                    g = ((p[0] + p[1]) + p[2]) + p[3]
                    if i < nw and widths[i]:
                        g = g[:, me * widths[i]:(me + 1) * widths[i]]
                    g_o[i][...] = g
                    if i < nw:
                        d, mn, vn = _adam_math(w_r[i][...], g, m_r[i][...], v_r[i][...])
                        d_o[i][...] = d
                        m_o[i][...] = mn
                        v_o[i][...] = vn

    g_shapes = [jax.ShapeDtypeStruct(ws[i].shape if i < nw else owns[i].shape, F32) for i in range(n)]
    w_shapes = [jax.ShapeDtypeStruct(w.shape, F32) for w in ws]
    whole = lambda a: pl.BlockSpec(a.shape, lambda i, p, nd=len(a.shape): (0,) * nd)
    ins = list(owns) + list(landed) + list(ws) + list(ms) + list(vs)
    out_shape = g_shapes + w_shapes * 3
    out = pl.pallas_call(
        body, name="adam_small",
        grid_spec=pltpu.PrefetchScalarGridSpec(num_scalar_prefetch=1, grid=(1,), in_specs=[whole(a) for a in ins],
                                               out_specs=[whole(a) for a in out_shape]),
        out_shape=out_shape, compiler_params=_cp("arbitrary"),
    )(place, *ins)
    return out[:n], out[n:n + nw], out[n + nw:n + 2 * nw], out[n + 2 * nw:]


def kernel(x, g_mix, w_in, b_gate, w_gk_up, b_gk, w_pool_grp, pool_scale, g_gla_head, w_pool_proj, w_gla_proj, w_out, g_ffn, w_up, w_conv, b_conv, w_down, g_final, loss_target, m_g_mix, m_w_in, m_b_gate, m_w_gk_up, m_b_gk, m_w_pool_grp, m_pool_scale, m_g_gla_head, m_w_pool_proj, m_w_gla_proj, m_w_out, m_g_ffn, m_w_up, m_w_conv, m_b_conv, m_w_down, m_g_final, v_g_mix, v_w_in, v_b_gate, v_w_gk_up, v_b_gk, v_w_pool_grp, v_pool_scale, v_g_gla_head, v_w_pool_proj, v_w_gla_proj, v_w_out, v_g_ffn, v_w_up, v_w_conv, v_b_conv, v_w_down, v_g_final):
    s = x.shape[1]
    ts = min(s, 512)
    tm = min(s, 256)
    cx, cy, cc = lax.axis_index("x"), lax.axis_index("y"), lax.axis_index("c")
    chip = 2 * cx + cy
    place = jnp.stack([chip, cc]).astype(jnp.int32)

    big_names = ("w_in", "w_pool_proj", "w_gla_proj", "w_out", "w_up", "w_down")
    axes = (1, 0, 0, 0, 0, 0)
    shards = dict(w_in=jnp.transpose(w_in[0]), w_pool_proj=w_pool_proj[0], w_gla_proj=w_gla_proj[0], w_out=w_out[0],
                  w_up=w_up[0], w_down=w_down[0])
    def landing(own_shards):
        return [lax.dynamic_update_slice(lax.empty((4,) + o_.shape, o_.dtype), o_[None], (chip, 0, 0))
                for o_ in own_shards]

    def gather_start(tag, lands, n_halves, group_axes, after, relayed=False):
        n_whole = len(lands) - n_halves
        plan = _gather_plan([l_.shape[1:] for l_ in lands[:n_halves]], group_axes, n_whole, relayed)
        n_copies = (2 if relayed else 3) * n_halves + 3 * n_whole
        handle, token = _split_start("gather_" + tag + "_start", [], lands, plan, n_copies, after)
        return (handle, plan, n_halves, group_axes), token

    def gather_relay(tag, started, after):
        handle, plan, n_halves, group_axes = started
        relay_plan = _relay_plan([b_.shape[1:] for b_ in handle[2]], group_axes)
        first, relay, token = _split_relay("gather_" + tag + "_relay", handle, plan, relay_plan, n_halves, after)
        return (first, plan, relay, relay_plan, group_axes), token

    def gather_finish_relayed(tag, relayed, after):
        first, plan, relay, relay_plan, group_axes = relayed
        lands = _split_wait("gather_" + tag + "_sent", first, 0, plan, after, arrived=True)[1]
        lands = _split_wait("gather_" + tag + "_wait", (relay[0], relay[1], lands), 0, relay_plan, after)[1]
        return _gather_share(lands, group_axes, "gather_" + tag + "_share")

    in_w, tok = gather_start("in", landing([jnp.transpose(w_in[0].astype(BF16))]), 1, axes[:1], g_mix, relayed=True)
    zero = tok[0, 0]
    own = landing([(shards[n] + zero).astype(BF16) for n in big_names[1:]] + [w_gk_up[0] + zero, w_conv[0] + zero])
    in_r, tok = gather_relay("in", in_w, own[4])
    mix_w, tok = gather_start("mix", own[0:3] + own[5:7], 3, axes[1:4], tok)
    up_w, tok = gather_start("up", own[3:4], 1, axes[4:5], tok)
    down_w, tok = gather_start("down", own[4:5], 1, axes[5:6], tok)

    def forward_start(tag, started, after):
        handle, plan, n_halves, group_axes = started
        lands = _split_wait("gather_" + tag + "_wait", handle, 0, plan, after)[1]
        plan = _share_plan([l_.shape[1:] for l_ in lands[:n_halves]], group_axes)
        share, token = _split_start("gather_" + tag + "_share_start", [], lands[:n_halves], plan, 3 * n_halves, after)
        return (share, plan, lands[n_halves:]), token

    def forward_done(tag, forwarded, after):
        share, plan, _ = forwarded
        return _split_wait("gather_" + tag + "_share_wait", share, 0, plan, after)[1]
    xs, tgt = x[0], loss_target[0]
    wgrp = w_pool_grp[0]
    h = _rmsnorm(xs, g_mix, tok, "norm_mix", ts)
    m_in_t, v_in_t = jnp.transpose(m_w_in[0]), jnp.transpose(v_w_in[0])
    h, m_in_t, v_in_t = lax.optimization_barrier((h, m_in_t, v_in_t))
    w_in_t = gather_finish_relayed("in", in_r, h)[0]
    nsh = N_IN // 4

    zr, w_in_rt = _in_proj(h, w_in_t, PROJ_TILE)
    p, pp = _pool_fwd(zr, wgrp, pool_scale)
    mix_f, tok = forward_start("mix", mix_w, pp)
    wgk4, wconv4 = mix_f[2]
    wgk_full = jnp.transpose(wgk4, (1, 0, 2)).reshape(GATE_RANK, 512) + tok[0, 0]
    wconv_full = jnp.transpose(wconv4, (1, 0, 2)).reshape(3, N_UP)
    wgk_pad = jnp.concatenate([wgk_full, jnp.zeros((128 - GATE_RANK, 512), F32)], axis=0)
    o, og, sp = _gla_fwd(zr, wgk_pad, b_gk, g_gla_head, ts)
    wpp, wgla, wout = forward_done("mix", mix_f, og)
    wgla, wout = wgla.reshape(D, D), wout.reshape(D, D)
    up_f, tok = forward_start("up", up_w, og)
    x1, mixed, yp, yg, h2 = _merge_fwd(xs, zr, pp, og, b_gate, wpp, wgla, wout, g_ffn, tok, ts)
    wup, = forward_done("up", up_f, x1)
    down_f, tok = forward_start("down", down_w, x1)
    u = _matmul_resident(h2, wup, tok, "ffn_up")
    wdown = forward_done("down", down_f, u)[0].reshape(D_FF, D)
    a, conv_out, dx2, dx2b, loss_part, dgfin = _ffn_down_loss(u, x1, tgt, wconv_full, b_conv, wdown,
                                                              g_final.reshape(1, D), tm)

    du, dbconv, dwconv = _ffn_bwd(dx2b, u, conv_out, wconv_full, wdown, tm)
    dw_down = _matmul_tn(a, dx2b, "dw_down", D, tm=D_FF // 2)
    dw_up = _matmul_tn(h2, du, "dw_up", UP_SHARD, shard_major=True)

    def exchange_start(tag, grads, group_axes, after):
        plan = _sibling_plan([g.shape[1:] for g in grads], group_axes)
        lands = [((4,) + _half_shape(g.shape[1:], ax), g.dtype) for g, ax in zip(grads, group_axes)]
        handle, token = _split_start("sibling_" + tag + "_start", grads, lands, plan, len(grads), after)
        return (handle, plan, len(grads)), token

    def partials(tag, names, group_axes, exchange, after):
        handle, plan, n = exchange
        mine, theirs = _split_wait("sibling_" + tag + "_wait", handle, n, plan, after)
        return zip(*[_chip_partial(place, g, t, ax, "chip_partial_" + nm)
                     for nm, ax, g, t in zip(names, group_axes, mine, theirs)])

    ffn_names, ffn_axes = ("w_up", "w_down"), (0, 0)
    ffn_x, token = exchange_start("ffn", [dw_up, dw_down.reshape(4, 704, D)], ffn_axes, du)
    dx1, dx1b, dgffn = _matmul_nt_normbwd(du, wup, x1, g_ffn, dx2, token, "ffn_up_bwd", ts)
    ffn_pf, ffn_pb = partials("ffn", ffn_names, ffn_axes, ffn_x, dx1b)
    ffn_plan = _reduce_plan(2, 0)
    ffn_handle, token = _split_start("reduce_ffn_start", ffn_pb, [((3,) + p.shape[1:], BF16) for p in ffn_pb],
                                     ffn_plan, 6, ffn_pf[0])

    dzr, dyp, dyg, dpp, do, dzog, dbgate, dghead = _merge_bwd(dx1b, zr, yp, yg, o, b_gate, g_gla_head, wpp, wgla, wout,
                                                             token, ts)
    dzr = lax.dynamic_update_slice(dzr, dzog, (0, OFF_OG))
    dw_out = _matmul_tn(mixed, dx1b, "dw_out", D, tm=512)
    dw_gla = _matmul_tn(og, dyg, "dw_gla", D, tm=512)
    dw_pp = _matmul_tn(pp, dyp, "dw_pp", 256, shard_major=True)

    out_names, out_axes = ("w_pool_proj", "w_gla_proj", "w_out"), (0, 0, 0)
    out_x, token = exchange_start("out", [dw_pp, dw_gla.reshape(4, 256, D), dw_out.reshape(4, 256, D)], out_axes, dpp)
    dzr, dwgrp, dscale = _pool_bwd(p, dpp, wgrp, pool_scale, token, dzr)
    out_pf, out_pb = partials("out", out_names, out_axes, out_x, dwgrp)
    out_plan = _reduce_plan(3, 0)
    out_handle, token = _split_start("reduce_out_start", out_pb, [((3,) + p_.shape[1:], BF16) for p_ in out_pb],
                                     out_plan, 9, out_pf[0])
    dq, dk, dzr, dgpre = _gla_bwd(zr, do, sp, wgk_pad, b_gk, token, dzr, ts)
    dzr, dwgk, dbgk = _gk_bwd(dgpre, zr, wgk_pad, dgpre, dzr, ts)
    dzr = lax.dynamic_update_slice(lax.dynamic_update_slice(dzr, dq, (0, OFF_Q)), dk, (0, OFF_K))
    dw_rt = _matmul_tn(dzr, h, "dw_in", D, tm=PROJ_TILE)

    def grad_rows(lo, hi):
        out = []
        for seg_lo, seg_hi, at in ((0, 1536, OFF_POOL), (1536, 3584, OFF_V), (3584, 3600, OFF_GK), (3600, N_IN, OFF_GATE)):
            a_, b_ = max(lo, seg_lo), min(hi, seg_hi)
            if a_ < b_:
                out.append(dw_rt[at + a_ - seg_lo:at + b_ - seg_lo])
        return jnp.concatenate(out, axis=0)

    dw_in_t = jnp.stack([grad_rows(j * nsh, (j + 1) * nsh) for j in range(4)])

    ms = dict(w_in=m_in_t, w_pool_proj=m_w_pool_proj[0], w_gla_proj=m_w_gla_proj[0], w_out=m_w_out[0],
              w_up=m_w_up[0], w_down=m_w_down[0])
    vs = dict(w_in=v_in_t, w_pool_proj=v_w_pool_proj[0], w_gla_proj=v_w_gla_proj[0], w_out=v_w_out[0],
              w_up=v_w_up[0], w_down=v_w_down[0])
    grad, delta, new_m, new_v = {}, {}, {}, {}

    def finish(names, group_axes, part_f, landed):
        return [_finish_half(pf, rb, ax, "finish_" + n) for n, ax, pf, rb in zip(names, group_axes, part_f, landed)]

    def update(names, group_axes, halves, sib_halves):
        for n, ax, mine, theirs in zip(names, group_axes, halves, sib_halves):
            res = _adam_halves(place, shards[n], mine, theirs, ms[n], vs[n], ax, "adam_" + n)
            if n == "w_in":
                res = [jnp.transpose(r_) for r_ in res]
            grad[n], delta[n], new_m[n], new_v[n] = [r_[None] for r_ in res]

    rest_names, rest_axes = ffn_names + out_names, ffn_axes + out_axes
    in_x, token = exchange_start("in", [dw_in_t], (1,), dw_rt)
    _, ffn_landed = _split_wait("reduce_ffn_wait", ffn_handle, 2, ffn_plan, token)
    _, out_landed = _split_wait("reduce_out_wait", out_handle, 3, out_plan, ffn_landed[0])
    rest_halves = lax.optimization_barrier(finish(rest_names, rest_axes, ffn_pf + out_pf, ffn_landed + out_landed))
    (in_pf,), (in_pb,) = partials("in", ("w_in",), (1,), in_x, rest_halves[-1])
    in_plan = _reduce_plan(1, 0)
    in_handle, token = _split_start("reduce_in_start", [in_pb], [((3,) + in_pb.shape[1:], BF16)], in_plan, 3, in_pf)
    rest_plan = _whole_to_sibling_plan(len(rest_halves))
    rest_share, token = _split_start("sibling_share_rest_start", rest_halves, [(h_.shape, F32) for h_ in rest_halves],
                                     rest_plan, len(rest_halves), token)
    grad_x, _, dgmix = _matmul_nt_normbwd(dzr, w_in_rt, xs, g_mix, dx1, token, "in_proj_bwd", ts, transposed=True)
    small_names = ("g_mix", "b_gate", "w_gk_up", "b_gk", "w_pool_grp", "pool_scale", "g_gla_head", "g_ffn", "w_conv",
                   "b_conv", "g_final")
    small_mine = [dgmix, dbgate, dwgk[:GATE_RANK], dbgk, dwgrp.reshape(4 * 128, 128), dscale, dghead, dgffn, dwconv, dbconv,
                  dgfin, loss_part]
    small_sib = _sibling_exchange([], (), small_mine, "sibling_exchange_small")
    small_chip = _add_many(small_mine, small_sib, "chip_partial_small")
    small_plan = _reduce_plan(0, len(small_chip))
    small_handle, token = _split_start("reduce_small_start", small_chip, [((4,) + a_.shape, F32) for a_ in small_chip],
                                       small_plan, 3 * len(small_chip), small_mine[0])

    rest_halves, rest_sib = _split_wait("sibling_share_rest_wait", rest_share, len(rest_halves), rest_plan, token)
    n_ffn = len(ffn_names)
    update(out_names, out_axes, rest_halves[n_ffn:], rest_sib[n_ffn:])
    updated = lax.optimization_barrier([delta[n] for n in out_names])
    _, in_landed = _split_wait("reduce_in_wait", in_handle, 1, in_plan, updated[0])
    in_halves = finish(("w_in",), (1,), (in_pf,), in_landed)
    update(("w_in",), (1,), in_halves, _sibling_share(in_halves, "sibling_share_in"))
    ffn_halves, _ = lax.optimization_barrier((rest_halves[:n_ffn], delta["w_in"]))
    update(ffn_names, ffn_axes, ffn_halves, rest_sib[:n_ffn])
    small_sent, small_landed = _split_wait("reduce_small_wait", small_handle, len(small_chip), small_plan, delta["w_in"])
    given = dict(g_mix=(g_mix, m_g_mix, v_g_mix), b_gate=(b_gate, m_b_gate, v_b_gate), w_gk_up=(w_gk_up, m_w_gk_up, v_w_gk_up),
                 b_gk=(b_gk, m_b_gk, v_b_gk), w_pool_grp=(w_pool_grp, m_w_pool_grp, v_w_pool_grp),
                 pool_scale=(pool_scale, m_pool_scale, v_pool_scale), g_gla_head=(g_gla_head, m_g_gla_head, v_g_gla_head),
                 g_ffn=(g_ffn, m_g_ffn, v_g_ffn), w_conv=(w_conv, m_w_conv, v_w_conv), b_conv=(b_conv, m_b_conv, v_b_conv),
                 g_final=(g_final, m_g_final, v_g_final))
    flat2 = lambda a: a.reshape(-1, a.shape[-1])
    widths = [dict(w_gk_up=HK, w_conv=UP_SHARD).get(n) for n in small_names]
    totals, ds, mo, vo = _adam_small(place, small_sent, small_landed, *[[flat2(given[n][k]) for n in small_names] for k in range(3)],
                                     widths)
    loss = totals[-1][0, 0]
    for i, n in enumerate(small_names):
        shp = given[n][0].shape
        grad[n], delta[n], new_m[n], new_v[n] = [r_.reshape(shp) for r_ in (totals[i], ds[i], mo[i], vo[i])]

    order = ("g_mix", "w_in", "b_gate", "w_gk_up", "b_gk", "w_pool_grp", "pool_scale", "g_gla_head", "w_pool_proj",
             "w_gla_proj", "w_out", "g_ffn", "w_up", "w_conv", "b_conv", "w_down", "g_final")
    return (loss, grad_x[None], *[grad[n] for n in order], *[delta[n] for n in order], *[new_m[n] for n in order],
            *[new_v[n] for n in order])
```

```python
import jax
import jax.numpy as jnp
from jax import lax
from jax.experimental import pallas as pl
from jax.experimental.pallas import tpu as pltpu

F32 = jnp.float32
BF16 = jnp.bfloat16
MESH = pl.DeviceIdType.MESH

D = 1024
EPS = 1e-6
CHUNK = 64
POOL_W = 512
POOL_WINDOWS = (2, 4, 8, 16)
HEADS = 4
HK = 128
HV = 256
GATE_RANK = 16
D_FF = 2816
N_UP = 2 * D_FF
N_IN = 5648
QSCALE = HK ** -0.5
N_INR = 5760
OFF_GATE, OFF_V, OFF_OG, OFF_POOL, OFF_Q, OFF_K, OFF_GK = 0, 2048, 3072, 4096, 4608, 5120, 5632

ADAM_LR, ADAM_B1, ADAM_B2, ADAM_EPS, ADAM_WD, ADAM_STEP = 0.001, 0.9, 0.999, 1e-08, 0.01, 10

VMEM_LIMIT = 56 * 1024 * 1024
PROJ_TILE = N_INR // 5
UP_SHARD = N_UP // 4


def _cp(*sem):
    return pltpu.CompilerParams(dimension_semantics=sem if sem else None, vmem_limit_bytes=VMEM_LIMIT)


def _dot(a, b):
    return jnp.dot(a, b, preferred_element_type=F32)


def _dot_nt(a, b):
    return lax.dot_general(a, b, (((1,), (1,)), ((), ())), preferred_element_type=F32)


def _dot_tn(a, b):
    return lax.dot_general(a, b, (((0,), (0,)), ((), ())), preferred_element_type=F32)


def _sigmoid(v):
    return 1.0 / (1.0 + jnp.exp(-v))


def _rows(shape):
    return lax.broadcasted_iota(jnp.int32, shape, 0)


def _pick_row(v, r):
    return jnp.sum(jnp.where(_rows(v.shape) == r, v, 0.0), axis=0, keepdims=True)


def _rmsnorm(x, g, after, name, ts):
    s = x.shape[0]

    def body(x_ref, g_ref, after_ref, h_ref):
        xv = x_ref[...]
        r = lax.rsqrt(jnp.mean(xv * xv, axis=-1, keepdims=True) + EPS)
        h_ref[...] = (xv * r * g_ref[...]).astype(BF16)

    return pl.pallas_call(
        body, name=name, grid=(s // ts,),
        in_specs=[pl.BlockSpec((ts, D), lambda i: (i, 0)), pl.BlockSpec((1, D), lambda i: (0, 0)), ANY],
        out_specs=pl.BlockSpec((ts, D), lambda i: (i, 0)), out_shape=jax.ShapeDtypeStruct((s, D), BF16),
        compiler_params=_cp("arbitrary"),
    )(x, g, after)


MM_ROWS = 512


def _matmul_resident(h, w, after, name):
    s = h.shape[0]
    nj, tn = w.shape[0], w.shape[2]
    rc = min(s, MM_ROWS)

    def body(h_ref, w_ref, after_ref, z_ref):
        for r0 in range(0, s, rc):
            z_ref[r0:r0 + rc, :] = _dot(h_ref[r0:r0 + rc, :], w_ref[...]).astype(BF16)

    return pl.pallas_call(
        body, name=name, grid=(nj,),
        in_specs=[pl.BlockSpec((s, D), lambda j: (0, 0)), pl.BlockSpec((None, D, tn), lambda j: (j, 0, 0)), ANY],
        out_specs=pl.BlockSpec((s, tn), lambda j: (0, j)), out_shape=jax.ShapeDtypeStruct((s, nj * tn), BF16),
        compiler_params=_cp("arbitrary"),
    )(h, w, after)


PROJ_PIECES = ((3600, 2048, OFF_GATE), (1536, 2048, OFF_V), (0, 1536, OFF_POOL), (3584, GATE_RANK, OFF_GK))


def _split_by_shard(pieces, rows_per_shard):
    out = []
    for src, n, dst in pieces:
        while n > 0:
            j, r = divmod(src, rows_per_shard)
            m = min(n, rows_per_shard - r)
            out.append((j, r, m, dst))
            src, n, dst = src + m, n - m, dst + m
    return tuple(out)


PROJ_SEGMENTS = _split_by_shard(PROJ_PIECES, N_IN // 4)


def _in_proj(h, w4, tn):
    s = h.shape[0]
    rc = min(s, MM_ROWS)
    nj = N_INR // tn
    first_use = [dst // tn for _, _, _, dst in PROJ_SEGMENTS]

    def body(h_ref, w_hbm, z_ref, wo_hbm, w_ref, stage, sems, out_sem):
        j = pl.program_id(0)
        cps = [pltpu.make_async_copy(w_hbm.at[k], stage.at[k], sems.at[k]) for k in range(4)]
        out_cp = pltpu.make_async_copy(w_ref, wo_hbm, out_sem.at[0])

        @pl.when(j == 0)
        def _():
            for cp in cps:
                cp.start()
            w_ref[OFF_GK + GATE_RANK:, :] = jnp.zeros((N_INR - OFF_GK - GATE_RANK, D), BF16)

        landed = set()
        for step in range(nj):
            due = [seg for seg, at in zip(PROJ_SEGMENTS, first_use) if at == step]
            if due:
                fresh = sorted({seg[0] for seg in due} - landed)
                landed.update(fresh)

                @pl.when(j == step)
                def _(due=due, fresh=fresh, last=step == max(first_use)):
                    for k in fresh:
                        cps[k].wait()
                    for k, r, n, dst in due:
                        w_ref[dst:dst + n, :] = stage[k, r:r + n, :]
                    if last:
                        out_cp.start()

        wt = w_ref[pl.ds(pl.multiple_of(j * tn, 128), tn), :]
        for r0 in range(0, s, rc):
            z_ref[r0:r0 + rc, :] = _dot_nt(h_ref[r0:r0 + rc, :], wt).astype(BF16)

        @pl.when(j == nj - 1)
        def _():
            out_cp.wait()

    return pl.pallas_call(
        body, name="in_proj", grid=(nj,),
        in_specs=[pl.BlockSpec((s, D), lambda j: (0, 0)), ANY],
        out_specs=[pl.BlockSpec((s, tn), lambda j: (0, j)), ANY],
        out_shape=[jax.ShapeDtypeStruct((s, N_INR), BF16), jax.ShapeDtypeStruct((N_INR, D), BF16)],
        scratch_shapes=[pltpu.VMEM((N_INR, D), BF16), pltpu.VMEM(w4.shape, BF16), pltpu.SemaphoreType.DMA((4,)),
                        pltpu.SemaphoreType.DMA((1,))],
        compiler_params=_cp("arbitrary"),
    )(h, w4)


def _matmul_nt_normbwd(dz, w, x, g, resid, after, name, ts, transposed=False):
    s = x.shape[0]
    w_vmem = w.shape if transposed else (D, w.shape[0] * w.shape[2])
    n_sems = 1 if transposed else w.shape[0]

    def body(dz_ref, w_hbm, x_ref, g_ref, r_ref, after_ref, o_ref, ob_ref, dg_ref, w_ref, sems):
        @pl.when(pl.program_id(0) == 0)
        def _():
            if transposed:
                cps = [pltpu.make_async_copy(w_hbm, w_ref, sems.at[0])]
            else:
                kc = w.shape[2]
                cps = [pltpu.make_async_copy(w_hbm.at[j], w_ref.at[:, pl.ds(j * kc, kc)], sems.at[j])
                       for j in range(w.shape[0])]
            for cp in cps:
                cp.start()
            for cp in cps:
                cp.wait()
            dg_ref[...] = jnp.zeros_like(dg_ref)

        dh = _dot(dz_ref[...], w_ref[...]) if transposed else _dot_nt(dz_ref[...], w_ref[...])
        xv = x_ref[...]
        r = lax.rsqrt(jnp.mean(xv * xv, axis=-1, keepdims=True) + EPS)
        xh = xv * r
        dg_ref[...] += jnp.sum(dh * xh, axis=0, keepdims=True)
        dxh = dh * g_ref[...]
        out = r_ref[...] + r * (dxh - xh * jnp.mean(dxh * xh, axis=-1, keepdims=True))
        o_ref[...] = out
        ob_ref[...] = out.astype(BF16)

    row = lambda i: (i, 0)
    kdim = dz.shape[1]
    return pl.pallas_call(
        body, name=name, grid=(s // ts,),
        in_specs=[pl.BlockSpec((ts, kdim), row), ANY, pl.BlockSpec((ts, D), row),
                  pl.BlockSpec((1, D), lambda i: (0, 0)), pl.BlockSpec((ts, D), row), ANY],
        out_specs=[pl.BlockSpec((ts, D), row), pl.BlockSpec((ts, D), row), pl.BlockSpec((1, D), lambda i: (0, 0))],
        out_shape=[jax.ShapeDtypeStruct((s, D), F32), jax.ShapeDtypeStruct((s, D), BF16),
                   jax.ShapeDtypeStruct((1, D), F32)],
        scratch_shapes=[pltpu.VMEM(w_vmem, BF16), pltpu.SemaphoreType.DMA((n_sems,))],
        compiler_params=_cp("arbitrary"),
    )(dz, w, x, g, resid, after)


def _matmul_tn(a, b, name, tn, shard_major=False, tm=None):
    s, m = a.shape
    n = b.shape[1]
    tm = m if tm is None else tm
    ni, nj = m // tm, n // tn

    def body(a_ref, b_ref, o_ref):
        o_ref[...] = _dot_tn(a_ref[...], b_ref[...]).astype(BF16)

    if shard_major:
        out_spec = pl.BlockSpec((None, tm, tn), lambda i, j: (j, i, 0))
        out_shape = jax.ShapeDtypeStruct((nj, m, tn), BF16)
    else:
        out_spec = pl.BlockSpec((tm, tn), lambda i, j: (i, j))
        out_shape = jax.ShapeDtypeStruct((m, n), BF16)
    return pl.pallas_call(
        body, name=name, grid=(ni, nj),
        in_specs=[pl.BlockSpec((s, tm), lambda i, j: (0, i)), pl.BlockSpec((s, tn), lambda i, j: (0, j))],
        out_specs=out_spec, out_shape=out_shape,
        compiler_params=_cp("arbitrary", "arbitrary"),
    )(a, b)


def _pool_fwd(zr, wgrp, scale):
    s = zr.shape[0]

    def body(u_ref, w_ref, sc_ref, p_ref, pp_ref):
        row = _rows((s, 128))
        for gi, win in enumerate(POOL_WINDOWS):
            cs = slice(gi * 128, (gi + 1) * 128)
            u = u_ref[:, cs].astype(F32)
            acc, k = u, 1
            while k < win:
                acc = acc + jnp.where(row >= k, pltpu.roll(acc, k, 0), 0.0)
                k *= 2
            cnt = jnp.minimum(row + 1, win).astype(F32)
            p = (acc / cnt - u).astype(BF16)
            p_ref[:, cs] = p
            pp_ref[:, cs] = (_dot(p, w_ref[gi].astype(BF16)) * sc_ref[:, cs]).astype(BF16)

    return pl.pallas_call(
        body, name="pool_fwd", grid=(1,),
        in_specs=[pl.BlockSpec((s, POOL_W), lambda i: (0, OFF_POOL // POOL_W)),
                  pl.BlockSpec((4, 128, 128), lambda i: (0, 0, 0)), pl.BlockSpec((1, POOL_W), lambda i: (0, 0))],
        out_specs=[pl.BlockSpec((s, POOL_W), lambda i: (0, 0))] * 2,
        out_shape=[jax.ShapeDtypeStruct((s, POOL_W), BF16)] * 2,
        compiler_params=_cp("arbitrary"),
    )(zr, wgrp, scale)


def _pool_bwd(p, dpp, wgrp, scale, after, dz):
    s = p.shape[0]

    def body(p_ref, dpp_ref, w_ref, sc_ref, after_ref, dz_in, dz_ref, dw_ref, dsc_ref):
        row = _rows((s, 128))
        for gi, win in enumerate(POOL_WINDOWS):
            cs = slice(gi * 128, (gi + 1) * 128)
            pv = p_ref[:, cs]
            wb = w_ref[gi].astype(BF16)
            dpp_v = dpp_ref[:, cs].astype(F32)
            dsc_ref[:, cs] = jnp.sum(dpp_v * _dot(pv, wb), axis=0, keepdims=True)
            dpm = (dpp_v * sc_ref[:, cs]).astype(BF16)
            dw_ref[gi] = _dot_tn(pv, dpm)
            dp = _dot_nt(dpm, wb)
            cnt = jnp.minimum(row + 1, win).astype(F32)
            acc, k = dp / cnt, 1
            while k < win:
                acc = acc + jnp.where(row < s - k, pltpu.roll(acc, s - k, 0), 0.0)
                k *= 2
            dz_ref[:, cs] = (acc - dp).astype(BF16)

    full = lambda i: (0, 0)
    return pl.pallas_call(
        body, name="pool_bwd", grid=(1,),
        in_specs=[pl.BlockSpec((s, POOL_W), full), pl.BlockSpec((s, POOL_W), full),
                  pl.BlockSpec((4, 128, 128), lambda i: (0, 0, 0)), pl.BlockSpec((1, POOL_W), full), ANY, ANY],
        out_specs=[pl.BlockSpec((s, POOL_W), lambda i: (0, OFF_POOL // POOL_W)),
                   pl.BlockSpec((4, 128, 128), lambda i: (0, 0, 0)), pl.BlockSpec((1, POOL_W), full)],
        out_shape=[jax.ShapeDtypeStruct(dz.shape, BF16), jax.ShapeDtypeStruct((4, 128, 128), F32),
                   jax.ShapeDtypeStruct((1, POOL_W), F32)],
        input_output_aliases={5: 0},
        compiler_params=_cp("arbitrary"),
    )(p, dpp, wgrp, scale, after, dz)


def _gla_decay(zgk_ref, wgk_ref, bgk_ref, rb):
    g = _dot(zgk_ref[...], wgk_ref[...].astype(BF16)) + bgk_ref[...]
    la = (jnp.minimum(g, 0.0) - jnp.log(1.0 + jnp.exp(-jnp.abs(g)))) * (1.0 / 16.0)
    rowm = _rows(la.shape) & (CHUNK - 1)
    bc, k = la, 1
    while k < CHUNK:
        bc = bc + jnp.where(rowm >= k, pltpu.roll(bc, k, 0), 0.0)
        k *= 2
    return g, jnp.exp(bc), jnp.exp(-bc)


GLA_HB = 4


def _gla_specs(rb, rmap):
    wk, wv = GLA_HB * HK, GLA_HB * HV
    return [pl.BlockSpec((rb, wk), lambda h, r: (rmap(h, r), OFF_Q // wk + h)),
            pl.BlockSpec((rb, wk), lambda h, r: (rmap(h, r), OFF_K // wk + h)),
            pl.BlockSpec((rb, wv), lambda h, r: (rmap(h, r), OFF_V // wv + h)),
            pl.BlockSpec((rb, 128), lambda h, r: (rmap(h, r), OFF_GK // 128))]


def _gla_fwd(zr, wgk, bgk, ghead, rb):
    s = zr.shape[0]
    nc = rb // CHUNK
    wk, wv = GLA_HB * HK, GLA_HB * HV

    def body(q_ref, k_ref, v_ref, zgk_ref, zog_ref, wgk_ref, bgk_ref, gh_ref, o_ref, og_ref, sp_ref, st_ref, kv_ref):
        @pl.when(pl.program_id(1) == 0)
        def _():
            st_ref[...] = jnp.zeros_like(st_ref)

        _, e_pos, e_neg = _gla_decay(zgk_ref, wgk_ref, bgk_ref, rb)
        lower = _rows((CHUNK, CHUNK)) >= lax.broadcasted_iota(jnp.int32, (CHUNK, CHUNK), 1)
        pairs = [(c, hh) for c in range(nc) for hh in range(GLA_HB)]
        rows = lambda c: slice(c * CHUNK, (c + 1) * CHUNK)
        cols_k = lambda hh: slice(hh * HK, (hh + 1) * HK)
        cols_v = lambda hh: slice(hh * HV, (hh + 1) * HV)
        qfws, pms, e_lasts = {}, {}, {}
        for c, hh in pairs:
            q = q_ref[rows(c), cols_k(hh)].astype(F32) * QSCALE
            k = k_ref[rows(c), cols_k(hh)].astype(F32)
            ec, fc = e_pos[rows(c), cols_k(hh)], e_neg[rows(c), cols_k(hh)]
            qfw = (q * ec).astype(BF16)
            kfw_f = k * fc
            s_fw = _dot_nt(qfw, kfw_f.astype(BF16))
            s_bw = _dot_nt((q * fc).astype(BF16), (k * ec).astype(BF16))
            e_last = _pick_row(ec, CHUNK - 1)
            kv_ref[c, hh] = _dot_tn(v_ref[rows(c), cols_v(hh)], (kfw_f * e_last).astype(BF16))
            qfws[c, hh], pms[c, hh], e_lasts[c, hh] = qfw, jnp.where(lower, s_fw, s_bw).astype(BF16), e_last
        for hh in range(GLA_HB):
            st = st_ref[hh]
            for c in range(nc):
                sp_ref[c, hh] = st.astype(BF16)
                st = st * e_lasts[c, hh] + kv_ref[c, hh]
            st_ref[hh] = st
        for c, hh in pairs:
            o = _dot(pms[c, hh], v_ref[rows(c), cols_v(hh)]) + _dot_nt(qfws[c, hh], sp_ref[c, hh])
            r = lax.rsqrt(jnp.mean(o * o, axis=-1, keepdims=True) + EPS)
            zo = zog_ref[rows(c), cols_v(hh)].astype(F32)
            o_ref[rows(c), cols_v(hh)] = o.astype(BF16)
            og_ref[rows(c), cols_v(hh)] = (o * r * gh_ref[...] * zo * _sigmoid(zo)).astype(BF16)

    rmap = lambda h, r: r
    return pl.pallas_call(
        body, name="gla_fwd", grid=(HEADS // GLA_HB, s // rb),
        in_specs=_gla_specs(rb, rmap) + [
            pl.BlockSpec((rb, wv), lambda h, r: (r, OFF_OG // wv + h)),
            pl.BlockSpec((128, wk), lambda h, r: (0, h)), pl.BlockSpec((1, wk), lambda h, r: (0, h)),
            pl.BlockSpec((1, HV), lambda h, r: (0, 0))],
        out_specs=[pl.BlockSpec((rb, wv), lambda h, r: (r, h)), pl.BlockSpec((rb, wv), lambda h, r: (r, h)),
                   pl.BlockSpec((nc, GLA_HB, HV, HK), lambda h, r: (r, h, 0, 0))],
        out_shape=[jax.ShapeDtypeStruct((s, D), BF16), jax.ShapeDtypeStruct((s, D), BF16),
                   jax.ShapeDtypeStruct((s // CHUNK, HEADS, HV, HK), BF16)],
        scratch_shapes=[pltpu.VMEM((GLA_HB, HV, HK), F32), pltpu.VMEM((nc, GLA_HB, HV, HK), F32)],
        compiler_params=_cp("arbitrary", "arbitrary"),
    )(zr, zr, zr, zr, zr, wgk, bgk, ghead)


def _gla_bwd(zr, do, sp, wgk, bgk, after, dz, rb):
    s = zr.shape[0]
    nc = rb // CHUNK
    nr = s // rb
    wk, wv = GLA_HB * HK, GLA_HB * HV

    def body(q_ref, k_ref, v_ref, zgk_ref, do_ref, sp_ref, wgk_ref, bgk_ref, after_ref, dz_in, dq_ref, dk_ref, dv_ref,
             dg_ref, gt_ref, dbc_ref, gs_ref):
        @pl.when(pl.program_id(1) == 0)
        def _():
            gt_ref[...] = jnp.zeros_like(gt_ref)

        g, e_pos, e_neg = _gla_decay(zgk_ref, wgk_ref, bgk_ref, rb)
        lower = _rows((CHUNK, CHUNK)) >= lax.broadcasted_iota(jnp.int32, (CHUNK, CHUNK), 1)
        is_last = _rows((CHUNK, HK)) == CHUNK - 1
        pairs = [(c, hh) for c in range(nc) for hh in range(GLA_HB)]
        rows = lambda c: slice(c * CHUNK, (c + 1) * CHUNK)
        cols_k = lambda hh: slice(hh * HK, (hh + 1) * HK)
        cols_v = lambda hh: slice(hh * HV, (hh + 1) * HV)
        e_lasts = {}
        for c, hh in pairs:
            ec = e_pos[rows(c), cols_k(hh)]
            qfw = (q_ref[rows(c), cols_k(hh)].astype(F32) * QSCALE * ec).astype(BF16)
            gs_ref[c, hh] = _dot_tn(do_ref[rows(c), cols_v(hh)], qfw)
            e_lasts[c, hh] = _pick_row(ec, CHUNK - 1)
        for hh in range(GLA_HB):
            gt = gt_ref[hh]
            for c in reversed(range(nc)):
                own = gs_ref[c, hh]
                gs_ref[c, hh] = gt
                gt = own + gt * e_lasts[c, hh]
            gt_ref[hh] = gt
        def decayed(c, hh):
            q = q_ref[rows(c), cols_k(hh)].astype(F32) * QSCALE
            k = k_ref[rows(c), cols_k(hh)].astype(F32)
            ec, fc = e_pos[rows(c), cols_k(hh)], e_neg[rows(c), cols_k(hh)]
            return ec, fc, q * ec, k * fc, q * fc, k * ec

        pms, dss = {}, {}
        for c, hh in pairs:
            _, _, qfw_f, kfw_f, qbw_f, kbw_f = decayed(c, hh)
            s_fw = _dot_nt(qfw_f.astype(BF16), kfw_f.astype(BF16))
            s_bw = _dot_nt(qbw_f.astype(BF16), kbw_f.astype(BF16))
            dp = _dot_nt(do_ref[rows(c), cols_v(hh)], v_ref[rows(c), cols_v(hh)])
            pms[c, hh] = jnp.where(lower, s_fw, s_bw).astype(BF16)
            dss[c, hh] = (jnp.where(lower, dp, 0.0).astype(BF16), jnp.where(lower, 0.0, dp).astype(BF16))
        for c, hh in pairs:
            sl, ck, cv = rows(c), cols_k(hh), cols_v(hh)
            v = v_ref[sl, cv]
            dov = do_ref[sl, cv]
            ec, fc, qfw_f, kfw_f, qbw_f, kbw_f = decayed(c, hh)
            qfw, kfw, qbw, kbw = qfw_f.astype(BF16), kfw_f.astype(BF16), qbw_f.astype(BF16), kbw_f.astype(BF16)
            pm = pms[c, hh]
            e_last = e_lasts[c, hh]
            kdec = (kfw_f * e_last).astype(BF16)
            gt = gs_ref[c, hh]
            gtb = gt.astype(BF16)
            spv = sp_ref[c, hh]
            dv_ref[sl, cv] = (_dot_tn(pm, dov) + _dot_nt(kdec, gtb)).astype(BF16)
            ds_fw, ds_bw = dss[c, hh]
            dqfw = _dot(ds_fw, kfw) + _dot(dov, spv)
            dkfw = _dot_tn(ds_fw, qfw)
            dqbw = _dot(ds_bw, kbw)
            dkbw = _dot_tn(ds_bw, qbw)
            dkdec = _dot(v, gtb)
            de_last = (jnp.sum(gt * spv.astype(F32), axis=0, keepdims=True)
                       + jnp.sum(dkdec * kfw_f, axis=0, keepdims=True))
            dkfw = dkfw + dkdec * e_last
            dq_ref[sl, ck] = ((dqfw * ec + dqbw * fc) * QSCALE).astype(BF16)
            dk_ref[sl, ck] = (dkfw * fc + dkbw * ec).astype(BF16)
            dbc = dqfw * qfw_f - dqbw * qbw_f + dkbw * kbw_f - dkfw * kfw_f
            dbc_ref[sl, ck] = dbc + jnp.where(is_last, de_last * e_last, 0.0)
        rowm = _rows((rb, wk)) & (CHUNK - 1)
        dla, kk = dbc_ref[...], 1
        while kk < CHUNK:
            dla = dla + jnp.where(rowm < CHUNK - kk, pltpu.roll(dla, rb - kk, 0), 0.0)
            kk *= 2
        dg_ref[...] = dla * (1.0 / 16.0) * _sigmoid(-g)

    rmap = lambda h, r: nr - 1 - r
    rev = lambda h, r: (nr - 1 - r, h)
    return pl.pallas_call(
        body, name="gla_bwd", grid=(HEADS // GLA_HB, nr),
        in_specs=_gla_specs(rb, rmap) + [
            pl.BlockSpec((rb, wv), rev),
            pl.BlockSpec((nc, GLA_HB, HV, HK), lambda h, r: (nr - 1 - r, h, 0, 0)),
            pl.BlockSpec((128, wk), lambda h, r: (0, h)), pl.BlockSpec((1, wk), lambda h, r: (0, h)), ANY, ANY],
        out_specs=[pl.BlockSpec((rb, wk), rev), pl.BlockSpec((rb, wk), rev),
                   pl.BlockSpec((rb, wv), lambda h, r: (nr - 1 - r, OFF_V // wv + h)), pl.BlockSpec((rb, wk), rev)],
        out_shape=[jax.ShapeDtypeStruct((s, HEADS * HK), BF16), jax.ShapeDtypeStruct((s, HEADS * HK), BF16),
                   jax.ShapeDtypeStruct(dz.shape, BF16), jax.ShapeDtypeStruct((s, HEADS * HK), F32)],
        scratch_shapes=[pltpu.VMEM((GLA_HB, HV, HK), F32), pltpu.VMEM((rb, wk), F32),
                        pltpu.VMEM((nc, GLA_HB, HV, HK), F32)],
        input_output_aliases={9: 2},
        compiler_params=_cp("arbitrary", "arbitrary"),
    )(zr, zr, zr, zr, do, sp, wgk, bgk, after, dz)


def _gk_bwd(dgpre, zr, wgk, after, dz, ts):
    s = zr.shape[0]

    def body(dg_ref, zgk_ref, w_ref, after_ref, dz_in, dz_ref, dw_ref, db_ref):
        @pl.when(pl.program_id(0) == 0)
        def _():
            dw_ref[...] = jnp.zeros_like(dw_ref)
            db_ref[...] = jnp.zeros_like(db_ref)

        dg = dg_ref[...]
        dgb = dg.astype(BF16)
        dz_ref[...] = _dot_nt(dgb, w_ref[...].astype(BF16)).astype(BF16)
        dw_ref[...] += _dot_tn(zgk_ref[...], dgb)
        db_ref[...] += jnp.sum(dg, axis=0, keepdims=True)

    return pl.pallas_call(
        body, name="gk_bwd", grid=(s // ts,),
        in_specs=[pl.BlockSpec((ts, 512), lambda i: (i, 0)), pl.BlockSpec((ts, 128), lambda i: (i, OFF_GK // 128)),
                  pl.BlockSpec((128, 512), lambda i: (0, 0)), ANY, ANY],
        out_specs=[pl.BlockSpec((ts, 128), lambda i: (i, OFF_GK // 128)), pl.BlockSpec((128, 512), lambda i: (0, 0)),
                   pl.BlockSpec((1, 512), lambda i: (0, 0))],
        out_shape=[jax.ShapeDtypeStruct(dz.shape, BF16), jax.ShapeDtypeStruct((128, 512), F32),
                   jax.ShapeDtypeStruct((1, 512), F32)],
        input_output_aliases={4: 0},
        compiler_params=_cp("arbitrary"),
    )(dgpre, zr, wgk, after, dz)


def _merge_fwd(x, zr, pp, og, bgate, wpp, wgla, wout, gffn, after, ts):
    s = x.shape[0]

    def body(x_ref, z0_ref, z1_ref, pp_ref, og_ref, bg_ref, wpp_ref, wgla_ref, wout_ref, gf_ref, after_ref,
             x1_ref, mix_ref, yp_ref, yg_ref, h2_ref):
        ppv = pp_ref[...]
        yp = jnp.concatenate([_dot(ppv, wpp_ref[j]) for j in range(4)], axis=1)
        yg = _dot(og_ref[...], wgla_ref[...])
        g0 = _sigmoid(z0_ref[...].astype(F32) + bg_ref[:, :D])
        g1 = _sigmoid(z1_ref[...].astype(F32) + bg_ref[:, D:])
        mixed = (g0 * yp + g1 * yg).astype(BF16)
        x1 = x_ref[...] + _dot(mixed, wout_ref[...])
        x1_ref[...] = x1
        mix_ref[...] = mixed
        yp_ref[...] = yp.astype(BF16)
        yg_ref[...] = yg.astype(BF16)
        r = lax.rsqrt(jnp.mean(x1 * x1, axis=-1, keepdims=True) + EPS)
        h2_ref[...] = (x1 * r * gf_ref[...]).astype(BF16)

    row = lambda i: (i, 0)
    const2 = lambda i: (0, 0)
    return pl.pallas_call(
        body, name="merge_fwd", grid=(s // ts,),
        in_specs=[pl.BlockSpec((ts, D), row), pl.BlockSpec((ts, D), lambda i: (i, 0)), pl.BlockSpec((ts, D), lambda i: (i, 1)),
                  pl.BlockSpec((ts, POOL_W), row), pl.BlockSpec((ts, D), row), pl.BlockSpec((1, 2 * D), const2),
                  pl.BlockSpec((4, POOL_W, 256), lambda i: (0, 0, 0)), pl.BlockSpec((D, D), const2),
                  pl.BlockSpec((D, D), const2), pl.BlockSpec((1, D), const2), ANY],
        out_specs=[pl.BlockSpec((ts, D), row)] * 5,
        out_shape=[jax.ShapeDtypeStruct((s, D), F32)] + [jax.ShapeDtypeStruct((s, D), BF16)] * 4,
        compiler_params=_cp("arbitrary"),
    )(x, zr, zr, pp, og, bgate, wpp, wgla, wout, gffn, after)


def _merge_bwd(dx1b, zr, yp, yg, o, bgate, ghead, wpp, wgla, wout, after, ts):
    s = dx1b.shape[0]

    def body(dx_ref, z0_ref, z1_ref, zog_ref, yp_ref, yg_ref, o_ref, bg_ref, gh_ref, wpp_ref, wgla_ref, wout_ref, after_ref,
             dzg_ref, dyp_ref, dyg_ref, dpp_ref, do_ref, dzog_ref, dbg_ref, dgh_ref):
        @pl.when(pl.program_id(0) == 0)
        def _():
            dbg_ref[...] = jnp.zeros_like(dbg_ref)
            dgh_ref[...] = jnp.zeros_like(dgh_ref)

        dmix = _dot_nt(dx_ref[...], wout_ref[...])
        g0 = _sigmoid(z0_ref[...].astype(F32) + bg_ref[:, :D])
        g1 = _sigmoid(z1_ref[...].astype(F32) + bg_ref[:, D:])
        dypb = (dmix * g0).astype(BF16)
        dygb = (dmix * g1).astype(BF16)
        dz0 = dmix * yp_ref[...].astype(F32) * g0 * (1.0 - g0)
        dz1 = dmix * yg_ref[...].astype(F32) * g1 * (1.0 - g1)
        dzg_ref[:, :D] = dz0.astype(BF16)
        dzg_ref[:, D:] = dz1.astype(BF16)
        dbg_ref[:, :D] += jnp.sum(dz0, axis=0, keepdims=True)
        dbg_ref[:, D:] += jnp.sum(dz1, axis=0, keepdims=True)
        dyp_ref[...] = dypb
        dyg_ref[...] = dygb
        dpp = _dot_nt(dypb[:, 0:256], wpp_ref[0])
        for j in range(1, 4):
            dpp = dpp + _dot_nt(dypb[:, j * 256:(j + 1) * 256], wpp_ref[j])
        dpp_ref[...] = dpp.astype(BF16)
        dog = _dot_nt(dygb, wgla_ref[...])
        gh = gh_ref[...]
        dgh = jnp.zeros((1, HV), F32)
        for h in range(HEADS):
            cs = slice(h * HV, (h + 1) * HV)
            ov = o_ref[:, cs].astype(F32)
            r = lax.rsqrt(jnp.mean(ov * ov, axis=-1, keepdims=True) + EPS)
            oh = ov * r
            zo = zog_ref[:, cs].astype(F32)
            sg = _sigmoid(zo)
            dog_h = dog[:, cs]
            don = dog_h * zo * sg
            dzog_ref[:, cs] = (dog_h * oh * gh * sg * (1.0 + zo * (1.0 - sg))).astype(BF16)
            dgh = dgh + jnp.sum(don * oh, axis=0, keepdims=True)
            doh = don * gh
            do_ref[:, cs] = (r * (doh - oh * jnp.mean(doh * oh, axis=-1, keepdims=True))).astype(BF16)
        dgh_ref[...] += dgh

    row = lambda i: (i, 0)
    const2 = lambda i: (0, 0)
    return pl.pallas_call(
        body, name="merge_bwd", grid=(s // ts,),
        in_specs=[pl.BlockSpec((ts, D), row), pl.BlockSpec((ts, D), lambda i: (i, 0)), pl.BlockSpec((ts, D), lambda i: (i, 1)),
                  pl.BlockSpec((ts, D), lambda i: (i, OFF_OG // D)), pl.BlockSpec((ts, D), row), pl.BlockSpec((ts, D), row),
                  pl.BlockSpec((ts, D), row), pl.BlockSpec((1, 2 * D), const2), pl.BlockSpec((1, HV), const2),
                  pl.BlockSpec((4, POOL_W, 256), lambda i: (0, 0, 0)), pl.BlockSpec((D, D), const2),
                  pl.BlockSpec((D, D), const2), ANY],
        out_specs=[pl.BlockSpec((ts, 2 * D), row), pl.BlockSpec((ts, D), row), pl.BlockSpec((ts, D), row),
                   pl.BlockSpec((ts, POOL_W), row), pl.BlockSpec((ts, D), row), pl.BlockSpec((ts, D), row),
                   pl.BlockSpec((1, 2 * D), const2), pl.BlockSpec((1, HV), const2)],
        out_shape=[jax.ShapeDtypeStruct((s, N_INR), BF16), jax.ShapeDtypeStruct((s, D), BF16),
                   jax.ShapeDtypeStruct((s, D), BF16), jax.ShapeDtypeStruct((s, POOL_W), BF16),
                   jax.ShapeDtypeStruct((s, D), BF16), jax.ShapeDtypeStruct((s, D), BF16),
                   jax.ShapeDtypeStruct((1, 2 * D), F32), jax.ShapeDtypeStruct((1, HV), F32)],
        compiler_params=_cp("arbitrary"),
    )(dx1b, zr, zr, zr, yp, yg, o, bgate, ghead, wpp, wgla, wout, after)


HALO = 16
CCH = D_FF // 2


def _conv_taps(u_ref, halo_ref, cs, first, ts):
    u = u_ref[:, cs].astype(F32)
    hal = halo_ref[:, cs].astype(F32)
    h1 = jnp.where(first, 0.0, _pick_row(hal, HALO - 1))
    h2 = jnp.where(first, 0.0, _pick_row(hal, HALO - 2))
    row8 = _rows((8, u.shape[1]))
    r1, r2 = pltpu.roll(u, 1, 0), pltpu.roll(u, 2, 0)
    r1 = jnp.concatenate([jnp.where(row8 == 0, h1, r1[:8]), r1[8:]], axis=0)
    r2 = jnp.concatenate([jnp.where(row8 == 0, h2, jnp.where(row8 == 1, h1, r2[:8])), r2[8:]], axis=0)
    return u, r1, r2


def _ffn_down_loss(u, x1, tgt, wconv, bconv, wdown, gfin, ts):
    s = x1.shape[0]

    def body(u_ref, halo_ref, x1_ref, t_ref, wc_ref, bc_ref, wd_ref, gf_ref, a_ref, c_ref, dx_ref, dxb_ref, ls_ref,
             dgf_ref):
        i = pl.program_id(0)

        @pl.when(i == 0)
        def _():
            ls_ref[...] = jnp.zeros_like(ls_ref)
            dgf_ref[...] = jnp.zeros_like(dgf_ref)

        first = i == 0
        acc = x1_ref[...]
        for hf in range(D_FF // CCH):
            cg = slice(hf * CCH, (hf + 1) * CCH)
            cv = slice(D_FF + hf * CCH, D_FF + (hf + 1) * CCH)
            vals = []
            for cs in (cg, cv):
                u0, u1, u2 = _conv_taps(u_ref, halo_ref, cs, first, ts)
                vals.append(bc_ref[:, cs] + wc_ref[0:1, cs] * u2 + wc_ref[1:2, cs] * u1 + wc_ref[2:3, cs] * u0)
                c_ref[:, cs] = vals[-1].astype(BF16)
            a = (vals[0] * _sigmoid(vals[0]) * vals[1]).astype(BF16)
            a_ref[:, cg] = a
            acc = acc + _dot(a, wd_ref[cg, :])
        r = lax.rsqrt(jnp.mean(acc * acc, axis=-1, keepdims=True) + EPS)
        xh = acc * r
        gf = gf_ref[...]
        err = xh * gf - t_ref[...]
        ls_ref[...] += (0.5 / D) * jnp.sum(jnp.sum(err * err, axis=-1, keepdims=True), axis=0, keepdims=True)
        dy = err * (1.0 / D)
        dgf_ref[...] += jnp.sum(dy * xh, axis=0, keepdims=True)
        dxh = dy * gf
        dx = r * (dxh - xh * jnp.mean(dxh * xh, axis=-1, keepdims=True))
        dx_ref[...] = dx
        dxb_ref[...] = dx.astype(BF16)

    row = lambda i: (i, 0)
    const2 = lambda i: (0, 0)
    return pl.pallas_call(
        body, name="ffn_down_loss", grid=(s // ts,),
        in_specs=[pl.BlockSpec((ts, N_UP), row),
                  pl.BlockSpec((HALO, N_UP), lambda i: (jnp.maximum(i * (ts // HALO) - 1, 0), 0)),
                  pl.BlockSpec((ts, D), row), pl.BlockSpec((ts, D), row), pl.BlockSpec((3, N_UP), const2),
                  pl.BlockSpec((1, N_UP), const2), pl.BlockSpec((D_FF, D), const2), pl.BlockSpec((1, D), const2)],
        out_specs=[pl.BlockSpec((ts, D_FF), row), pl.BlockSpec((ts, N_UP), row), pl.BlockSpec((ts, D), row),
                   pl.BlockSpec((ts, D), row), pl.BlockSpec((1, 128), const2), pl.BlockSpec((1, D), const2)],
        out_shape=[jax.ShapeDtypeStruct((s, D_FF), BF16), jax.ShapeDtypeStruct((s, N_UP), BF16),
                   jax.ShapeDtypeStruct((s, D), F32), jax.ShapeDtypeStruct((s, D), BF16),
                   jax.ShapeDtypeStruct((1, 128), F32), jax.ShapeDtypeStruct((1, D), F32)],
        compiler_params=_cp("arbitrary"),
    )(u, u, x1, tgt, wconv, bconv, wdown, gfin)


def _ffn_bwd(dx2b, u, c, wconv, wdown, ts):
    s = dx2b.shape[0]
    nt = s // ts

    def body(dx_ref, u_ref, c_ref, wc_ref, wd_ref, du_ref, db_ref, dw_ref, nxt_ref):
        @pl.when(pl.program_id(0) == 0)
        def _():
            db_ref[...] = jnp.zeros_like(db_ref)
            dw_ref[...] = jnp.zeros_like(dw_ref)
            nxt_ref[...] = jnp.zeros_like(nxt_ref)

        dxv = dx_ref[...]
        row8 = _rows((8, CCH))
        for hf in range(D_FF // CCH):
            cg = slice(hf * CCH, (hf + 1) * CCH)
            cv = slice(D_FF + hf * CCH, D_FF + (hf + 1) * CCH)
            da = _dot_nt(dxv, wd_ref[cg, :])
            gate = c_ref[:, cg].astype(F32)
            val = c_ref[:, cv].astype(F32)
            sg = _sigmoid(gate)
            dcs = (da * val * sg * (1.0 + gate * (1.0 - sg)), da * gate * sg)
            for cs, dc in zip((cg, cv), dcs):
                n1 = nxt_ref[0:1, cs]
                n2 = nxt_ref[1:2, cs]
                r1, r2 = pltpu.roll(dc, ts - 1, 0), pltpu.roll(dc, ts - 2, 0)
                f1 = jnp.concatenate([r1[:ts - 8], jnp.where(row8 == 7, n1, r1[ts - 8:])], axis=0)
                f2 = jnp.concatenate([r2[:ts - 8], jnp.where(row8 == 7, n2, jnp.where(row8 == 6, n1, r2[ts - 8:]))], axis=0)
                uv = u_ref[:, cs].astype(F32)
                db_ref[:, cs] += jnp.sum(dc, axis=0, keepdims=True)
                dw_ref[0:1, cs] += jnp.sum(f2 * uv, axis=0, keepdims=True)
                dw_ref[1:2, cs] += jnp.sum(f1 * uv, axis=0, keepdims=True)
                dw_ref[2:3, cs] += jnp.sum(dc * uv, axis=0, keepdims=True)
                du_ref[:, cs] = (wc_ref[2:3, cs] * dc + wc_ref[1:2, cs] * f1 + wc_ref[0:1, cs] * f2).astype(BF16)
                nxt_ref[:, cs] = dc[0:8, :]

    rev = lambda i: (nt - 1 - i, 0)
    const2 = lambda i: (0, 0)
    return pl.pallas_call(
        body, name="ffn_bwd", grid=(nt,),
        in_specs=[pl.BlockSpec((ts, D), rev), pl.BlockSpec((ts, N_UP), rev), pl.BlockSpec((ts, N_UP), rev),
                  pl.BlockSpec((3, N_UP), const2), pl.BlockSpec((D_FF, D), const2)],
        out_specs=[pl.BlockSpec((ts, N_UP), rev), pl.BlockSpec((1, N_UP), const2), pl.BlockSpec((3, N_UP), const2)],
        out_shape=[jax.ShapeDtypeStruct((s, N_UP), BF16), jax.ShapeDtypeStruct((1, N_UP), F32),
                   jax.ShapeDtypeStruct((3, N_UP), F32)],
        scratch_shapes=[pltpu.VMEM((8, N_UP), F32)],
        compiler_params=_cp("arbitrary"),
    )(dx2b, u, c, wconv, wdown)


ANY = pl.BlockSpec(memory_space=pl.ANY)


def _place():
    x, y, c = lax.axis_index("x"), lax.axis_index("y"), lax.axis_index("c")
    chips = [(1 - x, y), (x, 1 - y), (1 - x, 1 - y)]
    return x, y, c, chips


def _half(shape, c, axis):
    size = shape[axis] // 2
    cut = pl.ds(pl.multiple_of(c * size, 8 if axis == 0 else 128), size)
    return (cut, slice(None)) if axis == 0 else (slice(None), cut)


def _half_shape(shape, axis):
    return (shape[0] // 2, shape[1]) if axis == 0 else (shape[0], shape[1] // 2)


def _remote(src, dst, send_sems, recv_sems, k, to):
    return pltpu.make_async_remote_copy(src_ref=src, dst_ref=dst, send_sem=send_sems.at[k], recv_sem=recv_sems.at[k],
                                        device_id=to, device_id_type=MESH)


def _sibling_exchange(grads, axes, smalls, name):
    nb = len(grads)
    n = nb + len(smalls)

    def body(*refs):
        ins, outs = refs[:n], refs[n:2 * n]
        send_sems, recv_sems = refs[2 * n:]
        x, y, c, _ = _place()
        sib = (x, y, 1 - c)
        cps = []
        for a in range(nb):
            theirs = _half(grads[a].shape[1:], 1 - c, axes[a])
            cps.append(_remote(ins[a].at[(slice(None),) + theirs], outs[a], send_sems, recv_sems, a, sib))
        for a in range(nb, n):
            cps.append(_remote(ins[a], outs[a], send_sems, recv_sems, a, sib))
        for cp in cps:
            cp.start()
        for cp in cps:
            cp.wait()

    out_shape = [jax.ShapeDtypeStruct((4,) + _half_shape(g.shape[1:], ax), g.dtype) for g, ax in zip(grads, axes)]
    out_shape += [jax.ShapeDtypeStruct(a.shape, F32) for a in smalls]
    return pl.pallas_call(
        body, name=name, in_specs=[ANY] * n, out_specs=[ANY] * n, out_shape=out_shape,
        scratch_shapes=[pltpu.SemaphoreType.DMA((n,)), pltpu.SemaphoreType.DMA((n,))],
        compiler_params=pltpu.CompilerParams(has_side_effects=True),
    )(*grads, *smalls)


def _gather_share(lands, axes, name):
    n = len(lands)

    def body(*refs):
        outs = refs[n:2 * n]
        send_sems, recv_sems = refs[2 * n:]
        x, y, c, chips = _place()
        sib = (x, y, 1 - c)
        cps = []
        for a in range(n):
            mine = _half(lands[a].shape[1:], c, axes[a])
            for k, ch in enumerate(chips):
                landed = outs[a].at[(2 * ch[0] + ch[1],) + mine]
                cps.append(_remote(landed, landed, send_sems, recv_sems, 3 * a + k, sib))
        for cp in cps:
            cp.start()
        for a in range(n):
            other = _half(lands[a].shape[1:], 1 - c, axes[a])
            for k, ch in enumerate(chips):
                landed = outs[a].at[(2 * ch[0] + ch[1],) + other]
                _remote(landed, landed, send_sems, recv_sems, 3 * a + k, sib).wait_recv()
        for cp in cps:
            cp.wait_send()

    return pl.pallas_call(
        body, name=name, in_specs=[ANY] * n, out_specs=[ANY] * n,
        out_shape=[jax.ShapeDtypeStruct(a.shape, a.dtype) for a in lands],
        input_output_aliases={a: a for a in range(n)},
        scratch_shapes=[pltpu.SemaphoreType.DMA((3 * n,)), pltpu.SemaphoreType.DMA((3 * n,))],
        compiler_params=pltpu.CompilerParams(has_side_effects=True),
    )(*lands)


def _sibling_share(halves, name):
    n = len(halves)

    def body(*refs):
        ins, outs = refs[:n], refs[n:2 * n]
        send_sems, recv_sems = refs[2 * n:]
        x, y, c, _ = _place()
        cps = [_remote(ins[a], outs[a], send_sems, recv_sems, a, (x, y, 1 - c)) for a in range(n)]
        for cp in cps:
            cp.start()
        for cp in cps:
            cp.wait()

    return pl.pallas_call(
        body, name=name, in_specs=[ANY] * n, out_specs=[ANY] * n,
        out_shape=[jax.ShapeDtypeStruct(h.shape, F32) for h in halves],
        scratch_shapes=[pltpu.SemaphoreType.DMA((n,)), pltpu.SemaphoreType.DMA((n,))],
        compiler_params=pltpu.CompilerParams(has_side_effects=True),
    )(*halves)


HBM = pl.BlockSpec(memory_space=pltpu.HBM)
SEM = pl.BlockSpec(memory_space=pltpu.SEMAPHORE)
DATAFLOW = pltpu.SideEffectType.DATAFLOW_SIDE_EFFECTING


def _split_start(name, srcs, land_shapes, plan, n_copies, after):
    lands = [lax.empty(*ls) if isinstance(ls, tuple) else ls for ls in land_shapes]
    bufs = list(srcs) + lands
    nb, ns = len(bufs), len(srcs)

    def body(*refs):
        send_sems, recv_sems, token = refs[nb + 1], refs[nb + 2], refs[-1]
        for k, (src, dst, to) in enumerate(plan(refs[:ns], refs[ns:nb])):
            _remote(src, dst, send_sems, recv_sems, k, to).start()
        token[...] = jnp.zeros_like(token)

    res = pl.pallas_call(
        body, name=name,
        out_shape=(pltpu.SemaphoreType.DMA((n_copies,)), pltpu.SemaphoreType.DMA((n_copies,)),
                   *[pltpu.HBM(b.shape, b.dtype) for b in bufs], jax.ShapeDtypeStruct((8, 128), F32)),
        in_specs=[HBM] * nb + [ANY],
        out_specs=(SEM, SEM, *[HBM] * nb, pl.BlockSpec(memory_space=pltpu.VMEM)),
        input_output_aliases={i: 2 + i for i in range(nb)},
        compiler_params=pltpu.CompilerParams(has_side_effects=DATAFLOW),
    )(*[pltpu.with_memory_space_constraint(b, pltpu.HBM) for b in bufs], after)
    return (res[0], res[1], list(res[2:2 + nb])), res[-1]


def _split_relay(name, handle, plan, relay_plan, n_relay, after):
    send_sems, recv_sems, bufs = handle
    nb = len(bufs)

    def body(*refs):
        sends, recvs = refs[nb], refs[nb + 1]
        for k, (src, dst, to) in enumerate(plan((), refs[:nb])):
            _remote(src, dst, sends, recvs, k, to).wait_recv()
        relay_sends, relay_recvs, token = refs[nb + 3], refs[nb + 4], refs[-1]
        for k, (src, dst, to) in enumerate(relay_plan((), refs[:nb])):
            _remote(src, dst, relay_sends, relay_recvs, k, to).start()
        token[...] = jnp.zeros_like(token)

    res = pl.pallas_call(
        body, name=name,
        out_shape=(pltpu.SemaphoreType.DMA((n_relay,)), pltpu.SemaphoreType.DMA((n_relay,)),
                   *[pltpu.HBM(b.shape, b.dtype) for b in bufs], jax.ShapeDtypeStruct((8, 128), F32)),
        in_specs=[HBM] * nb + [SEM, SEM, ANY],
        out_specs=(SEM, SEM, *[HBM] * nb, pl.BlockSpec(memory_space=pltpu.VMEM)),
        input_output_aliases={i: 2 + i for i in range(nb)},
        compiler_params=pltpu.CompilerParams(has_side_effects=DATAFLOW),
    )(*bufs, send_sems, recv_sems, after)
    passed = list(res[2:2 + nb])
    return (send_sems, recv_sems, passed), (res[0], res[1], passed), res[-1]


def _split_wait(name, handle, n_srcs, plan, after, arrived=False):
    send_sems, recv_sems, bufs = handle
    nb = len(bufs)

    def body(*refs):
        sends, recvs = refs[nb], refs[nb + 1]
        for k, (src, dst, to) in enumerate(plan(refs[:n_srcs], refs[n_srcs:nb])):
            cp = _remote(src, dst, sends, recvs, k, to)
            cp.wait_send()
            if arrived:
                continue
            cp.wait_recv()

    res = pl.pallas_call(
        body, name=name, out_shape=[pltpu.HBM(b.shape, b.dtype) for b in bufs],
        in_specs=[HBM] * nb + [SEM, SEM, ANY], out_specs=[HBM] * nb,
        input_output_aliases={i: i for i in range(nb)},
        compiler_params=pltpu.CompilerParams(has_side_effects=DATAFLOW),
    )(*bufs, send_sems, recv_sems, after)
    return list(res[:n_srcs]), list(res[n_srcs:])


def _relay_plan(shapes, axes):
    def plan(srcs, lands):
        x, y, c, _ = _place()
        first = c == 0
        from_x, from_y = jnp.where(first, 1 - x, x), jnp.where(first, y, 1 - y)
        to = (jnp.where(first, x, 1 - x), jnp.where(first, 1 - y, y), c)
        out = []
        for a, (shape, axis) in enumerate(zip(shapes, axes)):
            got = lands[a].at[(2 * from_x + from_y,) + _half(shape, c, axis)]
            out.append((got, got, to))
        return out
    return plan


def _gather_plan(shapes, axes, n_whole=0, relayed=False):
    def plan(srcs, lands):
        x, y, c, chips = _place()
        me = 2 * x + y
        out = []
        for a, (shape, axis) in enumerate(zip(shapes, axes)):
            own = lands[a].at[(me,) + _half(shape, c, axis)]
            for ch in chips[:2] if relayed else chips:
                out.append((own, own, (ch[0], ch[1], c)))
        for a in range(len(shapes), len(shapes) + n_whole):
            for ch in chips:
                out.append((lands[a].at[me], lands[a].at[me], (ch[0], ch[1], c)))
        return out
    return plan


def _share_plan(shapes, axes):
    def plan(srcs, lands):
        x, y, c, chips = _place()
        out = []
        for a, (shape, axis) in enumerate(zip(shapes, axes)):
            mine = _half(shape, c, axis)
            for ch in chips:
                landed = lands[a].at[(2 * ch[0] + ch[1],) + mine]
                out.append((landed, landed, (x, y, 1 - c)))
        return out
    return plan


def _sibling_plan(shapes, axes):
    def plan(srcs, lands):
        x, y, c, _ = _place()
        return [(srcs[a].at[(slice(None),) + _half(shape, 1 - c, axis)], lands[a], (x, y, 1 - c))
                for a, (shape, axis) in enumerate(zip(shapes, axes))]
    return plan


def _whole_to_sibling_plan(n):
    def plan(srcs, lands):
        x, y, c, _ = _place()
        return [(srcs[a], lands[a], (x, y, 1 - c)) for a in range(n)]
    return plan


def _reduce_plan(n_big, n_small):
    def plan(srcs, lands):
        x, y, c, chips = _place()
        out = []
        for a in range(n_big):
            for k, ch in enumerate(chips):
                out.append((srcs[a].at[2 * ch[0] + ch[1]], lands[a].at[k], (ch[0], ch[1], c)))
        for a in range(n_big, n_big + n_small):
            for ch in chips:
                out.append((srcs[a], lands[a].at[2 * x + y], (ch[0], ch[1], c)))
        return out
    return plan


def _row_tile(rows, cols, mult):
    best = mult
    for t in range(mult, rows + 1, mult):
        if rows % t == 0 and t * cols * 4 <= (2 << 20):
            best = t
    return best if rows % best == 0 else rows


COL_TILE = 256


def _half_tiling(hshape, axis, mult):
    hr, hc = hshape
    if axis == 0:
        tr = _row_tile(hr, hc, mult)
        return tr, hc, hr // tr
    return hr, COL_TILE, hc // COL_TILE


def _tile_idx(axis, t):
    return (t, 0) if axis == 0 else (0, t)


def _chip_partial(place, g, t, axis, name):
    hshape = t.shape[1:]
    br, bc, nt = _half_tiling(hshape, axis, 16)

    def body(pl_ref, g_ref, t_ref, pf_ref, pb_ref):
        v = g_ref[...].astype(F32) + t_ref[...].astype(F32)
        pb_ref[...] = v.astype(BF16)

        @pl.when(pl.program_id(1) == pl_ref[0])
        def _():
            pf_ref[...] = v

    blk = (None, br, bc)
    return pl.pallas_call(
        body, name=name,
        grid_spec=pltpu.PrefetchScalarGridSpec(
            num_scalar_prefetch=1, grid=(nt, 4),
            in_specs=[pl.BlockSpec(blk, lambda i, j, p: (j,) + _tile_idx(axis, p[1] * nt + i)),
                      pl.BlockSpec(blk, lambda i, j, p: (j,) + _tile_idx(axis, i))],
            out_specs=[pl.BlockSpec((br, bc), lambda i, j, p: _tile_idx(axis, i)),
                       pl.BlockSpec(blk, lambda i, j, p: (j,) + _tile_idx(axis, i))]),
        out_shape=[jax.ShapeDtypeStruct(hshape, F32), jax.ShapeDtypeStruct((4,) + hshape, BF16)],
        compiler_params=_cp("arbitrary", "arbitrary"),
    )(place, g, t)


def _finish_half(pf, rb, axis, name):
    hshape = pf.shape
    br, bc, nt = _half_tiling(hshape, axis, 16)

    def body(pf_ref, rb_ref, o_ref):
        o_ref[...] = ((pf_ref[...] + rb_ref[0].astype(F32)) + rb_ref[1].astype(F32)) + rb_ref[2].astype(F32)

    return pl.pallas_call(
        body, name=name, grid=(nt,),
        in_specs=[pl.BlockSpec((br, bc), lambda i: _tile_idx(axis, i)),
                  pl.BlockSpec((3, br, bc), lambda i: (0,) + _tile_idx(axis, i))],
        out_specs=pl.BlockSpec((br, bc), lambda i: _tile_idx(axis, i)),
        out_shape=jax.ShapeDtypeStruct(hshape, F32),
        compiler_params=_cp("arbitrary"),
    )(pf, rb)


def _adam_math(w, g, m, v):
    m = ADAM_B1 * m + (1.0 - ADAM_B1) * g
    v = ADAM_B2 * v + (1.0 - ADAM_B2) * (g * g)
    m_hat = m / (1.0 - ADAM_B1 ** ADAM_STEP)
    v_hat = v / (1.0 - ADAM_B2 ** ADAM_STEP)
    return -ADAM_LR * (m_hat / (jnp.sqrt(v_hat) + ADAM_EPS) + ADAM_WD * w), m, v


def _adam_halves(place, w, mine, theirs, m, v, axis, name):
    br, bc, nt = _half_tiling(mine.shape, axis, 8)

    def body(pl_ref, w_ref, a_ref, b_ref, m_ref, v_ref, g_ref, d_ref, mo_ref, vo_ref):
        is_mine = pl.program_id(0) // nt == pl_ref[1]
        g = jnp.where(is_mine, a_ref[...], b_ref[...])
        d, mn, vn = _adam_math(w_ref[...], g, m_ref[...], v_ref[...])
        g_ref[...] = g
        d_ref[...] = d
        mo_ref[...] = mn
        vo_ref[...] = vn

    full = pl.BlockSpec((br, bc), lambda i, p: _tile_idx(axis, i))
    mine_spec = pl.BlockSpec((br, bc), lambda i, p: _tile_idx(axis, jnp.where(i // nt == p[1], i % nt, nt - 1)))
    theirs_spec = pl.BlockSpec((br, bc), lambda i, p: _tile_idx(axis, jnp.where(i // nt == p[1], 0, i % nt)))
    return pl.pallas_call(
        body, name=name,
        grid_spec=pltpu.PrefetchScalarGridSpec(
            num_scalar_prefetch=1, grid=(2 * nt,), in_specs=[full, mine_spec, theirs_spec, full, full],
            out_specs=[full] * 4),
        out_shape=[jax.ShapeDtypeStruct(w.shape, F32)] * 4, compiler_params=_cp("arbitrary"),
    )(place, w, mine, theirs, m, v)


def _add_many(xs, ys, name):
    n = len(xs)

    def body(*refs):
        for i in range(n):
            refs[2 * n + i][...] = refs[i][...] + refs[n + i][...]

    return pl.pallas_call(body, name=name, out_shape=[jax.ShapeDtypeStruct(a.shape, F32) for a in xs])(*xs, *ys)


def _adam_small(place, owns, landed, ws, ms, vs, widths):
    n, nw = len(owns), len(ws)

    def body(pl_ref, *refs):
        own_r, land_r = refs[:n], refs[n:2 * n]
        w_r, m_r, v_r = (refs[2 * n + k * nw:2 * n + (k + 1) * nw] for k in range(3))
        outs = refs[2 * n + 3 * nw:]
        g_o, d_o, m_o, v_o = outs[:n], outs[n:n + nw], outs[n + nw:n + 2 * nw], outs[n + 2 * nw:]
        for me in range(4):
            @pl.when(pl_ref[0] == me)
            def _(me=me):
                for i in range(n):
                    p = [own_r[i][...] if k == me else land_r[i][k] for k in range(4)]
                    g = ((p[0] + p[1]) + p[2]) + p[3]
                    if i < nw and widths[i]:
                        g = g[:, me * widths[i]:(me + 1) * widths[i]]
                    g_o[i][...] = g
                    if i < nw:
                        d, mn, vn = _adam_math(w_r[i][...], g, m_r[i][...], v_r[i][...])
                        d_o[i][...] = d
                        m_o[i][...] = mn
                        v_o[i][...] = vn

    g_shapes = [jax.ShapeDtypeStruct(ws[i].shape if i < nw else owns[i].shape, F32) for i in range(n)]
    w_shapes = [jax.ShapeDtypeStruct(w.shape, F32) for w in ws]
    whole = lambda a: pl.BlockSpec(a.shape, lambda i, p, nd=len(a.shape): (0,) * nd)
    ins = list(owns) + list(landed) + list(ws) + list(ms) + list(vs)
    out_shape = g_shapes + w_shapes * 3
    out = pl.pallas_call(
        body, name="adam_small",
        grid_spec=pltpu.PrefetchScalarGridSpec(num_scalar_prefetch=1, grid=(1,), in_specs=[whole(a) for a in ins],
                                               out_specs=[whole(a) for a in out_shape]),
        out_shape=out_shape, compiler_params=_cp("arbitrary"),
    )(place, *ins)
    return out[:n], out[n:n + nw], out[n + nw:n + 2 * nw], out[n + 2 * nw:]


def kernel(x, g_mix, w_in, b_gate, w_gk_up, b_gk, w_pool_grp, pool_scale, g_gla_head, w_pool_proj, w_gla_proj, w_out, g_ffn, w_up, w_conv, b_conv, w_down, g_final, loss_target, m_g_mix, m_w_in, m_b_gate, m_w_gk_up, m_b_gk, m_w_pool_grp, m_pool_scale, m_g_gla_head, m_w_pool_proj, m_w_gla_proj, m_w_out, m_g_ffn, m_w_up, m_w_conv, m_b_conv, m_w_down, m_g_final, v_g_mix, v_w_in, v_b_gate, v_w_gk_up, v_b_gk, v_w_pool_grp, v_pool_scale, v_g_gla_head, v_w_pool_proj, v_w_gla_proj, v_w_out, v_g_ffn, v_w_up, v_w_conv, v_b_conv, v_w_down, v_g_final):
    s = x.shape[1]
    ts = min(s, 512)
    tm = min(s, 256)
    cx, cy, cc = lax.axis_index("x"), lax.axis_index("y"), lax.axis_index("c")
    chip = 2 * cx + cy
    place = jnp.stack([chip, cc]).astype(jnp.int32)

    big_names = ("w_in", "w_pool_proj", "w_gla_proj", "w_out", "w_up", "w_down")
    axes = (1, 0, 0, 0, 0, 0)
    shards = dict(w_in=jnp.transpose(w_in[0]), w_pool_proj=w_pool_proj[0], w_gla_proj=w_gla_proj[0], w_out=w_out[0],
                  w_up=w_up[0], w_down=w_down[0])
    def landing(own_shards):
        return [lax.dynamic_update_slice(lax.empty((4,) + o_.shape, o_.dtype), o_[None], (chip, 0, 0))
                for o_ in own_shards]

    def gather_start(tag, lands, n_halves, group_axes, after, relayed=False):
        n_whole = len(lands) - n_halves
        plan = _gather_plan([l_.shape[1:] for l_ in lands[:n_halves]], group_axes, n_whole, relayed)
        n_copies = (2 if relayed else 3) * n_halves + 3 * n_whole
        handle, token = _split_start("gather_" + tag + "_start", [], lands, plan, n_copies, after)
        return (handle, plan, n_halves, group_axes), token

    def gather_relay(tag, started, after):
        handle, plan, n_halves, group_axes = started
        relay_plan = _relay_plan([b_.shape[1:] for b_ in handle[2]], group_axes)
        first, relay, token = _split_relay("gather_" + tag + "_relay", handle, plan, relay_plan, n_halves, after)
        return (first, plan, relay, relay_plan, group_axes), token

    def gather_finish_relayed(tag, relayed, after):
        first, plan, relay, relay_plan, group_axes = relayed
        lands = _split_wait("gather_" + tag + "_sent", first, 0, plan, after, arrived=True)[1]
        lands = _split_wait("gather_" + tag + "_wait", (relay[0], relay[1], lands), 0, relay_plan, after)[1]
        return _gather_share(lands, group_axes, "gather_" + tag + "_share")

    in_w, tok = gather_start("in", landing([jnp.transpose(w_in[0].astype(BF16))]), 1, axes[:1], g_mix, relayed=True)
    zero = tok[0, 0]
    own = landing([(shards[n] + zero).astype(BF16) for n in big_names[1:]] + [w_gk_up[0] + zero, w_conv[0] + zero])
    own = lax.optimization_barrier(own)
    in_r, tok = gather_relay("in", in_w, own[4])
    xs, tgt = x[0], loss_target[0]
    h = _rmsnorm(xs, g_mix, tok, "norm_mix", ts)
    m_in_t, v_in_t = jnp.transpose(m_w_in[0]), jnp.transpose(v_w_in[0])
    h, m_in_t, v_in_t = lax.optimization_barrier((h, m_in_t, v_in_t))
    mix_w, tok = gather_start("mix", own[0:3] + own[5:7], 3, axes[1:4], m_in_t)
    up_w, tok = gather_start("up", own[3:4], 1, axes[4:5], tok)
    down_w, tok = gather_start("down", own[4:5], 1, axes[5:6], tok)

    def forward_start(tag, started, after):
        handle, plan, n_halves, group_axes = started
        lands = _split_wait("gather_" + tag + "_wait", handle, 0, plan, after)[1]
        plan = _share_plan([l_.shape[1:] for l_ in lands[:n_halves]], group_axes)
        share, token = _split_start("gather_" + tag + "_share_start", [], lands[:n_halves], plan, 3 * n_halves, after)
        return (share, plan, lands[n_halves:]), token

    def forward_done(tag, forwarded, after):
        share, plan, _ = forwarded
        return _split_wait("gather_" + tag + "_share_wait", share, 0, plan, after)[1]
    wgrp = w_pool_grp[0]
    w_in_t = gather_finish_relayed("in", in_r, tok)[0]
    nsh = N_IN // 4

    zr, w_in_rt = _in_proj(h, w_in_t, PROJ_TILE)
    p, pp = _pool_fwd(zr, wgrp, pool_scale)
    mix_f, tok = forward_start("mix", mix_w, pp)
    wgk4, wconv4 = mix_f[2]
    wgk_full = jnp.transpose(wgk4, (1, 0, 2)).reshape(GATE_RANK, 512) + tok[0, 0]
    wconv_full = jnp.transpose(wconv4, (1, 0, 2)).reshape(3, N_UP)
    wgk_pad = jnp.concatenate([wgk_full, jnp.zeros((128 - GATE_RANK, 512), F32)], axis=0)
    o, og, sp = _gla_fwd(zr, wgk_pad, b_gk, g_gla_head, ts)
    wpp, wgla, wout = forward_done("mix", mix_f, og)
    wgla, wout = wgla.reshape(D, D), wout.reshape(D, D)
    up_f, tok = forward_start("up", up_w, og)
    x1, mixed, yp, yg, h2 = _merge_fwd(xs, zr, pp, og, b_gate, wpp, wgla, wout, g_ffn, tok, ts)
    wup, = forward_done("up", up_f, x1)
    down_f, tok = forward_start("down", down_w, x1)
    u = _matmul_resident(h2, wup, tok, "ffn_up")
    wdown = forward_done("down", down_f, u)[0].reshape(D_FF, D)
    a, conv_out, dx2, dx2b, loss_part, dgfin = _ffn_down_loss(u, x1, tgt, wconv_full, b_conv, wdown,
                                                              g_final.reshape(1, D), tm)

    du, dbconv, dwconv = _ffn_bwd(dx2b, u, conv_out, wconv_full, wdown, tm)
    dw_down = _matmul_tn(a, dx2b, "dw_down", D, tm=D_FF // 2)
    dw_up = _matmul_tn(h2, du, "dw_up", UP_SHARD, shard_major=True)

    def exchange_start(tag, grads, group_axes, after):
        plan = _sibling_plan([g.shape[1:] for g in grads], group_axes)
        lands = [((4,) + _half_shape(g.shape[1:], ax), g.dtype) for g, ax in zip(grads, group_axes)]
        handle, token = _split_start("sibling_" + tag + "_start", grads, lands, plan, len(grads), after)
        return (handle, plan, len(grads)), token

    def partials(tag, names, group_axes, exchange, after):
        handle, plan, n = exchange
        mine, theirs = _split_wait("sibling_" + tag + "_wait", handle, n, plan, after)
        return zip(*[_chip_partial(place, g, t, ax, "chip_partial_" + nm)
                     for nm, ax, g, t in zip(names, group_axes, mine, theirs)])

    ffn_names, ffn_axes = ("w_up", "w_down"), (0, 0)
    ffn_x, token = exchange_start("ffn", [dw_up, dw_down.reshape(4, 704, D)], ffn_axes, du)
    dx1, dx1b, dgffn = _matmul_nt_normbwd(du, wup, x1, g_ffn, dx2, token, "ffn_up_bwd", ts)
    ffn_pf, ffn_pb = partials("ffn", ffn_names, ffn_axes, ffn_x, dx1b)
    ffn_plan = _reduce_plan(2, 0)
    ffn_handle, token = _split_start("reduce_ffn_start", ffn_pb, [((3,) + p.shape[1:], BF16) for p in ffn_pb],
                                     ffn_plan, 6, ffn_pf[0])

    dzr, dyp, dyg, dpp, do, dzog, dbgate, dghead = _merge_bwd(dx1b, zr, yp, yg, o, b_gate, g_gla_head, wpp, wgla, wout,
                                                             token, ts)
    dzr = lax.dynamic_update_slice(dzr, dzog, (0, OFF_OG))
    dw_out = _matmul_tn(mixed, dx1b, "dw_out", D, tm=512)
    dw_gla = _matmul_tn(og, dyg, "dw_gla", D, tm=512)
    dw_pp = _matmul_tn(pp, dyp, "dw_pp", 256, shard_major=True)

    out_names, out_axes = ("w_pool_proj", "w_gla_proj", "w_out"), (0, 0, 0)
    out_x, token = exchange_start("out", [dw_pp, dw_gla.reshape(4, 256, D), dw_out.reshape(4, 256, D)], out_axes, dpp)
    dzr, dwgrp, dscale = _pool_bwd(p, dpp, wgrp, pool_scale, token, dzr)
    out_pf, out_pb = partials("out", out_names, out_axes, out_x, dwgrp)
    out_plan = _reduce_plan(3, 0)
    out_handle, token = _split_start("reduce_out_start", out_pb, [((3,) + p_.shape[1:], BF16) for p_ in out_pb],
                                     out_plan, 9, out_pf[0])
    dq, dk, dzr, dgpre = _gla_bwd(zr, do, sp, wgk_pad, b_gk, token, dzr, ts)
    dzr, dwgk, dbgk = _gk_bwd(dgpre, zr, wgk_pad, dgpre, dzr, ts)
    dzr = lax.dynamic_update_slice(lax.dynamic_update_slice(dzr, dq, (0, OFF_Q)), dk, (0, OFF_K))
    dw_rt = _matmul_tn(dzr, h, "dw_in", D, tm=PROJ_TILE)

    def grad_rows(lo, hi):
        out = []
        for seg_lo, seg_hi, at in ((0, 1536, OFF_POOL), (1536, 3584, OFF_V), (3584, 3600, OFF_GK), (3600, N_IN, OFF_GATE)):
            a_, b_ = max(lo, seg_lo), min(hi, seg_hi)
            if a_ < b_:
                out.append(dw_rt[at + a_ - seg_lo:at + b_ - seg_lo])
        return jnp.concatenate(out, axis=0)

    dw_in_t = jnp.stack([grad_rows(j * nsh, (j + 1) * nsh) for j in range(4)])

    ms = dict(w_in=m_in_t, w_pool_proj=m_w_pool_proj[0], w_gla_proj=m_w_gla_proj[0], w_out=m_w_out[0],
              w_up=m_w_up[0], w_down=m_w_down[0])
    vs = dict(w_in=v_in_t, w_pool_proj=v_w_pool_proj[0], w_gla_proj=v_w_gla_proj[0], w_out=v_w_out[0],
              w_up=v_w_up[0], w_down=v_w_down[0])
    grad, delta, new_m, new_v = {}, {}, {}, {}

    def finish(names, group_axes, part_f, landed):
        return [_finish_half(pf, rb, ax, "finish_" + n) for n, ax, pf, rb in zip(names, group_axes, part_f, landed)]

    def update(names, group_axes, halves, sib_halves):
        for n, ax, mine, theirs in zip(names, group_axes, halves, sib_halves):
            res = _adam_halves(place, shards[n], mine, theirs, ms[n], vs[n], ax, "adam_" + n)
            if n == "w_in":
                res = [jnp.transpose(r_) for r_ in res]
            grad[n], delta[n], new_m[n], new_v[n] = [r_[None] for r_ in res]

    rest_names, rest_axes = ffn_names + out_names, ffn_axes + out_axes
    in_x, token = exchange_start("in", [dw_in_t], (1,), dw_rt)
    _, ffn_landed = _split_wait("reduce_ffn_wait", ffn_handle, 2, ffn_plan, token)
    _, out_landed = _split_wait("reduce_out_wait", out_handle, 3, out_plan, ffn_landed[0])
    rest_halves = lax.optimization_barrier(finish(rest_names, rest_axes, ffn_pf + out_pf, ffn_landed + out_landed))
    (in_pf,), (in_pb,) = partials("in", ("w_in",), (1,), in_x, rest_halves[-1])
    in_plan = _reduce_plan(1, 0)
    in_handle, token = _split_start("reduce_in_start", [in_pb], [((3,) + in_pb.shape[1:], BF16)], in_plan, 3, in_pf)
    rest_plan = _whole_to_sibling_plan(len(rest_halves))
    rest_share, token = _split_start("sibling_share_rest_start", rest_halves, [(h_.shape, F32) for h_ in rest_halves],
                                     rest_plan, len(rest_halves), token)
    grad_x, _, dgmix = _matmul_nt_normbwd(dzr, w_in_rt, xs, g_mix, dx1, token, "in_proj_bwd", ts, transposed=True)
    small_names = ("g_mix", "b_gate", "w_gk_up", "b_gk", "w_pool_grp", "pool_scale", "g_gla_head", "g_ffn", "w_conv",
                   "b_conv", "g_final")
    small_mine = [dgmix, dbgate, dwgk[:GATE_RANK], dbgk, dwgrp.reshape(4 * 128, 128), dscale, dghead, dgffn, dwconv, dbconv,
                  dgfin, loss_part]
    small_sib = _sibling_exchange([], (), small_mine, "sibling_exchange_small")
    small_chip = _add_many(small_mine, small_sib, "chip_partial_small")
    small_plan = _reduce_plan(0, len(small_chip))
    small_handle, token = _split_start("reduce_small_start", small_chip, [((4,) + a_.shape, F32) for a_ in small_chip],
                                       small_plan, 3 * len(small_chip), small_mine[0])

    rest_halves, rest_sib = _split_wait("sibling_share_rest_wait", rest_share, len(rest_halves), rest_plan, token)
    n_ffn = len(ffn_names)
    update(out_names, out_axes, rest_halves[n_ffn:], rest_sib[n_ffn:])
    updated = lax.optimization_barrier([delta[n] for n in out_names])
    _, in_landed = _split_wait("reduce_in_wait", in_handle, 1, in_plan, updated[0])
    in_halves = finish(("w_in",), (1,), (in_pf,), in_landed)
    update(("w_in",), (1,), in_halves, _sibling_share(in_halves, "sibling_share_in"))
    ffn_halves, _ = lax.optimization_barrier((rest_halves[:n_ffn], delta["w_in"]))
    update(ffn_names, ffn_axes, ffn_halves, rest_sib[:n_ffn])
    small_sent, small_landed = _split_wait("reduce_small_wait", small_handle, len(small_chip), small_plan, delta["w_in"])
    given = dict(g_mix=(g_mix, m_g_mix, v_g_mix), b_gate=(b_gate, m_b_gate, v_b_gate), w_gk_up=(w_gk_up, m_w_gk_up, v_w_gk_up),
                 b_gk=(b_gk, m_b_gk, v_b_gk), w_pool_grp=(w_pool_grp, m_w_pool_grp, v_w_pool_grp),
                 pool_scale=(pool_scale, m_pool_scale, v_pool_scale), g_gla_head=(g_gla_head, m_g_gla_head, v_g_gla_head),
                 g_ffn=(g_ffn, m_g_ffn, v_g_ffn), w_conv=(w_conv, m_w_conv, v_w_conv), b_conv=(b_conv, m_b_conv, v_b_conv),
                 g_final=(g_final, m_g_final, v_g_final))
    flat2 = lambda a: a.reshape(-1, a.shape[-1])
    widths = [dict(w_gk_up=HK, w_conv=UP_SHARD).get(n) for n in small_names]
    totals, ds, mo, vo = _adam_small(place, small_sent, small_landed, *[[flat2(given[n][k]) for n in small_names] for k in range(3)],
                                     widths)
    loss = totals[-1][0, 0]
    for i, n in enumerate(small_names):
        shp = given[n][0].shape
        grad[n], delta[n], new_m[n], new_v[n] = [r_.reshape(shp) for r_ in (totals[i], ds[i], mo[i], vo[i])]

    order = ("g_mix", "w_in", "b_gate", "w_gk_up", "b_gk", "w_pool_grp", "pool_scale", "g_gla_head", "w_pool_proj",
             "w_gla_proj", "w_out", "g_ffn", "w_up", "w_conv", "b_conv", "w_down", "g_final")
    return (loss, grad_x[None], *[grad[n] for n in order], *[delta[n] for n in order], *[new_m[n] for n in order],
            *[new_v[n] for n in order])
```

```python
import jax
import jax.numpy as jnp
from jax import lax
from jax.experimental import pallas as pl
from jax.experimental.pallas import tpu as pltpu

F32 = jnp.float32
BF16 = jnp.bfloat16
MESH = pl.DeviceIdType.MESH

D = 1024
EPS = 1e-6
CHUNK = 64
POOL_W = 512
POOL_WINDOWS = (2, 4, 8, 16)
HEADS = 4
HK = 128
HV = 256
GATE_RANK = 16
D_FF = 2816
N_UP = 2 * D_FF
N_IN = 5648
QSCALE = HK ** -0.5
N_INR = 5760
OFF_GATE, OFF_V, OFF_OG, OFF_POOL, OFF_Q, OFF_K, OFF_GK = 0, 2048, 3072, 4096, 4608, 5120, 5632

ADAM_LR, ADAM_B1, ADAM_B2, ADAM_EPS, ADAM_WD, ADAM_STEP = 0.001, 0.9, 0.999, 1e-08, 0.01, 10

VMEM_LIMIT = 56 * 1024 * 1024
PROJ_TILE = N_INR // 5
UP_SHARD = N_UP // 4


def _cp(*sem):
    return pltpu.CompilerParams(dimension_semantics=sem if sem else None, vmem_limit_bytes=VMEM_LIMIT)


def _dot(a, b):
    return jnp.dot(a, b, preferred_element_type=F32)


def _dot_nt(a, b):
    return lax.dot_general(a, b, (((1,), (1,)), ((), ())), preferred_element_type=F32)


def _dot_tn(a, b):
    return lax.dot_general(a, b, (((0,), (0,)), ((), ())), preferred_element_type=F32)


def _sigmoid(v):
    return 1.0 / (1.0 + jnp.exp(-v))


def _rows(shape):
    return lax.broadcasted_iota(jnp.int32, shape, 0)


def _pick_row(v, r):
    return jnp.sum(jnp.where(_rows(v.shape) == r, v, 0.0), axis=0, keepdims=True)


def _rmsnorm(x, g, after, name, ts):
    s = x.shape[0]

    def body(x_ref, g_ref, after_ref, h_ref):
        xv = x_ref[...]
        r = lax.rsqrt(jnp.mean(xv * xv, axis=-1, keepdims=True) + EPS)
        h_ref[...] = (xv * r * g_ref[...]).astype(BF16)

    return pl.pallas_call(
        body, name=name, grid=(s // ts,),
        in_specs=[pl.BlockSpec((ts, D), lambda i: (i, 0)), pl.BlockSpec((1, D), lambda i: (0, 0)), ANY],
        out_specs=pl.BlockSpec((ts, D), lambda i: (i, 0)), out_shape=jax.ShapeDtypeStruct((s, D), BF16),
        compiler_params=_cp("arbitrary"),
    )(x, g, after)


MM_ROWS = 512


def _matmul_resident(h, w, after, name):
    s = h.shape[0]
    nj, tn = w.shape[0], w.shape[2]
    rc = min(s, MM_ROWS)

    def body(h_ref, w_ref, after_ref, z_ref):
        for r0 in range(0, s, rc):
            z_ref[r0:r0 + rc, :] = _dot(h_ref[r0:r0 + rc, :], w_ref[...]).astype(BF16)

    return pl.pallas_call(
        body, name=name, grid=(nj,),
        in_specs=[pl.BlockSpec((s, D), lambda j: (0, 0)), pl.BlockSpec((None, D, tn), lambda j: (j, 0, 0)), ANY],
        out_specs=pl.BlockSpec((s, tn), lambda j: (0, j)), out_shape=jax.ShapeDtypeStruct((s, nj * tn), BF16),
        compiler_params=_cp("arbitrary"),
    )(h, w, after)


PROJ_PIECES = ((3600, 2048, OFF_GATE), (1536, 2048, OFF_V), (0, 1536, OFF_POOL), (3584, GATE_RANK, OFF_GK))


def _split_by_shard(pieces, rows_per_shard):
    out = []
    for src, n, dst in pieces:
        while n > 0:
            j, r = divmod(src, rows_per_shard)
            m = min(n, rows_per_shard - r)
            out.append((j, r, m, dst))
            src, n, dst = src + m, n - m, dst + m
    return tuple(out)


PROJ_SEGMENTS = _split_by_shard(PROJ_PIECES, N_IN // 4)


def _in_proj(h, w4, tn):
    s = h.shape[0]
    rc = min(s, MM_ROWS)
    nj = N_INR // tn
    first_use = [dst // tn for _, _, _, dst in PROJ_SEGMENTS]

    def body(h_ref, w_hbm, z_ref, wo_hbm, w_ref, stage, sems, out_sem):
        j = pl.program_id(0)
        cps = [pltpu.make_async_copy(w_hbm.at[k], stage.at[k], sems.at[k]) for k in range(4)]
        out_cp = pltpu.make_async_copy(w_ref, wo_hbm, out_sem.at[0])

        @pl.when(j == 0)
        def _():
            for cp in cps:
                cp.start()
            w_ref[OFF_GK + GATE_RANK:, :] = jnp.zeros((N_INR - OFF_GK - GATE_RANK, D), BF16)

        landed = set()
        for step in range(nj):
            due = [seg for seg, at in zip(PROJ_SEGMENTS, first_use) if at == step]
            if due:
                fresh = sorted({seg[0] for seg in due} - landed)
                landed.update(fresh)

                @pl.when(j == step)
                def _(due=due, fresh=fresh, last=step == max(first_use)):
                    for k in fresh:
                        cps[k].wait()
                    for k, r, n, dst in due:
                        w_ref[dst:dst + n, :] = stage[k, r:r + n, :]
                    if last:
                        out_cp.start()

        wt = w_ref[pl.ds(pl.multiple_of(j * tn, 128), tn), :]
        for r0 in range(0, s, rc):
            z_ref[r0:r0 + rc, :] = _dot_nt(h_ref[r0:r0 + rc, :], wt).astype(BF16)

        @pl.when(j == nj - 1)
        def _():
            out_cp.wait()

    return pl.pallas_call(
        body, name="in_proj", grid=(nj,),
        in_specs=[pl.BlockSpec((s, D), lambda j: (0, 0)), ANY],
        out_specs=[pl.BlockSpec((s, tn), lambda j: (0, j)), ANY],
        out_shape=[jax.ShapeDtypeStruct((s, N_INR), BF16), jax.ShapeDtypeStruct((N_INR, D), BF16)],
        scratch_shapes=[pltpu.VMEM((N_INR, D), BF16), pltpu.VMEM(w4.shape, BF16), pltpu.SemaphoreType.DMA((4,)),
                        pltpu.SemaphoreType.DMA((1,))],
        compiler_params=_cp("arbitrary"),
    )(h, w4)


def _matmul_nt_normbwd(dz, w, x, g, resid, after, name, ts, transposed=False):
    s = x.shape[0]
    w_vmem = w.shape if transposed else (D, w.shape[0] * w.shape[2])
    n_sems = 1 if transposed else w.shape[0]

    def body(dz_ref, w_hbm, x_ref, g_ref, r_ref, after_ref, o_ref, ob_ref, dg_ref, w_ref, sems):
        @pl.when(pl.program_id(0) == 0)
        def _():
            if transposed:
                cps = [pltpu.make_async_copy(w_hbm, w_ref, sems.at[0])]
            else:
                kc = w.shape[2]
                cps = [pltpu.make_async_copy(w_hbm.at[j], w_ref.at[:, pl.ds(j * kc, kc)], sems.at[j])
                       for j in range(w.shape[0])]
            for cp in cps:
                cp.start()
            for cp in cps:
                cp.wait()
            dg_ref[...] = jnp.zeros_like(dg_ref)

        dh = _dot(dz_ref[...], w_ref[...]) if transposed else _dot_nt(dz_ref[...], w_ref[...])
        xv = x_ref[...]
        r = lax.rsqrt(jnp.mean(xv * xv, axis=-1, keepdims=True) + EPS)
        xh = xv * r
        dg_ref[...] += jnp.sum(dh * xh, axis=0, keepdims=True)
        dxh = dh * g_ref[...]
        out = r_ref[...] + r * (dxh - xh * jnp.mean(dxh * xh, axis=-1, keepdims=True))
        o_ref[...] = out
        ob_ref[...] = out.astype(BF16)

    row = lambda i: (i, 0)
    kdim = dz.shape[1]
    return pl.pallas_call(
        body, name=name, grid=(s // ts,),
        in_specs=[pl.BlockSpec((ts, kdim), row), ANY, pl.BlockSpec((ts, D), row),
                  pl.BlockSpec((1, D), lambda i: (0, 0)), pl.BlockSpec((ts, D), row), ANY],
        out_specs=[pl.BlockSpec((ts, D), row), pl.BlockSpec((ts, D), row), pl.BlockSpec((1, D), lambda i: (0, 0))],
        out_shape=[jax.ShapeDtypeStruct((s, D), F32), jax.ShapeDtypeStruct((s, D), BF16),
                   jax.ShapeDtypeStruct((1, D), F32)],
        scratch_shapes=[pltpu.VMEM(w_vmem, BF16), pltpu.SemaphoreType.DMA((n_sems,))],
        compiler_params=_cp("arbitrary"),
    )(dz, w, x, g, resid, after)


def _matmul_tn(a, b, name, tn, shard_major=False, tm=None):
    s, m = a.shape
    n = b.shape[1]
    tm = m if tm is None else tm
    ni, nj = m // tm, n // tn

    def body(a_ref, b_ref, o_ref):
        o_ref[...] = _dot_tn(a_ref[...], b_ref[...]).astype(BF16)

    if shard_major:
        out_spec = pl.BlockSpec((None, tm, tn), lambda i, j: (j, i, 0))
        out_shape = jax.ShapeDtypeStruct((nj, m, tn), BF16)
    else:
        out_spec = pl.BlockSpec((tm, tn), lambda i, j: (i, j))
        out_shape = jax.ShapeDtypeStruct((m, n), BF16)
    return pl.pallas_call(
        body, name=name, grid=(ni, nj),
        in_specs=[pl.BlockSpec((s, tm), lambda i, j: (0, i)), pl.BlockSpec((s, tn), lambda i, j: (0, j))],
        out_specs=out_spec, out_shape=out_shape,
        compiler_params=_cp("arbitrary", "arbitrary"),
    )(a, b)


def _pool_fwd(zr, wgrp, scale):
    s = zr.shape[0]

    def body(u_ref, w_ref, sc_ref, p_ref, pp_ref):
        row = _rows((s, 128))
        for gi, win in enumerate(POOL_WINDOWS):
            cs = slice(gi * 128, (gi + 1) * 128)
            u = u_ref[:, cs].astype(F32)
            acc, k = u, 1
            while k < win:
                acc = acc + jnp.where(row >= k, pltpu.roll(acc, k, 0), 0.0)
                k *= 2
            cnt = jnp.minimum(row + 1, win).astype(F32)
            p = (acc / cnt - u).astype(BF16)
            p_ref[:, cs] = p
            pp_ref[:, cs] = (_dot(p, w_ref[gi].astype(BF16)) * sc_ref[:, cs]).astype(BF16)

    return pl.pallas_call(
        body, name="pool_fwd", grid=(1,),
        in_specs=[pl.BlockSpec((s, POOL_W), lambda i: (0, OFF_POOL // POOL_W)),
                  pl.BlockSpec((4, 128, 128), lambda i: (0, 0, 0)), pl.BlockSpec((1, POOL_W), lambda i: (0, 0))],
        out_specs=[pl.BlockSpec((s, POOL_W), lambda i: (0, 0))] * 2,
        out_shape=[jax.ShapeDtypeStruct((s, POOL_W), BF16)] * 2,
        compiler_params=_cp("arbitrary"),
    )(zr, wgrp, scale)


def _pool_bwd(p, dpp, wgrp, scale, after, dz):
    s = p.shape[0]

    def body(p_ref, dpp_ref, w_ref, sc_ref, after_ref, dz_in, dz_ref, dw_ref, dsc_ref):
        row = _rows((s, 128))
        for gi, win in enumerate(POOL_WINDOWS):
            cs = slice(gi * 128, (gi + 1) * 128)
            pv = p_ref[:, cs]
            wb = w_ref[gi].astype(BF16)
            dpp_v = dpp_ref[:, cs].astype(F32)
            dsc_ref[:, cs] = jnp.sum(dpp_v * _dot(pv, wb), axis=0, keepdims=True)
            dpm = (dpp_v * sc_ref[:, cs]).astype(BF16)
            dw_ref[gi] = _dot_tn(pv, dpm)
            dp = _dot_nt(dpm, wb)
            cnt = jnp.minimum(row + 1, win).astype(F32)
            acc, k = dp / cnt, 1
            while k < win:
                acc = acc + jnp.where(row < s - k, pltpu.roll(acc, s - k, 0), 0.0)
                k *= 2
            dz_ref[:, cs] = (acc - dp).astype(BF16)

    full = lambda i: (0, 0)
    return pl.pallas_call(
        body, name="pool_bwd", grid=(1,),
        in_specs=[pl.BlockSpec((s, POOL_W), full), pl.BlockSpec((s, POOL_W), full),
                  pl.BlockSpec((4, 128, 128), lambda i: (0, 0, 0)), pl.BlockSpec((1, POOL_W), full), ANY, ANY],
        out_specs=[pl.BlockSpec((s, POOL_W), lambda i: (0, OFF_POOL // POOL_W)),
                   pl.BlockSpec((4, 128, 128), lambda i: (0, 0, 0)), pl.BlockSpec((1, POOL_W), full)],
        out_shape=[jax.ShapeDtypeStruct(dz.shape, BF16), jax.ShapeDtypeStruct((4, 128, 128), F32),
                   jax.ShapeDtypeStruct((1, POOL_W), F32)],
        input_output_aliases={5: 0},
        compiler_params=_cp("arbitrary"),
    )(p, dpp, wgrp, scale, after, dz)


def _gla_decay(zgk_ref, wgk_ref, bgk_ref, rb):
    g = _dot(zgk_ref[...], wgk_ref[...].astype(BF16)) + bgk_ref[...]
    la = (jnp.minimum(g, 0.0) - jnp.log(1.0 + jnp.exp(-jnp.abs(g)))) * (1.0 / 16.0)
    rowm = _rows(la.shape) & (CHUNK - 1)
    bc, k = la, 1
    while k < CHUNK:
        bc = bc + jnp.where(rowm >= k, pltpu.roll(bc, k, 0), 0.0)
        k *= 2
    return g, jnp.exp(bc), jnp.exp(-bc)


GLA_HB = 4


def _gla_specs(rb, rmap):
    wk, wv = GLA_HB * HK, GLA_HB * HV
    return [pl.BlockSpec((rb, wk), lambda h, r: (rmap(h, r), OFF_Q // wk + h)),
            pl.BlockSpec((rb, wk), lambda h, r: (rmap(h, r), OFF_K // wk + h)),
            pl.BlockSpec((rb, wv), lambda h, r: (rmap(h, r), OFF_V // wv + h)),
            pl.BlockSpec((rb, 128), lambda h, r: (rmap(h, r), OFF_GK // 128))]


def _gla_fwd(zr, wgk, bgk, ghead, rb):
    s = zr.shape[0]
    nc = rb // CHUNK
    wk, wv = GLA_HB * HK, GLA_HB * HV

    def body(q_ref, k_ref, v_ref, zgk_ref, zog_ref, wgk_ref, bgk_ref, gh_ref, o_ref, og_ref, sp_ref, st_ref, kv_ref):
        @pl.when(pl.program_id(1) == 0)
        def _():
            st_ref[...] = jnp.zeros_like(st_ref)

        _, e_pos, e_neg = _gla_decay(zgk_ref, wgk_ref, bgk_ref, rb)
        lower = _rows((CHUNK, CHUNK)) >= lax.broadcasted_iota(jnp.int32, (CHUNK, CHUNK), 1)
        pairs = [(c, hh) for c in range(nc) for hh in range(GLA_HB)]
        rows = lambda c: slice(c * CHUNK, (c + 1) * CHUNK)
        cols_k = lambda hh: slice(hh * HK, (hh + 1) * HK)
        cols_v = lambda hh: slice(hh * HV, (hh + 1) * HV)
        qfws, pms, e_lasts = {}, {}, {}
        for c, hh in pairs:
            q = q_ref[rows(c), cols_k(hh)].astype(F32) * QSCALE
            k = k_ref[rows(c), cols_k(hh)].astype(F32)
            ec, fc = e_pos[rows(c), cols_k(hh)], e_neg[rows(c), cols_k(hh)]
            qfw = (q * ec).astype(BF16)
            kfw_f = k * fc
            s_fw = _dot_nt(qfw, kfw_f.astype(BF16))
            s_bw = _dot_nt((q * fc).astype(BF16), (k * ec).astype(BF16))
            e_last = _pick_row(ec, CHUNK - 1)
            kv_ref[c, hh] = _dot_tn(v_ref[rows(c), cols_v(hh)], (kfw_f * e_last).astype(BF16))
            qfws[c, hh], pms[c, hh], e_lasts[c, hh] = qfw, jnp.where(lower, s_fw, s_bw).astype(BF16), e_last
        for hh in range(GLA_HB):
            st = st_ref[hh]
            for c in range(nc):
                sp_ref[c, hh] = st.astype(BF16)
                st = st * e_lasts[c, hh] + kv_ref[c, hh]
            st_ref[hh] = st
        for c, hh in pairs:
            o = _dot(pms[c, hh], v_ref[rows(c), cols_v(hh)]) + _dot_nt(qfws[c, hh], sp_ref[c, hh])
            r = lax.rsqrt(jnp.mean(o * o, axis=-1, keepdims=True) + EPS)
            zo = zog_ref[rows(c), cols_v(hh)].astype(F32)
            o_ref[rows(c), cols_v(hh)] = o.astype(BF16)
            og_ref[rows(c), cols_v(hh)] = (o * r * gh_ref[...] * zo * _sigmoid(zo)).astype(BF16)

    rmap = lambda h, r: r
    return pl.pallas_call(
        body, name="gla_fwd", grid=(HEADS // GLA_HB, s // rb),
        in_specs=_gla_specs(rb, rmap) + [
            pl.BlockSpec((rb, wv), lambda h, r: (r, OFF_OG // wv + h)),
            pl.BlockSpec((128, wk), lambda h, r: (0, h)), pl.BlockSpec((1, wk), lambda h, r: (0, h)),
            pl.BlockSpec((1, HV), lambda h, r: (0, 0))],
        out_specs=[pl.BlockSpec((rb, wv), lambda h, r: (r, h)), pl.BlockSpec((rb, wv), lambda h, r: (r, h)),
                   pl.BlockSpec((nc, GLA_HB, HV, HK), lambda h, r: (r, h, 0, 0))],
        out_shape=[jax.ShapeDtypeStruct((s, D), BF16), jax.ShapeDtypeStruct((s, D), BF16),
                   jax.ShapeDtypeStruct((s // CHUNK, HEADS, HV, HK), BF16)],
        scratch_shapes=[pltpu.VMEM((GLA_HB, HV, HK), F32), pltpu.VMEM((nc, GLA_HB, HV, HK), F32)],
        compiler_params=_cp("arbitrary", "arbitrary"),
    )(zr, zr, zr, zr, zr, wgk, bgk, ghead)


def _gla_bwd(zr, do, sp, wgk, bgk, after, dz, rb):
    s = zr.shape[0]
    nc = rb // CHUNK
    nr = s // rb
    wk, wv = GLA_HB * HK, GLA_HB * HV

    def body(q_ref, k_ref, v_ref, zgk_ref, do_ref, sp_ref, wgk_ref, bgk_ref, after_ref, dz_in, dq_ref, dk_ref, dv_ref,
             dg_ref, gt_ref, dbc_ref, gs_ref):
        @pl.when(pl.program_id(1) == 0)
        def _():
            gt_ref[...] = jnp.zeros_like(gt_ref)

        g, e_pos, e_neg = _gla_decay(zgk_ref, wgk_ref, bgk_ref, rb)
        lower = _rows((CHUNK, CHUNK)) >= lax.broadcasted_iota(jnp.int32, (CHUNK, CHUNK), 1)
        is_last = _rows((CHUNK, HK)) == CHUNK - 1
        pairs = [(c, hh) for c in range(nc) for hh in range(GLA_HB)]
        rows = lambda c: slice(c * CHUNK, (c + 1) * CHUNK)
        cols_k = lambda hh: slice(hh * HK, (hh + 1) * HK)
        cols_v = lambda hh: slice(hh * HV, (hh + 1) * HV)
        e_lasts = {}
        for c, hh in pairs:
            ec = e_pos[rows(c), cols_k(hh)]
            qfw = (q_ref[rows(c), cols_k(hh)].astype(F32) * QSCALE * ec).astype(BF16)
            gs_ref[c, hh] = _dot_tn(do_ref[rows(c), cols_v(hh)], qfw)
            e_lasts[c, hh] = _pick_row(ec, CHUNK - 1)
        for hh in range(GLA_HB):
            gt = gt_ref[hh]
            for c in reversed(range(nc)):
                own = gs_ref[c, hh]
                gs_ref[c, hh] = gt
                gt = own + gt * e_lasts[c, hh]
            gt_ref[hh] = gt
        def decayed(c, hh):
            q = q_ref[rows(c), cols_k(hh)].astype(F32) * QSCALE
            k = k_ref[rows(c), cols_k(hh)].astype(F32)
            ec, fc = e_pos[rows(c), cols_k(hh)], e_neg[rows(c), cols_k(hh)]
            return ec, fc, q * ec, k * fc, q * fc, k * ec

        pms, dss = {}, {}
        for c, hh in pairs:
            _, _, qfw_f, kfw_f, qbw_f, kbw_f = decayed(c, hh)
            s_fw = _dot_nt(qfw_f.astype(BF16), kfw_f.astype(BF16))
            s_bw = _dot_nt(qbw_f.astype(BF16), kbw_f.astype(BF16))
            dp = _dot_nt(do_ref[rows(c), cols_v(hh)], v_ref[rows(c), cols_v(hh)])
            pms[c, hh] = jnp.where(lower, s_fw, s_bw).astype(BF16)
            dss[c, hh] = (jnp.where(lower, dp, 0.0).astype(BF16), jnp.where(lower, 0.0, dp).astype(BF16))
        for c, hh in pairs:
            sl, ck, cv = rows(c), cols_k(hh), cols_v(hh)
            v = v_ref[sl, cv]
            dov = do_ref[sl, cv]
            ec, fc, qfw_f, kfw_f, qbw_f, kbw_f = decayed(c, hh)
            qfw, kfw, qbw, kbw = qfw_f.astype(BF16), kfw_f.astype(BF16), qbw_f.astype(BF16), kbw_f.astype(BF16)
            pm = pms[c, hh]
            e_last = e_lasts[c, hh]
            kdec = (kfw_f * e_last).astype(BF16)
            gt = gs_ref[c, hh]
            gtb = gt.astype(BF16)
            spv = sp_ref[c, hh]
            dv_ref[sl, cv] = (_dot_tn(pm, dov) + _dot_nt(kdec, gtb)).astype(BF16)
            ds_fw, ds_bw = dss[c, hh]
            dqfw = _dot(ds_fw, kfw) + _dot(dov, spv)
            dkfw = _dot_tn(ds_fw, qfw)
            dqbw = _dot(ds_bw, kbw)
            dkbw = _dot_tn(ds_bw, qbw)
            dkdec = _dot(v, gtb)
            de_last = (jnp.sum(gt * spv.astype(F32), axis=0, keepdims=True)
                       + jnp.sum(dkdec * kfw_f, axis=0, keepdims=True))
            dkfw = dkfw + dkdec * e_last
            dq_ref[sl, ck] = ((dqfw * ec + dqbw * fc) * QSCALE).astype(BF16)
            dk_ref[sl, ck] = (dkfw * fc + dkbw * ec).astype(BF16)
            dbc = dqfw * qfw_f - dqbw * qbw_f + dkbw * kbw_f - dkfw * kfw_f
            dbc_ref[sl, ck] = dbc + jnp.where(is_last, de_last * e_last, 0.0)
        rowm = _rows((rb, wk)) & (CHUNK - 1)
        dla, kk = dbc_ref[...], 1
        while kk < CHUNK:
            dla = dla + jnp.where(rowm < CHUNK - kk, pltpu.roll(dla, rb - kk, 0), 0.0)
            kk *= 2
        dg_ref[...] = dla * (1.0 / 16.0) * _sigmoid(-g)

    rmap = lambda h, r: nr - 1 - r
    rev = lambda h, r: (nr - 1 - r, h)
    return pl.pallas_call(
        body, name="gla_bwd", grid=(HEADS // GLA_HB, nr),
        in_specs=_gla_specs(rb, rmap) + [
            pl.BlockSpec((rb, wv), rev),
            pl.BlockSpec((nc, GLA_HB, HV, HK), lambda h, r: (nr - 1 - r, h, 0, 0)),
            pl.BlockSpec((128, wk), lambda h, r: (0, h)), pl.BlockSpec((1, wk), lambda h, r: (0, h)), ANY, ANY],
        out_specs=[pl.BlockSpec((rb, wk), rev), pl.BlockSpec((rb, wk), rev),
                   pl.BlockSpec((rb, wv), lambda h, r: (nr - 1 - r, OFF_V // wv + h)), pl.BlockSpec((rb, wk), rev)],
        out_shape=[jax.ShapeDtypeStruct((s, HEADS * HK), BF16), jax.ShapeDtypeStruct((s, HEADS * HK), BF16),
                   jax.ShapeDtypeStruct(dz.shape, BF16), jax.ShapeDtypeStruct((s, HEADS * HK), F32)],
        scratch_shapes=[pltpu.VMEM((GLA_HB, HV, HK), F32), pltpu.VMEM((rb, wk), F32),
                        pltpu.VMEM((nc, GLA_HB, HV, HK), F32)],
        input_output_aliases={9: 2},
        compiler_params=_cp("arbitrary", "arbitrary"),
    )(zr, zr, zr, zr, do, sp, wgk, bgk, after, dz)


def _gk_bwd(dgpre, zr, wgk, after, dz, ts):
    s = zr.shape[0]

    def body(dg_ref, zgk_ref, w_ref, after_ref, dz_in, dz_ref, dw_ref, db_ref):
        @pl.when(pl.program_id(0) == 0)
        def _():
            dw_ref[...] = jnp.zeros_like(dw_ref)
            db_ref[...] = jnp.zeros_like(db_ref)

        dg = dg_ref[...]
        dgb = dg.astype(BF16)
        dz_ref[...] = _dot_nt(dgb, w_ref[...].astype(BF16)).astype(BF16)
        dw_ref[...] += _dot_tn(zgk_ref[...], dgb)
        db_ref[...] += jnp.sum(dg, axis=0, keepdims=True)

    return pl.pallas_call(
        body, name="gk_bwd", grid=(s // ts,),
        in_specs=[pl.BlockSpec((ts, 512), lambda i: (i, 0)), pl.BlockSpec((ts, 128), lambda i: (i, OFF_GK // 128)),
                  pl.BlockSpec((128, 512), lambda i: (0, 0)), ANY, ANY],
        out_specs=[pl.BlockSpec((ts, 128), lambda i: (i, OFF_GK // 128)), pl.BlockSpec((128, 512), lambda i: (0, 0)),
                   pl.BlockSpec((1, 512), lambda i: (0, 0))],
        out_shape=[jax.ShapeDtypeStruct(dz.shape, BF16), jax.ShapeDtypeStruct((128, 512), F32),
                   jax.ShapeDtypeStruct((1, 512), F32)],
        input_output_aliases={4: 0},
        compiler_params=_cp("arbitrary"),
    )(dgpre, zr, wgk, after, dz)


def _merge_fwd(x, zr, pp, og, bgate, wpp, wgla, wout, gffn, after, ts):
    s = x.shape[0]

    def body(x_ref, z0_ref, z1_ref, pp_ref, og_ref, bg_ref, wpp_ref, wgla_ref, wout_ref, gf_ref, after_ref,
             x1_ref, mix_ref, yp_ref, yg_ref, h2_ref):
        ppv = pp_ref[...]
        yp = jnp.concatenate([_dot(ppv, wpp_ref[j]) for j in range(4)], axis=1)
        yg = _dot(og_ref[...], wgla_ref[...])
        g0 = _sigmoid(z0_ref[...].astype(F32) + bg_ref[:, :D])
        g1 = _sigmoid(z1_ref[...].astype(F32) + bg_ref[:, D:])
        mixed = (g0 * yp + g1 * yg).astype(BF16)
        x1 = x_ref[...] + _dot(mixed, wout_ref[...])
        x1_ref[...] = x1
        mix_ref[...] = mixed
        yp_ref[...] = yp.astype(BF16)
        yg_ref[...] = yg.astype(BF16)
        r = lax.rsqrt(jnp.mean(x1 * x1, axis=-1, keepdims=True) + EPS)
        h2_ref[...] = (x1 * r * gf_ref[...]).astype(BF16)

    row = lambda i: (i, 0)
    const2 = lambda i: (0, 0)
    return pl.pallas_call(
        body, name="merge_fwd", grid=(s // ts,),
        in_specs=[pl.BlockSpec((ts, D), row), pl.BlockSpec((ts, D), lambda i: (i, 0)), pl.BlockSpec((ts, D), lambda i: (i, 1)),
                  pl.BlockSpec((ts, POOL_W), row), pl.BlockSpec((ts, D), row), pl.BlockSpec((1, 2 * D), const2),
                  pl.BlockSpec((4, POOL_W, 256), lambda i: (0, 0, 0)), pl.BlockSpec((D, D), const2),
                  pl.BlockSpec((D, D), const2), pl.BlockSpec((1, D), const2), ANY],
        out_specs=[pl.BlockSpec((ts, D), row)] * 5,
        out_shape=[jax.ShapeDtypeStruct((s, D), F32)] + [jax.ShapeDtypeStruct((s, D), BF16)] * 4,
        compiler_params=_cp("arbitrary"),
    )(x, zr, zr, pp, og, bgate, wpp, wgla, wout, gffn, after)


def _merge_bwd(dx1b, zr, yp, yg, o, bgate, ghead, wpp, wgla, wout, after, ts):
    s = dx1b.shape[0]

    def body(dx_ref, z0_ref, z1_ref, zog_ref, yp_ref, yg_ref, o_ref, bg_ref, gh_ref, wpp_ref, wgla_ref, wout_ref, after_ref,
             dzg_ref, dyp_ref, dyg_ref, dpp_ref, do_ref, dzog_ref, dbg_ref, dgh_ref):
        @pl.when(pl.program_id(0) == 0)
        def _():
            dbg_ref[...] = jnp.zeros_like(dbg_ref)
            dgh_ref[...] = jnp.zeros_like(dgh_ref)

        dmix = _dot_nt(dx_ref[...], wout_ref[...])
        g0 = _sigmoid(z0_ref[...].astype(F32) + bg_ref[:, :D])
        g1 = _sigmoid(z1_ref[...].astype(F32) + bg_ref[:, D:])
        dypb = (dmix * g0).astype(BF16)
        dygb = (dmix * g1).astype(BF16)
        dz0 = dmix * yp_ref[...].astype(F32) * g0 * (1.0 - g0)
        dz1 = dmix * yg_ref[...].astype(F32) * g1 * (1.0 - g1)
        dzg_ref[:, :D] = dz0.astype(BF16)
        dzg_ref[:, D:] = dz1.astype(BF16)
        dbg_ref[:, :D] += jnp.sum(dz0, axis=0, keepdims=True)
        dbg_ref[:, D:] += jnp.sum(dz1, axis=0, keepdims=True)
        dyp_ref[...] = dypb
        dyg_ref[...] = dygb
        dpp = _dot_nt(dypb[:, 0:256], wpp_ref[0])
        for j in range(1, 4):
            dpp = dpp + _dot_nt(dypb[:, j * 256:(j + 1) * 256], wpp_ref[j])
        dpp_ref[...] = dpp.astype(BF16)
        dog = _dot_nt(dygb, wgla_ref[...])
        gh = gh_ref[...]
        dgh = jnp.zeros((1, HV), F32)
        for h in range(HEADS):
            cs = slice(h * HV, (h + 1) * HV)
            ov = o_ref[:, cs].astype(F32)
            r = lax.rsqrt(jnp.mean(ov * ov, axis=-1, keepdims=True) + EPS)
            oh = ov * r
            zo = zog_ref[:, cs].astype(F32)
            sg = _sigmoid(zo)
            dog_h = dog[:, cs]
            don = dog_h * zo * sg
            dzog_ref[:, cs] = (dog_h * oh * gh * sg * (1.0 + zo * (1.0 - sg))).astype(BF16)
            dgh = dgh + jnp.sum(don * oh, axis=0, keepdims=True)
            doh = don * gh
            do_ref[:, cs] = (r * (doh - oh * jnp.mean(doh * oh, axis=-1, keepdims=True))).astype(BF16)
        dgh_ref[...] += dgh

    row = lambda i: (i, 0)
    const2 = lambda i: (0, 0)
    return pl.pallas_call(
        body, name="merge_bwd", grid=(s // ts,),
        in_specs=[pl.BlockSpec((ts, D), row), pl.BlockSpec((ts, D), lambda i: (i, 0)), pl.BlockSpec((ts, D), lambda i: (i, 1)),
                  pl.BlockSpec((ts, D), lambda i: (i, OFF_OG // D)), pl.BlockSpec((ts, D), row), pl.BlockSpec((ts, D), row),
                  pl.BlockSpec((ts, D), row), pl.BlockSpec((1, 2 * D), const2), pl.BlockSpec((1, HV), const2),
                  pl.BlockSpec((4, POOL_W, 256), lambda i: (0, 0, 0)), pl.BlockSpec((D, D), const2),
                  pl.BlockSpec((D, D), const2), ANY],
        out_specs=[pl.BlockSpec((ts, 2 * D), row), pl.BlockSpec((ts, D), row), pl.BlockSpec((ts, D), row),
                   pl.BlockSpec((ts, POOL_W), row), pl.BlockSpec((ts, D), row), pl.BlockSpec((ts, D), row),
                   pl.BlockSpec((1, 2 * D), const2), pl.BlockSpec((1, HV), const2)],
        out_shape=[jax.ShapeDtypeStruct((s, N_INR), BF16), jax.ShapeDtypeStruct((s, D), BF16),
                   jax.ShapeDtypeStruct((s, D), BF16), jax.ShapeDtypeStruct((s, POOL_W), BF16),
                   jax.ShapeDtypeStruct((s, D), BF16), jax.ShapeDtypeStruct((s, D), BF16),
                   jax.ShapeDtypeStruct((1, 2 * D), F32), jax.ShapeDtypeStruct((1, HV), F32)],
        compiler_params=_cp("arbitrary"),
    )(dx1b, zr, zr, zr, yp, yg, o, bgate, ghead, wpp, wgla, wout, after)


HALO = 16
CCH = D_FF // 2


def _conv_taps(u_ref, halo_ref, cs, first, ts):
    u = u_ref[:, cs].astype(F32)
    hal = halo_ref[:, cs].astype(F32)
    h1 = jnp.where(first, 0.0, _pick_row(hal, HALO - 1))
    h2 = jnp.where(first, 0.0, _pick_row(hal, HALO - 2))
    row8 = _rows((8, u.shape[1]))
    r1, r2 = pltpu.roll(u, 1, 0), pltpu.roll(u, 2, 0)
    r1 = jnp.concatenate([jnp.where(row8 == 0, h1, r1[:8]), r1[8:]], axis=0)
    r2 = jnp.concatenate([jnp.where(row8 == 0, h2, jnp.where(row8 == 1, h1, r2[:8])), r2[8:]], axis=0)
    return u, r1, r2


def _ffn_down_loss(u, x1, tgt, wconv, bconv, wdown, gfin, ts):
    s = x1.shape[0]

    def body(u_ref, halo_ref, x1_ref, t_ref, wc_ref, bc_ref, wd_ref, gf_ref, a_ref, c_ref, dx_ref, dxb_ref, ls_ref,
             dgf_ref):
        i = pl.program_id(0)

        @pl.when(i == 0)
        def _():
            ls_ref[...] = jnp.zeros_like(ls_ref)
            dgf_ref[...] = jnp.zeros_like(dgf_ref)

        first = i == 0
        acc = x1_ref[...]
        for hf in range(D_FF // CCH):
            cg = slice(hf * CCH, (hf + 1) * CCH)
            cv = slice(D_FF + hf * CCH, D_FF + (hf + 1) * CCH)
            vals = []
            for cs in (cg, cv):
                u0, u1, u2 = _conv_taps(u_ref, halo_ref, cs, first, ts)
                vals.append(bc_ref[:, cs] + wc_ref[0:1, cs] * u2 + wc_ref[1:2, cs] * u1 + wc_ref[2:3, cs] * u0)
                c_ref[:, cs] = vals[-1].astype(BF16)
            a = (vals[0] * _sigmoid(vals[0]) * vals[1]).astype(BF16)
            a_ref[:, cg] = a
            acc = acc + _dot(a, wd_ref[cg, :])
        r = lax.rsqrt(jnp.mean(acc * acc, axis=-1, keepdims=True) + EPS)
        xh = acc * r
        gf = gf_ref[...]
        err = xh * gf - t_ref[...]
        ls_ref[...] += (0.5 / D) * jnp.sum(jnp.sum(err * err, axis=-1, keepdims=True), axis=0, keepdims=True)
        dy = err * (1.0 / D)
        dgf_ref[...] += jnp.sum(dy * xh, axis=0, keepdims=True)
        dxh = dy * gf
        dx = r * (dxh - xh * jnp.mean(dxh * xh, axis=-1, keepdims=True))
        dx_ref[...] = dx
        dxb_ref[...] = dx.astype(BF16)

    row = lambda i: (i, 0)
    const2 = lambda i: (0, 0)
    return pl.pallas_call(
        body, name="ffn_down_loss", grid=(s // ts,),
        in_specs=[pl.BlockSpec((ts, N_UP), row),
                  pl.BlockSpec((HALO, N_UP), lambda i: (jnp.maximum(i * (ts // HALO) - 1, 0), 0)),
                  pl.BlockSpec((ts, D), row), pl.BlockSpec((ts, D), row), pl.BlockSpec((3, N_UP), const2),
                  pl.BlockSpec((1, N_UP), const2), pl.BlockSpec((D_FF, D), const2), pl.BlockSpec((1, D), const2)],
        out_specs=[pl.BlockSpec((ts, D_FF), row), pl.BlockSpec((ts, N_UP), row), pl.BlockSpec((ts, D), row),
                   pl.BlockSpec((ts, D), row), pl.BlockSpec((1, 128), const2), pl.BlockSpec((1, D), const2)],
        out_shape=[jax.ShapeDtypeStruct((s, D_FF), BF16), jax.ShapeDtypeStruct((s, N_UP), BF16),
                   jax.ShapeDtypeStruct((s, D), F32), jax.ShapeDtypeStruct((s, D), BF16),
                   jax.ShapeDtypeStruct((1, 128), F32), jax.ShapeDtypeStruct((1, D), F32)],
        compiler_params=_cp("arbitrary"),
    )(u, u, x1, tgt, wconv, bconv, wdown, gfin)


def _ffn_bwd(dx2b, u, c, wconv, wdown, ts):
    s = dx2b.shape[0]
    nt = s // ts

    def body(dx_ref, u_ref, c_ref, wc_ref, wd_ref, du_ref, db_ref, dw_ref, nxt_ref):
        @pl.when(pl.program_id(0) == 0)
        def _():
            db_ref[...] = jnp.zeros_like(db_ref)
            dw_ref[...] = jnp.zeros_like(dw_ref)
            nxt_ref[...] = jnp.zeros_like(nxt_ref)

        dxv = dx_ref[...]
        row8 = _rows((8, CCH))
        for hf in range(D_FF // CCH):
            cg = slice(hf * CCH, (hf + 1) * CCH)
            cv = slice(D_FF + hf * CCH, D_FF + (hf + 1) * CCH)
            da = _dot_nt(dxv, wd_ref[cg, :])
            gate = c_ref[:, cg].astype(F32)
            val = c_ref[:, cv].astype(F32)
            sg = _sigmoid(gate)
            dcs = (da * val * sg * (1.0 + gate * (1.0 - sg)), da * gate * sg)
            for cs, dc in zip((cg, cv), dcs):
                n1 = nxt_ref[0:1, cs]
                n2 = nxt_ref[1:2, cs]
                r1, r2 = pltpu.roll(dc, ts - 1, 0), pltpu.roll(dc, ts - 2, 0)
                f1 = jnp.concatenate([r1[:ts - 8], jnp.where(row8 == 7, n1, r1[ts - 8:])], axis=0)
                f2 = jnp.concatenate([r2[:ts - 8], jnp.where(row8 == 7, n2, jnp.where(row8 == 6, n1, r2[ts - 8:]))], axis=0)
                uv = u_ref[:, cs].astype(F32)
                db_ref[:, cs] += jnp.sum(dc, axis=0, keepdims=True)
                dw_ref[0:1, cs] += jnp.sum(f2 * uv, axis=0, keepdims=True)
                dw_ref[1:2, cs] += jnp.sum(f1 * uv, axis=0, keepdims=True)
                dw_ref[2:3, cs] += jnp.sum(dc * uv, axis=0, keepdims=True)
                du_ref[:, cs] = (wc_ref[2:3, cs] * dc + wc_ref[1:2, cs] * f1 + wc_ref[0:1, cs] * f2).astype(BF16)
                nxt_ref[:, cs] = dc[0:8, :]

    rev = lambda i: (nt - 1 - i, 0)
    const2 = lambda i: (0, 0)
    return pl.pallas_call(
        body, name="ffn_bwd", grid=(nt,),
        in_specs=[pl.BlockSpec((ts, D), rev), pl.BlockSpec((ts, N_UP), rev), pl.BlockSpec((ts, N_UP), rev),
                  pl.BlockSpec((3, N_UP), const2), pl.BlockSpec((D_FF, D), const2)],
        out_specs=[pl.BlockSpec((ts, N_UP), rev), pl.BlockSpec((1, N_UP), const2), pl.BlockSpec((3, N_UP), const2)],
        out_shape=[jax.ShapeDtypeStruct((s, N_UP), BF16), jax.ShapeDtypeStruct((1, N_UP), F32),
                   jax.ShapeDtypeStruct((3, N_UP), F32)],
        scratch_shapes=[pltpu.VMEM((8, N_UP), F32)],
        compiler_params=_cp("arbitrary"),
    )(dx2b, u, c, wconv, wdown)


ANY = pl.BlockSpec(memory_space=pl.ANY)


def _place():
    x, y, c = lax.axis_index("x"), lax.axis_index("y"), lax.axis_index("c")
    chips = [(1 - x, y), (x, 1 - y), (1 - x, 1 - y)]
    return x, y, c, chips


def _half(shape, c, axis):
    size = shape[axis] // 2
    cut = pl.ds(pl.multiple_of(c * size, 8 if axis == 0 else 128), size)
    return (cut, slice(None)) if axis == 0 else (slice(None), cut)


def _half_shape(shape, axis):
    return (shape[0] // 2, shape[1]) if axis == 0 else (shape[0], shape[1] // 2)


def _remote(src, dst, send_sems, recv_sems, k, to):
    return pltpu.make_async_remote_copy(src_ref=src, dst_ref=dst, send_sem=send_sems.at[k], recv_sem=recv_sems.at[k],
                                        device_id=to, device_id_type=MESH)


def _sibling_exchange(grads, axes, smalls, name):
    nb = len(grads)
    n = nb + len(smalls)

    def body(*refs):
        ins, outs = refs[:n], refs[n:2 * n]
        send_sems, recv_sems = refs[2 * n:]
        x, y, c, _ = _place()
        sib = (x, y, 1 - c)
        cps = []
        for a in range(nb):
            theirs = _half(grads[a].shape[1:], 1 - c, axes[a])
            cps.append(_remote(ins[a].at[(slice(None),) + theirs], outs[a], send_sems, recv_sems, a, sib))
        for a in range(nb, n):
            cps.append(_remote(ins[a], outs[a], send_sems, recv_sems, a, sib))
        for cp in cps:
            cp.start()
        for cp in cps:
            cp.wait()

    out_shape = [jax.ShapeDtypeStruct((4,) + _half_shape(g.shape[1:], ax), g.dtype) for g, ax in zip(grads, axes)]
    out_shape += [jax.ShapeDtypeStruct(a.shape, F32) for a in smalls]
    return pl.pallas_call(
        body, name=name, in_specs=[ANY] * n, out_specs=[ANY] * n, out_shape=out_shape,
        scratch_shapes=[pltpu.SemaphoreType.DMA((n,)), pltpu.SemaphoreType.DMA((n,))],
        compiler_params=pltpu.CompilerParams(has_side_effects=True),
    )(*grads, *smalls)


def _gather_share(lands, axes, name):
    n = len(lands)

    def body(*refs):
        outs = refs[n:2 * n]
        send_sems, recv_sems = refs[2 * n:]
        x, y, c, chips = _place()
        sib = (x, y, 1 - c)
        cps = []
        for a in range(n):
            mine = _half(lands[a].shape[1:], c, axes[a])
            for k, ch in enumerate(chips):
                landed = outs[a].at[(2 * ch[0] + ch[1],) + mine]
                cps.append(_remote(landed, landed, send_sems, recv_sems, 3 * a + k, sib))
        for cp in cps:
            cp.start()
        for a in range(n):
            other = _half(lands[a].shape[1:], 1 - c, axes[a])
            for k, ch in enumerate(chips):
                landed = outs[a].at[(2 * ch[0] + ch[1],) + other]
                _remote(landed, landed, send_sems, recv_sems, 3 * a + k, sib).wait_recv()
        for cp in cps:
            cp.wait_send()

    return pl.pallas_call(
        body, name=name, in_specs=[ANY] * n, out_specs=[ANY] * n,
        out_shape=[jax.ShapeDtypeStruct(a.shape, a.dtype) for a in lands],
        input_output_aliases={a: a for a in range(n)},
        scratch_shapes=[pltpu.SemaphoreType.DMA((3 * n,)), pltpu.SemaphoreType.DMA((3 * n,))],
        compiler_params=pltpu.CompilerParams(has_side_effects=True),
    )(*lands)


def _sibling_share(halves, name):
    n = len(halves)

    def body(*refs):
        ins, outs = refs[:n], refs[n:2 * n]
        send_sems, recv_sems = refs[2 * n:]
        x, y, c, _ = _place()
        cps = [_remote(ins[a], outs[a], send_sems, recv_sems, a, (x, y, 1 - c)) for a in range(n)]
        for cp in cps:
            cp.start()
        for cp in cps:
            cp.wait()

    return pl.pallas_call(
        body, name=name, in_specs=[ANY] * n, out_specs=[ANY] * n,
        out_shape=[jax.ShapeDtypeStruct(h.shape, F32) for h in halves],
        scratch_shapes=[pltpu.SemaphoreType.DMA((n,)), pltpu.SemaphoreType.DMA((n,))],
        compiler_params=pltpu.CompilerParams(has_side_effects=True),
    )(*halves)


HBM = pl.BlockSpec(memory_space=pltpu.HBM)
SEM = pl.BlockSpec(memory_space=pltpu.SEMAPHORE)
DATAFLOW = pltpu.SideEffectType.DATAFLOW_SIDE_EFFECTING


def _split_start(name, srcs, land_shapes, plan, n_copies, after):
    lands = [lax.empty(*ls) if isinstance(ls, tuple) else ls for ls in land_shapes]
    bufs = list(srcs) + lands
    nb, ns = len(bufs), len(srcs)

    def body(*refs):
        send_sems, recv_sems, token = refs[nb + 1], refs[nb + 2], refs[-1]
        for k, (src, dst, to) in enumerate(plan(refs[:ns], refs[ns:nb])):
            _remote(src, dst, send_sems, recv_sems, k, to).start()
        token[...] = jnp.zeros_like(token)

    res = pl.pallas_call(
        body, name=name,
        out_shape=(pltpu.SemaphoreType.DMA((n_copies,)), pltpu.SemaphoreType.DMA((n_copies,)),
                   *[pltpu.HBM(b.shape, b.dtype) for b in bufs], jax.ShapeDtypeStruct((8, 128), F32)),
        in_specs=[HBM] * nb + [ANY],
        out_specs=(SEM, SEM, *[HBM] * nb, pl.BlockSpec(memory_space=pltpu.VMEM)),
        input_output_aliases={i: 2 + i for i in range(nb)},
        compiler_params=pltpu.CompilerParams(has_side_effects=DATAFLOW),
    )(*[pltpu.with_memory_space_constraint(b, pltpu.HBM) for b in bufs], after)
    return (res[0], res[1], list(res[2:2 + nb])), res[-1]


def _split_relay(name, handle, plan, relay_plan, n_relay, after):
    send_sems, recv_sems, bufs = handle
    nb = len(bufs)

    def body(*refs):
        sends, recvs = refs[nb], refs[nb + 1]
        for k, (src, dst, to) in enumerate(plan((), refs[:nb])):
            _remote(src, dst, sends, recvs, k, to).wait_recv()
        relay_sends, relay_recvs, token = refs[nb + 3], refs[nb + 4], refs[-1]
        for k, (src, dst, to) in enumerate(relay_plan((), refs[:nb])):
            _remote(src, dst, relay_sends, relay_recvs, k, to).start()
        token[...] = jnp.zeros_like(token)

    res = pl.pallas_call(
        body, name=name,
        out_shape=(pltpu.SemaphoreType.DMA((n_relay,)), pltpu.SemaphoreType.DMA((n_relay,)),
                   *[pltpu.HBM(b.shape, b.dtype) for b in bufs], jax.ShapeDtypeStruct((8, 128), F32)),
        in_specs=[HBM] * nb + [SEM, SEM, ANY],
        out_specs=(SEM, SEM, *[HBM] * nb, pl.BlockSpec(memory_space=pltpu.VMEM)),
        input_output_aliases={i: 2 + i for i in range(nb)},
        compiler_params=pltpu.CompilerParams(has_side_effects=DATAFLOW),
    )(*bufs, send_sems, recv_sems, after)
    passed = list(res[2:2 + nb])
    return (send_sems, recv_sems, passed), (res[0], res[1], passed), res[-1]


def _split_wait(name, handle, n_srcs, plan, after, arrived=False, first=0):
    send_sems, recv_sems, bufs = handle
    nb = len(bufs)

    def body(*refs):
        sends, recvs = refs[nb], refs[nb + 1]
        for k, (src, dst, to) in enumerate(plan(refs[:n_srcs], refs[n_srcs:nb])):
            cp = _remote(src, dst, sends, recvs, first + k, to)
            cp.wait_send()
            if arrived:
                continue
            cp.wait_recv()

    res = pl.pallas_call(
        body, name=name, out_shape=[pltpu.HBM(b.shape, b.dtype) for b in bufs],
        in_specs=[HBM] * nb + [SEM, SEM, ANY], out_specs=[HBM] * nb,
        input_output_aliases={i: i for i in range(nb)},
        compiler_params=pltpu.CompilerParams(has_side_effects=DATAFLOW),
    )(*bufs, send_sems, recv_sems, after)
    return list(res[:n_srcs]), list(res[n_srcs:])


def _relay_plan(shapes, axes):
    def plan(srcs, lands):
        x, y, c, _ = _place()
        first = c == 0
        from_x, from_y = jnp.where(first, 1 - x, x), jnp.where(first, y, 1 - y)
        to = (jnp.where(first, x, 1 - x), jnp.where(first, 1 - y, y), c)
        out = []
        for a, (shape, axis) in enumerate(zip(shapes, axes)):
            got = lands[a].at[(2 * from_x + from_y,) + _half(shape, c, axis)]
            out.append((got, got, to))
        return out
    return plan


def _gather_plan(shapes, axes, n_whole=0, relayed=False):
    def plan(srcs, lands):
        x, y, c, chips = _place()
        me = 2 * x + y
        out = []
        for a, (shape, axis) in enumerate(zip(shapes, axes)):
            own = lands[a].at[(me,) + _half(shape, c, axis)]
            for ch in chips[:2] if relayed else chips:
                out.append((own, own, (ch[0], ch[1], c)))
        for a in range(len(shapes), len(shapes) + n_whole):
            for ch in chips:
                out.append((lands[a].at[me], lands[a].at[me], (ch[0], ch[1], c)))
        return out
    return plan


def _share_plan(shapes, axes):
    def plan(srcs, lands):
        x, y, c, chips = _place()
        out = []
        for a, (shape, axis) in enumerate(zip(shapes, axes)):
            mine = _half(shape, c, axis)
            for ch in chips:
                landed = lands[a].at[(2 * ch[0] + ch[1],) + mine]
                out.append((landed, landed, (x, y, 1 - c)))
        return out
    return plan


def _sibling_plan(shapes, axes):
    def plan(srcs, lands):
        x, y, c, _ = _place()
        return [(srcs[a].at[(slice(None),) + _half(shape, 1 - c, axis)], lands[a], (x, y, 1 - c))
                for a, (shape, axis) in enumerate(zip(shapes, axes))]
    return plan


def _whole_to_sibling_plan(n):
    def plan(srcs, lands):
        x, y, c, _ = _place()
        return [(srcs[a], lands[a], (x, y, 1 - c)) for a in range(n)]
    return plan


def _reduce_plan(n_big, n_small):
    def plan(srcs, lands):
        x, y, c, chips = _place()
        out = []
        for a in range(n_big):
            for k, ch in enumerate(chips):
                out.append((srcs[a].at[2 * ch[0] + ch[1]], lands[a].at[k], (ch[0], ch[1], c)))
        for a in range(n_big, n_big + n_small):
            for ch in chips:
                out.append((srcs[a], lands[a].at[2 * x + y], (ch[0], ch[1], c)))
        return out
    return plan


def _row_tile(rows, cols, mult):
    best = mult
    for t in range(mult, rows + 1, mult):
        if rows % t == 0 and t * cols * 4 <= (2 << 20):
            best = t
    return best if rows % best == 0 else rows


COL_TILE = 256


def _half_tiling(hshape, axis, mult):
    hr, hc = hshape
    if axis == 0:
        tr = _row_tile(hr, hc, mult)
        return tr, hc, hr // tr
    return hr, COL_TILE, hc // COL_TILE


def _tile_idx(axis, t):
    return (t, 0) if axis == 0 else (0, t)


def _chip_partial(place, g, t, axis, name):
    hshape = t.shape[1:]
    br, bc, nt = _half_tiling(hshape, axis, 16)

    def body(pl_ref, g_ref, t_ref, pf_ref, pb_ref):
        v = g_ref[...].astype(F32) + t_ref[...].astype(F32)
        pb_ref[...] = v.astype(BF16)

        @pl.when(pl.program_id(1) == pl_ref[0])
        def _():
            pf_ref[...] = v

    blk = (None, br, bc)
    return pl.pallas_call(
        body, name=name,
        grid_spec=pltpu.PrefetchScalarGridSpec(
            num_scalar_prefetch=1, grid=(nt, 4),
            in_specs=[pl.BlockSpec(blk, lambda i, j, p: (j,) + _tile_idx(axis, p[1] * nt + i)),
                      pl.BlockSpec(blk, lambda i, j, p: (j,) + _tile_idx(axis, i))],
            out_specs=[pl.BlockSpec((br, bc), lambda i, j, p: _tile_idx(axis, i)),
                       pl.BlockSpec(blk, lambda i, j, p: (j,) + _tile_idx(axis, i))]),
        out_shape=[jax.ShapeDtypeStruct(hshape, F32), jax.ShapeDtypeStruct((4,) + hshape, BF16)],
        compiler_params=_cp("arbitrary", "arbitrary"),
    )(place, g, t)


def _finish_half(pf, rb, axis, name):
    hshape = pf.shape
    br, bc, nt = _half_tiling(hshape, axis, 16)

    def body(pf_ref, rb_ref, o_ref):
        o_ref[...] = ((pf_ref[...] + rb_ref[0].astype(F32)) + rb_ref[1].astype(F32)) + rb_ref[2].astype(F32)

    return pl.pallas_call(
        body, name=name, grid=(nt,),
        in_specs=[pl.BlockSpec((br, bc), lambda i: _tile_idx(axis, i)),
                  pl.BlockSpec((3, br, bc), lambda i: (0,) + _tile_idx(axis, i))],
        out_specs=pl.BlockSpec((br, bc), lambda i: _tile_idx(axis, i)),
        out_shape=jax.ShapeDtypeStruct(hshape, F32),
        compiler_params=_cp("arbitrary"),
    )(pf, rb)


def _adam_math(w, g, m, v):
    m = ADAM_B1 * m + (1.0 - ADAM_B1) * g
    v = ADAM_B2 * v + (1.0 - ADAM_B2) * (g * g)
    m_hat = m / (1.0 - ADAM_B1 ** ADAM_STEP)
    v_hat = v / (1.0 - ADAM_B2 ** ADAM_STEP)
    return -ADAM_LR * (m_hat / (jnp.sqrt(v_hat) + ADAM_EPS) + ADAM_WD * w), m, v


def _adam_halves(place, w, mine, theirs, m, v, axis, name):
    br, bc, nt = _half_tiling(mine.shape, axis, 8)

    def body(pl_ref, w_ref, a_ref, b_ref, m_ref, v_ref, g_ref, d_ref, mo_ref, vo_ref):
        is_mine = pl.program_id(0) // nt == pl_ref[1]
        g = jnp.where(is_mine, a_ref[...], b_ref[...])
        d, mn, vn = _adam_math(w_ref[...], g, m_ref[...], v_ref[...])
        g_ref[...] = g
        d_ref[...] = d
        mo_ref[...] = mn
        vo_ref[...] = vn

    full = pl.BlockSpec((br, bc), lambda i, p: _tile_idx(axis, i))
    mine_spec = pl.BlockSpec((br, bc), lambda i, p: _tile_idx(axis, jnp.where(i // nt == p[1], i % nt, nt - 1)))
    theirs_spec = pl.BlockSpec((br, bc), lambda i, p: _tile_idx(axis, jnp.where(i // nt == p[1], 0, i % nt)))
    return pl.pallas_call(
        body, name=name,
        grid_spec=pltpu.PrefetchScalarGridSpec(
            num_scalar_prefetch=1, grid=(2 * nt,), in_specs=[full, mine_spec, theirs_spec, full, full],
            out_specs=[full] * 4),
        out_shape=[jax.ShapeDtypeStruct(w.shape, F32)] * 4, compiler_params=_cp("arbitrary"),
    )(place, w, mine, theirs, m, v)


def _add_many(xs, ys, name):
    n = len(xs)

    def body(*refs):
        for i in range(n):
            refs[2 * n + i][...] = refs[i][...] + refs[n + i][...]

    return pl.pallas_call(body, name=name, out_shape=[jax.ShapeDtypeStruct(a.shape, F32) for a in xs])(*xs, *ys)


def _adam_small(place, owns, landed, ws, ms, vs, widths):
    n, nw = len(owns), len(ws)

    def body(pl_ref, *refs):
        own_r, land_r = refs[:n], refs[n:2 * n]
        w_r, m_r, v_r = (refs[2 * n + k * nw:2 * n + (k + 1) * nw] for k in range(3))
        outs = refs[2 * n + 3 * nw:]
        g_o, d_o, m_o, v_o = outs[:n], outs[n:n + nw], outs[n + nw:n + 2 * nw], outs[n + 2 * nw:]
        for me in range(4):
            @pl.when(pl_ref[0] == me)
            def _(me=me):
                for i in range(n):
                    p = [own_r[i][...] if k == me else land_r[i][k] for k in range(4)]
                    g = ((p[0] + p[1]) + p[2]) + p[3]
                    if i < nw and widths[i]:
                        g = g[:, me * widths[i]:(me + 1) * widths[i]]
                    g_o[i][...] = g
                    if i < nw:
                        d, mn, vn = _adam_math(w_r[i][...], g, m_r[i][...], v_r[i][...])
                        d_o[i][...] = d
                        m_o[i][...] = mn
                        v_o[i][...] = vn

    g_shapes = [jax.ShapeDtypeStruct(ws[i].shape if i < nw else owns[i].shape, F32) for i in range(n)]
    w_shapes = [jax.ShapeDtypeStruct(w.shape, F32) for w in ws]
    whole = lambda a: pl.BlockSpec(a.shape, lambda i, p, nd=len(a.shape): (0,) * nd)
    ins = list(owns) + list(landed) + list(ws) + list(ms) + list(vs)
    out_shape = g_shapes + w_shapes * 3
    out = pl.pallas_call(
        body, name="adam_small",
        grid_spec=pltpu.PrefetchScalarGridSpec(num_scalar_prefetch=1, grid=(1,), in_specs=[whole(a) for a in ins],
                                               out_specs=[whole(a) for a in out_shape]),
        out_shape=out_shape, compiler_params=_cp("arbitrary"),
    )(place, *ins)
    return out[:n], out[n:n + nw], out[n + nw:n + 2 * nw], out[n + 2 * nw:]


def kernel(x, g_mix, w_in, b_gate, w_gk_up, b_gk, w_pool_grp, pool_scale, g_gla_head, w_pool_proj, w_gla_proj, w_out, g_ffn, w_up, w_conv, b_conv, w_down, g_final, loss_target, m_g_mix, m_w_in, m_b_gate, m_w_gk_up, m_b_gk, m_w_pool_grp, m_pool_scale, m_g_gla_head, m_w_pool_proj, m_w_gla_proj, m_w_out, m_g_ffn, m_w_up, m_w_conv, m_b_conv, m_w_down, m_g_final, v_g_mix, v_w_in, v_b_gate, v_w_gk_up, v_b_gk, v_w_pool_grp, v_pool_scale, v_g_gla_head, v_w_pool_proj, v_w_gla_proj, v_w_out, v_g_ffn, v_w_up, v_w_conv, v_b_conv, v_w_down, v_g_final):
    s = x.shape[1]
    ts = min(s, 512)
    tm = min(s, 256)
    cx, cy, cc = lax.axis_index("x"), lax.axis_index("y"), lax.axis_index("c")
    chip = 2 * cx + cy
    place = jnp.stack([chip, cc]).astype(jnp.int32)

    big_names = ("w_in", "w_pool_proj", "w_gla_proj", "w_out", "w_up", "w_down")
    axes = (1, 0, 0, 0, 0, 0)
    shards = dict(w_in=jnp.transpose(w_in[0]), w_pool_proj=w_pool_proj[0], w_gla_proj=w_gla_proj[0], w_out=w_out[0],
                  w_up=w_up[0], w_down=w_down[0])
    def landing(own_shards):
        return [lax.dynamic_update_slice(lax.empty((4,) + o_.shape, o_.dtype), o_[None], (chip, 0, 0))
                for o_ in own_shards]

    def gather_start(tag, lands, n_halves, group_axes, after, relayed=False):
        n_whole = len(lands) - n_halves
        plan = _gather_plan([l_.shape[1:] for l_ in lands[:n_halves]], group_axes, n_whole, relayed)
        n_copies = (2 if relayed else 3) * n_halves + 3 * n_whole
        handle, token = _split_start("gather_" + tag + "_start", [], lands, plan, n_copies, after)
        return (handle, plan, n_halves, group_axes), token

    def gather_relay(tag, started, after):
        handle, plan, n_halves, group_axes = started
        relay_plan = _relay_plan([b_.shape[1:] for b_ in handle[2]], group_axes)
        first, relay, token = _split_relay("gather_" + tag + "_relay", handle, plan, relay_plan, n_halves, after)
        return (first, plan, relay, relay_plan, group_axes), token

    def gather_finish_relayed(tag, relayed, after):
        first, plan, relay, relay_plan, group_axes = relayed
        lands = _split_wait("gather_" + tag + "_sent", first, 0, plan, after, arrived=True)[1]
        lands = _split_wait("gather_" + tag + "_wait", (relay[0], relay[1], lands), 0, relay_plan, after)[1]
        return _gather_share(lands, group_axes, "gather_" + tag + "_share")

    in_w, tok = gather_start("in", landing([jnp.transpose(w_in[0].astype(BF16))]), 1, axes[:1], g_mix, relayed=True)
    zero = tok[0, 0]
    own = landing([(shards[n] + zero).astype(BF16) for n in big_names[1:]] + [w_gk_up[0] + zero, w_conv[0] + zero])
    own = lax.optimization_barrier(own)
    in_r, tok = gather_relay("in", in_w, own[4])
    xs, tgt = x[0], loss_target[0]
    h = _rmsnorm(xs, g_mix, tok, "norm_mix", ts)
    m_in_t, v_in_t = jnp.transpose(m_w_in[0]), jnp.transpose(v_w_in[0])
    h, m_in_t, v_in_t = lax.optimization_barrier((h, m_in_t, v_in_t))
    groups = ((own[0:3] + own[5:7], 3, axes[1:4]), (own[3:4], 1, axes[4:5]), (own[4:5], 1, axes[5:6]))
    plans = [_gather_plan([l_.shape[1:] for l_ in ls[:nh]], ax, len(ls) - nh) for ls, nh, ax in groups]
    counts = [3 * len(ls) for ls, _, _ in groups]
    bounds = [sum(len(ls) for ls, _, _ in groups[:i]) for i in range(4)]

    def rest_plan(srcs, lands):
        return [cp for i, pl_ in enumerate(plans) for cp in pl_((), lands[bounds[i]:bounds[i + 1]])]

    (rest_send, rest_recv, rest_bufs), tok = _split_start("gather_rest_start", [], [l_ for ls, _, _ in groups for l_ in ls],
                                                          rest_plan, sum(counts), m_in_t)
    mix_w, up_w, down_w = [((rest_send, rest_recv, rest_bufs[bounds[i]:bounds[i + 1]]), plans[i], groups[i][1], groups[i][2],
                            sum(counts[:i])) for i in range(3)]

    def forward_start(tag, started, after):
        handle, plan, n_halves, group_axes, first = started
        lands = _split_wait("gather_" + tag + "_wait", handle, 0, plan, after, first=first)[1]
        plan = _share_plan([l_.shape[1:] for l_ in lands[:n_halves]], group_axes)
        share, token = _split_start("gather_" + tag + "_share_start", [], lands[:n_halves], plan, 3 * n_halves, after)
        return (share, plan, lands[n_halves:]), token

    def forward_done(tag, forwarded, after):
        share, plan, _ = forwarded
        return _split_wait("gather_" + tag + "_share_wait", share, 0, plan, after)[1]
    wgrp = w_pool_grp[0]
    w_in_t = gather_finish_relayed("in", in_r, tok)[0]
    nsh = N_IN // 4

    zr, w_in_rt = _in_proj(h, w_in_t, PROJ_TILE)
    p, pp = _pool_fwd(zr, wgrp, pool_scale)
    mix_f, tok = forward_start("mix", mix_w, pp)
    wgk4, wconv4 = mix_f[2]
    wgk_full = jnp.transpose(wgk4, (1, 0, 2)).reshape(GATE_RANK, 512) + tok[0, 0]
    wconv_full = jnp.transpose(wconv4, (1, 0, 2)).reshape(3, N_UP)
    wgk_pad = jnp.concatenate([wgk_full, jnp.zeros((128 - GATE_RANK, 512), F32)], axis=0)
    o, og, sp = _gla_fwd(zr, wgk_pad, b_gk, g_gla_head, ts)
    wpp, wgla, wout = forward_done("mix", mix_f, og)
    wgla, wout = wgla.reshape(D, D), wout.reshape(D, D)
    up_f, tok = forward_start("up", up_w, og)
    x1, mixed, yp, yg, h2 = _merge_fwd(xs, zr, pp, og, b_gate, wpp, wgla, wout, g_ffn, tok, ts)
    wup, = forward_done("up", up_f, x1)
    down_f, tok = forward_start("down", down_w, x1)
    u = _matmul_resident(h2, wup, tok, "ffn_up")
    wdown = forward_done("down", down_f, u)[0].reshape(D_FF, D)
    a, conv_out, dx2, dx2b, loss_part, dgfin = _ffn_down_loss(u, x1, tgt, wconv_full, b_conv, wdown,
                                                              g_final.reshape(1, D), tm)

    du, dbconv, dwconv = _ffn_bwd(dx2b, u, conv_out, wconv_full, wdown, tm)
    dw_down = _matmul_tn(a, dx2b, "dw_down", D, tm=D_FF // 2)
    dw_up = _matmul_tn(h2, du, "dw_up", UP_SHARD, shard_major=True)

    def exchange_start(tag, grads, group_axes, after):
        plan = _sibling_plan([g.shape[1:] for g in grads], group_axes)
        lands = [((4,) + _half_shape(g.shape[1:], ax), g.dtype) for g, ax in zip(grads, group_axes)]
        handle, token = _split_start("sibling_" + tag + "_start", grads, lands, plan, len(grads), after)
        return (handle, plan, len(grads)), token

    def partials(tag, names, group_axes, exchange, after):
        handle, plan, n = exchange
        mine, theirs = _split_wait("sibling_" + tag + "_wait", handle, n, plan, after)
        return zip(*[_chip_partial(place, g, t, ax, "chip_partial_" + nm)
                     for nm, ax, g, t in zip(names, group_axes, mine, theirs)])

    ffn_names, ffn_axes = ("w_up", "w_down"), (0, 0)
    ffn_x, token = exchange_start("ffn", [dw_up, dw_down.reshape(4, 704, D)], ffn_axes, du)
    dx1, dx1b, dgffn = _matmul_nt_normbwd(du, wup, x1, g_ffn, dx2, token, "ffn_up_bwd", ts)
    ffn_pf, ffn_pb = partials("ffn", ffn_names, ffn_axes, ffn_x, dx1b)
    ffn_plan = _reduce_plan(2, 0)
    ffn_handle, token = _split_start("reduce_ffn_start", ffn_pb, [((3,) + p.shape[1:], BF16) for p in ffn_pb],
                                     ffn_plan, 6, ffn_pf[0])

    dzr, dyp, dyg, dpp, do, dzog, dbgate, dghead = _merge_bwd(dx1b, zr, yp, yg, o, b_gate, g_gla_head, wpp, wgla, wout,
                                                             token, ts)
    dzr = lax.dynamic_update_slice(dzr, dzog, (0, OFF_OG))
    dw_out = _matmul_tn(mixed, dx1b, "dw_out", D, tm=512)
    dw_gla = _matmul_tn(og, dyg, "dw_gla", D, tm=512)
    dw_pp = _matmul_tn(pp, dyp, "dw_pp", 256, shard_major=True)

    out_names, out_axes = ("w_pool_proj", "w_gla_proj", "w_out"), (0, 0, 0)
    out_x, token = exchange_start("out", [dw_pp, dw_gla.reshape(4, 256, D), dw_out.reshape(4, 256, D)], out_axes, dpp)
    dzr, dwgrp, dscale = _pool_bwd(p, dpp, wgrp, pool_scale, token, dzr)
    out_pf, out_pb = partials("out", out_names, out_axes, out_x, dwgrp)
    out_plan = _reduce_plan(3, 0)
    out_handle, token = _split_start("reduce_out_start", out_pb, [((3,) + p_.shape[1:], BF16) for p_ in out_pb],
                                     out_plan, 9, out_pf[0])
    dq, dk, dzr, dgpre = _gla_bwd(zr, do, sp, wgk_pad, b_gk, token, dzr, ts)
    dzr, dwgk, dbgk = _gk_bwd(dgpre, zr, wgk_pad, dgpre, dzr, ts)
    dzr = lax.dynamic_update_slice(lax.dynamic_update_slice(dzr, dq, (0, OFF_Q)), dk, (0, OFF_K))
    dw_rt = _matmul_tn(dzr, h, "dw_in", D, tm=PROJ_TILE)

    def grad_rows(lo, hi):
        out = []
        for seg_lo, seg_hi, at in ((0, 1536, OFF_POOL), (1536, 3584, OFF_V), (3584, 3600, OFF_GK), (3600, N_IN, OFF_GATE)):
            a_, b_ = max(lo, seg_lo), min(hi, seg_hi)
            if a_ < b_:
                out.append(dw_rt[at + a_ - seg_lo:at + b_ - seg_lo])
        return jnp.concatenate(out, axis=0)

    dw_in_t = jnp.stack([grad_rows(j * nsh, (j + 1) * nsh) for j in range(4)])

    ms = dict(w_in=m_in_t, w_pool_proj=m_w_pool_proj[0], w_gla_proj=m_w_gla_proj[0], w_out=m_w_out[0],
              w_up=m_w_up[0], w_down=m_w_down[0])
    vs = dict(w_in=v_in_t, w_pool_proj=v_w_pool_proj[0], w_gla_proj=v_w_gla_proj[0], w_out=v_w_out[0],
              w_up=v_w_up[0], w_down=v_w_down[0])
    grad, delta, new_m, new_v = {}, {}, {}, {}

    def finish(names, group_axes, part_f, landed):
        return [_finish_half(pf, rb, ax, "finish_" + n) for n, ax, pf, rb in zip(names, group_axes, part_f, landed)]

    def update(names, group_axes, halves, sib_halves):
        for n, ax, mine, theirs in zip(names, group_axes, halves, sib_halves):
            res = _adam_halves(place, shards[n], mine, theirs, ms[n], vs[n], ax, "adam_" + n)
            if n == "w_in":
                res = [jnp.transpose(r_) for r_ in res]
            grad[n], delta[n], new_m[n], new_v[n] = [r_[None] for r_ in res]

    rest_names, rest_axes = ffn_names + out_names, ffn_axes + out_axes
    in_x, token = exchange_start("in", [dw_in_t], (1,), dw_rt)
    _, ffn_landed = _split_wait("reduce_ffn_wait", ffn_handle, 2, ffn_plan, token)
    _, out_landed = _split_wait("reduce_out_wait", out_handle, 3, out_plan, ffn_landed[0])
    rest_halves = lax.optimization_barrier(finish(rest_names, rest_axes, ffn_pf + out_pf, ffn_landed + out_landed))
    (in_pf,), (in_pb,) = partials("in", ("w_in",), (1,), in_x, rest_halves[-1])
    in_plan = _reduce_plan(1, 0)
    in_handle, token = _split_start("reduce_in_start", [in_pb], [((3,) + in_pb.shape[1:], BF16)], in_plan, 3, in_pf)
    rest_plan = _whole_to_sibling_plan(len(rest_halves))
    rest_share, token = _split_start("sibling_share_rest_start", rest_halves, [(h_.shape, F32) for h_ in rest_halves],
                                     rest_plan, len(rest_halves), token)
    grad_x, _, dgmix = _matmul_nt_normbwd(dzr, w_in_rt, xs, g_mix, dx1, token, "in_proj_bwd", ts, transposed=True)
    small_names = ("g_mix", "b_gate", "w_gk_up", "b_gk", "w_pool_grp", "pool_scale", "g_gla_head", "g_ffn", "w_conv",
                   "b_conv", "g_final")
    small_mine = [dgmix, dbgate, dwgk[:GATE_RANK], dbgk, dwgrp.reshape(4 * 128, 128), dscale, dghead, dgffn, dwconv, dbconv,
                  dgfin, loss_part]
    small_sib = _sibling_exchange([], (), small_mine, "sibling_exchange_small")
    small_chip = _add_many(small_mine, small_sib, "chip_partial_small")
    small_plan = _reduce_plan(0, len(small_chip))
    small_handle, token = _split_start("reduce_small_start", small_chip, [((4,) + a_.shape, F32) for a_ in small_chip],
                                       small_plan, 3 * len(small_chip), small_mine[0])

    rest_halves, rest_sib = _split_wait("sibling_share_rest_wait", rest_share, len(rest_halves), rest_plan, token)
    n_ffn = len(ffn_names)
    update(out_names, out_axes, rest_halves[n_ffn:], rest_sib[n_ffn:])
    updated = lax.optimization_barrier([delta[n] for n in out_names])
    _, in_landed = _split_wait("reduce_in_wait", in_handle, 1, in_plan, updated[0])
    in_halves = finish(("w_in",), (1,), (in_pf,), in_landed)
    update(("w_in",), (1,), in_halves, _sibling_share(in_halves, "sibling_share_in"))
    ffn_halves, _ = lax.optimization_barrier((rest_halves[:n_ffn], delta["w_in"]))
    update(ffn_names, ffn_axes, ffn_halves, rest_sib[:n_ffn])
    small_sent, small_landed = _split_wait("reduce_small_wait", small_handle, len(small_chip), small_plan, delta["w_in"])
    given = dict(g_mix=(g_mix, m_g_mix, v_g_mix), b_gate=(b_gate, m_b_gate, v_b_gate), w_gk_up=(w_gk_up, m_w_gk_up, v_w_gk_up),
                 b_gk=(b_gk, m_b_gk, v_b_gk), w_pool_grp=(w_pool_grp, m_w_pool_grp, v_w_pool_grp),
                 pool_scale=(pool_scale, m_pool_scale, v_pool_scale), g_gla_head=(g_gla_head, m_g_gla_head, v_g_gla_head),
                 g_ffn=(g_ffn, m_g_ffn, v_g_ffn), w_conv=(w_conv, m_w_conv, v_w_conv), b_conv=(b_conv, m_b_conv, v_b_conv),
                 g_final=(g_final, m_g_final, v_g_final))
    flat2 = lambda a: a.reshape(-1, a.shape[-1])
    widths = [dict(w_gk_up=HK, w_conv=UP_SHARD).get(n) for n in small_names]
    totals, ds, mo, vo = _adam_small(place, small_sent, small_landed, *[[flat2(given[n][k]) for n in small_names] for k in range(3)],
                                     widths)
    loss = totals[-1][0, 0]
    for i, n in enumerate(small_names):
        shp = given[n][0].shape
        grad[n], delta[n], new_m[n], new_v[n] = [r_.reshape(shp) for r_ in (totals[i], ds[i], mo[i], vo[i])]

    order = ("g_mix", "w_in", "b_gate", "w_gk_up", "b_gk", "w_pool_grp", "pool_scale", "g_gla_head", "w_pool_proj",
             "w_gla_proj", "w_out", "g_ffn", "w_up", "w_conv", "b_conv", "w_down", "g_final")
    return (loss, grad_x[None], *[grad[n] for n in order], *[delta[n] for n in order], *[new_m[n] for n in order],
            *[new_v[n] for n in order])
```

```python
import jax
import jax.numpy as jnp
from jax import lax
from jax.experimental import pallas as pl
from jax.experimental.pallas import tpu as pltpu

F32 = jnp.float32
BF16 = jnp.bfloat16
MESH = pl.DeviceIdType.MESH

D = 1024
EPS = 1e-6
CHUNK = 64
POOL_W = 512
POOL_WINDOWS = (2, 4, 8, 16)
HEADS = 4
HK = 128
HV = 256
GATE_RANK = 16
D_FF = 2816
N_UP = 2 * D_FF
N_IN = 5648
QSCALE = HK ** -0.5
N_INR = 5760
OFF_GATE, OFF_V, OFF_OG, OFF_POOL, OFF_Q, OFF_K, OFF_GK = 0, 2048, 3072, 4096, 4608, 5120, 5632

ADAM_LR, ADAM_B1, ADAM_B2, ADAM_EPS, ADAM_WD, ADAM_STEP = 0.001, 0.9, 0.999, 1e-08, 0.01, 10

VMEM_LIMIT = 56 * 1024 * 1024
PROJ_TILE = N_INR // 5
UP_SHARD = N_UP // 4


def _cp(*sem):
    return pltpu.CompilerParams(dimension_semantics=sem if sem else None, vmem_limit_bytes=VMEM_LIMIT)


def _dot(a, b):
    return jnp.dot(a, b, preferred_element_type=F32)


def _dot_nt(a, b):
    return lax.dot_general(a, b, (((1,), (1,)), ((), ())), preferred_element_type=F32)


def _dot_tn(a, b):
    return lax.dot_general(a, b, (((0,), (0,)), ((), ())), preferred_element_type=F32)


def _sigmoid(v):
    return 1.0 / (1.0 + jnp.exp(-v))


def _rows(shape):
    return lax.broadcasted_iota(jnp.int32, shape, 0)


def _pick_row(v, r):
    return jnp.sum(jnp.where(_rows(v.shape) == r, v, 0.0), axis=0, keepdims=True)


def _rmsnorm(x, g, after, name, ts):
    s = x.shape[0]

    def body(x_ref, g_ref, after_ref, h_ref):
        xv = x_ref[...]
        r = lax.rsqrt(jnp.mean(xv * xv, axis=-1, keepdims=True) + EPS)
        h_ref[...] = (xv * r * g_ref[...]).astype(BF16)

    return pl.pallas_call(
        body, name=name, grid=(s // ts,),
        in_specs=[pl.BlockSpec((ts, D), lambda i: (i, 0)), pl.BlockSpec((1, D), lambda i: (0, 0)), ANY],
        out_specs=pl.BlockSpec((ts, D), lambda i: (i, 0)), out_shape=jax.ShapeDtypeStruct((s, D), BF16),
        compiler_params=_cp("arbitrary"),
    )(x, g, after)


MM_ROWS = 512


def _matmul_resident(h, w, after, name):
    s = h.shape[0]
    nj, tn = w.shape[0], w.shape[2]
    rc = min(s, MM_ROWS)

    def body(h_ref, w_ref, after_ref, z_ref):
        for r0 in range(0, s, rc):
            z_ref[r0:r0 + rc, :] = _dot(h_ref[r0:r0 + rc, :], w_ref[...]).astype(BF16)

    return pl.pallas_call(
        body, name=name, grid=(nj,),
        in_specs=[pl.BlockSpec((s, D), lambda j: (0, 0)), pl.BlockSpec((None, D, tn), lambda j: (j, 0, 0)), ANY],
        out_specs=pl.BlockSpec((s, tn), lambda j: (0, j)), out_shape=jax.ShapeDtypeStruct((s, nj * tn), BF16),
        compiler_params=_cp("arbitrary"),
    )(h, w, after)


PROJ_PIECES = ((3600, 2048, OFF_GATE), (1536, 2048, OFF_V), (0, 1536, OFF_POOL), (3584, GATE_RANK, OFF_GK))


def _split_by_shard(pieces, rows_per_shard):
    out = []
    for src, n, dst in pieces:
        while n > 0:
            j, r = divmod(src, rows_per_shard)
            m = min(n, rows_per_shard - r)
            out.append((j, r, m, dst))
            src, n, dst = src + m, n - m, dst + m
    return tuple(out)


PROJ_SEGMENTS = _split_by_shard(PROJ_PIECES, N_IN // 4)


def _in_proj(h, w4, tn):
    s = h.shape[0]
    rc = min(s, MM_ROWS)
    nj = N_INR // tn
    first_use = [dst // tn for _, _, _, dst in PROJ_SEGMENTS]

    def body(h_ref, w_hbm, z_ref, wo_hbm, w_ref, stage, sems, out_sem):
        j = pl.program_id(0)
        cps = [pltpu.make_async_copy(w_hbm.at[k], stage.at[k], sems.at[k]) for k in range(4)]
        out_cp = pltpu.make_async_copy(w_ref, wo_hbm, out_sem.at[0])

        @pl.when(j == 0)
        def _():
            for cp in cps:
                cp.start()
            w_ref[OFF_GK + GATE_RANK:, :] = jnp.zeros((N_INR - OFF_GK - GATE_RANK, D), BF16)

        landed = set()
        for step in range(nj):
            due = [seg for seg, at in zip(PROJ_SEGMENTS, first_use) if at == step]
            if due:
                fresh = sorted({seg[0] for seg in due} - landed)
                landed.update(fresh)

                @pl.when(j == step)
                def _(due=due, fresh=fresh, last=step == max(first_use)):
                    for k in fresh:
                        cps[k].wait()
                    for k, r, n, dst in due:
                        w_ref[dst:dst + n, :] = stage[k, r:r + n, :]
                    if last:
                        out_cp.start()

        wt = w_ref[pl.ds(pl.multiple_of(j * tn, 128), tn), :]
        for r0 in range(0, s, rc):
            z_ref[r0:r0 + rc, :] = _dot_nt(h_ref[r0:r0 + rc, :], wt).astype(BF16)

        @pl.when(j == nj - 1)
        def _():
            out_cp.wait()

    return pl.pallas_call(
        body, name="in_proj", grid=(nj,),
        in_specs=[pl.BlockSpec((s, D), lambda j: (0, 0)), ANY],
        out_specs=[pl.BlockSpec((s, tn), lambda j: (0, j)), ANY],
        out_shape=[jax.ShapeDtypeStruct((s, N_INR), BF16), jax.ShapeDtypeStruct((N_INR, D), BF16)],
        scratch_shapes=[pltpu.VMEM((N_INR, D), BF16), pltpu.VMEM(w4.shape, BF16), pltpu.SemaphoreType.DMA((4,)),
                        pltpu.SemaphoreType.DMA((1,))],
        compiler_params=_cp("arbitrary"),
    )(h, w4)


def _matmul_nt_normbwd(dz, w, x, g, resid, after, name, ts, transposed=False):
    s = x.shape[0]
    w_vmem = w.shape if transposed else (D, w.shape[0] * w.shape[2])
    n_sems = 1 if transposed else w.shape[0]

    def body(dz_ref, w_hbm, x_ref, g_ref, r_ref, after_ref, o_ref, ob_ref, dg_ref, w_ref, sems):
        @pl.when(pl.program_id(0) == 0)
        def _():
            if transposed:
                cps = [pltpu.make_async_copy(w_hbm, w_ref, sems.at[0])]
            else:
                kc = w.shape[2]
                cps = [pltpu.make_async_copy(w_hbm.at[j], w_ref.at[:, pl.ds(j * kc, kc)], sems.at[j])
                       for j in range(w.shape[0])]
            for cp in cps:
                cp.start()
            for cp in cps:
                cp.wait()
            dg_ref[...] = jnp.zeros_like(dg_ref)

        dh = _dot(dz_ref[...], w_ref[...]) if transposed else _dot_nt(dz_ref[...], w_ref[...])
        xv = x_ref[...]
        r = lax.rsqrt(jnp.mean(xv * xv, axis=-1, keepdims=True) + EPS)
        xh = xv * r
        dg_ref[...] += jnp.sum(dh * xh, axis=0, keepdims=True)
        dxh = dh * g_ref[...]
        out = r_ref[...] + r * (dxh - xh * jnp.mean(dxh * xh, axis=-1, keepdims=True))
        o_ref[...] = out
        ob_ref[...] = out.astype(BF16)

    row = lambda i: (i, 0)
    kdim = dz.shape[1]
    return pl.pallas_call(
        body, name=name, grid=(s // ts,),
        in_specs=[pl.BlockSpec((ts, kdim), row), ANY, pl.BlockSpec((ts, D), row),
                  pl.BlockSpec((1, D), lambda i: (0, 0)), pl.BlockSpec((ts, D), row), ANY],
        out_specs=[pl.BlockSpec((ts, D), row), pl.BlockSpec((ts, D), row), pl.BlockSpec((1, D), lambda i: (0, 0))],
        out_shape=[jax.ShapeDtypeStruct((s, D), F32), jax.ShapeDtypeStruct((s, D), BF16),
                   jax.ShapeDtypeStruct((1, D), F32)],
        scratch_shapes=[pltpu.VMEM(w_vmem, BF16), pltpu.SemaphoreType.DMA((n_sems,))],
        compiler_params=_cp("arbitrary"),
    )(dz, w, x, g, resid, after)


def _matmul_tn(a, b, name, tn, shard_major=False, tm=None):
    s, m = a.shape
    n = b.shape[1]
    tm = m if tm is None else tm
    ni, nj = m // tm, n // tn

    def body(a_ref, b_ref, o_ref):
        o_ref[...] = _dot_tn(a_ref[...], b_ref[...]).astype(BF16)

    if shard_major:
        out_spec = pl.BlockSpec((None, tm, tn), lambda i, j: (j, i, 0))
        out_shape = jax.ShapeDtypeStruct((nj, m, tn), BF16)
    else:
        out_spec = pl.BlockSpec((tm, tn), lambda i, j: (i, j))
        out_shape = jax.ShapeDtypeStruct((m, n), BF16)
    return pl.pallas_call(
        body, name=name, grid=(ni, nj),
        in_specs=[pl.BlockSpec((s, tm), lambda i, j: (0, i)), pl.BlockSpec((s, tn), lambda i, j: (0, j))],
        out_specs=out_spec, out_shape=out_shape,
        compiler_params=_cp("arbitrary", "arbitrary"),
    )(a, b)


def _pool_fwd(zr, wgrp, scale):
    s = zr.shape[0]

    def body(u_ref, w_ref, sc_ref, p_ref, pp_ref):
        row = _rows((s, 128))
        for gi, win in enumerate(POOL_WINDOWS):
            cs = slice(gi * 128, (gi + 1) * 128)
            u = u_ref[:, cs].astype(F32)
            acc, k = u, 1
            while k < win:
                acc = acc + jnp.where(row >= k, pltpu.roll(acc, k, 0), 0.0)
                k *= 2
            cnt = jnp.minimum(row + 1, win).astype(F32)
            p = (acc / cnt - u).astype(BF16)
            p_ref[:, cs] = p
            pp_ref[:, cs] = (_dot(p, w_ref[gi].astype(BF16)) * sc_ref[:, cs]).astype(BF16)

    return pl.pallas_call(
        body, name="pool_fwd", grid=(1,),
        in_specs=[pl.BlockSpec((s, POOL_W), lambda i: (0, OFF_POOL // POOL_W)),
                  pl.BlockSpec((4, 128, 128), lambda i: (0, 0, 0)), pl.BlockSpec((1, POOL_W), lambda i: (0, 0))],
        out_specs=[pl.BlockSpec((s, POOL_W), lambda i: (0, 0))] * 2,
        out_shape=[jax.ShapeDtypeStruct((s, POOL_W), BF16)] * 2,
        compiler_params=_cp("arbitrary"),
    )(zr, wgrp, scale)


def _pool_bwd(p, dpp, wgrp, scale, after, dz):
    s = p.shape[0]

    def body(p_ref, dpp_ref, w_ref, sc_ref, after_ref, dz_in, dz_ref, dw_ref, dsc_ref):
        row = _rows((s, 128))
        for gi, win in enumerate(POOL_WINDOWS):
            cs = slice(gi * 128, (gi + 1) * 128)
            pv = p_ref[:, cs]
            wb = w_ref[gi].astype(BF16)
            dpp_v = dpp_ref[:, cs].astype(F32)
            dsc_ref[:, cs] = jnp.sum(dpp_v * _dot(pv, wb), axis=0, keepdims=True)
            dpm = (dpp_v * sc_ref[:, cs]).astype(BF16)
            dw_ref[gi] = _dot_tn(pv, dpm)
            dp = _dot_nt(dpm, wb)
            cnt = jnp.minimum(row + 1, win).astype(F32)
            acc, k = dp / cnt, 1
            while k < win:
                acc = acc + jnp.where(row < s - k, pltpu.roll(acc, s - k, 0), 0.0)
                k *= 2
            dz_ref[:, cs] = (acc - dp).astype(BF16)

    full = lambda i: (0, 0)
    return pl.pallas_call(
        body, name="pool_bwd", grid=(1,),
        in_specs=[pl.BlockSpec((s, POOL_W), full), pl.BlockSpec((s, POOL_W), full),
                  pl.BlockSpec((4, 128, 128), lambda i: (0, 0, 0)), pl.BlockSpec((1, POOL_W), full), ANY, ANY],
        out_specs=[pl.BlockSpec((s, POOL_W), lambda i: (0, OFF_POOL // POOL_W)),
                   pl.BlockSpec((4, 128, 128), lambda i: (0, 0, 0)), pl.BlockSpec((1, POOL_W), full)],
        out_shape=[jax.ShapeDtypeStruct(dz.shape, BF16), jax.ShapeDtypeStruct((4, 128, 128), F32),
                   jax.ShapeDtypeStruct((1, POOL_W), F32)],
        input_output_aliases={5: 0},
        compiler_params=_cp("arbitrary"),
    )(p, dpp, wgrp, scale, after, dz)


def _gla_decay(zgk_ref, wgk_ref, bgk_ref, rb):
    g = _dot(zgk_ref[...], wgk_ref[...].astype(BF16)) + bgk_ref[...]
    la = (jnp.minimum(g, 0.0) - jnp.log(1.0 + jnp.exp(-jnp.abs(g)))) * (1.0 / 16.0)
    rowm = _rows(la.shape) & (CHUNK - 1)
    bc, k = la, 1
    while k < CHUNK:
        bc = bc + jnp.where(rowm >= k, pltpu.roll(bc, k, 0), 0.0)
        k *= 2
    return g, jnp.exp(bc), jnp.exp(-bc)


GLA_HB = 4


def _gla_specs(rb, rmap):
    wk, wv = GLA_HB * HK, GLA_HB * HV
    return [pl.BlockSpec((rb, wk), lambda h, r: (rmap(h, r), OFF_Q // wk + h)),
            pl.BlockSpec((rb, wk), lambda h, r: (rmap(h, r), OFF_K // wk + h)),
            pl.BlockSpec((rb, wv), lambda h, r: (rmap(h, r), OFF_V // wv + h)),
            pl.BlockSpec((rb, 128), lambda h, r: (rmap(h, r), OFF_GK // 128))]


def _gla_fwd(zr, wgk, bgk, ghead, rb):
    s = zr.shape[0]
    nc = rb // CHUNK
    wk, wv = GLA_HB * HK, GLA_HB * HV

    def body(q_ref, k_ref, v_ref, zgk_ref, zog_ref, wgk_ref, bgk_ref, gh_ref, o_ref, og_ref, sp_ref, st_ref, kv_ref):
        @pl.when(pl.program_id(1) == 0)
        def _():
            st_ref[...] = jnp.zeros_like(st_ref)

        _, e_pos, e_neg = _gla_decay(zgk_ref, wgk_ref, bgk_ref, rb)
        lower = _rows((CHUNK, CHUNK)) >= lax.broadcasted_iota(jnp.int32, (CHUNK, CHUNK), 1)
        pairs = [(c, hh) for c in range(nc) for hh in range(GLA_HB)]
        rows = lambda c: slice(c * CHUNK, (c + 1) * CHUNK)
        cols_k = lambda hh: slice(hh * HK, (hh + 1) * HK)
        cols_v = lambda hh: slice(hh * HV, (hh + 1) * HV)
        qfws, pms, e_lasts = {}, {}, {}
        for c, hh in pairs:
            q = q_ref[rows(c), cols_k(hh)].astype(F32) * QSCALE
            k = k_ref[rows(c), cols_k(hh)].astype(F32)
            ec, fc = e_pos[rows(c), cols_k(hh)], e_neg[rows(c), cols_k(hh)]
            qfw = (q * ec).astype(BF16)
            kfw_f = k * fc
            s_fw = _dot_nt(qfw, kfw_f.astype(BF16))
            s_bw = _dot_nt((q * fc).astype(BF16), (k * ec).astype(BF16))
            e_last = _pick_row(ec, CHUNK - 1)
            kv_ref[c, hh] = _dot_tn(v_ref[rows(c), cols_v(hh)], (kfw_f * e_last).astype(BF16))
            qfws[c, hh], pms[c, hh], e_lasts[c, hh] = qfw, jnp.where(lower, s_fw, s_bw).astype(BF16), e_last
        for hh in range(GLA_HB):
            st = st_ref[hh]
            for c in range(nc):
                sp_ref[c, hh] = st.astype(BF16)
                st = st * e_lasts[c, hh] + kv_ref[c, hh]
            st_ref[hh] = st
        for c, hh in pairs:
            o = _dot(pms[c, hh], v_ref[rows(c), cols_v(hh)]) + _dot_nt(qfws[c, hh], sp_ref[c, hh])
            r = lax.rsqrt(jnp.mean(o * o, axis=-1, keepdims=True) + EPS)
            zo = zog_ref[rows(c), cols_v(hh)].astype(F32)
            o_ref[rows(c), cols_v(hh)] = o.astype(BF16)
            og_ref[rows(c), cols_v(hh)] = (o * r * gh_ref[...] * zo * _sigmoid(zo)).astype(BF16)

    rmap = lambda h, r: r
    return pl.pallas_call(
        body, name="gla_fwd", grid=(HEADS // GLA_HB, s // rb),
        in_specs=_gla_specs(rb, rmap) + [
            pl.BlockSpec((rb, wv), lambda h, r: (r, OFF_OG // wv + h)),
            pl.BlockSpec((128, wk), lambda h, r: (0, h)), pl.BlockSpec((1, wk), lambda h, r: (0, h)),
            pl.BlockSpec((1, HV), lambda h, r: (0, 0))],
        out_specs=[pl.BlockSpec((rb, wv), lambda h, r: (r, h)), pl.BlockSpec((rb, wv), lambda h, r: (r, h)),
                   pl.BlockSpec((nc, GLA_HB, HV, HK), lambda h, r: (r, h, 0, 0))],
        out_shape=[jax.ShapeDtypeStruct((s, D), BF16), jax.ShapeDtypeStruct((s, D), BF16),
                   jax.ShapeDtypeStruct((s // CHUNK, HEADS, HV, HK), BF16)],
        scratch_shapes=[pltpu.VMEM((GLA_HB, HV, HK), F32), pltpu.VMEM((nc, GLA_HB, HV, HK), F32)],
        compiler_params=_cp("arbitrary", "arbitrary"),
    )(zr, zr, zr, zr, zr, wgk, bgk, ghead)


def _gla_bwd(zr, do, sp, wgk, bgk, after, dz, rb):
    s = zr.shape[0]
    nc = rb // CHUNK
    nr = s // rb
    wk, wv = GLA_HB * HK, GLA_HB * HV

    def body(q_ref, k_ref, v_ref, zgk_ref, do_ref, sp_ref, wgk_ref, bgk_ref, after_ref, dz_in, dq_ref, dk_ref, dv_ref,
             dg_ref, gt_ref, dbc_ref, gs_ref):
        @pl.when(pl.program_id(1) == 0)
        def _():
            gt_ref[...] = jnp.zeros_like(gt_ref)

        g, e_pos, e_neg = _gla_decay(zgk_ref, wgk_ref, bgk_ref, rb)
        lower = _rows((CHUNK, CHUNK)) >= lax.broadcasted_iota(jnp.int32, (CHUNK, CHUNK), 1)
        is_last = _rows((CHUNK, HK)) == CHUNK - 1
        pairs = [(c, hh) for c in range(nc) for hh in range(GLA_HB)]
        rows = lambda c: slice(c * CHUNK, (c + 1) * CHUNK)
        cols_k = lambda hh: slice(hh * HK, (hh + 1) * HK)
        cols_v = lambda hh: slice(hh * HV, (hh + 1) * HV)
        e_lasts = {}
        for c, hh in pairs:
            ec = e_pos[rows(c), cols_k(hh)]
            qfw = (q_ref[rows(c), cols_k(hh)].astype(F32) * QSCALE * ec).astype(BF16)
            gs_ref[c, hh] = _dot_tn(do_ref[rows(c), cols_v(hh)], qfw)
            e_lasts[c, hh] = _pick_row(ec, CHUNK - 1)
        for hh in range(GLA_HB):
            gt = gt_ref[hh]
            for c in reversed(range(nc)):
                own = gs_ref[c, hh]
                gs_ref[c, hh] = gt
                gt = own + gt * e_lasts[c, hh]
            gt_ref[hh] = gt
        def decayed(c, hh):
            q = q_ref[rows(c), cols_k(hh)].astype(F32) * QSCALE
            k = k_ref[rows(c), cols_k(hh)].astype(F32)
            ec, fc = e_pos[rows(c), cols_k(hh)], e_neg[rows(c), cols_k(hh)]
            return ec, fc, q * ec, k * fc, q * fc, k * ec

        pms, dss = {}, {}
        for c, hh in pairs:
            _, _, qfw_f, kfw_f, qbw_f, kbw_f = decayed(c, hh)
            s_fw = _dot_nt(qfw_f.astype(BF16), kfw_f.astype(BF16))
            s_bw = _dot_nt(qbw_f.astype(BF16), kbw_f.astype(BF16))
            dp = _dot_nt(do_ref[rows(c), cols_v(hh)], v_ref[rows(c), cols_v(hh)])
            pms[c, hh] = jnp.where(lower, s_fw, s_bw).astype(BF16)
            dss[c, hh] = (jnp.where(lower, dp, 0.0).astype(BF16), jnp.where(lower, 0.0, dp).astype(BF16))
        for c, hh in pairs:
            sl, ck, cv = rows(c), cols_k(hh), cols_v(hh)
            v = v_ref[sl, cv]
            dov = do_ref[sl, cv]
            ec, fc, qfw_f, kfw_f, qbw_f, kbw_f = decayed(c, hh)
            qfw, kfw, qbw, kbw = qfw_f.astype(BF16), kfw_f.astype(BF16), qbw_f.astype(BF16), kbw_f.astype(BF16)
            pm = pms[c, hh]
            e_last = e_lasts[c, hh]
            kdec = (kfw_f * e_last).astype(BF16)
            gt = gs_ref[c, hh]
            gtb = gt.astype(BF16)
            spv = sp_ref[c, hh]
            dv_ref[sl, cv] = (_dot_tn(pm, dov) + _dot_nt(kdec, gtb)).astype(BF16)
            ds_fw, ds_bw = dss[c, hh]
            dqfw = _dot(ds_fw, kfw) + _dot(dov, spv)
            dkfw = _dot_tn(ds_fw, qfw)
            dqbw = _dot(ds_bw, kbw)
            dkbw = _dot_tn(ds_bw, qbw)
            dkdec = _dot(v, gtb)
            de_last = (jnp.sum(gt * spv.astype(F32), axis=0, keepdims=True)
                       + jnp.sum(dkdec * kfw_f, axis=0, keepdims=True))
            dkfw = dkfw + dkdec * e_last
            dq_ref[sl, ck] = ((dqfw * ec + dqbw * fc) * QSCALE).astype(BF16)
            dk_ref[sl, ck] = (dkfw * fc + dkbw * ec).astype(BF16)
            dbc = dqfw * qfw_f - dqbw * qbw_f + dkbw * kbw_f - dkfw * kfw_f
            dbc_ref[sl, ck] = dbc + jnp.where(is_last, de_last * e_last, 0.0)
        rowm = _rows((rb, wk)) & (CHUNK - 1)
        dla, kk = dbc_ref[...], 1
        while kk < CHUNK:
            dla = dla + jnp.where(rowm < CHUNK - kk, pltpu.roll(dla, rb - kk, 0), 0.0)
            kk *= 2
        dg_ref[...] = dla * (1.0 / 16.0) * _sigmoid(-g)

    rmap = lambda h, r: nr - 1 - r
    rev = lambda h, r: (nr - 1 - r, h)
    return pl.pallas_call(
        body, name="gla_bwd", grid=(HEADS // GLA_HB, nr),
        in_specs=_gla_specs(rb, rmap) + [
            pl.BlockSpec((rb, wv), rev),
            pl.BlockSpec((nc, GLA_HB, HV, HK), lambda h, r: (nr - 1 - r, h, 0, 0)),
            pl.BlockSpec((128, wk), lambda h, r: (0, h)), pl.BlockSpec((1, wk), lambda h, r: (0, h)), ANY, ANY],
        out_specs=[pl.BlockSpec((rb, wk), rev), pl.BlockSpec((rb, wk), rev),
                   pl.BlockSpec((rb, wv), lambda h, r: (nr - 1 - r, OFF_V // wv + h)), pl.BlockSpec((rb, wk), rev)],
        out_shape=[jax.ShapeDtypeStruct((s, HEADS * HK), BF16), jax.ShapeDtypeStruct((s, HEADS * HK), BF16),
                   jax.ShapeDtypeStruct(dz.shape, BF16), jax.ShapeDtypeStruct((s, HEADS * HK), F32)],
        scratch_shapes=[pltpu.VMEM((GLA_HB, HV, HK), F32), pltpu.VMEM((rb, wk), F32),
                        pltpu.VMEM((nc, GLA_HB, HV, HK), F32)],
        input_output_aliases={9: 2},
        compiler_params=_cp("arbitrary", "arbitrary"),
    )(zr, zr, zr, zr, do, sp, wgk, bgk, after, dz)


def _gk_bwd(dgpre, zr, wgk, after, dz, ts):
    s = zr.shape[0]

    def body(dg_ref, zgk_ref, w_ref, after_ref, dz_in, dz_ref, dw_ref, db_ref):
        @pl.when(pl.program_id(0) == 0)
        def _():
            dw_ref[...] = jnp.zeros_like(dw_ref)
            db_ref[...] = jnp.zeros_like(db_ref)

        dg = dg_ref[...]
        dgb = dg.astype(BF16)
        dz_ref[...] = _dot_nt(dgb, w_ref[...].astype(BF16)).astype(BF16)
        dw_ref[...] += _dot_tn(zgk_ref[...], dgb)
        db_ref[...] += jnp.sum(dg, axis=0, keepdims=True)

    return pl.pallas_call(
        body, name="gk_bwd", grid=(s // ts,),
        in_specs=[pl.BlockSpec((ts, 512), lambda i: (i, 0)), pl.BlockSpec((ts, 128), lambda i: (i, OFF_GK // 128)),
                  pl.BlockSpec((128, 512), lambda i: (0, 0)), ANY, ANY],
        out_specs=[pl.BlockSpec((ts, 128), lambda i: (i, OFF_GK // 128)), pl.BlockSpec((128, 512), lambda i: (0, 0)),
                   pl.BlockSpec((1, 512), lambda i: (0, 0))],
        out_shape=[jax.ShapeDtypeStruct(dz.shape, BF16), jax.ShapeDtypeStruct((128, 512), F32),
                   jax.ShapeDtypeStruct((1, 512), F32)],
        input_output_aliases={4: 0},
        compiler_params=_cp("arbitrary"),
    )(dgpre, zr, wgk, after, dz)


def _merge_fwd(x, zr, pp, og, bgate, wpp, wgla, wout, gffn, after, ts):
    s = x.shape[0]

    def body(x_ref, z0_ref, z1_ref, pp_ref, og_ref, bg_ref, wpp_ref, wgla_ref, wout_ref, gf_ref, after_ref,
             x1_ref, mix_ref, yp_ref, yg_ref, h2_ref):
        ppv = pp_ref[...]
        yp = jnp.concatenate([_dot(ppv, wpp_ref[j]) for j in range(4)], axis=1)
        yg = _dot(og_ref[...], wgla_ref[...])
        g0 = _sigmoid(z0_ref[...].astype(F32) + bg_ref[:, :D])
        g1 = _sigmoid(z1_ref[...].astype(F32) + bg_ref[:, D:])
        mixed = (g0 * yp + g1 * yg).astype(BF16)
        x1 = x_ref[...] + _dot(mixed, wout_ref[...])
        x1_ref[...] = x1
        mix_ref[...] = mixed
        yp_ref[...] = yp.astype(BF16)
        yg_ref[...] = yg.astype(BF16)
        r = lax.rsqrt(jnp.mean(x1 * x1, axis=-1, keepdims=True) + EPS)
        h2_ref[...] = (x1 * r * gf_ref[...]).astype(BF16)

    row = lambda i: (i, 0)
    const2 = lambda i: (0, 0)
    return pl.pallas_call(
        body, name="merge_fwd", grid=(s // ts,),
        in_specs=[pl.BlockSpec((ts, D), row), pl.BlockSpec((ts, D), lambda i: (i, 0)), pl.BlockSpec((ts, D), lambda i: (i, 1)),
                  pl.BlockSpec((ts, POOL_W), row), pl.BlockSpec((ts, D), row), pl.BlockSpec((1, 2 * D), const2),
                  pl.BlockSpec((4, POOL_W, 256), lambda i: (0, 0, 0)), pl.BlockSpec((D, D), const2),
                  pl.BlockSpec((D, D), const2), pl.BlockSpec((1, D), const2), ANY],
        out_specs=[pl.BlockSpec((ts, D), row)] * 5,
        out_shape=[jax.ShapeDtypeStruct((s, D), F32)] + [jax.ShapeDtypeStruct((s, D), BF16)] * 4,
        compiler_params=_cp("arbitrary"),
    )(x, zr, zr, pp, og, bgate, wpp, wgla, wout, gffn, after)


def _merge_bwd(dx1b, zr, yp, yg, o, bgate, ghead, wpp, wgla, wout, after, ts):
    s = dx1b.shape[0]

    def body(dx_ref, z0_ref, z1_ref, zog_ref, yp_ref, yg_ref, o_ref, bg_ref, gh_ref, wpp_ref, wgla_ref, wout_ref, after_ref,
             dzg_ref, dyp_ref, dyg_ref, dpp_ref, do_ref, dzog_ref, dbg_ref, dgh_ref):
        @pl.when(pl.program_id(0) == 0)
        def _():
            dbg_ref[...] = jnp.zeros_like(dbg_ref)
            dgh_ref[...] = jnp.zeros_like(dgh_ref)

        dmix = _dot_nt(dx_ref[...], wout_ref[...])
        g0 = _sigmoid(z0_ref[...].astype(F32) + bg_ref[:, :D])
        g1 = _sigmoid(z1_ref[...].astype(F32) + bg_ref[:, D:])
        dypb = (dmix * g0).astype(BF16)
        dygb = (dmix * g1).astype(BF16)
        dz0 = dmix * yp_ref[...].astype(F32) * g0 * (1.0 - g0)
        dz1 = dmix * yg_ref[...].astype(F32) * g1 * (1.0 - g1)
        dzg_ref[:, :D] = dz0.astype(BF16)
        dzg_ref[:, D:] = dz1.astype(BF16)
        dbg_ref[:, :D] += jnp.sum(dz0, axis=0, keepdims=True)
        dbg_ref[:, D:] += jnp.sum(dz1, axis=0, keepdims=True)
        dyp_ref[...] = dypb
        dyg_ref[...] = dygb
        dpp = _dot_nt(dypb[:, 0:256], wpp_ref[0])
        for j in range(1, 4):
            dpp = dpp + _dot_nt(dypb[:, j * 256:(j + 1) * 256], wpp_ref[j])
        dpp_ref[...] = dpp.astype(BF16)
        dog = _dot_nt(dygb, wgla_ref[...])
        gh = gh_ref[...]
        dgh = jnp.zeros((1, HV), F32)
        for h in range(HEADS):
            cs = slice(h * HV, (h + 1) * HV)
            ov = o_ref[:, cs].astype(F32)
            r = lax.rsqrt(jnp.mean(ov * ov, axis=-1, keepdims=True) + EPS)
            oh = ov * r
            zo = zog_ref[:, cs].astype(F32)
            sg = _sigmoid(zo)
            dog_h = dog[:, cs]
            don = dog_h * zo * sg
            dzog_ref[:, cs] = (dog_h * oh * gh * sg * (1.0 + zo * (1.0 - sg))).astype(BF16)
            dgh = dgh + jnp.sum(don * oh, axis=0, keepdims=True)
            doh = don * gh
            do_ref[:, cs] = (r * (doh - oh * jnp.mean(doh * oh, axis=-1, keepdims=True))).astype(BF16)
        dgh_ref[...] += dgh

    row = lambda i: (i, 0)
    const2 = lambda i: (0, 0)
    return pl.pallas_call(
        body, name="merge_bwd", grid=(s // ts,),
        in_specs=[pl.BlockSpec((ts, D), row), pl.BlockSpec((ts, D), lambda i: (i, 0)), pl.BlockSpec((ts, D), lambda i: (i, 1)),
                  pl.BlockSpec((ts, D), lambda i: (i, OFF_OG // D)), pl.BlockSpec((ts, D), row), pl.BlockSpec((ts, D), row),
                  pl.BlockSpec((ts, D), row), pl.BlockSpec((1, 2 * D), const2), pl.BlockSpec((1, HV), const2),
                  pl.BlockSpec((4, POOL_W, 256), lambda i: (0, 0, 0)), pl.BlockSpec((D, D), const2),
                  pl.BlockSpec((D, D), const2), ANY],
        out_specs=[pl.BlockSpec((ts, 2 * D), row), pl.BlockSpec((ts, D), row), pl.BlockSpec((ts, D), row),
                   pl.BlockSpec((ts, POOL_W), row), pl.BlockSpec((ts, D), row), pl.BlockSpec((ts, D), row),
                   pl.BlockSpec((1, 2 * D), const2), pl.BlockSpec((1, HV), const2)],
        out_shape=[jax.ShapeDtypeStruct((s, N_INR), BF16), jax.ShapeDtypeStruct((s, D), BF16),
                   jax.ShapeDtypeStruct((s, D), BF16), jax.ShapeDtypeStruct((s, POOL_W), BF16),
                   jax.ShapeDtypeStruct((s, D), BF16), jax.ShapeDtypeStruct((s, D), BF16),
                   jax.ShapeDtypeStruct((1, 2 * D), F32), jax.ShapeDtypeStruct((1, HV), F32)],
        compiler_params=_cp("arbitrary"),
    )(dx1b, zr, zr, zr, yp, yg, o, bgate, ghead, wpp, wgla, wout, after)


HALO = 16
CCH = D_FF // 2


def _conv_taps(u_ref, halo_ref, cs, first, ts):
    u = u_ref[:, cs].astype(F32)
    hal = halo_ref[:, cs].astype(F32)
    h1 = jnp.where(first, 0.0, _pick_row(hal, HALO - 1))
    h2 = jnp.where(first, 0.0, _pick_row(hal, HALO - 2))
    row8 = _rows((8, u.shape[1]))
    r1, r2 = pltpu.roll(u, 1, 0), pltpu.roll(u, 2, 0)
    r1 = jnp.concatenate([jnp.where(row8 == 0, h1, r1[:8]), r1[8:]], axis=0)
    r2 = jnp.concatenate([jnp.where(row8 == 0, h2, jnp.where(row8 == 1, h1, r2[:8])), r2[8:]], axis=0)
    return u, r1, r2


def _ffn_down_loss(u, x1, tgt, wconv, bconv, wdown, gfin, ts):
    s = x1.shape[0]

    def body(u_ref, halo_ref, x1_ref, t_ref, wc_ref, bc_ref, wd_ref, gf_ref, a_ref, c_ref, dx_ref, dxb_ref, ls_ref,
             dgf_ref):
        i = pl.program_id(0)

        @pl.when(i == 0)
        def _():
            ls_ref[...] = jnp.zeros_like(ls_ref)
            dgf_ref[...] = jnp.zeros_like(dgf_ref)

        first = i == 0
        acc = x1_ref[...]
        for hf in range(D_FF // CCH):
            cg = slice(hf * CCH, (hf + 1) * CCH)
            cv = slice(D_FF + hf * CCH, D_FF + (hf + 1) * CCH)
            vals = []
            for cs in (cg, cv):
                u0, u1, u2 = _conv_taps(u_ref, halo_ref, cs, first, ts)
                vals.append(bc_ref[:, cs] + wc_ref[0:1, cs] * u2 + wc_ref[1:2, cs] * u1 + wc_ref[2:3, cs] * u0)
                c_ref[:, cs] = vals[-1].astype(BF16)
            a = (vals[0] * _sigmoid(vals[0]) * vals[1]).astype(BF16)
            a_ref[:, cg] = a
            acc = acc + _dot(a, wd_ref[cg, :])
        r = lax.rsqrt(jnp.mean(acc * acc, axis=-1, keepdims=True) + EPS)
        xh = acc * r
        gf = gf_ref[...]
        err = xh * gf - t_ref[...]
        ls_ref[...] += (0.5 / D) * jnp.sum(jnp.sum(err * err, axis=-1, keepdims=True), axis=0, keepdims=True)
        dy = err * (1.0 / D)
        dgf_ref[...] += jnp.sum(dy * xh, axis=0, keepdims=True)
        dxh = dy * gf
        dx = r * (dxh - xh * jnp.mean(dxh * xh, axis=-1, keepdims=True))
        dx_ref[...] = dx
        dxb_ref[...] = dx.astype(BF16)

    row = lambda i: (i, 0)
    const2 = lambda i: (0, 0)
    return pl.pallas_call(
        body, name="ffn_down_loss", grid=(s // ts,),
        in_specs=[pl.BlockSpec((ts, N_UP), row),
                  pl.BlockSpec((HALO, N_UP), lambda i: (jnp.maximum(i * (ts // HALO) - 1, 0), 0)),
                  pl.BlockSpec((ts, D), row), pl.BlockSpec((ts, D), row), pl.BlockSpec((3, N_UP), const2),
                  pl.BlockSpec((1, N_UP), const2), pl.BlockSpec((D_FF, D), const2), pl.BlockSpec((1, D), const2)],
        out_specs=[pl.BlockSpec((ts, D_FF), row), pl.BlockSpec((ts, N_UP), row), pl.BlockSpec((ts, D), row),
                   pl.BlockSpec((ts, D), row), pl.BlockSpec((1, 128), const2), pl.BlockSpec((1, D), const2)],
        out_shape=[jax.ShapeDtypeStruct((s, D_FF), BF16), jax.ShapeDtypeStruct((s, N_UP), BF16),
                   jax.ShapeDtypeStruct((s, D), F32), jax.ShapeDtypeStruct((s, D), BF16),
                   jax.ShapeDtypeStruct((1, 128), F32), jax.ShapeDtypeStruct((1, D), F32)],
        compiler_params=_cp("arbitrary"),
    )(u, u, x1, tgt, wconv, bconv, wdown, gfin)


def _ffn_bwd(dx2b, u, c, wconv, wdown, ts):
    s = dx2b.shape[0]
    nt = s // ts

    def body(dx_ref, u_ref, c_ref, wc_ref, wd_ref, du_ref, db_ref, dw_ref, nxt_ref):
        @pl.when(pl.program_id(0) == 0)
        def _():
            db_ref[...] = jnp.zeros_like(db_ref)
            dw_ref[...] = jnp.zeros_like(dw_ref)
            nxt_ref[...] = jnp.zeros_like(nxt_ref)

        dxv = dx_ref[...]
        row8 = _rows((8, CCH))
        for hf in range(D_FF // CCH):
            cg = slice(hf * CCH, (hf + 1) * CCH)
            cv = slice(D_FF + hf * CCH, D_FF + (hf + 1) * CCH)
            da = _dot_nt(dxv, wd_ref[cg, :])
            gate = c_ref[:, cg].astype(F32)
            val = c_ref[:, cv].astype(F32)
            sg = _sigmoid(gate)
            dcs = (da * val * sg * (1.0 + gate * (1.0 - sg)), da * gate * sg)
            for cs, dc in zip((cg, cv), dcs):
                n1 = nxt_ref[0:1, cs]
                n2 = nxt_ref[1:2, cs]
                r1, r2 = pltpu.roll(dc, ts - 1, 0), pltpu.roll(dc, ts - 2, 0)
                f1 = jnp.concatenate([r1[:ts - 8], jnp.where(row8 == 7, n1, r1[ts - 8:])], axis=0)
                f2 = jnp.concatenate([r2[:ts - 8], jnp.where(row8 == 7, n2, jnp.where(row8 == 6, n1, r2[ts - 8:]))], axis=0)
                uv = u_ref[:, cs].astype(F32)
                db_ref[:, cs] += jnp.sum(dc, axis=0, keepdims=True)
                dw_ref[0:1, cs] += jnp.sum(f2 * uv, axis=0, keepdims=True)
                dw_ref[1:2, cs] += jnp.sum(f1 * uv, axis=0, keepdims=True)
                dw_ref[2:3, cs] += jnp.sum(dc * uv, axis=0, keepdims=True)
                du_ref[:, cs] = (wc_ref[2:3, cs] * dc + wc_ref[1:2, cs] * f1 + wc_ref[0:1, cs] * f2).astype(BF16)
                nxt_ref[:, cs] = dc[0:8, :]

    rev = lambda i: (nt - 1 - i, 0)
    const2 = lambda i: (0, 0)
    return pl.pallas_call(
        body, name="ffn_bwd", grid=(nt,),
        in_specs=[pl.BlockSpec((ts, D), rev), pl.BlockSpec((ts, N_UP), rev), pl.BlockSpec((ts, N_UP), rev),
                  pl.BlockSpec((3, N_UP), const2), pl.BlockSpec((D_FF, D), const2)],
        out_specs=[pl.BlockSpec((ts, N_UP), rev), pl.BlockSpec((1, N_UP), const2), pl.BlockSpec((3, N_UP), const2)],
        out_shape=[jax.ShapeDtypeStruct((s, N_UP), BF16), jax.ShapeDtypeStruct((1, N_UP), F32),
                   jax.ShapeDtypeStruct((3, N_UP), F32)],
        scratch_shapes=[pltpu.VMEM((8, N_UP), F32)],
        compiler_params=_cp("arbitrary"),
    )(dx2b, u, c, wconv, wdown)


ANY = pl.BlockSpec(memory_space=pl.ANY)


def _place():
    x, y, c = lax.axis_index("x"), lax.axis_index("y"), lax.axis_index("c")
    chips = [(1 - x, y), (x, 1 - y), (1 - x, 1 - y)]
    return x, y, c, chips


def _half(shape, c, axis):
    size = shape[axis] // 2
    cut = pl.ds(pl.multiple_of(c * size, 8 if axis == 0 else 128), size)
    return (cut, slice(None)) if axis == 0 else (slice(None), cut)


def _half_shape(shape, axis):
    return (shape[0] // 2, shape[1]) if axis == 0 else (shape[0], shape[1] // 2)


def _remote(src, dst, send_sems, recv_sems, k, to):
    return pltpu.make_async_remote_copy(src_ref=src, dst_ref=dst, send_sem=send_sems.at[k], recv_sem=recv_sems.at[k],
                                        device_id=to, device_id_type=MESH)


def _sibling_exchange(grads, axes, smalls, name):
    nb = len(grads)
    n = nb + len(smalls)

    def body(*refs):
        ins, outs = refs[:n], refs[n:2 * n]
        send_sems, recv_sems = refs[2 * n:]
        x, y, c, _ = _place()
        sib = (x, y, 1 - c)
        cps = []
        for a in range(nb):
            theirs = _half(grads[a].shape[1:], 1 - c, axes[a])
            cps.append(_remote(ins[a].at[(slice(None),) + theirs], outs[a], send_sems, recv_sems, a, sib))
        for a in range(nb, n):
            cps.append(_remote(ins[a], outs[a], send_sems, recv_sems, a, sib))
        for cp in cps:
            cp.start()
        for cp in cps:
            cp.wait()

    out_shape = [jax.ShapeDtypeStruct((4,) + _half_shape(g.shape[1:], ax), g.dtype) for g, ax in zip(grads, axes)]
    out_shape += [jax.ShapeDtypeStruct(a.shape, F32) for a in smalls]
    return pl.pallas_call(
        body, name=name, in_specs=[ANY] * n, out_specs=[ANY] * n, out_shape=out_shape,
        scratch_shapes=[pltpu.SemaphoreType.DMA((n,)), pltpu.SemaphoreType.DMA((n,))],
        compiler_params=pltpu.CompilerParams(has_side_effects=True),
    )(*grads, *smalls)


def _gather_share(lands, axes, name):
    n = len(lands)

    def body(*refs):
        outs = refs[n:2 * n]
        send_sems, recv_sems = refs[2 * n:]
        x, y, c, chips = _place()
        sib = (x, y, 1 - c)
        cps = []
        for a in range(n):
            mine = _half(lands[a].shape[1:], c, axes[a])
            for k, ch in enumerate(chips):
                landed = outs[a].at[(2 * ch[0] + ch[1],) + mine]
                cps.append(_remote(landed, landed, send_sems, recv_sems, 3 * a + k, sib))
        for cp in cps:
            cp.start()
        for a in range(n):
            other = _half(lands[a].shape[1:], 1 - c, axes[a])
            for k, ch in enumerate(chips):
                landed = outs[a].at[(2 * ch[0] + ch[1],) + other]
                _remote(landed, landed, send_sems, recv_sems, 3 * a + k, sib).wait_recv()
        for cp in cps:
            cp.wait_send()

    return pl.pallas_call(
        body, name=name, in_specs=[ANY] * n, out_specs=[ANY] * n,
        out_shape=[jax.ShapeDtypeStruct(a.shape, a.dtype) for a in lands],
        input_output_aliases={a: a for a in range(n)},
        scratch_shapes=[pltpu.SemaphoreType.DMA((3 * n,)), pltpu.SemaphoreType.DMA((3 * n,))],
        compiler_params=pltpu.CompilerParams(has_side_effects=True),
    )(*lands)


def _sibling_share(halves, name):
    n = len(halves)

    def body(*refs):
        ins, outs = refs[:n], refs[n:2 * n]
        send_sems, recv_sems = refs[2 * n:]
        x, y, c, _ = _place()
        cps = [_remote(ins[a], outs[a], send_sems, recv_sems, a, (x, y, 1 - c)) for a in range(n)]
        for cp in cps:
            cp.start()
        for cp in cps:
            cp.wait()

    return pl.pallas_call(
        body, name=name, in_specs=[ANY] * n, out_specs=[ANY] * n,
        out_shape=[jax.ShapeDtypeStruct(h.shape, F32) for h in halves],
        scratch_shapes=[pltpu.SemaphoreType.DMA((n,)), pltpu.SemaphoreType.DMA((n,))],
        compiler_params=pltpu.CompilerParams(has_side_effects=True),
    )(*halves)


HBM = pl.BlockSpec(memory_space=pltpu.HBM)
SEM = pl.BlockSpec(memory_space=pltpu.SEMAPHORE)
DATAFLOW = pltpu.SideEffectType.DATAFLOW_SIDE_EFFECTING


def _split_start(name, srcs, land_shapes, plan, n_copies, after):
    lands = [lax.empty(*ls) if isinstance(ls, tuple) else ls for ls in land_shapes]
    bufs = list(srcs) + lands
    nb, ns = len(bufs), len(srcs)

    def body(*refs):
        send_sems, recv_sems, token = refs[nb + 1], refs[nb + 2], refs[-1]
        for k, (src, dst, to) in enumerate(plan(refs[:ns], refs[ns:nb])):
            _remote(src, dst, send_sems, recv_sems, k, to).start()
        token[...] = jnp.zeros_like(token)

    res = pl.pallas_call(
        body, name=name,
        out_shape=(pltpu.SemaphoreType.DMA((n_copies,)), pltpu.SemaphoreType.DMA((n_copies,)),
                   *[pltpu.HBM(b.shape, b.dtype) for b in bufs], jax.ShapeDtypeStruct((8, 128), F32)),
        in_specs=[HBM] * nb + [ANY],
        out_specs=(SEM, SEM, *[HBM] * nb, pl.BlockSpec(memory_space=pltpu.VMEM)),
        input_output_aliases={i: 2 + i for i in range(nb)},
        compiler_params=pltpu.CompilerParams(has_side_effects=DATAFLOW),
    )(*[pltpu.with_memory_space_constraint(b, pltpu.HBM) for b in bufs], after)
    return (res[0], res[1], list(res[2:2 + nb])), res[-1]


def _split_relay(name, handle, plan, relay_plan, n_relay, after):
    send_sems, recv_sems, bufs = handle
    nb = len(bufs)

    def body(*refs):
        sends, recvs = refs[nb], refs[nb + 1]
        for k, (src, dst, to) in enumerate(plan((), refs[:nb])):
            _remote(src, dst, sends, recvs, k, to).wait_recv()
        relay_sends, relay_recvs, token = refs[nb + 3], refs[nb + 4], refs[-1]
        for k, (src, dst, to) in enumerate(relay_plan((), refs[:nb])):
            _remote(src, dst, relay_sends, relay_recvs, k, to).start()
        token[...] = jnp.zeros_like(token)

    res = pl.pallas_call(
        body, name=name,
        out_shape=(pltpu.SemaphoreType.DMA((n_relay,)), pltpu.SemaphoreType.DMA((n_relay,)),
                   *[pltpu.HBM(b.shape, b.dtype) for b in bufs], jax.ShapeDtypeStruct((8, 128), F32)),
        in_specs=[HBM] * nb + [SEM, SEM, ANY],
        out_specs=(SEM, SEM, *[HBM] * nb, pl.BlockSpec(memory_space=pltpu.VMEM)),
        input_output_aliases={i: 2 + i for i in range(nb)},
        compiler_params=pltpu.CompilerParams(has_side_effects=DATAFLOW),
    )(*bufs, send_sems, recv_sems, after)
    passed = list(res[2:2 + nb])
    return (send_sems, recv_sems, passed), (res[0], res[1], passed), res[-1]


def _split_wait(name, handle, n_srcs, plan, after, arrived=False, first=0):
    send_sems, recv_sems, bufs = handle
    nb = len(bufs)

    def body(*refs):
        sends, recvs = refs[nb], refs[nb + 1]
        for k, (src, dst, to) in enumerate(plan(refs[:n_srcs], refs[n_srcs:nb])):
            cp = _remote(src, dst, sends, recvs, first + k, to)
            cp.wait_send()
            if arrived:
                continue
            cp.wait_recv()

    res = pl.pallas_call(
        body, name=name, out_shape=[pltpu.HBM(b.shape, b.dtype) for b in bufs],
        in_specs=[HBM] * nb + [SEM, SEM, ANY], out_specs=[HBM] * nb,
        input_output_aliases={i: i for i in range(nb)},
        compiler_params=pltpu.CompilerParams(has_side_effects=DATAFLOW),
    )(*bufs, send_sems, recv_sems, after)
    return list(res[:n_srcs]), list(res[n_srcs:])


def _relay_plan(shapes, axes):
    def plan(srcs, lands):
        x, y, c, _ = _place()
        first = c == 0
        from_x, from_y = jnp.where(first, 1 - x, x), jnp.where(first, y, 1 - y)
        to = (jnp.where(first, x, 1 - x), jnp.where(first, 1 - y, y), c)
        out = []
        for a, (shape, axis) in enumerate(zip(shapes, axes)):
            got = lands[a].at[(2 * from_x + from_y,) + _half(shape, c, axis)]
            out.append((got, got, to))
        return out
    return plan


def _gather_plan(shapes, axes, n_whole=0, relayed=False):
    def plan(srcs, lands):
        x, y, c, chips = _place()
        me = 2 * x + y
        out = []
        for a, (shape, axis) in enumerate(zip(shapes, axes)):
            own = lands[a].at[(me,) + _half(shape, c, axis)]
            for ch in chips[:2] if relayed else chips:
                out.append((own, own, (ch[0], ch[1], c)))
        for a in range(len(shapes), len(shapes) + n_whole):
            for ch in chips:
                out.append((lands[a].at[me], lands[a].at[me], (ch[0], ch[1], c)))
        return out
    return plan


def _share_plan(shapes, axes):
    def plan(srcs, lands):
        x, y, c, chips = _place()
        out = []
        for a, (shape, axis) in enumerate(zip(shapes, axes)):
            mine = _half(shape, c, axis)
            for ch in chips:
                landed = lands[a].at[(2 * ch[0] + ch[1],) + mine]
                out.append((landed, landed, (x, y, 1 - c)))
        return out
    return plan


def _sibling_plan(shapes, axes):
    def plan(srcs, lands):
        x, y, c, _ = _place()
        return [(srcs[a].at[(slice(None),) + _half(shape, 1 - c, axis)], lands[a], (x, y, 1 - c))
                for a, (shape, axis) in enumerate(zip(shapes, axes))]
    return plan


def _whole_to_sibling_plan(n):
    def plan(srcs, lands):
        x, y, c, _ = _place()
        return [(srcs[a], lands[a], (x, y, 1 - c)) for a in range(n)]
    return plan


def _reduce_plan(n_big, n_small):
    def plan(srcs, lands):
        x, y, c, chips = _place()
        out = []
        for a in range(n_big):
            for k, ch in enumerate(chips):
                out.append((srcs[a].at[2 * ch[0] + ch[1]], lands[a].at[k], (ch[0], ch[1], c)))
        for a in range(n_big, n_big + n_small):
            for ch in chips:
                out.append((srcs[a], lands[a].at[2 * x + y], (ch[0], ch[1], c)))
        return out
    return plan


def _row_tile(rows, cols, mult):
    best = mult
    for t in range(mult, rows + 1, mult):
        if rows % t == 0 and t * cols * 4 <= (2 << 20):
            best = t
    return best if rows % best == 0 else rows


COL_TILE = 256


def _half_tiling(hshape, axis, mult):
    hr, hc = hshape
    if axis == 0:
        tr = _row_tile(hr, hc, mult)
        return tr, hc, hr // tr
    return hr, COL_TILE, hc // COL_TILE


def _tile_idx(axis, t):
    return (t, 0) if axis == 0 else (0, t)


def _chip_partial(place, g, t, axis, name):
    hshape = t.shape[1:]
    br, bc, nt = _half_tiling(hshape, axis, 16)

    def body(pl_ref, g_ref, t_ref, pf_ref, pb_ref):
        v = g_ref[...].astype(F32) + t_ref[...].astype(F32)
        pb_ref[...] = v.astype(BF16)

        @pl.when(pl.program_id(1) == pl_ref[0])
        def _():
            pf_ref[...] = v

    blk = (None, br, bc)
    return pl.pallas_call(
        body, name=name,
        grid_spec=pltpu.PrefetchScalarGridSpec(
            num_scalar_prefetch=1, grid=(nt, 4),
            in_specs=[pl.BlockSpec(blk, lambda i, j, p: (j,) + _tile_idx(axis, p[1] * nt + i)),
                      pl.BlockSpec(blk, lambda i, j, p: (j,) + _tile_idx(axis, i))],
            out_specs=[pl.BlockSpec((br, bc), lambda i, j, p: _tile_idx(axis, i)),
                       pl.BlockSpec(blk, lambda i, j, p: (j,) + _tile_idx(axis, i))]),
        out_shape=[jax.ShapeDtypeStruct(hshape, F32), jax.ShapeDtypeStruct((4,) + hshape, BF16)],
        compiler_params=_cp("arbitrary", "arbitrary"),
    )(place, g, t)


def _finish_half(pf, rb, axis, name):
    hshape = pf.shape
    br, bc, nt = _half_tiling(hshape, axis, 16)

    def body(pf_ref, rb_ref, o_ref):
        o_ref[...] = ((pf_ref[...] + rb_ref[0].astype(F32)) + rb_ref[1].astype(F32)) + rb_ref[2].astype(F32)

    return pl.pallas_call(
        body, name=name, grid=(nt,),
        in_specs=[pl.BlockSpec((br, bc), lambda i: _tile_idx(axis, i)),
                  pl.BlockSpec((3, br, bc), lambda i: (0,) + _tile_idx(axis, i))],
        out_specs=pl.BlockSpec((br, bc), lambda i: _tile_idx(axis, i)),
        out_shape=jax.ShapeDtypeStruct(hshape, F32),
        compiler_params=_cp("arbitrary"),
    )(pf, rb)


def _adam_math(w, g, m, v):
    m = ADAM_B1 * m + (1.0 - ADAM_B1) * g
    v = ADAM_B2 * v + (1.0 - ADAM_B2) * (g * g)
    m_hat = m / (1.0 - ADAM_B1 ** ADAM_STEP)
    v_hat = v / (1.0 - ADAM_B2 ** ADAM_STEP)
    return -ADAM_LR * (m_hat / (jnp.sqrt(v_hat) + ADAM_EPS) + ADAM_WD * w), m, v


def _adam_halves(place, w, mine, theirs, m, v, axis, name):
    br, bc, nt = _half_tiling(mine.shape, axis, 8)

    def body(pl_ref, *hbm):
        core = pl_ref[1]

        def step(idx, w_ref, a_ref, b_ref, m_ref, v_ref, g_ref, d_ref, mo_ref, vo_ref):
            g = jnp.where(idx.index[0] // nt == core, a_ref[...], b_ref[...])
            d, mn, vn = _adam_math(w_ref[...], g, m_ref[...], v_ref[...])
            g_ref[...] = g
            d_ref[...] = d
            mo_ref[...] = mn
            vo_ref[...] = vn

        deep = dict(pipeline_mode=pl.Buffered(3))
        full = pl.BlockSpec((br, bc), lambda i: _tile_idx(axis, i))
        full_in = pl.BlockSpec((br, bc), lambda i: _tile_idx(axis, i), **deep)
        mine_spec = pl.BlockSpec((br, bc), lambda i: _tile_idx(axis, jnp.where(i // nt == core, i % nt, nt - 1)), **deep)
        theirs_spec = pl.BlockSpec((br, bc), lambda i: _tile_idx(axis, jnp.where(i // nt == core, 0, i % nt)), **deep)
        pltpu.emit_pipeline(step, grid=(2 * nt,), in_specs=[full_in, mine_spec, theirs_spec, full_in, full_in],
                            out_specs=[full] * 4, _explicit_indices=True)(*hbm)

    return pl.pallas_call(
        body, name=name, in_specs=[pl.BlockSpec(memory_space=pltpu.SMEM)] + [ANY] * 5, out_specs=[ANY] * 4,
        out_shape=[jax.ShapeDtypeStruct(w.shape, F32)] * 4, compiler_params=_cp(),
    )(place, w, mine, theirs, m, v)


def _add_many(xs, ys, name):
    n = len(xs)

    def body(*refs):
        for i in range(n):
            refs[2 * n + i][...] = refs[i][...] + refs[n + i][...]

    return pl.pallas_call(body, name=name, out_shape=[jax.ShapeDtypeStruct(a.shape, F32) for a in xs])(*xs, *ys)


def _adam_small(place, owns, landed, ws, ms, vs, widths):
    n, nw = len(owns), len(ws)

    def body(pl_ref, *refs):
        own_r, land_r = refs[:n], refs[n:2 * n]
        w_r, m_r, v_r = (refs[2 * n + k * nw:2 * n + (k + 1) * nw] for k in range(3))
        outs = refs[2 * n + 3 * nw:]
        g_o, d_o, m_o, v_o = outs[:n], outs[n:n + nw], outs[n + nw:n + 2 * nw], outs[n + 2 * nw:]
        for me in range(4):
            @pl.when(pl_ref[0] == me)
            def _(me=me):
                for i in range(n):
                    p = [own_r[i][...] if k == me else land_r[i][k] for k in range(4)]
                    g = ((p[0] + p[1]) + p[2]) + p[3]
                    if i < nw and widths[i]:
                        g = g[:, me * widths[i]:(me + 1) * widths[i]]
                    g_o[i][...] = g
                    if i < nw:
                        d, mn, vn = _adam_math(w_r[i][...], g, m_r[i][...], v_r[i][...])
                        d_o[i][...] = d
                        m_o[i][...] = mn
                        v_o[i][...] = vn

    g_shapes = [jax.ShapeDtypeStruct(ws[i].shape if i < nw else owns[i].shape, F32) for i in range(n)]
    w_shapes = [jax.ShapeDtypeStruct(w.shape, F32) for w in ws]
    whole = lambda a: pl.BlockSpec(a.shape, lambda i, p, nd=len(a.shape): (0,) * nd)
    ins = list(owns) + list(landed) + list(ws) + list(ms) + list(vs)
    out_shape = g_shapes + w_shapes * 3
    out = pl.pallas_call(
        body, name="adam_small",
        grid_spec=pltpu.PrefetchScalarGridSpec(num_scalar_prefetch=1, grid=(1,), in_specs=[whole(a) for a in ins],
                                               out_specs=[whole(a) for a in out_shape]),
        out_shape=out_shape, compiler_params=_cp("arbitrary"),
    )(place, *ins)
    return out[:n], out[n:n + nw], out[n + nw:n + 2 * nw], out[n + 2 * nw:]


def kernel(x, g_mix, w_in, b_gate, w_gk_up, b_gk, w_pool_grp, pool_scale, g_gla_head, w_pool_proj, w_gla_proj, w_out, g_ffn, w_up, w_conv, b_conv, w_down, g_final, loss_target, m_g_mix, m_w_in, m_b_gate, m_w_gk_up, m_b_gk, m_w_pool_grp, m_pool_scale, m_g_gla_head, m_w_pool_proj, m_w_gla_proj, m_w_out, m_g_ffn, m_w_up, m_w_conv, m_b_conv, m_w_down, m_g_final, v_g_mix, v_w_in, v_b_gate, v_w_gk_up, v_b_gk, v_w_pool_grp, v_pool_scale, v_g_gla_head, v_w_pool_proj, v_w_gla_proj, v_w_out, v_g_ffn, v_w_up, v_w_conv, v_b_conv, v_w_down, v_g_final):
    s = x.shape[1]
    ts = min(s, 512)
    tm = min(s, 256)
    cx, cy, cc = lax.axis_index("x"), lax.axis_index("y"), lax.axis_index("c")
    chip = 2 * cx + cy
    place = jnp.stack([chip, cc]).astype(jnp.int32)

    big_names = ("w_in", "w_pool_proj", "w_gla_proj", "w_out", "w_up", "w_down")
    axes = (1, 0, 0, 0, 0, 0)
    shards = dict(w_in=jnp.transpose(w_in[0]), w_pool_proj=w_pool_proj[0], w_gla_proj=w_gla_proj[0], w_out=w_out[0],
                  w_up=w_up[0], w_down=w_down[0])
    def landing(own_shards):
        return [lax.dynamic_update_slice(lax.empty((4,) + o_.shape, o_.dtype), o_[None], (chip, 0, 0))
                for o_ in own_shards]

    def gather_start(tag, lands, n_halves, group_axes, after, relayed=False):
        n_whole = len(lands) - n_halves
        plan = _gather_plan([l_.shape[1:] for l_ in lands[:n_halves]], group_axes, n_whole, relayed)
        n_copies = (2 if relayed else 3) * n_halves + 3 * n_whole
        handle, token = _split_start("gather_" + tag + "_start", [], lands, plan, n_copies, after)
        return (handle, plan, n_halves, group_axes), token

    def gather_relay(tag, started, after):
        handle, plan, n_halves, group_axes = started
        relay_plan = _relay_plan([b_.shape[1:] for b_ in handle[2]], group_axes)
        first, relay, token = _split_relay("gather_" + tag + "_relay", handle, plan, relay_plan, n_halves, after)
        return (first, plan, relay, relay_plan, group_axes), token

    def gather_finish_relayed(tag, relayed, after):
        first, plan, relay, relay_plan, group_axes = relayed
        lands = _split_wait("gather_" + tag + "_sent", first, 0, plan, after, arrived=True)[1]
        lands = _split_wait("gather_" + tag + "_wait", (relay[0], relay[1], lands), 0, relay_plan, after)[1]
        return _gather_share(lands, group_axes, "gather_" + tag + "_share")

    in_w, tok = gather_start("in", landing([jnp.transpose(w_in[0].astype(BF16))]), 1, axes[:1], g_mix, relayed=True)
    zero = tok[0, 0]
    own = landing([(shards[n] + zero).astype(BF16) for n in big_names[1:]] + [w_gk_up[0] + zero, w_conv[0] + zero])
    own = lax.optimization_barrier(own)
    in_r, tok = gather_relay("in", in_w, own[4])
    xs, tgt = x[0], loss_target[0]
    h = _rmsnorm(xs, g_mix, tok, "norm_mix", ts)
    m_in_t, v_in_t = jnp.transpose(m_w_in[0]), jnp.transpose(v_w_in[0])
    h, m_in_t, v_in_t = lax.optimization_barrier((h, m_in_t, v_in_t))
    groups = ((own[0:3] + own[5:7], 3, axes[1:4]), (own[3:4], 1, axes[4:5]), (own[4:5], 1, axes[5:6]))
    plans = [_gather_plan([l_.shape[1:] for l_ in ls[:nh]], ax, len(ls) - nh) for ls, nh, ax in groups]
    counts = [3 * len(ls) for ls, _, _ in groups]
    bounds = [sum(len(ls) for ls, _, _ in groups[:i]) for i in range(4)]

    def rest_plan(srcs, lands):
        return [cp for i, pl_ in enumerate(plans) for cp in pl_((), lands[bounds[i]:bounds[i + 1]])]

    (rest_send, rest_recv, rest_bufs), tok = _split_start("gather_rest_start", [], [l_ for ls, _, _ in groups for l_ in ls],
                                                          rest_plan, sum(counts), m_in_t)
    mix_w, up_w, down_w = [((rest_send, rest_recv, rest_bufs[bounds[i]:bounds[i + 1]]), plans[i], groups[i][1], groups[i][2],
                            sum(counts[:i])) for i in range(3)]

    def forward_start(tag, started, after):
        handle, plan, n_halves, group_axes, first = started
        lands = _split_wait("gather_" + tag + "_wait", handle, 0, plan, after, first=first)[1]
        plan = _share_plan([l_.shape[1:] for l_ in lands[:n_halves]], group_axes)
        share, token = _split_start("gather_" + tag + "_share_start", [], lands[:n_halves], plan, 3 * n_halves, after)
        return (share, plan, lands[n_halves:]), token

    def forward_done(tag, forwarded, after):
        share, plan, _ = forwarded
        return _split_wait("gather_" + tag + "_share_wait", share, 0, plan, after)[1]
    wgrp = w_pool_grp[0]
    w_in_t = gather_finish_relayed("in", in_r, tok)[0]
    nsh = N_IN // 4

    zr, w_in_rt = _in_proj(h, w_in_t, PROJ_TILE)
    p, pp = _pool_fwd(zr, wgrp, pool_scale)
    mix_f, tok = forward_start("mix", mix_w, pp)
    wgk4, wconv4 = mix_f[2]
    wgk_full = jnp.transpose(wgk4, (1, 0, 2)).reshape(GATE_RANK, 512) + tok[0, 0]
    wconv_full = jnp.transpose(wconv4, (1, 0, 2)).reshape(3, N_UP)
    wgk_pad = jnp.concatenate([wgk_full, jnp.zeros((128 - GATE_RANK, 512), F32)], axis=0)
    o, og, sp = _gla_fwd(zr, wgk_pad, b_gk, g_gla_head, ts)
    wpp, wgla, wout = forward_done("mix", mix_f, og)
    wgla, wout = wgla.reshape(D, D), wout.reshape(D, D)
    up_f, tok = forward_start("up", up_w, og)
    x1, mixed, yp, yg, h2 = _merge_fwd(xs, zr, pp, og, b_gate, wpp, wgla, wout, g_ffn, tok, ts)
    wup, = forward_done("up", up_f, x1)
    down_f, tok = forward_start("down", down_w, x1)
    u = _matmul_resident(h2, wup, tok, "ffn_up")
    wdown = forward_done("down", down_f, u)[0].reshape(D_FF, D)
    a, conv_out, dx2, dx2b, loss_part, dgfin = _ffn_down_loss(u, x1, tgt, wconv_full, b_conv, wdown,
                                                              g_final.reshape(1, D), tm)

    du, dbconv, dwconv = _ffn_bwd(dx2b, u, conv_out, wconv_full, wdown, tm)
    dw_down = _matmul_tn(a, dx2b, "dw_down", D, tm=D_FF // 2)
    dw_up = _matmul_tn(h2, du, "dw_up", UP_SHARD, shard_major=True)

    def exchange_start(tag, grads, group_axes, after):
        plan = _sibling_plan([g.shape[1:] for g in grads], group_axes)
        lands = [((4,) + _half_shape(g.shape[1:], ax), g.dtype) for g, ax in zip(grads, group_axes)]
        handle, token = _split_start("sibling_" + tag + "_start", grads, lands, plan, len(grads), after)
        return (handle, plan, len(grads)), token

    def partials(tag, names, group_axes, exchange, after):
        handle, plan, n = exchange
        mine, theirs = _split_wait("sibling_" + tag + "_wait", handle, n, plan, after)
        return zip(*[_chip_partial(place, g, t, ax, "chip_partial_" + nm)
                     for nm, ax, g, t in zip(names, group_axes, mine, theirs)])

    ffn_names, ffn_axes = ("w_up", "w_down"), (0, 0)
    ffn_x, token = exchange_start("ffn", [dw_up, dw_down.reshape(4, 704, D)], ffn_axes, du)
    dx1, dx1b, dgffn = _matmul_nt_normbwd(du, wup, x1, g_ffn, dx2, token, "ffn_up_bwd", ts)
    ffn_pf, ffn_pb = partials("ffn", ffn_names, ffn_axes, ffn_x, dx1b)
    ffn_plan = _reduce_plan(2, 0)
    ffn_handle, token = _split_start("reduce_ffn_start", ffn_pb, [((3,) + p.shape[1:], BF16) for p in ffn_pb],
                                     ffn_plan, 6, ffn_pf[0])

    dzr, dyp, dyg, dpp, do, dzog, dbgate, dghead = _merge_bwd(dx1b, zr, yp, yg, o, b_gate, g_gla_head, wpp, wgla, wout,
                                                             token, ts)
    dzr = lax.dynamic_update_slice(dzr, dzog, (0, OFF_OG))
    dw_out = _matmul_tn(mixed, dx1b, "dw_out", D, tm=512)
    dw_gla = _matmul_tn(og, dyg, "dw_gla", D, tm=512)
    dw_pp = _matmul_tn(pp, dyp, "dw_pp", 256, shard_major=True)

    out_names, out_axes = ("w_pool_proj", "w_gla_proj", "w_out"), (0, 0, 0)
    out_x, token = exchange_start("out", [dw_pp, dw_gla.reshape(4, 256, D), dw_out.reshape(4, 256, D)], out_axes, dpp)
    dzr, dwgrp, dscale = _pool_bwd(p, dpp, wgrp, pool_scale, token, dzr)
    out_pf, out_pb = partials("out", out_names, out_axes, out_x, dwgrp)
    out_plan = _reduce_plan(3, 0)
    out_handle, token = _split_start("reduce_out_start", out_pb, [((3,) + p_.shape[1:], BF16) for p_ in out_pb],
                                     out_plan, 9, out_pf[0])
    dq, dk, dzr, dgpre = _gla_bwd(zr, do, sp, wgk_pad, b_gk, token, dzr, ts)
    dzr, dwgk, dbgk = _gk_bwd(dgpre, zr, wgk_pad, dgpre, dzr, ts)
    dzr = lax.dynamic_update_slice(lax.dynamic_update_slice(dzr, dq, (0, OFF_Q)), dk, (0, OFF_K))
    dw_rt = _matmul_tn(dzr, h, "dw_in", D, tm=PROJ_TILE)

    def grad_rows(lo, hi):
        out = []
        for seg_lo, seg_hi, at in ((0, 1536, OFF_POOL), (1536, 3584, OFF_V), (3584, 3600, OFF_GK), (3600, N_IN, OFF_GATE)):
            a_, b_ = max(lo, seg_lo), min(hi, seg_hi)
            if a_ < b_:
                out.append(dw_rt[at + a_ - seg_lo:at + b_ - seg_lo])
        return jnp.concatenate(out, axis=0)

    dw_in_t = jnp.stack([grad_rows(j * nsh, (j + 1) * nsh) for j in range(4)])

    ms = dict(w_in=m_in_t, w_pool_proj=m_w_pool_proj[0], w_gla_proj=m_w_gla_proj[0], w_out=m_w_out[0],
              w_up=m_w_up[0], w_down=m_w_down[0])
    vs = dict(w_in=v_in_t, w_pool_proj=v_w_pool_proj[0], w_gla_proj=v_w_gla_proj[0], w_out=v_w_out[0],
              w_up=v_w_up[0], w_down=v_w_down[0])
    grad, delta, new_m, new_v = {}, {}, {}, {}

    def finish(names, group_axes, part_f, landed):
        return [_finish_half(pf, rb, ax, "finish_" + n) for n, ax, pf, rb in zip(names, group_axes, part_f, landed)]

    def update(names, group_axes, halves, sib_halves):
        for n, ax, mine, theirs in zip(names, group_axes, halves, sib_halves):
            res = _adam_halves(place, shards[n], mine, theirs, ms[n], vs[n], ax, "adam_" + n)
            if n == "w_in":
                res = [jnp.transpose(r_) for r_ in res]
            grad[n], delta[n], new_m[n], new_v[n] = [r_[None] for r_ in res]

    rest_names, rest_axes = ffn_names + out_names, ffn_axes + out_axes
    in_x, token = exchange_start("in", [dw_in_t], (1,), dw_rt)
    _, ffn_landed = _split_wait("reduce_ffn_wait", ffn_handle, 2, ffn_plan, token)
    _, out_landed = _split_wait("reduce_out_wait", out_handle, 3, out_plan, ffn_landed[0])
    rest_halves = lax.optimization_barrier(finish(rest_names, rest_axes, ffn_pf + out_pf, ffn_landed + out_landed))
    (in_pf,), (in_pb,) = partials("in", ("w_in",), (1,), in_x, rest_halves[-1])
    in_plan = _reduce_plan(1, 0)
    in_handle, token = _split_start("reduce_in_start", [in_pb], [((3,) + in_pb.shape[1:], BF16)], in_plan, 3, in_pf)
    rest_plan = _whole_to_sibling_plan(len(rest_halves))
    rest_share, token = _split_start("sibling_share_rest_start", rest_halves, [(h_.shape, F32) for h_ in rest_halves],
                                     rest_plan, len(rest_halves), token)
    grad_x, _, dgmix = _matmul_nt_normbwd(dzr, w_in_rt, xs, g_mix, dx1, token, "in_proj_bwd", ts, transposed=True)
    small_names = ("g_mix", "b_gate", "w_gk_up", "b_gk", "w_pool_grp", "pool_scale", "g_gla_head", "g_ffn", "w_conv",
                   "b_conv", "g_final")
    small_mine = [dgmix, dbgate, dwgk[:GATE_RANK], dbgk, dwgrp.reshape(4 * 128, 128), dscale, dghead, dgffn, dwconv, dbconv,
                  dgfin, loss_part]
    small_sib = _sibling_exchange([], (), small_mine, "sibling_exchange_small")
    small_chip = _add_many(small_mine, small_sib, "chip_partial_small")
    small_plan = _reduce_plan(0, len(small_chip))
    small_handle, token = _split_start("reduce_small_start", small_chip, [((4,) + a_.shape, F32) for a_ in small_chip],
                                       small_plan, 3 * len(small_chip), small_mine[0])

    rest_halves, rest_sib = _split_wait("sibling_share_rest_wait", rest_share, len(rest_halves), rest_plan, token)
    n_ffn = len(ffn_names)
    update(out_names, out_axes, rest_halves[n_ffn:], rest_sib[n_ffn:])
    updated = lax.optimization_barrier([delta[n] for n in out_names])
    _, in_landed = _split_wait("reduce_in_wait", in_handle, 1, in_plan, updated[0])
    in_halves = finish(("w_in",), (1,), (in_pf,), in_landed)
    update(("w_in",), (1,), in_halves, _sibling_share(in_halves, "sibling_share_in"))
    ffn_halves, _ = lax.optimization_barrier((rest_halves[:n_ffn], delta["w_in"]))
    update(ffn_names, ffn_axes, ffn_halves, rest_sib[:n_ffn])
    small_sent, small_landed = _split_wait("reduce_small_wait", small_handle, len(small_chip), small_plan, delta["w_in"])
    given = dict(g_mix=(g_mix, m_g_mix, v_g_mix), b_gate=(b_gate, m_b_gate, v_b_gate), w_gk_up=(w_gk_up, m_w_gk_up, v_w_gk_up),
                 b_gk=(b_gk, m_b_gk, v_b_gk), w_pool_grp=(w_pool_grp, m_w_pool_grp, v_w_pool_grp),
                 pool_scale=(pool_scale, m_pool_scale, v_pool_scale), g_gla_head=(g_gla_head, m_g_gla_head, v_g_gla_head),
                 g_ffn=(g_ffn, m_g_ffn, v_g_ffn), w_conv=(w_conv, m_w_conv, v_w_conv), b_conv=(b_conv, m_b_conv, v_b_conv),
                 g_final=(g_final, m_g_final, v_g_final))
    flat2 = lambda a: a.reshape(-1, a.shape[-1])
    widths = [dict(w_gk_up=HK, w_conv=UP_SHARD).get(n) for n in small_names]
    totals, ds, mo, vo = _adam_small(place, small_sent, small_landed, *[[flat2(given[n][k]) for n in small_names] for k in range(3)],
                                     widths)
    loss = totals[-1][0, 0]
    for i, n in enumerate(small_names):
        shp = given[n][0].shape
        grad[n], delta[n], new_m[n], new_v[n] = [r_.reshape(shp) for r_ in (totals[i], ds[i], mo[i], vo[i])]

    order = ("g_mix", "w_in", "b_gate", "w_gk_up", "b_gk", "w_pool_grp", "pool_scale", "g_gla_head", "w_pool_proj",
             "w_gla_proj", "w_out", "g_ffn", "w_up", "w_conv", "b_conv", "w_down", "g_final")
    return (loss, grad_x[None], *[grad[n] for n in order], *[delta[n] for n in order], *[new_m[n] for n in order],
            *[new_v[n] for n in order])
```

```python
import jax
import jax.numpy as jnp
from jax import lax
from jax.experimental import pallas as pl
from jax.experimental.pallas import tpu as pltpu

F32 = jnp.float32
BF16 = jnp.bfloat16
MESH = pl.DeviceIdType.MESH

D = 1024
EPS = 1e-6
CHUNK = 64
POOL_W = 512
POOL_WINDOWS = (2, 4, 8, 16)
HEADS = 4
HK = 128
HV = 256
GATE_RANK = 16
D_FF = 2816
N_UP = 2 * D_FF
N_IN = 5648
QSCALE = HK ** -0.5
N_INR = 5760
OFF_GATE, OFF_V, OFF_OG, OFF_POOL, OFF_Q, OFF_K, OFF_GK = 0, 2048, 3072, 4096, 4608, 5120, 5632

ADAM_LR, ADAM_B1, ADAM_B2, ADAM_EPS, ADAM_WD, ADAM_STEP = 0.001, 0.9, 0.999, 1e-08, 0.01, 10

VMEM_LIMIT = 56 * 1024 * 1024
PROJ_TILE = N_INR // 5
UP_SHARD = N_UP // 4


def _cp(*sem):
    return pltpu.CompilerParams(dimension_semantics=sem if sem else None, vmem_limit_bytes=VMEM_LIMIT)


def _dot(a, b):
    return jnp.dot(a, b, preferred_element_type=F32)


def _dot_nt(a, b):
    return lax.dot_general(a, b, (((1,), (1,)), ((), ())), preferred_element_type=F32)


def _dot_tn(a, b):
    return lax.dot_general(a, b, (((0,), (0,)), ((), ())), preferred_element_type=F32)


def _sigmoid(v):
    return 1.0 / (1.0 + jnp.exp(-v))


def _rows(shape):
    return lax.broadcasted_iota(jnp.int32, shape, 0)


def _pick_row(v, r):
    return jnp.sum(jnp.where(_rows(v.shape) == r, v, 0.0), axis=0, keepdims=True)


def _rmsnorm(x, g, after, name, ts):
    s = x.shape[0]

    def body(x_ref, g_ref, after_ref, h_ref):
        xv = x_ref[...]
        r = lax.rsqrt(jnp.mean(xv * xv, axis=-1, keepdims=True) + EPS)
        h_ref[...] = (xv * r * g_ref[...]).astype(BF16)

    return pl.pallas_call(
        body, name=name, grid=(s // ts,),
        in_specs=[pl.BlockSpec((ts, D), lambda i: (i, 0)), pl.BlockSpec((1, D), lambda i: (0, 0)), ANY],
        out_specs=pl.BlockSpec((ts, D), lambda i: (i, 0)), out_shape=jax.ShapeDtypeStruct((s, D), BF16),
        compiler_params=_cp("arbitrary"),
    )(x, g, after)


MM_ROWS = 512


def _matmul_resident(h, w, after, name):
    s = h.shape[0]
    nj, tn = w.shape[0], w.shape[2]
    rc = min(s, MM_ROWS)

    def body(h_ref, w_ref, after_ref, z_ref):
        for r0 in range(0, s, rc):
            z_ref[r0:r0 + rc, :] = _dot(h_ref[r0:r0 + rc, :], w_ref[...]).astype(BF16)

    return pl.pallas_call(
        body, name=name, grid=(nj,),
        in_specs=[pl.BlockSpec((s, D), lambda j: (0, 0)), pl.BlockSpec((None, D, tn), lambda j: (j, 0, 0)), ANY],
        out_specs=pl.BlockSpec((s, tn), lambda j: (0, j)), out_shape=jax.ShapeDtypeStruct((s, nj * tn), BF16),
        compiler_params=_cp("arbitrary"),
    )(h, w, after)


PROJ_PIECES = ((3600, 2048, OFF_GATE), (1536, 2048, OFF_V), (0, 1536, OFF_POOL), (3584, GATE_RANK, OFF_GK))


def _split_by_shard(pieces, rows_per_shard):
    out = []
    for src, n, dst in pieces:
        while n > 0:
            j, r = divmod(src, rows_per_shard)
            m = min(n, rows_per_shard - r)
            out.append((j, r, m, dst))
            src, n, dst = src + m, n - m, dst + m
    return tuple(out)


PROJ_SEGMENTS = _split_by_shard(PROJ_PIECES, N_IN // 4)


def _in_proj(h, w4, tn):
    s = h.shape[0]
    rc = min(s, MM_ROWS)
    nj = N_INR // tn
    first_use = [dst // tn for _, _, _, dst in PROJ_SEGMENTS]

    def body(h_ref, w_hbm, z_ref, wo_hbm, w_ref, stage, sems, out_sem):
        j = pl.program_id(0)
        cps = [pltpu.make_async_copy(w_hbm.at[k], stage.at[k], sems.at[k]) for k in range(4)]
        out_cp = pltpu.make_async_copy(w_ref, wo_hbm, out_sem.at[0])

        @pl.when(j == 0)
        def _():
            for cp in cps:
                cp.start()
            w_ref[OFF_GK + GATE_RANK:, :] = jnp.zeros((N_INR - OFF_GK - GATE_RANK, D), BF16)

        landed = set()
        for step in range(nj):
            due = [seg for seg, at in zip(PROJ_SEGMENTS, first_use) if at == step]
            if due:
                fresh = sorted({seg[0] for seg in due} - landed)
                landed.update(fresh)

                @pl.when(j == step)
                def _(due=due, fresh=fresh, last=step == max(first_use)):
                    for k in fresh:
                        cps[k].wait()
                    for k, r, n, dst in due:
                        w_ref[dst:dst + n, :] = stage[k, r:r + n, :]
                    if last:
                        out_cp.start()

        wt = w_ref[pl.ds(pl.multiple_of(j * tn, 128), tn), :]
        for r0 in range(0, s, rc):
            z_ref[r0:r0 + rc, :] = _dot_nt(h_ref[r0:r0 + rc, :], wt).astype(BF16)

        @pl.when(j == nj - 1)
        def _():
            out_cp.wait()

    return pl.pallas_call(
        body, name="in_proj", grid=(nj,),
        in_specs=[pl.BlockSpec((s, D), lambda j: (0, 0)), ANY],
        out_specs=[pl.BlockSpec((s, tn), lambda j: (0, j)), ANY],
        out_shape=[jax.ShapeDtypeStruct((s, N_INR), BF16), jax.ShapeDtypeStruct((N_INR, D), BF16)],
        scratch_shapes=[pltpu.VMEM((N_INR, D), BF16), pltpu.VMEM(w4.shape, BF16), pltpu.SemaphoreType.DMA((4,)),
                        pltpu.SemaphoreType.DMA((1,))],
        compiler_params=_cp("arbitrary"),
    )(h, w4)


def _matmul_nt_normbwd(dz, w, x, g, resid, after, name, ts, transposed=False):
    s = x.shape[0]
    w_vmem = w.shape if transposed else (D, w.shape[0] * w.shape[2])
    n_sems = 1 if transposed else w.shape[0]

    def body(dz_ref, w_hbm, x_ref, g_ref, r_ref, after_ref, o_ref, ob_ref, dg_ref, w_ref, sems):
        @pl.when(pl.program_id(0) == 0)
        def _():
            if transposed:
                cps = [pltpu.make_async_copy(w_hbm, w_ref, sems.at[0])]
            else:
                kc = w.shape[2]
                cps = [pltpu.make_async_copy(w_hbm.at[j], w_ref.at[:, pl.ds(j * kc, kc)], sems.at[j])
                       for j in range(w.shape[0])]
            for cp in cps:
                cp.start()
            for cp in cps:
                cp.wait()
            dg_ref[...] = jnp.zeros_like(dg_ref)

        dh = _dot(dz_ref[...], w_ref[...]) if transposed else _dot_nt(dz_ref[...], w_ref[...])
        xv = x_ref[...]
        r = lax.rsqrt(jnp.mean(xv * xv, axis=-1, keepdims=True) + EPS)
        xh = xv * r
        dg_ref[...] += jnp.sum(dh * xh, axis=0, keepdims=True)
        dxh = dh * g_ref[...]
        out = r_ref[...] + r * (dxh - xh * jnp.mean(dxh * xh, axis=-1, keepdims=True))
        o_ref[...] = out
        ob_ref[...] = out.astype(BF16)

    row = lambda i: (i, 0)
    kdim = dz.shape[1]
    return pl.pallas_call(
        body, name=name, grid=(s // ts,),
        in_specs=[pl.BlockSpec((ts, kdim), row), ANY, pl.BlockSpec((ts, D), row),
                  pl.BlockSpec((1, D), lambda i: (0, 0)), pl.BlockSpec((ts, D), row), ANY],
        out_specs=[pl.BlockSpec((ts, D), row), pl.BlockSpec((ts, D), row), pl.BlockSpec((1, D), lambda i: (0, 0))],
        out_shape=[jax.ShapeDtypeStruct((s, D), F32), jax.ShapeDtypeStruct((s, D), BF16),
                   jax.ShapeDtypeStruct((1, D), F32)],
        scratch_shapes=[pltpu.VMEM(w_vmem, BF16), pltpu.SemaphoreType.DMA((n_sems,))],
        compiler_params=_cp("arbitrary"),
    )(dz, w, x, g, resid, after)


def _matmul_tn(a, b, name, tn, shard_major=False, tm=None):
    s, m = a.shape
    n = b.shape[1]
    tm = m if tm is None else tm
    ni, nj = m // tm, n // tn

    def body(a_ref, b_ref, o_ref):
        o_ref[...] = _dot_tn(a_ref[...], b_ref[...]).astype(BF16)

    if shard_major:
        out_spec = pl.BlockSpec((None, tm, tn), lambda i, j: (j, i, 0))
        out_shape = jax.ShapeDtypeStruct((nj, m, tn), BF16)
    else:
        out_spec = pl.BlockSpec((tm, tn), lambda i, j: (i, j))
        out_shape = jax.ShapeDtypeStruct((m, n), BF16)
    return pl.pallas_call(
        body, name=name, grid=(ni, nj),
        in_specs=[pl.BlockSpec((s, tm), lambda i, j: (0, i)), pl.BlockSpec((s, tn), lambda i, j: (0, j))],
        out_specs=out_spec, out_shape=out_shape,
        compiler_params=_cp("arbitrary", "arbitrary"),
    )(a, b)


def _pool_fwd(zr, wgrp, scale):
    s = zr.shape[0]

    def body(u_ref, w_ref, sc_ref, p_ref, pp_ref):
        row = _rows((s, 128))
        for gi, win in enumerate(POOL_WINDOWS):
            cs = slice(gi * 128, (gi + 1) * 128)
            u = u_ref[:, cs].astype(F32)
            acc, k = u, 1
            while k < win:
                acc = acc + jnp.where(row >= k, pltpu.roll(acc, k, 0), 0.0)
                k *= 2
            cnt = jnp.minimum(row + 1, win).astype(F32)
            p = (acc / cnt - u).astype(BF16)
            p_ref[:, cs] = p
            pp_ref[:, cs] = (_dot(p, w_ref[gi].astype(BF16)) * sc_ref[:, cs]).astype(BF16)

    return pl.pallas_call(
        body, name="pool_fwd", grid=(1,),
        in_specs=[pl.BlockSpec((s, POOL_W), lambda i: (0, OFF_POOL // POOL_W)),
                  pl.BlockSpec((4, 128, 128), lambda i: (0, 0, 0)), pl.BlockSpec((1, POOL_W), lambda i: (0, 0))],
        out_specs=[pl.BlockSpec((s, POOL_W), lambda i: (0, 0))] * 2,
        out_shape=[jax.ShapeDtypeStruct((s, POOL_W), BF16)] * 2,
        compiler_params=_cp("arbitrary"),
    )(zr, wgrp, scale)


def _pool_bwd(p, dpp, wgrp, scale, after, dz):
    s = p.shape[0]

    def body(p_ref, dpp_ref, w_ref, sc_ref, after_ref, dz_in, dz_ref, dw_ref, dsc_ref):
        row = _rows((s, 128))
        for gi, win in enumerate(POOL_WINDOWS):
            cs = slice(gi * 128, (gi + 1) * 128)
            pv = p_ref[:, cs]
            wb = w_ref[gi].astype(BF16)
            dpp_v = dpp_ref[:, cs].astype(F32)
            dsc_ref[:, cs] = jnp.sum(dpp_v * _dot(pv, wb), axis=0, keepdims=True)
            dpm = (dpp_v * sc_ref[:, cs]).astype(BF16)
            dw_ref[gi] = _dot_tn(pv, dpm)
            dp = _dot_nt(dpm, wb)
            cnt = jnp.minimum(row + 1, win).astype(F32)
            acc, k = dp / cnt, 1
            while k < win:
                acc = acc + jnp.where(row < s - k, pltpu.roll(acc, s - k, 0), 0.0)
                k *= 2
            dz_ref[:, cs] = (acc - dp).astype(BF16)

    full = lambda i: (0, 0)
    return pl.pallas_call(
        body, name="pool_bwd", grid=(1,),
        in_specs=[pl.BlockSpec((s, POOL_W), full), pl.BlockSpec((s, POOL_W), full),
                  pl.BlockSpec((4, 128, 128), lambda i: (0, 0, 0)), pl.BlockSpec((1, POOL_W), full), ANY, ANY],
        out_specs=[pl.BlockSpec((s, POOL_W), lambda i: (0, OFF_POOL // POOL_W)),
                   pl.BlockSpec((4, 128, 128), lambda i: (0, 0, 0)), pl.BlockSpec((1, POOL_W), full)],
        out_shape=[jax.ShapeDtypeStruct(dz.shape, BF16), jax.ShapeDtypeStruct((4, 128, 128), F32),
                   jax.ShapeDtypeStruct((1, POOL_W), F32)],
        input_output_aliases={5: 0},
        compiler_params=_cp("arbitrary"),
    )(p, dpp, wgrp, scale, after, dz)


def _gla_decay(zgk_ref, wgk_ref, bgk_ref, rb):
    g = _dot(zgk_ref[...], wgk_ref[...].astype(BF16)) + bgk_ref[...]
    la = (jnp.minimum(g, 0.0) - jnp.log(1.0 + jnp.exp(-jnp.abs(g)))) * (1.0 / 16.0)
    rowm = _rows(la.shape) & (CHUNK - 1)
    bc, k = la, 1
    while k < CHUNK:
        bc = bc + jnp.where(rowm >= k, pltpu.roll(bc, k, 0), 0.0)
        k *= 2
    return g, jnp.exp(bc), jnp.exp(-bc)


GLA_HB = 4


def _gla_specs(rb, rmap):
    wk, wv = GLA_HB * HK, GLA_HB * HV
    return [pl.BlockSpec((rb, wk), lambda h, r: (rmap(h, r), OFF_Q // wk + h)),
            pl.BlockSpec((rb, wk), lambda h, r: (rmap(h, r), OFF_K // wk + h)),
            pl.BlockSpec((rb, wv), lambda h, r: (rmap(h, r), OFF_V // wv + h)),
            pl.BlockSpec((rb, 128), lambda h, r: (rmap(h, r), OFF_GK // 128))]


def _gla_fwd(zr, wgk, bgk, ghead, rb):
    s = zr.shape[0]
    nc = rb // CHUNK
    wk, wv = GLA_HB * HK, GLA_HB * HV

    def body(q_ref, k_ref, v_ref, zgk_ref, zog_ref, wgk_ref, bgk_ref, gh_ref, o_ref, og_ref, sp_ref, st_ref, kv_ref):
        @pl.when(pl.program_id(1) == 0)
        def _():
            st_ref[...] = jnp.zeros_like(st_ref)

        _, e_pos, e_neg = _gla_decay(zgk_ref, wgk_ref, bgk_ref, rb)
        lower = _rows((CHUNK, CHUNK)) >= lax.broadcasted_iota(jnp.int32, (CHUNK, CHUNK), 1)
        pairs = [(c, hh) for c in range(nc) for hh in range(GLA_HB)]
        rows = lambda c: slice(c * CHUNK, (c + 1) * CHUNK)
        cols_k = lambda hh: slice(hh * HK, (hh + 1) * HK)
        cols_v = lambda hh: slice(hh * HV, (hh + 1) * HV)
        qfws, pms, e_lasts = {}, {}, {}
        for c, hh in pairs:
            q = q_ref[rows(c), cols_k(hh)].astype(F32) * QSCALE
            k = k_ref[rows(c), cols_k(hh)].astype(F32)
            ec, fc = e_pos[rows(c), cols_k(hh)], e_neg[rows(c), cols_k(hh)]
            qfw = (q * ec).astype(BF16)
            kfw_f = k * fc
            s_fw = _dot_nt(qfw, kfw_f.astype(BF16))
            s_bw = _dot_nt((q * fc).astype(BF16), (k * ec).astype(BF16))
            e_last = _pick_row(ec, CHUNK - 1)
            kv_ref[c, hh] = _dot_tn(v_ref[rows(c), cols_v(hh)], (kfw_f * e_last).astype(BF16))
            qfws[c, hh], pms[c, hh], e_lasts[c, hh] = qfw, jnp.where(lower, s_fw, s_bw).astype(BF16), e_last
        for hh in range(GLA_HB):
            st = st_ref[hh]
            for c in range(nc):
                sp_ref[c, hh] = st.astype(BF16)
                st = st * e_lasts[c, hh] + kv_ref[c, hh]
            st_ref[hh] = st
        for c, hh in pairs:
            o = _dot(pms[c, hh], v_ref[rows(c), cols_v(hh)]) + _dot_nt(qfws[c, hh], sp_ref[c, hh])
            r = lax.rsqrt(jnp.mean(o * o, axis=-1, keepdims=True) + EPS)
            zo = zog_ref[rows(c), cols_v(hh)].astype(F32)
            o_ref[rows(c), cols_v(hh)] = o.astype(BF16)
            og_ref[rows(c), cols_v(hh)] = (o * r * gh_ref[...] * zo * _sigmoid(zo)).astype(BF16)

    rmap = lambda h, r: r
    return pl.pallas_call(
        body, name="gla_fwd", grid=(HEADS // GLA_HB, s // rb),
        in_specs=_gla_specs(rb, rmap) + [
            pl.BlockSpec((rb, wv), lambda h, r: (r, OFF_OG // wv + h)),
            pl.BlockSpec((128, wk), lambda h, r: (0, h)), pl.BlockSpec((1, wk), lambda h, r: (0, h)),
            pl.BlockSpec((1, HV), lambda h, r: (0, 0))],
        out_specs=[pl.BlockSpec((rb, wv), lambda h, r: (r, h)), pl.BlockSpec((rb, wv), lambda h, r: (r, h)),
                   pl.BlockSpec((nc, GLA_HB, HV, HK), lambda h, r: (r, h, 0, 0))],
        out_shape=[jax.ShapeDtypeStruct((s, D), BF16), jax.ShapeDtypeStruct((s, D), BF16),
                   jax.ShapeDtypeStruct((s // CHUNK, HEADS, HV, HK), BF16)],
        scratch_shapes=[pltpu.VMEM((GLA_HB, HV, HK), F32), pltpu.VMEM((nc, GLA_HB, HV, HK), F32)],
        compiler_params=_cp("arbitrary", "arbitrary"),
    )(zr, zr, zr, zr, zr, wgk, bgk, ghead)


def _gla_bwd(zr, do, sp, wgk, bgk, after, dz, rb):
    s = zr.shape[0]
    nc = rb // CHUNK
    nr = s // rb
    wk, wv = GLA_HB * HK, GLA_HB * HV

    def body(q_ref, k_ref, v_ref, zgk_ref, do_ref, sp_ref, wgk_ref, bgk_ref, after_ref, dz_in, dq_ref, dk_ref, dv_ref,
             dg_ref, gt_ref, dbc_ref, gs_ref):
        @pl.when(pl.program_id(1) == 0)
        def _():
            gt_ref[...] = jnp.zeros_like(gt_ref)

        g, e_pos, e_neg = _gla_decay(zgk_ref, wgk_ref, bgk_ref, rb)
        lower = _rows((CHUNK, CHUNK)) >= lax.broadcasted_iota(jnp.int32, (CHUNK, CHUNK), 1)
        is_last = _rows((CHUNK, HK)) == CHUNK - 1
        pairs = [(c, hh) for c in range(nc) for hh in range(GLA_HB)]
        rows = lambda c: slice(c * CHUNK, (c + 1) * CHUNK)
        cols_k = lambda hh: slice(hh * HK, (hh + 1) * HK)
        cols_v = lambda hh: slice(hh * HV, (hh + 1) * HV)
        e_lasts = {}
        for c, hh in pairs:
            ec = e_pos[rows(c), cols_k(hh)]
            qfw = (q_ref[rows(c), cols_k(hh)].astype(F32) * QSCALE * ec).astype(BF16)
            gs_ref[c, hh] = _dot_tn(do_ref[rows(c), cols_v(hh)], qfw)
            e_lasts[c, hh] = _pick_row(ec, CHUNK - 1)
        for hh in range(GLA_HB):
            gt = gt_ref[hh]
            for c in reversed(range(nc)):
                own = gs_ref[c, hh]
                gs_ref[c, hh] = gt
                gt = own + gt * e_lasts[c, hh]
            gt_ref[hh] = gt
        def decayed(c, hh):
            q = q_ref[rows(c), cols_k(hh)].astype(F32) * QSCALE
            k = k_ref[rows(c), cols_k(hh)].astype(F32)
            ec, fc = e_pos[rows(c), cols_k(hh)], e_neg[rows(c), cols_k(hh)]
            return ec, fc, q * ec, k * fc, q * fc, k * ec

        pms, dss = {}, {}
        for c, hh in pairs:
            _, _, qfw_f, kfw_f, qbw_f, kbw_f = decayed(c, hh)
            s_fw = _dot_nt(qfw_f.astype(BF16), kfw_f.astype(BF16))
            s_bw = _dot_nt(qbw_f.astype(BF16), kbw_f.astype(BF16))
            dp = _dot_nt(do_ref[rows(c), cols_v(hh)], v_ref[rows(c), cols_v(hh)])
            pms[c, hh] = jnp.where(lower, s_fw, s_bw).astype(BF16)
            dss[c, hh] = (jnp.where(lower, dp, 0.0).astype(BF16), jnp.where(lower, 0.0, dp).astype(BF16))
        for c, hh in pairs:
            sl, ck, cv = rows(c), cols_k(hh), cols_v(hh)
            v = v_ref[sl, cv]
            dov = do_ref[sl, cv]
            ec, fc, qfw_f, kfw_f, qbw_f, kbw_f = decayed(c, hh)
            qfw, kfw, qbw, kbw = qfw_f.astype(BF16), kfw_f.astype(BF16), qbw_f.astype(BF16), kbw_f.astype(BF16)
            pm = pms[c, hh]
            e_last = e_lasts[c, hh]
            kdec = (kfw_f * e_last).astype(BF16)
            gt = gs_ref[c, hh]
            gtb = gt.astype(BF16)
            spv = sp_ref[c, hh]
            dv_ref[sl, cv] = (_dot_tn(pm, dov) + _dot_nt(kdec, gtb)).astype(BF16)
            ds_fw, ds_bw = dss[c, hh]
            dqfw = _dot(ds_fw, kfw) + _dot(dov, spv)
            dkfw = _dot_tn(ds_fw, qfw)
            dqbw = _dot(ds_bw, kbw)
            dkbw = _dot_tn(ds_bw, qbw)
            dkdec = _dot(v, gtb)
            de_last = (jnp.sum(gt * spv.astype(F32), axis=0, keepdims=True)
                       + jnp.sum(dkdec * kfw_f, axis=0, keepdims=True))
            dkfw = dkfw + dkdec * e_last
            dq_ref[sl, ck] = ((dqfw * ec + dqbw * fc) * QSCALE).astype(BF16)
            dk_ref[sl, ck] = (dkfw * fc + dkbw * ec).astype(BF16)
            dbc = dqfw * qfw_f - dqbw * qbw_f + dkbw * kbw_f - dkfw * kfw_f
            dbc_ref[sl, ck] = dbc + jnp.where(is_last, de_last * e_last, 0.0)
        rowm = _rows((rb, wk)) & (CHUNK - 1)
        dla, kk = dbc_ref[...], 1
        while kk < CHUNK:
            dla = dla + jnp.where(rowm < CHUNK - kk, pltpu.roll(dla, rb - kk, 0), 0.0)
            kk *= 2
        dg_ref[...] = dla * (1.0 / 16.0) * _sigmoid(-g)

    rmap = lambda h, r: nr - 1 - r
    rev = lambda h, r: (nr - 1 - r, h)
    return pl.pallas_call(
        body, name="gla_bwd", grid=(HEADS // GLA_HB, nr),
        in_specs=_gla_specs(rb, rmap) + [
            pl.BlockSpec((rb, wv), rev),
            pl.BlockSpec((nc, GLA_HB, HV, HK), lambda h, r: (nr - 1 - r, h, 0, 0)),
            pl.BlockSpec((128, wk), lambda h, r: (0, h)), pl.BlockSpec((1, wk), lambda h, r: (0, h)), ANY, ANY],
        out_specs=[pl.BlockSpec((rb, wk), rev), pl.BlockSpec((rb, wk), rev),
                   pl.BlockSpec((rb, wv), lambda h, r: (nr - 1 - r, OFF_V // wv + h)), pl.BlockSpec((rb, wk), rev)],
        out_shape=[jax.ShapeDtypeStruct((s, HEADS * HK), BF16), jax.ShapeDtypeStruct((s, HEADS * HK), BF16),
                   jax.ShapeDtypeStruct(dz.shape, BF16), jax.ShapeDtypeStruct((s, HEADS * HK), F32)],
        scratch_shapes=[pltpu.VMEM((GLA_HB, HV, HK), F32), pltpu.VMEM((rb, wk), F32),
                        pltpu.VMEM((nc, GLA_HB, HV, HK), F32)],
        input_output_aliases={9: 2},
        compiler_params=_cp("arbitrary", "arbitrary"),
    )(zr, zr, zr, zr, do, sp, wgk, bgk, after, dz)


def _gk_bwd(dgpre, zr, wgk, after, dz, ts):
    s = zr.shape[0]

    def body(dg_ref, zgk_ref, w_ref, after_ref, dz_in, dz_ref, dw_ref, db_ref):
        @pl.when(pl.program_id(0) == 0)
        def _():
            dw_ref[...] = jnp.zeros_like(dw_ref)
            db_ref[...] = jnp.zeros_like(db_ref)

        dg = dg_ref[...]
        dgb = dg.astype(BF16)
        dz_ref[...] = _dot_nt(dgb, w_ref[...].astype(BF16)).astype(BF16)
        dw_ref[...] += _dot_tn(zgk_ref[...], dgb)
        db_ref[...] += jnp.sum(dg, axis=0, keepdims=True)

    return pl.pallas_call(
        body, name="gk_bwd", grid=(s // ts,),
        in_specs=[pl.BlockSpec((ts, 512), lambda i: (i, 0)), pl.BlockSpec((ts, 128), lambda i: (i, OFF_GK // 128)),
                  pl.BlockSpec((128, 512), lambda i: (0, 0)), ANY, ANY],
        out_specs=[pl.BlockSpec((ts, 128), lambda i: (i, OFF_GK // 128)), pl.BlockSpec((128, 512), lambda i: (0, 0)),
                   pl.BlockSpec((1, 512), lambda i: (0, 0))],
        out_shape=[jax.ShapeDtypeStruct(dz.shape, BF16), jax.ShapeDtypeStruct((128, 512), F32),
                   jax.ShapeDtypeStruct((1, 512), F32)],
        input_output_aliases={4: 0},
        compiler_params=_cp("arbitrary"),
    )(dgpre, zr, wgk, after, dz)


def _merge_fwd(x, zr, pp, og, bgate, wpp, wgla, wout, gffn, after, ts):
    s = x.shape[0]

    def body(x_ref, z0_ref, z1_ref, pp_ref, og_ref, bg_ref, wpp_ref, wgla_ref, wout_ref, gf_ref, after_ref,
             x1_ref, mix_ref, yp_ref, yg_ref, h2_ref):
        ppv = pp_ref[...]
        yp = jnp.concatenate([_dot(ppv, wpp_ref[j]) for j in range(4)], axis=1)
        yg = _dot(og_ref[...], wgla_ref[...])
        g0 = _sigmoid(z0_ref[...].astype(F32) + bg_ref[:, :D])
        g1 = _sigmoid(z1_ref[...].astype(F32) + bg_ref[:, D:])
        mixed = (g0 * yp + g1 * yg).astype(BF16)
        x1 = x_ref[...] + _dot(mixed, wout_ref[...])
        x1_ref[...] = x1
        mix_ref[...] = mixed
        yp_ref[...] = yp.astype(BF16)
        yg_ref[...] = yg.astype(BF16)
        r = lax.rsqrt(jnp.mean(x1 * x1, axis=-1, keepdims=True) + EPS)
        h2_ref[...] = (x1 * r * gf_ref[...]).astype(BF16)

    row = lambda i: (i, 0)
    const2 = lambda i: (0, 0)
    return pl.pallas_call(
        body, name="merge_fwd", grid=(s // ts,),
        in_specs=[pl.BlockSpec((ts, D), row), pl.BlockSpec((ts, D), lambda i: (i, 0)), pl.BlockSpec((ts, D), lambda i: (i, 1)),
                  pl.BlockSpec((ts, POOL_W), row), pl.BlockSpec((ts, D), row), pl.BlockSpec((1, 2 * D), const2),
                  pl.BlockSpec((4, POOL_W, 256), lambda i: (0, 0, 0)), pl.BlockSpec((D, D), const2),
                  pl.BlockSpec((D, D), const2), pl.BlockSpec((1, D), const2), ANY],
        out_specs=[pl.BlockSpec((ts, D), row)] * 5,
        out_shape=[jax.ShapeDtypeStruct((s, D), F32)] + [jax.ShapeDtypeStruct((s, D), BF16)] * 4,
        compiler_params=_cp("arbitrary"),
    )(x, zr, zr, pp, og, bgate, wpp, wgla, wout, gffn, after)


def _merge_bwd(dx1b, zr, yp, yg, o, bgate, ghead, wpp, wgla, wout, after, ts):
    s = dx1b.shape[0]

    def body(dx_ref, z0_ref, z1_ref, zog_ref, yp_ref, yg_ref, o_ref, bg_ref, gh_ref, wpp_ref, wgla_ref, wout_ref, after_ref,
             dzg_ref, dyp_ref, dyg_ref, dpp_ref, do_ref, dzog_ref, dbg_ref, dgh_ref):
        @pl.when(pl.program_id(0) == 0)
        def _():
            dbg_ref[...] = jnp.zeros_like(dbg_ref)
            dgh_ref[...] = jnp.zeros_like(dgh_ref)

        dmix = _dot_nt(dx_ref[...], wout_ref[...])
        g0 = _sigmoid(z0_ref[...].astype(F32) + bg_ref[:, :D])
        g1 = _sigmoid(z1_ref[...].astype(F32) + bg_ref[:, D:])
        dypb = (dmix * g0).astype(BF16)
        dygb = (dmix * g1).astype(BF16)
        dz0 = dmix * yp_ref[...].astype(F32) * g0 * (1.0 - g0)
        dz1 = dmix * yg_ref[...].astype(F32) * g1 * (1.0 - g1)
        dzg_ref[:, :D] = dz0.astype(BF16)
        dzg_ref[:, D:] = dz1.astype(BF16)
        dbg_ref[:, :D] += jnp.sum(dz0, axis=0, keepdims=True)
        dbg_ref[:, D:] += jnp.sum(dz1, axis=0, keepdims=True)
        dyp_ref[...] = dypb
        dyg_ref[...] = dygb
        dpp = _dot_nt(dypb[:, 0:256], wpp_ref[0])
        for j in range(1, 4):
            dpp = dpp + _dot_nt(dypb[:, j * 256:(j + 1) * 256], wpp_ref[j])
        dpp_ref[...] = dpp.astype(BF16)
        dog = _dot_nt(dygb, wgla_ref[...])
        gh = gh_ref[...]
        dgh = jnp.zeros((1, HV), F32)
        for h in range(HEADS):
            cs = slice(h * HV, (h + 1) * HV)
            ov = o_ref[:, cs].astype(F32)
            r = lax.rsqrt(jnp.mean(ov * ov, axis=-1, keepdims=True) + EPS)
            oh = ov * r
            zo = zog_ref[:, cs].astype(F32)
            sg = _sigmoid(zo)
            dog_h = dog[:, cs]
            don = dog_h * zo * sg
            dzog_ref[:, cs] = (dog_h * oh * gh * sg * (1.0 + zo * (1.0 - sg))).astype(BF16)
            dgh = dgh + jnp.sum(don * oh, axis=0, keepdims=True)
            doh = don * gh
            do_ref[:, cs] = (r * (doh - oh * jnp.mean(doh * oh, axis=-1, keepdims=True))).astype(BF16)
        dgh_ref[...] += dgh

    row = lambda i: (i, 0)
    const2 = lambda i: (0, 0)
    return pl.pallas_call(
        body, name="merge_bwd", grid=(s // ts,),
        in_specs=[pl.BlockSpec((ts, D), row), pl.BlockSpec((ts, D), lambda i: (i, 0)), pl.BlockSpec((ts, D), lambda i: (i, 1)),
                  pl.BlockSpec((ts, D), lambda i: (i, OFF_OG // D)), pl.BlockSpec((ts, D), row), pl.BlockSpec((ts, D), row),
                  pl.BlockSpec((ts, D), row), pl.BlockSpec((1, 2 * D), const2), pl.BlockSpec((1, HV), const2),
                  pl.BlockSpec((4, POOL_W, 256), lambda i: (0, 0, 0)), pl.BlockSpec((D, D), const2),
                  pl.BlockSpec((D, D), const2), ANY],
        out_specs=[pl.BlockSpec((ts, 2 * D), row), pl.BlockSpec((ts, D), row), pl.BlockSpec((ts, D), row),
                   pl.BlockSpec((ts, POOL_W), row), pl.BlockSpec((ts, D), row), pl.BlockSpec((ts, D), row),
                   pl.BlockSpec((1, 2 * D), const2), pl.BlockSpec((1, HV), const2)],
        out_shape=[jax.ShapeDtypeStruct((s, N_INR), BF16), jax.ShapeDtypeStruct((s, D), BF16),
                   jax.ShapeDtypeStruct((s, D), BF16), jax.ShapeDtypeStruct((s, POOL_W), BF16),
                   jax.ShapeDtypeStruct((s, D), BF16), jax.ShapeDtypeStruct((s, D), BF16),
                   jax.ShapeDtypeStruct((1, 2 * D), F32), jax.ShapeDtypeStruct((1, HV), F32)],
        compiler_params=_cp("arbitrary"),
    )(dx1b, zr, zr, zr, yp, yg, o, bgate, ghead, wpp, wgla, wout, after)


HALO = 16
CCH = D_FF // 2


def _conv_taps(u_ref, halo_ref, cs, first, ts):
    u = u_ref[:, cs].astype(F32)
    hal = halo_ref[:, cs].astype(F32)
    h1 = jnp.where(first, 0.0, _pick_row(hal, HALO - 1))
    h2 = jnp.where(first, 0.0, _pick_row(hal, HALO - 2))
    row8 = _rows((8, u.shape[1]))
    r1, r2 = pltpu.roll(u, 1, 0), pltpu.roll(u, 2, 0)
    r1 = jnp.concatenate([jnp.where(row8 == 0, h1, r1[:8]), r1[8:]], axis=0)
    r2 = jnp.concatenate([jnp.where(row8 == 0, h2, jnp.where(row8 == 1, h1, r2[:8])), r2[8:]], axis=0)
    return u, r1, r2


def _ffn_down_loss(u, x1, tgt, wconv, bconv, wdown, gfin, ts):
    s = x1.shape[0]

    def body(u_ref, halo_ref, x1_ref, t_ref, wc_ref, bc_ref, wd_ref, gf_ref, a_ref, c_ref, dx_ref, dxb_ref, ls_ref,
             dgf_ref):
        i = pl.program_id(0)

        @pl.when(i == 0)
        def _():
            ls_ref[...] = jnp.zeros_like(ls_ref)
            dgf_ref[...] = jnp.zeros_like(dgf_ref)

        first = i == 0
        acc = x1_ref[...]
        for hf in range(D_FF // CCH):
            cg = slice(hf * CCH, (hf + 1) * CCH)
            cv = slice(D_FF + hf * CCH, D_FF + (hf + 1) * CCH)
            vals = []
            for cs in (cg, cv):
                u0, u1, u2 = _conv_taps(u_ref, halo_ref, cs, first, ts)
                vals.append(bc_ref[:, cs] + wc_ref[0:1, cs] * u2 + wc_ref[1:2, cs] * u1 + wc_ref[2:3, cs] * u0)
                c_ref[:, cs] = vals[-1].astype(BF16)
            a = (vals[0] * _sigmoid(vals[0]) * vals[1]).astype(BF16)
            a_ref[:, cg] = a
            acc = acc + _dot(a, wd_ref[cg, :])
        r = lax.rsqrt(jnp.mean(acc * acc, axis=-1, keepdims=True) + EPS)
        xh = acc * r
        gf = gf_ref[...]
        err = xh * gf - t_ref[...]
        ls_ref[...] += (0.5 / D) * jnp.sum(jnp.sum(err * err, axis=-1, keepdims=True), axis=0, keepdims=True)
        dy = err * (1.0 / D)
        dgf_ref[...] += jnp.sum(dy * xh, axis=0, keepdims=True)
        dxh = dy * gf
        dx = r * (dxh - xh * jnp.mean(dxh * xh, axis=-1, keepdims=True))
        dx_ref[...] = dx
        dxb_ref[...] = dx.astype(BF16)

    row = lambda i: (i, 0)
    const2 = lambda i: (0, 0)
    return pl.pallas_call(
        body, name="ffn_down_loss", grid=(s // ts,),
        in_specs=[pl.BlockSpec((ts, N_UP), row),
                  pl.BlockSpec((HALO, N_UP), lambda i: (jnp.maximum(i * (ts // HALO) - 1, 0), 0)),
                  pl.BlockSpec((ts, D), row), pl.BlockSpec((ts, D), row), pl.BlockSpec((3, N_UP), const2),
                  pl.BlockSpec((1, N_UP), const2), pl.BlockSpec((D_FF, D), const2), pl.BlockSpec((1, D), const2)],
        out_specs=[pl.BlockSpec((ts, D_FF), row), pl.BlockSpec((ts, N_UP), row), pl.BlockSpec((ts, D), row),
                   pl.BlockSpec((ts, D), row), pl.BlockSpec((1, 128), const2), pl.BlockSpec((1, D), const2)],
        out_shape=[jax.ShapeDtypeStruct((s, D_FF), BF16), jax.ShapeDtypeStruct((s, N_UP), BF16),
                   jax.ShapeDtypeStruct((s, D), F32), jax.ShapeDtypeStruct((s, D), BF16),
                   jax.ShapeDtypeStruct((1, 128), F32), jax.ShapeDtypeStruct((1, D), F32)],
        compiler_params=_cp("arbitrary"),
    )(u, u, x1, tgt, wconv, bconv, wdown, gfin)


def _ffn_bwd(dx2b, u, c, wconv, wdown, ts):
    s = dx2b.shape[0]
    nt = s // ts

    def body(dx_ref, u_ref, c_ref, wc_ref, wd_ref, du_ref, db_ref, dw_ref, nxt_ref):
        @pl.when(pl.program_id(0) == 0)
        def _():
            db_ref[...] = jnp.zeros_like(db_ref)
            dw_ref[...] = jnp.zeros_like(dw_ref)
            nxt_ref[...] = jnp.zeros_like(nxt_ref)

        dxv = dx_ref[...]
        row8 = _rows((8, CCH))
        for hf in range(D_FF // CCH):
            cg = slice(hf * CCH, (hf + 1) * CCH)
            cv = slice(D_FF + hf * CCH, D_FF + (hf + 1) * CCH)
            da = _dot_nt(dxv, wd_ref[cg, :])
            gate = c_ref[:, cg].astype(F32)
            val = c_ref[:, cv].astype(F32)
            sg = _sigmoid(gate)
            dcs = (da * val * sg * (1.0 + gate * (1.0 - sg)), da * gate * sg)
            for cs, dc in zip((cg, cv), dcs):
                n1 = nxt_ref[0:1, cs]
                n2 = nxt_ref[1:2, cs]
                r1, r2 = pltpu.roll(dc, ts - 1, 0), pltpu.roll(dc, ts - 2, 0)
                f1 = jnp.concatenate([r1[:ts - 8], jnp.where(row8 == 7, n1, r1[ts - 8:])], axis=0)
                f2 = jnp.concatenate([r2[:ts - 8], jnp.where(row8 == 7, n2, jnp.where(row8 == 6, n1, r2[ts - 8:]))], axis=0)
                uv = u_ref[:, cs].astype(F32)
                db_ref[:, cs] += jnp.sum(dc, axis=0, keepdims=True)
                dw_ref[0:1, cs] += jnp.sum(f2 * uv, axis=0, keepdims=True)
                dw_ref[1:2, cs] += jnp.sum(f1 * uv, axis=0, keepdims=True)
                dw_ref[2:3, cs] += jnp.sum(dc * uv, axis=0, keepdims=True)
                du_ref[:, cs] = (wc_ref[2:3, cs] * dc + wc_ref[1:2, cs] * f1 + wc_ref[0:1, cs] * f2).astype(BF16)
                nxt_ref[:, cs] = dc[0:8, :]

    rev = lambda i: (nt - 1 - i, 0)
    const2 = lambda i: (0, 0)
    return pl.pallas_call(
        body, name="ffn_bwd", grid=(nt,),
        in_specs=[pl.BlockSpec((ts, D), rev), pl.BlockSpec((ts, N_UP), rev), pl.BlockSpec((ts, N_UP), rev),
                  pl.BlockSpec((3, N_UP), const2), pl.BlockSpec((D_FF, D), const2)],
        out_specs=[pl.BlockSpec((ts, N_UP), rev), pl.BlockSpec((1, N_UP), const2), pl.BlockSpec((3, N_UP), const2)],
        out_shape=[jax.ShapeDtypeStruct((s, N_UP), BF16), jax.ShapeDtypeStruct((1, N_UP), F32),
                   jax.ShapeDtypeStruct((3, N_UP), F32)],
        scratch_shapes=[pltpu.VMEM((8, N_UP), F32)],
        compiler_params=_cp("arbitrary"),
    )(dx2b, u, c, wconv, wdown)


ANY = pl.BlockSpec(memory_space=pl.ANY)


def _place():
    x, y, c = lax.axis_index("x"), lax.axis_index("y"), lax.axis_index("c")
    chips = [(1 - x, y), (x, 1 - y), (1 - x, 1 - y)]
    return x, y, c, chips


def _half(shape, c, axis):
    size = shape[axis] // 2
    cut = pl.ds(pl.multiple_of(c * size, 8 if axis == 0 else 128), size)
    return (cut, slice(None)) if axis == 0 else (slice(None), cut)


def _half_shape(shape, axis):
    return (shape[0] // 2, shape[1]) if axis == 0 else (shape[0], shape[1] // 2)


def _remote(src, dst, send_sems, recv_sems, k, to):
    return pltpu.make_async_remote_copy(src_ref=src, dst_ref=dst, send_sem=send_sems.at[k], recv_sem=recv_sems.at[k],
                                        device_id=to, device_id_type=MESH)


def _sibling_exchange(grads, axes, smalls, name):
    nb = len(grads)
    n = nb + len(smalls)

    def body(*refs):
        ins, outs = refs[:n], refs[n:2 * n]
        send_sems, recv_sems = refs[2 * n:]
        x, y, c, _ = _place()
        sib = (x, y, 1 - c)
        cps = []
        for a in range(nb):
            theirs = _half(grads[a].shape[1:], 1 - c, axes[a])
            cps.append(_remote(ins[a].at[(slice(None),) + theirs], outs[a], send_sems, recv_sems, a, sib))
        for a in range(nb, n):
            cps.append(_remote(ins[a], outs[a], send_sems, recv_sems, a, sib))
        for cp in cps:
            cp.start()
        for cp in cps:
            cp.wait()

    out_shape = [jax.ShapeDtypeStruct((4,) + _half_shape(g.shape[1:], ax), g.dtype) for g, ax in zip(grads, axes)]
    out_shape += [jax.ShapeDtypeStruct(a.shape, F32) for a in smalls]
    return pl.pallas_call(
        body, name=name, in_specs=[ANY] * n, out_specs=[ANY] * n, out_shape=out_shape,
        scratch_shapes=[pltpu.SemaphoreType.DMA((n,)), pltpu.SemaphoreType.DMA((n,))],
        compiler_params=pltpu.CompilerParams(has_side_effects=True),
    )(*grads, *smalls)


def _gather_share(lands, axes, name):
    n = len(lands)

    def body(*refs):
        outs = refs[n:2 * n]
        send_sems, recv_sems = refs[2 * n:]
        x, y, c, chips = _place()
        sib = (x, y, 1 - c)
        cps = []
        for a in range(n):
            mine = _half(lands[a].shape[1:], c, axes[a])
            for k, ch in enumerate(chips):
                landed = outs[a].at[(2 * ch[0] + ch[1],) + mine]
                cps.append(_remote(landed, landed, send_sems, recv_sems, 3 * a + k, sib))
        for cp in cps:
            cp.start()
        for a in range(n):
            other = _half(lands[a].shape[1:], 1 - c, axes[a])
            for k, ch in enumerate(chips):
                landed = outs[a].at[(2 * ch[0] + ch[1],) + other]
                _remote(landed, landed, send_sems, recv_sems, 3 * a + k, sib).wait_recv()
        for cp in cps:
            cp.wait_send()

    return pl.pallas_call(
        body, name=name, in_specs=[ANY] * n, out_specs=[ANY] * n,
        out_shape=[jax.ShapeDtypeStruct(a.shape, a.dtype) for a in lands],
        input_output_aliases={a: a for a in range(n)},
        scratch_shapes=[pltpu.SemaphoreType.DMA((3 * n,)), pltpu.SemaphoreType.DMA((3 * n,))],
        compiler_params=pltpu.CompilerParams(has_side_effects=True),
    )(*lands)


def _sibling_share(halves, name):
    n = len(halves)

    def body(*refs):
        ins, outs = refs[:n], refs[n:2 * n]
        send_sems, recv_sems = refs[2 * n:]
        x, y, c, _ = _place()
        cps = [_remote(ins[a], outs[a], send_sems, recv_sems, a, (x, y, 1 - c)) for a in range(n)]
        for cp in cps:
            cp.start()
        for cp in cps:
            cp.wait()

    return pl.pallas_call(
        body, name=name, in_specs=[ANY] * n, out_specs=[ANY] * n,
        out_shape=[jax.ShapeDtypeStruct(h.shape, F32) for h in halves],
        scratch_shapes=[pltpu.SemaphoreType.DMA((n,)), pltpu.SemaphoreType.DMA((n,))],
        compiler_params=pltpu.CompilerParams(has_side_effects=True),
    )(*halves)


HBM = pl.BlockSpec(memory_space=pltpu.HBM)
SEM = pl.BlockSpec(memory_space=pltpu.SEMAPHORE)
DATAFLOW = pltpu.SideEffectType.DATAFLOW_SIDE_EFFECTING


def _split_start(name, srcs, land_shapes, plan, n_copies, after):
    lands = [lax.empty(*ls) if isinstance(ls, tuple) else ls for ls in land_shapes]
    bufs = list(srcs) + lands
    nb, ns = len(bufs), len(srcs)

    def body(*refs):
        send_sems, recv_sems, token = refs[nb + 1], refs[nb + 2], refs[-1]
        for k, (src, dst, to) in enumerate(plan(refs[:ns], refs[ns:nb])):
            _remote(src, dst, send_sems, recv_sems, k, to).start()
        token[...] = jnp.zeros_like(token)

    res = pl.pallas_call(
        body, name=name,
        out_shape=(pltpu.SemaphoreType.DMA((n_copies,)), pltpu.SemaphoreType.DMA((n_copies,)),
                   *[pltpu.HBM(b.shape, b.dtype) for b in bufs], jax.ShapeDtypeStruct((8, 128), F32)),
        in_specs=[HBM] * nb + [ANY],
        out_specs=(SEM, SEM, *[HBM] * nb, pl.BlockSpec(memory_space=pltpu.VMEM)),
        input_output_aliases={i: 2 + i for i in range(nb)},
        compiler_params=pltpu.CompilerParams(has_side_effects=DATAFLOW),
    )(*[pltpu.with_memory_space_constraint(b, pltpu.HBM) for b in bufs], after)
    return (res[0], res[1], list(res[2:2 + nb])), res[-1]


def _split_relay(name, handle, plan, relay_plan, n_relay, after):
    send_sems, recv_sems, bufs = handle
    nb = len(bufs)

    def body(*refs):
        sends, recvs = refs[nb], refs[nb + 1]
        for k, (src, dst, to) in enumerate(plan((), refs[:nb])):
            _remote(src, dst, sends, recvs, k, to).wait_recv()
        relay_sends, relay_recvs, token = refs[nb + 3], refs[nb + 4], refs[-1]
        for k, (src, dst, to) in enumerate(relay_plan((), refs[:nb])):
            _remote(src, dst, relay_sends, relay_recvs, k, to).start()
        token[...] = jnp.zeros_like(token)

    res = pl.pallas_call(
        body, name=name,
        out_shape=(pltpu.SemaphoreType.DMA((n_relay,)), pltpu.SemaphoreType.DMA((n_relay,)),
                   *[pltpu.HBM(b.shape, b.dtype) for b in bufs], jax.ShapeDtypeStruct((8, 128), F32)),
        in_specs=[HBM] * nb + [SEM, SEM, ANY],
        out_specs=(SEM, SEM, *[HBM] * nb, pl.BlockSpec(memory_space=pltpu.VMEM)),
        input_output_aliases={i: 2 + i for i in range(nb)},
        compiler_params=pltpu.CompilerParams(has_side_effects=DATAFLOW),
    )(*bufs, send_sems, recv_sems, after)
    passed = list(res[2:2 + nb])
    return (send_sems, recv_sems, passed), (res[0], res[1], passed), res[-1]


def _split_wait(name, handle, n_srcs, plan, after, arrived=False, first=0):
    send_sems, recv_sems, bufs = handle
    nb = len(bufs)

    def body(*refs):
        sends, recvs = refs[nb], refs[nb + 1]
        for k, (src, dst, to) in enumerate(plan(refs[:n_srcs], refs[n_srcs:nb])):
            cp = _remote(src, dst, sends, recvs, first + k, to)
            cp.wait_send()
            if arrived:
                continue
            cp.wait_recv()

    res = pl.pallas_call(
        body, name=name, out_shape=[pltpu.HBM(b.shape, b.dtype) for b in bufs],
        in_specs=[HBM] * nb + [SEM, SEM, ANY], out_specs=[HBM] * nb,
        input_output_aliases={i: i for i in range(nb)},
        compiler_params=pltpu.CompilerParams(has_side_effects=DATAFLOW),
    )(*bufs, send_sems, recv_sems, after)
    return list(res[:n_srcs]), list(res[n_srcs:])


def _relay_plan(shapes, axes):
    def plan(srcs, lands):
        x, y, c, _ = _place()
        first = c == 0
        from_x, from_y = jnp.where(first, 1 - x, x), jnp.where(first, y, 1 - y)
        to = (jnp.where(first, x, 1 - x), jnp.where(first, 1 - y, y), c)
        out = []
        for a, (shape, axis) in enumerate(zip(shapes, axes)):
            got = lands[a].at[(2 * from_x + from_y,) + _half(shape, c, axis)]
            out.append((got, got, to))
        return out
    return plan


def _gather_plan(shapes, axes, n_whole=0, relayed=False):
    def plan(srcs, lands):
        x, y, c, chips = _place()
        me = 2 * x + y
        out = []
        for a, (shape, axis) in enumerate(zip(shapes, axes)):
            own = lands[a].at[(me,) + _half(shape, c, axis)]
            for ch in chips[:2] if relayed else chips:
                out.append((own, own, (ch[0], ch[1], c)))
        for a in range(len(shapes), len(shapes) + n_whole):
            for ch in chips:
                out.append((lands[a].at[me], lands[a].at[me], (ch[0], ch[1], c)))
        return out
    return plan


def _share_plan(shapes, axes):
    def plan(srcs, lands):
        x, y, c, chips = _place()
        out = []
        for a, (shape, axis) in enumerate(zip(shapes, axes)):
            mine = _half(shape, c, axis)
            for ch in chips:
                landed = lands[a].at[(2 * ch[0] + ch[1],) + mine]
                out.append((landed, landed, (x, y, 1 - c)))
        return out
    return plan


def _sibling_plan(shapes, axes):
    def plan(srcs, lands):
        x, y, c, _ = _place()
        return [(srcs[a].at[(slice(None),) + _half(shape, 1 - c, axis)], lands[a], (x, y, 1 - c))
                for a, (shape, axis) in enumerate(zip(shapes, axes))]
    return plan


def _whole_to_sibling_plan(n):
    def plan(srcs, lands):
        x, y, c, _ = _place()
        return [(srcs[a], lands[a], (x, y, 1 - c)) for a in range(n)]
    return plan


def _reduce_plan(n_big, n_small):
    def plan(srcs, lands):
        x, y, c, chips = _place()
        out = []
        for a in range(n_big):
            for k, ch in enumerate(chips):
                out.append((srcs[a].at[2 * ch[0] + ch[1]], lands[a].at[k], (ch[0], ch[1], c)))
        for a in range(n_big, n_big + n_small):
            for ch in chips:
                out.append((srcs[a], lands[a].at[2 * x + y], (ch[0], ch[1], c)))
        return out
    return plan


def _row_tile(rows, cols, mult):
    best = mult
    for t in range(mult, rows + 1, mult):
        if rows % t == 0 and t * cols * 4 <= (2 << 20):
            best = t
    return best if rows % best == 0 else rows


COL_TILE = 256


def _half_tiling(hshape, axis, mult):
    hr, hc = hshape
    if axis == 0:
        tr = _row_tile(hr, hc, mult)
        return tr, hc, hr // tr
    return hr, COL_TILE, hc // COL_TILE


def _tile_idx(axis, t):
    return (t, 0) if axis == 0 else (0, t)


def _chip_partial(place, g, t, axis, name):
    hshape = t.shape[1:]
    br, bc, nt = _half_tiling(hshape, axis, 16)

    def body(pl_ref, *hbm):
        chip, core = pl_ref[0], pl_ref[1]

        def step(idx, g_ref, t_ref, pf_ref, pb_ref):
            v = g_ref[...].astype(F32) + t_ref[...].astype(F32)
            pb_ref[...] = v.astype(BF16)

            @pl.when(idx.index[1] == chip)
            def _():
                pf_ref[...] = v

        blk = (None, br, bc)
        deep = dict(pipeline_mode=pl.Buffered(3))
        pltpu.emit_pipeline(
            step, grid=(nt, 4),
            in_specs=[pl.BlockSpec(blk, lambda i, j: (j,) + _tile_idx(axis, core * nt + i), **deep),
                      pl.BlockSpec(blk, lambda i, j: (j,) + _tile_idx(axis, i), **deep)],
            out_specs=[pl.BlockSpec((br, bc), lambda i, j: _tile_idx(axis, i)),
                       pl.BlockSpec(blk, lambda i, j: (j,) + _tile_idx(axis, i))],
            _explicit_indices=True)(*hbm)

    return pl.pallas_call(
        body, name=name, in_specs=[pl.BlockSpec(memory_space=pltpu.SMEM), ANY, ANY], out_specs=[ANY, ANY],
        out_shape=[jax.ShapeDtypeStruct(hshape, F32), jax.ShapeDtypeStruct((4,) + hshape, BF16)],
        compiler_params=_cp(),
    )(place, g, t)


def _finish_half(pf, rb, axis, name):
    hshape = pf.shape
    br, bc, nt = _half_tiling(hshape, axis, 16)

    def body(pf_ref, rb_ref, o_ref):
        o_ref[...] = ((pf_ref[...] + rb_ref[0].astype(F32)) + rb_ref[1].astype(F32)) + rb_ref[2].astype(F32)

    return pl.pallas_call(
        body, name=name, grid=(nt,),
        in_specs=[pl.BlockSpec((br, bc), lambda i: _tile_idx(axis, i)),
                  pl.BlockSpec((3, br, bc), lambda i: (0,) + _tile_idx(axis, i))],
        out_specs=pl.BlockSpec((br, bc), lambda i: _tile_idx(axis, i)),
        out_shape=jax.ShapeDtypeStruct(hshape, F32),
        compiler_params=_cp("arbitrary"),
    )(pf, rb)


def _adam_math(w, g, m, v):
    m = ADAM_B1 * m + (1.0 - ADAM_B1) * g
    v = ADAM_B2 * v + (1.0 - ADAM_B2) * (g * g)
    m_hat = m / (1.0 - ADAM_B1 ** ADAM_STEP)
    v_hat = v / (1.0 - ADAM_B2 ** ADAM_STEP)
    return -ADAM_LR * (m_hat / (jnp.sqrt(v_hat) + ADAM_EPS) + ADAM_WD * w), m, v


def _adam_halves(place, w, mine, theirs, m, v, axis, name):
    br, bc, nt = _half_tiling(mine.shape, axis, 8)

    def body(pl_ref, *hbm):
        core = pl_ref[1]

        def step(idx, w_ref, a_ref, b_ref, m_ref, v_ref, g_ref, d_ref, mo_ref, vo_ref):
            g = jnp.where(idx.index[0] // nt == core, a_ref[...], b_ref[...])
            d, mn, vn = _adam_math(w_ref[...], g, m_ref[...], v_ref[...])
            g_ref[...] = g
            d_ref[...] = d
            mo_ref[...] = mn
            vo_ref[...] = vn

        deep = dict(pipeline_mode=pl.Buffered(3))
        full = pl.BlockSpec((br, bc), lambda i: _tile_idx(axis, i))
        full_in = pl.BlockSpec((br, bc), lambda i: _tile_idx(axis, i), **deep)
        mine_spec = pl.BlockSpec((br, bc), lambda i: _tile_idx(axis, jnp.where(i // nt == core, i % nt, nt - 1)), **deep)
        theirs_spec = pl.BlockSpec((br, bc), lambda i: _tile_idx(axis, jnp.where(i // nt == core, 0, i % nt)), **deep)
        pltpu.emit_pipeline(step, grid=(2 * nt,), in_specs=[full_in, mine_spec, theirs_spec, full_in, full_in],
                            out_specs=[full] * 4, _explicit_indices=True)(*hbm)

    return pl.pallas_call(
        body, name=name, in_specs=[pl.BlockSpec(memory_space=pltpu.SMEM)] + [ANY] * 5, out_specs=[ANY] * 4,
        out_shape=[jax.ShapeDtypeStruct(w.shape, F32)] * 4, compiler_params=_cp(),
    )(place, w, mine, theirs, m, v)


def _add_many(xs, ys, name):
    n = len(xs)

    def body(*refs):
        for i in range(n):
            refs[2 * n + i][...] = refs[i][...] + refs[n + i][...]

    return pl.pallas_call(body, name=name, out_shape=[jax.ShapeDtypeStruct(a.shape, F32) for a in xs])(*xs, *ys)


def _adam_small(place, owns, landed, ws, ms, vs, widths):
    n, nw = len(owns), len(ws)

    def body(pl_ref, *refs):
        own_r, land_r = refs[:n], refs[n:2 * n]
        w_r, m_r, v_r = (refs[2 * n + k * nw:2 * n + (k + 1) * nw] for k in range(3))
        outs = refs[2 * n + 3 * nw:]
        g_o, d_o, m_o, v_o = outs[:n], outs[n:n + nw], outs[n + nw:n + 2 * nw], outs[n + 2 * nw:]
        for me in range(4):
            @pl.when(pl_ref[0] == me)
            def _(me=me):
                for i in range(n):
                    p = [own_r[i][...] if k == me else land_r[i][k] for k in range(4)]
                    g = ((p[0] + p[1]) + p[2]) + p[3]
                    if i < nw and widths[i]:
                        g = g[:, me * widths[i]:(me + 1) * widths[i]]
                    g_o[i][...] = g
                    if i < nw:
                        d, mn, vn = _adam_math(w_r[i][...], g, m_r[i][...], v_r[i][...])
                        d_o[i][...] = d
                        m_o[i][...] = mn
                        v_o[i][...] = vn

    g_shapes = [jax.ShapeDtypeStruct(ws[i].shape if i < nw else owns[i].shape, F32) for i in range(n)]
    w_shapes = [jax.ShapeDtypeStruct(w.shape, F32) for w in ws]
    whole = lambda a: pl.BlockSpec(a.shape, lambda i, p, nd=len(a.shape): (0,) * nd)
    ins = list(owns) + list(landed) + list(ws) + list(ms) + list(vs)
    out_shape = g_shapes + w_shapes * 3
    out = pl.pallas_call(
        body, name="adam_small",
        grid_spec=pltpu.PrefetchScalarGridSpec(num_scalar_prefetch=1, grid=(1,), in_specs=[whole(a) for a in ins],
                                               out_specs=[whole(a) for a in out_shape]),
        out_shape=out_shape, compiler_params=_cp("arbitrary"),
    )(place, *ins)
    return out[:n], out[n:n + nw], out[n + nw:n + 2 * nw], out[n + 2 * nw:]


def kernel(x, g_mix, w_in, b_gate, w_gk_up, b_gk, w_pool_grp, pool_scale, g_gla_head, w_pool_proj, w_gla_proj, w_out, g_ffn, w_up, w_conv, b_conv, w_down, g_final, loss_target, m_g_mix, m_w_in, m_b_gate, m_w_gk_up, m_b_gk, m_w_pool_grp, m_pool_scale, m_g_gla_head, m_w_pool_proj, m_w_gla_proj, m_w_out, m_g_ffn, m_w_up, m_w_conv, m_b_conv, m_w_down, m_g_final, v_g_mix, v_w_in, v_b_gate, v_w_gk_up, v_b_gk, v_w_pool_grp, v_pool_scale, v_g_gla_head, v_w_pool_proj, v_w_gla_proj, v_w_out, v_g_ffn, v_w_up, v_w_conv, v_b_conv, v_w_down, v_g_final):
    s = x.shape[1]
    ts = min(s, 512)
    tm = min(s, 256)
    cx, cy, cc = lax.axis_index("x"), lax.axis_index("y"), lax.axis_index("c")
    chip = 2 * cx + cy
    place = jnp.stack([chip, cc]).astype(jnp.int32)

    big_names = ("w_in", "w_pool_proj", "w_gla_proj", "w_out", "w_up", "w_down")
    axes = (1, 0, 0, 0, 0, 0)
    shards = dict(w_in=jnp.transpose(w_in[0]), w_pool_proj=w_pool_proj[0], w_gla_proj=w_gla_proj[0], w_out=w_out[0],
                  w_up=w_up[0], w_down=w_down[0])
    def landing(own_shards):
        return [lax.dynamic_update_slice(lax.empty((4,) + o_.shape, o_.dtype), o_[None], (chip, 0, 0))
                for o_ in own_shards]

    def gather_start(tag, lands, n_halves, group_axes, after, relayed=False):
        n_whole = len(lands) - n_halves
        plan = _gather_plan([l_.shape[1:] for l_ in lands[:n_halves]], group_axes, n_whole, relayed)
        n_copies = (2 if relayed else 3) * n_halves + 3 * n_whole
        handle, token = _split_start("gather_" + tag + "_start", [], lands, plan, n_copies, after)
        return (handle, plan, n_halves, group_axes), token

    def gather_relay(tag, started, after):
        handle, plan, n_halves, group_axes = started
        relay_plan = _relay_plan([b_.shape[1:] for b_ in handle[2]], group_axes)
        first, relay, token = _split_relay("gather_" + tag + "_relay", handle, plan, relay_plan, n_halves, after)
        return (first, plan, relay, relay_plan, group_axes), token

    def gather_finish_relayed(tag, relayed, after):
        first, plan, relay, relay_plan, group_axes = relayed
        lands = _split_wait("gather_" + tag + "_sent", first, 0, plan, after, arrived=True)[1]
        lands = _split_wait("gather_" + tag + "_wait", (relay[0], relay[1], lands), 0, relay_plan, after)[1]
        return _gather_share(lands, group_axes, "gather_" + tag + "_share")

    in_w, tok = gather_start("in", landing([jnp.transpose(w_in[0].astype(BF16))]), 1, axes[:1], g_mix, relayed=True)
    zero = tok[0, 0]
    own = landing([(shards[n] + zero).astype(BF16) for n in big_names[1:]] + [w_gk_up[0] + zero, w_conv[0] + zero])
    own = lax.optimization_barrier(own)
    in_r, tok = gather_relay("in", in_w, own[4])
    xs, tgt = x[0], loss_target[0]
    h = _rmsnorm(xs, g_mix, tok, "norm_mix", ts)
    m_in_t, v_in_t = jnp.transpose(m_w_in[0]), jnp.transpose(v_w_in[0])
    h, m_in_t, v_in_t = lax.optimization_barrier((h, m_in_t, v_in_t))
    groups = ((own[0:3] + own[5:7], 3, axes[1:4]), (own[3:4], 1, axes[4:5]), (own[4:5], 1, axes[5:6]))
    plans = [_gather_plan([l_.shape[1:] for l_ in ls[:nh]], ax, len(ls) - nh) for ls, nh, ax in groups]
    counts = [3 * len(ls) for ls, _, _ in groups]
    bounds = [sum(len(ls) for ls, _, _ in groups[:i]) for i in range(4)]

    def rest_plan(srcs, lands):
        return [cp for i, pl_ in enumerate(plans) for cp in pl_((), lands[bounds[i]:bounds[i + 1]])]

    (rest_send, rest_recv, rest_bufs), tok = _split_start("gather_rest_start", [], [l_ for ls, _, _ in groups for l_ in ls],
                                                          rest_plan, sum(counts), m_in_t)
    mix_w, up_w, down_w = [((rest_send, rest_recv, rest_bufs[bounds[i]:bounds[i + 1]]), plans[i], groups[i][1], groups[i][2],
                            sum(counts[:i])) for i in range(3)]

    def forward_start(tag, started, after):
        handle, plan, n_halves, group_axes, first = started
        lands = _split_wait("gather_" + tag + "_wait", handle, 0, plan, after, first=first)[1]
        plan = _share_plan([l_.shape[1:] for l_ in lands[:n_halves]], group_axes)
        share, token = _split_start("gather_" + tag + "_share_start", [], lands[:n_halves], plan, 3 * n_halves, after)
        return (share, plan, lands[n_halves:]), token

    def forward_done(tag, forwarded, after):
        share, plan, _ = forwarded
        return _split_wait("gather_" + tag + "_share_wait", share, 0, plan, after)[1]
    wgrp = w_pool_grp[0]
    w_in_t = gather_finish_relayed("in", in_r, tok)[0]
    nsh = N_IN // 4

    zr, w_in_rt = _in_proj(h, w_in_t, PROJ_TILE)
    p, pp = _pool_fwd(zr, wgrp, pool_scale)
    mix_f, tok = forward_start("mix", mix_w, pp)
    wgk4, wconv4 = mix_f[2]
    wgk_full = jnp.transpose(wgk4, (1, 0, 2)).reshape(GATE_RANK, 512) + tok[0, 0]
    wconv_full = jnp.transpose(wconv4, (1, 0, 2)).reshape(3, N_UP)
    wgk_pad = jnp.concatenate([wgk_full, jnp.zeros((128 - GATE_RANK, 512), F32)], axis=0)
    o, og, sp = _gla_fwd(zr, wgk_pad, b_gk, g_gla_head, ts)
    wpp, wgla, wout = forward_done("mix", mix_f, og)
    wgla, wout = wgla.reshape(D, D), wout.reshape(D, D)
    up_f, tok = forward_start("up", up_w, og)
    x1, mixed, yp, yg, h2 = _merge_fwd(xs, zr, pp, og, b_gate, wpp, wgla, wout, g_ffn, tok, ts)
    wup, = forward_done("up", up_f, x1)
    down_f, tok = forward_start("down", down_w, x1)
    u = _matmul_resident(h2, wup, tok, "ffn_up")
    wdown = forward_done("down", down_f, u)[0].reshape(D_FF, D)
    a, conv_out, dx2, dx2b, loss_part, dgfin = _ffn_down_loss(u, x1, tgt, wconv_full, b_conv, wdown,
                                                              g_final.reshape(1, D), tm)

    du, dbconv, dwconv = _ffn_bwd(dx2b, u, conv_out, wconv_full, wdown, tm)
    dw_down = _matmul_tn(a, dx2b, "dw_down", D, tm=D_FF // 2)
    dw_up = _matmul_tn(h2, du, "dw_up", UP_SHARD, shard_major=True)

    def exchange_start(tag, grads, group_axes, after):
        plan = _sibling_plan([g.shape[1:] for g in grads], group_axes)
        lands = [((4,) + _half_shape(g.shape[1:], ax), g.dtype) for g, ax in zip(grads, group_axes)]
        handle, token = _split_start("sibling_" + tag + "_start", grads, lands, plan, len(grads), after)
        return (handle, plan, len(grads)), token

    def partials(tag, names, group_axes, exchange, after):
        handle, plan, n = exchange
        mine, theirs = _split_wait("sibling_" + tag + "_wait", handle, n, plan, after)
        return zip(*[_chip_partial(place, g, t, ax, "chip_partial_" + nm)
                     for nm, ax, g, t in zip(names, group_axes, mine, theirs)])

    ffn_names, ffn_axes = ("w_up", "w_down"), (0, 0)
    ffn_x, token = exchange_start("ffn", [dw_up, dw_down.reshape(4, 704, D)], ffn_axes, du)
    dx1, dx1b, dgffn = _matmul_nt_normbwd(du, wup, x1, g_ffn, dx2, token, "ffn_up_bwd", ts)
    ffn_pf, ffn_pb = partials("ffn", ffn_names, ffn_axes, ffn_x, dx1b)
    ffn_plan = _reduce_plan(2, 0)
    ffn_handle, token = _split_start("reduce_ffn_start", ffn_pb, [((3,) + p.shape[1:], BF16) for p in ffn_pb],
                                     ffn_plan, 6, ffn_pf[0])

    dzr, dyp, dyg, dpp, do, dzog, dbgate, dghead = _merge_bwd(dx1b, zr, yp, yg, o, b_gate, g_gla_head, wpp, wgla, wout,
                                                             token, ts)
    dzr = lax.dynamic_update_slice(dzr, dzog, (0, OFF_OG))
    dw_out = _matmul_tn(mixed, dx1b, "dw_out", D, tm=512)
    dw_gla = _matmul_tn(og, dyg, "dw_gla", D, tm=512)
    dw_pp = _matmul_tn(pp, dyp, "dw_pp", 256, shard_major=True)

    out_names, out_axes = ("w_pool_proj", "w_gla_proj", "w_out"), (0, 0, 0)
    out_x, token = exchange_start("out", [dw_pp, dw_gla.reshape(4, 256, D), dw_out.reshape(4, 256, D)], out_axes, dpp)
    dzr, dwgrp, dscale = _pool_bwd(p, dpp, wgrp, pool_scale, token, dzr)
    out_pf, out_pb = partials("out", out_names, out_axes, out_x, dwgrp)
    out_plan = _reduce_plan(3, 0)
    out_handle, token = _split_start("reduce_out_start", out_pb, [((3,) + p_.shape[1:], BF16) for p_ in out_pb],
                                     out_plan, 9, out_pf[0])
    dq, dk, dzr, dgpre = _gla_bwd(zr, do, sp, wgk_pad, b_gk, token, dzr, ts)
    dzr, dwgk, dbgk = _gk_bwd(dgpre, zr, wgk_pad, dgpre, dzr, ts)
    dzr = lax.dynamic_update_slice(lax.dynamic_update_slice(dzr, dq, (0, OFF_Q)), dk, (0, OFF_K))
    dw_rt = _matmul_tn(dzr, h, "dw_in", D, tm=PROJ_TILE)

    def grad_rows(lo, hi):
        out = []
        for seg_lo, seg_hi, at in ((0, 1536, OFF_POOL), (1536, 3584, OFF_V), (3584, 3600, OFF_GK), (3600, N_IN, OFF_GATE)):
            a_, b_ = max(lo, seg_lo), min(hi, seg_hi)
            if a_ < b_:
                out.append(dw_rt[at + a_ - seg_lo:at + b_ - seg_lo])
        return jnp.concatenate(out, axis=0)

    dw_in_t = jnp.stack([grad_rows(j * nsh, (j + 1) * nsh) for j in range(4)])

    ms = dict(w_in=m_in_t, w_pool_proj=m_w_pool_proj[0], w_gla_proj=m_w_gla_proj[0], w_out=m_w_out[0],
              w_up=m_w_up[0], w_down=m_w_down[0])
    vs = dict(w_in=v_in_t, w_pool_proj=v_w_pool_proj[0], w_gla_proj=v_w_gla_proj[0], w_out=v_w_out[0],
              w_up=v_w_up[0], w_down=v_w_down[0])
    grad, delta, new_m, new_v = {}, {}, {}, {}

    def finish(names, group_axes, part_f, landed):
        return [_finish_half(pf, rb, ax, "finish_" + n) for n, ax, pf, rb in zip(names, group_axes, part_f, landed)]

    def update(names, group_axes, halves, sib_halves):
        for n, ax, mine, theirs in zip(names, group_axes, halves, sib_halves):
            res = _adam_halves(place, shards[n], mine, theirs, ms[n], vs[n], ax, "adam_" + n)
            if n == "w_in":
                res = [jnp.transpose(r_) for r_ in res]
            grad[n], delta[n], new_m[n], new_v[n] = [r_[None] for r_ in res]

    rest_names, rest_axes = ffn_names + out_names, ffn_axes + out_axes
    in_x, token = exchange_start("in", [dw_in_t], (1,), dw_rt)
    _, ffn_landed = _split_wait("reduce_ffn_wait", ffn_handle, 2, ffn_plan, token)
    _, out_landed = _split_wait("reduce_out_wait", out_handle, 3, out_plan, ffn_landed[0])
    rest_halves = lax.optimization_barrier(finish(rest_names, rest_axes, ffn_pf + out_pf, ffn_landed + out_landed))
    (in_pf,), (in_pb,) = partials("in", ("w_in",), (1,), in_x, rest_halves[-1])
    in_plan = _reduce_plan(1, 0)
    in_handle, token = _split_start("reduce_in_start", [in_pb], [((3,) + in_pb.shape[1:], BF16)], in_plan, 3, in_pf)
    rest_plan = _whole_to_sibling_plan(len(rest_halves))
    rest_share, token = _split_start("sibling_share_rest_start", rest_halves, [(h_.shape, F32) for h_ in rest_halves],
                                     rest_plan, len(rest_halves), token)
    grad_x, _, dgmix = _matmul_nt_normbwd(dzr, w_in_rt, xs, g_mix, dx1, token, "in_proj_bwd", ts, transposed=True)
    small_names = ("g_mix", "b_gate", "w_gk_up", "b_gk", "w_pool_grp", "pool_scale", "g_gla_head", "g_ffn", "w_conv",
                   "b_conv", "g_final")
    small_mine = [dgmix, dbgate, dwgk[:GATE_RANK], dbgk, dwgrp.reshape(4 * 128, 128), dscale, dghead, dgffn, dwconv, dbconv,
                  dgfin, loss_part]
    small_sib = _sibling_exchange([], (), small_mine, "sibling_exchange_small")
    small_chip = _add_many(small_mine, small_sib, "chip_partial_small")
    small_plan = _reduce_plan(0, len(small_chip))
    small_handle, token = _split_start("reduce_small_start", small_chip, [((4,) + a_.shape, F32) for a_ in small_chip],
                                       small_plan, 3 * len(small_chip), small_mine[0])

    rest_halves, rest_sib = _split_wait("sibling_share_rest_wait", rest_share, len(rest_halves), rest_plan, token)
    n_ffn = len(ffn_names)
    update(out_names, out_axes, rest_halves[n_ffn:], rest_sib[n_ffn:])
    updated = lax.optimization_barrier([delta[n] for n in out_names])
    _, in_landed = _split_wait("reduce_in_wait", in_handle, 1, in_plan, updated[0])
    in_halves = finish(("w_in",), (1,), (in_pf,), in_landed)
    update(("w_in",), (1,), in_halves, _sibling_share(in_halves, "sibling_share_in"))
    ffn_halves, _ = lax.optimization_barrier((rest_halves[:n_ffn], delta["w_in"]))
    update(ffn_names, ffn_axes, ffn_halves, rest_sib[:n_ffn])
    small_sent, small_landed = _split_wait("reduce_small_wait", small_handle, len(small_chip), small_plan, delta["w_in"])
    given = dict(g_mix=(g_mix, m_g_mix, v_g_mix), b_gate=(b_gate, m_b_gate, v_b_gate), w_gk_up=(w_gk_up, m_w_gk_up, v_w_gk_up),
                 b_gk=(b_gk, m_b_gk, v_b_gk), w_pool_grp=(w_pool_grp, m_w_pool_grp, v_w_pool_grp),
                 pool_scale=(pool_scale, m_pool_scale, v_pool_scale), g_gla_head=(g_gla_head, m_g_gla_head, v_g_gla_head),
                 g_ffn=(g_ffn, m_g_ffn, v_g_ffn), w_conv=(w_conv, m_w_conv, v_w_conv), b_conv=(b_conv, m_b_conv, v_b_conv),
                 g_final=(g_final, m_g_final, v_g_final))
    flat2 = lambda a: a.reshape(-1, a.shape[-1])
    widths = [dict(w_gk_up=HK, w_conv=UP_SHARD).get(n) for n in small_names]
    totals, ds, mo, vo = _adam_small(place, small_sent, small_landed, *[[flat2(given[n][k]) for n in small_names] for k in range(3)],
                                     widths)
    loss = totals[-1][0, 0]
    for i, n in enumerate(small_names):
        shp = given[n][0].shape
        grad[n], delta[n], new_m[n], new_v[n] = [r_.reshape(shp) for r_ in (totals[i], ds[i], mo[i], vo[i])]

    order = ("g_mix", "w_in", "b_gate", "w_gk_up", "b_gk", "w_pool_grp", "pool_scale", "g_gla_head", "w_pool_proj",
             "w_gla_proj", "w_out", "g_ffn", "w_up", "w_conv", "b_conv", "w_down", "g_final")
    return (loss, grad_x[None], *[grad[n] for n in order], *[delta[n] for n in order], *[new_m[n] for n in order],
            *[new_v[n] for n in order])
```

```python
import jax
import jax.numpy as jnp
from jax import lax
from jax.experimental import pallas as pl
from jax.experimental.pallas import tpu as pltpu

F32 = jnp.float32
BF16 = jnp.bfloat16
MESH = pl.DeviceIdType.MESH

D = 1024
EPS = 1e-6
CHUNK = 64
POOL_W = 512
POOL_WINDOWS = (2, 4, 8, 16)
HEADS = 4
HK = 128
HV = 256
GATE_RANK = 16
D_FF = 2816
N_UP = 2 * D_FF
N_IN = 5648
QSCALE = HK ** -0.5
N_INR = 5760
OFF_GATE, OFF_V, OFF_OG, OFF_POOL, OFF_Q, OFF_K, OFF_GK = 0, 2048, 3072, 4096, 4608, 5120, 5632

ADAM_LR, ADAM_B1, ADAM_B2, ADAM_EPS, ADAM_WD, ADAM_STEP = 0.001, 0.9, 0.999, 1e-08, 0.01, 10

VMEM_LIMIT = 56 * 1024 * 1024
PROJ_TILE = N_INR // 5
UP_SHARD = N_UP // 4


def _cp(*sem):
    return pltpu.CompilerParams(dimension_semantics=sem if sem else None, vmem_limit_bytes=VMEM_LIMIT)


def _dot(a, b):
    return jnp.dot(a, b, preferred_element_type=F32)


def _dot_nt(a, b):
    return lax.dot_general(a, b, (((1,), (1,)), ((), ())), preferred_element_type=F32)


def _dot_tn(a, b):
    return lax.dot_general(a, b, (((0,), (0,)), ((), ())), preferred_element_type=F32)


def _sigmoid(v):
    return 1.0 / (1.0 + jnp.exp(-v))


def _rows(shape):
    return lax.broadcasted_iota(jnp.int32, shape, 0)


def _pick_row(v, r):
    return jnp.sum(jnp.where(_rows(v.shape) == r, v, 0.0), axis=0, keepdims=True)


def _rmsnorm(x, g, after, name, ts):
    s = x.shape[0]

    def body(x_ref, g_ref, after_ref, h_ref):
        xv = x_ref[...]
        r = lax.rsqrt(jnp.mean(xv * xv, axis=-1, keepdims=True) + EPS)
        h_ref[...] = (xv * r * g_ref[...]).astype(BF16)

    return pl.pallas_call(
        body, name=name, grid=(s // ts,),
        in_specs=[pl.BlockSpec((ts, D), lambda i: (i, 0)), pl.BlockSpec((1, D), lambda i: (0, 0)), ANY],
        out_specs=pl.BlockSpec((ts, D), lambda i: (i, 0)), out_shape=jax.ShapeDtypeStruct((s, D), BF16),
        compiler_params=_cp("arbitrary"),
    )(x, g, after)


MM_ROWS = 512


def _matmul_resident(h, w, after, name):
    s = h.shape[0]
    nj, tn = w.shape[0], w.shape[2]
    rc = min(s, MM_ROWS)

    def body(h_ref, w_ref, after_ref, z_ref):
        for r0 in range(0, s, rc):
            z_ref[r0:r0 + rc, :] = _dot(h_ref[r0:r0 + rc, :], w_ref[...]).astype(BF16)

    return pl.pallas_call(
        body, name=name, grid=(nj,),
        in_specs=[pl.BlockSpec((s, D), lambda j: (0, 0)), pl.BlockSpec((None, D, tn), lambda j: (j, 0, 0)), ANY],
        out_specs=pl.BlockSpec((s, tn), lambda j: (0, j)), out_shape=jax.ShapeDtypeStruct((s, nj * tn), BF16),
        compiler_params=_cp("arbitrary"),
    )(h, w, after)


PROJ_PIECES = ((3600, 2048, OFF_GATE), (1536, 2048, OFF_V), (0, 1536, OFF_POOL), (3584, GATE_RANK, OFF_GK))


def _split_by_shard(pieces, rows_per_shard):
    out = []
    for src, n, dst in pieces:
        while n > 0:
            j, r = divmod(src, rows_per_shard)
            m = min(n, rows_per_shard - r)
            out.append((j, r, m, dst))
            src, n, dst = src + m, n - m, dst + m
    return tuple(out)


PROJ_SEGMENTS = _split_by_shard(PROJ_PIECES, N_IN // 4)


def _in_proj(h, w4, tn):
    s = h.shape[0]
    rc = min(s, MM_ROWS)
    nj = N_INR // tn
    first_use = [dst // tn for _, _, _, dst in PROJ_SEGMENTS]

    def body(h_ref, w_hbm, z_ref, wo_hbm, w_ref, stage, sems, out_sem):
        j = pl.program_id(0)
        cps = [pltpu.make_async_copy(w_hbm.at[k], stage.at[k], sems.at[k]) for k in range(4)]
        out_cp = pltpu.make_async_copy(w_ref, wo_hbm, out_sem.at[0])

        @pl.when(j == 0)
        def _():
            for cp in cps:
                cp.start()
            w_ref[OFF_GK + GATE_RANK:, :] = jnp.zeros((N_INR - OFF_GK - GATE_RANK, D), BF16)

        landed = set()
        for step in range(nj):
            due = [seg for seg, at in zip(PROJ_SEGMENTS, first_use) if at == step]
            if due:
                fresh = sorted({seg[0] for seg in due} - landed)
                landed.update(fresh)

                @pl.when(j == step)
                def _(due=due, fresh=fresh, last=step == max(first_use)):
                    for k in fresh:
                        cps[k].wait()
                    for k, r, n, dst in due:
                        w_ref[dst:dst + n, :] = stage[k, r:r + n, :]
                    if last:
                        out_cp.start()

        wt = w_ref[pl.ds(pl.multiple_of(j * tn, 128), tn), :]
        for r0 in range(0, s, rc):
            z_ref[r0:r0 + rc, :] = _dot_nt(h_ref[r0:r0 + rc, :], wt).astype(BF16)

        @pl.when(j == nj - 1)
        def _():
            out_cp.wait()

    return pl.pallas_call(
        body, name="in_proj", grid=(nj,),
        in_specs=[pl.BlockSpec((s, D), lambda j: (0, 0)), ANY],
        out_specs=[pl.BlockSpec((s, tn), lambda j: (0, j)), ANY],
        out_shape=[jax.ShapeDtypeStruct((s, N_INR), BF16), jax.ShapeDtypeStruct((N_INR, D), BF16)],
        scratch_shapes=[pltpu.VMEM((N_INR, D), BF16), pltpu.VMEM(w4.shape, BF16), pltpu.SemaphoreType.DMA((4,)),
                        pltpu.SemaphoreType.DMA((1,))],
        compiler_params=_cp("arbitrary"),
    )(h, w4)


def _matmul_nt_normbwd(dz, w, x, g, resid, after, name, ts, transposed=False):
    s = x.shape[0]
    w_vmem = w.shape if transposed else (D, w.shape[0] * w.shape[2])
    n_sems = 1 if transposed else w.shape[0]

    def body(dz_ref, w_hbm, x_ref, g_ref, r_ref, after_ref, o_ref, ob_ref, dg_ref, w_ref, sems):
        @pl.when(pl.program_id(0) == 0)
        def _():
            if transposed:
                cps = [pltpu.make_async_copy(w_hbm, w_ref, sems.at[0])]
            else:
                kc = w.shape[2]
                cps = [pltpu.make_async_copy(w_hbm.at[j], w_ref.at[:, pl.ds(j * kc, kc)], sems.at[j])
                       for j in range(w.shape[0])]
            for cp in cps:
                cp.start()
            for cp in cps:
                cp.wait()
            dg_ref[...] = jnp.zeros_like(dg_ref)

        dh = _dot(dz_ref[...], w_ref[...]) if transposed else _dot_nt(dz_ref[...], w_ref[...])
        xv = x_ref[...]
        r = lax.rsqrt(jnp.mean(xv * xv, axis=-1, keepdims=True) + EPS)
        xh = xv * r
        dg_ref[...] += jnp.sum(dh * xh, axis=0, keepdims=True)
        dxh = dh * g_ref[...]
        out = r_ref[...] + r * (dxh - xh * jnp.mean(dxh * xh, axis=-1, keepdims=True))
        o_ref[...] = out
        ob_ref[...] = out.astype(BF16)

    row = lambda i: (i, 0)
    kdim = dz.shape[1]
    return pl.pallas_call(
        body, name=name, grid=(s // ts,),
        in_specs=[pl.BlockSpec((ts, kdim), row), ANY, pl.BlockSpec((ts, D), row),
                  pl.BlockSpec((1, D), lambda i: (0, 0)), pl.BlockSpec((ts, D), row), ANY],
        out_specs=[pl.BlockSpec((ts, D), row), pl.BlockSpec((ts, D), row), pl.BlockSpec((1, D), lambda i: (0, 0))],
        out_shape=[jax.ShapeDtypeStruct((s, D), F32), jax.ShapeDtypeStruct((s, D), BF16),
                   jax.ShapeDtypeStruct((1, D), F32)],
        scratch_shapes=[pltpu.VMEM(w_vmem, BF16), pltpu.SemaphoreType.DMA((n_sems,))],
        compiler_params=_cp("arbitrary"),
    )(dz, w, x, g, resid, after)


def _matmul_tn(a, b, name, tn, shard_major=False, tm=None):
    s, m = a.shape
    n = b.shape[1]
    tm = m if tm is None else tm
    ni, nj = m // tm, n // tn

    def body(a_ref, b_ref, o_ref):
        o_ref[...] = _dot_tn(a_ref[...], b_ref[...]).astype(BF16)

    if shard_major:
        out_spec = pl.BlockSpec((None, tm, tn), lambda i, j: (j, i, 0))
        out_shape = jax.ShapeDtypeStruct((nj, m, tn), BF16)
    else:
        out_spec = pl.BlockSpec((tm, tn), lambda i, j: (i, j))
        out_shape = jax.ShapeDtypeStruct((m, n), BF16)
    return pl.pallas_call(
        body, name=name, grid=(ni, nj),
        in_specs=[pl.BlockSpec((s, tm), lambda i, j: (0, i)), pl.BlockSpec((s, tn), lambda i, j: (0, j))],
        out_specs=out_spec, out_shape=out_shape,
        compiler_params=_cp("arbitrary", "arbitrary"),
    )(a, b)


def _pool_fwd(zr, wgrp, scale):
    s = zr.shape[0]

    def body(u_ref, w_ref, sc_ref, p_ref, pp_ref):
        row = _rows((s, 128))
        for gi, win in enumerate(POOL_WINDOWS):
            cs = slice(gi * 128, (gi + 1) * 128)
            u = u_ref[:, cs].astype(F32)
            acc, k = u, 1
            while k < win:
                acc = acc + jnp.where(row >= k, pltpu.roll(acc, k, 0), 0.0)
                k *= 2
            cnt = jnp.minimum(row + 1, win).astype(F32)
            p = (acc / cnt - u).astype(BF16)
            p_ref[:, cs] = p
            pp_ref[:, cs] = (_dot(p, w_ref[gi].astype(BF16)) * sc_ref[:, cs]).astype(BF16)

    return pl.pallas_call(
        body, name="pool_fwd", grid=(1,),
        in_specs=[pl.BlockSpec((s, POOL_W), lambda i: (0, OFF_POOL // POOL_W)),
                  pl.BlockSpec((4, 128, 128), lambda i: (0, 0, 0)), pl.BlockSpec((1, POOL_W), lambda i: (0, 0))],
        out_specs=[pl.BlockSpec((s, POOL_W), lambda i: (0, 0))] * 2,
        out_shape=[jax.ShapeDtypeStruct((s, POOL_W), BF16)] * 2,
        compiler_params=_cp("arbitrary"),
    )(zr, wgrp, scale)


def _pool_bwd(p, dpp, wgrp, scale, after, dz):
    s = p.shape[0]

    def body(p_ref, dpp_ref, w_ref, sc_ref, after_ref, dz_in, dz_ref, dw_ref, dsc_ref):
        row = _rows((s, 128))
        for gi, win in enumerate(POOL_WINDOWS):
            cs = slice(gi * 128, (gi + 1) * 128)
            pv = p_ref[:, cs]
            wb = w_ref[gi].astype(BF16)
            dpp_v = dpp_ref[:, cs].astype(F32)
            dsc_ref[:, cs] = jnp.sum(dpp_v * _dot(pv, wb), axis=0, keepdims=True)
            dpm = (dpp_v * sc_ref[:, cs]).astype(BF16)
            dw_ref[gi] = _dot_tn(pv, dpm)
            dp = _dot_nt(dpm, wb)
            cnt = jnp.minimum(row + 1, win).astype(F32)
            acc, k = dp / cnt, 1
            while k < win:
                acc = acc + jnp.where(row < s - k, pltpu.roll(acc, s - k, 0), 0.0)
                k *= 2
            dz_ref[:, cs] = (acc - dp).astype(BF16)

    full = lambda i: (0, 0)
    return pl.pallas_call(
        body, name="pool_bwd", grid=(1,),
        in_specs=[pl.BlockSpec((s, POOL_W), full), pl.BlockSpec((s, POOL_W), full),
                  pl.BlockSpec((4, 128, 128), lambda i: (0, 0, 0)), pl.BlockSpec((1, POOL_W), full), ANY, ANY],
        out_specs=[pl.BlockSpec((s, POOL_W), lambda i: (0, OFF_POOL // POOL_W)),
                   pl.BlockSpec((4, 128, 128), lambda i: (0, 0, 0)), pl.BlockSpec((1, POOL_W), full)],
        out_shape=[jax.ShapeDtypeStruct(dz.shape, BF16), jax.ShapeDtypeStruct((4, 128, 128), F32),
                   jax.ShapeDtypeStruct((1, POOL_W), F32)],
        input_output_aliases={5: 0},
        compiler_params=_cp("arbitrary"),
    )(p, dpp, wgrp, scale, after, dz)


def _gla_decay(zgk_ref, wgk_ref, bgk_ref, rb):
    g = _dot(zgk_ref[...], wgk_ref[...].astype(BF16)) + bgk_ref[...]
    la = (jnp.minimum(g, 0.0) - jnp.log(1.0 + jnp.exp(-jnp.abs(g)))) * (1.0 / 16.0)
    rowm = _rows(la.shape) & (CHUNK - 1)
    bc, k = la, 1
    while k < CHUNK:
        bc = bc + jnp.where(rowm >= k, pltpu.roll(bc, k, 0), 0.0)
        k *= 2
    return g, jnp.exp(bc), jnp.exp(-bc)


GLA_HB = 4


def _gla_specs(rb, rmap):
    wk, wv = GLA_HB * HK, GLA_HB * HV
    return [pl.BlockSpec((rb, wk), lambda h, r: (rmap(h, r), OFF_Q // wk + h)),
            pl.BlockSpec((rb, wk), lambda h, r: (rmap(h, r), OFF_K // wk + h)),
            pl.BlockSpec((rb, wv), lambda h, r: (rmap(h, r), OFF_V // wv + h)),
            pl.BlockSpec((rb, 128), lambda h, r: (rmap(h, r), OFF_GK // 128))]


def _gla_fwd(zr, wgk, bgk, ghead, rb):
    s = zr.shape[0]
    nc = rb // CHUNK
    wk, wv = GLA_HB * HK, GLA_HB * HV

    def body(q_ref, k_ref, v_ref, zgk_ref, zog_ref, wgk_ref, bgk_ref, gh_ref, o_ref, og_ref, sp_ref, st_ref, kv_ref):
        @pl.when(pl.program_id(1) == 0)
        def _():
            st_ref[...] = jnp.zeros_like(st_ref)

        _, e_pos, e_neg = _gla_decay(zgk_ref, wgk_ref, bgk_ref, rb)
        lower = _rows((CHUNK, CHUNK)) >= lax.broadcasted_iota(jnp.int32, (CHUNK, CHUNK), 1)
        pairs = [(c, hh) for c in range(nc) for hh in range(GLA_HB)]
        rows = lambda c: slice(c * CHUNK, (c + 1) * CHUNK)
        cols_k = lambda hh: slice(hh * HK, (hh + 1) * HK)
        cols_v = lambda hh: slice(hh * HV, (hh + 1) * HV)
        qfws, pms, e_lasts = {}, {}, {}
        for c, hh in pairs:
            q = q_ref[rows(c), cols_k(hh)].astype(F32) * QSCALE
            k = k_ref[rows(c), cols_k(hh)].astype(F32)
            ec, fc = e_pos[rows(c), cols_k(hh)], e_neg[rows(c), cols_k(hh)]
            qfw = (q * ec).astype(BF16)
            kfw_f = k * fc
            s_fw = _dot_nt(qfw, kfw_f.astype(BF16))
            s_bw = _dot_nt((q * fc).astype(BF16), (k * ec).astype(BF16))
            e_last = _pick_row(ec, CHUNK - 1)
            kv_ref[c, hh] = _dot_tn(v_ref[rows(c), cols_v(hh)], (kfw_f * e_last).astype(BF16))
            qfws[c, hh], pms[c, hh], e_lasts[c, hh] = qfw, jnp.where(lower, s_fw, s_bw).astype(BF16), e_last
        for hh in range(GLA_HB):
            st = st_ref[hh]
            for c in range(nc):
                sp_ref[c, hh] = st.astype(BF16)
                st = st * e_lasts[c, hh] + kv_ref[c, hh]
            st_ref[hh] = st
        for c, hh in pairs:
            o = _dot(pms[c, hh], v_ref[rows(c), cols_v(hh)]) + _dot_nt(qfws[c, hh], sp_ref[c, hh])
            r = lax.rsqrt(jnp.mean(o * o, axis=-1, keepdims=True) + EPS)
            zo = zog_ref[rows(c), cols_v(hh)].astype(F32)
            o_ref[rows(c), cols_v(hh)] = o.astype(BF16)
            og_ref[rows(c), cols_v(hh)] = (o * r * gh_ref[...] * zo * _sigmoid(zo)).astype(BF16)

    rmap = lambda h, r: r
    return pl.pallas_call(
        body, name="gla_fwd", grid=(HEADS // GLA_HB, s // rb),
        in_specs=_gla_specs(rb, rmap) + [
            pl.BlockSpec((rb, wv), lambda h, r: (r, OFF_OG // wv + h)),
            pl.BlockSpec((128, wk), lambda h, r: (0, h)), pl.BlockSpec((1, wk), lambda h, r: (0, h)),
            pl.BlockSpec((1, HV), lambda h, r: (0, 0))],
        out_specs=[pl.BlockSpec((rb, wv), lambda h, r: (r, h)), pl.BlockSpec((rb, wv), lambda h, r: (r, h)),
                   pl.BlockSpec((nc, GLA_HB, HV, HK), lambda h, r: (r, h, 0, 0))],
        out_shape=[jax.ShapeDtypeStruct((s, D), BF16), jax.ShapeDtypeStruct((s, D), BF16),
                   jax.ShapeDtypeStruct((s // CHUNK, HEADS, HV, HK), BF16)],
        scratch_shapes=[pltpu.VMEM((GLA_HB, HV, HK), F32), pltpu.VMEM((nc, GLA_HB, HV, HK), F32)],
        compiler_params=_cp("arbitrary", "arbitrary"),
    )(zr, zr, zr, zr, zr, wgk, bgk, ghead)


def _gla_bwd(zr, do, sp, wgk, bgk, after, dz, rb):
    s = zr.shape[0]
    nc = rb // CHUNK
    nr = s // rb
    wk, wv = GLA_HB * HK, GLA_HB * HV

    def body(q_ref, k_ref, v_ref, zgk_ref, do_ref, sp_ref, wgk_ref, bgk_ref, after_ref, dz_in, dq_ref, dk_ref, dv_ref,
             dg_ref, gt_ref, dbc_ref, gs_ref):
        @pl.when(pl.program_id(1) == 0)
        def _():
            gt_ref[...] = jnp.zeros_like(gt_ref)

        g, e_pos, e_neg = _gla_decay(zgk_ref, wgk_ref, bgk_ref, rb)
        lower = _rows((CHUNK, CHUNK)) >= lax.broadcasted_iota(jnp.int32, (CHUNK, CHUNK), 1)
        is_last = _rows((CHUNK, HK)) == CHUNK - 1
        pairs = [(c, hh) for c in range(nc) for hh in range(GLA_HB)]
        rows = lambda c: slice(c * CHUNK, (c + 1) * CHUNK)
        cols_k = lambda hh: slice(hh * HK, (hh + 1) * HK)
        cols_v = lambda hh: slice(hh * HV, (hh + 1) * HV)
        e_lasts = {}
        for c, hh in pairs:
            ec = e_pos[rows(c), cols_k(hh)]
            qfw = (q_ref[rows(c), cols_k(hh)].astype(F32) * QSCALE * ec).astype(BF16)
            gs_ref[c, hh] = _dot_tn(do_ref[rows(c), cols_v(hh)], qfw)
            e_lasts[c, hh] = _pick_row(ec, CHUNK - 1)
        for hh in range(GLA_HB):
            gt = gt_ref[hh]
            for c in reversed(range(nc)):
                own = gs_ref[c, hh]
                gs_ref[c, hh] = gt
                gt = own + gt * e_lasts[c, hh]
            gt_ref[hh] = gt
        def decayed(c, hh):
            q = q_ref[rows(c), cols_k(hh)].astype(F32) * QSCALE
            k = k_ref[rows(c), cols_k(hh)].astype(F32)
            ec, fc = e_pos[rows(c), cols_k(hh)], e_neg[rows(c), cols_k(hh)]
            return ec, fc, q * ec, k * fc, q * fc, k * ec

        pms, dss = {}, {}
        for c, hh in pairs:
            _, _, qfw_f, kfw_f, qbw_f, kbw_f = decayed(c, hh)
            s_fw = _dot_nt(qfw_f.astype(BF16), kfw_f.astype(BF16))
            s_bw = _dot_nt(qbw_f.astype(BF16), kbw_f.astype(BF16))
            dp = _dot_nt(do_ref[rows(c), cols_v(hh)], v_ref[rows(c), cols_v(hh)])
            pms[c, hh] = jnp.where(lower, s_fw, s_bw).astype(BF16)
            dss[c, hh] = (jnp.where(lower, dp, 0.0).astype(BF16), jnp.where(lower, 0.0, dp).astype(BF16))
        for c, hh in pairs:
            sl, ck, cv = rows(c), cols_k(hh), cols_v(hh)
            v = v_ref[sl, cv]
            dov = do_ref[sl, cv]
            ec, fc, qfw_f, kfw_f, qbw_f, kbw_f = decayed(c, hh)
            qfw, kfw, qbw, kbw = qfw_f.astype(BF16), kfw_f.astype(BF16), qbw_f.astype(BF16), kbw_f.astype(BF16)
            pm = pms[c, hh]
            e_last = e_lasts[c, hh]
            kdec = (kfw_f * e_last).astype(BF16)
            gt = gs_ref[c, hh]
            gtb = gt.astype(BF16)
            spv = sp_ref[c, hh]
            dv_ref[sl, cv] = (_dot_tn(pm, dov) + _dot_nt(kdec, gtb)).astype(BF16)
            ds_fw, ds_bw = dss[c, hh]
            dqfw = _dot(ds_fw, kfw) + _dot(dov, spv)
            dkfw = _dot_tn(ds_fw, qfw)
            dqbw = _dot(ds_bw, kbw)
            dkbw = _dot_tn(ds_bw, qbw)
            dkdec = _dot(v, gtb)
            de_last = (jnp.sum(gt * spv.astype(F32), axis=0, keepdims=True)
                       + jnp.sum(dkdec * kfw_f, axis=0, keepdims=True))
            dkfw = dkfw + dkdec * e_last
            dq_ref[sl, ck] = ((dqfw * ec + dqbw * fc) * QSCALE).astype(BF16)
            dk_ref[sl, ck] = (dkfw * fc + dkbw * ec).astype(BF16)
            dbc = dqfw * qfw_f - dqbw * qbw_f + dkbw * kbw_f - dkfw * kfw_f
            dbc_ref[sl, ck] = dbc + jnp.where(is_last, de_last * e_last, 0.0)
        rowm = _rows((rb, wk)) & (CHUNK - 1)
        dla, kk = dbc_ref[...], 1
        while kk < CHUNK:
            dla = dla + jnp.where(rowm < CHUNK - kk, pltpu.roll(dla, rb - kk, 0), 0.0)
            kk *= 2
        dg_ref[...] = dla * (1.0 / 16.0) * _sigmoid(-g)

    rmap = lambda h, r: nr - 1 - r
    rev = lambda h, r: (nr - 1 - r, h)
    return pl.pallas_call(
        body, name="gla_bwd", grid=(HEADS // GLA_HB, nr),
        in_specs=_gla_specs(rb, rmap) + [
            pl.BlockSpec((rb, wv), rev),
            pl.BlockSpec((nc, GLA_HB, HV, HK), lambda h, r: (nr - 1 - r, h, 0, 0)),
            pl.BlockSpec((128, wk), lambda h, r: (0, h)), pl.BlockSpec((1, wk), lambda h, r: (0, h)), ANY, ANY],
        out_specs=[pl.BlockSpec((rb, wk), rev), pl.BlockSpec((rb, wk), rev),
                   pl.BlockSpec((rb, wv), lambda h, r: (nr - 1 - r, OFF_V // wv + h)), pl.BlockSpec((rb, wk), rev)],
        out_shape=[jax.ShapeDtypeStruct((s, HEADS * HK), BF16), jax.ShapeDtypeStruct((s, HEADS * HK), BF16),
                   jax.ShapeDtypeStruct(dz.shape, BF16), jax.ShapeDtypeStruct((s, HEADS * HK), F32)],
        scratch_shapes=[pltpu.VMEM((GLA_HB, HV, HK), F32), pltpu.VMEM((rb, wk), F32),
                        pltpu.VMEM((nc, GLA_HB, HV, HK), F32)],
        input_output_aliases={9: 2},
        compiler_params=_cp("arbitrary", "arbitrary"),
    )(zr, zr, zr, zr, do, sp, wgk, bgk, after, dz)


def _gk_bwd(dgpre, zr, wgk, after, dz, ts):
    s = zr.shape[0]

    def body(dg_ref, zgk_ref, w_ref, after_ref, dz_in, dz_ref, dw_ref, db_ref):
        @pl.when(pl.program_id(0) == 0)
        def _():
            dw_ref[...] = jnp.zeros_like(dw_ref)
            db_ref[...] = jnp.zeros_like(db_ref)

        dg = dg_ref[...]
        dgb = dg.astype(BF16)
        dz_ref[...] = _dot_nt(dgb, w_ref[...].astype(BF16)).astype(BF16)
        dw_ref[...] += _dot_tn(zgk_ref[...], dgb)
        db_ref[...] += jnp.sum(dg, axis=0, keepdims=True)

    return pl.pallas_call(
        body, name="gk_bwd", grid=(s // ts,),
        in_specs=[pl.BlockSpec((ts, 512), lambda i: (i, 0)), pl.BlockSpec((ts, 128), lambda i: (i, OFF_GK // 128)),
                  pl.BlockSpec((128, 512), lambda i: (0, 0)), ANY, ANY],
        out_specs=[pl.BlockSpec((ts, 128), lambda i: (i, OFF_GK // 128)), pl.BlockSpec((128, 512), lambda i: (0, 0)),
                   pl.BlockSpec((1, 512), lambda i: (0, 0))],
        out_shape=[jax.ShapeDtypeStruct(dz.shape, BF16), jax.ShapeDtypeStruct((128, 512), F32),
                   jax.ShapeDtypeStruct((1, 512), F32)],
        input_output_aliases={4: 0},
        compiler_params=_cp("arbitrary"),
    )(dgpre, zr, wgk, after, dz)


def _merge_fwd(x, zr, pp, og, bgate, wpp, wgla, wout, gffn, after, ts):
    s = x.shape[0]

    def body(x_ref, z0_ref, z1_ref, pp_ref, og_ref, bg_ref, wpp_ref, wgla_ref, wout_ref, gf_ref, after_ref,
             x1_ref, mix_ref, yp_ref, yg_ref, h2_ref):
        ppv = pp_ref[...]
        yp = jnp.concatenate([_dot(ppv, wpp_ref[j]) for j in range(4)], axis=1)
        yg = _dot(og_ref[...], wgla_ref[...])
        g0 = _sigmoid(z0_ref[...].astype(F32) + bg_ref[:, :D])
        g1 = _sigmoid(z1_ref[...].astype(F32) + bg_ref[:, D:])
        mixed = (g0 * yp + g1 * yg).astype(BF16)
        x1 = x_ref[...] + _dot(mixed, wout_ref[...])
        x1_ref[...] = x1
        mix_ref[...] = mixed
        yp_ref[...] = yp.astype(BF16)
        yg_ref[...] = yg.astype(BF16)
        r = lax.rsqrt(jnp.mean(x1 * x1, axis=-1, keepdims=True) + EPS)
        h2_ref[...] = (x1 * r * gf_ref[...]).astype(BF16)

    row = lambda i: (i, 0)
    const2 = lambda i: (0, 0)
    return pl.pallas_call(
        body, name="merge_fwd", grid=(s // ts,),
        in_specs=[pl.BlockSpec((ts, D), row), pl.BlockSpec((ts, D), lambda i: (i, 0)), pl.BlockSpec((ts, D), lambda i: (i, 1)),
                  pl.BlockSpec((ts, POOL_W), row), pl.BlockSpec((ts, D), row), pl.BlockSpec((1, 2 * D), const2),
                  pl.BlockSpec((4, POOL_W, 256), lambda i: (0, 0, 0)), pl.BlockSpec((D, D), const2),
                  pl.BlockSpec((D, D), const2), pl.BlockSpec((1, D), const2), ANY],
        out_specs=[pl.BlockSpec((ts, D), row)] * 5,
        out_shape=[jax.ShapeDtypeStruct((s, D), F32)] + [jax.ShapeDtypeStruct((s, D), BF16)] * 4,
        compiler_params=_cp("arbitrary"),
    )(x, zr, zr, pp, og, bgate, wpp, wgla, wout, gffn, after)


def _merge_bwd(dx1b, zr, yp, yg, o, bgate, ghead, wpp, wgla, wout, after, ts):
    s = dx1b.shape[0]

    def body(dx_ref, z0_ref, z1_ref, zog_ref, yp_ref, yg_ref, o_ref, bg_ref, gh_ref, wpp_ref, wgla_ref, wout_ref, after_ref,
             dzg_ref, dyp_ref, dyg_ref, dpp_ref, do_ref, dzog_ref, dbg_ref, dgh_ref):
        @pl.when(pl.program_id(0) == 0)
        def _():
            dbg_ref[...] = jnp.zeros_like(dbg_ref)
            dgh_ref[...] = jnp.zeros_like(dgh_ref)

        dmix = _dot_nt(dx_ref[...], wout_ref[...])
        g0 = _sigmoid(z0_ref[...].astype(F32) + bg_ref[:, :D])
        g1 = _sigmoid(z1_ref[...].astype(F32) + bg_ref[:, D:])
        dypb = (dmix * g0).astype(BF16)
        dygb = (dmix * g1).astype(BF16)
        dz0 = dmix * yp_ref[...].astype(F32) * g0 * (1.0 - g0)
        dz1 = dmix * yg_ref[...].astype(F32) * g1 * (1.0 - g1)
        dzg_ref[:, :D] = dz0.astype(BF16)
        dzg_ref[:, D:] = dz1.astype(BF16)
        dbg_ref[:, :D] += jnp.sum(dz0, axis=0, keepdims=True)
        dbg_ref[:, D:] += jnp.sum(dz1, axis=0, keepdims=True)
        dyp_ref[...] = dypb
        dyg_ref[...] = dygb
        dpp = _dot_nt(dypb[:, 0:256], wpp_ref[0])
        for j in range(1, 4):
            dpp = dpp + _dot_nt(dypb[:, j * 256:(j + 1) * 256], wpp_ref[j])
        dpp_ref[...] = dpp.astype(BF16)
        dog = _dot_nt(dygb, wgla_ref[...])
        gh = gh_ref[...]
        dgh = jnp.zeros((1, HV), F32)
        for h in range(HEADS):
            cs = slice(h * HV, (h + 1) * HV)
            ov = o_ref[:, cs].astype(F32)
            r = lax.rsqrt(jnp.mean(ov * ov, axis=-1, keepdims=True) + EPS)
            oh = ov * r
            zo = zog_ref[:, cs].astype(F32)
            sg = _sigmoid(zo)
            dog_h = dog[:, cs]
            don = dog_h * zo * sg
            dzog_ref[:, cs] = (dog_h * oh * gh * sg * (1.0 + zo * (1.0 - sg))).astype(BF16)
            dgh = dgh + jnp.sum(don * oh, axis=0, keepdims=True)
            doh = don * gh
            do_ref[:, cs] = (r * (doh - oh * jnp.mean(doh * oh, axis=-1, keepdims=True))).astype(BF16)
        dgh_ref[...] += dgh

    row = lambda i: (i, 0)
    const2 = lambda i: (0, 0)
    return pl.pallas_call(
        body, name="merge_bwd", grid=(s // ts,),
        in_specs=[pl.BlockSpec((ts, D), row), pl.BlockSpec((ts, D), lambda i: (i, 0)), pl.BlockSpec((ts, D), lambda i: (i, 1)),
                  pl.BlockSpec((ts, D), lambda i: (i, OFF_OG // D)), pl.BlockSpec((ts, D), row), pl.BlockSpec((ts, D), row),
                  pl.BlockSpec((ts, D), row), pl.BlockSpec((1, 2 * D), const2), pl.BlockSpec((1, HV), const2),
                  pl.BlockSpec((4, POOL_W, 256), lambda i: (0, 0, 0)), pl.BlockSpec((D, D), const2),
                  pl.BlockSpec((D, D), const2), ANY],
        out_specs=[pl.BlockSpec((ts, 2 * D), row), pl.BlockSpec((ts, D), row), pl.BlockSpec((ts, D), row),
                   pl.BlockSpec((ts, POOL_W), row), pl.BlockSpec((ts, D), row), pl.BlockSpec((ts, D), row),
                   pl.BlockSpec((1, 2 * D), const2), pl.BlockSpec((1, HV), const2)],
        out_shape=[jax.ShapeDtypeStruct((s, N_INR), BF16), jax.ShapeDtypeStruct((s, D), BF16),
                   jax.ShapeDtypeStruct((s, D), BF16), jax.ShapeDtypeStruct((s, POOL_W), BF16),
                   jax.ShapeDtypeStruct((s, D), BF16), jax.ShapeDtypeStruct((s, D), BF16),
                   jax.ShapeDtypeStruct((1, 2 * D), F32), jax.ShapeDtypeStruct((1, HV), F32)],
        compiler_params=_cp("arbitrary"),
    )(dx1b, zr, zr, zr, yp, yg, o, bgate, ghead, wpp, wgla, wout, after)


HALO = 16
CCH = D_FF // 2


def _conv_taps(u_ref, halo_ref, cs, first, ts):
    u = u_ref[:, cs].astype(F32)
    hal = halo_ref[:, cs].astype(F32)
    h1 = jnp.where(first, 0.0, _pick_row(hal, HALO - 1))
    h2 = jnp.where(first, 0.0, _pick_row(hal, HALO - 2))
    row8 = _rows((8, u.shape[1]))
    r1, r2 = pltpu.roll(u, 1, 0), pltpu.roll(u, 2, 0)
    r1 = jnp.concatenate([jnp.where(row8 == 0, h1, r1[:8]), r1[8:]], axis=0)
    r2 = jnp.concatenate([jnp.where(row8 == 0, h2, jnp.where(row8 == 1, h1, r2[:8])), r2[8:]], axis=0)
    return u, r1, r2


def _ffn_down_loss(u, x1, tgt, wconv, bconv, wdown, gfin, ts):
    s = x1.shape[0]

    def body(u_ref, halo_ref, x1_ref, t_ref, wc_ref, bc_ref, wd_ref, gf_ref, a_ref, c_ref, dx_ref, dxb_ref, ls_ref,
             dgf_ref):
        i = pl.program_id(0)

        @pl.when(i == 0)
        def _():
            ls_ref[...] = jnp.zeros_like(ls_ref)
            dgf_ref[...] = jnp.zeros_like(dgf_ref)

        first = i == 0
        acc = x1_ref[...]
        for hf in range(D_FF // CCH):
            cg = slice(hf * CCH, (hf + 1) * CCH)
            cv = slice(D_FF + hf * CCH, D_FF + (hf + 1) * CCH)
            vals = []
            for cs in (cg, cv):
                u0, u1, u2 = _conv_taps(u_ref, halo_ref, cs, first, ts)
                vals.append(bc_ref[:, cs] + wc_ref[0:1, cs] * u2 + wc_ref[1:2, cs] * u1 + wc_ref[2:3, cs] * u0)
                c_ref[:, cs] = vals[-1].astype(BF16)
            a = (vals[0] * _sigmoid(vals[0]) * vals[1]).astype(BF16)
            a_ref[:, cg] = a
            acc = acc + _dot(a, wd_ref[cg, :])
        r = lax.rsqrt(jnp.mean(acc * acc, axis=-1, keepdims=True) + EPS)
        xh = acc * r
        gf = gf_ref[...]
        err = xh * gf - t_ref[...]
        ls_ref[...] += (0.5 / D) * jnp.sum(jnp.sum(err * err, axis=-1, keepdims=True), axis=0, keepdims=True)
        dy = err * (1.0 / D)
        dgf_ref[...] += jnp.sum(dy * xh, axis=0, keepdims=True)
        dxh = dy * gf
        dx = r * (dxh - xh * jnp.mean(dxh * xh, axis=-1, keepdims=True))
        dx_ref[...] = dx
        dxb_ref[...] = dx.astype(BF16)

    row = lambda i: (i, 0)
    const2 = lambda i: (0, 0)
    return pl.pallas_call(
        body, name="ffn_down_loss", grid=(s // ts,),
        in_specs=[pl.BlockSpec((ts, N_UP), row),
                  pl.BlockSpec((HALO, N_UP), lambda i: (jnp.maximum(i * (ts // HALO) - 1, 0), 0)),
                  pl.BlockSpec((ts, D), row), pl.BlockSpec((ts, D), row), pl.BlockSpec((3, N_UP), const2),
                  pl.BlockSpec((1, N_UP), const2), pl.BlockSpec((D_FF, D), const2), pl.BlockSpec((1, D), const2)],
        out_specs=[pl.BlockSpec((ts, D_FF), row), pl.BlockSpec((ts, N_UP), row), pl.BlockSpec((ts, D), row),
                   pl.BlockSpec((ts, D), row), pl.BlockSpec((1, 128), const2), pl.BlockSpec((1, D), const2)],
        out_shape=[jax.ShapeDtypeStruct((s, D_FF), BF16), jax.ShapeDtypeStruct((s, N_UP), BF16),
                   jax.ShapeDtypeStruct((s, D), F32), jax.ShapeDtypeStruct((s, D), BF16),
                   jax.ShapeDtypeStruct((1, 128), F32), jax.ShapeDtypeStruct((1, D), F32)],
        compiler_params=_cp("arbitrary"),
    )(u, u, x1, tgt, wconv, bconv, wdown, gfin)


def _ffn_bwd(dx2b, u, c, wconv, wdown, ts):
    s = dx2b.shape[0]
    nt = s // ts

    def body(dx_ref, u_ref, c_ref, wc_ref, wd_ref, du_ref, db_ref, dw_ref, nxt_ref):
        @pl.when(pl.program_id(0) == 0)
        def _():
            db_ref[...] = jnp.zeros_like(db_ref)
            dw_ref[...] = jnp.zeros_like(dw_ref)
            nxt_ref[...] = jnp.zeros_like(nxt_ref)

        dxv = dx_ref[...]
        row8 = _rows((8, CCH))
        for hf in range(D_FF // CCH):
            cg = slice(hf * CCH, (hf + 1) * CCH)
            cv = slice(D_FF + hf * CCH, D_FF + (hf + 1) * CCH)
            da = _dot_nt(dxv, wd_ref[cg, :])
            gate = c_ref[:, cg].astype(F32)
            val = c_ref[:, cv].astype(F32)
            sg = _sigmoid(gate)
            dcs = (da * val * sg * (1.0 + gate * (1.0 - sg)), da * gate * sg)
            for cs, dc in zip((cg, cv), dcs):
                n1 = nxt_ref[0:1, cs]
                n2 = nxt_ref[1:2, cs]
                r1, r2 = pltpu.roll(dc, ts - 1, 0), pltpu.roll(dc, ts - 2, 0)
                f1 = jnp.concatenate([r1[:ts - 8], jnp.where(row8 == 7, n1, r1[ts - 8:])], axis=0)
                f2 = jnp.concatenate([r2[:ts - 8], jnp.where(row8 == 7, n2, jnp.where(row8 == 6, n1, r2[ts - 8:]))], axis=0)
                uv = u_ref[:, cs].astype(F32)
                db_ref[:, cs] += jnp.sum(dc, axis=0, keepdims=True)
                dw_ref[0:1, cs] += jnp.sum(f2 * uv, axis=0, keepdims=True)
                dw_ref[1:2, cs] += jnp.sum(f1 * uv, axis=0, keepdims=True)
                dw_ref[2:3, cs] += jnp.sum(dc * uv, axis=0, keepdims=True)
                du_ref[:, cs] = (wc_ref[2:3, cs] * dc + wc_ref[1:2, cs] * f1 + wc_ref[0:1, cs] * f2).astype(BF16)
                nxt_ref[:, cs] = dc[0:8, :]

    rev = lambda i: (nt - 1 - i, 0)
    const2 = lambda i: (0, 0)
    return pl.pallas_call(
        body, name="ffn_bwd", grid=(nt,),
        in_specs=[pl.BlockSpec((ts, D), rev), pl.BlockSpec((ts, N_UP), rev), pl.BlockSpec((ts, N_UP), rev),
                  pl.BlockSpec((3, N_UP), const2), pl.BlockSpec((D_FF, D), const2)],
        out_specs=[pl.BlockSpec((ts, N_UP), rev), pl.BlockSpec((1, N_UP), const2), pl.BlockSpec((3, N_UP), const2)],
        out_shape=[jax.ShapeDtypeStruct((s, N_UP), BF16), jax.ShapeDtypeStruct((1, N_UP), F32),
                   jax.ShapeDtypeStruct((3, N_UP), F32)],
        scratch_shapes=[pltpu.VMEM((8, N_UP), F32)],
        compiler_params=_cp("arbitrary"),
    )(dx2b, u, c, wconv, wdown)


ANY = pl.BlockSpec(memory_space=pl.ANY)


def _place():
    x, y, c = lax.axis_index("x"), lax.axis_index("y"), lax.axis_index("c")
    chips = [(1 - x, y), (x, 1 - y), (1 - x, 1 - y)]
    return x, y, c, chips


def _half(shape, c, axis):
    size = shape[axis] // 2
    cut = pl.ds(pl.multiple_of(c * size, 8 if axis == 0 else 128), size)
    return (cut, slice(None)) if axis == 0 else (slice(None), cut)


def _half_shape(shape, axis):
    return (shape[0] // 2, shape[1]) if axis == 0 else (shape[0], shape[1] // 2)


def _remote(src, dst, send_sems, recv_sems, k, to):
    return pltpu.make_async_remote_copy(src_ref=src, dst_ref=dst, send_sem=send_sems.at[k], recv_sem=recv_sems.at[k],
                                        device_id=to, device_id_type=MESH)


def _sibling_exchange(grads, axes, smalls, name):
    nb = len(grads)
    n = nb + len(smalls)

    def body(*refs):
        ins, outs = refs[:n], refs[n:2 * n]
        send_sems, recv_sems = refs[2 * n:]
        x, y, c, _ = _place()
        sib = (x, y, 1 - c)
        cps = []
        for a in range(nb):
            theirs = _half(grads[a].shape[1:], 1 - c, axes[a])
            cps.append(_remote(ins[a].at[(slice(None),) + theirs], outs[a], send_sems, recv_sems, a, sib))
        for a in range(nb, n):
            cps.append(_remote(ins[a], outs[a], send_sems, recv_sems, a, sib))
        for cp in cps:
            cp.start()
        for cp in cps:
            cp.wait()

    out_shape = [jax.ShapeDtypeStruct((4,) + _half_shape(g.shape[1:], ax), g.dtype) for g, ax in zip(grads, axes)]
    out_shape += [jax.ShapeDtypeStruct(a.shape, F32) for a in smalls]
    return pl.pallas_call(
        body, name=name, in_specs=[ANY] * n, out_specs=[ANY] * n, out_shape=out_shape,
        scratch_shapes=[pltpu.SemaphoreType.DMA((n,)), pltpu.SemaphoreType.DMA((n,))],
        compiler_params=pltpu.CompilerParams(has_side_effects=True),
    )(*grads, *smalls)


def _gather_share(lands, axes, name):
    n = len(lands)

    def body(*refs):
        outs = refs[n:2 * n]
        send_sems, recv_sems = refs[2 * n:]
        x, y, c, chips = _place()
        sib = (x, y, 1 - c)
        cps = []
        for a in range(n):
            mine = _half(lands[a].shape[1:], c, axes[a])
            for k, ch in enumerate(chips):
                landed = outs[a].at[(2 * ch[0] + ch[1],) + mine]
                cps.append(_remote(landed, landed, send_sems, recv_sems, 3 * a + k, sib))
        for cp in cps:
            cp.start()
        for a in range(n):
            other = _half(lands[a].shape[1:], 1 - c, axes[a])
            for k, ch in enumerate(chips):
                landed = outs[a].at[(2 * ch[0] + ch[1],) + other]
                _remote(landed, landed, send_sems, recv_sems, 3 * a + k, sib).wait_recv()
        for cp in cps:
            cp.wait_send()

    return pl.pallas_call(
        body, name=name, in_specs=[ANY] * n, out_specs=[ANY] * n,
        out_shape=[jax.ShapeDtypeStruct(a.shape, a.dtype) for a in lands],
        input_output_aliases={a: a for a in range(n)},
        scratch_shapes=[pltpu.SemaphoreType.DMA((3 * n,)), pltpu.SemaphoreType.DMA((3 * n,))],
        compiler_params=pltpu.CompilerParams(has_side_effects=True),
    )(*lands)


def _sibling_share(halves, name):
    n = len(halves)

    def body(*refs):
        ins, outs = refs[:n], refs[n:2 * n]
        send_sems, recv_sems = refs[2 * n:]
        x, y, c, _ = _place()
        cps = [_remote(ins[a], outs[a], send_sems, recv_sems, a, (x, y, 1 - c)) for a in range(n)]
        for cp in cps:
            cp.start()
        for cp in cps:
            cp.wait()

    return pl.pallas_call(
        body, name=name, in_specs=[ANY] * n, out_specs=[ANY] * n,
        out_shape=[jax.ShapeDtypeStruct(h.shape, F32) for h in halves],
        scratch_shapes=[pltpu.SemaphoreType.DMA((n,)), pltpu.SemaphoreType.DMA((n,))],
        compiler_params=pltpu.CompilerParams(has_side_effects=True),
    )(*halves)


HBM = pl.BlockSpec(memory_space=pltpu.HBM)
SEM = pl.BlockSpec(memory_space=pltpu.SEMAPHORE)
DATAFLOW = pltpu.SideEffectType.DATAFLOW_SIDE_EFFECTING


def _split_start(name, srcs, land_shapes, plan, n_copies, after):
    lands = [lax.empty(*ls) if isinstance(ls, tuple) else ls for ls in land_shapes]
    bufs = list(srcs) + lands
    nb, ns = len(bufs), len(srcs)

    def body(*refs):
        send_sems, recv_sems, token = refs[nb + 1], refs[nb + 2], refs[-1]
        for k, (src, dst, to) in enumerate(plan(refs[:ns], refs[ns:nb])):
            _remote(src, dst, send_sems, recv_sems, k, to).start()
        token[...] = jnp.zeros_like(token)

    res = pl.pallas_call(
        body, name=name,
        out_shape=(pltpu.SemaphoreType.DMA((n_copies,)), pltpu.SemaphoreType.DMA((n_copies,)),
                   *[pltpu.HBM(b.shape, b.dtype) for b in bufs], jax.ShapeDtypeStruct((8, 128), F32)),
        in_specs=[HBM] * nb + [ANY],
        out_specs=(SEM, SEM, *[HBM] * nb, pl.BlockSpec(memory_space=pltpu.VMEM)),
        input_output_aliases={i: 2 + i for i in range(nb)},
        compiler_params=pltpu.CompilerParams(has_side_effects=DATAFLOW),
    )(*[pltpu.with_memory_space_constraint(b, pltpu.HBM) for b in bufs], after)
    return (res[0], res[1], list(res[2:2 + nb])), res[-1]


def _split_relay(name, handle, plan, relay_plan, n_relay, after):
    send_sems, recv_sems, bufs = handle
    nb = len(bufs)

    def body(*refs):
        sends, recvs = refs[nb], refs[nb + 1]
        for k, (src, dst, to) in enumerate(plan((), refs[:nb])):
            _remote(src, dst, sends, recvs, k, to).wait_recv()
        relay_sends, relay_recvs, token = refs[nb + 3], refs[nb + 4], refs[-1]
        for k, (src, dst, to) in enumerate(relay_plan((), refs[:nb])):
            _remote(src, dst, relay_sends, relay_recvs, k, to).start()
        token[...] = jnp.zeros_like(token)

    res = pl.pallas_call(
        body, name=name,
        out_shape=(pltpu.SemaphoreType.DMA((n_relay,)), pltpu.SemaphoreType.DMA((n_relay,)),
                   *[pltpu.HBM(b.shape, b.dtype) for b in bufs], jax.ShapeDtypeStruct((8, 128), F32)),
        in_specs=[HBM] * nb + [SEM, SEM, ANY],
        out_specs=(SEM, SEM, *[HBM] * nb, pl.BlockSpec(memory_space=pltpu.VMEM)),
        input_output_aliases={i: 2 + i for i in range(nb)},
        compiler_params=pltpu.CompilerParams(has_side_effects=DATAFLOW),
    )(*bufs, send_sems, recv_sems, after)
    passed = list(res[2:2 + nb])
    return (send_sems, recv_sems, passed), (res[0], res[1], passed), res[-1]


def _split_wait(name, handle, n_srcs, plan, after, arrived=False, first=0):
    send_sems, recv_sems, bufs = handle
    nb = len(bufs)

    def body(*refs):
        sends, recvs = refs[nb], refs[nb + 1]
        for k, (src, dst, to) in enumerate(plan(refs[:n_srcs], refs[n_srcs:nb])):
            cp = _remote(src, dst, sends, recvs, first + k, to)
            cp.wait_send()
            if arrived:
                continue
            cp.wait_recv()

    res = pl.pallas_call(
        body, name=name, out_shape=[pltpu.HBM(b.shape, b.dtype) for b in bufs],
        in_specs=[HBM] * nb + [SEM, SEM, ANY], out_specs=[HBM] * nb,
        input_output_aliases={i: i for i in range(nb)},
        compiler_params=pltpu.CompilerParams(has_side_effects=DATAFLOW),
    )(*bufs, send_sems, recv_sems, after)
    return list(res[:n_srcs]), list(res[n_srcs:])


def _relay_plan(shapes, axes):
    def plan(srcs, lands):
        x, y, c, _ = _place()
        first = c == 0
        from_x, from_y = jnp.where(first, 1 - x, x), jnp.where(first, y, 1 - y)
        to = (jnp.where(first, x, 1 - x), jnp.where(first, 1 - y, y), c)
        out = []
        for a, (shape, axis) in enumerate(zip(shapes, axes)):
            got = lands[a].at[(2 * from_x + from_y,) + _half(shape, c, axis)]
            out.append((got, got, to))
        return out
    return plan


def _gather_plan(shapes, axes, n_whole=0, relayed=False):
    def plan(srcs, lands):
        x, y, c, chips = _place()
        me = 2 * x + y
        out = []
        for a, (shape, axis) in enumerate(zip(shapes, axes)):
            own = lands[a].at[(me,) + _half(shape, c, axis)]
            for ch in chips[:2] if relayed else chips:
                out.append((own, own, (ch[0], ch[1], c)))
        for a in range(len(shapes), len(shapes) + n_whole):
            for ch in chips:
                out.append((lands[a].at[me], lands[a].at[me], (ch[0], ch[1], c)))
        return out
    return plan


def _share_plan(shapes, axes):
    def plan(srcs, lands):
        x, y, c, chips = _place()
        out = []
        for a, (shape, axis) in enumerate(zip(shapes, axes)):
            mine = _half(shape, c, axis)
            for ch in chips:
                landed = lands[a].at[(2 * ch[0] + ch[1],) + mine]
                out.append((landed, landed, (x, y, 1 - c)))
        return out
    return plan


def _sibling_plan(shapes, axes):
    def plan(srcs, lands):
        x, y, c, _ = _place()
        return [(srcs[a].at[(slice(None),) + _half(shape, 1 - c, axis)], lands[a], (x, y, 1 - c))
                for a, (shape, axis) in enumerate(zip(shapes, axes))]
    return plan


def _whole_to_sibling_plan(n):
    def plan(srcs, lands):
        x, y, c, _ = _place()
        return [(srcs[a], lands[a], (x, y, 1 - c)) for a in range(n)]
    return plan


def _reduce_plan(n_big, n_small):
    def plan(srcs, lands):
        x, y, c, chips = _place()
        out = []
        for a in range(n_big):
            for k, ch in enumerate(chips):
                out.append((srcs[a].at[2 * ch[0] + ch[1]], lands[a].at[k], (ch[0], ch[1], c)))
        for a in range(n_big, n_big + n_small):
            for ch in chips:
                out.append((srcs[a], lands[a].at[2 * x + y], (ch[0], ch[1], c)))
        return out
    return plan


def _row_tile(rows, cols, mult):
    best = mult
    for t in range(mult, rows + 1, mult):
        if rows % t == 0 and t * cols * 4 <= (2 << 20):
            best = t
    return best if rows % best == 0 else rows


COL_TILE = 256


def _half_tiling(hshape, axis, mult):
    hr, hc = hshape
    if axis == 0:
        tr = _row_tile(hr, hc, mult)
        return tr, hc, hr // tr
    return hr, COL_TILE, hc // COL_TILE


def _tile_idx(axis, t):
    return (t, 0) if axis == 0 else (0, t)


def _chip_partial(place, g, t, axis, name):
    hshape = t.shape[1:]
    br, bc, nt = _half_tiling(hshape, axis, 16)

    def body(pl_ref, *hbm):
        chip, core = pl_ref[0], pl_ref[1]

        def step(idx, g_ref, t_ref, pf_ref, pb_ref):
            v = g_ref[...].astype(F32) + t_ref[...].astype(F32)
            pb_ref[...] = v.astype(BF16)

            @pl.when(idx.index[1] == chip)
            def _():
                pf_ref[...] = v

        blk = (None, br, bc)
        deep = dict(pipeline_mode=pl.Buffered(3))
        pltpu.emit_pipeline(
            step, grid=(nt, 4),
            in_specs=[pl.BlockSpec(blk, lambda i, j: (j,) + _tile_idx(axis, core * nt + i), **deep),
                      pl.BlockSpec(blk, lambda i, j: (j,) + _tile_idx(axis, i), **deep)],
            out_specs=[pl.BlockSpec((br, bc), lambda i, j: _tile_idx(axis, i)),
                       pl.BlockSpec(blk, lambda i, j: (j,) + _tile_idx(axis, i))],
            _explicit_indices=True)(*hbm)

    return pl.pallas_call(
        body, name=name, in_specs=[pl.BlockSpec(memory_space=pltpu.SMEM), ANY, ANY], out_specs=[ANY, ANY],
        out_shape=[jax.ShapeDtypeStruct(hshape, F32), jax.ShapeDtypeStruct((4,) + hshape, BF16)],
        compiler_params=_cp(),
    )(place, g, t)


def _finish_half(pf, rb, axis, name):
    hshape = pf.shape
    br, bc, nt = _half_tiling(hshape, axis, 16)

    def body(*hbm):
        def step(pf_ref, rb_ref, o_ref):
            o_ref[...] = ((pf_ref[...] + rb_ref[0].astype(F32)) + rb_ref[1].astype(F32)) + rb_ref[2].astype(F32)

        deep = dict(pipeline_mode=pl.Buffered(3))
        pltpu.emit_pipeline(
            step, grid=(nt,),
            in_specs=[pl.BlockSpec((br, bc), lambda i: _tile_idx(axis, i), **deep),
                      pl.BlockSpec((3, br, bc), lambda i: (0,) + _tile_idx(axis, i), **deep)],
            out_specs=[pl.BlockSpec((br, bc), lambda i: _tile_idx(axis, i))])(*hbm)

    return pl.pallas_call(
        body, name=name, in_specs=[ANY, ANY], out_specs=ANY,
        out_shape=jax.ShapeDtypeStruct(hshape, F32),
        compiler_params=_cp(),
    )(pf, rb)


def _adam_math(w, g, m, v):
    m = ADAM_B1 * m + (1.0 - ADAM_B1) * g
    v = ADAM_B2 * v + (1.0 - ADAM_B2) * (g * g)
    m_hat = m / (1.0 - ADAM_B1 ** ADAM_STEP)
    v_hat = v / (1.0 - ADAM_B2 ** ADAM_STEP)
    return -ADAM_LR * (m_hat / (jnp.sqrt(v_hat) + ADAM_EPS) + ADAM_WD * w), m, v


def _adam_halves(place, w, mine, theirs, m, v, axis, name):
    br, bc, nt = _half_tiling(mine.shape, axis, 8)

    def body(pl_ref, *hbm):
        core = pl_ref[1]

        def step(idx, w_ref, a_ref, b_ref, m_ref, v_ref, g_ref, d_ref, mo_ref, vo_ref):
            g = jnp.where(idx.index[0] // nt == core, a_ref[...], b_ref[...])
            d, mn, vn = _adam_math(w_ref[...], g, m_ref[...], v_ref[...])
            g_ref[...] = g
            d_ref[...] = d
            mo_ref[...] = mn
            vo_ref[...] = vn

        deep = dict(pipeline_mode=pl.Buffered(3))
        full = pl.BlockSpec((br, bc), lambda i: _tile_idx(axis, i))
        full_in = pl.BlockSpec((br, bc), lambda i: _tile_idx(axis, i), **deep)
        mine_spec = pl.BlockSpec((br, bc), lambda i: _tile_idx(axis, jnp.where(i // nt == core, i % nt, nt - 1)), **deep)
        theirs_spec = pl.BlockSpec((br, bc), lambda i: _tile_idx(axis, jnp.where(i // nt == core, 0, i % nt)), **deep)
        pltpu.emit_pipeline(step, grid=(2 * nt,), in_specs=[full_in, mine_spec, theirs_spec, full_in, full_in],
                            out_specs=[full] * 4, _explicit_indices=True)(*hbm)

    return pl.pallas_call(
        body, name=name, in_specs=[pl.BlockSpec(memory_space=pltpu.SMEM)] + [ANY] * 5, out_specs=[ANY] * 4,
        out_shape=[jax.ShapeDtypeStruct(w.shape, F32)] * 4, compiler_params=_cp(),
    )(place, w, mine, theirs, m, v)


def _add_many(xs, ys, name):
    n = len(xs)

    def body(*refs):
        for i in range(n):
            refs[2 * n + i][...] = refs[i][...] + refs[n + i][...]

    return pl.pallas_call(body, name=name, out_shape=[jax.ShapeDtypeStruct(a.shape, F32) for a in xs])(*xs, *ys)


def _adam_small(place, owns, landed, ws, ms, vs, widths):
    n, nw = len(owns), len(ws)

    def body(pl_ref, *refs):
        own_r, land_r = refs[:n], refs[n:2 * n]
        w_r, m_r, v_r = (refs[2 * n + k * nw:2 * n + (k + 1) * nw] for k in range(3))
        outs = refs[2 * n + 3 * nw:]
        g_o, d_o, m_o, v_o = outs[:n], outs[n:n + nw], outs[n + nw:n + 2 * nw], outs[n + 2 * nw:]
        for me in range(4):
            @pl.when(pl_ref[0] == me)
            def _(me=me):
                for i in range(n):
                    p = [own_r[i][...] if k == me else land_r[i][k] for k in range(4)]
                    g = ((p[0] + p[1]) + p[2]) + p[3]
                    if i < nw and widths[i]:
                        g = g[:, me * widths[i]:(me + 1) * widths[i]]
                    g_o[i][...] = g
                    if i < nw:
                        d, mn, vn = _adam_math(w_r[i][...], g, m_r[i][...], v_r[i][...])
                        d_o[i][...] = d
                        m_o[i][...] = mn
                        v_o[i][...] = vn

    g_shapes = [jax.ShapeDtypeStruct(ws[i].shape if i < nw else owns[i].shape, F32) for i in range(n)]
    w_shapes = [jax.ShapeDtypeStruct(w.shape, F32) for w in ws]
    whole = lambda a: pl.BlockSpec(a.shape, lambda i, p, nd=len(a.shape): (0,) * nd)
    ins = list(owns) + list(landed) + list(ws) + list(ms) + list(vs)
    out_shape = g_shapes + w_shapes * 3
    out = pl.pallas_call(
        body, name="adam_small",
        grid_spec=pltpu.PrefetchScalarGridSpec(num_scalar_prefetch=1, grid=(1,), in_specs=[whole(a) for a in ins],
                                               out_specs=[whole(a) for a in out_shape]),
        out_shape=out_shape, compiler_params=_cp("arbitrary"),
    )(place, *ins)
    return out[:n], out[n:n + nw], out[n + nw:n + 2 * nw], out[n + 2 * nw:]


def kernel(x, g_mix, w_in, b_gate, w_gk_up, b_gk, w_pool_grp, pool_scale, g_gla_head, w_pool_proj, w_gla_proj, w_out, g_ffn, w_up, w_conv, b_conv, w_down, g_final, loss_target, m_g_mix, m_w_in, m_b_gate, m_w_gk_up, m_b_gk, m_w_pool_grp, m_pool_scale, m_g_gla_head, m_w_pool_proj, m_w_gla_proj, m_w_out, m_g_ffn, m_w_up, m_w_conv, m_b_conv, m_w_down, m_g_final, v_g_mix, v_w_in, v_b_gate, v_w_gk_up, v_b_gk, v_w_pool_grp, v_pool_scale, v_g_gla_head, v_w_pool_proj, v_w_gla_proj, v_w_out, v_g_ffn, v_w_up, v_w_conv, v_b_conv, v_w_down, v_g_final):
    s = x.shape[1]
    ts = min(s, 512)
    tm = min(s, 256)
    cx, cy, cc = lax.axis_index("x"), lax.axis_index("y"), lax.axis_index("c")
    chip = 2 * cx + cy
    place = jnp.stack([chip, cc]).astype(jnp.int32)

    big_names = ("w_in", "w_pool_proj", "w_gla_proj", "w_out", "w_up", "w_down")
    axes = (1, 0, 0, 0, 0, 0)
    shards = dict(w_in=jnp.transpose(w_in[0]), w_pool_proj=w_pool_proj[0], w_gla_proj=w_gla_proj[0], w_out=w_out[0],
                  w_up=w_up[0], w_down=w_down[0])
    def landing(own_shards):
        return [lax.dynamic_update_slice(lax.empty((4,) + o_.shape, o_.dtype), o_[None], (chip, 0, 0))
                for o_ in own_shards]

    def gather_start(tag, lands, n_halves, group_axes, after, relayed=False):
        n_whole = len(lands) - n_halves
        plan = _gather_plan([l_.shape[1:] for l_ in lands[:n_halves]], group_axes, n_whole, relayed)
        n_copies = (2 if relayed else 3) * n_halves + 3 * n_whole
        handle, token = _split_start("gather_" + tag + "_start", [], lands, plan, n_copies, after)
        return (handle, plan, n_halves, group_axes), token

    def gather_relay(tag, started, after):
        handle, plan, n_halves, group_axes = started
        relay_plan = _relay_plan([b_.shape[1:] for b_ in handle[2]], group_axes)
        first, relay, token = _split_relay("gather_" + tag + "_relay", handle, plan, relay_plan, n_halves, after)
        return (first, plan, relay, relay_plan, group_axes), token

    def gather_finish_relayed(tag, relayed, after):
        first, plan, relay, relay_plan, group_axes = relayed
        lands = _split_wait("gather_" + tag + "_sent", first, 0, plan, after, arrived=True)[1]
        lands = _split_wait("gather_" + tag + "_wait", (relay[0], relay[1], lands), 0, relay_plan, after)[1]
        return _gather_share(lands, group_axes, "gather_" + tag + "_share")

    in_w, tok = gather_start("in", landing([jnp.transpose(w_in[0].astype(BF16))]), 1, axes[:1], g_mix, relayed=True)
    zero = tok[0, 0]
    own = landing([(shards[n] + zero).astype(BF16) for n in big_names[1:]] + [w_gk_up[0] + zero, w_conv[0] + zero])
    own = lax.optimization_barrier(own)
    in_r, tok = gather_relay("in", in_w, own[4])
    xs, tgt = x[0], loss_target[0]
    h = _rmsnorm(xs, g_mix, tok, "norm_mix", ts)
    m_in_t, v_in_t = jnp.transpose(m_w_in[0]), jnp.transpose(v_w_in[0])
    h, m_in_t, v_in_t = lax.optimization_barrier((h, m_in_t, v_in_t))
    groups = ((own[0:3] + own[5:7], 3, axes[1:4]), (own[3:4], 1, axes[4:5]), (own[4:5], 1, axes[5:6]))
    plans = [_gather_plan([l_.shape[1:] for l_ in ls[:nh]], ax, len(ls) - nh) for ls, nh, ax in groups]
    counts = [3 * len(ls) for ls, _, _ in groups]
    bounds = [sum(len(ls) for ls, _, _ in groups[:i]) for i in range(4)]

    def rest_plan(srcs, lands):
        return [cp for i, pl_ in enumerate(plans) for cp in pl_((), lands[bounds[i]:bounds[i + 1]])]

    (rest_send, rest_recv, rest_bufs), tok = _split_start("gather_rest_start", [], [l_ for ls, _, _ in groups for l_ in ls],
                                                          rest_plan, sum(counts), m_in_t)
    mix_w, up_w, down_w = [((rest_send, rest_recv, rest_bufs[bounds[i]:bounds[i + 1]]), plans[i], groups[i][1], groups[i][2],
                            sum(counts[:i])) for i in range(3)]

    def forward_start(tag, started, after):
        handle, plan, n_halves, group_axes, first = started
        lands = _split_wait("gather_" + tag + "_wait", handle, 0, plan, after, first=first)[1]
        plan = _share_plan([l_.shape[1:] for l_ in lands[:n_halves]], group_axes)
        share, token = _split_start("gather_" + tag + "_share_start", [], lands[:n_halves], plan, 3 * n_halves, after)
        return (share, plan, lands[n_halves:]), token

    def forward_done(tag, forwarded, after):
        share, plan, _ = forwarded
        return _split_wait("gather_" + tag + "_share_wait", share, 0, plan, after)[1]
    wgrp = w_pool_grp[0]
    w_in_t = gather_finish_relayed("in", in_r, tok)[0]
    nsh = N_IN // 4

    zr, w_in_rt = _in_proj(h, w_in_t, PROJ_TILE)
    p, pp = _pool_fwd(zr, wgrp, pool_scale)
    mix_f, tok = forward_start("mix", mix_w, pp)
    wgk4, wconv4 = mix_f[2]
    wgk_full = jnp.transpose(wgk4, (1, 0, 2)).reshape(GATE_RANK, 512) + tok[0, 0]
    wconv_full = jnp.transpose(wconv4, (1, 0, 2)).reshape(3, N_UP)
    wgk_pad = jnp.concatenate([wgk_full, jnp.zeros((128 - GATE_RANK, 512), F32)], axis=0)
    o, og, sp = _gla_fwd(zr, wgk_pad, b_gk, g_gla_head, ts)
    wpp, wgla, wout = forward_done("mix", mix_f, og)
    wgla, wout = wgla.reshape(D, D), wout.reshape(D, D)
    up_f, tok = forward_start("up", up_w, og)
    x1, mixed, yp, yg, h2 = _merge_fwd(xs, zr, pp, og, b_gate, wpp, wgla, wout, g_ffn, tok, ts)
    wup, = forward_done("up", up_f, x1)
    down_f, tok = forward_start("down", down_w, x1)
    u = _matmul_resident(h2, wup, tok, "ffn_up")
    wdown = forward_done("down", down_f, u)[0].reshape(D_FF, D)
    a, conv_out, dx2, dx2b, loss_part, dgfin = _ffn_down_loss(u, x1, tgt, wconv_full, b_conv, wdown,
                                                              g_final.reshape(1, D), tm)

    du, dbconv, dwconv = _ffn_bwd(dx2b, u, conv_out, wconv_full, wdown, tm)
    dw_down = _matmul_tn(a, dx2b, "dw_down", D, tm=D_FF // 2)
    dw_up = _matmul_tn(h2, du, "dw_up", UP_SHARD, shard_major=True)

    def exchange_start(tag, grads, group_axes, after):
        plan = _sibling_plan([g.shape[1:] for g in grads], group_axes)
        lands = [((4,) + _half_shape(g.shape[1:], ax), g.dtype) for g, ax in zip(grads, group_axes)]
        handle, token = _split_start("sibling_" + tag + "_start", grads, lands, plan, len(grads), after)
        return (handle, plan, len(grads)), token

    def partials(tag, names, group_axes, exchange, after):
        handle, plan, n = exchange
        mine, theirs = _split_wait("sibling_" + tag + "_wait", handle, n, plan, after)
        return zip(*[_chip_partial(place, g, t, ax, "chip_partial_" + nm)
                     for nm, ax, g, t in zip(names, group_axes, mine, theirs)])

    ffn_names, ffn_axes = ("w_up", "w_down"), (0, 0)
    ffn_x, token = exchange_start("ffn", [dw_up, dw_down.reshape(4, 704, D)], ffn_axes, du)
    dx1, dx1b, dgffn = _matmul_nt_normbwd(du, wup, x1, g_ffn, dx2, token, "ffn_up_bwd", ts)
    ffn_pf, ffn_pb = partials("ffn", ffn_names, ffn_axes, ffn_x, dx1b)
    ffn_plan = _reduce_plan(2, 0)
    ffn_handle, token = _split_start("reduce_ffn_start", ffn_pb, [((3,) + p.shape[1:], BF16) for p in ffn_pb],
                                     ffn_plan, 6, ffn_pf[0])

    dzr, dyp, dyg, dpp, do, dzog, dbgate, dghead = _merge_bwd(dx1b, zr, yp, yg, o, b_gate, g_gla_head, wpp, wgla, wout,
                                                             token, ts)
    dzr = lax.dynamic_update_slice(dzr, dzog, (0, OFF_OG))
    dw_out = _matmul_tn(mixed, dx1b, "dw_out", D, tm=512)
    dw_gla = _matmul_tn(og, dyg, "dw_gla", D, tm=512)
    dw_pp = _matmul_tn(pp, dyp, "dw_pp", 256, shard_major=True)

    out_names, out_axes = ("w_pool_proj", "w_gla_proj", "w_out"), (0, 0, 0)
    out_x, token = exchange_start("out", [dw_pp, dw_gla.reshape(4, 256, D), dw_out.reshape(4, 256, D)], out_axes, dpp)
    dzr, dwgrp, dscale = _pool_bwd(p, dpp, wgrp, pool_scale, token, dzr)
    out_pf, out_pb = partials("out", out_names, out_axes, out_x, dwgrp)
    out_plan = _reduce_plan(3, 0)
    out_handle, token = _split_start("reduce_out_start", out_pb, [((3,) + p_.shape[1:], BF16) for p_ in out_pb],
                                     out_plan, 9, out_pf[0])
    dq, dk, dzr, dgpre = _gla_bwd(zr, do, sp, wgk_pad, b_gk, token, dzr, ts)
    dzr, dwgk, dbgk = _gk_bwd(dgpre, zr, wgk_pad, dgpre, dzr, ts)
    dzr = lax.dynamic_update_slice(lax.dynamic_update_slice(dzr, dq, (0, OFF_Q)), dk, (0, OFF_K))
    dw_rt = _matmul_tn(dzr, h, "dw_in", D, tm=PROJ_TILE)

    def grad_rows(lo, hi):
        out = []
        for seg_lo, seg_hi, at in ((0, 1536, OFF_POOL), (1536, 3584, OFF_V), (3584, 3600, OFF_GK), (3600, N_IN, OFF_GATE)):
            a_, b_ = max(lo, seg_lo), min(hi, seg_hi)
            if a_ < b_:
                out.append(dw_rt[at + a_ - seg_lo:at + b_ - seg_lo])
        return jnp.concatenate(out, axis=0)

    dw_in_t = jnp.stack([grad_rows(j * nsh, (j + 1) * nsh) for j in range(4)])

    ms = dict(w_in=m_in_t, w_pool_proj=m_w_pool_proj[0], w_gla_proj=m_w_gla_proj[0], w_out=m_w_out[0],
              w_up=m_w_up[0], w_down=m_w_down[0])
    vs = dict(w_in=v_in_t, w_pool_proj=v_w_pool_proj[0], w_gla_proj=v_w_gla_proj[0], w_out=v_w_out[0],
              w_up=v_w_up[0], w_down=v_w_down[0])
    grad, delta, new_m, new_v = {}, {}, {}, {}

    def finish(names, group_axes, part_f, landed):
        return [_finish_half(pf, rb, ax, "finish_" + n) for n, ax, pf, rb in zip(names, group_axes, part_f, landed)]

    def update(names, group_axes, halves, sib_halves):
        for n, ax, mine, theirs in zip(names, group_axes, halves, sib_halves):
            res = _adam_halves(place, shards[n], mine, theirs, ms[n], vs[n], ax, "adam_" + n)
            if n == "w_in":
                res = [jnp.transpose(r_) for r_ in res]
            grad[n], delta[n], new_m[n], new_v[n] = [r_[None] for r_ in res]

    rest_names, rest_axes = ffn_names + out_names, ffn_axes + out_axes
    in_x, token = exchange_start("in", [dw_in_t], (1,), dw_rt)
    _, ffn_landed = _split_wait("reduce_ffn_wait", ffn_handle, 2, ffn_plan, token)
    _, out_landed = _split_wait("reduce_out_wait", out_handle, 3, out_plan, ffn_landed[0])
    rest_halves = lax.optimization_barrier(finish(rest_names, rest_axes, ffn_pf + out_pf, ffn_landed + out_landed))
    (in_pf,), (in_pb,) = partials("in", ("w_in",), (1,), in_x, rest_halves[-1])
    in_plan = _reduce_plan(1, 0)
    in_handle, token = _split_start("reduce_in_start", [in_pb], [((3,) + in_pb.shape[1:], BF16)], in_plan, 3, in_pf)
    rest_plan = _whole_to_sibling_plan(len(rest_halves))
    rest_share, token = _split_start("sibling_share_rest_start", rest_halves, [(h_.shape, F32) for h_ in rest_halves],
                                     rest_plan, len(rest_halves), token)
    grad_x, _, dgmix = _matmul_nt_normbwd(dzr, w_in_rt, xs, g_mix, dx1, token, "in_proj_bwd", ts, transposed=True)
    small_names = ("g_mix", "b_gate", "w_gk_up", "b_gk", "w_pool_grp", "pool_scale", "g_gla_head", "g_ffn", "w_conv",
                   "b_conv", "g_final")
    small_mine = [dgmix, dbgate, dwgk[:GATE_RANK], dbgk, dwgrp.reshape(4 * 128, 128), dscale, dghead, dgffn, dwconv, dbconv,
                  dgfin, loss_part]
    small_sib = _sibling_exchange([], (), small_mine, "sibling_exchange_small")
    small_chip = _add_many(small_mine, small_sib, "chip_partial_small")
    small_plan = _reduce_plan(0, len(small_chip))
    small_handle, token = _split_start("reduce_small_start", small_chip, [((4,) + a_.shape, F32) for a_ in small_chip],
                                       small_plan, 3 * len(small_chip), small_mine[0])

    rest_halves, rest_sib = _split_wait("sibling_share_rest_wait", rest_share, len(rest_halves), rest_plan, token)
    n_ffn = len(ffn_names)
    update(out_names, out_axes, rest_halves[n_ffn:], rest_sib[n_ffn:])
    updated = lax.optimization_barrier([delta[n] for n in out_names])
    _, in_landed = _split_wait("reduce_in_wait", in_handle, 1, in_plan, updated[0])
    in_halves = finish(("w_in",), (1,), (in_pf,), in_landed)
    update(("w_in",), (1,), in_halves, _sibling_share(in_halves, "sibling_share_in"))
    ffn_halves, _ = lax.optimization_barrier((rest_halves[:n_ffn], delta["w_in"]))
    update(ffn_names, ffn_axes, ffn_halves, rest_sib[:n_ffn])
    small_sent, small_landed = _split_wait("reduce_small_wait", small_handle, len(small_chip), small_plan, delta["w_in"])
    given = dict(g_mix=(g_mix, m_g_mix, v_g_mix), b_gate=(b_gate, m_b_gate, v_b_gate), w_gk_up=(w_gk_up, m_w_gk_up, v_w_gk_up),
                 b_gk=(b_gk, m_b_gk, v_b_gk), w_pool_grp=(w_pool_grp, m_w_pool_grp, v_w_pool_grp),
                 pool_scale=(pool_scale, m_pool_scale, v_pool_scale), g_gla_head=(g_gla_head, m_g_gla_head, v_g_gla_head),
                 g_ffn=(g_ffn, m_g_ffn, v_g_ffn), w_conv=(w_conv, m_w_conv, v_w_conv), b_conv=(b_conv, m_b_conv, v_b_conv),
                 g_final=(g_final, m_g_final, v_g_final))
    flat2 = lambda a: a.reshape(-1, a.shape[-1])
    widths = [dict(w_gk_up=HK, w_conv=UP_SHARD).get(n) for n in small_names]
    totals, ds, mo, vo = _adam_small(place, small_sent, small_landed, *[[flat2(given[n][k]) for n in small_names] for k in range(3)],
                                     widths)
    loss = totals[-1][0, 0]
    for i, n in enumerate(small_names):
        shp = given[n][0].shape
        grad[n], delta[n], new_m[n], new_v[n] = [r_.reshape(shp) for r_ in (totals[i], ds[i], mo[i], vo[i])]

    order = ("g_mix", "w_in", "b_gate", "w_gk_up", "b_gk", "w_pool_grp", "pool_scale", "g_gla_head", "w_pool_proj",
             "w_gla_proj", "w_out", "g_ffn", "w_up", "w_conv", "b_conv", "w_down", "g_final")
    return (loss, grad_x[None], *[grad[n] for n in order], *[delta[n] for n in order], *[new_m[n] for n in order],
            *[new_v[n] for n in order])
```
